```python
import math
import jax
import jax.numpy as jnp
from jax import lax
import numpy as np

D_MODEL = 1024
BATCH = 8
SEQ = 2048
DEPTH = 1

N_META = 16
D_MIX = 2 * D_MODEL
SSD_WIDTH = D_MIX // 2
SSD_HEAD_DIM = 64
SSD_HEADS = SSD_WIDTH // SSD_HEAD_DIM
SSD_GROUPS = 2
SSD_HPG = SSD_HEADS // SSD_GROUPS
SSD_STATE = 128
SSD_CONV = 4
SSD_CHUNK = 128
SSD_CONV_DIM = SSD_WIDTH + 2 * SSD_GROUPS * SSD_STATE
LRU_WIDTH = D_MIX - SSD_WIDTH
LRU_BLOCKS = 16
LRU_BLOCK_W = LRU_WIDTH // LRU_BLOCKS
LRU_CONV = 4
LRU_C = 8.0
D_FF = -(-(8 * D_MODEL) // (3 * 256)) * 256
IN_COLS = SSD_WIDTH + SSD_CONV_DIM + SSD_HEADS + 2 * LRU_WIDTH
IN_SPLITS = [SSD_WIDTH, SSD_WIDTH + SSD_CONV_DIM, SSD_WIDTH + SSD_CONV_DIM + SSD_HEADS, SSD_WIDTH + SSD_CONV_DIM + SSD_HEADS + LRU_WIDTH]
EPS = 1e-6

kernel_name = 'hymba_ssd_rglru_hybrid_block'


def rmsnorm(x, w):
    xf = x.astype(jnp.float32)
    y = xf * lax.rsqrt(jnp.mean(xf * xf, axis=-1, keepdims=True) + EPS)
    return (y * w.astype(jnp.float32)).astype(x.dtype)


def causal_dwconv(x, w, b):
    k, c = w.shape
    y = lax.conv_general_dilated(x, w[:, None, :].astype(x.dtype), window_strides=(1,), padding=[(k - 1, 0)], dimension_numbers=('NWC', 'WIO', 'NWC'), feature_group_count=c)
    return y + b.astype(x.dtype)


def _to_chunks(t, pad):
    t = jnp.pad(t, [(0, 0), (pad, 0)] + [(0, 0)] * (t.ndim - 2))
    return t.reshape((t.shape[0], -1, SSD_CHUNK) + t.shape[2:])


def ssd_mixer(z, xbc, dt_raw, conv_w, conv_b, dt_bias, a_log, d_skip, norm_w):
    bsz, seqlen, _ = z.shape
    f32 = jnp.float32
    xbc = jax.nn.silu(causal_dwconv(xbc, conv_w, conv_b))
    xs, b_in, c_in = jnp.split(xbc, [SSD_WIDTH, SSD_WIDTH + SSD_GROUPS * SSD_STATE], axis=-1)
    dt = jax.nn.softplus(dt_raw.astype(f32) + dt_bias.astype(f32))
    a = -jnp.exp(a_log.astype(f32)).reshape(SSD_GROUPS, SSD_HPG)
    pad = (-seqlen) % SSD_CHUNK
    x_c = _to_chunks(xs.astype(f32).reshape(bsz, seqlen, SSD_GROUPS, SSD_HPG, SSD_HEAD_DIM), pad)
    b_c = _to_chunks(b_in.astype(f32).reshape(bsz, seqlen, SSD_GROUPS, SSD_STATE), pad)
    c_c = _to_chunks(c_in.astype(f32).reshape(bsz, seqlen, SSD_GROUPS, SSD_STATE), pad)
    dt_c = _to_chunks(dt.reshape(bsz, seqlen, SSD_GROUPS, SSD_HPG), pad)
    cs = jnp.cumsum(dt_c * a, axis=2)
    xdt = x_c * dt_c[..., None]
    causal = jnp.tril(jnp.ones((SSD_CHUNK, SSD_CHUNK), dtype=bool))
    seg = cs[:, :, :, None] - cs[:, :, None, :]
    lmat = jnp.exp(jnp.where(causal[:, :, None, None], seg, -jnp.inf))
    cb = jnp.einsum('bclgn,bcsgn->bclsg', c_c, b_c)
    y_diag = jnp.einsum('bclsgj,bcsgjp->bclgjp', cb[..., None] * lmat, xdt)
    decay_states = jnp.exp(cs[:, :, -1:] - cs)
    states = jnp.einsum('bclgn,bclgjp->bcgjpn', b_c, xdt * decay_states[..., None])
    chunk_decay = jnp.exp(cs[:, :, -1])

    def step(h, inp):
        s, d = inp
        return h * d[..., None, None] + s, h

    h0 = jnp.zeros((bsz, SSD_GROUPS, SSD_HPG, SSD_HEAD_DIM, SSD_STATE), f32)
    _, prev = lax.scan(step, h0, (jnp.moveaxis(states, 1, 0), jnp.moveaxis(chunk_decay, 1, 0)))
    prev = jnp.moveaxis(prev, 0, 1)
    y_off = jnp.einsum('bclgn,bcgjpn->bclgjp', c_c, prev) * jnp.exp(cs)[..., None]
    y = (y_diag + y_off).reshape(bsz, -1, SSD_WIDTH)[:, pad:]
    y = y + (xs.astype(f32).reshape(bsz, seqlen, SSD_HEADS, SSD_HEAD_DIM) * d_skip.astype(f32)[:, None]).reshape(bsz, seqlen, SSD_WIDTH)
    y = y.astype(z.dtype)
    g = (y * jax.nn.silu(z)).reshape(bsz, seqlen, SSD_GROUPS, SSD_WIDTH // SSD_GROUPS)
    return rmsnorm(g, norm_w.reshape(SSD_GROUPS, -1)).reshape(bsz, seqlen, SSD_WIDTH)


def rglru_mixer(gate, xr, conv_w, conv_b, wa, ba, wx, bx, lam, norm_w):
    bsz, seqlen, _ = xr.shape
    f32 = jnp.float32
    xr = causal_dwconv(xr, conv_w, conv_b)
    xb = xr.reshape(bsz, seqlen, LRU_BLOCKS, LRU_BLOCK_W)
    r = jax.nn.sigmoid(jnp.einsum('btni,nij->btnj', xb, wa).reshape(bsz, seqlen, LRU_WIDTH) + ba)
    i = jax.nn.sigmoid(jnp.einsum('btni,nij->btnj', xb, wx).reshape(bsz, seqlen, LRU_WIDTH) + bx)
    log_a = -LRU_C * r.astype(f32) * jax.nn.softplus(-lam.astype(f32))
    a = jnp.exp(log_a)
    u = jnp.sqrt(-jnp.expm1(2.0 * log_a)) * (i * xr).astype(f32)

    def combine(left, right):
        a1, b1 = left
        a2, b2 = right
        return a1 * a2, a2 * b1 + b2

    _, h = lax.associative_scan(combine, (a, u), axis=1)
    y = jax.nn.gelu(gate) * h.astype(gate.dtype)
    return rmsnorm(y, norm_w)


def _fwd_setup_inputs(seed: int = 0) -> dict:
    key = jax.random.key(seed)
    ks = jax.random.split(key, 24)
    nrm = jax.random.normal
    dt0 = jnp.exp(jax.random.uniform(ks[6], (DEPTH, SSD_HEADS), minval=math.log(1e-3), maxval=math.log(1e-1)))
    a_base = jax.random.uniform(ks[15], (DEPTH, LRU_WIDTH), minval=0.9, maxval=0.999)
    s = a_base ** (1.0 / LRU_C)
    return {
        'x': nrm(ks[0], (BATCH, SEQ, D_MODEL), jnp.float32),
        'meta_tokens': nrm(ks[1], (N_META, D_MODEL), jnp.float32),
        'norm1_w': 1.0 + 0.02 * nrm(ks[2], (DEPTH, D_MODEL)),
        'w_in': nrm(ks[3], (DEPTH, D_MODEL, IN_COLS)) * D_MODEL ** -0.5,
        'ssd_conv_w': nrm(ks[4], (DEPTH, SSD_CONV, SSD_CONV_DIM)) * SSD_CONV ** -0.5,
        'ssd_conv_b': 0.02 * nrm(ks[5], (DEPTH, SSD_CONV_DIM)),
        'ssd_dt_bias': dt0 + jnp.log(-jnp.expm1(-dt0)),
        'ssd_a_log': jnp.log(jax.random.uniform(ks[7], (DEPTH, SSD_HEADS), minval=1.0, maxval=16.0)),
        'ssd_d': 1.0 + 0.1 * nrm(ks[8], (DEPTH, SSD_HEADS)),
        'ssd_norm_w': 1.0 + 0.02 * nrm(ks[9], (DEPTH, SSD_WIDTH)),
        'lru_conv_w': nrm(ks[10], (DEPTH, LRU_CONV, LRU_WIDTH)) * LRU_CONV ** -0.5,
        'lru_conv_b': 0.02 * nrm(ks[11], (DEPTH, LRU_WIDTH)),
        'lru_wa': nrm(ks[12], (DEPTH, LRU_BLOCKS, LRU_BLOCK_W, LRU_BLOCK_W)) * LRU_BLOCK_W ** -0.5,
        'lru_ba': 0.02 * nrm(ks[13], (DEPTH, LRU_WIDTH)),
        'lru_wx': nrm(ks[14], (DEPTH, LRU_BLOCKS, LRU_BLOCK_W, LRU_BLOCK_W)) * LRU_BLOCK_W ** -0.5,
        'lru_bx': 0.02 * nrm(ks[16], (DEPTH, LRU_WIDTH)),
        'lru_lambda': jnp.log(s) - jnp.log1p(-s),
        'lru_norm_w': 1.0 + 0.02 * nrm(ks[17], (DEPTH, LRU_WIDTH)),
        'w_out': nrm(ks[18], (DEPTH, D_MIX, D_MODEL)) * D_MIX ** -0.5,
        'norm2_w': 1.0 + 0.02 * nrm(ks[19], (DEPTH, D_MODEL)),
        'w_gate': nrm(ks[20], (DEPTH, D_MODEL, D_FF)) * D_MODEL ** -0.5,
        'w_up': nrm(ks[21], (DEPTH, D_MODEL, D_FF)) * D_MODEL ** -0.5,
        'w_down': nrm(ks[22], (DEPTH, D_FF, D_MODEL)) * D_FF ** -0.5,
        'final_norm_w': 1.0 + 0.02 * nrm(ks[23], (D_MODEL,)),
    }


def _fwd_reference(x, meta_tokens, norm1_w, w_in, ssd_conv_w, ssd_conv_b, ssd_dt_bias, ssd_a_log, ssd_d, ssd_norm_w, lru_conv_w, lru_conv_b, lru_wa, lru_ba, lru_wx, lru_bx, lru_lambda, lru_norm_w, w_out, norm2_w, w_gate, w_up, w_down, final_norm_w):
    bsz = x.shape[0]
    meta = jnp.broadcast_to(meta_tokens.astype(x.dtype)[None], (bsz, N_META, D_MODEL))
    h = jnp.concatenate([meta, x], axis=1)
    for li in range(DEPTH):
        u = rmsnorm(h, norm1_w[li])
        proj = u @ w_in[li]
        z, xbc, dt_raw, g_lru, x_lru = jnp.split(proj, IN_SPLITS, axis=-1)
        y_ssd = ssd_mixer(z, xbc, dt_raw, ssd_conv_w[li], ssd_conv_b[li], ssd_dt_bias[li], ssd_a_log[li], ssd_d[li], ssd_norm_w[li])
        y_lru = rglru_mixer(g_lru, x_lru, lru_conv_w[li], lru_conv_b[li], lru_wa[li], lru_ba[li], lru_wx[li], lru_bx[li], lru_lambda[li], lru_norm_w[li])
        h = h + jnp.concatenate([y_ssd, y_lru], axis=-1) @ w_out[li]
        u = rmsnorm(h, norm2_w[li])
        h = h + (jax.nn.silu(u @ w_gate[li]) * (u @ w_up[li])) @ w_down[li]
    h = rmsnorm(h, final_norm_w)
    return h[:, N_META:]


import jax as _jax
import jax.numpy as _jnp

TWIN_FORMAT = 'train_step'
FWD_PARAMS = ['x', 'meta_tokens', 'norm1_w', 'w_in', 'ssd_conv_w', 'ssd_conv_b', 'ssd_dt_bias', 'ssd_a_log', 'ssd_d', 'ssd_norm_w', 'lru_conv_w', 'lru_conv_b', 'lru_wa', 'lru_ba', 'lru_wx', 'lru_bx', 'lru_lambda', 'lru_norm_w', 'w_out', 'norm2_w', 'w_gate', 'w_up', 'w_down', 'final_norm_w']
TWIN_WEIGHTS = ['meta_tokens', 'norm1_w', 'w_in', 'ssd_conv_w', 'ssd_conv_b', 'ssd_dt_bias', 'ssd_a_log', 'ssd_d', 'ssd_norm_w', 'lru_conv_w', 'lru_conv_b', 'lru_wa', 'lru_ba', 'lru_wx', 'lru_bx', 'lru_lambda', 'lru_norm_w', 'w_out', 'norm2_w', 'w_gate', 'w_up', 'w_down', 'final_norm_w']
TWIN_DIFF_INPUT = 'x'
TWIN_INPUTS = ['x', 'meta_tokens', 'norm1_w', 'w_in', 'ssd_conv_w', 'ssd_conv_b', 'ssd_dt_bias', 'ssd_a_log', 'ssd_d', 'ssd_norm_w', 'lru_conv_w', 'lru_conv_b', 'lru_wa', 'lru_ba', 'lru_wx', 'lru_bx', 'lru_lambda', 'lru_norm_w', 'w_out', 'norm2_w', 'w_gate', 'w_up', 'w_down', 'final_norm_w', 'loss_target', 'm_meta_tokens', 'm_norm1_w', 'm_w_in', 'm_ssd_conv_w', 'm_ssd_conv_b', 'm_ssd_dt_bias', 'm_ssd_a_log', 'm_ssd_d', 'm_ssd_norm_w', 'm_lru_conv_w', 'm_lru_conv_b', 'm_lru_wa', 'm_lru_ba', 'm_lru_wx', 'm_lru_bx', 'm_lru_lambda', 'm_lru_norm_w', 'm_w_out', 'm_norm2_w', 'm_w_gate', 'm_w_up', 'm_w_down', 'm_final_norm_w', 'v_meta_tokens', 'v_norm1_w', 'v_w_in', 'v_ssd_conv_w', 'v_ssd_conv_b', 'v_ssd_dt_bias', 'v_ssd_a_log', 'v_ssd_d', 'v_ssd_norm_w', 'v_lru_conv_w', 'v_lru_conv_b', 'v_lru_wa', 'v_lru_ba', 'v_lru_wx', 'v_lru_bx', 'v_lru_lambda', 'v_lru_norm_w', 'v_w_out', 'v_norm2_w', 'v_w_gate', 'v_w_up', 'v_w_down', 'v_final_norm_w']
TWIN_OUTPUTS = ['loss', 'grad_x', 'grad_meta_tokens', 'grad_norm1_w', 'grad_w_in', 'grad_ssd_conv_w', 'grad_ssd_conv_b', 'grad_ssd_dt_bias', 'grad_ssd_a_log', 'grad_ssd_d', 'grad_ssd_norm_w', 'grad_lru_conv_w', 'grad_lru_conv_b', 'grad_lru_wa', 'grad_lru_ba', 'grad_lru_wx', 'grad_lru_bx', 'grad_lru_lambda', 'grad_lru_norm_w', 'grad_w_out', 'grad_norm2_w', 'grad_w_gate', 'grad_w_up', 'grad_w_down', 'grad_final_norm_w', 'delta_meta_tokens', 'delta_norm1_w', 'delta_w_in', 'delta_ssd_conv_w', 'delta_ssd_conv_b', 'delta_ssd_dt_bias', 'delta_ssd_a_log', 'delta_ssd_d', 'delta_ssd_norm_w', 'delta_lru_conv_w', 'delta_lru_conv_b', 'delta_lru_wa', 'delta_lru_ba', 'delta_lru_wx', 'delta_lru_bx', 'delta_lru_lambda', 'delta_lru_norm_w', 'delta_w_out', 'delta_norm2_w', 'delta_w_gate', 'delta_w_up', 'delta_w_down', 'delta_final_norm_w', 'new_m_meta_tokens', 'new_m_norm1_w', 'new_m_w_in', 'new_m_ssd_conv_w', 'new_m_ssd_conv_b', 'new_m_ssd_dt_bias', 'new_m_ssd_a_log', 'new_m_ssd_d', 'new_m_ssd_norm_w', 'new_m_lru_conv_w', 'new_m_lru_conv_b', 'new_m_lru_wa', 'new_m_lru_ba', 'new_m_lru_wx', 'new_m_lru_bx', 'new_m_lru_lambda', 'new_m_lru_norm_w', 'new_m_w_out', 'new_m_norm2_w', 'new_m_w_gate', 'new_m_w_up', 'new_m_w_down', 'new_m_final_norm_w', 'new_v_meta_tokens', 'new_v_norm1_w', 'new_v_w_in', 'new_v_ssd_conv_w', 'new_v_ssd_conv_b', 'new_v_ssd_dt_bias', 'new_v_ssd_a_log', 'new_v_ssd_d', 'new_v_ssd_norm_w', 'new_v_lru_conv_w', 'new_v_lru_conv_b', 'new_v_lru_wa', 'new_v_lru_ba', 'new_v_lru_wx', 'new_v_lru_bx', 'new_v_lru_lambda', 'new_v_lru_norm_w', 'new_v_w_out', 'new_v_norm2_w', 'new_v_w_gate', 'new_v_w_up', 'new_v_w_down', 'new_v_final_norm_w']
TWIN_LEAF_KINDS = {'loss': 'loss', 'grad_x': 'grad_x', 'grad_meta_tokens': 'grad_w', 'grad_norm1_w': 'grad_w', 'grad_w_in': 'grad_w', 'grad_ssd_conv_w': 'grad_w', 'grad_ssd_conv_b': 'grad_w', 'grad_ssd_dt_bias': 'grad_w', 'grad_ssd_a_log': 'grad_w', 'grad_ssd_d': 'grad_w', 'grad_ssd_norm_w': 'grad_w', 'grad_lru_conv_w': 'grad_w', 'grad_lru_conv_b': 'grad_w', 'grad_lru_wa': 'grad_w', 'grad_lru_ba': 'grad_w', 'grad_lru_wx': 'grad_w', 'grad_lru_bx': 'grad_w', 'grad_lru_lambda': 'grad_w', 'grad_lru_norm_w': 'grad_w', 'grad_w_out': 'grad_w', 'grad_norm2_w': 'grad_w', 'grad_w_gate': 'grad_w', 'grad_w_up': 'grad_w', 'grad_w_down': 'grad_w', 'grad_final_norm_w': 'grad_w', 'delta_meta_tokens': 'delta_w', 'delta_norm1_w': 'delta_w', 'delta_w_in': 'delta_w', 'delta_ssd_conv_w': 'delta_w', 'delta_ssd_conv_b': 'delta_w', 'delta_ssd_dt_bias': 'delta_w', 'delta_ssd_a_log': 'delta_w', 'delta_ssd_d': 'delta_w', 'delta_ssd_norm_w': 'delta_w', 'delta_lru_conv_w': 'delta_w', 'delta_lru_conv_b': 'delta_w', 'delta_lru_wa': 'delta_w', 'delta_lru_ba': 'delta_w', 'delta_lru_wx': 'delta_w', 'delta_lru_bx': 'delta_w', 'delta_lru_lambda': 'delta_w', 'delta_lru_norm_w': 'delta_w', 'delta_w_out': 'delta_w', 'delta_norm2_w': 'delta_w', 'delta_w_gate': 'delta_w', 'delta_w_up': 'delta_w', 'delta_w_down': 'delta_w', 'delta_final_norm_w': 'delta_w', 'new_m_meta_tokens': 'new_m', 'new_m_norm1_w': 'new_m', 'new_m_w_in': 'new_m', 'new_m_ssd_conv_w': 'new_m', 'new_m_ssd_conv_b': 'new_m', 'new_m_ssd_dt_bias': 'new_m', 'new_m_ssd_a_log': 'new_m', 'new_m_ssd_d': 'new_m', 'new_m_ssd_norm_w': 'new_m', 'new_m_lru_conv_w': 'new_m', 'new_m_lru_conv_b': 'new_m', 'new_m_lru_wa': 'new_m', 'new_m_lru_ba': 'new_m', 'new_m_lru_wx': 'new_m', 'new_m_lru_bx': 'new_m', 'new_m_lru_lambda': 'new_m', 'new_m_lru_norm_w': 'new_m', 'new_m_w_out': 'new_m', 'new_m_norm2_w': 'new_m', 'new_m_w_gate': 'new_m', 'new_m_w_up': 'new_m', 'new_m_w_down': 'new_m', 'new_m_final_norm_w': 'new_m', 'new_v_meta_tokens': 'new_v', 'new_v_norm1_w': 'new_v', 'new_v_w_in': 'new_v', 'new_v_ssd_conv_w': 'new_v', 'new_v_ssd_conv_b': 'new_v', 'new_v_ssd_dt_bias': 'new_v', 'new_v_ssd_a_log': 'new_v', 'new_v_ssd_d': 'new_v', 'new_v_ssd_norm_w': 'new_v', 'new_v_lru_conv_w': 'new_v', 'new_v_lru_conv_b': 'new_v', 'new_v_lru_wa': 'new_v', 'new_v_lru_ba': 'new_v', 'new_v_lru_wx': 'new_v', 'new_v_lru_bx': 'new_v', 'new_v_lru_lambda': 'new_v', 'new_v_lru_norm_w': 'new_v', 'new_v_w_out': 'new_v', 'new_v_norm2_w': 'new_v', 'new_v_w_gate': 'new_v', 'new_v_w_up': 'new_v', 'new_v_w_down': 'new_v', 'new_v_final_norm_w': 'new_v'}


def _forward(args):
    return _fwd_reference(*[args[k] for k in FWD_PARAMS])


def _output_shape():
    out = _jax.eval_shape(lambda: _forward(_fwd_setup_inputs(0)))
    return out.shape, out.dtype

N_MICROBATCH = 1
ADAM_LR = 0.001
ADAM_B1 = 0.9
ADAM_B2 = 0.999
ADAM_EPS = 1e-08
ADAM_WD = 0.01
ADAM_STEP = 10
PER_EXAMPLE_BATCH_AXIS = {'x': 0, 'loss_target': 0}
SHARED_INPUTS = []
_WEIGHT_DTYPES = {'meta_tokens': _jnp.float32, 'norm1_w': _jnp.float32, 'w_in': _jnp.float32, 'ssd_conv_w': _jnp.float32, 'ssd_conv_b': _jnp.float32, 'ssd_dt_bias': _jnp.float32, 'ssd_a_log': _jnp.float32, 'ssd_d': _jnp.float32, 'ssd_norm_w': _jnp.float32, 'lru_conv_w': _jnp.float32, 'lru_conv_b': _jnp.float32, 'lru_wa': _jnp.float32, 'lru_ba': _jnp.float32, 'lru_wx': _jnp.float32, 'lru_bx': _jnp.float32, 'lru_lambda': _jnp.float32, 'lru_norm_w': _jnp.float32, 'w_out': _jnp.float32, 'norm2_w': _jnp.float32, 'w_gate': _jnp.float32, 'w_up': _jnp.float32, 'w_down': _jnp.float32, 'final_norm_w': _jnp.float32}
MOMENT_SCALE = {'meta_tokens': 4.477679e-03, 'norm1_w': 1.387410e-01, 'w_in': 6.655093e-02, 'ssd_conv_w': 6.014777e-02, 'ssd_conv_b': 8.005138e-02, 'ssd_dt_bias': 9.860091e-02, 'ssd_a_log': 2.418193e-01, 'ssd_d': 3.436036e-01, 'ssd_norm_w': 6.983064e-02, 'lru_conv_w': 7.188320e-02, 'lru_conv_b': 6.425862e-01, 'lru_wa': 2.567597e-02, 'lru_ba': 1.940921e-02, 'lru_wx': 4.745409e-02, 'lru_bx': 2.439692e-02, 'lru_lambda': 3.590407e-02, 'lru_norm_w': 7.045282e-02, 'w_out': 9.571034e-02, 'norm2_w': 7.152622e-02, 'w_gate': 3.029397e-02, 'w_up': 2.938107e-02, 'w_down': 4.863511e-02, 'final_norm_w': 1.600849e+01}


def _to_microbatches(a, axis):
    t = _jnp.moveaxis(a, axis, 0)
    t = t.reshape((N_MICROBATCH, t.shape[0] // N_MICROBATCH) + t.shape[1:])
    return _jnp.moveaxis(t, 1, axis + 1)


def setup_inputs(seed: int = 0) -> dict:
    inp = _fwd_setup_inputs(seed)
    key = _jax.random.fold_in(_jax.random.key(seed), 7919)
    shape, _ = _output_shape()
    out = dict(inp)
    out["loss_target"] = _jax.random.normal(_jax.random.fold_in(key, 0), shape, _jnp.float32)
    for i, name in enumerate(TWIN_WEIGHTS):
        w = inp[name].astype(_jnp.float32)
        if MOMENT_SCALE is None:
            s = _jnp.sqrt(_jnp.mean(_jnp.square(w)) + 1e-30)
        else:
            s = MOMENT_SCALE[name]
        km, kv = _jax.random.split(_jax.random.fold_in(key, i + 1))
        out[name] = w
        out["m_" + name] = s * _jax.random.normal(km, w.shape, _jnp.float32)
        out["v_" + name] = (s * s) * _jax.random.uniform(kv, w.shape, _jnp.float32, 0.5, 1.5)
    if N_MICROBATCH > 1:
        for name, axis in PER_EXAMPLE_BATCH_AXIS.items():
            out[name] = _to_microbatches(out[name], axis)
    return {'x': out['x'], 'meta_tokens': out['meta_tokens'], 'norm1_w': out['norm1_w'], 'w_in': out['w_in'], 'ssd_conv_w': out['ssd_conv_w'], 'ssd_conv_b': out['ssd_conv_b'], 'ssd_dt_bias': out['ssd_dt_bias'], 'ssd_a_log': out['ssd_a_log'], 'ssd_d': out['ssd_d'], 'ssd_norm_w': out['ssd_norm_w'], 'lru_conv_w': out['lru_conv_w'], 'lru_conv_b': out['lru_conv_b'], 'lru_wa': out['lru_wa'], 'lru_ba': out['lru_ba'], 'lru_wx': out['lru_wx'], 'lru_bx': out['lru_bx'], 'lru_lambda': out['lru_lambda'], 'lru_norm_w': out['lru_norm_w'], 'w_out': out['w_out'], 'norm2_w': out['norm2_w'], 'w_gate': out['w_gate'], 'w_up': out['w_up'], 'w_down': out['w_down'], 'final_norm_w': out['final_norm_w'], 'loss_target': out['loss_target'], 'm_meta_tokens': out['m_meta_tokens'], 'm_norm1_w': out['m_norm1_w'], 'm_w_in': out['m_w_in'], 'm_ssd_conv_w': out['m_ssd_conv_w'], 'm_ssd_conv_b': out['m_ssd_conv_b'], 'm_ssd_dt_bias': out['m_ssd_dt_bias'], 'm_ssd_a_log': out['m_ssd_a_log'], 'm_ssd_d': out['m_ssd_d'], 'm_ssd_norm_w': out['m_ssd_norm_w'], 'm_lru_conv_w': out['m_lru_conv_w'], 'm_lru_conv_b': out['m_lru_conv_b'], 'm_lru_wa': out['m_lru_wa'], 'm_lru_ba': out['m_lru_ba'], 'm_lru_wx': out['m_lru_wx'], 'm_lru_bx': out['m_lru_bx'], 'm_lru_lambda': out['m_lru_lambda'], 'm_lru_norm_w': out['m_lru_norm_w'], 'm_w_out': out['m_w_out'], 'm_norm2_w': out['m_norm2_w'], 'm_w_gate': out['m_w_gate'], 'm_w_up': out['m_w_up'], 'm_w_down': out['m_w_down'], 'm_final_norm_w': out['m_final_norm_w'], 'v_meta_tokens': out['v_meta_tokens'], 'v_norm1_w': out['v_norm1_w'], 'v_w_in': out['v_w_in'], 'v_ssd_conv_w': out['v_ssd_conv_w'], 'v_ssd_conv_b': out['v_ssd_conv_b'], 'v_ssd_dt_bias': out['v_ssd_dt_bias'], 'v_ssd_a_log': out['v_ssd_a_log'], 'v_ssd_d': out['v_ssd_d'], 'v_ssd_norm_w': out['v_ssd_norm_w'], 'v_lru_conv_w': out['v_lru_conv_w'], 'v_lru_conv_b': out['v_lru_conv_b'], 'v_lru_wa': out['v_lru_wa'], 'v_lru_ba': out['v_lru_ba'], 'v_lru_wx': out['v_lru_wx'], 'v_lru_bx': out['v_lru_bx'], 'v_lru_lambda': out['v_lru_lambda'], 'v_lru_norm_w': out['v_lru_norm_w'], 'v_w_out': out['v_w_out'], 'v_norm2_w': out['v_norm2_w'], 'v_w_gate': out['v_w_gate'], 'v_w_up': out['v_w_up'], 'v_w_down': out['v_w_down'], 'v_final_norm_w': out['v_final_norm_w']}


def _loss(weights, diff, rest, loss_target):
    with _jax.named_scope("forward"):
        args = {**rest, TWIN_DIFF_INPUT: diff, **{k: w.astype(_WEIGHT_DTYPES[k]) for k, w in weights.items()}}
        y = _forward(args)
    with _jax.named_scope("loss_head"):
        err = _jnp.square(y.astype(_jnp.float32) - loss_target)
        return 0.5 * _jnp.sum(_jnp.mean(err, axis=-1)) if err.ndim else 0.5 * err


def _adamw(w, g, m, v):
    m = ADAM_B1 * m + (1.0 - ADAM_B1) * g
    v = ADAM_B2 * v + (1.0 - ADAM_B2) * _jnp.square(g)
    m_hat = m / (1.0 - ADAM_B1 ** ADAM_STEP)
    v_hat = v / (1.0 - ADAM_B2 ** ADAM_STEP)
    delta = -ADAM_LR * (m_hat / (_jnp.sqrt(v_hat) + ADAM_EPS) + ADAM_WD * w)
    return delta, m, v


def reference(x, meta_tokens, norm1_w, w_in, ssd_conv_w, ssd_conv_b, ssd_dt_bias, ssd_a_log, ssd_d, ssd_norm_w, lru_conv_w, lru_conv_b, lru_wa, lru_ba, lru_wx, lru_bx, lru_lambda, lru_norm_w, w_out, norm2_w, w_gate, w_up, w_down, final_norm_w, loss_target, m_meta_tokens, m_norm1_w, m_w_in, m_ssd_conv_w, m_ssd_conv_b, m_ssd_dt_bias, m_ssd_a_log, m_ssd_d, m_ssd_norm_w, m_lru_conv_w, m_lru_conv_b, m_lru_wa, m_lru_ba, m_lru_wx, m_lru_bx, m_lru_lambda, m_lru_norm_w, m_w_out, m_norm2_w, m_w_gate, m_w_up, m_w_down, m_final_norm_w, v_meta_tokens, v_norm1_w, v_w_in, v_ssd_conv_w, v_ssd_conv_b, v_ssd_dt_bias, v_ssd_a_log, v_ssd_d, v_ssd_norm_w, v_lru_conv_w, v_lru_conv_b, v_lru_wa, v_lru_ba, v_lru_wx, v_lru_bx, v_lru_lambda, v_lru_norm_w, v_w_out, v_norm2_w, v_w_gate, v_w_up, v_w_down, v_final_norm_w):
    given = dict(x=x, meta_tokens=meta_tokens, norm1_w=norm1_w, w_in=w_in, ssd_conv_w=ssd_conv_w, ssd_conv_b=ssd_conv_b, ssd_dt_bias=ssd_dt_bias, ssd_a_log=ssd_a_log, ssd_d=ssd_d, ssd_norm_w=ssd_norm_w, lru_conv_w=lru_conv_w, lru_conv_b=lru_conv_b, lru_wa=lru_wa, lru_ba=lru_ba, lru_wx=lru_wx, lru_bx=lru_bx, lru_lambda=lru_lambda, lru_norm_w=lru_norm_w, w_out=w_out, norm2_w=norm2_w, w_gate=w_gate, w_up=w_up, w_down=w_down, final_norm_w=final_norm_w, loss_target=loss_target, m_meta_tokens=m_meta_tokens, m_norm1_w=m_norm1_w, m_w_in=m_w_in, m_ssd_conv_w=m_ssd_conv_w, m_ssd_conv_b=m_ssd_conv_b, m_ssd_dt_bias=m_ssd_dt_bias, m_ssd_a_log=m_ssd_a_log, m_ssd_d=m_ssd_d, m_ssd_norm_w=m_ssd_norm_w, m_lru_conv_w=m_lru_conv_w, m_lru_conv_b=m_lru_conv_b, m_lru_wa=m_lru_wa, m_lru_ba=m_lru_ba, m_lru_wx=m_lru_wx, m_lru_bx=m_lru_bx, m_lru_lambda=m_lru_lambda, m_lru_norm_w=m_lru_norm_w, m_w_out=m_w_out, m_norm2_w=m_norm2_w, m_w_gate=m_w_gate, m_w_up=m_w_up, m_w_down=m_w_down, m_final_norm_w=m_final_norm_w, v_meta_tokens=v_meta_tokens, v_norm1_w=v_norm1_w, v_w_in=v_w_in, v_ssd_conv_w=v_ssd_conv_w, v_ssd_conv_b=v_ssd_conv_b, v_ssd_dt_bias=v_ssd_dt_bias, v_ssd_a_log=v_ssd_a_log, v_ssd_d=v_ssd_d, v_ssd_norm_w=v_ssd_norm_w, v_lru_conv_w=v_lru_conv_w, v_lru_conv_b=v_lru_conv_b, v_lru_wa=v_lru_wa, v_lru_ba=v_lru_ba, v_lru_wx=v_lru_wx, v_lru_bx=v_lru_bx, v_lru_lambda=v_lru_lambda, v_lru_norm_w=v_lru_norm_w, v_w_out=v_w_out, v_norm2_w=v_norm2_w, v_w_gate=v_w_gate, v_w_up=v_w_up, v_w_down=v_w_down, v_final_norm_w=v_final_norm_w)
    weights = {n: given[n] for n in TWIN_WEIGHTS}
    shared = {n: given[n] for n in SHARED_INPUTS}
    per_example = {n: given[n] for n in ['x']}
    grad_fn = _jax.value_and_grad(_loss, argnums=(0, 1))

    def one_microbatch(ex, loss_target):
        ex = dict(ex)
        diff = ex.pop(TWIN_DIFF_INPUT)
        return grad_fn(weights, diff, {**shared, **ex}, loss_target)

    if N_MICROBATCH == 1:
        loss, (grad_w, grad_x) = one_microbatch(per_example, given["loss_target"])
    else:
        def body(carry, xs):
            loss_sum, grad_sum = carry
            l_k, (gw_k, gx_k) = one_microbatch(xs[0], xs[1])
            with _jax.named_scope("update"):
                return (loss_sum + l_k, _jax.tree.map(_jnp.add, grad_sum, gw_k)), gx_k

        init = (_jnp.zeros((), _jnp.float32), _jax.tree.map(_jnp.zeros_like, weights))
        (loss, grad_w), grad_x = _jax.lax.scan(body, init, (per_example, given["loss_target"]))
    with _jax.named_scope("update"):
        delta_w, new_m, new_v = {}, {}, {}
        for n in TWIN_WEIGHTS:
            delta_w[n], new_m[n], new_v[n] = _adamw(weights[n], grad_w[n], given["m_" + n], given["v_" + n])
    return (loss, grad_x, *[grad_w[n] for n in TWIN_WEIGHTS], *[delta_w[n] for n in TWIN_WEIGHTS],
            *[new_m[n] for n in TWIN_WEIGHTS], *[new_v[n] for n in TWIN_WEIGHTS])
```

```python
import functools
import math

import jax
import jax.numpy as jnp
from jax import lax
from jax.experimental import pallas as pl
from jax.experimental.pallas import tpu as pltpu

F32 = jnp.float32
_MXU = jnp.bfloat16

D_MODEL = 1024
SEQ = 2048
N_META = 16
CHUNK = 128
T_ROWS = 2176
N_CHUNKS = T_ROWS // CHUNK
PAD_ROWS = T_ROWS - SEQ - N_META
X_ROW0 = PAD_ROWS + N_META
SSD_HEADS = 16
SSD_HEAD_DIM = 64
SSD_STATE = 128
SSD_GROUPS = 2
SSD_HPG = SSD_HEADS // SSD_GROUPS
SSD_WIDTH = 1024
LRU_WIDTH = 1024
LRU_C = 8.0
D_FF = 2816
EPS = 1e-6
IN_COLS = 4624
N_SHARDS = 4
N_DEV = 8

ADAM_LR = 0.001
ADAM_B1 = 0.9
ADAM_B2 = 0.999
ADAM_EPS = 1e-08
ADAM_WD = 0.01
ADAM_STEP = 10

VMEM_LIMIT_BYTES = 56 * 1024 * 1024

NN = (((1,), (0,)), ((), ()))
NT = (((1,), (1,)), ((), ()))
TN = (((0,), (0,)), ((), ()))


def _cparams(*sem):
    return pltpu.CompilerParams(dimension_semantics=sem, vmem_limit_bytes=VMEM_LIMIT_BYTES)


def _dot(a, b, dims=NN):
    return lax.dot_general(a.astype(_MXU), b.astype(_MXU), dims, preferred_element_type=F32)


def _dot_exact(a, b, dims=NN):
    return lax.dot_general(a, b, dims, preferred_element_type=F32, precision=lax.Precision.HIGHEST)


def _sigmoid(x):
    return 1.0 / (1.0 + jnp.exp(-x))


def _softplus(x):
    return jnp.maximum(x, 0.0) + jnp.log(1.0 + jnp.exp(-jnp.abs(x)))


def _silu(x):
    return x * _sigmoid(x)


def _silu_grad(x):
    s = _sigmoid(x)
    return s * (1.0 + x * (1.0 - s))


_GELU_C = math.sqrt(2.0 / math.pi)


def _gelu_and_grad(x):
    inner = _GELU_C * (x + 0.044715 * x * x * x)
    t = jnp.tanh(inner)
    g = 0.5 * x * (1.0 + t)
    dg = 0.5 * (1.0 + t) + 0.5 * x * (1.0 - t * t) * _GELU_C * (1.0 + 3.0 * 0.044715 * x * x)
    return g, dg


def _rms_fwd(x, w):
    rstd = lax.rsqrt(jnp.mean(x * x, axis=-1, keepdims=True) + EPS)
    return x * rstd * w


def _rms_bwd(x, w, dy):
    rstd = lax.rsqrt(jnp.mean(x * x, axis=-1, keepdims=True) + EPS)
    xhat = x * rstd
    dxhat = dy * w
    dx = rstd * (dxhat - xhat * jnp.mean(dxhat * xhat, axis=-1, keepdims=True))
    return dx, dy * xhat


def _mm(terms, m, n, *, tm, tn, mode, out_dtype, name, residual=None, n_outer=False):
    gm, gn = m // tm, n // tn
    assert gm * tm == m and gn * tn == n
    if n_outer:
        grid = (gn, gm)
        mi = lambda g0, g1: g1
        ni = lambda g0, g1: g0
    else:
        grid = (gm, gn)
        mi = lambda g0, g1: g0
        ni = lambda g0, g1: g1
    in_specs, args = [], []
    for (a, ka, b, kb, k) in terms:
        if mode == "tn":
            in_specs.append(pl.BlockSpec((k, tm), lambda g0, g1, ka=ka: (ka, mi(g0, g1))))
        else:
            in_specs.append(pl.BlockSpec((tm, k), lambda g0, g1, ka=ka: (mi(g0, g1), ka)))
        if mode == "nt":
            in_specs.append(pl.BlockSpec((tn, k), lambda g0, g1, kb=kb: (ni(g0, g1), kb)))
        else:
            in_specs.append(pl.BlockSpec((k, tn), lambda g0, g1, kb=kb: (kb, ni(g0, g1))))
        args += [a, b]
    if residual is not None:
        in_specs.append(pl.BlockSpec((tm, tn), lambda g0, g1: (mi(g0, g1), ni(g0, g1))))
        args.append(residual)
    dims = {"nn": NN, "nt": NT, "tn": TN}[mode]
    n_terms = len(terms)
    has_res = residual is not None

    def body(*refs):
        o_ref = refs[-1]
        acc = None
        for t in range(n_terms):
            d = lax.dot_general(refs[2 * t][...], refs[2 * t + 1][...], dims, preferred_element_type=F32)
            acc = d if acc is None else acc + d
        if has_res:
            acc = acc + refs[2 * n_terms][...]
        o_ref[...] = acc.astype(out_dtype)

    return pl.pallas_call(
        body, name=name, grid=grid, in_specs=in_specs,
        out_specs=pl.BlockSpec((tm, tn), lambda g0, g1: (mi(g0, g1), ni(g0, g1))),
        out_shape=jax.ShapeDtypeStruct((m, n), out_dtype),
        compiler_params=_cparams("parallel", "parallel"),
    )(*args)


def _embed(x, meta):
    def body(x_ref, meta_ref, o_ref):
        i = pl.program_id(0)

        @pl.when(i == 0)
        def _():
            o_ref[0:PAD_ROWS, :] = jnp.zeros((PAD_ROWS, D_MODEL), F32)
            o_ref[PAD_ROWS:CHUNK, :] = meta_ref[...]

        @pl.when(i > 0)
        def _():
            o_ref[...] = x_ref[...]

    return pl.pallas_call(
        body, name="embed", grid=(N_CHUNKS,),
        in_specs=[pl.BlockSpec((CHUNK, D_MODEL), lambda i: (jnp.maximum(i - 1, 0), 0)),
                  pl.BlockSpec((N_META, D_MODEL), lambda i: (0, 0))],
        out_specs=pl.BlockSpec((CHUNK, D_MODEL), lambda i: (i, 0)),
        out_shape=jax.ShapeDtypeStruct((T_ROWS, D_MODEL), F32),
        compiler_params=_cparams("parallel"),
    )(x, meta)


def _rmsnorm(h, w, *, name, tm=544):
    def body(h_ref, w_ref, o_ref):
        o_ref[...] = _rms_fwd(h_ref[...], w_ref[...]).astype(_MXU)

    return pl.pallas_call(
        body, name=name, grid=(T_ROWS // tm,),
        in_specs=[pl.BlockSpec((tm, D_MODEL), lambda i: (i, 0)), pl.BlockSpec((1, D_MODEL), lambda i: (0, 0))],
        out_specs=pl.BlockSpec((tm, D_MODEL), lambda i: (i, 0)),
        out_shape=jax.ShapeDtypeStruct((T_ROWS, D_MODEL), _MXU),
        compiler_params=_cparams("parallel"),
    )(h, w)


def _loss_head(h2, target, fw):
    def body(h_ref, t_ref, w_ref, loss_ref, dh_ref, dhb_ref, dw_ref, acc_ref):
        i = pl.program_id(0)

        @pl.when(i == 0)
        def _():
            acc_ref[...] = jnp.zeros_like(acc_ref)
            dw_ref[...] = jnp.zeros_like(dw_ref)

        h = h_ref[...]
        w = w_ref[...]
        y = _rms_fwd(h, w)
        live = (i > 0).astype(F32)
        err = (y - t_ref[...]) * live
        acc_ref[...] += jnp.sum(err * err, axis=0, keepdims=True)
        dy = err * (1.0 / D_MODEL)
        dx, dwr = _rms_bwd(h, w, dy)
        dh_ref[...] = dx
        dhb_ref[...] = dx.astype(_MXU)
        dw_ref[...] += jnp.sum(dwr, axis=0, keepdims=True)

        @pl.when(i == N_CHUNKS - 1)
        def _():
            tot = jnp.sum(acc_ref[...], axis=1, keepdims=True) * (0.5 / D_MODEL)
            loss_ref[...] = jnp.broadcast_to(tot, (1, 128))

    return pl.pallas_call(
        body, name="loss_head", grid=(N_CHUNKS,),
        in_specs=[pl.BlockSpec((CHUNK, D_MODEL), lambda i: (i, 0)),
                  pl.BlockSpec((CHUNK, D_MODEL), lambda i: (jnp.maximum(i - 1, 0), 0)),
                  pl.BlockSpec((1, D_MODEL), lambda i: (0, 0))],
        out_specs=[pl.BlockSpec((1, 128), lambda i: (0, 0)),
                   pl.BlockSpec((CHUNK, D_MODEL), lambda i: (i, 0)),
                   pl.BlockSpec((CHUNK, D_MODEL), lambda i: (i, 0)),
                   pl.BlockSpec((1, D_MODEL), lambda i: (0, 0))],
        out_shape=[jax.ShapeDtypeStruct((1, 128), F32),
                   jax.ShapeDtypeStruct((T_ROWS, D_MODEL), F32),
                   jax.ShapeDtypeStruct((T_ROWS, D_MODEL), _MXU),
                   jax.ShapeDtypeStruct((1, D_MODEL), F32)],
        scratch_shapes=[pltpu.VMEM((1, D_MODEL), F32)],
        compiler_params=_cparams("arbitrary"),
    )(h2, target, fw)


def _mm_nt_norm_bwd(terms, h, w, dres, *, name, tm=272):
    n_terms = len(terms)
    in_specs, args = [], []
    for (a, b, k) in terms:
        in_specs += [pl.BlockSpec((tm, k), lambda i: (i, 0)), pl.BlockSpec((D_MODEL, k), lambda i: (0, 0))]
        args += [a, b]
    in_specs += [pl.BlockSpec((tm, D_MODEL), lambda i: (i, 0)), pl.BlockSpec((1, D_MODEL), lambda i: (0, 0)),
                 pl.BlockSpec((tm, D_MODEL), lambda i: (i, 0))]
    args += [h, w, dres]

    def body(*refs):
        h_ref, w_ref, dres_ref, dh_ref, dhb_ref, dw_ref = refs[2 * n_terms:]

        @pl.when(pl.program_id(0) == 0)
        def _():
            dw_ref[...] = jnp.zeros_like(dw_ref)

        du = None
        for t in range(n_terms):
            d = lax.dot_general(refs[2 * t][...], refs[2 * t + 1][...], NT, preferred_element_type=F32)
            du = d if du is None else du + d
        dx, dwr = _rms_bwd(h_ref[...], w_ref[...], du)
        dh = dres_ref[...] + dx
        dh_ref[...] = dh
        dhb_ref[...] = dh.astype(_MXU)
        dw_ref[...] += jnp.sum(dwr, axis=0, keepdims=True)

    return pl.pallas_call(
        body, name=name, grid=(T_ROWS // tm,), in_specs=in_specs,
        out_specs=[pl.BlockSpec((tm, D_MODEL), lambda i: (i, 0)), pl.BlockSpec((tm, D_MODEL), lambda i: (i, 0)),
                   pl.BlockSpec((1, D_MODEL), lambda i: (0, 0))],
        out_shape=[jax.ShapeDtypeStruct((T_ROWS, D_MODEL), F32), jax.ShapeDtypeStruct((T_ROWS, D_MODEL), _MXU),
                   jax.ShapeDtypeStruct((1, D_MODEL), F32)],
        compiler_params=_cparams("arbitrary"),
    )(*args)


FFN_TM = 272
FFN_TN = 1408


def _ffn_up(u2, wg, wu):
    def body(u_ref, wg_ref, wu_ref, gp_ref, up_ref, act_ref):
        u = u_ref[...]
        gp = lax.dot_general(u, wg_ref[...], NN, preferred_element_type=F32)
        up = lax.dot_general(u, wu_ref[...], NN, preferred_element_type=F32)
        gp_ref[...] = gp
        up_ref[...] = up
        act_ref[...] = (_silu(gp) * up).astype(_MXU)

    tile = pl.BlockSpec((FFN_TM, FFN_TN), lambda j, i: (i, j))
    return pl.pallas_call(
        body, name="ffn_up", grid=(D_FF // FFN_TN, T_ROWS // FFN_TM),
        in_specs=[pl.BlockSpec((FFN_TM, D_MODEL), lambda j, i: (i, 0)),
                  pl.BlockSpec((D_MODEL, FFN_TN), lambda j, i: (0, j)),
                  pl.BlockSpec((D_MODEL, FFN_TN), lambda j, i: (0, j))],
        out_specs=[tile, tile, tile],
        out_shape=[jax.ShapeDtypeStruct((T_ROWS, D_FF), F32), jax.ShapeDtypeStruct((T_ROWS, D_FF), F32),
                   jax.ShapeDtypeStruct((T_ROWS, D_FF), _MXU)],
        compiler_params=_cparams("parallel", "parallel"),
    )(u2, wg, wu)


def _ffn_bwd_act(dh2b, wd, gp, up):
    def body(dh_ref, wd_ref, gp_ref, up_ref, dgp_ref, dup_ref):
        dact = lax.dot_general(dh_ref[...], wd_ref[...], NT, preferred_element_type=F32)
        gp = gp_ref[...]
        dgp_ref[...] = (dact * up_ref[...] * _silu_grad(gp)).astype(_MXU)
        dup_ref[...] = (dact * _silu(gp)).astype(_MXU)

    tile = pl.BlockSpec((FFN_TM, FFN_TN), lambda j, i: (i, j))
    return pl.pallas_call(
        body, name="ffn_bwd_act", grid=(D_FF // FFN_TN, T_ROWS // FFN_TM),
        in_specs=[pl.BlockSpec((FFN_TM, D_MODEL), lambda j, i: (i, 0)),
                  pl.BlockSpec((FFN_TN, D_MODEL), lambda j, i: (j, 0)), tile, tile],
        out_specs=[tile, tile],
        out_shape=[jax.ShapeDtypeStruct((T_ROWS, D_FF), _MXU), jax.ShapeDtypeStruct((T_ROWS, D_FF), _MXU)],
        compiler_params=_cparams("parallel", "parallel"),
    )(dh2b, wd, gp, up)


CONV_TC = 512
CONV_K = 4


def _conv_pre(x_ref, wv, bv, c):
    tc = wv.shape[1]
    r0 = c * CHUNK
    cur = x_ref[r0:r0 + CHUNK, :]
    prev8 = jnp.zeros((8, tc), F32) if c == 0 else x_ref[r0 - 8:r0, :]
    cat = jnp.concatenate([prev8, cur], axis=0)
    shifted = [cur] + [pltpu.roll(cat, s, 0)[8:8 + CHUNK] for s in range(1, CONV_K)]
    pre = bv
    for s in range(CONV_K):
        pre = pre + shifted[s] * wv[CONV_K - 1 - s:CONV_K - s]
    return pre, shifted


def _row_mask(c):
    if c > 0:
        return None
    return (lax.broadcasted_iota(jnp.int32, (CHUNK, 1), 0) >= PAD_ROWS).astype(F32)


def _conv_fwd(x, w, b, *, silu, name):
    cols = x.shape[1]
    tc = min(CONV_TC, cols)

    def body(x_ref, w_ref, b_ref, o_ref):
        wv, bv = w_ref[...], b_ref[...]
        for c in range(N_CHUNKS):
            pre, _ = _conv_pre(x_ref, wv, bv, c)
            y = _silu(pre) if silu else pre
            mask = _row_mask(c)
            if mask is not None:
                y = y * mask
            o_ref[c * CHUNK:(c + 1) * CHUNK, :] = y

    return pl.pallas_call(
        body, name=name, grid=(cols // tc,),
        in_specs=[pl.BlockSpec((T_ROWS, tc), lambda j: (0, j)), pl.BlockSpec((CONV_K, tc), lambda j: (0, j)),
                  pl.BlockSpec((1, tc), lambda j: (0, j))],
        out_specs=pl.BlockSpec((T_ROWS, tc), lambda j: (0, j)),
        out_shape=jax.ShapeDtypeStruct((T_ROWS, cols), F32),
        compiler_params=_cparams("parallel"),
    )(x, w, b)


def _conv_bwd(dy, x, w, b, *, silu, name):
    cols = x.shape[1]
    tc = min(CONV_TC, cols)

    def body(dy_ref, x_ref, w_ref, b_ref, dx_ref, dw_ref, db_ref):
        wv, bv = w_ref[...], b_ref[...]
        next8 = jnp.zeros((8, tc), F32)
        dws = [jnp.zeros((1, tc), F32) for _ in range(CONV_K)]
        db = jnp.zeros((1, tc), F32)
        for c in reversed(range(N_CHUNKS)):
            pre, shifted = _conv_pre(x_ref, wv, bv, c)
            dpre = dy_ref[c * CHUNK:(c + 1) * CHUNK, :]
            if silu:
                dpre = dpre * _silu_grad(pre)
            mask = _row_mask(c)
            if mask is not None:
                dpre = dpre * mask
            cat = jnp.concatenate([dpre, next8], axis=0)
            dx = dpre * wv[CONV_K - 1:CONV_K]
            for s in range(1, CONV_K):
                dx = dx + pltpu.roll(cat, CHUNK + 8 - s, 0)[0:CHUNK] * wv[CONV_K - 1 - s:CONV_K - s]
            dx_ref[c * CHUNK:(c + 1) * CHUNK, :] = dx.astype(_MXU)
            for s in range(CONV_K):
                k = CONV_K - 1 - s
                dws[k] = dws[k] + jnp.sum(dpre * shifted[s], axis=0, keepdims=True)
            db = db + jnp.sum(dpre, axis=0, keepdims=True)
            next8 = dpre[0:8]
        dw_ref[...] = jnp.concatenate(dws, axis=0)
        db_ref[...] = db

    return pl.pallas_call(
        body, name=name, grid=(cols // tc,),
        in_specs=[pl.BlockSpec((T_ROWS, tc), lambda j: (0, j)), pl.BlockSpec((T_ROWS, tc), lambda j: (0, j)),
                  pl.BlockSpec((CONV_K, tc), lambda j: (0, j)), pl.BlockSpec((1, tc), lambda j: (0, j))],
        out_specs=[pl.BlockSpec((T_ROWS, tc), lambda j: (0, j)), pl.BlockSpec((CONV_K, tc), lambda j: (0, j)),
                   pl.BlockSpec((1, tc), lambda j: (0, j))],
        out_shape=[jax.ShapeDtypeStruct((T_ROWS, cols), _MXU), jax.ShapeDtypeStruct((CONV_K, cols), F32),
                   jax.ShapeDtypeStruct((1, cols), F32)],
        compiler_params=_cparams("parallel"),
    )(dy, x, w, b)


def _ssd_chunk_common(dt_raw, prm, c):
    a_row = -jnp.exp(prm[1:2])
    dt = _softplus(dt_raw + prm[0:1])
    rows = lax.broadcasted_iota(jnp.int32, (CHUNK, 1), 0)
    real = jnp.logical_or(c > 0, rows >= PAD_ROWS)
    dt = jnp.where(real, dt, 0.0)
    li = lax.broadcasted_iota(jnp.int32, (CHUNK, CHUNK), 0)
    si = lax.broadcasted_iota(jnp.int32, (CHUNK, CHUNK), 1)
    causal = li >= si
    tri = causal.astype(F32)
    cs = _dot_exact(tri, dt * a_row)
    return dt, a_row, cs, cs.T, causal, tri, real


def _gated_norm_fwd(y, z, w):
    g = y * _silu(z)
    half = SSD_WIDTH // SSD_GROUPS
    outs = [_rms_fwd(g[:, k * half:(k + 1) * half], w[:, k * half:(k + 1) * half]) for k in range(SSD_GROUPS)]
    return jnp.concatenate(outs, axis=1)


def _ssd_fwd(xs, bc, dt_raw, z, prm, norm_w):
    def body(xs_ref, bc_ref, dt_ref, z_ref, prm_ref, nw_ref, y_ref, yn_ref, prev_ref, state):
        c = pl.program_id(0)

        @pl.when(c == 0)
        def _():
            state[...] = jnp.zeros_like(state)

        prm = prm_ref[...]
        dt, a_row, cs, cs_t, causal, _, _ = _ssd_chunk_common(dt_ref[...], prm, c)
        xs_all = xs_ref[...]
        bc_all = bc_ref[...]
        for g in range(SSD_GROUPS):
            b_g = bc_all[:, g * SSD_STATE:(g + 1) * SSD_STATE]
            c_g = bc_all[:, (SSD_GROUPS + g) * SSD_STATE:(SSD_GROUPS + g + 1) * SSD_STATE]
            cb = _dot(c_g, b_g, NT)
            for j in range(SSD_HPG):
                h = g * SSD_HPG + j
                lo, hi = h * SSD_HEAD_DIM, (h + 1) * SSD_HEAD_DIM
                cs_col = cs[:, h:h + 1]
                cs_last = cs[CHUNK - 1:CHUNK, h:h + 1]
                lmat = jnp.where(causal, jnp.exp(cs_col - cs_t[h:h + 1, :]), 0.0)
                xs_h = xs_all[:, lo:hi]
                xdt = xs_h * dt[:, h:h + 1]
                prev = state[h]
                y = _dot(cb * lmat, xdt)
                y = y + _dot(c_g, prev, NT) * jnp.exp(cs_col)
                y = y + xs_h * prm[2:3, h:h + 1]
                y_ref[:, lo:hi] = y
                prev_ref[0, h] = prev
                state[h] = prev * jnp.exp(cs_last) + _dot(xdt * jnp.exp(cs_last - cs_col), b_g, TN)
        yn_ref[...] = _gated_norm_fwd(y_ref[...], z_ref[...], nw_ref[...]).astype(_MXU)

    row = lambda w: pl.BlockSpec((CHUNK, w), lambda c: (c, 0))
    return pl.pallas_call(
        body, name="ssd_fwd", grid=(N_CHUNKS,),
        in_specs=[row(SSD_WIDTH), row(512), row(128), row(SSD_WIDTH),
                  pl.BlockSpec((8, 128), lambda c: (0, 0)), pl.BlockSpec((1, SSD_WIDTH), lambda c: (0, 0))],
        out_specs=[row(SSD_WIDTH), row(SSD_WIDTH),
                   pl.BlockSpec((1, SSD_HEADS, SSD_HEAD_DIM, SSD_STATE), lambda c: (c, 0, 0, 0))],
        out_shape=[jax.ShapeDtypeStruct((T_ROWS, SSD_WIDTH), F32), jax.ShapeDtypeStruct((T_ROWS, SSD_WIDTH), _MXU),
                   jax.ShapeDtypeStruct((N_CHUNKS, SSD_HEADS, SSD_HEAD_DIM, SSD_STATE), F32)],
        scratch_shapes=[pltpu.VMEM((SSD_HEADS, SSD_HEAD_DIM, SSD_STATE), F32)],
        compiler_params=_cparams("arbitrary"),
    )(xs, bc, dt_raw, z, prm, norm_w)


def _ssd_bwd(dyn, dyn_block, z, y_pre, xs, bc, dt_raw, prev, prm, norm_w):
    def body(dyn_ref, z_ref, y_ref, xs_ref, bc_ref, dt_ref, prev_ref, prm_ref, nw_ref,
             dz_ref, dxs_ref, dbc_ref, ddt_ref, dprm_ref, dnw_ref, dstate):
        step = pl.program_id(0)
        c = N_CHUNKS - 1 - step

        @pl.when(step == 0)
        def _():
            dstate[...] = jnp.zeros_like(dstate)
            dprm_ref[...] = jnp.zeros_like(dprm_ref)
            dnw_ref[...] = jnp.zeros_like(dnw_ref)

        prm = prm_ref[...]
        dt, a_row, cs, cs_t, causal, tri, real = _ssd_chunk_common(dt_ref[...], prm, c)
        realf = real.astype(F32)
        z = z_ref[...]
        y_all = y_ref[...]
        nw = nw_ref[...]
        dyn_all = dyn_ref[...]
        sz = _silu(z)
        gated = y_all * sz
        half = SSD_WIDTH // SSD_GROUPS
        dgs, dnws = [], []
        for k in range(SSD_GROUPS):
            sl = slice(k * half, (k + 1) * half)
            dgk, dwk = _rms_bwd(gated[:, sl], nw[:, sl], dyn_all[:, sl])
            dgs.append(dgk)
            dnws.append(jnp.sum(dwk, axis=0, keepdims=True))
        dgated = jnp.concatenate(dgs, axis=1)
        dnw_ref[...] += jnp.concatenate(dnws, axis=1)
        dz_ref[...] = (dgated * y_all * _silu_grad(z)).astype(_MXU)
        dy_all = dgated * sz

        xs_all = xs_ref[...]
        bc_all = bc_ref[...]
        lane = lax.broadcasted_iota(jnp.int32, (1, 128), 1)
        dcs_all = jnp.zeros((CHUNK, 128), F32)
        ddt_all = jnp.zeros((CHUNK, 128), F32)
        dd_row = jnp.zeros((1, 128), F32)
        last_row = (lax.broadcasted_iota(jnp.int32, (CHUNK, 1), 0) == CHUNK - 1).astype(F32)
        dbs, dcs_ = [], []
        for g in range(SSD_GROUPS):
            b_g = bc_all[:, g * SSD_STATE:(g + 1) * SSD_STATE]
            c_g = bc_all[:, (SSD_GROUPS + g) * SSD_STATE:(SSD_GROUPS + g + 1) * SSD_STATE]
            cb = _dot(c_g, b_g, NT)
            db_g = jnp.zeros((CHUNK, SSD_STATE), F32)
            dc_g = jnp.zeros((CHUNK, SSD_STATE), F32)
            dcb = jnp.zeros((CHUNK, CHUNK), F32)
            for j in range(SSD_HPG):
                h = g * SSD_HPG + j
                lo, hi = h * SSD_HEAD_DIM, (h + 1) * SSD_HEAD_DIM
                onehot = (lane == h).astype(F32)
                cs_col = cs[:, h:h + 1]
                cs_last = cs[CHUNK - 1:CHUNK, h:h + 1]
                e_cs = jnp.exp(cs_col)
                e_last = jnp.exp(cs_last)
                dec = jnp.exp(cs_last - cs_col)
                lmat = jnp.where(causal, jnp.exp(cs_col - cs_t[h:h + 1, :]), 0.0)
                m = cb * lmat
                dt_h = dt[:, h:h + 1]
                xs_h = xs_all[:, lo:hi]
                xdt = xs_h * dt_h
                dy = dy_all[:, lo:hi]
                prevh = prev_ref[0, h]
                dst = dstate[h]
                dd_row = dd_row + jnp.sum(dy * xs_h, keepdims=True) * onehot
                dcp = dy * e_cs
                dc_g = dc_g + _dot(dcp, prevh)
                dprev = _dot(dcp, c_g, TN)
                y_off = _dot(c_g, prevh, NT) * e_cs
                dcs_h = jnp.sum(dy * y_off, axis=1, keepdims=True)
                dm = jnp.where(causal, _dot(dy, xdt, NT), 0.0)
                dxdt = _dot(m, dy, TN)
                dseg = dm * m
                dcs_h = dcs_h + jnp.sum(dseg, axis=1, keepdims=True) - jnp.sum(dseg.T, axis=1, keepdims=True)
                dcb = dcb + dm * lmat
                db_g = db_g + _dot(xdt * dec, dst)
                bds = _dot(b_g, dst, NT)
                dxdt = dxdt + bds * dec
                v = jnp.sum(xdt * bds, axis=1, keepdims=True) * dec
                dlast = jnp.sum(v, keepdims=True) + jnp.sum(dst * prevh, keepdims=True) * e_last
                dcs_h = dcs_h - v + last_row * dlast
                dstate[h] = dst * e_last + dprev
                dxs_ref[:, lo:hi] = (dxdt * dt_h + dy * prm[2:3, h:h + 1]) * realf
                ddt_all = ddt_all + jnp.sum(dxdt * xs_h, axis=1, keepdims=True) * onehot
                dcs_all = dcs_all + dcs_h * onehot
            dc_g = dc_g + _dot(dcb, b_g)
            db_g = db_g + _dot(dcb, c_g, TN)
            dbs.append(db_g * realf)
            dcs_.append(dc_g * realf)
        dbc_ref[...] = jnp.concatenate(dbs + dcs_, axis=1)
        dda = _dot_exact(tri, dcs_all, TN)
        ddt = (ddt_all + dda * a_row) * realf
        ddt_raw = ddt * _sigmoid(dt_ref[...] + prm[0:1])
        ddt_ref[...] = ddt_raw.astype(_MXU)
        da_log = jnp.sum(dda * dt, axis=0, keepdims=True) * a_row
        dprm_ref[0:1, :] += jnp.sum(ddt_raw, axis=0, keepdims=True)
        dprm_ref[1:2, :] += da_log
        dprm_ref[2:3, :] += dd_row

    rev = lambda w, blk=0: pl.BlockSpec((CHUNK, w), lambda s, blk=blk: (N_CHUNKS - 1 - s, blk))
    return pl.pallas_call(
        body, name="ssd_bwd", grid=(N_CHUNKS,),
        in_specs=[rev(SSD_WIDTH, dyn_block), rev(SSD_WIDTH), rev(SSD_WIDTH), rev(SSD_WIDTH), rev(512), rev(128),
                  pl.BlockSpec((1, SSD_HEADS, SSD_HEAD_DIM, SSD_STATE), lambda s: (N_CHUNKS - 1 - s, 0, 0, 0)),
                  pl.BlockSpec((8, 128), lambda s: (0, 0)), pl.BlockSpec((1, SSD_WIDTH), lambda s: (0, 0))],
        out_specs=[rev(SSD_WIDTH), rev(SSD_WIDTH), rev(512), rev(128),
                   pl.BlockSpec((8, 128), lambda s: (0, 0)), pl.BlockSpec((1, SSD_WIDTH), lambda s: (0, 0))],
        out_shape=[jax.ShapeDtypeStruct((T_ROWS, SSD_WIDTH), _MXU), jax.ShapeDtypeStruct((T_ROWS, SSD_WIDTH), F32),
                   jax.ShapeDtypeStruct((T_ROWS, 512), F32), jax.ShapeDtypeStruct((T_ROWS, 128), _MXU),
                   jax.ShapeDtypeStruct((8, 128), F32), jax.ShapeDtypeStruct((1, SSD_WIDTH), F32)],
        scratch_shapes=[pltpu.VMEM((SSD_HEADS, SSD_HEAD_DIM, SSD_STATE), F32)],
        compiler_params=_cparams("arbitrary"),
    )(dyn, z, y_pre, xs, bc, dt_raw, prev, prm, norm_w)


LRU_PAIRS = 8


def _lru_gates(xr, wa_ref, wx_ref, prm):
    pre_r, pre_i = [], []
    for k in range(LRU_PAIRS):
        xk = xr[:, k * 128:(k + 1) * 128]
        pre_r.append(_dot(xk, wa_ref[k]))
        pre_i.append(_dot(xk, wx_ref[k]))
    r = _sigmoid(jnp.concatenate(pre_r, axis=1) + prm[0:1])
    i = _sigmoid(jnp.concatenate(pre_i, axis=1) + prm[1:2])
    sp = _softplus(-prm[2:3])
    log_a = (-LRU_C) * r * sp
    a = jnp.exp(log_a)
    s = jnp.sqrt(-jnp.tanh(log_a) * (a * a + 1.0))
    return r, i, a, s, sp


def _lru_fwd(xr, gate, wa, wx, prm):
    def body(xr_ref, g_ref, wa_ref, wx_ref, prm_ref, hs_ref, yn_ref, carry, a_s, u_s):
        @pl.when(pl.program_id(0) == 0)
        def _():
            carry[...] = jnp.zeros_like(carry)

        prm = prm_ref[...]
        xr_t = xr_ref[...]
        _, i, a, s, _ = _lru_gates(xr_t, wa_ref, wx_ref, prm)
        a_s[...] = a
        u_s[...] = s * (i * xr_t)
        rid = lax.broadcasted_iota(jnp.int32, (8, LRU_WIDTH), 0)

        def group(k, h):
            off = pl.multiple_of(k * 8, 8)
            a8 = a_s[pl.ds(off, 8), :]
            u8 = u_s[pl.ds(off, 8), :]
            out = jnp.zeros((8, LRU_WIDTH), F32)
            for r_ in range(8):
                h = a8[r_:r_ + 1] * h + u8[r_:r_ + 1]
                out = jnp.where(rid == r_, h, out)
            hs_ref[pl.ds(off, 8), :] = out
            return h

        carry[0:1, :] = lax.fori_loop(0, CHUNK // 8, group, carry[0:1, :])
        gel, _ = _gelu_and_grad(g_ref[...])
        yn_ref[...] = _rms_fwd(gel * hs_ref[...], prm[3:4]).astype(_MXU)

    row = pl.BlockSpec((CHUNK, LRU_WIDTH), lambda t: (t, 0))
    wspec = pl.BlockSpec((LRU_PAIRS, 128, 128), lambda t: (0, 0, 0))
    return pl.pallas_call(
        body, name="lru_fwd", grid=(N_CHUNKS,),
        in_specs=[row, row, wspec, wspec, pl.BlockSpec((8, LRU_WIDTH), lambda t: (0, 0))],
        out_specs=[row, row],
        out_shape=[jax.ShapeDtypeStruct((T_ROWS, LRU_WIDTH), F32), jax.ShapeDtypeStruct((T_ROWS, LRU_WIDTH), _MXU)],
        scratch_shapes=[pltpu.VMEM((8, LRU_WIDTH), F32), pltpu.VMEM((CHUNK, LRU_WIDTH), F32),
                        pltpu.VMEM((CHUNK, LRU_WIDTH), F32)],
        compiler_params=_cparams("arbitrary"),
    )(xr, gate, wa, wx, prm)


def _lru_bwd(dyn, dyn_block, gate, xr, hs, wa, wx, wa_t, wx_t, prm):
    def body(dyn_ref, g_ref, xr_ref, hs_ref, hsp_ref, wa_ref, wx_ref, wat_ref, wxt_ref, prm_ref,
             dg_ref, dxr_ref, dwa_ref, dwx_ref, dprm_ref, carry, a_s, d_s):
        step = pl.program_id(0)
        tile = N_CHUNKS - 1 - step

        @pl.when(step == 0)
        def _():
            carry[...] = jnp.zeros_like(carry)
            dwa_ref[...] = jnp.zeros_like(dwa_ref)
            dwx_ref[...] = jnp.zeros_like(dwx_ref)
            dprm_ref[...] = jnp.zeros_like(dprm_ref)

        prm = prm_ref[...]
        xr_t = xr_ref[...]
        r, i, a, s, sp = _lru_gates(xr_t, wa_ref, wx_ref, prm)
        hs_t = hs_ref[...]
        gel, dgel = _gelu_and_grad(g_ref[...])
        dy, dnw = _rms_bwd(gel * hs_t, prm[3:4], dyn_ref[...])
        dg_ref[...] = (dy * hs_t * dgel).astype(_MXU)
        a_s[...] = a
        d_s[...] = dy * gel
        rid = lax.broadcasted_iota(jnp.int32, (8, LRU_WIDTH), 0)

        def group(k, cr):
            off = pl.multiple_of((CHUNK // 8 - 1 - k) * 8, 8)
            a8 = a_s[pl.ds(off, 8), :]
            d8 = d_s[pl.ds(off, 8), :]
            out = jnp.zeros((8, LRU_WIDTH), F32)
            for r_ in reversed(range(8)):
                dht = d8[r_:r_ + 1] + cr
                out = jnp.where(rid == r_, dht, out)
                cr = a8[r_:r_ + 1] * dht
            d_s[pl.ds(off, 8), :] = out
            return cr

        carry[0:1, :] = lax.fori_loop(0, CHUNK // 8, group, carry[0:1, :])
        dht = d_s[...]
        before = hsp_ref[CHUNK - 8:CHUNK, :][7:8] * (tile > 0).astype(F32)
        first = lax.broadcasted_iota(jnp.int32, (CHUNK, 1), 0) == 0
        hprev = jnp.where(first, before, pltpu.roll(hs_t, 1, 0))
        da = dht * hprev
        ixr = i * xr_t
        ds = dht * ixr
        dlog_a = da * a - ds * (a * a) / s
        dr = dlog_a * ((-LRU_C) * sp)
        dsp = jnp.sum(dlog_a * ((-LRU_C) * r), axis=0, keepdims=True)
        dlam = dsp * (-_sigmoid(-prm[2:3]))
        di = dht * s * xr_t
        dpre_r = dr * r * (1.0 - r)
        dpre_i = di * i * (1.0 - i)
        dxr = dht * s * i
        parts = []
        for k in range(LRU_PAIRS):
            sl = slice(k * 128, (k + 1) * 128)
            parts.append(_dot(dpre_r[:, sl], wat_ref[k]) + _dot(dpre_i[:, sl], wxt_ref[k]))
            dwa_ref[k] += _dot(xr_t[:, sl], dpre_r[:, sl], TN)
            dwx_ref[k] += _dot(xr_t[:, sl], dpre_i[:, sl], TN)
        dxr_ref[...] = dxr + jnp.concatenate(parts, axis=1)
        dprm_ref[0:1, :] += jnp.sum(dpre_r, axis=0, keepdims=True)
        dprm_ref[1:2, :] += jnp.sum(dpre_i, axis=0, keepdims=True)
        dprm_ref[2:3, :] += dlam
        dprm_ref[3:4, :] += jnp.sum(dnw, axis=0, keepdims=True)

    rev = lambda blk=0: pl.BlockSpec((CHUNK, LRU_WIDTH), lambda s, blk=blk: (N_CHUNKS - 1 - s, blk))
    wspec = pl.BlockSpec((LRU_PAIRS, 128, 128), lambda s: (0, 0, 0))
    return pl.pallas_call(
        body, name="lru_bwd", grid=(N_CHUNKS,),
        in_specs=[rev(dyn_block), rev(), rev(), rev(),
                  pl.BlockSpec((CHUNK, LRU_WIDTH), lambda s: (jnp.maximum(N_CHUNKS - 2 - s, 0), 0)),
                  wspec, wspec, wspec, wspec, pl.BlockSpec((8, LRU_WIDTH), lambda s: (0, 0))],
        out_specs=[rev(), rev(), wspec, wspec, pl.BlockSpec((8, LRU_WIDTH), lambda s: (0, 0))],
        out_shape=[jax.ShapeDtypeStruct((T_ROWS, LRU_WIDTH), _MXU), jax.ShapeDtypeStruct((T_ROWS, LRU_WIDTH), F32),
                   jax.ShapeDtypeStruct((LRU_PAIRS, 128, 128), F32), jax.ShapeDtypeStruct((LRU_PAIRS, 128, 128), F32),
                   jax.ShapeDtypeStruct((8, LRU_WIDTH), F32)],
        scratch_shapes=[pltpu.VMEM((8, LRU_WIDTH), F32), pltpu.VMEM((CHUNK, LRU_WIDTH), F32),
                        pltpu.VMEM((CHUNK, LRU_WIDTH), F32)],
        compiler_params=_cparams("arbitrary"),
    )(dyn, gate, xr, hs, hs, wa, wx, wa_t, wx_t, prm)


SEC_NAMES = ("z", "xs", "bc", "dt", "g", "x")
SEC_WIDTH = {"z": 1024, "xs": 1024, "bc": 512, "dt": 128, "g": 1024, "x": 1024}


def _pair_blocks(w):
    w = w.reshape(LRU_PAIRS, 2, 64, 64)
    zero = jnp.zeros((LRU_PAIRS, 64, 64), w.dtype)
    top = jnp.concatenate([w[:, 0], zero], axis=2)
    bot = jnp.concatenate([zero, w[:, 1]], axis=2)
    return jnp.concatenate([top, bot], axis=1)


def _unpair_blocks(wp):
    return jnp.stack([wp[:, :64, :64], wp[:, 64:, 64:]], axis=1).reshape(16, 64, 64)


def _pad_lanes(v, width=128):
    return jnp.pad(v, ((0, 0), (0, width - v.shape[1])))


def _local_step(x, target, meta, p):
    g = {}
    h0 = _embed(x, meta)
    u1 = _rmsnorm(h0, p["norm1_w"], name="norm1")
    proj = {}
    for s in SEC_NAMES:
        wdt = SEC_WIDTH[s]
        proj[s] = _mm([(u1, 0, p["w_in_" + s], 0, D_MODEL)], T_ROWS, wdt, tm=544, tn=min(wdt, 512), mode="nn",
                      out_dtype=F32, name="proj_" + s)
    ssd_prm = jnp.concatenate([_pad_lanes(p["ssd_dt_bias"]), _pad_lanes(p["ssd_a_log"]), _pad_lanes(p["ssd_d"]),
                               jnp.zeros((5, 128), F32)], axis=0)
    xs_act = _conv_fwd(proj["xs"], p["ssd_conv_w"][:, :SSD_WIDTH], p["ssd_conv_b"][:, :SSD_WIDTH], silu=True,
                       name="ssd_conv_xs")
    bc_act = _conv_fwd(proj["bc"], p["ssd_conv_w"][:, SSD_WIDTH:], p["ssd_conv_b"][:, SSD_WIDTH:], silu=True,
                       name="ssd_conv_bc")
    y_pre, y_ssd, prev = _ssd_fwd(xs_act, bc_act, proj["dt"], proj["z"], ssd_prm, p["ssd_norm_w"])
    xr = _conv_fwd(proj["x"], p["lru_conv_w"], p["lru_conv_b"], silu=False, name="lru_conv")
    wa_p, wx_p = _pair_blocks(p["lru_wa"]), _pair_blocks(p["lru_wx"])
    lru_prm = jnp.concatenate([p["lru_ba"], p["lru_bx"], p["lru_lambda"], p["lru_norm_w"],
                               jnp.zeros((4, LRU_WIDTH), F32)], axis=0)
    hs, y_lru = _lru_fwd(xr, proj["g"], wa_p.astype(_MXU), wx_p.astype(_MXU), lru_prm)
    h1 = _mm([(y_ssd, 0, p["w_out"], 0, SSD_WIDTH), (y_lru, 0, p["w_out"], 1, LRU_WIDTH)], T_ROWS, D_MODEL,
             tm=544, tn=512, mode="nn", out_dtype=F32, name="out_proj", residual=h0)
    u2 = _rmsnorm(h1, p["norm2_w"], name="norm2")
    gp, up, act = _ffn_up(u2, p["w_gate"], p["w_up"])
    h2 = _mm([(act, 0, p["w_down"], 0, D_FF)], T_ROWS, D_MODEL, tm=544, tn=512, mode="nn", out_dtype=F32,
             name="ffn_down", residual=h1)
    loss, dh2, dh2b, g["final_norm_w"] = _loss_head(h2, target, p["final_norm_w"])
    dgp, dup = _ffn_bwd_act(dh2b, p["w_down"], gp, up)
    g["w_down"] = _mm([(act, 0, dh2b, 0, T_ROWS)], D_FF, D_MODEL, tm=1408, tn=512, mode="tn", out_dtype=F32,
                      name="dw_down")
    dh1, dh1b, g["norm2_w"] = _mm_nt_norm_bwd([(dgp, p["w_gate"], D_FF), (dup, p["w_up"], D_FF)], h1, p["norm2_w"],
                                              dh2, name="ffn_bwd_in")
    g["w_gate"] = _mm([(u2, 0, dgp, 0, T_ROWS)], D_MODEL, D_FF, tm=512, tn=1408, mode="tn", out_dtype=F32,
                      name="dw_gate")
    g["w_up"] = _mm([(u2, 0, dup, 0, T_ROWS)], D_MODEL, D_FF, tm=512, tn=1408, mode="tn", out_dtype=F32,
                    name="dw_up")
    dycat = _mm([(dh1b, 0, p["w_out"], 0, D_MODEL)], T_ROWS, 2 * D_MODEL, tm=544, tn=512, mode="nt", out_dtype=F32,
                name="out_proj_bwd")
    g["w_out"] = jnp.concatenate(
        [_mm([(y, 0, dh1b, 0, T_ROWS)], D_MODEL, D_MODEL, tm=512, tn=512, mode="tn", out_dtype=F32, name=nm)
         for y, nm in ((y_ssd, "dw_out_ssd"), (y_lru, "dw_out_lru"))], axis=0)
    dgate, dxr, dwa_p, dwx_p, dlru_prm = _lru_bwd(dycat, 1, proj["g"], xr, hs, wa_p.astype(_MXU), wx_p.astype(_MXU),
                                                  jnp.swapaxes(wa_p, 1, 2).astype(_MXU),
                                                  jnp.swapaxes(wx_p, 1, 2).astype(_MXU), lru_prm)
    g["lru_wa"], g["lru_wx"] = _unpair_blocks(dwa_p), _unpair_blocks(dwx_p)
    g["lru_ba"], g["lru_bx"], g["lru_lambda"], g["lru_norm_w"] = (dlru_prm[k:k + 1] for k in range(4))
    dx_lru, g["lru_conv_w"], g["lru_conv_b"] = _conv_bwd(dxr, proj["x"], p["lru_conv_w"], p["lru_conv_b"], silu=False,
                                                         name="lru_conv_bwd")
    dz, dxs_act, dbc_act, ddt, dssd_prm, g["ssd_norm_w"] = _ssd_bwd(dycat, 0, proj["z"], y_pre, xs_act, bc_act,
                                                                    proj["dt"], prev, ssd_prm, p["ssd_norm_w"])
    g["ssd_dt_bias"], g["ssd_a_log"], g["ssd_d"] = (dssd_prm[k:k + 1, :SSD_HEADS] for k in range(3))
    dxs, dcw_xs, dcb_xs = _conv_bwd(dxs_act, proj["xs"], p["ssd_conv_w"][:, :SSD_WIDTH],
                                    p["ssd_conv_b"][:, :SSD_WIDTH], silu=True, name="ssd_conv_xs_bwd")
    dbc, dcw_bc, dcb_bc = _conv_bwd(dbc_act, proj["bc"], p["ssd_conv_w"][:, SSD_WIDTH:],
                                    p["ssd_conv_b"][:, SSD_WIDTH:], silu=True, name="ssd_conv_bc_bwd")
    g["ssd_conv_w"] = jnp.concatenate([dcw_xs, dcw_bc], axis=1)
    g["ssd_conv_b"] = jnp.concatenate([dcb_xs, dcb_bc], axis=1)
    dproj = {"z": dz, "xs": dxs, "bc": dbc, "dt": ddt, "g": dgate, "x": dx_lru}
    dh0, _, g["norm1_w"] = _mm_nt_norm_bwd([(dproj[s], p["w_in_" + s], SEC_WIDTH[s]) for s in SEC_NAMES], h0,
                                           p["norm1_w"], dh1, name="in_proj_bwd")
    for s in SEC_NAMES:
        wdt = SEC_WIDTH[s]
        g["w_in_" + s] = _mm([(u1, 0, dproj[s], 0, T_ROWS)], D_MODEL, wdt, tm=512, tn=min(wdt, 512), mode="tn",
                             out_dtype=F32, name="dw_in_" + s)
    g["meta_tokens"] = dh0[PAD_ROWS:X_ROW0]
    return loss, dh0[X_ROW0:], g


MESH = pl.DeviceIdType.MESH
ANY = pl.BlockSpec(memory_space=pl.ANY)


def _my_place():
    return lax.axis_index("x"), lax.axis_index("y"), lax.axis_index("c")


def _other_chips(x, y):
    return [(1 - x, y), (x, 1 - y), (1 - x, 1 - y)]


def _gather_shards(shards):
    n = len(shards)

    def body(*refs):
        ins, outs = refs[:n], refs[n:2 * n]
        send_sems, recv_sems, local_sems = refs[2 * n:]
        x, y, c = _my_place()
        me = 2 * x + y
        peers = _other_chips(x, y)
        local = [pltpu.make_async_copy(ins[k], outs[k].at[me], local_sems.at[k]) for k in range(n)]
        for cp in local:
            cp.start()
        for k in range(n):
            for j, (px, py) in enumerate(peers):
                pltpu.make_async_remote_copy(
                    src_ref=ins[k], dst_ref=outs[k].at[me], send_sem=send_sems.at[3 * k + j],
                    recv_sem=recv_sems.at[3 * k + j], device_id=(px, py, c), device_id_type=MESH).start()
        for k in range(n):
            for j, (px, py) in enumerate(peers):
                pltpu.make_async_remote_copy(
                    src_ref=ins[k], dst_ref=outs[k].at[2 * px + py], send_sem=send_sems.at[3 * k + j],
                    recv_sem=recv_sems.at[3 * k + j], device_id=(px, py, c), device_id_type=MESH).wait()
        for cp in local:
            cp.wait()

    return pl.pallas_call(
        body, name="gather_weights", in_specs=[ANY] * n, out_specs=[ANY] * n,
        out_shape=[jax.ShapeDtypeStruct((N_SHARDS,) + s.shape, s.dtype) for s in shards],
        scratch_shapes=[pltpu.SemaphoreType.DMA((3 * n,)), pltpu.SemaphoreType.DMA((3 * n,)),
                        pltpu.SemaphoreType.DMA((n,))],
    )(*shards)


def _scatter_grads(grads4):
    n = len(grads4)

    def body(*refs):
        ins, outs = refs[:n], refs[n:2 * n]
        send_sems, recv_sems = refs[2 * n:]
        x, y, c = _my_place()
        peers = _other_chips(x, y)
        for k in range(n):
            for j, (px, py) in enumerate(peers):
                pltpu.make_async_remote_copy(
                    src_ref=ins[k].at[2 * px + py], dst_ref=outs[k].at[j], send_sem=send_sems.at[3 * k + j],
                    recv_sem=recv_sems.at[3 * k + j], device_id=(px, py, c), device_id_type=MESH).start()
        for k in range(n):
            for j, (px, py) in enumerate(peers):
                pltpu.make_async_remote_copy(
                    src_ref=ins[k].at[2 * px + py], dst_ref=outs[k].at[j], send_sem=send_sems.at[3 * k + j],
                    recv_sem=recv_sems.at[3 * k + j], device_id=(px, py, c), device_id_type=MESH).wait()

    return pl.pallas_call(
        body, name="scatter_grads", in_specs=[ANY] * n, out_specs=[ANY] * n,
        out_shape=[jax.ShapeDtypeStruct((3,) + g.shape[1:], g.dtype) for g in grads4],
        scratch_shapes=[pltpu.SemaphoreType.DMA((3 * n,)), pltpu.SemaphoreType.DMA((3 * n,))],
    )(*grads4)


def _swap_with_sibling(parts):
    n = len(parts)

    def body(*refs):
        ins, outs = refs[:n], refs[n:2 * n]
        send_sems, recv_sems = refs[2 * n:]
        x, y, c = _my_place()
        copies = [pltpu.make_async_remote_copy(
            src_ref=ins[k], dst_ref=outs[k], send_sem=send_sems.at[k], recv_sem=recv_sems.at[k],
            device_id=(x, y, 1 - c), device_id_type=MESH) for k in range(n)]
        for cp in copies:
            cp.start()
        for cp in copies:
            cp.wait()

    return pl.pallas_call(
        body, name="swap_with_sibling", in_specs=[ANY] * n, out_specs=[ANY] * n,
        out_shape=[jax.ShapeDtypeStruct(a.shape, a.dtype) for a in parts],
        scratch_shapes=[pltpu.SemaphoreType.DMA((n,)), pltpu.SemaphoreType.DMA((n,))],
    )(*parts)


def _all_reduce_small(pack):
    rows = pack.shape[0]

    def body(x_ref, o_ref, buf, send_sems, recv_sems):
        x, y, c = _my_place()
        me = 4 * x + 2 * y + c
        buf[me] = x_ref[...]
        masks = [(m >> 2 & 1, m >> 1 & 1, m & 1) for m in range(1, N_DEV)]

        def copy(i):
            mx, my, mc = masks[i]
            px, py, pc = x ^ mx, y ^ my, c ^ mc
            return pltpu.make_async_remote_copy(
                src_ref=x_ref, dst_ref=buf.at[me], send_sem=send_sems.at[i], recv_sem=recv_sems.at[i],
                device_id=(px, py, pc), device_id_type=MESH), 4 * px + 2 * py + pc

        for i in range(N_DEV - 1):
            copy(i)[0].start()
        for i in range(N_DEV - 1):
            cp, peer = copy(i)
            cp.wait_send()
            pltpu.make_async_remote_copy(
                src_ref=x_ref, dst_ref=buf.at[peer], send_sem=send_sems.at[i], recv_sem=recv_sems.at[i],
                device_id=(x, y, c), device_id_type=MESH).wait_recv()
        acc = buf[0]
        for d in range(1, N_DEV):
            acc = acc + buf[d]
        o_ref[...] = acc

    vmem = pl.BlockSpec(memory_space=pltpu.VMEM)
    return pl.pallas_call(
        body, name="all_reduce_small", in_specs=[vmem], out_specs=vmem,
        out_shape=jax.ShapeDtypeStruct(pack.shape, F32),
        scratch_shapes=[pltpu.VMEM((N_DEV, rows, pack.shape[1]), F32), pltpu.SemaphoreType.DMA((N_DEV - 1,)),
                        pltpu.SemaphoreType.DMA((N_DEV - 1,))],
        compiler_params=pltpu.CompilerParams(vmem_limit_bytes=VMEM_LIMIT_BYTES),
    )(pack)


def _row_tile(rows, limit=256):
    for t in range(limit, 7, -8):
        if rows % t == 0:
            return t
    return rows


def _partial_sum(own, land, *, name):
    r, c = own.shape
    tr = _row_tile(r)

    def body(own_ref, land_ref, o_ref):
        acc = own_ref[...]
        for j in range(3):
            acc = acc + land_ref[j].astype(F32)
        o_ref[...] = acc

    return pl.pallas_call(
        body, name=name, grid=(r // tr,),
        in_specs=[pl.BlockSpec((tr, c), lambda i: (i, 0)), pl.BlockSpec((3, tr, c), lambda i: (0, i, 0))],
        out_specs=pl.BlockSpec((tr, c), lambda i: (i, 0)),
        out_shape=jax.ShapeDtypeStruct((r, c), F32),
        compiler_params=_cparams("parallel"),
    )(own, land)


def _adamw_math(w, g, m, v):
    m = ADAM_B1 * m + (1.0 - ADAM_B1) * g
    v = ADAM_B2 * v + (1.0 - ADAM_B2) * (g * g)
    m_hat = m / (1.0 - ADAM_B1 ** ADAM_STEP)
    v_hat = v / (1.0 - ADAM_B2 ** ADAM_STEP)
    delta = -ADAM_LR * (m_hat / (jnp.sqrt(v_hat) + ADAM_EPS) + ADAM_WD * w)
    return delta, m, v


def _adamw(w, grad_parts, m, v, *, name):
    r, c = w.shape
    tr = _row_tile(r)
    n = len(grad_parts)

    def body(*refs):
        w_ref, m_ref, v_ref = refs[:3]
        g_refs = refs[3:3 + n]
        g_out, d_out, m_out, v_out = refs[3 + n:]
        g = g_refs[0][...]
        for k in range(1, n):
            g = g + g_refs[k][...]
        delta, m_new, v_new = _adamw_math(w_ref[...], g, m_ref[...], v_ref[...])
        g_out[...] = g
        d_out[...] = delta
        m_out[...] = m_new
        v_out[...] = v_new

    tile = pl.BlockSpec((tr, c), lambda i: (i, 0))
    return pl.pallas_call(
        body, name=name, grid=(r // tr,), in_specs=[tile] * (3 + n), out_specs=[tile] * 4,
        out_shape=[jax.ShapeDtypeStruct((r, c), F32)] * 4,
        compiler_params=_cparams("parallel"),
    )(w, m, v, *grad_parts)


WEIGHT_NAMES = ("meta_tokens", "norm1_w", "w_in", "ssd_conv_w", "ssd_conv_b", "ssd_dt_bias", "ssd_a_log", "ssd_d",
                "ssd_norm_w", "lru_conv_w", "lru_conv_b", "lru_wa", "lru_ba", "lru_wx", "lru_bx", "lru_lambda",
                "lru_norm_w", "w_out", "norm2_w", "w_gate", "w_up", "w_down", "final_norm_w")
BIG = ("w_in", "w_out", "w_gate", "w_up", "w_down")
SMALL_SHARDED = {"meta_tokens": (N_META, D_MODEL), "ssd_conv_w": (CONV_K, 1536), "lru_conv_w": (CONV_K, LRU_WIDTH)}
SMALL = tuple(n for n in WEIGHT_NAMES if n not in BIG)
PACK_COLS = 1024


def _pack(arrays):
    flat = jnp.concatenate([a.reshape(-1) for a in arrays])
    rows = -(-flat.shape[0] // (8 * PACK_COLS)) * 8
    return jnp.pad(flat, (0, rows * PACK_COLS - flat.shape[0])).reshape(rows, PACK_COLS)


def _unpack(pack, shapes):
    flat = pack.reshape(-1)
    out, off = [], 0
    for s in shapes:
        size = math.prod(s)
        out.append(flat[off:off + size].reshape(s))
        off += size
    return out


def _unshard_cols(g4):
    return jnp.swapaxes(g4, 0, 1).reshape(g4.shape[1], -1)


def _shard_cols(full):
    r = full.shape[0]
    return jnp.swapaxes(full.reshape(r, N_SHARDS, -1), 0, 1)


def kernel(x, meta_tokens, norm1_w, w_in, ssd_conv_w, ssd_conv_b, ssd_dt_bias, ssd_a_log, ssd_d, ssd_norm_w, lru_conv_w, lru_conv_b, lru_wa, lru_ba, lru_wx, lru_bx, lru_lambda, lru_norm_w, w_out, norm2_w, w_gate, w_up, w_down, final_norm_w, loss_target, m_meta_tokens, m_norm1_w, m_w_in, m_ssd_conv_w, m_ssd_conv_b, m_ssd_dt_bias, m_ssd_a_log, m_ssd_d, m_ssd_norm_w, m_lru_conv_w, m_lru_conv_b, m_lru_wa, m_lru_ba, m_lru_wx, m_lru_bx, m_lru_lambda, m_lru_norm_w, m_w_out, m_norm2_w, m_w_gate, m_w_up, m_w_down, m_final_norm_w, v_meta_tokens, v_norm1_w, v_w_in, v_ssd_conv_w, v_ssd_conv_b, v_ssd_dt_bias, v_ssd_a_log, v_ssd_d, v_ssd_norm_w, v_lru_conv_w, v_lru_conv_b, v_lru_wa, v_lru_ba, v_lru_wx, v_lru_bx, v_lru_lambda, v_lru_norm_w, v_w_out, v_norm2_w, v_w_gate, v_w_up, v_w_down, v_final_norm_w):
    w = dict(zip(WEIGHT_NAMES, (meta_tokens, norm1_w, w_in, ssd_conv_w, ssd_conv_b, ssd_dt_bias, ssd_a_log, ssd_d, ssd_norm_w, lru_conv_w, lru_conv_b, lru_wa, lru_ba, lru_wx, lru_bx, lru_lambda, lru_norm_w, w_out, norm2_w, w_gate, w_up, w_down, final_norm_w)))
    m = dict(zip(WEIGHT_NAMES, (m_meta_tokens, m_norm1_w, m_w_in, m_ssd_conv_w, m_ssd_conv_b, m_ssd_dt_bias, m_ssd_a_log, m_ssd_d, m_ssd_norm_w, m_lru_conv_w, m_lru_conv_b, m_lru_wa, m_lru_ba, m_lru_wx, m_lru_bx, m_lru_lambda, m_lru_norm_w, m_w_out, m_norm2_w, m_w_gate, m_w_up, m_w_down, m_final_norm_w)))
    v = dict(zip(WEIGHT_NAMES, (v_meta_tokens, v_norm1_w, v_w_in, v_ssd_conv_w, v_ssd_conv_b, v_ssd_dt_bias, v_ssd_a_log, v_ssd_d, v_ssd_norm_w, v_lru_conv_w, v_lru_conv_b, v_lru_wa, v_lru_ba, v_lru_wx, v_lru_bx, v_lru_lambda, v_lru_norm_w, v_w_out, v_norm2_w, v_w_gate, v_w_up, v_w_down, v_final_norm_w)))
    me = 2 * lax.axis_index("x") + lax.axis_index("y")

    big2d = {n: w[n][0] for n in BIG}
    small_local = jnp.concatenate([w["meta_tokens"].reshape(-1), w["ssd_conv_w"].reshape(-1),
                                   w["lru_conv_w"].reshape(-1)])[None]
    gathered = _gather_shards([big2d[n].astype(_MXU) for n in BIG] + [small_local])
    full = dict(zip(BIG, gathered[:-1]))
    sm = gathered[-1][:, 0]
    meta_full = _unshard_cols(sm[:, :4096].reshape(N_SHARDS, N_META, 256))
    ssd_conv_w_full = _unshard_cols(sm[:, 4096:5632].reshape(N_SHARDS, CONV_K, 384))
    lru_conv_w_full = _unshard_cols(sm[:, 5632:].reshape(N_SHARDS, CONV_K, 256))
    w_in_full = _unshard_cols(full["w_in"])
    p = {
        "w_in_z": w_in_full[:, :1024], "w_in_xs": w_in_full[:, 1024:2048], "w_in_bc": w_in_full[:, 2048:2560],
        "w_in_dt": jnp.pad(w_in_full[:, 2560:2576], ((0, 0), (0, 112))), "w_in_g": w_in_full[:, 2576:3600],
        "w_in_x": w_in_full[:, 3600:],
        "w_out": full["w_out"].reshape(2 * D_MODEL, D_MODEL),
        "w_gate": _unshard_cols(full["w_gate"]), "w_up": _unshard_cols(full["w_up"]),
        "w_down": full["w_down"].reshape(D_FF, D_MODEL),
        "ssd_conv_w": ssd_conv_w_full, "lru_conv_w": lru_conv_w_full,
        "lru_wa": w["lru_wa"][0], "lru_wx": w["lru_wx"][0], "final_norm_w": w["final_norm_w"][None],
    }
    for n in ("norm1_w", "ssd_conv_b", "ssd_dt_bias", "ssd_a_log", "ssd_d", "ssd_norm_w", "lru_conv_b", "lru_ba",
              "lru_bx", "lru_lambda", "lru_norm_w", "norm2_w"):
        p[n] = w[n]

    loss, grad_x, g = _local_step(x[0], loss_target[0], meta_full, p)

    g_w_in = jnp.concatenate([g["w_in_z"], g["w_in_xs"], g["w_in_bc"], g["w_in_dt"][:, :SSD_HEADS], g["w_in_g"],
                              g["w_in_x"]], axis=1)
    g4 = {"w_in": _shard_cols(g_w_in), "w_out": g["w_out"].reshape(N_SHARDS, 512, D_MODEL),
          "w_gate": _shard_cols(g["w_gate"]), "w_up": _shard_cols(g["w_up"]),
          "w_down": g["w_down"].reshape(N_SHARDS, 704, D_MODEL)}
    land = dict(zip(BIG, _scatter_grads([g4[n].astype(_MXU) for n in BIG])))
    part = {n: _partial_sum(lax.dynamic_index_in_dim(g4[n], me, 0, keepdims=False), land[n], name="partial_" + n)
            for n in BIG}
    sib = dict(zip(BIG, _swap_with_sibling([part[n] for n in BIG])))

    small_full_shape = {n: (SMALL_SHARDED[n] if n in SMALL_SHARDED else w[n].shape) for n in SMALL}
    red = _all_reduce_small(_pack([g[n] for n in SMALL] + [loss[0, :1]]))
    red_list = _unpack(red, [small_full_shape[n] for n in SMALL] + [(1,)])
    loss_total = red_list[-1][0]
    g_small = {}
    for n, arr in zip(SMALL, red_list[:-1]):
        if n in SMALL_SHARDED:
            cols = SMALL_SHARDED[n][1] // N_SHARDS
            arr = lax.dynamic_slice_in_dim(arr, me * cols, cols, axis=1)
        g_small[n] = arr.reshape(w[n].shape)

    grad, delta, new_m, new_v = {}, {}, {}, {}
    for n in BIG:
        outs = _adamw(big2d[n], [part[n], sib[n]], m[n][0], v[n][0], name="adamw_" + n)
        grad[n], delta[n], new_m[n], new_v[n] = (o[None] for o in outs)
    shapes = [w[n].shape for n in SMALL]
    packs = [_pack([d[n] for n in SMALL]) for d in (w, m, v, g_small)]
    _, d_pack, m_pack, v_pack = _adamw(packs[0], [packs[3]], packs[1], packs[2], name="adamw_small")
    for n, dn, mn, vn in zip(SMALL, _unpack(d_pack, shapes), _unpack(m_pack, shapes), _unpack(v_pack, shapes)):
        grad[n], delta[n], new_m[n], new_v[n] = g_small[n], dn, mn, vn

    return (loss_total, grad_x[None], *[grad[n] for n in WEIGHT_NAMES], *[delta[n] for n in WEIGHT_NAMES],
            *[new_m[n] for n in WEIGHT_NAMES], *[new_v[n] for n in WEIGHT_NAMES])
```

```python
import functools
import math

import jax
import jax.numpy as jnp
from jax import lax
from jax.experimental import pallas as pl
from jax.experimental.pallas import tpu as pltpu

F32 = jnp.float32
_MXU = jnp.bfloat16

D_MODEL = 1024
SEQ = 2048
N_META = 16
CHUNK = 128
T_ROWS = 2176
N_CHUNKS = T_ROWS // CHUNK
PAD_ROWS = T_ROWS - SEQ - N_META
X_ROW0 = PAD_ROWS + N_META
SSD_HEADS = 16
SSD_HEAD_DIM = 64
SSD_STATE = 128
SSD_GROUPS = 2
SSD_HPG = SSD_HEADS // SSD_GROUPS
SSD_WIDTH = 1024
LRU_WIDTH = 1024
LRU_C = 8.0
D_FF = 2816
EPS = 1e-6
IN_COLS = 4624
N_SHARDS = 4
N_DEV = 8

ADAM_LR = 0.001
ADAM_B1 = 0.9
ADAM_B2 = 0.999
ADAM_EPS = 1e-08
ADAM_WD = 0.01
ADAM_STEP = 10

VMEM_LIMIT_BYTES = 56 * 1024 * 1024

NN = (((1,), (0,)), ((), ()))
NT = (((1,), (1,)), ((), ()))
TN = (((0,), (0,)), ((), ()))


def _cparams(*sem):
    return pltpu.CompilerParams(dimension_semantics=sem, vmem_limit_bytes=VMEM_LIMIT_BYTES)


def _dot(a, b, dims=NN):
    return lax.dot_general(a.astype(_MXU), b.astype(_MXU), dims, preferred_element_type=F32)


def _dot_exact(a, b, dims=NN):
    return lax.dot_general(a, b, dims, preferred_element_type=F32, precision=lax.Precision.HIGHEST)


def _sigmoid(x):
    return 1.0 / (1.0 + jnp.exp(-x))


def _softplus(x):
    return jnp.maximum(x, 0.0) + jnp.log(1.0 + jnp.exp(-jnp.abs(x)))


def _silu(x):
    return x * _sigmoid(x)


def _silu_grad(x):
    s = _sigmoid(x)
    return s * (1.0 + x * (1.0 - s))


_GELU_C = math.sqrt(2.0 / math.pi)


def _gelu_and_grad(x):
    inner = _GELU_C * (x + 0.044715 * x * x * x)
    t = jnp.tanh(inner)
    g = 0.5 * x * (1.0 + t)
    dg = 0.5 * (1.0 + t) + 0.5 * x * (1.0 - t * t) * _GELU_C * (1.0 + 3.0 * 0.044715 * x * x)
    return g, dg


def _rms_fwd(x, w):
    rstd = lax.rsqrt(jnp.mean(x * x, axis=-1, keepdims=True) + EPS)
    return x * rstd * w


def _rms_bwd(x, w, dy):
    rstd = lax.rsqrt(jnp.mean(x * x, axis=-1, keepdims=True) + EPS)
    xhat = x * rstd
    dxhat = dy * w
    dx = rstd * (dxhat - xhat * jnp.mean(dxhat * xhat, axis=-1, keepdims=True))
    return dx, dy * xhat


def _mm(terms, m, n, *, tm, tn, mode, out_dtype, name, residual=None, n_outer=False):
    gm, gn = m // tm, n // tn
    assert gm * tm == m and gn * tn == n
    if n_outer:
        grid = (gn, gm)
        mi = lambda g0, g1: g1
        ni = lambda g0, g1: g0
    else:
        grid = (gm, gn)
        mi = lambda g0, g1: g0
        ni = lambda g0, g1: g1
    in_specs, args = [], []
    for (a, ka, b, kb, k) in terms:
        if mode == "tn":
            in_specs.append(pl.BlockSpec((k, tm), lambda g0, g1, ka=ka: (ka, mi(g0, g1))))
        else:
            in_specs.append(pl.BlockSpec((tm, k), lambda g0, g1, ka=ka: (mi(g0, g1), ka)))
        if mode == "nt":
            in_specs.append(pl.BlockSpec((tn, k), lambda g0, g1, kb=kb: (ni(g0, g1), kb)))
        else:
            in_specs.append(pl.BlockSpec((k, tn), lambda g0, g1, kb=kb: (kb, ni(g0, g1))))
        args += [a, b]
    if residual is not None:
        in_specs.append(pl.BlockSpec((tm, tn), lambda g0, g1: (mi(g0, g1), ni(g0, g1))))
        args.append(residual)
    dims = {"nn": NN, "nt": NT, "tn": TN}[mode]
    n_terms = len(terms)
    has_res = residual is not None

    def body(*refs):
        o_ref = refs[-1]
        acc = None
        for t in range(n_terms):
            d = lax.dot_general(refs[2 * t][...], refs[2 * t + 1][...], dims, preferred_element_type=F32)
            acc = d if acc is None else acc + d
        if has_res:
            acc = acc + refs[2 * n_terms][...]
        o_ref[...] = acc.astype(out_dtype)

    return pl.pallas_call(
        body, name=name, grid=grid, in_specs=in_specs,
        out_specs=pl.BlockSpec((tm, tn), lambda g0, g1: (mi(g0, g1), ni(g0, g1))),
        out_shape=jax.ShapeDtypeStruct((m, n), out_dtype),
        compiler_params=_cparams("parallel", "parallel"),
    )(*args)


def _embed(x, meta):
    def body(x_ref, meta_ref, o_ref):
        i = pl.program_id(0)

        @pl.when(i == 0)
        def _():
            o_ref[0:PAD_ROWS, :] = jnp.zeros((PAD_ROWS, D_MODEL), F32)
            o_ref[PAD_ROWS:CHUNK, :] = meta_ref[...]

        @pl.when(i > 0)
        def _():
            o_ref[...] = x_ref[...]

    return pl.pallas_call(
        body, name="embed", grid=(N_CHUNKS,),
        in_specs=[pl.BlockSpec((CHUNK, D_MODEL), lambda i: (jnp.maximum(i - 1, 0), 0)),
                  pl.BlockSpec((N_META, D_MODEL), lambda i: (0, 0))],
        out_specs=pl.BlockSpec((CHUNK, D_MODEL), lambda i: (i, 0)),
        out_shape=jax.ShapeDtypeStruct((T_ROWS, D_MODEL), F32),
        compiler_params=_cparams("parallel"),
    )(x, meta)


def _rmsnorm(h, w, *, name, tm=544):
    def body(h_ref, w_ref, o_ref):
        o_ref[...] = _rms_fwd(h_ref[...], w_ref[...]).astype(_MXU)

    return pl.pallas_call(
        body, name=name, grid=(T_ROWS // tm,),
        in_specs=[pl.BlockSpec((tm, D_MODEL), lambda i: (i, 0)), pl.BlockSpec((1, D_MODEL), lambda i: (0, 0))],
        out_specs=pl.BlockSpec((tm, D_MODEL), lambda i: (i, 0)),
        out_shape=jax.ShapeDtypeStruct((T_ROWS, D_MODEL), _MXU),
        compiler_params=_cparams("parallel"),
    )(h, w)


def _loss_head(h2, target, fw):
    def body(h_ref, t_ref, w_ref, loss_ref, dh_ref, dhb_ref, dw_ref, acc_ref):
        i = pl.program_id(0)

        @pl.when(i == 0)
        def _():
            acc_ref[...] = jnp.zeros_like(acc_ref)
            dw_ref[...] = jnp.zeros_like(dw_ref)

        h = h_ref[...]
        w = w_ref[...]
        y = _rms_fwd(h, w)
        live = (i > 0).astype(F32)
        err = (y - t_ref[...]) * live
        acc_ref[...] += jnp.sum(err * err, axis=0, keepdims=True)
        dy = err * (1.0 / D_MODEL)
        dx, dwr = _rms_bwd(h, w, dy)
        dh_ref[...] = dx
        dhb_ref[...] = dx.astype(_MXU)
        dw_ref[...] += jnp.sum(dwr, axis=0, keepdims=True)

        @pl.when(i == N_CHUNKS - 1)
        def _():
            tot = jnp.sum(acc_ref[...], axis=1, keepdims=True) * (0.5 / D_MODEL)
            loss_ref[...] = jnp.broadcast_to(tot, (1, 128))

    return pl.pallas_call(
        body, name="loss_head", grid=(N_CHUNKS,),
        in_specs=[pl.BlockSpec((CHUNK, D_MODEL), lambda i: (i, 0)),
                  pl.BlockSpec((CHUNK, D_MODEL), lambda i: (jnp.maximum(i - 1, 0), 0)),
                  pl.BlockSpec((1, D_MODEL), lambda i: (0, 0))],
        out_specs=[pl.BlockSpec((1, 128), lambda i: (0, 0)),
                   pl.BlockSpec((CHUNK, D_MODEL), lambda i: (i, 0)),
                   pl.BlockSpec((CHUNK, D_MODEL), lambda i: (i, 0)),
                   pl.BlockSpec((1, D_MODEL), lambda i: (0, 0))],
        out_shape=[jax.ShapeDtypeStruct((1, 128), F32),
                   jax.ShapeDtypeStruct((T_ROWS, D_MODEL), F32),
                   jax.ShapeDtypeStruct((T_ROWS, D_MODEL), _MXU),
                   jax.ShapeDtypeStruct((1, D_MODEL), F32)],
        scratch_shapes=[pltpu.VMEM((1, D_MODEL), F32)],
        compiler_params=_cparams("arbitrary"),
    )(h2, target, fw)


def _mm_norm_bwd(terms, h, w, dres, *, name, tm=272):
    n_terms = len(terms)
    in_specs, args = [], []
    for (a, b, k) in terms:
        in_specs += [pl.BlockSpec((tm, k), lambda i: (i, 0)), pl.BlockSpec((k, D_MODEL), lambda i: (0, 0))]
        args += [a, b]
    in_specs += [pl.BlockSpec((tm, D_MODEL), lambda i: (i, 0)), pl.BlockSpec((1, D_MODEL), lambda i: (0, 0)),
                 pl.BlockSpec((tm, D_MODEL), lambda i: (i, 0))]
    args += [h, w, dres]

    def body(*refs):
        h_ref, w_ref, dres_ref, dh_ref, dhb_ref, dw_ref = refs[2 * n_terms:]

        @pl.when(pl.program_id(0) == 0)
        def _():
            dw_ref[...] = jnp.zeros_like(dw_ref)

        du = None
        for t in range(n_terms):
            d = lax.dot_general(refs[2 * t][...], refs[2 * t + 1][...], NN, preferred_element_type=F32)
            du = d if du is None else du + d
        dx, dwr = _rms_bwd(h_ref[...], w_ref[...], du)
        dh = dres_ref[...] + dx
        dh_ref[...] = dh
        dhb_ref[...] = dh.astype(_MXU)
        dw_ref[...] += jnp.sum(dwr, axis=0, keepdims=True)

    return pl.pallas_call(
        body, name=name, grid=(T_ROWS // tm,), in_specs=in_specs,
        out_specs=[pl.BlockSpec((tm, D_MODEL), lambda i: (i, 0)), pl.BlockSpec((tm, D_MODEL), lambda i: (i, 0)),
                   pl.BlockSpec((1, D_MODEL), lambda i: (0, 0))],
        out_shape=[jax.ShapeDtypeStruct((T_ROWS, D_MODEL), F32), jax.ShapeDtypeStruct((T_ROWS, D_MODEL), _MXU),
                   jax.ShapeDtypeStruct((1, D_MODEL), F32)],
        compiler_params=_cparams("arbitrary"),
    )(*args)


FFN_TM = 272
FFN_TN = 1408


def _ffn_up(u2, wg_t, wu_t):
    def body(u_ref, wg_ref, wu_ref, gp_ref, up_ref, act_ref):
        u = u_ref[...]
        gp = lax.dot_general(u, wg_ref[...], NT, preferred_element_type=F32)
        up = lax.dot_general(u, wu_ref[...], NT, preferred_element_type=F32)
        gp_ref[...] = gp
        up_ref[...] = up
        act_ref[...] = (_silu(gp) * up).astype(_MXU)

    tile = pl.BlockSpec((FFN_TM, FFN_TN), lambda j, i: (i, j))
    return pl.pallas_call(
        body, name="ffn_up", grid=(D_FF // FFN_TN, T_ROWS // FFN_TM),
        in_specs=[pl.BlockSpec((FFN_TM, D_MODEL), lambda j, i: (i, 0)),
                  pl.BlockSpec((FFN_TN, D_MODEL), lambda j, i: (j, 0)),
                  pl.BlockSpec((FFN_TN, D_MODEL), lambda j, i: (j, 0))],
        out_specs=[tile, tile, tile],
        out_shape=[jax.ShapeDtypeStruct((T_ROWS, D_FF), F32), jax.ShapeDtypeStruct((T_ROWS, D_FF), F32),
                   jax.ShapeDtypeStruct((T_ROWS, D_FF), _MXU)],
        compiler_params=_cparams("parallel", "parallel"),
    )(u2, wg_t, wu_t)


def _ffn_bwd_act(dh2b, wd, gp, up):
    def body(dh_ref, wd_ref, gp_ref, up_ref, dgp_ref, dup_ref):
        dact = lax.dot_general(dh_ref[...], wd_ref[...], NT, preferred_element_type=F32)
        gp = gp_ref[...]
        dgp_ref[...] = (dact * up_ref[...] * _silu_grad(gp)).astype(_MXU)
        dup_ref[...] = (dact * _silu(gp)).astype(_MXU)

    tile = pl.BlockSpec((FFN_TM, FFN_TN), lambda j, i: (i, j))
    return pl.pallas_call(
        body, name="ffn_bwd_act", grid=(D_FF // FFN_TN, T_ROWS // FFN_TM),
        in_specs=[pl.BlockSpec((FFN_TM, D_MODEL), lambda j, i: (i, 0)),
                  pl.BlockSpec((FFN_TN, D_MODEL), lambda j, i: (j, 0)), tile, tile],
        out_specs=[tile, tile],
        out_shape=[jax.ShapeDtypeStruct((T_ROWS, D_FF), _MXU), jax.ShapeDtypeStruct((T_ROWS, D_FF), _MXU)],
        compiler_params=_cparams("parallel", "parallel"),
    )(dh2b, wd, gp, up)


CONV_TC = 512
CONV_K = 4


def _conv_pre(x_ref, wv, bv, c):
    tc = wv.shape[1]
    r0 = c * CHUNK
    cur = x_ref[r0:r0 + CHUNK, :]
    prev8 = jnp.zeros((8, tc), F32) if c == 0 else x_ref[r0 - 8:r0, :]
    cat = jnp.concatenate([prev8, cur], axis=0)
    shifted = [cur] + [pltpu.roll(cat, s, 0)[8:8 + CHUNK] for s in range(1, CONV_K)]
    pre = bv
    for s in range(CONV_K):
        pre = pre + shifted[s] * wv[CONV_K - 1 - s:CONV_K - s]
    return pre, shifted


def _row_mask(c):
    if c > 0:
        return None
    return (lax.broadcasted_iota(jnp.int32, (CHUNK, 1), 0) >= PAD_ROWS).astype(F32)


def _conv_fwd(x, w, b, *, silu, name):
    cols = x.shape[1]
    tc = min(CONV_TC, cols)

    def body(x_ref, w_ref, b_ref, o_ref):
        wv, bv = w_ref[...], b_ref[...]
        for c in range(N_CHUNKS):
            pre, _ = _conv_pre(x_ref, wv, bv, c)
            y = _silu(pre) if silu else pre
            mask = _row_mask(c)
            if mask is not None:
                y = y * mask
            o_ref[c * CHUNK:(c + 1) * CHUNK, :] = y

    return pl.pallas_call(
        body, name=name, grid=(cols // tc,),
        in_specs=[pl.BlockSpec((T_ROWS, tc), lambda j: (0, j)), pl.BlockSpec((CONV_K, tc), lambda j: (0, j)),
                  pl.BlockSpec((1, tc), lambda j: (0, j))],
        out_specs=pl.BlockSpec((T_ROWS, tc), lambda j: (0, j)),
        out_shape=jax.ShapeDtypeStruct((T_ROWS, cols), F32),
        compiler_params=_cparams("parallel"),
    )(x, w, b)


def _conv_bwd(dy, x, w, b, *, silu, name):
    cols = x.shape[1]
    tc = min(CONV_TC, cols)

    def body(dy_ref, x_ref, w_ref, b_ref, dx_ref, dw_ref, db_ref):
        wv, bv = w_ref[...], b_ref[...]
        next8 = jnp.zeros((8, tc), F32)
        dws = [jnp.zeros((1, tc), F32) for _ in range(CONV_K)]
        db = jnp.zeros((1, tc), F32)
        for c in reversed(range(N_CHUNKS)):
            pre, shifted = _conv_pre(x_ref, wv, bv, c)
            dpre = dy_ref[c * CHUNK:(c + 1) * CHUNK, :]
            if silu:
                dpre = dpre * _silu_grad(pre)
            mask = _row_mask(c)
            if mask is not None:
                dpre = dpre * mask
            cat = jnp.concatenate([dpre, next8], axis=0)
            dx = dpre * wv[CONV_K - 1:CONV_K]
            for s in range(1, CONV_K):
                dx = dx + pltpu.roll(cat, CHUNK + 8 - s, 0)[0:CHUNK] * wv[CONV_K - 1 - s:CONV_K - s]
            dx_ref[c * CHUNK:(c + 1) * CHUNK, :] = dx.astype(_MXU)
            for s in range(CONV_K):
                k = CONV_K - 1 - s
                dws[k] = dws[k] + jnp.sum(dpre * shifted[s], axis=0, keepdims=True)
            db = db + jnp.sum(dpre, axis=0, keepdims=True)
            next8 = dpre[0:8]
        dw_ref[...] = jnp.concatenate(dws, axis=0)
        db_ref[...] = db

    return pl.pallas_call(
        body, name=name, grid=(cols // tc,),
        in_specs=[pl.BlockSpec((T_ROWS, tc), lambda j: (0, j)), pl.BlockSpec((T_ROWS, tc), lambda j: (0, j)),
                  pl.BlockSpec((CONV_K, tc), lambda j: (0, j)), pl.BlockSpec((1, tc), lambda j: (0, j))],
        out_specs=[pl.BlockSpec((T_ROWS, tc), lambda j: (0, j)), pl.BlockSpec((CONV_K, tc), lambda j: (0, j)),
                   pl.BlockSpec((1, tc), lambda j: (0, j))],
        out_shape=[jax.ShapeDtypeStruct((T_ROWS, cols), _MXU), jax.ShapeDtypeStruct((CONV_K, cols), F32),
                   jax.ShapeDtypeStruct((1, cols), F32)],
        compiler_params=_cparams("parallel"),
    )(dy, x, w, b)


def _ssd_chunk_common(dt_raw, prm, c):
    a_row = -jnp.exp(prm[1:2])
    dt = _softplus(dt_raw + prm[0:1])
    rows = lax.broadcasted_iota(jnp.int32, (CHUNK, 1), 0)
    real = jnp.logical_or(c > 0, rows >= PAD_ROWS)
    dt = jnp.where(real, dt, 0.0)
    li = lax.broadcasted_iota(jnp.int32, (CHUNK, CHUNK), 0)
    si = lax.broadcasted_iota(jnp.int32, (CHUNK, CHUNK), 1)
    causal = li >= si
    tri = causal.astype(F32)
    cs = _dot_exact(tri, dt * a_row)
    return dt, a_row, cs, cs.T, causal, tri, real


def _gated_norm_fwd(y, z, w):
    g = y * _silu(z)
    half = SSD_WIDTH // SSD_GROUPS
    outs = [_rms_fwd(g[:, k * half:(k + 1) * half], w[:, k * half:(k + 1) * half]) for k in range(SSD_GROUPS)]
    return jnp.concatenate(outs, axis=1)


GROUP_W = SSD_WIDTH // SSD_GROUPS
PAIR_W = 2 * SSD_HEAD_DIM
STATE_SHAPE = (SSD_GROUPS, SSD_STATE, GROUP_W)


def _head_expander():
    r = lax.broadcasted_iota(jnp.int32, (128, SSD_WIDTH), 0)
    c = lax.broadcasted_iota(jnp.int32, (128, SSD_WIDTH), 1)
    return (c // SSD_HEAD_DIM == r).astype(F32)


def _ssd_expand(dt, cs, prm, ex):
    cs_x = _dot_exact(cs, ex)
    cs_last_x = cs_x[CHUNK - 1:CHUNK, :]
    return (_dot_exact(dt, ex), _dot_exact(prm, ex)[2:3], jnp.exp(cs_x), jnp.exp(cs_last_x),
            jnp.exp(cs_last_x - cs_x))


def _ssd_fwd(xs, bc, dt_raw, z, prm, norm_w, ex):
    def body(xs_ref, bc_ref, dt_ref, z_ref, prm_ref, nw_ref, ex_ref, y_ref, yn_ref, prev_ref, state):
        c = pl.program_id(0)

        @pl.when(c == 0)
        def _():
            state[...] = jnp.zeros_like(state)

        prm = prm_ref[...]
        dt, a_row, cs, cs_t, causal, _, _ = _ssd_chunk_common(dt_ref[...], prm, c)
        dt_x, d_x, e_cs_x, e_last_x, dec_x = _ssd_expand(dt, cs, prm, ex_ref[...])
        xs_all = xs_ref[...]
        bc_all = bc_ref[...]
        xdt = xs_all * dt_x
        xdec = xdt * dec_x
        lane_lo = lax.broadcasted_iota(jnp.int32, (1, PAIR_W), 1) < SSD_HEAD_DIM
        for g in range(SSD_GROUPS):
            gs = slice(g * GROUP_W, (g + 1) * GROUP_W)
            b_g = bc_all[:, g * SSD_STATE:(g + 1) * SSD_STATE]
            c_g = bc_all[:, (SSD_GROUPS + g) * SSD_STATE:(SSD_GROUPS + g + 1) * SSD_STATE]
            st = state[g]
            prev_ref[0, g] = st
            y_off = _dot(c_g, st) * e_cs_x[:, gs]
            state[g] = st * e_last_x[:, gs] + _dot(b_g.T, xdec[:, gs])
            cb = _dot(c_g, b_g, NT)
            for k in range(SSD_HPG // 2):
                h0 = g * SSD_HPG + 2 * k
                ps = slice(h0 * SSD_HEAD_DIM, h0 * SSD_HEAD_DIM + PAIR_W)
                xdt_pair = xdt[:, ps]
                yd = []
                for h in (h0, h0 + 1):
                    lmat = jnp.where(causal, jnp.exp(cs[:, h:h + 1] - cs_t[h:h + 1, :]), 0.0)
                    yd.append(_dot(cb * lmat, xdt_pair))
                y_ref[:, ps] = (jnp.where(lane_lo, yd[0], yd[1]) + y_off[:, k * PAIR_W:(k + 1) * PAIR_W]
                                + xs_all[:, ps] * d_x[:, ps])
        yn_ref[...] = _gated_norm_fwd(y_ref[...], z_ref[...], nw_ref[...]).astype(_MXU)

    row = lambda w: pl.BlockSpec((CHUNK, w), lambda c: (c, 0))
    return pl.pallas_call(
        body, name="ssd_fwd", grid=(N_CHUNKS,),
        in_specs=[row(SSD_WIDTH), row(512), row(128), row(SSD_WIDTH),
                  pl.BlockSpec((8, 128), lambda c: (0, 0)), pl.BlockSpec((1, SSD_WIDTH), lambda c: (0, 0)),
                  pl.BlockSpec((128, SSD_WIDTH), lambda c: (0, 0))],
        out_specs=[row(SSD_WIDTH), row(SSD_WIDTH),
                   pl.BlockSpec((1,) + STATE_SHAPE, lambda c: (c, 0, 0, 0))],
        out_shape=[jax.ShapeDtypeStruct((T_ROWS, SSD_WIDTH), F32), jax.ShapeDtypeStruct((T_ROWS, SSD_WIDTH), _MXU),
                   jax.ShapeDtypeStruct((N_CHUNKS,) + STATE_SHAPE, F32)],
        scratch_shapes=[pltpu.VMEM(STATE_SHAPE, F32)],
        compiler_params=_cparams("arbitrary"),
    )(xs, bc, dt_raw, z, prm, norm_w, ex)


def _ssd_bwd(dyn, dyn_block, z, y_pre, xs, bc, dt_raw, prev, prm, norm_w, ex):
    def body(dyn_ref, z_ref, y_ref, xs_ref, bc_ref, dt_ref, prev_ref, prm_ref, nw_ref, ex_ref,
             dz_ref, dxs_ref, dbc_ref, ddt_ref, dprm_ref, dnw_ref, dstate):
        step = pl.program_id(0)
        c = N_CHUNKS - 1 - step

        @pl.when(step == 0)
        def _():
            dstate[...] = jnp.zeros_like(dstate)
            dprm_ref[...] = jnp.zeros_like(dprm_ref)
            dnw_ref[...] = jnp.zeros_like(dnw_ref)

        prm = prm_ref[...]
        dt, a_row, cs, cs_t, causal, tri, real = _ssd_chunk_common(dt_ref[...], prm, c)
        realf = real.astype(F32)
        z = z_ref[...]
        y_all = y_ref[...]
        nw = nw_ref[...]
        dyn_all = dyn_ref[...]
        sz = _silu(z)
        gated = y_all * sz
        half = SSD_WIDTH // SSD_GROUPS
        dgs, dnws = [], []
        for k in range(SSD_GROUPS):
            sl = slice(k * half, (k + 1) * half)
            dgk, dwk = _rms_bwd(gated[:, sl], nw[:, sl], dyn_all[:, sl])
            dgs.append(dgk)
            dnws.append(jnp.sum(dwk, axis=0, keepdims=True))
        dgated = jnp.concatenate(dgs, axis=1)
        dnw_ref[...] += jnp.concatenate(dnws, axis=1)
        dz_ref[...] = (dgated * y_all * _silu_grad(z)).astype(_MXU)
        dy_all = dgated * sz

        ex = ex_ref[...]
        dt_x, d_x, e_cs_x, e_last_x, dec_x = _ssd_expand(dt, cs, prm, ex)
        xs_all = xs_ref[...]
        bc_all = bc_ref[...]
        xdt = xs_all * dt_x
        xdt_mxu = xdt.astype(_MXU).astype(F32)
        xdec = xdt * dec_x
        dcp = dy_all * e_cs_x
        lane_lo = lax.broadcasted_iota(jnp.int32, (1, PAIR_W), 1) < SSD_HEAD_DIM
        upper = (lax.broadcasted_iota(jnp.int32, (CHUNK, CHUNK), 0)
                 <= lax.broadcasted_iota(jnp.int32, (CHUNK, CHUNK), 1))
        last_row = (lax.broadcasted_iota(jnp.int32, (CHUNK, 1), 0) == CHUNK - 1).astype(F32)
        dbs, dcs_, dxdt_parts, last_parts = [], [], [], []
        for g in range(SSD_GROUPS):
            gs = slice(g * GROUP_W, (g + 1) * GROUP_W)
            b_g = bc_all[:, g * SSD_STATE:(g + 1) * SSD_STATE]
            c_g = bc_all[:, (SSD_GROUPS + g) * SSD_STATE:(SSD_GROUPS + g + 1) * SSD_STATE]
            prev_t = prev_ref[0, g]
            dst = dstate[g]
            dc_g = _dot(dcp[:, gs], prev_t, NT)
            db_g = _dot(xdec[:, gs], dst, NT)
            dxdt_state = _dot(b_g, dst) * dec_x[:, gs]
            dstate[g] = dst * e_last_x[:, gs] + _dot(c_g.T, dcp[:, gs])
            last_parts.append(jnp.sum(xdt_mxu[:, gs] * dxdt_state, axis=0, keepdims=True)
                              + jnp.sum(dst * prev_t, axis=0, keepdims=True) * e_last_x[:, gs])
            cb_t = _dot(b_g, c_g, NT)
            dcb_t = jnp.zeros((CHUNK, CHUNK), F32)
            for k in range(SSD_HPG // 2):
                h0 = g * SSD_HPG + 2 * k
                ps = slice(h0 * SSD_HEAD_DIM, h0 * SSD_HEAD_DIM + PAIR_W)
                dy_pair = dy_all[:, ps]
                xdt_pair = xdt[:, ps]
                dd = []
                for h in (h0, h0 + 1):
                    lmat_t = jnp.where(upper, jnp.exp(cs_t[h:h + 1, :] - cs[:, h:h + 1]), 0.0)
                    dd.append(_dot(cb_t * lmat_t, dy_pair))
                    mine = lane_lo if h == h0 else jnp.logical_not(lane_lo)
                    dcb_t = dcb_t + _dot(jnp.where(mine, xdt_pair, 0.0), dy_pair, NT) * lmat_t
                dxdt_parts.append(jnp.where(lane_lo, dd[0], dd[1]) + dxdt_state[:, k * PAIR_W:(k + 1) * PAIR_W])
            dc_g = dc_g + _dot(dcb_t, b_g, TN)
            db_g = db_g + _dot(dcb_t, c_g)
            dbs.append(db_g * realf)
            dcs_.append(dc_g * realf)
        dbc_ref[...] = jnp.concatenate(dbs + dcs_, axis=1)
        dxdt = jnp.concatenate(dxdt_parts, axis=1)
        dxs_ref[...] = (dxdt * dt_x + dy_all * d_x) * realf
        ddt_all = _dot_exact(dxdt * xs_all, ex, NT)
        rows = jnp.concatenate([jnp.concatenate(last_parts, axis=1), jnp.sum(dy_all * xs_all, axis=0, keepdims=True),
                                jnp.zeros((6, SSD_WIDTH), F32)], axis=0)
        rows = _dot_exact(rows, ex, NT)
        dd_row = rows[1:2]
        dy_mxu = dy_all.astype(_MXU).astype(F32)
        dcs_all = (_dot_exact(dy_mxu * (y_all - xs_all * d_x), ex, NT) - _dot_exact(xdt_mxu * dxdt, ex, NT)
                   + last_row * rows[0:1])
        dda = _dot_exact(tri, dcs_all, TN)
        ddt = (ddt_all + dda * a_row) * realf
        ddt_raw = ddt * _sigmoid(dt_ref[...] + prm[0:1])
        ddt_ref[...] = ddt_raw.astype(_MXU)
        da_log = jnp.sum(dda * dt, axis=0, keepdims=True) * a_row
        dprm_ref[0:1, :] += jnp.sum(ddt_raw, axis=0, keepdims=True)
        dprm_ref[1:2, :] += da_log
        dprm_ref[2:3, :] += dd_row

    rev = lambda w, blk=0: pl.BlockSpec((CHUNK, w), lambda s, blk=blk: (N_CHUNKS - 1 - s, blk))
    return pl.pallas_call(
        body, name="ssd_bwd", grid=(N_CHUNKS,),
        in_specs=[rev(SSD_WIDTH, dyn_block), rev(SSD_WIDTH), rev(SSD_WIDTH), rev(SSD_WIDTH), rev(512), rev(128),
                  pl.BlockSpec((1,) + STATE_SHAPE, lambda s: (N_CHUNKS - 1 - s, 0, 0, 0)),
                  pl.BlockSpec((8, 128), lambda s: (0, 0)), pl.BlockSpec((1, SSD_WIDTH), lambda s: (0, 0)),
                  pl.BlockSpec((128, SSD_WIDTH), lambda s: (0, 0))],
        out_specs=[rev(SSD_WIDTH), rev(SSD_WIDTH), rev(512), rev(128),
                   pl.BlockSpec((8, 128), lambda s: (0, 0)), pl.BlockSpec((1, SSD_WIDTH), lambda s: (0, 0))],
        out_shape=[jax.ShapeDtypeStruct((T_ROWS, SSD_WIDTH), _MXU), jax.ShapeDtypeStruct((T_ROWS, SSD_WIDTH), F32),
                   jax.ShapeDtypeStruct((T_ROWS, 512), F32), jax.ShapeDtypeStruct((T_ROWS, 128), _MXU),
                   jax.ShapeDtypeStruct((8, 128), F32), jax.ShapeDtypeStruct((1, SSD_WIDTH), F32)],
        scratch_shapes=[pltpu.VMEM(STATE_SHAPE, F32)],
        compiler_params=_cparams("arbitrary"),
    )(dyn, z, y_pre, xs, bc, dt_raw, prev, prm, norm_w, ex)


LRU_PAIRS = 8


def _lru_gates(xr, wa_ref, wx_ref, prm):
    pre_r, pre_i = [], []
    for k in range(LRU_PAIRS):
        xk = xr[:, k * 128:(k + 1) * 128]
        pre_r.append(_dot(xk, wa_ref[k]))
        pre_i.append(_dot(xk, wx_ref[k]))
    r = _sigmoid(jnp.concatenate(pre_r, axis=1) + prm[0:1])
    i = _sigmoid(jnp.concatenate(pre_i, axis=1) + prm[1:2])
    sp = _softplus(-prm[2:3])
    log_a = (-LRU_C) * r * sp
    a = jnp.exp(log_a)
    s = jnp.sqrt(-jnp.tanh(log_a) * (a * a + 1.0))
    return r, i, a, s, sp


def _lru_fwd(xr, gate, wa, wx, prm):
    def body(xr_ref, g_ref, wa_ref, wx_ref, prm_ref, hs_ref, yn_ref, carry, a_s, u_s):
        @pl.when(pl.program_id(0) == 0)
        def _():
            carry[...] = jnp.zeros_like(carry)

        prm = prm_ref[...]
        xr_t = xr_ref[...]
        _, i, a, s, _ = _lru_gates(xr_t, wa_ref, wx_ref, prm)
        a_s[...] = a
        u_s[...] = s * (i * xr_t)
        rid = lax.broadcasted_iota(jnp.int32, (8, LRU_WIDTH), 0)

        def group(k, h):
            off = pl.multiple_of(k * 8, 8)
            a8 = a_s[pl.ds(off, 8), :]
            u8 = u_s[pl.ds(off, 8), :]
            out = jnp.zeros((8, LRU_WIDTH), F32)
            for r_ in range(8):
                h = a8[r_:r_ + 1] * h + u8[r_:r_ + 1]
                out = jnp.where(rid == r_, h, out)
            hs_ref[pl.ds(off, 8), :] = out
            return h

        carry[0:1, :] = lax.fori_loop(0, CHUNK // 8, group, carry[0:1, :])
        gel, _ = _gelu_and_grad(g_ref[...])
        yn_ref[...] = _rms_fwd(gel * hs_ref[...], prm[3:4]).astype(_MXU)

    row = pl.BlockSpec((CHUNK, LRU_WIDTH), lambda t: (t, 0))
    wspec = pl.BlockSpec((LRU_PAIRS, 128, 128), lambda t: (0, 0, 0))
    return pl.pallas_call(
        body, name="lru_fwd", grid=(N_CHUNKS,),
        in_specs=[row, row, wspec, wspec, pl.BlockSpec((8, LRU_WIDTH), lambda t: (0, 0))],
        out_specs=[row, row],
        out_shape=[jax.ShapeDtypeStruct((T_ROWS, LRU_WIDTH), F32), jax.ShapeDtypeStruct((T_ROWS, LRU_WIDTH), _MXU)],
        scratch_shapes=[pltpu.VMEM((8, LRU_WIDTH), F32), pltpu.VMEM((CHUNK, LRU_WIDTH), F32),
                        pltpu.VMEM((CHUNK, LRU_WIDTH), F32)],
        compiler_params=_cparams("arbitrary"),
    )(xr, gate, wa, wx, prm)


def _lru_bwd(dyn, dyn_block, gate, xr, hs, wa, wx, wa_t, wx_t, prm):
    def body(dyn_ref, g_ref, xr_ref, hs_ref, hsp_ref, wa_ref, wx_ref, wat_ref, wxt_ref, prm_ref,
             dg_ref, dxr_ref, dwa_ref, dwx_ref, dprm_ref, carry, a_s, d_s):
        step = pl.program_id(0)
        tile = N_CHUNKS - 1 - step

        @pl.when(step == 0)
        def _():
            carry[...] = jnp.zeros_like(carry)
            dwa_ref[...] = jnp.zeros_like(dwa_ref)
            dwx_ref[...] = jnp.zeros_like(dwx_ref)
            dprm_ref[...] = jnp.zeros_like(dprm_ref)

        prm = prm_ref[...]
        xr_t = xr_ref[...]
        r, i, a, s, sp = _lru_gates(xr_t, wa_ref, wx_ref, prm)
        hs_t = hs_ref[...]
        gel, dgel = _gelu_and_grad(g_ref[...])
        dy, dnw = _rms_bwd(gel * hs_t, prm[3:4], dyn_ref[...])
        dg_ref[...] = (dy * hs_t * dgel).astype(_MXU)
        a_s[...] = a
        d_s[...] = dy * gel
        rid = lax.broadcasted_iota(jnp.int32, (8, LRU_WIDTH), 0)

        def group(k, cr):
            off = pl.multiple_of((CHUNK // 8 - 1 - k) * 8, 8)
            a8 = a_s[pl.ds(off, 8), :]
            d8 = d_s[pl.ds(off, 8), :]
            out = jnp.zeros((8, LRU_WIDTH), F32)
            for r_ in reversed(range(8)):
                dht = d8[r_:r_ + 1] + cr
                out = jnp.where(rid == r_, dht, out)
                cr = a8[r_:r_ + 1] * dht
            d_s[pl.ds(off, 8), :] = out
            return cr

        carry[0:1, :] = lax.fori_loop(0, CHUNK // 8, group, carry[0:1, :])
        dht = d_s[...]
        before = hsp_ref[CHUNK - 8:CHUNK, :][7:8] * (tile > 0).astype(F32)
        first = lax.broadcasted_iota(jnp.int32, (CHUNK, 1), 0) == 0
        hprev = jnp.where(first, before, pltpu.roll(hs_t, 1, 0))
        da = dht * hprev
        ixr = i * xr_t
        ds = dht * ixr
        dlog_a = da * a - ds * (a * a) / s
        dr = dlog_a * ((-LRU_C) * sp)
        dsp = jnp.sum(dlog_a * ((-LRU_C) * r), axis=0, keepdims=True)
        dlam = dsp * (-_sigmoid(-prm[2:3]))
        di = dht * s * xr_t
        dpre_r = dr * r * (1.0 - r)
        dpre_i = di * i * (1.0 - i)
        dxr = dht * s * i
        parts = []
        for k in range(LRU_PAIRS):
            sl = slice(k * 128, (k + 1) * 128)
            parts.append(_dot(dpre_r[:, sl], wat_ref[k]) + _dot(dpre_i[:, sl], wxt_ref[k]))
            dwa_ref[k] += _dot(xr_t[:, sl], dpre_r[:, sl], TN)
            dwx_ref[k] += _dot(xr_t[:, sl], dpre_i[:, sl], TN)
        dxr_ref[...] = dxr + jnp.concatenate(parts, axis=1)
        dprm_ref[0:1, :] += jnp.sum(dpre_r, axis=0, keepdims=True)
        dprm_ref[1:2, :] += jnp.sum(dpre_i, axis=0, keepdims=True)
        dprm_ref[2:3, :] += dlam
        dprm_ref[3:4, :] += jnp.sum(dnw, axis=0, keepdims=True)

    rev = lambda blk=0: pl.BlockSpec((CHUNK, LRU_WIDTH), lambda s, blk=blk: (N_CHUNKS - 1 - s, blk))
    wspec = pl.BlockSpec((LRU_PAIRS, 128, 128), lambda s: (0, 0, 0))
    return pl.pallas_call(
        body, name="lru_bwd", grid=(N_CHUNKS,),
        in_specs=[rev(dyn_block), rev(), rev(), rev(),
                  pl.BlockSpec((CHUNK, LRU_WIDTH), lambda s: (jnp.maximum(N_CHUNKS - 2 - s, 0), 0)),
                  wspec, wspec, wspec, wspec, pl.BlockSpec((8, LRU_WIDTH), lambda s: (0, 0))],
        out_specs=[rev(), rev(), wspec, wspec, pl.BlockSpec((8, LRU_WIDTH), lambda s: (0, 0))],
        out_shape=[jax.ShapeDtypeStruct((T_ROWS, LRU_WIDTH), _MXU), jax.ShapeDtypeStruct((T_ROWS, LRU_WIDTH), F32),
                   jax.ShapeDtypeStruct((LRU_PAIRS, 128, 128), F32), jax.ShapeDtypeStruct((LRU_PAIRS, 128, 128), F32),
                   jax.ShapeDtypeStruct((8, LRU_WIDTH), F32)],
        scratch_shapes=[pltpu.VMEM((8, LRU_WIDTH), F32), pltpu.VMEM((CHUNK, LRU_WIDTH), F32),
                        pltpu.VMEM((CHUNK, LRU_WIDTH), F32)],
        compiler_params=_cparams("arbitrary"),
    )(dyn, gate, xr, hs, hs, wa, wx, wa_t, wx_t, prm)


SEC_NAMES = ("z", "xs", "bc", "dt", "g", "x")
SEC_WIDTH = {"z": 1024, "xs": 1024, "bc": 512, "dt": 128, "g": 1024, "x": 1024}


def _pair_blocks(w):
    w = w.reshape(LRU_PAIRS, 2, 64, 64)
    zero = jnp.zeros((LRU_PAIRS, 64, 64), w.dtype)
    top = jnp.concatenate([w[:, 0], zero], axis=2)
    bot = jnp.concatenate([zero, w[:, 1]], axis=2)
    return jnp.concatenate([top, bot], axis=1)


def _unpair_blocks(wp):
    return jnp.stack([wp[:, :64, :64], wp[:, 64:, 64:]], axis=1).reshape(16, 64, 64)


def _pad_lanes(v, width=128):
    return jnp.pad(v, ((0, 0), (0, width - v.shape[1])))


def _local_step(x, target, meta, p):
    g = {}
    ex = _head_expander()
    h0 = _embed(x, meta)
    u1 = _rmsnorm(h0, p["norm1_w"], name="norm1")
    proj = {}
    for s in SEC_NAMES:
        wdt = SEC_WIDTH[s]
        proj[s] = _mm([(u1, 0, p["w_in_" + s], 0, D_MODEL)], T_ROWS, wdt, tm=544, tn=min(wdt, 512), mode="nt",
                      out_dtype=F32, name="proj_" + s)
    ssd_prm = jnp.concatenate([_pad_lanes(p["ssd_dt_bias"]), _pad_lanes(p["ssd_a_log"]), _pad_lanes(p["ssd_d"]),
                               jnp.zeros((5, 128), F32)], axis=0)
    xs_act = _conv_fwd(proj["xs"], p["ssd_conv_w"][:, :SSD_WIDTH], p["ssd_conv_b"][:, :SSD_WIDTH], silu=True,
                       name="ssd_conv_xs")
    bc_act = _conv_fwd(proj["bc"], p["ssd_conv_w"][:, SSD_WIDTH:], p["ssd_conv_b"][:, SSD_WIDTH:], silu=True,
                       name="ssd_conv_bc")
    y_pre, y_ssd, prev = _ssd_fwd(xs_act, bc_act, proj["dt"], proj["z"], ssd_prm, p["ssd_norm_w"], ex)
    xr = _conv_fwd(proj["x"], p["lru_conv_w"], p["lru_conv_b"], silu=False, name="lru_conv")
    wa_p, wx_p = _pair_blocks(p["lru_wa"]), _pair_blocks(p["lru_wx"])
    lru_prm = jnp.concatenate([p["lru_ba"], p["lru_bx"], p["lru_lambda"], p["lru_norm_w"],
                               jnp.zeros((4, LRU_WIDTH), F32)], axis=0)
    hs, y_lru = _lru_fwd(xr, proj["g"], wa_p.astype(_MXU), wx_p.astype(_MXU), lru_prm)
    h1 = _mm([(y_ssd, 0, p["w_out"], 0, SSD_WIDTH), (y_lru, 0, p["w_out"], 1, LRU_WIDTH)], T_ROWS, D_MODEL,
             tm=544, tn=512, mode="nn", out_dtype=F32, name="out_proj", residual=h0)
    u2 = _rmsnorm(h1, p["norm2_w"], name="norm2")
    gp, up, act = _ffn_up(u2, p["w_gate"], p["w_up"])
    h2 = _mm([(act, 0, p["w_down"], 0, D_FF)], T_ROWS, D_MODEL, tm=544, tn=512, mode="nn", out_dtype=F32,
             name="ffn_down", residual=h1)
    loss, dh2, dh2b, g["final_norm_w"] = _loss_head(h2, target, p["final_norm_w"])
    dgp, dup = _ffn_bwd_act(dh2b, p["w_down"], gp, up)
    g["w_down"] = _mm([(act, 0, dh2b, 0, T_ROWS)], D_FF, D_MODEL, tm=1408, tn=512, mode="tn", out_dtype=F32,
                      name="dw_down")
    dh1, dh1b, g["norm2_w"] = _mm_norm_bwd([(dgp, p["w_gate"], D_FF), (dup, p["w_up"], D_FF)], h1, p["norm2_w"],
                                           dh2, name="ffn_bwd_in")
    g["w_gate"] = _mm([(dgp, 0, u2, 0, T_ROWS)], D_FF, D_MODEL, tm=1408, tn=512, mode="tn", out_dtype=F32,
                      name="dw_gate")
    g["w_up"] = _mm([(dup, 0, u2, 0, T_ROWS)], D_FF, D_MODEL, tm=1408, tn=512, mode="tn", out_dtype=F32,
                    name="dw_up")
    dycat = _mm([(dh1b, 0, p["w_out"], 0, D_MODEL)], T_ROWS, 2 * D_MODEL, tm=544, tn=512, mode="nt", out_dtype=F32,
                name="out_proj_bwd")
    g["w_out"] = jnp.concatenate(
        [_mm([(y, 0, dh1b, 0, T_ROWS)], D_MODEL, D_MODEL, tm=512, tn=512, mode="tn", out_dtype=F32, name=nm)
         for y, nm in ((y_ssd, "dw_out_ssd"), (y_lru, "dw_out_lru"))], axis=0)
    dgate, dxr, dwa_p, dwx_p, dlru_prm = _lru_bwd(dycat, 1, proj["g"], xr, hs, wa_p.astype(_MXU), wx_p.astype(_MXU),
                                                  jnp.swapaxes(wa_p, 1, 2).astype(_MXU),
                                                  jnp.swapaxes(wx_p, 1, 2).astype(_MXU), lru_prm)
    g["lru_wa"], g["lru_wx"] = _unpair_blocks(dwa_p), _unpair_blocks(dwx_p)
    g["lru_ba"], g["lru_bx"], g["lru_lambda"], g["lru_norm_w"] = (dlru_prm[k:k + 1] for k in range(4))
    dx_lru, g["lru_conv_w"], g["lru_conv_b"] = _conv_bwd(dxr, proj["x"], p["lru_conv_w"], p["lru_conv_b"], silu=False,
                                                         name="lru_conv_bwd")
    dz, dxs_act, dbc_act, ddt, dssd_prm, g["ssd_norm_w"] = _ssd_bwd(dycat, 0, proj["z"], y_pre, xs_act, bc_act,
                                                                    proj["dt"], prev, ssd_prm, p["ssd_norm_w"], ex)
    g["ssd_dt_bias"], g["ssd_a_log"], g["ssd_d"] = (dssd_prm[k:k + 1, :SSD_HEADS] for k in range(3))
    dxs, dcw_xs, dcb_xs = _conv_bwd(dxs_act, proj["xs"], p["ssd_conv_w"][:, :SSD_WIDTH],
                                    p["ssd_conv_b"][:, :SSD_WIDTH], silu=True, name="ssd_conv_xs_bwd")
    dbc, dcw_bc, dcb_bc = _conv_bwd(dbc_act, proj["bc"], p["ssd_conv_w"][:, SSD_WIDTH:],
                                    p["ssd_conv_b"][:, SSD_WIDTH:], silu=True, name="ssd_conv_bc_bwd")
    g["ssd_conv_w"] = jnp.concatenate([dcw_xs, dcw_bc], axis=1)
    g["ssd_conv_b"] = jnp.concatenate([dcb_xs, dcb_bc], axis=1)
    dproj = {"z": dz, "xs": dxs, "bc": dbc, "dt": ddt, "g": dgate, "x": dx_lru}
    dh0, _, g["norm1_w"] = _mm_norm_bwd([(dproj[s], p["w_in_" + s], SEC_WIDTH[s]) for s in SEC_NAMES], h0,
                                        p["norm1_w"], dh1, name="in_proj_bwd")
    for s in SEC_NAMES:
        wdt = SEC_WIDTH[s]
        g["w_in_" + s] = _mm([(dproj[s], 0, u1, 0, T_ROWS)], wdt, D_MODEL, tm=min(wdt, 512), tn=512, mode="tn",
                             out_dtype=F32, name="dw_in_" + s)
    g["meta_tokens"] = dh0[PAD_ROWS:X_ROW0]
    return loss, dh0[X_ROW0:], g


MESH = pl.DeviceIdType.MESH
ANY = pl.BlockSpec(memory_space=pl.ANY)


def _my_place():
    return lax.axis_index("x"), lax.axis_index("y"), lax.axis_index("c")


def _other_chips(x, y):
    return [(1 - x, y), (x, 1 - y), (1 - x, 1 - y)]


def _gather_shards(shards):
    n = len(shards)

    def body(*refs):
        ins, outs = refs[:n], refs[n:2 * n]
        send_sems, recv_sems, local_sems = refs[2 * n:]
        x, y, c = _my_place()
        me = 2 * x + y
        peers = _other_chips(x, y)
        local = [pltpu.make_async_copy(ins[k], outs[k].at[me], local_sems.at[k]) for k in range(n)]
        for cp in local:
            cp.start()
        for k in range(n):
            for j, (px, py) in enumerate(peers):
                pltpu.make_async_remote_copy(
                    src_ref=ins[k], dst_ref=outs[k].at[me], send_sem=send_sems.at[3 * k + j],
                    recv_sem=recv_sems.at[3 * k + j], device_id=(px, py, c), device_id_type=MESH).start()
        for k in range(n):
            for j, (px, py) in enumerate(peers):
                pltpu.make_async_remote_copy(
                    src_ref=ins[k], dst_ref=outs[k].at[2 * px + py], send_sem=send_sems.at[3 * k + j],
                    recv_sem=recv_sems.at[3 * k + j], device_id=(px, py, c), device_id_type=MESH).wait()
        for cp in local:
            cp.wait()

    return pl.pallas_call(
        body, name="gather_weights", in_specs=[ANY] * n, out_specs=[ANY] * n,
        out_shape=[jax.ShapeDtypeStruct((N_SHARDS,) + s.shape, s.dtype) for s in shards],
        scratch_shapes=[pltpu.SemaphoreType.DMA((3 * n,)), pltpu.SemaphoreType.DMA((3 * n,)),
                        pltpu.SemaphoreType.DMA((n,))],
    )(*shards)


def _scatter_grads(grads4):
    n = len(grads4)

    def body(*refs):
        ins, outs = refs[:n], refs[n:2 * n]
        send_sems, recv_sems = refs[2 * n:]
        x, y, c = _my_place()
        peers = _other_chips(x, y)
        for k in range(n):
            for j, (px, py) in enumerate(peers):
                pltpu.make_async_remote_copy(
                    src_ref=ins[k].at[2 * px + py], dst_ref=outs[k].at[j], send_sem=send_sems.at[3 * k + j],
                    recv_sem=recv_sems.at[3 * k + j], device_id=(px, py, c), device_id_type=MESH).start()
        for k in range(n):
            for j, (px, py) in enumerate(peers):
                pltpu.make_async_remote_copy(
                    src_ref=ins[k].at[2 * px + py], dst_ref=outs[k].at[j], send_sem=send_sems.at[3 * k + j],
                    recv_sem=recv_sems.at[3 * k + j], device_id=(px, py, c), device_id_type=MESH).wait()

    return pl.pallas_call(
        body, name="scatter_grads", in_specs=[ANY] * n, out_specs=[ANY] * n,
        out_shape=[jax.ShapeDtypeStruct((3,) + g.shape[1:], g.dtype) for g in grads4],
        scratch_shapes=[pltpu.SemaphoreType.DMA((3 * n,)), pltpu.SemaphoreType.DMA((3 * n,))],
    )(*grads4)


def _swap_with_sibling(parts):
    n = len(parts)

    def body(*refs):
        ins, outs = refs[:n], refs[n:2 * n]
        send_sems, recv_sems = refs[2 * n:]
        x, y, c = _my_place()
        copies = [pltpu.make_async_remote_copy(
            src_ref=ins[k], dst_ref=outs[k], send_sem=send_sems.at[k], recv_sem=recv_sems.at[k],
            device_id=(x, y, 1 - c), device_id_type=MESH) for k in range(n)]
        for cp in copies:
            cp.start()
        for cp in copies:
            cp.wait()

    return pl.pallas_call(
        body, name="swap_with_sibling", in_specs=[ANY] * n, out_specs=[ANY] * n,
        out_shape=[jax.ShapeDtypeStruct(a.shape, a.dtype) for a in parts],
        scratch_shapes=[pltpu.SemaphoreType.DMA((n,)), pltpu.SemaphoreType.DMA((n,))],
    )(*parts)


def _all_reduce_small(pack):
    rows = pack.shape[0]

    def body(x_ref, o_ref, buf, send_sems, recv_sems):
        x, y, c = _my_place()
        me = 4 * x + 2 * y + c
        buf[me] = x_ref[...]
        masks = [(m >> 2 & 1, m >> 1 & 1, m & 1) for m in range(1, N_DEV)]

        def copy(i):
            mx, my, mc = masks[i]
            px, py, pc = x ^ mx, y ^ my, c ^ mc
            return pltpu.make_async_remote_copy(
                src_ref=x_ref, dst_ref=buf.at[me], send_sem=send_sems.at[i], recv_sem=recv_sems.at[i],
                device_id=(px, py, pc), device_id_type=MESH), 4 * px + 2 * py + pc

        for i in range(N_DEV - 1):
            copy(i)[0].start()
        for i in range(N_DEV - 1):
            cp, peer = copy(i)
            cp.wait_send()
            pltpu.make_async_remote_copy(
                src_ref=x_ref, dst_ref=buf.at[peer], send_sem=send_sems.at[i], recv_sem=recv_sems.at[i],
                device_id=(x, y, c), device_id_type=MESH).wait_recv()
        acc = buf[0]
        for d in range(1, N_DEV):
            acc = acc + buf[d]
        o_ref[...] = acc

    vmem = pl.BlockSpec(memory_space=pltpu.VMEM)
    return pl.pallas_call(
        body, name="all_reduce_small", in_specs=[vmem], out_specs=vmem,
        out_shape=jax.ShapeDtypeStruct(pack.shape, F32),
        scratch_shapes=[pltpu.VMEM((N_DEV, rows, pack.shape[1]), F32), pltpu.SemaphoreType.DMA((N_DEV - 1,)),
                        pltpu.SemaphoreType.DMA((N_DEV - 1,))],
        compiler_params=pltpu.CompilerParams(vmem_limit_bytes=VMEM_LIMIT_BYTES),
    )(pack)


def _elementwise_tile(rows, cols, limit=256):
    for t in range(limit, 15, -16):
        if rows % t == 0:
            return (t, cols), rows // t, lambda i: (i, 0)
    assert cols % limit == 0
    return (rows, limit), cols // limit, lambda i: (0, i)


def _partial_sum(own, land, *, name):
    r, c = own.shape
    tile, steps, imap = _elementwise_tile(r, c)

    def body(own_ref, land_ref, o_ref):
        acc = own_ref[...]
        for j in range(3):
            acc = acc + land_ref[j].astype(F32)
        o_ref[...] = acc

    return pl.pallas_call(
        body, name=name, grid=(steps,),
        in_specs=[pl.BlockSpec(tile, imap), pl.BlockSpec((3,) + tile, lambda i: (0,) + imap(i))],
        out_specs=pl.BlockSpec(tile, imap),
        out_shape=jax.ShapeDtypeStruct((r, c), F32),
        compiler_params=_cparams("parallel"),
    )(own, land)


def _adamw_math(w, g, m, v):
    m = ADAM_B1 * m + (1.0 - ADAM_B1) * g
    v = ADAM_B2 * v + (1.0 - ADAM_B2) * (g * g)
    m_hat = m / (1.0 - ADAM_B1 ** ADAM_STEP)
    v_hat = v / (1.0 - ADAM_B2 ** ADAM_STEP)
    delta = -ADAM_LR * (m_hat / (jnp.sqrt(v_hat) + ADAM_EPS) + ADAM_WD * w)
    return delta, m, v


def _adamw(w, grad_parts, m, v, *, name):
    r, c = w.shape
    tile_shape, steps, imap = _elementwise_tile(r, c)
    n = len(grad_parts)

    def body(*refs):
        w_ref, m_ref, v_ref = refs[:3]
        g_refs = refs[3:3 + n]
        g_out, d_out, m_out, v_out = refs[3 + n:]
        g = g_refs[0][...]
        for k in range(1, n):
            g = g + g_refs[k][...]
        delta, m_new, v_new = _adamw_math(w_ref[...], g, m_ref[...], v_ref[...])
        g_out[...] = g
        d_out[...] = delta
        m_out[...] = m_new
        v_out[...] = v_new

    tile = pl.BlockSpec(tile_shape, imap)
    return pl.pallas_call(
        body, name=name, grid=(steps,), in_specs=[tile] * (3 + n), out_specs=[tile] * 4,
        out_shape=[jax.ShapeDtypeStruct((r, c), F32)] * 4,
        compiler_params=_cparams("parallel"),
    )(w, m, v, *grad_parts)


WEIGHT_NAMES = ("meta_tokens", "norm1_w", "w_in", "ssd_conv_w", "ssd_conv_b", "ssd_dt_bias", "ssd_a_log", "ssd_d",
                "ssd_norm_w", "lru_conv_w", "lru_conv_b", "lru_wa", "lru_ba", "lru_wx", "lru_bx", "lru_lambda",
                "lru_norm_w", "w_out", "norm2_w", "w_gate", "w_up", "w_down", "final_norm_w")
BIG = ("w_in", "w_out", "w_gate", "w_up", "w_down")
SMALL_SHARDED = {"meta_tokens": (N_META, D_MODEL), "ssd_conv_w": (CONV_K, 1536), "lru_conv_w": (CONV_K, LRU_WIDTH)}
SMALL = tuple(n for n in WEIGHT_NAMES if n not in BIG)
PACK_COLS = 1024


def _pack(arrays):
    flat = jnp.concatenate([a.reshape(-1) for a in arrays])
    rows = -(-flat.shape[0] // (8 * PACK_COLS)) * 8
    return jnp.pad(flat, (0, rows * PACK_COLS - flat.shape[0])).reshape(rows, PACK_COLS)


def _unpack(pack, shapes):
    flat = pack.reshape(-1)
    out, off = [], 0
    for s in shapes:
        size = math.prod(s)
        out.append(flat[off:off + size].reshape(s))
        off += size
    return out


def _unshard_cols(g4):
    return jnp.swapaxes(g4, 0, 1).reshape(g4.shape[1], -1)


COL_SHARDED = ("w_in", "w_gate", "w_up")
IN_ROWS = {"z": (0, 1024), "xs": (1024, 2048), "bc": (2048, 2560), "dt": (2560, 2576), "g": (2576, 3600),
           "x": (3600, IN_COLS)}


def _rows_view(name, block):
    return jnp.swapaxes(block[0], 0, 1) if name in COL_SHARDED else block[0]


def _param_view(name, rows):
    return (jnp.swapaxes(rows, 0, 1) if name in COL_SHARDED else rows)[None]


def kernel(x, meta_tokens, norm1_w, w_in, ssd_conv_w, ssd_conv_b, ssd_dt_bias, ssd_a_log, ssd_d, ssd_norm_w, lru_conv_w, lru_conv_b, lru_wa, lru_ba, lru_wx, lru_bx, lru_lambda, lru_norm_w, w_out, norm2_w, w_gate, w_up, w_down, final_norm_w, loss_target, m_meta_tokens, m_norm1_w, m_w_in, m_ssd_conv_w, m_ssd_conv_b, m_ssd_dt_bias, m_ssd_a_log, m_ssd_d, m_ssd_norm_w, m_lru_conv_w, m_lru_conv_b, m_lru_wa, m_lru_ba, m_lru_wx, m_lru_bx, m_lru_lambda, m_lru_norm_w, m_w_out, m_norm2_w, m_w_gate, m_w_up, m_w_down, m_final_norm_w, v_meta_tokens, v_norm1_w, v_w_in, v_ssd_conv_w, v_ssd_conv_b, v_ssd_dt_bias, v_ssd_a_log, v_ssd_d, v_ssd_norm_w, v_lru_conv_w, v_lru_conv_b, v_lru_wa, v_lru_ba, v_lru_wx, v_lru_bx, v_lru_lambda, v_lru_norm_w, v_w_out, v_norm2_w, v_w_gate, v_w_up, v_w_down, v_final_norm_w):
    w = dict(zip(WEIGHT_NAMES, (meta_tokens, norm1_w, w_in, ssd_conv_w, ssd_conv_b, ssd_dt_bias, ssd_a_log, ssd_d, ssd_norm_w, lru_conv_w, lru_conv_b, lru_wa, lru_ba, lru_wx, lru_bx, lru_lambda, lru_norm_w, w_out, norm2_w, w_gate, w_up, w_down, final_norm_w)))
    m = dict(zip(WEIGHT_NAMES, (m_meta_tokens, m_norm1_w, m_w_in, m_ssd_conv_w, m_ssd_conv_b, m_ssd_dt_bias, m_ssd_a_log, m_ssd_d, m_ssd_norm_w, m_lru_conv_w, m_lru_conv_b, m_lru_wa, m_lru_ba, m_lru_wx, m_lru_bx, m_lru_lambda, m_lru_norm_w, m_w_out, m_norm2_w, m_w_gate, m_w_up, m_w_down, m_final_norm_w)))
    v = dict(zip(WEIGHT_NAMES, (v_meta_tokens, v_norm1_w, v_w_in, v_ssd_conv_w, v_ssd_conv_b, v_ssd_dt_bias, v_ssd_a_log, v_ssd_d, v_ssd_norm_w, v_lru_conv_w, v_lru_conv_b, v_lru_wa, v_lru_ba, v_lru_wx, v_lru_bx, v_lru_lambda, v_lru_norm_w, v_w_out, v_norm2_w, v_w_gate, v_w_up, v_w_down, v_final_norm_w)))
    me = 2 * lax.axis_index("x") + lax.axis_index("y")

    big2d = {n: _rows_view(n, w[n]) for n in BIG}
    small_local = jnp.concatenate([w["meta_tokens"].reshape(-1), w["ssd_conv_w"].reshape(-1),
                                   w["lru_conv_w"].reshape(-1)])[None]
    gathered = _gather_shards([big2d[n].astype(_MXU) for n in BIG] + [small_local])
    full = {n: a.reshape(-1, D_MODEL) for n, a in zip(BIG, gathered[:-1])}
    sm = gathered[-1][:, 0]
    meta_full = _unshard_cols(sm[:, :4096].reshape(N_SHARDS, N_META, 256))
    ssd_conv_w_full = _unshard_cols(sm[:, 4096:5632].reshape(N_SHARDS, CONV_K, 384))
    lru_conv_w_full = _unshard_cols(sm[:, 5632:].reshape(N_SHARDS, CONV_K, 256))
    p = {"w_in_" + s: full["w_in"][lo:hi] for s, (lo, hi) in IN_ROWS.items()}
    p["w_in_dt"] = jnp.pad(p["w_in_dt"], ((0, SEC_WIDTH["dt"] - SSD_HEADS), (0, 0)))
    p.update({
        "w_out": full["w_out"], "w_gate": full["w_gate"], "w_up": full["w_up"], "w_down": full["w_down"],
        "ssd_conv_w": ssd_conv_w_full, "lru_conv_w": lru_conv_w_full,
        "lru_wa": w["lru_wa"][0], "lru_wx": w["lru_wx"][0], "final_norm_w": w["final_norm_w"][None],
    })
    for n in ("norm1_w", "ssd_conv_b", "ssd_dt_bias", "ssd_a_log", "ssd_d", "ssd_norm_w", "lru_conv_b", "lru_ba",
              "lru_bx", "lru_lambda", "lru_norm_w", "norm2_w"):
        p[n] = w[n]

    loss, grad_x, g = _local_step(x[0], loss_target[0], meta_full, p)

    g["w_in"] = jnp.concatenate([g["w_in_" + s][:hi - lo] for s, (lo, hi) in IN_ROWS.items()], axis=0)
    g4 = {n: g[n].reshape(N_SHARDS, -1, D_MODEL) for n in BIG}
    land = dict(zip(BIG, _scatter_grads([g4[n].astype(_MXU) for n in BIG])))
    part = {n: _partial_sum(lax.dynamic_index_in_dim(g4[n], me, 0, keepdims=False), land[n], name="partial_" + n)
            for n in BIG}
    sib = dict(zip(BIG, _swap_with_sibling([part[n] for n in BIG])))

    small_full_shape = {n: (SMALL_SHARDED[n] if n in SMALL_SHARDED else w[n].shape) for n in SMALL}
    red = _all_reduce_small(_pack([g[n] for n in SMALL] + [loss[0, :1]]))
    red_list = _unpack(red, [small_full_shape[n] for n in SMALL] + [(1,)])
    loss_total = red_list[-1][0]
    g_small = {}
    for n, arr in zip(SMALL, red_list[:-1]):
        if n in SMALL_SHARDED:
            cols = SMALL_SHARDED[n][1] // N_SHARDS
            arr = lax.dynamic_slice_in_dim(arr, me * cols, cols, axis=1)
        g_small[n] = arr.reshape(w[n].shape)

    grad, delta, new_m, new_v = {}, {}, {}, {}
    for n in BIG:
        outs = _adamw(big2d[n], [part[n], sib[n]], _rows_view(n, m[n]), _rows_view(n, v[n]), name="adamw_" + n)
        grad[n], delta[n], new_m[n], new_v[n] = (_param_view(n, o) for o in outs)
    shapes = [w[n].shape for n in SMALL]
    packs = [_pack([d[n] for n in SMALL]) for d in (w, m, v, g_small)]
    _, d_pack, m_pack, v_pack = _adamw(packs[0], [packs[3]], packs[1], packs[2], name="adamw_small")
    for n, dn, mn, vn in zip(SMALL, _unpack(d_pack, shapes), _unpack(m_pack, shapes), _unpack(v_pack, shapes)):
        grad[n], delta[n], new_m[n], new_v[n] = g_small[n], dn, mn, vn

    return (loss_total, grad_x[None], *[grad[n] for n in WEIGHT_NAMES], *[delta[n] for n in WEIGHT_NAMES],
            *[new_m[n] for n in WEIGHT_NAMES], *[new_v[n] for n in WEIGHT_NAMES])
```

```python
import functools
import math

import jax
import jax.numpy as jnp
from jax import lax
from jax.experimental import pallas as pl
from jax.experimental.pallas import tpu as pltpu

F32 = jnp.float32
_MXU = jnp.bfloat16

D_MODEL = 1024
SEQ = 2048
N_META = 16
CHUNK = 128
T_ROWS = 2176
N_CHUNKS = T_ROWS // CHUNK
PAD_ROWS = T_ROWS - SEQ - N_META
X_ROW0 = PAD_ROWS + N_META
SSD_HEADS = 16
SSD_HEAD_DIM = 64
SSD_STATE = 128
SSD_GROUPS = 2
SSD_HPG = SSD_HEADS // SSD_GROUPS
SSD_WIDTH = 1024
LRU_WIDTH = 1024
LRU_C = 8.0
D_FF = 2816
EPS = 1e-6
IN_COLS = 4624
N_SHARDS = 4
N_DEV = 8

ADAM_LR = 0.001
ADAM_B1 = 0.9
ADAM_B2 = 0.999
ADAM_EPS = 1e-08
ADAM_WD = 0.01
ADAM_STEP = 10

VMEM_LIMIT_BYTES = 56 * 1024 * 1024

NN = (((1,), (0,)), ((), ()))
NT = (((1,), (1,)), ((), ()))
TN = (((0,), (0,)), ((), ()))


def _cparams(*sem):
    return pltpu.CompilerParams(dimension_semantics=sem, vmem_limit_bytes=VMEM_LIMIT_BYTES)


def _dot(a, b, dims=NN):
    return lax.dot_general(a.astype(_MXU), b.astype(_MXU), dims, preferred_element_type=F32)


def _dot_exact(a, b, dims=NN):
    return lax.dot_general(a, b, dims, preferred_element_type=F32, precision=lax.Precision.HIGHEST)


def _sigmoid(x):
    return 1.0 / (1.0 + jnp.exp(-x))


def _softplus(x):
    return jnp.maximum(x, 0.0) + jnp.log(1.0 + jnp.exp(-jnp.abs(x)))


def _silu(x):
    return x * _sigmoid(x)


def _silu_grad(x):
    s = _sigmoid(x)
    return s * (1.0 + x * (1.0 - s))


_GELU_C = math.sqrt(2.0 / math.pi)


def _gelu_and_grad(x):
    inner = _GELU_C * (x + 0.044715 * x * x * x)
    t = jnp.tanh(inner)
    g = 0.5 * x * (1.0 + t)
    dg = 0.5 * (1.0 + t) + 0.5 * x * (1.0 - t * t) * _GELU_C * (1.0 + 3.0 * 0.044715 * x * x)
    return g, dg


def _rms_fwd(x, w):
    rstd = lax.rsqrt(jnp.mean(x * x, axis=-1, keepdims=True) + EPS)
    return x * rstd * w


def _rms_bwd(x, w, dy):
    rstd = lax.rsqrt(jnp.mean(x * x, axis=-1, keepdims=True) + EPS)
    xhat = x * rstd
    dxhat = dy * w
    dx = rstd * (dxhat - xhat * jnp.mean(dxhat * xhat, axis=-1, keepdims=True))
    return dx, dy * xhat


def _mm(terms, m, n, *, tm, tn, mode, out_dtype, name, residual=None, n_outer=False, also_mxu=False):
    gm, gn = m // tm, n // tn
    assert gm * tm == m and gn * tn == n
    if n_outer:
        grid = (gn, gm)
        mi = lambda g0, g1: g1
        ni = lambda g0, g1: g0
    else:
        grid = (gm, gn)
        mi = lambda g0, g1: g0
        ni = lambda g0, g1: g1
    in_specs, args = [], []
    for (a, ka, b, kb, k) in terms:
        if mode == "tn":
            in_specs.append(pl.BlockSpec((k, tm), lambda g0, g1, ka=ka: (ka, mi(g0, g1))))
        else:
            in_specs.append(pl.BlockSpec((tm, k), lambda g0, g1, ka=ka: (mi(g0, g1), ka)))
        if mode == "nt":
            in_specs.append(pl.BlockSpec((tn, k), lambda g0, g1, kb=kb: (ni(g0, g1), kb)))
        else:
            in_specs.append(pl.BlockSpec((k, tn), lambda g0, g1, kb=kb: (kb, ni(g0, g1))))
        args += [a, b]
    if residual is not None:
        in_specs.append(pl.BlockSpec((tm, tn), lambda g0, g1: (mi(g0, g1), ni(g0, g1))))
        args.append(residual)
    dims = {"nn": NN, "nt": NT, "tn": TN}[mode]
    n_terms = len(terms)
    has_res = residual is not None

    n_in = len(args)

    def body(*refs):
        acc = None
        for t in range(n_terms):
            d = lax.dot_general(refs[2 * t][...], refs[2 * t + 1][...], dims, preferred_element_type=F32)
            acc = d if acc is None else acc + d
        if has_res:
            acc = acc + refs[2 * n_terms][...]
        refs[n_in][...] = acc.astype(out_dtype)
        if also_mxu:
            refs[n_in + 1][...] = acc.astype(_MXU)

    tile = pl.BlockSpec((tm, tn), lambda g0, g1: (mi(g0, g1), ni(g0, g1)))
    shape = jax.ShapeDtypeStruct((m, n), out_dtype)
    return pl.pallas_call(
        body, name=name, grid=grid, in_specs=in_specs,
        out_specs=[tile, tile] if also_mxu else tile,
        out_shape=[shape, jax.ShapeDtypeStruct((m, n), _MXU)] if also_mxu else shape,
        compiler_params=_cparams("parallel", "parallel"),
    )(*args)


def _embed(x, meta):
    def body(x_ref, meta_ref, o_ref):
        i = pl.program_id(0)

        @pl.when(i == 0)
        def _():
            o_ref[0:PAD_ROWS, :] = jnp.zeros((PAD_ROWS, D_MODEL), F32)
            o_ref[PAD_ROWS:CHUNK, :] = meta_ref[...]

        @pl.when(i > 0)
        def _():
            o_ref[...] = x_ref[...]

    return pl.pallas_call(
        body, name="embed", grid=(N_CHUNKS,),
        in_specs=[pl.BlockSpec((CHUNK, D_MODEL), lambda i: (jnp.maximum(i - 1, 0), 0)),
                  pl.BlockSpec((N_META, D_MODEL), lambda i: (0, 0))],
        out_specs=pl.BlockSpec((CHUNK, D_MODEL), lambda i: (i, 0)),
        out_shape=jax.ShapeDtypeStruct((T_ROWS, D_MODEL), F32),
        compiler_params=_cparams("parallel"),
    )(x, meta)


def _rmsnorm(h, w, *, name, tm=544):
    def body(h_ref, w_ref, o_ref):
        o_ref[...] = _rms_fwd(h_ref[...], w_ref[...]).astype(_MXU)

    return pl.pallas_call(
        body, name=name, grid=(T_ROWS // tm,),
        in_specs=[pl.BlockSpec((tm, D_MODEL), lambda i: (i, 0)), pl.BlockSpec((1, D_MODEL), lambda i: (0, 0))],
        out_specs=pl.BlockSpec((tm, D_MODEL), lambda i: (i, 0)),
        out_shape=jax.ShapeDtypeStruct((T_ROWS, D_MODEL), _MXU),
        compiler_params=_cparams("parallel"),
    )(h, w)


def _loss_head(h2, target, fw):
    def body(h_ref, t_ref, w_ref, loss_ref, dh_ref, dhb_ref, dw_ref, acc_ref):
        i = pl.program_id(0)

        @pl.when(i == 0)
        def _():
            acc_ref[...] = jnp.zeros_like(acc_ref)
            dw_ref[...] = jnp.zeros_like(dw_ref)

        h = h_ref[...]
        w = w_ref[...]
        y = _rms_fwd(h, w)
        live = (i > 0).astype(F32)
        err = (y - t_ref[...]) * live
        acc_ref[...] += jnp.sum(err * err, axis=0, keepdims=True)
        dy = err * (1.0 / D_MODEL)
        dx, dwr = _rms_bwd(h, w, dy)
        dh_ref[...] = dx
        dhb_ref[...] = dx.astype(_MXU)
        dw_ref[...] += jnp.sum(dwr, axis=0, keepdims=True)

        @pl.when(i == N_CHUNKS - 1)
        def _():
            tot = jnp.sum(acc_ref[...], axis=1, keepdims=True) * (0.5 / D_MODEL)
            loss_ref[...] = jnp.broadcast_to(tot, (1, 128))

    return pl.pallas_call(
        body, name="loss_head", grid=(N_CHUNKS,),
        in_specs=[pl.BlockSpec((CHUNK, D_MODEL), lambda i: (i, 0)),
                  pl.BlockSpec((CHUNK, D_MODEL), lambda i: (jnp.maximum(i - 1, 0), 0)),
                  pl.BlockSpec((1, D_MODEL), lambda i: (0, 0))],
        out_specs=[pl.BlockSpec((1, 128), lambda i: (0, 0)),
                   pl.BlockSpec((CHUNK, D_MODEL), lambda i: (i, 0)),
                   pl.BlockSpec((CHUNK, D_MODEL), lambda i: (i, 0)),
                   pl.BlockSpec((1, D_MODEL), lambda i: (0, 0))],
        out_shape=[jax.ShapeDtypeStruct((1, 128), F32),
                   jax.ShapeDtypeStruct((T_ROWS, D_MODEL), F32),
                   jax.ShapeDtypeStruct((T_ROWS, D_MODEL), _MXU),
                   jax.ShapeDtypeStruct((1, D_MODEL), F32)],
        scratch_shapes=[pltpu.VMEM((1, D_MODEL), F32)],
        compiler_params=_cparams("arbitrary"),
    )(h2, target, fw)


def _mm_norm_bwd(terms, h, w, dres, *, name, tm=272):
    n_terms = len(terms)
    in_specs, args = [], []
    for (a, b, k) in terms:
        in_specs += [pl.BlockSpec((tm, k), lambda i: (i, 0)), pl.BlockSpec((k, D_MODEL), lambda i: (0, 0))]
        args += [a, b]
    in_specs += [pl.BlockSpec((tm, D_MODEL), lambda i: (i, 0)), pl.BlockSpec((1, D_MODEL), lambda i: (0, 0)),
                 pl.BlockSpec((tm, D_MODEL), lambda i: (i, 0))]
    args += [h, w, dres]

    def body(*refs):
        h_ref, w_ref, dres_ref, dh_ref, dhb_ref, dw_ref = refs[2 * n_terms:]

        @pl.when(pl.program_id(0) == 0)
        def _():
            dw_ref[...] = jnp.zeros_like(dw_ref)

        du = None
        for t in range(n_terms):
            d = lax.dot_general(refs[2 * t][...], refs[2 * t + 1][...], NN, preferred_element_type=F32)
            du = d if du is None else du + d
        dx, dwr = _rms_bwd(h_ref[...], w_ref[...], du)
        dh = dres_ref[...] + dx
        dh_ref[...] = dh
        dhb_ref[...] = dh.astype(_MXU)
        dw_ref[...] += jnp.sum(dwr, axis=0, keepdims=True)

    return pl.pallas_call(
        body, name=name, grid=(T_ROWS // tm,), in_specs=in_specs,
        out_specs=[pl.BlockSpec((tm, D_MODEL), lambda i: (i, 0)), pl.BlockSpec((tm, D_MODEL), lambda i: (i, 0)),
                   pl.BlockSpec((1, D_MODEL), lambda i: (0, 0))],
        out_shape=[jax.ShapeDtypeStruct((T_ROWS, D_MODEL), F32), jax.ShapeDtypeStruct((T_ROWS, D_MODEL), _MXU),
                   jax.ShapeDtypeStruct((1, D_MODEL), F32)],
        compiler_params=_cparams("arbitrary"),
    )(*args)


FFN_TM = 272
FFN_TN = 1408


def _ffn_up(u2, wg_t, wu_t):
    def body(u_ref, wg_ref, wu_ref, gp_ref, up_ref, act_ref):
        u = u_ref[...]
        gp = lax.dot_general(u, wg_ref[...], NT, preferred_element_type=F32)
        up = lax.dot_general(u, wu_ref[...], NT, preferred_element_type=F32)
        gp_ref[...] = gp
        up_ref[...] = up
        act_ref[...] = (_silu(gp) * up).astype(_MXU)

    tile = pl.BlockSpec((FFN_TM, FFN_TN), lambda j, i: (i, j))
    return pl.pallas_call(
        body, name="ffn_up", grid=(D_FF // FFN_TN, T_ROWS // FFN_TM),
        in_specs=[pl.BlockSpec((FFN_TM, D_MODEL), lambda j, i: (i, 0)),
                  pl.BlockSpec((FFN_TN, D_MODEL), lambda j, i: (j, 0)),
                  pl.BlockSpec((FFN_TN, D_MODEL), lambda j, i: (j, 0))],
        out_specs=[tile, tile, tile],
        out_shape=[jax.ShapeDtypeStruct((T_ROWS, D_FF), F32), jax.ShapeDtypeStruct((T_ROWS, D_FF), F32),
                   jax.ShapeDtypeStruct((T_ROWS, D_FF), _MXU)],
        compiler_params=_cparams("parallel", "parallel"),
    )(u2, wg_t, wu_t)


def _ffn_bwd_act(dh2b, wd, gp, up):
    def body(dh_ref, wd_ref, gp_ref, up_ref, dgp_ref, dup_ref):
        dact = lax.dot_general(dh_ref[...], wd_ref[...], NT, preferred_element_type=F32)
        gp = gp_ref[...]
        dgp_ref[...] = (dact * up_ref[...] * _silu_grad(gp)).astype(_MXU)
        dup_ref[...] = (dact * _silu(gp)).astype(_MXU)

    tile = pl.BlockSpec((FFN_TM, FFN_TN), lambda j, i: (i, j))
    return pl.pallas_call(
        body, name="ffn_bwd_act", grid=(D_FF // FFN_TN, T_ROWS // FFN_TM),
        in_specs=[pl.BlockSpec((FFN_TM, D_MODEL), lambda j, i: (i, 0)),
                  pl.BlockSpec((FFN_TN, D_MODEL), lambda j, i: (j, 0)), tile, tile],
        out_specs=[tile, tile],
        out_shape=[jax.ShapeDtypeStruct((T_ROWS, D_FF), _MXU), jax.ShapeDtypeStruct((T_ROWS, D_FF), _MXU)],
        compiler_params=_cparams("parallel", "parallel"),
    )(dh2b, wd, gp, up)


CONV_TC = 512
CONV_K = 4


def _conv_pre(x_ref, wv, bv, c):
    tc = wv.shape[1]
    r0 = c * CHUNK
    cur = x_ref[r0:r0 + CHUNK, :]
    prev8 = jnp.zeros((8, tc), F32) if c == 0 else x_ref[r0 - 8:r0, :]
    cat = jnp.concatenate([prev8, cur], axis=0)
    shifted = [cur] + [pltpu.roll(cat, s, 0)[8:8 + CHUNK] for s in range(1, CONV_K)]
    pre = bv
    for s in range(CONV_K):
        pre = pre + shifted[s] * wv[CONV_K - 1 - s:CONV_K - s]
    return pre, shifted


def _row_mask(c):
    if c > 0:
        return None
    return (lax.broadcasted_iota(jnp.int32, (CHUNK, 1), 0) >= PAD_ROWS).astype(F32)


def _conv_fwd(x, w, b, *, silu, name):
    cols = x.shape[1]
    tc = min(CONV_TC, cols)

    def body(x_ref, w_ref, b_ref, o_ref):
        wv, bv = w_ref[...], b_ref[...]
        for c in range(N_CHUNKS):
            pre, _ = _conv_pre(x_ref, wv, bv, c)
            y = _silu(pre) if silu else pre
            mask = _row_mask(c)
            if mask is not None:
                y = y * mask
            o_ref[c * CHUNK:(c + 1) * CHUNK, :] = y

    return pl.pallas_call(
        body, name=name, grid=(cols // tc,),
        in_specs=[pl.BlockSpec((T_ROWS, tc), lambda j: (0, j)), pl.BlockSpec((CONV_K, tc), lambda j: (0, j)),
                  pl.BlockSpec((1, tc), lambda j: (0, j))],
        out_specs=pl.BlockSpec((T_ROWS, tc), lambda j: (0, j)),
        out_shape=jax.ShapeDtypeStruct((T_ROWS, cols), F32),
        compiler_params=_cparams("parallel"),
    )(x, w, b)


def _conv_bwd(dy, x, w, b, *, silu, name):
    cols = x.shape[1]
    tc = min(CONV_TC, cols)

    def body(dy_ref, x_ref, w_ref, b_ref, dx_ref, dw_ref, db_ref):
        wv, bv = w_ref[...], b_ref[...]
        next8 = jnp.zeros((8, tc), F32)
        dws = [jnp.zeros((1, tc), F32) for _ in range(CONV_K)]
        db = jnp.zeros((1, tc), F32)
        for c in reversed(range(N_CHUNKS)):
            pre, shifted = _conv_pre(x_ref, wv, bv, c)
            dpre = dy_ref[c * CHUNK:(c + 1) * CHUNK, :]
            if silu:
                dpre = dpre * _silu_grad(pre)
            mask = _row_mask(c)
            if mask is not None:
                dpre = dpre * mask
            cat = jnp.concatenate([dpre, next8], axis=0)
            dx = dpre * wv[CONV_K - 1:CONV_K]
            for s in range(1, CONV_K):
                dx = dx + pltpu.roll(cat, CHUNK + 8 - s, 0)[0:CHUNK] * wv[CONV_K - 1 - s:CONV_K - s]
            dx_ref[c * CHUNK:(c + 1) * CHUNK, :] = dx.astype(_MXU)
            for s in range(CONV_K):
                k = CONV_K - 1 - s
                dws[k] = dws[k] + jnp.sum(dpre * shifted[s], axis=0, keepdims=True)
            db = db + jnp.sum(dpre, axis=0, keepdims=True)
            next8 = dpre[0:8]
        dw_ref[...] = jnp.concatenate(dws, axis=0)
        db_ref[...] = db

    return pl.pallas_call(
        body, name=name, grid=(cols // tc,),
        in_specs=[pl.BlockSpec((T_ROWS, tc), lambda j: (0, j)), pl.BlockSpec((T_ROWS, tc), lambda j: (0, j)),
                  pl.BlockSpec((CONV_K, tc), lambda j: (0, j)), pl.BlockSpec((1, tc), lambda j: (0, j))],
        out_specs=[pl.BlockSpec((T_ROWS, tc), lambda j: (0, j)), pl.BlockSpec((CONV_K, tc), lambda j: (0, j)),
                   pl.BlockSpec((1, tc), lambda j: (0, j))],
        out_shape=[jax.ShapeDtypeStruct((T_ROWS, cols), _MXU), jax.ShapeDtypeStruct((CONV_K, cols), F32),
                   jax.ShapeDtypeStruct((1, cols), F32)],
        compiler_params=_cparams("parallel"),
    )(dy, x, w, b)


def _ssd_chunk_common(dt_raw, prm, c):
    a_row = -jnp.exp(prm[1:2])
    dt = _softplus(dt_raw + prm[0:1])
    rows = lax.broadcasted_iota(jnp.int32, (CHUNK, 1), 0)
    real = jnp.logical_or(c > 0, rows >= PAD_ROWS)
    dt = jnp.where(real, dt, 0.0)
    li = lax.broadcasted_iota(jnp.int32, (CHUNK, CHUNK), 0)
    si = lax.broadcasted_iota(jnp.int32, (CHUNK, CHUNK), 1)
    causal = li >= si
    tri = causal.astype(F32)
    cs = _dot_exact(tri, dt * a_row)
    return dt, a_row, cs, cs.T, causal, tri, real


def _gated_norm_fwd(y, z, w):
    g = y * _silu(z)
    half = SSD_WIDTH // SSD_GROUPS
    outs = [_rms_fwd(g[:, k * half:(k + 1) * half], w[:, k * half:(k + 1) * half]) for k in range(SSD_GROUPS)]
    return jnp.concatenate(outs, axis=1)


GROUP_W = SSD_WIDTH // SSD_GROUPS
PAIR_W = 2 * SSD_HEAD_DIM
STATE_SHAPE = (SSD_GROUPS, SSD_STATE, GROUP_W)


def _head_expander():
    r = lax.broadcasted_iota(jnp.int32, (128, SSD_WIDTH), 0)
    c = lax.broadcasted_iota(jnp.int32, (128, SSD_WIDTH), 1)
    return (c // SSD_HEAD_DIM == r).astype(F32)


def _ssd_expand(dt, cs, prm, ex):
    cs_x = _dot_exact(cs, ex)
    cs_last_x = cs_x[CHUNK - 1:CHUNK, :]
    return (_dot_exact(dt, ex), _dot_exact(prm, ex)[2:3], jnp.exp(cs_x), jnp.exp(cs_last_x),
            jnp.exp(cs_last_x - cs_x))


def _ssd_fwd(xs, bc, dt_raw, z, prm, norm_w, ex):
    def body(xs_ref, bc_ref, dt_ref, z_ref, prm_ref, nw_ref, ex_ref, y_ref, yn_ref, prev_ref, state):
        c = pl.program_id(0)

        @pl.when(c == 0)
        def _():
            state[...] = jnp.zeros_like(state)

        prm = prm_ref[...]
        dt, a_row, cs, cs_t, causal, _, _ = _ssd_chunk_common(dt_ref[...], prm, c)
        dt_x, d_x, e_cs_x, e_last_x, dec_x = _ssd_expand(dt, cs, prm, ex_ref[...])
        xs_all = xs_ref[...]
        bc_all = bc_ref[...]
        xdt = xs_all * dt_x
        xdec = xdt * dec_x
        lane_lo = lax.broadcasted_iota(jnp.int32, (1, PAIR_W), 1) < SSD_HEAD_DIM
        for g in range(SSD_GROUPS):
            gs = slice(g * GROUP_W, (g + 1) * GROUP_W)
            b_g = bc_all[:, g * SSD_STATE:(g + 1) * SSD_STATE]
            c_g = bc_all[:, (SSD_GROUPS + g) * SSD_STATE:(SSD_GROUPS + g + 1) * SSD_STATE]
            st = state[g]
            prev_ref[0, g] = st
            y_off = _dot(c_g, st) * e_cs_x[:, gs]
            state[g] = st * e_last_x[:, gs] + _dot(b_g.T, xdec[:, gs])
            cb = _dot(c_g, b_g, NT)
            for k in range(SSD_HPG // 2):
                h0 = g * SSD_HPG + 2 * k
                ps = slice(h0 * SSD_HEAD_DIM, h0 * SSD_HEAD_DIM + PAIR_W)
                xdt_pair = xdt[:, ps]
                yd = []
                for h in (h0, h0 + 1):
                    lmat = jnp.where(causal, jnp.exp(cs[:, h:h + 1] - cs_t[h:h + 1, :]), 0.0)
                    yd.append(_dot(cb * lmat, xdt_pair))
                y_ref[:, ps] = (jnp.where(lane_lo, yd[0], yd[1]) + y_off[:, k * PAIR_W:(k + 1) * PAIR_W]
                                + xs_all[:, ps] * d_x[:, ps])
        yn_ref[...] = _gated_norm_fwd(y_ref[...], z_ref[...], nw_ref[...]).astype(_MXU)

    row = lambda w: pl.BlockSpec((CHUNK, w), lambda c: (c, 0))
    return pl.pallas_call(
        body, name="ssd_fwd", grid=(N_CHUNKS,),
        in_specs=[row(SSD_WIDTH), row(512), row(128), row(SSD_WIDTH),
                  pl.BlockSpec((8, 128), lambda c: (0, 0)), pl.BlockSpec((1, SSD_WIDTH), lambda c: (0, 0)),
                  pl.BlockSpec((128, SSD_WIDTH), lambda c: (0, 0))],
        out_specs=[row(SSD_WIDTH), row(SSD_WIDTH),
                   pl.BlockSpec((1,) + STATE_SHAPE, lambda c: (c, 0, 0, 0))],
        out_shape=[jax.ShapeDtypeStruct((T_ROWS, SSD_WIDTH), F32), jax.ShapeDtypeStruct((T_ROWS, SSD_WIDTH), _MXU),
                   jax.ShapeDtypeStruct((N_CHUNKS,) + STATE_SHAPE, F32)],
        scratch_shapes=[pltpu.VMEM(STATE_SHAPE, F32)],
        compiler_params=_cparams("arbitrary"),
    )(xs, bc, dt_raw, z, prm, norm_w, ex)


def _ssd_bwd(dyn, dyn_block, z, y_pre, xs, bc, dt_raw, prev, prm, norm_w, ex):
    def body(dyn_ref, z_ref, y_ref, xs_ref, bc_ref, dt_ref, prev_ref, prm_ref, nw_ref, ex_ref,
             dz_ref, dxs_ref, dbc_ref, ddt_ref, dprm_ref, dnw_ref, dstate):
        step = pl.program_id(0)
        c = N_CHUNKS - 1 - step

        @pl.when(step == 0)
        def _():
            dstate[...] = jnp.zeros_like(dstate)
            dprm_ref[...] = jnp.zeros_like(dprm_ref)
            dnw_ref[...] = jnp.zeros_like(dnw_ref)

        prm = prm_ref[...]
        dt, a_row, cs, cs_t, causal, tri, real = _ssd_chunk_common(dt_ref[...], prm, c)
        realf = real.astype(F32)
        z = z_ref[...]
        y_all = y_ref[...]
        nw = nw_ref[...]
        dyn_all = dyn_ref[...]
        sz = _silu(z)
        gated = y_all * sz
        half = SSD_WIDTH // SSD_GROUPS
        dgs, dnws = [], []
        for k in range(SSD_GROUPS):
            sl = slice(k * half, (k + 1) * half)
            dgk, dwk = _rms_bwd(gated[:, sl], nw[:, sl], dyn_all[:, sl])
            dgs.append(dgk)
            dnws.append(jnp.sum(dwk, axis=0, keepdims=True))
        dgated = jnp.concatenate(dgs, axis=1)
        dnw_ref[...] += jnp.concatenate(dnws, axis=1)
        dz_ref[...] = (dgated * y_all * _silu_grad(z)).astype(_MXU)
        dy_all = dgated * sz

        ex = ex_ref[...]
        dt_x, d_x, e_cs_x, e_last_x, dec_x = _ssd_expand(dt, cs, prm, ex)
        xs_all = xs_ref[...]
        bc_all = bc_ref[...]
        xdt = xs_all * dt_x
        xdt_mxu = xdt.astype(_MXU).astype(F32)
        xdec = xdt * dec_x
        dcp = dy_all * e_cs_x
        lane_lo = lax.broadcasted_iota(jnp.int32, (1, PAIR_W), 1) < SSD_HEAD_DIM
        upper = (lax.broadcasted_iota(jnp.int32, (CHUNK, CHUNK), 0)
                 <= lax.broadcasted_iota(jnp.int32, (CHUNK, CHUNK), 1))
        last_row = (lax.broadcasted_iota(jnp.int32, (CHUNK, 1), 0) == CHUNK - 1).astype(F32)
        dbs, dcs_, dxdt_parts, last_parts = [], [], [], []
        for g in range(SSD_GROUPS):
            gs = slice(g * GROUP_W, (g + 1) * GROUP_W)
            b_g = bc_all[:, g * SSD_STATE:(g + 1) * SSD_STATE]
            c_g = bc_all[:, (SSD_GROUPS + g) * SSD_STATE:(SSD_GROUPS + g + 1) * SSD_STATE]
            prev_t = prev_ref[0, g]
            dst = dstate[g]
            dc_g = _dot(dcp[:, gs], prev_t, NT)
            db_g = _dot(xdec[:, gs], dst, NT)
            dxdt_state = _dot(b_g, dst) * dec_x[:, gs]
            dstate[g] = dst * e_last_x[:, gs] + _dot(c_g.T, dcp[:, gs])
            last_parts.append(jnp.sum(xdt_mxu[:, gs] * dxdt_state, axis=0, keepdims=True)
                              + jnp.sum(dst * prev_t, axis=0, keepdims=True) * e_last_x[:, gs])
            cb_t = _dot(b_g, c_g, NT)
            dcb_t = jnp.zeros((CHUNK, CHUNK), F32)
            for k in range(SSD_HPG // 2):
                h0 = g * SSD_HPG + 2 * k
                ps = slice(h0 * SSD_HEAD_DIM, h0 * SSD_HEAD_DIM + PAIR_W)
                dy_pair = dy_all[:, ps]
                xdt_pair = xdt[:, ps]
                dd = []
                for h in (h0, h0 + 1):
                    lmat_t = jnp.where(upper, jnp.exp(cs_t[h:h + 1, :] - cs[:, h:h + 1]), 0.0)
                    dd.append(_dot(cb_t * lmat_t, dy_pair))
                    mine = lane_lo if h == h0 else jnp.logical_not(lane_lo)
                    dcb_t = dcb_t + _dot(jnp.where(mine, xdt_pair, 0.0), dy_pair, NT) * lmat_t
                dxdt_parts.append(jnp.where(lane_lo, dd[0], dd[1]) + dxdt_state[:, k * PAIR_W:(k + 1) * PAIR_W])
            dc_g = dc_g + _dot(dcb_t, b_g, TN)
            db_g = db_g + _dot(dcb_t, c_g)
            dbs.append(db_g * realf)
            dcs_.append(dc_g * realf)
        dbc_ref[...] = jnp.concatenate(dbs + dcs_, axis=1)
        dxdt = jnp.concatenate(dxdt_parts, axis=1)
        dxs_ref[...] = (dxdt * dt_x + dy_all * d_x) * realf
        ddt_all = _dot_exact(dxdt * xs_all, ex, NT)
        rows = jnp.concatenate([jnp.concatenate(last_parts, axis=1), jnp.sum(dy_all * xs_all, axis=0, keepdims=True),
                                jnp.zeros((6, SSD_WIDTH), F32)], axis=0)
        rows = _dot_exact(rows, ex, NT)
        dd_row = rows[1:2]
        dy_mxu = dy_all.astype(_MXU).astype(F32)
        dcs_all = (_dot_exact(dy_mxu * (y_all - xs_all * d_x), ex, NT) - _dot_exact(xdt_mxu * dxdt, ex, NT)
                   + last_row * rows[0:1])
        dda = _dot_exact(tri, dcs_all, TN)
        ddt = (ddt_all + dda * a_row) * realf
        ddt_raw = ddt * _sigmoid(dt_ref[...] + prm[0:1])
        ddt_ref[...] = ddt_raw.astype(_MXU)
        da_log = jnp.sum(dda * dt, axis=0, keepdims=True) * a_row
        dprm_ref[0:1, :] += jnp.sum(ddt_raw, axis=0, keepdims=True)
        dprm_ref[1:2, :] += da_log
        dprm_ref[2:3, :] += dd_row

    rev = lambda w, blk=0: pl.BlockSpec((CHUNK, w), lambda s, blk=blk: (N_CHUNKS - 1 - s, blk))
    return pl.pallas_call(
        body, name="ssd_bwd", grid=(N_CHUNKS,),
        in_specs=[rev(SSD_WIDTH, dyn_block), rev(SSD_WIDTH), rev(SSD_WIDTH), rev(SSD_WIDTH), rev(512), rev(128),
                  pl.BlockSpec((1,) + STATE_SHAPE, lambda s: (N_CHUNKS - 1 - s, 0, 0, 0)),
                  pl.BlockSpec((8, 128), lambda s: (0, 0)), pl.BlockSpec((1, SSD_WIDTH), lambda s: (0, 0)),
                  pl.BlockSpec((128, SSD_WIDTH), lambda s: (0, 0))],
        out_specs=[rev(SSD_WIDTH), rev(SSD_WIDTH), rev(512), rev(128),
                   pl.BlockSpec((8, 128), lambda s: (0, 0)), pl.BlockSpec((1, SSD_WIDTH), lambda s: (0, 0))],
        out_shape=[jax.ShapeDtypeStruct((T_ROWS, SSD_WIDTH), _MXU), jax.ShapeDtypeStruct((T_ROWS, SSD_WIDTH), F32),
                   jax.ShapeDtypeStruct((T_ROWS, 512), F32), jax.ShapeDtypeStruct((T_ROWS, 128), _MXU),
                   jax.ShapeDtypeStruct((8, 128), F32), jax.ShapeDtypeStruct((1, SSD_WIDTH), F32)],
        scratch_shapes=[pltpu.VMEM(STATE_SHAPE, F32)],
        compiler_params=_cparams("arbitrary"),
    )(dyn, z, y_pre, xs, bc, dt_raw, prev, prm, norm_w, ex)


LRU_PAIRS = 8


def _lru_gates(xr, wa_ref, wx_ref, prm):
    pre_r, pre_i = [], []
    for k in range(LRU_PAIRS):
        xk = xr[:, k * 128:(k + 1) * 128]
        pre_r.append(_dot(xk, wa_ref[k]))
        pre_i.append(_dot(xk, wx_ref[k]))
    r = _sigmoid(jnp.concatenate(pre_r, axis=1) + prm[0:1])
    i = _sigmoid(jnp.concatenate(pre_i, axis=1) + prm[1:2])
    sp = _softplus(-prm[2:3])
    log_a = (-LRU_C) * r * sp
    a = jnp.exp(log_a)
    s = jnp.sqrt(-jnp.tanh(log_a) * (a * a + 1.0))
    return r, i, a, s, sp


def _lru_fwd(xr, gate, wa, wx, prm):
    def body(xr_ref, g_ref, wa_ref, wx_ref, prm_ref, hs_ref, yn_ref, carry, a_s, u_s):
        @pl.when(pl.program_id(0) == 0)
        def _():
            carry[...] = jnp.zeros_like(carry)

        prm = prm_ref[...]
        xr_t = xr_ref[...]
        _, i, a, s, _ = _lru_gates(xr_t, wa_ref, wx_ref, prm)
        a_s[...] = a
        u_s[...] = s * (i * xr_t)
        rid = lax.broadcasted_iota(jnp.int32, (8, LRU_WIDTH), 0)

        def group(k, h):
            off = pl.multiple_of(k * 8, 8)
            a8 = a_s[pl.ds(off, 8), :]
            u8 = u_s[pl.ds(off, 8), :]
            out = jnp.zeros((8, LRU_WIDTH), F32)
            for r_ in range(8):
                h = a8[r_:r_ + 1] * h + u8[r_:r_ + 1]
                out = jnp.where(rid == r_, h, out)
            hs_ref[pl.ds(off, 8), :] = out
            return h

        carry[0:1, :] = lax.fori_loop(0, CHUNK // 8, group, carry[0:1, :])
        gel, _ = _gelu_and_grad(g_ref[...])
        yn_ref[...] = _rms_fwd(gel * hs_ref[...], prm[3:4]).astype(_MXU)

    row = pl.BlockSpec((CHUNK, LRU_WIDTH), lambda t: (t, 0))
    wspec = pl.BlockSpec((LRU_PAIRS, 128, 128), lambda t: (0, 0, 0))
    return pl.pallas_call(
        body, name="lru_fwd", grid=(N_CHUNKS,),
        in_specs=[row, row, wspec, wspec, pl.BlockSpec((8, LRU_WIDTH), lambda t: (0, 0))],
        out_specs=[row, row],
        out_shape=[jax.ShapeDtypeStruct((T_ROWS, LRU_WIDTH), F32), jax.ShapeDtypeStruct((T_ROWS, LRU_WIDTH), _MXU)],
        scratch_shapes=[pltpu.VMEM((8, LRU_WIDTH), F32), pltpu.VMEM((CHUNK, LRU_WIDTH), F32),
                        pltpu.VMEM((CHUNK, LRU_WIDTH), F32)],
        compiler_params=_cparams("arbitrary"),
    )(xr, gate, wa, wx, prm)


def _lru_bwd(dyn, dyn_block, gate, xr, hs, wa, wx, wa_t, wx_t, prm):
    def body(dyn_ref, g_ref, xr_ref, hs_ref, hsp_ref, wa_ref, wx_ref, wat_ref, wxt_ref, prm_ref,
             dg_ref, dxr_ref, dwa_ref, dwx_ref, dprm_ref, carry, a_s, d_s):
        step = pl.program_id(0)
        tile = N_CHUNKS - 1 - step

        @pl.when(step == 0)
        def _():
            carry[...] = jnp.zeros_like(carry)
            dwa_ref[...] = jnp.zeros_like(dwa_ref)
            dwx_ref[...] = jnp.zeros_like(dwx_ref)
            dprm_ref[...] = jnp.zeros_like(dprm_ref)

        prm = prm_ref[...]
        xr_t = xr_ref[...]
        r, i, a, s, sp = _lru_gates(xr_t, wa_ref, wx_ref, prm)
        hs_t = hs_ref[...]
        gel, dgel = _gelu_and_grad(g_ref[...])
        dy, dnw = _rms_bwd(gel * hs_t, prm[3:4], dyn_ref[...])
        dg_ref[...] = (dy * hs_t * dgel).astype(_MXU)
        a_s[...] = a
        d_s[...] = dy * gel
        rid = lax.broadcasted_iota(jnp.int32, (8, LRU_WIDTH), 0)

        def group(k, cr):
            off = pl.multiple_of((CHUNK // 8 - 1 - k) * 8, 8)
            a8 = a_s[pl.ds(off, 8), :]
            d8 = d_s[pl.ds(off, 8), :]
            out = jnp.zeros((8, LRU_WIDTH), F32)
            for r_ in reversed(range(8)):
                dht = d8[r_:r_ + 1] + cr
                out = jnp.where(rid == r_, dht, out)
                cr = a8[r_:r_ + 1] * dht
            d_s[pl.ds(off, 8), :] = out
            return cr

        carry[0:1, :] = lax.fori_loop(0, CHUNK // 8, group, carry[0:1, :])
        dht = d_s[...]
        before = hsp_ref[CHUNK - 8:CHUNK, :][7:8] * (tile > 0).astype(F32)
        first = lax.broadcasted_iota(jnp.int32, (CHUNK, 1), 0) == 0
        hprev = jnp.where(first, before, pltpu.roll(hs_t, 1, 0))
        da = dht * hprev
        ixr = i * xr_t
        ds = dht * ixr
        dlog_a = da * a - ds * (a * a) / s
        dr = dlog_a * ((-LRU_C) * sp)
        dsp = jnp.sum(dlog_a * ((-LRU_C) * r), axis=0, keepdims=True)
        dlam = dsp * (-_sigmoid(-prm[2:3]))
        di = dht * s * xr_t
        dpre_r = dr * r * (1.0 - r)
        dpre_i = di * i * (1.0 - i)
        dxr = dht * s * i
        parts = []
        for k in range(LRU_PAIRS):
            sl = slice(k * 128, (k + 1) * 128)
            parts.append(_dot(dpre_r[:, sl], wat_ref[k]) + _dot(dpre_i[:, sl], wxt_ref[k]))
            dwa_ref[k] += _dot(xr_t[:, sl], dpre_r[:, sl], TN)
            dwx_ref[k] += _dot(xr_t[:, sl], dpre_i[:, sl], TN)
        dxr_ref[...] = dxr + jnp.concatenate(parts, axis=1)
        dprm_ref[0:1, :] += jnp.sum(dpre_r, axis=0, keepdims=True)
        dprm_ref[1:2, :] += jnp.sum(dpre_i, axis=0, keepdims=True)
        dprm_ref[2:3, :] += dlam
        dprm_ref[3:4, :] += jnp.sum(dnw, axis=0, keepdims=True)

    rev = lambda blk=0: pl.BlockSpec((CHUNK, LRU_WIDTH), lambda s, blk=blk: (N_CHUNKS - 1 - s, blk))
    wspec = pl.BlockSpec((LRU_PAIRS, 128, 128), lambda s: (0, 0, 0))
    return pl.pallas_call(
        body, name="lru_bwd", grid=(N_CHUNKS,),
        in_specs=[rev(dyn_block), rev(), rev(), rev(),
                  pl.BlockSpec((CHUNK, LRU_WIDTH), lambda s: (jnp.maximum(N_CHUNKS - 2 - s, 0), 0)),
                  wspec, wspec, wspec, wspec, pl.BlockSpec((8, LRU_WIDTH), lambda s: (0, 0))],
        out_specs=[rev(), rev(), wspec, wspec, pl.BlockSpec((8, LRU_WIDTH), lambda s: (0, 0))],
        out_shape=[jax.ShapeDtypeStruct((T_ROWS, LRU_WIDTH), _MXU), jax.ShapeDtypeStruct((T_ROWS, LRU_WIDTH), F32),
                   jax.ShapeDtypeStruct((LRU_PAIRS, 128, 128), F32), jax.ShapeDtypeStruct((LRU_PAIRS, 128, 128), F32),
                   jax.ShapeDtypeStruct((8, LRU_WIDTH), F32)],
        scratch_shapes=[pltpu.VMEM((8, LRU_WIDTH), F32), pltpu.VMEM((CHUNK, LRU_WIDTH), F32),
                        pltpu.VMEM((CHUNK, LRU_WIDTH), F32)],
        compiler_params=_cparams("arbitrary"),
    )(dyn, gate, xr, hs, hs, wa, wx, wa_t, wx_t, prm)


SEC_NAMES = ("z", "xs", "bc", "dt", "g", "x")
SEC_WIDTH = {"z": 1024, "xs": 1024, "bc": 512, "dt": 128, "g": 1024, "x": 1024}


def _pair_blocks(w):
    w = w.reshape(LRU_PAIRS, 2, 64, 64)
    zero = jnp.zeros((LRU_PAIRS, 64, 64), w.dtype)
    top = jnp.concatenate([w[:, 0], zero], axis=2)
    bot = jnp.concatenate([zero, w[:, 1]], axis=2)
    return jnp.concatenate([top, bot], axis=1)


def _unpair_blocks(wp):
    return jnp.stack([wp[:, :64, :64], wp[:, 64:, 64:]], axis=1).reshape(16, 64, 64)


def _pad_lanes(v, width=128):
    return jnp.pad(v, ((0, 0), (0, width - v.shape[1])))


class _Resident:
    def __init__(self, w_out, w_gate, w_up, w_down):
        self._w_out, self._ffn = w_out, (w_gate, w_up, w_down)

    def w_out(self, after):
        return self._w_out

    def ffn(self, after):
        return self._ffn

    def grads_ready(self, names, g, g_mxu):
        return jnp.zeros((1, 1), F32)


def _local_step(x, target, meta, p, late):
    g, g_mxu = {}, {}
    ex = _head_expander()
    h0 = _embed(x, meta)
    u1 = _rmsnorm(h0, p["norm1_w"], name="norm1")
    proj = {}
    for s in SEC_NAMES:
        wdt = SEC_WIDTH[s]
        proj[s] = _mm([(u1, 0, p["w_in_" + s], 0, D_MODEL)], T_ROWS, wdt, tm=544, tn=min(wdt, 512), mode="nt",
                      out_dtype=F32, name="proj_" + s)
    ssd_prm = jnp.concatenate([_pad_lanes(p["ssd_dt_bias"]), _pad_lanes(p["ssd_a_log"]), _pad_lanes(p["ssd_d"]),
                               jnp.zeros((5, 128), F32)], axis=0)
    xs_act = _conv_fwd(proj["xs"], p["ssd_conv_w"][:, :SSD_WIDTH], p["ssd_conv_b"][:, :SSD_WIDTH], silu=True,
                       name="ssd_conv_xs")
    bc_act = _conv_fwd(proj["bc"], p["ssd_conv_w"][:, SSD_WIDTH:], p["ssd_conv_b"][:, SSD_WIDTH:], silu=True,
                       name="ssd_conv_bc")
    y_pre, y_ssd, prev = _ssd_fwd(xs_act, bc_act, proj["dt"], proj["z"], ssd_prm, p["ssd_norm_w"], ex)
    xr = _conv_fwd(proj["x"], p["lru_conv_w"], p["lru_conv_b"], silu=False, name="lru_conv")
    wa_p, wx_p = _pair_blocks(p["lru_wa"]), _pair_blocks(p["lru_wx"])
    lru_prm = jnp.concatenate([p["lru_ba"], p["lru_bx"], p["lru_lambda"], p["lru_norm_w"],
                               jnp.zeros((4, LRU_WIDTH), F32)], axis=0)
    hs, y_lru = _lru_fwd(xr, proj["g"], wa_p.astype(_MXU), wx_p.astype(_MXU), lru_prm)
    ycat = jnp.concatenate([y_ssd, y_lru], axis=1)
    w_out = late.w_out(ycat)
    h1 = _mm([(ycat, 0, w_out, 0, 2 * D_MODEL)], T_ROWS, D_MODEL, tm=544, tn=512, mode="nn", out_dtype=F32,
             name="out_proj", residual=h0)
    u2 = _rmsnorm(h1, p["norm2_w"], name="norm2")
    w_gate, w_up, w_down = late.ffn(u2)
    gp, up, act = _ffn_up(u2, w_gate, w_up)
    h2 = _mm([(act, 0, w_down, 0, D_FF)], T_ROWS, D_MODEL, tm=544, tn=512, mode="nn", out_dtype=F32,
             name="ffn_down", residual=h1)
    loss, dh2, dh2b, g["final_norm_w"] = _loss_head(h2, target, p["final_norm_w"])
    dgp, dup = _ffn_bwd_act(dh2b, w_down, gp, up)
    g["w_down"], g_mxu["w_down"] = _mm([(act, 0, dh2b, 0, T_ROWS)], D_FF, D_MODEL, tm=1408, tn=512, mode="tn",
                                       out_dtype=F32, name="dw_down", also_mxu=True)
    dh1, dh1b, g["norm2_w"] = _mm_norm_bwd([(dgp, w_gate, D_FF), (dup, w_up, D_FF)], h1, p["norm2_w"], dh2,
                                           name="ffn_bwd_in")
    g["w_gate"], g_mxu["w_gate"] = _mm([(dgp, 0, u2, 0, T_ROWS)], D_FF, D_MODEL, tm=1408, tn=512, mode="tn",
                                       out_dtype=F32, name="dw_gate", also_mxu=True)
    g["w_up"], g_mxu["w_up"] = _mm([(dup, 0, u2, 0, T_ROWS)], D_FF, D_MODEL, tm=1408, tn=512, mode="tn",
                                   out_dtype=F32, name="dw_up", also_mxu=True)
    lru_prm = lru_prm + late.grads_ready(("w_down", "w_gate", "w_up"), g, g_mxu)
    dycat = _mm([(dh1b, 0, w_out, 0, D_MODEL)], T_ROWS, 2 * D_MODEL, tm=544, tn=512, mode="nt", out_dtype=F32,
                name="out_proj_bwd")
    g["w_out"], g_mxu["w_out"] = _mm([(ycat, 0, dh1b, 0, T_ROWS)], 2 * D_MODEL, D_MODEL, tm=512, tn=512, mode="tn",
                                     out_dtype=F32, name="dw_out", also_mxu=True)
    ssd_prm = ssd_prm + late.grads_ready(("w_out",), g, g_mxu)
    dgate, dxr, dwa_p, dwx_p, dlru_prm = _lru_bwd(dycat, 1, proj["g"], xr, hs, wa_p.astype(_MXU), wx_p.astype(_MXU),
                                                  jnp.swapaxes(wa_p, 1, 2).astype(_MXU),
                                                  jnp.swapaxes(wx_p, 1, 2).astype(_MXU), lru_prm)
    g["lru_wa"], g["lru_wx"] = _unpair_blocks(dwa_p), _unpair_blocks(dwx_p)
    g["lru_ba"], g["lru_bx"], g["lru_lambda"], g["lru_norm_w"] = (dlru_prm[k:k + 1] for k in range(4))
    dx_lru, g["lru_conv_w"], g["lru_conv_b"] = _conv_bwd(dxr, proj["x"], p["lru_conv_w"], p["lru_conv_b"], silu=False,
                                                         name="lru_conv_bwd")
    dz, dxs_act, dbc_act, ddt, dssd_prm, g["ssd_norm_w"] = _ssd_bwd(dycat, 0, proj["z"], y_pre, xs_act, bc_act,
                                                                    proj["dt"], prev, ssd_prm, p["ssd_norm_w"], ex)
    g["ssd_dt_bias"], g["ssd_a_log"], g["ssd_d"] = (dssd_prm[k:k + 1, :SSD_HEADS] for k in range(3))
    dxs, dcw_xs, dcb_xs = _conv_bwd(dxs_act, proj["xs"], p["ssd_conv_w"][:, :SSD_WIDTH],
                                    p["ssd_conv_b"][:, :SSD_WIDTH], silu=True, name="ssd_conv_xs_bwd")
    dbc, dcw_bc, dcb_bc = _conv_bwd(dbc_act, proj["bc"], p["ssd_conv_w"][:, SSD_WIDTH:],
                                    p["ssd_conv_b"][:, SSD_WIDTH:], silu=True, name="ssd_conv_bc_bwd")
    g["ssd_conv_w"] = jnp.concatenate([dcw_xs, dcw_bc], axis=1)
    g["ssd_conv_b"] = jnp.concatenate([dcb_xs, dcb_bc], axis=1)
    dproj = {"z": dz, "xs": dxs, "bc": dbc, "dt": ddt, "g": dgate, "x": dx_lru}
    dh0, _, g["norm1_w"] = _mm_norm_bwd([(dproj[s], p["w_in_" + s], SEC_WIDTH[s]) for s in SEC_NAMES], h0,
                                        p["norm1_w"], dh1, name="in_proj_bwd")
    for s in SEC_NAMES:
        wdt = SEC_WIDTH[s]
        g["w_in_" + s], g_mxu["w_in_" + s] = _mm([(dproj[s], 0, u1, 0, T_ROWS)], wdt, D_MODEL, tm=min(wdt, 512),
                                                 tn=512, mode="tn", out_dtype=F32, name="dw_in_" + s, also_mxu=True)
    g["meta_tokens"] = dh0[PAD_ROWS:X_ROW0]
    return loss, dh0[X_ROW0:], g, g_mxu


MESH = pl.DeviceIdType.MESH
ANY = pl.BlockSpec(memory_space=pl.ANY)


def _my_place():
    return lax.axis_index("x"), lax.axis_index("y"), lax.axis_index("c")


def _other_chips(x, y):
    return [(1 - x, y), (x, 1 - y), (1 - x, 1 - y)]


def _gather_shards(shards):
    n = len(shards)

    def body(*refs):
        ins, outs = refs[:n], refs[n:2 * n]
        send_sems, recv_sems, local_sems = refs[2 * n:]
        x, y, c = _my_place()
        me = 2 * x + y
        peers = _other_chips(x, y)
        local = [pltpu.make_async_copy(ins[k], outs[k].at[me], local_sems.at[k]) for k in range(n)]
        for cp in local:
            cp.start()
        for k in range(n):
            for j, (px, py) in enumerate(peers):
                pltpu.make_async_remote_copy(
                    src_ref=ins[k], dst_ref=outs[k].at[me], send_sem=send_sems.at[3 * k + j],
                    recv_sem=recv_sems.at[3 * k + j], device_id=(px, py, c), device_id_type=MESH).start()
        for k in range(n):
            for j, (px, py) in enumerate(peers):
                pltpu.make_async_remote_copy(
                    src_ref=ins[k], dst_ref=outs[k].at[2 * px + py], send_sem=send_sems.at[3 * k + j],
                    recv_sem=recv_sems.at[3 * k + j], device_id=(px, py, c), device_id_type=MESH).wait()
        for cp in local:
            cp.wait()

    return pl.pallas_call(
        body, name="gather_weights", in_specs=[ANY] * n, out_specs=[ANY] * n,
        out_shape=[jax.ShapeDtypeStruct((N_SHARDS,) + s.shape, s.dtype) for s in shards],
        scratch_shapes=[pltpu.SemaphoreType.DMA((3 * n,)), pltpu.SemaphoreType.DMA((3 * n,)),
                        pltpu.SemaphoreType.DMA((n,))],
    )(*shards)


def _scatter_grads(grads4):
    n = len(grads4)

    def body(*refs):
        ins, outs = refs[:n], refs[n:2 * n]
        send_sems, recv_sems = refs[2 * n:]
        x, y, c = _my_place()
        peers = _other_chips(x, y)
        for k in range(n):
            for j, (px, py) in enumerate(peers):
                pltpu.make_async_remote_copy(
                    src_ref=ins[k].at[2 * px + py], dst_ref=outs[k].at[j], send_sem=send_sems.at[3 * k + j],
                    recv_sem=recv_sems.at[3 * k + j], device_id=(px, py, c), device_id_type=MESH).start()
        for k in range(n):
            for j, (px, py) in enumerate(peers):
                pltpu.make_async_remote_copy(
                    src_ref=ins[k].at[2 * px + py], dst_ref=outs[k].at[j], send_sem=send_sems.at[3 * k + j],
                    recv_sem=recv_sems.at[3 * k + j], device_id=(px, py, c), device_id_type=MESH).wait()

    return pl.pallas_call(
        body, name="scatter_grads", in_specs=[ANY] * n, out_specs=[ANY] * n,
        out_shape=[jax.ShapeDtypeStruct((3,) + g.shape[1:], g.dtype) for g in grads4],
        scratch_shapes=[pltpu.SemaphoreType.DMA((3 * n,)), pltpu.SemaphoreType.DMA((3 * n,))],
    )(*grads4)


HBM_SPEC = pl.BlockSpec(memory_space=pltpu.HBM)
SEM_SPEC = pl.BlockSpec(memory_space=pltpu.SEMAPHORE)
SPLIT_EFFECT = pltpu.SideEffectType.DATAFLOW_SIDE_EFFECTING


def _gather_plan(bufs, x, y, c, incoming):
    plan = []
    for buf in bufs:
        for (px, py) in _other_chips(x, y):
            slot = 2 * px + py if incoming else 2 * x + y
            plan.append((buf.at[2 * x + y], buf.at[slot], (px, py, c)))
    return plan


def _scatter_plan(bufs, x, y, c, incoming):
    n = len(bufs) // 2
    plan = []
    for k in range(n):
        for j, (px, py) in enumerate(_other_chips(x, y)):
            plan.append((bufs[k].at[2 * px + py], bufs[n + k].at[j], (px, py, c)))
    return plan


def _split_start(bufs, plan, n_copies, after, *, name):
    n = len(bufs)

    def body(*refs):
        ins = refs[:n]
        send_sems, recv_sems = refs[n + 1], refs[n + 2]
        token = refs[-1]
        x, y, c = _my_place()
        for i, (src, dst, dev) in enumerate(plan(ins, x, y, c, False)):
            pltpu.make_async_remote_copy(src_ref=src, dst_ref=dst, send_sem=send_sems.at[i], recv_sem=recv_sems.at[i],
                                         device_id=dev, device_id_type=MESH).start()
        token[...] = jnp.zeros_like(token)

    outs = pl.pallas_call(
        body, name=name,
        out_shape=(pltpu.SemaphoreType.DMA((n_copies,)), pltpu.SemaphoreType.DMA((n_copies,)),
                   *[pltpu.HBM(b.shape, b.dtype) for b in bufs], jax.ShapeDtypeStruct((8, 128), F32)),
        in_specs=[HBM_SPEC] * n + [ANY],
        out_specs=(SEM_SPEC, SEM_SPEC, *[HBM_SPEC] * n, pl.BlockSpec(memory_space=pltpu.VMEM)),
        input_output_aliases={k: 2 + k for k in range(n)},
        compiler_params=pltpu.CompilerParams(has_side_effects=SPLIT_EFFECT),
    )(*[pltpu.with_memory_space_constraint(b, pltpu.HBM) for b in bufs], after)
    return outs[0], outs[1], list(outs[2:2 + n]), outs[-1]


def _split_wait(bufs, send_sems, recv_sems, plan, after, *, name):
    n = len(bufs)

    def body(*refs):
        ins = refs[:n]
        send_sems_ref, recv_sems_ref = refs[n], refs[n + 1]
        x, y, c = _my_place()
        for i, (src, dst, dev) in enumerate(plan(ins, x, y, c, True)):
            cp = pltpu.make_async_remote_copy(src_ref=src, dst_ref=dst, send_sem=send_sems_ref.at[i],
                                              recv_sem=recv_sems_ref.at[i], device_id=dev, device_id_type=MESH)
            cp.wait_send()
            cp.wait_recv()

    outs = pl.pallas_call(
        body, name=name, out_shape=tuple(pltpu.HBM(b.shape, b.dtype) for b in bufs),
        in_specs=[HBM_SPEC] * n + [SEM_SPEC, SEM_SPEC, ANY], out_specs=tuple([HBM_SPEC] * n),
        input_output_aliases={k: k for k in range(n)},
        compiler_params=pltpu.CompilerParams(has_side_effects=SPLIT_EFFECT),
    )(*bufs, send_sems, recv_sems, after)
    return list(outs)


def _fill_own_slot(shard, me_arr, *, name):
    r, c = shard.shape
    tile, steps, imap = _elementwise_tile(r, c)

    def body(me_ref, x_ref, o_ref):
        o_ref[0] = x_ref[...].astype(_MXU)

    return pl.pallas_call(
        body, name=name,
        grid_spec=pltpu.PrefetchScalarGridSpec(
            num_scalar_prefetch=1, grid=(steps,),
            in_specs=[pl.BlockSpec(tile, lambda i, me: imap(i))],
            out_specs=pl.BlockSpec((1,) + tile, lambda i, me: (me[0],) + imap(i))),
        out_shape=jax.ShapeDtypeStruct((N_SHARDS, r, c), _MXU),
        compiler_params=_cparams("parallel"),
    )(me_arr, shard)


def _swap_with_sibling(parts):
    n = len(parts)

    def body(*refs):
        ins, outs = refs[:n], refs[n:2 * n]
        send_sems, recv_sems = refs[2 * n:]
        x, y, c = _my_place()
        copies = [pltpu.make_async_remote_copy(
            src_ref=ins[k], dst_ref=outs[k], send_sem=send_sems.at[k], recv_sem=recv_sems.at[k],
            device_id=(x, y, 1 - c), device_id_type=MESH) for k in range(n)]
        for cp in copies:
            cp.start()
        for cp in copies:
            cp.wait()

    return pl.pallas_call(
        body, name="swap_with_sibling", in_specs=[ANY] * n, out_specs=[ANY] * n,
        out_shape=[jax.ShapeDtypeStruct(a.shape, a.dtype) for a in parts],
        scratch_shapes=[pltpu.SemaphoreType.DMA((n,)), pltpu.SemaphoreType.DMA((n,))],
    )(*parts)


def _all_reduce_small(pack):
    rows = pack.shape[0]

    def body(x_ref, o_ref, buf, send_sems, recv_sems):
        x, y, c = _my_place()
        me = 4 * x + 2 * y + c
        buf[me] = x_ref[...]
        masks = [(m >> 2 & 1, m >> 1 & 1, m & 1) for m in range(1, N_DEV)]

        def copy(i):
            mx, my, mc = masks[i]
            px, py, pc = x ^ mx, y ^ my, c ^ mc
            return pltpu.make_async_remote_copy(
                src_ref=x_ref, dst_ref=buf.at[me], send_sem=send_sems.at[i], recv_sem=recv_sems.at[i],
                device_id=(px, py, pc), device_id_type=MESH), 4 * px + 2 * py + pc

        for i in range(N_DEV - 1):
            copy(i)[0].start()
        for i in range(N_DEV - 1):
            cp, peer = copy(i)
            cp.wait_send()
            pltpu.make_async_remote_copy(
                src_ref=x_ref, dst_ref=buf.at[peer], send_sem=send_sems.at[i], recv_sem=recv_sems.at[i],
                device_id=(x, y, c), device_id_type=MESH).wait_recv()
        acc = buf[0]
        for d in range(1, N_DEV):
            acc = acc + buf[d]
        o_ref[...] = acc

    vmem = pl.BlockSpec(memory_space=pltpu.VMEM)
    return pl.pallas_call(
        body, name="all_reduce_small", in_specs=[vmem], out_specs=vmem,
        out_shape=jax.ShapeDtypeStruct(pack.shape, F32),
        scratch_shapes=[pltpu.VMEM((N_DEV, rows, pack.shape[1]), F32), pltpu.SemaphoreType.DMA((N_DEV - 1,)),
                        pltpu.SemaphoreType.DMA((N_DEV - 1,))],
        compiler_params=pltpu.CompilerParams(vmem_limit_bytes=VMEM_LIMIT_BYTES),
    )(pack)


def _elementwise_tile(rows, cols, limit=256):
    for t in range(limit, 15, -16):
        if rows % t == 0:
            return (t, cols), rows // t, lambda i: (i, 0)
    assert cols % limit == 0
    return (rows, limit), cols // limit, lambda i: (0, i)


def _partial_sum(g4, land, me_arr, *, name):
    _, r, c = g4.shape
    tile, steps, imap = _elementwise_tile(r, c)

    def body(me_ref, own_ref, land_ref, o_ref):
        acc = own_ref[0]
        for j in range(3):
            acc = acc + land_ref[j].astype(F32)
        o_ref[...] = acc

    return pl.pallas_call(
        body, name=name,
        grid_spec=pltpu.PrefetchScalarGridSpec(
            num_scalar_prefetch=1, grid=(steps,),
            in_specs=[pl.BlockSpec((1,) + tile, lambda i, me: (me[0],) + imap(i)),
                      pl.BlockSpec((3,) + tile, lambda i, me: (0,) + imap(i))],
            out_specs=pl.BlockSpec(tile, lambda i, me: imap(i))),
        out_shape=jax.ShapeDtypeStruct((r, c), F32),
        compiler_params=_cparams("parallel"),
    )(me_arr, g4, land)


def _adamw_math(w, g, m, v):
    m = ADAM_B1 * m + (1.0 - ADAM_B1) * g
    v = ADAM_B2 * v + (1.0 - ADAM_B2) * (g * g)
    m_hat = m / (1.0 - ADAM_B1 ** ADAM_STEP)
    v_hat = v / (1.0 - ADAM_B2 ** ADAM_STEP)
    delta = -ADAM_LR * (m_hat / (jnp.sqrt(v_hat) + ADAM_EPS) + ADAM_WD * w)
    return delta, m, v


def _adamw(w, grad_parts, m, v, *, name):
    r, c = w.shape
    tile_shape, steps, imap = _elementwise_tile(r, c)
    n = len(grad_parts)

    def body(*refs):
        w_ref, m_ref, v_ref = refs[:3]
        g_refs = refs[3:3 + n]
        g_out, d_out, m_out, v_out = refs[3 + n:]
        g = g_refs[0][...]
        for k in range(1, n):
            g = g + g_refs[k][...]
        delta, m_new, v_new = _adamw_math(w_ref[...], g, m_ref[...], v_ref[...])
        g_out[...] = g
        d_out[...] = delta
        m_out[...] = m_new
        v_out[...] = v_new

    tile = pl.BlockSpec(tile_shape, imap)
    return pl.pallas_call(
        body, name=name, grid=(steps,), in_specs=[tile] * (3 + n), out_specs=[tile] * 4,
        out_shape=[jax.ShapeDtypeStruct((r, c), F32)] * 4,
        compiler_params=_cparams("parallel"),
    )(w, m, v, *grad_parts)


WEIGHT_NAMES = ("meta_tokens", "norm1_w", "w_in", "ssd_conv_w", "ssd_conv_b", "ssd_dt_bias", "ssd_a_log", "ssd_d",
                "ssd_norm_w", "lru_conv_w", "lru_conv_b", "lru_wa", "lru_ba", "lru_wx", "lru_bx", "lru_lambda",
                "lru_norm_w", "w_out", "norm2_w", "w_gate", "w_up", "w_down", "final_norm_w")
BIG = ("w_in", "w_out", "w_gate", "w_up", "w_down")
FFN = ("w_gate", "w_up", "w_down")
LATE = ("w_out",) + FFN
SMALL_SHARDED = {"meta_tokens": (N_META, D_MODEL), "ssd_conv_w": (CONV_K, 1536), "lru_conv_w": (CONV_K, LRU_WIDTH)}
SMALL = tuple(n for n in WEIGHT_NAMES if n not in BIG)
PACK_COLS = 1024


def _pack(arrays):
    flat = jnp.concatenate([a.reshape(-1) for a in arrays])
    rows = -(-flat.shape[0] // (8 * PACK_COLS)) * 8
    return jnp.pad(flat, (0, rows * PACK_COLS - flat.shape[0])).reshape(rows, PACK_COLS)


def _unpack(pack, shapes):
    flat = pack.reshape(-1)
    out, off = [], 0
    for s in shapes:
        size = math.prod(s)
        out.append(flat[off:off + size].reshape(s))
        off += size
    return out


def _unshard_cols(g4):
    return jnp.swapaxes(g4, 0, 1).reshape(g4.shape[1], -1)


COL_SHARDED = ("w_in", "w_gate", "w_up")
IN_ROWS = {"z": (0, 1024), "xs": (1024, 2048), "bc": (2048, 2560), "dt": (2560, 2576), "g": (2576, 3600),
           "x": (3600, IN_COLS)}


def _rows_view(name, block):
    return jnp.swapaxes(block[0], 0, 1) if name in COL_SHARDED else block[0]


def _param_view(name, rows):
    return (jnp.swapaxes(rows, 0, 1) if name in COL_SHARDED else rows)[None]


def kernel(x, meta_tokens, norm1_w, w_in, ssd_conv_w, ssd_conv_b, ssd_dt_bias, ssd_a_log, ssd_d, ssd_norm_w, lru_conv_w, lru_conv_b, lru_wa, lru_ba, lru_wx, lru_bx, lru_lambda, lru_norm_w, w_out, norm2_w, w_gate, w_up, w_down, final_norm_w, loss_target, m_meta_tokens, m_norm1_w, m_w_in, m_ssd_conv_w, m_ssd_conv_b, m_ssd_dt_bias, m_ssd_a_log, m_ssd_d, m_ssd_norm_w, m_lru_conv_w, m_lru_conv_b, m_lru_wa, m_lru_ba, m_lru_wx, m_lru_bx, m_lru_lambda, m_lru_norm_w, m_w_out, m_norm2_w, m_w_gate, m_w_up, m_w_down, m_final_norm_w, v_meta_tokens, v_norm1_w, v_w_in, v_ssd_conv_w, v_ssd_conv_b, v_ssd_dt_bias, v_ssd_a_log, v_ssd_d, v_ssd_norm_w, v_lru_conv_w, v_lru_conv_b, v_lru_wa, v_lru_ba, v_lru_wx, v_lru_bx, v_lru_lambda, v_lru_norm_w, v_w_out, v_norm2_w, v_w_gate, v_w_up, v_w_down, v_final_norm_w):
    w = dict(zip(WEIGHT_NAMES, (meta_tokens, norm1_w, w_in, ssd_conv_w, ssd_conv_b, ssd_dt_bias, ssd_a_log, ssd_d, ssd_norm_w, lru_conv_w, lru_conv_b, lru_wa, lru_ba, lru_wx, lru_bx, lru_lambda, lru_norm_w, w_out, norm2_w, w_gate, w_up, w_down, final_norm_w)))
    m = dict(zip(WEIGHT_NAMES, (m_meta_tokens, m_norm1_w, m_w_in, m_ssd_conv_w, m_ssd_conv_b, m_ssd_dt_bias, m_ssd_a_log, m_ssd_d, m_ssd_norm_w, m_lru_conv_w, m_lru_conv_b, m_lru_wa, m_lru_ba, m_lru_wx, m_lru_bx, m_lru_lambda, m_lru_norm_w, m_w_out, m_norm2_w, m_w_gate, m_w_up, m_w_down, m_final_norm_w)))
    v = dict(zip(WEIGHT_NAMES, (v_meta_tokens, v_norm1_w, v_w_in, v_ssd_conv_w, v_ssd_conv_b, v_ssd_dt_bias, v_ssd_a_log, v_ssd_d, v_ssd_norm_w, v_lru_conv_w, v_lru_conv_b, v_lru_wa, v_lru_ba, v_lru_wx, v_lru_bx, v_lru_lambda, v_lru_norm_w, v_w_out, v_norm2_w, v_w_gate, v_w_up, v_w_down, v_final_norm_w)))
    me = 2 * lax.axis_index("x") + lax.axis_index("y")

    big2d = {n: _rows_view(n, w[n]) for n in BIG}
    small_local = jnp.concatenate([w["meta_tokens"].reshape(-1), w["ssd_conv_w"].reshape(-1),
                                   w["lru_conv_w"].reshape(-1)])[None]
    me_arr = me.astype(jnp.int32).reshape(1)
    w_in4, small4 = _gather_shards([big2d["w_in"].astype(_MXU), small_local])
    w_in_full = w_in4.reshape(-1, D_MODEL)
    sm = small4[:, 0]
    meta_full = _unshard_cols(sm[:, :4096].reshape(N_SHARDS, N_META, 256))
    ssd_conv_w_full = _unshard_cols(sm[:, 4096:5632].reshape(N_SHARDS, CONV_K, 384))
    lru_conv_w_full = _unshard_cols(sm[:, 5632:].reshape(N_SHARDS, CONV_K, 256))
    slots = {n: _fill_own_slot(big2d[n], me_arr, name="own_slot_" + n) for n in LATE}
    out_send, out_recv, out_bufs, tok_a = _split_start([slots["w_out"]], _gather_plan, 3, small4,
                                                       name="gather_w_out_start")
    ffn_send, ffn_recv, ffn_bufs, tok_b = _split_start([slots[n] for n in FFN], _gather_plan, 9, tok_a,
                                                       name="gather_ffn_start")

    p = {"w_in_" + s: w_in_full[lo:hi] for s, (lo, hi) in IN_ROWS.items()}
    p["w_in_dt"] = jnp.pad(p["w_in_dt"], ((0, SEC_WIDTH["dt"] - SSD_HEADS), (0, 0)))
    p.update({"ssd_conv_w": ssd_conv_w_full, "lru_conv_w": lru_conv_w_full,
              "lru_wa": w["lru_wa"][0], "lru_wx": w["lru_wx"][0], "final_norm_w": w["final_norm_w"][None]})
    for n in ("norm1_w", "ssd_conv_b", "ssd_dt_bias", "ssd_a_log", "ssd_d", "ssd_norm_w", "lru_conv_b", "lru_ba",
              "lru_bx", "lru_lambda", "lru_norm_w", "norm2_w"):
        p[n] = w[n]
    p["norm1_w"] = p["norm1_w"] + tok_b[:1, :1]

    class Late:
        def __init__(self):
            self.pending = []

        def w_out(self, after):
            (buf,) = _split_wait(out_bufs, out_send, out_recv, _gather_plan, after, name="gather_w_out_wait")
            return buf.reshape(-1, D_MODEL)

        def ffn(self, after):
            bufs = _split_wait(ffn_bufs, ffn_send, ffn_recv, _gather_plan, after, name="gather_ffn_wait")
            return tuple(b.reshape(-1, D_MODEL) for b in bufs)

        def grads_ready(self, names, g, g_mxu):
            srcs = [g_mxu[n].reshape(N_SHARDS, -1, D_MODEL) for n in names]
            lands = [lax.empty((3,) + s.shape[1:], _MXU) for s in srcs]
            tag = "_".join(names)
            send, recv, bufs, tok = _split_start(srcs + lands, _scatter_plan, 3 * len(names), g[names[-1]],
                                                 name="scatter_" + tag + "_start")
            self.pending.append((names, send, recv, bufs, tag))
            return tok[:1, :1]

        def landed(self, after):
            land = {}
            for names, send, recv, bufs, tag in self.pending:
                bufs = _split_wait(bufs, send, recv, _scatter_plan, after, name="scatter_" + tag + "_wait")
                land.update(zip(names, bufs[len(names):]))
            return land

    late = Late()

    loss, grad_x, g, g_mxu = _local_step(x[0], loss_target[0], meta_full, p, late)

    g["w_in"] = jnp.concatenate([g["w_in_" + s][:hi - lo] for s, (lo, hi) in IN_ROWS.items()], axis=0)
    g_mxu["w_in"] = jnp.concatenate([g_mxu["w_in_" + s][:hi - lo] for s, (lo, hi) in IN_ROWS.items()], axis=0)
    g4 = {n: g[n].reshape(N_SHARDS, -1, D_MODEL) for n in BIG}
    (land_w_in,) = _scatter_grads([g_mxu["w_in"].reshape(N_SHARDS, -1, D_MODEL)])
    land = late.landed(land_w_in)
    land["w_in"] = land_w_in
    part = {n: _partial_sum(g4[n], land[n], me_arr, name="partial_" + n) for n in BIG}
    sib = dict(zip(BIG, _swap_with_sibling([part[n] for n in BIG])))

    small_full_shape = {n: (SMALL_SHARDED[n] if n in SMALL_SHARDED else w[n].shape) for n in SMALL}
    red = _all_reduce_small(_pack([g[n] for n in SMALL] + [loss[0, :1]]))
    red_list = _unpack(red, [small_full_shape[n] for n in SMALL] + [(1,)])
    loss_total = red_list[-1][0]
    g_small = {}
    for n, arr in zip(SMALL, red_list[:-1]):
        if n in SMALL_SHARDED:
            cols = SMALL_SHARDED[n][1] // N_SHARDS
            arr = lax.dynamic_slice_in_dim(arr, me * cols, cols, axis=1)
        g_small[n] = arr.reshape(w[n].shape)

    grad, delta, new_m, new_v = {}, {}, {}, {}
    for n in BIG:
        outs = _adamw(big2d[n], [part[n], sib[n]], _rows_view(n, m[n]), _rows_view(n, v[n]), name="adamw_" + n)
        grad[n], delta[n], new_m[n], new_v[n] = (_param_view(n, o) for o in outs)
    shapes = [w[n].shape for n in SMALL]
    packs = [_pack([d[n] for n in SMALL]) for d in (w, m, v, g_small)]
    _, d_pack, m_pack, v_pack = _adamw(packs[0], [packs[3]], packs[1], packs[2], name="adamw_small")
    for n, dn, mn, vn in zip(SMALL, _unpack(d_pack, shapes), _unpack(m_pack, shapes), _unpack(v_pack, shapes)):
        grad[n], delta[n], new_m[n], new_v[n] = g_small[n], dn, mn, vn

    return (loss_total, grad_x[None], *[grad[n] for n in WEIGHT_NAMES], *[delta[n] for n in WEIGHT_NAMES],
            *[new_m[n] for n in WEIGHT_NAMES], *[new_v[n] for n in WEIGHT_NAMES])
```

```python
import functools
import math

import jax
import jax.numpy as jnp
from jax import lax
from jax.experimental import pallas as pl
from jax.experimental.pallas import tpu as pltpu

F32 = jnp.float32
_MXU = jnp.bfloat16

D_MODEL = 1024
SEQ = 2048
N_META = 16
CHUNK = 128
T_ROWS = 2176
N_CHUNKS = T_ROWS // CHUNK
PAD_ROWS = T_ROWS - SEQ - N_META
X_ROW0 = PAD_ROWS + N_META
SSD_HEADS = 16
SSD_HEAD_DIM = 64
SSD_STATE = 128
SSD_GROUPS = 2
SSD_HPG = SSD_HEADS // SSD_GROUPS
SSD_WIDTH = 1024
LRU_WIDTH = 1024
LRU_C = 8.0
D_FF = 2816
EPS = 1e-6
IN_COLS = 4624
N_SHARDS = 4
N_DEV = 8

ADAM_LR = 0.001
ADAM_B1 = 0.9
ADAM_B2 = 0.999
ADAM_EPS = 1e-08
ADAM_WD = 0.01
ADAM_STEP = 10

VMEM_LIMIT_BYTES = 56 * 1024 * 1024

NN = (((1,), (0,)), ((), ()))
NT = (((1,), (1,)), ((), ()))
TN = (((0,), (0,)), ((), ()))


def _cparams(*sem):
    return pltpu.CompilerParams(dimension_semantics=sem, vmem_limit_bytes=VMEM_LIMIT_BYTES)


def _dot(a, b, dims=NN):
    return lax.dot_general(a.astype(_MXU), b.astype(_MXU), dims, preferred_element_type=F32)


def _dot_exact(a, b, dims=NN):
    return lax.dot_general(a, b, dims, preferred_element_type=F32, precision=lax.Precision.HIGHEST)


def _sigmoid(x):
    return 1.0 / (1.0 + jnp.exp(-x))


def _softplus(x):
    return jnp.maximum(x, 0.0) + jnp.log(1.0 + jnp.exp(-jnp.abs(x)))


def _silu(x):
    return x * _sigmoid(x)


def _silu_grad(x):
    s = _sigmoid(x)
    return s * (1.0 + x * (1.0 - s))


_GELU_C = math.sqrt(2.0 / math.pi)


def _gelu_and_grad(x):
    inner = _GELU_C * (x + 0.044715 * x * x * x)
    t = jnp.tanh(inner)
    g = 0.5 * x * (1.0 + t)
    dg = 0.5 * (1.0 + t) + 0.5 * x * (1.0 - t * t) * _GELU_C * (1.0 + 3.0 * 0.044715 * x * x)
    return g, dg


def _rms_fwd(x, w):
    rstd = lax.rsqrt(jnp.mean(x * x, axis=-1, keepdims=True) + EPS)
    return x * rstd * w


def _rms_bwd(x, w, dy):
    rstd = lax.rsqrt(jnp.mean(x * x, axis=-1, keepdims=True) + EPS)
    xhat = x * rstd
    dxhat = dy * w
    dx = rstd * (dxhat - xhat * jnp.mean(dxhat * xhat, axis=-1, keepdims=True))
    return dx, dy * xhat


def _mm(terms, m, n, *, tm, tn, mode, out_dtype, name, residual=None, n_outer=False, also_mxu=False):
    gm, gn = m // tm, n // tn
    assert gm * tm == m and gn * tn == n
    if n_outer:
        grid = (gn, gm)
        mi = lambda g0, g1: g1
        ni = lambda g0, g1: g0
    else:
        grid = (gm, gn)
        mi = lambda g0, g1: g0
        ni = lambda g0, g1: g1
    in_specs, args = [], []
    for (a, ka, b, kb, k) in terms:
        if mode == "tn":
            in_specs.append(pl.BlockSpec((k, tm), lambda g0, g1, ka=ka: (ka, mi(g0, g1))))
        else:
            in_specs.append(pl.BlockSpec((tm, k), lambda g0, g1, ka=ka: (mi(g0, g1), ka)))
        if mode == "nt":
            in_specs.append(pl.BlockSpec((tn, k), lambda g0, g1, kb=kb: (ni(g0, g1), kb)))
        else:
            in_specs.append(pl.BlockSpec((k, tn), lambda g0, g1, kb=kb: (kb, ni(g0, g1))))
        args += [a, b]
    if residual is not None:
        in_specs.append(pl.BlockSpec((tm, tn), lambda g0, g1: (mi(g0, g1), ni(g0, g1))))
        args.append(residual)
    dims = {"nn": NN, "nt": NT, "tn": TN}[mode]
    n_terms = len(terms)
    has_res = residual is not None

    n_in = len(args)

    def body(*refs):
        acc = None
        for t in range(n_terms):
            d = lax.dot_general(refs[2 * t][...], refs[2 * t + 1][...], dims, preferred_element_type=F32)
            acc = d if acc is None else acc + d
        if has_res:
            acc = acc + refs[2 * n_terms][...]
        refs[n_in][...] = acc.astype(out_dtype)
        if also_mxu:
            refs[n_in + 1][...] = acc.astype(_MXU)

    tile = pl.BlockSpec((tm, tn), lambda g0, g1: (mi(g0, g1), ni(g0, g1)))
    shape = jax.ShapeDtypeStruct((m, n), out_dtype)
    return pl.pallas_call(
        body, name=name, grid=grid, in_specs=in_specs,
        out_specs=[tile, tile] if also_mxu else tile,
        out_shape=[shape, jax.ShapeDtypeStruct((m, n), _MXU)] if also_mxu else shape,
        compiler_params=_cparams("parallel", "parallel"),
    )(*args)


def _embed(x, meta):
    def body(x_ref, meta_ref, o_ref):
        i = pl.program_id(0)

        @pl.when(i == 0)
        def _():
            o_ref[0:PAD_ROWS, :] = jnp.zeros((PAD_ROWS, D_MODEL), F32)
            o_ref[PAD_ROWS:CHUNK, :] = meta_ref[...]

        @pl.when(i > 0)
        def _():
            o_ref[...] = x_ref[...]

    return pl.pallas_call(
        body, name="embed", grid=(N_CHUNKS,),
        in_specs=[pl.BlockSpec((CHUNK, D_MODEL), lambda i: (jnp.maximum(i - 1, 0), 0)),
                  pl.BlockSpec((N_META, D_MODEL), lambda i: (0, 0))],
        out_specs=pl.BlockSpec((CHUNK, D_MODEL), lambda i: (i, 0)),
        out_shape=jax.ShapeDtypeStruct((T_ROWS, D_MODEL), F32),
        compiler_params=_cparams("parallel"),
    )(x, meta)


def _rmsnorm(h, w, *, name, tm=544):
    def body(h_ref, w_ref, o_ref):
        o_ref[...] = _rms_fwd(h_ref[...], w_ref[...]).astype(_MXU)

    return pl.pallas_call(
        body, name=name, grid=(T_ROWS // tm,),
        in_specs=[pl.BlockSpec((tm, D_MODEL), lambda i: (i, 0)), pl.BlockSpec((1, D_MODEL), lambda i: (0, 0))],
        out_specs=pl.BlockSpec((tm, D_MODEL), lambda i: (i, 0)),
        out_shape=jax.ShapeDtypeStruct((T_ROWS, D_MODEL), _MXU),
        compiler_params=_cparams("parallel"),
    )(h, w)


def _loss_head(h2, target, fw):
    def body(h_ref, t_ref, w_ref, loss_ref, dh_ref, dhb_ref, dw_ref, acc_ref):
        i = pl.program_id(0)

        @pl.when(i == 0)
        def _():
            acc_ref[...] = jnp.zeros_like(acc_ref)
            dw_ref[...] = jnp.zeros_like(dw_ref)

        h = h_ref[...]
        w = w_ref[...]
        y = _rms_fwd(h, w)
        live = (i > 0).astype(F32)
        err = (y - t_ref[...]) * live
        acc_ref[...] += jnp.sum(err * err, axis=0, keepdims=True)
        dy = err * (1.0 / D_MODEL)
        dx, dwr = _rms_bwd(h, w, dy)
        dh_ref[...] = dx
        dhb_ref[...] = dx.astype(_MXU)
        dw_ref[...] += jnp.sum(dwr, axis=0, keepdims=True)

        @pl.when(i == N_CHUNKS - 1)
        def _():
            tot = jnp.sum(acc_ref[...], axis=1, keepdims=True) * (0.5 / D_MODEL)
            loss_ref[...] = jnp.broadcast_to(tot, (1, 128))

    return pl.pallas_call(
        body, name="loss_head", grid=(N_CHUNKS,),
        in_specs=[pl.BlockSpec((CHUNK, D_MODEL), lambda i: (i, 0)),
                  pl.BlockSpec((CHUNK, D_MODEL), lambda i: (jnp.maximum(i - 1, 0), 0)),
                  pl.BlockSpec((1, D_MODEL), lambda i: (0, 0))],
        out_specs=[pl.BlockSpec((1, 128), lambda i: (0, 0)),
                   pl.BlockSpec((CHUNK, D_MODEL), lambda i: (i, 0)),
                   pl.BlockSpec((CHUNK, D_MODEL), lambda i: (i, 0)),
                   pl.BlockSpec((1, D_MODEL), lambda i: (0, 0))],
        out_shape=[jax.ShapeDtypeStruct((1, 128), F32),
                   jax.ShapeDtypeStruct((T_ROWS, D_MODEL), F32),
                   jax.ShapeDtypeStruct((T_ROWS, D_MODEL), _MXU),
                   jax.ShapeDtypeStruct((1, D_MODEL), F32)],
        scratch_shapes=[pltpu.VMEM((1, D_MODEL), F32)],
        compiler_params=_cparams("arbitrary"),
    )(h2, target, fw)


def _mm_norm_bwd(terms, h, w, dres, *, name, tm=272):
    n_terms = len(terms)
    in_specs, args = [], []
    for (a, b, k) in terms:
        in_specs += [pl.BlockSpec((tm, k), lambda i: (i, 0)), pl.BlockSpec((k, D_MODEL), lambda i: (0, 0))]
        args += [a, b]
    in_specs += [pl.BlockSpec((tm, D_MODEL), lambda i: (i, 0)), pl.BlockSpec((1, D_MODEL), lambda i: (0, 0)),
                 pl.BlockSpec((tm, D_MODEL), lambda i: (i, 0))]
    args += [h, w, dres]

    def body(*refs):
        h_ref, w_ref, dres_ref, dh_ref, dhb_ref, dw_ref = refs[2 * n_terms:]

        @pl.when(pl.program_id(0) == 0)
        def _():
            dw_ref[...] = jnp.zeros_like(dw_ref)

        du = None
        for t in range(n_terms):
            d = lax.dot_general(refs[2 * t][...], refs[2 * t + 1][...], NN, preferred_element_type=F32)
            du = d if du is None else du + d
        dx, dwr = _rms_bwd(h_ref[...], w_ref[...], du)
        dh = dres_ref[...] + dx
        dh_ref[...] = dh
        dhb_ref[...] = dh.astype(_MXU)
        dw_ref[...] += jnp.sum(dwr, axis=0, keepdims=True)

    return pl.pallas_call(
        body, name=name, grid=(T_ROWS // tm,), in_specs=in_specs,
        out_specs=[pl.BlockSpec((tm, D_MODEL), lambda i: (i, 0)), pl.BlockSpec((tm, D_MODEL), lambda i: (i, 0)),
                   pl.BlockSpec((1, D_MODEL), lambda i: (0, 0))],
        out_shape=[jax.ShapeDtypeStruct((T_ROWS, D_MODEL), F32), jax.ShapeDtypeStruct((T_ROWS, D_MODEL), _MXU),
                   jax.ShapeDtypeStruct((1, D_MODEL), F32)],
        compiler_params=_cparams("arbitrary"),
    )(*args)


FFN_TM = 272
FFN_TN = 1408


def _ffn_up(u2, wg_t, wu_t):
    def body(u_ref, wg_ref, wu_ref, gp_ref, up_ref, act_ref):
        u = u_ref[...]
        gp = lax.dot_general(u, wg_ref[...], NT, preferred_element_type=F32)
        up = lax.dot_general(u, wu_ref[...], NT, preferred_element_type=F32)
        gp_ref[...] = gp
        up_ref[...] = up
        act_ref[...] = (_silu(gp) * up).astype(_MXU)

    tile = pl.BlockSpec((FFN_TM, FFN_TN), lambda j, i: (i, j))
    return pl.pallas_call(
        body, name="ffn_up", grid=(D_FF // FFN_TN, T_ROWS // FFN_TM),
        in_specs=[pl.BlockSpec((FFN_TM, D_MODEL), lambda j, i: (i, 0)),
                  pl.BlockSpec((FFN_TN, D_MODEL), lambda j, i: (j, 0)),
                  pl.BlockSpec((FFN_TN, D_MODEL), lambda j, i: (j, 0))],
        out_specs=[tile, tile, tile],
        out_shape=[jax.ShapeDtypeStruct((T_ROWS, D_FF), F32), jax.ShapeDtypeStruct((T_ROWS, D_FF), F32),
                   jax.ShapeDtypeStruct((T_ROWS, D_FF), _MXU)],
        compiler_params=_cparams("parallel", "parallel"),
    )(u2, wg_t, wu_t)


def _ffn_bwd_act(dh2b, wd, gp, up):
    def body(dh_ref, wd_ref, gp_ref, up_ref, dgp_ref, dup_ref):
        dact = lax.dot_general(dh_ref[...], wd_ref[...], NT, preferred_element_type=F32)
        gp = gp_ref[...]
        dgp_ref[...] = (dact * up_ref[...] * _silu_grad(gp)).astype(_MXU)
        dup_ref[...] = (dact * _silu(gp)).astype(_MXU)

    tile = pl.BlockSpec((FFN_TM, FFN_TN), lambda j, i: (i, j))
    return pl.pallas_call(
        body, name="ffn_bwd_act", grid=(D_FF // FFN_TN, T_ROWS // FFN_TM),
        in_specs=[pl.BlockSpec((FFN_TM, D_MODEL), lambda j, i: (i, 0)),
                  pl.BlockSpec((FFN_TN, D_MODEL), lambda j, i: (j, 0)), tile, tile],
        out_specs=[tile, tile],
        out_shape=[jax.ShapeDtypeStruct((T_ROWS, D_FF), _MXU), jax.ShapeDtypeStruct((T_ROWS, D_FF), _MXU)],
        compiler_params=_cparams("parallel", "parallel"),
    )(dh2b, wd, gp, up)


CONV_TC = 512
CONV_K = 4


def _conv_pre(x_ref, wv, bv, c):
    tc = wv.shape[1]
    r0 = c * CHUNK
    cur = x_ref[r0:r0 + CHUNK, :]
    prev8 = jnp.zeros((8, tc), F32) if c == 0 else x_ref[r0 - 8:r0, :]
    cat = jnp.concatenate([prev8, cur], axis=0)
    shifted = [cur] + [pltpu.roll(cat, s, 0)[8:8 + CHUNK] for s in range(1, CONV_K)]
    pre = bv
    for s in range(CONV_K):
        pre = pre + shifted[s] * wv[CONV_K - 1 - s:CONV_K - s]
    return pre, shifted


def _row_mask(c):
    if c > 0:
        return None
    return (lax.broadcasted_iota(jnp.int32, (CHUNK, 1), 0) >= PAD_ROWS).astype(F32)


def _conv_fwd(x, w, b, *, silu, name):
    cols = x.shape[1]
    tc = min(CONV_TC, cols)

    def body(x_ref, w_ref, b_ref, o_ref):
        wv, bv = w_ref[...], b_ref[...]
        for c in range(N_CHUNKS):
            pre, _ = _conv_pre(x_ref, wv, bv, c)
            y = _silu(pre) if silu else pre
            mask = _row_mask(c)
            if mask is not None:
                y = y * mask
            o_ref[c * CHUNK:(c + 1) * CHUNK, :] = y

    return pl.pallas_call(
        body, name=name, grid=(cols // tc,),
        in_specs=[pl.BlockSpec((T_ROWS, tc), lambda j: (0, j)), pl.BlockSpec((CONV_K, tc), lambda j: (0, j)),
                  pl.BlockSpec((1, tc), lambda j: (0, j))],
        out_specs=pl.BlockSpec((T_ROWS, tc), lambda j: (0, j)),
        out_shape=jax.ShapeDtypeStruct((T_ROWS, cols), F32),
        compiler_params=_cparams("parallel"),
    )(x, w, b)


def _conv_bwd(dy, x, w, b, *, silu, name):
    cols = x.shape[1]
    tc = min(CONV_TC, cols)

    def body(dy_ref, x_ref, w_ref, b_ref, dx_ref, dw_ref, db_ref):
        wv, bv = w_ref[...], b_ref[...]
        next8 = jnp.zeros((8, tc), F32)
        dws = [jnp.zeros((1, tc), F32) for _ in range(CONV_K)]
        db = jnp.zeros((1, tc), F32)
        for c in reversed(range(N_CHUNKS)):
            pre, shifted = _conv_pre(x_ref, wv, bv, c)
            dpre = dy_ref[c * CHUNK:(c + 1) * CHUNK, :]
            if silu:
                dpre = dpre * _silu_grad(pre)
            mask = _row_mask(c)
            if mask is not None:
                dpre = dpre * mask
            cat = jnp.concatenate([dpre, next8], axis=0)
            dx = dpre * wv[CONV_K - 1:CONV_K]
            for s in range(1, CONV_K):
                dx = dx + pltpu.roll(cat, CHUNK + 8 - s, 0)[0:CHUNK] * wv[CONV_K - 1 - s:CONV_K - s]
            dx_ref[c * CHUNK:(c + 1) * CHUNK, :] = dx.astype(_MXU)
            for s in range(CONV_K):
                k = CONV_K - 1 - s
                dws[k] = dws[k] + jnp.sum(dpre * shifted[s], axis=0, keepdims=True)
            db = db + jnp.sum(dpre, axis=0, keepdims=True)
            next8 = dpre[0:8]
        dw_ref[...] = jnp.concatenate(dws, axis=0)
        db_ref[...] = db

    return pl.pallas_call(
        body, name=name, grid=(cols // tc,),
        in_specs=[pl.BlockSpec((T_ROWS, tc), lambda j: (0, j)), pl.BlockSpec((T_ROWS, tc), lambda j: (0, j)),
                  pl.BlockSpec((CONV_K, tc), lambda j: (0, j)), pl.BlockSpec((1, tc), lambda j: (0, j))],
        out_specs=[pl.BlockSpec((T_ROWS, tc), lambda j: (0, j)), pl.BlockSpec((CONV_K, tc), lambda j: (0, j)),
                   pl.BlockSpec((1, tc), lambda j: (0, j))],
        out_shape=[jax.ShapeDtypeStruct((T_ROWS, cols), _MXU), jax.ShapeDtypeStruct((CONV_K, cols), F32),
                   jax.ShapeDtypeStruct((1, cols), F32)],
        compiler_params=_cparams("parallel"),
    )(dy, x, w, b)


def _ssd_chunk_common(dt_raw, prm, c):
    a_row = -jnp.exp(prm[1:2])
    dt = _softplus(dt_raw + prm[0:1])
    rows = lax.broadcasted_iota(jnp.int32, (CHUNK, 1), 0)
    real = jnp.logical_or(c > 0, rows >= PAD_ROWS)
    dt = jnp.where(real, dt, 0.0)
    li = lax.broadcasted_iota(jnp.int32, (CHUNK, CHUNK), 0)
    si = lax.broadcasted_iota(jnp.int32, (CHUNK, CHUNK), 1)
    causal = li >= si
    tri = causal.astype(F32)
    cs = _dot_exact(tri, dt * a_row)
    return dt, a_row, cs, cs.T, causal, tri, real


def _gated_norm_fwd(y, z, w):
    g = y * _silu(z)
    half = SSD_WIDTH // SSD_GROUPS
    outs = [_rms_fwd(g[:, k * half:(k + 1) * half], w[:, k * half:(k + 1) * half]) for k in range(SSD_GROUPS)]
    return jnp.concatenate(outs, axis=1)


GROUP_W = SSD_WIDTH // SSD_GROUPS
PAIR_W = 2 * SSD_HEAD_DIM
STATE_SHAPE = (SSD_GROUPS, SSD_STATE, GROUP_W)


def _head_expander():
    r = lax.broadcasted_iota(jnp.int32, (128, SSD_WIDTH), 0)
    c = lax.broadcasted_iota(jnp.int32, (128, SSD_WIDTH), 1)
    return (c // SSD_HEAD_DIM == r).astype(F32)


def _ssd_expand(dt, cs, prm, ex):
    cs_x = _dot_exact(cs, ex)
    cs_last_x = cs_x[CHUNK - 1:CHUNK, :]
    return (_dot_exact(dt, ex), _dot_exact(prm, ex)[2:3], jnp.exp(cs_x), jnp.exp(cs_last_x),
            jnp.exp(cs_last_x - cs_x))


def _ssd_fwd(xs, bc, dt_raw, z, prm, norm_w, ex):
    def body(xs_ref, bc_ref, dt_ref, z_ref, prm_ref, nw_ref, ex_ref, y_ref, yn_ref, prev_ref, state):
        c = pl.program_id(0)

        @pl.when(c == 0)
        def _():
            state[...] = jnp.zeros_like(state)

        prm = prm_ref[...]
        dt, a_row, cs, cs_t, causal, _, _ = _ssd_chunk_common(dt_ref[...], prm, c)
        dt_x, d_x, e_cs_x, e_last_x, dec_x = _ssd_expand(dt, cs, prm, ex_ref[...])
        xs_all = xs_ref[...]
        bc_all = bc_ref[...]
        xdt = xs_all * dt_x
        xdec = xdt * dec_x
        lane_lo = lax.broadcasted_iota(jnp.int32, (1, PAIR_W), 1) < SSD_HEAD_DIM
        for g in range(SSD_GROUPS):
            gs = slice(g * GROUP_W, (g + 1) * GROUP_W)
            b_g = bc_all[:, g * SSD_STATE:(g + 1) * SSD_STATE]
            c_g = bc_all[:, (SSD_GROUPS + g) * SSD_STATE:(SSD_GROUPS + g + 1) * SSD_STATE]
            st = state[g]
            prev_ref[0, g] = st
            y_off = _dot(c_g, st) * e_cs_x[:, gs]
            state[g] = st * e_last_x[:, gs] + _dot(b_g.T, xdec[:, gs])
            cb = _dot(c_g, b_g, NT)
            for k in range(SSD_HPG // 2):
                h0 = g * SSD_HPG + 2 * k
                ps = slice(h0 * SSD_HEAD_DIM, h0 * SSD_HEAD_DIM + PAIR_W)
                xdt_pair = xdt[:, ps]
                yd = []
                for h in (h0, h0 + 1):
                    lmat = jnp.where(causal, jnp.exp(cs[:, h:h + 1] - cs_t[h:h + 1, :]), 0.0)
                    yd.append(_dot(cb * lmat, xdt_pair))
                y_ref[:, ps] = (jnp.where(lane_lo, yd[0], yd[1]) + y_off[:, k * PAIR_W:(k + 1) * PAIR_W]
                                + xs_all[:, ps] * d_x[:, ps])
        yn_ref[...] = _gated_norm_fwd(y_ref[...], z_ref[...], nw_ref[...]).astype(_MXU)

    row = lambda w: pl.BlockSpec((CHUNK, w), lambda c: (c, 0))
    return pl.pallas_call(
        body, name="ssd_fwd", grid=(N_CHUNKS,),
        in_specs=[row(SSD_WIDTH), row(512), row(128), row(SSD_WIDTH),
                  pl.BlockSpec((8, 128), lambda c: (0, 0)), pl.BlockSpec((1, SSD_WIDTH), lambda c: (0, 0)),
                  pl.BlockSpec((128, SSD_WIDTH), lambda c: (0, 0))],
        out_specs=[row(SSD_WIDTH), row(SSD_WIDTH),
                   pl.BlockSpec((1,) + STATE_SHAPE, lambda c: (c, 0, 0, 0))],
        out_shape=[jax.ShapeDtypeStruct((T_ROWS, SSD_WIDTH), F32), jax.ShapeDtypeStruct((T_ROWS, SSD_WIDTH), _MXU),
                   jax.ShapeDtypeStruct((N_CHUNKS,) + STATE_SHAPE, F32)],
        scratch_shapes=[pltpu.VMEM(STATE_SHAPE, F32)],
        compiler_params=_cparams("arbitrary"),
    )(xs, bc, dt_raw, z, prm, norm_w, ex)


def _ssd_bwd(dyn, dyn_block, z, y_pre, xs, bc, dt_raw, prev, prm, norm_w, ex):
    def body(dyn_ref, z_ref, y_ref, xs_ref, bc_ref, dt_ref, prev_ref, prm_ref, nw_ref, ex_ref,
             dz_ref, dxs_ref, dbc_ref, ddt_ref, dprm_ref, dnw_ref, dstate):
        step = pl.program_id(0)
        c = N_CHUNKS - 1 - step

        @pl.when(step == 0)
        def _():
            dstate[...] = jnp.zeros_like(dstate)
            dprm_ref[...] = jnp.zeros_like(dprm_ref)
            dnw_ref[...] = jnp.zeros_like(dnw_ref)

        prm = prm_ref[...]
        dt, a_row, cs, cs_t, causal, tri, real = _ssd_chunk_common(dt_ref[...], prm, c)
        realf = real.astype(F32)
        z = z_ref[...]
        y_all = y_ref[...]
        nw = nw_ref[...]
        dyn_all = dyn_ref[...]
        sz = _silu(z)
        gated = y_all * sz
        half = SSD_WIDTH // SSD_GROUPS
        dgs, dnws = [], []
        for k in range(SSD_GROUPS):
            sl = slice(k * half, (k + 1) * half)
            dgk, dwk = _rms_bwd(gated[:, sl], nw[:, sl], dyn_all[:, sl])
            dgs.append(dgk)
            dnws.append(jnp.sum(dwk, axis=0, keepdims=True))
        dgated = jnp.concatenate(dgs, axis=1)
        dnw_ref[...] += jnp.concatenate(dnws, axis=1)
        dz_ref[...] = (dgated * y_all * _silu_grad(z)).astype(_MXU)
        dy_all = dgated * sz

        ex = ex_ref[...]
        dt_x, d_x, e_cs_x, e_last_x, dec_x = _ssd_expand(dt, cs, prm, ex)
        xs_all = xs_ref[...]
        bc_all = bc_ref[...]
        xdt = xs_all * dt_x
        xdt_mxu = xdt.astype(_MXU).astype(F32)
        xdec = xdt * dec_x
        dcp = dy_all * e_cs_x
        lane_lo = lax.broadcasted_iota(jnp.int32, (1, PAIR_W), 1) < SSD_HEAD_DIM
        upper = (lax.broadcasted_iota(jnp.int32, (CHUNK, CHUNK), 0)
                 <= lax.broadcasted_iota(jnp.int32, (CHUNK, CHUNK), 1))
        last_row = (lax.broadcasted_iota(jnp.int32, (CHUNK, 1), 0) == CHUNK - 1).astype(F32)
        dbs, dcs_, dxdt_parts, last_parts = [], [], [], []
        for g in range(SSD_GROUPS):
            gs = slice(g * GROUP_W, (g + 1) * GROUP_W)
            b_g = bc_all[:, g * SSD_STATE:(g + 1) * SSD_STATE]
            c_g = bc_all[:, (SSD_GROUPS + g) * SSD_STATE:(SSD_GROUPS + g + 1) * SSD_STATE]
            prev_t = prev_ref[0, g]
            dst = dstate[g]
            dc_g = _dot(dcp[:, gs], prev_t, NT)
            db_g = _dot(xdec[:, gs], dst, NT)
            dxdt_state = _dot(b_g, dst) * dec_x[:, gs]
            dstate[g] = dst * e_last_x[:, gs] + _dot(c_g.T, dcp[:, gs])
            last_parts.append(jnp.sum(xdt_mxu[:, gs] * dxdt_state, axis=0, keepdims=True)
                              + jnp.sum(dst * prev_t, axis=0, keepdims=True) * e_last_x[:, gs])
            cb_t = _dot(b_g, c_g, NT)
            dcb_t = jnp.zeros((CHUNK, CHUNK), F32)
            for k in range(SSD_HPG // 2):
                h0 = g * SSD_HPG + 2 * k
                ps = slice(h0 * SSD_HEAD_DIM, h0 * SSD_HEAD_DIM + PAIR_W)
                dy_pair = dy_all[:, ps]
                xdt_pair = xdt[:, ps]
                dd = []
                for h in (h0, h0 + 1):
                    lmat_t = jnp.where(upper, jnp.exp(cs_t[h:h + 1, :] - cs[:, h:h + 1]), 0.0)
                    dd.append(_dot(cb_t * lmat_t, dy_pair))
                    mine = lane_lo if h == h0 else jnp.logical_not(lane_lo)
                    dcb_t = dcb_t + _dot(jnp.where(mine, xdt_pair, 0.0), dy_pair, NT) * lmat_t
                dxdt_parts.append(jnp.where(lane_lo, dd[0], dd[1]) + dxdt_state[:, k * PAIR_W:(k + 1) * PAIR_W])
            dc_g = dc_g + _dot(dcb_t, b_g, TN)
            db_g = db_g + _dot(dcb_t, c_g)
            dbs.append(db_g * realf)
            dcs_.append(dc_g * realf)
        dbc_ref[...] = jnp.concatenate(dbs + dcs_, axis=1)
        dxdt = jnp.concatenate(dxdt_parts, axis=1)
        dxs_ref[...] = (dxdt * dt_x + dy_all * d_x) * realf
        ddt_all = _dot_exact(dxdt * xs_all, ex, NT)
        rows = jnp.concatenate([jnp.concatenate(last_parts, axis=1), jnp.sum(dy_all * xs_all, axis=0, keepdims=True),
                                jnp.zeros((6, SSD_WIDTH), F32)], axis=0)
        rows = _dot_exact(rows, ex, NT)
        dd_row = rows[1:2]
        dy_mxu = dy_all.astype(_MXU).astype(F32)
        dcs_all = (_dot_exact(dy_mxu * (y_all - xs_all * d_x), ex, NT) - _dot_exact(xdt_mxu * dxdt, ex, NT)
                   + last_row * rows[0:1])
        dda = _dot_exact(tri, dcs_all, TN)
        ddt = (ddt_all + dda * a_row) * realf
        ddt_raw = ddt * _sigmoid(dt_ref[...] + prm[0:1])
        ddt_ref[...] = ddt_raw.astype(_MXU)
        da_log = jnp.sum(dda * dt, axis=0, keepdims=True) * a_row
        dprm_ref[0:1, :] += jnp.sum(ddt_raw, axis=0, keepdims=True)
        dprm_ref[1:2, :] += da_log
        dprm_ref[2:3, :] += dd_row

    rev = lambda w, blk=0: pl.BlockSpec((CHUNK, w), lambda s, blk=blk: (N_CHUNKS - 1 - s, blk))
    return pl.pallas_call(
        body, name="ssd_bwd", grid=(N_CHUNKS,),
        in_specs=[rev(SSD_WIDTH, dyn_block), rev(SSD_WIDTH), rev(SSD_WIDTH), rev(SSD_WIDTH), rev(512), rev(128),
                  pl.BlockSpec((1,) + STATE_SHAPE, lambda s: (N_CHUNKS - 1 - s, 0, 0, 0)),
                  pl.BlockSpec((8, 128), lambda s: (0, 0)), pl.BlockSpec((1, SSD_WIDTH), lambda s: (0, 0)),
                  pl.BlockSpec((128, SSD_WIDTH), lambda s: (0, 0))],
        out_specs=[rev(SSD_WIDTH), rev(SSD_WIDTH), rev(512), rev(128),
                   pl.BlockSpec((8, 128), lambda s: (0, 0)), pl.BlockSpec((1, SSD_WIDTH), lambda s: (0, 0))],
        out_shape=[jax.ShapeDtypeStruct((T_ROWS, SSD_WIDTH), _MXU), jax.ShapeDtypeStruct((T_ROWS, SSD_WIDTH), F32),
                   jax.ShapeDtypeStruct((T_ROWS, 512), F32), jax.ShapeDtypeStruct((T_ROWS, 128), _MXU),
                   jax.ShapeDtypeStruct((8, 128), F32), jax.ShapeDtypeStruct((1, SSD_WIDTH), F32)],
        scratch_shapes=[pltpu.VMEM(STATE_SHAPE, F32)],
        compiler_params=_cparams("arbitrary"),
    )(dyn, z, y_pre, xs, bc, dt_raw, prev, prm, norm_w, ex)


LRU_PAIRS = 8


def _lru_gates(xr, wa_ref, wx_ref, prm):
    pre_r, pre_i = [], []
    for k in range(LRU_PAIRS):
        xk = xr[:, k * 128:(k + 1) * 128]
        pre_r.append(_dot(xk, wa_ref[k]))
        pre_i.append(_dot(xk, wx_ref[k]))
    r = _sigmoid(jnp.concatenate(pre_r, axis=1) + prm[0:1])
    i = _sigmoid(jnp.concatenate(pre_i, axis=1) + prm[1:2])
    sp = _softplus(-prm[2:3])
    log_a = (-LRU_C) * r * sp
    a = jnp.exp(log_a)
    s = jnp.sqrt(-jnp.tanh(log_a) * (a * a + 1.0))
    return r, i, a, s, sp


def _lru_fwd(xr, gate, wa, wx, prm):
    def body(xr_ref, g_ref, wa_ref, wx_ref, prm_ref, hs_ref, yn_ref, carry, a_s, u_s):
        @pl.when(pl.program_id(0) == 0)
        def _():
            carry[...] = jnp.zeros_like(carry)

        prm = prm_ref[...]
        xr_t = xr_ref[...]
        _, i, a, s, _ = _lru_gates(xr_t, wa_ref, wx_ref, prm)
        a_s[...] = a
        u_s[...] = s * (i * xr_t)
        rid = lax.broadcasted_iota(jnp.int32, (8, LRU_WIDTH), 0)

        def group(k, h):
            off = pl.multiple_of(k * 8, 8)
            a8 = a_s[pl.ds(off, 8), :]
            u8 = u_s[pl.ds(off, 8), :]
            out = jnp.zeros((8, LRU_WIDTH), F32)
            for r_ in range(8):
                h = a8[r_:r_ + 1] * h + u8[r_:r_ + 1]
                out = jnp.where(rid == r_, h, out)
            hs_ref[pl.ds(off, 8), :] = out
            return h

        carry[0:1, :] = lax.fori_loop(0, CHUNK // 8, group, carry[0:1, :])
        gel, _ = _gelu_and_grad(g_ref[...])
        yn_ref[...] = _rms_fwd(gel * hs_ref[...], prm[3:4]).astype(_MXU)

    row = pl.BlockSpec((CHUNK, LRU_WIDTH), lambda t: (t, 0))
    wspec = pl.BlockSpec((LRU_PAIRS, 128, 128), lambda t: (0, 0, 0))
    return pl.pallas_call(
        body, name="lru_fwd", grid=(N_CHUNKS,),
        in_specs=[row, row, wspec, wspec, pl.BlockSpec((8, LRU_WIDTH), lambda t: (0, 0))],
        out_specs=[row, row],
        out_shape=[jax.ShapeDtypeStruct((T_ROWS, LRU_WIDTH), F32), jax.ShapeDtypeStruct((T_ROWS, LRU_WIDTH), _MXU)],
        scratch_shapes=[pltpu.VMEM((8, LRU_WIDTH), F32), pltpu.VMEM((CHUNK, LRU_WIDTH), F32),
                        pltpu.VMEM((CHUNK, LRU_WIDTH), F32)],
        compiler_params=_cparams("arbitrary"),
    )(xr, gate, wa, wx, prm)


def _lru_bwd(dyn, dyn_block, gate, xr, hs, wa, wx, wa_t, wx_t, prm):
    def body(dyn_ref, g_ref, xr_ref, hs_ref, hsp_ref, wa_ref, wx_ref, wat_ref, wxt_ref, prm_ref,
             dg_ref, dxr_ref, dwa_ref, dwx_ref, dprm_ref, carry, a_s, d_s):
        step = pl.program_id(0)
        tile = N_CHUNKS - 1 - step

        @pl.when(step == 0)
        def _():
            carry[...] = jnp.zeros_like(carry)
            dwa_ref[...] = jnp.zeros_like(dwa_ref)
            dwx_ref[...] = jnp.zeros_like(dwx_ref)
            dprm_ref[...] = jnp.zeros_like(dprm_ref)

        prm = prm_ref[...]
        xr_t = xr_ref[...]
        r, i, a, s, sp = _lru_gates(xr_t, wa_ref, wx_ref, prm)
        hs_t = hs_ref[...]
        gel, dgel = _gelu_and_grad(g_ref[...])
        dy, dnw = _rms_bwd(gel * hs_t, prm[3:4], dyn_ref[...])
        dg_ref[...] = (dy * hs_t * dgel).astype(_MXU)
        a_s[...] = a
        d_s[...] = dy * gel
        rid = lax.broadcasted_iota(jnp.int32, (8, LRU_WIDTH), 0)

        def group(k, cr):
            off = pl.multiple_of((CHUNK // 8 - 1 - k) * 8, 8)
            a8 = a_s[pl.ds(off, 8), :]
            d8 = d_s[pl.ds(off, 8), :]
            out = jnp.zeros((8, LRU_WIDTH), F32)
            for r_ in reversed(range(8)):
                dht = d8[r_:r_ + 1] + cr
                out = jnp.where(rid == r_, dht, out)
                cr = a8[r_:r_ + 1] * dht
            d_s[pl.ds(off, 8), :] = out
            return cr

        carry[0:1, :] = lax.fori_loop(0, CHUNK // 8, group, carry[0:1, :])
        dht = d_s[...]
        before = hsp_ref[CHUNK - 8:CHUNK, :][7:8] * (tile > 0).astype(F32)
        first = lax.broadcasted_iota(jnp.int32, (CHUNK, 1), 0) == 0
        hprev = jnp.where(first, before, pltpu.roll(hs_t, 1, 0))
        da = dht * hprev
        ixr = i * xr_t
        ds = dht * ixr
        dlog_a = da * a - ds * (a * a) / s
        dr = dlog_a * ((-LRU_C) * sp)
        dsp = jnp.sum(dlog_a * ((-LRU_C) * r), axis=0, keepdims=True)
        dlam = dsp * (-_sigmoid(-prm[2:3]))
        di = dht * s * xr_t
        dpre_r = dr * r * (1.0 - r)
        dpre_i = di * i * (1.0 - i)
        dxr = dht * s * i
        parts = []
        for k in range(LRU_PAIRS):
            sl = slice(k * 128, (k + 1) * 128)
            parts.append(_dot(dpre_r[:, sl], wat_ref[k]) + _dot(dpre_i[:, sl], wxt_ref[k]))
            dwa_ref[k] += _dot(xr_t[:, sl], dpre_r[:, sl], TN)
            dwx_ref[k] += _dot(xr_t[:, sl], dpre_i[:, sl], TN)
        dxr_ref[...] = dxr + jnp.concatenate(parts, axis=1)
        dprm_ref[0:1, :] += jnp.sum(dpre_r, axis=0, keepdims=True)
        dprm_ref[1:2, :] += jnp.sum(dpre_i, axis=0, keepdims=True)
        dprm_ref[2:3, :] += dlam
        dprm_ref[3:4, :] += jnp.sum(dnw, axis=0, keepdims=True)

    rev = lambda blk=0: pl.BlockSpec((CHUNK, LRU_WIDTH), lambda s, blk=blk: (N_CHUNKS - 1 - s, blk))
    wspec = pl.BlockSpec((LRU_PAIRS, 128, 128), lambda s: (0, 0, 0))
    return pl.pallas_call(
        body, name="lru_bwd", grid=(N_CHUNKS,),
        in_specs=[rev(dyn_block), rev(), rev(), rev(),
                  pl.BlockSpec((CHUNK, LRU_WIDTH), lambda s: (jnp.maximum(N_CHUNKS - 2 - s, 0), 0)),
                  wspec, wspec, wspec, wspec, pl.BlockSpec((8, LRU_WIDTH), lambda s: (0, 0))],
        out_specs=[rev(), rev(), wspec, wspec, pl.BlockSpec((8, LRU_WIDTH), lambda s: (0, 0))],
        out_shape=[jax.ShapeDtypeStruct((T_ROWS, LRU_WIDTH), _MXU), jax.ShapeDtypeStruct((T_ROWS, LRU_WIDTH), F32),
                   jax.ShapeDtypeStruct((LRU_PAIRS, 128, 128), F32), jax.ShapeDtypeStruct((LRU_PAIRS, 128, 128), F32),
                   jax.ShapeDtypeStruct((8, LRU_WIDTH), F32)],
        scratch_shapes=[pltpu.VMEM((8, LRU_WIDTH), F32), pltpu.VMEM((CHUNK, LRU_WIDTH), F32),
                        pltpu.VMEM((CHUNK, LRU_WIDTH), F32)],
        compiler_params=_cparams("arbitrary"),
    )(dyn, gate, xr, hs, hs, wa, wx, wa_t, wx_t, prm)


SEC_NAMES = ("z", "xs", "bc", "dt", "g", "x")
SEC_WIDTH = {"z": 1024, "xs": 1024, "bc": 512, "dt": 128, "g": 1024, "x": 1024}


def _pair_blocks(w):
    w = w.reshape(LRU_PAIRS, 2, 64, 64)
    zero = jnp.zeros((LRU_PAIRS, 64, 64), w.dtype)
    top = jnp.concatenate([w[:, 0], zero], axis=2)
    bot = jnp.concatenate([zero, w[:, 1]], axis=2)
    return jnp.concatenate([top, bot], axis=1)


def _unpair_blocks(wp):
    return jnp.stack([wp[:, :64, :64], wp[:, 64:, 64:]], axis=1).reshape(16, 64, 64)


def _pad_lanes(v, width=128):
    return jnp.pad(v, ((0, 0), (0, width - v.shape[1])))


class _Resident:
    def __init__(self, w_out, w_gate, w_up, w_down):
        self._w_out, self._ffn = w_out, (w_gate, w_up, w_down)

    def w_out(self, after):
        return self._w_out

    def ffn(self, after):
        return self._ffn

    def grads_ready(self, names, g, g_mxu):
        return jnp.zeros((1, 1), F32)

    def small_ready(self, g, loss):
        pass


def _local_step(x, target, meta, p, late):
    g, g_mxu = {}, {}
    ex = _head_expander()
    h0 = _embed(x, meta)
    u1 = _rmsnorm(h0, p["norm1_w"], name="norm1")
    proj = {}
    for s in SEC_NAMES:
        wdt = SEC_WIDTH[s]
        proj[s] = _mm([(u1, 0, p["w_in_" + s], 0, D_MODEL)], T_ROWS, wdt, tm=544, tn=min(wdt, 512), mode="nt",
                      out_dtype=F32, name="proj_" + s)
    ssd_prm = jnp.concatenate([_pad_lanes(p["ssd_dt_bias"]), _pad_lanes(p["ssd_a_log"]), _pad_lanes(p["ssd_d"]),
                               jnp.zeros((5, 128), F32)], axis=0)
    xs_act = _conv_fwd(proj["xs"], p["ssd_conv_w"][:, :SSD_WIDTH], p["ssd_conv_b"][:, :SSD_WIDTH], silu=True,
                       name="ssd_conv_xs")
    bc_act = _conv_fwd(proj["bc"], p["ssd_conv_w"][:, SSD_WIDTH:], p["ssd_conv_b"][:, SSD_WIDTH:], silu=True,
                       name="ssd_conv_bc")
    y_pre, y_ssd, prev = _ssd_fwd(xs_act, bc_act, proj["dt"], proj["z"], ssd_prm, p["ssd_norm_w"], ex)
    xr = _conv_fwd(proj["x"], p["lru_conv_w"], p["lru_conv_b"], silu=False, name="lru_conv")
    wa_p, wx_p = _pair_blocks(p["lru_wa"]), _pair_blocks(p["lru_wx"])
    lru_prm = jnp.concatenate([p["lru_ba"], p["lru_bx"], p["lru_lambda"], p["lru_norm_w"],
                               jnp.zeros((4, LRU_WIDTH), F32)], axis=0)
    hs, y_lru = _lru_fwd(xr, proj["g"], wa_p.astype(_MXU), wx_p.astype(_MXU), lru_prm)
    ycat = jnp.concatenate([y_ssd, y_lru], axis=1)
    w_out = late.w_out(ycat)
    h1 = _mm([(ycat, 0, w_out, 0, 2 * D_MODEL)], T_ROWS, D_MODEL, tm=544, tn=512, mode="nn", out_dtype=F32,
             name="out_proj", residual=h0)
    u2 = _rmsnorm(h1, p["norm2_w"], name="norm2")
    w_gate, w_up, w_down = late.ffn(u2)
    gp, up, act = _ffn_up(u2, w_gate, w_up)
    h2 = _mm([(act, 0, w_down, 0, D_FF)], T_ROWS, D_MODEL, tm=544, tn=512, mode="nn", out_dtype=F32,
             name="ffn_down", residual=h1)
    loss, dh2, dh2b, g["final_norm_w"] = _loss_head(h2, target, p["final_norm_w"])
    dgp, dup = _ffn_bwd_act(dh2b, w_down, gp, up)
    g["w_down"], g_mxu["w_down"] = _mm([(act, 0, dh2b, 0, T_ROWS)], D_FF, D_MODEL, tm=1408, tn=512, mode="tn",
                                       out_dtype=F32, name="dw_down", also_mxu=True)
    dh1, dh1b, g["norm2_w"] = _mm_norm_bwd([(dgp, w_gate, D_FF), (dup, w_up, D_FF)], h1, p["norm2_w"], dh2,
                                           name="ffn_bwd_in")
    g["w_gate"], g_mxu["w_gate"] = _mm([(dgp, 0, u2, 0, T_ROWS)], D_FF, D_MODEL, tm=1408, tn=512, mode="tn",
                                       out_dtype=F32, name="dw_gate", also_mxu=True)
    g["w_up"], g_mxu["w_up"] = _mm([(dup, 0, u2, 0, T_ROWS)], D_FF, D_MODEL, tm=1408, tn=512, mode="tn",
                                   out_dtype=F32, name="dw_up", also_mxu=True)
    lru_prm = lru_prm + late.grads_ready(("w_down", "w_gate", "w_up"), g, g_mxu)
    dycat = _mm([(dh1b, 0, w_out, 0, D_MODEL)], T_ROWS, 2 * D_MODEL, tm=544, tn=512, mode="nt", out_dtype=F32,
                name="out_proj_bwd")
    g["w_out"], g_mxu["w_out"] = _mm([(ycat, 0, dh1b, 0, T_ROWS)], 2 * D_MODEL, D_MODEL, tm=512, tn=512, mode="tn",
                                     out_dtype=F32, name="dw_out", also_mxu=True)
    ssd_prm = ssd_prm + late.grads_ready(("w_out",), g, g_mxu)
    dgate, dxr, dwa_p, dwx_p, dlru_prm = _lru_bwd(dycat, 1, proj["g"], xr, hs, wa_p.astype(_MXU), wx_p.astype(_MXU),
                                                  jnp.swapaxes(wa_p, 1, 2).astype(_MXU),
                                                  jnp.swapaxes(wx_p, 1, 2).astype(_MXU), lru_prm)
    g["lru_wa"], g["lru_wx"] = _unpair_blocks(dwa_p), _unpair_blocks(dwx_p)
    g["lru_ba"], g["lru_bx"], g["lru_lambda"], g["lru_norm_w"] = (dlru_prm[k:k + 1] for k in range(4))
    dx_lru, g["lru_conv_w"], g["lru_conv_b"] = _conv_bwd(dxr, proj["x"], p["lru_conv_w"], p["lru_conv_b"], silu=False,
                                                         name="lru_conv_bwd")
    dz, dxs_act, dbc_act, ddt, dssd_prm, g["ssd_norm_w"] = _ssd_bwd(dycat, 0, proj["z"], y_pre, xs_act, bc_act,
                                                                    proj["dt"], prev, ssd_prm, p["ssd_norm_w"], ex)
    g["ssd_dt_bias"], g["ssd_a_log"], g["ssd_d"] = (dssd_prm[k:k + 1, :SSD_HEADS] for k in range(3))
    dxs, dcw_xs, dcb_xs = _conv_bwd(dxs_act, proj["xs"], p["ssd_conv_w"][:, :SSD_WIDTH],
                                    p["ssd_conv_b"][:, :SSD_WIDTH], silu=True, name="ssd_conv_xs_bwd")
    dbc, dcw_bc, dcb_bc = _conv_bwd(dbc_act, proj["bc"], p["ssd_conv_w"][:, SSD_WIDTH:],
                                    p["ssd_conv_b"][:, SSD_WIDTH:], silu=True, name="ssd_conv_bc_bwd")
    g["ssd_conv_w"] = jnp.concatenate([dcw_xs, dcw_bc], axis=1)
    g["ssd_conv_b"] = jnp.concatenate([dcb_xs, dcb_bc], axis=1)
    dproj = {"z": dz, "xs": dxs, "bc": dbc, "dt": ddt, "g": dgate, "x": dx_lru}
    dh0, _, g["norm1_w"] = _mm_norm_bwd([(dproj[s], p["w_in_" + s], SEC_WIDTH[s]) for s in SEC_NAMES], h0,
                                        p["norm1_w"], dh1, name="in_proj_bwd")
    g["meta_tokens"] = dh0[PAD_ROWS:X_ROW0]
    late.small_ready(g, loss)
    for s in SEC_NAMES:
        wdt = SEC_WIDTH[s]
        g["w_in_" + s], g_mxu["w_in_" + s] = _mm([(dproj[s], 0, u1, 0, T_ROWS)], wdt, D_MODEL, tm=min(wdt, 512),
                                                 tn=512, mode="tn", out_dtype=F32, name="dw_in_" + s, also_mxu=True)
    return loss, dh0[X_ROW0:], g, g_mxu


MESH = pl.DeviceIdType.MESH
ANY = pl.BlockSpec(memory_space=pl.ANY)


def _my_place():
    return lax.axis_index("x"), lax.axis_index("y"), lax.axis_index("c")


def _other_chips(x, y):
    return [(1 - x, y), (x, 1 - y), (1 - x, 1 - y)]


def _gather_shards(shards):
    n = len(shards)

    def body(*refs):
        ins, outs = refs[:n], refs[n:2 * n]
        send_sems, recv_sems, local_sems = refs[2 * n:]
        x, y, c = _my_place()
        me = 2 * x + y
        peers = _other_chips(x, y)
        local = [pltpu.make_async_copy(ins[k], outs[k].at[me], local_sems.at[k]) for k in range(n)]
        for cp in local:
            cp.start()
        for k in range(n):
            for j, (px, py) in enumerate(peers):
                pltpu.make_async_remote_copy(
                    src_ref=ins[k], dst_ref=outs[k].at[me], send_sem=send_sems.at[3 * k + j],
                    recv_sem=recv_sems.at[3 * k + j], device_id=(px, py, c), device_id_type=MESH).start()
        for k in range(n):
            for j, (px, py) in enumerate(peers):
                pltpu.make_async_remote_copy(
                    src_ref=ins[k], dst_ref=outs[k].at[2 * px + py], send_sem=send_sems.at[3 * k + j],
                    recv_sem=recv_sems.at[3 * k + j], device_id=(px, py, c), device_id_type=MESH).wait()
        for cp in local:
            cp.wait()

    return pl.pallas_call(
        body, name="gather_weights", in_specs=[ANY] * n, out_specs=[ANY] * n,
        out_shape=[jax.ShapeDtypeStruct((N_SHARDS,) + s.shape, s.dtype) for s in shards],
        scratch_shapes=[pltpu.SemaphoreType.DMA((3 * n,)), pltpu.SemaphoreType.DMA((3 * n,)),
                        pltpu.SemaphoreType.DMA((n,))],
    )(*shards)


def _scatter_grads(grads4):
    n = len(grads4)

    def body(*refs):
        ins, outs = refs[:n], refs[n:2 * n]
        send_sems, recv_sems = refs[2 * n:]
        x, y, c = _my_place()
        peers = _other_chips(x, y)
        for k in range(n):
            for j, (px, py) in enumerate(peers):
                pltpu.make_async_remote_copy(
                    src_ref=ins[k].at[2 * px + py], dst_ref=outs[k].at[j], send_sem=send_sems.at[3 * k + j],
                    recv_sem=recv_sems.at[3 * k + j], device_id=(px, py, c), device_id_type=MESH).start()
        for k in range(n):
            for j, (px, py) in enumerate(peers):
                pltpu.make_async_remote_copy(
                    src_ref=ins[k].at[2 * px + py], dst_ref=outs[k].at[j], send_sem=send_sems.at[3 * k + j],
                    recv_sem=recv_sems.at[3 * k + j], device_id=(px, py, c), device_id_type=MESH).wait()

    return pl.pallas_call(
        body, name="scatter_grads", in_specs=[ANY] * n, out_specs=[ANY] * n,
        out_shape=[jax.ShapeDtypeStruct((3,) + g.shape[1:], g.dtype) for g in grads4],
        scratch_shapes=[pltpu.SemaphoreType.DMA((3 * n,)), pltpu.SemaphoreType.DMA((3 * n,))],
    )(*grads4)


HBM_SPEC = pl.BlockSpec(memory_space=pltpu.HBM)
SEM_SPEC = pl.BlockSpec(memory_space=pltpu.SEMAPHORE)
SPLIT_EFFECT = pltpu.SideEffectType.DATAFLOW_SIDE_EFFECTING


def _gather_plan(bufs, x, y, c, incoming):
    plan = []
    for buf in bufs:
        for (px, py) in _other_chips(x, y):
            slot = 2 * px + py if incoming else 2 * x + y
            plan.append((buf.at[2 * x + y], buf.at[slot], (px, py, c)))
    return plan


def _scatter_plan(bufs, x, y, c, incoming):
    n = len(bufs) // 2
    plan = []
    for k in range(n):
        for j, (px, py) in enumerate(_other_chips(x, y)):
            plan.append((bufs[k].at[2 * px + py], bufs[n + k].at[j], (px, py, c)))
    return plan


def _split_start(bufs, plan, n_copies, after, *, name):
    n = len(bufs)

    def body(*refs):
        ins = refs[:n]
        send_sems, recv_sems = refs[n + 1], refs[n + 2]
        token = refs[-1]
        x, y, c = _my_place()
        for i, (src, dst, dev) in enumerate(plan(ins, x, y, c, False)):
            pltpu.make_async_remote_copy(src_ref=src, dst_ref=dst, send_sem=send_sems.at[i], recv_sem=recv_sems.at[i],
                                         device_id=dev, device_id_type=MESH).start()
        token[...] = jnp.zeros_like(token)

    outs = pl.pallas_call(
        body, name=name,
        out_shape=(pltpu.SemaphoreType.DMA((n_copies,)), pltpu.SemaphoreType.DMA((n_copies,)),
                   *[pltpu.HBM(b.shape, b.dtype) for b in bufs], jax.ShapeDtypeStruct((8, 128), F32)),
        in_specs=[HBM_SPEC] * n + [ANY],
        out_specs=(SEM_SPEC, SEM_SPEC, *[HBM_SPEC] * n, pl.BlockSpec(memory_space=pltpu.VMEM)),
        input_output_aliases={k: 2 + k for k in range(n)},
        compiler_params=pltpu.CompilerParams(has_side_effects=SPLIT_EFFECT),
    )(*[pltpu.with_memory_space_constraint(b, pltpu.HBM) for b in bufs], after)
    return outs[0], outs[1], list(outs[2:2 + n]), outs[-1]


def _split_wait(bufs, send_sems, recv_sems, plan, after, *, name):
    n = len(bufs)

    def body(*refs):
        ins = refs[:n]
        send_sems_ref, recv_sems_ref = refs[n], refs[n + 1]
        x, y, c = _my_place()
        for i, (src, dst, dev) in enumerate(plan(ins, x, y, c, True)):
            cp = pltpu.make_async_remote_copy(src_ref=src, dst_ref=dst, send_sem=send_sems_ref.at[i],
                                              recv_sem=recv_sems_ref.at[i], device_id=dev, device_id_type=MESH)
            cp.wait_send()
            cp.wait_recv()

    outs = pl.pallas_call(
        body, name=name, out_shape=tuple(pltpu.HBM(b.shape, b.dtype) for b in bufs),
        in_specs=[HBM_SPEC] * n + [SEM_SPEC, SEM_SPEC, ANY], out_specs=tuple([HBM_SPEC] * n),
        input_output_aliases={k: k for k in range(n)},
        compiler_params=pltpu.CompilerParams(has_side_effects=SPLIT_EFFECT),
    )(*bufs, send_sems, recv_sems, after)
    return list(outs)


def _fill_own_slot(shard, me_arr, *, name):
    r, c = shard.shape
    tile, steps, imap = _elementwise_tile(r, c)

    def body(me_ref, x_ref, o_ref):
        o_ref[0] = x_ref[...].astype(_MXU)

    return pl.pallas_call(
        body, name=name,
        grid_spec=pltpu.PrefetchScalarGridSpec(
            num_scalar_prefetch=1, grid=(steps,),
            in_specs=[pl.BlockSpec(tile, lambda i, me: imap(i))],
            out_specs=pl.BlockSpec((1,) + tile, lambda i, me: (me[0],) + imap(i))),
        out_shape=jax.ShapeDtypeStruct((N_SHARDS, r, c), _MXU),
        compiler_params=_cparams("parallel"),
    )(me_arr, shard)


def _swap_with_sibling(parts):
    n = len(parts)

    def body(*refs):
        ins, outs = refs[:n], refs[n:2 * n]
        send_sems, recv_sems = refs[2 * n:]
        x, y, c = _my_place()
        copies = [pltpu.make_async_remote_copy(
            src_ref=ins[k], dst_ref=outs[k], send_sem=send_sems.at[k], recv_sem=recv_sems.at[k],
            device_id=(x, y, 1 - c), device_id_type=MESH) for k in range(n)]
        for cp in copies:
            cp.start()
        for cp in copies:
            cp.wait()

    return pl.pallas_call(
        body, name="swap_with_sibling", in_specs=[ANY] * n, out_specs=[ANY] * n,
        out_shape=[jax.ShapeDtypeStruct(a.shape, a.dtype) for a in parts],
        scratch_shapes=[pltpu.SemaphoreType.DMA((n,)), pltpu.SemaphoreType.DMA((n,))],
    )(*parts)


def _everyone_plan(bufs, x, y, c, incoming):
    pack, land = bufs
    me = 4 * x + 2 * y + c
    plan = []
    for mask in range(1, N_DEV):
        px, py, pc = x ^ (mask >> 2 & 1), y ^ (mask >> 1 & 1), c ^ (mask & 1)
        plan.append((pack, land.at[4 * px + 2 * py + pc if incoming else me], (px, py, pc)))
    return plan


def _sum_devices(pack, land, dev_arr, *, name):
    def body(dev_ref, own_ref, land_ref, o_ref):
        dev = dev_ref[0]
        acc = None
        for d in range(N_DEV):
            term = jnp.where(dev == d, own_ref[...], land_ref[d])
            acc = term if acc is None else acc + term
        o_ref[...] = acc

    vmem = pl.BlockSpec(memory_space=pltpu.VMEM)
    return pl.pallas_call(
        body, name=name, in_specs=[pl.BlockSpec(memory_space=pltpu.SMEM), vmem, vmem], out_specs=vmem,
        out_shape=jax.ShapeDtypeStruct(pack.shape, F32),
        compiler_params=pltpu.CompilerParams(vmem_limit_bytes=VMEM_LIMIT_BYTES),
    )(dev_arr, pack, land)


def _adamw_native(ws, gs, ms, vs):
    n = len(ws)

    def body(*refs):
        for k in range(n):
            w_ref, g_ref, m_ref, v_ref = (refs[j * n + k] for j in range(4))
            delta, m_new, v_new = _adamw_math(w_ref[...], g_ref[...], m_ref[...], v_ref[...])
            refs[4 * n + k][...] = delta
            refs[5 * n + k][...] = m_new
            refs[6 * n + k][...] = v_new

    vmem = pl.BlockSpec(memory_space=pltpu.VMEM)
    shapes = [jax.ShapeDtypeStruct(a.shape, F32) for a in ws]
    outs = pl.pallas_call(
        body, name="adamw_small", in_specs=[vmem] * (4 * n), out_specs=[vmem] * (3 * n), out_shape=shapes * 3,
        compiler_params=pltpu.CompilerParams(vmem_limit_bytes=VMEM_LIMIT_BYTES),
    )(*ws, *gs, *ms, *vs)
    return outs[:n], outs[n:2 * n], outs[2 * n:]


def _elementwise_tile(rows, cols, limit=256):
    for t in range(limit, 15, -16):
        if rows % t == 0:
            return (t, cols), rows // t, lambda i: (i, 0)
    assert cols % limit == 0
    return (rows, limit), cols // limit, lambda i: (0, i)


def _partial_sum(g4, land, me_arr, *, name):
    _, r, c = g4.shape
    tile, steps, imap = _elementwise_tile(r, c)

    def body(me_ref, own_ref, land_ref, o_ref):
        acc = own_ref[0]
        for j in range(3):
            acc = acc + land_ref[j].astype(F32)
        o_ref[...] = acc

    return pl.pallas_call(
        body, name=name,
        grid_spec=pltpu.PrefetchScalarGridSpec(
            num_scalar_prefetch=1, grid=(steps,),
            in_specs=[pl.BlockSpec((1,) + tile, lambda i, me: (me[0],) + imap(i)),
                      pl.BlockSpec((3,) + tile, lambda i, me: (0,) + imap(i))],
            out_specs=pl.BlockSpec(tile, lambda i, me: imap(i))),
        out_shape=jax.ShapeDtypeStruct((r, c), F32),
        compiler_params=_cparams("parallel"),
    )(me_arr, g4, land)


def _adamw_math(w, g, m, v):
    m = ADAM_B1 * m + (1.0 - ADAM_B1) * g
    v = ADAM_B2 * v + (1.0 - ADAM_B2) * (g * g)
    m_hat = m / (1.0 - ADAM_B1 ** ADAM_STEP)
    v_hat = v / (1.0 - ADAM_B2 ** ADAM_STEP)
    delta = -ADAM_LR * (m_hat / (jnp.sqrt(v_hat) + ADAM_EPS) + ADAM_WD * w)
    return delta, m, v


def _adamw(w, grad_parts, m, v, *, name):
    r, c = w.shape
    tile_shape, steps, imap = _elementwise_tile(r, c)
    n = len(grad_parts)

    def body(*refs):
        w_ref, m_ref, v_ref = refs[:3]
        g_refs = refs[3:3 + n]
        g_out, d_out, m_out, v_out = refs[3 + n:]
        g = g_refs[0][...]
        for k in range(1, n):
            g = g + g_refs[k][...]
        delta, m_new, v_new = _adamw_math(w_ref[...], g, m_ref[...], v_ref[...])
        g_out[...] = g
        d_out[...] = delta
        m_out[...] = m_new
        v_out[...] = v_new

    tile = pl.BlockSpec(tile_shape, imap)
    return pl.pallas_call(
        body, name=name, grid=(steps,), in_specs=[tile] * (3 + n), out_specs=[tile] * 4,
        out_shape=[jax.ShapeDtypeStruct((r, c), F32)] * 4,
        compiler_params=_cparams("parallel"),
    )(w, m, v, *grad_parts)


WEIGHT_NAMES = ("meta_tokens", "norm1_w", "w_in", "ssd_conv_w", "ssd_conv_b", "ssd_dt_bias", "ssd_a_log", "ssd_d",
                "ssd_norm_w", "lru_conv_w", "lru_conv_b", "lru_wa", "lru_ba", "lru_wx", "lru_bx", "lru_lambda",
                "lru_norm_w", "w_out", "norm2_w", "w_gate", "w_up", "w_down", "final_norm_w")
BIG = ("w_in", "w_out", "w_gate", "w_up", "w_down")
FFN = ("w_gate", "w_up", "w_down")
LATE = ("w_out",) + FFN
SMALL_SHARDED = {"meta_tokens": (N_META, D_MODEL), "ssd_conv_w": (CONV_K, 1536), "lru_conv_w": (CONV_K, LRU_WIDTH)}
SMALL = tuple(n for n in WEIGHT_NAMES if n not in BIG)
PACK_COLS = 1024


def _pack(arrays):
    flat = jnp.concatenate([a.reshape(-1) for a in arrays])
    rows = -(-flat.shape[0] // (8 * PACK_COLS)) * 8
    return jnp.pad(flat, (0, rows * PACK_COLS - flat.shape[0])).reshape(rows, PACK_COLS)


def _unpack(pack, shapes):
    flat = pack.reshape(-1)
    out, off = [], 0
    for s in shapes:
        size = math.prod(s)
        out.append(flat[off:off + size].reshape(s))
        off += size
    return out


def _unshard_cols(g4):
    return jnp.swapaxes(g4, 0, 1).reshape(g4.shape[1], -1)


COL_SHARDED = ("w_in", "w_gate", "w_up")
IN_ROWS = {"z": (0, 1024), "xs": (1024, 2048), "bc": (2048, 2560), "dt": (2560, 2576), "g": (2576, 3600),
           "x": (3600, IN_COLS)}


def _rows_view(name, block):
    return jnp.swapaxes(block[0], 0, 1) if name in COL_SHARDED else block[0]


def _param_view(name, rows):
    return (jnp.swapaxes(rows, 0, 1) if name in COL_SHARDED else rows)[None]


def kernel(x, meta_tokens, norm1_w, w_in, ssd_conv_w, ssd_conv_b, ssd_dt_bias, ssd_a_log, ssd_d, ssd_norm_w, lru_conv_w, lru_conv_b, lru_wa, lru_ba, lru_wx, lru_bx, lru_lambda, lru_norm_w, w_out, norm2_w, w_gate, w_up, w_down, final_norm_w, loss_target, m_meta_tokens, m_norm1_w, m_w_in, m_ssd_conv_w, m_ssd_conv_b, m_ssd_dt_bias, m_ssd_a_log, m_ssd_d, m_ssd_norm_w, m_lru_conv_w, m_lru_conv_b, m_lru_wa, m_lru_ba, m_lru_wx, m_lru_bx, m_lru_lambda, m_lru_norm_w, m_w_out, m_norm2_w, m_w_gate, m_w_up, m_w_down, m_final_norm_w, v_meta_tokens, v_norm1_w, v_w_in, v_ssd_conv_w, v_ssd_conv_b, v_ssd_dt_bias, v_ssd_a_log, v_ssd_d, v_ssd_norm_w, v_lru_conv_w, v_lru_conv_b, v_lru_wa, v_lru_ba, v_lru_wx, v_lru_bx, v_lru_lambda, v_lru_norm_w, v_w_out, v_norm2_w, v_w_gate, v_w_up, v_w_down, v_final_norm_w):
    w = dict(zip(WEIGHT_NAMES, (meta_tokens, norm1_w, w_in, ssd_conv_w, ssd_conv_b, ssd_dt_bias, ssd_a_log, ssd_d, ssd_norm_w, lru_conv_w, lru_conv_b, lru_wa, lru_ba, lru_wx, lru_bx, lru_lambda, lru_norm_w, w_out, norm2_w, w_gate, w_up, w_down, final_norm_w)))
    m = dict(zip(WEIGHT_NAMES, (m_meta_tokens, m_norm1_w, m_w_in, m_ssd_conv_w, m_ssd_conv_b, m_ssd_dt_bias, m_ssd_a_log, m_ssd_d, m_ssd_norm_w, m_lru_conv_w, m_lru_conv_b, m_lru_wa, m_lru_ba, m_lru_wx, m_lru_bx, m_lru_lambda, m_lru_norm_w, m_w_out, m_norm2_w, m_w_gate, m_w_up, m_w_down, m_final_norm_w)))
    v = dict(zip(WEIGHT_NAMES, (v_meta_tokens, v_norm1_w, v_w_in, v_ssd_conv_w, v_ssd_conv_b, v_ssd_dt_bias, v_ssd_a_log, v_ssd_d, v_ssd_norm_w, v_lru_conv_w, v_lru_conv_b, v_lru_wa, v_lru_ba, v_lru_wx, v_lru_bx, v_lru_lambda, v_lru_norm_w, v_w_out, v_norm2_w, v_w_gate, v_w_up, v_w_down, v_final_norm_w)))
    me = 2 * lax.axis_index("x") + lax.axis_index("y")

    big2d = {n: _rows_view(n, w[n]) for n in BIG}
    small_local = jnp.concatenate([w["meta_tokens"].reshape(-1), w["ssd_conv_w"].reshape(-1),
                                   w["lru_conv_w"].reshape(-1)])[None]
    me_arr = me.astype(jnp.int32).reshape(1)
    dev_arr = (2 * me + lax.axis_index("c")).astype(jnp.int32).reshape(1)
    w_in4, small4 = _gather_shards([big2d["w_in"].astype(_MXU), small_local])
    w_in_full = w_in4.reshape(-1, D_MODEL)
    sm = small4[:, 0]
    meta_full = _unshard_cols(sm[:, :4096].reshape(N_SHARDS, N_META, 256))
    ssd_conv_w_full = _unshard_cols(sm[:, 4096:5632].reshape(N_SHARDS, CONV_K, 384))
    lru_conv_w_full = _unshard_cols(sm[:, 5632:].reshape(N_SHARDS, CONV_K, 256))
    slots = {n: _fill_own_slot(big2d[n], me_arr, name="own_slot_" + n) for n in LATE}
    out_send, out_recv, out_bufs, tok_a = _split_start([slots["w_out"]], _gather_plan, 3, small4,
                                                       name="gather_w_out_start")
    ffn_send, ffn_recv, ffn_bufs, tok_b = _split_start([slots[n] for n in FFN], _gather_plan, 9, tok_a,
                                                       name="gather_ffn_start")

    p = {"w_in_" + s: w_in_full[lo:hi] for s, (lo, hi) in IN_ROWS.items()}
    p["w_in_dt"] = jnp.pad(p["w_in_dt"], ((0, SEC_WIDTH["dt"] - SSD_HEADS), (0, 0)))
    p.update({"ssd_conv_w": ssd_conv_w_full, "lru_conv_w": lru_conv_w_full,
              "lru_wa": w["lru_wa"][0], "lru_wx": w["lru_wx"][0], "final_norm_w": w["final_norm_w"][None]})
    for n in ("norm1_w", "ssd_conv_b", "ssd_dt_bias", "ssd_a_log", "ssd_d", "ssd_norm_w", "lru_conv_b", "lru_ba",
              "lru_bx", "lru_lambda", "lru_norm_w", "norm2_w"):
        p[n] = w[n]
    p["norm1_w"] = p["norm1_w"] + tok_b[:1, :1]

    class Late:
        def __init__(self):
            self.pending = []

        def w_out(self, after):
            (buf,) = _split_wait(out_bufs, out_send, out_recv, _gather_plan, after, name="gather_w_out_wait")
            return buf.reshape(-1, D_MODEL)

        def ffn(self, after):
            bufs = _split_wait(ffn_bufs, ffn_send, ffn_recv, _gather_plan, after, name="gather_ffn_wait")
            return tuple(b.reshape(-1, D_MODEL) for b in bufs)

        def grads_ready(self, names, g, g_mxu):
            srcs = [g_mxu[n].reshape(N_SHARDS, -1, D_MODEL) for n in names]
            lands = [lax.empty((3,) + s.shape[1:], _MXU) for s in srcs]
            tag = "_".join(names)
            send, recv, bufs, tok = _split_start(srcs + lands, _scatter_plan, 3 * len(names), g[names[-1]],
                                                 name="scatter_" + tag + "_start")
            self.pending.append((names, send, recv, bufs, tag))
            return tok[:1, :1]

        def landed(self, after):
            land = {}
            for names, send, recv, bufs, tag in self.pending:
                bufs = _split_wait(bufs, send, recv, _scatter_plan, after, name="scatter_" + tag + "_wait")
                land.update(zip(names, bufs[len(names):]))
            return land

        def small_ready(self, g, loss):
            pack = _pack([g[n] for n in SMALL] + [loss[0, :1]])
            land = lax.empty((N_DEV,) + pack.shape, F32)
            self.small = _split_start([pack, land], _everyone_plan, N_DEV - 1, loss, name="all_reduce_small_start")

        def small_sum(self, after):
            send, recv, bufs, _ = self.small
            pack, land = _split_wait(bufs, send, recv, _everyone_plan, after, name="all_reduce_small_wait")
            return _sum_devices(pack, land, dev_arr, name="all_reduce_small_sum")

    late = Late()

    loss, grad_x, g, g_mxu = _local_step(x[0], loss_target[0], meta_full, p, late)

    g["w_in"] = jnp.concatenate([g["w_in_" + s][:hi - lo] for s, (lo, hi) in IN_ROWS.items()], axis=0)
    g_mxu["w_in"] = jnp.concatenate([g_mxu["w_in_" + s][:hi - lo] for s, (lo, hi) in IN_ROWS.items()], axis=0)
    g4 = {n: g[n].reshape(N_SHARDS, -1, D_MODEL) for n in BIG}
    (land_w_in,) = _scatter_grads([g_mxu["w_in"].reshape(N_SHARDS, -1, D_MODEL)])
    land = late.landed(land_w_in)
    land["w_in"] = land_w_in
    part = {n: _partial_sum(g4[n], land[n], me_arr, name="partial_" + n) for n in BIG}
    sib = dict(zip(BIG, _swap_with_sibling([part[n] for n in BIG])))

    small_full_shape = {n: (SMALL_SHARDED[n] if n in SMALL_SHARDED else w[n].shape) for n in SMALL}
    red_list = _unpack(late.small_sum(sib["w_in"]), [small_full_shape[n] for n in SMALL] + [(1,)])
    loss_total = red_list[-1][0]
    g_small = {}
    for n, arr in zip(SMALL, red_list[:-1]):
        if n in SMALL_SHARDED:
            cols = SMALL_SHARDED[n][1] // N_SHARDS
            arr = lax.dynamic_slice_in_dim(arr, me * cols, cols, axis=1)
        g_small[n] = arr.reshape(w[n].shape)

    grad, delta, new_m, new_v = {}, {}, {}, {}
    for n in BIG:
        outs = _adamw(big2d[n], [part[n], sib[n]], _rows_view(n, m[n]), _rows_view(n, v[n]), name="adamw_" + n)
        grad[n], delta[n], new_m[n], new_v[n] = (_param_view(n, o) for o in outs)
    two_d = lambda a: a.reshape(1, -1) if a.ndim == 1 else a
    deltas, new_ms, new_vs = _adamw_native(*[[two_d(d[n]) for n in SMALL] for d in (w, g_small, m, v)])
    for n, dn, mn, vn in zip(SMALL, deltas, new_ms, new_vs):
        grad[n], delta[n], new_m[n], new_v[n] = (g_small[n], dn.reshape(w[n].shape), mn.reshape(w[n].shape),
                                                 vn.reshape(w[n].shape))

    return (loss_total, grad_x[None], *[grad[n] for n in WEIGHT_NAMES], *[delta[n] for n in WEIGHT_NAMES],
            *[new_m[n] for n in WEIGHT_NAMES], *[new_v[n] for n in WEIGHT_NAMES])
```

```python
import functools
import math

import jax
import jax.numpy as jnp
from jax import lax
from jax.experimental import pallas as pl
from jax.experimental.pallas import tpu as pltpu

F32 = jnp.float32
_MXU = jnp.bfloat16

D_MODEL = 1024
SEQ = 2048
N_META = 16
CHUNK = 128
T_ROWS = 2176
N_CHUNKS = T_ROWS // CHUNK
PAD_ROWS = T_ROWS - SEQ - N_META
X_ROW0 = PAD_ROWS + N_META
SSD_HEADS = 16
SSD_HEAD_DIM = 64
SSD_STATE = 128
SSD_GROUPS = 2
SSD_HPG = SSD_HEADS // SSD_GROUPS
SSD_WIDTH = 1024
LRU_WIDTH = 1024
LRU_C = 8.0
D_FF = 2816
EPS = 1e-6
IN_COLS = 4624
N_SHARDS = 4
N_DEV = 8

ADAM_LR = 0.001
ADAM_B1 = 0.9
ADAM_B2 = 0.999
ADAM_EPS = 1e-08
ADAM_WD = 0.01
ADAM_STEP = 10

VMEM_LIMIT_BYTES = 56 * 1024 * 1024

NN = (((1,), (0,)), ((), ()))
NT = (((1,), (1,)), ((), ()))
TN = (((0,), (0,)), ((), ()))


def _cparams(*sem):
    return pltpu.CompilerParams(dimension_semantics=sem, vmem_limit_bytes=VMEM_LIMIT_BYTES)


def _dot(a, b, dims=NN):
    return lax.dot_general(a.astype(_MXU), b.astype(_MXU), dims, preferred_element_type=F32)


def _dot_exact(a, b, dims=NN):
    return lax.dot_general(a, b, dims, preferred_element_type=F32, precision=lax.Precision.HIGHEST)


def _sigmoid(x):
    return 1.0 / (1.0 + jnp.exp(-x))


def _softplus(x):
    return jnp.maximum(x, 0.0) + jnp.log(1.0 + jnp.exp(-jnp.abs(x)))


def _silu(x):
    return x * _sigmoid(x)


def _silu_grad(x):
    s = _sigmoid(x)
    return s * (1.0 + x * (1.0 - s))


_GELU_C = math.sqrt(2.0 / math.pi)


def _gelu_and_grad(x):
    inner = _GELU_C * (x + 0.044715 * x * x * x)
    t = jnp.tanh(inner)
    g = 0.5 * x * (1.0 + t)
    dg = 0.5 * (1.0 + t) + 0.5 * x * (1.0 - t * t) * _GELU_C * (1.0 + 3.0 * 0.044715 * x * x)
    return g, dg


def _rms_fwd(x, w):
    rstd = lax.rsqrt(jnp.mean(x * x, axis=-1, keepdims=True) + EPS)
    return x * rstd * w


def _rms_bwd(x, w, dy):
    rstd = lax.rsqrt(jnp.mean(x * x, axis=-1, keepdims=True) + EPS)
    xhat = x * rstd
    dxhat = dy * w
    dx = rstd * (dxhat - xhat * jnp.mean(dxhat * xhat, axis=-1, keepdims=True))
    return dx, dy * xhat


def _mm(terms, m, n, *, tm, tn, mode, out_dtype, name, residual=None, n_outer=False, also_mxu=False):
    gm, gn = m // tm, n // tn
    assert gm * tm == m and gn * tn == n
    if n_outer:
        grid = (gn, gm)
        mi = lambda g0, g1: g1
        ni = lambda g0, g1: g0
    else:
        grid = (gm, gn)
        mi = lambda g0, g1: g0
        ni = lambda g0, g1: g1
    in_specs, args = [], []
    for (a, ka, b, kb, k) in terms:
        if mode == "tn":
            in_specs.append(pl.BlockSpec((k, tm), lambda g0, g1, ka=ka: (ka, mi(g0, g1))))
        else:
            in_specs.append(pl.BlockSpec((tm, k), lambda g0, g1, ka=ka: (mi(g0, g1), ka)))
        if mode == "nt":
            in_specs.append(pl.BlockSpec((tn, k), lambda g0, g1, kb=kb: (ni(g0, g1), kb)))
        else:
            in_specs.append(pl.BlockSpec((k, tn), lambda g0, g1, kb=kb: (kb, ni(g0, g1))))
        args += [a, b]
    if residual is not None:
        in_specs.append(pl.BlockSpec((tm, tn), lambda g0, g1: (mi(g0, g1), ni(g0, g1))))
        args.append(residual)
    dims = {"nn": NN, "nt": NT, "tn": TN}[mode]
    n_terms = len(terms)
    has_res = residual is not None

    n_in = len(args)

    def body(*refs):
        acc = None
        for t in range(n_terms):
            d = lax.dot_general(refs[2 * t][...], refs[2 * t + 1][...], dims, preferred_element_type=F32)
            acc = d if acc is None else acc + d
        if has_res:
            acc = acc + refs[2 * n_terms][...]
        refs[n_in][...] = acc.astype(out_dtype)
        if also_mxu:
            refs[n_in + 1][...] = acc.astype(_MXU)

    tile = pl.BlockSpec((tm, tn), lambda g0, g1: (mi(g0, g1), ni(g0, g1)))
    shape = jax.ShapeDtypeStruct((m, n), out_dtype)
    return pl.pallas_call(
        body, name=name, grid=grid, in_specs=in_specs,
        out_specs=[tile, tile] if also_mxu else tile,
        out_shape=[shape, jax.ShapeDtypeStruct((m, n), _MXU)] if also_mxu else shape,
        compiler_params=_cparams("parallel", "parallel"),
    )(*args)


def _embed(x, meta):
    def body(x_ref, meta_ref, o_ref):
        i = pl.program_id(0)

        @pl.when(i == 0)
        def _():
            o_ref[0:PAD_ROWS, :] = jnp.zeros((PAD_ROWS, D_MODEL), F32)
            o_ref[PAD_ROWS:CHUNK, :] = meta_ref[...]

        @pl.when(i > 0)
        def _():
            o_ref[...] = x_ref[...]

    return pl.pallas_call(
        body, name="embed", grid=(N_CHUNKS,),
        in_specs=[pl.BlockSpec((CHUNK, D_MODEL), lambda i: (jnp.maximum(i - 1, 0), 0)),
                  pl.BlockSpec((N_META, D_MODEL), lambda i: (0, 0))],
        out_specs=pl.BlockSpec((CHUNK, D_MODEL), lambda i: (i, 0)),
        out_shape=jax.ShapeDtypeStruct((T_ROWS, D_MODEL), F32),
        compiler_params=_cparams("parallel"),
    )(x, meta)


def _rmsnorm(h, w, *, name, tm=544):
    def body(h_ref, w_ref, o_ref):
        o_ref[...] = _rms_fwd(h_ref[...], w_ref[...]).astype(_MXU)

    return pl.pallas_call(
        body, name=name, grid=(T_ROWS // tm,),
        in_specs=[pl.BlockSpec((tm, D_MODEL), lambda i: (i, 0)), pl.BlockSpec((1, D_MODEL), lambda i: (0, 0))],
        out_specs=pl.BlockSpec((tm, D_MODEL), lambda i: (i, 0)),
        out_shape=jax.ShapeDtypeStruct((T_ROWS, D_MODEL), _MXU),
        compiler_params=_cparams("parallel"),
    )(h, w)


def _loss_head(h2, target, fw):
    def body(h_ref, t_ref, w_ref, loss_ref, dh_ref, dhb_ref, dw_ref, acc_ref):
        i = pl.program_id(0)

        @pl.when(i == 0)
        def _():
            acc_ref[...] = jnp.zeros_like(acc_ref)
            dw_ref[...] = jnp.zeros_like(dw_ref)

        h = h_ref[...]
        w = w_ref[...]
        y = _rms_fwd(h, w)
        live = (i > 0).astype(F32)
        err = (y - t_ref[...]) * live
        acc_ref[...] += jnp.sum(err * err, axis=0, keepdims=True)
        dy = err * (1.0 / D_MODEL)
        dx, dwr = _rms_bwd(h, w, dy)
        dh_ref[...] = dx
        dhb_ref[...] = dx.astype(_MXU)
        dw_ref[...] += jnp.sum(dwr, axis=0, keepdims=True)

        @pl.when(i == N_CHUNKS - 1)
        def _():
            tot = jnp.sum(acc_ref[...], axis=1, keepdims=True) * (0.5 / D_MODEL)
            loss_ref[...] = jnp.broadcast_to(tot, (1, 128))

    return pl.pallas_call(
        body, name="loss_head", grid=(N_CHUNKS,),
        in_specs=[pl.BlockSpec((CHUNK, D_MODEL), lambda i: (i, 0)),
                  pl.BlockSpec((CHUNK, D_MODEL), lambda i: (jnp.maximum(i - 1, 0), 0)),
                  pl.BlockSpec((1, D_MODEL), lambda i: (0, 0))],
        out_specs=[pl.BlockSpec((1, 128), lambda i: (0, 0)),
                   pl.BlockSpec((CHUNK, D_MODEL), lambda i: (i, 0)),
                   pl.BlockSpec((CHUNK, D_MODEL), lambda i: (i, 0)),
                   pl.BlockSpec((1, D_MODEL), lambda i: (0, 0))],
        out_shape=[jax.ShapeDtypeStruct((1, 128), F32),
                   jax.ShapeDtypeStruct((T_ROWS, D_MODEL), F32),
                   jax.ShapeDtypeStruct((T_ROWS, D_MODEL), _MXU),
                   jax.ShapeDtypeStruct((1, D_MODEL), F32)],
        scratch_shapes=[pltpu.VMEM((1, D_MODEL), F32)],
        compiler_params=_cparams("arbitrary"),
    )(h2, target, fw)


def _mm_norm_bwd(terms, h, w, dres, *, name, tm=272):
    n_terms = len(terms)
    in_specs, args = [], []
    for (a, b, k) in terms:
        in_specs += [pl.BlockSpec((tm, k), lambda i: (i, 0)), pl.BlockSpec((k, D_MODEL), lambda i: (0, 0))]
        args += [a, b]
    in_specs += [pl.BlockSpec((tm, D_MODEL), lambda i: (i, 0)), pl.BlockSpec((1, D_MODEL), lambda i: (0, 0)),
                 pl.BlockSpec((tm, D_MODEL), lambda i: (i, 0))]
    args += [h, w, dres]

    def body(*refs):
        h_ref, w_ref, dres_ref, dh_ref, dhb_ref, dw_ref = refs[2 * n_terms:]

        @pl.when(pl.program_id(0) == 0)
        def _():
            dw_ref[...] = jnp.zeros_like(dw_ref)

        du = None
        for t in range(n_terms):
            d = lax.dot_general(refs[2 * t][...], refs[2 * t + 1][...], NN, preferred_element_type=F32)
            du = d if du is None else du + d
        dx, dwr = _rms_bwd(h_ref[...], w_ref[...], du)
        dh = dres_ref[...] + dx
        dh_ref[...] = dh
        dhb_ref[...] = dh.astype(_MXU)
        dw_ref[...] += jnp.sum(dwr, axis=0, keepdims=True)

    return pl.pallas_call(
        body, name=name, grid=(T_ROWS // tm,), in_specs=in_specs,
        out_specs=[pl.BlockSpec((tm, D_MODEL), lambda i: (i, 0)), pl.BlockSpec((tm, D_MODEL), lambda i: (i, 0)),
                   pl.BlockSpec((1, D_MODEL), lambda i: (0, 0))],
        out_shape=[jax.ShapeDtypeStruct((T_ROWS, D_MODEL), F32), jax.ShapeDtypeStruct((T_ROWS, D_MODEL), _MXU),
                   jax.ShapeDtypeStruct((1, D_MODEL), F32)],
        compiler_params=_cparams("arbitrary"),
    )(*args)


FFN_TM = 272
FFN_TN = 1408


def _ffn_up(u2, wg_t, wu_t):
    def body(u_ref, wg_ref, wu_ref, gp_ref, up_ref, act_ref):
        u = u_ref[...]
        gp = lax.dot_general(u, wg_ref[...], NT, preferred_element_type=F32)
        up = lax.dot_general(u, wu_ref[...], NT, preferred_element_type=F32)
        gp_ref[...] = gp
        up_ref[...] = up
        act_ref[...] = (_silu(gp) * up).astype(_MXU)

    tile = pl.BlockSpec((FFN_TM, FFN_TN), lambda j, i: (i, j))
    return pl.pallas_call(
        body, name="ffn_up", grid=(D_FF // FFN_TN, T_ROWS // FFN_TM),
        in_specs=[pl.BlockSpec((FFN_TM, D_MODEL), lambda j, i: (i, 0)),
                  pl.BlockSpec((FFN_TN, D_MODEL), lambda j, i: (j, 0)),
                  pl.BlockSpec((FFN_TN, D_MODEL), lambda j, i: (j, 0))],
        out_specs=[tile, tile, tile],
        out_shape=[jax.ShapeDtypeStruct((T_ROWS, D_FF), F32), jax.ShapeDtypeStruct((T_ROWS, D_FF), F32),
                   jax.ShapeDtypeStruct((T_ROWS, D_FF), _MXU)],
        compiler_params=_cparams("parallel", "parallel"),
    )(u2, wg_t, wu_t)


def _ffn_bwd_act(dh2b, wd, gp, up):
    def body(dh_ref, wd_ref, gp_ref, up_ref, dgp_ref, dup_ref):
        dact = lax.dot_general(dh_ref[...], wd_ref[...], NT, preferred_element_type=F32)
        gp = gp_ref[...]
        dgp_ref[...] = (dact * up_ref[...] * _silu_grad(gp)).astype(_MXU)
        dup_ref[...] = (dact * _silu(gp)).astype(_MXU)

    tile = pl.BlockSpec((FFN_TM, FFN_TN), lambda j, i: (i, j))
    return pl.pallas_call(
        body, name="ffn_bwd_act", grid=(D_FF // FFN_TN, T_ROWS // FFN_TM),
        in_specs=[pl.BlockSpec((FFN_TM, D_MODEL), lambda j, i: (i, 0)),
                  pl.BlockSpec((FFN_TN, D_MODEL), lambda j, i: (j, 0)), tile, tile],
        out_specs=[tile, tile],
        out_shape=[jax.ShapeDtypeStruct((T_ROWS, D_FF), _MXU), jax.ShapeDtypeStruct((T_ROWS, D_FF), _MXU)],
        compiler_params=_cparams("parallel", "parallel"),
    )(dh2b, wd, gp, up)


CONV_TC = 512
CONV_K = 4


def _conv_pre(x_ref, wv, bv, c):
    tc = wv.shape[1]
    r0 = c * CHUNK
    cur = x_ref[r0:r0 + CHUNK, :]
    prev8 = jnp.zeros((8, tc), F32) if c == 0 else x_ref[r0 - 8:r0, :]
    cat = jnp.concatenate([prev8, cur], axis=0)
    shifted = [cur] + [pltpu.roll(cat, s, 0)[8:8 + CHUNK] for s in range(1, CONV_K)]
    pre = bv
    for s in range(CONV_K):
        pre = pre + shifted[s] * wv[CONV_K - 1 - s:CONV_K - s]
    return pre, shifted


def _row_mask(c):
    if c > 0:
        return None
    return (lax.broadcasted_iota(jnp.int32, (CHUNK, 1), 0) >= PAD_ROWS).astype(F32)


def _conv_fwd(x, w, b, *, silu, name):
    cols = x.shape[1]
    tc = min(CONV_TC, cols)

    def body(x_ref, w_ref, b_ref, o_ref):
        wv, bv = w_ref[...], b_ref[...]
        for c in range(N_CHUNKS):
            pre, _ = _conv_pre(x_ref, wv, bv, c)
            y = _silu(pre) if silu else pre
            mask = _row_mask(c)
            if mask is not None:
                y = y * mask
            o_ref[c * CHUNK:(c + 1) * CHUNK, :] = y

    return pl.pallas_call(
        body, name=name, grid=(cols // tc,),
        in_specs=[pl.BlockSpec((T_ROWS, tc), lambda j: (0, j)), pl.BlockSpec((CONV_K, tc), lambda j: (0, j)),
                  pl.BlockSpec((1, tc), lambda j: (0, j))],
        out_specs=pl.BlockSpec((T_ROWS, tc), lambda j: (0, j)),
        out_shape=jax.ShapeDtypeStruct((T_ROWS, cols), F32),
        compiler_params=_cparams("parallel"),
    )(x, w, b)


def _conv_bwd(dy, x, w, b, *, silu, name):
    cols = x.shape[1]
    tc = min(CONV_TC, cols)

    def body(dy_ref, x_ref, w_ref, b_ref, dx_ref, dw_ref, db_ref):
        wv, bv = w_ref[...], b_ref[...]
        next8 = jnp.zeros((8, tc), F32)
        dws = [jnp.zeros((1, tc), F32) for _ in range(CONV_K)]
        db = jnp.zeros((1, tc), F32)
        for c in reversed(range(N_CHUNKS)):
            pre, shifted = _conv_pre(x_ref, wv, bv, c)
            dpre = dy_ref[c * CHUNK:(c + 1) * CHUNK, :]
            if silu:
                dpre = dpre * _silu_grad(pre)
            mask = _row_mask(c)
            if mask is not None:
                dpre = dpre * mask
            cat = jnp.concatenate([dpre, next8], axis=0)
            dx = dpre * wv[CONV_K - 1:CONV_K]
            for s in range(1, CONV_K):
                dx = dx + pltpu.roll(cat, CHUNK + 8 - s, 0)[0:CHUNK] * wv[CONV_K - 1 - s:CONV_K - s]
            dx_ref[c * CHUNK:(c + 1) * CHUNK, :] = dx.astype(_MXU)
            for s in range(CONV_K):
                k = CONV_K - 1 - s
                dws[k] = dws[k] + jnp.sum(dpre * shifted[s], axis=0, keepdims=True)
            db = db + jnp.sum(dpre, axis=0, keepdims=True)
            next8 = dpre[0:8]
        dw_ref[...] = jnp.concatenate(dws, axis=0)
        db_ref[...] = db

    return pl.pallas_call(
        body, name=name, grid=(cols // tc,),
        in_specs=[pl.BlockSpec((T_ROWS, tc), lambda j: (0, j)), pl.BlockSpec((T_ROWS, tc), lambda j: (0, j)),
                  pl.BlockSpec((CONV_K, tc), lambda j: (0, j)), pl.BlockSpec((1, tc), lambda j: (0, j))],
        out_specs=[pl.BlockSpec((T_ROWS, tc), lambda j: (0, j)), pl.BlockSpec((CONV_K, tc), lambda j: (0, j)),
                   pl.BlockSpec((1, tc), lambda j: (0, j))],
        out_shape=[jax.ShapeDtypeStruct((T_ROWS, cols), _MXU), jax.ShapeDtypeStruct((CONV_K, cols), F32),
                   jax.ShapeDtypeStruct((1, cols), F32)],
        compiler_params=_cparams("parallel"),
    )(dy, x, w, b)


def _ssd_chunk_common(dt_raw, prm, c):
    a_row = -jnp.exp(prm[1:2])
    dt = _softplus(dt_raw + prm[0:1])
    rows = lax.broadcasted_iota(jnp.int32, (CHUNK, 1), 0)
    real = jnp.logical_or(c > 0, rows >= PAD_ROWS)
    dt = jnp.where(real, dt, 0.0)
    li = lax.broadcasted_iota(jnp.int32, (CHUNK, CHUNK), 0)
    si = lax.broadcasted_iota(jnp.int32, (CHUNK, CHUNK), 1)
    causal = li >= si
    tri = causal.astype(F32)
    cs = _dot_exact(tri, dt * a_row)
    return dt, a_row, cs, cs.T, causal, tri, real


def _gated_norm_fwd(y, z, w):
    g = y * _silu(z)
    half = SSD_WIDTH // SSD_GROUPS
    outs = [_rms_fwd(g[:, k * half:(k + 1) * half], w[:, k * half:(k + 1) * half]) for k in range(SSD_GROUPS)]
    return jnp.concatenate(outs, axis=1)


GROUP_W = SSD_WIDTH // SSD_GROUPS
PAIR_W = 2 * SSD_HEAD_DIM
STATE_SHAPE = (SSD_GROUPS, SSD_STATE, GROUP_W)


def _head_expander():
    r = lax.broadcasted_iota(jnp.int32, (128, SSD_WIDTH), 0)
    c = lax.broadcasted_iota(jnp.int32, (128, SSD_WIDTH), 1)
    return (c // SSD_HEAD_DIM == r).astype(F32)


def _ssd_expand(dt, cs, prm, ex):
    cs_x = _dot_exact(cs, ex)
    cs_last_x = cs_x[CHUNK - 1:CHUNK, :]
    return (_dot_exact(dt, ex), _dot_exact(prm, ex)[2:3], jnp.exp(cs_x), jnp.exp(cs_last_x),
            jnp.exp(cs_last_x - cs_x))


def _ssd_fwd(xs, bc, dt_raw, z, prm, norm_w, ex):
    def body(xs_ref, bc_ref, dt_ref, z_ref, prm_ref, nw_ref, ex_ref, y_ref, yn_ref, prev_ref, state):
        c = pl.program_id(0)

        @pl.when(c == 0)
        def _():
            state[...] = jnp.zeros_like(state)

        prm = prm_ref[...]
        dt, a_row, cs, cs_t, causal, _, _ = _ssd_chunk_common(dt_ref[...], prm, c)
        dt_x, d_x, e_cs_x, e_last_x, dec_x = _ssd_expand(dt, cs, prm, ex_ref[...])
        xs_all = xs_ref[...]
        bc_all = bc_ref[...]
        xdt = xs_all * dt_x
        xdec = xdt * dec_x
        lane_lo = lax.broadcasted_iota(jnp.int32, (1, PAIR_W), 1) < SSD_HEAD_DIM
        for g in range(SSD_GROUPS):
            gs = slice(g * GROUP_W, (g + 1) * GROUP_W)
            b_g = bc_all[:, g * SSD_STATE:(g + 1) * SSD_STATE]
            c_g = bc_all[:, (SSD_GROUPS + g) * SSD_STATE:(SSD_GROUPS + g + 1) * SSD_STATE]
            st = state[g]
            prev_ref[0, g] = st
            y_off = _dot(c_g, st) * e_cs_x[:, gs]
            state[g] = st * e_last_x[:, gs] + _dot(b_g.T, xdec[:, gs])
            cb = _dot(c_g, b_g, NT)
            for k in range(SSD_HPG // 2):
                h0 = g * SSD_HPG + 2 * k
                ps = slice(h0 * SSD_HEAD_DIM, h0 * SSD_HEAD_DIM + PAIR_W)
                xdt_pair = xdt[:, ps]
                yd = []
                for h in (h0, h0 + 1):
                    lmat = jnp.where(causal, jnp.exp(cs[:, h:h + 1] - cs_t[h:h + 1, :]), 0.0)
                    yd.append(_dot(cb * lmat, xdt_pair))
                y_ref[:, ps] = (jnp.where(lane_lo, yd[0], yd[1]) + y_off[:, k * PAIR_W:(k + 1) * PAIR_W]
                                + xs_all[:, ps] * d_x[:, ps])
        yn_ref[...] = _gated_norm_fwd(y_ref[...], z_ref[...], nw_ref[...]).astype(_MXU)

    row = lambda w: pl.BlockSpec((CHUNK, w), lambda c: (c, 0))
    return pl.pallas_call(
        body, name="ssd_fwd", grid=(N_CHUNKS,),
        in_specs=[row(SSD_WIDTH), row(512), row(128), row(SSD_WIDTH),
                  pl.BlockSpec((8, 128), lambda c: (0, 0)), pl.BlockSpec((1, SSD_WIDTH), lambda c: (0, 0)),
                  pl.BlockSpec((128, SSD_WIDTH), lambda c: (0, 0))],
        out_specs=[row(SSD_WIDTH), row(SSD_WIDTH),
                   pl.BlockSpec((1,) + STATE_SHAPE, lambda c: (c, 0, 0, 0))],
        out_shape=[jax.ShapeDtypeStruct((T_ROWS, SSD_WIDTH), F32), jax.ShapeDtypeStruct((T_ROWS, SSD_WIDTH), _MXU),
                   jax.ShapeDtypeStruct((N_CHUNKS,) + STATE_SHAPE, F32)],
        scratch_shapes=[pltpu.VMEM(STATE_SHAPE, F32)],
        compiler_params=_cparams("arbitrary"),
    )(xs, bc, dt_raw, z, prm, norm_w, ex)


def _ssd_bwd(dyn, dyn_block, z, y_pre, xs, bc, dt_raw, prev, prm, norm_w, ex):
    def body(dyn_ref, z_ref, y_ref, xs_ref, bc_ref, dt_ref, prev_ref, prm_ref, nw_ref, ex_ref,
             dz_ref, dxs_ref, dbc_ref, ddt_ref, dprm_ref, dnw_ref, dstate):
        step = pl.program_id(0)
        c = N_CHUNKS - 1 - step

        @pl.when(step == 0)
        def _():
            dstate[...] = jnp.zeros_like(dstate)
            dprm_ref[...] = jnp.zeros_like(dprm_ref)
            dnw_ref[...] = jnp.zeros_like(dnw_ref)

        prm = prm_ref[...]
        dt, a_row, cs, cs_t, causal, tri, real = _ssd_chunk_common(dt_ref[...], prm, c)
        realf = real.astype(F32)
        z = z_ref[...]
        y_all = y_ref[...]
        nw = nw_ref[...]
        dyn_all = dyn_ref[...]
        sz = _silu(z)
        gated = y_all * sz
        half = SSD_WIDTH // SSD_GROUPS
        dgs, dnws = [], []
        for k in range(SSD_GROUPS):
            sl = slice(k * half, (k + 1) * half)
            dgk, dwk = _rms_bwd(gated[:, sl], nw[:, sl], dyn_all[:, sl])
            dgs.append(dgk)
            dnws.append(jnp.sum(dwk, axis=0, keepdims=True))
        dgated = jnp.concatenate(dgs, axis=1)
        dnw_ref[...] += jnp.concatenate(dnws, axis=1)
        dz_ref[...] = (dgated * y_all * _silu_grad(z)).astype(_MXU)
        dy_all = dgated * sz

        ex = ex_ref[...]
        dt_x, d_x, e_cs_x, e_last_x, dec_x = _ssd_expand(dt, cs, prm, ex)
        xs_all = xs_ref[...]
        bc_all = bc_ref[...]
        xdt = xs_all * dt_x
        xdt_mxu = xdt.astype(_MXU).astype(F32)
        xdec = xdt * dec_x
        dcp = dy_all * e_cs_x
        lane_lo = lax.broadcasted_iota(jnp.int32, (1, PAIR_W), 1) < SSD_HEAD_DIM
        upper = (lax.broadcasted_iota(jnp.int32, (CHUNK, CHUNK), 0)
                 <= lax.broadcasted_iota(jnp.int32, (CHUNK, CHUNK), 1))
        last_row = (lax.broadcasted_iota(jnp.int32, (CHUNK, 1), 0) == CHUNK - 1).astype(F32)
        dbs, dcs_, dxdt_parts, last_parts = [], [], [], []
        for g in range(SSD_GROUPS):
            gs = slice(g * GROUP_W, (g + 1) * GROUP_W)
            b_g = bc_all[:, g * SSD_STATE:(g + 1) * SSD_STATE]
            c_g = bc_all[:, (SSD_GROUPS + g) * SSD_STATE:(SSD_GROUPS + g + 1) * SSD_STATE]
            prev_t = prev_ref[0, g]
            dst = dstate[g]
            dc_g = _dot(dcp[:, gs], prev_t, NT)
            db_g = _dot(xdec[:, gs], dst, NT)
            dxdt_state = _dot(b_g, dst) * dec_x[:, gs]
            dstate[g] = dst * e_last_x[:, gs] + _dot(c_g.T, dcp[:, gs])
            last_parts.append(jnp.sum(xdt_mxu[:, gs] * dxdt_state, axis=0, keepdims=True)
                              + jnp.sum(dst * prev_t, axis=0, keepdims=True) * e_last_x[:, gs])
            cb_t = _dot(b_g, c_g, NT)
            dcb_t = jnp.zeros((CHUNK, CHUNK), F32)
            for k in range(SSD_HPG // 2):
                h0 = g * SSD_HPG + 2 * k
                ps = slice(h0 * SSD_HEAD_DIM, h0 * SSD_HEAD_DIM + PAIR_W)
                dy_pair = dy_all[:, ps]
                xdt_pair = xdt[:, ps]
                dd = []
                for h in (h0, h0 + 1):
                    lmat_t = jnp.where(upper, jnp.exp(cs_t[h:h + 1, :] - cs[:, h:h + 1]), 0.0)
                    dd.append(_dot(cb_t * lmat_t, dy_pair))
                    mine = lane_lo if h == h0 else jnp.logical_not(lane_lo)
                    dcb_t = dcb_t + _dot(jnp.where(mine, xdt_pair, 0.0), dy_pair, NT) * lmat_t
                dxdt_parts.append(jnp.where(lane_lo, dd[0], dd[1]) + dxdt_state[:, k * PAIR_W:(k + 1) * PAIR_W])
            dc_g = dc_g + _dot(dcb_t, b_g, TN)
            db_g = db_g + _dot(dcb_t, c_g)
            dbs.append(db_g * realf)
            dcs_.append(dc_g * realf)
        dbc_ref[...] = jnp.concatenate(dbs + dcs_, axis=1)
        dxdt = jnp.concatenate(dxdt_parts, axis=1)
        dxs_ref[...] = (dxdt * dt_x + dy_all * d_x) * realf
        ddt_all = _dot_exact(dxdt * xs_all, ex, NT)
        rows = jnp.concatenate([jnp.concatenate(last_parts, axis=1), jnp.sum(dy_all * xs_all, axis=0, keepdims=True),
                                jnp.zeros((6, SSD_WIDTH), F32)], axis=0)
        rows = _dot_exact(rows, ex, NT)
        dd_row = rows[1:2]
        dy_mxu = dy_all.astype(_MXU).astype(F32)
        dcs_all = (_dot_exact(dy_mxu * (y_all - xs_all * d_x), ex, NT) - _dot_exact(xdt_mxu * dxdt, ex, NT)
                   + last_row * rows[0:1])
        dda = _dot_exact(tri, dcs_all, TN)
        ddt = (ddt_all + dda * a_row) * realf
        ddt_raw = ddt * _sigmoid(dt_ref[...] + prm[0:1])
        ddt_ref[...] = ddt_raw.astype(_MXU)
        da_log = jnp.sum(dda * dt, axis=0, keepdims=True) * a_row
        dprm_ref[0:1, :] += jnp.sum(ddt_raw, axis=0, keepdims=True)
        dprm_ref[1:2, :] += da_log
        dprm_ref[2:3, :] += dd_row

    rev = lambda w, blk=0: pl.BlockSpec((CHUNK, w), lambda s, blk=blk: (N_CHUNKS - 1 - s, blk))
    return pl.pallas_call(
        body, name="ssd_bwd", grid=(N_CHUNKS,),
        in_specs=[rev(SSD_WIDTH, dyn_block), rev(SSD_WIDTH), rev(SSD_WIDTH), rev(SSD_WIDTH), rev(512), rev(128),
                  pl.BlockSpec((1,) + STATE_SHAPE, lambda s: (N_CHUNKS - 1 - s, 0, 0, 0)),
                  pl.BlockSpec((8, 128), lambda s: (0, 0)), pl.BlockSpec((1, SSD_WIDTH), lambda s: (0, 0)),
                  pl.BlockSpec((128, SSD_WIDTH), lambda s: (0, 0))],
        out_specs=[rev(SSD_WIDTH), rev(SSD_WIDTH), rev(512), rev(128),
                   pl.BlockSpec((8, 128), lambda s: (0, 0)), pl.BlockSpec((1, SSD_WIDTH), lambda s: (0, 0))],
        out_shape=[jax.ShapeDtypeStruct((T_ROWS, SSD_WIDTH), _MXU), jax.ShapeDtypeStruct((T_ROWS, SSD_WIDTH), F32),
                   jax.ShapeDtypeStruct((T_ROWS, 512), F32), jax.ShapeDtypeStruct((T_ROWS, 128), _MXU),
                   jax.ShapeDtypeStruct((8, 128), F32), jax.ShapeDtypeStruct((1, SSD_WIDTH), F32)],
        scratch_shapes=[pltpu.VMEM(STATE_SHAPE, F32)],
        compiler_params=_cparams("arbitrary"),
    )(dyn, z, y_pre, xs, bc, dt_raw, prev, prm, norm_w, ex)


LRU_PAIRS = 8


def _lru_gates(xr, wa_ref, wx_ref, prm):
    pre_r, pre_i = [], []
    for k in range(LRU_PAIRS):
        xk = xr[:, k * 128:(k + 1) * 128]
        pre_r.append(_dot(xk, wa_ref[k]))
        pre_i.append(_dot(xk, wx_ref[k]))
    r = _sigmoid(jnp.concatenate(pre_r, axis=1) + prm[0:1])
    i = _sigmoid(jnp.concatenate(pre_i, axis=1) + prm[1:2])
    sp = _softplus(-prm[2:3])
    log_a = (-LRU_C) * r * sp
    a = jnp.exp(log_a)
    s = jnp.sqrt(-jnp.tanh(log_a) * (a * a + 1.0))
    return r, i, a, s, sp


def _lru_fwd(xr, gate, wa, wx, prm):
    def body(xr_ref, g_ref, wa_ref, wx_ref, prm_ref, hs_ref, yn_ref, carry, a_s, u_s):
        @pl.when(pl.program_id(0) == 0)
        def _():
            carry[...] = jnp.zeros_like(carry)

        prm = prm_ref[...]
        xr_t = xr_ref[...]
        _, i, a, s, _ = _lru_gates(xr_t, wa_ref, wx_ref, prm)
        a_s[...] = a
        u_s[...] = s * (i * xr_t)
        rid = lax.broadcasted_iota(jnp.int32, (8, LRU_WIDTH), 0)

        def group(k, h):
            off = pl.multiple_of(k * 8, 8)
            a8 = a_s[pl.ds(off, 8), :]
            u8 = u_s[pl.ds(off, 8), :]
            out = jnp.zeros((8, LRU_WIDTH), F32)
            for r_ in range(8):
                h = a8[r_:r_ + 1] * h + u8[r_:r_ + 1]
                out = jnp.where(rid == r_, h, out)
            hs_ref[pl.ds(off, 8), :] = out
            return h

        carry[0:1, :] = lax.fori_loop(0, CHUNK // 8, group, carry[0:1, :])
        gel, _ = _gelu_and_grad(g_ref[...])
        yn_ref[...] = _rms_fwd(gel * hs_ref[...], prm[3:4]).astype(_MXU)

    row = pl.BlockSpec((CHUNK, LRU_WIDTH), lambda t: (t, 0))
    wspec = pl.BlockSpec((LRU_PAIRS, 128, 128), lambda t: (0, 0, 0))
    return pl.pallas_call(
        body, name="lru_fwd", grid=(N_CHUNKS,),
        in_specs=[row, row, wspec, wspec, pl.BlockSpec((8, LRU_WIDTH), lambda t: (0, 0))],
        out_specs=[row, row],
        out_shape=[jax.ShapeDtypeStruct((T_ROWS, LRU_WIDTH), F32), jax.ShapeDtypeStruct((T_ROWS, LRU_WIDTH), _MXU)],
        scratch_shapes=[pltpu.VMEM((8, LRU_WIDTH), F32), pltpu.VMEM((CHUNK, LRU_WIDTH), F32),
                        pltpu.VMEM((CHUNK, LRU_WIDTH), F32)],
        compiler_params=_cparams("arbitrary"),
    )(xr, gate, wa, wx, prm)


def _lru_bwd(dyn, dyn_block, gate, xr, hs, wa, wx, wa_t, wx_t, prm):
    def body(dyn_ref, g_ref, xr_ref, hs_ref, hsp_ref, wa_ref, wx_ref, wat_ref, wxt_ref, prm_ref,
             dg_ref, dxr_ref, dwa_ref, dwx_ref, dprm_ref, carry, a_s, d_s):
        step = pl.program_id(0)
        tile = N_CHUNKS - 1 - step

        @pl.when(step == 0)
        def _():
            carry[...] = jnp.zeros_like(carry)
            dwa_ref[...] = jnp.zeros_like(dwa_ref)
            dwx_ref[...] = jnp.zeros_like(dwx_ref)
            dprm_ref[...] = jnp.zeros_like(dprm_ref)

        prm = prm_ref[...]
        xr_t = xr_ref[...]
        r, i, a, s, sp = _lru_gates(xr_t, wa_ref, wx_ref, prm)
        hs_t = hs_ref[...]
        gel, dgel = _gelu_and_grad(g_ref[...])
        dy, dnw = _rms_bwd(gel * hs_t, prm[3:4], dyn_ref[...])
        dg_ref[...] = (dy * hs_t * dgel).astype(_MXU)
        a_s[...] = a
        d_s[...] = dy * gel
        rid = lax.broadcasted_iota(jnp.int32, (8, LRU_WIDTH), 0)

        def group(k, cr):
            off = pl.multiple_of((CHUNK // 8 - 1 - k) * 8, 8)
            a8 = a_s[pl.ds(off, 8), :]
            d8 = d_s[pl.ds(off, 8), :]
            out = jnp.zeros((8, LRU_WIDTH), F32)
            for r_ in reversed(range(8)):
                dht = d8[r_:r_ + 1] + cr
                out = jnp.where(rid == r_, dht, out)
                cr = a8[r_:r_ + 1] * dht
            d_s[pl.ds(off, 8), :] = out
            return cr

        carry[0:1, :] = lax.fori_loop(0, CHUNK // 8, group, carry[0:1, :])
        dht = d_s[...]
        before = hsp_ref[CHUNK - 8:CHUNK, :][7:8] * (tile > 0).astype(F32)
        first = lax.broadcasted_iota(jnp.int32, (CHUNK, 1), 0) == 0
        hprev = jnp.where(first, before, pltpu.roll(hs_t, 1, 0))
        da = dht * hprev
        ixr = i * xr_t
        ds = dht * ixr
        dlog_a = da * a - ds * (a * a) / s
        dr = dlog_a * ((-LRU_C) * sp)
        dsp = jnp.sum(dlog_a * ((-LRU_C) * r), axis=0, keepdims=True)
        dlam = dsp * (-_sigmoid(-prm[2:3]))
        di = dht * s * xr_t
        dpre_r = dr * r * (1.0 - r)
        dpre_i = di * i * (1.0 - i)
        dxr = dht * s * i
        parts = []
        for k in range(LRU_PAIRS):
            sl = slice(k * 128, (k + 1) * 128)
            parts.append(_dot(dpre_r[:, sl], wat_ref[k]) + _dot(dpre_i[:, sl], wxt_ref[k]))
            dwa_ref[k] += _dot(xr_t[:, sl], dpre_r[:, sl], TN)
            dwx_ref[k] += _dot(xr_t[:, sl], dpre_i[:, sl], TN)
        dxr_ref[...] = dxr + jnp.concatenate(parts, axis=1)
        dprm_ref[0:1, :] += jnp.sum(dpre_r, axis=0, keepdims=True)
        dprm_ref[1:2, :] += jnp.sum(dpre_i, axis=0, keepdims=True)
        dprm_ref[2:3, :] += dlam
        dprm_ref[3:4, :] += jnp.sum(dnw, axis=0, keepdims=True)

    rev = lambda blk=0: pl.BlockSpec((CHUNK, LRU_WIDTH), lambda s, blk=blk: (N_CHUNKS - 1 - s, blk))
    wspec = pl.BlockSpec((LRU_PAIRS, 128, 128), lambda s: (0, 0, 0))
    return pl.pallas_call(
        body, name="lru_bwd", grid=(N_CHUNKS,),
        in_specs=[rev(dyn_block), rev(), rev(), rev(),
                  pl.BlockSpec((CHUNK, LRU_WIDTH), lambda s: (jnp.maximum(N_CHUNKS - 2 - s, 0), 0)),
                  wspec, wspec, wspec, wspec, pl.BlockSpec((8, LRU_WIDTH), lambda s: (0, 0))],
        out_specs=[rev(), rev(), wspec, wspec, pl.BlockSpec((8, LRU_WIDTH), lambda s: (0, 0))],
        out_shape=[jax.ShapeDtypeStruct((T_ROWS, LRU_WIDTH), _MXU), jax.ShapeDtypeStruct((T_ROWS, LRU_WIDTH), F32),
                   jax.ShapeDtypeStruct((LRU_PAIRS, 128, 128), F32), jax.ShapeDtypeStruct((LRU_PAIRS, 128, 128), F32),
                   jax.ShapeDtypeStruct((8, LRU_WIDTH), F32)],
        scratch_shapes=[pltpu.VMEM((8, LRU_WIDTH), F32), pltpu.VMEM((CHUNK, LRU_WIDTH), F32),
                        pltpu.VMEM((CHUNK, LRU_WIDTH), F32)],
        compiler_params=_cparams("arbitrary"),
    )(dyn, gate, xr, hs, hs, wa, wx, wa_t, wx_t, prm)


SEC_NAMES = ("z", "xs", "bc", "dt", "g", "x")
SEC_WIDTH = {"z": 1024, "xs": 1024, "bc": 512, "dt": 128, "g": 1024, "x": 1024}


def _pair_blocks(w):
    w = w.reshape(LRU_PAIRS, 2, 64, 64)
    zero = jnp.zeros((LRU_PAIRS, 64, 64), w.dtype)
    top = jnp.concatenate([w[:, 0], zero], axis=2)
    bot = jnp.concatenate([zero, w[:, 1]], axis=2)
    return jnp.concatenate([top, bot], axis=1)


def _unpair_blocks(wp):
    return jnp.stack([wp[:, :64, :64], wp[:, 64:, 64:]], axis=1).reshape(16, 64, 64)


def _pad_lanes(v, width=128):
    return jnp.pad(v, ((0, 0), (0, width - v.shape[1])))


class _Resident:
    def __init__(self, w_out, w_gate, w_up, w_down):
        self._w_out, self._ffn = w_out, (w_gate, w_up, w_down)

    def w_out(self, after):
        return self._w_out

    def ffn(self, after):
        return self._ffn

    def grads_ready(self, names, g, g_mxu):
        return jnp.zeros((1, 1), F32)

    def small_ready(self, g, loss):
        pass

    def small_middle(self, after):
        pass


def _local_step(x, target, meta, p, late):
    g, g_mxu = {}, {}
    ex = _head_expander()
    h0 = _embed(x, meta)
    u1 = _rmsnorm(h0, p["norm1_w"], name="norm1")
    proj = {}
    for s in SEC_NAMES:
        wdt = SEC_WIDTH[s]
        proj[s] = _mm([(u1, 0, p["w_in_" + s], 0, D_MODEL)], T_ROWS, wdt, tm=544, tn=min(wdt, 512), mode="nt",
                      out_dtype=F32, name="proj_" + s)
    ssd_prm = jnp.concatenate([_pad_lanes(p["ssd_dt_bias"]), _pad_lanes(p["ssd_a_log"]), _pad_lanes(p["ssd_d"]),
                               jnp.zeros((5, 128), F32)], axis=0)
    xs_act = _conv_fwd(proj["xs"], p["ssd_conv_w"][:, :SSD_WIDTH], p["ssd_conv_b"][:, :SSD_WIDTH], silu=True,
                       name="ssd_conv_xs")
    bc_act = _conv_fwd(proj["bc"], p["ssd_conv_w"][:, SSD_WIDTH:], p["ssd_conv_b"][:, SSD_WIDTH:], silu=True,
                       name="ssd_conv_bc")
    y_pre, y_ssd, prev = _ssd_fwd(xs_act, bc_act, proj["dt"], proj["z"], ssd_prm, p["ssd_norm_w"], ex)
    xr = _conv_fwd(proj["x"], p["lru_conv_w"], p["lru_conv_b"], silu=False, name="lru_conv")
    wa_p, wx_p = _pair_blocks(p["lru_wa"]), _pair_blocks(p["lru_wx"])
    lru_prm = jnp.concatenate([p["lru_ba"], p["lru_bx"], p["lru_lambda"], p["lru_norm_w"],
                               jnp.zeros((4, LRU_WIDTH), F32)], axis=0)
    hs, y_lru = _lru_fwd(xr, proj["g"], wa_p.astype(_MXU), wx_p.astype(_MXU), lru_prm)
    ycat = jnp.concatenate([y_ssd, y_lru], axis=1)
    w_out = late.w_out(ycat)
    h1 = _mm([(ycat, 0, w_out, 0, 2 * D_MODEL)], T_ROWS, D_MODEL, tm=544, tn=512, mode="nn", out_dtype=F32,
             name="out_proj", residual=h0)
    u2 = _rmsnorm(h1, p["norm2_w"], name="norm2")
    w_gate, w_up, w_down = late.ffn(u2)
    gp, up, act = _ffn_up(u2, w_gate, w_up)
    h2 = _mm([(act, 0, w_down, 0, D_FF)], T_ROWS, D_MODEL, tm=544, tn=512, mode="nn", out_dtype=F32,
             name="ffn_down", residual=h1)
    loss, dh2, dh2b, g["final_norm_w"] = _loss_head(h2, target, p["final_norm_w"])
    dgp, dup = _ffn_bwd_act(dh2b, w_down, gp, up)
    g["w_down"], g_mxu["w_down"] = _mm([(act, 0, dh2b, 0, T_ROWS)], D_FF, D_MODEL, tm=1408, tn=512, mode="tn",
                                       out_dtype=F32, name="dw_down", also_mxu=True)
    dh1, dh1b, g["norm2_w"] = _mm_norm_bwd([(dgp, w_gate, D_FF), (dup, w_up, D_FF)], h1, p["norm2_w"], dh2,
                                           name="ffn_bwd_in")
    g["w_gate"], g_mxu["w_gate"] = _mm([(dgp, 0, u2, 0, T_ROWS)], D_FF, D_MODEL, tm=1408, tn=512, mode="tn",
                                       out_dtype=F32, name="dw_gate", also_mxu=True)
    g["w_up"], g_mxu["w_up"] = _mm([(dup, 0, u2, 0, T_ROWS)], D_FF, D_MODEL, tm=1408, tn=512, mode="tn",
                                   out_dtype=F32, name="dw_up", also_mxu=True)
    lru_prm = lru_prm + late.grads_ready(("w_down", "w_gate", "w_up"), g, g_mxu)
    dycat = _mm([(dh1b, 0, w_out, 0, D_MODEL)], T_ROWS, 2 * D_MODEL, tm=544, tn=512, mode="nt", out_dtype=F32,
                name="out_proj_bwd")
    g["w_out"], g_mxu["w_out"] = _mm([(ycat, 0, dh1b, 0, T_ROWS)], 2 * D_MODEL, D_MODEL, tm=512, tn=512, mode="tn",
                                     out_dtype=F32, name="dw_out", also_mxu=True)
    ssd_prm = ssd_prm + late.grads_ready(("w_out",), g, g_mxu)
    dgate, dxr, dwa_p, dwx_p, dlru_prm = _lru_bwd(dycat, 1, proj["g"], xr, hs, wa_p.astype(_MXU), wx_p.astype(_MXU),
                                                  jnp.swapaxes(wa_p, 1, 2).astype(_MXU),
                                                  jnp.swapaxes(wx_p, 1, 2).astype(_MXU), lru_prm)
    g["lru_wa"], g["lru_wx"] = _unpair_blocks(dwa_p), _unpair_blocks(dwx_p)
    g["lru_ba"], g["lru_bx"], g["lru_lambda"], g["lru_norm_w"] = (dlru_prm[k:k + 1] for k in range(4))
    dx_lru, g["lru_conv_w"], g["lru_conv_b"] = _conv_bwd(dxr, proj["x"], p["lru_conv_w"], p["lru_conv_b"], silu=False,
                                                         name="lru_conv_bwd")
    dz, dxs_act, dbc_act, ddt, dssd_prm, g["ssd_norm_w"] = _ssd_bwd(dycat, 0, proj["z"], y_pre, xs_act, bc_act,
                                                                    proj["dt"], prev, ssd_prm, p["ssd_norm_w"], ex)
    g["ssd_dt_bias"], g["ssd_a_log"], g["ssd_d"] = (dssd_prm[k:k + 1, :SSD_HEADS] for k in range(3))
    dxs, dcw_xs, dcb_xs = _conv_bwd(dxs_act, proj["xs"], p["ssd_conv_w"][:, :SSD_WIDTH],
                                    p["ssd_conv_b"][:, :SSD_WIDTH], silu=True, name="ssd_conv_xs_bwd")
    dbc, dcw_bc, dcb_bc = _conv_bwd(dbc_act, proj["bc"], p["ssd_conv_w"][:, SSD_WIDTH:],
                                    p["ssd_conv_b"][:, SSD_WIDTH:], silu=True, name="ssd_conv_bc_bwd")
    g["ssd_conv_w"] = jnp.concatenate([dcw_xs, dcw_bc], axis=1)
    g["ssd_conv_b"] = jnp.concatenate([dcb_xs, dcb_bc], axis=1)
    dproj = {"z": dz, "xs": dxs, "bc": dbc, "dt": ddt, "g": dgate, "x": dx_lru}
    dh0, _, g["norm1_w"] = _mm_norm_bwd([(dproj[s], p["w_in_" + s], SEC_WIDTH[s]) for s in SEC_NAMES], h0,
                                        p["norm1_w"], dh1, name="in_proj_bwd")
    g["meta_tokens"] = dh0[PAD_ROWS:X_ROW0]
    late.small_ready(g, loss)
    for s in SEC_NAMES:
        wdt = SEC_WIDTH[s]
        g["w_in_" + s], g_mxu["w_in_" + s] = _mm([(dproj[s], 0, u1, 0, T_ROWS)], wdt, D_MODEL, tm=min(wdt, 512),
                                                 tn=512, mode="tn", out_dtype=F32, name="dw_in_" + s, also_mxu=True)
        if s == "bc":
            late.small_middle(g["w_in_bc"])
    return loss, dh0[X_ROW0:], g, g_mxu


MESH = pl.DeviceIdType.MESH
ANY = pl.BlockSpec(memory_space=pl.ANY)


def _my_place():
    return lax.axis_index("x"), lax.axis_index("y"), lax.axis_index("c")


def _other_chips(x, y):
    return [(1 - x, y), (x, 1 - y), (1 - x, 1 - y)]


def _gather_first(big, small):
    half = big.shape[1] // 2

    def body(big_ref, small_ref, big4, small4, send_sems, recv_sems, local_sems):
        x, y, c = _my_place()
        me = 2 * x + y
        sibling = (x, y, 1 - c)
        peers = _other_chips(x, y)
        mine = pl.ds(pl.multiple_of(c * half, 128), half)
        theirs = pl.ds(pl.multiple_of((1 - c) * half, 128), half)

        def copy(k, src, dst, dev):
            return pltpu.make_async_remote_copy(src_ref=src, dst_ref=dst, send_sem=send_sems.at[k],
                                                recv_sem=recv_sems.at[k], device_id=dev, device_id_type=MESH)

        local = [pltpu.make_async_copy(big_ref, big4.at[me], local_sems.at[0]),
                 pltpu.make_async_copy(small_ref, small4.at[me], local_sems.at[1])]
        for cp in local:
            cp.start()
        first = []
        for j, (px, py) in enumerate(peers):
            first.append(copy(j, big_ref.at[:, mine], big4.at[me, :, mine], (px, py, c)))
            first.append(copy(3 + j, small_ref, small4.at[me], (px, py, c)))
        for cp in first:
            cp.start()
        passed = []
        for j, (px, py) in enumerate(peers):
            slot = 2 * px + py
            copy(j, big_ref.at[:, mine], big4.at[slot, :, mine], (px, py, c)).wait_recv()
            passed.append(copy(6 + j, big4.at[slot, :, mine], big4.at[slot, :, mine], sibling))
            passed[-1].start()
        for j, (px, py) in enumerate(peers):
            slot = 2 * px + py
            copy(6 + j, big4.at[slot, :, theirs], big4.at[slot, :, theirs], sibling).wait_recv()
            copy(3 + j, small_ref, small4.at[slot], (px, py, c)).wait_recv()
        for cp in first + passed:
            cp.wait_send()
        for cp in local:
            cp.wait()

    return pl.pallas_call(
        body, name="gather_first", in_specs=[ANY, ANY], out_specs=[ANY, ANY],
        out_shape=[jax.ShapeDtypeStruct((N_SHARDS,) + big.shape, big.dtype),
                   jax.ShapeDtypeStruct((N_SHARDS,) + small.shape, small.dtype)],
        scratch_shapes=[pltpu.SemaphoreType.DMA((9,)), pltpu.SemaphoreType.DMA((9,)), pltpu.SemaphoreType.DMA((2,))],
    )(big, small)


def _scatter_grads(grads4):
    n = len(grads4)

    def body(*refs):
        ins, outs = refs[:n], refs[n:2 * n]
        send_sems, recv_sems = refs[2 * n:]
        x, y, c = _my_place()
        peers = _other_chips(x, y)
        for k in range(n):
            for j, (px, py) in enumerate(peers):
                pltpu.make_async_remote_copy(
                    src_ref=ins[k].at[2 * px + py], dst_ref=outs[k].at[j], send_sem=send_sems.at[3 * k + j],
                    recv_sem=recv_sems.at[3 * k + j], device_id=(px, py, c), device_id_type=MESH).start()
        for k in range(n):
            for j, (px, py) in enumerate(peers):
                pltpu.make_async_remote_copy(
                    src_ref=ins[k].at[2 * px + py], dst_ref=outs[k].at[j], send_sem=send_sems.at[3 * k + j],
                    recv_sem=recv_sems.at[3 * k + j], device_id=(px, py, c), device_id_type=MESH).wait()

    return pl.pallas_call(
        body, name="scatter_grads", in_specs=[ANY] * n, out_specs=[ANY] * n,
        out_shape=[jax.ShapeDtypeStruct((3,) + g.shape[1:], g.dtype) for g in grads4],
        scratch_shapes=[pltpu.SemaphoreType.DMA((3 * n,)), pltpu.SemaphoreType.DMA((3 * n,))],
    )(*grads4)


HBM_SPEC = pl.BlockSpec(memory_space=pltpu.HBM)
SEM_SPEC = pl.BlockSpec(memory_space=pltpu.SEMAPHORE)
SPLIT_EFFECT = pltpu.SideEffectType.DATAFLOW_SIDE_EFFECTING


def _gather_plan(bufs, x, y, c, incoming):
    plan = []
    for buf in bufs:
        for (px, py) in _other_chips(x, y):
            slot = 2 * px + py if incoming else 2 * x + y
            plan.append((buf.at[2 * x + y], buf.at[slot], (px, py, c)))
    return plan


def _scatter_plan(bufs, x, y, c, incoming):
    n = len(bufs) // 2
    plan = []
    for k in range(n):
        for j, (px, py) in enumerate(_other_chips(x, y)):
            plan.append((bufs[k].at[2 * px + py], bufs[n + k].at[j], (px, py, c)))
    return plan


def _split_start(bufs, plan, n_copies, after, *, name):
    n = len(bufs)
    extra = [] if after is None else [after]

    def body(*refs):
        ins = refs[:n]
        send_sems, recv_sems = refs[n + len(extra)], refs[n + len(extra) + 1]
        token = refs[-1]
        x, y, c = _my_place()
        for i, (src, dst, dev) in enumerate(plan(ins, x, y, c, False)):
            pltpu.make_async_remote_copy(src_ref=src, dst_ref=dst, send_sem=send_sems.at[i], recv_sem=recv_sems.at[i],
                                         device_id=dev, device_id_type=MESH).start()
        token[...] = jnp.zeros_like(token)

    outs = pl.pallas_call(
        body, name=name,
        out_shape=(pltpu.SemaphoreType.DMA((n_copies,)), pltpu.SemaphoreType.DMA((n_copies,)),
                   *[pltpu.HBM(b.shape, b.dtype) for b in bufs], jax.ShapeDtypeStruct((8, 128), F32)),
        in_specs=[HBM_SPEC] * n + [ANY] * len(extra),
        out_specs=(SEM_SPEC, SEM_SPEC, *[HBM_SPEC] * n, pl.BlockSpec(memory_space=pltpu.VMEM)),
        input_output_aliases={k: 2 + k for k in range(n)},
        compiler_params=pltpu.CompilerParams(has_side_effects=SPLIT_EFFECT),
    )(*[pltpu.with_memory_space_constraint(b, pltpu.HBM) for b in bufs], *extra)
    return outs[0], outs[1], list(outs[2:2 + n]), outs[-1]


def _split_wait(bufs, send_sems, recv_sems, plan, after, *, name):
    n = len(bufs)

    def body(*refs):
        ins = refs[:n]
        send_sems_ref, recv_sems_ref = refs[n], refs[n + 1]
        x, y, c = _my_place()
        for i, (src, dst, dev) in enumerate(plan(ins, x, y, c, True)):
            cp = pltpu.make_async_remote_copy(src_ref=src, dst_ref=dst, send_sem=send_sems_ref.at[i],
                                              recv_sem=recv_sems_ref.at[i], device_id=dev, device_id_type=MESH)
            cp.wait_send()
            cp.wait_recv()

    outs = pl.pallas_call(
        body, name=name, out_shape=tuple(pltpu.HBM(b.shape, b.dtype) for b in bufs),
        in_specs=[HBM_SPEC] * n + [SEM_SPEC, SEM_SPEC, ANY], out_specs=tuple([HBM_SPEC] * n),
        input_output_aliases={k: k for k in range(n)},
        compiler_params=pltpu.CompilerParams(has_side_effects=SPLIT_EFFECT),
    )(*bufs, send_sems, recv_sems, after)
    return list(outs)


def _fill_own_slot(shard, me_arr, *, name):
    r, c = shard.shape
    tile, steps, imap = _elementwise_tile(r, c)

    def body(me_ref, x_ref, o_ref):
        o_ref[0] = x_ref[...].astype(_MXU)

    return pl.pallas_call(
        body, name=name,
        grid_spec=pltpu.PrefetchScalarGridSpec(
            num_scalar_prefetch=1, grid=(steps,),
            in_specs=[pl.BlockSpec(tile, lambda i, me: imap(i))],
            out_specs=pl.BlockSpec((1,) + tile, lambda i, me: (me[0],) + imap(i))),
        out_shape=jax.ShapeDtypeStruct((N_SHARDS, r, c), _MXU),
        compiler_params=_cparams("parallel"),
    )(me_arr, shard)


def _swap_with_sibling(parts):
    n = len(parts)

    def body(*refs):
        ins, outs = refs[:n], refs[n:2 * n]
        send_sems, recv_sems = refs[2 * n:]
        x, y, c = _my_place()
        copies = [pltpu.make_async_remote_copy(
            src_ref=ins[k], dst_ref=outs[k], send_sem=send_sems.at[k], recv_sem=recv_sems.at[k],
            device_id=(x, y, 1 - c), device_id_type=MESH) for k in range(n)]
        for cp in copies:
            cp.start()
        for cp in copies:
            cp.wait()

    return pl.pallas_call(
        body, name="swap_with_sibling", in_specs=[ANY] * n, out_specs=[ANY] * n,
        out_shape=[jax.ShapeDtypeStruct(a.shape, a.dtype) for a in parts],
        scratch_shapes=[pltpu.SemaphoreType.DMA((n,)), pltpu.SemaphoreType.DMA((n,))],
    )(*parts)


def _other_devices(x, y, c):
    out = []
    for mask in range(1, N_DEV):
        px, py, pc = x ^ (mask >> 2 & 1), y ^ (mask >> 1 & 1), c ^ (mask & 1)
        out.append(((px, py, pc), 4 * px + 2 * py + pc))
    return out


def _pieces_plan(bufs, x, y, c, incoming):
    pack, land = bufs
    me = 4 * x + 2 * y + c
    return [(pack.at[num], land.at[num if incoming else me], dev) for dev, num in _other_devices(x, y, c)]


def _spread_plan(bufs, x, y, c, incoming):
    piece, land = bufs
    me = 4 * x + 2 * y + c
    return [(piece, land.at[num if incoming else me], dev) for dev, num in _other_devices(x, y, c)]


def _sum_pieces(pack, land, dev_arr, *, name):
    def body(dev_ref, pack_ref, land_ref, o_ref):
        dev = dev_ref[0]
        own = pack_ref[dev]
        acc = None
        for d in range(N_DEV):
            term = jnp.where(dev == d, own, land_ref[d])
            acc = term if acc is None else acc + term
        o_ref[...] = acc

    vmem = pl.BlockSpec(memory_space=pltpu.VMEM)
    return pl.pallas_call(
        body, name=name, in_specs=[pl.BlockSpec(memory_space=pltpu.SMEM), vmem, vmem], out_specs=vmem,
        out_shape=jax.ShapeDtypeStruct(pack.shape[1:], F32),
    )(dev_arr, pack, land)


def _join_pieces(piece, land, dev_arr, *, name):
    def body(dev_ref, piece_ref, land_ref, o_ref):
        dev = dev_ref[0]
        for d in range(N_DEV):
            o_ref[d] = jnp.where(dev == d, piece_ref[...], land_ref[d])

    vmem = pl.BlockSpec(memory_space=pltpu.VMEM)
    return pl.pallas_call(
        body, name=name, in_specs=[pl.BlockSpec(memory_space=pltpu.SMEM), vmem, vmem], out_specs=vmem,
        out_shape=jax.ShapeDtypeStruct(land.shape, F32),
    )(dev_arr, piece, land)


def _adamw_native(ws, gs, ms, vs):
    n = len(ws)

    def body(*refs):
        for k in range(n):
            w_ref, g_ref, m_ref, v_ref = (refs[j * n + k] for j in range(4))
            delta, m_new, v_new = _adamw_math(w_ref[...], g_ref[...], m_ref[...], v_ref[...])
            refs[4 * n + k][...] = delta
            refs[5 * n + k][...] = m_new
            refs[6 * n + k][...] = v_new

    vmem = pl.BlockSpec(memory_space=pltpu.VMEM)
    shapes = [jax.ShapeDtypeStruct(a.shape, F32) for a in ws]
    outs = pl.pallas_call(
        body, name="adamw_small", in_specs=[vmem] * (4 * n), out_specs=[vmem] * (3 * n), out_shape=shapes * 3,
        compiler_params=pltpu.CompilerParams(vmem_limit_bytes=VMEM_LIMIT_BYTES),
    )(*ws, *gs, *ms, *vs)
    return outs[:n], outs[n:2 * n], outs[2 * n:]


def _elementwise_tile(rows, cols, limit=256):
    for t in range(limit, 15, -16):
        if rows % t == 0:
            return (t, cols), rows // t, lambda i: (i, 0)
    assert cols % limit == 0
    return (rows, limit), cols // limit, lambda i: (0, i)


def _partial_sum(g4, land, me_arr, *, name):
    _, r, c = g4.shape
    tile, steps, imap = _elementwise_tile(r, c)

    def body(me_ref, own_ref, land_ref, o_ref):
        acc = own_ref[0]
        for j in range(3):
            acc = acc + land_ref[j].astype(F32)
        o_ref[...] = acc

    return pl.pallas_call(
        body, name=name,
        grid_spec=pltpu.PrefetchScalarGridSpec(
            num_scalar_prefetch=1, grid=(steps,),
            in_specs=[pl.BlockSpec((1,) + tile, lambda i, me: (me[0],) + imap(i)),
                      pl.BlockSpec((3,) + tile, lambda i, me: (0,) + imap(i))],
            out_specs=pl.BlockSpec(tile, lambda i, me: imap(i))),
        out_shape=jax.ShapeDtypeStruct((r, c), F32),
        compiler_params=_cparams("parallel"),
    )(me_arr, g4, land)


def _adamw_math(w, g, m, v):
    m = ADAM_B1 * m + (1.0 - ADAM_B1) * g
    v = ADAM_B2 * v + (1.0 - ADAM_B2) * (g * g)
    m_hat = m / (1.0 - ADAM_B1 ** ADAM_STEP)
    v_hat = v / (1.0 - ADAM_B2 ** ADAM_STEP)
    delta = -ADAM_LR * (m_hat / (jnp.sqrt(v_hat) + ADAM_EPS) + ADAM_WD * w)
    return delta, m, v


def _adamw(w, grad_parts, m, v, *, name):
    r, c = w.shape
    tile_shape, steps, imap = _elementwise_tile(r, c)
    n = len(grad_parts)

    def body(*refs):
        w_ref, m_ref, v_ref = refs[:3]
        g_refs = refs[3:3 + n]
        g_out, d_out, m_out, v_out = refs[3 + n:]
        g = g_refs[0][...]
        for k in range(1, n):
            g = g + g_refs[k][...]
        delta, m_new, v_new = _adamw_math(w_ref[...], g, m_ref[...], v_ref[...])
        g_out[...] = g
        d_out[...] = delta
        m_out[...] = m_new
        v_out[...] = v_new

    tile = pl.BlockSpec(tile_shape, imap)
    return pl.pallas_call(
        body, name=name, grid=(steps,), in_specs=[tile] * (3 + n), out_specs=[tile] * 4,
        out_shape=[jax.ShapeDtypeStruct((r, c), F32)] * 4,
        compiler_params=_cparams("parallel"),
    )(w, m, v, *grad_parts)


WEIGHT_NAMES = ("meta_tokens", "norm1_w", "w_in", "ssd_conv_w", "ssd_conv_b", "ssd_dt_bias", "ssd_a_log", "ssd_d",
                "ssd_norm_w", "lru_conv_w", "lru_conv_b", "lru_wa", "lru_ba", "lru_wx", "lru_bx", "lru_lambda",
                "lru_norm_w", "w_out", "norm2_w", "w_gate", "w_up", "w_down", "final_norm_w")
BIG = ("w_in", "w_out", "w_gate", "w_up", "w_down")
FFN = ("w_gate", "w_up", "w_down")
LATE = ("w_out",) + FFN
SMALL_SHARDED = {"meta_tokens": (N_META, D_MODEL), "ssd_conv_w": (CONV_K, 1536), "lru_conv_w": (CONV_K, LRU_WIDTH)}
SMALL = tuple(n for n in WEIGHT_NAMES if n not in BIG)
PACK_COLS = 1024


def _pack(arrays, row_multiple):
    flat = jnp.concatenate([a.reshape(-1) for a in arrays])
    rows = -(-flat.shape[0] // (row_multiple * PACK_COLS)) * row_multiple
    return jnp.pad(flat, (0, rows * PACK_COLS - flat.shape[0])).reshape(rows, PACK_COLS)


def _unpack(pack, shapes):
    flat = pack.reshape(-1)
    out, off = [], 0
    for s in shapes:
        size = math.prod(s)
        out.append(flat[off:off + size].reshape(s))
        off += size
    return out


def _unshard_cols(g4):
    return jnp.swapaxes(g4, 0, 1).reshape(g4.shape[1], -1)


COL_SHARDED = ("w_in", "w_gate", "w_up")
IN_ROWS = {"z": (0, 1024), "xs": (1024, 2048), "bc": (2048, 2560), "dt": (2560, 2576), "g": (2576, 3600),
           "x": (3600, IN_COLS)}


def _rows_view(name, block):
    return jnp.swapaxes(block[0], 0, 1) if name in COL_SHARDED else block[0]


def _param_view(name, rows):
    return (jnp.swapaxes(rows, 0, 1) if name in COL_SHARDED else rows)[None]


def kernel(x, meta_tokens, norm1_w, w_in, ssd_conv_w, ssd_conv_b, ssd_dt_bias, ssd_a_log, ssd_d, ssd_norm_w, lru_conv_w, lru_conv_b, lru_wa, lru_ba, lru_wx, lru_bx, lru_lambda, lru_norm_w, w_out, norm2_w, w_gate, w_up, w_down, final_norm_w, loss_target, m_meta_tokens, m_norm1_w, m_w_in, m_ssd_conv_w, m_ssd_conv_b, m_ssd_dt_bias, m_ssd_a_log, m_ssd_d, m_ssd_norm_w, m_lru_conv_w, m_lru_conv_b, m_lru_wa, m_lru_ba, m_lru_wx, m_lru_bx, m_lru_lambda, m_lru_norm_w, m_w_out, m_norm2_w, m_w_gate, m_w_up, m_w_down, m_final_norm_w, v_meta_tokens, v_norm1_w, v_w_in, v_ssd_conv_w, v_ssd_conv_b, v_ssd_dt_bias, v_ssd_a_log, v_ssd_d, v_ssd_norm_w, v_lru_conv_w, v_lru_conv_b, v_lru_wa, v_lru_ba, v_lru_wx, v_lru_bx, v_lru_lambda, v_lru_norm_w, v_w_out, v_norm2_w, v_w_gate, v_w_up, v_w_down, v_final_norm_w):
    w = dict(zip(WEIGHT_NAMES, (meta_tokens, norm1_w, w_in, ssd_conv_w, ssd_conv_b, ssd_dt_bias, ssd_a_log, ssd_d, ssd_norm_w, lru_conv_w, lru_conv_b, lru_wa, lru_ba, lru_wx, lru_bx, lru_lambda, lru_norm_w, w_out, norm2_w, w_gate, w_up, w_down, final_norm_w)))
    m = dict(zip(WEIGHT_NAMES, (m_meta_tokens, m_norm1_w, m_w_in, m_ssd_conv_w, m_ssd_conv_b, m_ssd_dt_bias, m_ssd_a_log, m_ssd_d, m_ssd_norm_w, m_lru_conv_w, m_lru_conv_b, m_lru_wa, m_lru_ba, m_lru_wx, m_lru_bx, m_lru_lambda, m_lru_norm_w, m_w_out, m_norm2_w, m_w_gate, m_w_up, m_w_down, m_final_norm_w)))
    v = dict(zip(WEIGHT_NAMES, (v_meta_tokens, v_norm1_w, v_w_in, v_ssd_conv_w, v_ssd_conv_b, v_ssd_dt_bias, v_ssd_a_log, v_ssd_d, v_ssd_norm_w, v_lru_conv_w, v_lru_conv_b, v_lru_wa, v_lru_ba, v_lru_wx, v_lru_bx, v_lru_lambda, v_lru_norm_w, v_w_out, v_norm2_w, v_w_gate, v_w_up, v_w_down, v_final_norm_w)))
    me = 2 * lax.axis_index("x") + lax.axis_index("y")

    big2d = {n: _rows_view(n, w[n]) for n in BIG}
    small_local = jnp.concatenate([w["meta_tokens"].reshape(-1), w["ssd_conv_w"].reshape(-1),
                                   w["lru_conv_w"].reshape(-1)])[None]
    me_arr = me.astype(jnp.int32).reshape(1)
    dev_arr = (2 * me + lax.axis_index("c")).astype(jnp.int32).reshape(1)
    w_in4, small4 = _gather_first(big2d["w_in"].astype(_MXU), small_local)
    w_in_full = w_in4.reshape(-1, D_MODEL)
    sm = small4[:, 0]
    meta_full = _unshard_cols(sm[:, :4096].reshape(N_SHARDS, N_META, 256))
    ssd_conv_w_full = _unshard_cols(sm[:, 4096:5632].reshape(N_SHARDS, CONV_K, 384))
    lru_conv_w_full = _unshard_cols(sm[:, 5632:].reshape(N_SHARDS, CONV_K, 256))
    slots = {n: _fill_own_slot(big2d[n], me_arr, name="own_slot_" + n) for n in LATE}
    out_send, out_recv, out_bufs, tok_a = _split_start([slots["w_out"]], _gather_plan, 3, small4,
                                                       name="gather_w_out_start")
    ffn_send, ffn_recv, ffn_bufs, tok_b = _split_start([slots[n] for n in FFN], _gather_plan, 9, tok_a,
                                                       name="gather_ffn_start")

    p = {"w_in_" + s: w_in_full[lo:hi] for s, (lo, hi) in IN_ROWS.items()}
    p["w_in_dt"] = jnp.pad(p["w_in_dt"], ((0, SEC_WIDTH["dt"] - SSD_HEADS), (0, 0)))
    p.update({"ssd_conv_w": ssd_conv_w_full, "lru_conv_w": lru_conv_w_full,
              "lru_wa": w["lru_wa"][0], "lru_wx": w["lru_wx"][0], "final_norm_w": w["final_norm_w"][None]})
    for n in ("norm1_w", "ssd_conv_b", "ssd_dt_bias", "ssd_a_log", "ssd_d", "ssd_norm_w", "lru_conv_b", "lru_ba",
              "lru_bx", "lru_lambda", "lru_norm_w", "norm2_w"):
        p[n] = w[n]
    p["norm1_w"] = p["norm1_w"] + tok_b[:1, :1]

    class Late:
        def __init__(self):
            self.pending = []

        def w_out(self, after):
            (buf,) = _split_wait(out_bufs, out_send, out_recv, _gather_plan, after, name="gather_w_out_wait")
            return buf.reshape(-1, D_MODEL)

        def ffn(self, after):
            bufs = _split_wait(ffn_bufs, ffn_send, ffn_recv, _gather_plan, after, name="gather_ffn_wait")
            return tuple(b.reshape(-1, D_MODEL) for b in bufs)

        def grads_ready(self, names, g, g_mxu):
            srcs = [g_mxu[n].reshape(N_SHARDS, -1, D_MODEL) for n in names]
            lands = [lax.empty((3,) + s.shape[1:], _MXU) for s in srcs]
            tag = "_".join(names)
            send, recv, bufs, tok = _split_start(srcs + lands, _scatter_plan, 3 * len(names), g[names[-1]],
                                                 name="scatter_" + tag + "_start")
            self.pending.append((names, send, recv, bufs, tag))
            return tok[:1, :1]

        def landed(self, after):
            land = {}
            for names, send, recv, bufs, tag in self.pending:
                bufs = _split_wait(bufs, send, recv, _scatter_plan, after, name="scatter_" + tag + "_wait")
                land.update(zip(names, bufs[len(names):]))
            return land

        def small_ready(self, g, loss):
            pack = _pack([g[n] for n in SMALL] + [loss[0, :1]], 8 * N_DEV)
            pack = pack.reshape(N_DEV, -1, PACK_COLS)
            self.small = _split_start([pack, lax.empty(pack.shape, F32)], _pieces_plan, N_DEV - 1, loss,
                                      name="small_pieces_start")

        def small_middle(self, after):
            send, recv, bufs, _ = self.small
            pack, land = _split_wait(bufs, send, recv, _pieces_plan, after, name="small_pieces_wait")
            piece = _sum_pieces(pack, land, dev_arr, name="small_pieces_sum")
            self.small = _split_start([piece, lax.empty(pack.shape, F32)], _spread_plan, N_DEV - 1, None,
                                      name="small_spread_start")

        def small_sum(self, after):
            send, recv, bufs, _ = self.small
            piece, land = _split_wait(bufs, send, recv, _spread_plan, after, name="small_spread_wait")
            return _join_pieces(piece, land, dev_arr, name="small_join")

    late = Late()

    loss, grad_x, g, g_mxu = _local_step(x[0], loss_target[0], meta_full, p, late)

    g["w_in"] = jnp.concatenate([g["w_in_" + s][:hi - lo] for s, (lo, hi) in IN_ROWS.items()], axis=0)
    g_mxu["w_in"] = jnp.concatenate([g_mxu["w_in_" + s][:hi - lo] for s, (lo, hi) in IN_ROWS.items()], axis=0)
    g4 = {n: g[n].reshape(N_SHARDS, -1, D_MODEL) for n in BIG}
    (land_w_in,) = _scatter_grads([g_mxu["w_in"].reshape(N_SHARDS, -1, D_MODEL)])
    land = late.landed(land_w_in)
    land["w_in"] = land_w_in
    part = {n: _partial_sum(g4[n], land[n], me_arr, name="partial_" + n) for n in BIG}
    sib = dict(zip(BIG, _swap_with_sibling([part[n] for n in BIG])))

    small_full_shape = {n: (SMALL_SHARDED[n] if n in SMALL_SHARDED else w[n].shape) for n in SMALL}
    red_list = _unpack(late.small_sum(sib["w_in"]), [small_full_shape[n] for n in SMALL] + [(1,)])
    loss_total = red_list[-1][0]
    g_small = {}
    for n, arr in zip(SMALL, red_list[:-1]):
        if n in SMALL_SHARDED:
            cols = SMALL_SHARDED[n][1] // N_SHARDS
            arr = lax.dynamic_slice_in_dim(arr, me * cols, cols, axis=1)
        g_small[n] = arr.reshape(w[n].shape)

    grad, delta, new_m, new_v = {}, {}, {}, {}
    for n in BIG:
        outs = _adamw(big2d[n], [part[n], sib[n]], _rows_view(n, m[n]), _rows_view(n, v[n]), name="adamw_" + n)
        grad[n], delta[n], new_m[n], new_v[n] = (_param_view(n, o) for o in outs)
    two_d = lambda a: a.reshape(1, -1) if a.ndim == 1 else a
    deltas, new_ms, new_vs = _adamw_native(*[[two_d(d[n]) for n in SMALL] for d in (w, g_small, m, v)])
    for n, dn, mn, vn in zip(SMALL, deltas, new_ms, new_vs):
        grad[n], delta[n], new_m[n], new_v[n] = (g_small[n], dn.reshape(w[n].shape), mn.reshape(w[n].shape),
                                                 vn.reshape(w[n].shape))

    return (loss_total, grad_x[None], *[grad[n] for n in WEIGHT_NAMES], *[delta[n] for n in WEIGHT_NAMES],
            *[new_m[n] for n in WEIGHT_NAMES], *[new_v[n] for n in WEIGHT_NAMES])
```

```python
import functools
import math

import jax
import jax.numpy as jnp
from jax import lax
from jax.experimental import pallas as pl
from jax.experimental.pallas import tpu as pltpu

F32 = jnp.float32
_MXU = jnp.bfloat16

D_MODEL = 1024
SEQ = 2048
N_META = 16
CHUNK = 128
T_ROWS = 2176
N_CHUNKS = T_ROWS // CHUNK
PAD_ROWS = T_ROWS - SEQ - N_META
X_ROW0 = PAD_ROWS + N_META
SSD_HEADS = 16
SSD_HEAD_DIM = 64
SSD_STATE = 128
SSD_GROUPS = 2
SSD_HPG = SSD_HEADS // SSD_GROUPS
SSD_WIDTH = 1024
LRU_WIDTH = 1024
LRU_C = 8.0
D_FF = 2816
EPS = 1e-6
IN_COLS = 4624
N_SHARDS = 4
N_DEV = 8

ADAM_LR = 0.001
ADAM_B1 = 0.9
ADAM_B2 = 0.999
ADAM_EPS = 1e-08
ADAM_WD = 0.01
ADAM_STEP = 10

VMEM_LIMIT_BYTES = 56 * 1024 * 1024

NN = (((1,), (0,)), ((), ()))
NT = (((1,), (1,)), ((), ()))
TN = (((0,), (0,)), ((), ()))


def _cparams(*sem):
    return pltpu.CompilerParams(dimension_semantics=sem, vmem_limit_bytes=VMEM_LIMIT_BYTES)


def _dot(a, b, dims=NN):
    return lax.dot_general(a.astype(_MXU), b.astype(_MXU), dims, preferred_element_type=F32)


def _dot_exact(a, b, dims=NN):
    return lax.dot_general(a, b, dims, preferred_element_type=F32, precision=lax.Precision.HIGHEST)


def _sigmoid(x):
    return 1.0 / (1.0 + jnp.exp(-x))


def _softplus(x):
    return jnp.maximum(x, 0.0) + jnp.log(1.0 + jnp.exp(-jnp.abs(x)))


def _silu(x):
    return x * _sigmoid(x)


def _silu_grad(x):
    s = _sigmoid(x)
    return s * (1.0 + x * (1.0 - s))


_GELU_C = math.sqrt(2.0 / math.pi)


def _gelu_and_grad(x):
    inner = _GELU_C * (x + 0.044715 * x * x * x)
    t = jnp.tanh(inner)
    g = 0.5 * x * (1.0 + t)
    dg = 0.5 * (1.0 + t) + 0.5 * x * (1.0 - t * t) * _GELU_C * (1.0 + 3.0 * 0.044715 * x * x)
    return g, dg


def _rms_fwd(x, w):
    rstd = lax.rsqrt(jnp.mean(x * x, axis=-1, keepdims=True) + EPS)
    return x * rstd * w


def _rms_bwd(x, w, dy):
    rstd = lax.rsqrt(jnp.mean(x * x, axis=-1, keepdims=True) + EPS)
    xhat = x * rstd
    dxhat = dy * w
    dx = rstd * (dxhat - xhat * jnp.mean(dxhat * xhat, axis=-1, keepdims=True))
    return dx, dy * xhat


def _mm(terms, m, n, *, tm, tn, mode, out_dtype, name, residual=None, n_outer=False, also_mxu=False):
    gm, gn = m // tm, n // tn
    assert gm * tm == m and gn * tn == n
    if n_outer:
        grid = (gn, gm)
        mi = lambda g0, g1: g1
        ni = lambda g0, g1: g0
    else:
        grid = (gm, gn)
        mi = lambda g0, g1: g0
        ni = lambda g0, g1: g1
    in_specs, args = [], []
    for (a, ka, b, kb, k) in terms:
        if mode == "tn":
            in_specs.append(pl.BlockSpec((k, tm), lambda g0, g1, ka=ka: (ka, mi(g0, g1))))
        else:
            in_specs.append(pl.BlockSpec((tm, k), lambda g0, g1, ka=ka: (mi(g0, g1), ka)))
        if mode == "nt":
            in_specs.append(pl.BlockSpec((tn, k), lambda g0, g1, kb=kb: (ni(g0, g1), kb)))
        else:
            in_specs.append(pl.BlockSpec((k, tn), lambda g0, g1, kb=kb: (kb, ni(g0, g1))))
        args += [a, b]
    if residual is not None:
        in_specs.append(pl.BlockSpec((tm, tn), lambda g0, g1: (mi(g0, g1), ni(g0, g1))))
        args.append(residual)
    dims = {"nn": NN, "nt": NT, "tn": TN}[mode]
    n_terms = len(terms)
    has_res = residual is not None

    n_in = len(args)

    def body(*refs):
        acc = None
        for t in range(n_terms):
            d = lax.dot_general(refs[2 * t][...], refs[2 * t + 1][...], dims, preferred_element_type=F32)
            acc = d if acc is None else acc + d
        if has_res:
            acc = acc + refs[2 * n_terms][...]
        refs[n_in][...] = acc.astype(out_dtype)
        if also_mxu:
            refs[n_in + 1][...] = acc.astype(_MXU)

    tile = pl.BlockSpec((tm, tn), lambda g0, g1: (mi(g0, g1), ni(g0, g1)))
    shape = jax.ShapeDtypeStruct((m, n), out_dtype)
    return pl.pallas_call(
        body, name=name, grid=grid, in_specs=in_specs,
        out_specs=[tile, tile] if also_mxu else tile,
        out_shape=[shape, jax.ShapeDtypeStruct((m, n), _MXU)] if also_mxu else shape,
        compiler_params=_cparams("parallel", "parallel"),
    )(*args)


def _embed(x, meta):
    def body(x_ref, meta_ref, o_ref):
        i = pl.program_id(0)

        @pl.when(i == 0)
        def _():
            o_ref[0:PAD_ROWS, :] = jnp.zeros((PAD_ROWS, D_MODEL), F32)
            o_ref[PAD_ROWS:CHUNK, :] = meta_ref[...]

        @pl.when(i > 0)
        def _():
            o_ref[...] = x_ref[...]

    return pl.pallas_call(
        body, name="embed", grid=(N_CHUNKS,),
        in_specs=[pl.BlockSpec((CHUNK, D_MODEL), lambda i: (jnp.maximum(i - 1, 0), 0)),
                  pl.BlockSpec((N_META, D_MODEL), lambda i: (0, 0))],
        out_specs=pl.BlockSpec((CHUNK, D_MODEL), lambda i: (i, 0)),
        out_shape=jax.ShapeDtypeStruct((T_ROWS, D_MODEL), F32),
        compiler_params=_cparams("parallel"),
    )(x, meta)


def _rmsnorm(h, w, *, name, tm=544):
    def body(h_ref, w_ref, o_ref):
        o_ref[...] = _rms_fwd(h_ref[...], w_ref[...]).astype(_MXU)

    return pl.pallas_call(
        body, name=name, grid=(T_ROWS // tm,),
        in_specs=[pl.BlockSpec((tm, D_MODEL), lambda i: (i, 0)), pl.BlockSpec((1, D_MODEL), lambda i: (0, 0))],
        out_specs=pl.BlockSpec((tm, D_MODEL), lambda i: (i, 0)),
        out_shape=jax.ShapeDtypeStruct((T_ROWS, D_MODEL), _MXU),
        compiler_params=_cparams("parallel"),
    )(h, w)


def _loss_head(h2, target, fw):
    def body(h_ref, t_ref, w_ref, loss_ref, dh_ref, dhb_ref, dw_ref, acc_ref):
        i = pl.program_id(0)

        @pl.when(i == 0)
        def _():
            acc_ref[...] = jnp.zeros_like(acc_ref)
            dw_ref[...] = jnp.zeros_like(dw_ref)

        h = h_ref[...]
        w = w_ref[...]
        y = _rms_fwd(h, w)
        live = (i > 0).astype(F32)
        err = (y - t_ref[...]) * live
        acc_ref[...] += jnp.sum(err * err, axis=0, keepdims=True)
        dy = err * (1.0 / D_MODEL)
        dx, dwr = _rms_bwd(h, w, dy)
        dh_ref[...] = dx
        dhb_ref[...] = dx.astype(_MXU)
        dw_ref[...] += jnp.sum(dwr, axis=0, keepdims=True)

        @pl.when(i == N_CHUNKS - 1)
        def _():
            tot = jnp.sum(acc_ref[...], axis=1, keepdims=True) * (0.5 / D_MODEL)
            loss_ref[...] = jnp.broadcast_to(tot, (1, 128))

    return pl.pallas_call(
        body, name="loss_head", grid=(N_CHUNKS,),
        in_specs=[pl.BlockSpec((CHUNK, D_MODEL), lambda i: (i, 0)),
                  pl.BlockSpec((CHUNK, D_MODEL), lambda i: (jnp.maximum(i - 1, 0), 0)),
                  pl.BlockSpec((1, D_MODEL), lambda i: (0, 0))],
        out_specs=[pl.BlockSpec((1, 128), lambda i: (0, 0)),
                   pl.BlockSpec((CHUNK, D_MODEL), lambda i: (i, 0)),
                   pl.BlockSpec((CHUNK, D_MODEL), lambda i: (i, 0)),
                   pl.BlockSpec((1, D_MODEL), lambda i: (0, 0))],
        out_shape=[jax.ShapeDtypeStruct((1, 128), F32),
                   jax.ShapeDtypeStruct((T_ROWS, D_MODEL), F32),
                   jax.ShapeDtypeStruct((T_ROWS, D_MODEL), _MXU),
                   jax.ShapeDtypeStruct((1, D_MODEL), F32)],
        scratch_shapes=[pltpu.VMEM((1, D_MODEL), F32)],
        compiler_params=_cparams("arbitrary"),
    )(h2, target, fw)


def _mm_norm_bwd(terms, h, w, dres, *, name, tm=272):
    n_terms = len(terms)
    in_specs, args = [], []
    for (a, b, k) in terms:
        in_specs += [pl.BlockSpec((tm, k), lambda i: (i, 0)), pl.BlockSpec((k, D_MODEL), lambda i: (0, 0))]
        args += [a, b]
    in_specs += [pl.BlockSpec((tm, D_MODEL), lambda i: (i, 0)), pl.BlockSpec((1, D_MODEL), lambda i: (0, 0)),
                 pl.BlockSpec((tm, D_MODEL), lambda i: (i, 0))]
    args += [h, w, dres]

    def body(*refs):
        h_ref, w_ref, dres_ref, dh_ref, dhb_ref, dw_ref = refs[2 * n_terms:]

        @pl.when(pl.program_id(0) == 0)
        def _():
            dw_ref[...] = jnp.zeros_like(dw_ref)

        du = None
        for t in range(n_terms):
            d = lax.dot_general(refs[2 * t][...], refs[2 * t + 1][...], NN, preferred_element_type=F32)
            du = d if du is None else du + d
        dx, dwr = _rms_bwd(h_ref[...], w_ref[...], du)
        dh = dres_ref[...] + dx
        dh_ref[...] = dh
        dhb_ref[...] = dh.astype(_MXU)
        dw_ref[...] += jnp.sum(dwr, axis=0, keepdims=True)

    return pl.pallas_call(
        body, name=name, grid=(T_ROWS // tm,), in_specs=in_specs,
        out_specs=[pl.BlockSpec((tm, D_MODEL), lambda i: (i, 0)), pl.BlockSpec((tm, D_MODEL), lambda i: (i, 0)),
                   pl.BlockSpec((1, D_MODEL), lambda i: (0, 0))],
        out_shape=[jax.ShapeDtypeStruct((T_ROWS, D_MODEL), F32), jax.ShapeDtypeStruct((T_ROWS, D_MODEL), _MXU),
                   jax.ShapeDtypeStruct((1, D_MODEL), F32)],
        compiler_params=_cparams("arbitrary"),
    )(*args)


FFN_TM = 272
FFN_TN = 1408


def _ffn_up(u2, wg_t, wu_t):
    def body(u_ref, wg_ref, wu_ref, gp_ref, up_ref, act_ref):
        u = u_ref[...]
        gp = lax.dot_general(u, wg_ref[...], NT, preferred_element_type=F32)
        up = lax.dot_general(u, wu_ref[...], NT, preferred_element_type=F32)
        gp_ref[...] = gp
        up_ref[...] = up
        act_ref[...] = (_silu(gp) * up).astype(_MXU)

    tile = pl.BlockSpec((FFN_TM, FFN_TN), lambda j, i: (i, j))
    return pl.pallas_call(
        body, name="ffn_up", grid=(D_FF // FFN_TN, T_ROWS // FFN_TM),
        in_specs=[pl.BlockSpec((FFN_TM, D_MODEL), lambda j, i: (i, 0)),
                  pl.BlockSpec((FFN_TN, D_MODEL), lambda j, i: (j, 0)),
                  pl.BlockSpec((FFN_TN, D_MODEL), lambda j, i: (j, 0))],
        out_specs=[tile, tile, tile],
        out_shape=[jax.ShapeDtypeStruct((T_ROWS, D_FF), F32), jax.ShapeDtypeStruct((T_ROWS, D_FF), F32),
                   jax.ShapeDtypeStruct((T_ROWS, D_FF), _MXU)],
        compiler_params=_cparams("parallel", "parallel"),
    )(u2, wg_t, wu_t)


def _ffn_bwd_act(dh2b, wd, gp, up):
    def body(dh_ref, wd_ref, gp_ref, up_ref, dgp_ref, dup_ref):
        dact = lax.dot_general(dh_ref[...], wd_ref[...], NT, preferred_element_type=F32)
        gp = gp_ref[...]
        dgp_ref[...] = (dact * up_ref[...] * _silu_grad(gp)).astype(_MXU)
        dup_ref[...] = (dact * _silu(gp)).astype(_MXU)

    tile = pl.BlockSpec((FFN_TM, FFN_TN), lambda j, i: (i, j))
    return pl.pallas_call(
        body, name="ffn_bwd_act", grid=(D_FF // FFN_TN, T_ROWS // FFN_TM),
        in_specs=[pl.BlockSpec((FFN_TM, D_MODEL), lambda j, i: (i, 0)),
                  pl.BlockSpec((FFN_TN, D_MODEL), lambda j, i: (j, 0)), tile, tile],
        out_specs=[tile, tile],
        out_shape=[jax.ShapeDtypeStruct((T_ROWS, D_FF), _MXU), jax.ShapeDtypeStruct((T_ROWS, D_FF), _MXU)],
        compiler_params=_cparams("parallel", "parallel"),
    )(dh2b, wd, gp, up)


CONV_TC = 512
CONV_K = 4


def _conv_pre(x_ref, wv, bv, c):
    tc = wv.shape[1]
    r0 = c * CHUNK
    cur = x_ref[r0:r0 + CHUNK, :]
    prev8 = jnp.zeros((8, tc), F32) if c == 0 else x_ref[r0 - 8:r0, :]
    cat = jnp.concatenate([prev8, cur], axis=0)
    shifted = [cur] + [pltpu.roll(cat, s, 0)[8:8 + CHUNK] for s in range(1, CONV_K)]
    pre = bv
    for s in range(CONV_K):
        pre = pre + shifted[s] * wv[CONV_K - 1 - s:CONV_K - s]
    return pre, shifted


def _row_mask(c):
    if c > 0:
        return None
    return (lax.broadcasted_iota(jnp.int32, (CHUNK, 1), 0) >= PAD_ROWS).astype(F32)


def _conv_fwd(x, w, b, *, silu, name):
    cols = x.shape[1]
    tc = min(CONV_TC, cols)

    def body(x_ref, w_ref, b_ref, o_ref):
        wv, bv = w_ref[...], b_ref[...]
        for c in range(N_CHUNKS):
            pre, _ = _conv_pre(x_ref, wv, bv, c)
            y = _silu(pre) if silu else pre
            mask = _row_mask(c)
            if mask is not None:
                y = y * mask
            o_ref[c * CHUNK:(c + 1) * CHUNK, :] = y

    return pl.pallas_call(
        body, name=name, grid=(cols // tc,),
        in_specs=[pl.BlockSpec((T_ROWS, tc), lambda j: (0, j)), pl.BlockSpec((CONV_K, tc), lambda j: (0, j)),
                  pl.BlockSpec((1, tc), lambda j: (0, j))],
        out_specs=pl.BlockSpec((T_ROWS, tc), lambda j: (0, j)),
        out_shape=jax.ShapeDtypeStruct((T_ROWS, cols), F32),
        compiler_params=_cparams("parallel"),
    )(x, w, b)


def _conv_bwd(dy, x, w, b, *, silu, name):
    cols = x.shape[1]
    tc = min(CONV_TC, cols)

    def body(dy_ref, x_ref, w_ref, b_ref, dx_ref, dw_ref, db_ref):
        wv, bv = w_ref[...], b_ref[...]
        next8 = jnp.zeros((8, tc), F32)
        dws = [jnp.zeros((1, tc), F32) for _ in range(CONV_K)]
        db = jnp.zeros((1, tc), F32)
        for c in reversed(range(N_CHUNKS)):
            pre, shifted = _conv_pre(x_ref, wv, bv, c)
            dpre = dy_ref[c * CHUNK:(c + 1) * CHUNK, :]
            if silu:
                dpre = dpre * _silu_grad(pre)
            mask = _row_mask(c)
            if mask is not None:
                dpre = dpre * mask
            cat = jnp.concatenate([dpre, next8], axis=0)
            dx = dpre * wv[CONV_K - 1:CONV_K]
            for s in range(1, CONV_K):
                dx = dx + pltpu.roll(cat, CHUNK + 8 - s, 0)[0:CHUNK] * wv[CONV_K - 1 - s:CONV_K - s]
            dx_ref[c * CHUNK:(c + 1) * CHUNK, :] = dx.astype(_MXU)
            for s in range(CONV_K):
                k = CONV_K - 1 - s
                dws[k] = dws[k] + jnp.sum(dpre * shifted[s], axis=0, keepdims=True)
            db = db + jnp.sum(dpre, axis=0, keepdims=True)
            next8 = dpre[0:8]
        dw_ref[...] = jnp.concatenate(dws, axis=0)
        db_ref[...] = db

    return pl.pallas_call(
        body, name=name, grid=(cols // tc,),
        in_specs=[pl.BlockSpec((T_ROWS, tc), lambda j: (0, j)), pl.BlockSpec((T_ROWS, tc), lambda j: (0, j)),
                  pl.BlockSpec((CONV_K, tc), lambda j: (0, j)), pl.BlockSpec((1, tc), lambda j: (0, j))],
        out_specs=[pl.BlockSpec((T_ROWS, tc), lambda j: (0, j)), pl.BlockSpec((CONV_K, tc), lambda j: (0, j)),
                   pl.BlockSpec((1, tc), lambda j: (0, j))],
        out_shape=[jax.ShapeDtypeStruct((T_ROWS, cols), _MXU), jax.ShapeDtypeStruct((CONV_K, cols), F32),
                   jax.ShapeDtypeStruct((1, cols), F32)],
        compiler_params=_cparams("parallel"),
    )(dy, x, w, b)


def _ssd_chunk_common(dt_raw, prm, c):
    a_row = -jnp.exp(prm[1:2])
    dt = _softplus(dt_raw + prm[0:1])
    rows = lax.broadcasted_iota(jnp.int32, (CHUNK, 1), 0)
    real = jnp.logical_or(c > 0, rows >= PAD_ROWS)
    dt = jnp.where(real, dt, 0.0)
    li = lax.broadcasted_iota(jnp.int32, (CHUNK, CHUNK), 0)
    si = lax.broadcasted_iota(jnp.int32, (CHUNK, CHUNK), 1)
    causal = li >= si
    tri = causal.astype(F32)
    cs = _dot_exact(tri, dt * a_row)
    return dt, a_row, cs, cs.T, causal, tri, real


def _gated_norm_fwd(y, z, w):
    g = y * _silu(z)
    half = SSD_WIDTH // SSD_GROUPS
    outs = [_rms_fwd(g[:, k * half:(k + 1) * half], w[:, k * half:(k + 1) * half]) for k in range(SSD_GROUPS)]
    return jnp.concatenate(outs, axis=1)


GROUP_W = SSD_WIDTH // SSD_GROUPS
PAIR_W = 2 * SSD_HEAD_DIM
STATE_SHAPE = (SSD_GROUPS, SSD_STATE, GROUP_W)


def _head_expander():
    r = lax.broadcasted_iota(jnp.int32, (128, SSD_WIDTH), 0)
    c = lax.broadcasted_iota(jnp.int32, (128, SSD_WIDTH), 1)
    return (c // SSD_HEAD_DIM == r).astype(F32)


def _ssd_expand(dt, cs, prm, ex):
    cs_x = _dot_exact(cs, ex)
    cs_last_x = cs_x[CHUNK - 1:CHUNK, :]
    return (_dot_exact(dt, ex), _dot_exact(prm, ex)[2:3], jnp.exp(cs_x), jnp.exp(cs_last_x),
            jnp.exp(cs_last_x - cs_x))


def _ssd_fwd(xs, bc, dt_raw, z, prm, norm_w, ex):
    def body(xs_ref, bc_ref, dt_ref, z_ref, prm_ref, nw_ref, ex_ref, y_ref, yn_ref, prev_ref, state):
        c = pl.program_id(0)

        @pl.when(c == 0)
        def _():
            state[...] = jnp.zeros_like(state)

        prm = prm_ref[...]
        dt, a_row, cs, cs_t, causal, _, _ = _ssd_chunk_common(dt_ref[...], prm, c)
        dt_x, d_x, e_cs_x, e_last_x, dec_x = _ssd_expand(dt, cs, prm, ex_ref[...])
        xs_all = xs_ref[...]
        bc_all = bc_ref[...]
        xdt = xs_all * dt_x
        xdec = xdt * dec_x
        lane_lo = lax.broadcasted_iota(jnp.int32, (1, PAIR_W), 1) < SSD_HEAD_DIM
        for g in range(SSD_GROUPS):
            gs = slice(g * GROUP_W, (g + 1) * GROUP_W)
            b_g = bc_all[:, g * SSD_STATE:(g + 1) * SSD_STATE]
            c_g = bc_all[:, (SSD_GROUPS + g) * SSD_STATE:(SSD_GROUPS + g + 1) * SSD_STATE]
            st = state[g]
            prev_ref[0, g] = st
            y_off = _dot(c_g, st) * e_cs_x[:, gs]
            state[g] = st * e_last_x[:, gs] + _dot(b_g.T, xdec[:, gs])
            cb = _dot(c_g, b_g, NT)
            for k in range(SSD_HPG // 2):
                h0 = g * SSD_HPG + 2 * k
                ps = slice(h0 * SSD_HEAD_DIM, h0 * SSD_HEAD_DIM + PAIR_W)
                xdt_pair = xdt[:, ps]
                yd = []
                for h in (h0, h0 + 1):
                    lmat = jnp.where(causal, jnp.exp(cs[:, h:h + 1] - cs_t[h:h + 1, :]), 0.0)
                    yd.append(_dot(cb * lmat, xdt_pair))
                y_ref[:, ps] = (jnp.where(lane_lo, yd[0], yd[1]) + y_off[:, k * PAIR_W:(k + 1) * PAIR_W]
                                + xs_all[:, ps] * d_x[:, ps])
        yn_ref[...] = _gated_norm_fwd(y_ref[...], z_ref[...], nw_ref[...]).astype(_MXU)

    row = lambda w: pl.BlockSpec((CHUNK, w), lambda c: (c, 0))
    return pl.pallas_call(
        body, name="ssd_fwd", grid=(N_CHUNKS,),
        in_specs=[row(SSD_WIDTH), row(512), row(128), row(SSD_WIDTH),
                  pl.BlockSpec((8, 128), lambda c: (0, 0)), pl.BlockSpec((1, SSD_WIDTH), lambda c: (0, 0)),
                  pl.BlockSpec((128, SSD_WIDTH), lambda c: (0, 0))],
        out_specs=[row(SSD_WIDTH), row(SSD_WIDTH),
                   pl.BlockSpec((1,) + STATE_SHAPE, lambda c: (c, 0, 0, 0))],
        out_shape=[jax.ShapeDtypeStruct((T_ROWS, SSD_WIDTH), F32), jax.ShapeDtypeStruct((T_ROWS, SSD_WIDTH), _MXU),
                   jax.ShapeDtypeStruct((N_CHUNKS,) + STATE_SHAPE, F32)],
        scratch_shapes=[pltpu.VMEM(STATE_SHAPE, F32)],
        compiler_params=_cparams("arbitrary"),
    )(xs, bc, dt_raw, z, prm, norm_w, ex)


def _ssd_bwd(dyn, dyn_block, z, y_pre, xs, bc, dt_raw, prev, prm, norm_w, ex):
    def body(dyn_ref, z_ref, y_ref, xs_ref, bc_ref, dt_ref, prev_ref, prm_ref, nw_ref, ex_ref,
             dz_ref, dxs_ref, dbc_ref, ddt_ref, dprm_ref, dnw_ref, dstate):
        step = pl.program_id(0)
        c = N_CHUNKS - 1 - step

        @pl.when(step == 0)
        def _():
            dstate[...] = jnp.zeros_like(dstate)
            dprm_ref[...] = jnp.zeros_like(dprm_ref)
            dnw_ref[...] = jnp.zeros_like(dnw_ref)

        prm = prm_ref[...]
        dt, a_row, cs, cs_t, causal, tri, real = _ssd_chunk_common(dt_ref[...], prm, c)
        realf = real.astype(F32)
        z = z_ref[...]
        y_all = y_ref[...]
        nw = nw_ref[...]
        dyn_all = dyn_ref[...]
        sz = _silu(z)
        gated = y_all * sz
        half = SSD_WIDTH // SSD_GROUPS
        dgs, dnws = [], []
        for k in range(SSD_GROUPS):
            sl = slice(k * half, (k + 1) * half)
            dgk, dwk = _rms_bwd(gated[:, sl], nw[:, sl], dyn_all[:, sl])
            dgs.append(dgk)
            dnws.append(jnp.sum(dwk, axis=0, keepdims=True))
        dgated = jnp.concatenate(dgs, axis=1)
        dnw_ref[...] += jnp.concatenate(dnws, axis=1)
        dz_ref[...] = (dgated * y_all * _silu_grad(z)).astype(_MXU)
        dy_all = dgated * sz

        ex = ex_ref[...]
        dt_x, d_x, e_cs_x, e_last_x, dec_x = _ssd_expand(dt, cs, prm, ex)
        xs_all = xs_ref[...]
        bc_all = bc_ref[...]
        xdt = xs_all * dt_x
        xdt_mxu = xdt.astype(_MXU).astype(F32)
        xdec = xdt * dec_x
        dcp = dy_all * e_cs_x
        lane_lo = lax.broadcasted_iota(jnp.int32, (1, PAIR_W), 1) < SSD_HEAD_DIM
        upper = (lax.broadcasted_iota(jnp.int32, (CHUNK, CHUNK), 0)
                 <= lax.broadcasted_iota(jnp.int32, (CHUNK, CHUNK), 1))
        last_row = (lax.broadcasted_iota(jnp.int32, (CHUNK, 1), 0) == CHUNK - 1).astype(F32)
        dbs, dcs_, dxdt_parts, last_parts = [], [], [], []
        for g in range(SSD_GROUPS):
            gs = slice(g * GROUP_W, (g + 1) * GROUP_W)
            b_g = bc_all[:, g * SSD_STATE:(g + 1) * SSD_STATE]
            c_g = bc_all[:, (SSD_GROUPS + g) * SSD_STATE:(SSD_GROUPS + g + 1) * SSD_STATE]
            prev_t = prev_ref[0, g]
            dst = dstate[g]
            dc_g = _dot(dcp[:, gs], prev_t, NT)
            db_g = _dot(xdec[:, gs], dst, NT)
            dxdt_state = _dot(b_g, dst) * dec_x[:, gs]
            dstate[g] = dst * e_last_x[:, gs] + _dot(c_g.T, dcp[:, gs])
            last_parts.append(jnp.sum(xdt_mxu[:, gs] * dxdt_state, axis=0, keepdims=True)
                              + jnp.sum(dst * prev_t, axis=0, keepdims=True) * e_last_x[:, gs])
            cb_t = _dot(b_g, c_g, NT)
            dcb_t = jnp.zeros((CHUNK, CHUNK), F32)
            for k in range(SSD_HPG // 2):
                h0 = g * SSD_HPG + 2 * k
                ps = slice(h0 * SSD_HEAD_DIM, h0 * SSD_HEAD_DIM + PAIR_W)
                dy_pair = dy_all[:, ps]
                xdt_pair = xdt[:, ps]
                dd = []
                for h in (h0, h0 + 1):
                    lmat_t = jnp.where(upper, jnp.exp(cs_t[h:h + 1, :] - cs[:, h:h + 1]), 0.0)
                    dd.append(_dot(cb_t * lmat_t, dy_pair))
                    mine = lane_lo if h == h0 else jnp.logical_not(lane_lo)
                    dcb_t = dcb_t + _dot(jnp.where(mine, xdt_pair, 0.0), dy_pair, NT) * lmat_t
                dxdt_parts.append(jnp.where(lane_lo, dd[0], dd[1]) + dxdt_state[:, k * PAIR_W:(k + 1) * PAIR_W])
            dc_g = dc_g + _dot(dcb_t, b_g, TN)
            db_g = db_g + _dot(dcb_t, c_g)
            dbs.append(db_g * realf)
            dcs_.append(dc_g * realf)
        dbc_ref[...] = jnp.concatenate(dbs + dcs_, axis=1)
        dxdt = jnp.concatenate(dxdt_parts, axis=1)
        dxs_ref[...] = (dxdt * dt_x + dy_all * d_x) * realf
        ddt_all = _dot_exact(dxdt * xs_all, ex, NT)
        rows = jnp.concatenate([jnp.concatenate(last_parts, axis=1), jnp.sum(dy_all * xs_all, axis=0, keepdims=True),
                                jnp.zeros((6, SSD_WIDTH), F32)], axis=0)
        rows = _dot_exact(rows, ex, NT)
        dd_row = rows[1:2]
        dy_mxu = dy_all.astype(_MXU).astype(F32)
        dcs_all = (_dot_exact(dy_mxu * (y_all - xs_all * d_x), ex, NT) - _dot_exact(xdt_mxu * dxdt, ex, NT)
                   + last_row * rows[0:1])
        dda = _dot_exact(tri, dcs_all, TN)
        ddt = (ddt_all + dda * a_row) * realf
        ddt_raw = ddt * _sigmoid(dt_ref[...] + prm[0:1])
        ddt_ref[...] = ddt_raw.astype(_MXU)
        da_log = jnp.sum(dda * dt, axis=0, keepdims=True) * a_row
        dprm_ref[0:1, :] += jnp.sum(ddt_raw, axis=0, keepdims=True)
        dprm_ref[1:2, :] += da_log
        dprm_ref[2:3, :] += dd_row

    rev = lambda w, blk=0: pl.BlockSpec((CHUNK, w), lambda s, blk=blk: (N_CHUNKS - 1 - s, blk))
    return pl.pallas_call(
        body, name="ssd_bwd", grid=(N_CHUNKS,),
        in_specs=[rev(SSD_WIDTH, dyn_block), rev(SSD_WIDTH), rev(SSD_WIDTH), rev(SSD_WIDTH), rev(512), rev(128),
                  pl.BlockSpec((1,) + STATE_SHAPE, lambda s: (N_CHUNKS - 1 - s, 0, 0, 0)),
                  pl.BlockSpec((8, 128), lambda s: (0, 0)), pl.BlockSpec((1, SSD_WIDTH), lambda s: (0, 0)),
                  pl.BlockSpec((128, SSD_WIDTH), lambda s: (0, 0))],
        out_specs=[rev(SSD_WIDTH), rev(SSD_WIDTH), rev(512), rev(128),
                   pl.BlockSpec((8, 128), lambda s: (0, 0)), pl.BlockSpec((1, SSD_WIDTH), lambda s: (0, 0))],
        out_shape=[jax.ShapeDtypeStruct((T_ROWS, SSD_WIDTH), _MXU), jax.ShapeDtypeStruct((T_ROWS, SSD_WIDTH), F32),
                   jax.ShapeDtypeStruct((T_ROWS, 512), F32), jax.ShapeDtypeStruct((T_ROWS, 128), _MXU),
                   jax.ShapeDtypeStruct((8, 128), F32), jax.ShapeDtypeStruct((1, SSD_WIDTH), F32)],
        scratch_shapes=[pltpu.VMEM(STATE_SHAPE, F32)],
        compiler_params=_cparams("arbitrary"),
    )(dyn, z, y_pre, xs, bc, dt_raw, prev, prm, norm_w, ex)


LRU_PAIRS = 8


def _lru_gates(xr, wa_ref, wx_ref, prm):
    pre_r, pre_i = [], []
    for k in range(LRU_PAIRS):
        xk = xr[:, k * 128:(k + 1) * 128]
        pre_r.append(_dot(xk, wa_ref[k]))
        pre_i.append(_dot(xk, wx_ref[k]))
    r = _sigmoid(jnp.concatenate(pre_r, axis=1) + prm[0:1])
    i = _sigmoid(jnp.concatenate(pre_i, axis=1) + prm[1:2])
    sp = _softplus(-prm[2:3])
    log_a = (-LRU_C) * r * sp
    a = jnp.exp(log_a)
    s = jnp.sqrt(-jnp.tanh(log_a) * (a * a + 1.0))
    return r, i, a, s, sp


def _lru_fwd(xr, gate, wa, wx, prm):
    def body(xr_ref, g_ref, wa_ref, wx_ref, prm_ref, hs_ref, yn_ref, carry, a_s, u_s):
        @pl.when(pl.program_id(0) == 0)
        def _():
            carry[...] = jnp.zeros_like(carry)

        prm = prm_ref[...]
        xr_t = xr_ref[...]
        _, i, a, s, _ = _lru_gates(xr_t, wa_ref, wx_ref, prm)
        a_s[...] = a
        u_s[...] = s * (i * xr_t)
        rid = lax.broadcasted_iota(jnp.int32, (8, LRU_WIDTH), 0)

        def group(k, h):
            off = pl.multiple_of(k * 8, 8)
            a8 = a_s[pl.ds(off, 8), :]
            u8 = u_s[pl.ds(off, 8), :]
            out = jnp.zeros((8, LRU_WIDTH), F32)
            for r_ in range(8):
                h = a8[r_:r_ + 1] * h + u8[r_:r_ + 1]
                out = jnp.where(rid == r_, h, out)
            hs_ref[pl.ds(off, 8), :] = out
            return h

        carry[0:1, :] = lax.fori_loop(0, CHUNK // 8, group, carry[0:1, :])
        gel, _ = _gelu_and_grad(g_ref[...])
        yn_ref[...] = _rms_fwd(gel * hs_ref[...], prm[3:4]).astype(_MXU)

    row = pl.BlockSpec((CHUNK, LRU_WIDTH), lambda t: (t, 0))
    wspec = pl.BlockSpec((LRU_PAIRS, 128, 128), lambda t: (0, 0, 0))
    return pl.pallas_call(
        body, name="lru_fwd", grid=(N_CHUNKS,),
        in_specs=[row, row, wspec, wspec, pl.BlockSpec((8, LRU_WIDTH), lambda t: (0, 0))],
        out_specs=[row, row],
        out_shape=[jax.ShapeDtypeStruct((T_ROWS, LRU_WIDTH), F32), jax.ShapeDtypeStruct((T_ROWS, LRU_WIDTH), _MXU)],
        scratch_shapes=[pltpu.VMEM((8, LRU_WIDTH), F32), pltpu.VMEM((CHUNK, LRU_WIDTH), F32),
                        pltpu.VMEM((CHUNK, LRU_WIDTH), F32)],
        compiler_params=_cparams("arbitrary"),
    )(xr, gate, wa, wx, prm)


def _lru_bwd(dyn, dyn_block, gate, xr, hs, wa, wx, wa_t, wx_t, prm):
    def body(dyn_ref, g_ref, xr_ref, hs_ref, hsp_ref, wa_ref, wx_ref, wat_ref, wxt_ref, prm_ref,
             dg_ref, dxr_ref, dwa_ref, dwx_ref, dprm_ref, carry, a_s, d_s):
        step = pl.program_id(0)
        tile = N_CHUNKS - 1 - step

        @pl.when(step == 0)
        def _():
            carry[...] = jnp.zeros_like(carry)
            dwa_ref[...] = jnp.zeros_like(dwa_ref)
            dwx_ref[...] = jnp.zeros_like(dwx_ref)
            dprm_ref[...] = jnp.zeros_like(dprm_ref)

        prm = prm_ref[...]
        xr_t = xr_ref[...]
        r, i, a, s, sp = _lru_gates(xr_t, wa_ref, wx_ref, prm)
        hs_t = hs_ref[...]
        gel, dgel = _gelu_and_grad(g_ref[...])
        dy, dnw = _rms_bwd(gel * hs_t, prm[3:4], dyn_ref[...])
        dg_ref[...] = (dy * hs_t * dgel).astype(_MXU)
        a_s[...] = a
        d_s[...] = dy * gel
        rid = lax.broadcasted_iota(jnp.int32, (8, LRU_WIDTH), 0)

        def group(k, cr):
            off = pl.multiple_of((CHUNK // 8 - 1 - k) * 8, 8)
            a8 = a_s[pl.ds(off, 8), :]
            d8 = d_s[pl.ds(off, 8), :]
            out = jnp.zeros((8, LRU_WIDTH), F32)
            for r_ in reversed(range(8)):
                dht = d8[r_:r_ + 1] + cr
                out = jnp.where(rid == r_, dht, out)
                cr = a8[r_:r_ + 1] * dht
            d_s[pl.ds(off, 8), :] = out
            return cr

        carry[0:1, :] = lax.fori_loop(0, CHUNK // 8, group, carry[0:1, :])
        dht = d_s[...]
        before = hsp_ref[CHUNK - 8:CHUNK, :][7:8] * (tile > 0).astype(F32)
        first = lax.broadcasted_iota(jnp.int32, (CHUNK, 1), 0) == 0
        hprev = jnp.where(first, before, pltpu.roll(hs_t, 1, 0))
        da = dht * hprev
        ixr = i * xr_t
        ds = dht * ixr
        dlog_a = da * a - ds * (a * a) / s
        dr = dlog_a * ((-LRU_C) * sp)
        dsp = jnp.sum(dlog_a * ((-LRU_C) * r), axis=0, keepdims=True)
        dlam = dsp * (-_sigmoid(-prm[2:3]))
        di = dht * s * xr_t
        dpre_r = dr * r * (1.0 - r)
        dpre_i = di * i * (1.0 - i)
        dxr = dht * s * i
        parts = []
        for k in range(LRU_PAIRS):
            sl = slice(k * 128, (k + 1) * 128)
            parts.append(_dot(dpre_r[:, sl], wat_ref[k]) + _dot(dpre_i[:, sl], wxt_ref[k]))
            dwa_ref[k] += _dot(xr_t[:, sl], dpre_r[:, sl], TN)
            dwx_ref[k] += _dot(xr_t[:, sl], dpre_i[:, sl], TN)
        dxr_ref[...] = dxr + jnp.concatenate(parts, axis=1)
        dprm_ref[0:1, :] += jnp.sum(dpre_r, axis=0, keepdims=True)
        dprm_ref[1:2, :] += jnp.sum(dpre_i, axis=0, keepdims=True)
        dprm_ref[2:3, :] += dlam
        dprm_ref[3:4, :] += jnp.sum(dnw, axis=0, keepdims=True)

    rev = lambda blk=0: pl.BlockSpec((CHUNK, LRU_WIDTH), lambda s, blk=blk: (N_CHUNKS - 1 - s, blk))
    wspec = pl.BlockSpec((LRU_PAIRS, 128, 128), lambda s: (0, 0, 0))
    return pl.pallas_call(
        body, name="lru_bwd", grid=(N_CHUNKS,),
        in_specs=[rev(dyn_block), rev(), rev(), rev(),
                  pl.BlockSpec((CHUNK, LRU_WIDTH), lambda s: (jnp.maximum(N_CHUNKS - 2 - s, 0), 0)),
                  wspec, wspec, wspec, wspec, pl.BlockSpec((8, LRU_WIDTH), lambda s: (0, 0))],
        out_specs=[rev(), rev(), wspec, wspec, pl.BlockSpec((8, LRU_WIDTH), lambda s: (0, 0))],
        out_shape=[jax.ShapeDtypeStruct((T_ROWS, LRU_WIDTH), _MXU), jax.ShapeDtypeStruct((T_ROWS, LRU_WIDTH), F32),
                   jax.ShapeDtypeStruct((LRU_PAIRS, 128, 128), F32), jax.ShapeDtypeStruct((LRU_PAIRS, 128, 128), F32),
                   jax.ShapeDtypeStruct((8, LRU_WIDTH), F32)],
        scratch_shapes=[pltpu.VMEM((8, LRU_WIDTH), F32), pltpu.VMEM((CHUNK, LRU_WIDTH), F32),
                        pltpu.VMEM((CHUNK, LRU_WIDTH), F32)],
        compiler_params=_cparams("arbitrary"),
    )(dyn, gate, xr, hs, hs, wa, wx, wa_t, wx_t, prm)


SEC_NAMES = ("z", "xs", "bc", "dt", "g", "x")
SEC_WIDTH = {"z": 1024, "xs": 1024, "bc": 512, "dt": 128, "g": 1024, "x": 1024}


def _pair_blocks(w):
    w = w.reshape(LRU_PAIRS, 2, 64, 64)
    zero = jnp.zeros((LRU_PAIRS, 64, 64), w.dtype)
    top = jnp.concatenate([w[:, 0], zero], axis=2)
    bot = jnp.concatenate([zero, w[:, 1]], axis=2)
    return jnp.concatenate([top, bot], axis=1)


def _unpair_blocks(wp):
    return jnp.stack([wp[:, :64, :64], wp[:, 64:, 64:]], axis=1).reshape(16, 64, 64)


def _pad_lanes(v, width=128):
    return jnp.pad(v, ((0, 0), (0, width - v.shape[1])))


class _Resident:
    def __init__(self, w_out, w_gate, w_up, w_down):
        self._w_out, self._ffn = w_out, (w_gate, w_up, w_down)

    def w_out(self, after):
        return self._w_out

    def ffn(self, after):
        return self._ffn

    def grads_ready(self, names, g, g_mxu):
        return jnp.zeros((1, 1), F32)

    def small_ready(self, g, loss):
        return jnp.zeros((1, 1), F32)

    def small_middle(self, after):
        return jnp.zeros((1, 1), F32)


def _after(x, *deps):
    return lax.optimization_barrier((x, *deps))[0]


def _local_step(x, target, meta, p, late):
    g, g_mxu = {}, {}
    ex = _head_expander()
    h0 = _embed(x, meta)
    u1 = _rmsnorm(h0, p["norm1_w"], name="norm1")
    proj = {}
    for s in SEC_NAMES:
        wdt = SEC_WIDTH[s]
        proj[s] = _mm([(u1, 0, p["w_in_" + s], 0, D_MODEL)], T_ROWS, wdt, tm=544, tn=min(wdt, 512), mode="nt",
                      out_dtype=F32, name="proj_" + s)
    ssd_prm = jnp.concatenate([_pad_lanes(p["ssd_dt_bias"]), _pad_lanes(p["ssd_a_log"]), _pad_lanes(p["ssd_d"]),
                               jnp.zeros((5, 128), F32)], axis=0)
    xs_act = _conv_fwd(proj["xs"], p["ssd_conv_w"][:, :SSD_WIDTH], p["ssd_conv_b"][:, :SSD_WIDTH], silu=True,
                       name="ssd_conv_xs")
    bc_act = _conv_fwd(proj["bc"], p["ssd_conv_w"][:, SSD_WIDTH:], p["ssd_conv_b"][:, SSD_WIDTH:], silu=True,
                       name="ssd_conv_bc")
    y_pre, y_ssd, prev = _ssd_fwd(xs_act, bc_act, proj["dt"], proj["z"], ssd_prm, p["ssd_norm_w"], ex)
    xr = _conv_fwd(proj["x"], p["lru_conv_w"], p["lru_conv_b"], silu=False, name="lru_conv")
    wa_p, wx_p = _pair_blocks(p["lru_wa"]), _pair_blocks(p["lru_wx"])
    lru_prm = jnp.concatenate([p["lru_ba"], p["lru_bx"], p["lru_lambda"], p["lru_norm_w"],
                               jnp.zeros((4, LRU_WIDTH), F32)], axis=0)
    hs, y_lru = _lru_fwd(xr, proj["g"], wa_p.astype(_MXU), wx_p.astype(_MXU), lru_prm)
    ycat = jnp.concatenate([y_ssd, y_lru], axis=1)
    w_out = late.w_out(ycat)
    h1 = _mm([(ycat, 0, w_out, 0, 2 * D_MODEL)], T_ROWS, D_MODEL, tm=544, tn=512, mode="nn", out_dtype=F32,
             name="out_proj", residual=h0)
    u2 = _rmsnorm(h1, p["norm2_w"], name="norm2")
    w_gate, w_up, w_down = late.ffn(u2)
    gp, up, act = _ffn_up(u2, w_gate, w_up)
    h2 = _mm([(act, 0, w_down, 0, D_FF)], T_ROWS, D_MODEL, tm=544, tn=512, mode="nn", out_dtype=F32,
             name="ffn_down", residual=h1)
    loss, dh2, dh2b, g["final_norm_w"] = _loss_head(h2, target, p["final_norm_w"])
    dgp, dup = _ffn_bwd_act(dh2b, w_down, gp, up)
    g["w_down"], g_mxu["w_down"] = _mm([(act, 0, dh2b, 0, T_ROWS)], D_FF, D_MODEL, tm=1408, tn=512, mode="tn",
                                       out_dtype=F32, name="dw_down", also_mxu=True)
    dh1, dh1b, g["norm2_w"] = _mm_norm_bwd([(dgp, w_gate, D_FF), (dup, w_up, D_FF)], h1, p["norm2_w"], dh2,
                                           name="ffn_bwd_in")
    g["w_gate"], g_mxu["w_gate"] = _mm([(dgp, 0, u2, 0, T_ROWS)], D_FF, D_MODEL, tm=1408, tn=512, mode="tn",
                                       out_dtype=F32, name="dw_gate", also_mxu=True)
    g["w_up"], g_mxu["w_up"] = _mm([(dup, 0, u2, 0, T_ROWS)], D_FF, D_MODEL, tm=1408, tn=512, mode="tn",
                                   out_dtype=F32, name="dw_up", also_mxu=True)
    dh1b = _after(dh1b, late.grads_ready(("w_down", "w_gate", "w_up"), g, g_mxu))
    g["w_out"], g_mxu["w_out"] = _mm([(ycat, 0, dh1b, 0, T_ROWS)], 2 * D_MODEL, D_MODEL, tm=512, tn=512, mode="tn",
                                     out_dtype=F32, name="dw_out", also_mxu=True)
    dh1b = _after(dh1b, late.grads_ready(("w_out",), g, g_mxu))
    dycat = _mm([(dh1b, 0, w_out, 0, D_MODEL)], T_ROWS, 2 * D_MODEL, tm=544, tn=512, mode="nt", out_dtype=F32,
                name="out_proj_bwd")
    dgate, dxr, dwa_p, dwx_p, dlru_prm = _lru_bwd(dycat, 1, proj["g"], xr, hs, wa_p.astype(_MXU), wx_p.astype(_MXU),
                                                  jnp.swapaxes(wa_p, 1, 2).astype(_MXU),
                                                  jnp.swapaxes(wx_p, 1, 2).astype(_MXU), lru_prm)
    g["lru_wa"], g["lru_wx"] = _unpair_blocks(dwa_p), _unpair_blocks(dwx_p)
    g["lru_ba"], g["lru_bx"], g["lru_lambda"], g["lru_norm_w"] = (dlru_prm[k:k + 1] for k in range(4))
    dx_lru, g["lru_conv_w"], g["lru_conv_b"] = _conv_bwd(dxr, proj["x"], p["lru_conv_w"], p["lru_conv_b"], silu=False,
                                                         name="lru_conv_bwd")
    dz, dxs_act, dbc_act, ddt, dssd_prm, g["ssd_norm_w"] = _ssd_bwd(dycat, 0, proj["z"], y_pre, xs_act, bc_act,
                                                                    proj["dt"], prev, ssd_prm, p["ssd_norm_w"], ex)
    g["ssd_dt_bias"], g["ssd_a_log"], g["ssd_d"] = (dssd_prm[k:k + 1, :SSD_HEADS] for k in range(3))
    dxs, dcw_xs, dcb_xs = _conv_bwd(dxs_act, proj["xs"], p["ssd_conv_w"][:, :SSD_WIDTH],
                                    p["ssd_conv_b"][:, :SSD_WIDTH], silu=True, name="ssd_conv_xs_bwd")
    dbc, dcw_bc, dcb_bc = _conv_bwd(dbc_act, proj["bc"], p["ssd_conv_w"][:, SSD_WIDTH:],
                                    p["ssd_conv_b"][:, SSD_WIDTH:], silu=True, name="ssd_conv_bc_bwd")
    g["ssd_conv_w"] = jnp.concatenate([dcw_xs, dcw_bc], axis=1)
    g["ssd_conv_b"] = jnp.concatenate([dcb_xs, dcb_bc], axis=1)
    dproj = {"z": dz, "xs": dxs, "bc": dbc, "dt": ddt, "g": dgate, "x": dx_lru}
    dh0, _, g["norm1_w"] = _mm_norm_bwd([(dproj[s], p["w_in_" + s], SEC_WIDTH[s]) for s in SEC_NAMES], h0,
                                        p["norm1_w"], dh1, name="in_proj_bwd")
    g["meta_tokens"] = dh0[PAD_ROWS:X_ROW0]
    u1 = _after(u1, late.small_ready(g, loss))
    for s in SEC_NAMES:
        wdt = SEC_WIDTH[s]
        g["w_in_" + s], g_mxu["w_in_" + s] = _mm([(dproj[s], 0, u1, 0, T_ROWS)], wdt, D_MODEL, tm=min(wdt, 512),
                                                 tn=512, mode="tn", out_dtype=F32, name="dw_in_" + s, also_mxu=True)
        if s == "bc":
            u1 = _after(u1, late.small_middle(g["w_in_bc"]))
    return loss, dh0[X_ROW0:], g, g_mxu


MESH = pl.DeviceIdType.MESH
ANY = pl.BlockSpec(memory_space=pl.ANY)


def _my_place():
    return lax.axis_index("x"), lax.axis_index("y"), lax.axis_index("c")


def _other_chips(x, y):
    return [(1 - x, y), (x, 1 - y), (1 - x, 1 - y)]


def _gather_first(big, small):
    half = big.shape[1] // 2

    def body(big_ref, small_ref, big4, small4, send_sems, recv_sems, local_sems):
        x, y, c = _my_place()
        me = 2 * x + y
        sibling = (x, y, 1 - c)
        peers = _other_chips(x, y)
        mine = pl.ds(pl.multiple_of(c * half, 128), half)
        theirs = pl.ds(pl.multiple_of((1 - c) * half, 128), half)

        def copy(k, src, dst, dev):
            return pltpu.make_async_remote_copy(src_ref=src, dst_ref=dst, send_sem=send_sems.at[k],
                                                recv_sem=recv_sems.at[k], device_id=dev, device_id_type=MESH)

        local = [pltpu.make_async_copy(big_ref, big4.at[me], local_sems.at[0]),
                 pltpu.make_async_copy(small_ref, small4.at[me], local_sems.at[1])]
        for cp in local:
            cp.start()
        first = []
        for j, (px, py) in enumerate(peers):
            first.append(copy(j, big_ref.at[:, mine], big4.at[me, :, mine], (px, py, c)))
            first.append(copy(3 + j, small_ref, small4.at[me], (px, py, c)))
        for cp in first:
            cp.start()
        passed = []
        for j, (px, py) in enumerate(peers):
            slot = 2 * px + py
            copy(j, big_ref.at[:, mine], big4.at[slot, :, mine], (px, py, c)).wait_recv()
            passed.append(copy(6 + j, big4.at[slot, :, mine], big4.at[slot, :, mine], sibling))
            passed[-1].start()
        for j, (px, py) in enumerate(peers):
            slot = 2 * px + py
            copy(6 + j, big4.at[slot, :, theirs], big4.at[slot, :, theirs], sibling).wait_recv()
            copy(3 + j, small_ref, small4.at[slot], (px, py, c)).wait_recv()
        for cp in first + passed:
            cp.wait_send()
        for cp in local:
            cp.wait()

    return pl.pallas_call(
        body, name="gather_first", in_specs=[ANY, ANY], out_specs=[ANY, ANY],
        out_shape=[jax.ShapeDtypeStruct((N_SHARDS,) + big.shape, big.dtype),
                   jax.ShapeDtypeStruct((N_SHARDS,) + small.shape, small.dtype)],
        scratch_shapes=[pltpu.SemaphoreType.DMA((9,)), pltpu.SemaphoreType.DMA((9,)), pltpu.SemaphoreType.DMA((2,))],
    )(big, small)


def _scatter_grads(grads4):
    n = len(grads4)

    def body(*refs):
        ins, outs = refs[:n], refs[n:2 * n]
        send_sems, recv_sems = refs[2 * n:]
        x, y, c = _my_place()
        peers = _other_chips(x, y)
        for k in range(n):
            for j, (px, py) in enumerate(peers):
                pltpu.make_async_remote_copy(
                    src_ref=ins[k].at[2 * px + py], dst_ref=outs[k].at[j], send_sem=send_sems.at[3 * k + j],
                    recv_sem=recv_sems.at[3 * k + j], device_id=(px, py, c), device_id_type=MESH).start()
        for k in range(n):
            for j, (px, py) in enumerate(peers):
                pltpu.make_async_remote_copy(
                    src_ref=ins[k].at[2 * px + py], dst_ref=outs[k].at[j], send_sem=send_sems.at[3 * k + j],
                    recv_sem=recv_sems.at[3 * k + j], device_id=(px, py, c), device_id_type=MESH).wait()

    return pl.pallas_call(
        body, name="scatter_grads", in_specs=[ANY] * n, out_specs=[ANY] * n,
        out_shape=[jax.ShapeDtypeStruct((3,) + g.shape[1:], g.dtype) for g in grads4],
        scratch_shapes=[pltpu.SemaphoreType.DMA((3 * n,)), pltpu.SemaphoreType.DMA((3 * n,))],
    )(*grads4)


HBM_SPEC = pl.BlockSpec(memory_space=pltpu.HBM)
SEM_SPEC = pl.BlockSpec(memory_space=pltpu.SEMAPHORE)
SPLIT_EFFECT = pltpu.SideEffectType.DATAFLOW_SIDE_EFFECTING


def _gather_plan(bufs, x, y, c, incoming):
    plan = []
    for buf in bufs:
        for (px, py) in _other_chips(x, y):
            slot = 2 * px + py if incoming else 2 * x + y
            plan.append((buf.at[2 * x + y], buf.at[slot], (px, py, c)))
    return plan


def _scatter_plan(bufs, x, y, c, incoming):
    n = len(bufs) // 2
    plan = []
    for k in range(n):
        for j, (px, py) in enumerate(_other_chips(x, y)):
            plan.append((bufs[k].at[2 * px + py], bufs[n + k].at[j], (px, py, c)))
    return plan


def _split_start(bufs, plan, n_copies, after, *, name):
    n = len(bufs)
    extra = [] if after is None else [after]

    def body(*refs):
        ins = refs[:n]
        send_sems, recv_sems = refs[n + len(extra)], refs[n + len(extra) + 1]
        token = refs[-1]
        x, y, c = _my_place()
        for i, (src, dst, dev) in enumerate(plan(ins, x, y, c, False)):
            pltpu.make_async_remote_copy(src_ref=src, dst_ref=dst, send_sem=send_sems.at[i], recv_sem=recv_sems.at[i],
                                         device_id=dev, device_id_type=MESH).start()
        token[...] = jnp.zeros_like(token)

    outs = pl.pallas_call(
        body, name=name,
        out_shape=(pltpu.SemaphoreType.DMA((n_copies,)), pltpu.SemaphoreType.DMA((n_copies,)),
                   *[pltpu.HBM(b.shape, b.dtype) for b in bufs], jax.ShapeDtypeStruct((8, 128), F32)),
        in_specs=[HBM_SPEC] * n + [ANY] * len(extra),
        out_specs=(SEM_SPEC, SEM_SPEC, *[HBM_SPEC] * n, pl.BlockSpec(memory_space=pltpu.VMEM)),
        input_output_aliases={k: 2 + k for k in range(n)},
        compiler_params=pltpu.CompilerParams(has_side_effects=SPLIT_EFFECT),
    )(*[pltpu.with_memory_space_constraint(b, pltpu.HBM) for b in bufs], *extra)
    return outs[0], outs[1], list(outs[2:2 + n]), outs[-1]


def _split_wait(bufs, send_sems, recv_sems, plan, after, *, name):
    n = len(bufs)

    def body(*refs):
        ins = refs[:n]
        send_sems_ref, recv_sems_ref = refs[n], refs[n + 1]
        x, y, c = _my_place()
        for i, (src, dst, dev) in enumerate(plan(ins, x, y, c, True)):
            cp = pltpu.make_async_remote_copy(src_ref=src, dst_ref=dst, send_sem=send_sems_ref.at[i],
                                              recv_sem=recv_sems_ref.at[i], device_id=dev, device_id_type=MESH)
            cp.wait_send()
            cp.wait_recv()

    outs = pl.pallas_call(
        body, name=name, out_shape=tuple(pltpu.HBM(b.shape, b.dtype) for b in bufs),
        in_specs=[HBM_SPEC] * n + [SEM_SPEC, SEM_SPEC, ANY], out_specs=tuple([HBM_SPEC] * n),
        input_output_aliases={k: k for k in range(n)},
        compiler_params=pltpu.CompilerParams(has_side_effects=SPLIT_EFFECT),
    )(*bufs, send_sems, recv_sems, after)
    return list(outs)


def _fill_own_slot(shard, me_arr, *, name):
    r, c = shard.shape
    tile, steps, imap = _elementwise_tile(r, c)

    def body(me_ref, x_ref, o_ref):
        o_ref[0] = x_ref[...].astype(_MXU)

    return pl.pallas_call(
        body, name=name,
        grid_spec=pltpu.PrefetchScalarGridSpec(
            num_scalar_prefetch=1, grid=(steps,),
            in_specs=[pl.BlockSpec(tile, lambda i, me: imap(i))],
            out_specs=pl.BlockSpec((1,) + tile, lambda i, me: (me[0],) + imap(i))),
        out_shape=jax.ShapeDtypeStruct((N_SHARDS, r, c), _MXU),
        compiler_params=_cparams("parallel"),
    )(me_arr, shard)


def _swap_with_sibling(parts):
    n = len(parts)

    def body(*refs):
        ins, outs = refs[:n], refs[n:2 * n]
        send_sems, recv_sems = refs[2 * n:]
        x, y, c = _my_place()
        copies = [pltpu.make_async_remote_copy(
            src_ref=ins[k], dst_ref=outs[k], send_sem=send_sems.at[k], recv_sem=recv_sems.at[k],
            device_id=(x, y, 1 - c), device_id_type=MESH) for k in range(n)]
        for cp in copies:
            cp.start()
        for cp in copies:
            cp.wait()

    return pl.pallas_call(
        body, name="swap_with_sibling", in_specs=[ANY] * n, out_specs=[ANY] * n,
        out_shape=[jax.ShapeDtypeStruct(a.shape, a.dtype) for a in parts],
        scratch_shapes=[pltpu.SemaphoreType.DMA((n,)), pltpu.SemaphoreType.DMA((n,))],
    )(*parts)


def _other_devices(x, y, c):
    out = []
    for mask in range(1, N_DEV):
        px, py, pc = x ^ (mask >> 2 & 1), y ^ (mask >> 1 & 1), c ^ (mask & 1)
        out.append(((px, py, pc), 4 * px + 2 * py + pc))
    return out


def _pieces_plan(bufs, x, y, c, incoming):
    pack, land = bufs
    me = 4 * x + 2 * y + c
    return [(pack.at[num], land.at[num if incoming else me], dev) for dev, num in _other_devices(x, y, c)]


def _spread_plan(bufs, x, y, c, incoming):
    piece, land = bufs
    me = 4 * x + 2 * y + c
    return [(piece, land.at[num if incoming else me], dev) for dev, num in _other_devices(x, y, c)]


def _sum_pieces(pack, land, dev_arr, *, name):
    def body(dev_ref, pack_ref, land_ref, o_ref):
        dev = dev_ref[0]
        own = pack_ref[dev]
        acc = None
        for d in range(N_DEV):
            term = jnp.where(dev == d, own, land_ref[d])
            acc = term if acc is None else acc + term
        o_ref[...] = acc

    vmem = pl.BlockSpec(memory_space=pltpu.VMEM)
    return pl.pallas_call(
        body, name=name, in_specs=[pl.BlockSpec(memory_space=pltpu.SMEM), vmem, vmem], out_specs=vmem,
        out_shape=jax.ShapeDtypeStruct(pack.shape[1:], F32),
    )(dev_arr, pack, land)


def _join_pieces(piece, land, dev_arr, *, name):
    def body(dev_ref, piece_ref, land_ref, o_ref):
        dev = dev_ref[0]
        for d in range(N_DEV):
            o_ref[d] = jnp.where(dev == d, piece_ref[...], land_ref[d])

    vmem = pl.BlockSpec(memory_space=pltpu.VMEM)
    return pl.pallas_call(
        body, name=name, in_specs=[pl.BlockSpec(memory_space=pltpu.SMEM), vmem, vmem], out_specs=vmem,
        out_shape=jax.ShapeDtypeStruct(land.shape, F32),
    )(dev_arr, piece, land)


def _adamw_native(ws, gs, ms, vs):
    n = len(ws)

    def body(*refs):
        for k in range(n):
            w_ref, g_ref, m_ref, v_ref = (refs[j * n + k] for j in range(4))
            delta, m_new, v_new = _adamw_math(w_ref[...], g_ref[...], m_ref[...], v_ref[...])
            refs[4 * n + k][...] = delta
            refs[5 * n + k][...] = m_new
            refs[6 * n + k][...] = v_new

    vmem = pl.BlockSpec(memory_space=pltpu.VMEM)
    shapes = [jax.ShapeDtypeStruct(a.shape, F32) for a in ws]
    outs = pl.pallas_call(
        body, name="adamw_small", in_specs=[vmem] * (4 * n), out_specs=[vmem] * (3 * n), out_shape=shapes * 3,
        compiler_params=pltpu.CompilerParams(vmem_limit_bytes=VMEM_LIMIT_BYTES),
    )(*ws, *gs, *ms, *vs)
    return outs[:n], outs[n:2 * n], outs[2 * n:]


def _elementwise_tile(rows, cols, limit=256):
    for t in range(limit, 15, -16):
        if rows % t == 0:
            return (t, cols), rows // t, lambda i: (i, 0)
    assert cols % limit == 0
    return (rows, limit), cols // limit, lambda i: (0, i)


def _partial_sum(g4, land, me_arr, *, name):
    _, r, c = g4.shape
    tile, steps, imap = _elementwise_tile(r, c)

    def body(me_ref, own_ref, land_ref, o_ref):
        acc = own_ref[0]
        for j in range(3):
            acc = acc + land_ref[j].astype(F32)
        o_ref[...] = acc

    return pl.pallas_call(
        body, name=name,
        grid_spec=pltpu.PrefetchScalarGridSpec(
            num_scalar_prefetch=1, grid=(steps,),
            in_specs=[pl.BlockSpec((1,) + tile, lambda i, me: (me[0],) + imap(i)),
                      pl.BlockSpec((3,) + tile, lambda i, me: (0,) + imap(i))],
            out_specs=pl.BlockSpec(tile, lambda i, me: imap(i))),
        out_shape=jax.ShapeDtypeStruct((r, c), F32),
        compiler_params=_cparams("parallel"),
    )(me_arr, g4, land)


def _adamw_math(w, g, m, v):
    m = ADAM_B1 * m + (1.0 - ADAM_B1) * g
    v = ADAM_B2 * v + (1.0 - ADAM_B2) * (g * g)
    m_hat = m / (1.0 - ADAM_B1 ** ADAM_STEP)
    v_hat = v / (1.0 - ADAM_B2 ** ADAM_STEP)
    delta = -ADAM_LR * (m_hat / (jnp.sqrt(v_hat) + ADAM_EPS) + ADAM_WD * w)
    return delta, m, v


def _adamw(w, grad_parts, m, v, *, name):
    r, c = w.shape
    tile_shape, steps, imap = _elementwise_tile(r, c)
    n = len(grad_parts)

    def body(*refs):
        w_ref, m_ref, v_ref = refs[:3]
        g_refs = refs[3:3 + n]
        g_out, d_out, m_out, v_out = refs[3 + n:]
        g = g_refs[0][...]
        for k in range(1, n):
            g = g + g_refs[k][...]
        delta, m_new, v_new = _adamw_math(w_ref[...], g, m_ref[...], v_ref[...])
        g_out[...] = g
        d_out[...] = delta
        m_out[...] = m_new
        v_out[...] = v_new

    tile = pl.BlockSpec(tile_shape, imap)
    return pl.pallas_call(
        body, name=name, grid=(steps,), in_specs=[tile] * (3 + n), out_specs=[tile] * 4,
        out_shape=[jax.ShapeDtypeStruct((r, c), F32)] * 4,
        compiler_params=_cparams("parallel"),
    )(w, m, v, *grad_parts)


WEIGHT_NAMES = ("meta_tokens", "norm1_w", "w_in", "ssd_conv_w", "ssd_conv_b", "ssd_dt_bias", "ssd_a_log", "ssd_d",
                "ssd_norm_w", "lru_conv_w", "lru_conv_b", "lru_wa", "lru_ba", "lru_wx", "lru_bx", "lru_lambda",
                "lru_norm_w", "w_out", "norm2_w", "w_gate", "w_up", "w_down", "final_norm_w")
BIG = ("w_in", "w_out", "w_gate", "w_up", "w_down")
FFN = ("w_gate", "w_up", "w_down")
LATE = ("w_out",) + FFN
SMALL_SHARDED = {"meta_tokens": (N_META, D_MODEL), "ssd_conv_w": (CONV_K, 1536), "lru_conv_w": (CONV_K, LRU_WIDTH)}
SMALL = tuple(n for n in WEIGHT_NAMES if n not in BIG)
PACK_COLS = 1024


def _pack(arrays, row_multiple):
    flat = jnp.concatenate([a.reshape(-1) for a in arrays])
    rows = -(-flat.shape[0] // (row_multiple * PACK_COLS)) * row_multiple
    return jnp.pad(flat, (0, rows * PACK_COLS - flat.shape[0])).reshape(rows, PACK_COLS)


def _unpack(pack, shapes):
    flat = pack.reshape(-1)
    out, off = [], 0
    for s in shapes:
        size = math.prod(s)
        out.append(flat[off:off + size].reshape(s))
        off += size
    return out


def _unshard_cols(g4):
    return jnp.swapaxes(g4, 0, 1).reshape(g4.shape[1], -1)


COL_SHARDED = ("w_in", "w_gate", "w_up")
IN_ROWS = {"z": (0, 1024), "xs": (1024, 2048), "bc": (2048, 2560), "dt": (2560, 2576), "g": (2576, 3600),
           "x": (3600, IN_COLS)}


def _rows_view(name, block):
    return jnp.swapaxes(block[0], 0, 1) if name in COL_SHARDED else block[0]


def _param_view(name, rows):
    return (jnp.swapaxes(rows, 0, 1) if name in COL_SHARDED else rows)[None]


def kernel(x, meta_tokens, norm1_w, w_in, ssd_conv_w, ssd_conv_b, ssd_dt_bias, ssd_a_log, ssd_d, ssd_norm_w, lru_conv_w, lru_conv_b, lru_wa, lru_ba, lru_wx, lru_bx, lru_lambda, lru_norm_w, w_out, norm2_w, w_gate, w_up, w_down, final_norm_w, loss_target, m_meta_tokens, m_norm1_w, m_w_in, m_ssd_conv_w, m_ssd_conv_b, m_ssd_dt_bias, m_ssd_a_log, m_ssd_d, m_ssd_norm_w, m_lru_conv_w, m_lru_conv_b, m_lru_wa, m_lru_ba, m_lru_wx, m_lru_bx, m_lru_lambda, m_lru_norm_w, m_w_out, m_norm2_w, m_w_gate, m_w_up, m_w_down, m_final_norm_w, v_meta_tokens, v_norm1_w, v_w_in, v_ssd_conv_w, v_ssd_conv_b, v_ssd_dt_bias, v_ssd_a_log, v_ssd_d, v_ssd_norm_w, v_lru_conv_w, v_lru_conv_b, v_lru_wa, v_lru_ba, v_lru_wx, v_lru_bx, v_lru_lambda, v_lru_norm_w, v_w_out, v_norm2_w, v_w_gate, v_w_up, v_w_down, v_final_norm_w):
    w = dict(zip(WEIGHT_NAMES, (meta_tokens, norm1_w, w_in, ssd_conv_w, ssd_conv_b, ssd_dt_bias, ssd_a_log, ssd_d, ssd_norm_w, lru_conv_w, lru_conv_b, lru_wa, lru_ba, lru_wx, lru_bx, lru_lambda, lru_norm_w, w_out, norm2_w, w_gate, w_up, w_down, final_norm_w)))
    m = dict(zip(WEIGHT_NAMES, (m_meta_tokens, m_norm1_w, m_w_in, m_ssd_conv_w, m_ssd_conv_b, m_ssd_dt_bias, m_ssd_a_log, m_ssd_d, m_ssd_norm_w, m_lru_conv_w, m_lru_conv_b, m_lru_wa, m_lru_ba, m_lru_wx, m_lru_bx, m_lru_lambda, m_lru_norm_w, m_w_out, m_norm2_w, m_w_gate, m_w_up, m_w_down, m_final_norm_w)))
    v = dict(zip(WEIGHT_NAMES, (v_meta_tokens, v_norm1_w, v_w_in, v_ssd_conv_w, v_ssd_conv_b, v_ssd_dt_bias, v_ssd_a_log, v_ssd_d, v_ssd_norm_w, v_lru_conv_w, v_lru_conv_b, v_lru_wa, v_lru_ba, v_lru_wx, v_lru_bx, v_lru_lambda, v_lru_norm_w, v_w_out, v_norm2_w, v_w_gate, v_w_up, v_w_down, v_final_norm_w)))
    me = 2 * lax.axis_index("x") + lax.axis_index("y")

    big2d = {n: _rows_view(n, w[n]) for n in BIG}
    small_local = jnp.concatenate([w["meta_tokens"].reshape(-1), w["ssd_conv_w"].reshape(-1),
                                   w["lru_conv_w"].reshape(-1)])[None]
    me_arr = me.astype(jnp.int32).reshape(1)
    dev_arr = (2 * me + lax.axis_index("c")).astype(jnp.int32).reshape(1)
    w_in4, small4 = _gather_first(big2d["w_in"].astype(_MXU), small_local)
    w_in_full = w_in4.reshape(-1, D_MODEL)
    sm = small4[:, 0]
    meta_full = _unshard_cols(sm[:, :4096].reshape(N_SHARDS, N_META, 256))
    ssd_conv_w_full = _unshard_cols(sm[:, 4096:5632].reshape(N_SHARDS, CONV_K, 384))
    lru_conv_w_full = _unshard_cols(sm[:, 5632:].reshape(N_SHARDS, CONV_K, 256))
    slots = {n: _fill_own_slot(big2d[n], me_arr, name="own_slot_" + n) for n in LATE}
    out_send, out_recv, out_bufs, tok_a = _split_start([slots["w_out"]], _gather_plan, 3, small4,
                                                       name="gather_w_out_start")
    ffn_send, ffn_recv, ffn_bufs, tok_b = _split_start([slots[n] for n in FFN], _gather_plan, 9, tok_a,
                                                       name="gather_ffn_start")

    p = {"w_in_" + s: w_in_full[lo:hi] for s, (lo, hi) in IN_ROWS.items()}
    p["w_in_dt"] = jnp.pad(p["w_in_dt"], ((0, SEC_WIDTH["dt"] - SSD_HEADS), (0, 0)))
    p.update({"ssd_conv_w": ssd_conv_w_full, "lru_conv_w": lru_conv_w_full,
              "lru_wa": w["lru_wa"][0], "lru_wx": w["lru_wx"][0], "final_norm_w": w["final_norm_w"][None]})
    for n in ("norm1_w", "ssd_conv_b", "ssd_dt_bias", "ssd_a_log", "ssd_d", "ssd_norm_w", "lru_conv_b", "lru_ba",
              "lru_bx", "lru_lambda", "lru_norm_w", "norm2_w"):
        p[n] = w[n]
    p["norm1_w"] = p["norm1_w"] + tok_b[:1, :1]

    class Late:
        def __init__(self):
            self.pending = []

        def w_out(self, after):
            (buf,) = _split_wait(out_bufs, out_send, out_recv, _gather_plan, after, name="gather_w_out_wait")
            return buf.reshape(-1, D_MODEL)

        def ffn(self, after):
            bufs = _split_wait(ffn_bufs, ffn_send, ffn_recv, _gather_plan, after, name="gather_ffn_wait")
            return tuple(b.reshape(-1, D_MODEL) for b in bufs)

        def grads_ready(self, names, g, g_mxu):
            srcs = [g_mxu[n].reshape(N_SHARDS, -1, D_MODEL) for n in names]
            lands = [lax.empty((3,) + s.shape[1:], _MXU) for s in srcs]
            tag = "_".join(names)
            send, recv, bufs, tok = _split_start(srcs + lands, _scatter_plan, 3 * len(names), g[names[-1]],
                                                 name="scatter_" + tag + "_start")
            self.pending.append((names, send, recv, bufs, tag))
            return tok[:1, :1]

        def landed(self, after):
            land = {}
            for names, send, recv, bufs, tag in self.pending:
                bufs = _split_wait(bufs, send, recv, _scatter_plan, after, name="scatter_" + tag + "_wait")
                land.update(zip(names, bufs[len(names):]))
            return land

        def small_ready(self, g, loss):
            pack = _pack([g[n] for n in SMALL] + [loss[0, :1]], 8 * N_DEV)
            pack = pack.reshape(N_DEV, -1, PACK_COLS)
            self.small = _split_start([pack, lax.empty(pack.shape, F32)], _pieces_plan, N_DEV - 1, loss,
                                      name="small_pieces_start")
            return self.small[3]

        def small_middle(self, after):
            send, recv, bufs, _ = self.small
            pack, land = _split_wait(bufs, send, recv, _pieces_plan, after, name="small_pieces_wait")
            piece = _sum_pieces(pack, land, dev_arr, name="small_pieces_sum")
            self.small = _split_start([piece, lax.empty(pack.shape, F32)], _spread_plan, N_DEV - 1, None,
                                      name="small_spread_start")
            return self.small[3]

        def small_sum(self, after):
            send, recv, bufs, _ = self.small
            piece, land = _split_wait(bufs, send, recv, _spread_plan, after, name="small_spread_wait")
            return _join_pieces(piece, land, dev_arr, name="small_join")

    late = Late()

    loss, grad_x, g, g_mxu = _local_step(x[0], loss_target[0], meta_full, p, late)

    g["w_in"] = jnp.concatenate([g["w_in_" + s][:hi - lo] for s, (lo, hi) in IN_ROWS.items()], axis=0)
    g_mxu["w_in"] = jnp.concatenate([g_mxu["w_in_" + s][:hi - lo] for s, (lo, hi) in IN_ROWS.items()], axis=0)
    g4 = {n: g[n].reshape(N_SHARDS, -1, D_MODEL) for n in BIG}
    (land_w_in,) = _scatter_grads([g_mxu["w_in"].reshape(N_SHARDS, -1, D_MODEL)])
    land = late.landed(land_w_in)
    land["w_in"] = land_w_in
    part = {n: _partial_sum(g4[n], land[n], me_arr, name="partial_" + n) for n in BIG}
    sib = dict(zip(BIG, _swap_with_sibling([part[n] for n in BIG])))

    small_full_shape = {n: (SMALL_SHARDED[n] if n in SMALL_SHARDED else w[n].shape) for n in SMALL}
    red_list = _unpack(late.small_sum(sib["w_in"]), [small_full_shape[n] for n in SMALL] + [(1,)])
    loss_total = red_list[-1][0]
    g_small = {}
    for n, arr in zip(SMALL, red_list[:-1]):
        if n in SMALL_SHARDED:
            cols = SMALL_SHARDED[n][1] // N_SHARDS
            arr = lax.dynamic_slice_in_dim(arr, me * cols, cols, axis=1)
        g_small[n] = arr.reshape(w[n].shape)

    grad, delta, new_m, new_v = {}, {}, {}, {}
    for n in BIG:
        outs = _adamw(big2d[n], [part[n], sib[n]], _rows_view(n, m[n]), _rows_view(n, v[n]), name="adamw_" + n)
        grad[n], delta[n], new_m[n], new_v[n] = (_param_view(n, o) for o in outs)
    two_d = lambda a: a.reshape(1, -1) if a.ndim == 1 else a
    deltas, new_ms, new_vs = _adamw_native(*[[two_d(d[n]) for n in SMALL] for d in (w, g_small, m, v)])
    for n, dn, mn, vn in zip(SMALL, deltas, new_ms, new_vs):
        grad[n], delta[n], new_m[n], new_v[n] = (g_small[n], dn.reshape(w[n].shape), mn.reshape(w[n].shape),
                                                 vn.reshape(w[n].shape))

    return (loss_total, grad_x[None], *[grad[n] for n in WEIGHT_NAMES], *[delta[n] for n in WEIGHT_NAMES],
            *[new_m[n] for n in WEIGHT_NAMES], *[new_v[n] for n in WEIGHT_NAMES])
```

```python
import functools
import math

import jax
import jax.numpy as jnp
from jax import lax
from jax.experimental import pallas as pl
from jax.experimental.pallas import tpu as pltpu

F32 = jnp.float32
_MXU = jnp.bfloat16

D_MODEL = 1024
SEQ = 2048
N_META = 16
CHUNK = 128
T_ROWS = 2176
N_CHUNKS = T_ROWS // CHUNK
PAD_ROWS = T_ROWS - SEQ - N_META
X_ROW0 = PAD_ROWS + N_META
SSD_HEADS = 16
SSD_HEAD_DIM = 64
SSD_STATE = 128
SSD_GROUPS = 2
SSD_HPG = SSD_HEADS // SSD_GROUPS
SSD_WIDTH = 1024
LRU_WIDTH = 1024
LRU_C = 8.0
D_FF = 2816
EPS = 1e-6
IN_COLS = 4624
N_SHARDS = 4
N_DEV = 8

ADAM_LR = 0.001
ADAM_B1 = 0.9
ADAM_B2 = 0.999
ADAM_EPS = 1e-08
ADAM_WD = 0.01
ADAM_STEP = 10

VMEM_LIMIT_BYTES = 56 * 1024 * 1024

NN = (((1,), (0,)), ((), ()))
NT = (((1,), (1,)), ((), ()))
TN = (((0,), (0,)), ((), ()))


def _cparams(*sem):
    return pltpu.CompilerParams(dimension_semantics=sem, vmem_limit_bytes=VMEM_LIMIT_BYTES)


def _dot(a, b, dims=NN):
    return lax.dot_general(a.astype(_MXU), b.astype(_MXU), dims, preferred_element_type=F32)


def _dot_exact(a, b, dims=NN):
    return lax.dot_general(a, b, dims, preferred_element_type=F32, precision=lax.Precision.HIGHEST)


def _sigmoid(x):
    return 1.0 / (1.0 + jnp.exp(-x))


def _softplus(x):
    return jnp.maximum(x, 0.0) + jnp.log(1.0 + jnp.exp(-jnp.abs(x)))


def _silu(x):
    return x * _sigmoid(x)


def _silu_grad(x):
    s = _sigmoid(x)
    return s * (1.0 + x * (1.0 - s))


_GELU_C = math.sqrt(2.0 / math.pi)


def _gelu_and_grad(x):
    inner = _GELU_C * (x + 0.044715 * x * x * x)
    t = jnp.tanh(inner)
    g = 0.5 * x * (1.0 + t)
    dg = 0.5 * (1.0 + t) + 0.5 * x * (1.0 - t * t) * _GELU_C * (1.0 + 3.0 * 0.044715 * x * x)
    return g, dg


def _rms_fwd(x, w):
    rstd = lax.rsqrt(jnp.mean(x * x, axis=-1, keepdims=True) + EPS)
    return x * rstd * w


def _rms_bwd(x, w, dy):
    rstd = lax.rsqrt(jnp.mean(x * x, axis=-1, keepdims=True) + EPS)
    xhat = x * rstd
    dxhat = dy * w
    dx = rstd * (dxhat - xhat * jnp.mean(dxhat * xhat, axis=-1, keepdims=True))
    return dx, dy * xhat


def _mm(terms, m, n, *, tm, tn, mode, out_dtype, name, residual=None, n_outer=False, also_mxu=False, behind=()):
    gm, gn = m // tm, n // tn
    assert gm * tm == m and gn * tn == n
    if n_outer:
        grid = (gn, gm)
        mi = lambda g0, g1: g1
        ni = lambda g0, g1: g0
    else:
        grid = (gm, gn)
        mi = lambda g0, g1: g0
        ni = lambda g0, g1: g1
    in_specs, args = [], []
    for (a, ka, b, kb, k) in terms:
        if mode == "tn":
            in_specs.append(pl.BlockSpec((k, tm), lambda g0, g1, ka=ka: (ka, mi(g0, g1))))
        else:
            in_specs.append(pl.BlockSpec((tm, k), lambda g0, g1, ka=ka: (mi(g0, g1), ka)))
        if mode == "nt":
            in_specs.append(pl.BlockSpec((tn, k), lambda g0, g1, kb=kb: (ni(g0, g1), kb)))
        else:
            in_specs.append(pl.BlockSpec((k, tn), lambda g0, g1, kb=kb: (kb, ni(g0, g1))))
        args += [a, b]
    if residual is not None:
        in_specs.append(pl.BlockSpec((tm, tn), lambda g0, g1: (mi(g0, g1), ni(g0, g1))))
        args.append(residual)
    dims = {"nn": NN, "nt": NT, "tn": TN}[mode]
    n_terms = len(terms)
    has_res = residual is not None
    in_specs += [pl.BlockSpec(memory_space=pl.ANY)] * len(behind)
    args += list(behind)
    n_in = len(args)

    def body(*refs):
        acc = None
        for t in range(n_terms):
            d = lax.dot_general(refs[2 * t][...], refs[2 * t + 1][...], dims, preferred_element_type=F32)
            acc = d if acc is None else acc + d
        if has_res:
            acc = acc + refs[2 * n_terms][...]
        refs[n_in][...] = acc.astype(out_dtype)
        if also_mxu:
            refs[n_in + 1][...] = acc.astype(_MXU)

    tile = pl.BlockSpec((tm, tn), lambda g0, g1: (mi(g0, g1), ni(g0, g1)))
    shape = jax.ShapeDtypeStruct((m, n), out_dtype)
    return pl.pallas_call(
        body, name=name, grid=grid, in_specs=in_specs,
        out_specs=[tile, tile] if also_mxu else tile,
        out_shape=[shape, jax.ShapeDtypeStruct((m, n), _MXU)] if also_mxu else shape,
        compiler_params=_cparams("parallel", "parallel"),
    )(*args)


def _embed(x, meta):
    def body(x_ref, meta_ref, o_ref):
        i = pl.program_id(0)

        @pl.when(i == 0)
        def _():
            o_ref[0:PAD_ROWS, :] = jnp.zeros((PAD_ROWS, D_MODEL), F32)
            o_ref[PAD_ROWS:CHUNK, :] = meta_ref[...]

        @pl.when(i > 0)
        def _():
            o_ref[...] = x_ref[...]

    return pl.pallas_call(
        body, name="embed", grid=(N_CHUNKS,),
        in_specs=[pl.BlockSpec((CHUNK, D_MODEL), lambda i: (jnp.maximum(i - 1, 0), 0)),
                  pl.BlockSpec((N_META, D_MODEL), lambda i: (0, 0))],
        out_specs=pl.BlockSpec((CHUNK, D_MODEL), lambda i: (i, 0)),
        out_shape=jax.ShapeDtypeStruct((T_ROWS, D_MODEL), F32),
        compiler_params=_cparams("parallel"),
    )(x, meta)


def _rmsnorm(h, w, *, name, tm=544):
    def body(h_ref, w_ref, o_ref):
        o_ref[...] = _rms_fwd(h_ref[...], w_ref[...]).astype(_MXU)

    return pl.pallas_call(
        body, name=name, grid=(T_ROWS // tm,),
        in_specs=[pl.BlockSpec((tm, D_MODEL), lambda i: (i, 0)), pl.BlockSpec((1, D_MODEL), lambda i: (0, 0))],
        out_specs=pl.BlockSpec((tm, D_MODEL), lambda i: (i, 0)),
        out_shape=jax.ShapeDtypeStruct((T_ROWS, D_MODEL), _MXU),
        compiler_params=_cparams("parallel"),
    )(h, w)


def _loss_head(h2, target, fw):
    def body(h_ref, t_ref, w_ref, loss_ref, dh_ref, dhb_ref, dw_ref, acc_ref):
        i = pl.program_id(0)

        @pl.when(i == 0)
        def _():
            acc_ref[...] = jnp.zeros_like(acc_ref)
            dw_ref[...] = jnp.zeros_like(dw_ref)

        h = h_ref[...]
        w = w_ref[...]
        y = _rms_fwd(h, w)
        live = (i > 0).astype(F32)
        err = (y - t_ref[...]) * live
        acc_ref[...] += jnp.sum(err * err, axis=0, keepdims=True)
        dy = err * (1.0 / D_MODEL)
        dx, dwr = _rms_bwd(h, w, dy)
        dh_ref[...] = dx
        dhb_ref[...] = dx.astype(_MXU)
        dw_ref[...] += jnp.sum(dwr, axis=0, keepdims=True)

        @pl.when(i == N_CHUNKS - 1)
        def _():
            tot = jnp.sum(acc_ref[...], axis=1, keepdims=True) * (0.5 / D_MODEL)
            loss_ref[...] = jnp.broadcast_to(tot, (1, 128))

    return pl.pallas_call(
        body, name="loss_head", grid=(N_CHUNKS,),
        in_specs=[pl.BlockSpec((CHUNK, D_MODEL), lambda i: (i, 0)),
                  pl.BlockSpec((CHUNK, D_MODEL), lambda i: (jnp.maximum(i - 1, 0), 0)),
                  pl.BlockSpec((1, D_MODEL), lambda i: (0, 0))],
        out_specs=[pl.BlockSpec((1, 128), lambda i: (0, 0)),
                   pl.BlockSpec((CHUNK, D_MODEL), lambda i: (i, 0)),
                   pl.BlockSpec((CHUNK, D_MODEL), lambda i: (i, 0)),
                   pl.BlockSpec((1, D_MODEL), lambda i: (0, 0))],
        out_shape=[jax.ShapeDtypeStruct((1, 128), F32),
                   jax.ShapeDtypeStruct((T_ROWS, D_MODEL), F32),
                   jax.ShapeDtypeStruct((T_ROWS, D_MODEL), _MXU),
                   jax.ShapeDtypeStruct((1, D_MODEL), F32)],
        scratch_shapes=[pltpu.VMEM((1, D_MODEL), F32)],
        compiler_params=_cparams("arbitrary"),
    )(h2, target, fw)


def _mm_norm_bwd(terms, h, w, dres, *, name, tm=272):
    n_terms = len(terms)
    in_specs, args = [], []
    for (a, b, k) in terms:
        in_specs += [pl.BlockSpec((tm, k), lambda i: (i, 0)), pl.BlockSpec((k, D_MODEL), lambda i: (0, 0))]
        args += [a, b]
    in_specs += [pl.BlockSpec((tm, D_MODEL), lambda i: (i, 0)), pl.BlockSpec((1, D_MODEL), lambda i: (0, 0)),
                 pl.BlockSpec((tm, D_MODEL), lambda i: (i, 0))]
    args += [h, w, dres]

    def body(*refs):
        h_ref, w_ref, dres_ref, dh_ref, dhb_ref, dw_ref = refs[2 * n_terms:]

        @pl.when(pl.program_id(0) == 0)
        def _():
            dw_ref[...] = jnp.zeros_like(dw_ref)

        du = None
        for t in range(n_terms):
            d = lax.dot_general(refs[2 * t][...], refs[2 * t + 1][...], NN, preferred_element_type=F32)
            du = d if du is None else du + d
        dx, dwr = _rms_bwd(h_ref[...], w_ref[...], du)
        dh = dres_ref[...] + dx
        dh_ref[...] = dh
        dhb_ref[...] = dh.astype(_MXU)
        dw_ref[...] += jnp.sum(dwr, axis=0, keepdims=True)

    return pl.pallas_call(
        body, name=name, grid=(T_ROWS // tm,), in_specs=in_specs,
        out_specs=[pl.BlockSpec((tm, D_MODEL), lambda i: (i, 0)), pl.BlockSpec((tm, D_MODEL), lambda i: (i, 0)),
                   pl.BlockSpec((1, D_MODEL), lambda i: (0, 0))],
        out_shape=[jax.ShapeDtypeStruct((T_ROWS, D_MODEL), F32), jax.ShapeDtypeStruct((T_ROWS, D_MODEL), _MXU),
                   jax.ShapeDtypeStruct((1, D_MODEL), F32)],
        compiler_params=_cparams("arbitrary"),
    )(*args)


FFN_TM = 272
FFN_TN = 1408


def _ffn_up(u2, wg_t, wu_t):
    def body(u_ref, wg_ref, wu_ref, gp_ref, up_ref, act_ref):
        u = u_ref[...]
        gp = lax.dot_general(u, wg_ref[...], NT, preferred_element_type=F32)
        up = lax.dot_general(u, wu_ref[...], NT, preferred_element_type=F32)
        gp_ref[...] = gp
        up_ref[...] = up
        act_ref[...] = (_silu(gp) * up).astype(_MXU)

    tile = pl.BlockSpec((FFN_TM, FFN_TN), lambda j, i: (i, j))
    return pl.pallas_call(
        body, name="ffn_up", grid=(D_FF // FFN_TN, T_ROWS // FFN_TM),
        in_specs=[pl.BlockSpec((FFN_TM, D_MODEL), lambda j, i: (i, 0)),
                  pl.BlockSpec((FFN_TN, D_MODEL), lambda j, i: (j, 0)),
                  pl.BlockSpec((FFN_TN, D_MODEL), lambda j, i: (j, 0))],
        out_specs=[tile, tile, tile],
        out_shape=[jax.ShapeDtypeStruct((T_ROWS, D_FF), F32), jax.ShapeDtypeStruct((T_ROWS, D_FF), F32),
                   jax.ShapeDtypeStruct((T_ROWS, D_FF), _MXU)],
        compiler_params=_cparams("parallel", "parallel"),
    )(u2, wg_t, wu_t)


def _ffn_bwd_act(dh2b, wd, gp, up):
    def body(dh_ref, wd_ref, gp_ref, up_ref, dgp_ref, dup_ref):
        dact = lax.dot_general(dh_ref[...], wd_ref[...], NT, preferred_element_type=F32)
        gp = gp_ref[...]
        dgp_ref[...] = (dact * up_ref[...] * _silu_grad(gp)).astype(_MXU)
        dup_ref[...] = (dact * _silu(gp)).astype(_MXU)

    tile = pl.BlockSpec((FFN_TM, FFN_TN), lambda j, i: (i, j))
    return pl.pallas_call(
        body, name="ffn_bwd_act", grid=(D_FF // FFN_TN, T_ROWS // FFN_TM),
        in_specs=[pl.BlockSpec((FFN_TM, D_MODEL), lambda j, i: (i, 0)),
                  pl.BlockSpec((FFN_TN, D_MODEL), lambda j, i: (j, 0)), tile, tile],
        out_specs=[tile, tile],
        out_shape=[jax.ShapeDtypeStruct((T_ROWS, D_FF), _MXU), jax.ShapeDtypeStruct((T_ROWS, D_FF), _MXU)],
        compiler_params=_cparams("parallel", "parallel"),
    )(dh2b, wd, gp, up)


CONV_TC = 512
CONV_K = 4


def _conv_pre(x_ref, wv, bv, c):
    tc = wv.shape[1]
    r0 = c * CHUNK
    cur = x_ref[r0:r0 + CHUNK, :]
    prev8 = jnp.zeros((8, tc), F32) if c == 0 else x_ref[r0 - 8:r0, :]
    cat = jnp.concatenate([prev8, cur], axis=0)
    shifted = [cur] + [pltpu.roll(cat, s, 0)[8:8 + CHUNK] for s in range(1, CONV_K)]
    pre = bv
    for s in range(CONV_K):
        pre = pre + shifted[s] * wv[CONV_K - 1 - s:CONV_K - s]
    return pre, shifted


def _row_mask(c):
    if c > 0:
        return None
    return (lax.broadcasted_iota(jnp.int32, (CHUNK, 1), 0) >= PAD_ROWS).astype(F32)


def _conv_fwd(x, w, b, *, silu, name):
    cols = x.shape[1]
    tc = min(CONV_TC, cols)

    def body(x_ref, w_ref, b_ref, o_ref):
        wv, bv = w_ref[...], b_ref[...]
        for c in range(N_CHUNKS):
            pre, _ = _conv_pre(x_ref, wv, bv, c)
            y = _silu(pre) if silu else pre
            mask = _row_mask(c)
            if mask is not None:
                y = y * mask
            o_ref[c * CHUNK:(c + 1) * CHUNK, :] = y

    return pl.pallas_call(
        body, name=name, grid=(cols // tc,),
        in_specs=[pl.BlockSpec((T_ROWS, tc), lambda j: (0, j)), pl.BlockSpec((CONV_K, tc), lambda j: (0, j)),
                  pl.BlockSpec((1, tc), lambda j: (0, j))],
        out_specs=pl.BlockSpec((T_ROWS, tc), lambda j: (0, j)),
        out_shape=jax.ShapeDtypeStruct((T_ROWS, cols), F32),
        compiler_params=_cparams("parallel"),
    )(x, w, b)


def _conv_bwd(dy, x, w, b, *, silu, name):
    cols = x.shape[1]
    tc = min(CONV_TC, cols)

    def body(dy_ref, x_ref, w_ref, b_ref, dx_ref, dw_ref, db_ref):
        wv, bv = w_ref[...], b_ref[...]
        next8 = jnp.zeros((8, tc), F32)
        dws = [jnp.zeros((1, tc), F32) for _ in range(CONV_K)]
        db = jnp.zeros((1, tc), F32)
        for c in reversed(range(N_CHUNKS)):
            pre, shifted = _conv_pre(x_ref, wv, bv, c)
            dpre = dy_ref[c * CHUNK:(c + 1) * CHUNK, :]
            if silu:
                dpre = dpre * _silu_grad(pre)
            mask = _row_mask(c)
            if mask is not None:
                dpre = dpre * mask
            cat = jnp.concatenate([dpre, next8], axis=0)
            dx = dpre * wv[CONV_K - 1:CONV_K]
            for s in range(1, CONV_K):
                dx = dx + pltpu.roll(cat, CHUNK + 8 - s, 0)[0:CHUNK] * wv[CONV_K - 1 - s:CONV_K - s]
            dx_ref[c * CHUNK:(c + 1) * CHUNK, :] = dx.astype(_MXU)
            for s in range(CONV_K):
                k = CONV_K - 1 - s
                dws[k] = dws[k] + jnp.sum(dpre * shifted[s], axis=0, keepdims=True)
            db = db + jnp.sum(dpre, axis=0, keepdims=True)
            next8 = dpre[0:8]
        dw_ref[...] = jnp.concatenate(dws, axis=0)
        db_ref[...] = db

    return pl.pallas_call(
        body, name=name, grid=(cols // tc,),
        in_specs=[pl.BlockSpec((T_ROWS, tc), lambda j: (0, j)), pl.BlockSpec((T_ROWS, tc), lambda j: (0, j)),
                  pl.BlockSpec((CONV_K, tc), lambda j: (0, j)), pl.BlockSpec((1, tc), lambda j: (0, j))],
        out_specs=[pl.BlockSpec((T_ROWS, tc), lambda j: (0, j)), pl.BlockSpec((CONV_K, tc), lambda j: (0, j)),
                   pl.BlockSpec((1, tc), lambda j: (0, j))],
        out_shape=[jax.ShapeDtypeStruct((T_ROWS, cols), _MXU), jax.ShapeDtypeStruct((CONV_K, cols), F32),
                   jax.ShapeDtypeStruct((1, cols), F32)],
        compiler_params=_cparams("parallel"),
    )(dy, x, w, b)


def _ssd_chunk_common(dt_raw, prm, c):
    a_row = -jnp.exp(prm[1:2])
    dt = _softplus(dt_raw + prm[0:1])
    rows = lax.broadcasted_iota(jnp.int32, (CHUNK, 1), 0)
    real = jnp.logical_or(c > 0, rows >= PAD_ROWS)
    dt = jnp.where(real, dt, 0.0)
    li = lax.broadcasted_iota(jnp.int32, (CHUNK, CHUNK), 0)
    si = lax.broadcasted_iota(jnp.int32, (CHUNK, CHUNK), 1)
    causal = li >= si
    tri = causal.astype(F32)
    cs = _dot_exact(tri, dt * a_row)
    return dt, a_row, cs, cs.T, causal, tri, real


def _gated_norm_fwd(y, z, w):
    g = y * _silu(z)
    half = SSD_WIDTH // SSD_GROUPS
    outs = [_rms_fwd(g[:, k * half:(k + 1) * half], w[:, k * half:(k + 1) * half]) for k in range(SSD_GROUPS)]
    return jnp.concatenate(outs, axis=1)


GROUP_W = SSD_WIDTH // SSD_GROUPS
PAIR_W = 2 * SSD_HEAD_DIM
STATE_SHAPE = (SSD_GROUPS, SSD_STATE, GROUP_W)


def _head_expander():
    r = lax.broadcasted_iota(jnp.int32, (128, SSD_WIDTH), 0)
    c = lax.broadcasted_iota(jnp.int32, (128, SSD_WIDTH), 1)
    return (c // SSD_HEAD_DIM == r).astype(F32)


def _ssd_expand(dt, cs, prm, ex):
    cs_x = _dot_exact(cs, ex)
    cs_last_x = cs_x[CHUNK - 1:CHUNK, :]
    return (_dot_exact(dt, ex), _dot_exact(prm, ex)[2:3], jnp.exp(cs_x), jnp.exp(cs_last_x),
            jnp.exp(cs_last_x - cs_x))


def _ssd_fwd(xs, bc, dt_raw, z, prm, norm_w, ex):
    def body(xs_ref, bc_ref, dt_ref, z_ref, prm_ref, nw_ref, ex_ref, y_ref, yn_ref, prev_ref, state):
        c = pl.program_id(0)

        @pl.when(c == 0)
        def _():
            state[...] = jnp.zeros_like(state)

        prm = prm_ref[...]
        dt, a_row, cs, cs_t, causal, _, _ = _ssd_chunk_common(dt_ref[...], prm, c)
        dt_x, d_x, e_cs_x, e_last_x, dec_x = _ssd_expand(dt, cs, prm, ex_ref[...])
        xs_all = xs_ref[...]
        bc_all = bc_ref[...]
        xdt = xs_all * dt_x
        xdec = xdt * dec_x
        lane_lo = lax.broadcasted_iota(jnp.int32, (1, PAIR_W), 1) < SSD_HEAD_DIM
        for g in range(SSD_GROUPS):
            gs = slice(g * GROUP_W, (g + 1) * GROUP_W)
            b_g = bc_all[:, g * SSD_STATE:(g + 1) * SSD_STATE]
            c_g = bc_all[:, (SSD_GROUPS + g) * SSD_STATE:(SSD_GROUPS + g + 1) * SSD_STATE]
            st = state[g]
            prev_ref[0, g] = st
            y_off = _dot(c_g, st) * e_cs_x[:, gs]
            state[g] = st * e_last_x[:, gs] + _dot(b_g.T, xdec[:, gs])
            cb = _dot(c_g, b_g, NT)
            for k in range(SSD_HPG // 2):
                h0 = g * SSD_HPG + 2 * k
                ps = slice(h0 * SSD_HEAD_DIM, h0 * SSD_HEAD_DIM + PAIR_W)
                xdt_pair = xdt[:, ps]
                yd = []
                for h in (h0, h0 + 1):
                    lmat = jnp.where(causal, jnp.exp(cs[:, h:h + 1] - cs_t[h:h + 1, :]), 0.0)
                    yd.append(_dot(cb * lmat, xdt_pair))
                y_ref[:, ps] = (jnp.where(lane_lo, yd[0], yd[1]) + y_off[:, k * PAIR_W:(k + 1) * PAIR_W]
                                + xs_all[:, ps] * d_x[:, ps])
        yn_ref[...] = _gated_norm_fwd(y_ref[...], z_ref[...], nw_ref[...]).astype(_MXU)

    row = lambda w: pl.BlockSpec((CHUNK, w), lambda c: (c, 0))
    return pl.pallas_call(
        body, name="ssd_fwd", grid=(N_CHUNKS,),
        in_specs=[row(SSD_WIDTH), row(512), row(128), row(SSD_WIDTH),
                  pl.BlockSpec((8, 128), lambda c: (0, 0)), pl.BlockSpec((1, SSD_WIDTH), lambda c: (0, 0)),
                  pl.BlockSpec((128, SSD_WIDTH), lambda c: (0, 0))],
        out_specs=[row(SSD_WIDTH), row(SSD_WIDTH),
                   pl.BlockSpec((1,) + STATE_SHAPE, lambda c: (c, 0, 0, 0))],
        out_shape=[jax.ShapeDtypeStruct((T_ROWS, SSD_WIDTH), F32), jax.ShapeDtypeStruct((T_ROWS, SSD_WIDTH), _MXU),
                   jax.ShapeDtypeStruct((N_CHUNKS,) + STATE_SHAPE, F32)],
        scratch_shapes=[pltpu.VMEM(STATE_SHAPE, F32)],
        compiler_params=_cparams("arbitrary"),
    )(xs, bc, dt_raw, z, prm, norm_w, ex)


def _ssd_bwd(dyn, dyn_block, z, y_pre, xs, bc, dt_raw, prev, prm, norm_w, ex):
    def body(dyn_ref, z_ref, y_ref, xs_ref, bc_ref, dt_ref, prev_ref, prm_ref, nw_ref, ex_ref,
             dz_ref, dxs_ref, dbc_ref, ddt_ref, dprm_ref, dnw_ref, dstate):
        step = pl.program_id(0)
        c = N_CHUNKS - 1 - step

        @pl.when(step == 0)
        def _():
            dstate[...] = jnp.zeros_like(dstate)
            dprm_ref[...] = jnp.zeros_like(dprm_ref)
            dnw_ref[...] = jnp.zeros_like(dnw_ref)

        prm = prm_ref[...]
        dt, a_row, cs, cs_t, causal, tri, real = _ssd_chunk_common(dt_ref[...], prm, c)
        realf = real.astype(F32)
        z = z_ref[...]
        y_all = y_ref[...]
        nw = nw_ref[...]
        dyn_all = dyn_ref[...]
        sz = _silu(z)
        gated = y_all * sz
        half = SSD_WIDTH // SSD_GROUPS
        dgs, dnws = [], []
        for k in range(SSD_GROUPS):
            sl = slice(k * half, (k + 1) * half)
            dgk, dwk = _rms_bwd(gated[:, sl], nw[:, sl], dyn_all[:, sl])
            dgs.append(dgk)
            dnws.append(jnp.sum(dwk, axis=0, keepdims=True))
        dgated = jnp.concatenate(dgs, axis=1)
        dnw_ref[...] += jnp.concatenate(dnws, axis=1)
        dz_ref[...] = (dgated * y_all * _silu_grad(z)).astype(_MXU)
        dy_all = dgated * sz

        ex = ex_ref[...]
        dt_x, d_x, e_cs_x, e_last_x, dec_x = _ssd_expand(dt, cs, prm, ex)
        xs_all = xs_ref[...]
        bc_all = bc_ref[...]
        xdt = xs_all * dt_x
        xdt_mxu = xdt.astype(_MXU).astype(F32)
        xdec = xdt * dec_x
        dcp = dy_all * e_cs_x
        lane_lo = lax.broadcasted_iota(jnp.int32, (1, PAIR_W), 1) < SSD_HEAD_DIM
        upper = (lax.broadcasted_iota(jnp.int32, (CHUNK, CHUNK), 0)
                 <= lax.broadcasted_iota(jnp.int32, (CHUNK, CHUNK), 1))
        last_row = (lax.broadcasted_iota(jnp.int32, (CHUNK, 1), 0) == CHUNK - 1).astype(F32)
        dbs, dcs_, dxdt_parts, last_parts = [], [], [], []
        for g in range(SSD_GROUPS):
            gs = slice(g * GROUP_W, (g + 1) * GROUP_W)
            b_g = bc_all[:, g * SSD_STATE:(g + 1) * SSD_STATE]
            c_g = bc_all[:, (SSD_GROUPS + g) * SSD_STATE:(SSD_GROUPS + g + 1) * SSD_STATE]
            prev_t = prev_ref[0, g]
            dst = dstate[g]
            dc_g = _dot(dcp[:, gs], prev_t, NT)
            db_g = _dot(xdec[:, gs], dst, NT)
            dxdt_state = _dot(b_g, dst) * dec_x[:, gs]
            dstate[g] = dst * e_last_x[:, gs] + _dot(c_g.T, dcp[:, gs])
            last_parts.append(jnp.sum(xdt_mxu[:, gs] * dxdt_state, axis=0, keepdims=True)
                              + jnp.sum(dst * prev_t, axis=0, keepdims=True) * e_last_x[:, gs])
            cb_t = _dot(b_g, c_g, NT)
            dcb_t = jnp.zeros((CHUNK, CHUNK), F32)
            for k in range(SSD_HPG // 2):
                h0 = g * SSD_HPG + 2 * k
                ps = slice(h0 * SSD_HEAD_DIM, h0 * SSD_HEAD_DIM + PAIR_W)
                dy_pair = dy_all[:, ps]
                xdt_pair = xdt[:, ps]
                dd = []
                for h in (h0, h0 + 1):
                    lmat_t = jnp.where(upper, jnp.exp(cs_t[h:h + 1, :] - cs[:, h:h + 1]), 0.0)
                    dd.append(_dot(cb_t * lmat_t, dy_pair))
                    mine = lane_lo if h == h0 else jnp.logical_not(lane_lo)
                    dcb_t = dcb_t + _dot(jnp.where(mine, xdt_pair, 0.0), dy_pair, NT) * lmat_t
                dxdt_parts.append(jnp.where(lane_lo, dd[0], dd[1]) + dxdt_state[:, k * PAIR_W:(k + 1) * PAIR_W])
            dc_g = dc_g + _dot(dcb_t, b_g, TN)
            db_g = db_g + _dot(dcb_t, c_g)
            dbs.append(db_g * realf)
            dcs_.append(dc_g * realf)
        dbc_ref[...] = jnp.concatenate(dbs + dcs_, axis=1)
        dxdt = jnp.concatenate(dxdt_parts, axis=1)
        dxs_ref[...] = (dxdt * dt_x + dy_all * d_x) * realf
        ddt_all = _dot_exact(dxdt * xs_all, ex, NT)
        rows = jnp.concatenate([jnp.concatenate(last_parts, axis=1), jnp.sum(dy_all * xs_all, axis=0, keepdims=True),
                                jnp.zeros((6, SSD_WIDTH), F32)], axis=0)
        rows = _dot_exact(rows, ex, NT)
        dd_row = rows[1:2]
        dy_mxu = dy_all.astype(_MXU).astype(F32)
        dcs_all = (_dot_exact(dy_mxu * (y_all - xs_all * d_x), ex, NT) - _dot_exact(xdt_mxu * dxdt, ex, NT)
                   + last_row * rows[0:1])
        dda = _dot_exact(tri, dcs_all, TN)
        ddt = (ddt_all + dda * a_row) * realf
        ddt_raw = ddt * _sigmoid(dt_ref[...] + prm[0:1])
        ddt_ref[...] = ddt_raw.astype(_MXU)
        da_log = jnp.sum(dda * dt, axis=0, keepdims=True) * a_row
        dprm_ref[0:1, :] += jnp.sum(ddt_raw, axis=0, keepdims=True)
        dprm_ref[1:2, :] += da_log
        dprm_ref[2:3, :] += dd_row

    rev = lambda w, blk=0: pl.BlockSpec((CHUNK, w), lambda s, blk=blk: (N_CHUNKS - 1 - s, blk))
    return pl.pallas_call(
        body, name="ssd_bwd", grid=(N_CHUNKS,),
        in_specs=[rev(SSD_WIDTH, dyn_block), rev(SSD_WIDTH), rev(SSD_WIDTH), rev(SSD_WIDTH), rev(512), rev(128),
                  pl.BlockSpec((1,) + STATE_SHAPE, lambda s: (N_CHUNKS - 1 - s, 0, 0, 0)),
                  pl.BlockSpec((8, 128), lambda s: (0, 0)), pl.BlockSpec((1, SSD_WIDTH), lambda s: (0, 0)),
                  pl.BlockSpec((128, SSD_WIDTH), lambda s: (0, 0))],
        out_specs=[rev(SSD_WIDTH), rev(SSD_WIDTH), rev(512), rev(128),
                   pl.BlockSpec((8, 128), lambda s: (0, 0)), pl.BlockSpec((1, SSD_WIDTH), lambda s: (0, 0))],
        out_shape=[jax.ShapeDtypeStruct((T_ROWS, SSD_WIDTH), _MXU), jax.ShapeDtypeStruct((T_ROWS, SSD_WIDTH), F32),
                   jax.ShapeDtypeStruct((T_ROWS, 512), F32), jax.ShapeDtypeStruct((T_ROWS, 128), _MXU),
                   jax.ShapeDtypeStruct((8, 128), F32), jax.ShapeDtypeStruct((1, SSD_WIDTH), F32)],
        scratch_shapes=[pltpu.VMEM(STATE_SHAPE, F32)],
        compiler_params=_cparams("arbitrary"),
    )(dyn, z, y_pre, xs, bc, dt_raw, prev, prm, norm_w, ex)


LRU_PAIRS = 8


def _lru_gates(xr, wa_ref, wx_ref, prm):
    pre_r, pre_i = [], []
    for k in range(LRU_PAIRS):
        xk = xr[:, k * 128:(k + 1) * 128]
        pre_r.append(_dot(xk, wa_ref[k]))
        pre_i.append(_dot(xk, wx_ref[k]))
    r = _sigmoid(jnp.concatenate(pre_r, axis=1) + prm[0:1])
    i = _sigmoid(jnp.concatenate(pre_i, axis=1) + prm[1:2])
    sp = _softplus(-prm[2:3])
    log_a = (-LRU_C) * r * sp
    a = jnp.exp(log_a)
    s = jnp.sqrt(-jnp.tanh(log_a) * (a * a + 1.0))
    return r, i, a, s, sp


def _lru_fwd(xr, gate, wa, wx, prm):
    def body(xr_ref, g_ref, wa_ref, wx_ref, prm_ref, hs_ref, yn_ref, carry, a_s, u_s):
        @pl.when(pl.program_id(0) == 0)
        def _():
            carry[...] = jnp.zeros_like(carry)

        prm = prm_ref[...]
        xr_t = xr_ref[...]
        _, i, a, s, _ = _lru_gates(xr_t, wa_ref, wx_ref, prm)
        a_s[...] = a
        u_s[...] = s * (i * xr_t)
        rid = lax.broadcasted_iota(jnp.int32, (8, LRU_WIDTH), 0)

        def group(k, h):
            off = pl.multiple_of(k * 8, 8)
            a8 = a_s[pl.ds(off, 8), :]
            u8 = u_s[pl.ds(off, 8), :]
            out = jnp.zeros((8, LRU_WIDTH), F32)
            for r_ in range(8):
                h = a8[r_:r_ + 1] * h + u8[r_:r_ + 1]
                out = jnp.where(rid == r_, h, out)
            hs_ref[pl.ds(off, 8), :] = out
            return h

        carry[0:1, :] = lax.fori_loop(0, CHUNK // 8, group, carry[0:1, :])
        gel, _ = _gelu_and_grad(g_ref[...])
        yn_ref[...] = _rms_fwd(gel * hs_ref[...], prm[3:4]).astype(_MXU)

    row = pl.BlockSpec((CHUNK, LRU_WIDTH), lambda t: (t, 0))
    wspec = pl.BlockSpec((LRU_PAIRS, 128, 128), lambda t: (0, 0, 0))
    return pl.pallas_call(
        body, name="lru_fwd", grid=(N_CHUNKS,),
        in_specs=[row, row, wspec, wspec, pl.BlockSpec((8, LRU_WIDTH), lambda t: (0, 0))],
        out_specs=[row, row],
        out_shape=[jax.ShapeDtypeStruct((T_ROWS, LRU_WIDTH), F32), jax.ShapeDtypeStruct((T_ROWS, LRU_WIDTH), _MXU)],
        scratch_shapes=[pltpu.VMEM((8, LRU_WIDTH), F32), pltpu.VMEM((CHUNK, LRU_WIDTH), F32),
                        pltpu.VMEM((CHUNK, LRU_WIDTH), F32)],
        compiler_params=_cparams("arbitrary"),
    )(xr, gate, wa, wx, prm)


def _lru_bwd(dyn, dyn_block, gate, xr, hs, wa, wx, wa_t, wx_t, prm):
    def body(dyn_ref, g_ref, xr_ref, hs_ref, hsp_ref, wa_ref, wx_ref, wat_ref, wxt_ref, prm_ref,
             dg_ref, dxr_ref, dwa_ref, dwx_ref, dprm_ref, carry, a_s, d_s):
        step = pl.program_id(0)
        tile = N_CHUNKS - 1 - step

        @pl.when(step == 0)
        def _():
            carry[...] = jnp.zeros_like(carry)
            dwa_ref[...] = jnp.zeros_like(dwa_ref)
            dwx_ref[...] = jnp.zeros_like(dwx_ref)
            dprm_ref[...] = jnp.zeros_like(dprm_ref)

        prm = prm_ref[...]
        xr_t = xr_ref[...]
        r, i, a, s, sp = _lru_gates(xr_t, wa_ref, wx_ref, prm)
        hs_t = hs_ref[...]
        gel, dgel = _gelu_and_grad(g_ref[...])
        dy, dnw = _rms_bwd(gel * hs_t, prm[3:4], dyn_ref[...])
        dg_ref[...] = (dy * hs_t * dgel).astype(_MXU)
        a_s[...] = a
        d_s[...] = dy * gel
        rid = lax.broadcasted_iota(jnp.int32, (8, LRU_WIDTH), 0)

        def group(k, cr):
            off = pl.multiple_of((CHUNK // 8 - 1 - k) * 8, 8)
            a8 = a_s[pl.ds(off, 8), :]
            d8 = d_s[pl.ds(off, 8), :]
            out = jnp.zeros((8, LRU_WIDTH), F32)
            for r_ in reversed(range(8)):
                dht = d8[r_:r_ + 1] + cr
                out = jnp.where(rid == r_, dht, out)
                cr = a8[r_:r_ + 1] * dht
            d_s[pl.ds(off, 8), :] = out
            return cr

        carry[0:1, :] = lax.fori_loop(0, CHUNK // 8, group, carry[0:1, :])
        dht = d_s[...]
        before = hsp_ref[CHUNK - 8:CHUNK, :][7:8] * (tile > 0).astype(F32)
        first = lax.broadcasted_iota(jnp.int32, (CHUNK, 1), 0) == 0
        hprev = jnp.where(first, before, pltpu.roll(hs_t, 1, 0))
        da = dht * hprev
        ixr = i * xr_t
        ds = dht * ixr
        dlog_a = da * a - ds * (a * a) / s
        dr = dlog_a * ((-LRU_C) * sp)
        dsp = jnp.sum(dlog_a * ((-LRU_C) * r), axis=0, keepdims=True)
        dlam = dsp * (-_sigmoid(-prm[2:3]))
        di = dht * s * xr_t
        dpre_r = dr * r * (1.0 - r)
        dpre_i = di * i * (1.0 - i)
        dxr = dht * s * i
        parts = []
        for k in range(LRU_PAIRS):
            sl = slice(k * 128, (k + 1) * 128)
            parts.append(_dot(dpre_r[:, sl], wat_ref[k]) + _dot(dpre_i[:, sl], wxt_ref[k]))
            dwa_ref[k] += _dot(xr_t[:, sl], dpre_r[:, sl], TN)
            dwx_ref[k] += _dot(xr_t[:, sl], dpre_i[:, sl], TN)
        dxr_ref[...] = dxr + jnp.concatenate(parts, axis=1)
        dprm_ref[0:1, :] += jnp.sum(dpre_r, axis=0, keepdims=True)
        dprm_ref[1:2, :] += jnp.sum(dpre_i, axis=0, keepdims=True)
        dprm_ref[2:3, :] += dlam
        dprm_ref[3:4, :] += jnp.sum(dnw, axis=0, keepdims=True)

    rev = lambda blk=0: pl.BlockSpec((CHUNK, LRU_WIDTH), lambda s, blk=blk: (N_CHUNKS - 1 - s, blk))
    wspec = pl.BlockSpec((LRU_PAIRS, 128, 128), lambda s: (0, 0, 0))
    return pl.pallas_call(
        body, name="lru_bwd", grid=(N_CHUNKS,),
        in_specs=[rev(dyn_block), rev(), rev(), rev(),
                  pl.BlockSpec((CHUNK, LRU_WIDTH), lambda s: (jnp.maximum(N_CHUNKS - 2 - s, 0), 0)),
                  wspec, wspec, wspec, wspec, pl.BlockSpec((8, LRU_WIDTH), lambda s: (0, 0))],
        out_specs=[rev(), rev(), wspec, wspec, pl.BlockSpec((8, LRU_WIDTH), lambda s: (0, 0))],
        out_shape=[jax.ShapeDtypeStruct((T_ROWS, LRU_WIDTH), _MXU), jax.ShapeDtypeStruct((T_ROWS, LRU_WIDTH), F32),
                   jax.ShapeDtypeStruct((LRU_PAIRS, 128, 128), F32), jax.ShapeDtypeStruct((LRU_PAIRS, 128, 128), F32),
                   jax.ShapeDtypeStruct((8, LRU_WIDTH), F32)],
        scratch_shapes=[pltpu.VMEM((8, LRU_WIDTH), F32), pltpu.VMEM((CHUNK, LRU_WIDTH), F32),
                        pltpu.VMEM((CHUNK, LRU_WIDTH), F32)],
        compiler_params=_cparams("arbitrary"),
    )(dyn, gate, xr, hs, hs, wa, wx, wa_t, wx_t, prm)


SEC_NAMES = ("z", "xs", "bc", "dt", "g", "x")
SEC_WIDTH = {"z": 1024, "xs": 1024, "bc": 512, "dt": 128, "g": 1024, "x": 1024}


def _pair_blocks(w):
    w = w.reshape(LRU_PAIRS, 2, 64, 64)
    zero = jnp.zeros((LRU_PAIRS, 64, 64), w.dtype)
    top = jnp.concatenate([w[:, 0], zero], axis=2)
    bot = jnp.concatenate([zero, w[:, 1]], axis=2)
    return jnp.concatenate([top, bot], axis=1)


def _unpair_blocks(wp):
    return jnp.stack([wp[:, :64, :64], wp[:, 64:, 64:]], axis=1).reshape(16, 64, 64)


def _pad_lanes(v, width=128):
    return jnp.pad(v, ((0, 0), (0, width - v.shape[1])))


class _Resident:
    def __init__(self, w_out, w_gate, w_up, w_down):
        self._w_out, self._ffn = w_out, (w_gate, w_up, w_down)

    def w_out(self, after):
        return self._w_out

    def ffn(self, after):
        return self._ffn

    def grads_ready(self, names, g, g_mxu):
        return jnp.zeros((1, 1), F32)

    def small_ready(self, g, loss):
        return jnp.zeros((1, 1), F32)

    def small_middle(self, after):
        return jnp.zeros((1, 1), F32)


def _local_step(x, target, meta, p, late):
    g, g_mxu = {}, {}
    ex = _head_expander()
    h0 = _embed(x, meta)
    u1 = _rmsnorm(h0, p["norm1_w"], name="norm1")
    proj = {}
    for s in SEC_NAMES:
        wdt = SEC_WIDTH[s]
        proj[s] = _mm([(u1, 0, p["w_in_" + s], 0, D_MODEL)], T_ROWS, wdt, tm=544, tn=min(wdt, 512), mode="nt",
                      out_dtype=F32, name="proj_" + s)
    ssd_prm = jnp.concatenate([_pad_lanes(p["ssd_dt_bias"]), _pad_lanes(p["ssd_a_log"]), _pad_lanes(p["ssd_d"]),
                               jnp.zeros((5, 128), F32)], axis=0)
    xs_act = _conv_fwd(proj["xs"], p["ssd_conv_w"][:, :SSD_WIDTH], p["ssd_conv_b"][:, :SSD_WIDTH], silu=True,
                       name="ssd_conv_xs")
    bc_act = _conv_fwd(proj["bc"], p["ssd_conv_w"][:, SSD_WIDTH:], p["ssd_conv_b"][:, SSD_WIDTH:], silu=True,
                       name="ssd_conv_bc")
    y_pre, y_ssd, prev = _ssd_fwd(xs_act, bc_act, proj["dt"], proj["z"], ssd_prm, p["ssd_norm_w"], ex)
    xr = _conv_fwd(proj["x"], p["lru_conv_w"], p["lru_conv_b"], silu=False, name="lru_conv")
    wa_p, wx_p = _pair_blocks(p["lru_wa"]), _pair_blocks(p["lru_wx"])
    lru_prm = jnp.concatenate([p["lru_ba"], p["lru_bx"], p["lru_lambda"], p["lru_norm_w"],
                               jnp.zeros((4, LRU_WIDTH), F32)], axis=0)
    hs, y_lru = _lru_fwd(xr, proj["g"], wa_p.astype(_MXU), wx_p.astype(_MXU), lru_prm)
    ycat = jnp.concatenate([y_ssd, y_lru], axis=1)
    w_out = late.w_out(ycat)
    h1 = _mm([(ycat, 0, w_out, 0, 2 * D_MODEL)], T_ROWS, D_MODEL, tm=544, tn=512, mode="nn", out_dtype=F32,
             name="out_proj", residual=h0)
    u2 = _rmsnorm(h1, p["norm2_w"], name="norm2")
    w_gate, w_up, w_down = late.ffn(u2)
    gp, up, act = _ffn_up(u2, w_gate, w_up)
    h2 = _mm([(act, 0, w_down, 0, D_FF)], T_ROWS, D_MODEL, tm=544, tn=512, mode="nn", out_dtype=F32,
             name="ffn_down", residual=h1)
    loss, dh2, dh2b, g["final_norm_w"] = _loss_head(h2, target, p["final_norm_w"])
    dgp, dup = _ffn_bwd_act(dh2b, w_down, gp, up)
    g["w_down"], g_mxu["w_down"] = _mm([(act, 0, dh2b, 0, T_ROWS)], D_FF, D_MODEL, tm=1408, tn=512, mode="tn",
                                       out_dtype=F32, name="dw_down", also_mxu=True)
    dh1, dh1b, g["norm2_w"] = _mm_norm_bwd([(dgp, w_gate, D_FF), (dup, w_up, D_FF)], h1, p["norm2_w"], dh2,
                                           name="ffn_bwd_in")
    g["w_gate"], g_mxu["w_gate"] = _mm([(dgp, 0, u2, 0, T_ROWS)], D_FF, D_MODEL, tm=1408, tn=512, mode="tn",
                                       out_dtype=F32, name="dw_gate", also_mxu=True)
    g["w_up"], g_mxu["w_up"] = _mm([(dup, 0, u2, 0, T_ROWS)], D_FF, D_MODEL, tm=1408, tn=512, mode="tn",
                                   out_dtype=F32, name="dw_up", also_mxu=True)
    sent = late.grads_ready(("w_down", "w_gate", "w_up"), g, g_mxu)
    g["w_out"], g_mxu["w_out"] = _mm([(ycat, 0, dh1b, 0, T_ROWS)], 2 * D_MODEL, D_MODEL, tm=512, tn=512, mode="tn",
                                     out_dtype=F32, name="dw_out", also_mxu=True, behind=(sent,))
    sent = late.grads_ready(("w_out",), g, g_mxu)
    dycat = _mm([(dh1b, 0, w_out, 0, D_MODEL)], T_ROWS, 2 * D_MODEL, tm=544, tn=512, mode="nt", out_dtype=F32,
                name="out_proj_bwd", behind=(sent,))
    dgate, dxr, dwa_p, dwx_p, dlru_prm = _lru_bwd(dycat, 1, proj["g"], xr, hs, wa_p.astype(_MXU), wx_p.astype(_MXU),
                                                  jnp.swapaxes(wa_p, 1, 2).astype(_MXU),
                                                  jnp.swapaxes(wx_p, 1, 2).astype(_MXU), lru_prm)
    g["lru_wa"], g["lru_wx"] = _unpair_blocks(dwa_p), _unpair_blocks(dwx_p)
    g["lru_ba"], g["lru_bx"], g["lru_lambda"], g["lru_norm_w"] = (dlru_prm[k:k + 1] for k in range(4))
    dx_lru, g["lru_conv_w"], g["lru_conv_b"] = _conv_bwd(dxr, proj["x"], p["lru_conv_w"], p["lru_conv_b"], silu=False,
                                                         name="lru_conv_bwd")
    dz, dxs_act, dbc_act, ddt, dssd_prm, g["ssd_norm_w"] = _ssd_bwd(dycat, 0, proj["z"], y_pre, xs_act, bc_act,
                                                                    proj["dt"], prev, ssd_prm, p["ssd_norm_w"], ex)
    g["ssd_dt_bias"], g["ssd_a_log"], g["ssd_d"] = (dssd_prm[k:k + 1, :SSD_HEADS] for k in range(3))
    dxs, dcw_xs, dcb_xs = _conv_bwd(dxs_act, proj["xs"], p["ssd_conv_w"][:, :SSD_WIDTH],
                                    p["ssd_conv_b"][:, :SSD_WIDTH], silu=True, name="ssd_conv_xs_bwd")
    dbc, dcw_bc, dcb_bc = _conv_bwd(dbc_act, proj["bc"], p["ssd_conv_w"][:, SSD_WIDTH:],
                                    p["ssd_conv_b"][:, SSD_WIDTH:], silu=True, name="ssd_conv_bc_bwd")
    g["ssd_conv_w"] = jnp.concatenate([dcw_xs, dcw_bc], axis=1)
    g["ssd_conv_b"] = jnp.concatenate([dcb_xs, dcb_bc], axis=1)
    dproj = {"z": dz, "xs": dxs, "bc": dbc, "dt": ddt, "g": dgate, "x": dx_lru}
    dh0, _, g["norm1_w"] = _mm_norm_bwd([(dproj[s], p["w_in_" + s], SEC_WIDTH[s]) for s in SEC_NAMES], h0,
                                        p["norm1_w"], dh1, name="in_proj_bwd")
    g["meta_tokens"] = dh0[PAD_ROWS:X_ROW0]
    sent = late.small_ready(g, loss)
    for s in SEC_NAMES:
        wdt = SEC_WIDTH[s]
        g["w_in_" + s], g_mxu["w_in_" + s] = _mm([(dproj[s], 0, u1, 0, T_ROWS)], wdt, D_MODEL, tm=min(wdt, 512),
                                                 tn=512, mode="tn", out_dtype=F32, name="dw_in_" + s, also_mxu=True,
                                                 behind=(sent,))
        if s == "bc":
            sent = late.small_middle(g["w_in_bc"])
    return loss, dh0[X_ROW0:], g, g_mxu


MESH = pl.DeviceIdType.MESH
ANY = pl.BlockSpec(memory_space=pl.ANY)


def _my_place():
    return lax.axis_index("x"), lax.axis_index("y"), lax.axis_index("c")


def _other_chips(x, y):
    return [(1 - x, y), (x, 1 - y), (1 - x, 1 - y)]


def _gather_first(big, small):
    half = big.shape[1] // 2

    def body(big_ref, small_ref, big4, small4, send_sems, recv_sems, local_sems):
        x, y, c = _my_place()
        me = 2 * x + y
        sibling = (x, y, 1 - c)
        peers = _other_chips(x, y)
        mine = pl.ds(pl.multiple_of(c * half, 128), half)
        theirs = pl.ds(pl.multiple_of((1 - c) * half, 128), half)

        def copy(k, src, dst, dev):
            return pltpu.make_async_remote_copy(src_ref=src, dst_ref=dst, send_sem=send_sems.at[k],
                                                recv_sem=recv_sems.at[k], device_id=dev, device_id_type=MESH)

        local = [pltpu.make_async_copy(big_ref, big4.at[me], local_sems.at[0]),
                 pltpu.make_async_copy(small_ref, small4.at[me], local_sems.at[1])]
        for cp in local:
            cp.start()
        first = []
        for j, (px, py) in enumerate(peers):
            first.append(copy(j, big_ref.at[:, mine], big4.at[me, :, mine], (px, py, c)))
            first.append(copy(3 + j, small_ref, small4.at[me], (px, py, c)))
        for cp in first:
            cp.start()
        passed = []
        for j, (px, py) in enumerate(peers):
            slot = 2 * px + py
            copy(j, big_ref.at[:, mine], big4.at[slot, :, mine], (px, py, c)).wait_recv()
            passed.append(copy(6 + j, big4.at[slot, :, mine], big4.at[slot, :, mine], sibling))
            passed[-1].start()
        for j, (px, py) in enumerate(peers):
            slot = 2 * px + py
            copy(6 + j, big4.at[slot, :, theirs], big4.at[slot, :, theirs], sibling).wait_recv()
            copy(3 + j, small_ref, small4.at[slot], (px, py, c)).wait_recv()
        for cp in first + passed:
            cp.wait_send()
        for cp in local:
            cp.wait()

    return pl.pallas_call(
        body, name="gather_first", in_specs=[ANY, ANY], out_specs=[ANY, ANY],
        out_shape=[jax.ShapeDtypeStruct((N_SHARDS,) + big.shape, big.dtype),
                   jax.ShapeDtypeStruct((N_SHARDS,) + small.shape, small.dtype)],
        scratch_shapes=[pltpu.SemaphoreType.DMA((9,)), pltpu.SemaphoreType.DMA((9,)), pltpu.SemaphoreType.DMA((2,))],
    )(big, small)


def _scatter_grads(grads4):
    n = len(grads4)

    def body(*refs):
        ins, outs = refs[:n], refs[n:2 * n]
        send_sems, recv_sems = refs[2 * n:]
        x, y, c = _my_place()
        peers = _other_chips(x, y)
        for k in range(n):
            for j, (px, py) in enumerate(peers):
                pltpu.make_async_remote_copy(
                    src_ref=ins[k].at[2 * px + py], dst_ref=outs[k].at[j], send_sem=send_sems.at[3 * k + j],
                    recv_sem=recv_sems.at[3 * k + j], device_id=(px, py, c), device_id_type=MESH).start()
        for k in range(n):
            for j, (px, py) in enumerate(peers):
                pltpu.make_async_remote_copy(
                    src_ref=ins[k].at[2 * px + py], dst_ref=outs[k].at[j], send_sem=send_sems.at[3 * k + j],
                    recv_sem=recv_sems.at[3 * k + j], device_id=(px, py, c), device_id_type=MESH).wait()

    return pl.pallas_call(
        body, name="scatter_grads", in_specs=[ANY] * n, out_specs=[ANY] * n,
        out_shape=[jax.ShapeDtypeStruct((3,) + g.shape[1:], g.dtype) for g in grads4],
        scratch_shapes=[pltpu.SemaphoreType.DMA((3 * n,)), pltpu.SemaphoreType.DMA((3 * n,))],
    )(*grads4)


HBM_SPEC = pl.BlockSpec(memory_space=pltpu.HBM)
SEM_SPEC = pl.BlockSpec(memory_space=pltpu.SEMAPHORE)
SPLIT_EFFECT = pltpu.SideEffectType.DATAFLOW_SIDE_EFFECTING


def _gather_plan(bufs, x, y, c, incoming):
    plan = []
    for buf in bufs:
        for (px, py) in _other_chips(x, y):
            slot = 2 * px + py if incoming else 2 * x + y
            plan.append((buf.at[2 * x + y], buf.at[slot], (px, py, c)))
    return plan


def _scatter_plan(bufs, x, y, c, incoming):
    n = len(bufs) // 2
    plan = []
    for k in range(n):
        for j, (px, py) in enumerate(_other_chips(x, y)):
            plan.append((bufs[k].at[2 * px + py], bufs[n + k].at[j], (px, py, c)))
    return plan


def _split_start(bufs, plan, n_copies, after, *, name):
    n = len(bufs)
    extra = [] if after is None else [after]

    def body(*refs):
        ins = refs[:n]
        send_sems, recv_sems = refs[n + len(extra)], refs[n + len(extra) + 1]
        token = refs[-1]
        x, y, c = _my_place()
        for i, (src, dst, dev) in enumerate(plan(ins, x, y, c, False)):
            pltpu.make_async_remote_copy(src_ref=src, dst_ref=dst, send_sem=send_sems.at[i], recv_sem=recv_sems.at[i],
                                         device_id=dev, device_id_type=MESH).start()
        token[...] = jnp.zeros_like(token)

    outs = pl.pallas_call(
        body, name=name,
        out_shape=(pltpu.SemaphoreType.DMA((n_copies,)), pltpu.SemaphoreType.DMA((n_copies,)),
                   *[pltpu.HBM(b.shape, b.dtype) for b in bufs], jax.ShapeDtypeStruct((8, 128), F32)),
        in_specs=[HBM_SPEC] * n + [ANY] * len(extra),
        out_specs=(SEM_SPEC, SEM_SPEC, *[HBM_SPEC] * n, pl.BlockSpec(memory_space=pltpu.VMEM)),
        input_output_aliases={k: 2 + k for k in range(n)},
        compiler_params=pltpu.CompilerParams(has_side_effects=SPLIT_EFFECT),
    )(*[pltpu.with_memory_space_constraint(b, pltpu.HBM) for b in bufs], *extra)
    return outs[0], outs[1], list(outs[2:2 + n]), outs[-1]


def _split_wait(bufs, send_sems, recv_sems, plan, after, *, name):
    n = len(bufs)

    def body(*refs):
        ins = refs[:n]
        send_sems_ref, recv_sems_ref = refs[n], refs[n + 1]
        x, y, c = _my_place()
        for i, (src, dst, dev) in enumerate(plan(ins, x, y, c, True)):
            cp = pltpu.make_async_remote_copy(src_ref=src, dst_ref=dst, send_sem=send_sems_ref.at[i],
                                              recv_sem=recv_sems_ref.at[i], device_id=dev, device_id_type=MESH)
            cp.wait_send()
            cp.wait_recv()

    outs = pl.pallas_call(
        body, name=name, out_shape=tuple(pltpu.HBM(b.shape, b.dtype) for b in bufs),
        in_specs=[HBM_SPEC] * n + [SEM_SPEC, SEM_SPEC, ANY], out_specs=tuple([HBM_SPEC] * n),
        input_output_aliases={k: k for k in range(n)},
        compiler_params=pltpu.CompilerParams(has_side_effects=SPLIT_EFFECT),
    )(*bufs, send_sems, recv_sems, after)
    return list(outs)


def _fill_own_slot(shard, me_arr, *, name):
    r, c = shard.shape
    tile, steps, imap = _elementwise_tile(r, c)

    def body(me_ref, x_ref, o_ref):
        o_ref[0] = x_ref[...].astype(_MXU)

    return pl.pallas_call(
        body, name=name,
        grid_spec=pltpu.PrefetchScalarGridSpec(
            num_scalar_prefetch=1, grid=(steps,),
            in_specs=[pl.BlockSpec(tile, lambda i, me: imap(i))],
            out_specs=pl.BlockSpec((1,) + tile, lambda i, me: (me[0],) + imap(i))),
        out_shape=jax.ShapeDtypeStruct((N_SHARDS, r, c), _MXU),
        compiler_params=_cparams("parallel"),
    )(me_arr, shard)


def _swap_with_sibling(parts):
    n = len(parts)

    def body(*refs):
        ins, outs = refs[:n], refs[n:2 * n]
        send_sems, recv_sems = refs[2 * n:]
        x, y, c = _my_place()
        copies = [pltpu.make_async_remote_copy(
            src_ref=ins[k], dst_ref=outs[k], send_sem=send_sems.at[k], recv_sem=recv_sems.at[k],
            device_id=(x, y, 1 - c), device_id_type=MESH) for k in range(n)]
        for cp in copies:
            cp.start()
        for cp in copies:
            cp.wait()

    return pl.pallas_call(
        body, name="swap_with_sibling", in_specs=[ANY] * n, out_specs=[ANY] * n,
        out_shape=[jax.ShapeDtypeStruct(a.shape, a.dtype) for a in parts],
        scratch_shapes=[pltpu.SemaphoreType.DMA((n,)), pltpu.SemaphoreType.DMA((n,))],
    )(*parts)


def _other_devices(x, y, c):
    out = []
    for mask in range(1, N_DEV):
        px, py, pc = x ^ (mask >> 2 & 1), y ^ (mask >> 1 & 1), c ^ (mask & 1)
        out.append(((px, py, pc), 4 * px + 2 * py + pc))
    return out


def _pieces_plan(bufs, x, y, c, incoming):
    pack, land = bufs
    me = 4 * x + 2 * y + c
    return [(pack.at[num], land.at[num if incoming else me], dev) for dev, num in _other_devices(x, y, c)]


def _spread_plan(bufs, x, y, c, incoming):
    piece, land = bufs
    me = 4 * x + 2 * y + c
    return [(piece, land.at[num if incoming else me], dev) for dev, num in _other_devices(x, y, c)]


def _sum_pieces(pack, land, dev_arr, *, name):
    def body(dev_ref, pack_ref, land_ref, o_ref):
        dev = dev_ref[0]
        own = pack_ref[dev]
        acc = None
        for d in range(N_DEV):
            term = jnp.where(dev == d, own, land_ref[d])
            acc = term if acc is None else acc + term
        o_ref[...] = acc

    vmem = pl.BlockSpec(memory_space=pltpu.VMEM)
    return pl.pallas_call(
        body, name=name, in_specs=[pl.BlockSpec(memory_space=pltpu.SMEM), vmem, vmem], out_specs=vmem,
        out_shape=jax.ShapeDtypeStruct(pack.shape[1:], F32),
    )(dev_arr, pack, land)


def _join_pieces(piece, land, dev_arr, *, name):
    def body(dev_ref, piece_ref, land_ref, o_ref):
        dev = dev_ref[0]
        for d in range(N_DEV):
            o_ref[d] = jnp.where(dev == d, piece_ref[...], land_ref[d])

    vmem = pl.BlockSpec(memory_space=pltpu.VMEM)
    return pl.pallas_call(
        body, name=name, in_specs=[pl.BlockSpec(memory_space=pltpu.SMEM), vmem, vmem], out_specs=vmem,
        out_shape=jax.ShapeDtypeStruct(land.shape, F32),
    )(dev_arr, piece, land)


def _adamw_native(ws, gs, ms, vs):
    n = len(ws)

    def body(*refs):
        for k in range(n):
            w_ref, g_ref, m_ref, v_ref = (refs[j * n + k] for j in range(4))
            delta, m_new, v_new = _adamw_math(w_ref[...], g_ref[...], m_ref[...], v_ref[...])
            refs[4 * n + k][...] = delta
            refs[5 * n + k][...] = m_new
            refs[6 * n + k][...] = v_new

    vmem = pl.BlockSpec(memory_space=pltpu.VMEM)
    shapes = [jax.ShapeDtypeStruct(a.shape, F32) for a in ws]
    outs = pl.pallas_call(
        body, name="adamw_small", in_specs=[vmem] * (4 * n), out_specs=[vmem] * (3 * n), out_shape=shapes * 3,
        compiler_params=pltpu.CompilerParams(vmem_limit_bytes=VMEM_LIMIT_BYTES),
    )(*ws, *gs, *ms, *vs)
    return outs[:n], outs[n:2 * n], outs[2 * n:]


def _elementwise_tile(rows, cols, limit=256):
    for t in range(limit, 15, -16):
        if rows % t == 0:
            return (t, cols), rows // t, lambda i: (i, 0)
    assert cols % limit == 0
    return (rows, limit), cols // limit, lambda i: (0, i)


def _partial_sum(g4, land, me_arr, *, name):
    _, r, c = g4.shape
    tile, steps, imap = _elementwise_tile(r, c)

    def body(me_ref, own_ref, land_ref, o_ref):
        acc = own_ref[0]
        for j in range(3):
            acc = acc + land_ref[j].astype(F32)
        o_ref[...] = acc

    return pl.pallas_call(
        body, name=name,
        grid_spec=pltpu.PrefetchScalarGridSpec(
            num_scalar_prefetch=1, grid=(steps,),
            in_specs=[pl.BlockSpec((1,) + tile, lambda i, me: (me[0],) + imap(i)),
                      pl.BlockSpec((3,) + tile, lambda i, me: (0,) + imap(i))],
            out_specs=pl.BlockSpec(tile, lambda i, me: imap(i))),
        out_shape=jax.ShapeDtypeStruct((r, c), F32),
        compiler_params=_cparams("parallel"),
    )(me_arr, g4, land)


def _adamw_math(w, g, m, v):
    m = ADAM_B1 * m + (1.0 - ADAM_B1) * g
    v = ADAM_B2 * v + (1.0 - ADAM_B2) * (g * g)
    m_hat = m / (1.0 - ADAM_B1 ** ADAM_STEP)
    v_hat = v / (1.0 - ADAM_B2 ** ADAM_STEP)
    delta = -ADAM_LR * (m_hat / (jnp.sqrt(v_hat) + ADAM_EPS) + ADAM_WD * w)
    return delta, m, v


def _adamw(w, grad_parts, m, v, *, name):
    r, c = w.shape
    tile_shape, steps, imap = _elementwise_tile(r, c)
    n = len(grad_parts)

    def body(*refs):
        w_ref, m_ref, v_ref = refs[:3]
        g_refs = refs[3:3 + n]
        g_out, d_out, m_out, v_out = refs[3 + n:]
        g = g_refs[0][...]
        for k in range(1, n):
            g = g + g_refs[k][...]
        delta, m_new, v_new = _adamw_math(w_ref[...], g, m_ref[...], v_ref[...])
        g_out[...] = g
        d_out[...] = delta
        m_out[...] = m_new
        v_out[...] = v_new

    tile = pl.BlockSpec(tile_shape, imap)
    return pl.pallas_call(
        body, name=name, grid=(steps,), in_specs=[tile] * (3 + n), out_specs=[tile] * 4,
        out_shape=[jax.ShapeDtypeStruct((r, c), F32)] * 4,
        compiler_params=_cparams("parallel"),
    )(w, m, v, *grad_parts)


WEIGHT_NAMES = ("meta_tokens", "norm1_w", "w_in", "ssd_conv_w", "ssd_conv_b", "ssd_dt_bias", "ssd_a_log", "ssd_d",
                "ssd_norm_w", "lru_conv_w", "lru_conv_b", "lru_wa", "lru_ba", "lru_wx", "lru_bx", "lru_lambda",
                "lru_norm_w", "w_out", "norm2_w", "w_gate", "w_up", "w_down", "final_norm_w")
BIG = ("w_in", "w_out", "w_gate", "w_up", "w_down")
FFN = ("w_gate", "w_up", "w_down")
LATE = ("w_out",) + FFN
SMALL_SHARDED = {"meta_tokens": (N_META, D_MODEL), "ssd_conv_w": (CONV_K, 1536), "lru_conv_w": (CONV_K, LRU_WIDTH)}
SMALL = tuple(n for n in WEIGHT_NAMES if n not in BIG)
PACK_COLS = 1024


def _pack(arrays, row_multiple):
    flat = jnp.concatenate([a.reshape(-1) for a in arrays])
    rows = -(-flat.shape[0] // (row_multiple * PACK_COLS)) * row_multiple
    return jnp.pad(flat, (0, rows * PACK_COLS - flat.shape[0])).reshape(rows, PACK_COLS)


def _unpack(pack, shapes):
    flat = pack.reshape(-1)
    out, off = [], 0
    for s in shapes:
        size = math.prod(s)
        out.append(flat[off:off + size].reshape(s))
        off += size
    return out


def _unshard_cols(g4):
    return jnp.swapaxes(g4, 0, 1).reshape(g4.shape[1], -1)


COL_SHARDED = ("w_in", "w_gate", "w_up")
IN_ROWS = {"z": (0, 1024), "xs": (1024, 2048), "bc": (2048, 2560), "dt": (2560, 2576), "g": (2576, 3600),
           "x": (3600, IN_COLS)}


def _rows_view(name, block):
    return jnp.swapaxes(block[0], 0, 1) if name in COL_SHARDED else block[0]


def _param_view(name, rows):
    return (jnp.swapaxes(rows, 0, 1) if name in COL_SHARDED else rows)[None]


def kernel(x, meta_tokens, norm1_w, w_in, ssd_conv_w, ssd_conv_b, ssd_dt_bias, ssd_a_log, ssd_d, ssd_norm_w, lru_conv_w, lru_conv_b, lru_wa, lru_ba, lru_wx, lru_bx, lru_lambda, lru_norm_w, w_out, norm2_w, w_gate, w_up, w_down, final_norm_w, loss_target, m_meta_tokens, m_norm1_w, m_w_in, m_ssd_conv_w, m_ssd_conv_b, m_ssd_dt_bias, m_ssd_a_log, m_ssd_d, m_ssd_norm_w, m_lru_conv_w, m_lru_conv_b, m_lru_wa, m_lru_ba, m_lru_wx, m_lru_bx, m_lru_lambda, m_lru_norm_w, m_w_out, m_norm2_w, m_w_gate, m_w_up, m_w_down, m_final_norm_w, v_meta_tokens, v_norm1_w, v_w_in, v_ssd_conv_w, v_ssd_conv_b, v_ssd_dt_bias, v_ssd_a_log, v_ssd_d, v_ssd_norm_w, v_lru_conv_w, v_lru_conv_b, v_lru_wa, v_lru_ba, v_lru_wx, v_lru_bx, v_lru_lambda, v_lru_norm_w, v_w_out, v_norm2_w, v_w_gate, v_w_up, v_w_down, v_final_norm_w):
    w = dict(zip(WEIGHT_NAMES, (meta_tokens, norm1_w, w_in, ssd_conv_w, ssd_conv_b, ssd_dt_bias, ssd_a_log, ssd_d, ssd_norm_w, lru_conv_w, lru_conv_b, lru_wa, lru_ba, lru_wx, lru_bx, lru_lambda, lru_norm_w, w_out, norm2_w, w_gate, w_up, w_down, final_norm_w)))
    m = dict(zip(WEIGHT_NAMES, (m_meta_tokens, m_norm1_w, m_w_in, m_ssd_conv_w, m_ssd_conv_b, m_ssd_dt_bias, m_ssd_a_log, m_ssd_d, m_ssd_norm_w, m_lru_conv_w, m_lru_conv_b, m_lru_wa, m_lru_ba, m_lru_wx, m_lru_bx, m_lru_lambda, m_lru_norm_w, m_w_out, m_norm2_w, m_w_gate, m_w_up, m_w_down, m_final_norm_w)))
    v = dict(zip(WEIGHT_NAMES, (v_meta_tokens, v_norm1_w, v_w_in, v_ssd_conv_w, v_ssd_conv_b, v_ssd_dt_bias, v_ssd_a_log, v_ssd_d, v_ssd_norm_w, v_lru_conv_w, v_lru_conv_b, v_lru_wa, v_lru_ba, v_lru_wx, v_lru_bx, v_lru_lambda, v_lru_norm_w, v_w_out, v_norm2_w, v_w_gate, v_w_up, v_w_down, v_final_norm_w)))
    me = 2 * lax.axis_index("x") + lax.axis_index("y")

    big2d = {n: _rows_view(n, w[n]) for n in BIG}
    small_local = jnp.concatenate([w["meta_tokens"].reshape(-1), w["ssd_conv_w"].reshape(-1),
                                   w["lru_conv_w"].reshape(-1)])[None]
    me_arr = me.astype(jnp.int32).reshape(1)
    dev_arr = (2 * me + lax.axis_index("c")).astype(jnp.int32).reshape(1)
    w_in4, small4 = _gather_first(big2d["w_in"].astype(_MXU), small_local)
    w_in_full = w_in4.reshape(-1, D_MODEL)
    sm = small4[:, 0]
    meta_full = _unshard_cols(sm[:, :4096].reshape(N_SHARDS, N_META, 256))
    ssd_conv_w_full = _unshard_cols(sm[:, 4096:5632].reshape(N_SHARDS, CONV_K, 384))
    lru_conv_w_full = _unshard_cols(sm[:, 5632:].reshape(N_SHARDS, CONV_K, 256))
    slots = {n: _fill_own_slot(big2d[n], me_arr, name="own_slot_" + n) for n in LATE}
    out_send, out_recv, out_bufs, tok_a = _split_start([slots["w_out"]], _gather_plan, 3, small4,
                                                       name="gather_w_out_start")
    ffn_send, ffn_recv, ffn_bufs, tok_b = _split_start([slots[n] for n in FFN], _gather_plan, 9, tok_a,
                                                       name="gather_ffn_start")

    p = {"w_in_" + s: w_in_full[lo:hi] for s, (lo, hi) in IN_ROWS.items()}
    p["w_in_dt"] = jnp.pad(p["w_in_dt"], ((0, SEC_WIDTH["dt"] - SSD_HEADS), (0, 0)))
    p.update({"ssd_conv_w": ssd_conv_w_full, "lru_conv_w": lru_conv_w_full,
              "lru_wa": w["lru_wa"][0], "lru_wx": w["lru_wx"][0], "final_norm_w": w["final_norm_w"][None]})
    for n in ("norm1_w", "ssd_conv_b", "ssd_dt_bias", "ssd_a_log", "ssd_d", "ssd_norm_w", "lru_conv_b", "lru_ba",
              "lru_bx", "lru_lambda", "lru_norm_w", "norm2_w"):
        p[n] = w[n]
    p["norm1_w"] = p["norm1_w"] + tok_b[:1, :1]

    class Late:
        def __init__(self):
            self.pending = []

        def w_out(self, after):
            (buf,) = _split_wait(out_bufs, out_send, out_recv, _gather_plan, after, name="gather_w_out_wait")
            return buf.reshape(-1, D_MODEL)

        def ffn(self, after):
            bufs = _split_wait(ffn_bufs, ffn_send, ffn_recv, _gather_plan, after, name="gather_ffn_wait")
            return tuple(b.reshape(-1, D_MODEL) for b in bufs)

        def grads_ready(self, names, g, g_mxu):
            srcs = [g_mxu[n].reshape(N_SHARDS, -1, D_MODEL) for n in names]
            lands = [lax.empty((3,) + s.shape[1:], _MXU) for s in srcs]
            tag = "_".join(names)
            send, recv, bufs, tok = _split_start(srcs + lands, _scatter_plan, 3 * len(names), g[names[-1]],
                                                 name="scatter_" + tag + "_start")
            self.pending.append((names, send, recv, bufs, tag))
            return tok[:1, :1]

        def landed(self, after):
            land = {}
            for names, send, recv, bufs, tag in self.pending:
                bufs = _split_wait(bufs, send, recv, _scatter_plan, after, name="scatter_" + tag + "_wait")
                land.update(zip(names, bufs[len(names):]))
            return land

        def small_ready(self, g, loss):
            pack = _pack([g[n] for n in SMALL] + [loss[0, :1]], 8 * N_DEV)
            pack = pack.reshape(N_DEV, -1, PACK_COLS)
            self.small = _split_start([pack, lax.empty(pack.shape, F32)], _pieces_plan, N_DEV - 1, loss,
                                      name="small_pieces_start")
            return self.small[3]

        def small_middle(self, after):
            send, recv, bufs, _ = self.small
            pack, land = _split_wait(bufs, send, recv, _pieces_plan, after, name="small_pieces_wait")
            piece = _sum_pieces(pack, land, dev_arr, name="small_pieces_sum")
            self.small = _split_start([piece, lax.empty(pack.shape, F32)], _spread_plan, N_DEV - 1, None,
                                      name="small_spread_start")
            return self.small[3]

        def small_sum(self, after):
            send, recv, bufs, _ = self.small
            piece, land = _split_wait(bufs, send, recv, _spread_plan, after, name="small_spread_wait")
            return _join_pieces(piece, land, dev_arr, name="small_join")

    late = Late()

    loss, grad_x, g, g_mxu = _local_step(x[0], loss_target[0], meta_full, p, late)

    g["w_in"] = jnp.concatenate([g["w_in_" + s][:hi - lo] for s, (lo, hi) in IN_ROWS.items()], axis=0)
    g_mxu["w_in"] = jnp.concatenate([g_mxu["w_in_" + s][:hi - lo] for s, (lo, hi) in IN_ROWS.items()], axis=0)
    g4 = {n: g[n].reshape(N_SHARDS, -1, D_MODEL) for n in BIG}
    (land_w_in,) = _scatter_grads([g_mxu["w_in"].reshape(N_SHARDS, -1, D_MODEL)])
    land = late.landed(land_w_in)
    land["w_in"] = land_w_in
    part = {n: _partial_sum(g4[n], land[n], me_arr, name="partial_" + n) for n in BIG}
    sib = dict(zip(BIG, _swap_with_sibling([part[n] for n in BIG])))

    small_full_shape = {n: (SMALL_SHARDED[n] if n in SMALL_SHARDED else w[n].shape) for n in SMALL}
    red_list = _unpack(late.small_sum(sib["w_in"]), [small_full_shape[n] for n in SMALL] + [(1,)])
    loss_total = red_list[-1][0]
    g_small = {}
    for n, arr in zip(SMALL, red_list[:-1]):
        if n in SMALL_SHARDED:
            cols = SMALL_SHARDED[n][1] // N_SHARDS
            arr = lax.dynamic_slice_in_dim(arr, me * cols, cols, axis=1)
        g_small[n] = arr.reshape(w[n].shape)

    grad, delta, new_m, new_v = {}, {}, {}, {}
    for n in BIG:
        outs = _adamw(big2d[n], [part[n], sib[n]], _rows_view(n, m[n]), _rows_view(n, v[n]), name="adamw_" + n)
        grad[n], delta[n], new_m[n], new_v[n] = (_param_view(n, o) for o in outs)
    two_d = lambda a: a.reshape(1, -1) if a.ndim == 1 else a
    deltas, new_ms, new_vs = _adamw_native(*[[two_d(d[n]) for n in SMALL] for d in (w, g_small, m, v)])
    for n, dn, mn, vn in zip(SMALL, deltas, new_ms, new_vs):
        grad[n], delta[n], new_m[n], new_v[n] = (g_small[n], dn.reshape(w[n].shape), mn.reshape(w[n].shape),
                                                 vn.reshape(w[n].shape))

    return (loss_total, grad_x[None], *[grad[n] for n in WEIGHT_NAMES], *[delta[n] for n in WEIGHT_NAMES],
            *[new_m[n] for n in WEIGHT_NAMES], *[new_v[n] for n in WEIGHT_NAMES])
```

```python
import functools
import math

import jax
import jax.numpy as jnp
from jax import lax
from jax.experimental import pallas as pl
from jax.experimental.pallas import tpu as pltpu

F32 = jnp.float32
_MXU = jnp.bfloat16

D_MODEL = 1024
SEQ = 2048
N_META = 16
CHUNK = 128
T_ROWS = 2176
N_CHUNKS = T_ROWS // CHUNK
PAD_ROWS = T_ROWS - SEQ - N_META
X_ROW0 = PAD_ROWS + N_META
SSD_HEADS = 16
SSD_HEAD_DIM = 64
SSD_STATE = 128
SSD_GROUPS = 2
SSD_HPG = SSD_HEADS // SSD_GROUPS
SSD_WIDTH = 1024
LRU_WIDTH = 1024
LRU_C = 8.0
D_FF = 2816
EPS = 1e-6
IN_COLS = 4624
N_SHARDS = 4
N_DEV = 8

ADAM_LR = 0.001
ADAM_B1 = 0.9
ADAM_B2 = 0.999
ADAM_EPS = 1e-08
ADAM_WD = 0.01
ADAM_STEP = 10

VMEM_LIMIT_BYTES = 56 * 1024 * 1024

NN = (((1,), (0,)), ((), ()))
NT = (((1,), (1,)), ((), ()))
TN = (((0,), (0,)), ((), ()))


def _cparams(*sem):
    return pltpu.CompilerParams(dimension_semantics=sem, vmem_limit_bytes=VMEM_LIMIT_BYTES)


def _dot(a, b, dims=NN):
    return lax.dot_general(a.astype(_MXU), b.astype(_MXU), dims, preferred_element_type=F32)


def _dot_exact(a, b, dims=NN):
    return lax.dot_general(a, b, dims, preferred_element_type=F32, precision=lax.Precision.HIGHEST)


def _sigmoid(x):
    return 0.5 * (1.0 + jnp.tanh(0.5 * x))


def _softplus(x):
    return jnp.maximum(x, 0.0) + jnp.log(1.0 + jnp.exp(-jnp.abs(x)))


def _silu(x):
    return x * _sigmoid(x)


def _silu_grad(x):
    s = _sigmoid(x)
    return s * (1.0 + x * (1.0 - s))


_GELU_C = math.sqrt(2.0 / math.pi)


def _gelu_and_grad(x):
    inner = _GELU_C * (x + 0.044715 * x * x * x)
    t = jnp.tanh(inner)
    g = 0.5 * x * (1.0 + t)
    dg = 0.5 * (1.0 + t) + 0.5 * x * (1.0 - t * t) * _GELU_C * (1.0 + 3.0 * 0.044715 * x * x)
    return g, dg


def _rms_fwd(x, w):
    rstd = lax.rsqrt(jnp.mean(x * x, axis=-1, keepdims=True) + EPS)
    return x * rstd * w


def _rms_bwd(x, w, dy):
    rstd = lax.rsqrt(jnp.mean(x * x, axis=-1, keepdims=True) + EPS)
    xhat = x * rstd
    dxhat = dy * w
    dx = rstd * (dxhat - xhat * jnp.mean(dxhat * xhat, axis=-1, keepdims=True))
    return dx, dy * xhat


def _mm(terms, m, n, *, tm, tn, mode, out_dtype, name, residual=None, n_outer=False, also_mxu=False, behind=()):
    gm, gn = m // tm, n // tn
    assert gm * tm == m and gn * tn == n
    if n_outer:
        grid = (gn, gm)
        mi = lambda g0, g1: g1
        ni = lambda g0, g1: g0
    else:
        grid = (gm, gn)
        mi = lambda g0, g1: g0
        ni = lambda g0, g1: g1
    in_specs, args = [], []
    for (a, ka, b, kb, k) in terms:
        if mode == "tn":
            in_specs.append(pl.BlockSpec((k, tm), lambda g0, g1, ka=ka: (ka, mi(g0, g1))))
        else:
            in_specs.append(pl.BlockSpec((tm, k), lambda g0, g1, ka=ka: (mi(g0, g1), ka)))
        if mode == "nt":
            in_specs.append(pl.BlockSpec((tn, k), lambda g0, g1, kb=kb: (ni(g0, g1), kb)))
        else:
            in_specs.append(pl.BlockSpec((k, tn), lambda g0, g1, kb=kb: (kb, ni(g0, g1))))
        args += [a, b]
    if residual is not None:
        in_specs.append(pl.BlockSpec((tm, tn), lambda g0, g1: (mi(g0, g1), ni(g0, g1))))
        args.append(residual)
    dims = {"nn": NN, "nt": NT, "tn": TN}[mode]
    n_terms = len(terms)
    has_res = residual is not None
    in_specs += [pl.BlockSpec(memory_space=pl.ANY)] * len(behind)
    args += list(behind)
    n_in = len(args)

    def body(*refs):
        acc = None
        for t in range(n_terms):
            d = lax.dot_general(refs[2 * t][...], refs[2 * t + 1][...], dims, preferred_element_type=F32)
            acc = d if acc is None else acc + d
        if has_res:
            acc = acc + refs[2 * n_terms][...]
        refs[n_in][...] = acc.astype(out_dtype)
        if also_mxu:
            refs[n_in + 1][...] = acc.astype(_MXU)

    tile = pl.BlockSpec((tm, tn), lambda g0, g1: (mi(g0, g1), ni(g0, g1)))
    shape = jax.ShapeDtypeStruct((m, n), out_dtype)
    return pl.pallas_call(
        body, name=name, grid=grid, in_specs=in_specs,
        out_specs=[tile, tile] if also_mxu else tile,
        out_shape=[shape, jax.ShapeDtypeStruct((m, n), _MXU)] if also_mxu else shape,
        compiler_params=_cparams("parallel", "parallel"),
    )(*args)


def _embed(x, meta):
    def body(x_ref, meta_ref, o_ref):
        i = pl.program_id(0)

        @pl.when(i == 0)
        def _():
            o_ref[0:PAD_ROWS, :] = jnp.zeros((PAD_ROWS, D_MODEL), F32)
            o_ref[PAD_ROWS:CHUNK, :] = meta_ref[...]

        @pl.when(i > 0)
        def _():
            o_ref[...] = x_ref[...]

    return pl.pallas_call(
        body, name="embed", grid=(N_CHUNKS,),
        in_specs=[pl.BlockSpec((CHUNK, D_MODEL), lambda i: (jnp.maximum(i - 1, 0), 0)),
                  pl.BlockSpec((N_META, D_MODEL), lambda i: (0, 0))],
        out_specs=pl.BlockSpec((CHUNK, D_MODEL), lambda i: (i, 0)),
        out_shape=jax.ShapeDtypeStruct((T_ROWS, D_MODEL), F32),
        compiler_params=_cparams("parallel"),
    )(x, meta)


def _rmsnorm(h, w, *, name, tm=544):
    def body(h_ref, w_ref, o_ref):
        o_ref[...] = _rms_fwd(h_ref[...], w_ref[...]).astype(_MXU)

    return pl.pallas_call(
        body, name=name, grid=(T_ROWS // tm,),
        in_specs=[pl.BlockSpec((tm, D_MODEL), lambda i: (i, 0)), pl.BlockSpec((1, D_MODEL), lambda i: (0, 0))],
        out_specs=pl.BlockSpec((tm, D_MODEL), lambda i: (i, 0)),
        out_shape=jax.ShapeDtypeStruct((T_ROWS, D_MODEL), _MXU),
        compiler_params=_cparams("parallel"),
    )(h, w)


def _loss_head(h2, target, fw):
    def body(h_ref, t_ref, w_ref, loss_ref, dh_ref, dhb_ref, dw_ref, acc_ref):
        i = pl.program_id(0)

        @pl.when(i == 0)
        def _():
            acc_ref[...] = jnp.zeros_like(acc_ref)
            dw_ref[...] = jnp.zeros_like(dw_ref)

        h = h_ref[...]
        w = w_ref[...]
        y = _rms_fwd(h, w)
        live = (i > 0).astype(F32)
        err = (y - t_ref[...]) * live
        acc_ref[...] += jnp.sum(err * err, axis=0, keepdims=True)
        dy = err * (1.0 / D_MODEL)
        dx, dwr = _rms_bwd(h, w, dy)
        dh_ref[...] = dx
        dhb_ref[...] = dx.astype(_MXU)
        dw_ref[...] += jnp.sum(dwr, axis=0, keepdims=True)

        @pl.when(i == N_CHUNKS - 1)
        def _():
            tot = jnp.sum(acc_ref[...], axis=1, keepdims=True) * (0.5 / D_MODEL)
            loss_ref[...] = jnp.broadcast_to(tot, (1, 128))

    return pl.pallas_call(
        body, name="loss_head", grid=(N_CHUNKS,),
        in_specs=[pl.BlockSpec((CHUNK, D_MODEL), lambda i: (i, 0)),
                  pl.BlockSpec((CHUNK, D_MODEL), lambda i: (jnp.maximum(i - 1, 0), 0)),
                  pl.BlockSpec((1, D_MODEL), lambda i: (0, 0))],
        out_specs=[pl.BlockSpec((1, 128), lambda i: (0, 0)),
                   pl.BlockSpec((CHUNK, D_MODEL), lambda i: (i, 0)),
                   pl.BlockSpec((CHUNK, D_MODEL), lambda i: (i, 0)),
                   pl.BlockSpec((1, D_MODEL), lambda i: (0, 0))],
        out_shape=[jax.ShapeDtypeStruct((1, 128), F32),
                   jax.ShapeDtypeStruct((T_ROWS, D_MODEL), F32),
                   jax.ShapeDtypeStruct((T_ROWS, D_MODEL), _MXU),
                   jax.ShapeDtypeStruct((1, D_MODEL), F32)],
        scratch_shapes=[pltpu.VMEM((1, D_MODEL), F32)],
        compiler_params=_cparams("arbitrary"),
    )(h2, target, fw)


def _mm_norm_bwd(terms, h, w, dres, *, name, tm=272):
    n_terms = len(terms)
    in_specs, args = [], []
    for (a, b, k) in terms:
        in_specs += [pl.BlockSpec((tm, k), lambda i: (i, 0)), pl.BlockSpec((k, D_MODEL), lambda i: (0, 0))]
        args += [a, b]
    in_specs += [pl.BlockSpec((tm, D_MODEL), lambda i: (i, 0)), pl.BlockSpec((1, D_MODEL), lambda i: (0, 0)),
                 pl.BlockSpec((tm, D_MODEL), lambda i: (i, 0))]
    args += [h, w, dres]

    def body(*refs):
        h_ref, w_ref, dres_ref, dh_ref, dhb_ref, dw_ref = refs[2 * n_terms:]

        @pl.when(pl.program_id(0) == 0)
        def _():
            dw_ref[...] = jnp.zeros_like(dw_ref)

        du = None
        for t in range(n_terms):
            d = lax.dot_general(refs[2 * t][...], refs[2 * t + 1][...], NN, preferred_element_type=F32)
            du = d if du is None else du + d
        dx, dwr = _rms_bwd(h_ref[...], w_ref[...], du)
        dh = dres_ref[...] + dx
        dh_ref[...] = dh
        dhb_ref[...] = dh.astype(_MXU)
        dw_ref[...] += jnp.sum(dwr, axis=0, keepdims=True)

    return pl.pallas_call(
        body, name=name, grid=(T_ROWS // tm,), in_specs=in_specs,
        out_specs=[pl.BlockSpec((tm, D_MODEL), lambda i: (i, 0)), pl.BlockSpec((tm, D_MODEL), lambda i: (i, 0)),
                   pl.BlockSpec((1, D_MODEL), lambda i: (0, 0))],
        out_shape=[jax.ShapeDtypeStruct((T_ROWS, D_MODEL), F32), jax.ShapeDtypeStruct((T_ROWS, D_MODEL), _MXU),
                   jax.ShapeDtypeStruct((1, D_MODEL), F32)],
        compiler_params=_cparams("arbitrary"),
    )(*args)


FFN_TM = 272
FFN_TN = 1408


def _ffn_up(u2, wg_t, wu_t):
    def body(u_ref, wg_ref, wu_ref, gp_ref, up_ref, act_ref):
        u = u_ref[...]
        gp = lax.dot_general(u, wg_ref[...], NT, preferred_element_type=F32)
        up = lax.dot_general(u, wu_ref[...], NT, preferred_element_type=F32)
        gp_ref[...] = gp
        up_ref[...] = up
        act_ref[...] = (_silu(gp) * up).astype(_MXU)

    tile = pl.BlockSpec((FFN_TM, FFN_TN), lambda j, i: (i, j))
    return pl.pallas_call(
        body, name="ffn_up", grid=(D_FF // FFN_TN, T_ROWS // FFN_TM),
        in_specs=[pl.BlockSpec((FFN_TM, D_MODEL), lambda j, i: (i, 0)),
                  pl.BlockSpec((FFN_TN, D_MODEL), lambda j, i: (j, 0)),
                  pl.BlockSpec((FFN_TN, D_MODEL), lambda j, i: (j, 0))],
        out_specs=[tile, tile, tile],
        out_shape=[jax.ShapeDtypeStruct((T_ROWS, D_FF), F32), jax.ShapeDtypeStruct((T_ROWS, D_FF), F32),
                   jax.ShapeDtypeStruct((T_ROWS, D_FF), _MXU)],
        compiler_params=_cparams("parallel", "parallel"),
    )(u2, wg_t, wu_t)


def _ffn_bwd_act(dh2b, wd, gp, up):
    def body(dh_ref, wd_ref, gp_ref, up_ref, dgp_ref, dup_ref):
        dact = lax.dot_general(dh_ref[...], wd_ref[...], NT, preferred_element_type=F32)
        gp = gp_ref[...]
        dgp_ref[...] = (dact * up_ref[...] * _silu_grad(gp)).astype(_MXU)
        dup_ref[...] = (dact * _silu(gp)).astype(_MXU)

    tile = pl.BlockSpec((FFN_TM, FFN_TN), lambda j, i: (i, j))
    return pl.pallas_call(
        body, name="ffn_bwd_act", grid=(D_FF // FFN_TN, T_ROWS // FFN_TM),
        in_specs=[pl.BlockSpec((FFN_TM, D_MODEL), lambda j, i: (i, 0)),
                  pl.BlockSpec((FFN_TN, D_MODEL), lambda j, i: (j, 0)), tile, tile],
        out_specs=[tile, tile],
        out_shape=[jax.ShapeDtypeStruct((T_ROWS, D_FF), _MXU), jax.ShapeDtypeStruct((T_ROWS, D_FF), _MXU)],
        compiler_params=_cparams("parallel", "parallel"),
    )(dh2b, wd, gp, up)


CONV_TC = 512
CONV_K = 4


def _conv_pre(x_ref, wv, bv, c):
    tc = wv.shape[1]
    r0 = c * CHUNK
    cur = x_ref[r0:r0 + CHUNK, :]
    if c == 0:
        cat = jnp.concatenate([jnp.zeros((8, tc), F32), cur], axis=0)
        shifted = [cur] + [pltpu.roll(cat, s, 0)[8:8 + CHUNK] for s in range(1, CONV_K)]
    else:
        shifted = [cur] + [x_ref[r0 - s:r0 - s + CHUNK, :] for s in range(1, CONV_K)]
    pre = bv
    for s in range(CONV_K):
        pre = pre + shifted[s] * wv[CONV_K - 1 - s:CONV_K - s]
    return pre, shifted


def _row_mask(c):
    if c > 0:
        return None
    return (lax.broadcasted_iota(jnp.int32, (CHUNK, 1), 0) >= PAD_ROWS).astype(F32)


def _conv_fwd(x, w, b, *, silu, name):
    cols = x.shape[1]
    tc = min(CONV_TC, cols)

    def body(x_ref, w_ref, b_ref, o_ref):
        wv, bv = w_ref[...], b_ref[...]
        for c in range(N_CHUNKS):
            pre, _ = _conv_pre(x_ref, wv, bv, c)
            y = _silu(pre) if silu else pre
            mask = _row_mask(c)
            if mask is not None:
                y = y * mask
            o_ref[c * CHUNK:(c + 1) * CHUNK, :] = y

    return pl.pallas_call(
        body, name=name, grid=(cols // tc,),
        in_specs=[pl.BlockSpec((T_ROWS, tc), lambda j: (0, j)), pl.BlockSpec((CONV_K, tc), lambda j: (0, j)),
                  pl.BlockSpec((1, tc), lambda j: (0, j))],
        out_specs=pl.BlockSpec((T_ROWS, tc), lambda j: (0, j)),
        out_shape=jax.ShapeDtypeStruct((T_ROWS, cols), F32),
        compiler_params=_cparams("parallel"),
    )(x, w, b)


def _conv_bwd(dy, x, w, b, *, silu, name):
    cols = x.shape[1]
    tc = min(CONV_TC, cols)

    def body(dy_ref, x_ref, w_ref, b_ref, dx_ref, dw_ref, db_ref, dpre_s):
        wv, bv = w_ref[...], b_ref[...]
        dpre_s[T_ROWS:T_ROWS + 8, :] = jnp.zeros((8, tc), F32)
        dws = [jnp.zeros((1, tc), F32) for _ in range(CONV_K)]
        db = jnp.zeros((1, tc), F32)
        for c in reversed(range(N_CHUNKS)):
            r0 = c * CHUNK
            pre, shifted = _conv_pre(x_ref, wv, bv, c)
            dpre = dy_ref[r0:r0 + CHUNK, :]
            if silu:
                dpre = dpre * _silu_grad(pre)
            mask = _row_mask(c)
            if mask is not None:
                dpre = dpre * mask
            dpre_s[r0:r0 + CHUNK, :] = dpre
            dx = dpre * wv[CONV_K - 1:CONV_K]
            for s in range(1, CONV_K):
                dx = dx + dpre_s[r0 + s:r0 + s + CHUNK, :] * wv[CONV_K - 1 - s:CONV_K - s]
            dx_ref[r0:r0 + CHUNK, :] = dx.astype(_MXU)
            for s in range(CONV_K):
                k = CONV_K - 1 - s
                dws[k] = dws[k] + jnp.sum(dpre * shifted[s], axis=0, keepdims=True)
            db = db + jnp.sum(dpre, axis=0, keepdims=True)
        dw_ref[...] = jnp.concatenate(dws, axis=0)
        db_ref[...] = db

    return pl.pallas_call(
        body, name=name, grid=(cols // tc,),
        in_specs=[pl.BlockSpec((T_ROWS, tc), lambda j: (0, j)), pl.BlockSpec((T_ROWS, tc), lambda j: (0, j)),
                  pl.BlockSpec((CONV_K, tc), lambda j: (0, j)), pl.BlockSpec((1, tc), lambda j: (0, j))],
        out_specs=[pl.BlockSpec((T_ROWS, tc), lambda j: (0, j)), pl.BlockSpec((CONV_K, tc), lambda j: (0, j)),
                   pl.BlockSpec((1, tc), lambda j: (0, j))],
        out_shape=[jax.ShapeDtypeStruct((T_ROWS, cols), _MXU), jax.ShapeDtypeStruct((CONV_K, cols), F32),
                   jax.ShapeDtypeStruct((1, cols), F32)],
        scratch_shapes=[pltpu.VMEM((T_ROWS + 8, tc), F32)],
        compiler_params=_cparams("parallel"),
    )(dy, x, w, b)


def _ssd_chunk_common(dt_raw, prm, c):
    a_row = -jnp.exp(prm[1:2])
    dt = _softplus(dt_raw + prm[0:1])
    rows = lax.broadcasted_iota(jnp.int32, (CHUNK, 1), 0)
    real = jnp.logical_or(c > 0, rows >= PAD_ROWS)
    dt = jnp.where(real, dt, 0.0)
    li = lax.broadcasted_iota(jnp.int32, (CHUNK, CHUNK), 0)
    si = lax.broadcasted_iota(jnp.int32, (CHUNK, CHUNK), 1)
    causal = li >= si
    tri = causal.astype(F32)
    cs = _dot_exact(tri, dt * a_row)
    return dt, a_row, cs, cs.T, causal, tri, real


def _gated_norm_fwd(y, z, w):
    g = y * _silu(z)
    half = SSD_WIDTH // SSD_GROUPS
    outs = [_rms_fwd(g[:, k * half:(k + 1) * half], w[:, k * half:(k + 1) * half]) for k in range(SSD_GROUPS)]
    return jnp.concatenate(outs, axis=1)


GROUP_W = SSD_WIDTH // SSD_GROUPS
PAIR_W = 2 * SSD_HEAD_DIM
STATE_SHAPE = (SSD_GROUPS, SSD_STATE, GROUP_W)


def _head_expander():
    r = lax.broadcasted_iota(jnp.int32, (128, SSD_WIDTH), 0)
    c = lax.broadcasted_iota(jnp.int32, (128, SSD_WIDTH), 1)
    return (c // SSD_HEAD_DIM == r).astype(F32)


def _ssd_expand(dt, cs, prm, ex):
    cs_x = _dot_exact(cs, ex)
    cs_last_x = cs_x[CHUNK - 1:CHUNK, :]
    return (_dot_exact(dt, ex), _dot_exact(prm, ex)[2:3], jnp.exp(cs_x), jnp.exp(cs_last_x),
            jnp.exp(cs_last_x - cs_x))


def _ssd_fwd(xs, bc, dt_raw, z, prm, norm_w, ex):
    def body(xs_ref, bc_ref, dt_ref, z_ref, prm_ref, nw_ref, ex_ref, y_ref, yn_ref, prev_ref, state):
        c = pl.program_id(0)

        @pl.when(c == 0)
        def _():
            state[...] = jnp.zeros_like(state)

        prm = prm_ref[...]
        dt, a_row, cs, cs_t, causal, _, _ = _ssd_chunk_common(dt_ref[...], prm, c)
        dt_x, d_x, e_cs_x, e_last_x, dec_x = _ssd_expand(dt, cs, prm, ex_ref[...])
        xs_all = xs_ref[...]
        bc_all = bc_ref[...]
        xdt = xs_all * dt_x
        xdec = xdt * dec_x
        lane_lo = lax.broadcasted_iota(jnp.int32, (1, PAIR_W), 1) < SSD_HEAD_DIM
        for g in range(SSD_GROUPS):
            gs = slice(g * GROUP_W, (g + 1) * GROUP_W)
            b_g = bc_all[:, g * SSD_STATE:(g + 1) * SSD_STATE]
            c_g = bc_all[:, (SSD_GROUPS + g) * SSD_STATE:(SSD_GROUPS + g + 1) * SSD_STATE]
            st = state[g]
            prev_ref[0, g] = st
            y_off = _dot(c_g, st) * e_cs_x[:, gs]
            state[g] = st * e_last_x[:, gs] + _dot(b_g.T, xdec[:, gs])
            cb = _dot(c_g, b_g, NT)
            for k in range(SSD_HPG // 2):
                h0 = g * SSD_HPG + 2 * k
                ps = slice(h0 * SSD_HEAD_DIM, h0 * SSD_HEAD_DIM + PAIR_W)
                xdt_pair = xdt[:, ps]
                yd = []
                for h in (h0, h0 + 1):
                    lmat = jnp.where(causal, jnp.exp(cs[:, h:h + 1] - cs_t[h:h + 1, :]), 0.0)
                    yd.append(_dot(cb * lmat, xdt_pair))
                y_ref[:, ps] = (jnp.where(lane_lo, yd[0], yd[1]) + y_off[:, k * PAIR_W:(k + 1) * PAIR_W]
                                + xs_all[:, ps] * d_x[:, ps])
        yn_ref[...] = _gated_norm_fwd(y_ref[...], z_ref[...], nw_ref[...]).astype(_MXU)

    row = lambda w: pl.BlockSpec((CHUNK, w), lambda c: (c, 0))
    return pl.pallas_call(
        body, name="ssd_fwd", grid=(N_CHUNKS,),
        in_specs=[row(SSD_WIDTH), row(512), row(128), row(SSD_WIDTH),
                  pl.BlockSpec((8, 128), lambda c: (0, 0)), pl.BlockSpec((1, SSD_WIDTH), lambda c: (0, 0)),
                  pl.BlockSpec((128, SSD_WIDTH), lambda c: (0, 0))],
        out_specs=[row(SSD_WIDTH), row(SSD_WIDTH),
                   pl.BlockSpec((1,) + STATE_SHAPE, lambda c: (c, 0, 0, 0))],
        out_shape=[jax.ShapeDtypeStruct((T_ROWS, SSD_WIDTH), F32), jax.ShapeDtypeStruct((T_ROWS, SSD_WIDTH), _MXU),
                   jax.ShapeDtypeStruct((N_CHUNKS,) + STATE_SHAPE, F32)],
        scratch_shapes=[pltpu.VMEM(STATE_SHAPE, F32)],
        compiler_params=_cparams("arbitrary"),
    )(xs, bc, dt_raw, z, prm, norm_w, ex)


def _ssd_bwd(dyn, dyn_block, z, y_pre, xs, bc, dt_raw, prev, prm, norm_w, ex):
    def body(dyn_ref, z_ref, y_ref, xs_ref, bc_ref, dt_ref, prev_ref, prm_ref, nw_ref, ex_ref,
             dz_ref, dxs_ref, dbc_ref, ddt_ref, dprm_ref, dnw_ref, dstate):
        step = pl.program_id(0)
        c = N_CHUNKS - 1 - step

        @pl.when(step == 0)
        def _():
            dstate[...] = jnp.zeros_like(dstate)
            dprm_ref[...] = jnp.zeros_like(dprm_ref)
            dnw_ref[...] = jnp.zeros_like(dnw_ref)

        prm = prm_ref[...]
        dt, a_row, cs, cs_t, causal, tri, real = _ssd_chunk_common(dt_ref[...], prm, c)
        realf = real.astype(F32)
        z = z_ref[...]
        y_all = y_ref[...]
        nw = nw_ref[...]
        dyn_all = dyn_ref[...]
        sz = _silu(z)
        gated = y_all * sz
        half = SSD_WIDTH // SSD_GROUPS
        dgs, dnws = [], []
        for k in range(SSD_GROUPS):
            sl = slice(k * half, (k + 1) * half)
            dgk, dwk = _rms_bwd(gated[:, sl], nw[:, sl], dyn_all[:, sl])
            dgs.append(dgk)
            dnws.append(jnp.sum(dwk, axis=0, keepdims=True))
        dgated = jnp.concatenate(dgs, axis=1)
        dnw_ref[...] += jnp.concatenate(dnws, axis=1)
        dz_ref[...] = (dgated * y_all * _silu_grad(z)).astype(_MXU)
        dy_all = dgated * sz

        ex = ex_ref[...]
        dt_x, d_x, e_cs_x, e_last_x, dec_x = _ssd_expand(dt, cs, prm, ex)
        xs_all = xs_ref[...]
        bc_all = bc_ref[...]
        xdt = xs_all * dt_x
        xdt_mxu = xdt.astype(_MXU).astype(F32)
        xdec = xdt * dec_x
        dcp = dy_all * e_cs_x
        lane_lo = lax.broadcasted_iota(jnp.int32, (1, PAIR_W), 1) < SSD_HEAD_DIM
        upper = (lax.broadcasted_iota(jnp.int32, (CHUNK, CHUNK), 0)
                 <= lax.broadcasted_iota(jnp.int32, (CHUNK, CHUNK), 1))
        last_row = (lax.broadcasted_iota(jnp.int32, (CHUNK, 1), 0) == CHUNK - 1).astype(F32)
        dbs, dcs_, dxdt_parts, last_parts = [], [], [], []
        for g in range(SSD_GROUPS):
            gs = slice(g * GROUP_W, (g + 1) * GROUP_W)
            b_g = bc_all[:, g * SSD_STATE:(g + 1) * SSD_STATE]
            c_g = bc_all[:, (SSD_GROUPS + g) * SSD_STATE:(SSD_GROUPS + g + 1) * SSD_STATE]
            prev_t = prev_ref[0, g]
            dst = dstate[g]
            dc_g = _dot(dcp[:, gs], prev_t, NT)
            db_g = _dot(xdec[:, gs], dst, NT)
            dxdt_state = _dot(b_g, dst) * dec_x[:, gs]
            dstate[g] = dst * e_last_x[:, gs] + _dot(c_g.T, dcp[:, gs])
            last_parts.append(jnp.sum(xdt_mxu[:, gs] * dxdt_state, axis=0, keepdims=True)
                              + jnp.sum(dst * prev_t, axis=0, keepdims=True) * e_last_x[:, gs])
            cb_t = _dot(b_g, c_g, NT)
            dcb_t = jnp.zeros((CHUNK, CHUNK), F32)
            for k in range(SSD_HPG // 2):
                h0 = g * SSD_HPG + 2 * k
                ps = slice(h0 * SSD_HEAD_DIM, h0 * SSD_HEAD_DIM + PAIR_W)
                dy_pair = dy_all[:, ps]
                xdt_pair = xdt[:, ps]
                dd = []
                for h in (h0, h0 + 1):
                    lmat_t = jnp.where(upper, jnp.exp(cs_t[h:h + 1, :] - cs[:, h:h + 1]), 0.0)
                    dd.append(_dot(cb_t * lmat_t, dy_pair))
                    mine = lane_lo if h == h0 else jnp.logical_not(lane_lo)
                    dcb_t = dcb_t + _dot(jnp.where(mine, xdt_pair, 0.0), dy_pair, NT) * lmat_t
                dxdt_parts.append(jnp.where(lane_lo, dd[0], dd[1]) + dxdt_state[:, k * PAIR_W:(k + 1) * PAIR_W])
            dc_g = dc_g + _dot(dcb_t, b_g, TN)
            db_g = db_g + _dot(dcb_t, c_g)
            dbs.append(db_g * realf)
            dcs_.append(dc_g * realf)
        dbc_ref[...] = jnp.concatenate(dbs + dcs_, axis=1)
        dxdt = jnp.concatenate(dxdt_parts, axis=1)
        dxs_ref[...] = (dxdt * dt_x + dy_all * d_x) * realf
        ddt_all = _dot_exact(dxdt * xs_all, ex, NT)
        rows = jnp.concatenate([jnp.concatenate(last_parts, axis=1), jnp.sum(dy_all * xs_all, axis=0, keepdims=True),
                                jnp.zeros((6, SSD_WIDTH), F32)], axis=0)
        rows = _dot_exact(rows, ex, NT)
        dd_row = rows[1:2]
        dy_mxu = dy_all.astype(_MXU).astype(F32)
        dcs_all = (_dot_exact(dy_mxu * (y_all - xs_all * d_x), ex, NT) - _dot_exact(xdt_mxu * dxdt, ex, NT)
                   + last_row * rows[0:1])
        dda = _dot_exact(tri, dcs_all, TN)
        ddt = (ddt_all + dda * a_row) * realf
        ddt_raw = ddt * _sigmoid(dt_ref[...] + prm[0:1])
        ddt_ref[...] = ddt_raw.astype(_MXU)
        da_log = jnp.sum(dda * dt, axis=0, keepdims=True) * a_row
        dprm_ref[0:1, :] += jnp.sum(ddt_raw, axis=0, keepdims=True)
        dprm_ref[1:2, :] += da_log
        dprm_ref[2:3, :] += dd_row

    rev = lambda w, blk=0: pl.BlockSpec((CHUNK, w), lambda s, blk=blk: (N_CHUNKS - 1 - s, blk))
    return pl.pallas_call(
        body, name="ssd_bwd", grid=(N_CHUNKS,),
        in_specs=[rev(SSD_WIDTH, dyn_block), rev(SSD_WIDTH), rev(SSD_WIDTH), rev(SSD_WIDTH), rev(512), rev(128),
                  pl.BlockSpec((1,) + STATE_SHAPE, lambda s: (N_CHUNKS - 1 - s, 0, 0, 0)),
                  pl.BlockSpec((8, 128), lambda s: (0, 0)), pl.BlockSpec((1, SSD_WIDTH), lambda s: (0, 0)),
                  pl.BlockSpec((128, SSD_WIDTH), lambda s: (0, 0))],
        out_specs=[rev(SSD_WIDTH), rev(SSD_WIDTH), rev(512), rev(128),
                   pl.BlockSpec((8, 128), lambda s: (0, 0)), pl.BlockSpec((1, SSD_WIDTH), lambda s: (0, 0))],
        out_shape=[jax.ShapeDtypeStruct((T_ROWS, SSD_WIDTH), _MXU), jax.ShapeDtypeStruct((T_ROWS, SSD_WIDTH), F32),
                   jax.ShapeDtypeStruct((T_ROWS, 512), F32), jax.ShapeDtypeStruct((T_ROWS, 128), _MXU),
                   jax.ShapeDtypeStruct((8, 128), F32), jax.ShapeDtypeStruct((1, SSD_WIDTH), F32)],
        scratch_shapes=[pltpu.VMEM(STATE_SHAPE, F32)],
        compiler_params=_cparams("arbitrary"),
    )(dyn, z, y_pre, xs, bc, dt_raw, prev, prm, norm_w, ex)


LRU_PAIRS = 8


def _lru_gates(xr, wa_ref, wx_ref, prm):
    pre_r, pre_i = [], []
    for k in range(LRU_PAIRS):
        xk = xr[:, k * 128:(k + 1) * 128]
        pre_r.append(_dot(xk, wa_ref[k]))
        pre_i.append(_dot(xk, wx_ref[k]))
    r = _sigmoid(jnp.concatenate(pre_r, axis=1) + prm[0:1])
    i = _sigmoid(jnp.concatenate(pre_i, axis=1) + prm[1:2])
    sp = _softplus(-prm[2:3])
    log_a = (-LRU_C) * r * sp
    a = jnp.exp(log_a)
    s = jnp.sqrt(-jnp.tanh(log_a) * (a * a + 1.0))
    return r, i, a, s, sp


def _lru_fwd(xr, gate, wa, wx, prm):
    def body(xr_ref, g_ref, wa_ref, wx_ref, prm_ref, hs_ref, yn_ref, carry, a_s, u_s):
        @pl.when(pl.program_id(0) == 0)
        def _():
            carry[...] = jnp.zeros_like(carry)

        prm = prm_ref[...]
        xr_t = xr_ref[...]
        _, i, a, s, _ = _lru_gates(xr_t, wa_ref, wx_ref, prm)
        a_s[...] = a
        u_s[...] = s * (i * xr_t)
        rid = lax.broadcasted_iota(jnp.int32, (8, LRU_WIDTH), 0)

        def group(k, h):
            off = pl.multiple_of(k * 8, 8)
            a8 = a_s[pl.ds(off, 8), :]
            u8 = u_s[pl.ds(off, 8), :]
            out = jnp.zeros((8, LRU_WIDTH), F32)
            for r_ in range(8):
                h = a8[r_:r_ + 1] * h + u8[r_:r_ + 1]
                out = jnp.where(rid == r_, h, out)
            hs_ref[pl.ds(off, 8), :] = out
            return h

        carry[0:1, :] = lax.fori_loop(0, CHUNK // 8, group, carry[0:1, :])
        gel, _ = _gelu_and_grad(g_ref[...])
        yn_ref[...] = _rms_fwd(gel * hs_ref[...], prm[3:4]).astype(_MXU)

    row = pl.BlockSpec((CHUNK, LRU_WIDTH), lambda t: (t, 0))
    wspec = pl.BlockSpec((LRU_PAIRS, 128, 128), lambda t: (0, 0, 0))
    return pl.pallas_call(
        body, name="lru_fwd", grid=(N_CHUNKS,),
        in_specs=[row, row, wspec, wspec, pl.BlockSpec((8, LRU_WIDTH), lambda t: (0, 0))],
        out_specs=[row, row],
        out_shape=[jax.ShapeDtypeStruct((T_ROWS, LRU_WIDTH), F32), jax.ShapeDtypeStruct((T_ROWS, LRU_WIDTH), _MXU)],
        scratch_shapes=[pltpu.VMEM((8, LRU_WIDTH), F32), pltpu.VMEM((CHUNK, LRU_WIDTH), F32),
                        pltpu.VMEM((CHUNK, LRU_WIDTH), F32)],
        compiler_params=_cparams("arbitrary"),
    )(xr, gate, wa, wx, prm)


def _lru_bwd(dyn, dyn_block, gate, xr, hs, wa, wx, wa_t, wx_t, prm):
    def body(dyn_ref, g_ref, xr_ref, hs_ref, hsp_ref, wa_ref, wx_ref, wat_ref, wxt_ref, prm_ref,
             dg_ref, dxr_ref, dwa_ref, dwx_ref, dprm_ref, carry, a_s, d_s):
        step = pl.program_id(0)
        tile = N_CHUNKS - 1 - step

        @pl.when(step == 0)
        def _():
            carry[...] = jnp.zeros_like(carry)
            dwa_ref[...] = jnp.zeros_like(dwa_ref)
            dwx_ref[...] = jnp.zeros_like(dwx_ref)
            dprm_ref[...] = jnp.zeros_like(dprm_ref)

        prm = prm_ref[...]
        xr_t = xr_ref[...]
        r, i, a, s, sp = _lru_gates(xr_t, wa_ref, wx_ref, prm)
        hs_t = hs_ref[...]
        gel, dgel = _gelu_and_grad(g_ref[...])
        dy, dnw = _rms_bwd(gel * hs_t, prm[3:4], dyn_ref[...])
        dg_ref[...] = (dy * hs_t * dgel).astype(_MXU)
        a_s[...] = a
        d_s[...] = dy * gel
        rid = lax.broadcasted_iota(jnp.int32, (8, LRU_WIDTH), 0)

        def group(k, cr):
            off = pl.multiple_of((CHUNK // 8 - 1 - k) * 8, 8)
            a8 = a_s[pl.ds(off, 8), :]
            d8 = d_s[pl.ds(off, 8), :]
            out = jnp.zeros((8, LRU_WIDTH), F32)
            for r_ in reversed(range(8)):
                dht = d8[r_:r_ + 1] + cr
                out = jnp.where(rid == r_, dht, out)
                cr = a8[r_:r_ + 1] * dht
            d_s[pl.ds(off, 8), :] = out
            return cr

        carry[0:1, :] = lax.fori_loop(0, CHUNK // 8, group, carry[0:1, :])
        dht = d_s[...]
        before = hsp_ref[CHUNK - 8:CHUNK, :][7:8] * (tile > 0).astype(F32)
        first = lax.broadcasted_iota(jnp.int32, (CHUNK, 1), 0) == 0
        hprev = jnp.where(first, before, pltpu.roll(hs_t, 1, 0))
        da = dht * hprev
        ixr = i * xr_t
        ds = dht * ixr
        dlog_a = da * a - ds * (a * a) / s
        dr = dlog_a * ((-LRU_C) * sp)
        dsp = jnp.sum(dlog_a * ((-LRU_C) * r), axis=0, keepdims=True)
        dlam = dsp * (-_sigmoid(-prm[2:3]))
        di = dht * s * xr_t
        dpre_r = dr * r * (1.0 - r)
        dpre_i = di * i * (1.0 - i)
        dxr = dht * s * i
        parts = []
        for k in range(LRU_PAIRS):
            sl = slice(k * 128, (k + 1) * 128)
            parts.append(_dot(dpre_r[:, sl], wat_ref[k]) + _dot(dpre_i[:, sl], wxt_ref[k]))
            dwa_ref[k] += _dot(xr_t[:, sl], dpre_r[:, sl], TN)
            dwx_ref[k] += _dot(xr_t[:, sl], dpre_i[:, sl], TN)
        dxr_ref[...] = dxr + jnp.concatenate(parts, axis=1)
        dprm_ref[0:1, :] += jnp.sum(dpre_r, axis=0, keepdims=True)
        dprm_ref[1:2, :] += jnp.sum(dpre_i, axis=0, keepdims=True)
        dprm_ref[2:3, :] += dlam
        dprm_ref[3:4, :] += jnp.sum(dnw, axis=0, keepdims=True)

    rev = lambda blk=0: pl.BlockSpec((CHUNK, LRU_WIDTH), lambda s, blk=blk: (N_CHUNKS - 1 - s, blk))
    wspec = pl.BlockSpec((LRU_PAIRS, 128, 128), lambda s: (0, 0, 0))
    return pl.pallas_call(
        body, name="lru_bwd", grid=(N_CHUNKS,),
        in_specs=[rev(dyn_block), rev(), rev(), rev(),
                  pl.BlockSpec((CHUNK, LRU_WIDTH), lambda s: (jnp.maximum(N_CHUNKS - 2 - s, 0), 0)),
                  wspec, wspec, wspec, wspec, pl.BlockSpec((8, LRU_WIDTH), lambda s: (0, 0))],
        out_specs=[rev(), rev(), wspec, wspec, pl.BlockSpec((8, LRU_WIDTH), lambda s: (0, 0))],
        out_shape=[jax.ShapeDtypeStruct((T_ROWS, LRU_WIDTH), _MXU), jax.ShapeDtypeStruct((T_ROWS, LRU_WIDTH), F32),
                   jax.ShapeDtypeStruct((LRU_PAIRS, 128, 128), F32), jax.ShapeDtypeStruct((LRU_PAIRS, 128, 128), F32),
                   jax.ShapeDtypeStruct((8, LRU_WIDTH), F32)],
        scratch_shapes=[pltpu.VMEM((8, LRU_WIDTH), F32), pltpu.VMEM((CHUNK, LRU_WIDTH), F32),
                        pltpu.VMEM((CHUNK, LRU_WIDTH), F32)],
        compiler_params=_cparams("arbitrary"),
    )(dyn, gate, xr, hs, hs, wa, wx, wa_t, wx_t, prm)


SEC_NAMES = ("z", "xs", "bc", "dt", "g", "x")
SEC_WIDTH = {"z": 1024, "xs": 1024, "bc": 512, "dt": 128, "g": 1024, "x": 1024}


def _pair_blocks(w):
    w = w.reshape(LRU_PAIRS, 2, 64, 64)
    zero = jnp.zeros((LRU_PAIRS, 64, 64), w.dtype)
    top = jnp.concatenate([w[:, 0], zero], axis=2)
    bot = jnp.concatenate([zero, w[:, 1]], axis=2)
    return jnp.concatenate([top, bot], axis=1)


def _unpair_blocks(wp):
    return jnp.stack([wp[:, :64, :64], wp[:, 64:, 64:]], axis=1).reshape(16, 64, 64)


def _pad_lanes(v, width=128):
    return jnp.pad(v, ((0, 0), (0, width - v.shape[1])))


class _Resident:
    def __init__(self, w_out, w_gate, w_up, w_down):
        self._w_out, self._ffn = w_out, (w_gate, w_up, w_down)

    def w_out(self, after):
        return self._w_out

    def ffn(self, after):
        return self._ffn

    def grads_ready(self, names, g, g_mxu):
        return jnp.zeros((1, 1), F32)

    def small_ready(self, g, loss):
        return jnp.zeros((1, 1), F32)

    def small_middle(self, after):
        return jnp.zeros((1, 1), F32)


def _local_step(x, target, meta, p, late):
    g, g_mxu = {}, {}
    ex = _head_expander()
    h0 = _embed(x, meta)
    u1 = _rmsnorm(h0, p["norm1_w"], name="norm1")
    proj = {}
    for s in SEC_NAMES:
        wdt = SEC_WIDTH[s]
        proj[s] = _mm([(u1, 0, p["w_in_" + s], 0, D_MODEL)], T_ROWS, wdt, tm=544, tn=min(wdt, 512), mode="nt",
                      out_dtype=F32, name="proj_" + s)
    ssd_prm = jnp.concatenate([_pad_lanes(p["ssd_dt_bias"]), _pad_lanes(p["ssd_a_log"]), _pad_lanes(p["ssd_d"]),
                               jnp.zeros((5, 128), F32)], axis=0)
    xs_act = _conv_fwd(proj["xs"], p["ssd_conv_w"][:, :SSD_WIDTH], p["ssd_conv_b"][:, :SSD_WIDTH], silu=True,
                       name="ssd_conv_xs")
    bc_act = _conv_fwd(proj["bc"], p["ssd_conv_w"][:, SSD_WIDTH:], p["ssd_conv_b"][:, SSD_WIDTH:], silu=True,
                       name="ssd_conv_bc")
    y_pre, y_ssd, prev = _ssd_fwd(xs_act, bc_act, proj["dt"], proj["z"], ssd_prm, p["ssd_norm_w"], ex)
    xr = _conv_fwd(proj["x"], p["lru_conv_w"], p["lru_conv_b"], silu=False, name="lru_conv")
    wa_p, wx_p = _pair_blocks(p["lru_wa"]), _pair_blocks(p["lru_wx"])
    lru_prm = jnp.concatenate([p["lru_ba"], p["lru_bx"], p["lru_lambda"], p["lru_norm_w"],
                               jnp.zeros((4, LRU_WIDTH), F32)], axis=0)
    hs, y_lru = _lru_fwd(xr, proj["g"], wa_p.astype(_MXU), wx_p.astype(_MXU), lru_prm)
    ycat = jnp.concatenate([y_ssd, y_lru], axis=1)
    w_out = late.w_out(ycat)
    h1 = _mm([(ycat, 0, w_out, 0, 2 * D_MODEL)], T_ROWS, D_MODEL, tm=544, tn=512, mode="nn", out_dtype=F32,
             name="out_proj", residual=h0)
    u2 = _rmsnorm(h1, p["norm2_w"], name="norm2")
    w_gate, w_up, w_down = late.ffn(u2)
    gp, up, act = _ffn_up(u2, w_gate, w_up)
    h2 = _mm([(act, 0, w_down, 0, D_FF)], T_ROWS, D_MODEL, tm=544, tn=512, mode="nn", out_dtype=F32,
             name="ffn_down", residual=h1)
    loss, dh2, dh2b, g["final_norm_w"] = _loss_head(h2, target, p["final_norm_w"])
    dgp, dup = _ffn_bwd_act(dh2b, w_down, gp, up)
    g["w_down"], g_mxu["w_down"] = _mm([(act, 0, dh2b, 0, T_ROWS)], D_FF, D_MODEL, tm=1408, tn=512, mode="tn",
                                       out_dtype=F32, name="dw_down", also_mxu=True)
    dh1, dh1b, g["norm2_w"] = _mm_norm_bwd([(dgp, w_gate, D_FF), (dup, w_up, D_FF)], h1, p["norm2_w"], dh2,
                                           name="ffn_bwd_in")
    g["w_gate"], g_mxu["w_gate"] = _mm([(dgp, 0, u2, 0, T_ROWS)], D_FF, D_MODEL, tm=1408, tn=512, mode="tn",
                                       out_dtype=F32, name="dw_gate", also_mxu=True)
    g["w_up"], g_mxu["w_up"] = _mm([(dup, 0, u2, 0, T_ROWS)], D_FF, D_MODEL, tm=1408, tn=512, mode="tn",
                                   out_dtype=F32, name="dw_up", also_mxu=True)
    sent = late.grads_ready(("w_down", "w_gate", "w_up"), g, g_mxu)
    g["w_out"], g_mxu["w_out"] = _mm([(ycat, 0, dh1b, 0, T_ROWS)], 2 * D_MODEL, D_MODEL, tm=512, tn=512, mode="tn",
                                     out_dtype=F32, name="dw_out", also_mxu=True, behind=(sent,))
    sent = late.grads_ready(("w_out",), g, g_mxu)
    dycat = _mm([(dh1b, 0, w_out, 0, D_MODEL)], T_ROWS, 2 * D_MODEL, tm=544, tn=512, mode="nt", out_dtype=F32,
                name="out_proj_bwd", behind=(sent,))
    dgate, dxr, dwa_p, dwx_p, dlru_prm = _lru_bwd(dycat, 1, proj["g"], xr, hs, wa_p.astype(_MXU), wx_p.astype(_MXU),
                                                  jnp.swapaxes(wa_p, 1, 2).astype(_MXU),
                                                  jnp.swapaxes(wx_p, 1, 2).astype(_MXU), lru_prm)
    g["lru_wa"], g["lru_wx"] = _unpair_blocks(dwa_p), _unpair_blocks(dwx_p)
    g["lru_ba"], g["lru_bx"], g["lru_lambda"], g["lru_norm_w"] = (dlru_prm[k:k + 1] for k in range(4))
    dx_lru, g["lru_conv_w"], g["lru_conv_b"] = _conv_bwd(dxr, proj["x"], p["lru_conv_w"], p["lru_conv_b"], silu=False,
                                                         name="lru_conv_bwd")
    dz, dxs_act, dbc_act, ddt, dssd_prm, g["ssd_norm_w"] = _ssd_bwd(dycat, 0, proj["z"], y_pre, xs_act, bc_act,
                                                                    proj["dt"], prev, ssd_prm, p["ssd_norm_w"], ex)
    g["ssd_dt_bias"], g["ssd_a_log"], g["ssd_d"] = (dssd_prm[k:k + 1, :SSD_HEADS] for k in range(3))
    dxs, dcw_xs, dcb_xs = _conv_bwd(dxs_act, proj["xs"], p["ssd_conv_w"][:, :SSD_WIDTH],
                                    p["ssd_conv_b"][:, :SSD_WIDTH], silu=True, name="ssd_conv_xs_bwd")
    dbc, dcw_bc, dcb_bc = _conv_bwd(dbc_act, proj["bc"], p["ssd_conv_w"][:, SSD_WIDTH:],
                                    p["ssd_conv_b"][:, SSD_WIDTH:], silu=True, name="ssd_conv_bc_bwd")
    g["ssd_conv_w"] = jnp.concatenate([dcw_xs, dcw_bc], axis=1)
    g["ssd_conv_b"] = jnp.concatenate([dcb_xs, dcb_bc], axis=1)
    dproj = {"z": dz, "xs": dxs, "bc": dbc, "dt": ddt, "g": dgate, "x": dx_lru}
    dh0, _, g["norm1_w"] = _mm_norm_bwd([(dproj[s], p["w_in_" + s], SEC_WIDTH[s]) for s in SEC_NAMES], h0,
                                        p["norm1_w"], dh1, name="in_proj_bwd")
    g["meta_tokens"] = dh0[PAD_ROWS:X_ROW0]
    sent = late.small_ready(g, loss)
    for s in SEC_NAMES:
        wdt = SEC_WIDTH[s]
        g["w_in_" + s], g_mxu["w_in_" + s] = _mm([(dproj[s], 0, u1, 0, T_ROWS)], wdt, D_MODEL, tm=min(wdt, 512),
                                                 tn=512, mode="tn", out_dtype=F32, name="dw_in_" + s, also_mxu=True,
                                                 behind=(sent,))
        if s == "bc":
            sent = late.small_middle(g["w_in_bc"])
    return loss, dh0[X_ROW0:], g, g_mxu


MESH = pl.DeviceIdType.MESH
ANY = pl.BlockSpec(memory_space=pl.ANY)


def _my_place():
    return lax.axis_index("x"), lax.axis_index("y"), lax.axis_index("c")


def _other_chips(x, y):
    return [(1 - x, y), (x, 1 - y), (1 - x, 1 - y)]


def _gather_first(big, small):
    half = big.shape[1] // 2

    def body(big_ref, small_ref, big4, small4, send_sems, recv_sems, local_sems):
        x, y, c = _my_place()
        me = 2 * x + y
        sibling = (x, y, 1 - c)
        peers = _other_chips(x, y)
        mine = pl.ds(pl.multiple_of(c * half, 128), half)
        theirs = pl.ds(pl.multiple_of((1 - c) * half, 128), half)

        def copy(k, src, dst, dev):
            return pltpu.make_async_remote_copy(src_ref=src, dst_ref=dst, send_sem=send_sems.at[k],
                                                recv_sem=recv_sems.at[k], device_id=dev, device_id_type=MESH)

        local = [pltpu.make_async_copy(big_ref, big4.at[me], local_sems.at[0]),
                 pltpu.make_async_copy(small_ref, small4.at[me], local_sems.at[1])]
        for cp in local:
            cp.start()
        first = []
        for j, (px, py) in enumerate(peers):
            first.append(copy(j, big_ref.at[:, mine], big4.at[me, :, mine], (px, py, c)))
            first.append(copy(3 + j, small_ref, small4.at[me], (px, py, c)))
        for cp in first:
            cp.start()
        passed = []
        for j, (px, py) in enumerate(peers):
            slot = 2 * px + py
            copy(j, big_ref.at[:, mine], big4.at[slot, :, mine], (px, py, c)).wait_recv()
            passed.append(copy(6 + j, big4.at[slot, :, mine], big4.at[slot, :, mine], sibling))
            passed[-1].start()
        for j, (px, py) in enumerate(peers):
            slot = 2 * px + py
            copy(6 + j, big4.at[slot, :, theirs], big4.at[slot, :, theirs], sibling).wait_recv()
            copy(3 + j, small_ref, small4.at[slot], (px, py, c)).wait_recv()
        for cp in first + passed:
            cp.wait_send()
        for cp in local:
            cp.wait()

    return pl.pallas_call(
        body, name="gather_first", in_specs=[ANY, ANY], out_specs=[ANY, ANY],
        out_shape=[jax.ShapeDtypeStruct((N_SHARDS,) + big.shape, big.dtype),
                   jax.ShapeDtypeStruct((N_SHARDS,) + small.shape, small.dtype)],
        scratch_shapes=[pltpu.SemaphoreType.DMA((9,)), pltpu.SemaphoreType.DMA((9,)), pltpu.SemaphoreType.DMA((2,))],
    )(big, small)


HBM_SPEC = pl.BlockSpec(memory_space=pltpu.HBM)
SEM_SPEC = pl.BlockSpec(memory_space=pltpu.SEMAPHORE)
SPLIT_EFFECT = pltpu.SideEffectType.DATAFLOW_SIDE_EFFECTING


def _gather_plan(bufs, x, y, c, incoming):
    plan = []
    for buf in bufs:
        for (px, py) in _other_chips(x, y):
            slot = 2 * px + py if incoming else 2 * x + y
            plan.append((buf.at[2 * x + y], buf.at[slot], (px, py, c)))
    return plan


def _scatter_plan(bufs, x, y, c, incoming):
    n = len(bufs) // 2
    plan = []
    for k in range(n):
        for j, (px, py) in enumerate(_other_chips(x, y)):
            plan.append((bufs[k].at[2 * px + py], bufs[n + k].at[j], (px, py, c)))
    return plan


def _split_start(bufs, plan, n_copies, after, *, name):
    n = len(bufs)
    extra = [] if after is None else [after]

    def body(*refs):
        ins = refs[:n]
        send_sems, recv_sems = refs[n + len(extra)], refs[n + len(extra) + 1]
        token = refs[-1]
        x, y, c = _my_place()
        for i, (src, dst, dev) in enumerate(plan(ins, x, y, c, False)):
            pltpu.make_async_remote_copy(src_ref=src, dst_ref=dst, send_sem=send_sems.at[i], recv_sem=recv_sems.at[i],
                                         device_id=dev, device_id_type=MESH).start()
        token[...] = jnp.zeros_like(token)

    outs = pl.pallas_call(
        body, name=name,
        out_shape=(pltpu.SemaphoreType.DMA((n_copies,)), pltpu.SemaphoreType.DMA((n_copies,)),
                   *[pltpu.HBM(b.shape, b.dtype) for b in bufs], jax.ShapeDtypeStruct((8, 128), F32)),
        in_specs=[HBM_SPEC] * n + [ANY] * len(extra),
        out_specs=(SEM_SPEC, SEM_SPEC, *[HBM_SPEC] * n, pl.BlockSpec(memory_space=pltpu.VMEM)),
        input_output_aliases={k: 2 + k for k in range(n)},
        compiler_params=pltpu.CompilerParams(has_side_effects=SPLIT_EFFECT),
    )(*[pltpu.with_memory_space_constraint(b, pltpu.HBM) for b in bufs], *extra)
    return outs[0], outs[1], list(outs[2:2 + n]), outs[-1]


def _split_wait(bufs, send_sems, recv_sems, plan, after, *, name):
    n = len(bufs)

    def body(*refs):
        ins = refs[:n]
        send_sems_ref, recv_sems_ref = refs[n], refs[n + 1]
        x, y, c = _my_place()
        for i, (src, dst, dev) in enumerate(plan(ins, x, y, c, True)):
            cp = pltpu.make_async_remote_copy(src_ref=src, dst_ref=dst, send_sem=send_sems_ref.at[i],
                                              recv_sem=recv_sems_ref.at[i], device_id=dev, device_id_type=MESH)
            cp.wait_send()
            cp.wait_recv()

    outs = pl.pallas_call(
        body, name=name, out_shape=tuple(pltpu.HBM(b.shape, b.dtype) for b in bufs),
        in_specs=[HBM_SPEC] * n + [SEM_SPEC, SEM_SPEC, ANY], out_specs=tuple([HBM_SPEC] * n),
        input_output_aliases={k: k for k in range(n)},
        compiler_params=pltpu.CompilerParams(has_side_effects=SPLIT_EFFECT),
    )(*bufs, send_sems, recv_sems, after)
    return list(outs)


def _fill_own_slot(shard, me_arr, *, name):
    r, c = shard.shape
    tile, steps, imap = _elementwise_tile(r, c)

    def body(me_ref, x_ref, o_ref):
        o_ref[0] = x_ref[...].astype(_MXU)

    return pl.pallas_call(
        body, name=name,
        grid_spec=pltpu.PrefetchScalarGridSpec(
            num_scalar_prefetch=1, grid=(steps,),
            in_specs=[pl.BlockSpec(tile, lambda i, me: imap(i))],
            out_specs=pl.BlockSpec((1,) + tile, lambda i, me: (me[0],) + imap(i))),
        out_shape=jax.ShapeDtypeStruct((N_SHARDS, r, c), _MXU),
        compiler_params=_cparams("parallel"),
    )(me_arr, shard)


def _swap_with_sibling(parts, *, name):
    n = len(parts)

    def body(*refs):
        ins, outs = refs[:n], refs[n:2 * n]
        send_sems, recv_sems = refs[2 * n:]
        x, y, c = _my_place()
        copies = [pltpu.make_async_remote_copy(
            src_ref=ins[k], dst_ref=outs[k], send_sem=send_sems.at[k], recv_sem=recv_sems.at[k],
            device_id=(x, y, 1 - c), device_id_type=MESH) for k in range(n)]
        for cp in copies:
            cp.start()
        for cp in copies:
            cp.wait()

    return pl.pallas_call(
        body, name=name, in_specs=[ANY] * n, out_specs=[ANY] * n,
        out_shape=[jax.ShapeDtypeStruct(a.shape, a.dtype) for a in parts],
        scratch_shapes=[pltpu.SemaphoreType.DMA((n,)), pltpu.SemaphoreType.DMA((n,))],
    )(*parts)


def _other_devices(x, y, c):
    out = []
    for mask in range(1, N_DEV):
        px, py, pc = x ^ (mask >> 2 & 1), y ^ (mask >> 1 & 1), c ^ (mask & 1)
        out.append(((px, py, pc), 4 * px + 2 * py + pc))
    return out


def _pieces_plan(bufs, x, y, c, incoming):
    pack, land = bufs
    me = 4 * x + 2 * y + c
    return [(pack.at[num], land.at[num if incoming else me], dev) for dev, num in _other_devices(x, y, c)]


def _spread_plan(bufs, x, y, c, incoming):
    piece, land = bufs
    me = 4 * x + 2 * y + c
    return [(piece, land.at[num if incoming else me], dev) for dev, num in _other_devices(x, y, c)]


def _sum_pieces(pack, land, dev_arr, *, name):
    def body(dev_ref, pack_ref, land_ref, o_ref):
        dev = dev_ref[0]
        own = pack_ref[dev]
        acc = None
        for d in range(N_DEV):
            term = jnp.where(dev == d, own, land_ref[d])
            acc = term if acc is None else acc + term
        o_ref[...] = acc

    vmem = pl.BlockSpec(memory_space=pltpu.VMEM)
    return pl.pallas_call(
        body, name=name, in_specs=[pl.BlockSpec(memory_space=pltpu.SMEM), vmem, vmem], out_specs=vmem,
        out_shape=jax.ShapeDtypeStruct(pack.shape[1:], F32),
    )(dev_arr, pack, land)


def _join_pieces(piece, land, dev_arr, *, name):
    def body(dev_ref, piece_ref, land_ref, o_ref):
        dev = dev_ref[0]
        for d in range(N_DEV):
            o_ref[d] = jnp.where(dev == d, piece_ref[...], land_ref[d])

    vmem = pl.BlockSpec(memory_space=pltpu.VMEM)
    return pl.pallas_call(
        body, name=name, in_specs=[pl.BlockSpec(memory_space=pltpu.SMEM), vmem, vmem], out_specs=vmem,
        out_shape=jax.ShapeDtypeStruct(land.shape, F32),
    )(dev_arr, piece, land)


def _adamw_native(ws, gs, ms, vs):
    n = len(ws)

    def body(*refs):
        for k in range(n):
            w_ref, g_ref, m_ref, v_ref = (refs[j * n + k] for j in range(4))
            delta, m_new, v_new = _adamw_math(w_ref[...], g_ref[...], m_ref[...], v_ref[...])
            refs[4 * n + k][...] = delta
            refs[5 * n + k][...] = m_new
            refs[6 * n + k][...] = v_new

    vmem = pl.BlockSpec(memory_space=pltpu.VMEM)
    shapes = [jax.ShapeDtypeStruct(a.shape, F32) for a in ws]
    outs = pl.pallas_call(
        body, name="adamw_small", in_specs=[vmem] * (4 * n), out_specs=[vmem] * (3 * n), out_shape=shapes * 3,
        compiler_params=pltpu.CompilerParams(vmem_limit_bytes=VMEM_LIMIT_BYTES),
    )(*ws, *gs, *ms, *vs)
    return outs[:n], outs[n:2 * n], outs[2 * n:]


def _elementwise_tile(rows, cols, limit=256):
    for t in range(limit, 15, -16):
        if rows % t == 0:
            return (t, cols), rows // t, lambda i: (i, 0)
    assert cols % limit == 0
    return (rows, limit), cols // limit, lambda i: (0, i)


def _partial_sum(g4, land, me_arr, *, name):
    _, r, c = g4.shape
    tile, steps, imap = _elementwise_tile(r, c)

    def body(me_ref, own_ref, land_ref, o_ref):
        acc = own_ref[0]
        for j in range(3):
            acc = acc + land_ref[j].astype(F32)
        o_ref[...] = acc

    return pl.pallas_call(
        body, name=name,
        grid_spec=pltpu.PrefetchScalarGridSpec(
            num_scalar_prefetch=1, grid=(steps,),
            in_specs=[pl.BlockSpec((1,) + tile, lambda i, me: (me[0],) + imap(i)),
                      pl.BlockSpec((3,) + tile, lambda i, me: (0,) + imap(i))],
            out_specs=pl.BlockSpec(tile, lambda i, me: imap(i))),
        out_shape=jax.ShapeDtypeStruct((r, c), F32),
        compiler_params=_cparams("parallel"),
    )(me_arr, g4, land)


def _adamw_math(w, g, m, v):
    m = ADAM_B1 * m + (1.0 - ADAM_B1) * g
    v = ADAM_B2 * v + (1.0 - ADAM_B2) * (g * g)
    m_hat = m / (1.0 - ADAM_B1 ** ADAM_STEP)
    v_hat = v / (1.0 - ADAM_B2 ** ADAM_STEP)
    delta = -ADAM_LR * (m_hat / (jnp.sqrt(v_hat) + ADAM_EPS) + ADAM_WD * w)
    return delta, m, v


def _adamw(w, grad_parts, m, v, *, name):
    r, c = w.shape
    tile_shape, steps, imap = _elementwise_tile(r, c)
    n = len(grad_parts)

    def body(*refs):
        w_ref, m_ref, v_ref = refs[:3]
        g_refs = refs[3:3 + n]
        g_out, d_out, m_out, v_out = refs[3 + n:]
        g = g_refs[0][...]
        for k in range(1, n):
            g = g + g_refs[k][...]
        delta, m_new, v_new = _adamw_math(w_ref[...], g, m_ref[...], v_ref[...])
        g_out[...] = g
        d_out[...] = delta
        m_out[...] = m_new
        v_out[...] = v_new

    tile = pl.BlockSpec(tile_shape, imap)
    return pl.pallas_call(
        body, name=name, grid=(steps,), in_specs=[tile] * (3 + n), out_specs=[tile] * 4,
        out_shape=[jax.ShapeDtypeStruct((r, c), F32)] * 4,
        compiler_params=_cparams("parallel"),
    )(w, m, v, *grad_parts)


WEIGHT_NAMES = ("meta_tokens", "norm1_w", "w_in", "ssd_conv_w", "ssd_conv_b", "ssd_dt_bias", "ssd_a_log", "ssd_d",
                "ssd_norm_w", "lru_conv_w", "lru_conv_b", "lru_wa", "lru_ba", "lru_wx", "lru_bx", "lru_lambda",
                "lru_norm_w", "w_out", "norm2_w", "w_gate", "w_up", "w_down", "final_norm_w")
BIG = ("w_in", "w_out", "w_gate", "w_up", "w_down")
FFN = ("w_gate", "w_up", "w_down")
LATE = ("w_out",) + FFN
SMALL_SHARDED = {"meta_tokens": (N_META, D_MODEL), "ssd_conv_w": (CONV_K, 1536), "lru_conv_w": (CONV_K, LRU_WIDTH)}
SMALL = tuple(n for n in WEIGHT_NAMES if n not in BIG)
PACK_COLS = 1024


def _pack(arrays, row_multiple):
    flat = jnp.concatenate([a.reshape(-1) for a in arrays])
    rows = -(-flat.shape[0] // (row_multiple * PACK_COLS)) * row_multiple
    return jnp.pad(flat, (0, rows * PACK_COLS - flat.shape[0])).reshape(rows, PACK_COLS)


def _unpack(pack, shapes):
    flat = pack.reshape(-1)
    out, off = [], 0
    for s in shapes:
        size = math.prod(s)
        out.append(flat[off:off + size].reshape(s))
        off += size
    return out


def _unshard_cols(g4):
    return jnp.swapaxes(g4, 0, 1).reshape(g4.shape[1], -1)


COL_SHARDED = ("w_in", "w_gate", "w_up")
IN_ROWS = {"z": (0, 1024), "xs": (1024, 2048), "bc": (2048, 2560), "dt": (2560, 2576), "g": (2576, 3600),
           "x": (3600, IN_COLS)}


def _rows_view(name, block):
    return jnp.swapaxes(block[0], 0, 1) if name in COL_SHARDED else block[0]


def _param_view(name, rows):
    return (jnp.swapaxes(rows, 0, 1) if name in COL_SHARDED else rows)[None]


def kernel(x, meta_tokens, norm1_w, w_in, ssd_conv_w, ssd_conv_b, ssd_dt_bias, ssd_a_log, ssd_d, ssd_norm_w, lru_conv_w, lru_conv_b, lru_wa, lru_ba, lru_wx, lru_bx, lru_lambda, lru_norm_w, w_out, norm2_w, w_gate, w_up, w_down, final_norm_w, loss_target, m_meta_tokens, m_norm1_w, m_w_in, m_ssd_conv_w, m_ssd_conv_b, m_ssd_dt_bias, m_ssd_a_log, m_ssd_d, m_ssd_norm_w, m_lru_conv_w, m_lru_conv_b, m_lru_wa, m_lru_ba, m_lru_wx, m_lru_bx, m_lru_lambda, m_lru_norm_w, m_w_out, m_norm2_w, m_w_gate, m_w_up, m_w_down, m_final_norm_w, v_meta_tokens, v_norm1_w, v_w_in, v_ssd_conv_w, v_ssd_conv_b, v_ssd_dt_bias, v_ssd_a_log, v_ssd_d, v_ssd_norm_w, v_lru_conv_w, v_lru_conv_b, v_lru_wa, v_lru_ba, v_lru_wx, v_lru_bx, v_lru_lambda, v_lru_norm_w, v_w_out, v_norm2_w, v_w_gate, v_w_up, v_w_down, v_final_norm_w):
    w = dict(zip(WEIGHT_NAMES, (meta_tokens, norm1_w, w_in, ssd_conv_w, ssd_conv_b, ssd_dt_bias, ssd_a_log, ssd_d, ssd_norm_w, lru_conv_w, lru_conv_b, lru_wa, lru_ba, lru_wx, lru_bx, lru_lambda, lru_norm_w, w_out, norm2_w, w_gate, w_up, w_down, final_norm_w)))
    m = dict(zip(WEIGHT_NAMES, (m_meta_tokens, m_norm1_w, m_w_in, m_ssd_conv_w, m_ssd_conv_b, m_ssd_dt_bias, m_ssd_a_log, m_ssd_d, m_ssd_norm_w, m_lru_conv_w, m_lru_conv_b, m_lru_wa, m_lru_ba, m_lru_wx, m_lru_bx, m_lru_lambda, m_lru_norm_w, m_w_out, m_norm2_w, m_w_gate, m_w_up, m_w_down, m_final_norm_w)))
    v = dict(zip(WEIGHT_NAMES, (v_meta_tokens, v_norm1_w, v_w_in, v_ssd_conv_w, v_ssd_conv_b, v_ssd_dt_bias, v_ssd_a_log, v_ssd_d, v_ssd_norm_w, v_lru_conv_w, v_lru_conv_b, v_lru_wa, v_lru_ba, v_lru_wx, v_lru_bx, v_lru_lambda, v_lru_norm_w, v_w_out, v_norm2_w, v_w_gate, v_w_up, v_w_down, v_final_norm_w)))
    me = 2 * lax.axis_index("x") + lax.axis_index("y")

    big2d = {n: _rows_view(n, w[n]) for n in BIG}
    small_local = jnp.concatenate([w["meta_tokens"].reshape(-1), w["ssd_conv_w"].reshape(-1),
                                   w["lru_conv_w"].reshape(-1)])[None]
    me_arr = me.astype(jnp.int32).reshape(1)
    dev_arr = (2 * me + lax.axis_index("c")).astype(jnp.int32).reshape(1)
    w_in4, small4 = _gather_first(big2d["w_in"].astype(_MXU), small_local)
    w_in_full = w_in4.reshape(-1, D_MODEL)
    sm = small4[:, 0]
    meta_full = _unshard_cols(sm[:, :4096].reshape(N_SHARDS, N_META, 256))
    ssd_conv_w_full = _unshard_cols(sm[:, 4096:5632].reshape(N_SHARDS, CONV_K, 384))
    lru_conv_w_full = _unshard_cols(sm[:, 5632:].reshape(N_SHARDS, CONV_K, 256))
    slots = {n: _fill_own_slot(big2d[n], me_arr, name="own_slot_" + n) for n in LATE}
    out_send, out_recv, out_bufs, tok_a = _split_start([slots["w_out"]], _gather_plan, 3, small4,
                                                       name="gather_w_out_start")
    ffn_send, ffn_recv, ffn_bufs, tok_b = _split_start([slots[n] for n in FFN], _gather_plan, 9, tok_a,
                                                       name="gather_ffn_start")

    p = {"w_in_" + s: w_in_full[lo:hi] for s, (lo, hi) in IN_ROWS.items()}
    p["w_in_dt"] = jnp.pad(p["w_in_dt"], ((0, SEC_WIDTH["dt"] - SSD_HEADS), (0, 0)))
    p.update({"ssd_conv_w": ssd_conv_w_full, "lru_conv_w": lru_conv_w_full,
              "lru_wa": w["lru_wa"][0], "lru_wx": w["lru_wx"][0], "final_norm_w": w["final_norm_w"][None]})
    for n in ("norm1_w", "ssd_conv_b", "ssd_dt_bias", "ssd_a_log", "ssd_d", "ssd_norm_w", "lru_conv_b", "lru_ba",
              "lru_bx", "lru_lambda", "lru_norm_w", "norm2_w"):
        p[n] = w[n]
    p["norm1_w"] = p["norm1_w"] + tok_b[:1, :1]

    class Late:
        def __init__(self):
            self.pending = []

        def w_out(self, after):
            (buf,) = _split_wait(out_bufs, out_send, out_recv, _gather_plan, after, name="gather_w_out_wait")
            return buf.reshape(-1, D_MODEL)

        def ffn(self, after):
            bufs = _split_wait(ffn_bufs, ffn_send, ffn_recv, _gather_plan, after, name="gather_ffn_wait")
            return tuple(b.reshape(-1, D_MODEL) for b in bufs)

        def grads_ready(self, names, g, g_mxu):
            srcs = [g_mxu[n].reshape(N_SHARDS, -1, D_MODEL) for n in names]
            lands = [lax.empty((3,) + s.shape[1:], _MXU) for s in srcs]
            tag = "_".join(names)
            send, recv, bufs, tok = _split_start(srcs + lands, _scatter_plan, 3 * len(names), g[names[-1]],
                                                 name="scatter_" + tag + "_start")
            self.pending.append((names, send, recv, bufs, tag))
            self.in_flight = bufs[0]
            return tok[:1, :1]

        def landed(self, after, which):
            land = {}
            for names, send, recv, bufs, tag in self.pending:
                if names[0] in which:
                    bufs = _split_wait(bufs, send, recv, _scatter_plan, after, name="scatter_" + tag + "_wait")
                    land.update(zip(names, bufs[len(names):]))
            return land

        def small_ready(self, g, loss):
            pack = _pack([g[n] for n in SMALL] + [loss[0, :1]], 8 * N_DEV)
            pack = pack.reshape(N_DEV, -1, PACK_COLS)
            self.small = _split_start([pack, lax.empty(pack.shape, F32)], _pieces_plan, N_DEV - 1, loss,
                                      name="small_pieces_start")
            return self.small[3]

        def small_middle(self, after):
            send, recv, bufs, _ = self.small
            pack, land = _split_wait(bufs, send, recv, _pieces_plan, after, name="small_pieces_wait")
            piece = _sum_pieces(pack, land, dev_arr, name="small_pieces_sum")
            self.small = _split_start([piece, lax.empty(pack.shape, F32)], _spread_plan, N_DEV - 1, None,
                                      name="small_spread_start")
            return self.small[3]

        def small_sum(self, after):
            send, recv, bufs, _ = self.small
            piece, land = _split_wait(bufs, send, recv, _spread_plan, after, name="small_spread_wait")
            return _join_pieces(piece, land, dev_arr, name="small_join")

    late = Late()

    loss, grad_x, g, g_mxu = _local_step(x[0], loss_target[0], meta_full, p, late)

    g["w_in"] = jnp.concatenate([g["w_in_" + s][:hi - lo] for s, (lo, hi) in IN_ROWS.items()], axis=0)
    g_mxu["w_in"] = jnp.concatenate([g_mxu["w_in_" + s][:hi - lo] for s, (lo, hi) in IN_ROWS.items()], axis=0)
    g4 = {n: g[n].reshape(N_SHARDS, -1, D_MODEL) for n in BIG}
    late.grads_ready(("w_in",), g, g_mxu)
    land = late.landed(late.in_flight, LATE)
    part = {n: _partial_sum(g4[n], land[n], me_arr, name="partial_" + n) for n in LATE}
    sib = dict(zip(LATE, _swap_with_sibling([part[n] for n in LATE], name="swap_late")))

    small_full_shape = {n: (SMALL_SHARDED[n] if n in SMALL_SHARDED else w[n].shape) for n in SMALL}
    red_list = _unpack(late.small_sum(sib["w_out"]), [small_full_shape[n] for n in SMALL] + [(1,)])
    loss_total = red_list[-1][0]
    g_small = {}
    for n, arr in zip(SMALL, red_list[:-1]):
        if n in SMALL_SHARDED:
            cols = SMALL_SHARDED[n][1] // N_SHARDS
            arr = lax.dynamic_slice_in_dim(arr, me * cols, cols, axis=1)
        g_small[n] = arr.reshape(w[n].shape)

    grad, delta, new_m, new_v = {}, {}, {}, {}

    def update_big(n):
        outs = _adamw(big2d[n], [part[n], sib[n]], _rows_view(n, m[n]), _rows_view(n, v[n]), name="adamw_" + n)
        grad[n], delta[n], new_m[n], new_v[n] = (_param_view(n, o) for o in outs)
        return outs[0]

    two_d = lambda a: a.reshape(1, -1) if a.ndim == 1 else a
    deltas, new_ms, new_vs = _adamw_native(*[[two_d(d[n]) for n in SMALL] for d in (w, g_small, m, v)])
    for n, dn, mn, vn in zip(SMALL, deltas, new_ms, new_vs):
        grad[n], delta[n], new_m[n], new_v[n] = (g_small[n], dn.reshape(w[n].shape), mn.reshape(w[n].shape),
                                                 vn.reshape(w[n].shape))
    for n in LATE:
        last = update_big(n)
    land.update(late.landed(last, ("w_in",)))
    part["w_in"] = _partial_sum(g4["w_in"], land["w_in"], me_arr, name="partial_w_in")
    (sib["w_in"],) = _swap_with_sibling([part["w_in"]], name="swap_w_in")
    update_big("w_in")

    return (loss_total, grad_x[None], *[grad[n] for n in WEIGHT_NAMES], *[delta[n] for n in WEIGHT_NAMES],
            *[new_m[n] for n in WEIGHT_NAMES], *[new_v[n] for n in WEIGHT_NAMES])
```

```python
import functools
import math

import jax
import jax.numpy as jnp
from jax import lax
from jax.experimental import pallas as pl
from jax.experimental.pallas import tpu as pltpu

F32 = jnp.float32
_MXU = jnp.bfloat16

D_MODEL = 1024
SEQ = 2048
N_META = 16
CHUNK = 128
T_ROWS = 2176
N_CHUNKS = T_ROWS // CHUNK
PAD_ROWS = T_ROWS - SEQ - N_META
X_ROW0 = PAD_ROWS + N_META
SSD_HEADS = 16
SSD_HEAD_DIM = 64
SSD_STATE = 128
SSD_GROUPS = 2
SSD_HPG = SSD_HEADS // SSD_GROUPS
SSD_WIDTH = 1024
LRU_WIDTH = 1024
LRU_C = 8.0
D_FF = 2816
EPS = 1e-6
IN_COLS = 4624
N_SHARDS = 4
N_DEV = 8

ADAM_LR = 0.001
ADAM_B1 = 0.9
ADAM_B2 = 0.999
ADAM_EPS = 1e-08
ADAM_WD = 0.01
ADAM_STEP = 10

VMEM_LIMIT_BYTES = 56 * 1024 * 1024

NN = (((1,), (0,)), ((), ()))
NT = (((1,), (1,)), ((), ()))
TN = (((0,), (0,)), ((), ()))


def _cparams(*sem):
    return pltpu.CompilerParams(dimension_semantics=sem, vmem_limit_bytes=VMEM_LIMIT_BYTES)


def _dot(a, b, dims=NN):
    return lax.dot_general(a.astype(_MXU), b.astype(_MXU), dims, preferred_element_type=F32)


def _dot_onehot(a, b, dims=NN, *, data=0):
    ops = [a, b]
    mask = ops[1 - data].astype(jnp.bfloat16)
    rest = ops[data]
    acc = None
    for _ in range(3):
        piece = rest.astype(jnp.bfloat16)
        ops[data], ops[1 - data] = piece, mask
        d = lax.dot_general(ops[0], ops[1], dims, preferred_element_type=F32)
        acc = d if acc is None else acc + d
        rest = rest - piece.astype(F32)
    return acc


def _sigmoid(x):
    return 0.5 * (1.0 + jnp.tanh(0.5 * x))


def _softplus(x):
    return jnp.maximum(x, 0.0) + jnp.log(1.0 + jnp.exp(-jnp.abs(x)))


def _silu(x):
    return x * _sigmoid(x)


def _silu_grad(x):
    s = _sigmoid(x)
    return s * (1.0 + x * (1.0 - s))


_GELU_C = math.sqrt(2.0 / math.pi)


def _gelu_and_grad(x):
    inner = _GELU_C * (x + 0.044715 * x * x * x)
    t = jnp.tanh(inner)
    g = 0.5 * x * (1.0 + t)
    dg = 0.5 * (1.0 + t) + 0.5 * x * (1.0 - t * t) * _GELU_C * (1.0 + 3.0 * 0.044715 * x * x)
    return g, dg


def _rms_fwd(x, w):
    rstd = lax.rsqrt(jnp.mean(x * x, axis=-1, keepdims=True) + EPS)
    return x * rstd * w


def _rms_bwd(x, w, dy):
    rstd = lax.rsqrt(jnp.mean(x * x, axis=-1, keepdims=True) + EPS)
    xhat = x * rstd
    dxhat = dy * w
    dx = rstd * (dxhat - xhat * jnp.mean(dxhat * xhat, axis=-1, keepdims=True))
    return dx, dy * xhat


def _mm(terms, m, n, *, tm, tn, mode, out_dtype, name, residual=None, n_outer=False, also_mxu=False, behind=()):
    gm, gn = m // tm, n // tn
    assert gm * tm == m and gn * tn == n
    if n_outer:
        grid = (gn, gm)
        mi = lambda g0, g1: g1
        ni = lambda g0, g1: g0
    else:
        grid = (gm, gn)
        mi = lambda g0, g1: g0
        ni = lambda g0, g1: g1
    in_specs, args = [], []
    for (a, ka, b, kb, k) in terms:
        if mode == "tn":
            in_specs.append(pl.BlockSpec((k, tm), lambda g0, g1, ka=ka: (ka, mi(g0, g1))))
        else:
            in_specs.append(pl.BlockSpec((tm, k), lambda g0, g1, ka=ka: (mi(g0, g1), ka)))
        if mode == "nt":
            in_specs.append(pl.BlockSpec((tn, k), lambda g0, g1, kb=kb: (ni(g0, g1), kb)))
        else:
            in_specs.append(pl.BlockSpec((k, tn), lambda g0, g1, kb=kb: (kb, ni(g0, g1))))
        args += [a, b]
    if residual is not None:
        in_specs.append(pl.BlockSpec((tm, tn), lambda g0, g1: (mi(g0, g1), ni(g0, g1))))
        args.append(residual)
    dims = {"nn": NN, "nt": NT, "tn": TN}[mode]
    n_terms = len(terms)
    has_res = residual is not None
    in_specs += [pl.BlockSpec(memory_space=pl.ANY)] * len(behind)
    args += list(behind)
    n_in = len(args)

    def body(*refs):
        acc = None
        for t in range(n_terms):
            d = lax.dot_general(refs[2 * t][...], refs[2 * t + 1][...], dims, preferred_element_type=F32)
            acc = d if acc is None else acc + d
        if has_res:
            acc = acc + refs[2 * n_terms][...]
        refs[n_in][...] = acc.astype(out_dtype)
        if also_mxu:
            refs[n_in + 1][...] = acc.astype(_MXU)

    tile = pl.BlockSpec((tm, tn), lambda g0, g1: (mi(g0, g1), ni(g0, g1)))
    shape = jax.ShapeDtypeStruct((m, n), out_dtype)
    return pl.pallas_call(
        body, name=name, grid=grid, in_specs=in_specs,
        out_specs=[tile, tile] if also_mxu else tile,
        out_shape=[shape, jax.ShapeDtypeStruct((m, n), _MXU)] if also_mxu else shape,
        compiler_params=_cparams("parallel", "parallel"),
    )(*args)


def _embed(x, meta):
    def body(x_ref, meta_ref, o_ref):
        i = pl.program_id(0)

        @pl.when(i == 0)
        def _():
            o_ref[0:PAD_ROWS, :] = jnp.zeros((PAD_ROWS, D_MODEL), F32)
            o_ref[PAD_ROWS:CHUNK, :] = meta_ref[...]

        @pl.when(i > 0)
        def _():
            o_ref[...] = x_ref[...]

    return pl.pallas_call(
        body, name="embed", grid=(N_CHUNKS,),
        in_specs=[pl.BlockSpec((CHUNK, D_MODEL), lambda i: (jnp.maximum(i - 1, 0), 0)),
                  pl.BlockSpec((N_META, D_MODEL), lambda i: (0, 0))],
        out_specs=pl.BlockSpec((CHUNK, D_MODEL), lambda i: (i, 0)),
        out_shape=jax.ShapeDtypeStruct((T_ROWS, D_MODEL), F32),
        compiler_params=_cparams("parallel"),
    )(x, meta)


def _rmsnorm(h, w, *, name, tm=544):
    def body(h_ref, w_ref, o_ref):
        o_ref[...] = _rms_fwd(h_ref[...], w_ref[...]).astype(_MXU)

    return pl.pallas_call(
        body, name=name, grid=(T_ROWS // tm,),
        in_specs=[pl.BlockSpec((tm, D_MODEL), lambda i: (i, 0)), pl.BlockSpec((1, D_MODEL), lambda i: (0, 0))],
        out_specs=pl.BlockSpec((tm, D_MODEL), lambda i: (i, 0)),
        out_shape=jax.ShapeDtypeStruct((T_ROWS, D_MODEL), _MXU),
        compiler_params=_cparams("parallel"),
    )(h, w)


def _norm_proj(h, w, sections, *, name, tm=272):
    widths = [s.shape[0] for s in sections]
    n = len(sections)

    def body(*refs):
        h_ref, w_ref = refs[:2]
        u_ref = refs[2 + n]
        u = _rms_fwd(h_ref[...], w_ref[...]).astype(_MXU)
        u_ref[...] = u
        for k in range(n):
            refs[3 + n + k][...] = lax.dot_general(u, refs[2 + k][...], NT, preferred_element_type=F32)

    row = lambda width: pl.BlockSpec((tm, width), lambda i: (i, 0))
    outs = pl.pallas_call(
        body, name=name, grid=(T_ROWS // tm,),
        in_specs=[row(D_MODEL), pl.BlockSpec((1, D_MODEL), lambda i: (0, 0))]
        + [pl.BlockSpec((wd, D_MODEL), lambda i: (0, 0)) for wd in widths],
        out_specs=[row(D_MODEL)] + [row(wd) for wd in widths],
        out_shape=[jax.ShapeDtypeStruct((T_ROWS, D_MODEL), _MXU)]
        + [jax.ShapeDtypeStruct((T_ROWS, wd), F32) for wd in widths],
        compiler_params=_cparams("parallel"),
    )(h, w, *sections)
    return outs[0], list(outs[1:])


def _loss_head(h2, target, fw):
    def body(h_ref, t_ref, w_ref, loss_ref, dh_ref, dhb_ref, dw_ref, acc_ref):
        i = pl.program_id(0)

        @pl.when(i == 0)
        def _():
            acc_ref[...] = jnp.zeros_like(acc_ref)
            dw_ref[...] = jnp.zeros_like(dw_ref)

        h = h_ref[...]
        w = w_ref[...]
        y = _rms_fwd(h, w)
        live = (i > 0).astype(F32)
        err = (y - t_ref[...]) * live
        acc_ref[...] += jnp.sum(err * err, axis=0, keepdims=True)
        dy = err * (1.0 / D_MODEL)
        dx, dwr = _rms_bwd(h, w, dy)
        dh_ref[...] = dx
        dhb_ref[...] = dx.astype(_MXU)
        dw_ref[...] += jnp.sum(dwr, axis=0, keepdims=True)

        @pl.when(i == N_CHUNKS - 1)
        def _():
            tot = jnp.sum(acc_ref[...], axis=1, keepdims=True) * (0.5 / D_MODEL)
            loss_ref[...] = jnp.broadcast_to(tot, (1, 128))

    return pl.pallas_call(
        body, name="loss_head", grid=(N_CHUNKS,),
        in_specs=[pl.BlockSpec((CHUNK, D_MODEL), lambda i: (i, 0)),
                  pl.BlockSpec((CHUNK, D_MODEL), lambda i: (jnp.maximum(i - 1, 0), 0)),
                  pl.BlockSpec((1, D_MODEL), lambda i: (0, 0))],
        out_specs=[pl.BlockSpec((1, 128), lambda i: (0, 0)),
                   pl.BlockSpec((CHUNK, D_MODEL), lambda i: (i, 0)),
                   pl.BlockSpec((CHUNK, D_MODEL), lambda i: (i, 0)),
                   pl.BlockSpec((1, D_MODEL), lambda i: (0, 0))],
        out_shape=[jax.ShapeDtypeStruct((1, 128), F32),
                   jax.ShapeDtypeStruct((T_ROWS, D_MODEL), F32),
                   jax.ShapeDtypeStruct((T_ROWS, D_MODEL), _MXU),
                   jax.ShapeDtypeStruct((1, D_MODEL), F32)],
        scratch_shapes=[pltpu.VMEM((1, D_MODEL), F32)],
        compiler_params=_cparams("arbitrary"),
    )(h2, target, fw)


def _mm_norm_bwd(terms, h, w, dres, *, name, tm=272):
    n_terms = len(terms)
    in_specs, args = [], []
    for (a, b, k) in terms:
        in_specs += [pl.BlockSpec((tm, k), lambda i: (i, 0)), pl.BlockSpec((k, D_MODEL), lambda i: (0, 0))]
        args += [a, b]
    in_specs += [pl.BlockSpec((tm, D_MODEL), lambda i: (i, 0)), pl.BlockSpec((1, D_MODEL), lambda i: (0, 0)),
                 pl.BlockSpec((tm, D_MODEL), lambda i: (i, 0))]
    args += [h, w, dres]

    def body(*refs):
        h_ref, w_ref, dres_ref, dh_ref, dhb_ref, dw_ref = refs[2 * n_terms:]

        @pl.when(pl.program_id(0) == 0)
        def _():
            dw_ref[...] = jnp.zeros_like(dw_ref)

        du = None
        for t in range(n_terms):
            d = lax.dot_general(refs[2 * t][...], refs[2 * t + 1][...], NN, preferred_element_type=F32)
            du = d if du is None else du + d
        dx, dwr = _rms_bwd(h_ref[...], w_ref[...], du)
        dh = dres_ref[...] + dx
        dh_ref[...] = dh
        dhb_ref[...] = dh.astype(_MXU)
        dw_ref[...] += jnp.sum(dwr, axis=0, keepdims=True)

    return pl.pallas_call(
        body, name=name, grid=(T_ROWS // tm,), in_specs=in_specs,
        out_specs=[pl.BlockSpec((tm, D_MODEL), lambda i: (i, 0)), pl.BlockSpec((tm, D_MODEL), lambda i: (i, 0)),
                   pl.BlockSpec((1, D_MODEL), lambda i: (0, 0))],
        out_shape=[jax.ShapeDtypeStruct((T_ROWS, D_MODEL), F32), jax.ShapeDtypeStruct((T_ROWS, D_MODEL), _MXU),
                   jax.ShapeDtypeStruct((1, D_MODEL), F32)],
        compiler_params=_cparams("arbitrary"),
    )(*args)


FFN_TM = 272
FFN_TN = 1408


def _ffn_up(u2, wg_t, wu_t):
    def body(u_ref, wg_ref, wu_ref, gp_ref, up_ref, act_ref):
        u = u_ref[...]
        gp = lax.dot_general(u, wg_ref[...], NT, preferred_element_type=F32)
        up = lax.dot_general(u, wu_ref[...], NT, preferred_element_type=F32)
        gp_ref[...] = gp
        up_ref[...] = up
        act_ref[...] = (_silu(gp) * up).astype(_MXU)

    tile = pl.BlockSpec((FFN_TM, FFN_TN), lambda j, i: (i, j))
    return pl.pallas_call(
        body, name="ffn_up", grid=(D_FF // FFN_TN, T_ROWS // FFN_TM),
        in_specs=[pl.BlockSpec((FFN_TM, D_MODEL), lambda j, i: (i, 0)),
                  pl.BlockSpec((FFN_TN, D_MODEL), lambda j, i: (j, 0)),
                  pl.BlockSpec((FFN_TN, D_MODEL), lambda j, i: (j, 0))],
        out_specs=[tile, tile, tile],
        out_shape=[jax.ShapeDtypeStruct((T_ROWS, D_FF), F32), jax.ShapeDtypeStruct((T_ROWS, D_FF), F32),
                   jax.ShapeDtypeStruct((T_ROWS, D_FF), _MXU)],
        compiler_params=_cparams("parallel", "parallel"),
    )(u2, wg_t, wu_t)


def _ffn_bwd_act(dh2b, wd, gp, up):
    def body(dh_ref, wd_ref, gp_ref, up_ref, dgp_ref, dup_ref):
        dact = lax.dot_general(dh_ref[...], wd_ref[...], NT, preferred_element_type=F32)
        gp = gp_ref[...]
        dgp_ref[...] = (dact * up_ref[...] * _silu_grad(gp)).astype(_MXU)
        dup_ref[...] = (dact * _silu(gp)).astype(_MXU)

    tile = pl.BlockSpec((FFN_TM, FFN_TN), lambda j, i: (i, j))
    return pl.pallas_call(
        body, name="ffn_bwd_act", grid=(D_FF // FFN_TN, T_ROWS // FFN_TM),
        in_specs=[pl.BlockSpec((FFN_TM, D_MODEL), lambda j, i: (i, 0)),
                  pl.BlockSpec((FFN_TN, D_MODEL), lambda j, i: (j, 0)), tile, tile],
        out_specs=[tile, tile],
        out_shape=[jax.ShapeDtypeStruct((T_ROWS, D_FF), _MXU), jax.ShapeDtypeStruct((T_ROWS, D_FF), _MXU)],
        compiler_params=_cparams("parallel", "parallel"),
    )(dh2b, wd, gp, up)


CONV_TC = 512
CONV_K = 4


def _conv_pre(x_ref, wv, bv, c):
    tc = wv.shape[1]
    r0 = c * CHUNK
    cur = x_ref[r0:r0 + CHUNK, :]
    if c == 0:
        cat = jnp.concatenate([jnp.zeros((8, tc), F32), cur], axis=0)
        shifted = [cur] + [pltpu.roll(cat, s, 0)[8:8 + CHUNK] for s in range(1, CONV_K)]
    else:
        shifted = [cur] + [x_ref[r0 - s:r0 - s + CHUNK, :] for s in range(1, CONV_K)]
    pre = bv
    for s in range(CONV_K):
        pre = pre + shifted[s] * wv[CONV_K - 1 - s:CONV_K - s]
    return pre, shifted


def _row_mask(c):
    if c > 0:
        return None
    return (lax.broadcasted_iota(jnp.int32, (CHUNK, 1), 0) >= PAD_ROWS).astype(F32)


def _conv_fwd(x, w, b, *, silu, name):
    cols = x.shape[1]
    tc = min(CONV_TC, cols)

    def body(x_ref, w_ref, b_ref, o_ref):
        wv, bv = w_ref[...], b_ref[...]
        for c in range(N_CHUNKS):
            pre, _ = _conv_pre(x_ref, wv, bv, c)
            y = _silu(pre) if silu else pre
            mask = _row_mask(c)
            if mask is not None:
                y = y * mask
            o_ref[c * CHUNK:(c + 1) * CHUNK, :] = y

    return pl.pallas_call(
        body, name=name, grid=(cols // tc,),
        in_specs=[pl.BlockSpec((T_ROWS, tc), lambda j: (0, j)), pl.BlockSpec((CONV_K, tc), lambda j: (0, j)),
                  pl.BlockSpec((1, tc), lambda j: (0, j))],
        out_specs=pl.BlockSpec((T_ROWS, tc), lambda j: (0, j)),
        out_shape=jax.ShapeDtypeStruct((T_ROWS, cols), F32),
        compiler_params=_cparams("parallel"),
    )(x, w, b)


def _conv_bwd(dy, x, w, b, *, silu, name):
    cols = x.shape[1]
    tc = min(CONV_TC, cols)

    def body(dy_ref, x_ref, w_ref, b_ref, dx_ref, dw_ref, db_ref):
        wv, bv = w_ref[...], b_ref[...]
        next8 = jnp.zeros((8, tc), F32)
        dws = [jnp.zeros((1, tc), F32) for _ in range(CONV_K)]
        db = jnp.zeros((1, tc), F32)
        for c in reversed(range(N_CHUNKS)):
            r0 = c * CHUNK
            pre, shifted = _conv_pre(x_ref, wv, bv, c)
            dpre = dy_ref[r0:r0 + CHUNK, :]
            if silu:
                dpre = dpre * _silu_grad(pre)
            mask = _row_mask(c)
            if mask is not None:
                dpre = dpre * mask
            cat = jnp.concatenate([dpre, next8], axis=0)
            dx = dpre * wv[CONV_K - 1:CONV_K]
            for s in range(1, CONV_K):
                dx = dx + pltpu.roll(cat, CHUNK + 8 - s, 0)[0:CHUNK] * wv[CONV_K - 1 - s:CONV_K - s]
            dx_ref[r0:r0 + CHUNK, :] = dx.astype(_MXU)
            for s in range(CONV_K):
                k = CONV_K - 1 - s
                dws[k] = dws[k] + jnp.sum(dpre * shifted[s], axis=0, keepdims=True)
            db = db + jnp.sum(dpre, axis=0, keepdims=True)
            next8 = dpre[0:8]
        dw_ref[...] = jnp.concatenate(dws, axis=0)
        db_ref[...] = db

    return pl.pallas_call(
        body, name=name, grid=(cols // tc,),
        in_specs=[pl.BlockSpec((T_ROWS, tc), lambda j: (0, j)), pl.BlockSpec((T_ROWS, tc), lambda j: (0, j)),
                  pl.BlockSpec((CONV_K, tc), lambda j: (0, j)), pl.BlockSpec((1, tc), lambda j: (0, j))],
        out_specs=[pl.BlockSpec((T_ROWS, tc), lambda j: (0, j)), pl.BlockSpec((CONV_K, tc), lambda j: (0, j)),
                   pl.BlockSpec((1, tc), lambda j: (0, j))],
        out_shape=[jax.ShapeDtypeStruct((T_ROWS, cols), _MXU), jax.ShapeDtypeStruct((CONV_K, cols), F32),
                   jax.ShapeDtypeStruct((1, cols), F32)],
        compiler_params=_cparams("parallel"),
    )(dy, x, w, b)


def _ssd_chunk_common(dt_raw, prm, c):
    a_row = -jnp.exp(prm[1:2])
    dt = _softplus(dt_raw + prm[0:1])
    rows = lax.broadcasted_iota(jnp.int32, (CHUNK, 1), 0)
    real = jnp.logical_or(c > 0, rows >= PAD_ROWS)
    dt = jnp.where(real, dt, 0.0)
    li = lax.broadcasted_iota(jnp.int32, (CHUNK, CHUNK), 0)
    si = lax.broadcasted_iota(jnp.int32, (CHUNK, CHUNK), 1)
    causal = li >= si
    tri = causal.astype(F32)
    cs = _dot_onehot(tri, dt * a_row, data=1)
    return dt, a_row, cs, cs.T, causal, tri, real


def _gated_norm_fwd(y, z, w):
    g = y * _silu(z)
    half = SSD_WIDTH // SSD_GROUPS
    outs = [_rms_fwd(g[:, k * half:(k + 1) * half], w[:, k * half:(k + 1) * half]) for k in range(SSD_GROUPS)]
    return jnp.concatenate(outs, axis=1)


GROUP_W = SSD_WIDTH // SSD_GROUPS
PAIR_W = 2 * SSD_HEAD_DIM
STATE_SHAPE = (SSD_GROUPS, SSD_STATE, GROUP_W)


def _head_expander():
    r = lax.broadcasted_iota(jnp.int32, (128, SSD_WIDTH), 0)
    c = lax.broadcasted_iota(jnp.int32, (128, SSD_WIDTH), 1)
    return (c // SSD_HEAD_DIM == r).astype(F32)


def _ssd_expand(dt, cs, prm, ex):
    cs_x = _dot_onehot(cs, ex)
    cs_last_x = cs_x[CHUNK - 1:CHUNK, :]
    return (_dot_onehot(dt, ex), _dot_onehot(prm, ex)[2:3], jnp.exp(cs_x), jnp.exp(cs_last_x),
            jnp.exp(cs_last_x - cs_x))


def _ssd_fwd(xs, bc, dt_raw, z, prm, norm_w, ex):
    def body(xs_ref, bc_ref, dt_ref, z_ref, prm_ref, nw_ref, ex_ref, y_ref, yn_ref, prev_ref, state):
        c = pl.program_id(0)

        @pl.when(c == 0)
        def _():
            state[...] = jnp.zeros_like(state)

        prm = prm_ref[...]
        dt, a_row, cs, cs_t, causal, _, _ = _ssd_chunk_common(dt_ref[...], prm, c)
        dt_x, d_x, e_cs_x, e_last_x, dec_x = _ssd_expand(dt, cs, prm, ex_ref[...])
        xs_all = xs_ref[...]
        bc_all = bc_ref[...]
        xdt = xs_all * dt_x
        xdec = xdt * dec_x
        lane_lo = lax.broadcasted_iota(jnp.int32, (1, PAIR_W), 1) < SSD_HEAD_DIM
        for g in range(SSD_GROUPS):
            gs = slice(g * GROUP_W, (g + 1) * GROUP_W)
            b_g = bc_all[:, g * SSD_STATE:(g + 1) * SSD_STATE]
            c_g = bc_all[:, (SSD_GROUPS + g) * SSD_STATE:(SSD_GROUPS + g + 1) * SSD_STATE]
            st = state[g]
            prev_ref[0, g] = st
            y_off = _dot(c_g, st) * e_cs_x[:, gs]
            state[g] = st * e_last_x[:, gs] + _dot(b_g.T, xdec[:, gs])
            cb = _dot(c_g, b_g, NT)
            for k in range(SSD_HPG // 2):
                h0 = g * SSD_HPG + 2 * k
                ps = slice(h0 * SSD_HEAD_DIM, h0 * SSD_HEAD_DIM + PAIR_W)
                xdt_pair = xdt[:, ps]
                yd = []
                for h in (h0, h0 + 1):
                    lmat = jnp.where(causal, jnp.exp(cs[:, h:h + 1] - cs_t[h:h + 1, :]), 0.0)
                    yd.append(_dot(cb * lmat, xdt_pair))
                y_ref[:, ps] = (jnp.where(lane_lo, yd[0], yd[1]) + y_off[:, k * PAIR_W:(k + 1) * PAIR_W]
                                + xs_all[:, ps] * d_x[:, ps])
        yn_ref[...] = _gated_norm_fwd(y_ref[...], z_ref[...], nw_ref[...]).astype(_MXU)

    row = lambda w: pl.BlockSpec((CHUNK, w), lambda c: (c, 0))
    return pl.pallas_call(
        body, name="ssd_fwd", grid=(N_CHUNKS,),
        in_specs=[row(SSD_WIDTH), row(512), row(128), row(SSD_WIDTH),
                  pl.BlockSpec((8, 128), lambda c: (0, 0)), pl.BlockSpec((1, SSD_WIDTH), lambda c: (0, 0)),
                  pl.BlockSpec((128, SSD_WIDTH), lambda c: (0, 0))],
        out_specs=[row(SSD_WIDTH), row(SSD_WIDTH),
                   pl.BlockSpec((1,) + STATE_SHAPE, lambda c: (c, 0, 0, 0))],
        out_shape=[jax.ShapeDtypeStruct((T_ROWS, SSD_WIDTH), F32), jax.ShapeDtypeStruct((T_ROWS, SSD_WIDTH), _MXU),
                   jax.ShapeDtypeStruct((N_CHUNKS,) + STATE_SHAPE, F32)],
        scratch_shapes=[pltpu.VMEM(STATE_SHAPE, F32)],
        compiler_params=_cparams("arbitrary"),
    )(xs, bc, dt_raw, z, prm, norm_w, ex)


def _ssd_bwd(dyn, dyn_block, z, y_pre, xs, bc, dt_raw, prev, prm, norm_w, ex):
    def body(dyn_ref, z_ref, y_ref, xs_ref, bc_ref, dt_ref, prev_ref, prm_ref, nw_ref, ex_ref,
             dz_ref, dxs_ref, dbc_ref, ddt_ref, dprm_ref, dnw_ref, dstate):
        step = pl.program_id(0)
        c = N_CHUNKS - 1 - step

        @pl.when(step == 0)
        def _():
            dstate[...] = jnp.zeros_like(dstate)
            dprm_ref[...] = jnp.zeros_like(dprm_ref)
            dnw_ref[...] = jnp.zeros_like(dnw_ref)

        prm = prm_ref[...]
        dt, a_row, cs, cs_t, causal, tri, real = _ssd_chunk_common(dt_ref[...], prm, c)
        realf = real.astype(F32)
        z = z_ref[...]
        y_all = y_ref[...]
        nw = nw_ref[...]
        dyn_all = dyn_ref[...]
        sz = _silu(z)
        gated = y_all * sz
        half = SSD_WIDTH // SSD_GROUPS
        dgs, dnws = [], []
        for k in range(SSD_GROUPS):
            sl = slice(k * half, (k + 1) * half)
            dgk, dwk = _rms_bwd(gated[:, sl], nw[:, sl], dyn_all[:, sl])
            dgs.append(dgk)
            dnws.append(jnp.sum(dwk, axis=0, keepdims=True))
        dgated = jnp.concatenate(dgs, axis=1)
        dnw_ref[...] += jnp.concatenate(dnws, axis=1)
        dz_ref[...] = (dgated * y_all * _silu_grad(z)).astype(_MXU)
        dy_all = dgated * sz

        ex = ex_ref[...]
        dt_x, d_x, e_cs_x, e_last_x, dec_x = _ssd_expand(dt, cs, prm, ex)
        xs_all = xs_ref[...]
        bc_all = bc_ref[...]
        xdt = xs_all * dt_x
        xdt_mxu = xdt.astype(_MXU).astype(F32)
        xdec = xdt * dec_x
        dcp = dy_all * e_cs_x
        lane_lo = lax.broadcasted_iota(jnp.int32, (1, PAIR_W), 1) < SSD_HEAD_DIM
        upper = (lax.broadcasted_iota(jnp.int32, (CHUNK, CHUNK), 0)
                 <= lax.broadcasted_iota(jnp.int32, (CHUNK, CHUNK), 1))
        last_row = (lax.broadcasted_iota(jnp.int32, (CHUNK, 1), 0) == CHUNK - 1).astype(F32)
        dbs, dcs_, dxdt_parts, last_parts = [], [], [], []
        for g in range(SSD_GROUPS):
            gs = slice(g * GROUP_W, (g + 1) * GROUP_W)
            b_g = bc_all[:, g * SSD_STATE:(g + 1) * SSD_STATE]
            c_g = bc_all[:, (SSD_GROUPS + g) * SSD_STATE:(SSD_GROUPS + g + 1) * SSD_STATE]
            prev_t = prev_ref[0, g]
            dst = dstate[g]
            dc_g = _dot(dcp[:, gs], prev_t, NT)
            db_g = _dot(xdec[:, gs], dst, NT)
            dxdt_state = _dot(b_g, dst) * dec_x[:, gs]
            dstate[g] = dst * e_last_x[:, gs] + _dot(c_g.T, dcp[:, gs])
            last_parts.append(jnp.sum(xdt_mxu[:, gs] * dxdt_state, axis=0, keepdims=True)
                              + jnp.sum(dst * prev_t, axis=0, keepdims=True) * e_last_x[:, gs])
            cb_t = _dot(b_g, c_g, NT)
            dcb_t = jnp.zeros((CHUNK, CHUNK), F32)
            for k in range(SSD_HPG // 2):
                h0 = g * SSD_HPG + 2 * k
                ps = slice(h0 * SSD_HEAD_DIM, h0 * SSD_HEAD_DIM + PAIR_W)
                dy_pair = dy_all[:, ps]
                xdt_pair = xdt[:, ps]
                dd = []
                for h in (h0, h0 + 1):
                    lmat_t = jnp.where(upper, jnp.exp(cs_t[h:h + 1, :] - cs[:, h:h + 1]), 0.0)
                    dd.append(_dot(cb_t * lmat_t, dy_pair))
                    mine = lane_lo if h == h0 else jnp.logical_not(lane_lo)
                    dcb_t = dcb_t + _dot(jnp.where(mine, xdt_pair, 0.0), dy_pair, NT) * lmat_t
                dxdt_parts.append(jnp.where(lane_lo, dd[0], dd[1]) + dxdt_state[:, k * PAIR_W:(k + 1) * PAIR_W])
            dc_g = dc_g + _dot(dcb_t, b_g, TN)
            db_g = db_g + _dot(dcb_t, c_g)
            dbs.append(db_g * realf)
            dcs_.append(dc_g * realf)
        dbc_ref[...] = jnp.concatenate(dbs + dcs_, axis=1)
        dxdt = jnp.concatenate(dxdt_parts, axis=1)
        dxs_ref[...] = (dxdt * dt_x + dy_all * d_x) * realf
        ddt_all = _dot_onehot(dxdt * xs_all, ex, NT)
        rows = jnp.concatenate([jnp.concatenate(last_parts, axis=1), jnp.sum(dy_all * xs_all, axis=0, keepdims=True),
                                jnp.zeros((6, SSD_WIDTH), F32)], axis=0)
        rows = _dot_onehot(rows, ex, NT)
        dd_row = rows[1:2]
        dy_mxu = dy_all.astype(_MXU).astype(F32)
        dcs_all = (_dot_onehot(dy_mxu * (y_all - xs_all * d_x), ex, NT) - _dot_onehot(xdt_mxu * dxdt, ex, NT)
                   + last_row * rows[0:1])
        dda = _dot_onehot(tri, dcs_all, TN, data=1)
        ddt = (ddt_all + dda * a_row) * realf
        ddt_raw = ddt * _sigmoid(dt_ref[...] + prm[0:1])
        ddt_ref[...] = ddt_raw.astype(_MXU)
        da_log = jnp.sum(dda * dt, axis=0, keepdims=True) * a_row
        dprm_ref[0:1, :] += jnp.sum(ddt_raw, axis=0, keepdims=True)
        dprm_ref[1:2, :] += da_log
        dprm_ref[2:3, :] += dd_row

    rev = lambda w, blk=0: pl.BlockSpec((CHUNK, w), lambda s, blk=blk: (N_CHUNKS - 1 - s, blk))
    return pl.pallas_call(
        body, name="ssd_bwd", grid=(N_CHUNKS,),
        in_specs=[rev(SSD_WIDTH, dyn_block), rev(SSD_WIDTH), rev(SSD_WIDTH), rev(SSD_WIDTH), rev(512), rev(128),
                  pl.BlockSpec((1,) + STATE_SHAPE, lambda s: (N_CHUNKS - 1 - s, 0, 0, 0)),
                  pl.BlockSpec((8, 128), lambda s: (0, 0)), pl.BlockSpec((1, SSD_WIDTH), lambda s: (0, 0)),
                  pl.BlockSpec((128, SSD_WIDTH), lambda s: (0, 0))],
        out_specs=[rev(SSD_WIDTH), rev(SSD_WIDTH), rev(512), rev(128),
                   pl.BlockSpec((8, 128), lambda s: (0, 0)), pl.BlockSpec((1, SSD_WIDTH), lambda s: (0, 0))],
        out_shape=[jax.ShapeDtypeStruct((T_ROWS, SSD_WIDTH), _MXU), jax.ShapeDtypeStruct((T_ROWS, SSD_WIDTH), F32),
                   jax.ShapeDtypeStruct((T_ROWS, 512), F32), jax.ShapeDtypeStruct((T_ROWS, 128), _MXU),
                   jax.ShapeDtypeStruct((8, 128), F32), jax.ShapeDtypeStruct((1, SSD_WIDTH), F32)],
        scratch_shapes=[pltpu.VMEM(STATE_SHAPE, F32)],
        compiler_params=_cparams("arbitrary"),
    )(dyn, z, y_pre, xs, bc, dt_raw, prev, prm, norm_w, ex)


LRU_PAIRS = 8


def _lru_gates(xr, wa_ref, wx_ref, prm):
    pre_r, pre_i = [], []
    for k in range(LRU_PAIRS):
        xk = xr[:, k * 128:(k + 1) * 128]
        pre_r.append(_dot(xk, wa_ref[k]))
        pre_i.append(_dot(xk, wx_ref[k]))
    r = _sigmoid(jnp.concatenate(pre_r, axis=1) + prm[0:1])
    i = _sigmoid(jnp.concatenate(pre_i, axis=1) + prm[1:2])
    sp = _softplus(-prm[2:3])
    log_a = (-LRU_C) * r * sp
    a = jnp.exp(log_a)
    s = jnp.sqrt(-jnp.tanh(log_a) * (a * a + 1.0))
    return r, i, a, s, sp


def _lru_fwd(xr, gate, wa, wx, prm):
    def body(xr_ref, g_ref, wa_ref, wx_ref, prm_ref, hs_ref, yn_ref, carry, a_s, u_s):
        @pl.when(pl.program_id(0) == 0)
        def _():
            carry[...] = jnp.zeros_like(carry)

        prm = prm_ref[...]
        xr_t = xr_ref[...]
        _, i, a, s, _ = _lru_gates(xr_t, wa_ref, wx_ref, prm)
        a_s[...] = a
        u_s[...] = s * (i * xr_t)
        rid = lax.broadcasted_iota(jnp.int32, (8, LRU_WIDTH), 0)

        def group(k, h):
            off = pl.multiple_of(k * 8, 8)
            a8 = a_s[pl.ds(off, 8), :]
            u8 = u_s[pl.ds(off, 8), :]
            out = jnp.zeros((8, LRU_WIDTH), F32)
            for r_ in range(8):
                h = a8[r_:r_ + 1] * h + u8[r_:r_ + 1]
                out = jnp.where(rid == r_, h, out)
            hs_ref[pl.ds(off, 8), :] = out
            return h

        carry[0:1, :] = lax.fori_loop(0, CHUNK // 8, group, carry[0:1, :])
        gel, _ = _gelu_and_grad(g_ref[...])
        yn_ref[...] = _rms_fwd(gel * hs_ref[...], prm[3:4]).astype(_MXU)

    row = pl.BlockSpec((CHUNK, LRU_WIDTH), lambda t: (t, 0))
    wspec = pl.BlockSpec((LRU_PAIRS, 128, 128), lambda t: (0, 0, 0))
    return pl.pallas_call(
        body, name="lru_fwd", grid=(N_CHUNKS,),
        in_specs=[row, row, wspec, wspec, pl.BlockSpec((8, LRU_WIDTH), lambda t: (0, 0))],
        out_specs=[row, row],
        out_shape=[jax.ShapeDtypeStruct((T_ROWS, LRU_WIDTH), F32), jax.ShapeDtypeStruct((T_ROWS, LRU_WIDTH), _MXU)],
        scratch_shapes=[pltpu.VMEM((8, LRU_WIDTH), F32), pltpu.VMEM((CHUNK, LRU_WIDTH), F32),
                        pltpu.VMEM((CHUNK, LRU_WIDTH), F32)],
        compiler_params=_cparams("arbitrary"),
    )(xr, gate, wa, wx, prm)


def _lru_bwd(dyn, dyn_block, gate, xr, hs, wa, wx, wa_t, wx_t, prm):
    def body(dyn_ref, g_ref, xr_ref, hs_ref, hsp_ref, wa_ref, wx_ref, wat_ref, wxt_ref, prm_ref,
             dg_ref, dxr_ref, dwa_ref, dwx_ref, dprm_ref, carry, a_s, d_s):
        step = pl.program_id(0)
        tile = N_CHUNKS - 1 - step

        @pl.when(step == 0)
        def _():
            carry[...] = jnp.zeros_like(carry)
            dwa_ref[...] = jnp.zeros_like(dwa_ref)
            dwx_ref[...] = jnp.zeros_like(dwx_ref)
            dprm_ref[...] = jnp.zeros_like(dprm_ref)

        prm = prm_ref[...]
        xr_t = xr_ref[...]
        r, i, a, s, sp = _lru_gates(xr_t, wa_ref, wx_ref, prm)
        hs_t = hs_ref[...]
        gel, dgel = _gelu_and_grad(g_ref[...])
        dy, dnw = _rms_bwd(gel * hs_t, prm[3:4], dyn_ref[...])
        dg_ref[...] = (dy * hs_t * dgel).astype(_MXU)
        a_s[...] = a
        d_s[...] = dy * gel
        rid = lax.broadcasted_iota(jnp.int32, (8, LRU_WIDTH), 0)

        def group(k, cr):
            off = pl.multiple_of((CHUNK // 8 - 1 - k) * 8, 8)
            a8 = a_s[pl.ds(off, 8), :]
            d8 = d_s[pl.ds(off, 8), :]
            out = jnp.zeros((8, LRU_WIDTH), F32)
            for r_ in reversed(range(8)):
                dht = d8[r_:r_ + 1] + cr
                out = jnp.where(rid == r_, dht, out)
                cr = a8[r_:r_ + 1] * dht
            d_s[pl.ds(off, 8), :] = out
            return cr

        carry[0:1, :] = lax.fori_loop(0, CHUNK // 8, group, carry[0:1, :])
        dht = d_s[...]
        before = hsp_ref[CHUNK - 8:CHUNK, :][7:8] * (tile > 0).astype(F32)
        first = lax.broadcasted_iota(jnp.int32, (CHUNK, 1), 0) == 0
        hprev = jnp.where(first, before, pltpu.roll(hs_t, 1, 0))
        da = dht * hprev
        ixr = i * xr_t
        ds = dht * ixr
        dlog_a = da * a - ds * (a * a) / s
        dr = dlog_a * ((-LRU_C) * sp)
        dsp = jnp.sum(dlog_a * ((-LRU_C) * r), axis=0, keepdims=True)
        dlam = dsp * (-_sigmoid(-prm[2:3]))
        di = dht * s * xr_t
        dpre_r = dr * r * (1.0 - r)
        dpre_i = di * i * (1.0 - i)
        dxr = dht * s * i
        parts = []
        for k in range(LRU_PAIRS):
            sl = slice(k * 128, (k + 1) * 128)
            parts.append(_dot(dpre_r[:, sl], wat_ref[k]) + _dot(dpre_i[:, sl], wxt_ref[k]))
            dwa_ref[k] += _dot(xr_t[:, sl], dpre_r[:, sl], TN)
            dwx_ref[k] += _dot(xr_t[:, sl], dpre_i[:, sl], TN)
        dxr_ref[...] = dxr + jnp.concatenate(parts, axis=1)
        dprm_ref[0:1, :] += jnp.sum(dpre_r, axis=0, keepdims=True)
        dprm_ref[1:2, :] += jnp.sum(dpre_i, axis=0, keepdims=True)
        dprm_ref[2:3, :] += dlam
        dprm_ref[3:4, :] += jnp.sum(dnw, axis=0, keepdims=True)

    rev = lambda blk=0: pl.BlockSpec((CHUNK, LRU_WIDTH), lambda s, blk=blk: (N_CHUNKS - 1 - s, blk))
    wspec = pl.BlockSpec((LRU_PAIRS, 128, 128), lambda s: (0, 0, 0))
    return pl.pallas_call(
        body, name="lru_bwd", grid=(N_CHUNKS,),
        in_specs=[rev(dyn_block), rev(), rev(), rev(),
                  pl.BlockSpec((CHUNK, LRU_WIDTH), lambda s: (jnp.maximum(N_CHUNKS - 2 - s, 0), 0)),
                  wspec, wspec, wspec, wspec, pl.BlockSpec((8, LRU_WIDTH), lambda s: (0, 0))],
        out_specs=[rev(), rev(), wspec, wspec, pl.BlockSpec((8, LRU_WIDTH), lambda s: (0, 0))],
        out_shape=[jax.ShapeDtypeStruct((T_ROWS, LRU_WIDTH), _MXU), jax.ShapeDtypeStruct((T_ROWS, LRU_WIDTH), F32),
                   jax.ShapeDtypeStruct((LRU_PAIRS, 128, 128), F32), jax.ShapeDtypeStruct((LRU_PAIRS, 128, 128), F32),
                   jax.ShapeDtypeStruct((8, LRU_WIDTH), F32)],
        scratch_shapes=[pltpu.VMEM((8, LRU_WIDTH), F32), pltpu.VMEM((CHUNK, LRU_WIDTH), F32),
                        pltpu.VMEM((CHUNK, LRU_WIDTH), F32)],
        compiler_params=_cparams("arbitrary"),
    )(dyn, gate, xr, hs, hs, wa, wx, wa_t, wx_t, prm)


SEC_NAMES = ("z", "xs", "bc", "dt", "g", "x")
SEC_WIDTH = {"z": 1024, "xs": 1024, "bc": 512, "dt": 128, "g": 1024, "x": 1024}


def _pair_blocks(w):
    w = w.reshape(LRU_PAIRS, 2, 64, 64)
    zero = jnp.zeros((LRU_PAIRS, 64, 64), w.dtype)
    top = jnp.concatenate([w[:, 0], zero], axis=2)
    bot = jnp.concatenate([zero, w[:, 1]], axis=2)
    return jnp.concatenate([top, bot], axis=1)


def _unpair_blocks(wp):
    return jnp.stack([wp[:, :64, :64], wp[:, 64:, 64:]], axis=1).reshape(16, 64, 64)


def _pad_lanes(v, width=128):
    return jnp.pad(v, ((0, 0), (0, width - v.shape[1])))


class _Resident:
    def __init__(self, w_out, w_gate, w_up, w_down):
        self._w_out, self._ffn = w_out, (w_gate, w_up, w_down)

    def w_out(self, after):
        return self._w_out

    def ffn(self, after):
        return self._ffn

    def grads_ready(self, names, g, g_mxu):
        return jnp.zeros((1, 1), F32)

    def small_ready(self, g, loss):
        return jnp.zeros((1, 1), F32)

    def small_middle(self, after):
        return jnp.zeros((1, 1), F32)


def _local_step(x, target, meta, p, late):
    g, g_mxu = {}, {}
    ex = _head_expander()
    h0 = _embed(x, meta)
    u1, projs = _norm_proj(h0, p["norm1_w"], [p["w_in_" + s] for s in SEC_NAMES], name="norm_in_proj")
    proj = dict(zip(SEC_NAMES, projs))
    ssd_prm = jnp.concatenate([_pad_lanes(p["ssd_dt_bias"]), _pad_lanes(p["ssd_a_log"]), _pad_lanes(p["ssd_d"]),
                               jnp.zeros((5, 128), F32)], axis=0)
    xs_act = _conv_fwd(proj["xs"], p["ssd_conv_w"][:, :SSD_WIDTH], p["ssd_conv_b"][:, :SSD_WIDTH], silu=True,
                       name="ssd_conv_xs")
    bc_act = _conv_fwd(proj["bc"], p["ssd_conv_w"][:, SSD_WIDTH:], p["ssd_conv_b"][:, SSD_WIDTH:], silu=True,
                       name="ssd_conv_bc")
    y_pre, y_ssd, prev = _ssd_fwd(xs_act, bc_act, proj["dt"], proj["z"], ssd_prm, p["ssd_norm_w"], ex)
    xr = _conv_fwd(proj["x"], p["lru_conv_w"], p["lru_conv_b"], silu=False, name="lru_conv")
    wa_p, wx_p = _pair_blocks(p["lru_wa"]), _pair_blocks(p["lru_wx"])
    lru_prm = jnp.concatenate([p["lru_ba"], p["lru_bx"], p["lru_lambda"], p["lru_norm_w"],
                               jnp.zeros((4, LRU_WIDTH), F32)], axis=0)
    hs, y_lru = _lru_fwd(xr, proj["g"], wa_p.astype(_MXU), wx_p.astype(_MXU), lru_prm)
    ycat = jnp.concatenate([y_ssd, y_lru], axis=1)
    w_out = late.w_out(ycat)
    h1 = _mm([(ycat, 0, w_out, 0, 2 * D_MODEL)], T_ROWS, D_MODEL, tm=544, tn=512, mode="nn", out_dtype=F32,
             name="out_proj", residual=h0)
    u2 = _rmsnorm(h1, p["norm2_w"], name="norm2")
    w_gate, w_up, w_down = late.ffn(u2)
    gp, up, act = _ffn_up(u2, w_gate, w_up)
    h2 = _mm([(act, 0, w_down, 0, D_FF)], T_ROWS, D_MODEL, tm=544, tn=512, mode="nn", out_dtype=F32,
             name="ffn_down", residual=h1)
    loss, dh2, dh2b, g["final_norm_w"] = _loss_head(h2, target, p["final_norm_w"])
    dgp, dup = _ffn_bwd_act(dh2b, w_down, gp, up)
    g["w_down"], g_mxu["w_down"] = _mm([(act, 0, dh2b, 0, T_ROWS)], D_FF, D_MODEL, tm=1408, tn=512, mode="tn",
                                       out_dtype=F32, name="dw_down", also_mxu=True)
    dh1, dh1b, g["norm2_w"] = _mm_norm_bwd([(dgp, w_gate, D_FF), (dup, w_up, D_FF)], h1, p["norm2_w"], dh2,
                                           name="ffn_bwd_in")
    g["w_gate"], g_mxu["w_gate"] = _mm([(dgp, 0, u2, 0, T_ROWS)], D_FF, D_MODEL, tm=1408, tn=512, mode="tn",
                                       out_dtype=F32, name="dw_gate", also_mxu=True)
    g["w_up"], g_mxu["w_up"] = _mm([(dup, 0, u2, 0, T_ROWS)], D_FF, D_MODEL, tm=1408, tn=512, mode="tn",
                                   out_dtype=F32, name="dw_up", also_mxu=True)
    sent = late.grads_ready(("w_down", "w_gate", "w_up"), g, g_mxu)
    g["w_out"], g_mxu["w_out"] = _mm([(ycat, 0, dh1b, 0, T_ROWS)], 2 * D_MODEL, D_MODEL, tm=512, tn=512, mode="tn",
                                     out_dtype=F32, name="dw_out", also_mxu=True, behind=(sent,))
    sent = late.grads_ready(("w_out",), g, g_mxu)
    dycat = _mm([(dh1b, 0, w_out, 0, D_MODEL)], T_ROWS, 2 * D_MODEL, tm=544, tn=512, mode="nt", out_dtype=F32,
                name="out_proj_bwd", behind=(sent,))
    dgate, dxr, dwa_p, dwx_p, dlru_prm = _lru_bwd(dycat, 1, proj["g"], xr, hs, wa_p.astype(_MXU), wx_p.astype(_MXU),
                                                  jnp.swapaxes(wa_p, 1, 2).astype(_MXU),
                                                  jnp.swapaxes(wx_p, 1, 2).astype(_MXU), lru_prm)
    g["lru_wa"], g["lru_wx"] = _unpair_blocks(dwa_p), _unpair_blocks(dwx_p)
    g["lru_ba"], g["lru_bx"], g["lru_lambda"], g["lru_norm_w"] = (dlru_prm[k:k + 1] for k in range(4))
    dx_lru, g["lru_conv_w"], g["lru_conv_b"] = _conv_bwd(dxr, proj["x"], p["lru_conv_w"], p["lru_conv_b"], silu=False,
                                                         name="lru_conv_bwd")
    dz, dxs_act, dbc_act, ddt, dssd_prm, g["ssd_norm_w"] = _ssd_bwd(dycat, 0, proj["z"], y_pre, xs_act, bc_act,
                                                                    proj["dt"], prev, ssd_prm, p["ssd_norm_w"], ex)
    g["ssd_dt_bias"], g["ssd_a_log"], g["ssd_d"] = (dssd_prm[k:k + 1, :SSD_HEADS] for k in range(3))
    dxs, dcw_xs, dcb_xs = _conv_bwd(dxs_act, proj["xs"], p["ssd_conv_w"][:, :SSD_WIDTH],
                                    p["ssd_conv_b"][:, :SSD_WIDTH], silu=True, name="ssd_conv_xs_bwd")
    dbc, dcw_bc, dcb_bc = _conv_bwd(dbc_act, proj["bc"], p["ssd_conv_w"][:, SSD_WIDTH:],
                                    p["ssd_conv_b"][:, SSD_WIDTH:], silu=True, name="ssd_conv_bc_bwd")
    g["ssd_conv_w"] = jnp.concatenate([dcw_xs, dcw_bc], axis=1)
    g["ssd_conv_b"] = jnp.concatenate([dcb_xs, dcb_bc], axis=1)
    dproj = {"z": dz, "xs": dxs, "bc": dbc, "dt": ddt, "g": dgate, "x": dx_lru}
    dh0, _, g["norm1_w"] = _mm_norm_bwd([(dproj[s], p["w_in_" + s], SEC_WIDTH[s]) for s in SEC_NAMES], h0,
                                        p["norm1_w"], dh1, name="in_proj_bwd")
    g["meta_tokens"] = dh0[PAD_ROWS:X_ROW0]
    sent = late.small_ready(g, loss)
    for s in SEC_NAMES:
        wdt = SEC_WIDTH[s]
        g["w_in_" + s], g_mxu["w_in_" + s] = _mm([(dproj[s], 0, u1, 0, T_ROWS)], wdt, D_MODEL, tm=min(wdt, 512),
                                                 tn=512, mode="tn", out_dtype=F32, name="dw_in_" + s, also_mxu=True,
                                                 behind=(sent,))
        if s == "bc":
            sent = late.small_middle(g["w_in_bc"])
    return loss, dh0[X_ROW0:], g, g_mxu


MESH = pl.DeviceIdType.MESH
ANY = pl.BlockSpec(memory_space=pl.ANY)


def _my_place():
    return lax.axis_index("x"), lax.axis_index("y"), lax.axis_index("c")


def _other_chips(x, y):
    return [(1 - x, y), (x, 1 - y), (1 - x, 1 - y)]


def _gather_first(big, small):
    half = big.shape[1] // 2

    def body(big_ref, small_ref, big4, small4, send_sems, recv_sems, local_sems):
        x, y, c = _my_place()
        me = 2 * x + y
        sibling = (x, y, 1 - c)
        peers = _other_chips(x, y)
        mine = pl.ds(pl.multiple_of(c * half, 128), half)
        theirs = pl.ds(pl.multiple_of((1 - c) * half, 128), half)

        def copy(k, src, dst, dev):
            return pltpu.make_async_remote_copy(src_ref=src, dst_ref=dst, send_sem=send_sems.at[k],
                                                recv_sem=recv_sems.at[k], device_id=dev, device_id_type=MESH)

        local = [pltpu.make_async_copy(big_ref, big4.at[me], local_sems.at[0]),
                 pltpu.make_async_copy(small_ref, small4.at[me], local_sems.at[1])]
        for cp in local:
            cp.start()
        first = []
        for j, (px, py) in enumerate(peers):
            first.append(copy(j, big_ref.at[:, mine], big4.at[me, :, mine], (px, py, c)))
            first.append(copy(3 + j, small_ref, small4.at[me], (px, py, c)))
        for cp in first:
            cp.start()
        passed = []
        for j, (px, py) in enumerate(peers):
            slot = 2 * px + py
            copy(j, big_ref.at[:, mine], big4.at[slot, :, mine], (px, py, c)).wait_recv()
            passed.append(copy(6 + j, big4.at[slot, :, mine], big4.at[slot, :, mine], sibling))
            passed[-1].start()
        for j, (px, py) in enumerate(peers):
            slot = 2 * px + py
            copy(6 + j, big4.at[slot, :, theirs], big4.at[slot, :, theirs], sibling).wait_recv()
            copy(3 + j, small_ref, small4.at[slot], (px, py, c)).wait_recv()
        for cp in first + passed:
            cp.wait_send()
        for cp in local:
            cp.wait()

    return pl.pallas_call(
        body, name="gather_first", in_specs=[ANY, ANY], out_specs=[ANY, ANY],
        out_shape=[jax.ShapeDtypeStruct((N_SHARDS,) + big.shape, big.dtype),
                   jax.ShapeDtypeStruct((N_SHARDS,) + small.shape, small.dtype)],
        scratch_shapes=[pltpu.SemaphoreType.DMA((9,)), pltpu.SemaphoreType.DMA((9,)), pltpu.SemaphoreType.DMA((2,))],
    )(big, small)


HBM_SPEC = pl.BlockSpec(memory_space=pltpu.HBM)
SEM_SPEC = pl.BlockSpec(memory_space=pltpu.SEMAPHORE)
SPLIT_EFFECT = pltpu.SideEffectType.DATAFLOW_SIDE_EFFECTING


def _gather_plan(bufs, x, y, c, incoming):
    plan = []
    for buf in bufs:
        for (px, py) in _other_chips(x, y):
            slot = 2 * px + py if incoming else 2 * x + y
            plan.append((buf.at[2 * x + y], buf.at[slot], (px, py, c)))
    return plan


def _scatter_plan(bufs, x, y, c, incoming):
    n = len(bufs) // 2
    plan = []
    for k in range(n):
        for j, (px, py) in enumerate(_other_chips(x, y)):
            plan.append((bufs[k].at[2 * px + py], bufs[n + k].at[j], (px, py, c)))
    return plan


def _split_start(bufs, plan, n_copies, after, *, name):
    n = len(bufs)
    extra = [] if after is None else [after]

    def body(*refs):
        ins = refs[:n]
        send_sems, recv_sems = refs[n + len(extra)], refs[n + len(extra) + 1]
        token = refs[-1]
        x, y, c = _my_place()
        for i, (src, dst, dev) in enumerate(plan(ins, x, y, c, False)):
            pltpu.make_async_remote_copy(src_ref=src, dst_ref=dst, send_sem=send_sems.at[i], recv_sem=recv_sems.at[i],
                                         device_id=dev, device_id_type=MESH).start()
        token[...] = jnp.zeros_like(token)

    outs = pl.pallas_call(
        body, name=name,
        out_shape=(pltpu.SemaphoreType.DMA((n_copies,)), pltpu.SemaphoreType.DMA((n_copies,)),
                   *[pltpu.HBM(b.shape, b.dtype) for b in bufs], jax.ShapeDtypeStruct((8, 128), F32)),
        in_specs=[HBM_SPEC] * n + [ANY] * len(extra),
        out_specs=(SEM_SPEC, SEM_SPEC, *[HBM_SPEC] * n, pl.BlockSpec(memory_space=pltpu.VMEM)),
        input_output_aliases={k: 2 + k for k in range(n)},
        compiler_params=pltpu.CompilerParams(has_side_effects=SPLIT_EFFECT),
    )(*[pltpu.with_memory_space_constraint(b, pltpu.HBM) for b in bufs], *extra)
    return outs[0], outs[1], list(outs[2:2 + n]), outs[-1]


def _split_wait(bufs, send_sems, recv_sems, plan, after, *, name):
    n = len(bufs)

    def body(*refs):
        ins = refs[:n]
        send_sems_ref, recv_sems_ref = refs[n], refs[n + 1]
        x, y, c = _my_place()
        for i, (src, dst, dev) in enumerate(plan(ins, x, y, c, True)):
            cp = pltpu.make_async_remote_copy(src_ref=src, dst_ref=dst, send_sem=send_sems_ref.at[i],
                                              recv_sem=recv_sems_ref.at[i], device_id=dev, device_id_type=MESH)
            cp.wait_send()
            cp.wait_recv()

    outs = pl.pallas_call(
        body, name=name, out_shape=tuple(pltpu.HBM(b.shape, b.dtype) for b in bufs),
        in_specs=[HBM_SPEC] * n + [SEM_SPEC, SEM_SPEC, ANY], out_specs=tuple([HBM_SPEC] * n),
        input_output_aliases={k: k for k in range(n)},
        compiler_params=pltpu.CompilerParams(has_side_effects=SPLIT_EFFECT),
    )(*bufs, send_sems, recv_sems, after)
    return list(outs)


def _fill_own_slot(shard, me_arr, *, name):
    r, c = shard.shape
    tile, steps, imap = _elementwise_tile(r, c)

    def body(me_ref, x_ref, o_ref):
        o_ref[0] = x_ref[...].astype(_MXU)

    return pl.pallas_call(
        body, name=name,
        grid_spec=pltpu.PrefetchScalarGridSpec(
            num_scalar_prefetch=1, grid=(steps,),
            in_specs=[pl.BlockSpec(tile, lambda i, me: imap(i))],
            out_specs=pl.BlockSpec((1,) + tile, lambda i, me: (me[0],) + imap(i))),
        out_shape=jax.ShapeDtypeStruct((N_SHARDS, r, c), _MXU),
        compiler_params=_cparams("parallel"),
    )(me_arr, shard)


def _swap_with_sibling(parts, *, name):
    n = len(parts)

    def body(*refs):
        ins, outs = refs[:n], refs[n:2 * n]
        send_sems, recv_sems = refs[2 * n:]
        x, y, c = _my_place()
        copies = [pltpu.make_async_remote_copy(
            src_ref=ins[k], dst_ref=outs[k], send_sem=send_sems.at[k], recv_sem=recv_sems.at[k],
            device_id=(x, y, 1 - c), device_id_type=MESH) for k in range(n)]
        for cp in copies:
            cp.start()
        for cp in copies:
            cp.wait()

    return pl.pallas_call(
        body, name=name, in_specs=[ANY] * n, out_specs=[ANY] * n,
        out_shape=[jax.ShapeDtypeStruct(a.shape, a.dtype) for a in parts],
        scratch_shapes=[pltpu.SemaphoreType.DMA((n,)), pltpu.SemaphoreType.DMA((n,))],
    )(*parts)


def _other_devices(x, y, c):
    out = []
    for mask in range(1, N_DEV):
        px, py, pc = x ^ (mask >> 2 & 1), y ^ (mask >> 1 & 1), c ^ (mask & 1)
        out.append(((px, py, pc), 4 * px + 2 * py + pc))
    return out


def _pieces_plan(bufs, x, y, c, incoming):
    pack, land = bufs
    me = 4 * x + 2 * y + c
    return [(pack.at[num], land.at[num if incoming else me], dev) for dev, num in _other_devices(x, y, c)]


def _spread_plan(bufs, x, y, c, incoming):
    piece, land = bufs
    me = 4 * x + 2 * y + c
    return [(piece, land.at[num if incoming else me], dev) for dev, num in _other_devices(x, y, c)]


def _sum_pieces(pack, land, dev_arr, *, name):
    def body(dev_ref, pack_ref, land_ref, o_ref):
        dev = dev_ref[0]
        own = pack_ref[dev]
        acc = None
        for d in range(N_DEV):
            term = jnp.where(dev == d, own, land_ref[d])
            acc = term if acc is None else acc + term
        o_ref[...] = acc

    vmem = pl.BlockSpec(memory_space=pltpu.VMEM)
    return pl.pallas_call(
        body, name=name, in_specs=[pl.BlockSpec(memory_space=pltpu.SMEM), vmem, vmem], out_specs=vmem,
        out_shape=jax.ShapeDtypeStruct(pack.shape[1:], F32),
    )(dev_arr, pack, land)


def _join_pieces(piece, land, dev_arr, *, name):
    def body(dev_ref, piece_ref, land_ref, o_ref):
        dev = dev_ref[0]
        for d in range(N_DEV):
            o_ref[d] = jnp.where(dev == d, piece_ref[...], land_ref[d])

    vmem = pl.BlockSpec(memory_space=pltpu.VMEM)
    return pl.pallas_call(
        body, name=name, in_specs=[pl.BlockSpec(memory_space=pltpu.SMEM), vmem, vmem], out_specs=vmem,
        out_shape=jax.ShapeDtypeStruct(land.shape, F32),
    )(dev_arr, piece, land)


def _adamw_native(ws, gs, ms, vs):
    n = len(ws)

    def body(*refs):
        for k in range(n):
            w_ref, g_ref, m_ref, v_ref = (refs[j * n + k] for j in range(4))
            delta, m_new, v_new = _adamw_math(w_ref[...], g_ref[...], m_ref[...], v_ref[...])
            refs[4 * n + k][...] = delta
            refs[5 * n + k][...] = m_new
            refs[6 * n + k][...] = v_new

    vmem = pl.BlockSpec(memory_space=pltpu.VMEM)
    shapes = [jax.ShapeDtypeStruct(a.shape, F32) for a in ws]
    outs = pl.pallas_call(
        body, name="adamw_small", in_specs=[vmem] * (4 * n), out_specs=[vmem] * (3 * n), out_shape=shapes * 3,
        compiler_params=pltpu.CompilerParams(vmem_limit_bytes=VMEM_LIMIT_BYTES),
    )(*ws, *gs, *ms, *vs)
    return outs[:n], outs[n:2 * n], outs[2 * n:]


def _elementwise_tile(rows, cols, limit=256):
    for t in range(limit, 15, -16):
        if rows % t == 0:
            return (t, cols), rows // t, lambda i: (i, 0)
    assert cols % limit == 0
    return (rows, limit), cols // limit, lambda i: (0, i)


def _partial_sum(g4, land, me_arr, *, name):
    _, r, c = g4.shape
    tile, steps, imap = _elementwise_tile(r, c)

    def body(me_ref, own_ref, land_ref, o_ref):
        acc = own_ref[0]
        for j in range(3):
            acc = acc + land_ref[j].astype(F32)
        o_ref[...] = acc

    return pl.pallas_call(
        body, name=name,
        grid_spec=pltpu.PrefetchScalarGridSpec(
            num_scalar_prefetch=1, grid=(steps,),
            in_specs=[pl.BlockSpec((1,) + tile, lambda i, me: (me[0],) + imap(i)),
                      pl.BlockSpec((3,) + tile, lambda i, me: (0,) + imap(i))],
            out_specs=pl.BlockSpec(tile, lambda i, me: imap(i))),
        out_shape=jax.ShapeDtypeStruct((r, c), F32),
        compiler_params=_cparams("parallel"),
    )(me_arr, g4, land)


def _adamw_math(w, g, m, v):
    m = ADAM_B1 * m + (1.0 - ADAM_B1) * g
    v = ADAM_B2 * v + (1.0 - ADAM_B2) * (g * g)
    m_hat = m / (1.0 - ADAM_B1 ** ADAM_STEP)
    v_hat = v / (1.0 - ADAM_B2 ** ADAM_STEP)
    delta = -ADAM_LR * (m_hat / (jnp.sqrt(v_hat) + ADAM_EPS) + ADAM_WD * w)
    return delta, m, v


def _adamw(w, grad_parts, m, v, *, name):
    r, c = w.shape
    tile_shape, steps, imap = _elementwise_tile(r, c)
    n = len(grad_parts)

    def body(*refs):
        w_ref, m_ref, v_ref = refs[:3]
        g_refs = refs[3:3 + n]
        g_out, d_out, m_out, v_out = refs[3 + n:]
        g = g_refs[0][...]
        for k in range(1, n):
            g = g + g_refs[k][...]
        delta, m_new, v_new = _adamw_math(w_ref[...], g, m_ref[...], v_ref[...])
        g_out[...] = g
        d_out[...] = delta
        m_out[...] = m_new
        v_out[...] = v_new

    tile = pl.BlockSpec(tile_shape, imap)
    return pl.pallas_call(
        body, name=name, grid=(steps,), in_specs=[tile] * (3 + n), out_specs=[tile] * 4,
        out_shape=[jax.ShapeDtypeStruct((r, c), F32)] * 4,
        compiler_params=_cparams("parallel"),
    )(w, m, v, *grad_parts)


WEIGHT_NAMES = ("meta_tokens", "norm1_w", "w_in", "ssd_conv_w", "ssd_conv_b", "ssd_dt_bias", "ssd_a_log", "ssd_d",
                "ssd_norm_w", "lru_conv_w", "lru_conv_b", "lru_wa", "lru_ba", "lru_wx", "lru_bx", "lru_lambda",
                "lru_norm_w", "w_out", "norm2_w", "w_gate", "w_up", "w_down", "final_norm_w")
BIG = ("w_in", "w_out", "w_gate", "w_up", "w_down")
FFN = ("w_gate", "w_up", "w_down")
LATE = ("w_out",) + FFN
SMALL_SHARDED = {"meta_tokens": (N_META, D_MODEL), "ssd_conv_w": (CONV_K, 1536), "lru_conv_w": (CONV_K, LRU_WIDTH)}
SMALL = tuple(n for n in WEIGHT_NAMES if n not in BIG)
PACK_COLS = 1024


def _pack(arrays, row_multiple):
    flat = jnp.concatenate([a.reshape(-1) for a in arrays])
    rows = -(-flat.shape[0] // (row_multiple * PACK_COLS)) * row_multiple
    return jnp.pad(flat, (0, rows * PACK_COLS - flat.shape[0])).reshape(rows, PACK_COLS)


def _unpack(pack, shapes):
    flat = pack.reshape(-1)
    out, off = [], 0
    for s in shapes:
        size = math.prod(s)
        out.append(flat[off:off + size].reshape(s))
        off += size
    return out


def _unshard_cols(g4):
    return jnp.swapaxes(g4, 0, 1).reshape(g4.shape[1], -1)


COL_SHARDED = ("w_in", "w_gate", "w_up")
IN_ROWS = {"z": (0, 1024), "xs": (1024, 2048), "bc": (2048, 2560), "dt": (2560, 2576), "g": (2576, 3600),
           "x": (3600, IN_COLS)}


def _rows_view(name, block):
    return jnp.swapaxes(block[0], 0, 1) if name in COL_SHARDED else block[0]


def _param_view(name, rows):
    return (jnp.swapaxes(rows, 0, 1) if name in COL_SHARDED else rows)[None]


def kernel(x, meta_tokens, norm1_w, w_in, ssd_conv_w, ssd_conv_b, ssd_dt_bias, ssd_a_log, ssd_d, ssd_norm_w, lru_conv_w, lru_conv_b, lru_wa, lru_ba, lru_wx, lru_bx, lru_lambda, lru_norm_w, w_out, norm2_w, w_gate, w_up, w_down, final_norm_w, loss_target, m_meta_tokens, m_norm1_w, m_w_in, m_ssd_conv_w, m_ssd_conv_b, m_ssd_dt_bias, m_ssd_a_log, m_ssd_d, m_ssd_norm_w, m_lru_conv_w, m_lru_conv_b, m_lru_wa, m_lru_ba, m_lru_wx, m_lru_bx, m_lru_lambda, m_lru_norm_w, m_w_out, m_norm2_w, m_w_gate, m_w_up, m_w_down, m_final_norm_w, v_meta_tokens, v_norm1_w, v_w_in, v_ssd_conv_w, v_ssd_conv_b, v_ssd_dt_bias, v_ssd_a_log, v_ssd_d, v_ssd_norm_w, v_lru_conv_w, v_lru_conv_b, v_lru_wa, v_lru_ba, v_lru_wx, v_lru_bx, v_lru_lambda, v_lru_norm_w, v_w_out, v_norm2_w, v_w_gate, v_w_up, v_w_down, v_final_norm_w):
    w = dict(zip(WEIGHT_NAMES, (meta_tokens, norm1_w, w_in, ssd_conv_w, ssd_conv_b, ssd_dt_bias, ssd_a_log, ssd_d, ssd_norm_w, lru_conv_w, lru_conv_b, lru_wa, lru_ba, lru_wx, lru_bx, lru_lambda, lru_norm_w, w_out, norm2_w, w_gate, w_up, w_down, final_norm_w)))
    m = dict(zip(WEIGHT_NAMES, (m_meta_tokens, m_norm1_w, m_w_in, m_ssd_conv_w, m_ssd_conv_b, m_ssd_dt_bias, m_ssd_a_log, m_ssd_d, m_ssd_norm_w, m_lru_conv_w, m_lru_conv_b, m_lru_wa, m_lru_ba, m_lru_wx, m_lru_bx, m_lru_lambda, m_lru_norm_w, m_w_out, m_norm2_w, m_w_gate, m_w_up, m_w_down, m_final_norm_w)))
    v = dict(zip(WEIGHT_NAMES, (v_meta_tokens, v_norm1_w, v_w_in, v_ssd_conv_w, v_ssd_conv_b, v_ssd_dt_bias, v_ssd_a_log, v_ssd_d, v_ssd_norm_w, v_lru_conv_w, v_lru_conv_b, v_lru_wa, v_lru_ba, v_lru_wx, v_lru_bx, v_lru_lambda, v_lru_norm_w, v_w_out, v_norm2_w, v_w_gate, v_w_up, v_w_down, v_final_norm_w)))
    me = 2 * lax.axis_index("x") + lax.axis_index("y")

    big2d = {n: _rows_view(n, w[n]) for n in BIG}
    small_local = jnp.concatenate([w["meta_tokens"].reshape(-1), w["ssd_conv_w"].reshape(-1),
                                   w["lru_conv_w"].reshape(-1)])[None]
    me_arr = me.astype(jnp.int32).reshape(1)
    dev_arr = (2 * me + lax.axis_index("c")).astype(jnp.int32).reshape(1)
    w_in4, small4 = _gather_first(big2d["w_in"].astype(_MXU), small_local)
    w_in_full = w_in4.reshape(-1, D_MODEL)
    sm = small4[:, 0]
    meta_full = _unshard_cols(sm[:, :4096].reshape(N_SHARDS, N_META, 256))
    ssd_conv_w_full = _unshard_cols(sm[:, 4096:5632].reshape(N_SHARDS, CONV_K, 384))
    lru_conv_w_full = _unshard_cols(sm[:, 5632:].reshape(N_SHARDS, CONV_K, 256))
    slots = {n: _fill_own_slot(big2d[n], me_arr, name="own_slot_" + n) for n in LATE}
    out_send, out_recv, out_bufs, tok_a = _split_start([slots["w_out"]], _gather_plan, 3, small4,
                                                       name="gather_w_out_start")
    ffn_send, ffn_recv, ffn_bufs, tok_b = _split_start([slots[n] for n in FFN], _gather_plan, 9, tok_a,
                                                       name="gather_ffn_start")

    p = {"w_in_" + s: w_in_full[lo:hi] for s, (lo, hi) in IN_ROWS.items()}
    p["w_in_dt"] = jnp.pad(p["w_in_dt"], ((0, SEC_WIDTH["dt"] - SSD_HEADS), (0, 0)))
    p.update({"ssd_conv_w": ssd_conv_w_full, "lru_conv_w": lru_conv_w_full,
              "lru_wa": w["lru_wa"][0], "lru_wx": w["lru_wx"][0], "final_norm_w": w["final_norm_w"][None]})
    for n in ("norm1_w", "ssd_conv_b", "ssd_dt_bias", "ssd_a_log", "ssd_d", "ssd_norm_w", "lru_conv_b", "lru_ba",
              "lru_bx", "lru_lambda", "lru_norm_w", "norm2_w"):
        p[n] = w[n]
    p["norm1_w"] = p["norm1_w"] + tok_b[:1, :1]

    class Late:
        def __init__(self):
            self.pending = []

        def w_out(self, after):
            (buf,) = _split_wait(out_bufs, out_send, out_recv, _gather_plan, after, name="gather_w_out_wait")
            return buf.reshape(-1, D_MODEL)

        def ffn(self, after):
            bufs = _split_wait(ffn_bufs, ffn_send, ffn_recv, _gather_plan, after, name="gather_ffn_wait")
            return tuple(b.reshape(-1, D_MODEL) for b in bufs)

        def grads_ready(self, names, g, g_mxu):
            srcs = [g_mxu[n].reshape(N_SHARDS, -1, D_MODEL) for n in names]
            lands = [lax.empty((3,) + s.shape[1:], _MXU) for s in srcs]
            tag = "_".join(names)
            send, recv, bufs, tok = _split_start(srcs + lands, _scatter_plan, 3 * len(names), g[names[-1]],
                                                 name="scatter_" + tag + "_start")
            self.pending.append((names, send, recv, bufs, tag))
            self.in_flight = bufs[0]
            return tok[:1, :1]

        def landed(self, after, which):
            land = {}
            for names, send, recv, bufs, tag in self.pending:
                if names[0] in which:
                    bufs = _split_wait(bufs, send, recv, _scatter_plan, after, name="scatter_" + tag + "_wait")
                    land.update(zip(names, bufs[len(names):]))
            return land

        def small_ready(self, g, loss):
            pack = _pack([g[n] for n in SMALL] + [loss[0, :1]], 8 * N_DEV)
            pack = pack.reshape(N_DEV, -1, PACK_COLS)
            self.small = _split_start([pack, lax.empty(pack.shape, F32)], _pieces_plan, N_DEV - 1, loss,
                                      name="small_pieces_start")
            return self.small[3]

        def small_middle(self, after):
            send, recv, bufs, _ = self.small
            pack, land = _split_wait(bufs, send, recv, _pieces_plan, after, name="small_pieces_wait")
            piece = _sum_pieces(pack, land, dev_arr, name="small_pieces_sum")
            self.small = _split_start([piece, lax.empty(pack.shape, F32)], _spread_plan, N_DEV - 1, None,
                                      name="small_spread_start")
            return self.small[3]

        def small_sum(self, after):
            send, recv, bufs, _ = self.small
            piece, land = _split_wait(bufs, send, recv, _spread_plan, after, name="small_spread_wait")
            return _join_pieces(piece, land, dev_arr, name="small_join")

    late = Late()

    loss, grad_x, g, g_mxu = _local_step(x[0], loss_target[0], meta_full, p, late)

    g["w_in"] = jnp.concatenate([g["w_in_" + s][:hi - lo] for s, (lo, hi) in IN_ROWS.items()], axis=0)
    g_mxu["w_in"] = jnp.concatenate([g_mxu["w_in_" + s][:hi - lo] for s, (lo, hi) in IN_ROWS.items()], axis=0)
    g4 = {n: g[n].reshape(N_SHARDS, -1, D_MODEL) for n in BIG}
    late.grads_ready(("w_in",), g, g_mxu)
    land = late.landed(late.in_flight, LATE)
    part = {n: _partial_sum(g4[n], land[n], me_arr, name="partial_" + n) for n in LATE}
    sib = dict(zip(LATE, _swap_with_sibling([part[n] for n in LATE], name="swap_late")))

    small_full_shape = {n: (SMALL_SHARDED[n] if n in SMALL_SHARDED else w[n].shape) for n in SMALL}
    red_list = _unpack(late.small_sum(sib["w_out"]), [small_full_shape[n] for n in SMALL] + [(1,)])
    loss_total = red_list[-1][0]
    g_small = {}
    for n, arr in zip(SMALL, red_list[:-1]):
        if n in SMALL_SHARDED:
            cols = SMALL_SHARDED[n][1] // N_SHARDS
            arr = lax.dynamic_slice_in_dim(arr, me * cols, cols, axis=1)
        g_small[n] = arr.reshape(w[n].shape)

    grad, delta, new_m, new_v = {}, {}, {}, {}

    def update_big(n):
        outs = _adamw(big2d[n], [part[n], sib[n]], _rows_view(n, m[n]), _rows_view(n, v[n]), name="adamw_" + n)
        grad[n], delta[n], new_m[n], new_v[n] = (_param_view(n, o) for o in outs)
        return outs[0]

    two_d = lambda a: a.reshape(1, -1) if a.ndim == 1 else a
    deltas, new_ms, new_vs = _adamw_native(*[[two_d(d[n]) for n in SMALL] for d in (w, g_small, m, v)])
    for n, dn, mn, vn in zip(SMALL, deltas, new_ms, new_vs):
        grad[n], delta[n], new_m[n], new_v[n] = (g_small[n], dn.reshape(w[n].shape), mn.reshape(w[n].shape),
                                                 vn.reshape(w[n].shape))
    for n in LATE:
        last = update_big(n)
    land.update(late.landed(last, ("w_in",)))
    part["w_in"] = _partial_sum(g4["w_in"], land["w_in"], me_arr, name="partial_w_in")
    (sib["w_in"],) = _swap_with_sibling([part["w_in"]], name="swap_w_in")
    update_big("w_in")

    return (loss_total, grad_x[None], *[grad[n] for n in WEIGHT_NAMES], *[delta[n] for n in WEIGHT_NAMES],
            *[new_m[n] for n in WEIGHT_NAMES], *[new_v[n] for n in WEIGHT_NAMES])
```

```python
import functools
import math

import jax
import jax.numpy as jnp
from jax import lax
from jax.experimental import pallas as pl
from jax.experimental.pallas import tpu as pltpu

F32 = jnp.float32
_MXU = jnp.bfloat16

D_MODEL = 1024
SEQ = 2048
N_META = 16
CHUNK = 128
T_ROWS = 2176
N_CHUNKS = T_ROWS // CHUNK
PAD_ROWS = T_ROWS - SEQ - N_META
X_ROW0 = PAD_ROWS + N_META
SSD_HEADS = 16
SSD_HEAD_DIM = 64
SSD_STATE = 128
SSD_GROUPS = 2
SSD_HPG = SSD_HEADS // SSD_GROUPS
SSD_WIDTH = 1024
LRU_WIDTH = 1024
LRU_C = 8.0
D_FF = 2816
EPS = 1e-6
IN_COLS = 4624
N_SHARDS = 4
N_DEV = 8

ADAM_LR = 0.001
ADAM_B1 = 0.9
ADAM_B2 = 0.999
ADAM_EPS = 1e-08
ADAM_WD = 0.01
ADAM_STEP = 10

VMEM_LIMIT_BYTES = 56 * 1024 * 1024

NN = (((1,), (0,)), ((), ()))
NT = (((1,), (1,)), ((), ()))
TN = (((0,), (0,)), ((), ()))


def _cparams(*sem):
    return pltpu.CompilerParams(dimension_semantics=sem, vmem_limit_bytes=VMEM_LIMIT_BYTES)


def _dot(a, b, dims=NN):
    return lax.dot_general(a.astype(_MXU), b.astype(_MXU), dims, preferred_element_type=F32)


def _dot_onehot(a, b, dims=NN, *, data=0):
    ops = [a, b]
    mask = ops[1 - data].astype(jnp.bfloat16)
    rest = ops[data]
    acc = None
    for _ in range(3):
        piece = rest.astype(jnp.bfloat16)
        ops[data], ops[1 - data] = piece, mask
        d = lax.dot_general(ops[0], ops[1], dims, preferred_element_type=F32)
        acc = d if acc is None else acc + d
        rest = rest - piece.astype(F32)
    return acc


def _sigmoid(x):
    return 0.5 * (1.0 + jnp.tanh(0.5 * x))


def _softplus(x):
    return jnp.maximum(x, 0.0) + jnp.log(1.0 + jnp.exp(-jnp.abs(x)))


def _silu(x):
    return x * _sigmoid(x)


def _silu_grad(x):
    s = _sigmoid(x)
    return s * (1.0 + x * (1.0 - s))


_GELU_C = math.sqrt(2.0 / math.pi)


def _gelu_and_grad(x):
    inner = _GELU_C * (x + 0.044715 * x * x * x)
    t = jnp.tanh(inner)
    g = 0.5 * x * (1.0 + t)
    dg = 0.5 * (1.0 + t) + 0.5 * x * (1.0 - t * t) * _GELU_C * (1.0 + 3.0 * 0.044715 * x * x)
    return g, dg


def _rms_fwd(x, w):
    rstd = lax.rsqrt(jnp.mean(x * x, axis=-1, keepdims=True) + EPS)
    return x * rstd * w


def _rms_bwd(x, w, dy):
    rstd = lax.rsqrt(jnp.mean(x * x, axis=-1, keepdims=True) + EPS)
    xhat = x * rstd
    dxhat = dy * w
    dx = rstd * (dxhat - xhat * jnp.mean(dxhat * xhat, axis=-1, keepdims=True))
    return dx, dy * xhat


def _mm(terms, m, n, *, tm, tn, mode, out_dtype, name, residual=None, n_outer=False, also_mxu=False, behind=()):
    gm, gn = m // tm, n // tn
    assert gm * tm == m and gn * tn == n
    if n_outer:
        grid = (gn, gm)
        mi = lambda g0, g1: g1
        ni = lambda g0, g1: g0
    else:
        grid = (gm, gn)
        mi = lambda g0, g1: g0
        ni = lambda g0, g1: g1
    in_specs, args = [], []
    for (a, ka, b, kb, k) in terms:
        if mode == "tn":
            in_specs.append(pl.BlockSpec((k, tm), lambda g0, g1, ka=ka: (ka, mi(g0, g1))))
        else:
            in_specs.append(pl.BlockSpec((tm, k), lambda g0, g1, ka=ka: (mi(g0, g1), ka)))
        if mode == "nt":
            in_specs.append(pl.BlockSpec((tn, k), lambda g0, g1, kb=kb: (ni(g0, g1), kb)))
        else:
            in_specs.append(pl.BlockSpec((k, tn), lambda g0, g1, kb=kb: (kb, ni(g0, g1))))
        args += [a, b]
    if residual is not None:
        in_specs.append(pl.BlockSpec((tm, tn), lambda g0, g1: (mi(g0, g1), ni(g0, g1))))
        args.append(residual)
    dims = {"nn": NN, "nt": NT, "tn": TN}[mode]
    n_terms = len(terms)
    has_res = residual is not None
    in_specs += [pl.BlockSpec(memory_space=pl.ANY)] * len(behind)
    args += list(behind)
    n_in = len(args)

    def body(*refs):
        acc = None
        for t in range(n_terms):
            d = lax.dot_general(refs[2 * t][...], refs[2 * t + 1][...], dims, preferred_element_type=F32)
            acc = d if acc is None else acc + d
        if has_res:
            acc = acc + refs[2 * n_terms][...]
        refs[n_in][...] = acc.astype(out_dtype)
        if also_mxu:
            refs[n_in + 1][...] = acc.astype(_MXU)

    tile = pl.BlockSpec((tm, tn), lambda g0, g1: (mi(g0, g1), ni(g0, g1)))
    shape = jax.ShapeDtypeStruct((m, n), out_dtype)
    return pl.pallas_call(
        body, name=name, grid=grid, in_specs=in_specs,
        out_specs=[tile, tile] if also_mxu else tile,
        out_shape=[shape, jax.ShapeDtypeStruct((m, n), _MXU)] if also_mxu else shape,
        compiler_params=_cparams("parallel", "parallel"),
    )(*args)


def _embed(x, meta):
    def body(x_ref, meta_ref, o_ref):
        i = pl.program_id(0)

        @pl.when(i == 0)
        def _():
            o_ref[0:PAD_ROWS, :] = jnp.zeros((PAD_ROWS, D_MODEL), F32)
            o_ref[PAD_ROWS:CHUNK, :] = meta_ref[...]

        @pl.when(i > 0)
        def _():
            o_ref[...] = x_ref[...]

    return pl.pallas_call(
        body, name="embed", grid=(N_CHUNKS,),
        in_specs=[pl.BlockSpec((CHUNK, D_MODEL), lambda i: (jnp.maximum(i - 1, 0), 0)),
                  pl.BlockSpec((N_META, D_MODEL), lambda i: (0, 0))],
        out_specs=pl.BlockSpec((CHUNK, D_MODEL), lambda i: (i, 0)),
        out_shape=jax.ShapeDtypeStruct((T_ROWS, D_MODEL), F32),
        compiler_params=_cparams("parallel"),
    )(x, meta)


def _rmsnorm(h, w, *, name, tm=544):
    def body(h_ref, w_ref, o_ref):
        o_ref[...] = _rms_fwd(h_ref[...], w_ref[...]).astype(_MXU)

    return pl.pallas_call(
        body, name=name, grid=(T_ROWS // tm,),
        in_specs=[pl.BlockSpec((tm, D_MODEL), lambda i: (i, 0)), pl.BlockSpec((1, D_MODEL), lambda i: (0, 0))],
        out_specs=pl.BlockSpec((tm, D_MODEL), lambda i: (i, 0)),
        out_shape=jax.ShapeDtypeStruct((T_ROWS, D_MODEL), _MXU),
        compiler_params=_cparams("parallel"),
    )(h, w)


def _norm_proj(h, w, sections, *, name, tm=544):
    widths = [s.shape[0] for s in sections]
    n = len(sections)

    def body(*refs):
        h_ref, w_ref = refs[:2]
        u_ref = refs[2 + n]
        u = _rms_fwd(h_ref[...], w_ref[...]).astype(_MXU)
        u_ref[...] = u
        for k in range(n):
            refs[3 + n + k][...] = lax.dot_general(u, refs[2 + k][...], NT, preferred_element_type=F32)

    row = lambda width: pl.BlockSpec((tm, width), lambda i: (i, 0))
    outs = pl.pallas_call(
        body, name=name, grid=(T_ROWS // tm,),
        in_specs=[row(D_MODEL), pl.BlockSpec((1, D_MODEL), lambda i: (0, 0))]
        + [pl.BlockSpec((wd, D_MODEL), lambda i: (0, 0)) for wd in widths],
        out_specs=[row(D_MODEL)] + [row(wd) for wd in widths],
        out_shape=[jax.ShapeDtypeStruct((T_ROWS, D_MODEL), _MXU)]
        + [jax.ShapeDtypeStruct((T_ROWS, wd), F32) for wd in widths],
        compiler_params=_cparams("parallel"),
    )(h, w, *sections)
    return outs[0], list(outs[1:])


def _loss_head(h2, target, fw):
    def body(h_ref, t_ref, w_ref, loss_ref, dh_ref, dhb_ref, dw_ref, acc_ref):
        i = pl.program_id(0)

        @pl.when(i == 0)
        def _():
            acc_ref[...] = jnp.zeros_like(acc_ref)
            dw_ref[...] = jnp.zeros_like(dw_ref)

        h = h_ref[...]
        w = w_ref[...]
        y = _rms_fwd(h, w)
        live = (i > 0).astype(F32)
        err = (y - t_ref[...]) * live
        acc_ref[...] += jnp.sum(err * err, axis=0, keepdims=True)
        dy = err * (1.0 / D_MODEL)
        dx, dwr = _rms_bwd(h, w, dy)
        dh_ref[...] = dx
        dhb_ref[...] = dx.astype(_MXU)
        dw_ref[...] += jnp.sum(dwr, axis=0, keepdims=True)

        @pl.when(i == N_CHUNKS - 1)
        def _():
            tot = jnp.sum(acc_ref[...], axis=1, keepdims=True) * (0.5 / D_MODEL)
            loss_ref[...] = jnp.broadcast_to(tot, (1, 128))

    return pl.pallas_call(
        body, name="loss_head", grid=(N_CHUNKS,),
        in_specs=[pl.BlockSpec((CHUNK, D_MODEL), lambda i: (i, 0)),
                  pl.BlockSpec((CHUNK, D_MODEL), lambda i: (jnp.maximum(i - 1, 0), 0)),
                  pl.BlockSpec((1, D_MODEL), lambda i: (0, 0))],
        out_specs=[pl.BlockSpec((1, 128), lambda i: (0, 0)),
                   pl.BlockSpec((CHUNK, D_MODEL), lambda i: (i, 0)),
                   pl.BlockSpec((CHUNK, D_MODEL), lambda i: (i, 0)),
                   pl.BlockSpec((1, D_MODEL), lambda i: (0, 0))],
        out_shape=[jax.ShapeDtypeStruct((1, 128), F32),
                   jax.ShapeDtypeStruct((T_ROWS, D_MODEL), F32),
                   jax.ShapeDtypeStruct((T_ROWS, D_MODEL), _MXU),
                   jax.ShapeDtypeStruct((1, D_MODEL), F32)],
        scratch_shapes=[pltpu.VMEM((1, D_MODEL), F32)],
        compiler_params=_cparams("arbitrary"),
    )(h2, target, fw)


def _mm_norm_bwd(terms, h, w, dres, *, name, tm=544):
    n_terms = len(terms)
    in_specs, args = [], []
    for (a, b, k) in terms:
        in_specs += [pl.BlockSpec((tm, k), lambda i: (i, 0)), pl.BlockSpec((k, D_MODEL), lambda i: (0, 0))]
        args += [a, b]
    in_specs += [pl.BlockSpec((tm, D_MODEL), lambda i: (i, 0)), pl.BlockSpec((1, D_MODEL), lambda i: (0, 0)),
                 pl.BlockSpec((tm, D_MODEL), lambda i: (i, 0))]
    args += [h, w, dres]

    def body(*refs):
        h_ref, w_ref, dres_ref, dh_ref, dhb_ref, dw_ref = refs[2 * n_terms:]

        @pl.when(pl.program_id(0) == 0)
        def _():
            dw_ref[...] = jnp.zeros_like(dw_ref)

        du = None
        for t in range(n_terms):
            d = lax.dot_general(refs[2 * t][...], refs[2 * t + 1][...], NN, preferred_element_type=F32)
            du = d if du is None else du + d
        dx, dwr = _rms_bwd(h_ref[...], w_ref[...], du)
        dh = dres_ref[...] + dx
        dh_ref[...] = dh
        dhb_ref[...] = dh.astype(_MXU)
        dw_ref[...] += jnp.sum(dwr, axis=0, keepdims=True)

    return pl.pallas_call(
        body, name=name, grid=(T_ROWS // tm,), in_specs=in_specs,
        out_specs=[pl.BlockSpec((tm, D_MODEL), lambda i: (i, 0)), pl.BlockSpec((tm, D_MODEL), lambda i: (i, 0)),
                   pl.BlockSpec((1, D_MODEL), lambda i: (0, 0))],
        out_shape=[jax.ShapeDtypeStruct((T_ROWS, D_MODEL), F32), jax.ShapeDtypeStruct((T_ROWS, D_MODEL), _MXU),
                   jax.ShapeDtypeStruct((1, D_MODEL), F32)],
        compiler_params=_cparams("arbitrary"),
    )(*args)


FFN_TM = T_ROWS
FFN_TN = 256


def _ffn_up(u2, wg_t, wu_t):
    def body(u_ref, wg_ref, wu_ref, gp_ref, up_ref, act_ref):
        u = u_ref[...]
        gp = lax.dot_general(u, wg_ref[...], NT, preferred_element_type=F32)
        up = lax.dot_general(u, wu_ref[...], NT, preferred_element_type=F32)
        gp_ref[...] = gp
        up_ref[...] = up
        act_ref[...] = (_silu(gp) * up).astype(_MXU)

    tile = pl.BlockSpec((FFN_TM, FFN_TN), lambda j, i: (i, j))
    return pl.pallas_call(
        body, name="ffn_up", grid=(D_FF // FFN_TN, T_ROWS // FFN_TM),
        in_specs=[pl.BlockSpec((FFN_TM, D_MODEL), lambda j, i: (i, 0)),
                  pl.BlockSpec((FFN_TN, D_MODEL), lambda j, i: (j, 0)),
                  pl.BlockSpec((FFN_TN, D_MODEL), lambda j, i: (j, 0))],
        out_specs=[tile, tile, tile],
        out_shape=[jax.ShapeDtypeStruct((T_ROWS, D_FF), F32), jax.ShapeDtypeStruct((T_ROWS, D_FF), F32),
                   jax.ShapeDtypeStruct((T_ROWS, D_FF), _MXU)],
        compiler_params=_cparams("parallel", "parallel"),
    )(u2, wg_t, wu_t)


def _ffn_bwd_act(dh2b, wd, gp, up):
    def body(dh_ref, wd_ref, gp_ref, up_ref, dgp_ref, dup_ref):
        dact = lax.dot_general(dh_ref[...], wd_ref[...], NT, preferred_element_type=F32)
        gp = gp_ref[...]
        dgp_ref[...] = (dact * up_ref[...] * _silu_grad(gp)).astype(_MXU)
        dup_ref[...] = (dact * _silu(gp)).astype(_MXU)

    tile = pl.BlockSpec((FFN_TM, FFN_TN), lambda j, i: (i, j))
    return pl.pallas_call(
        body, name="ffn_bwd_act", grid=(D_FF // FFN_TN, T_ROWS // FFN_TM),
        in_specs=[pl.BlockSpec((FFN_TM, D_MODEL), lambda j, i: (i, 0)),
                  pl.BlockSpec((FFN_TN, D_MODEL), lambda j, i: (j, 0)), tile, tile],
        out_specs=[tile, tile],
        out_shape=[jax.ShapeDtypeStruct((T_ROWS, D_FF), _MXU), jax.ShapeDtypeStruct((T_ROWS, D_FF), _MXU)],
        compiler_params=_cparams("parallel", "parallel"),
    )(dh2b, wd, gp, up)


CONV_TC = 512
CONV_K = 4


def _conv_pre(x_ref, wv, bv, c):
    tc = wv.shape[1]
    r0 = c * CHUNK
    cur = x_ref[r0:r0 + CHUNK, :]
    if c == 0:
        cat = jnp.concatenate([jnp.zeros((8, tc), F32), cur], axis=0)
        shifted = [cur] + [pltpu.roll(cat, s, 0)[8:8 + CHUNK] for s in range(1, CONV_K)]
    else:
        shifted = [cur] + [x_ref[r0 - s:r0 - s + CHUNK, :] for s in range(1, CONV_K)]
    pre = bv
    for s in range(CONV_K):
        pre = pre + shifted[s] * wv[CONV_K - 1 - s:CONV_K - s]
    return pre, shifted


def _row_mask(c):
    if c > 0:
        return None
    return (lax.broadcasted_iota(jnp.int32, (CHUNK, 1), 0) >= PAD_ROWS).astype(F32)


def _conv_fwd(x, w, b, *, silu, name):
    cols = x.shape[1]
    tc = min(CONV_TC, cols)

    def body(x_ref, w_ref, b_ref, o_ref):
        wv, bv = w_ref[...], b_ref[...]
        for c in range(N_CHUNKS):
            pre, _ = _conv_pre(x_ref, wv, bv, c)
            y = _silu(pre) if silu else pre
            mask = _row_mask(c)
            if mask is not None:
                y = y * mask
            o_ref[c * CHUNK:(c + 1) * CHUNK, :] = y

    return pl.pallas_call(
        body, name=name, grid=(cols // tc,),
        in_specs=[pl.BlockSpec((T_ROWS, tc), lambda j: (0, j)), pl.BlockSpec((CONV_K, tc), lambda j: (0, j)),
                  pl.BlockSpec((1, tc), lambda j: (0, j))],
        out_specs=pl.BlockSpec((T_ROWS, tc), lambda j: (0, j)),
        out_shape=jax.ShapeDtypeStruct((T_ROWS, cols), F32),
        compiler_params=_cparams("parallel"),
    )(x, w, b)


def _conv_bwd(dy, x, w, b, *, silu, name):
    cols = x.shape[1]
    tc = min(CONV_TC, cols)

    def body(dy_ref, x_ref, w_ref, b_ref, dx_ref, dw_ref, db_ref):
        wv, bv = w_ref[...], b_ref[...]
        next8 = jnp.zeros((8, tc), F32)
        dws = [jnp.zeros((1, tc), F32) for _ in range(CONV_K)]
        db = jnp.zeros((1, tc), F32)
        for c in reversed(range(N_CHUNKS)):
            r0 = c * CHUNK
            pre, shifted = _conv_pre(x_ref, wv, bv, c)
            dpre = dy_ref[r0:r0 + CHUNK, :]
            if silu:
                dpre = dpre * _silu_grad(pre)
            mask = _row_mask(c)
            if mask is not None:
                dpre = dpre * mask
            cat = jnp.concatenate([dpre, next8], axis=0)
            dx = dpre * wv[CONV_K - 1:CONV_K]
            for s in range(1, CONV_K):
                dx = dx + pltpu.roll(cat, CHUNK + 8 - s, 0)[0:CHUNK] * wv[CONV_K - 1 - s:CONV_K - s]
            dx_ref[r0:r0 + CHUNK, :] = dx.astype(_MXU)
            for s in range(CONV_K):
                k = CONV_K - 1 - s
                dws[k] = dws[k] + jnp.sum(dpre * shifted[s], axis=0, keepdims=True)
            db = db + jnp.sum(dpre, axis=0, keepdims=True)
            next8 = dpre[0:8]
        dw_ref[...] = jnp.concatenate(dws, axis=0)
        db_ref[...] = db

    return pl.pallas_call(
        body, name=name, grid=(cols // tc,),
        in_specs=[pl.BlockSpec((T_ROWS, tc), lambda j: (0, j)), pl.BlockSpec((T_ROWS, tc), lambda j: (0, j)),
                  pl.BlockSpec((CONV_K, tc), lambda j: (0, j)), pl.BlockSpec((1, tc), lambda j: (0, j))],
        out_specs=[pl.BlockSpec((T_ROWS, tc), lambda j: (0, j)), pl.BlockSpec((CONV_K, tc), lambda j: (0, j)),
                   pl.BlockSpec((1, tc), lambda j: (0, j))],
        out_shape=[jax.ShapeDtypeStruct((T_ROWS, cols), _MXU), jax.ShapeDtypeStruct((CONV_K, cols), F32),
                   jax.ShapeDtypeStruct((1, cols), F32)],
        compiler_params=_cparams("parallel"),
    )(dy, x, w, b)


def _ssd_chunk_common(dt_raw, prm, c):
    a_row = -jnp.exp(prm[1:2])
    dt = _softplus(dt_raw + prm[0:1])
    rows = lax.broadcasted_iota(jnp.int32, (CHUNK, 1), 0)
    real = jnp.logical_or(c > 0, rows >= PAD_ROWS)
    dt = jnp.where(real, dt, 0.0)
    li = lax.broadcasted_iota(jnp.int32, (CHUNK, CHUNK), 0)
    si = lax.broadcasted_iota(jnp.int32, (CHUNK, CHUNK), 1)
    causal = li >= si
    tri = causal.astype(F32)
    cs = _dot_onehot(tri, dt * a_row, data=1)
    return dt, a_row, cs, cs.T, causal, tri, real


def _gated_norm_fwd(y, z, w):
    g = y * _silu(z)
    half = SSD_WIDTH // SSD_GROUPS
    outs = [_rms_fwd(g[:, k * half:(k + 1) * half], w[:, k * half:(k + 1) * half]) for k in range(SSD_GROUPS)]
    return jnp.concatenate(outs, axis=1)


GROUP_W = SSD_WIDTH // SSD_GROUPS
PAIR_W = 2 * SSD_HEAD_DIM
STATE_SHAPE = (SSD_GROUPS, SSD_STATE, GROUP_W)


def _head_expander():
    r = lax.broadcasted_iota(jnp.int32, (128, SSD_WIDTH), 0)
    c = lax.broadcasted_iota(jnp.int32, (128, SSD_WIDTH), 1)
    return (c // SSD_HEAD_DIM == r).astype(F32)


def _ssd_expand(dt, cs, prm, ex):
    cs_x = _dot_onehot(cs, ex)
    cs_last_x = cs_x[CHUNK - 1:CHUNK, :]
    return (_dot_onehot(dt, ex), _dot_onehot(prm, ex)[2:3], jnp.exp(cs_x), jnp.exp(cs_last_x),
            jnp.exp(cs_last_x - cs_x))


def _ssd_fwd(xs, bc, dt_raw, z, prm, norm_w, ex):
    def body(xs_ref, bc_ref, dt_ref, z_ref, prm_ref, nw_ref, ex_ref, y_ref, yn_ref, prev_ref, state):
        c = pl.program_id(0)

        @pl.when(c == 0)
        def _():
            state[...] = jnp.zeros_like(state)

        prm = prm_ref[...]
        dt, a_row, cs, cs_t, causal, _, _ = _ssd_chunk_common(dt_ref[...], prm, c)
        dt_x, d_x, e_cs_x, e_last_x, dec_x = _ssd_expand(dt, cs, prm, ex_ref[...])
        xs_all = xs_ref[...]
        bc_all = bc_ref[...]
        xdt = xs_all * dt_x
        xdec = xdt * dec_x
        lane_lo = lax.broadcasted_iota(jnp.int32, (1, PAIR_W), 1) < SSD_HEAD_DIM
        for g in range(SSD_GROUPS):
            gs = slice(g * GROUP_W, (g + 1) * GROUP_W)
            b_g = bc_all[:, g * SSD_STATE:(g + 1) * SSD_STATE]
            c_g = bc_all[:, (SSD_GROUPS + g) * SSD_STATE:(SSD_GROUPS + g + 1) * SSD_STATE]
            st = state[g]
            prev_ref[0, g] = st
            y_off = _dot(c_g, st) * e_cs_x[:, gs]
            state[g] = st * e_last_x[:, gs] + _dot(b_g.T, xdec[:, gs])
            cb = _dot(c_g, b_g, NT)
            for k in range(SSD_HPG // 2):
                h0 = g * SSD_HPG + 2 * k
                ps = slice(h0 * SSD_HEAD_DIM, h0 * SSD_HEAD_DIM + PAIR_W)
                xdt_pair = xdt[:, ps]
                yd = []
                for h in (h0, h0 + 1):
                    lmat = jnp.where(causal, jnp.exp(cs[:, h:h + 1] - cs_t[h:h + 1, :]), 0.0)
                    yd.append(_dot(cb * lmat, xdt_pair))
                y_ref[:, ps] = (jnp.where(lane_lo, yd[0], yd[1]) + y_off[:, k * PAIR_W:(k + 1) * PAIR_W]
                                + xs_all[:, ps] * d_x[:, ps])
        yn_ref[...] = _gated_norm_fwd(y_ref[...], z_ref[...], nw_ref[...]).astype(_MXU)

    row = lambda w: pl.BlockSpec((CHUNK, w), lambda c: (c, 0))
    return pl.pallas_call(
        body, name="ssd_fwd", grid=(N_CHUNKS,),
        in_specs=[row(SSD_WIDTH), row(512), row(128), row(SSD_WIDTH),
                  pl.BlockSpec((8, 128), lambda c: (0, 0)), pl.BlockSpec((1, SSD_WIDTH), lambda c: (0, 0)),
                  pl.BlockSpec((128, SSD_WIDTH), lambda c: (0, 0))],
        out_specs=[row(SSD_WIDTH), row(SSD_WIDTH),
                   pl.BlockSpec((1,) + STATE_SHAPE, lambda c: (c, 0, 0, 0))],
        out_shape=[jax.ShapeDtypeStruct((T_ROWS, SSD_WIDTH), F32), jax.ShapeDtypeStruct((T_ROWS, SSD_WIDTH), _MXU),
                   jax.ShapeDtypeStruct((N_CHUNKS,) + STATE_SHAPE, F32)],
        scratch_shapes=[pltpu.VMEM(STATE_SHAPE, F32)],
        compiler_params=_cparams("arbitrary"),
    )(xs, bc, dt_raw, z, prm, norm_w, ex)


def _ssd_bwd(dyn, dyn_block, z, y_pre, xs, bc, dt_raw, prev, prm, norm_w, ex):
    def body(dyn_ref, z_ref, y_ref, xs_ref, bc_ref, dt_ref, prev_ref, prm_ref, nw_ref, ex_ref,
             dz_ref, dxs_ref, dbc_ref, ddt_ref, dprm_ref, dnw_ref, dstate):
        step = pl.program_id(0)
        c = N_CHUNKS - 1 - step

        @pl.when(step == 0)
        def _():
            dstate[...] = jnp.zeros_like(dstate)
            dprm_ref[...] = jnp.zeros_like(dprm_ref)
            dnw_ref[...] = jnp.zeros_like(dnw_ref)

        prm = prm_ref[...]
        dt, a_row, cs, cs_t, causal, tri, real = _ssd_chunk_common(dt_ref[...], prm, c)
        realf = real.astype(F32)
        z = z_ref[...]
        y_all = y_ref[...]
        nw = nw_ref[...]
        dyn_all = dyn_ref[...]
        sz = _silu(z)
        gated = y_all * sz
        half = SSD_WIDTH // SSD_GROUPS
        dgs, dnws = [], []
        for k in range(SSD_GROUPS):
            sl = slice(k * half, (k + 1) * half)
            dgk, dwk = _rms_bwd(gated[:, sl], nw[:, sl], dyn_all[:, sl])
            dgs.append(dgk)
            dnws.append(jnp.sum(dwk, axis=0, keepdims=True))
        dgated = jnp.concatenate(dgs, axis=1)
        dnw_ref[...] += jnp.concatenate(dnws, axis=1)
        dz_ref[...] = (dgated * y_all * _silu_grad(z)).astype(_MXU)
        dy_all = dgated * sz

        ex = ex_ref[...]
        dt_x, d_x, e_cs_x, e_last_x, dec_x = _ssd_expand(dt, cs, prm, ex)
        xs_all = xs_ref[...]
        bc_all = bc_ref[...]
        xdt = xs_all * dt_x
        xdt_mxu = xdt.astype(_MXU).astype(F32)
        xdec = xdt * dec_x
        dcp = dy_all * e_cs_x
        lane_lo = lax.broadcasted_iota(jnp.int32, (1, PAIR_W), 1) < SSD_HEAD_DIM
        upper = (lax.broadcasted_iota(jnp.int32, (CHUNK, CHUNK), 0)
                 <= lax.broadcasted_iota(jnp.int32, (CHUNK, CHUNK), 1))
        last_row = (lax.broadcasted_iota(jnp.int32, (CHUNK, 1), 0) == CHUNK - 1).astype(F32)
        dbs, dcs_, dxdt_parts, last_parts = [], [], [], []
        for g in range(SSD_GROUPS):
            gs = slice(g * GROUP_W, (g + 1) * GROUP_W)
            b_g = bc_all[:, g * SSD_STATE:(g + 1) * SSD_STATE]
            c_g = bc_all[:, (SSD_GROUPS + g) * SSD_STATE:(SSD_GROUPS + g + 1) * SSD_STATE]
            prev_t = prev_ref[0, g]
            dst = dstate[g]
            dc_g = _dot(dcp[:, gs], prev_t, NT)
            db_g = _dot(xdec[:, gs], dst, NT)
            dxdt_state = _dot(b_g, dst) * dec_x[:, gs]
            dstate[g] = dst * e_last_x[:, gs] + _dot(c_g.T, dcp[:, gs])
            last_parts.append(jnp.sum(xdt_mxu[:, gs] * dxdt_state, axis=0, keepdims=True)
                              + jnp.sum(dst * prev_t, axis=0, keepdims=True) * e_last_x[:, gs])
            cb_t = _dot(b_g, c_g, NT)
            dcb_t = jnp.zeros((CHUNK, CHUNK), F32)
            for k in range(SSD_HPG // 2):
                h0 = g * SSD_HPG + 2 * k
                ps = slice(h0 * SSD_HEAD_DIM, h0 * SSD_HEAD_DIM + PAIR_W)
                dy_pair = dy_all[:, ps]
                xdt_pair = xdt[:, ps]
                dd = []
                for h in (h0, h0 + 1):
                    lmat_t = jnp.where(upper, jnp.exp(cs_t[h:h + 1, :] - cs[:, h:h + 1]), 0.0)
                    dd.append(_dot(cb_t * lmat_t, dy_pair))
                    mine = lane_lo if h == h0 else jnp.logical_not(lane_lo)
                    dcb_t = dcb_t + _dot(jnp.where(mine, xdt_pair, 0.0), dy_pair, NT) * lmat_t
                dxdt_parts.append(jnp.where(lane_lo, dd[0], dd[1]) + dxdt_state[:, k * PAIR_W:(k + 1) * PAIR_W])
            dc_g = dc_g + _dot(dcb_t, b_g, TN)
            db_g = db_g + _dot(dcb_t, c_g)
            dbs.append(db_g * realf)
            dcs_.append(dc_g * realf)
        dbc_ref[...] = jnp.concatenate(dbs + dcs_, axis=1)
        dxdt = jnp.concatenate(dxdt_parts, axis=1)
        dxs_ref[...] = (dxdt * dt_x + dy_all * d_x) * realf
        ddt_all = _dot_onehot(dxdt * xs_all, ex, NT)
        rows = jnp.concatenate([jnp.concatenate(last_parts, axis=1), jnp.sum(dy_all * xs_all, axis=0, keepdims=True),
                                jnp.zeros((6, SSD_WIDTH), F32)], axis=0)
        rows = _dot_onehot(rows, ex, NT)
        dd_row = rows[1:2]
        dy_mxu = dy_all.astype(_MXU).astype(F32)
        dcs_all = (_dot_onehot(dy_mxu * (y_all - xs_all * d_x), ex, NT) - _dot_onehot(xdt_mxu * dxdt, ex, NT)
                   + last_row * rows[0:1])
        dda = _dot_onehot(tri, dcs_all, TN, data=1)
        ddt = (ddt_all + dda * a_row) * realf
        ddt_raw = ddt * _sigmoid(dt_ref[...] + prm[0:1])
        ddt_ref[...] = ddt_raw.astype(_MXU)
        da_log = jnp.sum(dda * dt, axis=0, keepdims=True) * a_row
        dprm_ref[0:1, :] += jnp.sum(ddt_raw, axis=0, keepdims=True)
        dprm_ref[1:2, :] += da_log
        dprm_ref[2:3, :] += dd_row

    rev = lambda w, blk=0: pl.BlockSpec((CHUNK, w), lambda s, blk=blk: (N_CHUNKS - 1 - s, blk))
    return pl.pallas_call(
        body, name="ssd_bwd", grid=(N_CHUNKS,),
        in_specs=[rev(SSD_WIDTH, dyn_block), rev(SSD_WIDTH), rev(SSD_WIDTH), rev(SSD_WIDTH), rev(512), rev(128),
                  pl.BlockSpec((1,) + STATE_SHAPE, lambda s: (N_CHUNKS - 1 - s, 0, 0, 0)),
                  pl.BlockSpec((8, 128), lambda s: (0, 0)), pl.BlockSpec((1, SSD_WIDTH), lambda s: (0, 0)),
                  pl.BlockSpec((128, SSD_WIDTH), lambda s: (0, 0))],
        out_specs=[rev(SSD_WIDTH), rev(SSD_WIDTH), rev(512), rev(128),
                   pl.BlockSpec((8, 128), lambda s: (0, 0)), pl.BlockSpec((1, SSD_WIDTH), lambda s: (0, 0))],
        out_shape=[jax.ShapeDtypeStruct((T_ROWS, SSD_WIDTH), _MXU), jax.ShapeDtypeStruct((T_ROWS, SSD_WIDTH), F32),
                   jax.ShapeDtypeStruct((T_ROWS, 512), F32), jax.ShapeDtypeStruct((T_ROWS, 128), _MXU),
                   jax.ShapeDtypeStruct((8, 128), F32), jax.ShapeDtypeStruct((1, SSD_WIDTH), F32)],
        scratch_shapes=[pltpu.VMEM(STATE_SHAPE, F32)],
        compiler_params=_cparams("arbitrary"),
    )(dyn, z, y_pre, xs, bc, dt_raw, prev, prm, norm_w, ex)


LRU_PAIRS = 8


def _lru_gates(xr, wa_ref, wx_ref, prm):
    pre_r, pre_i = [], []
    for k in range(LRU_PAIRS):
        xk = xr[:, k * 128:(k + 1) * 128]
        pre_r.append(_dot(xk, wa_ref[k]))
        pre_i.append(_dot(xk, wx_ref[k]))
    r = _sigmoid(jnp.concatenate(pre_r, axis=1) + prm[0:1])
    i = _sigmoid(jnp.concatenate(pre_i, axis=1) + prm[1:2])
    sp = _softplus(-prm[2:3])
    log_a = (-LRU_C) * r * sp
    a = jnp.exp(log_a)
    s = jnp.sqrt(-jnp.tanh(log_a) * (a * a + 1.0))
    return r, i, a, s, sp


def _lru_fwd(xr, gate, wa, wx, prm):
    def body(xr_ref, g_ref, wa_ref, wx_ref, prm_ref, hs_ref, yn_ref, carry, a_s, u_s):
        @pl.when(pl.program_id(0) == 0)
        def _():
            carry[...] = jnp.zeros_like(carry)

        prm = prm_ref[...]
        xr_t = xr_ref[...]
        _, i, a, s, _ = _lru_gates(xr_t, wa_ref, wx_ref, prm)
        a_s[...] = a
        u_s[...] = s * (i * xr_t)
        rid = lax.broadcasted_iota(jnp.int32, (8, LRU_WIDTH), 0)

        def group(k, h):
            off = pl.multiple_of(k * 8, 8)
            a8 = a_s[pl.ds(off, 8), :]
            u8 = u_s[pl.ds(off, 8), :]
            out = jnp.zeros((8, LRU_WIDTH), F32)
            for r_ in range(8):
                h = a8[r_:r_ + 1] * h + u8[r_:r_ + 1]
                out = jnp.where(rid == r_, h, out)
            hs_ref[pl.ds(off, 8), :] = out
            return h

        carry[0:1, :] = lax.fori_loop(0, CHUNK // 8, group, carry[0:1, :])
        gel, _ = _gelu_and_grad(g_ref[...])
        yn_ref[...] = _rms_fwd(gel * hs_ref[...], prm[3:4]).astype(_MXU)

    row = pl.BlockSpec((CHUNK, LRU_WIDTH), lambda t: (t, 0))
    wspec = pl.BlockSpec((LRU_PAIRS, 128, 128), lambda t: (0, 0, 0))
    return pl.pallas_call(
        body, name="lru_fwd", grid=(N_CHUNKS,),
        in_specs=[row, row, wspec, wspec, pl.BlockSpec((8, LRU_WIDTH), lambda t: (0, 0))],
        out_specs=[row, row],
        out_shape=[jax.ShapeDtypeStruct((T_ROWS, LRU_WIDTH), F32), jax.ShapeDtypeStruct((T_ROWS, LRU_WIDTH), _MXU)],
        scratch_shapes=[pltpu.VMEM((8, LRU_WIDTH), F32), pltpu.VMEM((CHUNK, LRU_WIDTH), F32),
                        pltpu.VMEM((CHUNK, LRU_WIDTH), F32)],
        compiler_params=_cparams("arbitrary"),
    )(xr, gate, wa, wx, prm)


def _lru_bwd(dyn, dyn_block, gate, xr, hs, wa, wx, wa_t, wx_t, prm):
    def body(dyn_ref, g_ref, xr_ref, hs_ref, hsp_ref, wa_ref, wx_ref, wat_ref, wxt_ref, prm_ref,
             dg_ref, dxr_ref, dwa_ref, dwx_ref, dprm_ref, carry, a_s, d_s):
        step = pl.program_id(0)
        tile = N_CHUNKS - 1 - step

        @pl.when(step == 0)
        def _():
            carry[...] = jnp.zeros_like(carry)
            dwa_ref[...] = jnp.zeros_like(dwa_ref)
            dwx_ref[...] = jnp.zeros_like(dwx_ref)
            dprm_ref[...] = jnp.zeros_like(dprm_ref)

        prm = prm_ref[...]
        xr_t = xr_ref[...]
        r, i, a, s, sp = _lru_gates(xr_t, wa_ref, wx_ref, prm)
        hs_t = hs_ref[...]
        gel, dgel = _gelu_and_grad(g_ref[...])
        dy, dnw = _rms_bwd(gel * hs_t, prm[3:4], dyn_ref[...])
        dg_ref[...] = (dy * hs_t * dgel).astype(_MXU)
        a_s[...] = a
        d_s[...] = dy * gel
        rid = lax.broadcasted_iota(jnp.int32, (8, LRU_WIDTH), 0)

        def group(k, cr):
            off = pl.multiple_of((CHUNK // 8 - 1 - k) * 8, 8)
            a8 = a_s[pl.ds(off, 8), :]
            d8 = d_s[pl.ds(off, 8), :]
            out = jnp.zeros((8, LRU_WIDTH), F32)
            for r_ in reversed(range(8)):
                dht = d8[r_:r_ + 1] + cr
                out = jnp.where(rid == r_, dht, out)
                cr = a8[r_:r_ + 1] * dht
            d_s[pl.ds(off, 8), :] = out
            return cr

        carry[0:1, :] = lax.fori_loop(0, CHUNK // 8, group, carry[0:1, :])
        dht = d_s[...]
        before = hsp_ref[CHUNK - 8:CHUNK, :][7:8] * (tile > 0).astype(F32)
        first = lax.broadcasted_iota(jnp.int32, (CHUNK, 1), 0) == 0
        hprev = jnp.where(first, before, pltpu.roll(hs_t, 1, 0))
        da = dht * hprev
        ixr = i * xr_t
        ds = dht * ixr
        dlog_a = da * a - ds * (a * a) / s
        dr = dlog_a * ((-LRU_C) * sp)
        dsp = jnp.sum(dlog_a * ((-LRU_C) * r), axis=0, keepdims=True)
        dlam = dsp * (-_sigmoid(-prm[2:3]))
        di = dht * s * xr_t
        dpre_r = dr * r * (1.0 - r)
        dpre_i = di * i * (1.0 - i)
        dxr = dht * s * i
        parts = []
        for k in range(LRU_PAIRS):
            sl = slice(k * 128, (k + 1) * 128)
            parts.append(_dot(dpre_r[:, sl], wat_ref[k]) + _dot(dpre_i[:, sl], wxt_ref[k]))
            dwa_ref[k] += _dot(xr_t[:, sl], dpre_r[:, sl], TN)
            dwx_ref[k] += _dot(xr_t[:, sl], dpre_i[:, sl], TN)
        dxr_ref[...] = dxr + jnp.concatenate(parts, axis=1)
        dprm_ref[0:1, :] += jnp.sum(dpre_r, axis=0, keepdims=True)
        dprm_ref[1:2, :] += jnp.sum(dpre_i, axis=0, keepdims=True)
        dprm_ref[2:3, :] += dlam
        dprm_ref[3:4, :] += jnp.sum(dnw, axis=0, keepdims=True)

    rev = lambda blk=0: pl.BlockSpec((CHUNK, LRU_WIDTH), lambda s, blk=blk: (N_CHUNKS - 1 - s, blk))
    wspec = pl.BlockSpec((LRU_PAIRS, 128, 128), lambda s: (0, 0, 0))
    return pl.pallas_call(
        body, name="lru_bwd", grid=(N_CHUNKS,),
        in_specs=[rev(dyn_block), rev(), rev(), rev(),
                  pl.BlockSpec((CHUNK, LRU_WIDTH), lambda s: (jnp.maximum(N_CHUNKS - 2 - s, 0), 0)),
                  wspec, wspec, wspec, wspec, pl.BlockSpec((8, LRU_WIDTH), lambda s: (0, 0))],
        out_specs=[rev(), rev(), wspec, wspec, pl.BlockSpec((8, LRU_WIDTH), lambda s: (0, 0))],
        out_shape=[jax.ShapeDtypeStruct((T_ROWS, LRU_WIDTH), _MXU), jax.ShapeDtypeStruct((T_ROWS, LRU_WIDTH), F32),
                   jax.ShapeDtypeStruct((LRU_PAIRS, 128, 128), F32), jax.ShapeDtypeStruct((LRU_PAIRS, 128, 128), F32),
                   jax.ShapeDtypeStruct((8, LRU_WIDTH), F32)],
        scratch_shapes=[pltpu.VMEM((8, LRU_WIDTH), F32), pltpu.VMEM((CHUNK, LRU_WIDTH), F32),
                        pltpu.VMEM((CHUNK, LRU_WIDTH), F32)],
        compiler_params=_cparams("arbitrary"),
    )(dyn, gate, xr, hs, hs, wa, wx, wa_t, wx_t, prm)


SEC_NAMES = ("z", "xs", "bc", "dt", "g", "x")
SEC_WIDTH = {"z": 1024, "xs": 1024, "bc": 512, "dt": 128, "g": 1024, "x": 1024}


def _pair_blocks(w):
    w = w.reshape(LRU_PAIRS, 2, 64, 64)
    zero = jnp.zeros((LRU_PAIRS, 64, 64), w.dtype)
    top = jnp.concatenate([w[:, 0], zero], axis=2)
    bot = jnp.concatenate([zero, w[:, 1]], axis=2)
    return jnp.concatenate([top, bot], axis=1)


def _unpair_blocks(wp):
    return jnp.stack([wp[:, :64, :64], wp[:, 64:, 64:]], axis=1).reshape(16, 64, 64)


def _pad_lanes(v, width=128):
    return jnp.pad(v, ((0, 0), (0, width - v.shape[1])))


class _Resident:
    def __init__(self, w_out, w_gate, w_up, w_down):
        self._w_out, self._ffn = w_out, (w_gate, w_up, w_down)

    def w_out(self, after):
        return self._w_out

    def ffn(self, after):
        return self._ffn

    def grads_ready(self, names, g, g_mxu):
        return jnp.zeros((1, 1), F32)

    def small_ready(self, g, loss):
        return jnp.zeros((1, 1), F32)

    def small_middle(self, after):
        return jnp.zeros((1, 1), F32)


def _local_step(x, target, meta, p, late):
    g, g_mxu = {}, {}
    ex = _head_expander()
    h0 = _embed(x, meta)
    u1, projs = _norm_proj(h0, p["norm1_w"], [p["w_in_" + s] for s in SEC_NAMES], name="norm_in_proj")
    proj = dict(zip(SEC_NAMES, projs))
    ssd_prm = jnp.concatenate([_pad_lanes(p["ssd_dt_bias"]), _pad_lanes(p["ssd_a_log"]), _pad_lanes(p["ssd_d"]),
                               jnp.zeros((5, 128), F32)], axis=0)
    xs_act = _conv_fwd(proj["xs"], p["ssd_conv_w"][:, :SSD_WIDTH], p["ssd_conv_b"][:, :SSD_WIDTH], silu=True,
                       name="ssd_conv_xs")
    bc_act = _conv_fwd(proj["bc"], p["ssd_conv_w"][:, SSD_WIDTH:], p["ssd_conv_b"][:, SSD_WIDTH:], silu=True,
                       name="ssd_conv_bc")
    y_pre, y_ssd, prev = _ssd_fwd(xs_act, bc_act, proj["dt"], proj["z"], ssd_prm, p["ssd_norm_w"], ex)
    xr = _conv_fwd(proj["x"], p["lru_conv_w"], p["lru_conv_b"], silu=False, name="lru_conv")
    wa_p, wx_p = _pair_blocks(p["lru_wa"]), _pair_blocks(p["lru_wx"])
    lru_prm = jnp.concatenate([p["lru_ba"], p["lru_bx"], p["lru_lambda"], p["lru_norm_w"],
                               jnp.zeros((4, LRU_WIDTH), F32)], axis=0)
    hs, y_lru = _lru_fwd(xr, proj["g"], wa_p.astype(_MXU), wx_p.astype(_MXU), lru_prm)
    ycat = jnp.concatenate([y_ssd, y_lru], axis=1)
    w_out = late.w_out(ycat)
    h1 = _mm([(ycat, 0, w_out, 0, 2 * D_MODEL)], T_ROWS, D_MODEL, tm=T_ROWS, tn=256, mode="nn", out_dtype=F32,
             name="out_proj", residual=h0)
    u2 = _rmsnorm(h1, p["norm2_w"], name="norm2")
    w_gate, w_up, w_down = late.ffn(u2)
    gp, up, act = _ffn_up(u2, w_gate, w_up)
    h2 = _mm([(act, 0, w_down, 0, D_FF)], T_ROWS, D_MODEL, tm=T_ROWS, tn=256, mode="nn", out_dtype=F32,
             name="ffn_down", residual=h1)
    loss, dh2, dh2b, g["final_norm_w"] = _loss_head(h2, target, p["final_norm_w"])
    dgp, dup = _ffn_bwd_act(dh2b, w_down, gp, up)
    g["w_down"], g_mxu["w_down"] = _mm([(act, 0, dh2b, 0, T_ROWS)], D_FF, D_MODEL, tm=1408, tn=512, mode="tn",
                                       out_dtype=F32, name="dw_down", also_mxu=True)
    dh1, dh1b, g["norm2_w"] = _mm_norm_bwd([(dgp, w_gate, D_FF), (dup, w_up, D_FF)], h1, p["norm2_w"], dh2,
                                           name="ffn_bwd_in")
    g["w_gate"], g_mxu["w_gate"] = _mm([(dgp, 0, u2, 0, T_ROWS)], D_FF, D_MODEL, tm=1408, tn=512, mode="tn",
                                       out_dtype=F32, name="dw_gate", also_mxu=True)
    g["w_up"], g_mxu["w_up"] = _mm([(dup, 0, u2, 0, T_ROWS)], D_FF, D_MODEL, tm=1408, tn=512, mode="tn",
                                   out_dtype=F32, name="dw_up", also_mxu=True)
    sent = late.grads_ready(("w_down", "w_gate", "w_up"), g, g_mxu)
    g["w_out"], g_mxu["w_out"] = _mm([(ycat, 0, dh1b, 0, T_ROWS)], 2 * D_MODEL, D_MODEL, tm=1024, tn=512, mode="tn",
                                     out_dtype=F32, name="dw_out", also_mxu=True, behind=(sent,))
    sent = late.grads_ready(("w_out",), g, g_mxu)
    dycat = _mm([(dh1b, 0, w_out, 0, D_MODEL)], T_ROWS, 2 * D_MODEL, tm=T_ROWS, tn=256, mode="nt", out_dtype=F32,
                name="out_proj_bwd", behind=(sent,))
    dgate, dxr, dwa_p, dwx_p, dlru_prm = _lru_bwd(dycat, 1, proj["g"], xr, hs, wa_p.astype(_MXU), wx_p.astype(_MXU),
                                                  jnp.swapaxes(wa_p, 1, 2).astype(_MXU),
                                                  jnp.swapaxes(wx_p, 1, 2).astype(_MXU), lru_prm)
    g["lru_wa"], g["lru_wx"] = _unpair_blocks(dwa_p), _unpair_blocks(dwx_p)
    g["lru_ba"], g["lru_bx"], g["lru_lambda"], g["lru_norm_w"] = (dlru_prm[k:k + 1] for k in range(4))
    dx_lru, g["lru_conv_w"], g["lru_conv_b"] = _conv_bwd(dxr, proj["x"], p["lru_conv_w"], p["lru_conv_b"], silu=False,
                                                         name="lru_conv_bwd")
    dz, dxs_act, dbc_act, ddt, dssd_prm, g["ssd_norm_w"] = _ssd_bwd(dycat, 0, proj["z"], y_pre, xs_act, bc_act,
                                                                    proj["dt"], prev, ssd_prm, p["ssd_norm_w"], ex)
    g["ssd_dt_bias"], g["ssd_a_log"], g["ssd_d"] = (dssd_prm[k:k + 1, :SSD_HEADS] for k in range(3))
    dxs, dcw_xs, dcb_xs = _conv_bwd(dxs_act, proj["xs"], p["ssd_conv_w"][:, :SSD_WIDTH],
                                    p["ssd_conv_b"][:, :SSD_WIDTH], silu=True, name="ssd_conv_xs_bwd")
    dbc, dcw_bc, dcb_bc = _conv_bwd(dbc_act, proj["bc"], p["ssd_conv_w"][:, SSD_WIDTH:],
                                    p["ssd_conv_b"][:, SSD_WIDTH:], silu=True, name="ssd_conv_bc_bwd")
    g["ssd_conv_w"] = jnp.concatenate([dcw_xs, dcw_bc], axis=1)
    g["ssd_conv_b"] = jnp.concatenate([dcb_xs, dcb_bc], axis=1)
    dproj = {"z": dz, "xs": dxs, "bc": dbc, "dt": ddt, "g": dgate, "x": dx_lru}
    dh0, _, g["norm1_w"] = _mm_norm_bwd([(dproj[s], p["w_in_" + s], SEC_WIDTH[s]) for s in SEC_NAMES], h0,
                                        p["norm1_w"], dh1, name="in_proj_bwd")
    g["meta_tokens"] = dh0[PAD_ROWS:X_ROW0]
    sent = late.small_ready(g, loss)
    for s in SEC_NAMES:
        wdt = SEC_WIDTH[s]
        g["w_in_" + s], g_mxu["w_in_" + s] = _mm([(dproj[s], 0, u1, 0, T_ROWS)], wdt, D_MODEL, tm=min(wdt, 1024),
                                                 tn=512, mode="tn", out_dtype=F32, name="dw_in_" + s, also_mxu=True,
                                                 behind=(sent,))
        if s == "bc":
            sent = late.small_middle(g["w_in_bc"])
    return loss, dh0[X_ROW0:], g, g_mxu


MESH = pl.DeviceIdType.MESH
ANY = pl.BlockSpec(memory_space=pl.ANY)


def _my_place():
    return lax.axis_index("x"), lax.axis_index("y"), lax.axis_index("c")


def _other_chips(x, y):
    return [(1 - x, y), (x, 1 - y), (1 - x, 1 - y)]


def _gather_first(big, small):
    half = big.shape[1] // 2

    def body(big_ref, small_ref, big4, small4, send_sems, recv_sems, local_sems):
        x, y, c = _my_place()
        me = 2 * x + y
        sibling = (x, y, 1 - c)
        peers = _other_chips(x, y)
        mine = pl.ds(pl.multiple_of(c * half, 128), half)
        theirs = pl.ds(pl.multiple_of((1 - c) * half, 128), half)

        def copy(k, src, dst, dev):
            return pltpu.make_async_remote_copy(src_ref=src, dst_ref=dst, send_sem=send_sems.at[k],
                                                recv_sem=recv_sems.at[k], device_id=dev, device_id_type=MESH)

        local = [pltpu.make_async_copy(big_ref, big4.at[me], local_sems.at[0]),
                 pltpu.make_async_copy(small_ref, small4.at[me], local_sems.at[1])]
        for cp in local:
            cp.start()
        first = []
        for j, (px, py) in enumerate(peers):
            first.append(copy(j, big_ref.at[:, mine], big4.at[me, :, mine], (px, py, c)))
            first.append(copy(3 + j, small_ref, small4.at[me], (px, py, c)))
        for cp in first:
            cp.start()
        passed = []
        for j, (px, py) in enumerate(peers):
            slot = 2 * px + py
            copy(j, big_ref.at[:, mine], big4.at[slot, :, mine], (px, py, c)).wait_recv()
            passed.append(copy(6 + j, big4.at[slot, :, mine], big4.at[slot, :, mine], sibling))
            passed[-1].start()
        for j, (px, py) in enumerate(peers):
            slot = 2 * px + py
            copy(6 + j, big4.at[slot, :, theirs], big4.at[slot, :, theirs], sibling).wait_recv()
            copy(3 + j, small_ref, small4.at[slot], (px, py, c)).wait_recv()
        for cp in first + passed:
            cp.wait_send()
        for cp in local:
            cp.wait()

    return pl.pallas_call(
        body, name="gather_first", in_specs=[ANY, ANY], out_specs=[ANY, ANY],
        out_shape=[jax.ShapeDtypeStruct((N_SHARDS,) + big.shape, big.dtype),
                   jax.ShapeDtypeStruct((N_SHARDS,) + small.shape, small.dtype)],
        scratch_shapes=[pltpu.SemaphoreType.DMA((9,)), pltpu.SemaphoreType.DMA((9,)), pltpu.SemaphoreType.DMA((2,))],
    )(big, small)


HBM_SPEC = pl.BlockSpec(memory_space=pltpu.HBM)
SEM_SPEC = pl.BlockSpec(memory_space=pltpu.SEMAPHORE)
SPLIT_EFFECT = pltpu.SideEffectType.DATAFLOW_SIDE_EFFECTING


def _gather_plan(bufs, x, y, c, incoming):
    plan = []
    for buf in bufs:
        for (px, py) in _other_chips(x, y):
            slot = 2 * px + py if incoming else 2 * x + y
            plan.append((buf.at[2 * x + y], buf.at[slot], (px, py, c)))
    return plan


def _scatter_plan(bufs, x, y, c, incoming):
    n = len(bufs) // 2
    plan = []
    for k in range(n):
        for j, (px, py) in enumerate(_other_chips(x, y)):
            plan.append((bufs[k].at[2 * px + py], bufs[n + k].at[j], (px, py, c)))
    return plan


def _split_start(bufs, plan, n_copies, after, *, name):
    n = len(bufs)
    extra = [] if after is None else [after]

    def body(*refs):
        ins = refs[:n]
        send_sems, recv_sems = refs[n + len(extra)], refs[n + len(extra) + 1]
        token = refs[-1]
        x, y, c = _my_place()
        for i, (src, dst, dev) in enumerate(plan(ins, x, y, c, False)):
            pltpu.make_async_remote_copy(src_ref=src, dst_ref=dst, send_sem=send_sems.at[i], recv_sem=recv_sems.at[i],
                                         device_id=dev, device_id_type=MESH).start()
        token[...] = jnp.zeros_like(token)

    outs = pl.pallas_call(
        body, name=name,
        out_shape=(pltpu.SemaphoreType.DMA((n_copies,)), pltpu.SemaphoreType.DMA((n_copies,)),
                   *[pltpu.HBM(b.shape, b.dtype) for b in bufs], jax.ShapeDtypeStruct((8, 128), F32)),
        in_specs=[HBM_SPEC] * n + [ANY] * len(extra),
        out_specs=(SEM_SPEC, SEM_SPEC, *[HBM_SPEC] * n, pl.BlockSpec(memory_space=pltpu.VMEM)),
        input_output_aliases={k: 2 + k for k in range(n)},
        compiler_params=pltpu.CompilerParams(has_side_effects=SPLIT_EFFECT),
    )(*[pltpu.with_memory_space_constraint(b, pltpu.HBM) for b in bufs], *extra)
    return outs[0], outs[1], list(outs[2:2 + n]), outs[-1]


def _split_wait(bufs, send_sems, recv_sems, plan, after, *, name):
    n = len(bufs)

    def body(*refs):
        ins = refs[:n]
        send_sems_ref, recv_sems_ref = refs[n], refs[n + 1]
        x, y, c = _my_place()
        for i, (src, dst, dev) in enumerate(plan(ins, x, y, c, True)):
            cp = pltpu.make_async_remote_copy(src_ref=src, dst_ref=dst, send_sem=send_sems_ref.at[i],
                                              recv_sem=recv_sems_ref.at[i], device_id=dev, device_id_type=MESH)
            cp.wait_send()
            cp.wait_recv()

    outs = pl.pallas_call(
        body, name=name, out_shape=tuple(pltpu.HBM(b.shape, b.dtype) for b in bufs),
        in_specs=[HBM_SPEC] * n + [SEM_SPEC, SEM_SPEC, ANY], out_specs=tuple([HBM_SPEC] * n),
        input_output_aliases={k: k for k in range(n)},
        compiler_params=pltpu.CompilerParams(has_side_effects=SPLIT_EFFECT),
    )(*bufs, send_sems, recv_sems, after)
    return list(outs)


def _fill_own_slot(shard, me_arr, *, name):
    r, c = shard.shape
    tile, steps, imap = _elementwise_tile(r, c)

    def body(me_ref, x_ref, o_ref):
        o_ref[0] = x_ref[...].astype(_MXU)

    return pl.pallas_call(
        body, name=name,
        grid_spec=pltpu.PrefetchScalarGridSpec(
            num_scalar_prefetch=1, grid=(steps,),
            in_specs=[pl.BlockSpec(tile, lambda i, me: imap(i))],
            out_specs=pl.BlockSpec((1,) + tile, lambda i, me: (me[0],) + imap(i))),
        out_shape=jax.ShapeDtypeStruct((N_SHARDS, r, c), _MXU),
        compiler_params=_cparams("parallel"),
    )(me_arr, shard)


def _swap_with_sibling(parts, *, name):
    n = len(parts)

    def body(*refs):
        ins, outs = refs[:n], refs[n:2 * n]
        send_sems, recv_sems = refs[2 * n:]
        x, y, c = _my_place()
        copies = [pltpu.make_async_remote_copy(
            src_ref=ins[k], dst_ref=outs[k], send_sem=send_sems.at[k], recv_sem=recv_sems.at[k],
            device_id=(x, y, 1 - c), device_id_type=MESH) for k in range(n)]
        for cp in copies:
            cp.start()
        for cp in copies:
            cp.wait()

    return pl.pallas_call(
        body, name=name, in_specs=[ANY] * n, out_specs=[ANY] * n,
        out_shape=[jax.ShapeDtypeStruct(a.shape, a.dtype) for a in parts],
        scratch_shapes=[pltpu.SemaphoreType.DMA((n,)), pltpu.SemaphoreType.DMA((n,))],
    )(*parts)


def _other_devices(x, y, c):
    out = []
    for mask in range(1, N_DEV):
        px, py, pc = x ^ (mask >> 2 & 1), y ^ (mask >> 1 & 1), c ^ (mask & 1)
        out.append(((px, py, pc), 4 * px + 2 * py + pc))
    return out


def _pieces_plan(bufs, x, y, c, incoming):
    pack, land = bufs
    me = 4 * x + 2 * y + c
    return [(pack.at[num], land.at[num if incoming else me], dev) for dev, num in _other_devices(x, y, c)]


def _spread_plan(bufs, x, y, c, incoming):
    piece, land = bufs
    me = 4 * x + 2 * y + c
    return [(piece, land.at[num if incoming else me], dev) for dev, num in _other_devices(x, y, c)]


def _sum_pieces(pack, land, dev_arr, *, name):
    def body(dev_ref, pack_ref, land_ref, o_ref):
        dev = dev_ref[0]
        own = pack_ref[dev]
        acc = None
        for d in range(N_DEV):
            term = jnp.where(dev == d, own, land_ref[d])
            acc = term if acc is None else acc + term
        o_ref[...] = acc

    vmem = pl.BlockSpec(memory_space=pltpu.VMEM)
    return pl.pallas_call(
        body, name=name, in_specs=[pl.BlockSpec(memory_space=pltpu.SMEM), vmem, vmem], out_specs=vmem,
        out_shape=jax.ShapeDtypeStruct(pack.shape[1:], F32),
    )(dev_arr, pack, land)


def _join_pieces(piece, land, dev_arr, *, name):
    def body(dev_ref, piece_ref, land_ref, o_ref):
        dev = dev_ref[0]
        for d in range(N_DEV):
            o_ref[d] = jnp.where(dev == d, piece_ref[...], land_ref[d])

    vmem = pl.BlockSpec(memory_space=pltpu.VMEM)
    return pl.pallas_call(
        body, name=name, in_specs=[pl.BlockSpec(memory_space=pltpu.SMEM), vmem, vmem], out_specs=vmem,
        out_shape=jax.ShapeDtypeStruct(land.shape, F32),
    )(dev_arr, piece, land)


def _adamw_native(ws, gs, ms, vs):
    n = len(ws)

    def body(*refs):
        for k in range(n):
            w_ref, g_ref, m_ref, v_ref = (refs[j * n + k] for j in range(4))
            delta, m_new, v_new = _adamw_math(w_ref[...], g_ref[...], m_ref[...], v_ref[...])
            refs[4 * n + k][...] = delta
            refs[5 * n + k][...] = m_new
            refs[6 * n + k][...] = v_new

    vmem = pl.BlockSpec(memory_space=pltpu.VMEM)
    shapes = [jax.ShapeDtypeStruct(a.shape, F32) for a in ws]
    outs = pl.pallas_call(
        body, name="adamw_small", in_specs=[vmem] * (4 * n), out_specs=[vmem] * (3 * n), out_shape=shapes * 3,
        compiler_params=pltpu.CompilerParams(vmem_limit_bytes=VMEM_LIMIT_BYTES),
    )(*ws, *gs, *ms, *vs)
    return outs[:n], outs[n:2 * n], outs[2 * n:]


def _elementwise_tile(rows, cols, limit=256):
    for t in range(limit, 15, -16):
        if rows % t == 0:
            return (t, cols), rows // t, lambda i: (i, 0)
    assert cols % limit == 0
    return (rows, limit), cols // limit, lambda i: (0, i)


def _partial_sum(g4, land, me_arr, *, name):
    _, r, c = g4.shape
    tile, steps, imap = _elementwise_tile(r, c)

    def body(me_ref, own_ref, land_ref, o_ref):
        acc = own_ref[0]
        for j in range(3):
            acc = acc + land_ref[j].astype(F32)
        o_ref[...] = acc

    return pl.pallas_call(
        body, name=name,
        grid_spec=pltpu.PrefetchScalarGridSpec(
            num_scalar_prefetch=1, grid=(steps,),
            in_specs=[pl.BlockSpec((1,) + tile, lambda i, me: (me[0],) + imap(i)),
                      pl.BlockSpec((3,) + tile, lambda i, me: (0,) + imap(i))],
            out_specs=pl.BlockSpec(tile, lambda i, me: imap(i))),
        out_shape=jax.ShapeDtypeStruct((r, c), F32),
        compiler_params=_cparams("parallel"),
    )(me_arr, g4, land)


def _adamw_math(w, g, m, v):
    m = ADAM_B1 * m + (1.0 - ADAM_B1) * g
    v = ADAM_B2 * v + (1.0 - ADAM_B2) * (g * g)
    m_hat = m / (1.0 - ADAM_B1 ** ADAM_STEP)
    v_hat = v / (1.0 - ADAM_B2 ** ADAM_STEP)
    delta = -ADAM_LR * (m_hat / (jnp.sqrt(v_hat) + ADAM_EPS) + ADAM_WD * w)
    return delta, m, v


def _adamw(w, grad_parts, m, v, *, name):
    r, c = w.shape
    tile_shape, steps, imap = _elementwise_tile(r, c)
    n = len(grad_parts)

    def body(*refs):
        w_ref, m_ref, v_ref = refs[:3]
        g_refs = refs[3:3 + n]
        g_out, d_out, m_out, v_out = refs[3 + n:]
        g = g_refs[0][...]
        for k in range(1, n):
            g = g + g_refs[k][...]
        delta, m_new, v_new = _adamw_math(w_ref[...], g, m_ref[...], v_ref[...])
        g_out[...] = g
        d_out[...] = delta
        m_out[...] = m_new
        v_out[...] = v_new

    tile = pl.BlockSpec(tile_shape, imap)
    return pl.pallas_call(
        body, name=name, grid=(steps,), in_specs=[tile] * (3 + n), out_specs=[tile] * 4,
        out_shape=[jax.ShapeDtypeStruct((r, c), F32)] * 4,
        compiler_params=_cparams("parallel"),
    )(w, m, v, *grad_parts)


WEIGHT_NAMES = ("meta_tokens", "norm1_w", "w_in", "ssd_conv_w", "ssd_conv_b", "ssd_dt_bias", "ssd_a_log", "ssd_d",
                "ssd_norm_w", "lru_conv_w", "lru_conv_b", "lru_wa", "lru_ba", "lru_wx", "lru_bx", "lru_lambda",
                "lru_norm_w", "w_out", "norm2_w", "w_gate", "w_up", "w_down", "final_norm_w")
BIG = ("w_in", "w_out", "w_gate", "w_up", "w_down")
FFN = ("w_gate", "w_up", "w_down")
LATE = ("w_out",) + FFN
SMALL_SHARDED = {"meta_tokens": (N_META, D_MODEL), "ssd_conv_w": (CONV_K, 1536), "lru_conv_w": (CONV_K, LRU_WIDTH)}
SMALL = tuple(n for n in WEIGHT_NAMES if n not in BIG)
PACK_COLS = 1024


def _pack(arrays, row_multiple):
    flat = jnp.concatenate([a.reshape(-1) for a in arrays])
    rows = -(-flat.shape[0] // (row_multiple * PACK_COLS)) * row_multiple
    return jnp.pad(flat, (0, rows * PACK_COLS - flat.shape[0])).reshape(rows, PACK_COLS)


def _unpack(pack, shapes):
    flat = pack.reshape(-1)
    out, off = [], 0
    for s in shapes:
        size = math.prod(s)
        out.append(flat[off:off + size].reshape(s))
        off += size
    return out


def _unshard_cols(g4):
    return jnp.swapaxes(g4, 0, 1).reshape(g4.shape[1], -1)


COL_SHARDED = ("w_in", "w_gate", "w_up")
IN_ROWS = {"z": (0, 1024), "xs": (1024, 2048), "bc": (2048, 2560), "dt": (2560, 2576), "g": (2576, 3600),
           "x": (3600, IN_COLS)}


def _rows_view(name, block):
    return jnp.swapaxes(block[0], 0, 1) if name in COL_SHARDED else block[0]


def _param_view(name, rows):
    return (jnp.swapaxes(rows, 0, 1) if name in COL_SHARDED else rows)[None]


def kernel(x, meta_tokens, norm1_w, w_in, ssd_conv_w, ssd_conv_b, ssd_dt_bias, ssd_a_log, ssd_d, ssd_norm_w, lru_conv_w, lru_conv_b, lru_wa, lru_ba, lru_wx, lru_bx, lru_lambda, lru_norm_w, w_out, norm2_w, w_gate, w_up, w_down, final_norm_w, loss_target, m_meta_tokens, m_norm1_w, m_w_in, m_ssd_conv_w, m_ssd_conv_b, m_ssd_dt_bias, m_ssd_a_log, m_ssd_d, m_ssd_norm_w, m_lru_conv_w, m_lru_conv_b, m_lru_wa, m_lru_ba, m_lru_wx, m_lru_bx, m_lru_lambda, m_lru_norm_w, m_w_out, m_norm2_w, m_w_gate, m_w_up, m_w_down, m_final_norm_w, v_meta_tokens, v_norm1_w, v_w_in, v_ssd_conv_w, v_ssd_conv_b, v_ssd_dt_bias, v_ssd_a_log, v_ssd_d, v_ssd_norm_w, v_lru_conv_w, v_lru_conv_b, v_lru_wa, v_lru_ba, v_lru_wx, v_lru_bx, v_lru_lambda, v_lru_norm_w, v_w_out, v_norm2_w, v_w_gate, v_w_up, v_w_down, v_final_norm_w):
    w = dict(zip(WEIGHT_NAMES, (meta_tokens, norm1_w, w_in, ssd_conv_w, ssd_conv_b, ssd_dt_bias, ssd_a_log, ssd_d, ssd_norm_w, lru_conv_w, lru_conv_b, lru_wa, lru_ba, lru_wx, lru_bx, lru_lambda, lru_norm_w, w_out, norm2_w, w_gate, w_up, w_down, final_norm_w)))
    m = dict(zip(WEIGHT_NAMES, (m_meta_tokens, m_norm1_w, m_w_in, m_ssd_conv_w, m_ssd_conv_b, m_ssd_dt_bias, m_ssd_a_log, m_ssd_d, m_ssd_norm_w, m_lru_conv_w, m_lru_conv_b, m_lru_wa, m_lru_ba, m_lru_wx, m_lru_bx, m_lru_lambda, m_lru_norm_w, m_w_out, m_norm2_w, m_w_gate, m_w_up, m_w_down, m_final_norm_w)))
    v = dict(zip(WEIGHT_NAMES, (v_meta_tokens, v_norm1_w, v_w_in, v_ssd_conv_w, v_ssd_conv_b, v_ssd_dt_bias, v_ssd_a_log, v_ssd_d, v_ssd_norm_w, v_lru_conv_w, v_lru_conv_b, v_lru_wa, v_lru_ba, v_lru_wx, v_lru_bx, v_lru_lambda, v_lru_norm_w, v_w_out, v_norm2_w, v_w_gate, v_w_up, v_w_down, v_final_norm_w)))
    me = 2 * lax.axis_index("x") + lax.axis_index("y")

    big2d = {n: _rows_view(n, w[n]) for n in BIG}
    small_local = jnp.concatenate([w["meta_tokens"].reshape(-1), w["ssd_conv_w"].reshape(-1),
                                   w["lru_conv_w"].reshape(-1)])[None]
    me_arr = me.astype(jnp.int32).reshape(1)
    dev_arr = (2 * me + lax.axis_index("c")).astype(jnp.int32).reshape(1)
    w_in4, small4 = _gather_first(big2d["w_in"].astype(_MXU), small_local)
    w_in_full = w_in4.reshape(-1, D_MODEL)
    sm = small4[:, 0]
    meta_full = _unshard_cols(sm[:, :4096].reshape(N_SHARDS, N_META, 256))
    ssd_conv_w_full = _unshard_cols(sm[:, 4096:5632].reshape(N_SHARDS, CONV_K, 384))
    lru_conv_w_full = _unshard_cols(sm[:, 5632:].reshape(N_SHARDS, CONV_K, 256))
    slots = {n: _fill_own_slot(big2d[n], me_arr, name="own_slot_" + n) for n in LATE}
    out_send, out_recv, out_bufs, tok_a = _split_start([slots["w_out"]], _gather_plan, 3, small4,
                                                       name="gather_w_out_start")
    ffn_send, ffn_recv, ffn_bufs, tok_b = _split_start([slots[n] for n in FFN], _gather_plan, 9, tok_a,
                                                       name="gather_ffn_start")

    p = {"w_in_" + s: w_in_full[lo:hi] for s, (lo, hi) in IN_ROWS.items()}
    p["w_in_dt"] = jnp.pad(p["w_in_dt"], ((0, SEC_WIDTH["dt"] - SSD_HEADS), (0, 0)))
    p.update({"ssd_conv_w": ssd_conv_w_full, "lru_conv_w": lru_conv_w_full,
              "lru_wa": w["lru_wa"][0], "lru_wx": w["lru_wx"][0], "final_norm_w": w["final_norm_w"][None]})
    for n in ("norm1_w", "ssd_conv_b", "ssd_dt_bias", "ssd_a_log", "ssd_d", "ssd_norm_w", "lru_conv_b", "lru_ba",
              "lru_bx", "lru_lambda", "lru_norm_w", "norm2_w"):
        p[n] = w[n]
    p["norm1_w"] = p["norm1_w"] + tok_b[:1, :1]

    class Late:
        def __init__(self):
            self.pending = []

        def w_out(self, after):
            (buf,) = _split_wait(out_bufs, out_send, out_recv, _gather_plan, after, name="gather_w_out_wait")
            return buf.reshape(-1, D_MODEL)

        def ffn(self, after):
            bufs = _split_wait(ffn_bufs, ffn_send, ffn_recv, _gather_plan, after, name="gather_ffn_wait")
            return tuple(b.reshape(-1, D_MODEL) for b in bufs)

        def grads_ready(self, names, g, g_mxu):
            srcs = [g_mxu[n].reshape(N_SHARDS, -1, D_MODEL) for n in names]
            lands = [lax.empty((3,) + s.shape[1:], _MXU) for s in srcs]
            tag = "_".join(names)
            send, recv, bufs, tok = _split_start(srcs + lands, _scatter_plan, 3 * len(names), g[names[-1]],
                                                 name="scatter_" + tag + "_start")
            self.pending.append((names, send, recv, bufs, tag))
            self.in_flight = bufs[0]
            return tok[:1, :1]

        def landed(self, after, which):
            land = {}
            for names, send, recv, bufs, tag in self.pending:
                if names[0] in which:
                    bufs = _split_wait(bufs, send, recv, _scatter_plan, after, name="scatter_" + tag + "_wait")
                    land.update(zip(names, bufs[len(names):]))
            return land

        def small_ready(self, g, loss):
            pack = _pack([g[n] for n in SMALL] + [loss[0, :1]], 8 * N_DEV)
            pack = pack.reshape(N_DEV, -1, PACK_COLS)
            self.small = _split_start([pack, lax.empty(pack.shape, F32)], _pieces_plan, N_DEV - 1, loss,
                                      name="small_pieces_start")
            return self.small[3]

        def small_middle(self, after):
            send, recv, bufs, _ = self.small
            pack, land = _split_wait(bufs, send, recv, _pieces_plan, after, name="small_pieces_wait")
            piece = _sum_pieces(pack, land, dev_arr, name="small_pieces_sum")
            self.small = _split_start([piece, lax.empty(pack.shape, F32)], _spread_plan, N_DEV - 1, None,
                                      name="small_spread_start")
            return self.small[3]

        def small_sum(self, after):
            send, recv, bufs, _ = self.small
            piece, land = _split_wait(bufs, send, recv, _spread_plan, after, name="small_spread_wait")
            return _join_pieces(piece, land, dev_arr, name="small_join")

    late = Late()

    loss, grad_x, g, g_mxu = _local_step(x[0], loss_target[0], meta_full, p, late)

    g["w_in"] = jnp.concatenate([g["w_in_" + s][:hi - lo] for s, (lo, hi) in IN_ROWS.items()], axis=0)
    g_mxu["w_in"] = jnp.concatenate([g_mxu["w_in_" + s][:hi - lo] for s, (lo, hi) in IN_ROWS.items()], axis=0)
    g4 = {n: g[n].reshape(N_SHARDS, -1, D_MODEL) for n in BIG}
    late.grads_ready(("w_in",), g, g_mxu)
    land = late.landed(late.in_flight, LATE)
    part = {n: _partial_sum(g4[n], land[n], me_arr, name="partial_" + n) for n in LATE}
    sib = dict(zip(LATE, _swap_with_sibling([part[n] for n in LATE], name="swap_late")))

    small_full_shape = {n: (SMALL_SHARDED[n] if n in SMALL_SHARDED else w[n].shape) for n in SMALL}
    red_list = _unpack(late.small_sum(sib["w_out"]), [small_full_shape[n] for n in SMALL] + [(1,)])
    loss_total = red_list[-1][0]
    g_small = {}
    for n, arr in zip(SMALL, red_list[:-1]):
        if n in SMALL_SHARDED:
            cols = SMALL_SHARDED[n][1] // N_SHARDS
            arr = lax.dynamic_slice_in_dim(arr, me * cols, cols, axis=1)
        g_small[n] = arr.reshape(w[n].shape)

    grad, delta, new_m, new_v = {}, {}, {}, {}

    def update_big(n):
        outs = _adamw(big2d[n], [part[n], sib[n]], _rows_view(n, m[n]), _rows_view(n, v[n]), name="adamw_" + n)
        grad[n], delta[n], new_m[n], new_v[n] = (_param_view(n, o) for o in outs)
        return outs[0]

    two_d = lambda a: a.reshape(1, -1) if a.ndim == 1 else a
    deltas, new_ms, new_vs = _adamw_native(*[[two_d(d[n]) for n in SMALL] for d in (w, g_small, m, v)])
    for n, dn, mn, vn in zip(SMALL, deltas, new_ms, new_vs):
        grad[n], delta[n], new_m[n], new_v[n] = (g_small[n], dn.reshape(w[n].shape), mn.reshape(w[n].shape),
                                                 vn.reshape(w[n].shape))
    for n in LATE:
        last = update_big(n)
    land.update(late.landed(last, ("w_in",)))
    part["w_in"] = _partial_sum(g4["w_in"], land["w_in"], me_arr, name="partial_w_in")
    (sib["w_in"],) = _swap_with_sibling([part["w_in"]], name="swap_w_in")
    update_big("w_in")

    return (loss_total, grad_x[None], *[grad[n] for n in WEIGHT_NAMES], *[delta[n] for n in WEIGHT_NAMES],
            *[new_m[n] for n in WEIGHT_NAMES], *[new_v[n] for n in WEIGHT_NAMES])
```

```python
import functools
import math

import jax
import jax.numpy as jnp
from jax import lax
from jax.experimental import pallas as pl
from jax.experimental.pallas import tpu as pltpu

F32 = jnp.float32
_MXU = jnp.bfloat16

D_MODEL = 1024
SEQ = 2048
N_META = 16
CHUNK = 128
T_ROWS = 2176
N_CHUNKS = T_ROWS // CHUNK
PAD_ROWS = T_ROWS - SEQ - N_META
X_ROW0 = PAD_ROWS + N_META
SSD_HEADS = 16
SSD_HEAD_DIM = 64
SSD_STATE = 128
SSD_GROUPS = 2
SSD_HPG = SSD_HEADS // SSD_GROUPS
SSD_WIDTH = 1024
LRU_WIDTH = 1024
LRU_C = 8.0
D_FF = 2816
EPS = 1e-6
IN_COLS = 4624
N_SHARDS = 4
N_DEV = 8

ADAM_LR = 0.001
ADAM_B1 = 0.9
ADAM_B2 = 0.999
ADAM_EPS = 1e-08
ADAM_WD = 0.01
ADAM_STEP = 10

VMEM_LIMIT_BYTES = 56 * 1024 * 1024

NN = (((1,), (0,)), ((), ()))
NT = (((1,), (1,)), ((), ()))
TN = (((0,), (0,)), ((), ()))


def _cparams(*sem):
    return pltpu.CompilerParams(dimension_semantics=sem, vmem_limit_bytes=VMEM_LIMIT_BYTES)


def _dot(a, b, dims=NN):
    return lax.dot_general(a.astype(_MXU), b.astype(_MXU), dims, preferred_element_type=F32)


def _dot_onehot(a, b, dims=NN, *, data=0):
    ops = [a, b]
    mask = ops[1 - data].astype(jnp.bfloat16)
    rest = ops[data]
    acc = None
    for _ in range(3):
        piece = rest.astype(jnp.bfloat16)
        ops[data], ops[1 - data] = piece, mask
        d = lax.dot_general(ops[0], ops[1], dims, preferred_element_type=F32)
        acc = d if acc is None else acc + d
        rest = rest - piece.astype(F32)
    return acc


def _sigmoid(x):
    return 0.5 * (1.0 + jnp.tanh(0.5 * x))


def _softplus(x):
    return jnp.maximum(x, 0.0) + jnp.log(1.0 + jnp.exp(-jnp.abs(x)))


def _silu(x):
    return x * _sigmoid(x)


def _silu_grad(x):
    s = _sigmoid(x)
    return s * (1.0 + x * (1.0 - s))


_GELU_C = math.sqrt(2.0 / math.pi)


def _gelu_and_grad(x):
    inner = _GELU_C * (x + 0.044715 * x * x * x)
    t = jnp.tanh(inner)
    g = 0.5 * x * (1.0 + t)
    dg = 0.5 * (1.0 + t) + 0.5 * x * (1.0 - t * t) * _GELU_C * (1.0 + 3.0 * 0.044715 * x * x)
    return g, dg


def _rms_fwd(x, w):
    rstd = lax.rsqrt(jnp.mean(x * x, axis=-1, keepdims=True) + EPS)
    return x * rstd * w


def _rms_bwd(x, w, dy):
    rstd = lax.rsqrt(jnp.mean(x * x, axis=-1, keepdims=True) + EPS)
    xhat = x * rstd
    dxhat = dy * w
    dx = rstd * (dxhat - xhat * jnp.mean(dxhat * xhat, axis=-1, keepdims=True))
    return dx, dy * xhat


def _mm(terms, m, n, *, tm, tn, mode, out_dtype, name, residual=None, n_outer=False, also_mxu=False, behind=()):
    gm, gn = m // tm, n // tn
    assert gm * tm == m and gn * tn == n
    if n_outer:
        grid = (gn, gm)
        mi = lambda g0, g1: g1
        ni = lambda g0, g1: g0
    else:
        grid = (gm, gn)
        mi = lambda g0, g1: g0
        ni = lambda g0, g1: g1
    in_specs, args = [], []
    for (a, ka, b, kb, k) in terms:
        if mode == "tn":
            in_specs.append(pl.BlockSpec((k, tm), lambda g0, g1, ka=ka: (ka, mi(g0, g1))))
        else:
            in_specs.append(pl.BlockSpec((tm, k), lambda g0, g1, ka=ka: (mi(g0, g1), ka)))
        if mode == "nt":
            in_specs.append(pl.BlockSpec((tn, k), lambda g0, g1, kb=kb: (ni(g0, g1), kb)))
        else:
            in_specs.append(pl.BlockSpec((k, tn), lambda g0, g1, kb=kb: (kb, ni(g0, g1))))
        args += [a, b]
    if residual is not None:
        in_specs.append(pl.BlockSpec((tm, tn), lambda g0, g1: (mi(g0, g1), ni(g0, g1))))
        args.append(residual)
    dims = {"nn": NN, "nt": NT, "tn": TN}[mode]
    n_terms = len(terms)
    has_res = residual is not None
    in_specs += [pl.BlockSpec(memory_space=pl.ANY)] * len(behind)
    args += list(behind)
    n_in = len(args)

    def body(*refs):
        acc = None
        for t in range(n_terms):
            d = lax.dot_general(refs[2 * t][...], refs[2 * t + 1][...], dims, preferred_element_type=F32)
            acc = d if acc is None else acc + d
        if has_res:
            acc = acc + refs[2 * n_terms][...]
        refs[n_in][...] = acc.astype(out_dtype)
        if also_mxu:
            refs[n_in + 1][...] = acc.astype(_MXU)

    tile = pl.BlockSpec((tm, tn), lambda g0, g1: (mi(g0, g1), ni(g0, g1)))
    shape = jax.ShapeDtypeStruct((m, n), out_dtype)
    return pl.pallas_call(
        body, name=name, grid=grid, in_specs=in_specs,
        out_specs=[tile, tile] if also_mxu else tile,
        out_shape=[shape, jax.ShapeDtypeStruct((m, n), _MXU)] if also_mxu else shape,
        compiler_params=_cparams("parallel", "parallel"),
    )(*args)


def _embed(x, meta):
    def body(x_ref, meta_ref, o_ref):
        i = pl.program_id(0)

        @pl.when(i == 0)
        def _():
            o_ref[0:PAD_ROWS, :] = jnp.zeros((PAD_ROWS, D_MODEL), F32)
            o_ref[PAD_ROWS:CHUNK, :] = meta_ref[...]

        @pl.when(i > 0)
        def _():
            o_ref[...] = x_ref[...]

    return pl.pallas_call(
        body, name="embed", grid=(N_CHUNKS,),
        in_specs=[pl.BlockSpec((CHUNK, D_MODEL), lambda i: (jnp.maximum(i - 1, 0), 0)),
                  pl.BlockSpec((N_META, D_MODEL), lambda i: (0, 0))],
        out_specs=pl.BlockSpec((CHUNK, D_MODEL), lambda i: (i, 0)),
        out_shape=jax.ShapeDtypeStruct((T_ROWS, D_MODEL), F32),
        compiler_params=_cparams("parallel"),
    )(x, meta)


def _rmsnorm(h, w, *, name, tm=544):
    def body(h_ref, w_ref, o_ref):
        o_ref[...] = _rms_fwd(h_ref[...], w_ref[...]).astype(_MXU)

    return pl.pallas_call(
        body, name=name, grid=(T_ROWS // tm,),
        in_specs=[pl.BlockSpec((tm, D_MODEL), lambda i: (i, 0)), pl.BlockSpec((1, D_MODEL), lambda i: (0, 0))],
        out_specs=pl.BlockSpec((tm, D_MODEL), lambda i: (i, 0)),
        out_shape=jax.ShapeDtypeStruct((T_ROWS, D_MODEL), _MXU),
        compiler_params=_cparams("parallel"),
    )(h, w)


def _norm_proj(h, w, sections, *, name, tm=544):
    widths = [s.shape[0] for s in sections]
    n = len(sections)

    def body(*refs):
        h_ref, w_ref = refs[:2]
        u_ref = refs[2 + n]
        u = _rms_fwd(h_ref[...], w_ref[...]).astype(_MXU)
        u_ref[...] = u
        for k in range(n):
            refs[3 + n + k][...] = lax.dot_general(u, refs[2 + k][...], NT, preferred_element_type=F32)

    row = lambda width: pl.BlockSpec((tm, width), lambda i: (i, 0))
    outs = pl.pallas_call(
        body, name=name, grid=(T_ROWS // tm,),
        in_specs=[row(D_MODEL), pl.BlockSpec((1, D_MODEL), lambda i: (0, 0))]
        + [pl.BlockSpec((wd, D_MODEL), lambda i: (0, 0)) for wd in widths],
        out_specs=[row(D_MODEL)] + [row(wd) for wd in widths],
        out_shape=[jax.ShapeDtypeStruct((T_ROWS, D_MODEL), _MXU)]
        + [jax.ShapeDtypeStruct((T_ROWS, wd), F32) for wd in widths],
        compiler_params=_cparams("parallel"),
    )(h, w, *sections)
    return outs[0], list(outs[1:])


def _loss_head(h2, target, fw):
    def body(h_ref, t_ref, w_ref, loss_ref, dh_ref, dhb_ref, dw_ref, acc_ref):
        i = pl.program_id(0)

        @pl.when(i == 0)
        def _():
            acc_ref[...] = jnp.zeros_like(acc_ref)
            dw_ref[...] = jnp.zeros_like(dw_ref)

        h = h_ref[...]
        w = w_ref[...]
        y = _rms_fwd(h, w)
        live = (i > 0).astype(F32)
        err = (y - t_ref[...]) * live
        acc_ref[...] += jnp.sum(err * err, axis=0, keepdims=True)
        dy = err * (1.0 / D_MODEL)
        dx, dwr = _rms_bwd(h, w, dy)
        dh_ref[...] = dx
        dhb_ref[...] = dx.astype(_MXU)
        dw_ref[...] += jnp.sum(dwr, axis=0, keepdims=True)

        @pl.when(i == N_CHUNKS - 1)
        def _():
            tot = jnp.sum(acc_ref[...], axis=1, keepdims=True) * (0.5 / D_MODEL)
            loss_ref[...] = jnp.broadcast_to(tot, (1, 128))

    return pl.pallas_call(
        body, name="loss_head", grid=(N_CHUNKS,),
        in_specs=[pl.BlockSpec((CHUNK, D_MODEL), lambda i: (i, 0)),
                  pl.BlockSpec((CHUNK, D_MODEL), lambda i: (jnp.maximum(i - 1, 0), 0)),
                  pl.BlockSpec((1, D_MODEL), lambda i: (0, 0))],
        out_specs=[pl.BlockSpec((1, 128), lambda i: (0, 0)),
                   pl.BlockSpec((CHUNK, D_MODEL), lambda i: (i, 0)),
                   pl.BlockSpec((CHUNK, D_MODEL), lambda i: (i, 0)),
                   pl.BlockSpec((1, D_MODEL), lambda i: (0, 0))],
        out_shape=[jax.ShapeDtypeStruct((1, 128), F32),
                   jax.ShapeDtypeStruct((T_ROWS, D_MODEL), F32),
                   jax.ShapeDtypeStruct((T_ROWS, D_MODEL), _MXU),
                   jax.ShapeDtypeStruct((1, D_MODEL), F32)],
        scratch_shapes=[pltpu.VMEM((1, D_MODEL), F32)],
        compiler_params=_cparams("arbitrary"),
    )(h2, target, fw)


def _mm_norm_bwd(terms, h, w, dres, *, name, tm=544):
    n_terms = len(terms)
    in_specs, args = [], []
    for (a, b, k) in terms:
        in_specs += [pl.BlockSpec((tm, k), lambda i: (i, 0)), pl.BlockSpec((k, D_MODEL), lambda i: (0, 0))]
        args += [a, b]
    in_specs += [pl.BlockSpec((tm, D_MODEL), lambda i: (i, 0)), pl.BlockSpec((1, D_MODEL), lambda i: (0, 0)),
                 pl.BlockSpec((tm, D_MODEL), lambda i: (i, 0))]
    args += [h, w, dres]

    def body(*refs):
        h_ref, w_ref, dres_ref, dh_ref, dhb_ref, dw_ref = refs[2 * n_terms:]

        @pl.when(pl.program_id(0) == 0)
        def _():
            dw_ref[...] = jnp.zeros_like(dw_ref)

        du = None
        for t in range(n_terms):
            d = lax.dot_general(refs[2 * t][...], refs[2 * t + 1][...], NN, preferred_element_type=F32)
            du = d if du is None else du + d
        dx, dwr = _rms_bwd(h_ref[...], w_ref[...], du)
        dh = dres_ref[...] + dx
        dh_ref[...] = dh
        dhb_ref[...] = dh.astype(_MXU)
        dw_ref[...] += jnp.sum(dwr, axis=0, keepdims=True)

    return pl.pallas_call(
        body, name=name, grid=(T_ROWS // tm,), in_specs=in_specs,
        out_specs=[pl.BlockSpec((tm, D_MODEL), lambda i: (i, 0)), pl.BlockSpec((tm, D_MODEL), lambda i: (i, 0)),
                   pl.BlockSpec((1, D_MODEL), lambda i: (0, 0))],
        out_shape=[jax.ShapeDtypeStruct((T_ROWS, D_MODEL), F32), jax.ShapeDtypeStruct((T_ROWS, D_MODEL), _MXU),
                   jax.ShapeDtypeStruct((1, D_MODEL), F32)],
        compiler_params=_cparams("arbitrary"),
    )(*args)


FFN_TM = T_ROWS
FFN_TN = 256


def _ffn_up(u2, wg_t, wu_t):
    def body(u_ref, wg_ref, wu_ref, gp_ref, up_ref, act_ref):
        u = u_ref[...]
        gp = lax.dot_general(u, wg_ref[...], NT, preferred_element_type=F32)
        up = lax.dot_general(u, wu_ref[...], NT, preferred_element_type=F32)
        gp_ref[...] = gp
        up_ref[...] = up
        act_ref[...] = (_silu(gp) * up).astype(_MXU)

    tile = pl.BlockSpec((FFN_TM, FFN_TN), lambda j, i: (i, j))
    return pl.pallas_call(
        body, name="ffn_up", grid=(D_FF // FFN_TN, T_ROWS // FFN_TM),
        in_specs=[pl.BlockSpec((FFN_TM, D_MODEL), lambda j, i: (i, 0)),
                  pl.BlockSpec((FFN_TN, D_MODEL), lambda j, i: (j, 0)),
                  pl.BlockSpec((FFN_TN, D_MODEL), lambda j, i: (j, 0))],
        out_specs=[tile, tile, tile],
        out_shape=[jax.ShapeDtypeStruct((T_ROWS, D_FF), F32), jax.ShapeDtypeStruct((T_ROWS, D_FF), F32),
                   jax.ShapeDtypeStruct((T_ROWS, D_FF), _MXU)],
        compiler_params=_cparams("parallel", "parallel"),
    )(u2, wg_t, wu_t)


def _ffn_bwd_act(dh2b, wd, gp, up):
    def body(dh_ref, wd_ref, gp_ref, up_ref, dgp_ref, dup_ref):
        dact = lax.dot_general(dh_ref[...], wd_ref[...], NT, preferred_element_type=F32)
        gp = gp_ref[...]
        dgp_ref[...] = (dact * up_ref[...] * _silu_grad(gp)).astype(_MXU)
        dup_ref[...] = (dact * _silu(gp)).astype(_MXU)

    tile = pl.BlockSpec((FFN_TM, FFN_TN), lambda j, i: (i, j))
    return pl.pallas_call(
        body, name="ffn_bwd_act", grid=(D_FF // FFN_TN, T_ROWS // FFN_TM),
        in_specs=[pl.BlockSpec((FFN_TM, D_MODEL), lambda j, i: (i, 0)),
                  pl.BlockSpec((FFN_TN, D_MODEL), lambda j, i: (j, 0)), tile, tile],
        out_specs=[tile, tile],
        out_shape=[jax.ShapeDtypeStruct((T_ROWS, D_FF), _MXU), jax.ShapeDtypeStruct((T_ROWS, D_FF), _MXU)],
        compiler_params=_cparams("parallel", "parallel"),
    )(dh2b, wd, gp, up)


CONV_TC = 512
CONV_K = 4


def _conv_pre(x_ref, wv, bv, c):
    tc = wv.shape[1]
    r0 = c * CHUNK
    cur = x_ref[r0:r0 + CHUNK, :]
    if c == 0:
        cat = jnp.concatenate([jnp.zeros((8, tc), F32), cur], axis=0)
        shifted = [cur] + [pltpu.roll(cat, s, 0)[8:8 + CHUNK] for s in range(1, CONV_K)]
    else:
        shifted = [cur] + [x_ref[r0 - s:r0 - s + CHUNK, :] for s in range(1, CONV_K)]
    pre = bv
    for s in range(CONV_K):
        pre = pre + shifted[s] * wv[CONV_K - 1 - s:CONV_K - s]
    return pre, shifted


def _row_mask(c):
    if c > 0:
        return None
    return (lax.broadcasted_iota(jnp.int32, (CHUNK, 1), 0) >= PAD_ROWS).astype(F32)


def _conv_fwd(x, w, b, *, silu, name):
    cols = x.shape[1]
    tc = min(CONV_TC, cols)

    def body(x_ref, w_ref, b_ref, o_ref):
        wv, bv = w_ref[...], b_ref[...]
        for c in range(N_CHUNKS):
            pre, _ = _conv_pre(x_ref, wv, bv, c)
            y = _silu(pre) if silu else pre
            mask = _row_mask(c)
            if mask is not None:
                y = y * mask
            o_ref[c * CHUNK:(c + 1) * CHUNK, :] = y

    return pl.pallas_call(
        body, name=name, grid=(cols // tc,),
        in_specs=[pl.BlockSpec((T_ROWS, tc), lambda j: (0, j)), pl.BlockSpec((CONV_K, tc), lambda j: (0, j)),
                  pl.BlockSpec((1, tc), lambda j: (0, j))],
        out_specs=pl.BlockSpec((T_ROWS, tc), lambda j: (0, j)),
        out_shape=jax.ShapeDtypeStruct((T_ROWS, cols), F32),
        compiler_params=_cparams("parallel"),
    )(x, w, b)


def _conv_bwd(dy, x, w, b, *, silu, name):
    cols = x.shape[1]
    tc = min(CONV_TC, cols)

    def body(dy_ref, x_ref, w_ref, b_ref, dx_ref, dw_ref, db_ref):
        wv, bv = w_ref[...], b_ref[...]
        next8 = jnp.zeros((8, tc), F32)
        dws = [jnp.zeros((1, tc), F32) for _ in range(CONV_K)]
        db = jnp.zeros((1, tc), F32)
        for c in reversed(range(N_CHUNKS)):
            r0 = c * CHUNK
            pre, shifted = _conv_pre(x_ref, wv, bv, c)
            dpre = dy_ref[r0:r0 + CHUNK, :]
            if silu:
                dpre = dpre * _silu_grad(pre)
            mask = _row_mask(c)
            if mask is not None:
                dpre = dpre * mask
            cat = jnp.concatenate([dpre, next8], axis=0)
            dx = dpre * wv[CONV_K - 1:CONV_K]
            for s in range(1, CONV_K):
                dx = dx + pltpu.roll(cat, CHUNK + 8 - s, 0)[0:CHUNK] * wv[CONV_K - 1 - s:CONV_K - s]
            dx_ref[r0:r0 + CHUNK, :] = dx.astype(_MXU)
            for s in range(CONV_K):
                k = CONV_K - 1 - s
                dws[k] = dws[k] + jnp.sum(dpre * shifted[s], axis=0, keepdims=True)
            db = db + jnp.sum(dpre, axis=0, keepdims=True)
            next8 = dpre[0:8]
        dw_ref[...] = jnp.concatenate(dws, axis=0)
        db_ref[...] = db

    return pl.pallas_call(
        body, name=name, grid=(cols // tc,),
        in_specs=[pl.BlockSpec((T_ROWS, tc), lambda j: (0, j)), pl.BlockSpec((T_ROWS, tc), lambda j: (0, j)),
                  pl.BlockSpec((CONV_K, tc), lambda j: (0, j)), pl.BlockSpec((1, tc), lambda j: (0, j))],
        out_specs=[pl.BlockSpec((T_ROWS, tc), lambda j: (0, j)), pl.BlockSpec((CONV_K, tc), lambda j: (0, j)),
                   pl.BlockSpec((1, tc), lambda j: (0, j))],
        out_shape=[jax.ShapeDtypeStruct((T_ROWS, cols), _MXU), jax.ShapeDtypeStruct((CONV_K, cols), F32),
                   jax.ShapeDtypeStruct((1, cols), F32)],
        compiler_params=_cparams("parallel"),
    )(dy, x, w, b)


def _ssd_chunk_common(dt_raw, prm, c):
    a_row = -jnp.exp(prm[1:2])
    dt = _softplus(dt_raw + prm[0:1])
    rows = lax.broadcasted_iota(jnp.int32, (CHUNK, 1), 0)
    real = jnp.logical_or(c > 0, rows >= PAD_ROWS)
    dt = jnp.where(real, dt, 0.0)
    li = lax.broadcasted_iota(jnp.int32, (CHUNK, CHUNK), 0)
    si = lax.broadcasted_iota(jnp.int32, (CHUNK, CHUNK), 1)
    causal = li >= si
    tri = causal.astype(F32)
    cs = _dot_onehot(tri, dt * a_row, data=1)
    return dt, a_row, cs, cs.T, causal, tri, real


def _gated_norm_fwd(y, z, w):
    g = y * _silu(z)
    half = SSD_WIDTH // SSD_GROUPS
    outs = [_rms_fwd(g[:, k * half:(k + 1) * half], w[:, k * half:(k + 1) * half]) for k in range(SSD_GROUPS)]
    return jnp.concatenate(outs, axis=1)


GROUP_W = SSD_WIDTH // SSD_GROUPS
PAIR_W = 2 * SSD_HEAD_DIM
STATE_SHAPE = (SSD_GROUPS, SSD_STATE, GROUP_W)


def _head_expander():
    r = lax.broadcasted_iota(jnp.int32, (128, SSD_WIDTH), 0)
    c = lax.broadcasted_iota(jnp.int32, (128, SSD_WIDTH), 1)
    return (c // SSD_HEAD_DIM == r).astype(F32)


def _ssd_expand(dt, cs, prm, ex):
    cs_x = _dot_onehot(cs, ex)
    cs_last_x = cs_x[CHUNK - 1:CHUNK, :]
    return (_dot_onehot(dt, ex), _dot_onehot(prm, ex)[2:3], jnp.exp(cs_x), jnp.exp(cs_last_x),
            jnp.exp(cs_last_x - cs_x))


def _ssd_fwd(xs, bc, dt_raw, z, prm, norm_w, ex):
    def body(xs_ref, bc_ref, dt_ref, z_ref, prm_ref, nw_ref, ex_ref, y_ref, yn_ref, prev_ref, state):
        c = pl.program_id(0)

        @pl.when(c == 0)
        def _():
            state[...] = jnp.zeros_like(state)

        prm = prm_ref[...]
        dt, a_row, cs, cs_t, causal, _, _ = _ssd_chunk_common(dt_ref[...], prm, c)
        dt_x, d_x, e_cs_x, e_last_x, dec_x = _ssd_expand(dt, cs, prm, ex_ref[...])
        xs_all = xs_ref[...]
        bc_all = bc_ref[...]
        xdt = xs_all * dt_x
        xdec = xdt * dec_x
        lane_lo = lax.broadcasted_iota(jnp.int32, (1, PAIR_W), 1) < SSD_HEAD_DIM
        for g in range(SSD_GROUPS):
            gs = slice(g * GROUP_W, (g + 1) * GROUP_W)
            b_g = bc_all[:, g * SSD_STATE:(g + 1) * SSD_STATE]
            c_g = bc_all[:, (SSD_GROUPS + g) * SSD_STATE:(SSD_GROUPS + g + 1) * SSD_STATE]
            st = state[g]
            prev_ref[0, g] = st
            y_off = _dot(c_g, st) * e_cs_x[:, gs]
            state[g] = st * e_last_x[:, gs] + _dot(b_g.T, xdec[:, gs])
            cb = _dot(c_g, b_g, NT)
            for k in range(SSD_HPG // 2):
                h0 = g * SSD_HPG + 2 * k
                ps = slice(h0 * SSD_HEAD_DIM, h0 * SSD_HEAD_DIM + PAIR_W)
                xdt_pair = xdt[:, ps]
                yd = []
                for h in (h0, h0 + 1):
                    lmat = jnp.where(causal, jnp.exp(cs[:, h:h + 1] - cs_t[h:h + 1, :]), 0.0)
                    yd.append(_dot(cb * lmat, xdt_pair))
                y_ref[:, ps] = (jnp.where(lane_lo, yd[0], yd[1]) + y_off[:, k * PAIR_W:(k + 1) * PAIR_W]
                                + xs_all[:, ps] * d_x[:, ps])
        yn_ref[...] = _gated_norm_fwd(y_ref[...], z_ref[...], nw_ref[...]).astype(_MXU)

    row = lambda w: pl.BlockSpec((CHUNK, w), lambda c: (c, 0))
    return pl.pallas_call(
        body, name="ssd_fwd", grid=(N_CHUNKS,),
        in_specs=[row(SSD_WIDTH), row(512), row(128), row(SSD_WIDTH),
                  pl.BlockSpec((8, 128), lambda c: (0, 0)), pl.BlockSpec((1, SSD_WIDTH), lambda c: (0, 0)),
                  pl.BlockSpec((128, SSD_WIDTH), lambda c: (0, 0))],
        out_specs=[row(SSD_WIDTH), row(SSD_WIDTH),
                   pl.BlockSpec((1,) + STATE_SHAPE, lambda c: (c, 0, 0, 0))],
        out_shape=[jax.ShapeDtypeStruct((T_ROWS, SSD_WIDTH), F32), jax.ShapeDtypeStruct((T_ROWS, SSD_WIDTH), _MXU),
                   jax.ShapeDtypeStruct((N_CHUNKS,) + STATE_SHAPE, F32)],
        scratch_shapes=[pltpu.VMEM(STATE_SHAPE, F32)],
        compiler_params=_cparams("arbitrary"),
    )(xs, bc, dt_raw, z, prm, norm_w, ex)


def _ssd_bwd(dyn, dyn_block, z, y_pre, xs, bc, dt_raw, prev, prm, norm_w, ex):
    def body(dyn_ref, z_ref, y_ref, xs_ref, bc_ref, dt_ref, prev_ref, prm_ref, nw_ref, ex_ref,
             dz_ref, dxs_ref, dbc_ref, ddt_ref, dprm_ref, dnw_ref, dstate):
        step = pl.program_id(0)
        c = N_CHUNKS - 1 - step

        @pl.when(step == 0)
        def _():
            dstate[...] = jnp.zeros_like(dstate)
            dprm_ref[...] = jnp.zeros_like(dprm_ref)
            dnw_ref[...] = jnp.zeros_like(dnw_ref)

        prm = prm_ref[...]
        dt, a_row, cs, cs_t, causal, tri, real = _ssd_chunk_common(dt_ref[...], prm, c)
        realf = real.astype(F32)
        z = z_ref[...]
        y_all = y_ref[...]
        nw = nw_ref[...]
        dyn_all = dyn_ref[...]
        sz = _silu(z)
        gated = y_all * sz
        half = SSD_WIDTH // SSD_GROUPS
        dgs, dnws = [], []
        for k in range(SSD_GROUPS):
            sl = slice(k * half, (k + 1) * half)
            dgk, dwk = _rms_bwd(gated[:, sl], nw[:, sl], dyn_all[:, sl])
            dgs.append(dgk)
            dnws.append(jnp.sum(dwk, axis=0, keepdims=True))
        dgated = jnp.concatenate(dgs, axis=1)
        dnw_ref[...] += jnp.concatenate(dnws, axis=1)
        dz_ref[...] = (dgated * y_all * _silu_grad(z)).astype(_MXU)
        dy_all = dgated * sz

        ex = ex_ref[...]
        dt_x, d_x, e_cs_x, e_last_x, dec_x = _ssd_expand(dt, cs, prm, ex)
        xs_all = xs_ref[...]
        bc_all = bc_ref[...]
        xdt = xs_all * dt_x
        xdt_mxu = xdt.astype(_MXU).astype(F32)
        xdec = xdt * dec_x
        dcp = dy_all * e_cs_x
        lane_lo = lax.broadcasted_iota(jnp.int32, (1, PAIR_W), 1) < SSD_HEAD_DIM
        upper = (lax.broadcasted_iota(jnp.int32, (CHUNK, CHUNK), 0)
                 <= lax.broadcasted_iota(jnp.int32, (CHUNK, CHUNK), 1))
        last_row = (lax.broadcasted_iota(jnp.int32, (CHUNK, 1), 0) == CHUNK - 1).astype(F32)
        dbs, dcs_, dxdt_parts, last_parts = [], [], [], []
        for g in range(SSD_GROUPS):
            gs = slice(g * GROUP_W, (g + 1) * GROUP_W)
            b_g = bc_all[:, g * SSD_STATE:(g + 1) * SSD_STATE]
            c_g = bc_all[:, (SSD_GROUPS + g) * SSD_STATE:(SSD_GROUPS + g + 1) * SSD_STATE]
            prev_t = prev_ref[0, g]
            dst = dstate[g]
            dc_g = _dot(dcp[:, gs], prev_t, NT)
            db_g = _dot(xdec[:, gs], dst, NT)
            dxdt_state = _dot(b_g, dst) * dec_x[:, gs]
            dstate[g] = dst * e_last_x[:, gs] + _dot(c_g.T, dcp[:, gs])
            last_parts.append(jnp.sum(xdt_mxu[:, gs] * dxdt_state, axis=0, keepdims=True)
                              + jnp.sum(dst * prev_t, axis=0, keepdims=True) * e_last_x[:, gs])
            cb_t = _dot(b_g, c_g, NT)
            dcb_t = jnp.zeros((CHUNK, CHUNK), F32)
            for k in range(SSD_HPG // 2):
                h0 = g * SSD_HPG + 2 * k
                ps = slice(h0 * SSD_HEAD_DIM, h0 * SSD_HEAD_DIM + PAIR_W)
                dy_pair = dy_all[:, ps]
                xdt_pair = xdt[:, ps]
                dd = []
                for h in (h0, h0 + 1):
                    lmat_t = jnp.where(upper, jnp.exp(cs_t[h:h + 1, :] - cs[:, h:h + 1]), 0.0)
                    dd.append(_dot(cb_t * lmat_t, dy_pair))
                    mine = lane_lo if h == h0 else jnp.logical_not(lane_lo)
                    dcb_t = dcb_t + _dot(jnp.where(mine, xdt_pair, 0.0), dy_pair, NT) * lmat_t
                dxdt_parts.append(jnp.where(lane_lo, dd[0], dd[1]) + dxdt_state[:, k * PAIR_W:(k + 1) * PAIR_W])
            dc_g = dc_g + _dot(dcb_t, b_g, TN)
            db_g = db_g + _dot(dcb_t, c_g)
            dbs.append(db_g * realf)
            dcs_.append(dc_g * realf)
        dbc_ref[...] = jnp.concatenate(dbs + dcs_, axis=1)
        dxdt = jnp.concatenate(dxdt_parts, axis=1)
        dxs_ref[...] = (dxdt * dt_x + dy_all * d_x) * realf
        ddt_all = _dot_onehot(dxdt * xs_all, ex, NT)
        rows = jnp.concatenate([jnp.concatenate(last_parts, axis=1), jnp.sum(dy_all * xs_all, axis=0, keepdims=True),
                                jnp.zeros((6, SSD_WIDTH), F32)], axis=0)
        rows = _dot_onehot(rows, ex, NT)
        dd_row = rows[1:2]
        dy_mxu = dy_all.astype(_MXU).astype(F32)
        dcs_all = (_dot_onehot(dy_mxu * (y_all - xs_all * d_x), ex, NT) - _dot_onehot(xdt_mxu * dxdt, ex, NT)
                   + last_row * rows[0:1])
        dda = _dot_onehot(tri, dcs_all, TN, data=1)
        ddt = (ddt_all + dda * a_row) * realf
        ddt_raw = ddt * _sigmoid(dt_ref[...] + prm[0:1])
        ddt_ref[...] = ddt_raw.astype(_MXU)
        da_log = jnp.sum(dda * dt, axis=0, keepdims=True) * a_row
        dprm_ref[0:1, :] += jnp.sum(ddt_raw, axis=0, keepdims=True)
        dprm_ref[1:2, :] += da_log
        dprm_ref[2:3, :] += dd_row

    rev = lambda w, blk=0: pl.BlockSpec((CHUNK, w), lambda s, blk=blk: (N_CHUNKS - 1 - s, blk))
    return pl.pallas_call(
        body, name="ssd_bwd", grid=(N_CHUNKS,),
        in_specs=[rev(SSD_WIDTH, dyn_block), rev(SSD_WIDTH), rev(SSD_WIDTH), rev(SSD_WIDTH), rev(512), rev(128),
                  pl.BlockSpec((1,) + STATE_SHAPE, lambda s: (N_CHUNKS - 1 - s, 0, 0, 0)),
                  pl.BlockSpec((8, 128), lambda s: (0, 0)), pl.BlockSpec((1, SSD_WIDTH), lambda s: (0, 0)),
                  pl.BlockSpec((128, SSD_WIDTH), lambda s: (0, 0))],
        out_specs=[rev(SSD_WIDTH), rev(SSD_WIDTH), rev(512), rev(128),
                   pl.BlockSpec((8, 128), lambda s: (0, 0)), pl.BlockSpec((1, SSD_WIDTH), lambda s: (0, 0))],
        out_shape=[jax.ShapeDtypeStruct((T_ROWS, SSD_WIDTH), _MXU), jax.ShapeDtypeStruct((T_ROWS, SSD_WIDTH), F32),
                   jax.ShapeDtypeStruct((T_ROWS, 512), F32), jax.ShapeDtypeStruct((T_ROWS, 128), _MXU),
                   jax.ShapeDtypeStruct((8, 128), F32), jax.ShapeDtypeStruct((1, SSD_WIDTH), F32)],
        scratch_shapes=[pltpu.VMEM(STATE_SHAPE, F32)],
        compiler_params=_cparams("arbitrary"),
    )(dyn, z, y_pre, xs, bc, dt_raw, prev, prm, norm_w, ex)


LRU_PAIRS = 8


def _lru_gates(xr, wa_ref, wx_ref, prm):
    pre_r, pre_i = [], []
    for k in range(LRU_PAIRS):
        xk = xr[:, k * 128:(k + 1) * 128]
        pre_r.append(_dot(xk, wa_ref[k]))
        pre_i.append(_dot(xk, wx_ref[k]))
    r = _sigmoid(jnp.concatenate(pre_r, axis=1) + prm[0:1])
    i = _sigmoid(jnp.concatenate(pre_i, axis=1) + prm[1:2])
    sp = _softplus(-prm[2:3])
    log_a = (-LRU_C) * r * sp
    a = jnp.exp(log_a)
    s = jnp.sqrt(-jnp.tanh(log_a) * (a * a + 1.0))
    return r, i, a, s, sp


def _lru_fwd(xr, gate, wa, wx, prm):
    def body(xr_ref, g_ref, wa_ref, wx_ref, prm_ref, hs_ref, yn_ref, carry, a_s, u_s):
        @pl.when(pl.program_id(0) == 0)
        def _():
            carry[...] = jnp.zeros_like(carry)

        prm = prm_ref[...]
        xr_t = xr_ref[...]
        _, i, a, s, _ = _lru_gates(xr_t, wa_ref, wx_ref, prm)
        a_s[...] = a
        u_s[...] = s * (i * xr_t)
        rid = lax.broadcasted_iota(jnp.int32, (8, LRU_WIDTH), 0)

        def group(k, h):
            off = pl.multiple_of(k * 8, 8)
            a8 = a_s[pl.ds(off, 8), :]
            u8 = u_s[pl.ds(off, 8), :]
            out = jnp.zeros((8, LRU_WIDTH), F32)
            for r_ in range(8):
                h = a8[r_:r_ + 1] * h + u8[r_:r_ + 1]
                out = jnp.where(rid == r_, h, out)
            hs_ref[pl.ds(off, 8), :] = out
            return h

        carry[0:1, :] = lax.fori_loop(0, CHUNK // 8, group, carry[0:1, :])
        gel, _ = _gelu_and_grad(g_ref[...])
        yn_ref[...] = _rms_fwd(gel * hs_ref[...], prm[3:4]).astype(_MXU)

    row = pl.BlockSpec((CHUNK, LRU_WIDTH), lambda t: (t, 0))
    wspec = pl.BlockSpec((LRU_PAIRS, 128, 128), lambda t: (0, 0, 0))
    return pl.pallas_call(
        body, name="lru_fwd", grid=(N_CHUNKS,),
        in_specs=[row, row, wspec, wspec, pl.BlockSpec((8, LRU_WIDTH), lambda t: (0, 0))],
        out_specs=[row, row],
        out_shape=[jax.ShapeDtypeStruct((T_ROWS, LRU_WIDTH), F32), jax.ShapeDtypeStruct((T_ROWS, LRU_WIDTH), _MXU)],
        scratch_shapes=[pltpu.VMEM((8, LRU_WIDTH), F32), pltpu.VMEM((CHUNK, LRU_WIDTH), F32),
                        pltpu.VMEM((CHUNK, LRU_WIDTH), F32)],
        compiler_params=_cparams("arbitrary"),
    )(xr, gate, wa, wx, prm)


def _lru_bwd(dyn, dyn_block, gate, xr, hs, wa, wx, wa_t, wx_t, prm):
    def body(dyn_ref, g_ref, xr_ref, hs_ref, hsp_ref, wa_ref, wx_ref, wat_ref, wxt_ref, prm_ref,
             dg_ref, dxr_ref, dwa_ref, dwx_ref, dprm_ref, carry, a_s, d_s):
        step = pl.program_id(0)
        tile = N_CHUNKS - 1 - step

        @pl.when(step == 0)
        def _():
            carry[...] = jnp.zeros_like(carry)
            dwa_ref[...] = jnp.zeros_like(dwa_ref)
            dwx_ref[...] = jnp.zeros_like(dwx_ref)
            dprm_ref[...] = jnp.zeros_like(dprm_ref)

        prm = prm_ref[...]
        xr_t = xr_ref[...]
        r, i, a, s, sp = _lru_gates(xr_t, wa_ref, wx_ref, prm)
        hs_t = hs_ref[...]
        gel, dgel = _gelu_and_grad(g_ref[...])
        dy, dnw = _rms_bwd(gel * hs_t, prm[3:4], dyn_ref[...])
        dg_ref[...] = (dy * hs_t * dgel).astype(_MXU)
        a_s[...] = a
        d_s[...] = dy * gel
        rid = lax.broadcasted_iota(jnp.int32, (8, LRU_WIDTH), 0)

        def group(k, cr):
            off = pl.multiple_of((CHUNK // 8 - 1 - k) * 8, 8)
            a8 = a_s[pl.ds(off, 8), :]
            d8 = d_s[pl.ds(off, 8), :]
            out = jnp.zeros((8, LRU_WIDTH), F32)
            for r_ in reversed(range(8)):
                dht = d8[r_:r_ + 1] + cr
                out = jnp.where(rid == r_, dht, out)
                cr = a8[r_:r_ + 1] * dht
            d_s[pl.ds(off, 8), :] = out
            return cr

        carry[0:1, :] = lax.fori_loop(0, CHUNK // 8, group, carry[0:1, :])
        dht = d_s[...]
        before = hsp_ref[CHUNK - 8:CHUNK, :][7:8] * (tile > 0).astype(F32)
        first = lax.broadcasted_iota(jnp.int32, (CHUNK, 1), 0) == 0
        hprev = jnp.where(first, before, pltpu.roll(hs_t, 1, 0))
        da = dht * hprev
        ixr = i * xr_t
        ds = dht * ixr
        dlog_a = da * a - ds * (a * a) * lax.rsqrt(s * s)
        dr = dlog_a * ((-LRU_C) * sp)
        dsp = jnp.sum(dlog_a * ((-LRU_C) * r), axis=0, keepdims=True)
        dlam = dsp * (-_sigmoid(-prm[2:3]))
        di = dht * s * xr_t
        dpre_r = dr * r * (1.0 - r)
        dpre_i = di * i * (1.0 - i)
        dxr = dht * s * i
        parts = []
        for k in range(LRU_PAIRS):
            sl = slice(k * 128, (k + 1) * 128)
            parts.append(_dot(dpre_r[:, sl], wat_ref[k]) + _dot(dpre_i[:, sl], wxt_ref[k]))
            dwa_ref[k] += _dot(xr_t[:, sl], dpre_r[:, sl], TN)
            dwx_ref[k] += _dot(xr_t[:, sl], dpre_i[:, sl], TN)
        dxr_ref[...] = dxr + jnp.concatenate(parts, axis=1)
        dprm_ref[0:1, :] += jnp.sum(dpre_r, axis=0, keepdims=True)
        dprm_ref[1:2, :] += jnp.sum(dpre_i, axis=0, keepdims=True)
        dprm_ref[2:3, :] += dlam
        dprm_ref[3:4, :] += jnp.sum(dnw, axis=0, keepdims=True)

    rev = lambda blk=0: pl.BlockSpec((CHUNK, LRU_WIDTH), lambda s, blk=blk: (N_CHUNKS - 1 - s, blk))
    wspec = pl.BlockSpec((LRU_PAIRS, 128, 128), lambda s: (0, 0, 0))
    return pl.pallas_call(
        body, name="lru_bwd", grid=(N_CHUNKS,),
        in_specs=[rev(dyn_block), rev(), rev(), rev(),
                  pl.BlockSpec((CHUNK, LRU_WIDTH), lambda s: (jnp.maximum(N_CHUNKS - 2 - s, 0), 0)),
                  wspec, wspec, wspec, wspec, pl.BlockSpec((8, LRU_WIDTH), lambda s: (0, 0))],
        out_specs=[rev(), rev(), wspec, wspec, pl.BlockSpec((8, LRU_WIDTH), lambda s: (0, 0))],
        out_shape=[jax.ShapeDtypeStruct((T_ROWS, LRU_WIDTH), _MXU), jax.ShapeDtypeStruct((T_ROWS, LRU_WIDTH), F32),
                   jax.ShapeDtypeStruct((LRU_PAIRS, 128, 128), F32), jax.ShapeDtypeStruct((LRU_PAIRS, 128, 128), F32),
                   jax.ShapeDtypeStruct((8, LRU_WIDTH), F32)],
        scratch_shapes=[pltpu.VMEM((8, LRU_WIDTH), F32), pltpu.VMEM((CHUNK, LRU_WIDTH), F32),
                        pltpu.VMEM((CHUNK, LRU_WIDTH), F32)],
        compiler_params=_cparams("arbitrary"),
    )(dyn, gate, xr, hs, hs, wa, wx, wa_t, wx_t, prm)


SEC_NAMES = ("z", "xs", "bc", "dt", "g", "x")
SEC_WIDTH = {"z": 1024, "xs": 1024, "bc": 512, "dt": 128, "g": 1024, "x": 1024}


def _pair_blocks(w):
    w = w.reshape(LRU_PAIRS, 2, 64, 64)
    zero = jnp.zeros((LRU_PAIRS, 64, 64), w.dtype)
    top = jnp.concatenate([w[:, 0], zero], axis=2)
    bot = jnp.concatenate([zero, w[:, 1]], axis=2)
    return jnp.concatenate([top, bot], axis=1)


def _unpair_blocks(wp):
    return jnp.stack([wp[:, :64, :64], wp[:, 64:, 64:]], axis=1).reshape(16, 64, 64)


def _pad_lanes(v, width=128):
    return jnp.pad(v, ((0, 0), (0, width - v.shape[1])))


class _Resident:
    def __init__(self, w_out, w_gate, w_up, w_down):
        self._w_out, self._ffn = w_out, (w_gate, w_up, w_down)

    def w_out(self, after):
        return self._w_out

    def ffn(self, after):
        return self._ffn

    def grads_ready(self, names, g, g_mxu):
        return jnp.zeros((1, 1), F32)

    def small_ready(self, g, loss):
        return jnp.zeros((1, 1), F32)

    def small_middle(self, after):
        return jnp.zeros((1, 1), F32)


def _local_step(x, target, meta, p, late):
    g, g_mxu = {}, {}
    ex = _head_expander()
    h0 = _embed(x, meta)
    u1, projs = _norm_proj(h0, p["norm1_w"], [p["w_in_" + s] for s in SEC_NAMES], name="norm_in_proj")
    proj = dict(zip(SEC_NAMES, projs))
    ssd_prm = jnp.concatenate([_pad_lanes(p["ssd_dt_bias"]), _pad_lanes(p["ssd_a_log"]), _pad_lanes(p["ssd_d"]),
                               jnp.zeros((5, 128), F32)], axis=0)
    xs_act = _conv_fwd(proj["xs"], p["ssd_conv_w"][:, :SSD_WIDTH], p["ssd_conv_b"][:, :SSD_WIDTH], silu=True,
                       name="ssd_conv_xs")
    bc_act = _conv_fwd(proj["bc"], p["ssd_conv_w"][:, SSD_WIDTH:], p["ssd_conv_b"][:, SSD_WIDTH:], silu=True,
                       name="ssd_conv_bc")
    y_pre, y_ssd, prev = _ssd_fwd(xs_act, bc_act, proj["dt"], proj["z"], ssd_prm, p["ssd_norm_w"], ex)
    xr = _conv_fwd(proj["x"], p["lru_conv_w"], p["lru_conv_b"], silu=False, name="lru_conv")
    wa_p, wx_p = _pair_blocks(p["lru_wa"]), _pair_blocks(p["lru_wx"])
    lru_prm = jnp.concatenate([p["lru_ba"], p["lru_bx"], p["lru_lambda"], p["lru_norm_w"],
                               jnp.zeros((4, LRU_WIDTH), F32)], axis=0)
    hs, y_lru = _lru_fwd(xr, proj["g"], wa_p.astype(_MXU), wx_p.astype(_MXU), lru_prm)
    ycat = jnp.concatenate([y_ssd, y_lru], axis=1)
    w_out = late.w_out(ycat)
    h1 = _mm([(ycat, 0, w_out, 0, 2 * D_MODEL)], T_ROWS, D_MODEL, tm=T_ROWS, tn=256, mode="nn", out_dtype=F32,
             name="out_proj", residual=h0)
    u2 = _rmsnorm(h1, p["norm2_w"], name="norm2")
    w_gate, w_up, w_down = late.ffn(u2)
    gp, up, act = _ffn_up(u2, w_gate, w_up)
    h2 = _mm([(act, 0, w_down, 0, D_FF)], T_ROWS, D_MODEL, tm=T_ROWS, tn=256, mode="nn", out_dtype=F32,
             name="ffn_down", residual=h1)
    loss, dh2, dh2b, g["final_norm_w"] = _loss_head(h2, target, p["final_norm_w"])
    dgp, dup = _ffn_bwd_act(dh2b, w_down, gp, up)
    g["w_down"], g_mxu["w_down"] = _mm([(act, 0, dh2b, 0, T_ROWS)], D_FF, D_MODEL, tm=1408, tn=512, mode="tn",
                                       out_dtype=F32, name="dw_down", also_mxu=True)
    dh1, dh1b, g["norm2_w"] = _mm_norm_bwd([(dgp, w_gate, D_FF), (dup, w_up, D_FF)], h1, p["norm2_w"], dh2,
                                           name="ffn_bwd_in")
    g["w_gate"], g_mxu["w_gate"] = _mm([(dgp, 0, u2, 0, T_ROWS)], D_FF, D_MODEL, tm=1408, tn=512, mode="tn",
                                       out_dtype=F32, name="dw_gate", also_mxu=True)
    g["w_up"], g_mxu["w_up"] = _mm([(dup, 0, u2, 0, T_ROWS)], D_FF, D_MODEL, tm=1408, tn=512, mode="tn",
                                   out_dtype=F32, name="dw_up", also_mxu=True)
    sent = late.grads_ready(("w_down", "w_gate", "w_up"), g, g_mxu)
    g["w_out"], g_mxu["w_out"] = _mm([(ycat, 0, dh1b, 0, T_ROWS)], 2 * D_MODEL, D_MODEL, tm=1024, tn=512, mode="tn",
                                     out_dtype=F32, name="dw_out", also_mxu=True, behind=(sent,))
    sent = late.grads_ready(("w_out",), g, g_mxu)
    dycat = _mm([(dh1b, 0, w_out, 0, D_MODEL)], T_ROWS, 2 * D_MODEL, tm=T_ROWS, tn=256, mode="nt", out_dtype=F32,
                name="out_proj_bwd", behind=(sent,))
    dgate, dxr, dwa_p, dwx_p, dlru_prm = _lru_bwd(dycat, 1, proj["g"], xr, hs, wa_p.astype(_MXU), wx_p.astype(_MXU),
                                                  jnp.swapaxes(wa_p, 1, 2).astype(_MXU),
                                                  jnp.swapaxes(wx_p, 1, 2).astype(_MXU), lru_prm)
    g["lru_wa"], g["lru_wx"] = _unpair_blocks(dwa_p), _unpair_blocks(dwx_p)
    g["lru_ba"], g["lru_bx"], g["lru_lambda"], g["lru_norm_w"] = (dlru_prm[k:k + 1] for k in range(4))
    dx_lru, g["lru_conv_w"], g["lru_conv_b"] = _conv_bwd(dxr, proj["x"], p["lru_conv_w"], p["lru_conv_b"], silu=False,
                                                         name="lru_conv_bwd")
    dz, dxs_act, dbc_act, ddt, dssd_prm, g["ssd_norm_w"] = _ssd_bwd(dycat, 0, proj["z"], y_pre, xs_act, bc_act,
                                                                    proj["dt"], prev, ssd_prm, p["ssd_norm_w"], ex)
    g["ssd_dt_bias"], g["ssd_a_log"], g["ssd_d"] = (dssd_prm[k:k + 1, :SSD_HEADS] for k in range(3))
    dxs, dcw_xs, dcb_xs = _conv_bwd(dxs_act, proj["xs"], p["ssd_conv_w"][:, :SSD_WIDTH],
                                    p["ssd_conv_b"][:, :SSD_WIDTH], silu=True, name="ssd_conv_xs_bwd")
    dbc, dcw_bc, dcb_bc = _conv_bwd(dbc_act, proj["bc"], p["ssd_conv_w"][:, SSD_WIDTH:],
                                    p["ssd_conv_b"][:, SSD_WIDTH:], silu=True, name="ssd_conv_bc_bwd")
    g["ssd_conv_w"] = jnp.concatenate([dcw_xs, dcw_bc], axis=1)
    g["ssd_conv_b"] = jnp.concatenate([dcb_xs, dcb_bc], axis=1)
    dproj = {"z": dz, "xs": dxs, "bc": dbc, "dt": ddt, "g": dgate, "x": dx_lru}
    dh0, _, g["norm1_w"] = _mm_norm_bwd([(dproj[s], p["w_in_" + s], SEC_WIDTH[s]) for s in SEC_NAMES], h0,
                                        p["norm1_w"], dh1, name="in_proj_bwd")
    g["meta_tokens"] = dh0[PAD_ROWS:X_ROW0]
    sent = late.small_ready(g, loss)
    for s in SEC_NAMES:
        wdt = SEC_WIDTH[s]
        g["w_in_" + s], g_mxu["w_in_" + s] = _mm([(dproj[s], 0, u1, 0, T_ROWS)], wdt, D_MODEL, tm=min(wdt, 1024),
                                                 tn=512, mode="tn", out_dtype=F32, name="dw_in_" + s, also_mxu=True,
                                                 behind=(sent,))
        if s == "bc":
            sent = late.small_middle(g["w_in_bc"])
    return loss, dh0[X_ROW0:], g, g_mxu


MESH = pl.DeviceIdType.MESH
ANY = pl.BlockSpec(memory_space=pl.ANY)


def _my_place():
    return lax.axis_index("x"), lax.axis_index("y"), lax.axis_index("c")


def _other_chips(x, y):
    return [(1 - x, y), (x, 1 - y), (1 - x, 1 - y)]


def _gather_first(big, small):
    half = big.shape[1] // 2

    def body(big_ref, small_ref, big4, small4, send_sems, recv_sems, local_sems):
        x, y, c = _my_place()
        me = 2 * x + y
        sibling = (x, y, 1 - c)
        peers = _other_chips(x, y)
        mine = pl.ds(pl.multiple_of(c * half, 128), half)
        theirs = pl.ds(pl.multiple_of((1 - c) * half, 128), half)

        def copy(k, src, dst, dev):
            return pltpu.make_async_remote_copy(src_ref=src, dst_ref=dst, send_sem=send_sems.at[k],
                                                recv_sem=recv_sems.at[k], device_id=dev, device_id_type=MESH)

        local = [pltpu.make_async_copy(big_ref, big4.at[me], local_sems.at[0]),
                 pltpu.make_async_copy(small_ref, small4.at[me], local_sems.at[1])]
        for cp in local:
            cp.start()
        first = []
        for j, (px, py) in enumerate(peers):
            first.append(copy(j, big_ref.at[:, mine], big4.at[me, :, mine], (px, py, c)))
            first.append(copy(3 + j, small_ref, small4.at[me], (px, py, c)))
        for cp in first:
            cp.start()
        passed = []
        for j, (px, py) in enumerate(peers):
            slot = 2 * px + py
            copy(j, big_ref.at[:, mine], big4.at[slot, :, mine], (px, py, c)).wait_recv()
            passed.append(copy(6 + j, big4.at[slot, :, mine], big4.at[slot, :, mine], sibling))
            passed[-1].start()
        for j, (px, py) in enumerate(peers):
            slot = 2 * px + py
            copy(6 + j, big4.at[slot, :, theirs], big4.at[slot, :, theirs], sibling).wait_recv()
            copy(3 + j, small_ref, small4.at[slot], (px, py, c)).wait_recv()
        for cp in first + passed:
            cp.wait_send()
        for cp in local:
            cp.wait()

    return pl.pallas_call(
        body, name="gather_first", in_specs=[ANY, ANY], out_specs=[ANY, ANY],
        out_shape=[jax.ShapeDtypeStruct((N_SHARDS,) + big.shape, big.dtype),
                   jax.ShapeDtypeStruct((N_SHARDS,) + small.shape, small.dtype)],
        scratch_shapes=[pltpu.SemaphoreType.DMA((9,)), pltpu.SemaphoreType.DMA((9,)), pltpu.SemaphoreType.DMA((2,))],
    )(big, small)


HBM_SPEC = pl.BlockSpec(memory_space=pltpu.HBM)
SEM_SPEC = pl.BlockSpec(memory_space=pltpu.SEMAPHORE)
SPLIT_EFFECT = pltpu.SideEffectType.DATAFLOW_SIDE_EFFECTING


def _gather_plan(bufs, x, y, c, incoming):
    plan = []
    for buf in bufs:
        for (px, py) in _other_chips(x, y):
            slot = 2 * px + py if incoming else 2 * x + y
            plan.append((buf.at[2 * x + y], buf.at[slot], (px, py, c)))
    return plan


def _scatter_plan(bufs, x, y, c, incoming):
    n = len(bufs) // 2
    plan = []
    for k in range(n):
        for j, (px, py) in enumerate(_other_chips(x, y)):
            plan.append((bufs[k].at[2 * px + py], bufs[n + k].at[j], (px, py, c)))
    return plan


def _split_start(bufs, plan, n_copies, after, *, name):
    n = len(bufs)
    extra = [] if after is None else [after]

    def body(*refs):
        ins = refs[:n]
        send_sems, recv_sems = refs[n + len(extra)], refs[n + len(extra) + 1]
        token = refs[-1]
        x, y, c = _my_place()
        for i, (src, dst, dev) in enumerate(plan(ins, x, y, c, False)):
            pltpu.make_async_remote_copy(src_ref=src, dst_ref=dst, send_sem=send_sems.at[i], recv_sem=recv_sems.at[i],
                                         device_id=dev, device_id_type=MESH).start()
        token[...] = jnp.zeros_like(token)

    outs = pl.pallas_call(
        body, name=name,
        out_shape=(pltpu.SemaphoreType.DMA((n_copies,)), pltpu.SemaphoreType.DMA((n_copies,)),
                   *[pltpu.HBM(b.shape, b.dtype) for b in bufs], jax.ShapeDtypeStruct((8, 128), F32)),
        in_specs=[HBM_SPEC] * n + [ANY] * len(extra),
        out_specs=(SEM_SPEC, SEM_SPEC, *[HBM_SPEC] * n, pl.BlockSpec(memory_space=pltpu.VMEM)),
        input_output_aliases={k: 2 + k for k in range(n)},
        compiler_params=pltpu.CompilerParams(has_side_effects=SPLIT_EFFECT),
    )(*[pltpu.with_memory_space_constraint(b, pltpu.HBM) for b in bufs], *extra)
    return outs[0], outs[1], list(outs[2:2 + n]), outs[-1]


def _split_wait(bufs, send_sems, recv_sems, plan, after, *, name):
    n = len(bufs)

    def body(*refs):
        ins = refs[:n]
        send_sems_ref, recv_sems_ref = refs[n], refs[n + 1]
        x, y, c = _my_place()
        for i, (src, dst, dev) in enumerate(plan(ins, x, y, c, True)):
            cp = pltpu.make_async_remote_copy(src_ref=src, dst_ref=dst, send_sem=send_sems_ref.at[i],
                                              recv_sem=recv_sems_ref.at[i], device_id=dev, device_id_type=MESH)
            cp.wait_send()
            cp.wait_recv()

    outs = pl.pallas_call(
        body, name=name, out_shape=tuple(pltpu.HBM(b.shape, b.dtype) for b in bufs),
        in_specs=[HBM_SPEC] * n + [SEM_SPEC, SEM_SPEC, ANY], out_specs=tuple([HBM_SPEC] * n),
        input_output_aliases={k: k for k in range(n)},
        compiler_params=pltpu.CompilerParams(has_side_effects=SPLIT_EFFECT),
    )(*bufs, send_sems, recv_sems, after)
    return list(outs)


def _fill_own_slot(shard, me_arr, *, name):
    r, c = shard.shape
    tile, steps, imap = _elementwise_tile(r, c)

    def body(me_ref, x_ref, o_ref):
        o_ref[0] = x_ref[...].astype(_MXU)

    return pl.pallas_call(
        body, name=name,
        grid_spec=pltpu.PrefetchScalarGridSpec(
            num_scalar_prefetch=1, grid=(steps,),
            in_specs=[pl.BlockSpec(tile, lambda i, me: imap(i))],
            out_specs=pl.BlockSpec((1,) + tile, lambda i, me: (me[0],) + imap(i))),
        out_shape=jax.ShapeDtypeStruct((N_SHARDS, r, c), _MXU),
        compiler_params=_cparams("parallel"),
    )(me_arr, shard)


def _swap_with_sibling(parts, *, name):
    n = len(parts)

    def body(*refs):
        ins, outs = refs[:n], refs[n:2 * n]
        send_sems, recv_sems = refs[2 * n:]
        x, y, c = _my_place()
        copies = [pltpu.make_async_remote_copy(
            src_ref=ins[k], dst_ref=outs[k], send_sem=send_sems.at[k], recv_sem=recv_sems.at[k],
            device_id=(x, y, 1 - c), device_id_type=MESH) for k in range(n)]
        for cp in copies:
            cp.start()
        for cp in copies:
            cp.wait()

    return pl.pallas_call(
        body, name=name, in_specs=[ANY] * n, out_specs=[ANY] * n,
        out_shape=[jax.ShapeDtypeStruct(a.shape, a.dtype) for a in parts],
        scratch_shapes=[pltpu.SemaphoreType.DMA((n,)), pltpu.SemaphoreType.DMA((n,))],
    )(*parts)


def _other_devices(x, y, c):
    out = []
    for mask in range(1, N_DEV):
        px, py, pc = x ^ (mask >> 2 & 1), y ^ (mask >> 1 & 1), c ^ (mask & 1)
        out.append(((px, py, pc), 4 * px + 2 * py + pc))
    return out


def _pieces_plan(bufs, x, y, c, incoming):
    pack, land = bufs
    me = 4 * x + 2 * y + c
    return [(pack.at[num], land.at[num if incoming else me], dev) for dev, num in _other_devices(x, y, c)]


def _spread_plan(bufs, x, y, c, incoming):
    piece, land = bufs
    me = 4 * x + 2 * y + c
    return [(piece, land.at[num if incoming else me], dev) for dev, num in _other_devices(x, y, c)]


def _sum_pieces(pack, land, dev_arr, *, name):
    def body(dev_ref, pack_ref, land_ref, o_ref):
        dev = dev_ref[0]
        own = pack_ref[dev]
        acc = None
        for d in range(N_DEV):
            term = jnp.where(dev == d, own, land_ref[d])
            acc = term if acc is None else acc + term
        o_ref[...] = acc

    vmem = pl.BlockSpec(memory_space=pltpu.VMEM)
    return pl.pallas_call(
        body, name=name, in_specs=[pl.BlockSpec(memory_space=pltpu.SMEM), vmem, vmem], out_specs=vmem,
        out_shape=jax.ShapeDtypeStruct(pack.shape[1:], F32),
    )(dev_arr, pack, land)


def _join_pieces(piece, land, dev_arr, *, name):
    def body(dev_ref, piece_ref, land_ref, o_ref):
        dev = dev_ref[0]
        for d in range(N_DEV):
            o_ref[d] = jnp.where(dev == d, piece_ref[...], land_ref[d])

    vmem = pl.BlockSpec(memory_space=pltpu.VMEM)
    return pl.pallas_call(
        body, name=name, in_specs=[pl.BlockSpec(memory_space=pltpu.SMEM), vmem, vmem], out_specs=vmem,
        out_shape=jax.ShapeDtypeStruct(land.shape, F32),
    )(dev_arr, piece, land)


def _adamw_native(ws, gs, ms, vs):
    n = len(ws)

    def body(*refs):
        for k in range(n):
            w_ref, g_ref, m_ref, v_ref = (refs[j * n + k] for j in range(4))
            delta, m_new, v_new = _adamw_math(w_ref[...], g_ref[...], m_ref[...], v_ref[...])
            refs[4 * n + k][...] = delta
            refs[5 * n + k][...] = m_new
            refs[6 * n + k][...] = v_new

    vmem = pl.BlockSpec(memory_space=pltpu.VMEM)
    shapes = [jax.ShapeDtypeStruct(a.shape, F32) for a in ws]
    outs = pl.pallas_call(
        body, name="adamw_small", in_specs=[vmem] * (4 * n), out_specs=[vmem] * (3 * n), out_shape=shapes * 3,
        compiler_params=pltpu.CompilerParams(vmem_limit_bytes=VMEM_LIMIT_BYTES),
    )(*ws, *gs, *ms, *vs)
    return outs[:n], outs[n:2 * n], outs[2 * n:]


def _elementwise_tile(rows, cols, limit=256):
    for t in range(limit, 15, -16):
        if rows % t == 0:
            return (t, cols), rows // t, lambda i: (i, 0)
    assert cols % limit == 0
    return (rows, limit), cols // limit, lambda i: (0, i)


def _partial_sum(own, land, me_arr, *, name):
    r, c = own.shape[-2:]
    tile, steps, imap = _elementwise_tile(r, c)
    whole = own.ndim == 3

    def body(me_ref, own_ref, land_ref, o_ref):
        acc = own_ref[0] if whole else own_ref[...]
        for j in range(3):
            acc = acc + land_ref[j].astype(F32)
        o_ref[...] = acc

    own_spec = (pl.BlockSpec((1,) + tile, lambda i, me: (me[0],) + imap(i)) if whole
                else pl.BlockSpec(tile, lambda i, me: imap(i)))
    return pl.pallas_call(
        body, name=name,
        grid_spec=pltpu.PrefetchScalarGridSpec(
            num_scalar_prefetch=1, grid=(steps,),
            in_specs=[own_spec, pl.BlockSpec((3,) + tile, lambda i, me: (0,) + imap(i))],
            out_specs=pl.BlockSpec(tile, lambda i, me: imap(i))),
        out_shape=jax.ShapeDtypeStruct((r, c), F32),
        compiler_params=_cparams("parallel"),
    )(me_arr, own, land)


def _adamw_math(w, g, m, v):
    m = ADAM_B1 * m + (1.0 - ADAM_B1) * g
    v = ADAM_B2 * v + (1.0 - ADAM_B2) * (g * g)
    m_hat = m / (1.0 - ADAM_B1 ** ADAM_STEP)
    v_hat = v / (1.0 - ADAM_B2 ** ADAM_STEP)
    delta = -ADAM_LR * (m_hat / (jnp.sqrt(v_hat) + ADAM_EPS) + ADAM_WD * w)
    return delta, m, v


def _adamw(w, grad_parts, m, v, *, name):
    r, c = w.shape
    tile_shape, steps, imap = _elementwise_tile(r, c)
    n = len(grad_parts)

    def body(*refs):
        w_ref, m_ref, v_ref = refs[:3]
        g_refs = refs[3:3 + n]
        g_out, d_out, m_out, v_out = refs[3 + n:]
        g = g_refs[0][...]
        for k in range(1, n):
            g = g + g_refs[k][...]
        delta, m_new, v_new = _adamw_math(w_ref[...], g, m_ref[...], v_ref[...])
        g_out[...] = g
        d_out[...] = delta
        m_out[...] = m_new
        v_out[...] = v_new

    tile = pl.BlockSpec(tile_shape, imap)
    return pl.pallas_call(
        body, name=name, grid=(steps,), in_specs=[tile] * (3 + n), out_specs=[tile] * 4,
        out_shape=[jax.ShapeDtypeStruct((r, c), F32)] * 4,
        compiler_params=_cparams("parallel"),
    )(w, m, v, *grad_parts)


WEIGHT_NAMES = ("meta_tokens", "norm1_w", "w_in", "ssd_conv_w", "ssd_conv_b", "ssd_dt_bias", "ssd_a_log", "ssd_d",
                "ssd_norm_w", "lru_conv_w", "lru_conv_b", "lru_wa", "lru_ba", "lru_wx", "lru_bx", "lru_lambda",
                "lru_norm_w", "w_out", "norm2_w", "w_gate", "w_up", "w_down", "final_norm_w")
BIG = ("w_in", "w_out", "w_gate", "w_up", "w_down")
FFN = ("w_gate", "w_up", "w_down")
LATE = ("w_out",) + FFN
SMALL_SHARDED = {"meta_tokens": (N_META, D_MODEL), "ssd_conv_w": (CONV_K, 1536), "lru_conv_w": (CONV_K, LRU_WIDTH)}
SMALL = tuple(n for n in WEIGHT_NAMES if n not in BIG)
PACK_COLS = 1024


def _pack(arrays, row_multiple):
    flat = jnp.concatenate([a.reshape(-1) for a in arrays])
    rows = -(-flat.shape[0] // (row_multiple * PACK_COLS)) * row_multiple
    return jnp.pad(flat, (0, rows * PACK_COLS - flat.shape[0])).reshape(rows, PACK_COLS)


def _unpack(pack, shapes):
    flat = pack.reshape(-1)
    out, off = [], 0
    for s in shapes:
        size = math.prod(s)
        out.append(flat[off:off + size].reshape(s))
        off += size
    return out


def _unshard_cols(g4):
    return jnp.swapaxes(g4, 0, 1).reshape(g4.shape[1], -1)


COL_SHARDED = ("w_in", "w_gate", "w_up")
IN_ROWS = {"z": (0, 1024), "xs": (1024, 2048), "bc": (2048, 2560), "dt": (2560, 2576), "g": (2576, 3600),
           "x": (3600, IN_COLS)}


def _w_in_shard_rows(k, sections):
    lo, hi = k * (IN_COLS // N_SHARDS), (k + 1) * (IN_COLS // N_SHARDS)
    parts = []
    for arr, (a, b) in zip(sections, IN_ROWS.values()):
        if max(lo, a) < min(hi, b):
            parts.append(arr[max(lo, a) - a:min(hi, b) - a])
    return jnp.concatenate(parts, axis=0)


def _rows_view(name, block):
    return jnp.swapaxes(block[0], 0, 1) if name in COL_SHARDED else block[0]


def _param_view(name, rows):
    return (jnp.swapaxes(rows, 0, 1) if name in COL_SHARDED else rows)[None]


def kernel(x, meta_tokens, norm1_w, w_in, ssd_conv_w, ssd_conv_b, ssd_dt_bias, ssd_a_log, ssd_d, ssd_norm_w, lru_conv_w, lru_conv_b, lru_wa, lru_ba, lru_wx, lru_bx, lru_lambda, lru_norm_w, w_out, norm2_w, w_gate, w_up, w_down, final_norm_w, loss_target, m_meta_tokens, m_norm1_w, m_w_in, m_ssd_conv_w, m_ssd_conv_b, m_ssd_dt_bias, m_ssd_a_log, m_ssd_d, m_ssd_norm_w, m_lru_conv_w, m_lru_conv_b, m_lru_wa, m_lru_ba, m_lru_wx, m_lru_bx, m_lru_lambda, m_lru_norm_w, m_w_out, m_norm2_w, m_w_gate, m_w_up, m_w_down, m_final_norm_w, v_meta_tokens, v_norm1_w, v_w_in, v_ssd_conv_w, v_ssd_conv_b, v_ssd_dt_bias, v_ssd_a_log, v_ssd_d, v_ssd_norm_w, v_lru_conv_w, v_lru_conv_b, v_lru_wa, v_lru_ba, v_lru_wx, v_lru_bx, v_lru_lambda, v_lru_norm_w, v_w_out, v_norm2_w, v_w_gate, v_w_up, v_w_down, v_final_norm_w):
    w = dict(zip(WEIGHT_NAMES, (meta_tokens, norm1_w, w_in, ssd_conv_w, ssd_conv_b, ssd_dt_bias, ssd_a_log, ssd_d, ssd_norm_w, lru_conv_w, lru_conv_b, lru_wa, lru_ba, lru_wx, lru_bx, lru_lambda, lru_norm_w, w_out, norm2_w, w_gate, w_up, w_down, final_norm_w)))
    m = dict(zip(WEIGHT_NAMES, (m_meta_tokens, m_norm1_w, m_w_in, m_ssd_conv_w, m_ssd_conv_b, m_ssd_dt_bias, m_ssd_a_log, m_ssd_d, m_ssd_norm_w, m_lru_conv_w, m_lru_conv_b, m_lru_wa, m_lru_ba, m_lru_wx, m_lru_bx, m_lru_lambda, m_lru_norm_w, m_w_out, m_norm2_w, m_w_gate, m_w_up, m_w_down, m_final_norm_w)))
    v = dict(zip(WEIGHT_NAMES, (v_meta_tokens, v_norm1_w, v_w_in, v_ssd_conv_w, v_ssd_conv_b, v_ssd_dt_bias, v_ssd_a_log, v_ssd_d, v_ssd_norm_w, v_lru_conv_w, v_lru_conv_b, v_lru_wa, v_lru_ba, v_lru_wx, v_lru_bx, v_lru_lambda, v_lru_norm_w, v_w_out, v_norm2_w, v_w_gate, v_w_up, v_w_down, v_final_norm_w)))
    me = 2 * lax.axis_index("x") + lax.axis_index("y")

    big2d = {n: _rows_view(n, w[n]) for n in BIG}
    small_local = jnp.concatenate([w["meta_tokens"].reshape(-1), w["ssd_conv_w"].reshape(-1),
                                   w["lru_conv_w"].reshape(-1)])[None]
    me_arr = me.astype(jnp.int32).reshape(1)
    dev_arr = (2 * me + lax.axis_index("c")).astype(jnp.int32).reshape(1)
    w_in4, small4 = _gather_first(big2d["w_in"].astype(_MXU), small_local)
    w_in_full = w_in4.reshape(-1, D_MODEL)
    sm = small4[:, 0]
    meta_full = _unshard_cols(sm[:, :4096].reshape(N_SHARDS, N_META, 256))
    ssd_conv_w_full = _unshard_cols(sm[:, 4096:5632].reshape(N_SHARDS, CONV_K, 384))
    lru_conv_w_full = _unshard_cols(sm[:, 5632:].reshape(N_SHARDS, CONV_K, 256))
    slots = {n: _fill_own_slot(big2d[n], me_arr, name="own_slot_" + n) for n in LATE}
    out_send, out_recv, out_bufs, tok_a = _split_start([slots["w_out"]], _gather_plan, 3, small4,
                                                       name="gather_w_out_start")
    ffn_send, ffn_recv, ffn_bufs, tok_b = _split_start([slots[n] for n in FFN], _gather_plan, 9, tok_a,
                                                       name="gather_ffn_start")

    p = {"w_in_" + s: w_in_full[lo:hi] for s, (lo, hi) in IN_ROWS.items()}
    p["w_in_dt"] = jnp.pad(p["w_in_dt"], ((0, SEC_WIDTH["dt"] - SSD_HEADS), (0, 0)))
    p.update({"ssd_conv_w": ssd_conv_w_full, "lru_conv_w": lru_conv_w_full,
              "lru_wa": w["lru_wa"][0], "lru_wx": w["lru_wx"][0], "final_norm_w": w["final_norm_w"][None]})
    for n in ("norm1_w", "ssd_conv_b", "ssd_dt_bias", "ssd_a_log", "ssd_d", "ssd_norm_w", "lru_conv_b", "lru_ba",
              "lru_bx", "lru_lambda", "lru_norm_w", "norm2_w"):
        p[n] = w[n]
    p["norm1_w"] = p["norm1_w"] + tok_b[:1, :1]

    class Late:
        def __init__(self):
            self.pending = []

        def w_out(self, after):
            (buf,) = _split_wait(out_bufs, out_send, out_recv, _gather_plan, after, name="gather_w_out_wait")
            return buf.reshape(-1, D_MODEL)

        def ffn(self, after):
            bufs = _split_wait(ffn_bufs, ffn_send, ffn_recv, _gather_plan, after, name="gather_ffn_wait")
            return tuple(b.reshape(-1, D_MODEL) for b in bufs)

        def grads_ready(self, names, g, g_mxu):
            srcs = [g_mxu[n].reshape(N_SHARDS, -1, D_MODEL) for n in names]
            lands = [lax.empty((3,) + s.shape[1:], _MXU) for s in srcs]
            tag = "_".join(names)
            send, recv, bufs, tok = _split_start(srcs + lands, _scatter_plan, 3 * len(names), None,
                                                 name="scatter_" + tag + "_start")
            self.pending.append((names, send, recv, bufs, tag))
            self.in_flight = bufs[0]
            return tok[:1, :1]

        def landed(self, after, which):
            land = {}
            for names, send, recv, bufs, tag in self.pending:
                if names[0] in which:
                    bufs = _split_wait(bufs, send, recv, _scatter_plan, after, name="scatter_" + tag + "_wait")
                    land.update(zip(names, bufs[len(names):]))
            return land

        def small_ready(self, g, loss):
            pack = _pack([g[n] for n in SMALL] + [loss[0, :1]], 8 * N_DEV)
            pack = pack.reshape(N_DEV, -1, PACK_COLS)
            self.small = _split_start([pack, lax.empty(pack.shape, F32)], _pieces_plan, N_DEV - 1, loss,
                                      name="small_pieces_start")
            return self.small[3]

        def small_middle(self, after):
            send, recv, bufs, _ = self.small
            pack, land = _split_wait(bufs, send, recv, _pieces_plan, after, name="small_pieces_wait")
            piece = _sum_pieces(pack, land, dev_arr, name="small_pieces_sum")
            self.small = _split_start([piece, lax.empty(pack.shape, F32)], _spread_plan, N_DEV - 1, None,
                                      name="small_spread_start")
            return self.small[3]

        def small_sum(self, after):
            send, recv, bufs, _ = self.small
            piece, land = _split_wait(bufs, send, recv, _spread_plan, after, name="small_spread_wait")
            return _join_pieces(piece, land, dev_arr, name="small_join")

    late = Late()

    loss, grad_x, g, g_mxu = _local_step(x[0], loss_target[0], meta_full, p, late)

    g_mxu["w_in"] = jnp.concatenate([g_mxu["w_in_" + s][:hi - lo] for s, (lo, hi) in IN_ROWS.items()], axis=0)
    g4 = {n: g[n].reshape(N_SHARDS, -1, D_MODEL) for n in LATE}
    g4["w_in"] = lax.switch(me, [functools.partial(_w_in_shard_rows, k) for k in range(N_SHARDS)],
                            [g["w_in_" + s] for s in SEC_NAMES])
    late.grads_ready(("w_in",), g, g_mxu)
    land = late.landed(late.in_flight, LATE)
    part = {n: _partial_sum(g4[n], land[n], me_arr, name="partial_" + n) for n in LATE}
    sib = dict(zip(LATE, _swap_with_sibling([part[n] for n in LATE], name="swap_late")))

    small_full_shape = {n: (SMALL_SHARDED[n] if n in SMALL_SHARDED else w[n].shape) for n in SMALL}
    red_list = _unpack(late.small_sum(sib["w_out"]), [small_full_shape[n] for n in SMALL] + [(1,)])
    loss_total = red_list[-1][0]
    g_small = {}
    for n, arr in zip(SMALL, red_list[:-1]):
        if n in SMALL_SHARDED:
            cols = SMALL_SHARDED[n][1] // N_SHARDS
            arr = lax.dynamic_slice_in_dim(arr, me * cols, cols, axis=1)
        g_small[n] = arr.reshape(w[n].shape)

    grad, delta, new_m, new_v = {}, {}, {}, {}

    def update_big(n):
        outs = _adamw(big2d[n], [part[n], sib[n]], _rows_view(n, m[n]), _rows_view(n, v[n]), name="adamw_" + n)
        grad[n], delta[n], new_m[n], new_v[n] = (_param_view(n, o) for o in outs)
        return outs[0]

    two_d = lambda a: a.reshape(1, -1) if a.ndim == 1 else a
    deltas, new_ms, new_vs = _adamw_native(*[[two_d(d[n]) for n in SMALL] for d in (w, g_small, m, v)])
    for n, dn, mn, vn in zip(SMALL, deltas, new_ms, new_vs):
        grad[n], delta[n], new_m[n], new_v[n] = (g_small[n], dn.reshape(w[n].shape), mn.reshape(w[n].shape),
                                                 vn.reshape(w[n].shape))
    for n in LATE:
        last = update_big(n)
    land.update(late.landed(last, ("w_in",)))
    part["w_in"] = _partial_sum(g4["w_in"], land["w_in"], me_arr, name="partial_w_in")
    (sib["w_in"],) = _swap_with_sibling([part["w_in"]], name="swap_w_in")
    update_big("w_in")

    return (loss_total, grad_x[None], *[grad[n] for n in WEIGHT_NAMES], *[delta[n] for n in WEIGHT_NAMES],
            *[new_m[n] for n in WEIGHT_NAMES], *[new_v[n] for n in WEIGHT_NAMES])
```

```python
import functools
import math

import jax
import jax.numpy as jnp
from jax import lax
from jax.experimental import pallas as pl
from jax.experimental.pallas import tpu as pltpu

F32 = jnp.float32
_MXU = jnp.bfloat16

D_MODEL = 1024
SEQ = 2048
N_META = 16
CHUNK = 128
T_ROWS = 2176
N_CHUNKS = T_ROWS // CHUNK
PAD_ROWS = T_ROWS - SEQ - N_META
X_ROW0 = PAD_ROWS + N_META
SSD_HEADS = 16
SSD_HEAD_DIM = 64
SSD_STATE = 128
SSD_GROUPS = 2
SSD_HPG = SSD_HEADS // SSD_GROUPS
SSD_WIDTH = 1024
LRU_WIDTH = 1024
LRU_C = 8.0
D_FF = 2816
EPS = 1e-6
IN_COLS = 4624
N_SHARDS = 4
N_DEV = 8

ADAM_LR = 0.001
ADAM_B1 = 0.9
ADAM_B2 = 0.999
ADAM_EPS = 1e-08
ADAM_WD = 0.01
ADAM_STEP = 10

VMEM_LIMIT_BYTES = 56 * 1024 * 1024

NN = (((1,), (0,)), ((), ()))
NT = (((1,), (1,)), ((), ()))
TN = (((0,), (0,)), ((), ()))


def _cparams(*sem):
    return pltpu.CompilerParams(dimension_semantics=sem, vmem_limit_bytes=VMEM_LIMIT_BYTES)


def _dot(a, b, dims=NN):
    return lax.dot_general(a.astype(_MXU), b.astype(_MXU), dims, preferred_element_type=F32)


def _dot_onehot(a, b, dims=NN, *, data=0):
    ops = [a, b]
    mask = ops[1 - data].astype(jnp.bfloat16)
    rest = ops[data]
    acc = None
    for _ in range(3):
        piece = rest.astype(jnp.bfloat16)
        ops[data], ops[1 - data] = piece, mask
        d = lax.dot_general(ops[0], ops[1], dims, preferred_element_type=F32)
        acc = d if acc is None else acc + d
        rest = rest - piece.astype(F32)
    return acc


def _sigmoid(x):
    return 0.5 * (1.0 + jnp.tanh(0.5 * x))


def _softplus(x):
    return jnp.maximum(x, 0.0) + jnp.log(1.0 + jnp.exp(-jnp.abs(x)))


def _silu(x):
    return x * _sigmoid(x)


def _silu_grad(x):
    s = _sigmoid(x)
    return s * (1.0 + x * (1.0 - s))


_GELU_C = math.sqrt(2.0 / math.pi)


def _gelu_and_grad(x):
    inner = _GELU_C * (x + 0.044715 * x * x * x)
    t = jnp.tanh(inner)
    g = 0.5 * x * (1.0 + t)
    dg = 0.5 * (1.0 + t) + 0.5 * x * (1.0 - t * t) * _GELU_C * (1.0 + 3.0 * 0.044715 * x * x)
    return g, dg


def _rms_fwd(x, w):
    rstd = lax.rsqrt(jnp.mean(x * x, axis=-1, keepdims=True) + EPS)
    return x * rstd * w


def _rms_bwd(x, w, dy):
    rstd = lax.rsqrt(jnp.mean(x * x, axis=-1, keepdims=True) + EPS)
    xhat = x * rstd
    dxhat = dy * w
    dx = rstd * (dxhat - xhat * jnp.mean(dxhat * xhat, axis=-1, keepdims=True))
    return dx, dy * xhat


def _mm(terms, m, n, *, tm, tn, mode, out_dtype, name, residual=None, n_outer=False, also_mxu=False, behind=()):
    gm, gn = m // tm, n // tn
    assert gm * tm == m and gn * tn == n
    if n_outer:
        grid = (gn, gm)
        mi = lambda g0, g1: g1
        ni = lambda g0, g1: g0
    else:
        grid = (gm, gn)
        mi = lambda g0, g1: g0
        ni = lambda g0, g1: g1
    in_specs, args = [], []
    for (a, ka, b, kb, k) in terms:
        if mode == "tn":
            in_specs.append(pl.BlockSpec((k, tm), lambda g0, g1, ka=ka: (ka, mi(g0, g1))))
        else:
            in_specs.append(pl.BlockSpec((tm, k), lambda g0, g1, ka=ka: (mi(g0, g1), ka)))
        if mode == "nt":
            in_specs.append(pl.BlockSpec((tn, k), lambda g0, g1, kb=kb: (ni(g0, g1), kb)))
        else:
            in_specs.append(pl.BlockSpec((k, tn), lambda g0, g1, kb=kb: (kb, ni(g0, g1))))
        args += [a, b]
    if residual is not None:
        in_specs.append(pl.BlockSpec((tm, tn), lambda g0, g1: (mi(g0, g1), ni(g0, g1))))
        args.append(residual)
    dims = {"nn": NN, "nt": NT, "tn": TN}[mode]
    n_terms = len(terms)
    has_res = residual is not None
    in_specs += [pl.BlockSpec(memory_space=pl.ANY)] * len(behind)
    args += list(behind)
    n_in = len(args)

    def body(*refs):
        acc = None
        for t in range(n_terms):
            d = lax.dot_general(refs[2 * t][...], refs[2 * t + 1][...], dims, preferred_element_type=F32)
            acc = d if acc is None else acc + d
        if has_res:
            acc = acc + refs[2 * n_terms][...]
        refs[n_in][...] = acc.astype(out_dtype)
        if also_mxu:
            refs[n_in + 1][...] = acc.astype(_MXU)

    tile = pl.BlockSpec((tm, tn), lambda g0, g1: (mi(g0, g1), ni(g0, g1)))
    shape = jax.ShapeDtypeStruct((m, n), out_dtype)
    return pl.pallas_call(
        body, name=name, grid=grid, in_specs=in_specs,
        out_specs=[tile, tile] if also_mxu else tile,
        out_shape=[shape, jax.ShapeDtypeStruct((m, n), _MXU)] if also_mxu else shape,
        compiler_params=_cparams("parallel", "parallel"),
    )(*args)


def _embed(x, meta):
    def body(x_ref, meta_ref, o_ref):
        i = pl.program_id(0)

        @pl.when(i == 0)
        def _():
            o_ref[0:PAD_ROWS, :] = jnp.zeros((PAD_ROWS, D_MODEL), F32)
            o_ref[PAD_ROWS:CHUNK, :] = meta_ref[...]

        @pl.when(i > 0)
        def _():
            o_ref[...] = x_ref[...]

    return pl.pallas_call(
        body, name="embed", grid=(N_CHUNKS,),
        in_specs=[pl.BlockSpec((CHUNK, D_MODEL), lambda i: (jnp.maximum(i - 1, 0), 0)),
                  pl.BlockSpec((N_META, D_MODEL), lambda i: (0, 0))],
        out_specs=pl.BlockSpec((CHUNK, D_MODEL), lambda i: (i, 0)),
        out_shape=jax.ShapeDtypeStruct((T_ROWS, D_MODEL), F32),
        compiler_params=_cparams("parallel"),
    )(x, meta)


def _rmsnorm(h, w, *, name, tm=544):
    def body(h_ref, w_ref, o_ref):
        o_ref[...] = _rms_fwd(h_ref[...], w_ref[...]).astype(_MXU)

    return pl.pallas_call(
        body, name=name, grid=(T_ROWS // tm,),
        in_specs=[pl.BlockSpec((tm, D_MODEL), lambda i: (i, 0)), pl.BlockSpec((1, D_MODEL), lambda i: (0, 0))],
        out_specs=pl.BlockSpec((tm, D_MODEL), lambda i: (i, 0)),
        out_shape=jax.ShapeDtypeStruct((T_ROWS, D_MODEL), _MXU),
        compiler_params=_cparams("parallel"),
    )(h, w)


def _norm_proj(h, w, sections, *, name, tm=544):
    widths = [s.shape[0] for s in sections]
    n = len(sections)

    def body(*refs):
        h_ref, w_ref = refs[:2]
        u_ref = refs[2 + n]
        u = _rms_fwd(h_ref[...], w_ref[...]).astype(_MXU)
        u_ref[...] = u
        for k in range(n):
            refs[3 + n + k][...] = lax.dot_general(u, refs[2 + k][...], NT, preferred_element_type=F32)

    row = lambda width: pl.BlockSpec((tm, width), lambda i: (i, 0))
    outs = pl.pallas_call(
        body, name=name, grid=(T_ROWS // tm,),
        in_specs=[row(D_MODEL), pl.BlockSpec((1, D_MODEL), lambda i: (0, 0))]
        + [pl.BlockSpec((wd, D_MODEL), lambda i: (0, 0)) for wd in widths],
        out_specs=[row(D_MODEL)] + [row(wd) for wd in widths],
        out_shape=[jax.ShapeDtypeStruct((T_ROWS, D_MODEL), _MXU)]
        + [jax.ShapeDtypeStruct((T_ROWS, wd), F32) for wd in widths],
        compiler_params=_cparams("parallel"),
    )(h, w, *sections)
    return outs[0], list(outs[1:])


def _loss_head(h2, target, fw):
    def body(h_ref, t_ref, w_ref, loss_ref, dh_ref, dhb_ref, dw_ref, acc_ref):
        i = pl.program_id(0)

        @pl.when(i == 0)
        def _():
            acc_ref[...] = jnp.zeros_like(acc_ref)
            dw_ref[...] = jnp.zeros_like(dw_ref)

        h = h_ref[...]
        w = w_ref[...]
        y = _rms_fwd(h, w)
        live = (i > 0).astype(F32)
        err = (y - t_ref[...]) * live
        acc_ref[...] += jnp.sum(err * err, axis=0, keepdims=True)
        dy = err * (1.0 / D_MODEL)
        dx, dwr = _rms_bwd(h, w, dy)
        dh_ref[...] = dx
        dhb_ref[...] = dx.astype(_MXU)
        dw_ref[...] += jnp.sum(dwr, axis=0, keepdims=True)

        @pl.when(i == N_CHUNKS - 1)
        def _():
            tot = jnp.sum(acc_ref[...], axis=1, keepdims=True) * (0.5 / D_MODEL)
            loss_ref[...] = jnp.broadcast_to(tot, (1, 128))

    return pl.pallas_call(
        body, name="loss_head", grid=(N_CHUNKS,),
        in_specs=[pl.BlockSpec((CHUNK, D_MODEL), lambda i: (i, 0)),
                  pl.BlockSpec((CHUNK, D_MODEL), lambda i: (jnp.maximum(i - 1, 0), 0)),
                  pl.BlockSpec((1, D_MODEL), lambda i: (0, 0))],
        out_specs=[pl.BlockSpec((1, 128), lambda i: (0, 0)),
                   pl.BlockSpec((CHUNK, D_MODEL), lambda i: (i, 0)),
                   pl.BlockSpec((CHUNK, D_MODEL), lambda i: (i, 0)),
                   pl.BlockSpec((1, D_MODEL), lambda i: (0, 0))],
        out_shape=[jax.ShapeDtypeStruct((1, 128), F32),
                   jax.ShapeDtypeStruct((T_ROWS, D_MODEL), F32),
                   jax.ShapeDtypeStruct((T_ROWS, D_MODEL), _MXU),
                   jax.ShapeDtypeStruct((1, D_MODEL), F32)],
        scratch_shapes=[pltpu.VMEM((1, D_MODEL), F32)],
        compiler_params=_cparams("arbitrary"),
    )(h2, target, fw)


def _mm_norm_bwd(terms, h, w, dres, *, name, tm=544):
    n_terms = len(terms)
    in_specs, args = [], []
    for (a, b, k) in terms:
        in_specs += [pl.BlockSpec((tm, k), lambda i: (i, 0)), pl.BlockSpec((k, D_MODEL), lambda i: (0, 0))]
        args += [a, b]
    in_specs += [pl.BlockSpec((tm, D_MODEL), lambda i: (i, 0)), pl.BlockSpec((1, D_MODEL), lambda i: (0, 0)),
                 pl.BlockSpec((tm, D_MODEL), lambda i: (i, 0))]
    args += [h, w, dres]

    def body(*refs):
        h_ref, w_ref, dres_ref, dh_ref, dhb_ref, dw_ref = refs[2 * n_terms:]

        @pl.when(pl.program_id(0) == 0)
        def _():
            dw_ref[...] = jnp.zeros_like(dw_ref)

        du = None
        for t in range(n_terms):
            d = lax.dot_general(refs[2 * t][...], refs[2 * t + 1][...], NN, preferred_element_type=F32)
            du = d if du is None else du + d
        dx, dwr = _rms_bwd(h_ref[...], w_ref[...], du)
        dh = dres_ref[...] + dx
        dh_ref[...] = dh
        dhb_ref[...] = dh.astype(_MXU)
        dw_ref[...] += jnp.sum(dwr, axis=0, keepdims=True)

    return pl.pallas_call(
        body, name=name, grid=(T_ROWS // tm,), in_specs=in_specs,
        out_specs=[pl.BlockSpec((tm, D_MODEL), lambda i: (i, 0)), pl.BlockSpec((tm, D_MODEL), lambda i: (i, 0)),
                   pl.BlockSpec((1, D_MODEL), lambda i: (0, 0))],
        out_shape=[jax.ShapeDtypeStruct((T_ROWS, D_MODEL), F32), jax.ShapeDtypeStruct((T_ROWS, D_MODEL), _MXU),
                   jax.ShapeDtypeStruct((1, D_MODEL), F32)],
        compiler_params=_cparams("arbitrary"),
    )(*args)


FFN_TM = T_ROWS
FFN_TN = 256


def _ffn_up(u2, wg_t, wu_t):
    def body(u_ref, wg_ref, wu_ref, gp_ref, up_ref, act_ref):
        u = u_ref[...]
        gp = lax.dot_general(u, wg_ref[...], NT, preferred_element_type=F32)
        up = lax.dot_general(u, wu_ref[...], NT, preferred_element_type=F32)
        gp_ref[...] = gp
        up_ref[...] = up
        act_ref[...] = (_silu(gp) * up).astype(_MXU)

    tile = pl.BlockSpec((FFN_TM, FFN_TN), lambda j, i: (i, j))
    return pl.pallas_call(
        body, name="ffn_up", grid=(D_FF // FFN_TN, T_ROWS // FFN_TM),
        in_specs=[pl.BlockSpec((FFN_TM, D_MODEL), lambda j, i: (i, 0)),
                  pl.BlockSpec((FFN_TN, D_MODEL), lambda j, i: (j, 0)),
                  pl.BlockSpec((FFN_TN, D_MODEL), lambda j, i: (j, 0))],
        out_specs=[tile, tile, tile],
        out_shape=[jax.ShapeDtypeStruct((T_ROWS, D_FF), F32), jax.ShapeDtypeStruct((T_ROWS, D_FF), F32),
                   jax.ShapeDtypeStruct((T_ROWS, D_FF), _MXU)],
        compiler_params=_cparams("parallel", "parallel"),
    )(u2, wg_t, wu_t)


def _ffn_bwd_act(dh2b, wd, gp, up):
    def body(dh_ref, wd_ref, gp_ref, up_ref, dgp_ref, dup_ref):
        dact = lax.dot_general(dh_ref[...], wd_ref[...], NT, preferred_element_type=F32)
        gp = gp_ref[...]
        dgp_ref[...] = (dact * up_ref[...] * _silu_grad(gp)).astype(_MXU)
        dup_ref[...] = (dact * _silu(gp)).astype(_MXU)

    tile = pl.BlockSpec((FFN_TM, FFN_TN), lambda j, i: (i, j))
    return pl.pallas_call(
        body, name="ffn_bwd_act", grid=(D_FF // FFN_TN, T_ROWS // FFN_TM),
        in_specs=[pl.BlockSpec((FFN_TM, D_MODEL), lambda j, i: (i, 0)),
                  pl.BlockSpec((FFN_TN, D_MODEL), lambda j, i: (j, 0)), tile, tile],
        out_specs=[tile, tile],
        out_shape=[jax.ShapeDtypeStruct((T_ROWS, D_FF), _MXU), jax.ShapeDtypeStruct((T_ROWS, D_FF), _MXU)],
        compiler_params=_cparams("parallel", "parallel"),
    )(dh2b, wd, gp, up)


CONV_TC = 512
CONV_K = 4


def _conv_pre(x_ref, wv, bv, c):
    tc = wv.shape[1]
    r0 = c * CHUNK
    cur = x_ref[r0:r0 + CHUNK, :]
    if c == 0:
        cat = jnp.concatenate([jnp.zeros((8, tc), F32), cur], axis=0)
        shifted = [cur] + [pltpu.roll(cat, s, 0)[8:8 + CHUNK] for s in range(1, CONV_K)]
    else:
        shifted = [cur] + [x_ref[r0 - s:r0 - s + CHUNK, :] for s in range(1, CONV_K)]
    pre = bv
    for s in range(CONV_K):
        pre = pre + shifted[s] * wv[CONV_K - 1 - s:CONV_K - s]
    return pre, shifted


def _row_mask(c):
    if c > 0:
        return None
    return (lax.broadcasted_iota(jnp.int32, (CHUNK, 1), 0) >= PAD_ROWS).astype(F32)


def _conv_fwd(x, w, b, *, silu, name):
    cols = x.shape[1]
    tc = min(CONV_TC, cols)

    def body(x_ref, w_ref, b_ref, o_ref):
        wv, bv = w_ref[...], b_ref[...]
        for c in range(N_CHUNKS):
            pre, _ = _conv_pre(x_ref, wv, bv, c)
            y = _silu(pre) if silu else pre
            mask = _row_mask(c)
            if mask is not None:
                y = y * mask
            o_ref[c * CHUNK:(c + 1) * CHUNK, :] = y

    return pl.pallas_call(
        body, name=name, grid=(cols // tc,),
        in_specs=[pl.BlockSpec((T_ROWS, tc), lambda j: (0, j)), pl.BlockSpec((CONV_K, tc), lambda j: (0, j)),
                  pl.BlockSpec((1, tc), lambda j: (0, j))],
        out_specs=pl.BlockSpec((T_ROWS, tc), lambda j: (0, j)),
        out_shape=jax.ShapeDtypeStruct((T_ROWS, cols), F32),
        compiler_params=_cparams("parallel"),
    )(x, w, b)


def _conv_bwd(dy, x, w, b, *, silu, name):
    cols = x.shape[1]
    tc = min(CONV_TC, cols)

    def body(dy_ref, x_ref, w_ref, b_ref, dx_ref, dw_ref, db_ref):
        wv, bv = w_ref[...], b_ref[...]
        next8 = jnp.zeros((8, tc), F32)
        dws = [jnp.zeros((1, tc), F32) for _ in range(CONV_K)]
        db = jnp.zeros((1, tc), F32)
        for c in reversed(range(N_CHUNKS)):
            r0 = c * CHUNK
            pre, shifted = _conv_pre(x_ref, wv, bv, c)
            dpre = dy_ref[r0:r0 + CHUNK, :]
            if silu:
                dpre = dpre * _silu_grad(pre)
            mask = _row_mask(c)
            if mask is not None:
                dpre = dpre * mask
            cat = jnp.concatenate([dpre, next8], axis=0)
            dx = dpre * wv[CONV_K - 1:CONV_K]
            for s in range(1, CONV_K):
                dx = dx + pltpu.roll(cat, CHUNK + 8 - s, 0)[0:CHUNK] * wv[CONV_K - 1 - s:CONV_K - s]
            dx_ref[r0:r0 + CHUNK, :] = dx.astype(_MXU)
            for s in range(CONV_K):
                k = CONV_K - 1 - s
                dws[k] = dws[k] + jnp.sum(dpre * shifted[s], axis=0, keepdims=True)
            db = db + jnp.sum(dpre, axis=0, keepdims=True)
            next8 = dpre[0:8]
        dw_ref[...] = jnp.concatenate(dws, axis=0)
        db_ref[...] = db

    return pl.pallas_call(
        body, name=name, grid=(cols // tc,),
        in_specs=[pl.BlockSpec((T_ROWS, tc), lambda j: (0, j)), pl.BlockSpec((T_ROWS, tc), lambda j: (0, j)),
                  pl.BlockSpec((CONV_K, tc), lambda j: (0, j)), pl.BlockSpec((1, tc), lambda j: (0, j))],
        out_specs=[pl.BlockSpec((T_ROWS, tc), lambda j: (0, j)), pl.BlockSpec((CONV_K, tc), lambda j: (0, j)),
                   pl.BlockSpec((1, tc), lambda j: (0, j))],
        out_shape=[jax.ShapeDtypeStruct((T_ROWS, cols), _MXU), jax.ShapeDtypeStruct((CONV_K, cols), F32),
                   jax.ShapeDtypeStruct((1, cols), F32)],
        compiler_params=_cparams("parallel"),
    )(dy, x, w, b)


def _ssd_chunk_common(dt_raw, prm, c):
    a_row = -jnp.exp(prm[1:2])
    dt = _softplus(dt_raw + prm[0:1])
    rows = lax.broadcasted_iota(jnp.int32, (CHUNK, 1), 0)
    real = jnp.logical_or(c > 0, rows >= PAD_ROWS)
    dt = jnp.where(real, dt, 0.0)
    li = lax.broadcasted_iota(jnp.int32, (CHUNK, CHUNK), 0)
    si = lax.broadcasted_iota(jnp.int32, (CHUNK, CHUNK), 1)
    causal = li >= si
    tri = causal.astype(F32)
    cs = _dot_onehot(tri, dt * a_row, data=1)
    return dt, a_row, cs, cs.T, causal, tri, real


def _gated_norm_fwd(y, z, w):
    g = y * _silu(z)
    half = SSD_WIDTH // SSD_GROUPS
    outs = [_rms_fwd(g[:, k * half:(k + 1) * half], w[:, k * half:(k + 1) * half]) for k in range(SSD_GROUPS)]
    return jnp.concatenate(outs, axis=1)


GROUP_W = SSD_WIDTH // SSD_GROUPS
PAIR_W = 2 * SSD_HEAD_DIM
STATE_SHAPE = (SSD_GROUPS, SSD_STATE, GROUP_W)


def _head_expander():
    r = lax.broadcasted_iota(jnp.int32, (128, SSD_WIDTH), 0)
    c = lax.broadcasted_iota(jnp.int32, (128, SSD_WIDTH), 1)
    return (c // SSD_HEAD_DIM == r).astype(F32)


def _ssd_expand(dt, cs, prm, ex):
    cs_x = _dot_onehot(cs, ex)
    cs_last_x = cs_x[CHUNK - 1:CHUNK, :]
    return (_dot_onehot(dt, ex), _dot_onehot(prm, ex)[2:3], jnp.exp(cs_x), jnp.exp(cs_last_x),
            jnp.exp(cs_last_x - cs_x))


def _ssd_fwd(xs, bc, dt_raw, z, prm, norm_w, ex):
    def body(xs_ref, bc_ref, dt_ref, z_ref, prm_ref, nw_ref, ex_ref, y_ref, yn_ref, prev_ref, state):
        c = pl.program_id(0)

        @pl.when(c == 0)
        def _():
            state[...] = jnp.zeros_like(state)

        prm = prm_ref[...]
        dt, a_row, cs, cs_t, causal, _, _ = _ssd_chunk_common(dt_ref[...], prm, c)
        dt_x, d_x, e_cs_x, e_last_x, dec_x = _ssd_expand(dt, cs, prm, ex_ref[...])
        xs_all = xs_ref[...]
        bc_all = bc_ref[...]
        xdt = xs_all * dt_x
        xdec = xdt * dec_x
        lane_lo = lax.broadcasted_iota(jnp.int32, (1, PAIR_W), 1) < SSD_HEAD_DIM
        for g in range(SSD_GROUPS):
            gs = slice(g * GROUP_W, (g + 1) * GROUP_W)
            b_g = bc_all[:, g * SSD_STATE:(g + 1) * SSD_STATE]
            c_g = bc_all[:, (SSD_GROUPS + g) * SSD_STATE:(SSD_GROUPS + g + 1) * SSD_STATE]
            st = state[g]
            prev_ref[0, g] = st
            y_off = _dot(c_g, st) * e_cs_x[:, gs]
            state[g] = st * e_last_x[:, gs] + _dot(b_g.T, xdec[:, gs])
            cb = _dot(c_g, b_g, NT)
            for k in range(SSD_HPG // 2):
                h0 = g * SSD_HPG + 2 * k
                ps = slice(h0 * SSD_HEAD_DIM, h0 * SSD_HEAD_DIM + PAIR_W)
                xdt_pair = xdt[:, ps]
                yd = []
                for h in (h0, h0 + 1):
                    lmat = jnp.where(causal, jnp.exp(cs[:, h:h + 1] - cs_t[h:h + 1, :]), 0.0)
                    yd.append(_dot(cb * lmat, xdt_pair))
                y_ref[:, ps] = (jnp.where(lane_lo, yd[0], yd[1]) + y_off[:, k * PAIR_W:(k + 1) * PAIR_W]
                                + xs_all[:, ps] * d_x[:, ps])
        yn_ref[...] = _gated_norm_fwd(y_ref[...], z_ref[...], nw_ref[...]).astype(_MXU)

    row = lambda w: pl.BlockSpec((CHUNK, w), lambda c: (c, 0))
    return pl.pallas_call(
        body, name="ssd_fwd", grid=(N_CHUNKS,),
        in_specs=[row(SSD_WIDTH), row(512), row(128), row(SSD_WIDTH),
                  pl.BlockSpec((8, 128), lambda c: (0, 0)), pl.BlockSpec((1, SSD_WIDTH), lambda c: (0, 0)),
                  pl.BlockSpec((128, SSD_WIDTH), lambda c: (0, 0))],
        out_specs=[row(SSD_WIDTH), row(SSD_WIDTH),
                   pl.BlockSpec((1,) + STATE_SHAPE, lambda c: (c, 0, 0, 0))],
        out_shape=[jax.ShapeDtypeStruct((T_ROWS, SSD_WIDTH), F32), jax.ShapeDtypeStruct((T_ROWS, SSD_WIDTH), _MXU),
                   jax.ShapeDtypeStruct((N_CHUNKS,) + STATE_SHAPE, F32)],
        scratch_shapes=[pltpu.VMEM(STATE_SHAPE, F32)],
        compiler_params=_cparams("arbitrary"),
    )(xs, bc, dt_raw, z, prm, norm_w, ex)


def _ssd_bwd(dyn, dyn_block, z, y_pre, xs, bc, dt_raw, prev, prm, norm_w, ex):
    def body(dyn_ref, z_ref, y_ref, xs_ref, bc_ref, dt_ref, prev_ref, prm_ref, nw_ref, ex_ref,
             dz_ref, dxs_ref, dbc_ref, ddt_ref, dprm_ref, dnw_ref, dstate):
        step = pl.program_id(0)
        c = N_CHUNKS - 1 - step

        @pl.when(step == 0)
        def _():
            dstate[...] = jnp.zeros_like(dstate)
            dprm_ref[...] = jnp.zeros_like(dprm_ref)
            dnw_ref[...] = jnp.zeros_like(dnw_ref)

        prm = prm_ref[...]
        dt, a_row, cs, cs_t, causal, tri, real = _ssd_chunk_common(dt_ref[...], prm, c)
        realf = real.astype(F32)
        z = z_ref[...]
        y_all = y_ref[...]
        nw = nw_ref[...]
        dyn_all = dyn_ref[...]
        sz = _silu(z)
        gated = y_all * sz
        half = SSD_WIDTH // SSD_GROUPS
        dgs, dnws = [], []
        for k in range(SSD_GROUPS):
            sl = slice(k * half, (k + 1) * half)
            dgk, dwk = _rms_bwd(gated[:, sl], nw[:, sl], dyn_all[:, sl])
            dgs.append(dgk)
            dnws.append(jnp.sum(dwk, axis=0, keepdims=True))
        dgated = jnp.concatenate(dgs, axis=1)
        dnw_ref[...] += jnp.concatenate(dnws, axis=1)
        dz_ref[...] = (dgated * y_all * _silu_grad(z)).astype(_MXU)
        dy_all = dgated * sz

        ex = ex_ref[...]
        dt_x, d_x, e_cs_x, e_last_x, dec_x = _ssd_expand(dt, cs, prm, ex)
        xs_all = xs_ref[...]
        bc_all = bc_ref[...]
        xdt = xs_all * dt_x
        xdt_mxu = xdt.astype(_MXU).astype(F32)
        xdec = xdt * dec_x
        dcp = dy_all * e_cs_x
        lane_lo = lax.broadcasted_iota(jnp.int32, (1, PAIR_W), 1) < SSD_HEAD_DIM
        upper = (lax.broadcasted_iota(jnp.int32, (CHUNK, CHUNK), 0)
                 <= lax.broadcasted_iota(jnp.int32, (CHUNK, CHUNK), 1))
        last_row = (lax.broadcasted_iota(jnp.int32, (CHUNK, 1), 0) == CHUNK - 1).astype(F32)
        dbs, dcs_, dxdt_parts, last_parts = [], [], [], []
        for g in range(SSD_GROUPS):
            gs = slice(g * GROUP_W, (g + 1) * GROUP_W)
            b_g = bc_all[:, g * SSD_STATE:(g + 1) * SSD_STATE]
            c_g = bc_all[:, (SSD_GROUPS + g) * SSD_STATE:(SSD_GROUPS + g + 1) * SSD_STATE]
            prev_t = prev_ref[0, g]
            dst = dstate[g]
            dc_g = _dot(dcp[:, gs], prev_t, NT)
            db_g = _dot(xdec[:, gs], dst, NT)
            dxdt_state = _dot(b_g, dst) * dec_x[:, gs]
            dstate[g] = dst * e_last_x[:, gs] + _dot(c_g.T, dcp[:, gs])
            last_parts.append(jnp.sum(xdt_mxu[:, gs] * dxdt_state, axis=0, keepdims=True)
                              + jnp.sum(dst * prev_t, axis=0, keepdims=True) * e_last_x[:, gs])
            cb_t = _dot(b_g, c_g, NT)
            dcb_t = jnp.zeros((CHUNK, CHUNK), F32)
            for k in range(SSD_HPG // 2):
                h0 = g * SSD_HPG + 2 * k
                ps = slice(h0 * SSD_HEAD_DIM, h0 * SSD_HEAD_DIM + PAIR_W)
                dy_pair = dy_all[:, ps]
                xdt_pair = xdt[:, ps]
                dd = []
                for h in (h0, h0 + 1):
                    lmat_t = jnp.where(upper, jnp.exp(cs_t[h:h + 1, :] - cs[:, h:h + 1]), 0.0)
                    dd.append(_dot(cb_t * lmat_t, dy_pair))
                    mine = lane_lo if h == h0 else jnp.logical_not(lane_lo)
                    dcb_t = dcb_t + _dot(jnp.where(mine, xdt_pair, 0.0), dy_pair, NT) * lmat_t
                dxdt_parts.append(jnp.where(lane_lo, dd[0], dd[1]) + dxdt_state[:, k * PAIR_W:(k + 1) * PAIR_W])
            dc_g = dc_g + _dot(dcb_t, b_g, TN)
            db_g = db_g + _dot(dcb_t, c_g)
            dbs.append(db_g * realf)
            dcs_.append(dc_g * realf)
        dbc_ref[...] = jnp.concatenate(dbs + dcs_, axis=1)
        dxdt = jnp.concatenate(dxdt_parts, axis=1)
        dxs_ref[...] = (dxdt * dt_x + dy_all * d_x) * realf
        ddt_all = _dot_onehot(dxdt * xs_all, ex, NT)
        rows = jnp.concatenate([jnp.concatenate(last_parts, axis=1), jnp.sum(dy_all * xs_all, axis=0, keepdims=True),
                                jnp.zeros((6, SSD_WIDTH), F32)], axis=0)
        rows = _dot_onehot(rows, ex, NT)
        dd_row = rows[1:2]
        dy_mxu = dy_all.astype(_MXU).astype(F32)
        dcs_all = (_dot_onehot(dy_mxu * (y_all - xs_all * d_x), ex, NT) - _dot_onehot(xdt_mxu * dxdt, ex, NT)
                   + last_row * rows[0:1])
        dda = _dot_onehot(tri, dcs_all, TN, data=1)
        ddt = (ddt_all + dda * a_row) * realf
        ddt_raw = ddt * _sigmoid(dt_ref[...] + prm[0:1])
        ddt_ref[...] = ddt_raw.astype(_MXU)
        da_log = jnp.sum(dda * dt, axis=0, keepdims=True) * a_row
        dprm_ref[0:1, :] += jnp.sum(ddt_raw, axis=0, keepdims=True)
        dprm_ref[1:2, :] += da_log
        dprm_ref[2:3, :] += dd_row

    rev = lambda w, blk=0: pl.BlockSpec((CHUNK, w), lambda s, blk=blk: (N_CHUNKS - 1 - s, blk))
    return pl.pallas_call(
        body, name="ssd_bwd", grid=(N_CHUNKS,),
        in_specs=[rev(SSD_WIDTH, dyn_block), rev(SSD_WIDTH), rev(SSD_WIDTH), rev(SSD_WIDTH), rev(512), rev(128),
                  pl.BlockSpec((1,) + STATE_SHAPE, lambda s: (N_CHUNKS - 1 - s, 0, 0, 0)),
                  pl.BlockSpec((8, 128), lambda s: (0, 0)), pl.BlockSpec((1, SSD_WIDTH), lambda s: (0, 0)),
                  pl.BlockSpec((128, SSD_WIDTH), lambda s: (0, 0))],
        out_specs=[rev(SSD_WIDTH), rev(SSD_WIDTH), rev(512), rev(128),
                   pl.BlockSpec((8, 128), lambda s: (0, 0)), pl.BlockSpec((1, SSD_WIDTH), lambda s: (0, 0))],
        out_shape=[jax.ShapeDtypeStruct((T_ROWS, SSD_WIDTH), _MXU), jax.ShapeDtypeStruct((T_ROWS, SSD_WIDTH), F32),
                   jax.ShapeDtypeStruct((T_ROWS, 512), F32), jax.ShapeDtypeStruct((T_ROWS, 128), _MXU),
                   jax.ShapeDtypeStruct((8, 128), F32), jax.ShapeDtypeStruct((1, SSD_WIDTH), F32)],
        scratch_shapes=[pltpu.VMEM(STATE_SHAPE, F32)],
        compiler_params=_cparams("arbitrary"),
    )(dyn, z, y_pre, xs, bc, dt_raw, prev, prm, norm_w, ex)


LRU_PAIRS = 8


def _lru_gates(xr, wa_ref, wx_ref, prm):
    pre_r, pre_i = [], []
    for k in range(LRU_PAIRS):
        xk = xr[:, k * 128:(k + 1) * 128]
        pre_r.append(_dot(xk, wa_ref[k]))
        pre_i.append(_dot(xk, wx_ref[k]))
    r = _sigmoid(jnp.concatenate(pre_r, axis=1) + prm[0:1])
    i = _sigmoid(jnp.concatenate(pre_i, axis=1) + prm[1:2])
    sp = _softplus(-prm[2:3])
    log_a = (-LRU_C) * r * sp
    a = jnp.exp(log_a)
    s = jnp.sqrt(-jnp.tanh(log_a) * (a * a + 1.0))
    return r, i, a, s, sp


def _lru_fwd(xr, gate, wa, wx, prm):
    def body(xr_ref, g_ref, wa_ref, wx_ref, prm_ref, hs_ref, yn_ref, carry, a_s, u_s):
        @pl.when(pl.program_id(0) == 0)
        def _():
            carry[...] = jnp.zeros_like(carry)

        prm = prm_ref[...]
        xr_t = xr_ref[...]
        _, i, a, s, _ = _lru_gates(xr_t, wa_ref, wx_ref, prm)
        a_s[...] = a
        u_s[...] = s * (i * xr_t)
        rid = lax.broadcasted_iota(jnp.int32, (8, LRU_WIDTH), 0)

        def group(k, h):
            off = pl.multiple_of(k * 8, 8)
            a8 = a_s[pl.ds(off, 8), :]
            u8 = u_s[pl.ds(off, 8), :]
            out = jnp.zeros((8, LRU_WIDTH), F32)
            for r_ in range(8):
                h = a8[r_:r_ + 1] * h + u8[r_:r_ + 1]
                out = jnp.where(rid == r_, h, out)
            hs_ref[pl.ds(off, 8), :] = out
            return h

        carry[0:1, :] = lax.fori_loop(0, CHUNK // 8, group, carry[0:1, :])
        gel, _ = _gelu_and_grad(g_ref[...])
        yn_ref[...] = _rms_fwd(gel * hs_ref[...], prm[3:4]).astype(_MXU)

    row = pl.BlockSpec((CHUNK, LRU_WIDTH), lambda t: (t, 0))
    wspec = pl.BlockSpec((LRU_PAIRS, 128, 128), lambda t: (0, 0, 0))
    return pl.pallas_call(
        body, name="lru_fwd", grid=(N_CHUNKS,),
        in_specs=[row, row, wspec, wspec, pl.BlockSpec((8, LRU_WIDTH), lambda t: (0, 0))],
        out_specs=[row, row],
        out_shape=[jax.ShapeDtypeStruct((T_ROWS, LRU_WIDTH), F32), jax.ShapeDtypeStruct((T_ROWS, LRU_WIDTH), _MXU)],
        scratch_shapes=[pltpu.VMEM((8, LRU_WIDTH), F32), pltpu.VMEM((CHUNK, LRU_WIDTH), F32),
                        pltpu.VMEM((CHUNK, LRU_WIDTH), F32)],
        compiler_params=_cparams("arbitrary"),
    )(xr, gate, wa, wx, prm)


def _lru_bwd(dyn, dyn_block, gate, xr, hs, wa, wx, wa_t, wx_t, prm):
    def body(dyn_ref, g_ref, xr_ref, hs_ref, hsp_ref, wa_ref, wx_ref, wat_ref, wxt_ref, prm_ref,
             dg_ref, dxr_ref, dwa_ref, dwx_ref, dprm_ref, carry, a_s, d_s):
        step = pl.program_id(0)
        tile = N_CHUNKS - 1 - step

        @pl.when(step == 0)
        def _():
            carry[...] = jnp.zeros_like(carry)
            dwa_ref[...] = jnp.zeros_like(dwa_ref)
            dwx_ref[...] = jnp.zeros_like(dwx_ref)
            dprm_ref[...] = jnp.zeros_like(dprm_ref)

        prm = prm_ref[...]
        xr_t = xr_ref[...]
        r, i, a, s, sp = _lru_gates(xr_t, wa_ref, wx_ref, prm)
        hs_t = hs_ref[...]
        gel, dgel = _gelu_and_grad(g_ref[...])
        dy, dnw = _rms_bwd(gel * hs_t, prm[3:4], dyn_ref[...])
        dg_ref[...] = (dy * hs_t * dgel).astype(_MXU)
        a_s[...] = a
        d_s[...] = dy * gel
        rid = lax.broadcasted_iota(jnp.int32, (8, LRU_WIDTH), 0)

        def group(k, cr):
            off = pl.multiple_of((CHUNK // 8 - 1 - k) * 8, 8)
            a8 = a_s[pl.ds(off, 8), :]
            d8 = d_s[pl.ds(off, 8), :]
            out = jnp.zeros((8, LRU_WIDTH), F32)
            for r_ in reversed(range(8)):
                dht = d8[r_:r_ + 1] + cr
                out = jnp.where(rid == r_, dht, out)
                cr = a8[r_:r_ + 1] * dht
            d_s[pl.ds(off, 8), :] = out
            return cr

        carry[0:1, :] = lax.fori_loop(0, CHUNK // 8, group, carry[0:1, :])
        dht = d_s[...]
        before = hsp_ref[CHUNK - 8:CHUNK, :][7:8] * (tile > 0).astype(F32)
        first = lax.broadcasted_iota(jnp.int32, (CHUNK, 1), 0) == 0
        hprev = jnp.where(first, before, pltpu.roll(hs_t, 1, 0))
        da = dht * hprev
        ixr = i * xr_t
        ds = dht * ixr
        dlog_a = da * a - ds * (a * a) * lax.rsqrt(s * s)
        dr = dlog_a * ((-LRU_C) * sp)
        dsp = jnp.sum(dlog_a * ((-LRU_C) * r), axis=0, keepdims=True)
        dlam = dsp * (-_sigmoid(-prm[2:3]))
        di = dht * s * xr_t
        dpre_r = dr * r * (1.0 - r)
        dpre_i = di * i * (1.0 - i)
        dxr = dht * s * i
        parts = []
        for k in range(LRU_PAIRS):
            sl = slice(k * 128, (k + 1) * 128)
            parts.append(_dot(dpre_r[:, sl], wat_ref[k]) + _dot(dpre_i[:, sl], wxt_ref[k]))
            dwa_ref[k] += _dot(xr_t[:, sl], dpre_r[:, sl], TN)
            dwx_ref[k] += _dot(xr_t[:, sl], dpre_i[:, sl], TN)
        dxr_ref[...] = dxr + jnp.concatenate(parts, axis=1)
        dprm_ref[0:1, :] += jnp.sum(dpre_r, axis=0, keepdims=True)
        dprm_ref[1:2, :] += jnp.sum(dpre_i, axis=0, keepdims=True)
        dprm_ref[2:3, :] += dlam
        dprm_ref[3:4, :] += jnp.sum(dnw, axis=0, keepdims=True)

    rev = lambda blk=0: pl.BlockSpec((CHUNK, LRU_WIDTH), lambda s, blk=blk: (N_CHUNKS - 1 - s, blk))
    wspec = pl.BlockSpec((LRU_PAIRS, 128, 128), lambda s: (0, 0, 0))
    return pl.pallas_call(
        body, name="lru_bwd", grid=(N_CHUNKS,),
        in_specs=[rev(dyn_block), rev(), rev(), rev(),
                  pl.BlockSpec((CHUNK, LRU_WIDTH), lambda s: (jnp.maximum(N_CHUNKS - 2 - s, 0), 0)),
                  wspec, wspec, wspec, wspec, pl.BlockSpec((8, LRU_WIDTH), lambda s: (0, 0))],
        out_specs=[rev(), rev(), wspec, wspec, pl.BlockSpec((8, LRU_WIDTH), lambda s: (0, 0))],
        out_shape=[jax.ShapeDtypeStruct((T_ROWS, LRU_WIDTH), _MXU), jax.ShapeDtypeStruct((T_ROWS, LRU_WIDTH), F32),
                   jax.ShapeDtypeStruct((LRU_PAIRS, 128, 128), F32), jax.ShapeDtypeStruct((LRU_PAIRS, 128, 128), F32),
                   jax.ShapeDtypeStruct((8, LRU_WIDTH), F32)],
        scratch_shapes=[pltpu.VMEM((8, LRU_WIDTH), F32), pltpu.VMEM((CHUNK, LRU_WIDTH), F32),
                        pltpu.VMEM((CHUNK, LRU_WIDTH), F32)],
        compiler_params=_cparams("arbitrary"),
    )(dyn, gate, xr, hs, hs, wa, wx, wa_t, wx_t, prm)


SEC_NAMES = ("z", "xs", "bc", "dt", "g", "x")
SEC_WIDTH = {"z": 1024, "xs": 1024, "bc": 512, "dt": 128, "g": 1024, "x": 1024}


def _pair_blocks(w):
    w = w.reshape(LRU_PAIRS, 2, 64, 64)
    zero = jnp.zeros((LRU_PAIRS, 64, 64), w.dtype)
    top = jnp.concatenate([w[:, 0], zero], axis=2)
    bot = jnp.concatenate([zero, w[:, 1]], axis=2)
    return jnp.concatenate([top, bot], axis=1)


def _unpair_blocks(wp):
    return jnp.stack([wp[:, :64, :64], wp[:, 64:, 64:]], axis=1).reshape(16, 64, 64)


def _pad_lanes(v, width=128):
    return jnp.pad(v, ((0, 0), (0, width - v.shape[1])))


class _Resident:
    def __init__(self, w_out, w_gate, w_up, w_down):
        self._w_out, self._ffn = w_out, (w_gate, w_up, w_down)

    def mid_forward(self, after):
        return jnp.zeros((1, 1), F32)

    def w_out(self, after):
        return self._w_out

    def ffn(self, after):
        return self._ffn

    def grads_ready(self, names, g, g_mxu):
        return jnp.zeros((1, 1), F32)

    def small_ready(self, g, loss):
        return jnp.zeros((1, 1), F32)

    def small_middle(self, after):
        return jnp.zeros((1, 1), F32)


def _local_step(x, target, meta, p, late):
    g, g_mxu = {}, {}
    ex = _head_expander()
    h0 = _embed(x, meta)
    u1, projs = _norm_proj(h0, p["norm1_w"], [p["w_in_" + s] for s in SEC_NAMES], name="norm_in_proj")
    proj = dict(zip(SEC_NAMES, projs))
    ssd_prm = jnp.concatenate([_pad_lanes(p["ssd_dt_bias"]), _pad_lanes(p["ssd_a_log"]), _pad_lanes(p["ssd_d"]),
                               jnp.zeros((5, 128), F32)], axis=0)
    xs_act = _conv_fwd(proj["xs"], p["ssd_conv_w"][:, :SSD_WIDTH], p["ssd_conv_b"][:, :SSD_WIDTH], silu=True,
                       name="ssd_conv_xs")
    bc_act = _conv_fwd(proj["bc"], p["ssd_conv_w"][:, SSD_WIDTH:], p["ssd_conv_b"][:, SSD_WIDTH:], silu=True,
                       name="ssd_conv_bc")
    y_pre, y_ssd, prev = _ssd_fwd(xs_act, bc_act, proj["dt"], proj["z"], ssd_prm, p["ssd_norm_w"], ex)
    xr = _conv_fwd(proj["x"], p["lru_conv_w"], p["lru_conv_b"], silu=False, name="lru_conv")
    wa_p, wx_p = _pair_blocks(p["lru_wa"]), _pair_blocks(p["lru_wx"])
    lru_prm = jnp.concatenate([p["lru_ba"], p["lru_bx"], p["lru_lambda"], p["lru_norm_w"],
                               jnp.zeros((4, LRU_WIDTH), F32)], axis=0)
    hs, y_lru = _lru_fwd(xr, proj["g"], wa_p.astype(_MXU), wx_p.astype(_MXU), lru_prm + late.mid_forward(xr))
    ycat = jnp.concatenate([y_ssd, y_lru], axis=1)
    w_out = late.w_out(ycat)
    h1 = _mm([(ycat, 0, w_out, 0, 2 * D_MODEL)], T_ROWS, D_MODEL, tm=T_ROWS, tn=256, mode="nn", out_dtype=F32,
             name="out_proj", residual=h0)
    u2 = _rmsnorm(h1, p["norm2_w"], name="norm2")
    w_gate, w_up, w_down = late.ffn(u2)
    gp, up, act = _ffn_up(u2, w_gate, w_up)
    h2 = _mm([(act, 0, w_down, 0, D_FF)], T_ROWS, D_MODEL, tm=T_ROWS, tn=256, mode="nn", out_dtype=F32,
             name="ffn_down", residual=h1)
    loss, dh2, dh2b, g["final_norm_w"] = _loss_head(h2, target, p["final_norm_w"])
    dgp, dup = _ffn_bwd_act(dh2b, w_down, gp, up)
    g["w_down"], g_mxu["w_down"] = _mm([(act, 0, dh2b, 0, T_ROWS)], D_FF, D_MODEL, tm=1408, tn=512, mode="tn",
                                       out_dtype=F32, name="dw_down", also_mxu=True)
    dh1, dh1b, g["norm2_w"] = _mm_norm_bwd([(dgp, w_gate, D_FF), (dup, w_up, D_FF)], h1, p["norm2_w"], dh2,
                                           name="ffn_bwd_in")
    g["w_gate"], g_mxu["w_gate"] = _mm([(dgp, 0, u2, 0, T_ROWS)], D_FF, D_MODEL, tm=1408, tn=512, mode="tn",
                                       out_dtype=F32, name="dw_gate", also_mxu=True)
    g["w_up"], g_mxu["w_up"] = _mm([(dup, 0, u2, 0, T_ROWS)], D_FF, D_MODEL, tm=1408, tn=512, mode="tn",
                                   out_dtype=F32, name="dw_up", also_mxu=True)
    sent = late.grads_ready(("w_down", "w_gate", "w_up"), g, g_mxu)
    g["w_out"], g_mxu["w_out"] = _mm([(ycat, 0, dh1b, 0, T_ROWS)], 2 * D_MODEL, D_MODEL, tm=1024, tn=512, mode="tn",
                                     out_dtype=F32, name="dw_out", also_mxu=True, behind=(sent,))
    sent = late.grads_ready(("w_out",), g, g_mxu)
    dycat = _mm([(dh1b, 0, w_out, 0, D_MODEL)], T_ROWS, 2 * D_MODEL, tm=T_ROWS, tn=256, mode="nt", out_dtype=F32,
                name="out_proj_bwd", behind=(sent,))
    dgate, dxr, dwa_p, dwx_p, dlru_prm = _lru_bwd(dycat, 1, proj["g"], xr, hs, wa_p.astype(_MXU), wx_p.astype(_MXU),
                                                  jnp.swapaxes(wa_p, 1, 2).astype(_MXU),
                                                  jnp.swapaxes(wx_p, 1, 2).astype(_MXU), lru_prm)
    g["lru_wa"], g["lru_wx"] = _unpair_blocks(dwa_p), _unpair_blocks(dwx_p)
    g["lru_ba"], g["lru_bx"], g["lru_lambda"], g["lru_norm_w"] = (dlru_prm[k:k + 1] for k in range(4))
    dx_lru, g["lru_conv_w"], g["lru_conv_b"] = _conv_bwd(dxr, proj["x"], p["lru_conv_w"], p["lru_conv_b"], silu=False,
                                                         name="lru_conv_bwd")
    dz, dxs_act, dbc_act, ddt, dssd_prm, g["ssd_norm_w"] = _ssd_bwd(dycat, 0, proj["z"], y_pre, xs_act, bc_act,
                                                                    proj["dt"], prev, ssd_prm, p["ssd_norm_w"], ex)
    g["ssd_dt_bias"], g["ssd_a_log"], g["ssd_d"] = (dssd_prm[k:k + 1, :SSD_HEADS] for k in range(3))
    dxs, dcw_xs, dcb_xs = _conv_bwd(dxs_act, proj["xs"], p["ssd_conv_w"][:, :SSD_WIDTH],
                                    p["ssd_conv_b"][:, :SSD_WIDTH], silu=True, name="ssd_conv_xs_bwd")
    dbc, dcw_bc, dcb_bc = _conv_bwd(dbc_act, proj["bc"], p["ssd_conv_w"][:, SSD_WIDTH:],
                                    p["ssd_conv_b"][:, SSD_WIDTH:], silu=True, name="ssd_conv_bc_bwd")
    g["ssd_conv_w"] = jnp.concatenate([dcw_xs, dcw_bc], axis=1)
    g["ssd_conv_b"] = jnp.concatenate([dcb_xs, dcb_bc], axis=1)
    dproj = {"z": dz, "xs": dxs, "bc": dbc, "dt": ddt, "g": dgate, "x": dx_lru}
    dh0, _, g["norm1_w"] = _mm_norm_bwd([(dproj[s], p["w_in_" + s], SEC_WIDTH[s]) for s in SEC_NAMES], h0,
                                        p["norm1_w"], dh1, name="in_proj_bwd")
    g["meta_tokens"] = dh0[PAD_ROWS:X_ROW0]
    sent = late.small_ready(g, loss)
    for s in SEC_NAMES:
        wdt = SEC_WIDTH[s]
        g["w_in_" + s], g_mxu["w_in_" + s] = _mm([(dproj[s], 0, u1, 0, T_ROWS)], wdt, D_MODEL, tm=min(wdt, 1024),
                                                 tn=512, mode="tn", out_dtype=F32, name="dw_in_" + s, also_mxu=True,
                                                 behind=(sent,))
        if s == "bc":
            sent = late.small_middle(g["w_in_bc"])
    return loss, dh0[X_ROW0:], g, g_mxu


MESH = pl.DeviceIdType.MESH
ANY = pl.BlockSpec(memory_space=pl.ANY)


def _my_place():
    return lax.axis_index("x"), lax.axis_index("y"), lax.axis_index("c")


def _other_chips(x, y):
    return [(1 - x, y), (x, 1 - y), (1 - x, 1 - y)]


def _gather_first(big, small):
    half = big.shape[1] // 2

    def body(big_ref, small_ref, big4, small4, send_sems, recv_sems, local_sems):
        x, y, c = _my_place()
        me = 2 * x + y
        sibling = (x, y, 1 - c)
        peers = _other_chips(x, y)
        mine = pl.ds(pl.multiple_of(c * half, 128), half)
        theirs = pl.ds(pl.multiple_of((1 - c) * half, 128), half)

        def copy(k, src, dst, dev):
            return pltpu.make_async_remote_copy(src_ref=src, dst_ref=dst, send_sem=send_sems.at[k],
                                                recv_sem=recv_sems.at[k], device_id=dev, device_id_type=MESH)

        local = [pltpu.make_async_copy(big_ref, big4.at[me], local_sems.at[0]),
                 pltpu.make_async_copy(small_ref, small4.at[me], local_sems.at[1])]
        for cp in local:
            cp.start()
        first = []
        for j, (px, py) in enumerate(peers):
            first.append(copy(j, big_ref.at[:, mine], big4.at[me, :, mine], (px, py, c)))
            first.append(copy(3 + j, small_ref, small4.at[me], (px, py, c)))
        for cp in first:
            cp.start()
        passed = []
        for j, (px, py) in enumerate(peers):
            slot = 2 * px + py
            copy(j, big_ref.at[:, mine], big4.at[slot, :, mine], (px, py, c)).wait_recv()
            passed.append(copy(6 + j, big4.at[slot, :, mine], big4.at[slot, :, mine], sibling))
            passed[-1].start()
        for j, (px, py) in enumerate(peers):
            slot = 2 * px + py
            copy(6 + j, big4.at[slot, :, theirs], big4.at[slot, :, theirs], sibling).wait_recv()
            copy(3 + j, small_ref, small4.at[slot], (px, py, c)).wait_recv()
        for cp in first + passed:
            cp.wait_send()
        for cp in local:
            cp.wait()

    return pl.pallas_call(
        body, name="gather_first", in_specs=[ANY, ANY], out_specs=[ANY, ANY],
        out_shape=[jax.ShapeDtypeStruct((N_SHARDS,) + big.shape, big.dtype),
                   jax.ShapeDtypeStruct((N_SHARDS,) + small.shape, small.dtype)],
        scratch_shapes=[pltpu.SemaphoreType.DMA((9,)), pltpu.SemaphoreType.DMA((9,)), pltpu.SemaphoreType.DMA((2,))],
    )(big, small)


HBM_SPEC = pl.BlockSpec(memory_space=pltpu.HBM)
SEM_SPEC = pl.BlockSpec(memory_space=pltpu.SEMAPHORE)
SPLIT_EFFECT = pltpu.SideEffectType.DATAFLOW_SIDE_EFFECTING


def _half_cols(buf, c, other=False):
    half = buf.shape[-1] // 2
    return pl.ds(pl.multiple_of(((1 - c) if other else c) * half, 128), half)


def _halves_plan(bufs, x, y, c, incoming):
    plan = []
    for buf in bufs:
        cols = _half_cols(buf, c)
        for (px, py) in _other_chips(x, y):
            slot = 2 * px + py if incoming else 2 * x + y
            plan.append((buf.at[2 * x + y, :, cols], buf.at[slot, :, cols], (px, py, c)))
    return plan


def _forward_plan(bufs, x, y, c, incoming):
    plan = []
    for buf in bufs:
        for (px, py) in _other_chips(x, y):
            slot = 2 * px + py
            plan.append((buf.at[slot, :, _half_cols(buf, c)], buf.at[slot, :, _half_cols(buf, c, other=incoming)],
                         (x, y, 1 - c)))
    return plan


def _scatter_plan(bufs, x, y, c, incoming):
    n = len(bufs) // 2
    plan = []
    for k in range(n):
        for j, (px, py) in enumerate(_other_chips(x, y)):
            plan.append((bufs[k].at[2 * px + py], bufs[n + k].at[j], (px, py, c)))
    return plan


def _split_start(bufs, plan, n_copies, after, *, name):
    n = len(bufs)
    extra = [] if after is None else [after]

    def body(*refs):
        ins = refs[:n]
        send_sems, recv_sems = refs[n + len(extra)], refs[n + len(extra) + 1]
        token = refs[-1]
        x, y, c = _my_place()
        for i, (src, dst, dev) in enumerate(plan(ins, x, y, c, False)):
            pltpu.make_async_remote_copy(src_ref=src, dst_ref=dst, send_sem=send_sems.at[i], recv_sem=recv_sems.at[i],
                                         device_id=dev, device_id_type=MESH).start()
        token[...] = jnp.zeros_like(token)

    outs = pl.pallas_call(
        body, name=name,
        out_shape=(pltpu.SemaphoreType.DMA((n_copies,)), pltpu.SemaphoreType.DMA((n_copies,)),
                   *[pltpu.HBM(b.shape, b.dtype) for b in bufs], jax.ShapeDtypeStruct((8, 128), F32)),
        in_specs=[HBM_SPEC] * n + [ANY] * len(extra),
        out_specs=(SEM_SPEC, SEM_SPEC, *[HBM_SPEC] * n, pl.BlockSpec(memory_space=pltpu.VMEM)),
        input_output_aliases={k: 2 + k for k in range(n)},
        compiler_params=pltpu.CompilerParams(has_side_effects=SPLIT_EFFECT),
    )(*[pltpu.with_memory_space_constraint(b, pltpu.HBM) for b in bufs], *extra)
    return outs[0], outs[1], list(outs[2:2 + n]), outs[-1]


def _split_wait(bufs, send_sems, recv_sems, plan, after, *, name):
    n = len(bufs)

    def body(*refs):
        ins = refs[:n]
        send_sems_ref, recv_sems_ref = refs[n], refs[n + 1]
        x, y, c = _my_place()
        for i, (src, dst, dev) in enumerate(plan(ins, x, y, c, True)):
            cp = pltpu.make_async_remote_copy(src_ref=src, dst_ref=dst, send_sem=send_sems_ref.at[i],
                                              recv_sem=recv_sems_ref.at[i], device_id=dev, device_id_type=MESH)
            cp.wait_send()
            cp.wait_recv()

    outs = pl.pallas_call(
        body, name=name, out_shape=tuple(pltpu.HBM(b.shape, b.dtype) for b in bufs),
        in_specs=[HBM_SPEC] * n + [SEM_SPEC, SEM_SPEC, ANY], out_specs=tuple([HBM_SPEC] * n),
        input_output_aliases={k: k for k in range(n)},
        compiler_params=pltpu.CompilerParams(has_side_effects=SPLIT_EFFECT),
    )(*bufs, send_sems, recv_sems, after)
    return list(outs)


def _fill_own_slot(shard, me_arr, *, name):
    r, c = shard.shape
    tile, steps, imap = _elementwise_tile(r, c)

    def body(me_ref, x_ref, o_ref):
        o_ref[0] = x_ref[...].astype(_MXU)

    return pl.pallas_call(
        body, name=name,
        grid_spec=pltpu.PrefetchScalarGridSpec(
            num_scalar_prefetch=1, grid=(steps,),
            in_specs=[pl.BlockSpec(tile, lambda i, me: imap(i))],
            out_specs=pl.BlockSpec((1,) + tile, lambda i, me: (me[0],) + imap(i))),
        out_shape=jax.ShapeDtypeStruct((N_SHARDS, r, c), _MXU),
        compiler_params=_cparams("parallel"),
    )(me_arr, shard)


def _swap_with_sibling(parts, *, name):
    n = len(parts)

    def body(*refs):
        ins, outs = refs[:n], refs[n:2 * n]
        send_sems, recv_sems = refs[2 * n:]
        x, y, c = _my_place()
        copies = [pltpu.make_async_remote_copy(
            src_ref=ins[k], dst_ref=outs[k], send_sem=send_sems.at[k], recv_sem=recv_sems.at[k],
            device_id=(x, y, 1 - c), device_id_type=MESH) for k in range(n)]
        for cp in copies:
            cp.start()
        for cp in copies:
            cp.wait()

    return pl.pallas_call(
        body, name=name, in_specs=[ANY] * n, out_specs=[ANY] * n,
        out_shape=[jax.ShapeDtypeStruct(a.shape, a.dtype) for a in parts],
        scratch_shapes=[pltpu.SemaphoreType.DMA((n,)), pltpu.SemaphoreType.DMA((n,))],
    )(*parts)


def _other_devices(x, y, c):
    out = []
    for mask in range(1, N_DEV):
        px, py, pc = x ^ (mask >> 2 & 1), y ^ (mask >> 1 & 1), c ^ (mask & 1)
        out.append(((px, py, pc), 4 * px + 2 * py + pc))
    return out


def _pieces_plan(bufs, x, y, c, incoming):
    pack, land = bufs
    me = 4 * x + 2 * y + c
    return [(pack.at[num], land.at[num if incoming else me], dev) for dev, num in _other_devices(x, y, c)]


def _spread_plan(bufs, x, y, c, incoming):
    piece, land = bufs
    me = 4 * x + 2 * y + c
    return [(piece, land.at[num if incoming else me], dev) for dev, num in _other_devices(x, y, c)]


def _sum_pieces(pack, land, dev_arr, *, name):
    def body(dev_ref, pack_ref, land_ref, o_ref):
        dev = dev_ref[0]
        own = pack_ref[dev]
        acc = None
        for d in range(N_DEV):
            term = jnp.where(dev == d, own, land_ref[d])
            acc = term if acc is None else acc + term
        o_ref[...] = acc

    vmem = pl.BlockSpec(memory_space=pltpu.VMEM)
    return pl.pallas_call(
        body, name=name, in_specs=[pl.BlockSpec(memory_space=pltpu.SMEM), vmem, vmem], out_specs=vmem,
        out_shape=jax.ShapeDtypeStruct(pack.shape[1:], F32),
    )(dev_arr, pack, land)


def _join_pieces(piece, land, dev_arr, *, name):
    def body(dev_ref, piece_ref, land_ref, o_ref):
        dev = dev_ref[0]
        for d in range(N_DEV):
            o_ref[d] = jnp.where(dev == d, piece_ref[...], land_ref[d])

    vmem = pl.BlockSpec(memory_space=pltpu.VMEM)
    return pl.pallas_call(
        body, name=name, in_specs=[pl.BlockSpec(memory_space=pltpu.SMEM), vmem, vmem], out_specs=vmem,
        out_shape=jax.ShapeDtypeStruct(land.shape, F32),
    )(dev_arr, piece, land)


def _adamw_native(ws, gs, ms, vs):
    n = len(ws)

    def body(*refs):
        for k in range(n):
            w_ref, g_ref, m_ref, v_ref = (refs[j * n + k] for j in range(4))
            delta, m_new, v_new = _adamw_math(w_ref[...], g_ref[...], m_ref[...], v_ref[...])
            refs[4 * n + k][...] = delta
            refs[5 * n + k][...] = m_new
            refs[6 * n + k][...] = v_new

    vmem = pl.BlockSpec(memory_space=pltpu.VMEM)
    shapes = [jax.ShapeDtypeStruct(a.shape, F32) for a in ws]
    outs = pl.pallas_call(
        body, name="adamw_small", in_specs=[vmem] * (4 * n), out_specs=[vmem] * (3 * n), out_shape=shapes * 3,
        compiler_params=pltpu.CompilerParams(vmem_limit_bytes=VMEM_LIMIT_BYTES),
    )(*ws, *gs, *ms, *vs)
    return outs[:n], outs[n:2 * n], outs[2 * n:]


def _elementwise_tile(rows, cols):
    for t in range(256, 15, -16):
        if rows % t == 0:
            return (t, cols), rows // t, lambda i: (i, 0)
    assert cols % 256 == 0
    return (rows, 256), cols // 256, lambda i: (0, i)


def _partial_sum(own, land, me_arr, *, name):
    r, c = own.shape[-2:]
    tile, steps, imap = _elementwise_tile(r, c)
    whole = own.ndim == 3

    def body(me_ref, own_ref, land_ref, o_ref):
        acc = own_ref[0] if whole else own_ref[...]
        for j in range(3):
            acc = acc + land_ref[j].astype(F32)
        o_ref[...] = acc

    own_spec = (pl.BlockSpec((1,) + tile, lambda i, me: (me[0],) + imap(i)) if whole
                else pl.BlockSpec(tile, lambda i, me: imap(i)))
    return pl.pallas_call(
        body, name=name,
        grid_spec=pltpu.PrefetchScalarGridSpec(
            num_scalar_prefetch=1, grid=(steps,),
            in_specs=[own_spec, pl.BlockSpec((3,) + tile, lambda i, me: (0,) + imap(i))],
            out_specs=pl.BlockSpec(tile, lambda i, me: imap(i))),
        out_shape=jax.ShapeDtypeStruct((r, c), F32),
        compiler_params=_cparams("parallel"),
    )(me_arr, own, land)


def _adamw_math(w, g, m, v):
    m = ADAM_B1 * m + (1.0 - ADAM_B1) * g
    v = ADAM_B2 * v + (1.0 - ADAM_B2) * (g * g)
    m_hat = m / (1.0 - ADAM_B1 ** ADAM_STEP)
    v_hat = v / (1.0 - ADAM_B2 ** ADAM_STEP)
    delta = -ADAM_LR * (m_hat / (jnp.sqrt(v_hat) + ADAM_EPS) + ADAM_WD * w)
    return delta, m, v


def _adamw(w, grad_parts, m, v, *, name):
    r, c = w.shape
    tile_shape, steps, imap = _elementwise_tile(r, c)
    n = len(grad_parts)

    def body(*refs):
        w_ref, m_ref, v_ref = refs[:3]
        g_refs = refs[3:3 + n]
        g_out, d_out, m_out, v_out = refs[3 + n:]
        g = g_refs[0][...]
        for k in range(1, n):
            g = g + g_refs[k][...]
        delta, m_new, v_new = _adamw_math(w_ref[...], g, m_ref[...], v_ref[...])
        g_out[...] = g
        d_out[...] = delta
        m_out[...] = m_new
        v_out[...] = v_new

    tile = pl.BlockSpec(tile_shape, imap)
    return pl.pallas_call(
        body, name=name, grid=(steps,), in_specs=[tile] * (3 + n), out_specs=[tile] * 4,
        out_shape=[jax.ShapeDtypeStruct((r, c), F32)] * 4,
        compiler_params=_cparams("parallel"),
    )(w, m, v, *grad_parts)


WEIGHT_NAMES = ("meta_tokens", "norm1_w", "w_in", "ssd_conv_w", "ssd_conv_b", "ssd_dt_bias", "ssd_a_log", "ssd_d",
                "ssd_norm_w", "lru_conv_w", "lru_conv_b", "lru_wa", "lru_ba", "lru_wx", "lru_bx", "lru_lambda",
                "lru_norm_w", "w_out", "norm2_w", "w_gate", "w_up", "w_down", "final_norm_w")
BIG = ("w_in", "w_out", "w_gate", "w_up", "w_down")
FFN = ("w_gate", "w_up", "w_down")
LATE = ("w_out",) + FFN
SMALL_SHARDED = {"meta_tokens": (N_META, D_MODEL), "ssd_conv_w": (CONV_K, 1536), "lru_conv_w": (CONV_K, LRU_WIDTH)}
SMALL = tuple(n for n in WEIGHT_NAMES if n not in BIG)
PACK_COLS = 1024


def _pack(arrays, row_multiple):
    flat = jnp.concatenate([a.reshape(-1) for a in arrays])
    rows = -(-flat.shape[0] // (row_multiple * PACK_COLS)) * row_multiple
    return jnp.pad(flat, (0, rows * PACK_COLS - flat.shape[0])).reshape(rows, PACK_COLS)


def _unpack(pack, shapes):
    flat = pack.reshape(-1)
    out, off = [], 0
    for s in shapes:
        size = math.prod(s)
        out.append(flat[off:off + size].reshape(s))
        off += size
    return out


def _unshard_cols(g4):
    return jnp.swapaxes(g4, 0, 1).reshape(g4.shape[1], -1)


COL_SHARDED = ("w_in", "w_gate", "w_up")
IN_ROWS = {"z": (0, 1024), "xs": (1024, 2048), "bc": (2048, 2560), "dt": (2560, 2576), "g": (2576, 3600),
           "x": (3600, IN_COLS)}


def _rows_of_shards(shards4, lo, hi):
    r = shards4.shape[1]
    parts = [shards4[k, max(lo, k * r) - k * r:min(hi, (k + 1) * r) - k * r]
             for k in range(N_SHARDS) if max(lo, k * r) < min(hi, (k + 1) * r)]
    return parts[0] if len(parts) == 1 else jnp.concatenate(parts, axis=0)


def _w_in_shard_rows(k, sections):
    lo, hi = k * (IN_COLS // N_SHARDS), (k + 1) * (IN_COLS // N_SHARDS)
    parts = []
    for arr, (a, b) in zip(sections, IN_ROWS.values()):
        if max(lo, a) < min(hi, b):
            parts.append(arr[max(lo, a) - a:min(hi, b) - a])
    return jnp.concatenate(parts, axis=0)


def _rows_view(name, block):
    return jnp.swapaxes(block[0], 0, 1) if name in COL_SHARDED else block[0]


def _param_view(name, rows):
    return (jnp.swapaxes(rows, 0, 1) if name in COL_SHARDED else rows)[None]


def kernel(x, meta_tokens, norm1_w, w_in, ssd_conv_w, ssd_conv_b, ssd_dt_bias, ssd_a_log, ssd_d, ssd_norm_w, lru_conv_w, lru_conv_b, lru_wa, lru_ba, lru_wx, lru_bx, lru_lambda, lru_norm_w, w_out, norm2_w, w_gate, w_up, w_down, final_norm_w, loss_target, m_meta_tokens, m_norm1_w, m_w_in, m_ssd_conv_w, m_ssd_conv_b, m_ssd_dt_bias, m_ssd_a_log, m_ssd_d, m_ssd_norm_w, m_lru_conv_w, m_lru_conv_b, m_lru_wa, m_lru_ba, m_lru_wx, m_lru_bx, m_lru_lambda, m_lru_norm_w, m_w_out, m_norm2_w, m_w_gate, m_w_up, m_w_down, m_final_norm_w, v_meta_tokens, v_norm1_w, v_w_in, v_ssd_conv_w, v_ssd_conv_b, v_ssd_dt_bias, v_ssd_a_log, v_ssd_d, v_ssd_norm_w, v_lru_conv_w, v_lru_conv_b, v_lru_wa, v_lru_ba, v_lru_wx, v_lru_bx, v_lru_lambda, v_lru_norm_w, v_w_out, v_norm2_w, v_w_gate, v_w_up, v_w_down, v_final_norm_w):
    w = dict(zip(WEIGHT_NAMES, (meta_tokens, norm1_w, w_in, ssd_conv_w, ssd_conv_b, ssd_dt_bias, ssd_a_log, ssd_d, ssd_norm_w, lru_conv_w, lru_conv_b, lru_wa, lru_ba, lru_wx, lru_bx, lru_lambda, lru_norm_w, w_out, norm2_w, w_gate, w_up, w_down, final_norm_w)))
    m = dict(zip(WEIGHT_NAMES, (m_meta_tokens, m_norm1_w, m_w_in, m_ssd_conv_w, m_ssd_conv_b, m_ssd_dt_bias, m_ssd_a_log, m_ssd_d, m_ssd_norm_w, m_lru_conv_w, m_lru_conv_b, m_lru_wa, m_lru_ba, m_lru_wx, m_lru_bx, m_lru_lambda, m_lru_norm_w, m_w_out, m_norm2_w, m_w_gate, m_w_up, m_w_down, m_final_norm_w)))
    v = dict(zip(WEIGHT_NAMES, (v_meta_tokens, v_norm1_w, v_w_in, v_ssd_conv_w, v_ssd_conv_b, v_ssd_dt_bias, v_ssd_a_log, v_ssd_d, v_ssd_norm_w, v_lru_conv_w, v_lru_conv_b, v_lru_wa, v_lru_ba, v_lru_wx, v_lru_bx, v_lru_lambda, v_lru_norm_w, v_w_out, v_norm2_w, v_w_gate, v_w_up, v_w_down, v_final_norm_w)))
    me = 2 * lax.axis_index("x") + lax.axis_index("y")

    big2d = {n: _rows_view(n, w[n]) for n in BIG}
    small_local = jnp.concatenate([w["meta_tokens"].reshape(-1), w["ssd_conv_w"].reshape(-1),
                                   w["lru_conv_w"].reshape(-1)])[None]
    me_arr = me.astype(jnp.int32).reshape(1)
    dev_arr = (2 * me + lax.axis_index("c")).astype(jnp.int32).reshape(1)
    w_in4, small4 = _gather_first(big2d["w_in"].astype(_MXU), small_local)
    sm = small4[:, 0]
    meta_full = _unshard_cols(sm[:, :4096].reshape(N_SHARDS, N_META, 256))
    ssd_conv_w_full = _unshard_cols(sm[:, 4096:5632].reshape(N_SHARDS, CONV_K, 384))
    lru_conv_w_full = _unshard_cols(sm[:, 5632:].reshape(N_SHARDS, CONV_K, 256))
    slots = [_fill_own_slot(big2d[n], me_arr, name="own_slot_" + n) for n in LATE]
    late_send, late_recv, late_bufs, tok_b = _split_start(slots, _halves_plan, 3 * len(LATE), small4,
                                                          name="gather_late_start")

    p = {"w_in_" + s: _rows_of_shards(w_in4, lo, hi) for s, (lo, hi) in IN_ROWS.items()}
    p["w_in_dt"] = jnp.pad(p["w_in_dt"], ((0, SEC_WIDTH["dt"] - SSD_HEADS), (0, 0)))
    p.update({"ssd_conv_w": ssd_conv_w_full, "lru_conv_w": lru_conv_w_full,
              "lru_wa": w["lru_wa"][0], "lru_wx": w["lru_wx"][0], "final_norm_w": w["final_norm_w"][None]})
    for n in ("norm1_w", "ssd_conv_b", "ssd_dt_bias", "ssd_a_log", "ssd_d", "ssd_norm_w", "lru_conv_b", "lru_ba",
              "lru_bx", "lru_lambda", "lru_norm_w", "norm2_w"):
        p[n] = w[n]
    p["norm1_w"] = p["norm1_w"] + tok_b[:1, :1]

    class Late:
        def __init__(self):
            self.pending = []

        def mid_forward(self, after):
            bufs = _split_wait(late_bufs, late_send, late_recv, _halves_plan, after, name="gather_late_wait")
            self.forward = _split_start(bufs, _forward_plan, 3 * len(LATE), None, name="forward_late_start")
            return self.forward[3][:1, :1]

        def w_out(self, after):
            send, recv, bufs, _ = self.forward
            bufs = _split_wait(bufs, send, recv, _forward_plan, after, name="forward_late_wait")
            self.late = dict(zip(LATE, (b.reshape(-1, D_MODEL) for b in bufs)))
            return self.late["w_out"]

        def ffn(self, after):
            return tuple(self.late[n] for n in FFN)

        def grads_ready(self, names, g, g_mxu):
            srcs = [g_mxu[n].reshape(N_SHARDS, -1, D_MODEL) for n in names]
            lands = [lax.empty((3,) + s.shape[1:], _MXU) for s in srcs]
            tag = "_".join(names)
            send, recv, bufs, tok = _split_start(srcs + lands, _scatter_plan, 3 * len(names), None,
                                                 name="scatter_" + tag + "_start")
            self.pending.append((names, send, recv, bufs, tag))
            self.in_flight = bufs[0]
            return tok[:1, :1]

        def landed(self, after, which):
            land = {}
            for names, send, recv, bufs, tag in self.pending:
                if names[0] in which:
                    bufs = _split_wait(bufs, send, recv, _scatter_plan, after, name="scatter_" + tag + "_wait")
                    land.update(zip(names, bufs[len(names):]))
            return land

        def small_ready(self, g, loss):
            pack = _pack([g[n] for n in SMALL] + [loss[0, :1]], 8 * N_DEV)
            pack = pack.reshape(N_DEV, -1, PACK_COLS)
            self.small = _split_start([pack, lax.empty(pack.shape, F32)], _pieces_plan, N_DEV - 1, loss,
                                      name="small_pieces_start")
            return self.small[3]

        def small_middle(self, after):
            send, recv, bufs, _ = self.small
            pack, land = _split_wait(bufs, send, recv, _pieces_plan, after, name="small_pieces_wait")
            piece = _sum_pieces(pack, land, dev_arr, name="small_pieces_sum")
            self.small = _split_start([piece, lax.empty(pack.shape, F32)], _spread_plan, N_DEV - 1, None,
                                      name="small_spread_start")
            return self.small[3]

        def small_sum(self, after):
            send, recv, bufs, _ = self.small
            piece, land = _split_wait(bufs, send, recv, _spread_plan, after, name="small_spread_wait")
            return _join_pieces(piece, land, dev_arr, name="small_join")

    late = Late()

    loss, grad_x, g, g_mxu = _local_step(x[0], loss_target[0], meta_full, p, late)

    g_mxu["w_in"] = jnp.stack([_w_in_shard_rows(k, [g_mxu["w_in_" + s] for s in SEC_NAMES])
                               for k in range(N_SHARDS)])
    g4 = {n: g[n].reshape(N_SHARDS, -1, D_MODEL) for n in LATE}
    g4["w_in"] = lax.switch(me, [functools.partial(_w_in_shard_rows, k) for k in range(N_SHARDS)],
                            [g["w_in_" + s] for s in SEC_NAMES])
    late.grads_ready(("w_in",), g, g_mxu)
    land = late.landed(late.in_flight, LATE)
    part = {n: _partial_sum(g4[n], land[n], me_arr, name="partial_" + n) for n in LATE}
    sib = dict(zip(LATE, _swap_with_sibling([part[n] for n in LATE], name="swap_late")))

    small_full_shape = {n: (SMALL_SHARDED[n] if n in SMALL_SHARDED else w[n].shape) for n in SMALL}
    red_list = _unpack(late.small_sum(sib["w_out"]), [small_full_shape[n] for n in SMALL] + [(1,)])
    loss_total = red_list[-1][0]
    g_small = {}
    for n, arr in zip(SMALL, red_list[:-1]):
        if n in SMALL_SHARDED:
            cols = SMALL_SHARDED[n][1] // N_SHARDS
            arr = lax.dynamic_slice_in_dim(arr, me * cols, cols, axis=1)
        g_small[n] = arr.reshape(w[n].shape)

    grad, delta, new_m, new_v = {}, {}, {}, {}

    def update_big(n):
        outs = _adamw(big2d[n], [part[n], sib[n]], _rows_view(n, m[n]), _rows_view(n, v[n]), name="adamw_" + n)
        grad[n], delta[n], new_m[n], new_v[n] = (_param_view(n, o) for o in outs)
        return outs[0]

    two_d = lambda a: a.reshape(1, -1) if a.ndim == 1 else a
    deltas, new_ms, new_vs = _adamw_native(*[[two_d(d[n]) for n in SMALL] for d in (w, g_small, m, v)])
    for n, dn, mn, vn in zip(SMALL, deltas, new_ms, new_vs):
        grad[n], delta[n], new_m[n], new_v[n] = (g_small[n], dn.reshape(w[n].shape), mn.reshape(w[n].shape),
                                                 vn.reshape(w[n].shape))
    for n in LATE:
        last = update_big(n)
    land.update(late.landed(last, ("w_in",)))
    part["w_in"] = _partial_sum(g4["w_in"], land["w_in"], me_arr, name="partial_w_in")
    (sib["w_in"],) = _swap_with_sibling([part["w_in"]], name="swap_w_in")
    update_big("w_in")

    return (loss_total, grad_x[None], *[grad[n] for n in WEIGHT_NAMES], *[delta[n] for n in WEIGHT_NAMES],
            *[new_m[n] for n in WEIGHT_NAMES], *[new_v[n] for n in WEIGHT_NAMES])
```

```python
import functools
import math

import jax
import jax.numpy as jnp
from jax import lax
from jax.experimental import pallas as pl
from jax.experimental.pallas import tpu as pltpu

F32 = jnp.float32
_MXU = jnp.bfloat16

D_MODEL = 1024
SEQ = 2048
N_META = 16
CHUNK = 128
T_ROWS = 2176
N_CHUNKS = T_ROWS // CHUNK
PAD_ROWS = T_ROWS - SEQ - N_META
X_ROW0 = PAD_ROWS + N_META
SSD_HEADS = 16
SSD_HEAD_DIM = 64
SSD_STATE = 128
SSD_GROUPS = 2
SSD_HPG = SSD_HEADS // SSD_GROUPS
SSD_WIDTH = 1024
LRU_WIDTH = 1024
LRU_C = 8.0
D_FF = 2816
EPS = 1e-6
IN_COLS = 4624
N_SHARDS = 4
N_DEV = 8

ADAM_LR = 0.001
ADAM_B1 = 0.9
ADAM_B2 = 0.999
ADAM_EPS = 1e-08
ADAM_WD = 0.01
ADAM_STEP = 10

VMEM_LIMIT_BYTES = 56 * 1024 * 1024

NN = (((1,), (0,)), ((), ()))
NT = (((1,), (1,)), ((), ()))
TN = (((0,), (0,)), ((), ()))


def _cparams(*sem):
    return pltpu.CompilerParams(dimension_semantics=sem, vmem_limit_bytes=VMEM_LIMIT_BYTES)


def _dot(a, b, dims=NN):
    return lax.dot_general(a.astype(_MXU), b.astype(_MXU), dims, preferred_element_type=F32)


def _dot_onehot(a, b, dims=NN, *, data=0, pieces=3):
    ops = [a, b]
    mask = ops[1 - data].astype(jnp.bfloat16)
    rest = ops[data]
    acc = None
    for _ in range(pieces):
        piece = rest.astype(jnp.bfloat16)
        ops[data], ops[1 - data] = piece, mask
        d = lax.dot_general(ops[0], ops[1], dims, preferred_element_type=F32)
        acc = d if acc is None else acc + d
        rest = rest - piece.astype(F32)
    return acc


def _sigmoid(x):
    return 0.5 * (1.0 + jnp.tanh(0.5 * x))


def _softplus(x):
    return jnp.maximum(x, 0.0) + jnp.log(1.0 + jnp.exp(-jnp.abs(x)))


def _silu(x):
    return x * _sigmoid(x)


def _silu_grad(x):
    s = _sigmoid(x)
    return s * (1.0 + x * (1.0 - s))


_GELU_C = math.sqrt(2.0 / math.pi)


def _gelu_and_grad(x):
    inner = _GELU_C * (x + 0.044715 * x * x * x)
    t = jnp.tanh(inner)
    g = 0.5 * x * (1.0 + t)
    dg = 0.5 * (1.0 + t) + 0.5 * x * (1.0 - t * t) * _GELU_C * (1.0 + 3.0 * 0.044715 * x * x)
    return g, dg


def _rms_fwd(x, w):
    rstd = lax.rsqrt(jnp.mean(x * x, axis=-1, keepdims=True) + EPS)
    return x * rstd * w


def _rms_bwd(x, w, dy):
    rstd = lax.rsqrt(jnp.mean(x * x, axis=-1, keepdims=True) + EPS)
    xhat = x * rstd
    dxhat = dy * w
    dx = rstd * (dxhat - xhat * jnp.mean(dxhat * xhat, axis=-1, keepdims=True))
    return dx, dy * xhat


def _mm(terms, m, n, *, tm, tn, mode, out_dtype, name, residual=None, n_outer=False, also_mxu=False, behind=()):
    gm, gn = m // tm, n // tn
    assert gm * tm == m and gn * tn == n
    if n_outer:
        grid = (gn, gm)
        mi = lambda g0, g1: g1
        ni = lambda g0, g1: g0
    else:
        grid = (gm, gn)
        mi = lambda g0, g1: g0
        ni = lambda g0, g1: g1
    in_specs, args = [], []
    for (a, ka, b, kb, k) in terms:
        if mode == "tn":
            in_specs.append(pl.BlockSpec((k, tm), lambda g0, g1, ka=ka: (ka, mi(g0, g1))))
        else:
            in_specs.append(pl.BlockSpec((tm, k), lambda g0, g1, ka=ka: (mi(g0, g1), ka)))
        if mode == "nt":
            in_specs.append(pl.BlockSpec((tn, k), lambda g0, g1, kb=kb: (ni(g0, g1), kb)))
        else:
            in_specs.append(pl.BlockSpec((k, tn), lambda g0, g1, kb=kb: (kb, ni(g0, g1))))
        args += [a, b]
    if residual is not None:
        in_specs.append(pl.BlockSpec((tm, tn), lambda g0, g1: (mi(g0, g1), ni(g0, g1))))
        args.append(residual)
    dims = {"nn": NN, "nt": NT, "tn": TN}[mode]
    n_terms = len(terms)
    has_res = residual is not None
    in_specs += [pl.BlockSpec(memory_space=pl.ANY)] * len(behind)
    args += list(behind)
    n_in = len(args)

    def body(*refs):
        acc = None
        for t in range(n_terms):
            d = lax.dot_general(refs[2 * t][...], refs[2 * t + 1][...], dims, preferred_element_type=F32)
            acc = d if acc is None else acc + d
        if has_res:
            acc = acc + refs[2 * n_terms][...]
        refs[n_in][...] = acc.astype(out_dtype)
        if also_mxu:
            refs[n_in + 1][...] = acc.astype(_MXU)

    tile = pl.BlockSpec((tm, tn), lambda g0, g1: (mi(g0, g1), ni(g0, g1)))
    shape = jax.ShapeDtypeStruct((m, n), out_dtype)
    return pl.pallas_call(
        body, name=name, grid=grid, in_specs=in_specs,
        out_specs=[tile, tile] if also_mxu else tile,
        out_shape=[shape, jax.ShapeDtypeStruct((m, n), _MXU)] if also_mxu else shape,
        compiler_params=_cparams("parallel", "parallel"),
    )(*args)


def _embed(x, meta, behind=()):
    def body(x_ref, meta_ref, *rest):
        o_ref = rest[-1]
        i = pl.program_id(0)

        @pl.when(i == 0)
        def _():
            o_ref[0:PAD_ROWS, :] = jnp.zeros((PAD_ROWS, D_MODEL), F32)
            o_ref[PAD_ROWS:CHUNK, :] = meta_ref[...]

        @pl.when(i > 0)
        def _():
            o_ref[...] = x_ref[...]

    return pl.pallas_call(
        body, name="embed", grid=(N_CHUNKS,),
        in_specs=[pl.BlockSpec((CHUNK, D_MODEL), lambda i: (jnp.maximum(i - 1, 0), 0)),
                  pl.BlockSpec((N_META, D_MODEL), lambda i: (0, 0))] + [pl.BlockSpec(memory_space=pl.ANY)] * len(behind),
        out_specs=pl.BlockSpec((CHUNK, D_MODEL), lambda i: (i, 0)),
        out_shape=jax.ShapeDtypeStruct((T_ROWS, D_MODEL), F32),
        compiler_params=_cparams("parallel"),
    )(x, meta, *behind)


def _rmsnorm(h, w, *, name, tm=544):
    def body(h_ref, w_ref, o_ref):
        o_ref[...] = _rms_fwd(h_ref[...], w_ref[...]).astype(_MXU)

    return pl.pallas_call(
        body, name=name, grid=(T_ROWS // tm,),
        in_specs=[pl.BlockSpec((tm, D_MODEL), lambda i: (i, 0)), pl.BlockSpec((1, D_MODEL), lambda i: (0, 0))],
        out_specs=pl.BlockSpec((tm, D_MODEL), lambda i: (i, 0)),
        out_shape=jax.ShapeDtypeStruct((T_ROWS, D_MODEL), _MXU),
        compiler_params=_cparams("parallel"),
    )(h, w)


def _norm_proj(h, w, sections, *, name, tm=544):
    widths = [s.shape[0] for s in sections]
    n = len(sections)

    def body(*refs):
        h_ref, w_ref = refs[:2]
        u_ref = refs[2 + n]
        u = _rms_fwd(h_ref[...], w_ref[...]).astype(_MXU)
        u_ref[...] = u
        for k in range(n):
            refs[3 + n + k][...] = lax.dot_general(u, refs[2 + k][...], NT, preferred_element_type=F32)

    row = lambda width: pl.BlockSpec((tm, width), lambda i: (i, 0))
    outs = pl.pallas_call(
        body, name=name, grid=(T_ROWS // tm,),
        in_specs=[row(D_MODEL), pl.BlockSpec((1, D_MODEL), lambda i: (0, 0))]
        + [pl.BlockSpec((wd, D_MODEL), lambda i: (0, 0)) for wd in widths],
        out_specs=[row(D_MODEL)] + [row(wd) for wd in widths],
        out_shape=[jax.ShapeDtypeStruct((T_ROWS, D_MODEL), _MXU)]
        + [jax.ShapeDtypeStruct((T_ROWS, wd), F32) for wd in widths],
        compiler_params=_cparams("parallel"),
    )(h, w, *sections)
    return outs[0], list(outs[1:])


def _loss_head(h2, target, fw):
    def body(h_ref, t_ref, w_ref, loss_ref, dh_ref, dhb_ref, dw_ref, acc_ref):
        i = pl.program_id(0)

        @pl.when(i == 0)
        def _():
            acc_ref[...] = jnp.zeros_like(acc_ref)
            dw_ref[...] = jnp.zeros_like(dw_ref)

        h = h_ref[...]
        w = w_ref[...]
        y = _rms_fwd(h, w)
        live = (i > 0).astype(F32)
        err = (y - t_ref[...]) * live
        acc_ref[...] += jnp.sum(err * err, axis=0, keepdims=True)
        dy = err * (1.0 / D_MODEL)
        dx, dwr = _rms_bwd(h, w, dy)
        dh_ref[...] = dx
        dhb_ref[...] = dx.astype(_MXU)
        dw_ref[...] += jnp.sum(dwr, axis=0, keepdims=True)

        @pl.when(i == N_CHUNKS - 1)
        def _():
            tot = jnp.sum(acc_ref[...], axis=1, keepdims=True) * (0.5 / D_MODEL)
            loss_ref[...] = jnp.broadcast_to(tot, (1, 128))

    return pl.pallas_call(
        body, name="loss_head", grid=(N_CHUNKS,),
        in_specs=[pl.BlockSpec((CHUNK, D_MODEL), lambda i: (i, 0)),
                  pl.BlockSpec((CHUNK, D_MODEL), lambda i: (jnp.maximum(i - 1, 0), 0)),
                  pl.BlockSpec((1, D_MODEL), lambda i: (0, 0))],
        out_specs=[pl.BlockSpec((1, 128), lambda i: (0, 0)),
                   pl.BlockSpec((CHUNK, D_MODEL), lambda i: (i, 0)),
                   pl.BlockSpec((CHUNK, D_MODEL), lambda i: (i, 0)),
                   pl.BlockSpec((1, D_MODEL), lambda i: (0, 0))],
        out_shape=[jax.ShapeDtypeStruct((1, 128), F32),
                   jax.ShapeDtypeStruct((T_ROWS, D_MODEL), F32),
                   jax.ShapeDtypeStruct((T_ROWS, D_MODEL), _MXU),
                   jax.ShapeDtypeStruct((1, D_MODEL), F32)],
        scratch_shapes=[pltpu.VMEM((1, D_MODEL), F32)],
        compiler_params=_cparams("arbitrary"),
    )(h2, target, fw)


def _mm_norm_bwd(terms, h, w, dres, *, name, tm=544, behind=()):
    n_terms = len(terms)
    in_specs, args = [], []
    for (a, b, k) in terms:
        in_specs += [pl.BlockSpec((tm, k), lambda i: (i, 0)), pl.BlockSpec((k, D_MODEL), lambda i: (0, 0))]
        args += [a, b]
    in_specs += [pl.BlockSpec((tm, D_MODEL), lambda i: (i, 0)), pl.BlockSpec((1, D_MODEL), lambda i: (0, 0)),
                 pl.BlockSpec((tm, D_MODEL), lambda i: (i, 0))] + [pl.BlockSpec(memory_space=pl.ANY)] * len(behind)
    args += [h, w, dres, *behind]

    def body(*refs):
        h_ref, w_ref, dres_ref = refs[2 * n_terms:2 * n_terms + 3]
        dh_ref, dhb_ref, dw_ref = refs[2 * n_terms + 3 + len(behind):]

        @pl.when(pl.program_id(0) == 0)
        def _():
            dw_ref[...] = jnp.zeros_like(dw_ref)

        du = None
        for t in range(n_terms):
            d = lax.dot_general(refs[2 * t][...], refs[2 * t + 1][...], NN, preferred_element_type=F32)
            du = d if du is None else du + d
        dx, dwr = _rms_bwd(h_ref[...], w_ref[...], du)
        dh = dres_ref[...] + dx
        dh_ref[...] = dh
        dhb_ref[...] = dh.astype(_MXU)
        dw_ref[...] += jnp.sum(dwr, axis=0, keepdims=True)

    return pl.pallas_call(
        body, name=name, grid=(T_ROWS // tm,), in_specs=in_specs,
        out_specs=[pl.BlockSpec((tm, D_MODEL), lambda i: (i, 0)), pl.BlockSpec((tm, D_MODEL), lambda i: (i, 0)),
                   pl.BlockSpec((1, D_MODEL), lambda i: (0, 0))],
        out_shape=[jax.ShapeDtypeStruct((T_ROWS, D_MODEL), F32), jax.ShapeDtypeStruct((T_ROWS, D_MODEL), _MXU),
                   jax.ShapeDtypeStruct((1, D_MODEL), F32)],
        compiler_params=_cparams("arbitrary"),
    )(*args)


FFN_TM = T_ROWS
FFN_TN = 256


def _ffn_up(u2, wg_t, wu_t):
    def body(u_ref, wg_ref, wu_ref, gp_ref, up_ref, act_ref):
        u = u_ref[...]
        gp = lax.dot_general(u, wg_ref[...], NT, preferred_element_type=F32)
        up = lax.dot_general(u, wu_ref[...], NT, preferred_element_type=F32)
        gp_ref[...] = gp
        up_ref[...] = up
        act_ref[...] = (_silu(gp) * up).astype(_MXU)

    tile = pl.BlockSpec((FFN_TM, FFN_TN), lambda j, i: (i, j))
    return pl.pallas_call(
        body, name="ffn_up", grid=(D_FF // FFN_TN, T_ROWS // FFN_TM),
        in_specs=[pl.BlockSpec((FFN_TM, D_MODEL), lambda j, i: (i, 0)),
                  pl.BlockSpec((FFN_TN, D_MODEL), lambda j, i: (j, 0)),
                  pl.BlockSpec((FFN_TN, D_MODEL), lambda j, i: (j, 0))],
        out_specs=[tile, tile, tile],
        out_shape=[jax.ShapeDtypeStruct((T_ROWS, D_FF), F32), jax.ShapeDtypeStruct((T_ROWS, D_FF), F32),
                   jax.ShapeDtypeStruct((T_ROWS, D_FF), _MXU)],
        compiler_params=_cparams("parallel", "parallel"),
    )(u2, wg_t, wu_t)


def _ffn_bwd_act(dh2b, wd, gp, up):
    def body(dh_ref, wd_ref, gp_ref, up_ref, dgp_ref, dup_ref):
        dact = lax.dot_general(dh_ref[...], wd_ref[...], NT, preferred_element_type=F32)
        gp = gp_ref[...]
        dgp_ref[...] = (dact * up_ref[...] * _silu_grad(gp)).astype(_MXU)
        dup_ref[...] = (dact * _silu(gp)).astype(_MXU)

    tile = pl.BlockSpec((FFN_TM, FFN_TN), lambda j, i: (i, j))
    return pl.pallas_call(
        body, name="ffn_bwd_act", grid=(D_FF // FFN_TN, T_ROWS // FFN_TM),
        in_specs=[pl.BlockSpec((FFN_TM, D_MODEL), lambda j, i: (i, 0)),
                  pl.BlockSpec((FFN_TN, D_MODEL), lambda j, i: (j, 0)), tile, tile],
        out_specs=[tile, tile],
        out_shape=[jax.ShapeDtypeStruct((T_ROWS, D_FF), _MXU), jax.ShapeDtypeStruct((T_ROWS, D_FF), _MXU)],
        compiler_params=_cparams("parallel", "parallel"),
    )(dh2b, wd, gp, up)


CONV_TC = 512
CONV_K = 4


def _conv_pre(x_ref, wv, bv, c):
    tc = wv.shape[1]
    r0 = c * CHUNK
    cur = x_ref[r0:r0 + CHUNK, :]
    if c == 0:
        cat = jnp.concatenate([jnp.zeros((8, tc), F32), cur], axis=0)
        shifted = [cur] + [pltpu.roll(cat, s, 0)[8:8 + CHUNK] for s in range(1, CONV_K)]
    else:
        shifted = [cur] + [x_ref[r0 - s:r0 - s + CHUNK, :] for s in range(1, CONV_K)]
    pre = bv
    for s in range(CONV_K):
        pre = pre + shifted[s] * wv[CONV_K - 1 - s:CONV_K - s]
    return pre, shifted


def _row_mask(c):
    if c > 0:
        return None
    return (lax.broadcasted_iota(jnp.int32, (CHUNK, 1), 0) >= PAD_ROWS).astype(F32)


def _conv_fwd(x, w, b, *, silu, name):
    cols = x.shape[1]
    tc = min(CONV_TC, cols)

    def body(x_ref, w_ref, b_ref, o_ref):
        wv, bv = w_ref[...], b_ref[...]
        for c in range(N_CHUNKS):
            pre, _ = _conv_pre(x_ref, wv, bv, c)
            y = _silu(pre) if silu else pre
            mask = _row_mask(c)
            if mask is not None:
                y = y * mask
            o_ref[c * CHUNK:(c + 1) * CHUNK, :] = y

    return pl.pallas_call(
        body, name=name, grid=(cols // tc,),
        in_specs=[pl.BlockSpec((T_ROWS, tc), lambda j: (0, j)), pl.BlockSpec((CONV_K, tc), lambda j: (0, j)),
                  pl.BlockSpec((1, tc), lambda j: (0, j))],
        out_specs=pl.BlockSpec((T_ROWS, tc), lambda j: (0, j)),
        out_shape=jax.ShapeDtypeStruct((T_ROWS, cols), F32),
        compiler_params=_cparams("parallel"),
    )(x, w, b)


def _conv_bwd(dy, x, w, b, *, silu, name):
    cols = x.shape[1]
    tc = min(CONV_TC, cols)

    def body(dy_ref, x_ref, w_ref, b_ref, dx_ref, dw_ref, db_ref):
        wv, bv = w_ref[...], b_ref[...]
        next8 = jnp.zeros((8, tc), F32)
        dws = [jnp.zeros((1, tc), F32) for _ in range(CONV_K)]
        db = jnp.zeros((1, tc), F32)
        for c in reversed(range(N_CHUNKS)):
            r0 = c * CHUNK
            pre, shifted = _conv_pre(x_ref, wv, bv, c)
            dpre = dy_ref[r0:r0 + CHUNK, :]
            if silu:
                dpre = dpre * _silu_grad(pre)
            mask = _row_mask(c)
            if mask is not None:
                dpre = dpre * mask
            cat = jnp.concatenate([dpre, next8], axis=0)
            dx = dpre * wv[CONV_K - 1:CONV_K]
            for s in range(1, CONV_K):
                dx = dx + pltpu.roll(cat, CHUNK + 8 - s, 0)[0:CHUNK] * wv[CONV_K - 1 - s:CONV_K - s]
            dx_ref[r0:r0 + CHUNK, :] = dx.astype(_MXU)
            for s in range(CONV_K):
                k = CONV_K - 1 - s
                dws[k] = dws[k] + jnp.sum(dpre * shifted[s], axis=0, keepdims=True)
            db = db + jnp.sum(dpre, axis=0, keepdims=True)
            next8 = dpre[0:8]
        dw_ref[...] = jnp.concatenate(dws, axis=0)
        db_ref[...] = db

    return pl.pallas_call(
        body, name=name, grid=(cols // tc,),
        in_specs=[pl.BlockSpec((T_ROWS, tc), lambda j: (0, j)), pl.BlockSpec((T_ROWS, tc), lambda j: (0, j)),
                  pl.BlockSpec((CONV_K, tc), lambda j: (0, j)), pl.BlockSpec((1, tc), lambda j: (0, j))],
        out_specs=[pl.BlockSpec((T_ROWS, tc), lambda j: (0, j)), pl.BlockSpec((CONV_K, tc), lambda j: (0, j)),
                   pl.BlockSpec((1, tc), lambda j: (0, j))],
        out_shape=[jax.ShapeDtypeStruct((T_ROWS, cols), _MXU), jax.ShapeDtypeStruct((CONV_K, cols), F32),
                   jax.ShapeDtypeStruct((1, cols), F32)],
        compiler_params=_cparams("parallel"),
    )(dy, x, w, b)


def _ssd_chunk_common(dt_raw, prm, c):
    a_row = -jnp.exp(prm[1:2])
    dt = _softplus(dt_raw + prm[0:1])
    rows = lax.broadcasted_iota(jnp.int32, (CHUNK, 1), 0)
    real = jnp.logical_or(c > 0, rows >= PAD_ROWS)
    dt = jnp.where(real, dt, 0.0)
    li = lax.broadcasted_iota(jnp.int32, (CHUNK, CHUNK), 0)
    si = lax.broadcasted_iota(jnp.int32, (CHUNK, CHUNK), 1)
    causal = li >= si
    tri = causal.astype(F32)
    cs = _dot_onehot(tri, dt * a_row, data=1)
    return dt, a_row, cs, cs.T, causal, tri, real


def _gated_norm_fwd(y, z, w):
    g = y * _silu(z)
    half = SSD_WIDTH // SSD_GROUPS
    outs = [_rms_fwd(g[:, k * half:(k + 1) * half], w[:, k * half:(k + 1) * half]) for k in range(SSD_GROUPS)]
    return jnp.concatenate(outs, axis=1)


GROUP_W = SSD_WIDTH // SSD_GROUPS
PAIR_W = 2 * SSD_HEAD_DIM
STATE_SHAPE = (SSD_GROUPS, SSD_STATE, GROUP_W)


def _head_expander():
    r = lax.broadcasted_iota(jnp.int32, (128, SSD_WIDTH), 0)
    c = lax.broadcasted_iota(jnp.int32, (128, SSD_WIDTH), 1)
    return (c // SSD_HEAD_DIM == r).astype(F32)


def _ssd_expand(dt, cs, prm, ex):
    cs_x = _dot_onehot(cs, ex)
    cs_last_x = cs_x[CHUNK - 1:CHUNK, :]
    return (_dot_onehot(dt, ex, pieces=2), _dot_onehot(prm, ex)[2:3], jnp.exp(cs_x), jnp.exp(cs_last_x),
            jnp.exp(cs_last_x - cs_x))


def _ssd_fwd(xs, bc, dt_raw, z, prm, norm_w, ex):
    def body(xs_ref, bc_ref, dt_ref, z_ref, prm_ref, nw_ref, ex_ref, y_ref, yn_ref, prev_ref, state):
        c = pl.program_id(0)

        @pl.when(c == 0)
        def _():
            state[...] = jnp.zeros_like(state)

        prm = prm_ref[...]
        dt, a_row, cs, cs_t, causal, _, _ = _ssd_chunk_common(dt_ref[...], prm, c)
        dt_x, d_x, e_cs_x, e_last_x, dec_x = _ssd_expand(dt, cs, prm, ex_ref[...])
        xs_all = xs_ref[...]
        bc_all = bc_ref[...]
        xdt = xs_all * dt_x
        xdec = xdt * dec_x
        lane_lo = lax.broadcasted_iota(jnp.int32, (1, PAIR_W), 1) < SSD_HEAD_DIM
        for g in range(SSD_GROUPS):
            gs = slice(g * GROUP_W, (g + 1) * GROUP_W)
            b_g = bc_all[:, g * SSD_STATE:(g + 1) * SSD_STATE]
            c_g = bc_all[:, (SSD_GROUPS + g) * SSD_STATE:(SSD_GROUPS + g + 1) * SSD_STATE]
            st = state[g]
            prev_ref[0, g] = st
            y_off = _dot(c_g, st) * e_cs_x[:, gs]
            state[g] = st * e_last_x[:, gs] + _dot(b_g.T, xdec[:, gs])
            cb = _dot(c_g, b_g, NT)
            for k in range(SSD_HPG // 2):
                h0 = g * SSD_HPG + 2 * k
                ps = slice(h0 * SSD_HEAD_DIM, h0 * SSD_HEAD_DIM + PAIR_W)
                xdt_pair = xdt[:, ps]
                yd = []
                for h in (h0, h0 + 1):
                    lmat = jnp.where(causal, jnp.exp(cs[:, h:h + 1] - cs_t[h:h + 1, :]), 0.0)
                    yd.append(_dot(cb * lmat, xdt_pair))
                y_ref[:, ps] = (jnp.where(lane_lo, yd[0], yd[1]) + y_off[:, k * PAIR_W:(k + 1) * PAIR_W]
                                + xs_all[:, ps] * d_x[:, ps])
        yn_ref[...] = _gated_norm_fwd(y_ref[...], z_ref[...], nw_ref[...]).astype(_MXU)

    row = lambda w: pl.BlockSpec((CHUNK, w), lambda c: (c, 0))
    return pl.pallas_call(
        body, name="ssd_fwd", grid=(N_CHUNKS,),
        in_specs=[row(SSD_WIDTH), row(512), row(128), row(SSD_WIDTH),
                  pl.BlockSpec((8, 128), lambda c: (0, 0)), pl.BlockSpec((1, SSD_WIDTH), lambda c: (0, 0)),
                  pl.BlockSpec((128, SSD_WIDTH), lambda c: (0, 0))],
        out_specs=[row(SSD_WIDTH), row(SSD_WIDTH),
                   pl.BlockSpec((1,) + STATE_SHAPE, lambda c: (c, 0, 0, 0))],
        out_shape=[jax.ShapeDtypeStruct((T_ROWS, SSD_WIDTH), F32), jax.ShapeDtypeStruct((T_ROWS, SSD_WIDTH), _MXU),
                   jax.ShapeDtypeStruct((N_CHUNKS,) + STATE_SHAPE, F32)],
        scratch_shapes=[pltpu.VMEM(STATE_SHAPE, F32)],
        compiler_params=_cparams("arbitrary"),
    )(xs, bc, dt_raw, z, prm, norm_w, ex)


def _ssd_bwd(dyn, dyn_block, z, y_pre, xs, bc, dt_raw, prev, prm, norm_w, ex):
    def body(dyn_ref, z_ref, y_ref, xs_ref, bc_ref, dt_ref, prev_ref, prm_ref, nw_ref, ex_ref,
             dz_ref, dxs_ref, dbc_ref, ddt_ref, dprm_ref, dnw_ref, dstate):
        step = pl.program_id(0)
        c = N_CHUNKS - 1 - step

        @pl.when(step == 0)
        def _():
            dstate[...] = jnp.zeros_like(dstate)
            dprm_ref[...] = jnp.zeros_like(dprm_ref)
            dnw_ref[...] = jnp.zeros_like(dnw_ref)

        prm = prm_ref[...]
        dt, a_row, cs, cs_t, causal, tri, real = _ssd_chunk_common(dt_ref[...], prm, c)
        realf = real.astype(F32)
        z = z_ref[...]
        y_all = y_ref[...]
        nw = nw_ref[...]
        dyn_all = dyn_ref[...]
        sz = _silu(z)
        gated = y_all * sz
        half = SSD_WIDTH // SSD_GROUPS
        dgs, dnws = [], []
        for k in range(SSD_GROUPS):
            sl = slice(k * half, (k + 1) * half)
            dgk, dwk = _rms_bwd(gated[:, sl], nw[:, sl], dyn_all[:, sl])
            dgs.append(dgk)
            dnws.append(jnp.sum(dwk, axis=0, keepdims=True))
        dgated = jnp.concatenate(dgs, axis=1)
        dnw_ref[...] += jnp.concatenate(dnws, axis=1)
        dz_ref[...] = (dgated * y_all * _silu_grad(z)).astype(_MXU)
        dy_all = dgated * sz

        ex = ex_ref[...]
        dt_x, d_x, e_cs_x, e_last_x, dec_x = _ssd_expand(dt, cs, prm, ex)
        xs_all = xs_ref[...]
        bc_all = bc_ref[...]
        xdt = xs_all * dt_x
        xdt_mxu = xdt.astype(_MXU).astype(F32)
        xdec = xdt * dec_x
        dcp = dy_all * e_cs_x
        lane_lo = lax.broadcasted_iota(jnp.int32, (1, PAIR_W), 1) < SSD_HEAD_DIM
        upper = (lax.broadcasted_iota(jnp.int32, (CHUNK, CHUNK), 0)
                 <= lax.broadcasted_iota(jnp.int32, (CHUNK, CHUNK), 1))
        last_row = (lax.broadcasted_iota(jnp.int32, (CHUNK, 1), 0) == CHUNK - 1).astype(F32)
        dbs, dcs_, dxdt_parts, last_parts = [], [], [], []
        for g in range(SSD_GROUPS):
            gs = slice(g * GROUP_W, (g + 1) * GROUP_W)
            b_g = bc_all[:, g * SSD_STATE:(g + 1) * SSD_STATE]
            c_g = bc_all[:, (SSD_GROUPS + g) * SSD_STATE:(SSD_GROUPS + g + 1) * SSD_STATE]
            prev_t = prev_ref[0, g]
            dst = dstate[g]
            dc_g = _dot(dcp[:, gs], prev_t, NT)
            db_g = _dot(xdec[:, gs], dst, NT)
            dxdt_state = _dot(b_g, dst) * dec_x[:, gs]
            dstate[g] = dst * e_last_x[:, gs] + _dot(c_g.T, dcp[:, gs])
            last_parts.append(jnp.sum(xdt_mxu[:, gs] * dxdt_state, axis=0, keepdims=True)
                              + jnp.sum(dst * prev_t, axis=0, keepdims=True) * e_last_x[:, gs])
            cb_t = _dot(b_g, c_g, NT)
            dcb_t = jnp.zeros((CHUNK, CHUNK), F32)
            for k in range(SSD_HPG // 2):
                h0 = g * SSD_HPG + 2 * k
                ps = slice(h0 * SSD_HEAD_DIM, h0 * SSD_HEAD_DIM + PAIR_W)
                dy_pair = dy_all[:, ps]
                xdt_pair = xdt[:, ps]
                dd = []
                for h in (h0, h0 + 1):
                    lmat_t = jnp.where(upper, jnp.exp(cs_t[h:h + 1, :] - cs[:, h:h + 1]), 0.0)
                    dd.append(_dot(cb_t * lmat_t, dy_pair))
                    mine = lane_lo if h == h0 else jnp.logical_not(lane_lo)
                    dcb_t = dcb_t + _dot(jnp.where(mine, xdt_pair, 0.0), dy_pair, NT) * lmat_t
                dxdt_parts.append(jnp.where(lane_lo, dd[0], dd[1]) + dxdt_state[:, k * PAIR_W:(k + 1) * PAIR_W])
            dc_g = dc_g + _dot(dcb_t, b_g, TN)
            db_g = db_g + _dot(dcb_t, c_g)
            dbs.append(db_g * realf)
            dcs_.append(dc_g * realf)
        dbc_ref[...] = jnp.concatenate(dbs + dcs_, axis=1)
        dxdt = jnp.concatenate(dxdt_parts, axis=1)
        dxs_ref[...] = (dxdt * dt_x + dy_all * d_x) * realf
        ddt_all = _dot_onehot(dxdt * xs_all, ex, NT, pieces=2)
        rows = jnp.concatenate([jnp.concatenate(last_parts, axis=1), jnp.sum(dy_all * xs_all, axis=0, keepdims=True),
                                jnp.zeros((6, SSD_WIDTH), F32)], axis=0)
        rows = _dot_onehot(rows, ex, NT, pieces=2)
        dd_row = rows[1:2]
        dy_mxu = dy_all.astype(_MXU).astype(F32)
        dcs_all = (_dot_onehot(dy_mxu * (y_all - xs_all * d_x), ex, NT) - _dot_onehot(xdt_mxu * dxdt, ex, NT)
                   + last_row * rows[0:1])
        dda = _dot_onehot(tri, dcs_all, TN, data=1)
        ddt = (ddt_all + dda * a_row) * realf
        ddt_raw = ddt * _sigmoid(dt_ref[...] + prm[0:1])
        ddt_ref[...] = ddt_raw.astype(_MXU)
        da_log = jnp.sum(dda * dt, axis=0, keepdims=True) * a_row
        dprm_ref[0:1, :] += jnp.sum(ddt_raw, axis=0, keepdims=True)
        dprm_ref[1:2, :] += da_log
        dprm_ref[2:3, :] += dd_row

    rev = lambda w, blk=0: pl.BlockSpec((CHUNK, w), lambda s, blk=blk: (N_CHUNKS - 1 - s, blk))
    return pl.pallas_call(
        body, name="ssd_bwd", grid=(N_CHUNKS,),
        in_specs=[rev(SSD_WIDTH, dyn_block), rev(SSD_WIDTH), rev(SSD_WIDTH), rev(SSD_WIDTH), rev(512), rev(128),
                  pl.BlockSpec((1,) + STATE_SHAPE, lambda s: (N_CHUNKS - 1 - s, 0, 0, 0)),
                  pl.BlockSpec((8, 128), lambda s: (0, 0)), pl.BlockSpec((1, SSD_WIDTH), lambda s: (0, 0)),
                  pl.BlockSpec((128, SSD_WIDTH), lambda s: (0, 0))],
        out_specs=[rev(SSD_WIDTH), rev(SSD_WIDTH), rev(512), rev(128),
                   pl.BlockSpec((8, 128), lambda s: (0, 0)), pl.BlockSpec((1, SSD_WIDTH), lambda s: (0, 0))],
        out_shape=[jax.ShapeDtypeStruct((T_ROWS, SSD_WIDTH), _MXU), jax.ShapeDtypeStruct((T_ROWS, SSD_WIDTH), F32),
                   jax.ShapeDtypeStruct((T_ROWS, 512), F32), jax.ShapeDtypeStruct((T_ROWS, 128), _MXU),
                   jax.ShapeDtypeStruct((8, 128), F32), jax.ShapeDtypeStruct((1, SSD_WIDTH), F32)],
        scratch_shapes=[pltpu.VMEM(STATE_SHAPE, F32)],
        compiler_params=_cparams("arbitrary"),
    )(dyn, z, y_pre, xs, bc, dt_raw, prev, prm, norm_w, ex)


LRU_PAIRS = 8


def _lru_gates(xr, wa_ref, wx_ref, prm):
    pre_r, pre_i = [], []
    for k in range(LRU_PAIRS):
        xk = xr[:, k * 128:(k + 1) * 128]
        pre_r.append(_dot(xk, wa_ref[k]))
        pre_i.append(_dot(xk, wx_ref[k]))
    r = _sigmoid(jnp.concatenate(pre_r, axis=1) + prm[0:1])
    i = _sigmoid(jnp.concatenate(pre_i, axis=1) + prm[1:2])
    sp = _softplus(-prm[2:3])
    log_a = (-LRU_C) * r * sp
    a = jnp.exp(log_a)
    s = jnp.sqrt(-jnp.tanh(log_a) * (a * a + 1.0))
    return r, i, a, s, sp


def _lru_fwd(xr, gate, wa, wx, prm):
    def body(xr_ref, g_ref, wa_ref, wx_ref, prm_ref, hs_ref, yn_ref, carry, a_s, u_s):
        @pl.when(pl.program_id(0) == 0)
        def _():
            carry[...] = jnp.zeros_like(carry)

        prm = prm_ref[...]
        xr_t = xr_ref[...]
        _, i, a, s, _ = _lru_gates(xr_t, wa_ref, wx_ref, prm)
        a_s[...] = a
        u_s[...] = s * (i * xr_t)
        rid = lax.broadcasted_iota(jnp.int32, (8, LRU_WIDTH), 0)

        def group(k, h):
            off = pl.multiple_of(k * 8, 8)
            a8 = a_s[pl.ds(off, 8), :]
            u8 = u_s[pl.ds(off, 8), :]
            out = jnp.zeros((8, LRU_WIDTH), F32)
            for r_ in range(8):
                h = a8[r_:r_ + 1] * h + u8[r_:r_ + 1]
                out = jnp.where(rid == r_, h, out)
            hs_ref[pl.ds(off, 8), :] = out
            return h

        carry[0:1, :] = lax.fori_loop(0, CHUNK // 8, group, carry[0:1, :])
        gel, _ = _gelu_and_grad(g_ref[...])
        yn_ref[...] = _rms_fwd(gel * hs_ref[...], prm[3:4]).astype(_MXU)

    row = pl.BlockSpec((CHUNK, LRU_WIDTH), lambda t: (t, 0))
    wspec = pl.BlockSpec((LRU_PAIRS, 128, 128), lambda t: (0, 0, 0))
    return pl.pallas_call(
        body, name="lru_fwd", grid=(N_CHUNKS,),
        in_specs=[row, row, wspec, wspec, pl.BlockSpec((8, LRU_WIDTH), lambda t: (0, 0))],
        out_specs=[row, row],
        out_shape=[jax.ShapeDtypeStruct((T_ROWS, LRU_WIDTH), F32), jax.ShapeDtypeStruct((T_ROWS, LRU_WIDTH), _MXU)],
        scratch_shapes=[pltpu.VMEM((8, LRU_WIDTH), F32), pltpu.VMEM((CHUNK, LRU_WIDTH), F32),
                        pltpu.VMEM((CHUNK, LRU_WIDTH), F32)],
        compiler_params=_cparams("arbitrary"),
    )(xr, gate, wa, wx, prm)


def _lru_bwd(dyn, dyn_block, gate, xr, hs, wa, wx, wa_t, wx_t, prm):
    def body(dyn_ref, g_ref, xr_ref, hs_ref, hsp_ref, wa_ref, wx_ref, wat_ref, wxt_ref, prm_ref,
             dg_ref, dxr_ref, dwa_ref, dwx_ref, dprm_ref, carry, a_s, d_s):
        step = pl.program_id(0)
        tile = N_CHUNKS - 1 - step

        @pl.when(step == 0)
        def _():
            carry[...] = jnp.zeros_like(carry)
            dwa_ref[...] = jnp.zeros_like(dwa_ref)
            dwx_ref[...] = jnp.zeros_like(dwx_ref)
            dprm_ref[...] = jnp.zeros_like(dprm_ref)

        prm = prm_ref[...]
        xr_t = xr_ref[...]
        r, i, a, s, sp = _lru_gates(xr_t, wa_ref, wx_ref, prm)
        hs_t = hs_ref[...]
        gel, dgel = _gelu_and_grad(g_ref[...])
        dy, dnw = _rms_bwd(gel * hs_t, prm[3:4], dyn_ref[...])
        dg_ref[...] = (dy * hs_t * dgel).astype(_MXU)
        a_s[...] = a
        d_s[...] = dy * gel
        rid = lax.broadcasted_iota(jnp.int32, (8, LRU_WIDTH), 0)

        def group(k, cr):
            off = pl.multiple_of((CHUNK // 8 - 1 - k) * 8, 8)
            a8 = a_s[pl.ds(off, 8), :]
            d8 = d_s[pl.ds(off, 8), :]
            out = jnp.zeros((8, LRU_WIDTH), F32)
            for r_ in reversed(range(8)):
                dht = d8[r_:r_ + 1] + cr
                out = jnp.where(rid == r_, dht, out)
                cr = a8[r_:r_ + 1] * dht
            d_s[pl.ds(off, 8), :] = out
            return cr

        carry[0:1, :] = lax.fori_loop(0, CHUNK // 8, group, carry[0:1, :])
        dht = d_s[...]
        before = hsp_ref[CHUNK - 8:CHUNK, :][7:8] * (tile > 0).astype(F32)
        first = lax.broadcasted_iota(jnp.int32, (CHUNK, 1), 0) == 0
        hprev = jnp.where(first, before, pltpu.roll(hs_t, 1, 0))
        da = dht * hprev
        ixr = i * xr_t
        ds = dht * ixr
        dlog_a = da * a - ds * (a * a) * lax.rsqrt(s * s)
        dr = dlog_a * ((-LRU_C) * sp)
        dsp = jnp.sum(dlog_a * ((-LRU_C) * r), axis=0, keepdims=True)
        dlam = dsp * (-_sigmoid(-prm[2:3]))
        di = dht * s * xr_t
        dpre_r = dr * r * (1.0 - r)
        dpre_i = di * i * (1.0 - i)
        dxr = dht * s * i
        parts = []
        for k in range(LRU_PAIRS):
            sl = slice(k * 128, (k + 1) * 128)
            parts.append(_dot(dpre_r[:, sl], wat_ref[k]) + _dot(dpre_i[:, sl], wxt_ref[k]))
            dwa_ref[k] += _dot(xr_t[:, sl], dpre_r[:, sl], TN)
            dwx_ref[k] += _dot(xr_t[:, sl], dpre_i[:, sl], TN)
        dxr_ref[...] = dxr + jnp.concatenate(parts, axis=1)
        dprm_ref[0:1, :] += jnp.sum(dpre_r, axis=0, keepdims=True)
        dprm_ref[1:2, :] += jnp.sum(dpre_i, axis=0, keepdims=True)
        dprm_ref[2:3, :] += dlam
        dprm_ref[3:4, :] += jnp.sum(dnw, axis=0, keepdims=True)

    rev = lambda blk=0: pl.BlockSpec((CHUNK, LRU_WIDTH), lambda s, blk=blk: (N_CHUNKS - 1 - s, blk))
    wspec = pl.BlockSpec((LRU_PAIRS, 128, 128), lambda s: (0, 0, 0))
    return pl.pallas_call(
        body, name="lru_bwd", grid=(N_CHUNKS,),
        in_specs=[rev(dyn_block), rev(), rev(), rev(),
                  pl.BlockSpec((CHUNK, LRU_WIDTH), lambda s: (jnp.maximum(N_CHUNKS - 2 - s, 0), 0)),
                  wspec, wspec, wspec, wspec, pl.BlockSpec((8, LRU_WIDTH), lambda s: (0, 0))],
        out_specs=[rev(), rev(), wspec, wspec, pl.BlockSpec((8, LRU_WIDTH), lambda s: (0, 0))],
        out_shape=[jax.ShapeDtypeStruct((T_ROWS, LRU_WIDTH), _MXU), jax.ShapeDtypeStruct((T_ROWS, LRU_WIDTH), F32),
                   jax.ShapeDtypeStruct((LRU_PAIRS, 128, 128), F32), jax.ShapeDtypeStruct((LRU_PAIRS, 128, 128), F32),
                   jax.ShapeDtypeStruct((8, LRU_WIDTH), F32)],
        scratch_shapes=[pltpu.VMEM((8, LRU_WIDTH), F32), pltpu.VMEM((CHUNK, LRU_WIDTH), F32),
                        pltpu.VMEM((CHUNK, LRU_WIDTH), F32)],
        compiler_params=_cparams("arbitrary"),
    )(dyn, gate, xr, hs, hs, wa, wx, wa_t, wx_t, prm)


SEC_NAMES = ("z", "xs", "bc", "dt", "g", "x")
SEC_WIDTH = {"z": 1024, "xs": 1024, "bc": 512, "dt": 128, "g": 1024, "x": 1024}


def _pair_blocks(w):
    w = w.reshape(LRU_PAIRS, 2, 64, 64)
    zero = jnp.zeros((LRU_PAIRS, 64, 64), w.dtype)
    top = jnp.concatenate([w[:, 0], zero], axis=2)
    bot = jnp.concatenate([zero, w[:, 1]], axis=2)
    return jnp.concatenate([top, bot], axis=1)


def _unpair_blocks(wp):
    return jnp.stack([wp[:, :64, :64], wp[:, 64:, 64:]], axis=1).reshape(16, 64, 64)


def _pad_lanes(v, width=128):
    return jnp.pad(v, ((0, 0), (0, width - v.shape[1])))


class _Resident:
    before_embed = ()

    def __init__(self, w_in_sections, w_out, w_gate, w_up, w_down):
        self._w_in, self._w_out, self._ffn = w_in_sections, w_out, (w_gate, w_up, w_down)

    def w_in(self, after):
        return self._w_in

    def mid_forward(self, after):
        return jnp.zeros((1, 1), F32)

    def w_out(self, after):
        return self._w_out

    def ffn(self, after):
        return self._ffn

    def grads_ready(self, names, g, g_mxu):
        return jnp.zeros((1, 1), F32)

    def small_ready(self, g, loss):
        return jnp.zeros((1, 1), F32)

    def small_middle(self, after):
        return jnp.zeros((1, 1), F32)


def _local_step(x, target, meta, p, late):
    g, g_mxu = {}, {}
    ex = _head_expander()
    h0 = _embed(x, meta, late.before_embed)
    w_in = late.w_in(h0)
    u1, projs = _norm_proj(h0, p["norm1_w"], [w_in[s] for s in SEC_NAMES], name="norm_in_proj")
    proj = dict(zip(SEC_NAMES, projs))
    ssd_prm = jnp.concatenate([_pad_lanes(p["ssd_dt_bias"]), _pad_lanes(p["ssd_a_log"]), _pad_lanes(p["ssd_d"]),
                               jnp.zeros((5, 128), F32)], axis=0)
    xs_act = _conv_fwd(proj["xs"], p["ssd_conv_w"][:, :SSD_WIDTH], p["ssd_conv_b"][:, :SSD_WIDTH], silu=True,
                       name="ssd_conv_xs")
    bc_act = _conv_fwd(proj["bc"], p["ssd_conv_w"][:, SSD_WIDTH:], p["ssd_conv_b"][:, SSD_WIDTH:], silu=True,
                       name="ssd_conv_bc")
    y_pre, y_ssd, prev = _ssd_fwd(xs_act, bc_act, proj["dt"], proj["z"], ssd_prm, p["ssd_norm_w"], ex)
    xr = _conv_fwd(proj["x"], p["lru_conv_w"], p["lru_conv_b"], silu=False, name="lru_conv")
    wa_p, wx_p = _pair_blocks(p["lru_wa"]), _pair_blocks(p["lru_wx"])
    lru_prm = jnp.concatenate([p["lru_ba"], p["lru_bx"], p["lru_lambda"], p["lru_norm_w"],
                               jnp.zeros((4, LRU_WIDTH), F32)], axis=0)
    hs, y_lru = _lru_fwd(xr, proj["g"], wa_p.astype(_MXU), wx_p.astype(_MXU), lru_prm + late.mid_forward(xr))
    ycat = jnp.concatenate([y_ssd, y_lru], axis=1)
    w_out = late.w_out(ycat)
    h1 = _mm([(ycat, 0, w_out, 0, 2 * D_MODEL)], T_ROWS, D_MODEL, tm=T_ROWS, tn=256, mode="nn", out_dtype=F32,
             name="out_proj", residual=h0)
    u2 = _rmsnorm(h1, p["norm2_w"], name="norm2")
    w_gate, w_up, w_down = late.ffn(u2)
    gp, up, act = _ffn_up(u2, w_gate, w_up)
    h2 = _mm([(act, 0, w_down, 0, D_FF)], T_ROWS, D_MODEL, tm=T_ROWS, tn=256, mode="nn", out_dtype=F32,
             name="ffn_down", residual=h1)
    loss, dh2, dh2b, g["final_norm_w"] = _loss_head(h2, target, p["final_norm_w"])
    dgp, dup = _ffn_bwd_act(dh2b, w_down, gp, up)
    g["w_down"], g_mxu["w_down"] = _mm([(act, 0, dh2b, 0, T_ROWS)], D_FF, D_MODEL, tm=1408, tn=512, mode="tn",
                                       out_dtype=F32, name="dw_down", also_mxu=True)
    dh1, dh1b, g["norm2_w"] = _mm_norm_bwd([(dgp, w_gate, D_FF), (dup, w_up, D_FF)], h1, p["norm2_w"], dh2,
                                           name="ffn_bwd_in")
    g["w_gate"], g_mxu["w_gate"] = _mm([(dgp, 0, u2, 0, T_ROWS)], D_FF, D_MODEL, tm=1408, tn=512, mode="tn",
                                       out_dtype=F32, name="dw_gate", also_mxu=True)
    g["w_up"], g_mxu["w_up"] = _mm([(dup, 0, u2, 0, T_ROWS)], D_FF, D_MODEL, tm=1408, tn=512, mode="tn",
                                   out_dtype=F32, name="dw_up", also_mxu=True)
    sent = late.grads_ready(("w_down", "w_gate", "w_up"), g, g_mxu)
    g["w_out"], g_mxu["w_out"] = _mm([(ycat, 0, dh1b, 0, T_ROWS)], 2 * D_MODEL, D_MODEL, tm=1024, tn=512, mode="tn",
                                     out_dtype=F32, name="dw_out", also_mxu=True, behind=(sent,))
    sent = late.grads_ready(("w_out",), g, g_mxu)
    dycat = _mm([(dh1b, 0, w_out, 0, D_MODEL)], T_ROWS, 2 * D_MODEL, tm=T_ROWS, tn=256, mode="nt", out_dtype=F32,
                name="out_proj_bwd", behind=(sent,))
    dgate, dxr, dwa_p, dwx_p, dlru_prm = _lru_bwd(dycat, 1, proj["g"], xr, hs, wa_p.astype(_MXU), wx_p.astype(_MXU),
                                                  jnp.swapaxes(wa_p, 1, 2).astype(_MXU),
                                                  jnp.swapaxes(wx_p, 1, 2).astype(_MXU), lru_prm)
    g["lru_wa"], g["lru_wx"] = _unpair_blocks(dwa_p), _unpair_blocks(dwx_p)
    g["lru_ba"], g["lru_bx"], g["lru_lambda"], g["lru_norm_w"] = (dlru_prm[k:k + 1] for k in range(4))
    dx_lru, g["lru_conv_w"], g["lru_conv_b"] = _conv_bwd(dxr, proj["x"], p["lru_conv_w"], p["lru_conv_b"], silu=False,
                                                         name="lru_conv_bwd")
    dz, dxs_act, dbc_act, ddt, dssd_prm, g["ssd_norm_w"] = _ssd_bwd(dycat, 0, proj["z"], y_pre, xs_act, bc_act,
                                                                    proj["dt"], prev, ssd_prm, p["ssd_norm_w"], ex)
    g["ssd_dt_bias"], g["ssd_a_log"], g["ssd_d"] = (dssd_prm[k:k + 1, :SSD_HEADS] for k in range(3))
    dxs, dcw_xs, dcb_xs = _conv_bwd(dxs_act, proj["xs"], p["ssd_conv_w"][:, :SSD_WIDTH],
                                    p["ssd_conv_b"][:, :SSD_WIDTH], silu=True, name="ssd_conv_xs_bwd")
    dbc, dcw_bc, dcb_bc = _conv_bwd(dbc_act, proj["bc"], p["ssd_conv_w"][:, SSD_WIDTH:],
                                    p["ssd_conv_b"][:, SSD_WIDTH:], silu=True, name="ssd_conv_bc_bwd")
    g["ssd_conv_w"] = jnp.concatenate([dcw_xs, dcw_bc], axis=1)
    g["ssd_conv_b"] = jnp.concatenate([dcb_xs, dcb_bc], axis=1)
    dproj = {"z": dz, "xs": dxs, "bc": dbc, "dt": ddt, "g": dgate, "x": dx_lru}
    for s in SEC_NAMES:
        wdt = SEC_WIDTH[s]
        g["w_in_" + s], g_mxu["w_in_" + s] = _mm([(dproj[s], 0, u1, 0, T_ROWS)], wdt, D_MODEL, tm=min(wdt, 1024),
                                                 tn=512, mode="tn", out_dtype=F32, name="dw_in_" + s, also_mxu=True)
    sent = late.grads_ready(("w_in",), g, g_mxu)
    dh0, _, g["norm1_w"] = _mm_norm_bwd([(dproj[s], w_in[s], SEC_WIDTH[s]) for s in SEC_NAMES], h0,
                                        p["norm1_w"], dh1, name="in_proj_bwd", behind=(sent,))
    g["meta_tokens"] = dh0[PAD_ROWS:X_ROW0]
    late.small_ready(g, loss)
    return loss, dh0[X_ROW0:], g, g_mxu


MESH = pl.DeviceIdType.MESH
ANY = pl.BlockSpec(memory_space=pl.ANY)


def _my_place():
    return lax.axis_index("x"), lax.axis_index("y"), lax.axis_index("c")


def _other_chips(x, y):
    return [(1 - x, y), (x, 1 - y), (1 - x, 1 - y)]


HBM_SPEC = pl.BlockSpec(memory_space=pltpu.HBM)
SEM_SPEC = pl.BlockSpec(memory_space=pltpu.SEMAPHORE)
SPLIT_EFFECT = pltpu.SideEffectType.DATAFLOW_SIDE_EFFECTING


def _half_cols(buf, c, other=False):
    half = buf.shape[-1] // 2
    return pl.ds(pl.multiple_of(((1 - c) if other else c) * half, 128), half)


def _halves_plan(bufs, x, y, c, incoming):
    plan = []
    for buf in bufs:
        cols = _half_cols(buf, c)
        for (px, py) in _other_chips(x, y):
            slot = 2 * px + py if incoming else 2 * x + y
            plan.append((buf.at[2 * x + y, :, cols], buf.at[slot, :, cols], (px, py, c)))
    return plan


def _forward_plan(bufs, x, y, c, incoming):
    plan = []
    for buf in bufs:
        for (px, py) in _other_chips(x, y):
            slot = 2 * px + py
            plan.append((buf.at[slot, :, _half_cols(buf, c)], buf.at[slot, :, _half_cols(buf, c, other=incoming)],
                         (x, y, 1 - c)))
    return plan


def _scatter_plan(bufs, x, y, c, incoming):
    n = len(bufs) // 2
    plan = []
    for k in range(n):
        for j, (px, py) in enumerate(_other_chips(x, y)):
            plan.append((bufs[k].at[2 * px + py], bufs[n + k].at[j], (px, py, c)))
    return plan


def _split_start(bufs, plan, n_copies, after, *, name):
    n = len(bufs)
    extra = [] if after is None else [after]

    def body(*refs):
        ins = refs[:n]
        send_sems, recv_sems = refs[n + len(extra)], refs[n + len(extra) + 1]
        token = refs[-1]
        x, y, c = _my_place()
        for i, (src, dst, dev) in enumerate(plan(ins, x, y, c, False)):
            pltpu.make_async_remote_copy(src_ref=src, dst_ref=dst, send_sem=send_sems.at[i], recv_sem=recv_sems.at[i],
                                         device_id=dev, device_id_type=MESH).start()
        token[...] = jnp.zeros_like(token)

    outs = pl.pallas_call(
        body, name=name,
        out_shape=(pltpu.SemaphoreType.DMA((n_copies,)), pltpu.SemaphoreType.DMA((n_copies,)),
                   *[pltpu.HBM(b.shape, b.dtype) for b in bufs], jax.ShapeDtypeStruct((8, 128), F32)),
        in_specs=[HBM_SPEC] * n + [ANY] * len(extra),
        out_specs=(SEM_SPEC, SEM_SPEC, *[HBM_SPEC] * n, pl.BlockSpec(memory_space=pltpu.VMEM)),
        input_output_aliases={k: 2 + k for k in range(n)},
        compiler_params=pltpu.CompilerParams(has_side_effects=SPLIT_EFFECT),
    )(*[pltpu.with_memory_space_constraint(b, pltpu.HBM) for b in bufs], *extra)
    return outs[0], outs[1], list(outs[2:2 + n]), outs[-1]


def _split_wait(bufs, send_sems, recv_sems, plan, after, *, name):
    n = len(bufs)
    after = list(after) if isinstance(after, (list, tuple)) else [after]

    def body(*refs):
        ins = refs[:n]
        send_sems_ref, recv_sems_ref = refs[n], refs[n + 1]
        x, y, c = _my_place()
        for i, (src, dst, dev) in enumerate(plan(ins, x, y, c, True)):
            cp = pltpu.make_async_remote_copy(src_ref=src, dst_ref=dst, send_sem=send_sems_ref.at[i],
                                              recv_sem=recv_sems_ref.at[i], device_id=dev, device_id_type=MESH)
            cp.wait_send()
            cp.wait_recv()

    outs = pl.pallas_call(
        body, name=name, out_shape=tuple(pltpu.HBM(b.shape, b.dtype) for b in bufs),
        in_specs=[HBM_SPEC] * n + [SEM_SPEC, SEM_SPEC] + [ANY] * len(after), out_specs=tuple([HBM_SPEC] * n),
        input_output_aliases={k: k for k in range(n)},
        compiler_params=pltpu.CompilerParams(has_side_effects=SPLIT_EFFECT),
    )(*bufs, send_sems, recv_sems, *after)
    return list(outs)


def _fill_own_slots(shards, me_arr, *, name, behind=()):
    n = len(shards)
    n_in = n + len(behind)

    def body(me_ref, *refs):
        for k in range(n):
            refs[n_in + k][0] = refs[k][...].astype(_MXU)

    half = D_MODEL // 2
    return pl.pallas_call(
        body, name=name,
        grid_spec=pltpu.PrefetchScalarGridSpec(
            num_scalar_prefetch=1, grid=(2,),
            in_specs=[pl.BlockSpec((s.shape[0], half), lambda i, me: (0, i)) for s in shards]
            + [pl.BlockSpec(memory_space=pl.ANY)] * len(behind),
            out_specs=[pl.BlockSpec((1, s.shape[0], half), lambda i, me: (me[0], 0, i)) for s in shards]),
        out_shape=[jax.ShapeDtypeStruct((N_SHARDS,) + s.shape, _MXU) for s in shards],
        compiler_params=_cparams("parallel"),
    )(me_arr, *shards, *behind)


def _gather_small(small):
    def body(s_ref, o_ref, send_sems, recv_sems, local_sem):
        x, y, c = _my_place()
        me = 2 * x + y
        local = pltpu.make_async_copy(s_ref, o_ref.at[me], local_sem)
        local.start()
        copies = [(pltpu.make_async_remote_copy(src_ref=s_ref, dst_ref=o_ref.at[me], send_sem=send_sems.at[j],
                                                recv_sem=recv_sems.at[j], device_id=(px, py, c), device_id_type=MESH),
                   2 * px + py) for j, (px, py) in enumerate(_other_chips(x, y))]
        for cp, _ in copies:
            cp.start()
        for j, (cp, slot) in enumerate(copies):
            cp.wait_send()
            pltpu.make_async_remote_copy(src_ref=s_ref, dst_ref=o_ref.at[slot], send_sem=send_sems.at[j],
                                         recv_sem=recv_sems.at[j], device_id=(x, y, c),
                                         device_id_type=MESH).wait_recv()
        local.wait()

    return pl.pallas_call(
        body, name="gather_small", in_specs=[ANY], out_specs=ANY,
        out_shape=jax.ShapeDtypeStruct((N_SHARDS,) + small.shape, small.dtype),
        scratch_shapes=[pltpu.SemaphoreType.DMA((3,)), pltpu.SemaphoreType.DMA((3,)), pltpu.SemaphoreType.DMA],
    )(small)


def _swap_with_sibling(parts, *, name):
    n = len(parts)

    def body(*refs):
        ins, outs = refs[:n], refs[n:2 * n]
        send_sems, recv_sems = refs[2 * n:]
        x, y, c = _my_place()
        copies = [pltpu.make_async_remote_copy(
            src_ref=ins[k], dst_ref=outs[k], send_sem=send_sems.at[k], recv_sem=recv_sems.at[k],
            device_id=(x, y, 1 - c), device_id_type=MESH) for k in range(n)]
        for cp in copies:
            cp.start()
        for cp in copies:
            cp.wait()

    return pl.pallas_call(
        body, name=name, in_specs=[ANY] * n, out_specs=[ANY] * n,
        out_shape=[jax.ShapeDtypeStruct(a.shape, a.dtype) for a in parts],
        scratch_shapes=[pltpu.SemaphoreType.DMA((n,)), pltpu.SemaphoreType.DMA((n,))],
    )(*parts)


def _other_devices(x, y, c):
    out = []
    for mask in range(1, N_DEV):
        px, py, pc = x ^ (mask >> 2 & 1), y ^ (mask >> 1 & 1), c ^ (mask & 1)
        out.append(((px, py, pc), 4 * px + 2 * py + pc))
    return out


def _pieces_plan(bufs, x, y, c, incoming):
    pack, land = bufs
    me = 4 * x + 2 * y + c
    return [(pack.at[num], land.at[num if incoming else me], dev) for dev, num in _other_devices(x, y, c)]


def _spread_plan(bufs, x, y, c, incoming):
    piece, land = bufs
    me = 4 * x + 2 * y + c
    return [(piece, land.at[num if incoming else me], dev) for dev, num in _other_devices(x, y, c)]


def _sum_pieces(pack, land, dev_arr, *, name):
    def body(dev_ref, pack_ref, land_ref, o_ref):
        dev = dev_ref[0]
        own = pack_ref[dev]
        acc = None
        for d in range(N_DEV):
            term = jnp.where(dev == d, own, land_ref[d])
            acc = term if acc is None else acc + term
        o_ref[...] = acc

    vmem = pl.BlockSpec(memory_space=pltpu.VMEM)
    return pl.pallas_call(
        body, name=name, in_specs=[pl.BlockSpec(memory_space=pltpu.SMEM), vmem, vmem], out_specs=vmem,
        out_shape=jax.ShapeDtypeStruct(pack.shape[1:], F32),
    )(dev_arr, pack, land)


def _join_pieces(piece, land, dev_arr, *, name):
    def body(dev_ref, piece_ref, land_ref, o_ref):
        dev = dev_ref[0]
        for d in range(N_DEV):
            o_ref[d] = jnp.where(dev == d, piece_ref[...], land_ref[d])

    vmem = pl.BlockSpec(memory_space=pltpu.VMEM)
    return pl.pallas_call(
        body, name=name, in_specs=[pl.BlockSpec(memory_space=pltpu.SMEM), vmem, vmem], out_specs=vmem,
        out_shape=jax.ShapeDtypeStruct(land.shape, F32),
    )(dev_arr, piece, land)


def _adamw_native(ws, gs, ms, vs):
    n = len(ws)

    def body(*refs):
        for k in range(n):
            w_ref, g_ref, m_ref, v_ref = (refs[j * n + k] for j in range(4))
            delta, m_new, v_new = _adamw_math(w_ref[...], g_ref[...], m_ref[...], v_ref[...])
            refs[4 * n + k][...] = delta
            refs[5 * n + k][...] = m_new
            refs[6 * n + k][...] = v_new

    vmem = pl.BlockSpec(memory_space=pltpu.VMEM)
    shapes = [jax.ShapeDtypeStruct(a.shape, F32) for a in ws]
    outs = pl.pallas_call(
        body, name="adamw_small", in_specs=[vmem] * (4 * n), out_specs=[vmem] * (3 * n), out_shape=shapes * 3,
        compiler_params=pltpu.CompilerParams(vmem_limit_bytes=VMEM_LIMIT_BYTES),
    )(*ws, *gs, *ms, *vs)
    return outs[:n], outs[n:2 * n], outs[2 * n:]


def _elementwise_tile(rows, cols):
    for t in range(256, 15, -16):
        if rows % t == 0:
            return (t, cols), rows // t, lambda i: (i, 0)
    assert cols % 256 == 0
    return (rows, 256), cols // 256, lambda i: (0, i)


def _partial_sum(own, land, me_arr, *, name):
    r, c = own.shape[-2:]
    tile, steps, imap = _elementwise_tile(r, c)
    whole = own.ndim == 3

    def body(me_ref, own_ref, land_ref, o_ref):
        acc = own_ref[0] if whole else own_ref[...]
        for j in range(3):
            acc = acc + land_ref[j].astype(F32)
        o_ref[...] = acc

    own_spec = (pl.BlockSpec((1,) + tile, lambda i, me: (me[0],) + imap(i)) if whole
                else pl.BlockSpec(tile, lambda i, me: imap(i)))
    return pl.pallas_call(
        body, name=name,
        grid_spec=pltpu.PrefetchScalarGridSpec(
            num_scalar_prefetch=1, grid=(steps,),
            in_specs=[own_spec, pl.BlockSpec((3,) + tile, lambda i, me: (0,) + imap(i))],
            out_specs=pl.BlockSpec(tile, lambda i, me: imap(i))),
        out_shape=jax.ShapeDtypeStruct((r, c), F32),
        compiler_params=_cparams("parallel"),
    )(me_arr, own, land)


def _adamw_math(w, g, m, v):
    m = ADAM_B1 * m + (1.0 - ADAM_B1) * g
    v = ADAM_B2 * v + (1.0 - ADAM_B2) * (g * g)
    m_hat = m / (1.0 - ADAM_B1 ** ADAM_STEP)
    v_hat = v / (1.0 - ADAM_B2 ** ADAM_STEP)
    delta = -ADAM_LR * (m_hat / (jnp.sqrt(v_hat) + ADAM_EPS) + ADAM_WD * w)
    return delta, m, v


def _adamw(w, grad_parts, m, v, *, name):
    r, c = w.shape
    tile_shape, steps, imap = _elementwise_tile(r, c)
    n = len(grad_parts)

    def body(*refs):
        w_ref, m_ref, v_ref = refs[:3]
        g_refs = refs[3:3 + n]
        g_out, d_out, m_out, v_out = refs[3 + n:]
        g = g_refs[0][...]
        for k in range(1, n):
            g = g + g_refs[k][...]
        delta, m_new, v_new = _adamw_math(w_ref[...], g, m_ref[...], v_ref[...])
        g_out[...] = g
        d_out[...] = delta
        m_out[...] = m_new
        v_out[...] = v_new

    tile = pl.BlockSpec(tile_shape, imap)
    return pl.pallas_call(
        body, name=name, grid=(steps,), in_specs=[tile] * (3 + n), out_specs=[tile] * 4,
        out_shape=[jax.ShapeDtypeStruct((r, c), F32)] * 4,
        compiler_params=_cparams("parallel"),
    )(w, m, v, *grad_parts)


WEIGHT_NAMES = ("meta_tokens", "norm1_w", "w_in", "ssd_conv_w", "ssd_conv_b", "ssd_dt_bias", "ssd_a_log", "ssd_d",
                "ssd_norm_w", "lru_conv_w", "lru_conv_b", "lru_wa", "lru_ba", "lru_wx", "lru_bx", "lru_lambda",
                "lru_norm_w", "w_out", "norm2_w", "w_gate", "w_up", "w_down", "final_norm_w")
BIG = ("w_in", "w_out", "w_gate", "w_up", "w_down")
FFN = ("w_gate", "w_up", "w_down")
LATE = ("w_out",) + FFN
SMALL_SHARDED = {"meta_tokens": (N_META, D_MODEL), "ssd_conv_w": (CONV_K, 1536), "lru_conv_w": (CONV_K, LRU_WIDTH)}
SMALL = tuple(n for n in WEIGHT_NAMES if n not in BIG)
PACK_COLS = 1024


def _pack(arrays, row_multiple):
    flat = jnp.concatenate([a.reshape(-1) for a in arrays])
    rows = -(-flat.shape[0] // (row_multiple * PACK_COLS)) * row_multiple
    return jnp.pad(flat, (0, rows * PACK_COLS - flat.shape[0])).reshape(rows, PACK_COLS)


def _unpack(pack, shapes):
    flat = pack.reshape(-1)
    out, off = [], 0
    for s in shapes:
        size = math.prod(s)
        out.append(flat[off:off + size].reshape(s))
        off += size
    return out


def _unshard_cols(g4):
    return jnp.swapaxes(g4, 0, 1).reshape(g4.shape[1], -1)


COL_SHARDED = ("w_in", "w_gate", "w_up")
IN_ROWS = {"z": (0, 1024), "xs": (1024, 2048), "bc": (2048, 2560), "dt": (2560, 2576), "g": (2576, 3600),
           "x": (3600, IN_COLS)}


def _rows_of_shards(shards4, lo, hi):
    r = shards4.shape[1]
    parts = [shards4[k, max(lo, k * r) - k * r:min(hi, (k + 1) * r) - k * r]
             for k in range(N_SHARDS) if max(lo, k * r) < min(hi, (k + 1) * r)]
    return parts[0] if len(parts) == 1 else jnp.concatenate(parts, axis=0)


def _w_in_shard_rows(k, sections):
    lo, hi = k * (IN_COLS // N_SHARDS), (k + 1) * (IN_COLS // N_SHARDS)
    parts = []
    for arr, (a, b) in zip(sections, IN_ROWS.values()):
        if max(lo, a) < min(hi, b):
            parts.append(arr[max(lo, a) - a:min(hi, b) - a])
    return jnp.concatenate(parts, axis=0)


def _rows_view(name, block):
    return jnp.swapaxes(block[0], 0, 1) if name in COL_SHARDED else block[0]


def _param_view(name, rows):
    return (jnp.swapaxes(rows, 0, 1) if name in COL_SHARDED else rows)[None]


def kernel(x, meta_tokens, norm1_w, w_in, ssd_conv_w, ssd_conv_b, ssd_dt_bias, ssd_a_log, ssd_d, ssd_norm_w, lru_conv_w, lru_conv_b, lru_wa, lru_ba, lru_wx, lru_bx, lru_lambda, lru_norm_w, w_out, norm2_w, w_gate, w_up, w_down, final_norm_w, loss_target, m_meta_tokens, m_norm1_w, m_w_in, m_ssd_conv_w, m_ssd_conv_b, m_ssd_dt_bias, m_ssd_a_log, m_ssd_d, m_ssd_norm_w, m_lru_conv_w, m_lru_conv_b, m_lru_wa, m_lru_ba, m_lru_wx, m_lru_bx, m_lru_lambda, m_lru_norm_w, m_w_out, m_norm2_w, m_w_gate, m_w_up, m_w_down, m_final_norm_w, v_meta_tokens, v_norm1_w, v_w_in, v_ssd_conv_w, v_ssd_conv_b, v_ssd_dt_bias, v_ssd_a_log, v_ssd_d, v_ssd_norm_w, v_lru_conv_w, v_lru_conv_b, v_lru_wa, v_lru_ba, v_lru_wx, v_lru_bx, v_lru_lambda, v_lru_norm_w, v_w_out, v_norm2_w, v_w_gate, v_w_up, v_w_down, v_final_norm_w):
    w = dict(zip(WEIGHT_NAMES, (meta_tokens, norm1_w, w_in, ssd_conv_w, ssd_conv_b, ssd_dt_bias, ssd_a_log, ssd_d, ssd_norm_w, lru_conv_w, lru_conv_b, lru_wa, lru_ba, lru_wx, lru_bx, lru_lambda, lru_norm_w, w_out, norm2_w, w_gate, w_up, w_down, final_norm_w)))
    m = dict(zip(WEIGHT_NAMES, (m_meta_tokens, m_norm1_w, m_w_in, m_ssd_conv_w, m_ssd_conv_b, m_ssd_dt_bias, m_ssd_a_log, m_ssd_d, m_ssd_norm_w, m_lru_conv_w, m_lru_conv_b, m_lru_wa, m_lru_ba, m_lru_wx, m_lru_bx, m_lru_lambda, m_lru_norm_w, m_w_out, m_norm2_w, m_w_gate, m_w_up, m_w_down, m_final_norm_w)))
    v = dict(zip(WEIGHT_NAMES, (v_meta_tokens, v_norm1_w, v_w_in, v_ssd_conv_w, v_ssd_conv_b, v_ssd_dt_bias, v_ssd_a_log, v_ssd_d, v_ssd_norm_w, v_lru_conv_w, v_lru_conv_b, v_lru_wa, v_lru_ba, v_lru_wx, v_lru_bx, v_lru_lambda, v_lru_norm_w, v_w_out, v_norm2_w, v_w_gate, v_w_up, v_w_down, v_final_norm_w)))
    me = 2 * lax.axis_index("x") + lax.axis_index("y")

    big2d = {n: _rows_view(n, w[n]) for n in BIG}
    small_local = jnp.concatenate([w["meta_tokens"].reshape(-1), w["ssd_conv_w"].reshape(-1),
                                   w["lru_conv_w"].reshape(-1)])[None]
    me_arr = me.astype(jnp.int32).reshape(1)
    dev_arr = (2 * me + lax.axis_index("c")).astype(jnp.int32).reshape(1)
    small4 = _gather_small(small_local)
    (w_in_slot,) = _fill_own_slots([big2d["w_in"]], me_arr, name="own_slot_w_in")
    in_send, in_recv, in_bufs, in_tok = _split_start([w_in_slot], _halves_plan, 3, small4, name="gather_w_in_start")
    late_slots = _fill_own_slots([big2d[n] for n in LATE], me_arr, name="own_slots_late", behind=(in_tok,))
    sm = small4[:, 0]
    meta_full = _unshard_cols(sm[:, :4096].reshape(N_SHARDS, N_META, 256))
    ssd_conv_w_full = _unshard_cols(sm[:, 4096:5632].reshape(N_SHARDS, CONV_K, 384))
    lru_conv_w_full = _unshard_cols(sm[:, 5632:].reshape(N_SHARDS, CONV_K, 256))

    p = {"ssd_conv_w": ssd_conv_w_full, "lru_conv_w": lru_conv_w_full,
         "lru_wa": w["lru_wa"][0], "lru_wx": w["lru_wx"][0], "final_norm_w": w["final_norm_w"][None]}
    for n in ("norm1_w", "ssd_conv_b", "ssd_dt_bias", "ssd_a_log", "ssd_d", "ssd_norm_w", "lru_conv_b", "lru_ba",
              "lru_bx", "lru_lambda", "lru_norm_w", "norm2_w"):
        p[n] = w[n]

    class Late:
        def __init__(self):
            self.pending = []
            self.before_embed = (late_slots[0],)

        def w_in(self, after):
            (buf,) = _split_wait(in_bufs, in_send, in_recv, _halves_plan, after, name="gather_w_in_wait")
            send, recv, bufs, tok = _split_start([buf], _forward_plan, 3, None, name="forward_w_in_start")
            self.late_gather = _split_start(late_slots, _halves_plan, 3 * len(LATE), tok, name="gather_late_start")
            (w_in4,) = _split_wait(bufs, send, recv, _forward_plan, self.late_gather[2][0], name="forward_w_in_wait")
            sections = {s: _rows_of_shards(w_in4, lo, hi) for s, (lo, hi) in IN_ROWS.items()}
            sections["dt"] = jnp.pad(sections["dt"], ((0, SEC_WIDTH["dt"] - SSD_HEADS), (0, 0)))
            return sections

        def mid_forward(self, after):
            send, recv, bufs, _ = self.late_gather
            bufs = _split_wait(bufs, send, recv, _halves_plan, after, name="gather_late_wait")
            self.forward = _split_start(bufs, _forward_plan, 3 * len(LATE), None, name="forward_late_start")
            return self.forward[3][:1, :1]

        def w_out(self, after):
            send, recv, bufs, _ = self.forward
            bufs = _split_wait(bufs, send, recv, _forward_plan, after, name="forward_late_wait")
            self.late = dict(zip(LATE, (b.reshape(-1, D_MODEL) for b in bufs)))
            return self.late["w_out"]

        def ffn(self, after):
            return tuple(self.late[n] for n in FFN)

        def grads_ready(self, names, g, g_mxu):
            if names == ("w_in",):
                g_mxu["w_in"] = jnp.stack([_w_in_shard_rows(k, [g_mxu["w_in_" + s] for s in SEC_NAMES])
                                           for k in range(N_SHARDS)])
            srcs = [g_mxu[n].reshape(N_SHARDS, -1, D_MODEL) for n in names]
            lands = [lax.empty((3,) + s.shape[1:], _MXU) for s in srcs]
            tag = "_".join(names)
            send, recv, bufs, tok = _split_start(srcs + lands, _scatter_plan, 3 * len(names), None,
                                                 name="scatter_" + tag + "_start")
            self.pending.append((names, send, recv, bufs, tag))
            self.in_flight = bufs[0]
            return tok[:1, :1]

        def landed(self, after, which):
            land = {}
            for names, send, recv, bufs, tag in self.pending:
                if names[0] in which:
                    bufs = _split_wait(bufs, send, recv, _scatter_plan, after, name="scatter_" + tag + "_wait")
                    land.update(zip(names, bufs[len(names):]))
            return land

        def small_ready(self, g, loss):
            pack = _pack([g[n] for n in SMALL] + [loss[0, :1]], 8 * N_DEV)
            pack = pack.reshape(N_DEV, -1, PACK_COLS)
            self.small = _split_start([pack, lax.empty(pack.shape, F32)], _pieces_plan, N_DEV - 1, loss,
                                      name="small_pieces_start")
            return self.small[3]

        def small_middle(self, after):
            send, recv, bufs, _ = self.small
            pack, land = _split_wait(bufs, send, recv, _pieces_plan, after, name="small_pieces_wait")
            piece = _sum_pieces(pack, land, dev_arr, name="small_pieces_sum")
            self.small = _split_start([piece, lax.empty(pack.shape, F32)], _spread_plan, N_DEV - 1, None,
                                      name="small_spread_start")
            return self.small[3]

        def small_sum(self, after):
            send, recv, bufs, _ = self.small
            piece, land = _split_wait(bufs, send, recv, _spread_plan, after, name="small_spread_wait")
            return _join_pieces(piece, land, dev_arr, name="small_join")

    late = Late()

    loss, grad_x, g, g_mxu = _local_step(x[0], loss_target[0], meta_full, p, late)

    g4 = {n: g[n].reshape(N_SHARDS, -1, D_MODEL) for n in LATE}
    g4["w_in"] = lax.switch(me, [functools.partial(_w_in_shard_rows, k) for k in range(N_SHARDS)],
                            [g["w_in_" + s] for s in SEC_NAMES])
    land = late.landed(late.in_flight, LATE)
    part = {n: _partial_sum(g4[n], land[n], me_arr, name="partial_" + n) for n in LATE}
    late.small_middle(part["w_down"])
    sib = dict(zip(LATE, _swap_with_sibling([part[n] for n in LATE], name="swap_late")))

    small_full_shape = {n: (SMALL_SHARDED[n] if n in SMALL_SHARDED else w[n].shape) for n in SMALL}
    red_list = _unpack(late.small_sum(sib["w_out"]), [small_full_shape[n] for n in SMALL] + [(1,)])
    loss_total = red_list[-1][0]
    g_small = {}
    for n, arr in zip(SMALL, red_list[:-1]):
        if n in SMALL_SHARDED:
            cols = SMALL_SHARDED[n][1] // N_SHARDS
            arr = lax.dynamic_slice_in_dim(arr, me * cols, cols, axis=1)
        g_small[n] = arr.reshape(w[n].shape)

    grad, delta, new_m, new_v = {}, {}, {}, {}

    def update_big(n):
        outs = _adamw(big2d[n], [part[n], sib[n]], _rows_view(n, m[n]), _rows_view(n, v[n]), name="adamw_" + n)
        grad[n], delta[n], new_m[n], new_v[n] = (_param_view(n, o) for o in outs)
        return outs[0]

    two_d = lambda a: a.reshape(1, -1) if a.ndim == 1 else a
    deltas, new_ms, new_vs = _adamw_native(*[[two_d(d[n]) for n in SMALL] for d in (w, g_small, m, v)])
    for n, dn, mn, vn in zip(SMALL, deltas, new_ms, new_vs):
        grad[n], delta[n], new_m[n], new_v[n] = (g_small[n], dn.reshape(w[n].shape), mn.reshape(w[n].shape),
                                                 vn.reshape(w[n].shape))
    land.update(late.landed([update_big(n) for n in LATE] + [deltas[0]], ("w_in",)))
    part["w_in"] = _partial_sum(g4["w_in"], land["w_in"], me_arr, name="partial_w_in")
    (sib["w_in"],) = _swap_with_sibling([part["w_in"]], name="swap_w_in")
    update_big("w_in")

    return (loss_total, grad_x[None], *[grad[n] for n in WEIGHT_NAMES], *[delta[n] for n in WEIGHT_NAMES],
            *[new_m[n] for n in WEIGHT_NAMES], *[new_v[n] for n in WEIGHT_NAMES])
```

```python
import functools
import math

import jax
import jax.numpy as jnp
from jax import lax
from jax.experimental import pallas as pl
from jax.experimental.pallas import tpu as pltpu

F32 = jnp.float32
_MXU = jnp.bfloat16

D_MODEL = 1024
SEQ = 2048
N_META = 16
CHUNK = 128
T_ROWS = 2176
N_CHUNKS = T_ROWS // CHUNK
PAD_ROWS = T_ROWS - SEQ - N_META
X_ROW0 = PAD_ROWS + N_META
SSD_HEADS = 16
SSD_HEAD_DIM = 64
SSD_STATE = 128
SSD_GROUPS = 2
SSD_HPG = SSD_HEADS // SSD_GROUPS
SSD_WIDTH = 1024
LRU_WIDTH = 1024
LRU_C = 8.0
D_FF = 2816
EPS = 1e-6
IN_COLS = 4624
N_SHARDS = 4
N_DEV = 8

ADAM_LR = 0.001
ADAM_B1 = 0.9
ADAM_B2 = 0.999
ADAM_EPS = 1e-08
ADAM_WD = 0.01
ADAM_STEP = 10

VMEM_LIMIT_BYTES = 56 * 1024 * 1024

NN = (((1,), (0,)), ((), ()))
NT = (((1,), (1,)), ((), ()))
TN = (((0,), (0,)), ((), ()))


def _cparams(*sem):
    return pltpu.CompilerParams(dimension_semantics=sem, vmem_limit_bytes=VMEM_LIMIT_BYTES)


def _dot(a, b, dims=NN):
    return lax.dot_general(a.astype(_MXU), b.astype(_MXU), dims, preferred_element_type=F32)


def _dot_onehot(a, b, dims=NN, *, data=0, pieces=3):
    ops = [a, b]
    mask = ops[1 - data].astype(jnp.bfloat16)
    rest = ops[data]
    acc = None
    for _ in range(pieces):
        piece = rest.astype(jnp.bfloat16)
        ops[data], ops[1 - data] = piece, mask
        d = lax.dot_general(ops[0], ops[1], dims, preferred_element_type=F32)
        acc = d if acc is None else acc + d
        rest = rest - piece.astype(F32)
    return acc


def _sigmoid(x):
    return 0.5 * (1.0 + jnp.tanh(0.5 * x))


def _softplus(x):
    return jnp.maximum(x, 0.0) + jnp.log(1.0 + jnp.exp(-jnp.abs(x)))


def _silu(x):
    return x * _sigmoid(x)


def _silu_grad(x):
    s = _sigmoid(x)
    return s * (1.0 + x * (1.0 - s))


_GELU_C = math.sqrt(2.0 / math.pi)


def _gelu_and_grad(x):
    inner = _GELU_C * (x + 0.044715 * x * x * x)
    t = jnp.tanh(inner)
    g = 0.5 * x * (1.0 + t)
    dg = 0.5 * (1.0 + t) + 0.5 * x * (1.0 - t * t) * _GELU_C * (1.0 + 3.0 * 0.044715 * x * x)
    return g, dg


def _rms_fwd(x, w):
    rstd = lax.rsqrt(jnp.mean(x * x, axis=-1, keepdims=True) + EPS)
    return x * rstd * w


def _rms_bwd(x, w, dy):
    rstd = lax.rsqrt(jnp.mean(x * x, axis=-1, keepdims=True) + EPS)
    xhat = x * rstd
    dxhat = dy * w
    dx = rstd * (dxhat - xhat * jnp.mean(dxhat * xhat, axis=-1, keepdims=True))
    return dx, dy * xhat


def _mm(terms, m, n, *, tm, tn, mode, out_dtype, name, residual=None, n_outer=False, also_mxu=False, behind=()):
    gm, gn = m // tm, n // tn
    assert gm * tm == m and gn * tn == n
    if n_outer:
        grid = (gn, gm)
        mi = lambda g0, g1: g1
        ni = lambda g0, g1: g0
    else:
        grid = (gm, gn)
        mi = lambda g0, g1: g0
        ni = lambda g0, g1: g1
    in_specs, args = [], []
    for (a, ka, b, kb, k) in terms:
        if mode == "tn":
            in_specs.append(pl.BlockSpec((k, tm), lambda g0, g1, ka=ka: (ka, mi(g0, g1))))
        else:
            in_specs.append(pl.BlockSpec((tm, k), lambda g0, g1, ka=ka: (mi(g0, g1), ka)))
        if mode == "nt":
            in_specs.append(pl.BlockSpec((tn, k), lambda g0, g1, kb=kb: (ni(g0, g1), kb)))
        else:
            in_specs.append(pl.BlockSpec((k, tn), lambda g0, g1, kb=kb: (kb, ni(g0, g1))))
        args += [a, b]
    if residual is not None:
        in_specs.append(pl.BlockSpec((tm, tn), lambda g0, g1: (mi(g0, g1), ni(g0, g1))))
        args.append(residual)
    dims = {"nn": NN, "nt": NT, "tn": TN}[mode]
    n_terms = len(terms)
    has_res = residual is not None
    in_specs += [pl.BlockSpec(memory_space=pl.ANY)] * len(behind)
    args += list(behind)
    n_in = len(args)

    def body(*refs):
        acc = None
        for t in range(n_terms):
            d = lax.dot_general(refs[2 * t][...], refs[2 * t + 1][...], dims, preferred_element_type=F32)
            acc = d if acc is None else acc + d
        if has_res:
            acc = acc + refs[2 * n_terms][...]
        refs[n_in][...] = acc.astype(out_dtype)
        if also_mxu:
            refs[n_in + 1][...] = acc.astype(_MXU)

    tile = pl.BlockSpec((tm, tn), lambda g0, g1: (mi(g0, g1), ni(g0, g1)))
    shape = jax.ShapeDtypeStruct((m, n), out_dtype)
    return pl.pallas_call(
        body, name=name, grid=grid, in_specs=in_specs,
        out_specs=[tile, tile] if also_mxu else tile,
        out_shape=[shape, jax.ShapeDtypeStruct((m, n), _MXU)] if also_mxu else shape,
        compiler_params=_cparams("parallel", "parallel"),
    )(*args)


def _embed(x, meta, behind=()):
    def body(x_ref, meta_ref, *rest):
        o_ref = rest[-1]
        i = pl.program_id(0)

        @pl.when(i == 0)
        def _():
            o_ref[0:PAD_ROWS, :] = jnp.zeros((PAD_ROWS, D_MODEL), F32)
            o_ref[PAD_ROWS:CHUNK, :] = meta_ref[...]

        @pl.when(i > 0)
        def _():
            o_ref[...] = x_ref[...]

    return pl.pallas_call(
        body, name="embed", grid=(N_CHUNKS,),
        in_specs=[pl.BlockSpec((CHUNK, D_MODEL), lambda i: (jnp.maximum(i - 1, 0), 0)),
                  pl.BlockSpec((N_META, D_MODEL), lambda i: (0, 0))] + [pl.BlockSpec(memory_space=pl.ANY)] * len(behind),
        out_specs=pl.BlockSpec((CHUNK, D_MODEL), lambda i: (i, 0)),
        out_shape=jax.ShapeDtypeStruct((T_ROWS, D_MODEL), F32),
        compiler_params=_cparams("parallel"),
    )(x, meta, *behind)


def _rmsnorm(h, w, *, name, tm=544):
    def body(h_ref, w_ref, o_ref):
        o_ref[...] = _rms_fwd(h_ref[...], w_ref[...]).astype(_MXU)

    return pl.pallas_call(
        body, name=name, grid=(T_ROWS // tm,),
        in_specs=[pl.BlockSpec((tm, D_MODEL), lambda i: (i, 0)), pl.BlockSpec((1, D_MODEL), lambda i: (0, 0))],
        out_specs=pl.BlockSpec((tm, D_MODEL), lambda i: (i, 0)),
        out_shape=jax.ShapeDtypeStruct((T_ROWS, D_MODEL), _MXU),
        compiler_params=_cparams("parallel"),
    )(h, w)


def _norm_proj(h, w, sections, *, name, tm=544):
    widths = [s.shape[0] for s in sections]
    n = len(sections)

    def body(*refs):
        h_ref, w_ref = refs[:2]
        u_ref = refs[2 + n]
        u = _rms_fwd(h_ref[...], w_ref[...]).astype(_MXU)
        u_ref[...] = u
        for k in range(n):
            refs[3 + n + k][...] = lax.dot_general(u, refs[2 + k][...], NT, preferred_element_type=F32)

    row = lambda width: pl.BlockSpec((tm, width), lambda i: (i, 0))
    outs = pl.pallas_call(
        body, name=name, grid=(T_ROWS // tm,),
        in_specs=[row(D_MODEL), pl.BlockSpec((1, D_MODEL), lambda i: (0, 0))]
        + [pl.BlockSpec((wd, D_MODEL), lambda i: (0, 0)) for wd in widths],
        out_specs=[row(D_MODEL)] + [row(wd) for wd in widths],
        out_shape=[jax.ShapeDtypeStruct((T_ROWS, D_MODEL), _MXU)]
        + [jax.ShapeDtypeStruct((T_ROWS, wd), F32) for wd in widths],
        compiler_params=_cparams("parallel"),
    )(h, w, *sections)
    return outs[0], list(outs[1:])


def _loss_head(h2, target, fw):
    def body(h_ref, t_ref, w_ref, loss_ref, dh_ref, dhb_ref, dw_ref, acc_ref):
        i = pl.program_id(0)

        @pl.when(i == 0)
        def _():
            acc_ref[...] = jnp.zeros_like(acc_ref)
            dw_ref[...] = jnp.zeros_like(dw_ref)

        h = h_ref[...]
        w = w_ref[...]
        y = _rms_fwd(h, w)
        live = (i > 0).astype(F32)
        err = (y - t_ref[...]) * live
        acc_ref[...] += jnp.sum(err * err, axis=0, keepdims=True)
        dy = err * (1.0 / D_MODEL)
        dx, dwr = _rms_bwd(h, w, dy)
        dh_ref[...] = dx
        dhb_ref[...] = dx.astype(_MXU)
        dw_ref[...] += jnp.sum(dwr, axis=0, keepdims=True)

        @pl.when(i == N_CHUNKS - 1)
        def _():
            tot = jnp.sum(acc_ref[...], axis=1, keepdims=True) * (0.5 / D_MODEL)
            loss_ref[...] = jnp.broadcast_to(tot, (1, 128))

    return pl.pallas_call(
        body, name="loss_head", grid=(N_CHUNKS,),
        in_specs=[pl.BlockSpec((CHUNK, D_MODEL), lambda i: (i, 0)),
                  pl.BlockSpec((CHUNK, D_MODEL), lambda i: (jnp.maximum(i - 1, 0), 0)),
                  pl.BlockSpec((1, D_MODEL), lambda i: (0, 0))],
        out_specs=[pl.BlockSpec((1, 128), lambda i: (0, 0)),
                   pl.BlockSpec((CHUNK, D_MODEL), lambda i: (i, 0)),
                   pl.BlockSpec((CHUNK, D_MODEL), lambda i: (i, 0)),
                   pl.BlockSpec((1, D_MODEL), lambda i: (0, 0))],
        out_shape=[jax.ShapeDtypeStruct((1, 128), F32),
                   jax.ShapeDtypeStruct((T_ROWS, D_MODEL), F32),
                   jax.ShapeDtypeStruct((T_ROWS, D_MODEL), _MXU),
                   jax.ShapeDtypeStruct((1, D_MODEL), F32)],
        scratch_shapes=[pltpu.VMEM((1, D_MODEL), F32)],
        compiler_params=_cparams("arbitrary"),
    )(h2, target, fw)


def _mm_norm_bwd(terms, h, w, dres, *, name, tm=544, behind=()):
    n_terms = len(terms)
    in_specs, args = [], []
    for (a, b, k) in terms:
        in_specs += [pl.BlockSpec((tm, k), lambda i: (i, 0)), pl.BlockSpec((k, D_MODEL), lambda i: (0, 0))]
        args += [a, b]
    in_specs += [pl.BlockSpec((tm, D_MODEL), lambda i: (i, 0)), pl.BlockSpec((1, D_MODEL), lambda i: (0, 0)),
                 pl.BlockSpec((tm, D_MODEL), lambda i: (i, 0))] + [pl.BlockSpec(memory_space=pl.ANY)] * len(behind)
    args += [h, w, dres, *behind]

    def body(*refs):
        h_ref, w_ref, dres_ref = refs[2 * n_terms:2 * n_terms + 3]
        dh_ref, dhb_ref, dw_ref = refs[2 * n_terms + 3 + len(behind):]

        @pl.when(pl.program_id(0) == 0)
        def _():
            dw_ref[...] = jnp.zeros_like(dw_ref)

        du = None
        for t in range(n_terms):
            d = lax.dot_general(refs[2 * t][...], refs[2 * t + 1][...], NN, preferred_element_type=F32)
            du = d if du is None else du + d
        dx, dwr = _rms_bwd(h_ref[...], w_ref[...], du)
        dh = dres_ref[...] + dx
        dh_ref[...] = dh
        dhb_ref[...] = dh.astype(_MXU)
        dw_ref[...] += jnp.sum(dwr, axis=0, keepdims=True)

    return pl.pallas_call(
        body, name=name, grid=(T_ROWS // tm,), in_specs=in_specs,
        out_specs=[pl.BlockSpec((tm, D_MODEL), lambda i: (i, 0)), pl.BlockSpec((tm, D_MODEL), lambda i: (i, 0)),
                   pl.BlockSpec((1, D_MODEL), lambda i: (0, 0))],
        out_shape=[jax.ShapeDtypeStruct((T_ROWS, D_MODEL), F32), jax.ShapeDtypeStruct((T_ROWS, D_MODEL), _MXU),
                   jax.ShapeDtypeStruct((1, D_MODEL), F32)],
        compiler_params=_cparams("arbitrary"),
    )(*args)


FFN_TM = T_ROWS
FFN_TN = 256


def _ffn_up(u2, wg_t, wu_t):
    def body(u_ref, wg_ref, wu_ref, gp_ref, up_ref, act_ref):
        u = u_ref[...]
        gp = lax.dot_general(u, wg_ref[...], NT, preferred_element_type=F32)
        up = lax.dot_general(u, wu_ref[...], NT, preferred_element_type=F32)
        gp_ref[...] = gp
        up_ref[...] = up
        act_ref[...] = (_silu(gp) * up).astype(_MXU)

    tile = pl.BlockSpec((FFN_TM, FFN_TN), lambda j, i: (i, j))
    return pl.pallas_call(
        body, name="ffn_up", grid=(D_FF // FFN_TN, T_ROWS // FFN_TM),
        in_specs=[pl.BlockSpec((FFN_TM, D_MODEL), lambda j, i: (i, 0)),
                  pl.BlockSpec((FFN_TN, D_MODEL), lambda j, i: (j, 0)),
                  pl.BlockSpec((FFN_TN, D_MODEL), lambda j, i: (j, 0))],
        out_specs=[tile, tile, tile],
        out_shape=[jax.ShapeDtypeStruct((T_ROWS, D_FF), F32), jax.ShapeDtypeStruct((T_ROWS, D_FF), F32),
                   jax.ShapeDtypeStruct((T_ROWS, D_FF), _MXU)],
        compiler_params=_cparams("parallel", "parallel"),
    )(u2, wg_t, wu_t)


def _ffn_bwd_act(dh2b, wd, gp, up):
    def body(dh_ref, wd_ref, gp_ref, up_ref, dgp_ref, dup_ref):
        dact = lax.dot_general(dh_ref[...], wd_ref[...], NT, preferred_element_type=F32)
        gp = gp_ref[...]
        dgp_ref[...] = (dact * up_ref[...] * _silu_grad(gp)).astype(_MXU)
        dup_ref[...] = (dact * _silu(gp)).astype(_MXU)

    tile = pl.BlockSpec((FFN_TM, FFN_TN), lambda j, i: (i, j))
    return pl.pallas_call(
        body, name="ffn_bwd_act", grid=(D_FF // FFN_TN, T_ROWS // FFN_TM),
        in_specs=[pl.BlockSpec((FFN_TM, D_MODEL), lambda j, i: (i, 0)),
                  pl.BlockSpec((FFN_TN, D_MODEL), lambda j, i: (j, 0)), tile, tile],
        out_specs=[tile, tile],
        out_shape=[jax.ShapeDtypeStruct((T_ROWS, D_FF), _MXU), jax.ShapeDtypeStruct((T_ROWS, D_FF), _MXU)],
        compiler_params=_cparams("parallel", "parallel"),
    )(dh2b, wd, gp, up)


CONV_TC = 512
CONV_K = 4


def _conv_pre(x_ref, wv, bv, c):
    tc = wv.shape[1]
    r0 = c * CHUNK
    cur = x_ref[r0:r0 + CHUNK, :]
    if c == 0:
        cat = jnp.concatenate([jnp.zeros((8, tc), F32), cur], axis=0)
        shifted = [cur] + [pltpu.roll(cat, s, 0)[8:8 + CHUNK] for s in range(1, CONV_K)]
    else:
        shifted = [cur] + [x_ref[r0 - s:r0 - s + CHUNK, :] for s in range(1, CONV_K)]
    pre = bv
    for s in range(CONV_K):
        pre = pre + shifted[s] * wv[CONV_K - 1 - s:CONV_K - s]
    return pre, shifted


def _row_mask(c):
    if c > 0:
        return None
    return (lax.broadcasted_iota(jnp.int32, (CHUNK, 1), 0) >= PAD_ROWS).astype(F32)


def _conv_fwd(x, w, b, *, silu, name):
    cols = x.shape[1]
    tc = min(CONV_TC, cols)

    def body(x_ref, w_ref, b_ref, o_ref):
        wv, bv = w_ref[...], b_ref[...]
        for c in range(N_CHUNKS):
            pre, _ = _conv_pre(x_ref, wv, bv, c)
            y = _silu(pre) if silu else pre
            mask = _row_mask(c)
            if mask is not None:
                y = y * mask
            o_ref[c * CHUNK:(c + 1) * CHUNK, :] = y

    return pl.pallas_call(
        body, name=name, grid=(cols // tc,),
        in_specs=[pl.BlockSpec((T_ROWS, tc), lambda j: (0, j)), pl.BlockSpec((CONV_K, tc), lambda j: (0, j)),
                  pl.BlockSpec((1, tc), lambda j: (0, j))],
        out_specs=pl.BlockSpec((T_ROWS, tc), lambda j: (0, j)),
        out_shape=jax.ShapeDtypeStruct((T_ROWS, cols), F32),
        compiler_params=_cparams("parallel"),
    )(x, w, b)


def _conv_bwd(dy, x, w, b, *, silu, name):
    cols = x.shape[1]
    tc = min(CONV_TC, cols)

    def body(dy_ref, x_ref, w_ref, b_ref, dx_ref, dw_ref, db_ref):
        wv, bv = w_ref[...], b_ref[...]
        next8 = jnp.zeros((8, tc), F32)
        dws = [jnp.zeros((1, tc), F32) for _ in range(CONV_K)]
        db = jnp.zeros((1, tc), F32)
        for c in reversed(range(N_CHUNKS)):
            r0 = c * CHUNK
            pre, shifted = _conv_pre(x_ref, wv, bv, c)
            dpre = dy_ref[r0:r0 + CHUNK, :]
            if silu:
                dpre = dpre * _silu_grad(pre)
            mask = _row_mask(c)
            if mask is not None:
                dpre = dpre * mask
            cat = jnp.concatenate([dpre, next8], axis=0)
            dx = dpre * wv[CONV_K - 1:CONV_K]
            for s in range(1, CONV_K):
                dx = dx + pltpu.roll(cat, CHUNK + 8 - s, 0)[0:CHUNK] * wv[CONV_K - 1 - s:CONV_K - s]
            dx_ref[r0:r0 + CHUNK, :] = dx.astype(_MXU)
            for s in range(CONV_K):
                k = CONV_K - 1 - s
                dws[k] = dws[k] + jnp.sum(dpre * shifted[s], axis=0, keepdims=True)
            db = db + jnp.sum(dpre, axis=0, keepdims=True)
            next8 = dpre[0:8]
        dw_ref[...] = jnp.concatenate(dws, axis=0)
        db_ref[...] = db

    return pl.pallas_call(
        body, name=name, grid=(cols // tc,),
        in_specs=[pl.BlockSpec((T_ROWS, tc), lambda j: (0, j)), pl.BlockSpec((T_ROWS, tc), lambda j: (0, j)),
                  pl.BlockSpec((CONV_K, tc), lambda j: (0, j)), pl.BlockSpec((1, tc), lambda j: (0, j))],
        out_specs=[pl.BlockSpec((T_ROWS, tc), lambda j: (0, j)), pl.BlockSpec((CONV_K, tc), lambda j: (0, j)),
                   pl.BlockSpec((1, tc), lambda j: (0, j))],
        out_shape=[jax.ShapeDtypeStruct((T_ROWS, cols), _MXU), jax.ShapeDtypeStruct((CONV_K, cols), F32),
                   jax.ShapeDtypeStruct((1, cols), F32)],
        compiler_params=_cparams("parallel"),
    )(dy, x, w, b)


def _ssd_chunk_common(dt_raw, prm, c):
    a_row = -jnp.exp(prm[1:2])
    dt = _softplus(dt_raw + prm[0:1])
    rows = lax.broadcasted_iota(jnp.int32, (CHUNK, 1), 0)
    real = jnp.logical_or(c > 0, rows >= PAD_ROWS)
    dt = jnp.where(real, dt, 0.0)
    li = lax.broadcasted_iota(jnp.int32, (CHUNK, CHUNK), 0)
    si = lax.broadcasted_iota(jnp.int32, (CHUNK, CHUNK), 1)
    causal = li >= si
    tri = causal.astype(F32)
    cs = _dot_onehot(tri, dt * a_row, data=1)
    return dt, a_row, cs, cs.T, causal, tri, real


def _gated_norm_fwd(y, z, w):
    g = y * _silu(z)
    half = SSD_WIDTH // SSD_GROUPS
    outs = [_rms_fwd(g[:, k * half:(k + 1) * half], w[:, k * half:(k + 1) * half]) for k in range(SSD_GROUPS)]
    return jnp.concatenate(outs, axis=1)


GROUP_W = SSD_WIDTH // SSD_GROUPS
PAIR_W = 2 * SSD_HEAD_DIM
STATE_SHAPE = (SSD_GROUPS, SSD_STATE, GROUP_W)


def _head_expander():
    r = lax.broadcasted_iota(jnp.int32, (128, SSD_WIDTH), 0)
    c = lax.broadcasted_iota(jnp.int32, (128, SSD_WIDTH), 1)
    return (c // SSD_HEAD_DIM == r).astype(F32)


def _ssd_expand(dt, cs, prm, ex):
    cs_x = _dot_onehot(cs, ex)
    cs_last_x = cs_x[CHUNK - 1:CHUNK, :]
    return (_dot_onehot(dt, ex, pieces=2), _dot_onehot(prm, ex)[2:3], jnp.exp(cs_x), jnp.exp(cs_last_x),
            jnp.exp(cs_last_x - cs_x))


def _ssd_fwd(xs, bc, dt_raw, z, prm, norm_w, ex):
    def body(xs_ref, bc_ref, dt_ref, z_ref, prm_ref, nw_ref, ex_ref, y_ref, yn_ref, prev_ref, state):
        c = pl.program_id(0)

        @pl.when(c == 0)
        def _():
            state[...] = jnp.zeros_like(state)

        prm = prm_ref[...]
        dt, a_row, cs, cs_t, causal, _, _ = _ssd_chunk_common(dt_ref[...], prm, c)
        dt_x, d_x, e_cs_x, e_last_x, dec_x = _ssd_expand(dt, cs, prm, ex_ref[...])
        xs_all = xs_ref[...]
        bc_all = bc_ref[...]
        xdt = xs_all * dt_x
        xdec = xdt * dec_x
        lane_lo = lax.broadcasted_iota(jnp.int32, (1, PAIR_W), 1) < SSD_HEAD_DIM
        for g in range(SSD_GROUPS):
            gs = slice(g * GROUP_W, (g + 1) * GROUP_W)
            b_g = bc_all[:, g * SSD_STATE:(g + 1) * SSD_STATE]
            c_g = bc_all[:, (SSD_GROUPS + g) * SSD_STATE:(SSD_GROUPS + g + 1) * SSD_STATE]
            st = state[g]
            prev_ref[0, g] = st
            y_off = _dot(c_g, st) * e_cs_x[:, gs]
            state[g] = st * e_last_x[:, gs] + _dot(b_g.T, xdec[:, gs])
            cb = _dot(c_g, b_g, NT)
            for k in range(SSD_HPG // 2):
                h0 = g * SSD_HPG + 2 * k
                ps = slice(h0 * SSD_HEAD_DIM, h0 * SSD_HEAD_DIM + PAIR_W)
                xdt_pair = xdt[:, ps]
                yd = []
                for h in (h0, h0 + 1):
                    lmat = jnp.where(causal, jnp.exp(cs[:, h:h + 1] - cs_t[h:h + 1, :]), 0.0)
                    yd.append(_dot(cb * lmat, xdt_pair))
                y_ref[:, ps] = (jnp.where(lane_lo, yd[0], yd[1]) + y_off[:, k * PAIR_W:(k + 1) * PAIR_W]
                                + xs_all[:, ps] * d_x[:, ps])
        yn_ref[...] = _gated_norm_fwd(y_ref[...], z_ref[...], nw_ref[...]).astype(_MXU)

    row = lambda w: pl.BlockSpec((CHUNK, w), lambda c: (c, 0))
    return pl.pallas_call(
        body, name="ssd_fwd", grid=(N_CHUNKS,),
        in_specs=[row(SSD_WIDTH), row(512), row(128), row(SSD_WIDTH),
                  pl.BlockSpec((8, 128), lambda c: (0, 0)), pl.BlockSpec((1, SSD_WIDTH), lambda c: (0, 0)),
                  pl.BlockSpec((128, SSD_WIDTH), lambda c: (0, 0))],
        out_specs=[row(SSD_WIDTH), row(SSD_WIDTH),
                   pl.BlockSpec((1,) + STATE_SHAPE, lambda c: (c, 0, 0, 0))],
        out_shape=[jax.ShapeDtypeStruct((T_ROWS, SSD_WIDTH), F32), jax.ShapeDtypeStruct((T_ROWS, SSD_WIDTH), _MXU),
                   jax.ShapeDtypeStruct((N_CHUNKS,) + STATE_SHAPE, F32)],
        scratch_shapes=[pltpu.VMEM(STATE_SHAPE, F32)],
        compiler_params=_cparams("arbitrary"),
    )(xs, bc, dt_raw, z, prm, norm_w, ex)


def _ssd_bwd(dyn, dyn_block, z, y_pre, xs, bc, dt_raw, prev, prm, norm_w, ex):
    def body(dyn_ref, z_ref, y_ref, xs_ref, bc_ref, dt_ref, prev_ref, prm_ref, nw_ref, ex_ref,
             dz_ref, dxs_ref, dbc_ref, ddt_ref, dprm_ref, dnw_ref, dstate):
        step = pl.program_id(0)
        c = N_CHUNKS - 1 - step

        @pl.when(step == 0)
        def _():
            dstate[...] = jnp.zeros_like(dstate)
            dprm_ref[...] = jnp.zeros_like(dprm_ref)
            dnw_ref[...] = jnp.zeros_like(dnw_ref)

        prm = prm_ref[...]
        dt, a_row, cs, cs_t, causal, tri, real = _ssd_chunk_common(dt_ref[...], prm, c)
        realf = real.astype(F32)
        z = z_ref[...]
        y_all = y_ref[...]
        nw = nw_ref[...]
        dyn_all = dyn_ref[...]
        sz = _silu(z)
        gated = y_all * sz
        half = SSD_WIDTH // SSD_GROUPS
        dgs, dnws = [], []
        for k in range(SSD_GROUPS):
            sl = slice(k * half, (k + 1) * half)
            dgk, dwk = _rms_bwd(gated[:, sl], nw[:, sl], dyn_all[:, sl])
            dgs.append(dgk)
            dnws.append(jnp.sum(dwk, axis=0, keepdims=True))
        dgated = jnp.concatenate(dgs, axis=1)
        dnw_ref[...] += jnp.concatenate(dnws, axis=1)
        dz_ref[...] = (dgated * y_all * _silu_grad(z)).astype(_MXU)
        dy_all = dgated * sz

        ex = ex_ref[...]
        dt_x, d_x, e_cs_x, e_last_x, dec_x = _ssd_expand(dt, cs, prm, ex)
        xs_all = xs_ref[...]
        bc_all = bc_ref[...]
        xdt = xs_all * dt_x
        xdt_mxu = xdt.astype(_MXU).astype(F32)
        xdec = xdt * dec_x
        dcp = dy_all * e_cs_x
        lane_lo = lax.broadcasted_iota(jnp.int32, (1, PAIR_W), 1) < SSD_HEAD_DIM
        upper = (lax.broadcasted_iota(jnp.int32, (CHUNK, CHUNK), 0)
                 <= lax.broadcasted_iota(jnp.int32, (CHUNK, CHUNK), 1))
        last_row = (lax.broadcasted_iota(jnp.int32, (CHUNK, 1), 0) == CHUNK - 1).astype(F32)
        dbs, dcs_, dxdt_parts, last_parts = [], [], [], []
        for g in range(SSD_GROUPS):
            gs = slice(g * GROUP_W, (g + 1) * GROUP_W)
            b_g = bc_all[:, g * SSD_STATE:(g + 1) * SSD_STATE]
            c_g = bc_all[:, (SSD_GROUPS + g) * SSD_STATE:(SSD_GROUPS + g + 1) * SSD_STATE]
            prev_t = prev_ref[0, g]
            dst = dstate[g]
            dc_g = _dot(dcp[:, gs], prev_t, NT)
            db_g = _dot(xdec[:, gs], dst, NT)
            dxdt_state = _dot(b_g, dst) * dec_x[:, gs]
            dstate[g] = dst * e_last_x[:, gs] + _dot(c_g.T, dcp[:, gs])
            last_parts.append(jnp.sum(xdt_mxu[:, gs] * dxdt_state, axis=0, keepdims=True)
                              + jnp.sum(dst * prev_t, axis=0, keepdims=True) * e_last_x[:, gs])
            cb_t = _dot(b_g, c_g, NT)
            dcb_t = jnp.zeros((CHUNK, CHUNK), F32)
            for k in range(SSD_HPG // 2):
                h0 = g * SSD_HPG + 2 * k
                ps = slice(h0 * SSD_HEAD_DIM, h0 * SSD_HEAD_DIM + PAIR_W)
                dy_pair = dy_all[:, ps]
                xdt_pair = xdt[:, ps]
                dd = []
                for h in (h0, h0 + 1):
                    lmat_t = jnp.where(upper, jnp.exp(cs_t[h:h + 1, :] - cs[:, h:h + 1]), 0.0)
                    dd.append(_dot(cb_t * lmat_t, dy_pair))
                    mine = lane_lo if h == h0 else jnp.logical_not(lane_lo)
                    dcb_t = dcb_t + _dot(jnp.where(mine, xdt_pair, 0.0), dy_pair, NT) * lmat_t
                dxdt_parts.append(jnp.where(lane_lo, dd[0], dd[1]) + dxdt_state[:, k * PAIR_W:(k + 1) * PAIR_W])
            dc_g = dc_g + _dot(dcb_t, b_g, TN)
            db_g = db_g + _dot(dcb_t, c_g)
            dbs.append(db_g * realf)
            dcs_.append(dc_g * realf)
        dbc_ref[...] = jnp.concatenate(dbs + dcs_, axis=1)
        dxdt = jnp.concatenate(dxdt_parts, axis=1)
        dxs_ref[...] = (dxdt * dt_x + dy_all * d_x) * realf
        ddt_all = _dot_onehot(dxdt * xs_all, ex, NT, pieces=2)
        rows = jnp.concatenate([jnp.concatenate(last_parts, axis=1), jnp.sum(dy_all * xs_all, axis=0, keepdims=True),
                                jnp.zeros((6, SSD_WIDTH), F32)], axis=0)
        rows = _dot_onehot(rows, ex, NT, pieces=2)
        dd_row = rows[1:2]
        dy_mxu = dy_all.astype(_MXU).astype(F32)
        dcs_all = (_dot_onehot(dy_mxu * (y_all - xs_all * d_x), ex, NT) - _dot_onehot(xdt_mxu * dxdt, ex, NT)
                   + last_row * rows[0:1])
        dda = _dot_onehot(tri, dcs_all, TN, data=1)
        ddt = (ddt_all + dda * a_row) * realf
        ddt_raw = ddt * _sigmoid(dt_ref[...] + prm[0:1])
        ddt_ref[...] = ddt_raw.astype(_MXU)
        da_log = jnp.sum(dda * dt, axis=0, keepdims=True) * a_row
        dprm_ref[0:1, :] += jnp.sum(ddt_raw, axis=0, keepdims=True)
        dprm_ref[1:2, :] += da_log
        dprm_ref[2:3, :] += dd_row

    rev = lambda w, blk=0: pl.BlockSpec((CHUNK, w), lambda s, blk=blk: (N_CHUNKS - 1 - s, blk))
    return pl.pallas_call(
        body, name="ssd_bwd", grid=(N_CHUNKS,),
        in_specs=[rev(SSD_WIDTH, dyn_block), rev(SSD_WIDTH), rev(SSD_WIDTH), rev(SSD_WIDTH), rev(512), rev(128),
                  pl.BlockSpec((1,) + STATE_SHAPE, lambda s: (N_CHUNKS - 1 - s, 0, 0, 0)),
                  pl.BlockSpec((8, 128), lambda s: (0, 0)), pl.BlockSpec((1, SSD_WIDTH), lambda s: (0, 0)),
                  pl.BlockSpec((128, SSD_WIDTH), lambda s: (0, 0))],
        out_specs=[rev(SSD_WIDTH), rev(SSD_WIDTH), rev(512), rev(128),
                   pl.BlockSpec((8, 128), lambda s: (0, 0)), pl.BlockSpec((1, SSD_WIDTH), lambda s: (0, 0))],
        out_shape=[jax.ShapeDtypeStruct((T_ROWS, SSD_WIDTH), _MXU), jax.ShapeDtypeStruct((T_ROWS, SSD_WIDTH), F32),
                   jax.ShapeDtypeStruct((T_ROWS, 512), F32), jax.ShapeDtypeStruct((T_ROWS, 128), _MXU),
                   jax.ShapeDtypeStruct((8, 128), F32), jax.ShapeDtypeStruct((1, SSD_WIDTH), F32)],
        scratch_shapes=[pltpu.VMEM(STATE_SHAPE, F32)],
        compiler_params=_cparams("arbitrary"),
    )(dyn, z, y_pre, xs, bc, dt_raw, prev, prm, norm_w, ex)


LRU_PAIRS = 8


def _lru_gates(xr, wa_ref, wx_ref, prm):
    pre_r, pre_i = [], []
    for k in range(LRU_PAIRS):
        xk = xr[:, k * 128:(k + 1) * 128]
        pre_r.append(_dot(xk, wa_ref[k]))
        pre_i.append(_dot(xk, wx_ref[k]))
    r = _sigmoid(jnp.concatenate(pre_r, axis=1) + prm[0:1])
    i = _sigmoid(jnp.concatenate(pre_i, axis=1) + prm[1:2])
    sp = _softplus(-prm[2:3])
    log_a = (-LRU_C) * r * sp
    a = jnp.exp(log_a)
    s = jnp.sqrt(-jnp.tanh(log_a) * (a * a + 1.0))
    return r, i, a, s, sp


def _lru_fwd(xr, gate, wa, wx, prm):
    def body(xr_ref, g_ref, wa_ref, wx_ref, prm_ref, hs_ref, yn_ref, carry, a_s, u_s):
        @pl.when(pl.program_id(0) == 0)
        def _():
            carry[...] = jnp.zeros_like(carry)

        prm = prm_ref[...]
        xr_t = xr_ref[...]
        _, i, a, s, _ = _lru_gates(xr_t, wa_ref, wx_ref, prm)
        a_s[...] = a
        u_s[...] = s * (i * xr_t)
        rid = lax.broadcasted_iota(jnp.int32, (8, LRU_WIDTH), 0)

        def group(k, h):
            off = pl.multiple_of(k * 8, 8)
            a8 = a_s[pl.ds(off, 8), :]
            u8 = u_s[pl.ds(off, 8), :]
            out = jnp.zeros((8, LRU_WIDTH), F32)
            for r_ in range(8):
                h = a8[r_:r_ + 1] * h + u8[r_:r_ + 1]
                out = jnp.where(rid == r_, h, out)
            hs_ref[pl.ds(off, 8), :] = out
            return h

        carry[0:1, :] = lax.fori_loop(0, CHUNK // 8, group, carry[0:1, :])
        gel, _ = _gelu_and_grad(g_ref[...])
        yn_ref[...] = _rms_fwd(gel * hs_ref[...], prm[3:4]).astype(_MXU)

    row = pl.BlockSpec((CHUNK, LRU_WIDTH), lambda t: (t, 0))
    wspec = pl.BlockSpec((LRU_PAIRS, 128, 128), lambda t: (0, 0, 0))
    return pl.pallas_call(
        body, name="lru_fwd", grid=(N_CHUNKS,),
        in_specs=[row, row, wspec, wspec, pl.BlockSpec((8, LRU_WIDTH), lambda t: (0, 0))],
        out_specs=[row, row],
        out_shape=[jax.ShapeDtypeStruct((T_ROWS, LRU_WIDTH), F32), jax.ShapeDtypeStruct((T_ROWS, LRU_WIDTH), _MXU)],
        scratch_shapes=[pltpu.VMEM((8, LRU_WIDTH), F32), pltpu.VMEM((CHUNK, LRU_WIDTH), F32),
                        pltpu.VMEM((CHUNK, LRU_WIDTH), F32)],
        compiler_params=_cparams("arbitrary"),
    )(xr, gate, wa, wx, prm)


def _lru_bwd(dyn, dyn_block, gate, xr, hs, wa, wx, wa_t, wx_t, prm):
    def body(dyn_ref, g_ref, xr_ref, hs_ref, hsp_ref, wa_ref, wx_ref, wat_ref, wxt_ref, prm_ref,
             dg_ref, dxr_ref, dwa_ref, dwx_ref, dprm_ref, carry, a_s, d_s):
        step = pl.program_id(0)
        tile = N_CHUNKS - 1 - step

        @pl.when(step == 0)
        def _():
            carry[...] = jnp.zeros_like(carry)
            dwa_ref[...] = jnp.zeros_like(dwa_ref)
            dwx_ref[...] = jnp.zeros_like(dwx_ref)
            dprm_ref[...] = jnp.zeros_like(dprm_ref)

        prm = prm_ref[...]
        xr_t = xr_ref[...]
        r, i, a, s, sp = _lru_gates(xr_t, wa_ref, wx_ref, prm)
        hs_t = hs_ref[...]
        gel, dgel = _gelu_and_grad(g_ref[...])
        dy, dnw = _rms_bwd(gel * hs_t, prm[3:4], dyn_ref[...])
        dg_ref[...] = (dy * hs_t * dgel).astype(_MXU)
        a_s[...] = a
        d_s[...] = dy * gel
        rid = lax.broadcasted_iota(jnp.int32, (8, LRU_WIDTH), 0)

        def group(k, cr):
            off = pl.multiple_of((CHUNK // 8 - 1 - k) * 8, 8)
            a8 = a_s[pl.ds(off, 8), :]
            d8 = d_s[pl.ds(off, 8), :]
            out = jnp.zeros((8, LRU_WIDTH), F32)
            for r_ in reversed(range(8)):
                dht = d8[r_:r_ + 1] + cr
                out = jnp.where(rid == r_, dht, out)
                cr = a8[r_:r_ + 1] * dht
            d_s[pl.ds(off, 8), :] = out
            return cr

        carry[0:1, :] = lax.fori_loop(0, CHUNK // 8, group, carry[0:1, :])
        dht = d_s[...]
        before = hsp_ref[CHUNK - 8:CHUNK, :][7:8] * (tile > 0).astype(F32)
        first = lax.broadcasted_iota(jnp.int32, (CHUNK, 1), 0) == 0
        hprev = jnp.where(first, before, pltpu.roll(hs_t, 1, 0))
        da = dht * hprev
        ixr = i * xr_t
        ds = dht * ixr
        dlog_a = da * a - ds * (a * a) * lax.rsqrt(s * s)
        dr = dlog_a * ((-LRU_C) * sp)
        dsp = jnp.sum(dlog_a * ((-LRU_C) * r), axis=0, keepdims=True)
        dlam = dsp * (-_sigmoid(-prm[2:3]))
        di = dht * s * xr_t
        dpre_r = dr * r * (1.0 - r)
        dpre_i = di * i * (1.0 - i)
        dxr = dht * s * i
        parts = []
        for k in range(LRU_PAIRS):
            sl = slice(k * 128, (k + 1) * 128)
            parts.append(_dot(dpre_r[:, sl], wat_ref[k]) + _dot(dpre_i[:, sl], wxt_ref[k]))
            dwa_ref[k] += _dot(xr_t[:, sl], dpre_r[:, sl], TN)
            dwx_ref[k] += _dot(xr_t[:, sl], dpre_i[:, sl], TN)
        dxr_ref[...] = dxr + jnp.concatenate(parts, axis=1)
        dprm_ref[0:1, :] += jnp.sum(dpre_r, axis=0, keepdims=True)
        dprm_ref[1:2, :] += jnp.sum(dpre_i, axis=0, keepdims=True)
        dprm_ref[2:3, :] += dlam
        dprm_ref[3:4, :] += jnp.sum(dnw, axis=0, keepdims=True)

    rev = lambda blk=0: pl.BlockSpec((CHUNK, LRU_WIDTH), lambda s, blk=blk: (N_CHUNKS - 1 - s, blk))
    wspec = pl.BlockSpec((LRU_PAIRS, 128, 128), lambda s: (0, 0, 0))
    return pl.pallas_call(
        body, name="lru_bwd", grid=(N_CHUNKS,),
        in_specs=[rev(dyn_block), rev(), rev(), rev(),
                  pl.BlockSpec((CHUNK, LRU_WIDTH), lambda s: (jnp.maximum(N_CHUNKS - 2 - s, 0), 0)),
                  wspec, wspec, wspec, wspec, pl.BlockSpec((8, LRU_WIDTH), lambda s: (0, 0))],
        out_specs=[rev(), rev(), wspec, wspec, pl.BlockSpec((8, LRU_WIDTH), lambda s: (0, 0))],
        out_shape=[jax.ShapeDtypeStruct((T_ROWS, LRU_WIDTH), _MXU), jax.ShapeDtypeStruct((T_ROWS, LRU_WIDTH), F32),
                   jax.ShapeDtypeStruct((LRU_PAIRS, 128, 128), F32), jax.ShapeDtypeStruct((LRU_PAIRS, 128, 128), F32),
                   jax.ShapeDtypeStruct((8, LRU_WIDTH), F32)],
        scratch_shapes=[pltpu.VMEM((8, LRU_WIDTH), F32), pltpu.VMEM((CHUNK, LRU_WIDTH), F32),
                        pltpu.VMEM((CHUNK, LRU_WIDTH), F32)],
        compiler_params=_cparams("arbitrary"),
    )(dyn, gate, xr, hs, hs, wa, wx, wa_t, wx_t, prm)


SEC_NAMES = ("z", "xs", "bc", "dt", "g", "x")
SEC_WIDTH = {"z": 1024, "xs": 1024, "bc": 512, "dt": 128, "g": 1024, "x": 1024}


def _pair_blocks(w):
    w = w.reshape(LRU_PAIRS, 2, 64, 64)
    zero = jnp.zeros((LRU_PAIRS, 64, 64), w.dtype)
    top = jnp.concatenate([w[:, 0], zero], axis=2)
    bot = jnp.concatenate([zero, w[:, 1]], axis=2)
    return jnp.concatenate([top, bot], axis=1)


def _unpair_blocks(wp):
    return jnp.stack([wp[:, :64, :64], wp[:, 64:, 64:]], axis=1).reshape(16, 64, 64)


def _pad_lanes(v, width=128):
    return jnp.pad(v, ((0, 0), (0, width - v.shape[1])))


class _Resident:
    before_embed = ()

    def __init__(self, w_in_sections, w_out, w_gate, w_up, w_down):
        self._w_in, self._w_out, self._ffn = w_in_sections, w_out, (w_gate, w_up, w_down)

    def w_in(self, after):
        return self._w_in

    def mid_forward(self, after):
        return jnp.zeros((1, 1), F32)

    def w_out(self, after):
        return self._w_out

    def ffn(self, after):
        return self._ffn

    def grads_ready(self, names, g, g_mxu):
        return jnp.zeros((1, 1), F32)

    def small_ready(self, g, loss):
        return jnp.zeros((1, 1), F32)

    def small_middle(self, after):
        return jnp.zeros((1, 1), F32)


def _local_step(x, target, meta, p, late):
    g, g_mxu = {}, {}
    ex = _head_expander()
    h0 = _embed(x, meta, late.before_embed)
    w_in = late.w_in(h0)
    u1, projs = _norm_proj(h0, p["norm1_w"], [w_in[s] for s in SEC_NAMES], name="norm_in_proj")
    proj = dict(zip(SEC_NAMES, projs))
    ssd_prm = jnp.concatenate([_pad_lanes(p["ssd_dt_bias"]), _pad_lanes(p["ssd_a_log"]), _pad_lanes(p["ssd_d"]),
                               jnp.zeros((5, 128), F32)], axis=0)
    xs_act = _conv_fwd(proj["xs"], p["ssd_conv_w"][:, :SSD_WIDTH], p["ssd_conv_b"][:, :SSD_WIDTH], silu=True,
                       name="ssd_conv_xs")
    bc_act = _conv_fwd(proj["bc"], p["ssd_conv_w"][:, SSD_WIDTH:], p["ssd_conv_b"][:, SSD_WIDTH:], silu=True,
                       name="ssd_conv_bc")
    y_pre, y_ssd, prev = _ssd_fwd(xs_act, bc_act, proj["dt"], proj["z"], ssd_prm, p["ssd_norm_w"], ex)
    xr = _conv_fwd(proj["x"], p["lru_conv_w"], p["lru_conv_b"], silu=False, name="lru_conv")
    wa_p, wx_p = _pair_blocks(p["lru_wa"]), _pair_blocks(p["lru_wx"])
    lru_prm = jnp.concatenate([p["lru_ba"], p["lru_bx"], p["lru_lambda"], p["lru_norm_w"],
                               jnp.zeros((4, LRU_WIDTH), F32)], axis=0)
    hs, y_lru = _lru_fwd(xr, proj["g"], wa_p.astype(_MXU), wx_p.astype(_MXU), lru_prm + late.mid_forward(xr))
    ycat = jnp.concatenate([y_ssd, y_lru], axis=1)
    w_out = late.w_out(ycat)
    h1 = _mm([(ycat, 0, w_out, 0, 2 * D_MODEL)], T_ROWS, D_MODEL, tm=T_ROWS, tn=256, mode="nn", out_dtype=F32,
             name="out_proj", residual=h0)
    u2 = _rmsnorm(h1, p["norm2_w"], name="norm2")
    w_gate, w_up, w_down = late.ffn(u2)
    gp, up, act = _ffn_up(u2, w_gate, w_up)
    h2 = _mm([(act, 0, w_down, 0, D_FF)], T_ROWS, D_MODEL, tm=T_ROWS, tn=256, mode="nn", out_dtype=F32,
             name="ffn_down", residual=h1)
    loss, dh2, dh2b, g["final_norm_w"] = _loss_head(h2, target, p["final_norm_w"])
    dgp, dup = _ffn_bwd_act(dh2b, w_down, gp, up)
    g["w_down"], g_mxu["w_down"] = _mm([(act, 0, dh2b, 0, T_ROWS)], D_FF, D_MODEL, tm=1408, tn=512, mode="tn",
                                       out_dtype=F32, name="dw_down", also_mxu=True)
    dh1, dh1b, g["norm2_w"] = _mm_norm_bwd([(dgp, w_gate, D_FF), (dup, w_up, D_FF)], h1, p["norm2_w"], dh2,
                                           name="ffn_bwd_in")
    g["w_gate"], g_mxu["w_gate"] = _mm([(dgp, 0, u2, 0, T_ROWS)], D_FF, D_MODEL, tm=1408, tn=512, mode="tn",
                                       out_dtype=F32, name="dw_gate", also_mxu=True)
    g["w_up"], g_mxu["w_up"] = _mm([(dup, 0, u2, 0, T_ROWS)], D_FF, D_MODEL, tm=1408, tn=512, mode="tn",
                                   out_dtype=F32, name="dw_up", also_mxu=True)
    sent = late.grads_ready(("w_down", "w_gate", "w_up"), g, g_mxu)
    g["w_out"], g_mxu["w_out"] = _mm([(ycat, 0, dh1b, 0, T_ROWS)], 2 * D_MODEL, D_MODEL, tm=1024, tn=512, mode="tn",
                                     out_dtype=F32, name="dw_out", also_mxu=True, behind=(sent,))
    sent = late.grads_ready(("w_out",), g, g_mxu)
    dycat = _mm([(dh1b, 0, w_out, 0, D_MODEL)], T_ROWS, 2 * D_MODEL, tm=T_ROWS, tn=256, mode="nt", out_dtype=F32,
                name="out_proj_bwd", behind=(sent,))
    dgate, dxr, dwa_p, dwx_p, dlru_prm = _lru_bwd(dycat, 1, proj["g"], xr, hs, wa_p.astype(_MXU), wx_p.astype(_MXU),
                                                  jnp.swapaxes(wa_p, 1, 2).astype(_MXU),
                                                  jnp.swapaxes(wx_p, 1, 2).astype(_MXU), lru_prm)
    g["lru_wa"], g["lru_wx"] = _unpair_blocks(dwa_p), _unpair_blocks(dwx_p)
    g["lru_ba"], g["lru_bx"], g["lru_lambda"], g["lru_norm_w"] = (dlru_prm[k:k + 1] for k in range(4))
    dx_lru, g["lru_conv_w"], g["lru_conv_b"] = _conv_bwd(dxr, proj["x"], p["lru_conv_w"], p["lru_conv_b"], silu=False,
                                                         name="lru_conv_bwd")
    dz, dxs_act, dbc_act, ddt, dssd_prm, g["ssd_norm_w"] = _ssd_bwd(dycat, 0, proj["z"], y_pre, xs_act, bc_act,
                                                                    proj["dt"], prev, ssd_prm, p["ssd_norm_w"], ex)
    g["ssd_dt_bias"], g["ssd_a_log"], g["ssd_d"] = (dssd_prm[k:k + 1, :SSD_HEADS] for k in range(3))
    dxs, dcw_xs, dcb_xs = _conv_bwd(dxs_act, proj["xs"], p["ssd_conv_w"][:, :SSD_WIDTH],
                                    p["ssd_conv_b"][:, :SSD_WIDTH], silu=True, name="ssd_conv_xs_bwd")
    dbc, dcw_bc, dcb_bc = _conv_bwd(dbc_act, proj["bc"], p["ssd_conv_w"][:, SSD_WIDTH:],
                                    p["ssd_conv_b"][:, SSD_WIDTH:], silu=True, name="ssd_conv_bc_bwd")
    g["ssd_conv_w"] = jnp.concatenate([dcw_xs, dcw_bc], axis=1)
    g["ssd_conv_b"] = jnp.concatenate([dcb_xs, dcb_bc], axis=1)
    dproj = {"z": dz, "xs": dxs, "bc": dbc, "dt": ddt, "g": dgate, "x": dx_lru}
    dh0, _, g["norm1_w"] = _mm_norm_bwd([(dproj[s], w_in[s], SEC_WIDTH[s]) for s in SEC_NAMES], h0,
                                        p["norm1_w"], dh1, name="in_proj_bwd")
    g["meta_tokens"] = dh0[PAD_ROWS:X_ROW0]
    sent = late.small_ready(g, loss)
    for s in SEC_NAMES:
        wdt = SEC_WIDTH[s]
        g["w_in_" + s], g_mxu["w_in_" + s] = _mm([(dproj[s], 0, u1, 0, T_ROWS)], wdt, D_MODEL, tm=min(wdt, 1024),
                                                 tn=512, mode="tn", out_dtype=F32, name="dw_in_" + s, also_mxu=True,
                                                 behind=(sent,))
        if s == "bc":
            sent = late.small_middle([g["w_in_z"], g["w_in_xs"], g["w_in_bc"]])
    late.grads_ready(("w_in",), g, g_mxu)
    return loss, dh0[X_ROW0:], g, g_mxu


MESH = pl.DeviceIdType.MESH
ANY = pl.BlockSpec(memory_space=pl.ANY)


def _my_place():
    return lax.axis_index("x"), lax.axis_index("y"), lax.axis_index("c")


def _other_chips(x, y):
    return [(1 - x, y), (x, 1 - y), (1 - x, 1 - y)]


HBM_SPEC = pl.BlockSpec(memory_space=pltpu.HBM)
SEM_SPEC = pl.BlockSpec(memory_space=pltpu.SEMAPHORE)
SPLIT_EFFECT = pltpu.SideEffectType.DATAFLOW_SIDE_EFFECTING


def _half_cols(buf, c, other=False):
    half = buf.shape[-1] // 2
    return pl.ds(pl.multiple_of(((1 - c) if other else c) * half, 128), half)


def _halves_plan(bufs, x, y, c, incoming):
    plan = []
    for buf in bufs:
        cols = _half_cols(buf, c)
        for (px, py) in _other_chips(x, y):
            slot = 2 * px + py if incoming else 2 * x + y
            plan.append((buf.at[2 * x + y, :, cols], buf.at[slot, :, cols], (px, py, c)))
    return plan


def _forward_plan(bufs, x, y, c, incoming):
    plan = []
    for buf in bufs:
        for (px, py) in _other_chips(x, y):
            slot = 2 * px + py
            plan.append((buf.at[slot, :, _half_cols(buf, c)], buf.at[slot, :, _half_cols(buf, c, other=incoming)],
                         (x, y, 1 - c)))
    return plan


def _scatter_plan(bufs, x, y, c, incoming):
    n = len(bufs) // 2
    plan = []
    for k in range(n):
        for j, (px, py) in enumerate(_other_chips(x, y)):
            plan.append((bufs[k].at[2 * px + py], bufs[n + k].at[j], (px, py, c)))
    return plan


def _split_start(bufs, plan, n_copies, after, *, name):
    n = len(bufs)
    extra = [] if after is None else [after]

    def body(*refs):
        ins = refs[:n]
        send_sems, recv_sems = refs[n + len(extra)], refs[n + len(extra) + 1]
        token = refs[-1]
        x, y, c = _my_place()
        for i, (src, dst, dev) in enumerate(plan(ins, x, y, c, False)):
            pltpu.make_async_remote_copy(src_ref=src, dst_ref=dst, send_sem=send_sems.at[i], recv_sem=recv_sems.at[i],
                                         device_id=dev, device_id_type=MESH).start()
        token[...] = jnp.zeros_like(token)

    outs = pl.pallas_call(
        body, name=name,
        out_shape=(pltpu.SemaphoreType.DMA((n_copies,)), pltpu.SemaphoreType.DMA((n_copies,)),
                   *[pltpu.HBM(b.shape, b.dtype) for b in bufs], jax.ShapeDtypeStruct((8, 128), F32)),
        in_specs=[HBM_SPEC] * n + [ANY] * len(extra),
        out_specs=(SEM_SPEC, SEM_SPEC, *[HBM_SPEC] * n, pl.BlockSpec(memory_space=pltpu.VMEM)),
        input_output_aliases={k: 2 + k for k in range(n)},
        compiler_params=pltpu.CompilerParams(has_side_effects=SPLIT_EFFECT),
    )(*[pltpu.with_memory_space_constraint(b, pltpu.HBM) for b in bufs], *extra)
    return outs[0], outs[1], list(outs[2:2 + n]), outs[-1]


def _split_wait(bufs, send_sems, recv_sems, plan, after, *, name):
    n = len(bufs)
    after = list(after) if isinstance(after, (list, tuple)) else [after]

    def body(*refs):
        ins = refs[:n]
        send_sems_ref, recv_sems_ref = refs[n], refs[n + 1]
        x, y, c = _my_place()
        for i, (src, dst, dev) in enumerate(plan(ins, x, y, c, True)):
            cp = pltpu.make_async_remote_copy(src_ref=src, dst_ref=dst, send_sem=send_sems_ref.at[i],
                                              recv_sem=recv_sems_ref.at[i], device_id=dev, device_id_type=MESH)
            cp.wait_send()
            cp.wait_recv()

    outs = pl.pallas_call(
        body, name=name, out_shape=tuple(pltpu.HBM(b.shape, b.dtype) for b in bufs),
        in_specs=[HBM_SPEC] * n + [SEM_SPEC, SEM_SPEC] + [ANY] * len(after), out_specs=tuple([HBM_SPEC] * n),
        input_output_aliases={k: k for k in range(n)},
        compiler_params=pltpu.CompilerParams(has_side_effects=SPLIT_EFFECT),
    )(*bufs, send_sems, recv_sems, *after)
    return list(outs)


def _fill_own_slots(shards, me_arr, *, name, behind=()):
    n = len(shards)
    n_in = n + len(behind)

    def body(me_ref, *refs):
        for k in range(n):
            refs[n_in + k][0] = refs[k][...].astype(_MXU)

    half = D_MODEL // 2
    return pl.pallas_call(
        body, name=name,
        grid_spec=pltpu.PrefetchScalarGridSpec(
            num_scalar_prefetch=1, grid=(2,),
            in_specs=[pl.BlockSpec((s.shape[0], half), lambda i, me: (0, i)) for s in shards]
            + [pl.BlockSpec(memory_space=pl.ANY)] * len(behind),
            out_specs=[pl.BlockSpec((1, s.shape[0], half), lambda i, me: (me[0], 0, i)) for s in shards]),
        out_shape=[jax.ShapeDtypeStruct((N_SHARDS,) + s.shape, _MXU) for s in shards],
        compiler_params=_cparams("parallel"),
    )(me_arr, *shards, *behind)


def _gather_small(small):
    def body(s_ref, o_ref, send_sems, recv_sems, local_sem):
        x, y, c = _my_place()
        me = 2 * x + y
        local = pltpu.make_async_copy(s_ref, o_ref.at[me], local_sem)
        local.start()
        copies = [(pltpu.make_async_remote_copy(src_ref=s_ref, dst_ref=o_ref.at[me], send_sem=send_sems.at[j],
                                                recv_sem=recv_sems.at[j], device_id=(px, py, c), device_id_type=MESH),
                   2 * px + py) for j, (px, py) in enumerate(_other_chips(x, y))]
        for cp, _ in copies:
            cp.start()
        for j, (cp, slot) in enumerate(copies):
            cp.wait_send()
            pltpu.make_async_remote_copy(src_ref=s_ref, dst_ref=o_ref.at[slot], send_sem=send_sems.at[j],
                                         recv_sem=recv_sems.at[j], device_id=(x, y, c),
                                         device_id_type=MESH).wait_recv()
        local.wait()

    return pl.pallas_call(
        body, name="gather_small", in_specs=[ANY], out_specs=ANY,
        out_shape=jax.ShapeDtypeStruct((N_SHARDS,) + small.shape, small.dtype),
        scratch_shapes=[pltpu.SemaphoreType.DMA((3,)), pltpu.SemaphoreType.DMA((3,)), pltpu.SemaphoreType.DMA],
    )(small)


def _swap_with_sibling(parts, *, name):
    n = len(parts)

    def body(*refs):
        ins, outs = refs[:n], refs[n:2 * n]
        send_sems, recv_sems = refs[2 * n:]
        x, y, c = _my_place()
        copies = [pltpu.make_async_remote_copy(
            src_ref=ins[k], dst_ref=outs[k], send_sem=send_sems.at[k], recv_sem=recv_sems.at[k],
            device_id=(x, y, 1 - c), device_id_type=MESH) for k in range(n)]
        for cp in copies:
            cp.start()
        for cp in copies:
            cp.wait()

    return pl.pallas_call(
        body, name=name, in_specs=[ANY] * n, out_specs=[ANY] * n,
        out_shape=[jax.ShapeDtypeStruct(a.shape, a.dtype) for a in parts],
        scratch_shapes=[pltpu.SemaphoreType.DMA((n,)), pltpu.SemaphoreType.DMA((n,))],
    )(*parts)


def _other_devices(x, y, c):
    out = []
    for mask in range(1, N_DEV):
        px, py, pc = x ^ (mask >> 2 & 1), y ^ (mask >> 1 & 1), c ^ (mask & 1)
        out.append(((px, py, pc), 4 * px + 2 * py + pc))
    return out


def _pieces_plan(bufs, x, y, c, incoming):
    pack, land = bufs
    me = 4 * x + 2 * y + c
    return [(pack.at[num], land.at[num if incoming else me], dev) for dev, num in _other_devices(x, y, c)]


def _spread_plan(bufs, x, y, c, incoming):
    piece, land = bufs
    me = 4 * x + 2 * y + c
    return [(piece, land.at[num if incoming else me], dev) for dev, num in _other_devices(x, y, c)]


def _sum_pieces(pack, land, dev_arr, *, name):
    def body(dev_ref, pack_ref, land_ref, o_ref):
        dev = dev_ref[0]
        own = pack_ref[dev]
        acc = None
        for d in range(N_DEV):
            term = jnp.where(dev == d, own, land_ref[d])
            acc = term if acc is None else acc + term
        o_ref[...] = acc

    vmem = pl.BlockSpec(memory_space=pltpu.VMEM)
    return pl.pallas_call(
        body, name=name, in_specs=[pl.BlockSpec(memory_space=pltpu.SMEM), vmem, vmem], out_specs=vmem,
        out_shape=jax.ShapeDtypeStruct(pack.shape[1:], F32),
    )(dev_arr, pack, land)


def _join_pieces(piece, land, dev_arr, *, name):
    def body(dev_ref, piece_ref, land_ref, o_ref):
        dev = dev_ref[0]
        for d in range(N_DEV):
            o_ref[d] = jnp.where(dev == d, piece_ref[...], land_ref[d])

    vmem = pl.BlockSpec(memory_space=pltpu.VMEM)
    return pl.pallas_call(
        body, name=name, in_specs=[pl.BlockSpec(memory_space=pltpu.SMEM), vmem, vmem], out_specs=vmem,
        out_shape=jax.ShapeDtypeStruct(land.shape, F32),
    )(dev_arr, piece, land)


def _adamw_native(ws, gs, ms, vs):
    n = len(ws)

    def body(*refs):
        for k in range(n):
            w_ref, g_ref, m_ref, v_ref = (refs[j * n + k] for j in range(4))
            delta, m_new, v_new = _adamw_math(w_ref[...], g_ref[...], m_ref[...], v_ref[...])
            refs[4 * n + k][...] = delta
            refs[5 * n + k][...] = m_new
            refs[6 * n + k][...] = v_new

    vmem = pl.BlockSpec(memory_space=pltpu.VMEM)
    shapes = [jax.ShapeDtypeStruct(a.shape, F32) for a in ws]
    outs = pl.pallas_call(
        body, name="adamw_small", in_specs=[vmem] * (4 * n), out_specs=[vmem] * (3 * n), out_shape=shapes * 3,
        compiler_params=pltpu.CompilerParams(vmem_limit_bytes=VMEM_LIMIT_BYTES),
    )(*ws, *gs, *ms, *vs)
    return outs[:n], outs[n:2 * n], outs[2 * n:]


def _elementwise_tile(rows, cols):
    for t in range(256, 15, -16):
        if rows % t == 0:
            return (t, cols), rows // t, lambda i: (i, 0)
    assert cols % 256 == 0
    return (rows, 256), cols // 256, lambda i: (0, i)


def _partial_sum(own, land, me_arr, *, name):
    r, c = own.shape[-2:]
    tile, steps, imap = _elementwise_tile(r, c)
    whole = own.ndim == 3

    def body(me_ref, own_ref, land_ref, o_ref):
        acc = own_ref[0] if whole else own_ref[...]
        for j in range(3):
            acc = acc + land_ref[j].astype(F32)
        o_ref[...] = acc

    own_spec = (pl.BlockSpec((1,) + tile, lambda i, me: (me[0],) + imap(i)) if whole
                else pl.BlockSpec(tile, lambda i, me: imap(i)))
    return pl.pallas_call(
        body, name=name,
        grid_spec=pltpu.PrefetchScalarGridSpec(
            num_scalar_prefetch=1, grid=(steps,),
            in_specs=[own_spec, pl.BlockSpec((3,) + tile, lambda i, me: (0,) + imap(i))],
            out_specs=pl.BlockSpec(tile, lambda i, me: imap(i))),
        out_shape=jax.ShapeDtypeStruct((r, c), F32),
        compiler_params=_cparams("parallel"),
    )(me_arr, own, land)


def _adamw_math(w, g, m, v):
    m = ADAM_B1 * m + (1.0 - ADAM_B1) * g
    v = ADAM_B2 * v + (1.0 - ADAM_B2) * (g * g)
    m_hat = m / (1.0 - ADAM_B1 ** ADAM_STEP)
    v_hat = v / (1.0 - ADAM_B2 ** ADAM_STEP)
    delta = -ADAM_LR * (m_hat / (jnp.sqrt(v_hat) + ADAM_EPS) + ADAM_WD * w)
    return delta, m, v


def _adamw(w, grad_parts, m, v, *, name):
    r, c = w.shape
    tile_shape, steps, imap = _elementwise_tile(r, c)
    n = len(grad_parts)

    def body(*refs):
        w_ref, m_ref, v_ref = refs[:3]
        g_refs = refs[3:3 + n]
        g_out, d_out, m_out, v_out = refs[3 + n:]
        g = g_refs[0][...]
        for k in range(1, n):
            g = g + g_refs[k][...]
        delta, m_new, v_new = _adamw_math(w_ref[...], g, m_ref[...], v_ref[...])
        g_out[...] = g
        d_out[...] = delta
        m_out[...] = m_new
        v_out[...] = v_new

    tile = pl.BlockSpec(tile_shape, imap)
    return pl.pallas_call(
        body, name=name, grid=(steps,), in_specs=[tile] * (3 + n), out_specs=[tile] * 4,
        out_shape=[jax.ShapeDtypeStruct((r, c), F32)] * 4,
        compiler_params=_cparams("parallel"),
    )(w, m, v, *grad_parts)


WEIGHT_NAMES = ("meta_tokens", "norm1_w", "w_in", "ssd_conv_w", "ssd_conv_b", "ssd_dt_bias", "ssd_a_log", "ssd_d",
                "ssd_norm_w", "lru_conv_w", "lru_conv_b", "lru_wa", "lru_ba", "lru_wx", "lru_bx", "lru_lambda",
                "lru_norm_w", "w_out", "norm2_w", "w_gate", "w_up", "w_down", "final_norm_w")
BIG = ("w_in", "w_out", "w_gate", "w_up", "w_down")
FFN = ("w_gate", "w_up", "w_down")
LATE = ("w_out",) + FFN
SMALL_SHARDED = {"meta_tokens": (N_META, D_MODEL), "ssd_conv_w": (CONV_K, 1536), "lru_conv_w": (CONV_K, LRU_WIDTH)}
SMALL = tuple(n for n in WEIGHT_NAMES if n not in BIG)
PACK_COLS = 1024


def _pack(arrays, row_multiple):
    flat = jnp.concatenate([a.reshape(-1) for a in arrays])
    rows = -(-flat.shape[0] // (row_multiple * PACK_COLS)) * row_multiple
    return jnp.pad(flat, (0, rows * PACK_COLS - flat.shape[0])).reshape(rows, PACK_COLS)


def _unpack(pack, shapes):
    flat = pack.reshape(-1)
    out, off = [], 0
    for s in shapes:
        size = math.prod(s)
        out.append(flat[off:off + size].reshape(s))
        off += size
    return out


def _unshard_cols(g4):
    return jnp.swapaxes(g4, 0, 1).reshape(g4.shape[1], -1)


COL_SHARDED = ("w_in", "w_gate", "w_up")
IN_ROWS = {"z": (0, 1024), "xs": (1024, 2048), "bc": (2048, 2560), "dt": (2560, 2576), "g": (2576, 3600),
           "x": (3600, IN_COLS)}


def _rows_of_shards(shards4, lo, hi):
    r = shards4.shape[1]
    parts = [shards4[k, max(lo, k * r) - k * r:min(hi, (k + 1) * r) - k * r]
             for k in range(N_SHARDS) if max(lo, k * r) < min(hi, (k + 1) * r)]
    return parts[0] if len(parts) == 1 else jnp.concatenate(parts, axis=0)


def _w_in_shard_rows(k, sections):
    lo, hi = k * (IN_COLS // N_SHARDS), (k + 1) * (IN_COLS // N_SHARDS)
    parts = []
    for arr, (a, b) in zip(sections, IN_ROWS.values()):
        if max(lo, a) < min(hi, b):
            parts.append(arr[max(lo, a) - a:min(hi, b) - a])
    return jnp.concatenate(parts, axis=0)


def _rows_view(name, block):
    return jnp.swapaxes(block[0], 0, 1) if name in COL_SHARDED else block[0]


def _param_view(name, rows):
    return (jnp.swapaxes(rows, 0, 1) if name in COL_SHARDED else rows)[None]


def kernel(x, meta_tokens, norm1_w, w_in, ssd_conv_w, ssd_conv_b, ssd_dt_bias, ssd_a_log, ssd_d, ssd_norm_w, lru_conv_w, lru_conv_b, lru_wa, lru_ba, lru_wx, lru_bx, lru_lambda, lru_norm_w, w_out, norm2_w, w_gate, w_up, w_down, final_norm_w, loss_target, m_meta_tokens, m_norm1_w, m_w_in, m_ssd_conv_w, m_ssd_conv_b, m_ssd_dt_bias, m_ssd_a_log, m_ssd_d, m_ssd_norm_w, m_lru_conv_w, m_lru_conv_b, m_lru_wa, m_lru_ba, m_lru_wx, m_lru_bx, m_lru_lambda, m_lru_norm_w, m_w_out, m_norm2_w, m_w_gate, m_w_up, m_w_down, m_final_norm_w, v_meta_tokens, v_norm1_w, v_w_in, v_ssd_conv_w, v_ssd_conv_b, v_ssd_dt_bias, v_ssd_a_log, v_ssd_d, v_ssd_norm_w, v_lru_conv_w, v_lru_conv_b, v_lru_wa, v_lru_ba, v_lru_wx, v_lru_bx, v_lru_lambda, v_lru_norm_w, v_w_out, v_norm2_w, v_w_gate, v_w_up, v_w_down, v_final_norm_w):
    w = dict(zip(WEIGHT_NAMES, (meta_tokens, norm1_w, w_in, ssd_conv_w, ssd_conv_b, ssd_dt_bias, ssd_a_log, ssd_d, ssd_norm_w, lru_conv_w, lru_conv_b, lru_wa, lru_ba, lru_wx, lru_bx, lru_lambda, lru_norm_w, w_out, norm2_w, w_gate, w_up, w_down, final_norm_w)))
    m = dict(zip(WEIGHT_NAMES, (m_meta_tokens, m_norm1_w, m_w_in, m_ssd_conv_w, m_ssd_conv_b, m_ssd_dt_bias, m_ssd_a_log, m_ssd_d, m_ssd_norm_w, m_lru_conv_w, m_lru_conv_b, m_lru_wa, m_lru_ba, m_lru_wx, m_lru_bx, m_lru_lambda, m_lru_norm_w, m_w_out, m_norm2_w, m_w_gate, m_w_up, m_w_down, m_final_norm_w)))
    v = dict(zip(WEIGHT_NAMES, (v_meta_tokens, v_norm1_w, v_w_in, v_ssd_conv_w, v_ssd_conv_b, v_ssd_dt_bias, v_ssd_a_log, v_ssd_d, v_ssd_norm_w, v_lru_conv_w, v_lru_conv_b, v_lru_wa, v_lru_ba, v_lru_wx, v_lru_bx, v_lru_lambda, v_lru_norm_w, v_w_out, v_norm2_w, v_w_gate, v_w_up, v_w_down, v_final_norm_w)))
    me = 2 * lax.axis_index("x") + lax.axis_index("y")

    big2d = {n: _rows_view(n, w[n]) for n in BIG}
    small_local = jnp.concatenate([w["meta_tokens"].reshape(-1), w["ssd_conv_w"].reshape(-1),
                                   w["lru_conv_w"].reshape(-1)])[None]
    me_arr = me.astype(jnp.int32).reshape(1)
    dev_arr = (2 * me + lax.axis_index("c")).astype(jnp.int32).reshape(1)
    small4 = _gather_small(small_local)
    (w_in_slot,) = _fill_own_slots([big2d["w_in"]], me_arr, name="own_slot_w_in")
    in_send, in_recv, in_bufs, in_tok = _split_start([w_in_slot], _halves_plan, 3, small4, name="gather_w_in_start")
    late_slots = _fill_own_slots([big2d[n] for n in LATE], me_arr, name="own_slots_late", behind=(in_tok,))
    sm = small4[:, 0]
    meta_full = _unshard_cols(sm[:, :4096].reshape(N_SHARDS, N_META, 256))
    ssd_conv_w_full = _unshard_cols(sm[:, 4096:5632].reshape(N_SHARDS, CONV_K, 384))
    lru_conv_w_full = _unshard_cols(sm[:, 5632:].reshape(N_SHARDS, CONV_K, 256))

    p = {"ssd_conv_w": ssd_conv_w_full, "lru_conv_w": lru_conv_w_full,
         "lru_wa": w["lru_wa"][0], "lru_wx": w["lru_wx"][0], "final_norm_w": w["final_norm_w"][None]}
    for n in ("norm1_w", "ssd_conv_b", "ssd_dt_bias", "ssd_a_log", "ssd_d", "ssd_norm_w", "lru_conv_b", "lru_ba",
              "lru_bx", "lru_lambda", "lru_norm_w", "norm2_w"):
        p[n] = w[n]

    class Late:
        def __init__(self):
            self.pending = []
            self.before_embed = (late_slots[0],)

        def w_in(self, after):
            (buf,) = _split_wait(in_bufs, in_send, in_recv, _halves_plan, after, name="gather_w_in_wait")
            send, recv, bufs, tok = _split_start([buf], _forward_plan, 3, None, name="forward_w_in_start")
            self.late_gather = _split_start(late_slots, _halves_plan, 3 * len(LATE), tok, name="gather_late_start")
            (w_in4,) = _split_wait(bufs, send, recv, _forward_plan, self.late_gather[2][0], name="forward_w_in_wait")
            sections = {s: _rows_of_shards(w_in4, lo, hi) for s, (lo, hi) in IN_ROWS.items()}
            sections["dt"] = jnp.pad(sections["dt"], ((0, SEC_WIDTH["dt"] - SSD_HEADS), (0, 0)))
            return sections

        def mid_forward(self, after):
            send, recv, bufs, _ = self.late_gather
            bufs = _split_wait(bufs, send, recv, _halves_plan, after, name="gather_late_wait")
            self.forward = _split_start(bufs, _forward_plan, 3 * len(LATE), None, name="forward_late_start")
            return self.forward[3][:1, :1]

        def w_out(self, after):
            send, recv, bufs, _ = self.forward
            bufs = _split_wait(bufs, send, recv, _forward_plan, after, name="forward_late_wait")
            self.late = dict(zip(LATE, (b.reshape(-1, D_MODEL) for b in bufs)))
            return self.late["w_out"]

        def ffn(self, after):
            return tuple(self.late[n] for n in FFN)

        def grads_ready(self, names, g, g_mxu):
            if names == ("w_in",):
                g_mxu["w_in"] = jnp.stack([_w_in_shard_rows(k, [g_mxu["w_in_" + s] for s in SEC_NAMES])
                                           for k in range(N_SHARDS)])
            srcs = [g_mxu[n].reshape(N_SHARDS, -1, D_MODEL) for n in names]
            lands = [lax.empty((3,) + s.shape[1:], _MXU) for s in srcs]
            tag = "_".join(names)
            send, recv, bufs, tok = _split_start(srcs + lands, _scatter_plan, 3 * len(names), None,
                                                 name="scatter_" + tag + "_start")
            self.pending.append((names, send, recv, bufs, tag))
            self.in_flight = bufs[0]
            return tok[:1, :1]

        def landed(self, after, which):
            land = {}
            for names, send, recv, bufs, tag in self.pending:
                if names[0] in which:
                    bufs = _split_wait(bufs, send, recv, _scatter_plan, after, name="scatter_" + tag + "_wait")
                    land.update(zip(names, bufs[len(names):]))
            return land

        def small_ready(self, g, loss):
            pack = _pack([g[n] for n in SMALL] + [loss[0, :1]], 8 * N_DEV)
            pack = pack.reshape(N_DEV, -1, PACK_COLS)
            self.small = _split_start([pack, lax.empty(pack.shape, F32)], _pieces_plan, N_DEV - 1, loss,
                                      name="small_pieces_start")
            return self.small[3]

        def small_middle(self, after):
            send, recv, bufs, _ = self.small
            pack, land = _split_wait(bufs, send, recv, _pieces_plan, after, name="small_pieces_wait")
            piece = _sum_pieces(pack, land, dev_arr, name="small_pieces_sum")
            self.small = _split_start([piece, lax.empty(pack.shape, F32)], _spread_plan, N_DEV - 1, None,
                                      name="small_spread_start")
            return self.small[3]

        def small_sum(self, after):
            send, recv, bufs, _ = self.small
            piece, land = _split_wait(bufs, send, recv, _spread_plan, after, name="small_spread_wait")
            return _join_pieces(piece, land, dev_arr, name="small_join")

    late = Late()

    loss, grad_x, g, g_mxu = _local_step(x[0], loss_target[0], meta_full, p, late)

    g4 = {n: g[n].reshape(N_SHARDS, -1, D_MODEL) for n in LATE}
    g4["w_in"] = lax.switch(me, [functools.partial(_w_in_shard_rows, k) for k in range(N_SHARDS)],
                            [g["w_in_" + s] for s in SEC_NAMES])
    land = late.landed(late.in_flight, LATE)
    part = {n: _partial_sum(g4[n], land[n], me_arr, name="partial_" + n) for n in LATE}
    sib = dict(zip(LATE, _swap_with_sibling([part[n] for n in LATE], name="swap_late")))

    small_full_shape = {n: (SMALL_SHARDED[n] if n in SMALL_SHARDED else w[n].shape) for n in SMALL}
    red_list = _unpack(late.small_sum(sib["w_out"]), [small_full_shape[n] for n in SMALL] + [(1,)])
    loss_total = red_list[-1][0]
    g_small = {}
    for n, arr in zip(SMALL, red_list[:-1]):
        if n in SMALL_SHARDED:
            cols = SMALL_SHARDED[n][1] // N_SHARDS
            arr = lax.dynamic_slice_in_dim(arr, me * cols, cols, axis=1)
        g_small[n] = arr.reshape(w[n].shape)

    grad, delta, new_m, new_v = {}, {}, {}, {}

    def update_big(n):
        outs = _adamw(big2d[n], [part[n], sib[n]], _rows_view(n, m[n]), _rows_view(n, v[n]), name="adamw_" + n)
        grad[n], delta[n], new_m[n], new_v[n] = (_param_view(n, o) for o in outs)
        return outs[0]

    two_d = lambda a: a.reshape(1, -1) if a.ndim == 1 else a
    deltas, new_ms, new_vs = _adamw_native(*[[two_d(d[n]) for n in SMALL] for d in (w, g_small, m, v)])
    for n, dn, mn, vn in zip(SMALL, deltas, new_ms, new_vs):
        grad[n], delta[n], new_m[n], new_v[n] = (g_small[n], dn.reshape(w[n].shape), mn.reshape(w[n].shape),
                                                 vn.reshape(w[n].shape))
    land.update(late.landed([update_big(n) for n in LATE] + [deltas[0]], ("w_in",)))
    part["w_in"] = _partial_sum(g4["w_in"], land["w_in"], me_arr, name="partial_w_in")
    (sib["w_in"],) = _swap_with_sibling([part["w_in"]], name="swap_w_in")
    update_big("w_in")

    return (loss_total, grad_x[None], *[grad[n] for n in WEIGHT_NAMES], *[delta[n] for n in WEIGHT_NAMES],
            *[new_m[n] for n in WEIGHT_NAMES], *[new_v[n] for n in WEIGHT_NAMES])
```

```python
import functools
import math

import jax
import jax.numpy as jnp
from jax import lax
from jax.experimental import pallas as pl
from jax.experimental.pallas import tpu as pltpu

F32 = jnp.float32
_MXU = jnp.bfloat16

D_MODEL = 1024
SEQ = 2048
N_META = 16
CHUNK = 128
T_ROWS = 2176
N_CHUNKS = T_ROWS // CHUNK
PAD_ROWS = T_ROWS - SEQ - N_META
X_ROW0 = PAD_ROWS + N_META
SSD_HEADS = 16
SSD_HEAD_DIM = 64
SSD_STATE = 128
SSD_GROUPS = 2
SSD_HPG = SSD_HEADS // SSD_GROUPS
SSD_WIDTH = 1024
LRU_WIDTH = 1024
LRU_C = 8.0
D_FF = 2816
EPS = 1e-6
IN_COLS = 4624
N_SHARDS = 4
N_DEV = 8

ADAM_LR = 0.001
ADAM_B1 = 0.9
ADAM_B2 = 0.999
ADAM_EPS = 1e-08
ADAM_WD = 0.01
ADAM_STEP = 10

VMEM_LIMIT_BYTES = 56 * 1024 * 1024

NN = (((1,), (0,)), ((), ()))
NT = (((1,), (1,)), ((), ()))
TN = (((0,), (0,)), ((), ()))


def _cparams(*sem):
    return pltpu.CompilerParams(dimension_semantics=sem, vmem_limit_bytes=VMEM_LIMIT_BYTES)


def _dot(a, b, dims=NN):
    return lax.dot_general(a.astype(_MXU), b.astype(_MXU), dims, preferred_element_type=F32)


def _dot_onehot(a, b, dims=NN, *, data=0, pieces=3):
    ops = [a, b]
    mask = ops[1 - data].astype(jnp.bfloat16)
    rest = ops[data]
    acc = None
    for _ in range(pieces):
        piece = rest.astype(jnp.bfloat16)
        ops[data], ops[1 - data] = piece, mask
        d = lax.dot_general(ops[0], ops[1], dims, preferred_element_type=F32)
        acc = d if acc is None else acc + d
        rest = rest - piece.astype(F32)
    return acc


def _sigmoid(x):
    return 0.5 * (1.0 + jnp.tanh(0.5 * x))


def _softplus(x):
    return jnp.maximum(x, 0.0) + jnp.log(1.0 + jnp.exp(-jnp.abs(x)))


def _silu(x):
    return x * _sigmoid(x)


def _silu_grad(x):
    s = _sigmoid(x)
    return s * (1.0 + x * (1.0 - s))


_GELU_C = math.sqrt(2.0 / math.pi)


def _gelu_and_grad(x):
    inner = _GELU_C * (x + 0.044715 * x * x * x)
    t = jnp.tanh(inner)
    g = 0.5 * x * (1.0 + t)
    dg = 0.5 * (1.0 + t) + 0.5 * x * (1.0 - t * t) * _GELU_C * (1.0 + 3.0 * 0.044715 * x * x)
    return g, dg


def _rms_fwd(x, w):
    rstd = lax.rsqrt(jnp.mean(x * x, axis=-1, keepdims=True) + EPS)
    return x * rstd * w


def _rms_bwd(x, w, dy):
    rstd = lax.rsqrt(jnp.mean(x * x, axis=-1, keepdims=True) + EPS)
    xhat = x * rstd
    dxhat = dy * w
    dx = rstd * (dxhat - xhat * jnp.mean(dxhat * xhat, axis=-1, keepdims=True))
    return dx, dy * xhat


def _mm(terms, m, n, *, tm, tn, mode, out_dtype, name, residual=None, n_outer=False, also_mxu=False, behind=()):
    gm, gn = m // tm, n // tn
    assert gm * tm == m and gn * tn == n
    if n_outer:
        grid = (gn, gm)
        mi = lambda g0, g1: g1
        ni = lambda g0, g1: g0
    else:
        grid = (gm, gn)
        mi = lambda g0, g1: g0
        ni = lambda g0, g1: g1
    in_specs, args = [], []
    for (a, ka, b, kb, k) in terms:
        if mode == "tn":
            in_specs.append(pl.BlockSpec((k, tm), lambda g0, g1, ka=ka: (ka, mi(g0, g1))))
        else:
            in_specs.append(pl.BlockSpec((tm, k), lambda g0, g1, ka=ka: (mi(g0, g1), ka)))
        if mode == "nt":
            in_specs.append(pl.BlockSpec((tn, k), lambda g0, g1, kb=kb: (ni(g0, g1), kb)))
        else:
            in_specs.append(pl.BlockSpec((k, tn), lambda g0, g1, kb=kb: (kb, ni(g0, g1))))
        args += [a, b]
    if residual is not None:
        in_specs.append(pl.BlockSpec((tm, tn), lambda g0, g1: (mi(g0, g1), ni(g0, g1))))
        args.append(residual)
    dims = {"nn": NN, "nt": NT, "tn": TN}[mode]
    n_terms = len(terms)
    has_res = residual is not None
    in_specs += [pl.BlockSpec(memory_space=pl.ANY)] * len(behind)
    args += list(behind)
    n_in = len(args)

    def body(*refs):
        acc = None
        for t in range(n_terms):
            d = lax.dot_general(refs[2 * t][...], refs[2 * t + 1][...], dims, preferred_element_type=F32)
            acc = d if acc is None else acc + d
        if has_res:
            acc = acc + refs[2 * n_terms][...]
        refs[n_in][...] = acc.astype(out_dtype)
        if also_mxu:
            refs[n_in + 1][...] = acc.astype(_MXU)

    tile = pl.BlockSpec((tm, tn), lambda g0, g1: (mi(g0, g1), ni(g0, g1)))
    shape = jax.ShapeDtypeStruct((m, n), out_dtype)
    return pl.pallas_call(
        body, name=name, grid=grid, in_specs=in_specs,
        out_specs=[tile, tile] if also_mxu else tile,
        out_shape=[shape, jax.ShapeDtypeStruct((m, n), _MXU)] if also_mxu else shape,
        compiler_params=_cparams("parallel", "parallel"),
    )(*args)


def _embed(x, meta, behind=()):
    def body(x_ref, meta_ref, *rest):
        o_ref = rest[-1]
        i = pl.program_id(0)

        @pl.when(i == 0)
        def _():
            o_ref[0:PAD_ROWS, :] = jnp.zeros((PAD_ROWS, D_MODEL), F32)
            o_ref[PAD_ROWS:CHUNK, :] = meta_ref[...]

        @pl.when(i > 0)
        def _():
            o_ref[...] = x_ref[...]

    return pl.pallas_call(
        body, name="embed", grid=(N_CHUNKS,),
        in_specs=[pl.BlockSpec((CHUNK, D_MODEL), lambda i: (jnp.maximum(i - 1, 0), 0)),
                  pl.BlockSpec((N_META, D_MODEL), lambda i: (0, 0))] + [pl.BlockSpec(memory_space=pl.ANY)] * len(behind),
        out_specs=pl.BlockSpec((CHUNK, D_MODEL), lambda i: (i, 0)),
        out_shape=jax.ShapeDtypeStruct((T_ROWS, D_MODEL), F32),
        compiler_params=_cparams("parallel"),
    )(x, meta, *behind)


def _rmsnorm(h, w, *, name, tm=544):
    def body(h_ref, w_ref, o_ref):
        o_ref[...] = _rms_fwd(h_ref[...], w_ref[...]).astype(_MXU)

    return pl.pallas_call(
        body, name=name, grid=(T_ROWS // tm,),
        in_specs=[pl.BlockSpec((tm, D_MODEL), lambda i: (i, 0)), pl.BlockSpec((1, D_MODEL), lambda i: (0, 0))],
        out_specs=pl.BlockSpec((tm, D_MODEL), lambda i: (i, 0)),
        out_shape=jax.ShapeDtypeStruct((T_ROWS, D_MODEL), _MXU),
        compiler_params=_cparams("parallel"),
    )(h, w)


def _norm_proj(h, w, sections, *, name, tm=544):
    widths = [s.shape[0] for s in sections]
    n = len(sections)

    def body(*refs):
        h_ref, w_ref = refs[:2]
        u_ref = refs[2 + n]
        u = _rms_fwd(h_ref[...], w_ref[...]).astype(_MXU)
        u_ref[...] = u
        for k in range(n):
            refs[3 + n + k][...] = lax.dot_general(u, refs[2 + k][...], NT, preferred_element_type=F32)

    row = lambda width: pl.BlockSpec((tm, width), lambda i: (i, 0))
    outs = pl.pallas_call(
        body, name=name, grid=(T_ROWS // tm,),
        in_specs=[row(D_MODEL), pl.BlockSpec((1, D_MODEL), lambda i: (0, 0))]
        + [pl.BlockSpec((wd, D_MODEL), lambda i: (0, 0)) for wd in widths],
        out_specs=[row(D_MODEL)] + [row(wd) for wd in widths],
        out_shape=[jax.ShapeDtypeStruct((T_ROWS, D_MODEL), _MXU)]
        + [jax.ShapeDtypeStruct((T_ROWS, wd), F32) for wd in widths],
        compiler_params=_cparams("parallel"),
    )(h, w, *sections)
    return outs[0], list(outs[1:])


def _loss_head(h2, target, fw):
    def body(h_ref, t_ref, w_ref, loss_ref, dh_ref, dhb_ref, dw_ref, acc_ref):
        i = pl.program_id(0)

        @pl.when(i == 0)
        def _():
            acc_ref[...] = jnp.zeros_like(acc_ref)
            dw_ref[...] = jnp.zeros_like(dw_ref)

        h = h_ref[...]
        w = w_ref[...]
        y = _rms_fwd(h, w)
        live = (i > 0).astype(F32)
        err = (y - t_ref[...]) * live
        acc_ref[...] += jnp.sum(err * err, axis=0, keepdims=True)
        dy = err * (1.0 / D_MODEL)
        dx, dwr = _rms_bwd(h, w, dy)
        dh_ref[...] = dx
        dhb_ref[...] = dx.astype(_MXU)
        dw_ref[...] += jnp.sum(dwr, axis=0, keepdims=True)

        @pl.when(i == N_CHUNKS - 1)
        def _():
            tot = jnp.sum(acc_ref[...], axis=1, keepdims=True) * (0.5 / D_MODEL)
            loss_ref[...] = jnp.broadcast_to(tot, (1, 128))

    return pl.pallas_call(
        body, name="loss_head", grid=(N_CHUNKS,),
        in_specs=[pl.BlockSpec((CHUNK, D_MODEL), lambda i: (i, 0)),
                  pl.BlockSpec((CHUNK, D_MODEL), lambda i: (jnp.maximum(i - 1, 0), 0)),
                  pl.BlockSpec((1, D_MODEL), lambda i: (0, 0))],
        out_specs=[pl.BlockSpec((1, 128), lambda i: (0, 0)),
                   pl.BlockSpec((CHUNK, D_MODEL), lambda i: (i, 0)),
                   pl.BlockSpec((CHUNK, D_MODEL), lambda i: (i, 0)),
                   pl.BlockSpec((1, D_MODEL), lambda i: (0, 0))],
        out_shape=[jax.ShapeDtypeStruct((1, 128), F32),
                   jax.ShapeDtypeStruct((T_ROWS, D_MODEL), F32),
                   jax.ShapeDtypeStruct((T_ROWS, D_MODEL), _MXU),
                   jax.ShapeDtypeStruct((1, D_MODEL), F32)],
        scratch_shapes=[pltpu.VMEM((1, D_MODEL), F32)],
        compiler_params=_cparams("arbitrary"),
    )(h2, target, fw)


def _mm_norm_bwd(terms, h, w, dres, *, name, tm=272, behind=()):
    n_terms = len(terms)
    in_specs, args = [], []
    for (a, b, k) in terms:
        in_specs += [pl.BlockSpec((tm, k), lambda i: (i, 0)), pl.BlockSpec((k, D_MODEL), lambda i: (0, 0))]
        args += [a, b]
    in_specs += [pl.BlockSpec((tm, D_MODEL), lambda i: (i, 0)), pl.BlockSpec((1, D_MODEL), lambda i: (0, 0)),
                 pl.BlockSpec((tm, D_MODEL), lambda i: (i, 0))] + [pl.BlockSpec(memory_space=pl.ANY)] * len(behind)
    args += [h, w, dres, *behind]

    def body(*refs):
        h_ref, w_ref, dres_ref = refs[2 * n_terms:2 * n_terms + 3]
        dh_ref, dhb_ref, dw_ref = refs[2 * n_terms + 3 + len(behind):]

        @pl.when(pl.program_id(0) == 0)
        def _():
            dw_ref[...] = jnp.zeros_like(dw_ref)

        du = None
        for t in range(n_terms):
            d = lax.dot_general(refs[2 * t][...], refs[2 * t + 1][...], NN, preferred_element_type=F32)
            du = d if du is None else du + d
        dx, dwr = _rms_bwd(h_ref[...], w_ref[...], du)
        dh = dres_ref[...] + dx
        dh_ref[...] = dh
        dhb_ref[...] = dh.astype(_MXU)
        dw_ref[...] += jnp.sum(dwr, axis=0, keepdims=True)

    return pl.pallas_call(
        body, name=name, grid=(T_ROWS // tm,), in_specs=in_specs,
        out_specs=[pl.BlockSpec((tm, D_MODEL), lambda i: (i, 0)), pl.BlockSpec((tm, D_MODEL), lambda i: (i, 0)),
                   pl.BlockSpec((1, D_MODEL), lambda i: (0, 0))],
        out_shape=[jax.ShapeDtypeStruct((T_ROWS, D_MODEL), F32), jax.ShapeDtypeStruct((T_ROWS, D_MODEL), _MXU),
                   jax.ShapeDtypeStruct((1, D_MODEL), F32)],
        compiler_params=_cparams("arbitrary"),
    )(*args)


FFN_TM = T_ROWS
FFN_TN = 256


def _ffn_up(u2, wg_t, wu_t):
    def body(u_ref, wg_ref, wu_ref, gp_ref, up_ref, act_ref):
        u = u_ref[...]
        gp = lax.dot_general(u, wg_ref[...], NT, preferred_element_type=F32)
        up = lax.dot_general(u, wu_ref[...], NT, preferred_element_type=F32)
        gp_ref[...] = gp.astype(_MXU)
        up_ref[...] = up.astype(_MXU)
        act_ref[...] = (_silu(gp) * up).astype(_MXU)

    tile = pl.BlockSpec((FFN_TM, FFN_TN), lambda j, i: (i, j))
    return pl.pallas_call(
        body, name="ffn_up", grid=(D_FF // FFN_TN, T_ROWS // FFN_TM),
        in_specs=[pl.BlockSpec((FFN_TM, D_MODEL), lambda j, i: (i, 0)),
                  pl.BlockSpec((FFN_TN, D_MODEL), lambda j, i: (j, 0)),
                  pl.BlockSpec((FFN_TN, D_MODEL), lambda j, i: (j, 0))],
        out_specs=[tile, tile, tile],
        out_shape=[jax.ShapeDtypeStruct((T_ROWS, D_FF), _MXU)] * 3,
        compiler_params=_cparams("parallel", "parallel"),
    )(u2, wg_t, wu_t)


def _ffn_bwd_act(dh2b, wd, gp, up):
    def body(dh_ref, wd_ref, gp_ref, up_ref, dgp_ref, dup_ref):
        dact = lax.dot_general(dh_ref[...], wd_ref[...], NT, preferred_element_type=F32)
        gp = gp_ref[...].astype(F32)
        dgp_ref[...] = (dact * up_ref[...].astype(F32) * _silu_grad(gp)).astype(_MXU)
        dup_ref[...] = (dact * _silu(gp)).astype(_MXU)

    tile = pl.BlockSpec((FFN_TM, FFN_TN), lambda j, i: (i, j))
    return pl.pallas_call(
        body, name="ffn_bwd_act", grid=(D_FF // FFN_TN, T_ROWS // FFN_TM),
        in_specs=[pl.BlockSpec((FFN_TM, D_MODEL), lambda j, i: (i, 0)),
                  pl.BlockSpec((FFN_TN, D_MODEL), lambda j, i: (j, 0)), tile, tile],
        out_specs=[tile, tile],
        out_shape=[jax.ShapeDtypeStruct((T_ROWS, D_FF), _MXU), jax.ShapeDtypeStruct((T_ROWS, D_FF), _MXU)],
        compiler_params=_cparams("parallel", "parallel"),
    )(dh2b, wd, gp, up)


CONV_TC = 512
CONV_K = 4


def _conv_pre(x_ref, wv, bv, c):
    tc = wv.shape[1]
    r0 = c * CHUNK
    cur = x_ref[r0:r0 + CHUNK, :]
    if c == 0:
        cat = jnp.concatenate([jnp.zeros((8, tc), F32), cur], axis=0)
        shifted = [cur] + [pltpu.roll(cat, s, 0)[8:8 + CHUNK] for s in range(1, CONV_K)]
    else:
        shifted = [cur] + [x_ref[r0 - s:r0 - s + CHUNK, :] for s in range(1, CONV_K)]
    pre = bv
    for s in range(CONV_K):
        pre = pre + shifted[s] * wv[CONV_K - 1 - s:CONV_K - s]
    return pre, shifted


def _row_mask(c):
    if c > 0:
        return None
    return (lax.broadcasted_iota(jnp.int32, (CHUNK, 1), 0) >= PAD_ROWS).astype(F32)


def _conv_fwd(x, w, b, *, silu, name):
    cols = x.shape[1]
    tc = min(CONV_TC, cols)

    def body(x_ref, w_ref, b_ref, o_ref):
        wv, bv = w_ref[...], b_ref[...]
        for c in range(N_CHUNKS):
            pre, _ = _conv_pre(x_ref, wv, bv, c)
            y = _silu(pre) if silu else pre
            mask = _row_mask(c)
            if mask is not None:
                y = y * mask
            o_ref[c * CHUNK:(c + 1) * CHUNK, :] = y

    return pl.pallas_call(
        body, name=name, grid=(cols // tc,),
        in_specs=[pl.BlockSpec((T_ROWS, tc), lambda j: (0, j)), pl.BlockSpec((CONV_K, tc), lambda j: (0, j)),
                  pl.BlockSpec((1, tc), lambda j: (0, j))],
        out_specs=pl.BlockSpec((T_ROWS, tc), lambda j: (0, j)),
        out_shape=jax.ShapeDtypeStruct((T_ROWS, cols), F32),
        compiler_params=_cparams("parallel"),
    )(x, w, b)


def _conv_bwd(dy, x, w, b, *, silu, name):
    cols = x.shape[1]
    tc = min(CONV_TC, cols)

    def body(dy_ref, x_ref, w_ref, b_ref, dx_ref, dw_ref, db_ref):
        wv, bv = w_ref[...], b_ref[...]
        next8 = jnp.zeros((8, tc), F32)
        dws = [jnp.zeros((1, tc), F32) for _ in range(CONV_K)]
        db = jnp.zeros((1, tc), F32)
        for c in reversed(range(N_CHUNKS)):
            r0 = c * CHUNK
            pre, shifted = _conv_pre(x_ref, wv, bv, c)
            dpre = dy_ref[r0:r0 + CHUNK, :]
            if silu:
                dpre = dpre * _silu_grad(pre)
            mask = _row_mask(c)
            if mask is not None:
                dpre = dpre * mask
            cat = jnp.concatenate([dpre, next8], axis=0)
            dx = dpre * wv[CONV_K - 1:CONV_K]
            for s in range(1, CONV_K):
                dx = dx + pltpu.roll(cat, CHUNK + 8 - s, 0)[0:CHUNK] * wv[CONV_K - 1 - s:CONV_K - s]
            dx_ref[r0:r0 + CHUNK, :] = dx.astype(_MXU)
            for s in range(CONV_K):
                k = CONV_K - 1 - s
                dws[k] = dws[k] + jnp.sum(dpre * shifted[s], axis=0, keepdims=True)
            db = db + jnp.sum(dpre, axis=0, keepdims=True)
            next8 = dpre[0:8]
        dw_ref[...] = jnp.concatenate(dws, axis=0)
        db_ref[...] = db

    return pl.pallas_call(
        body, name=name, grid=(cols // tc,),
        in_specs=[pl.BlockSpec((T_ROWS, tc), lambda j: (0, j)), pl.BlockSpec((T_ROWS, tc), lambda j: (0, j)),
                  pl.BlockSpec((CONV_K, tc), lambda j: (0, j)), pl.BlockSpec((1, tc), lambda j: (0, j))],
        out_specs=[pl.BlockSpec((T_ROWS, tc), lambda j: (0, j)), pl.BlockSpec((CONV_K, tc), lambda j: (0, j)),
                   pl.BlockSpec((1, tc), lambda j: (0, j))],
        out_shape=[jax.ShapeDtypeStruct((T_ROWS, cols), _MXU), jax.ShapeDtypeStruct((CONV_K, cols), F32),
                   jax.ShapeDtypeStruct((1, cols), F32)],
        compiler_params=_cparams("parallel"),
    )(dy, x, w, b)


def _ssd_chunk_common(dt_raw, prm, c):
    a_row = -jnp.exp(prm[1:2])
    dt = _softplus(dt_raw + prm[0:1])
    rows = lax.broadcasted_iota(jnp.int32, (CHUNK, 1), 0)
    real = jnp.logical_or(c > 0, rows >= PAD_ROWS)
    dt = jnp.where(real, dt, 0.0)
    li = lax.broadcasted_iota(jnp.int32, (CHUNK, CHUNK), 0)
    si = lax.broadcasted_iota(jnp.int32, (CHUNK, CHUNK), 1)
    causal = li >= si
    tri = causal.astype(F32)
    cs = _dot_onehot(tri, dt * a_row, data=1)
    return dt, a_row, cs, cs.T, causal, tri, real


def _gated_norm_fwd(y, z, w):
    g = y * _silu(z)
    half = SSD_WIDTH // SSD_GROUPS
    outs = [_rms_fwd(g[:, k * half:(k + 1) * half], w[:, k * half:(k + 1) * half]) for k in range(SSD_GROUPS)]
    return jnp.concatenate(outs, axis=1)


GROUP_W = SSD_WIDTH // SSD_GROUPS
PAIR_W = 2 * SSD_HEAD_DIM
STATE_SHAPE = (SSD_GROUPS, SSD_STATE, GROUP_W)


def _head_expander():
    r = lax.broadcasted_iota(jnp.int32, (128, SSD_WIDTH), 0)
    c = lax.broadcasted_iota(jnp.int32, (128, SSD_WIDTH), 1)
    return (c // SSD_HEAD_DIM == r).astype(F32)


def _ssd_expand(dt, cs, prm, ex):
    cs_x = _dot_onehot(cs, ex)
    cs_last_x = cs_x[CHUNK - 1:CHUNK, :]
    return (_dot_onehot(dt, ex, pieces=2), _dot_onehot(prm, ex)[2:3], jnp.exp(cs_x), jnp.exp(cs_last_x),
            jnp.exp(cs_last_x - cs_x))


def _ssd_fwd(xs, bc, dt_raw, z, prm, norm_w, ex):
    def body(xs_ref, bc_ref, dt_ref, z_ref, prm_ref, nw_ref, ex_ref, y_ref, yn_ref, prev_ref, state):
        c = pl.program_id(0)

        @pl.when(c == 0)
        def _():
            state[...] = jnp.zeros_like(state)

        prm = prm_ref[...]
        dt, a_row, cs, cs_t, causal, _, _ = _ssd_chunk_common(dt_ref[...], prm, c)
        dt_x, d_x, e_cs_x, e_last_x, dec_x = _ssd_expand(dt, cs, prm, ex_ref[...])
        xs_all = xs_ref[...]
        bc_all = bc_ref[...]
        xdt = xs_all * dt_x
        xdec = xdt * dec_x
        lane_lo = lax.broadcasted_iota(jnp.int32, (1, PAIR_W), 1) < SSD_HEAD_DIM
        for g in range(SSD_GROUPS):
            gs = slice(g * GROUP_W, (g + 1) * GROUP_W)
            b_g = bc_all[:, g * SSD_STATE:(g + 1) * SSD_STATE]
            c_g = bc_all[:, (SSD_GROUPS + g) * SSD_STATE:(SSD_GROUPS + g + 1) * SSD_STATE]
            st = state[g]
            prev_ref[0, g] = st
            y_off = _dot(c_g, st) * e_cs_x[:, gs]
            state[g] = st * e_last_x[:, gs] + _dot(b_g.T, xdec[:, gs])
            cb = _dot(c_g, b_g, NT)
            for k in range(SSD_HPG // 2):
                h0 = g * SSD_HPG + 2 * k
                ps = slice(h0 * SSD_HEAD_DIM, h0 * SSD_HEAD_DIM + PAIR_W)
                xdt_pair = xdt[:, ps]
                yd = []
                for h in (h0, h0 + 1):
                    lmat = jnp.where(causal, jnp.exp(cs[:, h:h + 1] - cs_t[h:h + 1, :]), 0.0)
                    yd.append(_dot(cb * lmat, xdt_pair))
                y_ref[:, ps] = (jnp.where(lane_lo, yd[0], yd[1]) + y_off[:, k * PAIR_W:(k + 1) * PAIR_W]
                                + xs_all[:, ps] * d_x[:, ps])
        yn_ref[...] = _gated_norm_fwd(y_ref[...], z_ref[...], nw_ref[...]).astype(_MXU)

    row = lambda w: pl.BlockSpec((CHUNK, w), lambda c: (c, 0))
    return pl.pallas_call(
        body, name="ssd_fwd", grid=(N_CHUNKS,),
        in_specs=[row(SSD_WIDTH), row(512), row(128), row(SSD_WIDTH),
                  pl.BlockSpec((8, 128), lambda c: (0, 0)), pl.BlockSpec((1, SSD_WIDTH), lambda c: (0, 0)),
                  pl.BlockSpec((128, SSD_WIDTH), lambda c: (0, 0))],
        out_specs=[row(SSD_WIDTH), row(SSD_WIDTH),
                   pl.BlockSpec((1,) + STATE_SHAPE, lambda c: (c, 0, 0, 0))],
        out_shape=[jax.ShapeDtypeStruct((T_ROWS, SSD_WIDTH), F32), jax.ShapeDtypeStruct((T_ROWS, SSD_WIDTH), _MXU),
                   jax.ShapeDtypeStruct((N_CHUNKS,) + STATE_SHAPE, F32)],
        scratch_shapes=[pltpu.VMEM(STATE_SHAPE, F32)],
        compiler_params=_cparams("arbitrary"),
    )(xs, bc, dt_raw, z, prm, norm_w, ex)


def _ssd_bwd(dyn, dyn_block, z, y_pre, xs, bc, dt_raw, prev, prm, norm_w, ex):
    def body(dyn_ref, z_ref, y_ref, xs_ref, bc_ref, dt_ref, prev_ref, prm_ref, nw_ref, ex_ref,
             dz_ref, dxs_ref, dbc_ref, ddt_ref, dprm_ref, dnw_ref, dstate):
        step = pl.program_id(0)
        c = N_CHUNKS - 1 - step

        @pl.when(step == 0)
        def _():
            dstate[...] = jnp.zeros_like(dstate)
            dprm_ref[...] = jnp.zeros_like(dprm_ref)
            dnw_ref[...] = jnp.zeros_like(dnw_ref)

        prm = prm_ref[...]
        dt, a_row, cs, cs_t, causal, tri, real = _ssd_chunk_common(dt_ref[...], prm, c)
        realf = real.astype(F32)
        z = z_ref[...]
        y_all = y_ref[...]
        nw = nw_ref[...]
        dyn_all = dyn_ref[...]
        sz = _silu(z)
        gated = y_all * sz
        half = SSD_WIDTH // SSD_GROUPS
        dgs, dnws = [], []
        for k in range(SSD_GROUPS):
            sl = slice(k * half, (k + 1) * half)
            dgk, dwk = _rms_bwd(gated[:, sl], nw[:, sl], dyn_all[:, sl])
            dgs.append(dgk)
            dnws.append(jnp.sum(dwk, axis=0, keepdims=True))
        dgated = jnp.concatenate(dgs, axis=1)
        dnw_ref[...] += jnp.concatenate(dnws, axis=1)
        dz_ref[...] = (dgated * y_all * _silu_grad(z)).astype(_MXU)
        dy_all = dgated * sz

        ex = ex_ref[...]
        dt_x, d_x, e_cs_x, e_last_x, dec_x = _ssd_expand(dt, cs, prm, ex)
        xs_all = xs_ref[...]
        bc_all = bc_ref[...]
        xdt = xs_all * dt_x
        xdt_mxu = xdt.astype(_MXU).astype(F32)
        xdec = xdt * dec_x
        dcp = dy_all * e_cs_x
        lane_lo = lax.broadcasted_iota(jnp.int32, (1, PAIR_W), 1) < SSD_HEAD_DIM
        upper = (lax.broadcasted_iota(jnp.int32, (CHUNK, CHUNK), 0)
                 <= lax.broadcasted_iota(jnp.int32, (CHUNK, CHUNK), 1))
        last_row = (lax.broadcasted_iota(jnp.int32, (CHUNK, 1), 0) == CHUNK - 1).astype(F32)
        dbs, dcs_, dxdt_parts, last_parts = [], [], [], []
        for g in range(SSD_GROUPS):
            gs = slice(g * GROUP_W, (g + 1) * GROUP_W)
            b_g = bc_all[:, g * SSD_STATE:(g + 1) * SSD_STATE]
            c_g = bc_all[:, (SSD_GROUPS + g) * SSD_STATE:(SSD_GROUPS + g + 1) * SSD_STATE]
            prev_t = prev_ref[0, g]
            dst = dstate[g]
            dc_g = _dot(dcp[:, gs], prev_t, NT)
            db_g = _dot(xdec[:, gs], dst, NT)
            dxdt_state = _dot(b_g, dst) * dec_x[:, gs]
            dstate[g] = dst * e_last_x[:, gs] + _dot(c_g.T, dcp[:, gs])
            last_parts.append(jnp.sum(xdt_mxu[:, gs] * dxdt_state, axis=0, keepdims=True)
                              + jnp.sum(dst * prev_t, axis=0, keepdims=True) * e_last_x[:, gs])
            cb_t = _dot(b_g, c_g, NT)
            dcb_t = jnp.zeros((CHUNK, CHUNK), F32)
            for k in range(SSD_HPG // 2):
                h0 = g * SSD_HPG + 2 * k
                ps = slice(h0 * SSD_HEAD_DIM, h0 * SSD_HEAD_DIM + PAIR_W)
                dy_pair = dy_all[:, ps]
                xdt_pair = xdt[:, ps]
                dd = []
                for h in (h0, h0 + 1):
                    lmat_t = jnp.where(upper, jnp.exp(cs_t[h:h + 1, :] - cs[:, h:h + 1]), 0.0)
                    dd.append(_dot(cb_t * lmat_t, dy_pair))
                    mine = lane_lo if h == h0 else jnp.logical_not(lane_lo)
                    dcb_t = dcb_t + _dot(jnp.where(mine, xdt_pair, 0.0), dy_pair, NT) * lmat_t
                dxdt_parts.append(jnp.where(lane_lo, dd[0], dd[1]) + dxdt_state[:, k * PAIR_W:(k + 1) * PAIR_W])
            dc_g = dc_g + _dot(dcb_t, b_g, TN)
            db_g = db_g + _dot(dcb_t, c_g)
            dbs.append(db_g * realf)
            dcs_.append(dc_g * realf)
        dbc_ref[...] = jnp.concatenate(dbs + dcs_, axis=1)
        dxdt = jnp.concatenate(dxdt_parts, axis=1)
        dxs_ref[...] = (dxdt * dt_x + dy_all * d_x) * realf
        ddt_all = _dot_onehot(dxdt * xs_all, ex, NT, pieces=2)
        rows = jnp.concatenate([jnp.concatenate(last_parts, axis=1), jnp.sum(dy_all * xs_all, axis=0, keepdims=True),
                                jnp.zeros((6, SSD_WIDTH), F32)], axis=0)
        rows = _dot_onehot(rows, ex, NT, pieces=2)
        dd_row = rows[1:2]
        dy_mxu = dy_all.astype(_MXU).astype(F32)
        dcs_all = (_dot_onehot(dy_mxu * (y_all - xs_all * d_x), ex, NT) - _dot_onehot(xdt_mxu * dxdt, ex, NT)
                   + last_row * rows[0:1])
        dda = _dot_onehot(tri, dcs_all, TN, data=1)
        ddt = (ddt_all + dda * a_row) * realf
        ddt_raw = ddt * _sigmoid(dt_ref[...] + prm[0:1])
        ddt_ref[...] = ddt_raw.astype(_MXU)
        da_log = jnp.sum(dda * dt, axis=0, keepdims=True) * a_row
        dprm_ref[0:1, :] += jnp.sum(ddt_raw, axis=0, keepdims=True)
        dprm_ref[1:2, :] += da_log
        dprm_ref[2:3, :] += dd_row

    rev = lambda w, blk=0: pl.BlockSpec((CHUNK, w), lambda s, blk=blk: (N_CHUNKS - 1 - s, blk))
    return pl.pallas_call(
        body, name="ssd_bwd", grid=(N_CHUNKS,),
        in_specs=[rev(SSD_WIDTH, dyn_block), rev(SSD_WIDTH), rev(SSD_WIDTH), rev(SSD_WIDTH), rev(512), rev(128),
                  pl.BlockSpec((1,) + STATE_SHAPE, lambda s: (N_CHUNKS - 1 - s, 0, 0, 0)),
                  pl.BlockSpec((8, 128), lambda s: (0, 0)), pl.BlockSpec((1, SSD_WIDTH), lambda s: (0, 0)),
                  pl.BlockSpec((128, SSD_WIDTH), lambda s: (0, 0))],
        out_specs=[rev(SSD_WIDTH), rev(SSD_WIDTH), rev(512), rev(128),
                   pl.BlockSpec((8, 128), lambda s: (0, 0)), pl.BlockSpec((1, SSD_WIDTH), lambda s: (0, 0))],
        out_shape=[jax.ShapeDtypeStruct((T_ROWS, SSD_WIDTH), _MXU), jax.ShapeDtypeStruct((T_ROWS, SSD_WIDTH), F32),
                   jax.ShapeDtypeStruct((T_ROWS, 512), F32), jax.ShapeDtypeStruct((T_ROWS, 128), _MXU),
                   jax.ShapeDtypeStruct((8, 128), F32), jax.ShapeDtypeStruct((1, SSD_WIDTH), F32)],
        scratch_shapes=[pltpu.VMEM(STATE_SHAPE, F32)],
        compiler_params=_cparams("arbitrary"),
    )(dyn, z, y_pre, xs, bc, dt_raw, prev, prm, norm_w, ex)


LRU_PAIRS = 8


def _lru_gates(xr, wa_ref, wx_ref, prm):
    pre_r, pre_i = [], []
    for k in range(LRU_PAIRS):
        xk = xr[:, k * 128:(k + 1) * 128]
        pre_r.append(_dot(xk, wa_ref[k]))
        pre_i.append(_dot(xk, wx_ref[k]))
    r = _sigmoid(jnp.concatenate(pre_r, axis=1) + prm[0:1])
    i = _sigmoid(jnp.concatenate(pre_i, axis=1) + prm[1:2])
    sp = _softplus(-prm[2:3])
    log_a = (-LRU_C) * r * sp
    a = jnp.exp(log_a)
    s = jnp.sqrt(-jnp.tanh(log_a) * (a * a + 1.0))
    return r, i, a, s, sp


def _lru_fwd(xr, gate, wa, wx, prm):
    def body(xr_ref, g_ref, wa_ref, wx_ref, prm_ref, hs_ref, yn_ref, carry, a_s, u_s):
        @pl.when(pl.program_id(0) == 0)
        def _():
            carry[...] = jnp.zeros_like(carry)

        prm = prm_ref[...]
        xr_t = xr_ref[...]
        _, i, a, s, _ = _lru_gates(xr_t, wa_ref, wx_ref, prm)
        a_s[...] = a
        u_s[...] = s * (i * xr_t)
        rid = lax.broadcasted_iota(jnp.int32, (8, LRU_WIDTH), 0)

        def group(k, before):
            off = pl.multiple_of(k * 8, 8)
            a8 = a_s[pl.ds(off, 8), :]
            u8 = u_s[pl.ds(off, 8), :]
            for d in (1, 2, 4):
                keep = rid >= d
                u8 = u8 + a8 * jnp.where(keep, pltpu.roll(u8, d, 0), 0.0)
                a8 = a8 * jnp.where(keep, pltpu.roll(a8, d, 0), 1.0)
            h8 = u8 + a8 * before
            hs_ref[pl.ds(off, 8), :] = h8
            return jnp.broadcast_to(h8[7:8], (8, LRU_WIDTH))

        carry[...] = lax.fori_loop(0, CHUNK // 8, group, carry[...])
        gel, _ = _gelu_and_grad(g_ref[...])
        yn_ref[...] = _rms_fwd(gel * hs_ref[...], prm[3:4]).astype(_MXU)

    row = pl.BlockSpec((CHUNK, LRU_WIDTH), lambda t: (t, 0))
    wspec = pl.BlockSpec((LRU_PAIRS, 128, 128), lambda t: (0, 0, 0))
    return pl.pallas_call(
        body, name="lru_fwd", grid=(N_CHUNKS,),
        in_specs=[row, row, wspec, wspec, pl.BlockSpec((8, LRU_WIDTH), lambda t: (0, 0))],
        out_specs=[row, row],
        out_shape=[jax.ShapeDtypeStruct((T_ROWS, LRU_WIDTH), F32), jax.ShapeDtypeStruct((T_ROWS, LRU_WIDTH), _MXU)],
        scratch_shapes=[pltpu.VMEM((8, LRU_WIDTH), F32), pltpu.VMEM((CHUNK, LRU_WIDTH), F32),
                        pltpu.VMEM((CHUNK, LRU_WIDTH), F32)],
        compiler_params=_cparams("arbitrary"),
    )(xr, gate, wa, wx, prm)


def _lru_bwd(dyn, dyn_block, gate, xr, hs, wa, wx, wa_t, wx_t, prm):
    def body(dyn_ref, g_ref, xr_ref, hs_ref, hsp_ref, wa_ref, wx_ref, wat_ref, wxt_ref, prm_ref,
             dg_ref, dxr_ref, dwa_ref, dwx_ref, dprm_ref, carry, a_s, d_s):
        step = pl.program_id(0)
        tile = N_CHUNKS - 1 - step

        @pl.when(step == 0)
        def _():
            carry[...] = jnp.zeros_like(carry)
            dwa_ref[...] = jnp.zeros_like(dwa_ref)
            dwx_ref[...] = jnp.zeros_like(dwx_ref)
            dprm_ref[...] = jnp.zeros_like(dprm_ref)

        prm = prm_ref[...]
        xr_t = xr_ref[...]
        r, i, a, s, sp = _lru_gates(xr_t, wa_ref, wx_ref, prm)
        hs_t = hs_ref[...]
        gel, dgel = _gelu_and_grad(g_ref[...])
        dy, dnw = _rms_bwd(gel * hs_t, prm[3:4], dyn_ref[...])
        dg_ref[...] = (dy * hs_t * dgel).astype(_MXU)
        a_s[...] = a
        d_s[...] = dy * gel
        rid = lax.broadcasted_iota(jnp.int32, (8, LRU_WIDTH), 0)

        def group(k, behind):
            off = pl.multiple_of((CHUNK // 8 - 1 - k) * 8, 8)
            a8 = a_s[pl.ds(off, 8), :]
            d8 = d_s[pl.ds(off, 8), :]
            c8 = jnp.where(rid == 7, 1.0, pltpu.roll(a8, 7, 0))
            for d in (1, 2, 4):
                keep = rid < 8 - d
                d8 = d8 + c8 * jnp.where(keep, pltpu.roll(d8, 8 - d, 0), 0.0)
                c8 = c8 * jnp.where(keep, pltpu.roll(c8, 8 - d, 0), 1.0)
            dht8 = d8 + c8 * behind
            d_s[pl.ds(off, 8), :] = dht8
            return jnp.broadcast_to(a8[0:1] * dht8[0:1], (8, LRU_WIDTH))

        carry[...] = lax.fori_loop(0, CHUNK // 8, group, carry[...])
        dht = d_s[...]
        before = hsp_ref[CHUNK - 8:CHUNK, :][7:8] * (tile > 0).astype(F32)
        first = lax.broadcasted_iota(jnp.int32, (CHUNK, 1), 0) == 0
        hprev = jnp.where(first, before, pltpu.roll(hs_t, 1, 0))
        da = dht * hprev
        ixr = i * xr_t
        ds = dht * ixr
        dlog_a = da * a - ds * (a * a) * lax.rsqrt(s * s)
        dr = dlog_a * ((-LRU_C) * sp)
        dsp = jnp.sum(dlog_a * ((-LRU_C) * r), axis=0, keepdims=True)
        dlam = dsp * (-_sigmoid(-prm[2:3]))
        di = dht * s * xr_t
        dpre_r = dr * r * (1.0 - r)
        dpre_i = di * i * (1.0 - i)
        dxr = dht * s * i
        parts = []
        for k in range(LRU_PAIRS):
            sl = slice(k * 128, (k + 1) * 128)
            parts.append(_dot(dpre_r[:, sl], wat_ref[k]) + _dot(dpre_i[:, sl], wxt_ref[k]))
            dwa_ref[k] += _dot(xr_t[:, sl], dpre_r[:, sl], TN)
            dwx_ref[k] += _dot(xr_t[:, sl], dpre_i[:, sl], TN)
        dxr_ref[...] = dxr + jnp.concatenate(parts, axis=1)
        dprm_ref[0:1, :] += jnp.sum(dpre_r, axis=0, keepdims=True)
        dprm_ref[1:2, :] += jnp.sum(dpre_i, axis=0, keepdims=True)
        dprm_ref[2:3, :] += dlam
        dprm_ref[3:4, :] += jnp.sum(dnw, axis=0, keepdims=True)

    rev = lambda blk=0: pl.BlockSpec((CHUNK, LRU_WIDTH), lambda s, blk=blk: (N_CHUNKS - 1 - s, blk))
    wspec = pl.BlockSpec((LRU_PAIRS, 128, 128), lambda s: (0, 0, 0))
    return pl.pallas_call(
        body, name="lru_bwd", grid=(N_CHUNKS,),
        in_specs=[rev(dyn_block), rev(), rev(), rev(),
                  pl.BlockSpec((CHUNK, LRU_WIDTH), lambda s: (jnp.maximum(N_CHUNKS - 2 - s, 0), 0)),
                  wspec, wspec, wspec, wspec, pl.BlockSpec((8, LRU_WIDTH), lambda s: (0, 0))],
        out_specs=[rev(), rev(), wspec, wspec, pl.BlockSpec((8, LRU_WIDTH), lambda s: (0, 0))],
        out_shape=[jax.ShapeDtypeStruct((T_ROWS, LRU_WIDTH), _MXU), jax.ShapeDtypeStruct((T_ROWS, LRU_WIDTH), F32),
                   jax.ShapeDtypeStruct((LRU_PAIRS, 128, 128), F32), jax.ShapeDtypeStruct((LRU_PAIRS, 128, 128), F32),
                   jax.ShapeDtypeStruct((8, LRU_WIDTH), F32)],
        scratch_shapes=[pltpu.VMEM((8, LRU_WIDTH), F32), pltpu.VMEM((CHUNK, LRU_WIDTH), F32),
                        pltpu.VMEM((CHUNK, LRU_WIDTH), F32)],
        compiler_params=_cparams("arbitrary"),
    )(dyn, gate, xr, hs, hs, wa, wx, wa_t, wx_t, prm)


SEC_NAMES = ("z", "xs", "bc", "dt", "g", "x")
SEC_WIDTH = {"z": 1024, "xs": 1024, "bc": 512, "dt": 128, "g": 1024, "x": 1024}


def _pair_blocks(w):
    w = w.reshape(LRU_PAIRS, 2, 64, 64)
    zero = jnp.zeros((LRU_PAIRS, 64, 64), w.dtype)
    top = jnp.concatenate([w[:, 0], zero], axis=2)
    bot = jnp.concatenate([zero, w[:, 1]], axis=2)
    return jnp.concatenate([top, bot], axis=1)


def _unpair_blocks(wp):
    return jnp.stack([wp[:, :64, :64], wp[:, 64:, 64:]], axis=1).reshape(16, 64, 64)


def _pad_lanes(v, width=128):
    return jnp.pad(v, ((0, 0), (0, width - v.shape[1])))


class _Resident:
    before_embed = ()

    def __init__(self, w_in_sections, w_out, w_gate, w_up, w_down):
        self._w_in, self._w_out, self._ffn = w_in_sections, w_out, (w_gate, w_up, w_down)

    def w_in(self, after):
        return self._w_in

    def mid_forward(self, after):
        return jnp.zeros((1, 1), F32)

    def w_out(self, after):
        return self._w_out

    def ffn(self, after):
        return self._ffn

    def grads_ready(self, names, g, g_mxu):
        return jnp.zeros((1, 1), F32)

    def small_ready(self, g, loss):
        return jnp.zeros((1, 1), F32)

    def small_middle(self, after):
        return jnp.zeros((1, 1), F32)


def _local_step(x, target, meta, p, late):
    g, g_mxu = {}, {}
    ex = _head_expander()
    h0 = _embed(x, meta, late.before_embed)
    w_in = late.w_in(h0)
    u1, projs = _norm_proj(h0, p["norm1_w"], [w_in[s] for s in SEC_NAMES], name="norm_in_proj")
    proj = dict(zip(SEC_NAMES, projs))
    ssd_prm = jnp.concatenate([_pad_lanes(p["ssd_dt_bias"]), _pad_lanes(p["ssd_a_log"]), _pad_lanes(p["ssd_d"]),
                               jnp.zeros((5, 128), F32)], axis=0)
    xs_act = _conv_fwd(proj["xs"], p["ssd_conv_w"][:, :SSD_WIDTH], p["ssd_conv_b"][:, :SSD_WIDTH], silu=True,
                       name="ssd_conv_xs")
    bc_act = _conv_fwd(proj["bc"], p["ssd_conv_w"][:, SSD_WIDTH:], p["ssd_conv_b"][:, SSD_WIDTH:], silu=True,
                       name="ssd_conv_bc")
    y_pre, y_ssd, prev = _ssd_fwd(xs_act, bc_act, proj["dt"], proj["z"], ssd_prm, p["ssd_norm_w"], ex)
    xr = _conv_fwd(proj["x"], p["lru_conv_w"], p["lru_conv_b"], silu=False, name="lru_conv")
    wa_p, wx_p = _pair_blocks(p["lru_wa"]), _pair_blocks(p["lru_wx"])
    lru_prm = jnp.concatenate([p["lru_ba"], p["lru_bx"], p["lru_lambda"], p["lru_norm_w"],
                               jnp.zeros((4, LRU_WIDTH), F32)], axis=0)
    hs, y_lru = _lru_fwd(xr, proj["g"], wa_p.astype(_MXU), wx_p.astype(_MXU), lru_prm + late.mid_forward(xr))
    ycat = jnp.concatenate([y_ssd, y_lru], axis=1)
    w_out = late.w_out(ycat)
    h1 = _mm([(ycat, 0, w_out, 0, 2 * D_MODEL)], T_ROWS, D_MODEL, tm=T_ROWS, tn=256, mode="nn", out_dtype=F32,
             name="out_proj", residual=h0)
    u2 = _rmsnorm(h1, p["norm2_w"], name="norm2")
    w_gate, w_up, w_down = late.ffn(u2)
    gp, up, act = _ffn_up(u2, w_gate, w_up)
    h2 = _mm([(act, 0, w_down, 0, D_FF)], T_ROWS, D_MODEL, tm=T_ROWS, tn=256, mode="nn", out_dtype=F32,
             name="ffn_down", residual=h1)
    loss, dh2, dh2b, g["final_norm_w"] = _loss_head(h2, target, p["final_norm_w"])
    dgp, dup = _ffn_bwd_act(dh2b, w_down, gp, up)
    g["w_down"], g_mxu["w_down"] = _mm([(act, 0, dh2b, 0, T_ROWS)], D_FF, D_MODEL, tm=1408, tn=512, mode="tn",
                                       out_dtype=F32, name="dw_down", also_mxu=True)
    dh1, dh1b, g["norm2_w"] = _mm_norm_bwd([(dgp, w_gate, D_FF), (dup, w_up, D_FF)], h1, p["norm2_w"], dh2,
                                           name="ffn_bwd_in")
    g["w_gate"], g_mxu["w_gate"] = _mm([(dgp, 0, u2, 0, T_ROWS)], D_FF, D_MODEL, tm=1408, tn=512, mode="tn",
                                       out_dtype=F32, name="dw_gate", also_mxu=True)
    g["w_up"], g_mxu["w_up"] = _mm([(dup, 0, u2, 0, T_ROWS)], D_FF, D_MODEL, tm=1408, tn=512, mode="tn",
                                   out_dtype=F32, name="dw_up", also_mxu=True)
    g["w_out"], g_mxu["w_out"] = _mm([(ycat, 0, dh1b, 0, T_ROWS)], 2 * D_MODEL, D_MODEL, tm=1024, tn=512, mode="tn",
                                     out_dtype=F32, name="dw_out", also_mxu=True)
    sent = late.grads_ready(("w_down", "w_gate", "w_up", "w_out"), g, g_mxu)
    dycat = _mm([(dh1b, 0, w_out, 0, D_MODEL)], T_ROWS, 2 * D_MODEL, tm=T_ROWS, tn=256, mode="nt", out_dtype=F32,
                name="out_proj_bwd", behind=(sent,))
    dgate, dxr, dwa_p, dwx_p, dlru_prm = _lru_bwd(dycat, 1, proj["g"], xr, hs, wa_p.astype(_MXU), wx_p.astype(_MXU),
                                                  jnp.swapaxes(wa_p, 1, 2).astype(_MXU),
                                                  jnp.swapaxes(wx_p, 1, 2).astype(_MXU), lru_prm)
    g["lru_wa"], g["lru_wx"] = _unpair_blocks(dwa_p), _unpair_blocks(dwx_p)
    g["lru_ba"], g["lru_bx"], g["lru_lambda"], g["lru_norm_w"] = (dlru_prm[k:k + 1] for k in range(4))
    dx_lru, g["lru_conv_w"], g["lru_conv_b"] = _conv_bwd(dxr, proj["x"], p["lru_conv_w"], p["lru_conv_b"], silu=False,
                                                         name="lru_conv_bwd")
    dz, dxs_act, dbc_act, ddt, dssd_prm, g["ssd_norm_w"] = _ssd_bwd(dycat, 0, proj["z"], y_pre, xs_act, bc_act,
                                                                    proj["dt"], prev, ssd_prm, p["ssd_norm_w"], ex)
    g["ssd_dt_bias"], g["ssd_a_log"], g["ssd_d"] = (dssd_prm[k:k + 1, :SSD_HEADS] for k in range(3))
    dxs, dcw_xs, dcb_xs = _conv_bwd(dxs_act, proj["xs"], p["ssd_conv_w"][:, :SSD_WIDTH],
                                    p["ssd_conv_b"][:, :SSD_WIDTH], silu=True, name="ssd_conv_xs_bwd")
    dbc, dcw_bc, dcb_bc = _conv_bwd(dbc_act, proj["bc"], p["ssd_conv_w"][:, SSD_WIDTH:],
                                    p["ssd_conv_b"][:, SSD_WIDTH:], silu=True, name="ssd_conv_bc_bwd")
    g["ssd_conv_w"] = jnp.concatenate([dcw_xs, dcw_bc], axis=1)
    g["ssd_conv_b"] = jnp.concatenate([dcb_xs, dcb_bc], axis=1)
    dproj = {"z": dz, "xs": dxs, "bc": dbc, "dt": ddt, "g": dgate, "x": dx_lru}
    dh0, _, g["norm1_w"] = _mm_norm_bwd([(dproj[s], w_in[s], SEC_WIDTH[s]) for s in SEC_NAMES], h0,
                                        p["norm1_w"], dh1, name="in_proj_bwd")
    g["meta_tokens"] = dh0[PAD_ROWS:X_ROW0]
    sent = late.small_ready(g, loss)
    for s in SEC_NAMES:
        wdt = SEC_WIDTH[s]
        g["w_in_" + s], g_mxu["w_in_" + s] = _mm([(dproj[s], 0, u1, 0, T_ROWS)], wdt, D_MODEL, tm=min(wdt, 1024),
                                                 tn=512, mode="tn", out_dtype=F32, name="dw_in_" + s, also_mxu=True,
                                                 behind=(sent,))
        if s == "bc":
            sent = late.small_middle([g["w_in_z"], g["w_in_xs"], g["w_in_bc"]])
    late.grads_ready(("w_in",), g, g_mxu)
    return loss, dh0[X_ROW0:], g, g_mxu


MESH = pl.DeviceIdType.MESH
ANY = pl.BlockSpec(memory_space=pl.ANY)


def _my_place():
    return lax.axis_index("x"), lax.axis_index("y"), lax.axis_index("c")


def _other_chips(x, y):
    return [(1 - x, y), (x, 1 - y), (1 - x, 1 - y)]


HBM_SPEC = pl.BlockSpec(memory_space=pltpu.HBM)
SEM_SPEC = pl.BlockSpec(memory_space=pltpu.SEMAPHORE)
SPLIT_EFFECT = pltpu.SideEffectType.DATAFLOW_SIDE_EFFECTING


def _half_cols(buf, c, other=False):
    half = buf.shape[-1] // 2
    return pl.ds(pl.multiple_of(((1 - c) if other else c) * half, 128), half)


def _halves_plan(bufs, x, y, c, incoming):
    plan = []
    for buf in bufs:
        cols = _half_cols(buf, c)
        for (px, py) in _other_chips(x, y):
            slot = 2 * px + py if incoming else 2 * x + y
            plan.append((buf.at[2 * x + y, :, cols], buf.at[slot, :, cols], (px, py, c)))
    return plan


def _forward_plan(bufs, x, y, c, incoming):
    plan = []
    for buf in bufs:
        for (px, py) in _other_chips(x, y):
            slot = 2 * px + py
            plan.append((buf.at[slot, :, _half_cols(buf, c)], buf.at[slot, :, _half_cols(buf, c, other=incoming)],
                         (x, y, 1 - c)))
    return plan


def _scatter_plan(bufs, x, y, c, incoming):
    n = len(bufs) // 2
    plan = []
    for k in range(n):
        for j, (px, py) in enumerate(_other_chips(x, y)):
            plan.append((bufs[k].at[2 * px + py], bufs[n + k].at[j], (px, py, c)))
    return plan


def _split_start(bufs, plan, n_copies, after, *, name):
    n = len(bufs)
    extra = [] if after is None else [after]

    def body(*refs):
        ins = refs[:n]
        send_sems, recv_sems = refs[n + len(extra)], refs[n + len(extra) + 1]
        token = refs[-1]
        x, y, c = _my_place()
        for i, (src, dst, dev) in enumerate(plan(ins, x, y, c, False)):
            pltpu.make_async_remote_copy(src_ref=src, dst_ref=dst, send_sem=send_sems.at[i], recv_sem=recv_sems.at[i],
                                         device_id=dev, device_id_type=MESH).start()
        token[...] = jnp.zeros_like(token)

    outs = pl.pallas_call(
        body, name=name,
        out_shape=(pltpu.SemaphoreType.DMA((n_copies,)), pltpu.SemaphoreType.DMA((n_copies,)),
                   *[pltpu.HBM(b.shape, b.dtype) for b in bufs], jax.ShapeDtypeStruct((8, 128), F32)),
        in_specs=[HBM_SPEC] * n + [ANY] * len(extra),
        out_specs=(SEM_SPEC, SEM_SPEC, *[HBM_SPEC] * n, pl.BlockSpec(memory_space=pltpu.VMEM)),
        input_output_aliases={k: 2 + k for k in range(n)},
        compiler_params=pltpu.CompilerParams(has_side_effects=SPLIT_EFFECT),
    )(*[pltpu.with_memory_space_constraint(b, pltpu.HBM) for b in bufs], *extra)
    return outs[0], outs[1], list(outs[2:2 + n]), outs[-1]


def _split_wait(bufs, send_sems, recv_sems, plan, after, *, name):
    n = len(bufs)
    after = list(after) if isinstance(after, (list, tuple)) else [after]

    def body(*refs):
        ins = refs[:n]
        send_sems_ref, recv_sems_ref = refs[n], refs[n + 1]
        x, y, c = _my_place()
        for i, (src, dst, dev) in enumerate(plan(ins, x, y, c, True)):
            cp = pltpu.make_async_remote_copy(src_ref=src, dst_ref=dst, send_sem=send_sems_ref.at[i],
                                              recv_sem=recv_sems_ref.at[i], device_id=dev, device_id_type=MESH)
            cp.wait_send()
            cp.wait_recv()

    outs = pl.pallas_call(
        body, name=name, out_shape=tuple(pltpu.HBM(b.shape, b.dtype) for b in bufs),
        in_specs=[HBM_SPEC] * n + [SEM_SPEC, SEM_SPEC] + [ANY] * len(after), out_specs=tuple([HBM_SPEC] * n),
        input_output_aliases={k: k for k in range(n)},
        compiler_params=pltpu.CompilerParams(has_side_effects=SPLIT_EFFECT),
    )(*bufs, send_sems, recv_sems, *after)
    return list(outs)


def _fill_own_slots(shards, me_arr, *, name, behind=()):
    n = len(shards)
    n_in = n + len(behind)

    def body(me_ref, *refs):
        for k in range(n):
            refs[n_in + k][0] = refs[k][...].astype(_MXU)

    half = D_MODEL // 2
    return pl.pallas_call(
        body, name=name,
        grid_spec=pltpu.PrefetchScalarGridSpec(
            num_scalar_prefetch=1, grid=(2,),
            in_specs=[pl.BlockSpec((s.shape[0], half), lambda i, me: (0, i)) for s in shards]
            + [pl.BlockSpec(memory_space=pl.ANY)] * len(behind),
            out_specs=[pl.BlockSpec((1, s.shape[0], half), lambda i, me: (me[0], 0, i)) for s in shards]),
        out_shape=[jax.ShapeDtypeStruct((N_SHARDS,) + s.shape, _MXU) for s in shards],
        compiler_params=_cparams("parallel"),
    )(me_arr, *shards, *behind)


def _gather_small(small):
    def body(s_ref, o_ref, send_sems, recv_sems, local_sem):
        x, y, c = _my_place()
        me = 2 * x + y
        local = pltpu.make_async_copy(s_ref, o_ref.at[me], local_sem)
        local.start()
        copies = [(pltpu.make_async_remote_copy(src_ref=s_ref, dst_ref=o_ref.at[me], send_sem=send_sems.at[j],
                                                recv_sem=recv_sems.at[j], device_id=(px, py, c), device_id_type=MESH),
                   2 * px + py) for j, (px, py) in enumerate(_other_chips(x, y))]
        for cp, _ in copies:
            cp.start()
        for j, (cp, slot) in enumerate(copies):
            cp.wait_send()
            pltpu.make_async_remote_copy(src_ref=s_ref, dst_ref=o_ref.at[slot], send_sem=send_sems.at[j],
                                         recv_sem=recv_sems.at[j], device_id=(x, y, c),
                                         device_id_type=MESH).wait_recv()
        local.wait()

    return pl.pallas_call(
        body, name="gather_small", in_specs=[ANY], out_specs=ANY,
        out_shape=jax.ShapeDtypeStruct((N_SHARDS,) + small.shape, small.dtype),
        scratch_shapes=[pltpu.SemaphoreType.DMA((3,)), pltpu.SemaphoreType.DMA((3,)), pltpu.SemaphoreType.DMA],
    )(small)


def _swap_with_sibling(parts, *, name):
    n = len(parts)

    def body(*refs):
        ins, outs = refs[:n], refs[n:2 * n]
        send_sems, recv_sems = refs[2 * n:]
        x, y, c = _my_place()
        copies = [pltpu.make_async_remote_copy(
            src_ref=ins[k], dst_ref=outs[k], send_sem=send_sems.at[k], recv_sem=recv_sems.at[k],
            device_id=(x, y, 1 - c), device_id_type=MESH) for k in range(n)]
        for cp in copies:
            cp.start()
        for cp in copies:
            cp.wait()

    return pl.pallas_call(
        body, name=name, in_specs=[ANY] * n, out_specs=[ANY] * n,
        out_shape=[jax.ShapeDtypeStruct(a.shape, a.dtype) for a in parts],
        scratch_shapes=[pltpu.SemaphoreType.DMA((n,)), pltpu.SemaphoreType.DMA((n,))],
    )(*parts)


def _other_devices(x, y, c):
    out = []
    for mask in range(1, N_DEV):
        px, py, pc = x ^ (mask >> 2 & 1), y ^ (mask >> 1 & 1), c ^ (mask & 1)
        out.append(((px, py, pc), 4 * px + 2 * py + pc))
    return out


def _pieces_plan(bufs, x, y, c, incoming):
    pack, land = bufs
    me = 4 * x + 2 * y + c
    return [(pack.at[num], land.at[num if incoming else me], dev) for dev, num in _other_devices(x, y, c)]


def _spread_plan(bufs, x, y, c, incoming):
    piece, land = bufs
    me = 4 * x + 2 * y + c
    return [(piece, land.at[num if incoming else me], dev) for dev, num in _other_devices(x, y, c)]


def _sum_pieces(pack, land, dev_arr, *, name):
    def body(dev_ref, pack_ref, land_ref, o_ref):
        dev = dev_ref[0]
        own = pack_ref[dev]
        acc = None
        for d in range(N_DEV):
            term = jnp.where(dev == d, own, land_ref[d])
            acc = term if acc is None else acc + term
        o_ref[...] = acc

    vmem = pl.BlockSpec(memory_space=pltpu.VMEM)
    return pl.pallas_call(
        body, name=name, in_specs=[pl.BlockSpec(memory_space=pltpu.SMEM), vmem, vmem], out_specs=vmem,
        out_shape=jax.ShapeDtypeStruct(pack.shape[1:], F32),
    )(dev_arr, pack, land)


def _join_pieces(piece, land, dev_arr, *, name):
    def body(dev_ref, piece_ref, land_ref, o_ref):
        dev = dev_ref[0]
        for d in range(N_DEV):
            o_ref[d] = jnp.where(dev == d, piece_ref[...], land_ref[d])

    vmem = pl.BlockSpec(memory_space=pltpu.VMEM)
    return pl.pallas_call(
        body, name=name, in_specs=[pl.BlockSpec(memory_space=pltpu.SMEM), vmem, vmem], out_specs=vmem,
        out_shape=jax.ShapeDtypeStruct(land.shape, F32),
    )(dev_arr, piece, land)


def _adamw_native(ws, gs, ms, vs):
    n = len(ws)

    def body(*refs):
        for k in range(n):
            w_ref, g_ref, m_ref, v_ref = (refs[j * n + k] for j in range(4))
            delta, m_new, v_new = _adamw_math(w_ref[...], g_ref[...], m_ref[...], v_ref[...])
            refs[4 * n + k][...] = delta
            refs[5 * n + k][...] = m_new
            refs[6 * n + k][...] = v_new

    vmem = pl.BlockSpec(memory_space=pltpu.VMEM)
    shapes = [jax.ShapeDtypeStruct(a.shape, F32) for a in ws]
    outs = pl.pallas_call(
        body, name="adamw_small", in_specs=[vmem] * (4 * n), out_specs=[vmem] * (3 * n), out_shape=shapes * 3,
        compiler_params=pltpu.CompilerParams(vmem_limit_bytes=VMEM_LIMIT_BYTES),
    )(*ws, *gs, *ms, *vs)
    return outs[:n], outs[n:2 * n], outs[2 * n:]


def _elementwise_tile(rows, cols):
    for t in range(256, 15, -16):
        if rows % t == 0:
            return (t, cols), rows // t, lambda i: (i, 0)
    assert cols % 256 == 0
    return (rows, 256), cols // 256, lambda i: (0, i)


def _partial_sum(own, land, me_arr, *, name):
    r, c = own.shape[-2:]
    tile, steps, imap = _elementwise_tile(r, c)
    whole = own.ndim == 3

    def body(me_ref, own_ref, land_ref, o_ref):
        acc = own_ref[0] if whole else own_ref[...]
        for j in range(3):
            acc = acc + land_ref[j].astype(F32)
        o_ref[...] = acc

    own_spec = (pl.BlockSpec((1,) + tile, lambda i, me: (me[0],) + imap(i)) if whole
                else pl.BlockSpec(tile, lambda i, me: imap(i)))
    return pl.pallas_call(
        body, name=name,
        grid_spec=pltpu.PrefetchScalarGridSpec(
            num_scalar_prefetch=1, grid=(steps,),
            in_specs=[own_spec, pl.BlockSpec((3,) + tile, lambda i, me: (0,) + imap(i))],
            out_specs=pl.BlockSpec(tile, lambda i, me: imap(i))),
        out_shape=jax.ShapeDtypeStruct((r, c), F32),
        compiler_params=_cparams("parallel"),
    )(me_arr, own, land)


def _adamw_math(w, g, m, v):
    m = ADAM_B1 * m + (1.0 - ADAM_B1) * g
    v = ADAM_B2 * v + (1.0 - ADAM_B2) * (g * g)
    m_hat = m / (1.0 - ADAM_B1 ** ADAM_STEP)
    v_hat = v / (1.0 - ADAM_B2 ** ADAM_STEP)
    delta = -ADAM_LR * (m_hat / (jnp.sqrt(v_hat) + ADAM_EPS) + ADAM_WD * w)
    return delta, m, v


def _adamw(w, grad_parts, m, v, *, name):
    r, c = w.shape
    tile_shape, steps, imap = _elementwise_tile(r, c)
    n = len(grad_parts)

    def body(*refs):
        w_ref, m_ref, v_ref = refs[:3]
        g_refs = refs[3:3 + n]
        g_out, d_out, m_out, v_out = refs[3 + n:]
        g = g_refs[0][...]
        for k in range(1, n):
            g = g + g_refs[k][...]
        delta, m_new, v_new = _adamw_math(w_ref[...], g, m_ref[...], v_ref[...])
        g_out[...] = g
        d_out[...] = delta
        m_out[...] = m_new
        v_out[...] = v_new

    tile = pl.BlockSpec(tile_shape, imap)
    return pl.pallas_call(
        body, name=name, grid=(steps,), in_specs=[tile] * (3 + n), out_specs=[tile] * 4,
        out_shape=[jax.ShapeDtypeStruct((r, c), F32)] * 4,
        compiler_params=_cparams("parallel"),
    )(w, m, v, *grad_parts)


WEIGHT_NAMES = ("meta_tokens", "norm1_w", "w_in", "ssd_conv_w", "ssd_conv_b", "ssd_dt_bias", "ssd_a_log", "ssd_d",
                "ssd_norm_w", "lru_conv_w", "lru_conv_b", "lru_wa", "lru_ba", "lru_wx", "lru_bx", "lru_lambda",
                "lru_norm_w", "w_out", "norm2_w", "w_gate", "w_up", "w_down", "final_norm_w")
BIG = ("w_in", "w_out", "w_gate", "w_up", "w_down")
FFN = ("w_gate", "w_up", "w_down")
LATE = ("w_out",) + FFN
SMALL_SHARDED = {"meta_tokens": (N_META, D_MODEL), "ssd_conv_w": (CONV_K, 1536), "lru_conv_w": (CONV_K, LRU_WIDTH)}
SMALL = tuple(n for n in WEIGHT_NAMES if n not in BIG)
PACK_COLS = 1024


def _pack(arrays, row_multiple):
    flat = jnp.concatenate([a.reshape(-1) for a in arrays])
    rows = -(-flat.shape[0] // (row_multiple * PACK_COLS)) * row_multiple
    return jnp.pad(flat, (0, rows * PACK_COLS - flat.shape[0])).reshape(rows, PACK_COLS)


def _unpack(pack, shapes):
    flat = pack.reshape(-1)
    out, off = [], 0
    for s in shapes:
        size = math.prod(s)
        out.append(flat[off:off + size].reshape(s))
        off += size
    return out


def _unshard_cols(g4):
    return jnp.swapaxes(g4, 0, 1).reshape(g4.shape[1], -1)


COL_SHARDED = ("w_in", "w_gate", "w_up")
IN_ROWS = {"z": (0, 1024), "xs": (1024, 2048), "bc": (2048, 2560), "dt": (2560, 2576), "g": (2576, 3600),
           "x": (3600, IN_COLS)}


def _rows_of_shards(shards4, lo, hi):
    r = shards4.shape[1]
    parts = [shards4[k, max(lo, k * r) - k * r:min(hi, (k + 1) * r) - k * r]
             for k in range(N_SHARDS) if max(lo, k * r) < min(hi, (k + 1) * r)]
    return parts[0] if len(parts) == 1 else jnp.concatenate(parts, axis=0)


def _w_in_shard_rows(k, sections):
    lo, hi = k * (IN_COLS // N_SHARDS), (k + 1) * (IN_COLS // N_SHARDS)
    parts = []
    for arr, (a, b) in zip(sections, IN_ROWS.values()):
        if max(lo, a) < min(hi, b):
            parts.append(arr[max(lo, a) - a:min(hi, b) - a])
    return jnp.concatenate(parts, axis=0)


def _rows_view(name, block):
    return jnp.swapaxes(block[0], 0, 1) if name in COL_SHARDED else block[0]


def _param_view(name, rows):
    return (jnp.swapaxes(rows, 0, 1) if name in COL_SHARDED else rows)[None]


def kernel(x, meta_tokens, norm1_w, w_in, ssd_conv_w, ssd_conv_b, ssd_dt_bias, ssd_a_log, ssd_d, ssd_norm_w, lru_conv_w, lru_conv_b, lru_wa, lru_ba, lru_wx, lru_bx, lru_lambda, lru_norm_w, w_out, norm2_w, w_gate, w_up, w_down, final_norm_w, loss_target, m_meta_tokens, m_norm1_w, m_w_in, m_ssd_conv_w, m_ssd_conv_b, m_ssd_dt_bias, m_ssd_a_log, m_ssd_d, m_ssd_norm_w, m_lru_conv_w, m_lru_conv_b, m_lru_wa, m_lru_ba, m_lru_wx, m_lru_bx, m_lru_lambda, m_lru_norm_w, m_w_out, m_norm2_w, m_w_gate, m_w_up, m_w_down, m_final_norm_w, v_meta_tokens, v_norm1_w, v_w_in, v_ssd_conv_w, v_ssd_conv_b, v_ssd_dt_bias, v_ssd_a_log, v_ssd_d, v_ssd_norm_w, v_lru_conv_w, v_lru_conv_b, v_lru_wa, v_lru_ba, v_lru_wx, v_lru_bx, v_lru_lambda, v_lru_norm_w, v_w_out, v_norm2_w, v_w_gate, v_w_up, v_w_down, v_final_norm_w):
    w = dict(zip(WEIGHT_NAMES, (meta_tokens, norm1_w, w_in, ssd_conv_w, ssd_conv_b, ssd_dt_bias, ssd_a_log, ssd_d, ssd_norm_w, lru_conv_w, lru_conv_b, lru_wa, lru_ba, lru_wx, lru_bx, lru_lambda, lru_norm_w, w_out, norm2_w, w_gate, w_up, w_down, final_norm_w)))
    m = dict(zip(WEIGHT_NAMES, (m_meta_tokens, m_norm1_w, m_w_in, m_ssd_conv_w, m_ssd_conv_b, m_ssd_dt_bias, m_ssd_a_log, m_ssd_d, m_ssd_norm_w, m_lru_conv_w, m_lru_conv_b, m_lru_wa, m_lru_ba, m_lru_wx, m_lru_bx, m_lru_lambda, m_lru_norm_w, m_w_out, m_norm2_w, m_w_gate, m_w_up, m_w_down, m_final_norm_w)))
    v = dict(zip(WEIGHT_NAMES, (v_meta_tokens, v_norm1_w, v_w_in, v_ssd_conv_w, v_ssd_conv_b, v_ssd_dt_bias, v_ssd_a_log, v_ssd_d, v_ssd_norm_w, v_lru_conv_w, v_lru_conv_b, v_lru_wa, v_lru_ba, v_lru_wx, v_lru_bx, v_lru_lambda, v_lru_norm_w, v_w_out, v_norm2_w, v_w_gate, v_w_up, v_w_down, v_final_norm_w)))
    me = 2 * lax.axis_index("x") + lax.axis_index("y")

    big2d = {n: _rows_view(n, w[n]) for n in BIG}
    small_local = jnp.concatenate([w["meta_tokens"].reshape(-1), w["ssd_conv_w"].reshape(-1),
                                   w["lru_conv_w"].reshape(-1)])[None]
    me_arr = me.astype(jnp.int32).reshape(1)
    dev_arr = (2 * me + lax.axis_index("c")).astype(jnp.int32).reshape(1)
    small4 = _gather_small(small_local)
    (w_in_slot,) = _fill_own_slots([big2d["w_in"]], me_arr, name="own_slot_w_in")
    in_send, in_recv, in_bufs, in_tok = _split_start([w_in_slot], _halves_plan, 3, small4, name="gather_w_in_start")
    late_slots = _fill_own_slots([big2d[n] for n in LATE], me_arr, name="own_slots_late", behind=(in_tok,))
    sm = small4[:, 0]
    meta_full = _unshard_cols(sm[:, :4096].reshape(N_SHARDS, N_META, 256))
    ssd_conv_w_full = _unshard_cols(sm[:, 4096:5632].reshape(N_SHARDS, CONV_K, 384))
    lru_conv_w_full = _unshard_cols(sm[:, 5632:].reshape(N_SHARDS, CONV_K, 256))

    p = {"ssd_conv_w": ssd_conv_w_full, "lru_conv_w": lru_conv_w_full,
         "lru_wa": w["lru_wa"][0], "lru_wx": w["lru_wx"][0], "final_norm_w": w["final_norm_w"][None]}
    for n in ("norm1_w", "ssd_conv_b", "ssd_dt_bias", "ssd_a_log", "ssd_d", "ssd_norm_w", "lru_conv_b", "lru_ba",
              "lru_bx", "lru_lambda", "lru_norm_w", "norm2_w"):
        p[n] = w[n]

    class Late:
        def __init__(self):
            self.pending = []
            self.before_embed = (late_slots[0],)

        def w_in(self, after):
            (buf,) = _split_wait(in_bufs, in_send, in_recv, _halves_plan, after, name="gather_w_in_wait")
            send, recv, bufs, tok = _split_start([buf], _forward_plan, 3, None, name="forward_w_in_start")
            self.late_gather = _split_start(late_slots, _halves_plan, 3 * len(LATE), tok, name="gather_late_start")
            (w_in4,) = _split_wait(bufs, send, recv, _forward_plan, self.late_gather[2][0], name="forward_w_in_wait")
            sections = {s: _rows_of_shards(w_in4, lo, hi) for s, (lo, hi) in IN_ROWS.items()}
            sections["dt"] = jnp.pad(sections["dt"], ((0, SEC_WIDTH["dt"] - SSD_HEADS), (0, 0)))
            return sections

        def mid_forward(self, after):
            send, recv, bufs, _ = self.late_gather
            bufs = _split_wait(bufs, send, recv, _halves_plan, after, name="gather_late_wait")
            self.forward = _split_start(bufs, _forward_plan, 3 * len(LATE), None, name="forward_late_start")
            return self.forward[3][:1, :1]

        def w_out(self, after):
            send, recv, bufs, _ = self.forward
            bufs = _split_wait(bufs, send, recv, _forward_plan, after, name="forward_late_wait")
            self.late = dict(zip(LATE, (b.reshape(-1, D_MODEL) for b in bufs)))
            return self.late["w_out"]

        def ffn(self, after):
            return tuple(self.late[n] for n in FFN)

        def grads_ready(self, names, g, g_mxu):
            if names == ("w_in",):
                g_mxu["w_in"] = jnp.stack([_w_in_shard_rows(k, [g_mxu["w_in_" + s] for s in SEC_NAMES])
                                           for k in range(N_SHARDS)])
            srcs = [g_mxu[n].reshape(N_SHARDS, -1, D_MODEL) for n in names]
            lands = [lax.empty((3,) + s.shape[1:], _MXU) for s in srcs]
            tag = "_".join(names)
            send, recv, bufs, tok = _split_start(srcs + lands, _scatter_plan, 3 * len(names), None,
                                                 name="scatter_" + tag + "_start")
            self.pending.append((names, send, recv, bufs, tag))
            self.in_flight = bufs[0]
            return tok[:1, :1]

        def landed(self, after, which):
            land = {}
            for names, send, recv, bufs, tag in self.pending:
                if names[0] in which:
                    bufs = _split_wait(bufs, send, recv, _scatter_plan, after, name="scatter_" + tag + "_wait")
                    land.update(zip(names, bufs[len(names):]))
            return land

        def small_ready(self, g, loss):
            pack = _pack([g[n] for n in SMALL] + [loss[0, :1]], 8 * N_DEV)
            pack = pack.reshape(N_DEV, -1, PACK_COLS)
            self.small = _split_start([pack, lax.empty(pack.shape, F32)], _pieces_plan, N_DEV - 1, loss,
                                      name="small_pieces_start")
            return self.small[3]

        def small_middle(self, after):
            send, recv, bufs, _ = self.small
            pack, land = _split_wait(bufs, send, recv, _pieces_plan, after, name="small_pieces_wait")
            piece = _sum_pieces(pack, land, dev_arr, name="small_pieces_sum")
            self.small = _split_start([piece, lax.empty(pack.shape, F32)], _spread_plan, N_DEV - 1, None,
                                      name="small_spread_start")
            return self.small[3]

        def small_sum(self, after):
            send, recv, bufs, _ = self.small
            piece, land = _split_wait(bufs, send, recv, _spread_plan, after, name="small_spread_wait")
            return _join_pieces(piece, land, dev_arr, name="small_join")

    late = Late()

    loss, grad_x, g, g_mxu = _local_step(x[0], loss_target[0], meta_full, p, late)

    g4 = {n: g[n].reshape(N_SHARDS, -1, D_MODEL) for n in LATE}
    g4["w_in"] = lax.switch(me, [functools.partial(_w_in_shard_rows, k) for k in range(N_SHARDS)],
                            [g["w_in_" + s] for s in SEC_NAMES])
    land = late.landed(late.in_flight, LATE)
    part = {n: _partial_sum(g4[n], land[n], me_arr, name="partial_" + n) for n in LATE}
    sib = dict(zip(LATE, _swap_with_sibling([part[n] for n in LATE], name="swap_late")))

    small_full_shape = {n: (SMALL_SHARDED[n] if n in SMALL_SHARDED else w[n].shape) for n in SMALL}
    red_list = _unpack(late.small_sum(sib["w_out"]), [small_full_shape[n] for n in SMALL] + [(1,)])
    loss_total = red_list[-1][0]
    g_small = {}
    for n, arr in zip(SMALL, red_list[:-1]):
        if n in SMALL_SHARDED:
            cols = SMALL_SHARDED[n][1] // N_SHARDS
            arr = lax.dynamic_slice_in_dim(arr, me * cols, cols, axis=1)
        g_small[n] = arr.reshape(w[n].shape)

    grad, delta, new_m, new_v = {}, {}, {}, {}

    def update_big(n):
        outs = _adamw(big2d[n], [part[n], sib[n]], _rows_view(n, m[n]), _rows_view(n, v[n]), name="adamw_" + n)
        grad[n], delta[n], new_m[n], new_v[n] = (_param_view(n, o) for o in outs)
        return outs[0]

    two_d = lambda a: a.reshape(1, -1) if a.ndim == 1 else a
    deltas, new_ms, new_vs = _adamw_native(*[[two_d(d[n]) for n in SMALL] for d in (w, g_small, m, v)])
    for n, dn, mn, vn in zip(SMALL, deltas, new_ms, new_vs):
        grad[n], delta[n], new_m[n], new_v[n] = (g_small[n], dn.reshape(w[n].shape), mn.reshape(w[n].shape),
                                                 vn.reshape(w[n].shape))
    land.update(late.landed([update_big(n) for n in LATE] + [deltas[0]], ("w_in",)))
    part["w_in"] = _partial_sum(g4["w_in"], land["w_in"], me_arr, name="partial_w_in")
    (sib["w_in"],) = _swap_with_sibling([part["w_in"]], name="swap_w_in")
    update_big("w_in")

    return (loss_total, grad_x[None], *[grad[n] for n in WEIGHT_NAMES], *[delta[n] for n in WEIGHT_NAMES],
            *[new_m[n] for n in WEIGHT_NAMES], *[new_v[n] for n in WEIGHT_NAMES])
```

```python
import functools
import math

import jax
import jax.numpy as jnp
from jax import lax
from jax.experimental import pallas as pl
from jax.experimental.pallas import tpu as pltpu

F32 = jnp.float32
_MXU = jnp.bfloat16

D_MODEL = 1024
SEQ = 2048
N_META = 16
CHUNK = 128
T_ROWS = 2176
N_CHUNKS = T_ROWS // CHUNK
PAD_ROWS = T_ROWS - SEQ - N_META
X_ROW0 = PAD_ROWS + N_META
SSD_HEADS = 16
SSD_HEAD_DIM = 64
SSD_STATE = 128
SSD_GROUPS = 2
SSD_HPG = SSD_HEADS // SSD_GROUPS
SSD_WIDTH = 1024
LRU_WIDTH = 1024
LRU_C = 8.0
D_FF = 2816
EPS = 1e-6
IN_COLS = 4624
N_SHARDS = 4
N_DEV = 8

ADAM_LR = 0.001
ADAM_B1 = 0.9
ADAM_B2 = 0.999
ADAM_EPS = 1e-08
ADAM_WD = 0.01
ADAM_STEP = 10

VMEM_LIMIT_BYTES = 56 * 1024 * 1024

NN = (((1,), (0,)), ((), ()))
NT = (((1,), (1,)), ((), ()))
TN = (((0,), (0,)), ((), ()))


def _cparams(*sem):
    return pltpu.CompilerParams(dimension_semantics=sem, vmem_limit_bytes=VMEM_LIMIT_BYTES)


def _dot(a, b, dims=NN):
    return lax.dot_general(a.astype(_MXU), b.astype(_MXU), dims, preferred_element_type=F32)


def _dot_onehot(a, b, dims=NN, *, data=0, pieces=3):
    ops = [a, b]
    mask = ops[1 - data].astype(jnp.bfloat16)
    rest = ops[data]
    acc = None
    for _ in range(pieces):
        piece = rest.astype(jnp.bfloat16)
        ops[data], ops[1 - data] = piece, mask
        d = lax.dot_general(ops[0], ops[1], dims, preferred_element_type=F32)
        acc = d if acc is None else acc + d
        rest = rest - piece.astype(F32)
    return acc


def _sigmoid(x):
    return 0.5 * (1.0 + jnp.tanh(0.5 * x))


def _softplus(x):
    return jnp.maximum(x, 0.0) + jnp.log(1.0 + jnp.exp(-jnp.abs(x)))


def _silu(x):
    return x * _sigmoid(x)


def _silu_grad(x):
    s = _sigmoid(x)
    return s * (1.0 + x * (1.0 - s))


_GELU_C = math.sqrt(2.0 / math.pi)


def _gelu_and_grad(x):
    inner = _GELU_C * (x + 0.044715 * x * x * x)
    t = jnp.tanh(inner)
    g = 0.5 * x * (1.0 + t)
    dg = 0.5 * (1.0 + t) + 0.5 * x * (1.0 - t * t) * _GELU_C * (1.0 + 3.0 * 0.044715 * x * x)
    return g, dg


def _rms_fwd(x, w):
    rstd = lax.rsqrt(jnp.mean(x * x, axis=-1, keepdims=True) + EPS)
    return x * rstd * w


def _rms_bwd(x, w, dy):
    rstd = lax.rsqrt(jnp.mean(x * x, axis=-1, keepdims=True) + EPS)
    xhat = x * rstd
    dxhat = dy * w
    dx = rstd * (dxhat - xhat * jnp.mean(dxhat * xhat, axis=-1, keepdims=True))
    return dx, dy * xhat


def _mm(terms, m, n, *, tm, tn, mode, out_dtype, name, residual=None, n_outer=False, also_mxu=False, behind=()):
    gm, gn = m // tm, n // tn
    assert gm * tm == m and gn * tn == n
    if n_outer:
        grid = (gn, gm)
        mi = lambda g0, g1: g1
        ni = lambda g0, g1: g0
    else:
        grid = (gm, gn)
        mi = lambda g0, g1: g0
        ni = lambda g0, g1: g1
    in_specs, args = [], []
    for (a, ka, b, kb, k) in terms:
        if mode == "tn":
            in_specs.append(pl.BlockSpec((k, tm), lambda g0, g1, ka=ka: (ka, mi(g0, g1))))
        else:
            in_specs.append(pl.BlockSpec((tm, k), lambda g0, g1, ka=ka: (mi(g0, g1), ka)))
        if mode == "nt":
            in_specs.append(pl.BlockSpec((tn, k), lambda g0, g1, kb=kb: (ni(g0, g1), kb)))
        else:
            in_specs.append(pl.BlockSpec((k, tn), lambda g0, g1, kb=kb: (kb, ni(g0, g1))))
        args += [a, b]
    if residual is not None:
        in_specs.append(pl.BlockSpec((tm, tn), lambda g0, g1: (mi(g0, g1), ni(g0, g1))))
        args.append(residual)
    dims = {"nn": NN, "nt": NT, "tn": TN}[mode]
    n_terms = len(terms)
    has_res = residual is not None
    in_specs += [pl.BlockSpec(memory_space=pl.ANY)] * len(behind)
    args += list(behind)
    n_in = len(args)

    def body(*refs):
        acc = None
        for t in range(n_terms):
            d = lax.dot_general(refs[2 * t][...], refs[2 * t + 1][...], dims, preferred_element_type=F32)
            acc = d if acc is None else acc + d
        if has_res:
            acc = acc + refs[2 * n_terms][...]
        refs[n_in][...] = acc.astype(out_dtype)
        if also_mxu:
            refs[n_in + 1][...] = acc.astype(_MXU)

    tile = pl.BlockSpec((tm, tn), lambda g0, g1: (mi(g0, g1), ni(g0, g1)))
    shape = jax.ShapeDtypeStruct((m, n), out_dtype)
    return pl.pallas_call(
        body, name=name, grid=grid, in_specs=in_specs,
        out_specs=[tile, tile] if also_mxu else tile,
        out_shape=[shape, jax.ShapeDtypeStruct((m, n), _MXU)] if also_mxu else shape,
        compiler_params=_cparams("parallel", "parallel"),
    )(*args)


def _embed(x, meta, behind=()):
    def body(x_ref, meta_ref, *rest):
        o_ref = rest[-1]
        i = pl.program_id(0)

        @pl.when(i == 0)
        def _():
            o_ref[0:PAD_ROWS, :] = jnp.zeros((PAD_ROWS, D_MODEL), F32)
            o_ref[PAD_ROWS:CHUNK, :] = meta_ref[...]

        @pl.when(i > 0)
        def _():
            o_ref[...] = x_ref[...]

    return pl.pallas_call(
        body, name="embed", grid=(N_CHUNKS,),
        in_specs=[pl.BlockSpec((CHUNK, D_MODEL), lambda i: (jnp.maximum(i - 1, 0), 0)),
                  pl.BlockSpec((N_META, D_MODEL), lambda i: (0, 0))] + [pl.BlockSpec(memory_space=pl.ANY)] * len(behind),
        out_specs=pl.BlockSpec((CHUNK, D_MODEL), lambda i: (i, 0)),
        out_shape=jax.ShapeDtypeStruct((T_ROWS, D_MODEL), F32),
        compiler_params=_cparams("parallel"),
    )(x, meta, *behind)


def _rmsnorm(h, w, *, name, tm=544):
    def body(h_ref, w_ref, o_ref):
        o_ref[...] = _rms_fwd(h_ref[...], w_ref[...]).astype(_MXU)

    return pl.pallas_call(
        body, name=name, grid=(T_ROWS // tm,),
        in_specs=[pl.BlockSpec((tm, D_MODEL), lambda i: (i, 0)), pl.BlockSpec((1, D_MODEL), lambda i: (0, 0))],
        out_specs=pl.BlockSpec((tm, D_MODEL), lambda i: (i, 0)),
        out_shape=jax.ShapeDtypeStruct((T_ROWS, D_MODEL), _MXU),
        compiler_params=_cparams("parallel"),
    )(h, w)


def _norm_proj(h, w, sections, *, name, tm=544):
    widths = [s.shape[0] for s in sections]
    n = len(sections)

    def body(*refs):
        h_ref, w_ref = refs[:2]
        u_ref = refs[2 + n]
        u = _rms_fwd(h_ref[...], w_ref[...]).astype(_MXU)
        u_ref[...] = u
        for k in range(n):
            refs[3 + n + k][...] = lax.dot_general(u, refs[2 + k][...], NT, preferred_element_type=F32)

    row = lambda width: pl.BlockSpec((tm, width), lambda i: (i, 0))
    outs = pl.pallas_call(
        body, name=name, grid=(T_ROWS // tm,),
        in_specs=[row(D_MODEL), pl.BlockSpec((1, D_MODEL), lambda i: (0, 0))]
        + [pl.BlockSpec((wd, D_MODEL), lambda i: (0, 0)) for wd in widths],
        out_specs=[row(D_MODEL)] + [row(wd) for wd in widths],
        out_shape=[jax.ShapeDtypeStruct((T_ROWS, D_MODEL), _MXU)]
        + [jax.ShapeDtypeStruct((T_ROWS, wd), F32) for wd in widths],
        compiler_params=_cparams("parallel"),
    )(h, w, *sections)
    return outs[0], list(outs[1:])


def _loss_head(h2, target, fw):
    def body(h_ref, t_ref, w_ref, loss_ref, dh_ref, dhb_ref, dw_ref, acc_ref):
        i = pl.program_id(0)

        @pl.when(i == 0)
        def _():
            acc_ref[...] = jnp.zeros_like(acc_ref)
            dw_ref[...] = jnp.zeros_like(dw_ref)

        h = h_ref[...]
        w = w_ref[...]
        y = _rms_fwd(h, w)
        live = (i > 0).astype(F32)
        err = (y - t_ref[...]) * live
        acc_ref[...] += jnp.sum(err * err, axis=0, keepdims=True)
        dy = err * (1.0 / D_MODEL)
        dx, dwr = _rms_bwd(h, w, dy)
        dh_ref[...] = dx
        dhb_ref[...] = dx.astype(_MXU)
        dw_ref[...] += jnp.sum(dwr, axis=0, keepdims=True)

        @pl.when(i == N_CHUNKS - 1)
        def _():
            tot = jnp.sum(acc_ref[...], axis=1, keepdims=True) * (0.5 / D_MODEL)
            loss_ref[...] = jnp.broadcast_to(tot, (1, 128))

    return pl.pallas_call(
        body, name="loss_head", grid=(N_CHUNKS,),
        in_specs=[pl.BlockSpec((CHUNK, D_MODEL), lambda i: (i, 0)),
                  pl.BlockSpec((CHUNK, D_MODEL), lambda i: (jnp.maximum(i - 1, 0), 0)),
                  pl.BlockSpec((1, D_MODEL), lambda i: (0, 0))],
        out_specs=[pl.BlockSpec((1, 128), lambda i: (0, 0)),
                   pl.BlockSpec((CHUNK, D_MODEL), lambda i: (i, 0)),
                   pl.BlockSpec((CHUNK, D_MODEL), lambda i: (i, 0)),
                   pl.BlockSpec((1, D_MODEL), lambda i: (0, 0))],
        out_shape=[jax.ShapeDtypeStruct((1, 128), F32),
                   jax.ShapeDtypeStruct((T_ROWS, D_MODEL), F32),
                   jax.ShapeDtypeStruct((T_ROWS, D_MODEL), _MXU),
                   jax.ShapeDtypeStruct((1, D_MODEL), F32)],
        scratch_shapes=[pltpu.VMEM((1, D_MODEL), F32)],
        compiler_params=_cparams("arbitrary"),
    )(h2, target, fw)


def _mm_norm_bwd(terms, h, w, dres, *, name, tm=272, behind=()):
    n_terms = len(terms)
    in_specs, args = [], []
    for (a, b, k) in terms:
        in_specs += [pl.BlockSpec((tm, k), lambda i: (i, 0)), pl.BlockSpec((k, D_MODEL), lambda i: (0, 0))]
        args += [a, b]
    in_specs += [pl.BlockSpec((tm, D_MODEL), lambda i: (i, 0)), pl.BlockSpec((1, D_MODEL), lambda i: (0, 0)),
                 pl.BlockSpec((tm, D_MODEL), lambda i: (i, 0))] + [pl.BlockSpec(memory_space=pl.ANY)] * len(behind)
    args += [h, w, dres, *behind]

    def body(*refs):
        h_ref, w_ref, dres_ref = refs[2 * n_terms:2 * n_terms + 3]
        dh_ref, dhb_ref, dw_ref = refs[2 * n_terms + 3 + len(behind):]

        @pl.when(pl.program_id(0) == 0)
        def _():
            dw_ref[...] = jnp.zeros_like(dw_ref)

        du = None
        for t in range(n_terms):
            d = lax.dot_general(refs[2 * t][...], refs[2 * t + 1][...], NN, preferred_element_type=F32)
            du = d if du is None else du + d
        dx, dwr = _rms_bwd(h_ref[...], w_ref[...], du)
        dh = dres_ref[...] + dx
        dh_ref[...] = dh
        dhb_ref[...] = dh.astype(_MXU)
        dw_ref[...] += jnp.sum(dwr, axis=0, keepdims=True)

    return pl.pallas_call(
        body, name=name, grid=(T_ROWS // tm,), in_specs=in_specs,
        out_specs=[pl.BlockSpec((tm, D_MODEL), lambda i: (i, 0)), pl.BlockSpec((tm, D_MODEL), lambda i: (i, 0)),
                   pl.BlockSpec((1, D_MODEL), lambda i: (0, 0))],
        out_shape=[jax.ShapeDtypeStruct((T_ROWS, D_MODEL), F32), jax.ShapeDtypeStruct((T_ROWS, D_MODEL), _MXU),
                   jax.ShapeDtypeStruct((1, D_MODEL), F32)],
        compiler_params=_cparams("arbitrary"),
    )(*args)


FFN_TM = T_ROWS
FFN_TN = 256


def _ffn_up(u2, wg_t, wu_t):
    def body(u_ref, wg_ref, wu_ref, gp_ref, up_ref, act_ref):
        u = u_ref[...]
        gp = lax.dot_general(u, wg_ref[...], NT, preferred_element_type=F32)
        up = lax.dot_general(u, wu_ref[...], NT, preferred_element_type=F32)
        gp_ref[...] = gp.astype(_MXU)
        up_ref[...] = up.astype(_MXU)
        act_ref[...] = (_silu(gp) * up).astype(_MXU)

    tile = pl.BlockSpec((FFN_TM, FFN_TN), lambda j, i: (i, j))
    return pl.pallas_call(
        body, name="ffn_up", grid=(D_FF // FFN_TN, T_ROWS // FFN_TM),
        in_specs=[pl.BlockSpec((FFN_TM, D_MODEL), lambda j, i: (i, 0)),
                  pl.BlockSpec((FFN_TN, D_MODEL), lambda j, i: (j, 0)),
                  pl.BlockSpec((FFN_TN, D_MODEL), lambda j, i: (j, 0))],
        out_specs=[tile, tile, tile],
        out_shape=[jax.ShapeDtypeStruct((T_ROWS, D_FF), _MXU)] * 3,
        compiler_params=_cparams("parallel", "parallel"),
    )(u2, wg_t, wu_t)


def _ffn_bwd_act(dh2b, wd, gp, up):
    def body(dh_ref, wd_ref, gp_ref, up_ref, dgp_ref, dup_ref):
        dact = lax.dot_general(dh_ref[...], wd_ref[...], NT, preferred_element_type=F32)
        gp = gp_ref[...].astype(F32)
        dgp_ref[...] = (dact * up_ref[...].astype(F32) * _silu_grad(gp)).astype(_MXU)
        dup_ref[...] = (dact * _silu(gp)).astype(_MXU)

    tile = pl.BlockSpec((FFN_TM, FFN_TN), lambda j, i: (i, j))
    return pl.pallas_call(
        body, name="ffn_bwd_act", grid=(D_FF // FFN_TN, T_ROWS // FFN_TM),
        in_specs=[pl.BlockSpec((FFN_TM, D_MODEL), lambda j, i: (i, 0)),
                  pl.BlockSpec((FFN_TN, D_MODEL), lambda j, i: (j, 0)), tile, tile],
        out_specs=[tile, tile],
        out_shape=[jax.ShapeDtypeStruct((T_ROWS, D_FF), _MXU), jax.ShapeDtypeStruct((T_ROWS, D_FF), _MXU)],
        compiler_params=_cparams("parallel", "parallel"),
    )(dh2b, wd, gp, up)


CONV_TC = 512
CONV_K = 4


def _conv_pre(x_ref, wv, bv, c):
    tc = wv.shape[1]
    r0 = c * CHUNK
    cur = x_ref[r0:r0 + CHUNK, :]
    if c == 0:
        cat = jnp.concatenate([jnp.zeros((8, tc), F32), cur], axis=0)
        shifted = [cur] + [pltpu.roll(cat, s, 0)[8:8 + CHUNK] for s in range(1, CONV_K)]
    else:
        shifted = [cur] + [x_ref[r0 - s:r0 - s + CHUNK, :] for s in range(1, CONV_K)]
    pre = bv
    for s in range(CONV_K):
        pre = pre + shifted[s] * wv[CONV_K - 1 - s:CONV_K - s]
    return pre, shifted


def _row_mask(c):
    if c > 0:
        return None
    return (lax.broadcasted_iota(jnp.int32, (CHUNK, 1), 0) >= PAD_ROWS).astype(F32)


def _conv_fwd(x, w, b, *, silu, name):
    cols = x.shape[1]
    tc = min(CONV_TC, cols)

    def body(x_ref, w_ref, b_ref, o_ref):
        wv, bv = w_ref[...], b_ref[...]
        for c in range(N_CHUNKS):
            pre, _ = _conv_pre(x_ref, wv, bv, c)
            y = _silu(pre) if silu else pre
            mask = _row_mask(c)
            if mask is not None:
                y = y * mask
            o_ref[c * CHUNK:(c + 1) * CHUNK, :] = y

    return pl.pallas_call(
        body, name=name, grid=(cols // tc,),
        in_specs=[pl.BlockSpec((T_ROWS, tc), lambda j: (0, j)), pl.BlockSpec((CONV_K, tc), lambda j: (0, j)),
                  pl.BlockSpec((1, tc), lambda j: (0, j))],
        out_specs=pl.BlockSpec((T_ROWS, tc), lambda j: (0, j)),
        out_shape=jax.ShapeDtypeStruct((T_ROWS, cols), F32),
        compiler_params=_cparams("parallel"),
    )(x, w, b)


def _conv_bwd(dy, x, w, b, *, silu, name):
    cols = x.shape[1]
    tc = min(CONV_TC, cols)

    def body(dy_ref, x_ref, w_ref, b_ref, dx_ref, dw_ref, db_ref):
        wv, bv = w_ref[...], b_ref[...]
        next8 = jnp.zeros((8, tc), F32)
        dws = [jnp.zeros((1, tc), F32) for _ in range(CONV_K)]
        db = jnp.zeros((1, tc), F32)
        for c in reversed(range(N_CHUNKS)):
            r0 = c * CHUNK
            pre, shifted = _conv_pre(x_ref, wv, bv, c)
            dpre = dy_ref[r0:r0 + CHUNK, :]
            if silu:
                dpre = dpre * _silu_grad(pre)
            mask = _row_mask(c)
            if mask is not None:
                dpre = dpre * mask
            cat = jnp.concatenate([dpre, next8], axis=0)
            dx = dpre * wv[CONV_K - 1:CONV_K]
            for s in range(1, CONV_K):
                dx = dx + pltpu.roll(cat, CHUNK + 8 - s, 0)[0:CHUNK] * wv[CONV_K - 1 - s:CONV_K - s]
            dx_ref[r0:r0 + CHUNK, :] = dx.astype(_MXU)
            for s in range(CONV_K):
                k = CONV_K - 1 - s
                dws[k] = dws[k] + jnp.sum(dpre * shifted[s], axis=0, keepdims=True)
            db = db + jnp.sum(dpre, axis=0, keepdims=True)
            next8 = dpre[0:8]
        dw_ref[...] = jnp.concatenate(dws, axis=0)
        db_ref[...] = db

    return pl.pallas_call(
        body, name=name, grid=(cols // tc,),
        in_specs=[pl.BlockSpec((T_ROWS, tc), lambda j: (0, j)), pl.BlockSpec((T_ROWS, tc), lambda j: (0, j)),
                  pl.BlockSpec((CONV_K, tc), lambda j: (0, j)), pl.BlockSpec((1, tc), lambda j: (0, j))],
        out_specs=[pl.BlockSpec((T_ROWS, tc), lambda j: (0, j)), pl.BlockSpec((CONV_K, tc), lambda j: (0, j)),
                   pl.BlockSpec((1, tc), lambda j: (0, j))],
        out_shape=[jax.ShapeDtypeStruct((T_ROWS, cols), _MXU), jax.ShapeDtypeStruct((CONV_K, cols), F32),
                   jax.ShapeDtypeStruct((1, cols), F32)],
        compiler_params=_cparams("parallel"),
    )(dy, x, w, b)


def _ssd_chunk_common(dt_raw, prm, c):
    a_row = -jnp.exp(prm[1:2])
    dt = _softplus(dt_raw + prm[0:1])
    rows = lax.broadcasted_iota(jnp.int32, (CHUNK, 1), 0)
    real = jnp.logical_or(c > 0, rows >= PAD_ROWS)
    dt = jnp.where(real, dt, 0.0)
    li = lax.broadcasted_iota(jnp.int32, (CHUNK, CHUNK), 0)
    si = lax.broadcasted_iota(jnp.int32, (CHUNK, CHUNK), 1)
    causal = li >= si
    tri = causal.astype(F32)
    cs = _dot_onehot(tri, dt * a_row, data=1)
    return dt, a_row, cs, cs.T, causal, tri, real


def _gated_norm_fwd(y, z, w):
    g = y * _silu(z)
    half = SSD_WIDTH // SSD_GROUPS
    outs = [_rms_fwd(g[:, k * half:(k + 1) * half], w[:, k * half:(k + 1) * half]) for k in range(SSD_GROUPS)]
    return jnp.concatenate(outs, axis=1)


GROUP_W = SSD_WIDTH // SSD_GROUPS
PAIR_W = 2 * SSD_HEAD_DIM
STATE_SHAPE = (SSD_GROUPS, SSD_STATE, GROUP_W)


def _head_expander():
    r = lax.broadcasted_iota(jnp.int32, (128, SSD_WIDTH), 0)
    c = lax.broadcasted_iota(jnp.int32, (128, SSD_WIDTH), 1)
    return (c // SSD_HEAD_DIM == r).astype(F32)


def _ssd_expand(dt, cs, prm, ex):
    cs_x = _dot_onehot(cs, ex)
    cs_last_x = cs_x[CHUNK - 1:CHUNK, :]
    return (_dot_onehot(dt, ex, pieces=2), _dot_onehot(prm, ex)[2:3], jnp.exp(cs_x), jnp.exp(cs_last_x),
            jnp.exp(cs_last_x - cs_x))


def _ssd_fwd(xs, bc, dt_raw, z, prm, norm_w, ex):
    def body(xs_ref, bc_ref, dt_ref, z_ref, prm_ref, nw_ref, ex_ref, y_ref, yn_ref, prev_ref, state):
        c = pl.program_id(0)

        @pl.when(c == 0)
        def _():
            state[...] = jnp.zeros_like(state)

        prm = prm_ref[...]
        dt, a_row, cs, cs_t, causal, _, _ = _ssd_chunk_common(dt_ref[...], prm, c)
        dt_x, d_x, e_cs_x, e_last_x, dec_x = _ssd_expand(dt, cs, prm, ex_ref[...])
        xs_all = xs_ref[...]
        bc_all = bc_ref[...]
        xdt = xs_all * dt_x
        xdec = xdt * dec_x
        lane_lo = lax.broadcasted_iota(jnp.int32, (1, PAIR_W), 1) < SSD_HEAD_DIM
        for g in range(SSD_GROUPS):
            gs = slice(g * GROUP_W, (g + 1) * GROUP_W)
            b_g = bc_all[:, g * SSD_STATE:(g + 1) * SSD_STATE]
            c_g = bc_all[:, (SSD_GROUPS + g) * SSD_STATE:(SSD_GROUPS + g + 1) * SSD_STATE]
            st = state[g]
            prev_ref[0, g] = st
            y_off = _dot(c_g, st) * e_cs_x[:, gs]
            state[g] = st * e_last_x[:, gs] + _dot(b_g.T, xdec[:, gs])
            cb = _dot(c_g, b_g, NT)
            for k in range(SSD_HPG // 2):
                h0 = g * SSD_HPG + 2 * k
                ps = slice(h0 * SSD_HEAD_DIM, h0 * SSD_HEAD_DIM + PAIR_W)
                xdt_pair = xdt[:, ps]
                yd = []
                for h in (h0, h0 + 1):
                    lmat = jnp.where(causal, jnp.exp(cs[:, h:h + 1] - cs_t[h:h + 1, :]), 0.0)
                    yd.append(_dot(cb * lmat, xdt_pair))
                y_ref[:, ps] = (jnp.where(lane_lo, yd[0], yd[1]) + y_off[:, k * PAIR_W:(k + 1) * PAIR_W]
                                + xs_all[:, ps] * d_x[:, ps])
        yn_ref[...] = _gated_norm_fwd(y_ref[...], z_ref[...], nw_ref[...]).astype(_MXU)

    row = lambda w: pl.BlockSpec((CHUNK, w), lambda c: (c, 0))
    return pl.pallas_call(
        body, name="ssd_fwd", grid=(N_CHUNKS,),
        in_specs=[row(SSD_WIDTH), row(512), row(128), row(SSD_WIDTH),
                  pl.BlockSpec((8, 128), lambda c: (0, 0)), pl.BlockSpec((1, SSD_WIDTH), lambda c: (0, 0)),
                  pl.BlockSpec((128, SSD_WIDTH), lambda c: (0, 0))],
        out_specs=[row(SSD_WIDTH), row(SSD_WIDTH),
                   pl.BlockSpec((1,) + STATE_SHAPE, lambda c: (c, 0, 0, 0))],
        out_shape=[jax.ShapeDtypeStruct((T_ROWS, SSD_WIDTH), F32), jax.ShapeDtypeStruct((T_ROWS, SSD_WIDTH), _MXU),
                   jax.ShapeDtypeStruct((N_CHUNKS,) + STATE_SHAPE, F32)],
        scratch_shapes=[pltpu.VMEM(STATE_SHAPE, F32)],
        compiler_params=_cparams("arbitrary"),
    )(xs, bc, dt_raw, z, prm, norm_w, ex)


def _ssd_bwd(dyn, dyn_block, z, y_pre, xs, bc, dt_raw, prev, prm, norm_w, ex):
    def body(dyn_ref, z_ref, y_ref, xs_ref, bc_ref, dt_ref, prev_ref, prm_ref, nw_ref, ex_ref,
             dz_ref, dxs_ref, dbc_ref, ddt_ref, dprm_ref, dnw_ref, dstate):
        step = pl.program_id(0)
        c = N_CHUNKS - 1 - step

        @pl.when(step == 0)
        def _():
            dstate[...] = jnp.zeros_like(dstate)
            dprm_ref[...] = jnp.zeros_like(dprm_ref)
            dnw_ref[...] = jnp.zeros_like(dnw_ref)

        prm = prm_ref[...]
        dt, a_row, cs, cs_t, causal, tri, real = _ssd_chunk_common(dt_ref[...], prm, c)
        realf = real.astype(F32)
        z = z_ref[...]
        y_all = y_ref[...]
        nw = nw_ref[...]
        dyn_all = dyn_ref[...]
        sz = _silu(z)
        gated = y_all * sz
        half = SSD_WIDTH // SSD_GROUPS
        dgs, dnws = [], []
        for k in range(SSD_GROUPS):
            sl = slice(k * half, (k + 1) * half)
            dgk, dwk = _rms_bwd(gated[:, sl], nw[:, sl], dyn_all[:, sl])
            dgs.append(dgk)
            dnws.append(jnp.sum(dwk, axis=0, keepdims=True))
        dgated = jnp.concatenate(dgs, axis=1)
        dnw_ref[...] += jnp.concatenate(dnws, axis=1)
        dz_ref[...] = (dgated * y_all * _silu_grad(z)).astype(_MXU)
        dy_all = dgated * sz

        ex = ex_ref[...]
        dt_x, d_x, e_cs_x, e_last_x, dec_x = _ssd_expand(dt, cs, prm, ex)
        xs_all = xs_ref[...]
        bc_all = bc_ref[...]
        xdt = xs_all * dt_x
        xdt_mxu = xdt.astype(_MXU).astype(F32)
        xdec = xdt * dec_x
        dcp = dy_all * e_cs_x
        lane_lo = lax.broadcasted_iota(jnp.int32, (1, PAIR_W), 1) < SSD_HEAD_DIM
        upper = (lax.broadcasted_iota(jnp.int32, (CHUNK, CHUNK), 0)
                 <= lax.broadcasted_iota(jnp.int32, (CHUNK, CHUNK), 1))
        last_row = (lax.broadcasted_iota(jnp.int32, (CHUNK, 1), 0) == CHUNK - 1).astype(F32)
        dbs, dcs_, dxdt_parts, last_parts = [], [], [], []
        for g in range(SSD_GROUPS):
            gs = slice(g * GROUP_W, (g + 1) * GROUP_W)
            b_g = bc_all[:, g * SSD_STATE:(g + 1) * SSD_STATE]
            c_g = bc_all[:, (SSD_GROUPS + g) * SSD_STATE:(SSD_GROUPS + g + 1) * SSD_STATE]
            prev_t = prev_ref[0, g]
            dst = dstate[g]
            dc_g = _dot(dcp[:, gs], prev_t, NT)
            db_g = _dot(xdec[:, gs], dst, NT)
            dxdt_state = _dot(b_g, dst) * dec_x[:, gs]
            dstate[g] = dst * e_last_x[:, gs] + _dot(c_g.T, dcp[:, gs])
            last_parts.append(jnp.sum(xdt_mxu[:, gs] * dxdt_state, axis=0, keepdims=True)
                              + jnp.sum(dst * prev_t, axis=0, keepdims=True) * e_last_x[:, gs])
            cb_t = _dot(b_g, c_g, NT)
            dcb_t = jnp.zeros((CHUNK, CHUNK), F32)
            for k in range(SSD_HPG // 2):
                h0 = g * SSD_HPG + 2 * k
                ps = slice(h0 * SSD_HEAD_DIM, h0 * SSD_HEAD_DIM + PAIR_W)
                dy_pair = dy_all[:, ps]
                xdt_pair = xdt[:, ps]
                dd = []
                for h in (h0, h0 + 1):
                    lmat_t = jnp.where(upper, jnp.exp(cs_t[h:h + 1, :] - cs[:, h:h + 1]), 0.0)
                    dd.append(_dot(cb_t * lmat_t, dy_pair))
                    mine = lane_lo if h == h0 else jnp.logical_not(lane_lo)
                    dcb_t = dcb_t + _dot(jnp.where(mine, xdt_pair, 0.0), dy_pair, NT) * lmat_t
                dxdt_parts.append(jnp.where(lane_lo, dd[0], dd[1]) + dxdt_state[:, k * PAIR_W:(k + 1) * PAIR_W])
            dc_g = dc_g + _dot(dcb_t, b_g, TN)
            db_g = db_g + _dot(dcb_t, c_g)
            dbs.append(db_g * realf)
            dcs_.append(dc_g * realf)
        dbc_ref[...] = jnp.concatenate(dbs + dcs_, axis=1)
        dxdt = jnp.concatenate(dxdt_parts, axis=1)
        dxs_ref[...] = (dxdt * dt_x + dy_all * d_x) * realf
        ddt_all = _dot_onehot(dxdt * xs_all, ex, NT, pieces=2)
        rows = jnp.concatenate([jnp.concatenate(last_parts, axis=1), jnp.sum(dy_all * xs_all, axis=0, keepdims=True),
                                jnp.zeros((6, SSD_WIDTH), F32)], axis=0)
        rows = _dot_onehot(rows, ex, NT, pieces=2)
        dd_row = rows[1:2]
        dy_mxu = dy_all.astype(_MXU).astype(F32)
        dcs_all = (_dot_onehot(dy_mxu * (y_all - xs_all * d_x), ex, NT) - _dot_onehot(xdt_mxu * dxdt, ex, NT)
                   + last_row * rows[0:1])
        dda = _dot_onehot(tri, dcs_all, TN, data=1)
        ddt = (ddt_all + dda * a_row) * realf
        ddt_raw = ddt * _sigmoid(dt_ref[...] + prm[0:1])
        ddt_ref[...] = ddt_raw.astype(_MXU)
        da_log = jnp.sum(dda * dt, axis=0, keepdims=True) * a_row
        dprm_ref[0:1, :] += jnp.sum(ddt_raw, axis=0, keepdims=True)
        dprm_ref[1:2, :] += da_log
        dprm_ref[2:3, :] += dd_row

    rev = lambda w, blk=0: pl.BlockSpec((CHUNK, w), lambda s, blk=blk: (N_CHUNKS - 1 - s, blk))
    return pl.pallas_call(
        body, name="ssd_bwd", grid=(N_CHUNKS,),
        in_specs=[rev(SSD_WIDTH, dyn_block), rev(SSD_WIDTH), rev(SSD_WIDTH), rev(SSD_WIDTH), rev(512), rev(128),
                  pl.BlockSpec((1,) + STATE_SHAPE, lambda s: (N_CHUNKS - 1 - s, 0, 0, 0)),
                  pl.BlockSpec((8, 128), lambda s: (0, 0)), pl.BlockSpec((1, SSD_WIDTH), lambda s: (0, 0)),
                  pl.BlockSpec((128, SSD_WIDTH), lambda s: (0, 0))],
        out_specs=[rev(SSD_WIDTH), rev(SSD_WIDTH), rev(512), rev(128),
                   pl.BlockSpec((8, 128), lambda s: (0, 0)), pl.BlockSpec((1, SSD_WIDTH), lambda s: (0, 0))],
        out_shape=[jax.ShapeDtypeStruct((T_ROWS, SSD_WIDTH), _MXU), jax.ShapeDtypeStruct((T_ROWS, SSD_WIDTH), F32),
                   jax.ShapeDtypeStruct((T_ROWS, 512), F32), jax.ShapeDtypeStruct((T_ROWS, 128), _MXU),
                   jax.ShapeDtypeStruct((8, 128), F32), jax.ShapeDtypeStruct((1, SSD_WIDTH), F32)],
        scratch_shapes=[pltpu.VMEM(STATE_SHAPE, F32)],
        compiler_params=_cparams("arbitrary"),
    )(dyn, z, y_pre, xs, bc, dt_raw, prev, prm, norm_w, ex)


LRU_PAIRS = 8


def _lru_gates(xr, wa_ref, wx_ref, prm):
    pre_r, pre_i = [], []
    for k in range(LRU_PAIRS):
        xk = xr[:, k * 128:(k + 1) * 128]
        pre_r.append(_dot(xk, wa_ref[k]))
        pre_i.append(_dot(xk, wx_ref[k]))
    r = _sigmoid(jnp.concatenate(pre_r, axis=1) + prm[0:1])
    i = _sigmoid(jnp.concatenate(pre_i, axis=1) + prm[1:2])
    sp = _softplus(-prm[2:3])
    log_a = (-LRU_C) * r * sp
    a = jnp.exp(log_a)
    s = jnp.sqrt(-jnp.tanh(log_a) * (a * a + 1.0))
    return r, i, a, s, sp


def _lru_fwd(xr, gate, wa, wx, prm):
    def body(xr_ref, g_ref, wa_ref, wx_ref, prm_ref, hs_ref, yn_ref, carry, a_s, u_s):
        @pl.when(pl.program_id(0) == 0)
        def _():
            carry[...] = jnp.zeros_like(carry)

        prm = prm_ref[...]
        xr_t = xr_ref[...]
        _, i, a, s, _ = _lru_gates(xr_t, wa_ref, wx_ref, prm)
        a_s[...] = a
        u_s[...] = s * (i * xr_t)
        rid = lax.broadcasted_iota(jnp.int32, (8, LRU_WIDTH), 0)

        def group(k, before):
            off = pl.multiple_of(k * 8, 8)
            a8 = a_s[pl.ds(off, 8), :]
            u8 = u_s[pl.ds(off, 8), :]
            for d in (1, 2, 4):
                keep = rid >= d
                u8 = u8 + a8 * jnp.where(keep, pltpu.roll(u8, d, 0), 0.0)
                a8 = a8 * jnp.where(keep, pltpu.roll(a8, d, 0), 1.0)
            h8 = u8 + a8 * before
            hs_ref[pl.ds(off, 8), :] = h8
            return jnp.broadcast_to(h8[7:8], (8, LRU_WIDTH))

        carry[...] = lax.fori_loop(0, CHUNK // 8, group, carry[...])
        gel, _ = _gelu_and_grad(g_ref[...])
        yn_ref[...] = _rms_fwd(gel * hs_ref[...], prm[3:4]).astype(_MXU)

    row = pl.BlockSpec((CHUNK, LRU_WIDTH), lambda t: (t, 0))
    wspec = pl.BlockSpec((LRU_PAIRS, 128, 128), lambda t: (0, 0, 0))
    return pl.pallas_call(
        body, name="lru_fwd", grid=(N_CHUNKS,),
        in_specs=[row, row, wspec, wspec, pl.BlockSpec((8, LRU_WIDTH), lambda t: (0, 0))],
        out_specs=[row, row],
        out_shape=[jax.ShapeDtypeStruct((T_ROWS, LRU_WIDTH), F32), jax.ShapeDtypeStruct((T_ROWS, LRU_WIDTH), _MXU)],
        scratch_shapes=[pltpu.VMEM((8, LRU_WIDTH), F32), pltpu.VMEM((CHUNK, LRU_WIDTH), F32),
                        pltpu.VMEM((CHUNK, LRU_WIDTH), F32)],
        compiler_params=_cparams("arbitrary"),
    )(xr, gate, wa, wx, prm)


def _lru_bwd(dyn, dyn_block, gate, xr, hs, wa, wx, wa_t, wx_t, prm):
    def body(dyn_ref, g_ref, xr_ref, hs_ref, hsp_ref, wa_ref, wx_ref, wat_ref, wxt_ref, prm_ref,
             dg_ref, dxr_ref, dwa_ref, dwx_ref, dprm_ref, carry, a_s, d_s):
        step = pl.program_id(0)
        tile = N_CHUNKS - 1 - step

        @pl.when(step == 0)
        def _():
            carry[...] = jnp.zeros_like(carry)
            dwa_ref[...] = jnp.zeros_like(dwa_ref)
            dwx_ref[...] = jnp.zeros_like(dwx_ref)
            dprm_ref[...] = jnp.zeros_like(dprm_ref)

        prm = prm_ref[...]
        xr_t = xr_ref[...]
        r, i, a, s, sp = _lru_gates(xr_t, wa_ref, wx_ref, prm)
        hs_t = hs_ref[...]
        gel, dgel = _gelu_and_grad(g_ref[...])
        dy, dnw = _rms_bwd(gel * hs_t, prm[3:4], dyn_ref[...])
        dg_ref[...] = (dy * hs_t * dgel).astype(_MXU)
        a_s[...] = a
        d_s[...] = dy * gel
        rid = lax.broadcasted_iota(jnp.int32, (8, LRU_WIDTH), 0)

        def group(k, behind):
            off = pl.multiple_of((CHUNK // 8 - 1 - k) * 8, 8)
            a8 = a_s[pl.ds(off, 8), :]
            d8 = d_s[pl.ds(off, 8), :]
            c8 = jnp.where(rid == 7, 1.0, pltpu.roll(a8, 7, 0))
            for d in (1, 2, 4):
                keep = rid < 8 - d
                d8 = d8 + c8 * jnp.where(keep, pltpu.roll(d8, 8 - d, 0), 0.0)
                c8 = c8 * jnp.where(keep, pltpu.roll(c8, 8 - d, 0), 1.0)
            dht8 = d8 + c8 * behind
            d_s[pl.ds(off, 8), :] = dht8
            return jnp.broadcast_to(a8[0:1] * dht8[0:1], (8, LRU_WIDTH))

        carry[...] = lax.fori_loop(0, CHUNK // 8, group, carry[...])
        dht = d_s[...]
        before = hsp_ref[CHUNK - 8:CHUNK, :][7:8] * (tile > 0).astype(F32)
        first = lax.broadcasted_iota(jnp.int32, (CHUNK, 1), 0) == 0
        hprev = jnp.where(first, before, pltpu.roll(hs_t, 1, 0))
        da = dht * hprev
        ixr = i * xr_t
        ds = dht * ixr
        dlog_a = da * a - ds * (a * a) * lax.rsqrt(s * s)
        dr = dlog_a * ((-LRU_C) * sp)
        dsp = jnp.sum(dlog_a * ((-LRU_C) * r), axis=0, keepdims=True)
        dlam = dsp * (-_sigmoid(-prm[2:3]))
        di = dht * s * xr_t
        dpre_r = dr * r * (1.0 - r)
        dpre_i = di * i * (1.0 - i)
        dxr = dht * s * i
        parts = []
        for k in range(LRU_PAIRS):
            sl = slice(k * 128, (k + 1) * 128)
            parts.append(_dot(dpre_r[:, sl], wat_ref[k]) + _dot(dpre_i[:, sl], wxt_ref[k]))
            dwa_ref[k] += _dot(xr_t[:, sl], dpre_r[:, sl], TN)
            dwx_ref[k] += _dot(xr_t[:, sl], dpre_i[:, sl], TN)
        dxr_ref[...] = dxr + jnp.concatenate(parts, axis=1)
        dprm_ref[0:1, :] += jnp.sum(dpre_r, axis=0, keepdims=True)
        dprm_ref[1:2, :] += jnp.sum(dpre_i, axis=0, keepdims=True)
        dprm_ref[2:3, :] += dlam
        dprm_ref[3:4, :] += jnp.sum(dnw, axis=0, keepdims=True)

    rev = lambda blk=0: pl.BlockSpec((CHUNK, LRU_WIDTH), lambda s, blk=blk: (N_CHUNKS - 1 - s, blk))
    wspec = pl.BlockSpec((LRU_PAIRS, 128, 128), lambda s: (0, 0, 0))
    return pl.pallas_call(
        body, name="lru_bwd", grid=(N_CHUNKS,),
        in_specs=[rev(dyn_block), rev(), rev(), rev(),
                  pl.BlockSpec((CHUNK, LRU_WIDTH), lambda s: (jnp.maximum(N_CHUNKS - 2 - s, 0), 0)),
                  wspec, wspec, wspec, wspec, pl.BlockSpec((8, LRU_WIDTH), lambda s: (0, 0))],
        out_specs=[rev(), rev(), wspec, wspec, pl.BlockSpec((8, LRU_WIDTH), lambda s: (0, 0))],
        out_shape=[jax.ShapeDtypeStruct((T_ROWS, LRU_WIDTH), _MXU), jax.ShapeDtypeStruct((T_ROWS, LRU_WIDTH), F32),
                   jax.ShapeDtypeStruct((LRU_PAIRS, 128, 128), F32), jax.ShapeDtypeStruct((LRU_PAIRS, 128, 128), F32),
                   jax.ShapeDtypeStruct((8, LRU_WIDTH), F32)],
        scratch_shapes=[pltpu.VMEM((8, LRU_WIDTH), F32), pltpu.VMEM((CHUNK, LRU_WIDTH), F32),
                        pltpu.VMEM((CHUNK, LRU_WIDTH), F32)],
        compiler_params=_cparams("arbitrary"),
    )(dyn, gate, xr, hs, hs, wa, wx, wa_t, wx_t, prm)


SEC_NAMES = ("z", "xs", "bc", "dt", "g", "x")
SEC_WIDTH = {"z": 1024, "xs": 1024, "bc": 512, "dt": 128, "g": 1024, "x": 1024}


def _pair_blocks(w):
    w = w.reshape(LRU_PAIRS, 2, 64, 64)
    zero = jnp.zeros((LRU_PAIRS, 64, 64), w.dtype)
    top = jnp.concatenate([w[:, 0], zero], axis=2)
    bot = jnp.concatenate([zero, w[:, 1]], axis=2)
    return jnp.concatenate([top, bot], axis=1)


def _unpair_blocks(wp):
    return jnp.stack([wp[:, :64, :64], wp[:, 64:, 64:]], axis=1).reshape(16, 64, 64)


def _pad_lanes(v, width=128):
    return jnp.pad(v, ((0, 0), (0, width - v.shape[1])))


class _Resident:
    before_embed = ()

    def __init__(self, w_in_sections, w_out, w_gate, w_up, w_down):
        self._w_in, self._w_out, self._ffn = w_in_sections, w_out, (w_gate, w_up, w_down)

    def w_in(self, after):
        return self._w_in

    def mid_forward(self, after):
        return jnp.zeros((1, 1), F32)

    def w_out(self, after):
        return self._w_out

    def ffn(self, after):
        return self._ffn

    def grads_ready(self, names, g, g_mxu):
        return jnp.zeros((1, 1), F32)

    def small_ready(self, g, loss):
        return jnp.zeros((1, 1), F32)

    def small_middle(self, after):
        return jnp.zeros((1, 1), F32)


def _local_step(x, target, meta, p, late):
    g, g_mxu = {}, {}
    ex = _head_expander()
    h0 = _embed(x, meta, late.before_embed)
    w_in = late.w_in(h0)
    u1, projs = _norm_proj(h0, p["norm1_w"], [w_in[s] for s in SEC_NAMES], name="norm_in_proj")
    proj = dict(zip(SEC_NAMES, projs))
    ssd_prm = jnp.concatenate([_pad_lanes(p["ssd_dt_bias"]), _pad_lanes(p["ssd_a_log"]), _pad_lanes(p["ssd_d"]),
                               jnp.zeros((5, 128), F32)], axis=0)
    xs_act = _conv_fwd(proj["xs"], p["ssd_conv_w"][:, :SSD_WIDTH], p["ssd_conv_b"][:, :SSD_WIDTH], silu=True,
                       name="ssd_conv_xs")
    bc_act = _conv_fwd(proj["bc"], p["ssd_conv_w"][:, SSD_WIDTH:], p["ssd_conv_b"][:, SSD_WIDTH:], silu=True,
                       name="ssd_conv_bc")
    y_pre, y_ssd, prev = _ssd_fwd(xs_act, bc_act, proj["dt"], proj["z"], ssd_prm, p["ssd_norm_w"], ex)
    xr = _conv_fwd(proj["x"], p["lru_conv_w"], p["lru_conv_b"], silu=False, name="lru_conv")
    wa_p, wx_p = _pair_blocks(p["lru_wa"]), _pair_blocks(p["lru_wx"])
    lru_prm = jnp.concatenate([p["lru_ba"], p["lru_bx"], p["lru_lambda"], p["lru_norm_w"],
                               jnp.zeros((4, LRU_WIDTH), F32)], axis=0)
    hs, y_lru = _lru_fwd(xr, proj["g"], wa_p.astype(_MXU), wx_p.astype(_MXU),
                         lru_prm + late.mid_forward([xr, y_ssd]))
    ycat = jnp.concatenate([y_ssd, y_lru], axis=1)
    w_out = late.w_out(ycat)
    h1 = _mm([(ycat, 0, w_out, 0, 2 * D_MODEL)], T_ROWS, D_MODEL, tm=T_ROWS, tn=256, mode="nn", out_dtype=F32,
             name="out_proj", residual=h0)
    u2 = _rmsnorm(h1, p["norm2_w"], name="norm2")
    w_gate, w_up, w_down = late.ffn(u2)
    gp, up, act = _ffn_up(u2, w_gate, w_up)
    h2 = _mm([(act, 0, w_down, 0, D_FF)], T_ROWS, D_MODEL, tm=T_ROWS, tn=256, mode="nn", out_dtype=F32,
             name="ffn_down", residual=h1)
    loss, dh2, dh2b, g["final_norm_w"] = _loss_head(h2, target, p["final_norm_w"])
    dgp, dup = _ffn_bwd_act(dh2b, w_down, gp, up)
    g["w_down"], g_mxu["w_down"] = _mm([(act, 0, dh2b, 0, T_ROWS)], D_FF, D_MODEL, tm=1408, tn=512, mode="tn",
                                       out_dtype=F32, name="dw_down", also_mxu=True)
    dh1, dh1b, g["norm2_w"] = _mm_norm_bwd([(dgp, w_gate, D_FF), (dup, w_up, D_FF)], h1, p["norm2_w"], dh2,
                                           name="ffn_bwd_in")
    g["w_gate"], g_mxu["w_gate"] = _mm([(dgp, 0, u2, 0, T_ROWS)], D_FF, D_MODEL, tm=1408, tn=512, mode="tn",
                                       out_dtype=F32, name="dw_gate", also_mxu=True)
    g["w_up"], g_mxu["w_up"] = _mm([(dup, 0, u2, 0, T_ROWS)], D_FF, D_MODEL, tm=1408, tn=512, mode="tn",
                                   out_dtype=F32, name="dw_up", also_mxu=True)
    g["w_out"], g_mxu["w_out"] = _mm([(ycat, 0, dh1b, 0, T_ROWS)], 2 * D_MODEL, D_MODEL, tm=1024, tn=512, mode="tn",
                                     out_dtype=F32, name="dw_out", also_mxu=True)
    sent = late.grads_ready(("w_down", "w_gate", "w_up", "w_out"), g, g_mxu)
    dycat = _mm([(dh1b, 0, w_out, 0, D_MODEL)], T_ROWS, 2 * D_MODEL, tm=T_ROWS, tn=256, mode="nt", out_dtype=F32,
                name="out_proj_bwd", behind=(sent,))
    dgate, dxr, dwa_p, dwx_p, dlru_prm = _lru_bwd(dycat, 1, proj["g"], xr, hs, wa_p.astype(_MXU), wx_p.astype(_MXU),
                                                  jnp.swapaxes(wa_p, 1, 2).astype(_MXU),
                                                  jnp.swapaxes(wx_p, 1, 2).astype(_MXU), lru_prm)
    g["lru_wa"], g["lru_wx"] = _unpair_blocks(dwa_p), _unpair_blocks(dwx_p)
    g["lru_ba"], g["lru_bx"], g["lru_lambda"], g["lru_norm_w"] = (dlru_prm[k:k + 1] for k in range(4))
    dx_lru, g["lru_conv_w"], g["lru_conv_b"] = _conv_bwd(dxr, proj["x"], p["lru_conv_w"], p["lru_conv_b"], silu=False,
                                                         name="lru_conv_bwd")
    dz, dxs_act, dbc_act, ddt, dssd_prm, g["ssd_norm_w"] = _ssd_bwd(dycat, 0, proj["z"], y_pre, xs_act, bc_act,
                                                                    proj["dt"], prev, ssd_prm, p["ssd_norm_w"], ex)
    g["ssd_dt_bias"], g["ssd_a_log"], g["ssd_d"] = (dssd_prm[k:k + 1, :SSD_HEADS] for k in range(3))
    dxs, dcw_xs, dcb_xs = _conv_bwd(dxs_act, proj["xs"], p["ssd_conv_w"][:, :SSD_WIDTH],
                                    p["ssd_conv_b"][:, :SSD_WIDTH], silu=True, name="ssd_conv_xs_bwd")
    dbc, dcw_bc, dcb_bc = _conv_bwd(dbc_act, proj["bc"], p["ssd_conv_w"][:, SSD_WIDTH:],
                                    p["ssd_conv_b"][:, SSD_WIDTH:], silu=True, name="ssd_conv_bc_bwd")
    g["ssd_conv_w"] = jnp.concatenate([dcw_xs, dcw_bc], axis=1)
    g["ssd_conv_b"] = jnp.concatenate([dcb_xs, dcb_bc], axis=1)
    dproj = {"z": dz, "xs": dxs, "bc": dbc, "dt": ddt, "g": dgate, "x": dx_lru}
    dh0, _, g["norm1_w"] = _mm_norm_bwd([(dproj[s], w_in[s], SEC_WIDTH[s]) for s in SEC_NAMES], h0,
                                        p["norm1_w"], dh1, name="in_proj_bwd")
    g["meta_tokens"] = dh0[PAD_ROWS:X_ROW0]
    sent = late.small_ready(g, loss)
    for s in SEC_NAMES:
        wdt = SEC_WIDTH[s]
        g["w_in_" + s], g_mxu["w_in_" + s] = _mm([(dproj[s], 0, u1, 0, T_ROWS)], wdt, D_MODEL, tm=min(wdt, 1024),
                                                 tn=512, mode="tn", out_dtype=F32, name="dw_in_" + s, also_mxu=True,
                                                 behind=(sent,))
        if s == "bc":
            sent = late.small_middle([g["w_in_z"], g["w_in_xs"], g["w_in_bc"]])
    late.grads_ready(("w_in",), g, g_mxu)
    return loss, dh0[X_ROW0:], g, g_mxu


MESH = pl.DeviceIdType.MESH
ANY = pl.BlockSpec(memory_space=pl.ANY)


def _my_place():
    return lax.axis_index("x"), lax.axis_index("y"), lax.axis_index("c")


def _other_chips(x, y):
    return [(1 - x, y), (x, 1 - y), (1 - x, 1 - y)]


HBM_SPEC = pl.BlockSpec(memory_space=pltpu.HBM)
SEM_SPEC = pl.BlockSpec(memory_space=pltpu.SEMAPHORE)
SPLIT_EFFECT = pltpu.SideEffectType.DATAFLOW_SIDE_EFFECTING


def _half_cols(buf, c, other=False):
    half = buf.shape[-1] // 2
    return pl.ds(pl.multiple_of(((1 - c) if other else c) * half, 128), half)


def _halves_plan(bufs, x, y, c, incoming):
    plan = []
    for buf in bufs:
        cols = _half_cols(buf, c)
        for (px, py) in _other_chips(x, y):
            slot = 2 * px + py if incoming else 2 * x + y
            plan.append((buf.at[2 * x + y, :, cols], buf.at[slot, :, cols], (px, py, c)))
    return plan


def _forward_plan(bufs, x, y, c, incoming):
    plan = []
    for buf in bufs:
        for (px, py) in _other_chips(x, y):
            slot = 2 * px + py
            plan.append((buf.at[slot, :, _half_cols(buf, c)], buf.at[slot, :, _half_cols(buf, c, other=incoming)],
                         (x, y, 1 - c)))
    return plan


def _scatter_plan(bufs, x, y, c, incoming):
    n = len(bufs) // 2
    plan = []
    for k in range(n):
        for j, (px, py) in enumerate(_other_chips(x, y)):
            plan.append((bufs[k].at[2 * px + py], bufs[n + k].at[j], (px, py, c)))
    return plan


def _split_start(bufs, plan, n_copies, after, *, name):
    n = len(bufs)
    extra = [] if after is None else [after]

    def body(*refs):
        ins = refs[:n]
        send_sems, recv_sems = refs[n + len(extra)], refs[n + len(extra) + 1]
        token = refs[-1]
        x, y, c = _my_place()
        for i, (src, dst, dev) in enumerate(plan(ins, x, y, c, False)):
            pltpu.make_async_remote_copy(src_ref=src, dst_ref=dst, send_sem=send_sems.at[i], recv_sem=recv_sems.at[i],
                                         device_id=dev, device_id_type=MESH).start()
        token[...] = jnp.zeros_like(token)

    outs = pl.pallas_call(
        body, name=name,
        out_shape=(pltpu.SemaphoreType.DMA((n_copies,)), pltpu.SemaphoreType.DMA((n_copies,)),
                   *[pltpu.HBM(b.shape, b.dtype) for b in bufs], jax.ShapeDtypeStruct((8, 128), F32)),
        in_specs=[HBM_SPEC] * n + [ANY] * len(extra),
        out_specs=(SEM_SPEC, SEM_SPEC, *[HBM_SPEC] * n, pl.BlockSpec(memory_space=pltpu.VMEM)),
        input_output_aliases={k: 2 + k for k in range(n)},
        compiler_params=pltpu.CompilerParams(has_side_effects=SPLIT_EFFECT),
    )(*[pltpu.with_memory_space_constraint(b, pltpu.HBM) for b in bufs], *extra)
    return outs[0], outs[1], list(outs[2:2 + n]), outs[-1]


def _split_wait(bufs, send_sems, recv_sems, plan, after, *, name):
    n = len(bufs)
    after = list(after) if isinstance(after, (list, tuple)) else [after]

    def body(*refs):
        ins = refs[:n]
        send_sems_ref, recv_sems_ref = refs[n], refs[n + 1]
        x, y, c = _my_place()
        for i, (src, dst, dev) in enumerate(plan(ins, x, y, c, True)):
            cp = pltpu.make_async_remote_copy(src_ref=src, dst_ref=dst, send_sem=send_sems_ref.at[i],
                                              recv_sem=recv_sems_ref.at[i], device_id=dev, device_id_type=MESH)
            cp.wait_send()
            cp.wait_recv()

    outs = pl.pallas_call(
        body, name=name, out_shape=tuple(pltpu.HBM(b.shape, b.dtype) for b in bufs),
        in_specs=[HBM_SPEC] * n + [SEM_SPEC, SEM_SPEC] + [ANY] * len(after), out_specs=tuple([HBM_SPEC] * n),
        input_output_aliases={k: k for k in range(n)},
        compiler_params=pltpu.CompilerParams(has_side_effects=SPLIT_EFFECT),
    )(*bufs, send_sems, recv_sems, *after)
    return list(outs)


def _fill_own_slots(shards, me_arr, *, name, behind=()):
    n = len(shards)
    n_in = n + len(behind)

    def body(me_ref, *refs):
        for k in range(n):
            refs[n_in + k][0] = refs[k][...].astype(_MXU)

    half = D_MODEL // 2
    return pl.pallas_call(
        body, name=name,
        grid_spec=pltpu.PrefetchScalarGridSpec(
            num_scalar_prefetch=1, grid=(2,),
            in_specs=[pl.BlockSpec((s.shape[0], half), lambda i, me: (0, i)) for s in shards]
            + [pl.BlockSpec(memory_space=pl.ANY)] * len(behind),
            out_specs=[pl.BlockSpec((1, s.shape[0], half), lambda i, me: (me[0], 0, i)) for s in shards]),
        out_shape=[jax.ShapeDtypeStruct((N_SHARDS,) + s.shape, _MXU) for s in shards],
        compiler_params=_cparams("parallel"),
    )(me_arr, *shards, *behind)


def _gather_small(small):
    def body(s_ref, o_ref, send_sems, recv_sems, local_sem):
        x, y, c = _my_place()
        me = 2 * x + y
        local = pltpu.make_async_copy(s_ref, o_ref.at[me], local_sem)
        local.start()
        copies = [(pltpu.make_async_remote_copy(src_ref=s_ref, dst_ref=o_ref.at[me], send_sem=send_sems.at[j],
                                                recv_sem=recv_sems.at[j], device_id=(px, py, c), device_id_type=MESH),
                   2 * px + py) for j, (px, py) in enumerate(_other_chips(x, y))]
        for cp, _ in copies:
            cp.start()
        for j, (cp, slot) in enumerate(copies):
            cp.wait_send()
            pltpu.make_async_remote_copy(src_ref=s_ref, dst_ref=o_ref.at[slot], send_sem=send_sems.at[j],
                                         recv_sem=recv_sems.at[j], device_id=(x, y, c),
                                         device_id_type=MESH).wait_recv()
        local.wait()

    return pl.pallas_call(
        body, name="gather_small", in_specs=[ANY], out_specs=ANY,
        out_shape=jax.ShapeDtypeStruct((N_SHARDS,) + small.shape, small.dtype),
        scratch_shapes=[pltpu.SemaphoreType.DMA((3,)), pltpu.SemaphoreType.DMA((3,)), pltpu.SemaphoreType.DMA],
    )(small)


def _swap_with_sibling(parts, *, name):
    n = len(parts)

    def body(*refs):
        ins, outs = refs[:n], refs[n:2 * n]
        send_sems, recv_sems = refs[2 * n:]
        x, y, c = _my_place()
        copies = [pltpu.make_async_remote_copy(
            src_ref=ins[k], dst_ref=outs[k], send_sem=send_sems.at[k], recv_sem=recv_sems.at[k],
            device_id=(x, y, 1 - c), device_id_type=MESH) for k in range(n)]
        for cp in copies:
            cp.start()
        for cp in copies:
            cp.wait()

    return pl.pallas_call(
        body, name=name, in_specs=[ANY] * n, out_specs=[ANY] * n,
        out_shape=[jax.ShapeDtypeStruct(a.shape, a.dtype) for a in parts],
        scratch_shapes=[pltpu.SemaphoreType.DMA((n,)), pltpu.SemaphoreType.DMA((n,))],
    )(*parts)


def _other_devices(x, y, c):
    out = []
    for mask in range(1, N_DEV):
        px, py, pc = x ^ (mask >> 2 & 1), y ^ (mask >> 1 & 1), c ^ (mask & 1)
        out.append(((px, py, pc), 4 * px + 2 * py + pc))
    return out


def _pieces_plan(bufs, x, y, c, incoming):
    pack, land = bufs
    me = 4 * x + 2 * y + c
    return [(pack.at[num], land.at[num if incoming else me], dev) for dev, num in _other_devices(x, y, c)]


def _spread_plan(bufs, x, y, c, incoming):
    piece, land = bufs
    me = 4 * x + 2 * y + c
    return [(piece, land.at[num if incoming else me], dev) for dev, num in _other_devices(x, y, c)]


def _sum_pieces(pack, land, dev_arr, *, name):
    def body(dev_ref, pack_ref, land_ref, o_ref):
        dev = dev_ref[0]
        own = pack_ref[dev]
        acc = None
        for d in range(N_DEV):
            term = jnp.where(dev == d, own, land_ref[d])
            acc = term if acc is None else acc + term
        o_ref[...] = acc

    vmem = pl.BlockSpec(memory_space=pltpu.VMEM)
    return pl.pallas_call(
        body, name=name, in_specs=[pl.BlockSpec(memory_space=pltpu.SMEM), vmem, vmem], out_specs=vmem,
        out_shape=jax.ShapeDtypeStruct(pack.shape[1:], F32),
    )(dev_arr, pack, land)


def _join_pieces(piece, land, dev_arr, *, name):
    def body(dev_ref, piece_ref, land_ref, o_ref):
        dev = dev_ref[0]
        for d in range(N_DEV):
            o_ref[d] = jnp.where(dev == d, piece_ref[...], land_ref[d])

    vmem = pl.BlockSpec(memory_space=pltpu.VMEM)
    return pl.pallas_call(
        body, name=name, in_specs=[pl.BlockSpec(memory_space=pltpu.SMEM), vmem, vmem], out_specs=vmem,
        out_shape=jax.ShapeDtypeStruct(land.shape, F32),
    )(dev_arr, piece, land)


def _adamw_native(ws, gs, ms, vs):
    n = len(ws)

    def body(*refs):
        for k in range(n):
            w_ref, g_ref, m_ref, v_ref = (refs[j * n + k] for j in range(4))
            delta, m_new, v_new = _adamw_math(w_ref[...], g_ref[...], m_ref[...], v_ref[...])
            refs[4 * n + k][...] = delta
            refs[5 * n + k][...] = m_new
            refs[6 * n + k][...] = v_new

    vmem = pl.BlockSpec(memory_space=pltpu.VMEM)
    shapes = [jax.ShapeDtypeStruct(a.shape, F32) for a in ws]
    outs = pl.pallas_call(
        body, name="adamw_small", in_specs=[vmem] * (4 * n), out_specs=[vmem] * (3 * n), out_shape=shapes * 3,
        compiler_params=pltpu.CompilerParams(vmem_limit_bytes=VMEM_LIMIT_BYTES),
    )(*ws, *gs, *ms, *vs)
    return outs[:n], outs[n:2 * n], outs[2 * n:]


def _elementwise_tile(rows, cols):
    for t in range(256, 15, -16):
        if rows % t == 0:
            return (t, cols), rows // t, lambda i: (i, 0)
    assert cols % 256 == 0
    return (rows, 256), cols // 256, lambda i: (0, i)


def _partial_sum(own, land, me_arr, *, name):
    r, c = own.shape[-2:]
    tile, steps, imap = _elementwise_tile(r, c)
    whole = own.ndim == 3

    def body(me_ref, own_ref, land_ref, o_ref):
        acc = own_ref[0] if whole else own_ref[...]
        for j in range(3):
            acc = acc + land_ref[j].astype(F32)
        o_ref[...] = acc

    own_spec = (pl.BlockSpec((1,) + tile, lambda i, me: (me[0],) + imap(i)) if whole
                else pl.BlockSpec(tile, lambda i, me: imap(i)))
    return pl.pallas_call(
        body, name=name,
        grid_spec=pltpu.PrefetchScalarGridSpec(
            num_scalar_prefetch=1, grid=(steps,),
            in_specs=[own_spec, pl.BlockSpec((3,) + tile, lambda i, me: (0,) + imap(i))],
            out_specs=pl.BlockSpec(tile, lambda i, me: imap(i))),
        out_shape=jax.ShapeDtypeStruct((r, c), F32),
        compiler_params=_cparams("parallel"),
    )(me_arr, own, land)


def _adamw_math(w, g, m, v):
    m = ADAM_B1 * m + (1.0 - ADAM_B1) * g
    v = ADAM_B2 * v + (1.0 - ADAM_B2) * (g * g)
    m_hat = m / (1.0 - ADAM_B1 ** ADAM_STEP)
    v_hat = v / (1.0 - ADAM_B2 ** ADAM_STEP)
    delta = -ADAM_LR * (m_hat / (jnp.sqrt(v_hat) + ADAM_EPS) + ADAM_WD * w)
    return delta, m, v


def _adamw(w, grad_parts, m, v, *, name):
    r, c = w.shape
    tile_shape, steps, imap = _elementwise_tile(r, c)
    n = len(grad_parts)

    def body(*refs):
        w_ref, m_ref, v_ref = refs[:3]
        g_refs = refs[3:3 + n]
        g_out, d_out, m_out, v_out = refs[3 + n:]
        g = g_refs[0][...]
        for k in range(1, n):
            g = g + g_refs[k][...]
        delta, m_new, v_new = _adamw_math(w_ref[...], g, m_ref[...], v_ref[...])
        g_out[...] = g
        d_out[...] = delta
        m_out[...] = m_new
        v_out[...] = v_new

    tile = pl.BlockSpec(tile_shape, imap)
    return pl.pallas_call(
        body, name=name, grid=(steps,), in_specs=[tile] * (3 + n), out_specs=[tile] * 4,
        out_shape=[jax.ShapeDtypeStruct((r, c), F32)] * 4,
        compiler_params=_cparams("parallel"),
    )(w, m, v, *grad_parts)


WEIGHT_NAMES = ("meta_tokens", "norm1_w", "w_in", "ssd_conv_w", "ssd_conv_b", "ssd_dt_bias", "ssd_a_log", "ssd_d",
                "ssd_norm_w", "lru_conv_w", "lru_conv_b", "lru_wa", "lru_ba", "lru_wx", "lru_bx", "lru_lambda",
                "lru_norm_w", "w_out", "norm2_w", "w_gate", "w_up", "w_down", "final_norm_w")
BIG = ("w_in", "w_out", "w_gate", "w_up", "w_down")
FFN = ("w_gate", "w_up", "w_down")
LATE = ("w_out",) + FFN
SMALL_SHARDED = {"meta_tokens": (N_META, D_MODEL), "ssd_conv_w": (CONV_K, 1536), "lru_conv_w": (CONV_K, LRU_WIDTH)}
SMALL = tuple(n for n in WEIGHT_NAMES if n not in BIG)
PACK_COLS = 1024


def _pack(arrays, row_multiple):
    flat = jnp.concatenate([a.reshape(-1) for a in arrays])
    rows = -(-flat.shape[0] // (row_multiple * PACK_COLS)) * row_multiple
    return jnp.pad(flat, (0, rows * PACK_COLS - flat.shape[0])).reshape(rows, PACK_COLS)


def _unpack(pack, shapes):
    flat = pack.reshape(-1)
    out, off = [], 0
    for s in shapes:
        size = math.prod(s)
        out.append(flat[off:off + size].reshape(s))
        off += size
    return out


def _unshard_cols(g4):
    return jnp.swapaxes(g4, 0, 1).reshape(g4.shape[1], -1)


COL_SHARDED = ("w_in", "w_gate", "w_up")
IN_ROWS = {"z": (0, 1024), "xs": (1024, 2048), "bc": (2048, 2560), "dt": (2560, 2576), "g": (2576, 3600),
           "x": (3600, IN_COLS)}


def _rows_of_shards(shards4, lo, hi):
    r = shards4.shape[1]
    parts = [shards4[k, max(lo, k * r) - k * r:min(hi, (k + 1) * r) - k * r]
             for k in range(N_SHARDS) if max(lo, k * r) < min(hi, (k + 1) * r)]
    return parts[0] if len(parts) == 1 else jnp.concatenate(parts, axis=0)


def _w_in_shard_rows(k, sections):
    lo, hi = k * (IN_COLS // N_SHARDS), (k + 1) * (IN_COLS // N_SHARDS)
    parts = []
    for arr, (a, b) in zip(sections, IN_ROWS.values()):
        if max(lo, a) < min(hi, b):
            parts.append(arr[max(lo, a) - a:min(hi, b) - a])
    return jnp.concatenate(parts, axis=0)


def _rows_view(name, block):
    return jnp.swapaxes(block[0], 0, 1) if name in COL_SHARDED else block[0]


def _param_view(name, rows):
    return (jnp.swapaxes(rows, 0, 1) if name in COL_SHARDED else rows)[None]


def kernel(x, meta_tokens, norm1_w, w_in, ssd_conv_w, ssd_conv_b, ssd_dt_bias, ssd_a_log, ssd_d, ssd_norm_w, lru_conv_w, lru_conv_b, lru_wa, lru_ba, lru_wx, lru_bx, lru_lambda, lru_norm_w, w_out, norm2_w, w_gate, w_up, w_down, final_norm_w, loss_target, m_meta_tokens, m_norm1_w, m_w_in, m_ssd_conv_w, m_ssd_conv_b, m_ssd_dt_bias, m_ssd_a_log, m_ssd_d, m_ssd_norm_w, m_lru_conv_w, m_lru_conv_b, m_lru_wa, m_lru_ba, m_lru_wx, m_lru_bx, m_lru_lambda, m_lru_norm_w, m_w_out, m_norm2_w, m_w_gate, m_w_up, m_w_down, m_final_norm_w, v_meta_tokens, v_norm1_w, v_w_in, v_ssd_conv_w, v_ssd_conv_b, v_ssd_dt_bias, v_ssd_a_log, v_ssd_d, v_ssd_norm_w, v_lru_conv_w, v_lru_conv_b, v_lru_wa, v_lru_ba, v_lru_wx, v_lru_bx, v_lru_lambda, v_lru_norm_w, v_w_out, v_norm2_w, v_w_gate, v_w_up, v_w_down, v_final_norm_w):
    w = dict(zip(WEIGHT_NAMES, (meta_tokens, norm1_w, w_in, ssd_conv_w, ssd_conv_b, ssd_dt_bias, ssd_a_log, ssd_d, ssd_norm_w, lru_conv_w, lru_conv_b, lru_wa, lru_ba, lru_wx, lru_bx, lru_lambda, lru_norm_w, w_out, norm2_w, w_gate, w_up, w_down, final_norm_w)))
    m = dict(zip(WEIGHT_NAMES, (m_meta_tokens, m_norm1_w, m_w_in, m_ssd_conv_w, m_ssd_conv_b, m_ssd_dt_bias, m_ssd_a_log, m_ssd_d, m_ssd_norm_w, m_lru_conv_w, m_lru_conv_b, m_lru_wa, m_lru_ba, m_lru_wx, m_lru_bx, m_lru_lambda, m_lru_norm_w, m_w_out, m_norm2_w, m_w_gate, m_w_up, m_w_down, m_final_norm_w)))
    v = dict(zip(WEIGHT_NAMES, (v_meta_tokens, v_norm1_w, v_w_in, v_ssd_conv_w, v_ssd_conv_b, v_ssd_dt_bias, v_ssd_a_log, v_ssd_d, v_ssd_norm_w, v_lru_conv_w, v_lru_conv_b, v_lru_wa, v_lru_ba, v_lru_wx, v_lru_bx, v_lru_lambda, v_lru_norm_w, v_w_out, v_norm2_w, v_w_gate, v_w_up, v_w_down, v_final_norm_w)))
    me = 2 * lax.axis_index("x") + lax.axis_index("y")

    big2d = {n: _rows_view(n, w[n]) for n in BIG}
    small_local = jnp.concatenate([w["meta_tokens"].reshape(-1), w["ssd_conv_w"].reshape(-1),
                                   w["lru_conv_w"].reshape(-1)])[None]
    me_arr = me.astype(jnp.int32).reshape(1)
    dev_arr = (2 * me + lax.axis_index("c")).astype(jnp.int32).reshape(1)
    small4 = _gather_small(small_local)
    (w_in_slot,) = _fill_own_slots([big2d["w_in"]], me_arr, name="own_slot_w_in")
    in_send, in_recv, in_bufs, in_tok = _split_start([w_in_slot], _halves_plan, 3, small4, name="gather_w_in_start")
    late_slots = _fill_own_slots([big2d[n] for n in LATE], me_arr, name="own_slots_late", behind=(in_tok,))
    sm = small4[:, 0]
    meta_full = _unshard_cols(sm[:, :4096].reshape(N_SHARDS, N_META, 256))
    ssd_conv_w_full = _unshard_cols(sm[:, 4096:5632].reshape(N_SHARDS, CONV_K, 384))
    lru_conv_w_full = _unshard_cols(sm[:, 5632:].reshape(N_SHARDS, CONV_K, 256))

    p = {"ssd_conv_w": ssd_conv_w_full, "lru_conv_w": lru_conv_w_full,
         "lru_wa": w["lru_wa"][0], "lru_wx": w["lru_wx"][0], "final_norm_w": w["final_norm_w"][None]}
    for n in ("norm1_w", "ssd_conv_b", "ssd_dt_bias", "ssd_a_log", "ssd_d", "ssd_norm_w", "lru_conv_b", "lru_ba",
              "lru_bx", "lru_lambda", "lru_norm_w", "norm2_w"):
        p[n] = w[n]

    class Late:
        def __init__(self):
            self.pending = []
            self.before_embed = (late_slots[0],)

        def w_in(self, after):
            (buf,) = _split_wait(in_bufs, in_send, in_recv, _halves_plan, after, name="gather_w_in_wait")
            send, recv, bufs, tok = _split_start([buf], _forward_plan, 3, None, name="forward_w_in_start")
            self.late_gather = _split_start(late_slots, _halves_plan, 3 * len(LATE), tok, name="gather_late_start")
            (w_in4,) = _split_wait(bufs, send, recv, _forward_plan, self.late_gather[2][0], name="forward_w_in_wait")
            sections = {s: _rows_of_shards(w_in4, lo, hi) for s, (lo, hi) in IN_ROWS.items()}
            sections["dt"] = jnp.pad(sections["dt"], ((0, SEC_WIDTH["dt"] - SSD_HEADS), (0, 0)))
            return sections

        def mid_forward(self, after):
            send, recv, bufs, _ = self.late_gather
            bufs = _split_wait(bufs, send, recv, _halves_plan, after, name="gather_late_wait")
            self.forward = _split_start(bufs, _forward_plan, 3 * len(LATE), None, name="forward_late_start")
            return self.forward[3][:1, :1]

        def w_out(self, after):
            send, recv, bufs, _ = self.forward
            bufs = _split_wait(bufs, send, recv, _forward_plan, after, name="forward_late_wait")
            self.late = dict(zip(LATE, (b.reshape(-1, D_MODEL) for b in bufs)))
            return self.late["w_out"]

        def ffn(self, after):
            return tuple(self.late[n] for n in FFN)

        def grads_ready(self, names, g, g_mxu):
            if names == ("w_in",):
                g_mxu["w_in"] = jnp.stack([_w_in_shard_rows(k, [g_mxu["w_in_" + s] for s in SEC_NAMES])
                                           for k in range(N_SHARDS)])
            srcs = [g_mxu[n].reshape(N_SHARDS, -1, D_MODEL) for n in names]
            lands = [lax.empty((3,) + s.shape[1:], _MXU) for s in srcs]
            tag = "_".join(names)
            send, recv, bufs, tok = _split_start(srcs + lands, _scatter_plan, 3 * len(names), None,
                                                 name="scatter_" + tag + "_start")
            self.pending.append((names, send, recv, bufs, tag))
            self.in_flight = bufs[0]
            return tok[:1, :1]

        def landed(self, after, which):
            land = {}
            for names, send, recv, bufs, tag in self.pending:
                if names[0] in which:
                    bufs = _split_wait(bufs, send, recv, _scatter_plan, after, name="scatter_" + tag + "_wait")
                    land.update(zip(names, bufs[len(names):]))
            return land

        def small_ready(self, g, loss):
            pack = _pack([g[n] for n in SMALL] + [loss[0, :1]], 8 * N_DEV)
            pack = pack.reshape(N_DEV, -1, PACK_COLS)
            self.small = _split_start([pack, lax.empty(pack.shape, F32)], _pieces_plan, N_DEV - 1, loss,
                                      name="small_pieces_start")
            return self.small[3]

        def small_middle(self, after):
            send, recv, bufs, _ = self.small
            pack, land = _split_wait(bufs, send, recv, _pieces_plan, after, name="small_pieces_wait")
            piece = _sum_pieces(pack, land, dev_arr, name="small_pieces_sum")
            self.small = _split_start([piece, lax.empty(pack.shape, F32)], _spread_plan, N_DEV - 1, None,
                                      name="small_spread_start")
            return self.small[3]

        def small_sum(self, after):
            send, recv, bufs, _ = self.small
            piece, land = _split_wait(bufs, send, recv, _spread_plan, after, name="small_spread_wait")
            return _join_pieces(piece, land, dev_arr, name="small_join")

    late = Late()

    loss, grad_x, g, g_mxu = _local_step(x[0], loss_target[0], meta_full, p, late)

    g4 = {n: g[n].reshape(N_SHARDS, -1, D_MODEL) for n in LATE}
    g4["w_in"] = lax.switch(me, [functools.partial(_w_in_shard_rows, k) for k in range(N_SHARDS)],
                            [g["w_in_" + s] for s in SEC_NAMES])
    land = late.landed(late.in_flight, LATE)
    part = {n: _partial_sum(g4[n], land[n], me_arr, name="partial_" + n) for n in LATE}
    sib = dict(zip(LATE, _swap_with_sibling([part[n] for n in LATE], name="swap_late")))

    small_full_shape = {n: (SMALL_SHARDED[n] if n in SMALL_SHARDED else w[n].shape) for n in SMALL}
    red_list = _unpack(late.small_sum(sib["w_out"]), [small_full_shape[n] for n in SMALL] + [(1,)])
    loss_total = red_list[-1][0]
    g_small = {}
    for n, arr in zip(SMALL, red_list[:-1]):
        if n in SMALL_SHARDED:
            cols = SMALL_SHARDED[n][1] // N_SHARDS
            arr = lax.dynamic_slice_in_dim(arr, me * cols, cols, axis=1)
        g_small[n] = arr.reshape(w[n].shape)

    grad, delta, new_m, new_v = {}, {}, {}, {}

    def update_big(n):
        outs = _adamw(big2d[n], [part[n], sib[n]], _rows_view(n, m[n]), _rows_view(n, v[n]), name="adamw_" + n)
        grad[n], delta[n], new_m[n], new_v[n] = (_param_view(n, o) for o in outs)
        return outs[0]

    two_d = lambda a: a.reshape(1, -1) if a.ndim == 1 else a
    deltas, new_ms, new_vs = _adamw_native(*[[two_d(d[n]) for n in SMALL] for d in (w, g_small, m, v)])
    for n, dn, mn, vn in zip(SMALL, deltas, new_ms, new_vs):
        grad[n], delta[n], new_m[n], new_v[n] = (g_small[n], dn.reshape(w[n].shape), mn.reshape(w[n].shape),
                                                 vn.reshape(w[n].shape))
    land.update(late.landed([update_big(n) for n in LATE] + [deltas[0]], ("w_in",)))
    part["w_in"] = _partial_sum(g4["w_in"], land["w_in"], me_arr, name="partial_w_in")
    (sib["w_in"],) = _swap_with_sibling([part["w_in"]], name="swap_w_in")
    update_big("w_in")

    return (loss_total, grad_x[None], *[grad[n] for n in WEIGHT_NAMES], *[delta[n] for n in WEIGHT_NAMES],
            *[new_m[n] for n in WEIGHT_NAMES], *[new_v[n] for n in WEIGHT_NAMES])
```

```python
import functools
import math

import jax
import jax.numpy as jnp
from jax import lax
from jax.experimental import pallas as pl
from jax.experimental.pallas import tpu as pltpu

F32 = jnp.float32
_MXU = jnp.bfloat16

D_MODEL = 1024
SEQ = 2048
N_META = 16
CHUNK = 128
T_ROWS = 2176
N_CHUNKS = T_ROWS // CHUNK
PAD_ROWS = T_ROWS - SEQ - N_META
X_ROW0 = PAD_ROWS + N_META
SSD_HEADS = 16
SSD_HEAD_DIM = 64
SSD_STATE = 128
SSD_GROUPS = 2
SSD_HPG = SSD_HEADS // SSD_GROUPS
SSD_WIDTH = 1024
LRU_WIDTH = 1024
LRU_C = 8.0
D_FF = 2816
EPS = 1e-6
IN_COLS = 4624
N_SHARDS = 4
N_DEV = 8

ADAM_LR = 0.001
ADAM_B1 = 0.9
ADAM_B2 = 0.999
ADAM_EPS = 1e-08
ADAM_WD = 0.01
ADAM_STEP = 10

VMEM_LIMIT_BYTES = 56 * 1024 * 1024

NN = (((1,), (0,)), ((), ()))
NT = (((1,), (1,)), ((), ()))
TN = (((0,), (0,)), ((), ()))


def _cparams(*sem):
    return pltpu.CompilerParams(dimension_semantics=sem, vmem_limit_bytes=VMEM_LIMIT_BYTES)


def _dot(a, b, dims=NN):
    return lax.dot_general(a.astype(_MXU), b.astype(_MXU), dims, preferred_element_type=F32)


def _dot_onehot(a, b, dims=NN, *, data=0, pieces=3):
    ops = [a, b]
    mask = ops[1 - data].astype(jnp.bfloat16)
    rest = ops[data]
    acc = None
    for _ in range(pieces):
        piece = rest.astype(jnp.bfloat16)
        ops[data], ops[1 - data] = piece, mask
        d = lax.dot_general(ops[0], ops[1], dims, preferred_element_type=F32)
        acc = d if acc is None else acc + d
        rest = rest - piece.astype(F32)
    return acc


def _sigmoid(x):
    return 0.5 * (1.0 + jnp.tanh(0.5 * x))


def _softplus(x):
    return jnp.maximum(x, 0.0) + jnp.log(1.0 + jnp.exp(-jnp.abs(x)))


def _silu(x):
    return x * _sigmoid(x)


def _silu_grad(x):
    s = _sigmoid(x)
    return s * (1.0 + x * (1.0 - s))


_GELU_C = math.sqrt(2.0 / math.pi)


def _gelu_and_grad(x):
    inner = _GELU_C * (x + 0.044715 * x * x * x)
    t = jnp.tanh(inner)
    g = 0.5 * x * (1.0 + t)
    dg = 0.5 * (1.0 + t) + 0.5 * x * (1.0 - t * t) * _GELU_C * (1.0 + 3.0 * 0.044715 * x * x)
    return g, dg


def _rms_fwd(x, w):
    rstd = lax.rsqrt(jnp.mean(x * x, axis=-1, keepdims=True) + EPS)
    return x * rstd * w


def _rms_bwd(x, w, dy):
    rstd = lax.rsqrt(jnp.mean(x * x, axis=-1, keepdims=True) + EPS)
    xhat = x * rstd
    dxhat = dy * w
    dx = rstd * (dxhat - xhat * jnp.mean(dxhat * xhat, axis=-1, keepdims=True))
    return dx, dy * xhat


def _mm(terms, m, n, *, tm, tn, mode, out_dtype, name, residual=None, n_outer=False, also_mxu=False, behind=()):
    gm, gn = m // tm, n // tn
    assert gm * tm == m and gn * tn == n
    if n_outer:
        grid = (gn, gm)
        mi = lambda g0, g1: g1
        ni = lambda g0, g1: g0
    else:
        grid = (gm, gn)
        mi = lambda g0, g1: g0
        ni = lambda g0, g1: g1
    in_specs, args = [], []
    for (a, ka, b, kb, k) in terms:
        if mode == "tn":
            in_specs.append(pl.BlockSpec((k, tm), lambda g0, g1, ka=ka: (ka, mi(g0, g1))))
        else:
            in_specs.append(pl.BlockSpec((tm, k), lambda g0, g1, ka=ka: (mi(g0, g1), ka)))
        if mode == "nt":
            in_specs.append(pl.BlockSpec((tn, k), lambda g0, g1, kb=kb: (ni(g0, g1), kb)))
        else:
            in_specs.append(pl.BlockSpec((k, tn), lambda g0, g1, kb=kb: (kb, ni(g0, g1))))
        args += [a, b]
    if residual is not None:
        in_specs.append(pl.BlockSpec((tm, tn), lambda g0, g1: (mi(g0, g1), ni(g0, g1))))
        args.append(residual)
    dims = {"nn": NN, "nt": NT, "tn": TN}[mode]
    n_terms = len(terms)
    has_res = residual is not None
    in_specs += [pl.BlockSpec(memory_space=pl.ANY)] * len(behind)
    args += list(behind)
    n_in = len(args)

    def body(*refs):
        acc = None
        for t in range(n_terms):
            d = lax.dot_general(refs[2 * t][...], refs[2 * t + 1][...], dims, preferred_element_type=F32)
            acc = d if acc is None else acc + d
        if has_res:
            acc = acc + refs[2 * n_terms][...]
        refs[n_in][...] = acc.astype(out_dtype)
        if also_mxu:
            refs[n_in + 1][...] = acc.astype(_MXU)

    tile = pl.BlockSpec((tm, tn), lambda g0, g1: (mi(g0, g1), ni(g0, g1)))
    shape = jax.ShapeDtypeStruct((m, n), out_dtype)
    return pl.pallas_call(
        body, name=name, grid=grid, in_specs=in_specs,
        out_specs=[tile, tile] if also_mxu else tile,
        out_shape=[shape, jax.ShapeDtypeStruct((m, n), _MXU)] if also_mxu else shape,
        compiler_params=_cparams("parallel", "parallel"),
    )(*args)


def _embed(x, meta, behind=()):
    def body(x_ref, meta_ref, *rest):
        o_ref = rest[-1]
        i = pl.program_id(0)

        @pl.when(i == 0)
        def _():
            o_ref[0:PAD_ROWS, :] = jnp.zeros((PAD_ROWS, D_MODEL), F32)
            o_ref[PAD_ROWS:CHUNK, :] = meta_ref[...]

        @pl.when(i > 0)
        def _():
            o_ref[...] = x_ref[...]

    return pl.pallas_call(
        body, name="embed", grid=(N_CHUNKS,),
        in_specs=[pl.BlockSpec((CHUNK, D_MODEL), lambda i: (jnp.maximum(i - 1, 0), 0)),
                  pl.BlockSpec((N_META, D_MODEL), lambda i: (0, 0))] + [pl.BlockSpec(memory_space=pl.ANY)] * len(behind),
        out_specs=pl.BlockSpec((CHUNK, D_MODEL), lambda i: (i, 0)),
        out_shape=jax.ShapeDtypeStruct((T_ROWS, D_MODEL), F32),
        compiler_params=_cparams("parallel"),
    )(x, meta, *behind)


def _rmsnorm(h, w, *, name, tm=544):
    def body(h_ref, w_ref, o_ref):
        o_ref[...] = _rms_fwd(h_ref[...], w_ref[...]).astype(_MXU)

    return pl.pallas_call(
        body, name=name, grid=(T_ROWS // tm,),
        in_specs=[pl.BlockSpec((tm, D_MODEL), lambda i: (i, 0)), pl.BlockSpec((1, D_MODEL), lambda i: (0, 0))],
        out_specs=pl.BlockSpec((tm, D_MODEL), lambda i: (i, 0)),
        out_shape=jax.ShapeDtypeStruct((T_ROWS, D_MODEL), _MXU),
        compiler_params=_cparams("parallel"),
    )(h, w)


def _norm_proj(h, w, sections, *, name, tm=544):
    widths = [s.shape[0] for s in sections]
    n = len(sections)

    def body(*refs):
        h_ref, w_ref = refs[:2]
        u_ref = refs[2 + n]
        u = _rms_fwd(h_ref[...], w_ref[...]).astype(_MXU)
        u_ref[...] = u
        for k in range(n):
            refs[3 + n + k][...] = lax.dot_general(u, refs[2 + k][...], NT, preferred_element_type=F32)

    row = lambda width: pl.BlockSpec((tm, width), lambda i: (i, 0))
    outs = pl.pallas_call(
        body, name=name, grid=(T_ROWS // tm,),
        in_specs=[row(D_MODEL), pl.BlockSpec((1, D_MODEL), lambda i: (0, 0))]
        + [pl.BlockSpec((wd, D_MODEL), lambda i: (0, 0)) for wd in widths],
        out_specs=[row(D_MODEL)] + [row(wd) for wd in widths],
        out_shape=[jax.ShapeDtypeStruct((T_ROWS, D_MODEL), _MXU)]
        + [jax.ShapeDtypeStruct((T_ROWS, wd), F32) for wd in widths],
        compiler_params=_cparams("parallel"),
    )(h, w, *sections)
    return outs[0], list(outs[1:])


def _loss_head(h2, target, fw):
    def body(h_ref, t_ref, w_ref, loss_ref, dh_ref, dhb_ref, dw_ref, acc_ref):
        i = pl.program_id(0)

        @pl.when(i == 0)
        def _():
            acc_ref[...] = jnp.zeros_like(acc_ref)
            dw_ref[...] = jnp.zeros_like(dw_ref)

        h = h_ref[...]
        w = w_ref[...]
        y = _rms_fwd(h, w)
        live = (i > 0).astype(F32)
        err = (y - t_ref[...]) * live
        acc_ref[...] += jnp.sum(err * err, axis=0, keepdims=True)
        dy = err * (1.0 / D_MODEL)
        dx, dwr = _rms_bwd(h, w, dy)
        dh_ref[...] = dx
        dhb_ref[...] = dx.astype(_MXU)
        dw_ref[...] += jnp.sum(dwr, axis=0, keepdims=True)

        @pl.when(i == N_CHUNKS - 1)
        def _():
            tot = jnp.sum(acc_ref[...], axis=1, keepdims=True) * (0.5 / D_MODEL)
            loss_ref[...] = jnp.broadcast_to(tot, (1, 128))

    return pl.pallas_call(
        body, name="loss_head", grid=(N_CHUNKS,),
        in_specs=[pl.BlockSpec((CHUNK, D_MODEL), lambda i: (i, 0)),
                  pl.BlockSpec((CHUNK, D_MODEL), lambda i: (jnp.maximum(i - 1, 0), 0)),
                  pl.BlockSpec((1, D_MODEL), lambda i: (0, 0))],
        out_specs=[pl.BlockSpec((1, 128), lambda i: (0, 0)),
                   pl.BlockSpec((CHUNK, D_MODEL), lambda i: (i, 0)),
                   pl.BlockSpec((CHUNK, D_MODEL), lambda i: (i, 0)),
                   pl.BlockSpec((1, D_MODEL), lambda i: (0, 0))],
        out_shape=[jax.ShapeDtypeStruct((1, 128), F32),
                   jax.ShapeDtypeStruct((T_ROWS, D_MODEL), F32),
                   jax.ShapeDtypeStruct((T_ROWS, D_MODEL), _MXU),
                   jax.ShapeDtypeStruct((1, D_MODEL), F32)],
        scratch_shapes=[pltpu.VMEM((1, D_MODEL), F32)],
        compiler_params=_cparams("arbitrary"),
    )(h2, target, fw)


def _mm_norm_bwd(terms, h, w, dres, *, name, tm=272, behind=()):
    n_terms = len(terms)
    in_specs, args = [], []
    for (a, b, k) in terms:
        in_specs += [pl.BlockSpec((tm, k), lambda i: (i, 0)), pl.BlockSpec((k, D_MODEL), lambda i: (0, 0))]
        args += [a, b]
    in_specs += [pl.BlockSpec((tm, D_MODEL), lambda i: (i, 0)), pl.BlockSpec((1, D_MODEL), lambda i: (0, 0)),
                 pl.BlockSpec((tm, D_MODEL), lambda i: (i, 0))] + [pl.BlockSpec(memory_space=pl.ANY)] * len(behind)
    args += [h, w, dres, *behind]

    def body(*refs):
        h_ref, w_ref, dres_ref = refs[2 * n_terms:2 * n_terms + 3]
        dh_ref, dhb_ref, dw_ref = refs[2 * n_terms + 3 + len(behind):]

        @pl.when(pl.program_id(0) == 0)
        def _():
            dw_ref[...] = jnp.zeros_like(dw_ref)

        du = None
        for t in range(n_terms):
            d = lax.dot_general(refs[2 * t][...], refs[2 * t + 1][...], NN, preferred_element_type=F32)
            du = d if du is None else du + d
        dx, dwr = _rms_bwd(h_ref[...], w_ref[...], du)
        dh = dres_ref[...] + dx
        dh_ref[...] = dh
        dhb_ref[...] = dh.astype(_MXU)
        dw_ref[...] += jnp.sum(dwr, axis=0, keepdims=True)

    return pl.pallas_call(
        body, name=name, grid=(T_ROWS // tm,), in_specs=in_specs,
        out_specs=[pl.BlockSpec((tm, D_MODEL), lambda i: (i, 0)), pl.BlockSpec((tm, D_MODEL), lambda i: (i, 0)),
                   pl.BlockSpec((1, D_MODEL), lambda i: (0, 0))],
        out_shape=[jax.ShapeDtypeStruct((T_ROWS, D_MODEL), F32), jax.ShapeDtypeStruct((T_ROWS, D_MODEL), _MXU),
                   jax.ShapeDtypeStruct((1, D_MODEL), F32)],
        compiler_params=_cparams("arbitrary"),
    )(*args)


FFN_TM = T_ROWS
FFN_TN = 256


def _ffn_up(u2, wg_t, wu_t):
    def body(u_ref, wg_ref, wu_ref, gp_ref, up_ref, act_ref):
        u = u_ref[...]
        gp = lax.dot_general(u, wg_ref[...], NT, preferred_element_type=F32)
        up = lax.dot_general(u, wu_ref[...], NT, preferred_element_type=F32)
        gp_ref[...] = gp.astype(_MXU)
        up_ref[...] = up.astype(_MXU)
        act_ref[...] = (_silu(gp) * up).astype(_MXU)

    tile = pl.BlockSpec((FFN_TM, FFN_TN), lambda j, i: (i, j))
    return pl.pallas_call(
        body, name="ffn_up", grid=(D_FF // FFN_TN, T_ROWS // FFN_TM),
        in_specs=[pl.BlockSpec((FFN_TM, D_MODEL), lambda j, i: (i, 0)),
                  pl.BlockSpec((FFN_TN, D_MODEL), lambda j, i: (j, 0)),
                  pl.BlockSpec((FFN_TN, D_MODEL), lambda j, i: (j, 0))],
        out_specs=[tile, tile, tile],
        out_shape=[jax.ShapeDtypeStruct((T_ROWS, D_FF), _MXU)] * 3,
        compiler_params=_cparams("parallel", "parallel"),
    )(u2, wg_t, wu_t)


def _ffn_bwd_act(dh2b, wd, gp, up):
    def body(dh_ref, wd_ref, gp_ref, up_ref, dgp_ref, dup_ref):
        dact = lax.dot_general(dh_ref[...], wd_ref[...], NT, preferred_element_type=F32)
        gp = gp_ref[...].astype(F32)
        dgp_ref[...] = (dact * up_ref[...].astype(F32) * _silu_grad(gp)).astype(_MXU)
        dup_ref[...] = (dact * _silu(gp)).astype(_MXU)

    tile = pl.BlockSpec((FFN_TM, FFN_TN), lambda j, i: (i, j))
    return pl.pallas_call(
        body, name="ffn_bwd_act", grid=(D_FF // FFN_TN, T_ROWS // FFN_TM),
        in_specs=[pl.BlockSpec((FFN_TM, D_MODEL), lambda j, i: (i, 0)),
                  pl.BlockSpec((FFN_TN, D_MODEL), lambda j, i: (j, 0)), tile, tile],
        out_specs=[tile, tile],
        out_shape=[jax.ShapeDtypeStruct((T_ROWS, D_FF), _MXU), jax.ShapeDtypeStruct((T_ROWS, D_FF), _MXU)],
        compiler_params=_cparams("parallel", "parallel"),
    )(dh2b, wd, gp, up)


CONV_TC = 512
CONV_K = 4


def _conv_pre(x_ref, wv, bv, c):
    tc = wv.shape[1]
    r0 = c * CHUNK
    cur = x_ref[r0:r0 + CHUNK, :]
    if c == 0:
        cat = jnp.concatenate([jnp.zeros((8, tc), F32), cur], axis=0)
        shifted = [cur] + [pltpu.roll(cat, s, 0)[8:8 + CHUNK] for s in range(1, CONV_K)]
    else:
        shifted = [cur] + [x_ref[r0 - s:r0 - s + CHUNK, :] for s in range(1, CONV_K)]
    pre = bv
    for s in range(CONV_K):
        pre = pre + shifted[s] * wv[CONV_K - 1 - s:CONV_K - s]
    return pre, shifted


def _row_mask(c):
    if c > 0:
        return None
    return (lax.broadcasted_iota(jnp.int32, (CHUNK, 1), 0) >= PAD_ROWS).astype(F32)


def _conv_fwd(x, w, b, *, silu, name):
    cols = x.shape[1]
    tc = min(CONV_TC, cols)

    def body(x_ref, w_ref, b_ref, o_ref):
        wv, bv = w_ref[...], b_ref[...]
        for c in range(N_CHUNKS):
            pre, _ = _conv_pre(x_ref, wv, bv, c)
            y = _silu(pre) if silu else pre
            mask = _row_mask(c)
            if mask is not None:
                y = y * mask
            o_ref[c * CHUNK:(c + 1) * CHUNK, :] = y

    return pl.pallas_call(
        body, name=name, grid=(cols // tc,),
        in_specs=[pl.BlockSpec((T_ROWS, tc), lambda j: (0, j)), pl.BlockSpec((CONV_K, tc), lambda j: (0, j)),
                  pl.BlockSpec((1, tc), lambda j: (0, j))],
        out_specs=pl.BlockSpec((T_ROWS, tc), lambda j: (0, j)),
        out_shape=jax.ShapeDtypeStruct((T_ROWS, cols), F32),
        compiler_params=_cparams("parallel"),
    )(x, w, b)


def _conv_bwd(dy, x, w, b, *, silu, name):
    cols = x.shape[1]
    tc = min(CONV_TC, cols)

    def body(dy_ref, x_ref, w_ref, b_ref, dx_ref, dw_ref, db_ref):
        wv, bv = w_ref[...], b_ref[...]
        next8 = jnp.zeros((8, tc), F32)
        dws = [jnp.zeros((1, tc), F32) for _ in range(CONV_K)]
        db = jnp.zeros((1, tc), F32)
        for c in reversed(range(N_CHUNKS)):
            r0 = c * CHUNK
            pre, shifted = _conv_pre(x_ref, wv, bv, c)
            dpre = dy_ref[r0:r0 + CHUNK, :]
            if silu:
                dpre = dpre * _silu_grad(pre)
            mask = _row_mask(c)
            if mask is not None:
                dpre = dpre * mask
            cat = jnp.concatenate([dpre, next8], axis=0)
            dx = dpre * wv[CONV_K - 1:CONV_K]
            for s in range(1, CONV_K):
                dx = dx + pltpu.roll(cat, CHUNK + 8 - s, 0)[0:CHUNK] * wv[CONV_K - 1 - s:CONV_K - s]
            dx_ref[r0:r0 + CHUNK, :] = dx.astype(_MXU)
            for s in range(CONV_K):
                k = CONV_K - 1 - s
                dws[k] = dws[k] + jnp.sum(dpre * shifted[s], axis=0, keepdims=True)
            db = db + jnp.sum(dpre, axis=0, keepdims=True)
            next8 = dpre[0:8]
        dw_ref[...] = jnp.concatenate(dws, axis=0)
        db_ref[...] = db

    return pl.pallas_call(
        body, name=name, grid=(cols // tc,),
        in_specs=[pl.BlockSpec((T_ROWS, tc), lambda j: (0, j)), pl.BlockSpec((T_ROWS, tc), lambda j: (0, j)),
                  pl.BlockSpec((CONV_K, tc), lambda j: (0, j)), pl.BlockSpec((1, tc), lambda j: (0, j))],
        out_specs=[pl.BlockSpec((T_ROWS, tc), lambda j: (0, j)), pl.BlockSpec((CONV_K, tc), lambda j: (0, j)),
                   pl.BlockSpec((1, tc), lambda j: (0, j))],
        out_shape=[jax.ShapeDtypeStruct((T_ROWS, cols), _MXU), jax.ShapeDtypeStruct((CONV_K, cols), F32),
                   jax.ShapeDtypeStruct((1, cols), F32)],
        compiler_params=_cparams("parallel"),
    )(dy, x, w, b)


def _ssd_chunk_common(dt_raw, prm, c):
    a_row = -jnp.exp(prm[1:2])
    dt = _softplus(dt_raw + prm[0:1])
    rows = lax.broadcasted_iota(jnp.int32, (CHUNK, 1), 0)
    real = jnp.logical_or(c > 0, rows >= PAD_ROWS)
    dt = jnp.where(real, dt, 0.0)
    li = lax.broadcasted_iota(jnp.int32, (CHUNK, CHUNK), 0)
    si = lax.broadcasted_iota(jnp.int32, (CHUNK, CHUNK), 1)
    causal = li >= si
    tri = causal.astype(F32)
    cs = _dot_onehot(tri, dt * a_row, data=1)
    return dt, a_row, cs, cs.T, causal, tri, real


def _gated_norm_fwd(y, z, w):
    g = y * _silu(z)
    half = SSD_WIDTH // SSD_GROUPS
    outs = [_rms_fwd(g[:, k * half:(k + 1) * half], w[:, k * half:(k + 1) * half]) for k in range(SSD_GROUPS)]
    return jnp.concatenate(outs, axis=1)


GROUP_W = SSD_WIDTH // SSD_GROUPS
PAIR_W = 2 * SSD_HEAD_DIM
STATE_SHAPE = (SSD_GROUPS, SSD_STATE, GROUP_W)


def _head_expander():
    r = lax.broadcasted_iota(jnp.int32, (128, SSD_WIDTH), 0)
    c = lax.broadcasted_iota(jnp.int32, (128, SSD_WIDTH), 1)
    return (c // SSD_HEAD_DIM == r).astype(F32)


def _ssd_expand(dt, cs, prm, ex):
    cs_x = _dot_onehot(cs, ex)
    cs_last_x = cs_x[CHUNK - 1:CHUNK, :]
    return (_dot_onehot(dt, ex, pieces=2), _dot_onehot(prm, ex)[2:3], jnp.exp(cs_x), jnp.exp(cs_last_x),
            jnp.exp(cs_last_x - cs_x))


def _ssd_fwd(xs, bc, dt_raw, z, prm, norm_w, ex):
    def body(xs_ref, bc_ref, dt_ref, z_ref, prm_ref, nw_ref, ex_ref, y_ref, yn_ref, prev_ref, state):
        c = pl.program_id(0)

        @pl.when(c == 0)
        def _():
            state[...] = jnp.zeros_like(state)

        prm = prm_ref[...]
        dt, a_row, cs, cs_t, causal, _, _ = _ssd_chunk_common(dt_ref[...], prm, c)
        dt_x, d_x, e_cs_x, e_last_x, dec_x = _ssd_expand(dt, cs, prm, ex_ref[...])
        xs_all = xs_ref[...]
        bc_all = bc_ref[...]
        xdt = xs_all * dt_x
        xdec = xdt * dec_x
        lane_lo = lax.broadcasted_iota(jnp.int32, (1, PAIR_W), 1) < SSD_HEAD_DIM
        for g in range(SSD_GROUPS):
            gs = slice(g * GROUP_W, (g + 1) * GROUP_W)
            b_g = bc_all[:, g * SSD_STATE:(g + 1) * SSD_STATE]
            c_g = bc_all[:, (SSD_GROUPS + g) * SSD_STATE:(SSD_GROUPS + g + 1) * SSD_STATE]
            st = state[g]
            prev_ref[0, g] = st
            y_off = _dot(c_g, st) * e_cs_x[:, gs]
            state[g] = st * e_last_x[:, gs] + _dot(b_g.T, xdec[:, gs])
            cb = _dot(c_g, b_g, NT)
            for k in range(SSD_HPG // 2):
                h0 = g * SSD_HPG + 2 * k
                ps = slice(h0 * SSD_HEAD_DIM, h0 * SSD_HEAD_DIM + PAIR_W)
                xdt_pair = xdt[:, ps]
                yd = []
                for h in (h0, h0 + 1):
                    lmat = jnp.where(causal, jnp.exp(cs[:, h:h + 1] - cs_t[h:h + 1, :]), 0.0)
                    yd.append(_dot(cb * lmat, xdt_pair))
                y_ref[:, ps] = (jnp.where(lane_lo, yd[0], yd[1]) + y_off[:, k * PAIR_W:(k + 1) * PAIR_W]
                                + xs_all[:, ps] * d_x[:, ps])
        yn_ref[...] = _gated_norm_fwd(y_ref[...], z_ref[...], nw_ref[...]).astype(_MXU)

    row = lambda w: pl.BlockSpec((CHUNK, w), lambda c: (c, 0))
    return pl.pallas_call(
        body, name="ssd_fwd", grid=(N_CHUNKS,),
        in_specs=[row(SSD_WIDTH), row(512), row(128), row(SSD_WIDTH),
                  pl.BlockSpec((8, 128), lambda c: (0, 0)), pl.BlockSpec((1, SSD_WIDTH), lambda c: (0, 0)),
                  pl.BlockSpec((128, SSD_WIDTH), lambda c: (0, 0))],
        out_specs=[row(SSD_WIDTH), row(SSD_WIDTH),
                   pl.BlockSpec((1,) + STATE_SHAPE, lambda c: (c, 0, 0, 0))],
        out_shape=[jax.ShapeDtypeStruct((T_ROWS, SSD_WIDTH), F32), jax.ShapeDtypeStruct((T_ROWS, SSD_WIDTH), _MXU),
                   jax.ShapeDtypeStruct((N_CHUNKS,) + STATE_SHAPE, F32)],
        scratch_shapes=[pltpu.VMEM(STATE_SHAPE, F32)],
        compiler_params=_cparams("arbitrary"),
    )(xs, bc, dt_raw, z, prm, norm_w, ex)


def _ssd_bwd(dyn, dyn_block, z, y_pre, xs, bc, dt_raw, prev, prm, norm_w, ex):
    def body(dyn_ref, z_ref, y_ref, xs_ref, bc_ref, dt_ref, prev_ref, prm_ref, nw_ref, ex_ref,
             dz_ref, dxs_ref, dbc_ref, ddt_ref, dprm_ref, dnw_ref, dstate):
        step = pl.program_id(0)
        c = N_CHUNKS - 1 - step

        @pl.when(step == 0)
        def _():
            dstate[...] = jnp.zeros_like(dstate)
            dprm_ref[...] = jnp.zeros_like(dprm_ref)
            dnw_ref[...] = jnp.zeros_like(dnw_ref)

        prm = prm_ref[...]
        dt, a_row, cs, cs_t, causal, tri, real = _ssd_chunk_common(dt_ref[...], prm, c)
        realf = real.astype(F32)
        z = z_ref[...]
        y_all = y_ref[...]
        nw = nw_ref[...]
        dyn_all = dyn_ref[...]
        sz = _silu(z)
        gated = y_all * sz
        half = SSD_WIDTH // SSD_GROUPS
        dgs, dnws = [], []
        for k in range(SSD_GROUPS):
            sl = slice(k * half, (k + 1) * half)
            dgk, dwk = _rms_bwd(gated[:, sl], nw[:, sl], dyn_all[:, sl])
            dgs.append(dgk)
            dnws.append(jnp.sum(dwk, axis=0, keepdims=True))
        dgated = jnp.concatenate(dgs, axis=1)
        dnw_ref[...] += jnp.concatenate(dnws, axis=1)
        dz_ref[...] = (dgated * y_all * _silu_grad(z)).astype(_MXU)
        dy_all = dgated * sz

        ex = ex_ref[...]
        dt_x, d_x, e_cs_x, e_last_x, dec_x = _ssd_expand(dt, cs, prm, ex)
        xs_all = xs_ref[...]
        bc_all = bc_ref[...]
        xdt = xs_all * dt_x
        xdt_mxu = xdt.astype(_MXU).astype(F32)
        xdec = xdt * dec_x
        dcp = dy_all * e_cs_x
        lane_lo = lax.broadcasted_iota(jnp.int32, (1, PAIR_W), 1) < SSD_HEAD_DIM
        upper = (lax.broadcasted_iota(jnp.int32, (CHUNK, CHUNK), 0)
                 <= lax.broadcasted_iota(jnp.int32, (CHUNK, CHUNK), 1))
        last_row = (lax.broadcasted_iota(jnp.int32, (CHUNK, 1), 0) == CHUNK - 1).astype(F32)
        dbs, dcs_, dxdt_parts, last_parts = [], [], [], []
        for g in range(SSD_GROUPS):
            gs = slice(g * GROUP_W, (g + 1) * GROUP_W)
            b_g = bc_all[:, g * SSD_STATE:(g + 1) * SSD_STATE]
            c_g = bc_all[:, (SSD_GROUPS + g) * SSD_STATE:(SSD_GROUPS + g + 1) * SSD_STATE]
            prev_t = prev_ref[0, g]
            dst = dstate[g]
            dc_g = _dot(dcp[:, gs], prev_t, NT)
            db_g = _dot(xdec[:, gs], dst, NT)
            dxdt_state = _dot(b_g, dst) * dec_x[:, gs]
            dstate[g] = dst * e_last_x[:, gs] + _dot(c_g.T, dcp[:, gs])
            last_parts.append(jnp.sum(xdt_mxu[:, gs] * dxdt_state, axis=0, keepdims=True)
                              + jnp.sum(dst * prev_t, axis=0, keepdims=True) * e_last_x[:, gs])
            cb_t = _dot(b_g, c_g, NT)
            dcb_t = jnp.zeros((CHUNK, CHUNK), F32)
            for k in range(SSD_HPG // 2):
                h0 = g * SSD_HPG + 2 * k
                ps = slice(h0 * SSD_HEAD_DIM, h0 * SSD_HEAD_DIM + PAIR_W)
                dy_pair = dy_all[:, ps]
                xdt_pair = xdt[:, ps]
                dd = []
                for h in (h0, h0 + 1):
                    lmat_t = jnp.where(upper, jnp.exp(cs_t[h:h + 1, :] - cs[:, h:h + 1]), 0.0)
                    dd.append(_dot(cb_t * lmat_t, dy_pair))
                    mine = lane_lo if h == h0 else jnp.logical_not(lane_lo)
                    dcb_t = dcb_t + _dot(jnp.where(mine, xdt_pair, 0.0), dy_pair, NT) * lmat_t
                dxdt_parts.append(jnp.where(lane_lo, dd[0], dd[1]) + dxdt_state[:, k * PAIR_W:(k + 1) * PAIR_W])
            dc_g = dc_g + _dot(dcb_t, b_g, TN)
            db_g = db_g + _dot(dcb_t, c_g)
            dbs.append(db_g * realf)
            dcs_.append(dc_g * realf)
        dbc_ref[...] = jnp.concatenate(dbs + dcs_, axis=1)
        dxdt = jnp.concatenate(dxdt_parts, axis=1)
        dxs_ref[...] = (dxdt * dt_x + dy_all * d_x) * realf
        ddt_all = _dot_onehot(dxdt * xs_all, ex, NT, pieces=2)
        rows = jnp.concatenate([jnp.concatenate(last_parts, axis=1), jnp.sum(dy_all * xs_all, axis=0, keepdims=True),
                                jnp.zeros((6, SSD_WIDTH), F32)], axis=0)
        rows = _dot_onehot(rows, ex, NT, pieces=2)
        dd_row = rows[1:2]
        dy_mxu = dy_all.astype(_MXU).astype(F32)
        dcs_all = (_dot_onehot(dy_mxu * (y_all - xs_all * d_x), ex, NT) - _dot_onehot(xdt_mxu * dxdt, ex, NT)
                   + last_row * rows[0:1])
        dda = _dot_onehot(tri, dcs_all, TN, data=1)
        ddt = (ddt_all + dda * a_row) * realf
        ddt_raw = ddt * _sigmoid(dt_ref[...] + prm[0:1])
        ddt_ref[...] = ddt_raw.astype(_MXU)
        da_log = jnp.sum(dda * dt, axis=0, keepdims=True) * a_row
        dprm_ref[0:1, :] += jnp.sum(ddt_raw, axis=0, keepdims=True)
        dprm_ref[1:2, :] += da_log
        dprm_ref[2:3, :] += dd_row

    rev = lambda w, blk=0: pl.BlockSpec((CHUNK, w), lambda s, blk=blk: (N_CHUNKS - 1 - s, blk))
    return pl.pallas_call(
        body, name="ssd_bwd", grid=(N_CHUNKS,),
        in_specs=[rev(SSD_WIDTH, dyn_block), rev(SSD_WIDTH), rev(SSD_WIDTH), rev(SSD_WIDTH), rev(512), rev(128),
                  pl.BlockSpec((1,) + STATE_SHAPE, lambda s: (N_CHUNKS - 1 - s, 0, 0, 0)),
                  pl.BlockSpec((8, 128), lambda s: (0, 0)), pl.BlockSpec((1, SSD_WIDTH), lambda s: (0, 0)),
                  pl.BlockSpec((128, SSD_WIDTH), lambda s: (0, 0))],
        out_specs=[rev(SSD_WIDTH), rev(SSD_WIDTH), rev(512), rev(128),
                   pl.BlockSpec((8, 128), lambda s: (0, 0)), pl.BlockSpec((1, SSD_WIDTH), lambda s: (0, 0))],
        out_shape=[jax.ShapeDtypeStruct((T_ROWS, SSD_WIDTH), _MXU), jax.ShapeDtypeStruct((T_ROWS, SSD_WIDTH), F32),
                   jax.ShapeDtypeStruct((T_ROWS, 512), F32), jax.ShapeDtypeStruct((T_ROWS, 128), _MXU),
                   jax.ShapeDtypeStruct((8, 128), F32), jax.ShapeDtypeStruct((1, SSD_WIDTH), F32)],
        scratch_shapes=[pltpu.VMEM(STATE_SHAPE, F32)],
        compiler_params=_cparams("arbitrary"),
    )(dyn, z, y_pre, xs, bc, dt_raw, prev, prm, norm_w, ex)


LRU_PAIRS = 8


def _lru_gates(xr, wa_ref, wx_ref, prm):
    pre_r, pre_i = [], []
    for k in range(LRU_PAIRS):
        xk = xr[:, k * 128:(k + 1) * 128]
        pre_r.append(_dot(xk, wa_ref[k]))
        pre_i.append(_dot(xk, wx_ref[k]))
    r = _sigmoid(jnp.concatenate(pre_r, axis=1) + prm[0:1])
    i = _sigmoid(jnp.concatenate(pre_i, axis=1) + prm[1:2])
    sp = _softplus(-prm[2:3])
    log_a = (-LRU_C) * r * sp
    a = jnp.exp(log_a)
    s = jnp.sqrt(-jnp.tanh(log_a) * (a * a + 1.0))
    return r, i, a, s, sp


def _lru_fwd(xr, gate, wa, wx, prm):
    def body(xr_ref, g_ref, wa_ref, wx_ref, prm_ref, hs_ref, yn_ref, carry, a_s, u_s):
        @pl.when(pl.program_id(0) == 0)
        def _():
            carry[...] = jnp.zeros_like(carry)

        prm = prm_ref[...]
        xr_t = xr_ref[...]
        _, i, a, s, _ = _lru_gates(xr_t, wa_ref, wx_ref, prm)
        a_s[...] = a
        u_s[...] = s * (i * xr_t)
        rid = lax.broadcasted_iota(jnp.int32, (8, LRU_WIDTH), 0)

        def group(k, before):
            off = pl.multiple_of(k * 8, 8)
            a8 = a_s[pl.ds(off, 8), :]
            u8 = u_s[pl.ds(off, 8), :]
            for d in (1, 2, 4):
                keep = rid >= d
                u8 = u8 + a8 * jnp.where(keep, pltpu.roll(u8, d, 0), 0.0)
                a8 = a8 * jnp.where(keep, pltpu.roll(a8, d, 0), 1.0)
            h8 = u8 + a8 * before
            hs_ref[pl.ds(off, 8), :] = h8
            return jnp.broadcast_to(h8[7:8], (8, LRU_WIDTH))

        carry[...] = lax.fori_loop(0, CHUNK // 8, group, carry[...])
        gel, _ = _gelu_and_grad(g_ref[...])
        yn_ref[...] = _rms_fwd(gel * hs_ref[...], prm[3:4]).astype(_MXU)

    row = pl.BlockSpec((CHUNK, LRU_WIDTH), lambda t: (t, 0))
    wspec = pl.BlockSpec((LRU_PAIRS, 128, 128), lambda t: (0, 0, 0))
    return pl.pallas_call(
        body, name="lru_fwd", grid=(N_CHUNKS,),
        in_specs=[row, row, wspec, wspec, pl.BlockSpec((8, LRU_WIDTH), lambda t: (0, 0))],
        out_specs=[row, row],
        out_shape=[jax.ShapeDtypeStruct((T_ROWS, LRU_WIDTH), F32), jax.ShapeDtypeStruct((T_ROWS, LRU_WIDTH), _MXU)],
        scratch_shapes=[pltpu.VMEM((8, LRU_WIDTH), F32), pltpu.VMEM((CHUNK, LRU_WIDTH), F32),
                        pltpu.VMEM((CHUNK, LRU_WIDTH), F32)],
        compiler_params=_cparams("arbitrary"),
    )(xr, gate, wa, wx, prm)


def _lru_bwd(dyn, dyn_block, gate, xr, hs, wa, wx, wa_t, wx_t, prm):
    def body(dyn_ref, g_ref, xr_ref, hs_ref, hsp_ref, wa_ref, wx_ref, wat_ref, wxt_ref, prm_ref,
             dg_ref, dxr_ref, dwa_ref, dwx_ref, dprm_ref, carry, a_s, d_s):
        step = pl.program_id(0)
        tile = N_CHUNKS - 1 - step

        @pl.when(step == 0)
        def _():
            carry[...] = jnp.zeros_like(carry)
            dwa_ref[...] = jnp.zeros_like(dwa_ref)
            dwx_ref[...] = jnp.zeros_like(dwx_ref)
            dprm_ref[...] = jnp.zeros_like(dprm_ref)

        prm = prm_ref[...]
        xr_t = xr_ref[...]
        r, i, a, s, sp = _lru_gates(xr_t, wa_ref, wx_ref, prm)
        hs_t = hs_ref[...]
        gel, dgel = _gelu_and_grad(g_ref[...])
        dy, dnw = _rms_bwd(gel * hs_t, prm[3:4], dyn_ref[...])
        dg_ref[...] = (dy * hs_t * dgel).astype(_MXU)
        a_s[...] = a
        d_s[...] = dy * gel
        rid = lax.broadcasted_iota(jnp.int32, (8, LRU_WIDTH), 0)

        def group(k, behind):
            off = pl.multiple_of((CHUNK // 8 - 1 - k) * 8, 8)
            a8 = a_s[pl.ds(off, 8), :]
            d8 = d_s[pl.ds(off, 8), :]
            c8 = jnp.where(rid == 7, 1.0, pltpu.roll(a8, 7, 0))
            for d in (1, 2, 4):
                keep = rid < 8 - d
                d8 = d8 + c8 * jnp.where(keep, pltpu.roll(d8, 8 - d, 0), 0.0)
                c8 = c8 * jnp.where(keep, pltpu.roll(c8, 8 - d, 0), 1.0)
            dht8 = d8 + c8 * behind
            d_s[pl.ds(off, 8), :] = dht8
            return jnp.broadcast_to(a8[0:1] * dht8[0:1], (8, LRU_WIDTH))

        carry[...] = lax.fori_loop(0, CHUNK // 8, group, carry[...])
        dht = d_s[...]
        before = hsp_ref[CHUNK - 8:CHUNK, :][7:8] * (tile > 0).astype(F32)
        first = lax.broadcasted_iota(jnp.int32, (CHUNK, 1), 0) == 0
        hprev = jnp.where(first, before, pltpu.roll(hs_t, 1, 0))
        da = dht * hprev
        ixr = i * xr_t
        ds = dht * ixr
        dlog_a = da * a - ds * (a * a) * lax.rsqrt(s * s)
        dr = dlog_a * ((-LRU_C) * sp)
        dsp = jnp.sum(dlog_a * ((-LRU_C) * r), axis=0, keepdims=True)
        dlam = dsp * (-_sigmoid(-prm[2:3]))
        di = dht * s * xr_t
        dpre_r = dr * r * (1.0 - r)
        dpre_i = di * i * (1.0 - i)
        dxr = dht * s * i
        parts = []
        for k in range(LRU_PAIRS):
            sl = slice(k * 128, (k + 1) * 128)
            parts.append(_dot(dpre_r[:, sl], wat_ref[k]) + _dot(dpre_i[:, sl], wxt_ref[k]))
            dwa_ref[k] += _dot(xr_t[:, sl], dpre_r[:, sl], TN)
            dwx_ref[k] += _dot(xr_t[:, sl], dpre_i[:, sl], TN)
        dxr_ref[...] = dxr + jnp.concatenate(parts, axis=1)
        dprm_ref[0:1, :] += jnp.sum(dpre_r, axis=0, keepdims=True)
        dprm_ref[1:2, :] += jnp.sum(dpre_i, axis=0, keepdims=True)
        dprm_ref[2:3, :] += dlam
        dprm_ref[3:4, :] += jnp.sum(dnw, axis=0, keepdims=True)

    rev = lambda blk=0: pl.BlockSpec((CHUNK, LRU_WIDTH), lambda s, blk=blk: (N_CHUNKS - 1 - s, blk))
    wspec = pl.BlockSpec((LRU_PAIRS, 128, 128), lambda s: (0, 0, 0))
    return pl.pallas_call(
        body, name="lru_bwd", grid=(N_CHUNKS,),
        in_specs=[rev(dyn_block), rev(), rev(), rev(),
                  pl.BlockSpec((CHUNK, LRU_WIDTH), lambda s: (jnp.maximum(N_CHUNKS - 2 - s, 0), 0)),
                  wspec, wspec, wspec, wspec, pl.BlockSpec((8, LRU_WIDTH), lambda s: (0, 0))],
        out_specs=[rev(), rev(), wspec, wspec, pl.BlockSpec((8, LRU_WIDTH), lambda s: (0, 0))],
        out_shape=[jax.ShapeDtypeStruct((T_ROWS, LRU_WIDTH), _MXU), jax.ShapeDtypeStruct((T_ROWS, LRU_WIDTH), F32),
                   jax.ShapeDtypeStruct((LRU_PAIRS, 128, 128), F32), jax.ShapeDtypeStruct((LRU_PAIRS, 128, 128), F32),
                   jax.ShapeDtypeStruct((8, LRU_WIDTH), F32)],
        scratch_shapes=[pltpu.VMEM((8, LRU_WIDTH), F32), pltpu.VMEM((CHUNK, LRU_WIDTH), F32),
                        pltpu.VMEM((CHUNK, LRU_WIDTH), F32)],
        compiler_params=_cparams("arbitrary"),
    )(dyn, gate, xr, hs, hs, wa, wx, wa_t, wx_t, prm)


SEC_NAMES = ("z", "xs", "bc", "dt", "g", "x")
SEC_WIDTH = {"z": 1024, "xs": 1024, "bc": 512, "dt": 128, "g": 1024, "x": 1024}


def _pair_blocks(w):
    w = w.reshape(LRU_PAIRS, 2, 64, 64)
    zero = jnp.zeros((LRU_PAIRS, 64, 64), w.dtype)
    top = jnp.concatenate([w[:, 0], zero], axis=2)
    bot = jnp.concatenate([zero, w[:, 1]], axis=2)
    return jnp.concatenate([top, bot], axis=1)


def _unpair_blocks(wp):
    return jnp.stack([wp[:, :64, :64], wp[:, 64:, 64:]], axis=1).reshape(16, 64, 64)


def _pad_lanes(v, width=128):
    return jnp.pad(v, ((0, 0), (0, width - v.shape[1])))


class _Resident:
    before_embed = ()

    def __init__(self, w_in_sections, w_out, w_gate, w_up, w_down):
        self._w_in, self._w_out, self._ffn = w_in_sections, w_out, (w_gate, w_up, w_down)

    def w_in(self, after):
        return self._w_in

    def mid_forward(self, after):
        return jnp.zeros((1, 1), F32)

    def w_out(self, after):
        return self._w_out

    def ffn(self, after):
        return self._ffn

    def grads_ready(self, names, g, g_mxu):
        return jnp.zeros((1, 1), F32)

    def small_ready(self, g, loss):
        return jnp.zeros((1, 1), F32)

    def small_middle(self, after):
        return jnp.zeros((1, 1), F32)


def _local_step(x, target, meta, p, late):
    g, g_mxu = {}, {}
    ex = _head_expander()
    h0 = _embed(x, meta, late.before_embed)
    w_in = late.w_in(h0)
    u1, projs = _norm_proj(h0, p["norm1_w"], [w_in[s] for s in SEC_NAMES], name="norm_in_proj")
    proj = dict(zip(SEC_NAMES, projs))
    ssd_prm = jnp.concatenate([_pad_lanes(p["ssd_dt_bias"]), _pad_lanes(p["ssd_a_log"]), _pad_lanes(p["ssd_d"]),
                               jnp.zeros((5, 128), F32)], axis=0)
    xs_act = _conv_fwd(proj["xs"], p["ssd_conv_w"][:, :SSD_WIDTH], p["ssd_conv_b"][:, :SSD_WIDTH], silu=True,
                       name="ssd_conv_xs")
    bc_act = _conv_fwd(proj["bc"], p["ssd_conv_w"][:, SSD_WIDTH:], p["ssd_conv_b"][:, SSD_WIDTH:], silu=True,
                       name="ssd_conv_bc")
    y_pre, y_ssd, prev = _ssd_fwd(xs_act, bc_act, proj["dt"], proj["z"], ssd_prm, p["ssd_norm_w"], ex)
    xr = _conv_fwd(proj["x"], p["lru_conv_w"], p["lru_conv_b"], silu=False, name="lru_conv")
    wa_p, wx_p = _pair_blocks(p["lru_wa"]), _pair_blocks(p["lru_wx"])
    lru_prm = jnp.concatenate([p["lru_ba"], p["lru_bx"], p["lru_lambda"], p["lru_norm_w"],
                               jnp.zeros((4, LRU_WIDTH), F32)], axis=0)
    hs, y_lru = _lru_fwd(xr, proj["g"], wa_p.astype(_MXU), wx_p.astype(_MXU),
                         lru_prm + late.mid_forward([xr, y_ssd]))
    ycat = jnp.concatenate([y_ssd, y_lru], axis=1)
    w_out = late.w_out(ycat)
    h1 = _mm([(ycat, 0, w_out, 0, 2 * D_MODEL)], T_ROWS, D_MODEL, tm=T_ROWS, tn=256, mode="nn", out_dtype=F32,
             name="out_proj", residual=h0)
    u2 = _rmsnorm(h1, p["norm2_w"], name="norm2")
    w_gate, w_up, w_down = late.ffn(u2)
    gp, up, act = _ffn_up(u2, w_gate, w_up)
    h2 = _mm([(act, 0, w_down, 0, D_FF)], T_ROWS, D_MODEL, tm=T_ROWS, tn=256, mode="nn", out_dtype=F32,
             name="ffn_down", residual=h1)
    loss, dh2, dh2b, g["final_norm_w"] = _loss_head(h2, target, p["final_norm_w"])
    dgp, dup = _ffn_bwd_act(dh2b, w_down, gp, up)
    g["w_down"], g_mxu["w_down"] = _mm([(act, 0, dh2b, 0, T_ROWS)], D_FF, D_MODEL, tm=1408, tn=512, mode="tn",
                                       out_dtype=F32, name="dw_down", also_mxu=True)
    dh1, dh1b, g["norm2_w"] = _mm_norm_bwd([(dgp, w_gate, D_FF), (dup, w_up, D_FF)], h1, p["norm2_w"], dh2,
                                           name="ffn_bwd_in")
    g["w_gate"], g_mxu["w_gate"] = _mm([(dgp, 0, u2, 0, T_ROWS)], D_FF, D_MODEL, tm=1408, tn=512, mode="tn",
                                       out_dtype=F32, name="dw_gate", also_mxu=True)
    g["w_up"], g_mxu["w_up"] = _mm([(dup, 0, u2, 0, T_ROWS)], D_FF, D_MODEL, tm=1408, tn=512, mode="tn",
                                   out_dtype=F32, name="dw_up", also_mxu=True)
    g["w_out"], g_mxu["w_out"] = _mm([(ycat, 0, dh1b, 0, T_ROWS)], 2 * D_MODEL, D_MODEL, tm=1024, tn=512, mode="tn",
                                     out_dtype=F32, name="dw_out", also_mxu=True)
    sent = late.grads_ready(("w_down", "w_gate", "w_up", "w_out"), g, g_mxu)
    dycat = _mm([(dh1b, 0, w_out, 0, D_MODEL)], T_ROWS, 2 * D_MODEL, tm=T_ROWS, tn=256, mode="nt", out_dtype=F32,
                name="out_proj_bwd", behind=(sent,))
    dgate, dxr, dwa_p, dwx_p, dlru_prm = _lru_bwd(dycat, 1, proj["g"], xr, hs, wa_p.astype(_MXU), wx_p.astype(_MXU),
                                                  jnp.swapaxes(wa_p, 1, 2).astype(_MXU),
                                                  jnp.swapaxes(wx_p, 1, 2).astype(_MXU), lru_prm)
    g["lru_wa"], g["lru_wx"] = _unpair_blocks(dwa_p), _unpair_blocks(dwx_p)
    g["lru_ba"], g["lru_bx"], g["lru_lambda"], g["lru_norm_w"] = (dlru_prm[k:k + 1] for k in range(4))
    dx_lru, g["lru_conv_w"], g["lru_conv_b"] = _conv_bwd(dxr, proj["x"], p["lru_conv_w"], p["lru_conv_b"], silu=False,
                                                         name="lru_conv_bwd")
    dz, dxs_act, dbc_act, ddt, dssd_prm, g["ssd_norm_w"] = _ssd_bwd(dycat, 0, proj["z"], y_pre, xs_act, bc_act,
                                                                    proj["dt"], prev, ssd_prm, p["ssd_norm_w"], ex)
    g["ssd_dt_bias"], g["ssd_a_log"], g["ssd_d"] = (dssd_prm[k:k + 1, :SSD_HEADS] for k in range(3))
    dxs, dcw_xs, dcb_xs = _conv_bwd(dxs_act, proj["xs"], p["ssd_conv_w"][:, :SSD_WIDTH],
                                    p["ssd_conv_b"][:, :SSD_WIDTH], silu=True, name="ssd_conv_xs_bwd")
    dbc, dcw_bc, dcb_bc = _conv_bwd(dbc_act, proj["bc"], p["ssd_conv_w"][:, SSD_WIDTH:],
                                    p["ssd_conv_b"][:, SSD_WIDTH:], silu=True, name="ssd_conv_bc_bwd")
    g["ssd_conv_w"] = jnp.concatenate([dcw_xs, dcw_bc], axis=1)
    g["ssd_conv_b"] = jnp.concatenate([dcb_xs, dcb_bc], axis=1)
    dproj = {"z": dz, "xs": dxs, "bc": dbc, "dt": ddt, "g": dgate, "x": dx_lru}
    dh0, _, g["norm1_w"] = _mm_norm_bwd([(dproj[s], w_in[s], SEC_WIDTH[s]) for s in SEC_NAMES], h0,
                                        p["norm1_w"], dh1, name="in_proj_bwd")
    g["meta_tokens"] = dh0[PAD_ROWS:X_ROW0]
    sent = late.small_ready(g, loss)
    for s in SEC_NAMES:
        wdt = SEC_WIDTH[s]
        g["w_in_" + s], g_mxu["w_in_" + s] = _mm([(dproj[s], 0, u1, 0, T_ROWS)], wdt, D_MODEL, tm=min(wdt, 1024),
                                                 tn=512, mode="tn", out_dtype=F32, name="dw_in_" + s, also_mxu=True,
                                                 behind=(sent,))
        if s == "bc":
            sent = late.small_middle([g["w_in_z"], g["w_in_xs"], g["w_in_bc"]])
    late.grads_ready(("w_in",), g, g_mxu)
    return loss, dh0[X_ROW0:], g, g_mxu


MESH = pl.DeviceIdType.MESH
ANY = pl.BlockSpec(memory_space=pl.ANY)


def _my_place():
    return lax.axis_index("x"), lax.axis_index("y"), lax.axis_index("c")


def _other_chips(x, y):
    return [(1 - x, y), (x, 1 - y), (1 - x, 1 - y)]


HBM_SPEC = pl.BlockSpec(memory_space=pltpu.HBM)
SEM_SPEC = pl.BlockSpec(memory_space=pltpu.SEMAPHORE)
SPLIT_EFFECT = pltpu.SideEffectType.DATAFLOW_SIDE_EFFECTING


def _half_cols(buf, c, other=False):
    half = buf.shape[-1] // 2
    return pl.ds(pl.multiple_of(((1 - c) if other else c) * half, 128), half)


def _halves_plan(bufs, x, y, c, incoming):
    plan = []
    for buf in bufs:
        cols = _half_cols(buf, c)
        for (px, py) in _other_chips(x, y):
            slot = 2 * px + py if incoming else 2 * x + y
            plan.append((buf.at[2 * x + y, :, cols], buf.at[slot, :, cols], (px, py, c)))
    return plan


def _forward_plan(bufs, x, y, c, incoming):
    plan = []
    for buf in bufs:
        for (px, py) in _other_chips(x, y):
            slot = 2 * px + py
            plan.append((buf.at[slot, :, _half_cols(buf, c)], buf.at[slot, :, _half_cols(buf, c, other=incoming)],
                         (x, y, 1 - c)))
    return plan


def _scatter_plan(bufs, x, y, c, incoming):
    n = len(bufs) // 2
    plan = []
    for k in range(n):
        for j, (px, py) in enumerate(_other_chips(x, y)):
            plan.append((bufs[k].at[2 * px + py], bufs[n + k].at[j], (px, py, c)))
    return plan


def _split_start(bufs, plan, n_copies, after, *, name):
    n = len(bufs)
    extra = [] if after is None else [after]

    def body(*refs):
        ins = refs[:n]
        send_sems, recv_sems = refs[n + len(extra)], refs[n + len(extra) + 1]
        token = refs[-1]
        x, y, c = _my_place()
        for i, (src, dst, dev) in enumerate(plan(ins, x, y, c, False)):
            pltpu.make_async_remote_copy(src_ref=src, dst_ref=dst, send_sem=send_sems.at[i], recv_sem=recv_sems.at[i],
                                         device_id=dev, device_id_type=MESH).start()
        token[...] = jnp.zeros_like(token)

    outs = pl.pallas_call(
        body, name=name,
        out_shape=(pltpu.SemaphoreType.DMA((n_copies,)), pltpu.SemaphoreType.DMA((n_copies,)),
                   *[pltpu.HBM(b.shape, b.dtype) for b in bufs], jax.ShapeDtypeStruct((8, 128), F32)),
        in_specs=[HBM_SPEC] * n + [ANY] * len(extra),
        out_specs=(SEM_SPEC, SEM_SPEC, *[HBM_SPEC] * n, pl.BlockSpec(memory_space=pltpu.VMEM)),
        input_output_aliases={k: 2 + k for k in range(n)},
        compiler_params=pltpu.CompilerParams(has_side_effects=SPLIT_EFFECT),
    )(*[pltpu.with_memory_space_constraint(b, pltpu.HBM) for b in bufs], *extra)
    return outs[0], outs[1], list(outs[2:2 + n]), outs[-1]


def _split_wait(bufs, send_sems, recv_sems, plan, after, *, name):
    n = len(bufs)
    after = list(after) if isinstance(after, (list, tuple)) else [after]

    def body(*refs):
        ins = refs[:n]
        send_sems_ref, recv_sems_ref = refs[n], refs[n + 1]
        x, y, c = _my_place()
        for i, (src, dst, dev) in enumerate(plan(ins, x, y, c, True)):
            cp = pltpu.make_async_remote_copy(src_ref=src, dst_ref=dst, send_sem=send_sems_ref.at[i],
                                              recv_sem=recv_sems_ref.at[i], device_id=dev, device_id_type=MESH)
            cp.wait_send()
            cp.wait_recv()

    outs = pl.pallas_call(
        body, name=name, out_shape=tuple(pltpu.HBM(b.shape, b.dtype) for b in bufs),
        in_specs=[HBM_SPEC] * n + [SEM_SPEC, SEM_SPEC] + [ANY] * len(after), out_specs=tuple([HBM_SPEC] * n),
        input_output_aliases={k: k for k in range(n)},
        compiler_params=pltpu.CompilerParams(has_side_effects=SPLIT_EFFECT),
    )(*bufs, send_sems, recv_sems, *after)
    return list(outs)


def _fill_own_slots(shards, me_arr, *, name, behind=()):
    n = len(shards)
    n_in = n + len(behind)

    def body(me_ref, *refs):
        for k in range(n):
            refs[n_in + k][0] = refs[k][...].astype(_MXU)

    half = D_MODEL // 2
    return pl.pallas_call(
        body, name=name,
        grid_spec=pltpu.PrefetchScalarGridSpec(
            num_scalar_prefetch=1, grid=(2,),
            in_specs=[pl.BlockSpec((s.shape[0], half), lambda i, me: (0, i)) for s in shards]
            + [pl.BlockSpec(memory_space=pl.ANY)] * len(behind),
            out_specs=[pl.BlockSpec((1, s.shape[0], half), lambda i, me: (me[0], 0, i)) for s in shards]),
        out_shape=[jax.ShapeDtypeStruct((N_SHARDS,) + s.shape, _MXU) for s in shards],
        compiler_params=_cparams("parallel"),
    )(me_arr, *shards, *behind)


def _gather_small(small):
    def body(s_ref, o_ref, send_sems, recv_sems, local_sem):
        x, y, c = _my_place()
        me = 2 * x + y
        local = pltpu.make_async_copy(s_ref, o_ref.at[me], local_sem)
        local.start()
        copies = [(pltpu.make_async_remote_copy(src_ref=s_ref, dst_ref=o_ref.at[me], send_sem=send_sems.at[j],
                                                recv_sem=recv_sems.at[j], device_id=(px, py, c), device_id_type=MESH),
                   2 * px + py) for j, (px, py) in enumerate(_other_chips(x, y))]
        for cp, _ in copies:
            cp.start()
        for j, (cp, slot) in enumerate(copies):
            cp.wait_send()
            pltpu.make_async_remote_copy(src_ref=s_ref, dst_ref=o_ref.at[slot], send_sem=send_sems.at[j],
                                         recv_sem=recv_sems.at[j], device_id=(x, y, c),
                                         device_id_type=MESH).wait_recv()
        local.wait()

    return pl.pallas_call(
        body, name="gather_small", in_specs=[ANY], out_specs=ANY,
        out_shape=jax.ShapeDtypeStruct((N_SHARDS,) + small.shape, small.dtype),
        scratch_shapes=[pltpu.SemaphoreType.DMA((3,)), pltpu.SemaphoreType.DMA((3,)), pltpu.SemaphoreType.DMA],
    )(small)


def _swap_with_sibling(parts, *, name):
    n = len(parts)

    def body(*refs):
        ins, outs = refs[:n], refs[n:2 * n]
        send_sems, recv_sems = refs[2 * n:]
        x, y, c = _my_place()
        copies = [pltpu.make_async_remote_copy(
            src_ref=ins[k], dst_ref=outs[k], send_sem=send_sems.at[k], recv_sem=recv_sems.at[k],
            device_id=(x, y, 1 - c), device_id_type=MESH) for k in range(n)]
        for cp in copies:
            cp.start()
        for cp in copies:
            cp.wait()

    return pl.pallas_call(
        body, name=name, in_specs=[ANY] * n, out_specs=[ANY] * n,
        out_shape=[jax.ShapeDtypeStruct(a.shape, a.dtype) for a in parts],
        scratch_shapes=[pltpu.SemaphoreType.DMA((n,)), pltpu.SemaphoreType.DMA((n,))],
    )(*parts)


def _other_devices(x, y, c):
    out = []
    for mask in range(1, N_DEV):
        px, py, pc = x ^ (mask >> 2 & 1), y ^ (mask >> 1 & 1), c ^ (mask & 1)
        out.append(((px, py, pc), 4 * px + 2 * py + pc))
    return out


def _pieces_plan(bufs, x, y, c, incoming):
    pack, land = bufs
    me = 4 * x + 2 * y + c
    return [(pack.at[num], land.at[num if incoming else me], dev) for dev, num in _other_devices(x, y, c)]


def _spread_plan(bufs, x, y, c, incoming):
    piece, land = bufs
    me = 4 * x + 2 * y + c
    return [(piece, land.at[num if incoming else me], dev) for dev, num in _other_devices(x, y, c)]


def _sum_pieces(pack, land, dev_arr, *, name):
    def body(dev_ref, pack_ref, land_ref, o_ref):
        dev = dev_ref[0]
        own = pack_ref[dev]
        acc = None
        for d in range(N_DEV):
            term = jnp.where(dev == d, own, land_ref[d])
            acc = term if acc is None else acc + term
        o_ref[...] = acc

    vmem = pl.BlockSpec(memory_space=pltpu.VMEM)
    return pl.pallas_call(
        body, name=name, in_specs=[pl.BlockSpec(memory_space=pltpu.SMEM), vmem, vmem], out_specs=vmem,
        out_shape=jax.ShapeDtypeStruct(pack.shape[1:], F32),
    )(dev_arr, pack, land)


def _join_pieces(piece, land, dev_arr, *, name):
    def body(dev_ref, piece_ref, land_ref, o_ref):
        dev = dev_ref[0]
        for d in range(N_DEV):
            o_ref[d] = jnp.where(dev == d, piece_ref[...], land_ref[d])

    vmem = pl.BlockSpec(memory_space=pltpu.VMEM)
    return pl.pallas_call(
        body, name=name, in_specs=[pl.BlockSpec(memory_space=pltpu.SMEM), vmem, vmem], out_specs=vmem,
        out_shape=jax.ShapeDtypeStruct(land.shape, F32),
    )(dev_arr, piece, land)


def _adamw_native(ws, gs, ms, vs):
    n = len(ws)

    def body(*refs):
        for k in range(n):
            w_ref, g_ref, m_ref, v_ref = (refs[j * n + k] for j in range(4))
            delta, m_new, v_new = _adamw_math(w_ref[...], g_ref[...], m_ref[...], v_ref[...])
            refs[4 * n + k][...] = delta
            refs[5 * n + k][...] = m_new
            refs[6 * n + k][...] = v_new

    vmem = pl.BlockSpec(memory_space=pltpu.VMEM)
    shapes = [jax.ShapeDtypeStruct(a.shape, F32) for a in ws]
    outs = pl.pallas_call(
        body, name="adamw_small", in_specs=[vmem] * (4 * n), out_specs=[vmem] * (3 * n), out_shape=shapes * 3,
        compiler_params=pltpu.CompilerParams(vmem_limit_bytes=VMEM_LIMIT_BYTES),
    )(*ws, *gs, *ms, *vs)
    return outs[:n], outs[n:2 * n], outs[2 * n:]


def _elementwise_tile(rows, cols):
    for t in range(256, 15, -16):
        if rows % t == 0:
            return (t, cols), rows // t, lambda i: (i, 0)
    assert cols % 256 == 0
    return (rows, 256), cols // 256, lambda i: (0, i)


def _partial_sum(own, land, me_arr, *, name):
    r, c = own.shape[-2:]
    tile, steps, imap = _elementwise_tile(r, c)
    whole = own.ndim == 3

    def body(me_ref, own_ref, land_ref, o_ref):
        acc = own_ref[0] if whole else own_ref[...]
        for j in range(3):
            acc = acc + land_ref[j].astype(F32)
        o_ref[...] = acc.astype(_MXU)

    own_spec = (pl.BlockSpec((1,) + tile, lambda i, me: (me[0],) + imap(i)) if whole
                else pl.BlockSpec(tile, lambda i, me: imap(i)))
    return pl.pallas_call(
        body, name=name,
        grid_spec=pltpu.PrefetchScalarGridSpec(
            num_scalar_prefetch=1, grid=(steps,),
            in_specs=[own_spec, pl.BlockSpec((3,) + tile, lambda i, me: (0,) + imap(i))],
            out_specs=pl.BlockSpec(tile, lambda i, me: imap(i))),
        out_shape=jax.ShapeDtypeStruct((r, c), _MXU),
        compiler_params=_cparams("parallel"),
    )(me_arr, own, land)


def _adamw_math(w, g, m, v):
    m = ADAM_B1 * m + (1.0 - ADAM_B1) * g
    v = ADAM_B2 * v + (1.0 - ADAM_B2) * (g * g)
    m_hat = m / (1.0 - ADAM_B1 ** ADAM_STEP)
    v_hat = v / (1.0 - ADAM_B2 ** ADAM_STEP)
    delta = -ADAM_LR * (m_hat / (jnp.sqrt(v_hat) + ADAM_EPS) + ADAM_WD * w)
    return delta, m, v


def _adamw(w, grad_parts, m, v, *, name):
    r, c = w.shape
    tile_shape, steps, imap = _elementwise_tile(r, c)
    n = len(grad_parts)

    def body(*refs):
        w_ref, m_ref, v_ref = refs[:3]
        g_refs = refs[3:3 + n]
        g_out, d_out, m_out, v_out = refs[3 + n:]
        g = g_refs[0][...].astype(F32)
        for k in range(1, n):
            g = g + g_refs[k][...].astype(F32)
        delta, m_new, v_new = _adamw_math(w_ref[...], g, m_ref[...], v_ref[...])
        g_out[...] = g
        d_out[...] = delta
        m_out[...] = m_new
        v_out[...] = v_new

    tile = pl.BlockSpec(tile_shape, imap)
    return pl.pallas_call(
        body, name=name, grid=(steps,), in_specs=[tile] * (3 + n), out_specs=[tile] * 4,
        out_shape=[jax.ShapeDtypeStruct((r, c), F32)] * 4,
        compiler_params=_cparams("parallel"),
    )(w, m, v, *grad_parts)


WEIGHT_NAMES = ("meta_tokens", "norm1_w", "w_in", "ssd_conv_w", "ssd_conv_b", "ssd_dt_bias", "ssd_a_log", "ssd_d",
                "ssd_norm_w", "lru_conv_w", "lru_conv_b", "lru_wa", "lru_ba", "lru_wx", "lru_bx", "lru_lambda",
                "lru_norm_w", "w_out", "norm2_w", "w_gate", "w_up", "w_down", "final_norm_w")
BIG = ("w_in", "w_out", "w_gate", "w_up", "w_down")
FFN = ("w_gate", "w_up", "w_down")
LATE = ("w_out",) + FFN
SMALL_SHARDED = {"meta_tokens": (N_META, D_MODEL), "ssd_conv_w": (CONV_K, 1536), "lru_conv_w": (CONV_K, LRU_WIDTH)}
SMALL = tuple(n for n in WEIGHT_NAMES if n not in BIG)
PACK_COLS = 1024


def _pack(arrays, row_multiple):
    flat = jnp.concatenate([a.reshape(-1) for a in arrays])
    rows = -(-flat.shape[0] // (row_multiple * PACK_COLS)) * row_multiple
    return jnp.pad(flat, (0, rows * PACK_COLS - flat.shape[0])).reshape(rows, PACK_COLS)


def _unpack(pack, shapes):
    flat = pack.reshape(-1)
    out, off = [], 0
    for s in shapes:
        size = math.prod(s)
        out.append(flat[off:off + size].reshape(s))
        off += size
    return out


def _unshard_cols(g4):
    return jnp.swapaxes(g4, 0, 1).reshape(g4.shape[1], -1)


COL_SHARDED = ("w_in", "w_gate", "w_up")
IN_ROWS = {"z": (0, 1024), "xs": (1024, 2048), "bc": (2048, 2560), "dt": (2560, 2576), "g": (2576, 3600),
           "x": (3600, IN_COLS)}


def _rows_of_shards(shards4, lo, hi):
    r = shards4.shape[1]
    parts = [shards4[k, max(lo, k * r) - k * r:min(hi, (k + 1) * r) - k * r]
             for k in range(N_SHARDS) if max(lo, k * r) < min(hi, (k + 1) * r)]
    return parts[0] if len(parts) == 1 else jnp.concatenate(parts, axis=0)


def _w_in_shard_rows(k, sections):
    lo, hi = k * (IN_COLS // N_SHARDS), (k + 1) * (IN_COLS // N_SHARDS)
    parts = []
    for arr, (a, b) in zip(sections, IN_ROWS.values()):
        if max(lo, a) < min(hi, b):
            parts.append(arr[max(lo, a) - a:min(hi, b) - a])
    return jnp.concatenate(parts, axis=0)


def _rows_view(name, block):
    return jnp.swapaxes(block[0], 0, 1) if name in COL_SHARDED else block[0]


def _param_view(name, rows):
    return (jnp.swapaxes(rows, 0, 1) if name in COL_SHARDED else rows)[None]


def kernel(x, meta_tokens, norm1_w, w_in, ssd_conv_w, ssd_conv_b, ssd_dt_bias, ssd_a_log, ssd_d, ssd_norm_w, lru_conv_w, lru_conv_b, lru_wa, lru_ba, lru_wx, lru_bx, lru_lambda, lru_norm_w, w_out, norm2_w, w_gate, w_up, w_down, final_norm_w, loss_target, m_meta_tokens, m_norm1_w, m_w_in, m_ssd_conv_w, m_ssd_conv_b, m_ssd_dt_bias, m_ssd_a_log, m_ssd_d, m_ssd_norm_w, m_lru_conv_w, m_lru_conv_b, m_lru_wa, m_lru_ba, m_lru_wx, m_lru_bx, m_lru_lambda, m_lru_norm_w, m_w_out, m_norm2_w, m_w_gate, m_w_up, m_w_down, m_final_norm_w, v_meta_tokens, v_norm1_w, v_w_in, v_ssd_conv_w, v_ssd_conv_b, v_ssd_dt_bias, v_ssd_a_log, v_ssd_d, v_ssd_norm_w, v_lru_conv_w, v_lru_conv_b, v_lru_wa, v_lru_ba, v_lru_wx, v_lru_bx, v_lru_lambda, v_lru_norm_w, v_w_out, v_norm2_w, v_w_gate, v_w_up, v_w_down, v_final_norm_w):
    w = dict(zip(WEIGHT_NAMES, (meta_tokens, norm1_w, w_in, ssd_conv_w, ssd_conv_b, ssd_dt_bias, ssd_a_log, ssd_d, ssd_norm_w, lru_conv_w, lru_conv_b, lru_wa, lru_ba, lru_wx, lru_bx, lru_lambda, lru_norm_w, w_out, norm2_w, w_gate, w_up, w_down, final_norm_w)))
    m = dict(zip(WEIGHT_NAMES, (m_meta_tokens, m_norm1_w, m_w_in, m_ssd_conv_w, m_ssd_conv_b, m_ssd_dt_bias, m_ssd_a_log, m_ssd_d, m_ssd_norm_w, m_lru_conv_w, m_lru_conv_b, m_lru_wa, m_lru_ba, m_lru_wx, m_lru_bx, m_lru_lambda, m_lru_norm_w, m_w_out, m_norm2_w, m_w_gate, m_w_up, m_w_down, m_final_norm_w)))
    v = dict(zip(WEIGHT_NAMES, (v_meta_tokens, v_norm1_w, v_w_in, v_ssd_conv_w, v_ssd_conv_b, v_ssd_dt_bias, v_ssd_a_log, v_ssd_d, v_ssd_norm_w, v_lru_conv_w, v_lru_conv_b, v_lru_wa, v_lru_ba, v_lru_wx, v_lru_bx, v_lru_lambda, v_lru_norm_w, v_w_out, v_norm2_w, v_w_gate, v_w_up, v_w_down, v_final_norm_w)))
    me = 2 * lax.axis_index("x") + lax.axis_index("y")

    big2d = {n: _rows_view(n, w[n]) for n in BIG}
    small_local = jnp.concatenate([w["meta_tokens"].reshape(-1), w["ssd_conv_w"].reshape(-1),
                                   w["lru_conv_w"].reshape(-1)])[None]
    me_arr = me.astype(jnp.int32).reshape(1)
    dev_arr = (2 * me + lax.axis_index("c")).astype(jnp.int32).reshape(1)
    small4 = _gather_small(small_local)
    (w_in_slot,) = _fill_own_slots([big2d["w_in"]], me_arr, name="own_slot_w_in")
    in_send, in_recv, in_bufs, in_tok = _split_start([w_in_slot], _halves_plan, 3, small4, name="gather_w_in_start")
    late_slots = _fill_own_slots([big2d[n] for n in LATE], me_arr, name="own_slots_late", behind=(in_tok,))
    sm = small4[:, 0]
    meta_full = _unshard_cols(sm[:, :4096].reshape(N_SHARDS, N_META, 256))
    ssd_conv_w_full = _unshard_cols(sm[:, 4096:5632].reshape(N_SHARDS, CONV_K, 384))
    lru_conv_w_full = _unshard_cols(sm[:, 5632:].reshape(N_SHARDS, CONV_K, 256))

    p = {"ssd_conv_w": ssd_conv_w_full, "lru_conv_w": lru_conv_w_full,
         "lru_wa": w["lru_wa"][0], "lru_wx": w["lru_wx"][0], "final_norm_w": w["final_norm_w"][None]}
    for n in ("norm1_w", "ssd_conv_b", "ssd_dt_bias", "ssd_a_log", "ssd_d", "ssd_norm_w", "lru_conv_b", "lru_ba",
              "lru_bx", "lru_lambda", "lru_norm_w", "norm2_w"):
        p[n] = w[n]

    class Late:
        def __init__(self):
            self.pending = []
            self.before_embed = (late_slots[0],)

        def w_in(self, after):
            (buf,) = _split_wait(in_bufs, in_send, in_recv, _halves_plan, after, name="gather_w_in_wait")
            send, recv, bufs, tok = _split_start([buf], _forward_plan, 3, None, name="forward_w_in_start")
            self.late_gather = _split_start(late_slots, _halves_plan, 3 * len(LATE), tok, name="gather_late_start")
            (w_in4,) = _split_wait(bufs, send, recv, _forward_plan, self.late_gather[2][0], name="forward_w_in_wait")
            sections = {s: _rows_of_shards(w_in4, lo, hi) for s, (lo, hi) in IN_ROWS.items()}
            sections["dt"] = jnp.pad(sections["dt"], ((0, SEC_WIDTH["dt"] - SSD_HEADS), (0, 0)))
            return sections

        def mid_forward(self, after):
            send, recv, bufs, _ = self.late_gather
            bufs = _split_wait(bufs, send, recv, _halves_plan, after, name="gather_late_wait")
            self.forward = _split_start(bufs, _forward_plan, 3 * len(LATE), None, name="forward_late_start")
            return self.forward[3][:1, :1]

        def w_out(self, after):
            send, recv, bufs, _ = self.forward
            bufs = _split_wait(bufs, send, recv, _forward_plan, after, name="forward_late_wait")
            self.late = dict(zip(LATE, (b.reshape(-1, D_MODEL) for b in bufs)))
            return self.late["w_out"]

        def ffn(self, after):
            return tuple(self.late[n] for n in FFN)

        def grads_ready(self, names, g, g_mxu):
            if names == ("w_in",):
                g_mxu["w_in"] = jnp.stack([_w_in_shard_rows(k, [g_mxu["w_in_" + s] for s in SEC_NAMES])
                                           for k in range(N_SHARDS)])
            srcs = [g_mxu[n].reshape(N_SHARDS, -1, D_MODEL) for n in names]
            lands = [lax.empty((3,) + s.shape[1:], _MXU) for s in srcs]
            tag = "_".join(names)
            send, recv, bufs, tok = _split_start(srcs + lands, _scatter_plan, 3 * len(names), None,
                                                 name="scatter_" + tag + "_start")
            self.pending.append((names, send, recv, bufs, tag))
            self.in_flight = bufs[0]
            return tok[:1, :1]

        def landed(self, after, which):
            land = {}
            for names, send, recv, bufs, tag in self.pending:
                if names[0] in which:
                    bufs = _split_wait(bufs, send, recv, _scatter_plan, after, name="scatter_" + tag + "_wait")
                    land.update(zip(names, bufs[len(names):]))
            return land

        def small_ready(self, g, loss):
            pack = _pack([g[n] for n in SMALL] + [loss[0, :1]], 8 * N_DEV)
            pack = pack.reshape(N_DEV, -1, PACK_COLS)
            self.small = _split_start([pack, lax.empty(pack.shape, F32)], _pieces_plan, N_DEV - 1, loss,
                                      name="small_pieces_start")
            return self.small[3]

        def small_middle(self, after):
            send, recv, bufs, _ = self.small
            pack, land = _split_wait(bufs, send, recv, _pieces_plan, after, name="small_pieces_wait")
            piece = _sum_pieces(pack, land, dev_arr, name="small_pieces_sum")
            self.small = _split_start([piece, lax.empty(pack.shape, F32)], _spread_plan, N_DEV - 1, None,
                                      name="small_spread_start")
            return self.small[3]

        def small_sum(self, after):
            send, recv, bufs, _ = self.small
            piece, land = _split_wait(bufs, send, recv, _spread_plan, after, name="small_spread_wait")
            return _join_pieces(piece, land, dev_arr, name="small_join")

    late = Late()

    loss, grad_x, g, g_mxu = _local_step(x[0], loss_target[0], meta_full, p, late)

    g4 = {n: g[n].reshape(N_SHARDS, -1, D_MODEL) for n in LATE}
    g4["w_in"] = lax.switch(me, [functools.partial(_w_in_shard_rows, k) for k in range(N_SHARDS)],
                            [g["w_in_" + s] for s in SEC_NAMES])
    land = late.landed(late.in_flight, LATE)
    part = {n: _partial_sum(g4[n], land[n], me_arr, name="partial_" + n) for n in LATE}
    sib = dict(zip(LATE, _swap_with_sibling([part[n] for n in LATE], name="swap_late")))

    small_full_shape = {n: (SMALL_SHARDED[n] if n in SMALL_SHARDED else w[n].shape) for n in SMALL}
    red_list = _unpack(late.small_sum(sib["w_out"]), [small_full_shape[n] for n in SMALL] + [(1,)])
    loss_total = red_list[-1][0]
    g_small = {}
    for n, arr in zip(SMALL, red_list[:-1]):
        if n in SMALL_SHARDED:
            cols = SMALL_SHARDED[n][1] // N_SHARDS
            arr = lax.dynamic_slice_in_dim(arr, me * cols, cols, axis=1)
        g_small[n] = arr.reshape(w[n].shape)

    grad, delta, new_m, new_v = {}, {}, {}, {}

    def update_big(n):
        outs = _adamw(big2d[n], [part[n], sib[n]], _rows_view(n, m[n]), _rows_view(n, v[n]), name="adamw_" + n)
        grad[n], delta[n], new_m[n], new_v[n] = (_param_view(n, o) for o in outs)
        return outs[0]

    two_d = lambda a: a.reshape(1, -1) if a.ndim == 1 else a
    deltas, new_ms, new_vs = _adamw_native(*[[two_d(d[n]) for n in SMALL] for d in (w, g_small, m, v)])
    for n, dn, mn, vn in zip(SMALL, deltas, new_ms, new_vs):
        grad[n], delta[n], new_m[n], new_v[n] = (g_small[n], dn.reshape(w[n].shape), mn.reshape(w[n].shape),
                                                 vn.reshape(w[n].shape))
    land.update(late.landed([update_big(n) for n in LATE] + [deltas[0]], ("w_in",)))
    part["w_in"] = _partial_sum(g4["w_in"], land["w_in"], me_arr, name="partial_w_in")
    (sib["w_in"],) = _swap_with_sibling([part["w_in"]], name="swap_w_in")
    update_big("w_in")

    return (loss_total, grad_x[None], *[grad[n] for n in WEIGHT_NAMES], *[delta[n] for n in WEIGHT_NAMES],
            *[new_m[n] for n in WEIGHT_NAMES], *[new_v[n] for n in WEIGHT_NAMES])
```

```python
import functools
import math

import jax
import jax.numpy as jnp
from jax import lax
from jax.experimental import pallas as pl
from jax.experimental.pallas import tpu as pltpu

F32 = jnp.float32
_MXU = jnp.bfloat16

D_MODEL = 1024
SEQ = 2048
N_META = 16
CHUNK = 128
T_ROWS = 2176
N_CHUNKS = T_ROWS // CHUNK
PAD_ROWS = T_ROWS - SEQ - N_META
X_ROW0 = PAD_ROWS + N_META
SSD_HEADS = 16
SSD_HEAD_DIM = 64
SSD_STATE = 128
SSD_GROUPS = 2
SSD_HPG = SSD_HEADS // SSD_GROUPS
SSD_WIDTH = 1024
LRU_WIDTH = 1024
LRU_C = 8.0
D_FF = 2816
EPS = 1e-6
IN_COLS = 4624
N_SHARDS = 4
N_DEV = 8

ADAM_LR = 0.001
ADAM_B1 = 0.9
ADAM_B2 = 0.999
ADAM_EPS = 1e-08
ADAM_WD = 0.01
ADAM_STEP = 10

VMEM_LIMIT_BYTES = 56 * 1024 * 1024

NN = (((1,), (0,)), ((), ()))
NT = (((1,), (1,)), ((), ()))
TN = (((0,), (0,)), ((), ()))


def _cparams(*sem):
    return pltpu.CompilerParams(dimension_semantics=sem, vmem_limit_bytes=VMEM_LIMIT_BYTES)


def _dot(a, b, dims=NN):
    return lax.dot_general(a.astype(_MXU), b.astype(_MXU), dims, preferred_element_type=F32)


def _dot_onehot(a, b, dims=NN, *, data=0, pieces=3):
    ops = [a, b]
    mask = ops[1 - data].astype(jnp.bfloat16)
    rest = ops[data]
    acc = None
    for _ in range(pieces):
        piece = rest.astype(jnp.bfloat16)
        ops[data], ops[1 - data] = piece, mask
        d = lax.dot_general(ops[0], ops[1], dims, preferred_element_type=F32)
        acc = d if acc is None else acc + d
        rest = rest - piece.astype(F32)
    return acc


def _sigmoid(x):
    return 0.5 * (1.0 + jnp.tanh(0.5 * x))


def _softplus(x):
    return jnp.maximum(x, 0.0) + jnp.log(1.0 + jnp.exp(-jnp.abs(x)))


def _silu(x):
    return x * _sigmoid(x)


def _silu_grad(x):
    s = _sigmoid(x)
    return s * (1.0 + x * (1.0 - s))


_GELU_C = math.sqrt(2.0 / math.pi)


def _gelu_and_grad(x):
    inner = _GELU_C * (x + 0.044715 * x * x * x)
    t = jnp.tanh(inner)
    g = 0.5 * x * (1.0 + t)
    dg = 0.5 * (1.0 + t) + 0.5 * x * (1.0 - t * t) * _GELU_C * (1.0 + 3.0 * 0.044715 * x * x)
    return g, dg


def _rms_fwd(x, w):
    rstd = lax.rsqrt(jnp.mean(x * x, axis=-1, keepdims=True) + EPS)
    return x * rstd * w


def _rms_bwd(x, w, dy):
    rstd = lax.rsqrt(jnp.mean(x * x, axis=-1, keepdims=True) + EPS)
    xhat = x * rstd
    dxhat = dy * w
    dx = rstd * (dxhat - xhat * jnp.mean(dxhat * xhat, axis=-1, keepdims=True))
    return dx, dy * xhat


def _mm(terms, m, n, *, tm, tn, mode, out_dtype, name, residual=None, n_outer=False, also_mxu=False, behind=()):
    gm, gn = m // tm, n // tn
    assert gm * tm == m and gn * tn == n
    if n_outer:
        grid = (gn, gm)
        mi = lambda g0, g1: g1
        ni = lambda g0, g1: g0
    else:
        grid = (gm, gn)
        mi = lambda g0, g1: g0
        ni = lambda g0, g1: g1
    in_specs, args = [], []
    for (a, ka, b, kb, k) in terms:
        if mode == "tn":
            in_specs.append(pl.BlockSpec((k, tm), lambda g0, g1, ka=ka: (ka, mi(g0, g1))))
        else:
            in_specs.append(pl.BlockSpec((tm, k), lambda g0, g1, ka=ka: (mi(g0, g1), ka)))
        if mode == "nt":
            in_specs.append(pl.BlockSpec((tn, k), lambda g0, g1, kb=kb: (ni(g0, g1), kb)))
        else:
            in_specs.append(pl.BlockSpec((k, tn), lambda g0, g1, kb=kb: (kb, ni(g0, g1))))
        args += [a, b]
    if residual is not None:
        in_specs.append(pl.BlockSpec((tm, tn), lambda g0, g1: (mi(g0, g1), ni(g0, g1))))
        args.append(residual)
    dims = {"nn": NN, "nt": NT, "tn": TN}[mode]
    n_terms = len(terms)
    has_res = residual is not None
    in_specs += [pl.BlockSpec(memory_space=pl.ANY)] * len(behind)
    args += list(behind)
    n_in = len(args)

    def body(*refs):
        acc = None
        for t in range(n_terms):
            d = lax.dot_general(refs[2 * t][...], refs[2 * t + 1][...], dims, preferred_element_type=F32)
            acc = d if acc is None else acc + d
        if has_res:
            acc = acc + refs[2 * n_terms][...]
        refs[n_in][...] = acc.astype(out_dtype)
        if also_mxu:
            refs[n_in + 1][...] = acc.astype(_MXU)

    tile = pl.BlockSpec((tm, tn), lambda g0, g1: (mi(g0, g1), ni(g0, g1)))
    shape = jax.ShapeDtypeStruct((m, n), out_dtype)
    return pl.pallas_call(
        body, name=name, grid=grid, in_specs=in_specs,
        out_specs=[tile, tile] if also_mxu else tile,
        out_shape=[shape, jax.ShapeDtypeStruct((m, n), _MXU)] if also_mxu else shape,
        compiler_params=_cparams("parallel", "parallel"),
    )(*args)


def _embed(x, meta, behind=()):
    def body(x_ref, meta_ref, *rest):
        o_ref = rest[-1]
        i = pl.program_id(0)

        @pl.when(i == 0)
        def _():
            o_ref[0:PAD_ROWS, :] = jnp.zeros((PAD_ROWS, D_MODEL), F32)
            o_ref[PAD_ROWS:CHUNK, :] = meta_ref[...]

        @pl.when(i > 0)
        def _():
            o_ref[...] = x_ref[...]

    return pl.pallas_call(
        body, name="embed", grid=(N_CHUNKS,),
        in_specs=[pl.BlockSpec((CHUNK, D_MODEL), lambda i: (jnp.maximum(i - 1, 0), 0)),
                  pl.BlockSpec((N_META, D_MODEL), lambda i: (0, 0))] + [pl.BlockSpec(memory_space=pl.ANY)] * len(behind),
        out_specs=pl.BlockSpec((CHUNK, D_MODEL), lambda i: (i, 0)),
        out_shape=jax.ShapeDtypeStruct((T_ROWS, D_MODEL), F32),
        compiler_params=_cparams("parallel"),
    )(x, meta, *behind)


def _rmsnorm(h, w, *, name, tm=544):
    def body(h_ref, w_ref, o_ref):
        o_ref[...] = _rms_fwd(h_ref[...], w_ref[...]).astype(_MXU)

    return pl.pallas_call(
        body, name=name, grid=(T_ROWS // tm,),
        in_specs=[pl.BlockSpec((tm, D_MODEL), lambda i: (i, 0)), pl.BlockSpec((1, D_MODEL), lambda i: (0, 0))],
        out_specs=pl.BlockSpec((tm, D_MODEL), lambda i: (i, 0)),
        out_shape=jax.ShapeDtypeStruct((T_ROWS, D_MODEL), _MXU),
        compiler_params=_cparams("parallel"),
    )(h, w)


def _norm_proj(h, w, sections, *, name, tm=544):
    widths = [s.shape[0] for s in sections]
    n = len(sections)

    def body(*refs):
        h_ref, w_ref = refs[:2]
        u_ref = refs[2 + n]
        u = _rms_fwd(h_ref[...], w_ref[...]).astype(_MXU)
        u_ref[...] = u
        for k in range(n):
            refs[3 + n + k][...] = lax.dot_general(u, refs[2 + k][...], NT, preferred_element_type=F32)

    row = lambda width: pl.BlockSpec((tm, width), lambda i: (i, 0))
    outs = pl.pallas_call(
        body, name=name, grid=(T_ROWS // tm,),
        in_specs=[row(D_MODEL), pl.BlockSpec((1, D_MODEL), lambda i: (0, 0))]
        + [pl.BlockSpec((wd, D_MODEL), lambda i: (0, 0)) for wd in widths],
        out_specs=[row(D_MODEL)] + [row(wd) for wd in widths],
        out_shape=[jax.ShapeDtypeStruct((T_ROWS, D_MODEL), _MXU)]
        + [jax.ShapeDtypeStruct((T_ROWS, wd), F32) for wd in widths],
        compiler_params=_cparams("parallel"),
    )(h, w, *sections)
    return outs[0], list(outs[1:])


def _loss_head(h2, target, fw):
    def body(h_ref, t_ref, w_ref, loss_ref, dh_ref, dhb_ref, dw_ref, acc_ref):
        i = pl.program_id(0)

        @pl.when(i == 0)
        def _():
            acc_ref[...] = jnp.zeros_like(acc_ref)
            dw_ref[...] = jnp.zeros_like(dw_ref)

        h = h_ref[...]
        w = w_ref[...]
        y = _rms_fwd(h, w)
        live = (i > 0).astype(F32)
        err = (y - t_ref[...]) * live
        acc_ref[...] += jnp.sum(err * err, axis=0, keepdims=True)
        dy = err * (1.0 / D_MODEL)
        dx, dwr = _rms_bwd(h, w, dy)
        dh_ref[...] = dx
        dhb_ref[...] = dx.astype(_MXU)
        dw_ref[...] += jnp.sum(dwr, axis=0, keepdims=True)

        @pl.when(i == N_CHUNKS - 1)
        def _():
            tot = jnp.sum(acc_ref[...], axis=1, keepdims=True) * (0.5 / D_MODEL)
            loss_ref[...] = jnp.broadcast_to(tot, (1, 128))

    return pl.pallas_call(
        body, name="loss_head", grid=(N_CHUNKS,),
        in_specs=[pl.BlockSpec((CHUNK, D_MODEL), lambda i: (i, 0)),
                  pl.BlockSpec((CHUNK, D_MODEL), lambda i: (jnp.maximum(i - 1, 0), 0)),
                  pl.BlockSpec((1, D_MODEL), lambda i: (0, 0))],
        out_specs=[pl.BlockSpec((1, 128), lambda i: (0, 0)),
                   pl.BlockSpec((CHUNK, D_MODEL), lambda i: (i, 0)),
                   pl.BlockSpec((CHUNK, D_MODEL), lambda i: (i, 0)),
                   pl.BlockSpec((1, D_MODEL), lambda i: (0, 0))],
        out_shape=[jax.ShapeDtypeStruct((1, 128), F32),
                   jax.ShapeDtypeStruct((T_ROWS, D_MODEL), F32),
                   jax.ShapeDtypeStruct((T_ROWS, D_MODEL), _MXU),
                   jax.ShapeDtypeStruct((1, D_MODEL), F32)],
        scratch_shapes=[pltpu.VMEM((1, D_MODEL), F32)],
        compiler_params=_cparams("arbitrary"),
    )(h2, target, fw)


def _mm_norm_bwd(terms, h, w, dres, *, name, tm=272, behind=()):
    n_terms = len(terms)
    in_specs, args = [], []
    for (a, b, k) in terms:
        in_specs += [pl.BlockSpec((tm, k), lambda i: (i, 0)), pl.BlockSpec((k, D_MODEL), lambda i: (0, 0))]
        args += [a, b]
    in_specs += [pl.BlockSpec((tm, D_MODEL), lambda i: (i, 0)), pl.BlockSpec((1, D_MODEL), lambda i: (0, 0)),
                 pl.BlockSpec((tm, D_MODEL), lambda i: (i, 0))] + [pl.BlockSpec(memory_space=pl.ANY)] * len(behind)
    args += [h, w, dres, *behind]

    def body(*refs):
        h_ref, w_ref, dres_ref = refs[2 * n_terms:2 * n_terms + 3]
        dh_ref, dhb_ref, dw_ref = refs[2 * n_terms + 3 + len(behind):]

        @pl.when(pl.program_id(0) == 0)
        def _():
            dw_ref[...] = jnp.zeros_like(dw_ref)

        du = None
        for t in range(n_terms):
            d = lax.dot_general(refs[2 * t][...], refs[2 * t + 1][...], NN, preferred_element_type=F32)
            du = d if du is None else du + d
        dx, dwr = _rms_bwd(h_ref[...], w_ref[...], du)
        dh = dres_ref[...] + dx
        dh_ref[...] = dh
        dhb_ref[...] = dh.astype(_MXU)
        dw_ref[...] += jnp.sum(dwr, axis=0, keepdims=True)

    return pl.pallas_call(
        body, name=name, grid=(T_ROWS // tm,), in_specs=in_specs,
        out_specs=[pl.BlockSpec((tm, D_MODEL), lambda i: (i, 0)), pl.BlockSpec((tm, D_MODEL), lambda i: (i, 0)),
                   pl.BlockSpec((1, D_MODEL), lambda i: (0, 0))],
        out_shape=[jax.ShapeDtypeStruct((T_ROWS, D_MODEL), F32), jax.ShapeDtypeStruct((T_ROWS, D_MODEL), _MXU),
                   jax.ShapeDtypeStruct((1, D_MODEL), F32)],
        compiler_params=_cparams("arbitrary"),
    )(*args)


FFN_TM = T_ROWS
FFN_TN = 256


def _ffn_up(u2, wg_t, wu_t):
    def body(u_ref, wg_ref, wu_ref, gp_ref, up_ref, act_ref):
        u = u_ref[...]
        gp = lax.dot_general(u, wg_ref[...], NT, preferred_element_type=F32)
        up = lax.dot_general(u, wu_ref[...], NT, preferred_element_type=F32)
        gp_ref[...] = gp.astype(_MXU)
        up_ref[...] = up.astype(_MXU)
        act_ref[...] = (_silu(gp) * up).astype(_MXU)

    tile = pl.BlockSpec((FFN_TM, FFN_TN), lambda j, i: (i, j))
    return pl.pallas_call(
        body, name="ffn_up", grid=(D_FF // FFN_TN, T_ROWS // FFN_TM),
        in_specs=[pl.BlockSpec((FFN_TM, D_MODEL), lambda j, i: (i, 0)),
                  pl.BlockSpec((FFN_TN, D_MODEL), lambda j, i: (j, 0)),
                  pl.BlockSpec((FFN_TN, D_MODEL), lambda j, i: (j, 0))],
        out_specs=[tile, tile, tile],
        out_shape=[jax.ShapeDtypeStruct((T_ROWS, D_FF), _MXU)] * 3,
        compiler_params=_cparams("parallel", "parallel"),
    )(u2, wg_t, wu_t)


def _ffn_bwd_act(dh2b, wd, gp, up):
    def body(dh_ref, wd_ref, gp_ref, up_ref, dgp_ref, dup_ref):
        dact = lax.dot_general(dh_ref[...], wd_ref[...], NT, preferred_element_type=F32)
        gp = gp_ref[...].astype(F32)
        dgp_ref[...] = (dact * up_ref[...].astype(F32) * _silu_grad(gp)).astype(_MXU)
        dup_ref[...] = (dact * _silu(gp)).astype(_MXU)

    tile = pl.BlockSpec((FFN_TM, FFN_TN), lambda j, i: (i, j))
    return pl.pallas_call(
        body, name="ffn_bwd_act", grid=(D_FF // FFN_TN, T_ROWS // FFN_TM),
        in_specs=[pl.BlockSpec((FFN_TM, D_MODEL), lambda j, i: (i, 0)),
                  pl.BlockSpec((FFN_TN, D_MODEL), lambda j, i: (j, 0)), tile, tile],
        out_specs=[tile, tile],
        out_shape=[jax.ShapeDtypeStruct((T_ROWS, D_FF), _MXU), jax.ShapeDtypeStruct((T_ROWS, D_FF), _MXU)],
        compiler_params=_cparams("parallel", "parallel"),
    )(dh2b, wd, gp, up)


CONV_TC = 512
CONV_K = 4


def _conv_pre(x_ref, wv, bv, c):
    tc = wv.shape[1]
    r0 = c * CHUNK
    cur = x_ref[r0:r0 + CHUNK, :]
    if c == 0:
        cat = jnp.concatenate([jnp.zeros((8, tc), F32), cur], axis=0)
        shifted = [cur] + [pltpu.roll(cat, s, 0)[8:8 + CHUNK] for s in range(1, CONV_K)]
    else:
        shifted = [cur] + [x_ref[r0 - s:r0 - s + CHUNK, :] for s in range(1, CONV_K)]
    pre = bv
    for s in range(CONV_K):
        pre = pre + shifted[s] * wv[CONV_K - 1 - s:CONV_K - s]
    return pre, shifted


def _row_mask(c):
    if c > 0:
        return None
    return (lax.broadcasted_iota(jnp.int32, (CHUNK, 1), 0) >= PAD_ROWS).astype(F32)


def _conv_fwd(x, w, b, *, silu, name):
    cols = x.shape[1]
    tc = min(CONV_TC, cols)

    def body(x_ref, w_ref, b_ref, o_ref):
        wv, bv = w_ref[...], b_ref[...]
        for c in range(N_CHUNKS):
            pre, _ = _conv_pre(x_ref, wv, bv, c)
            y = _silu(pre) if silu else pre
            mask = _row_mask(c)
            if mask is not None:
                y = y * mask
            o_ref[c * CHUNK:(c + 1) * CHUNK, :] = y

    return pl.pallas_call(
        body, name=name, grid=(cols // tc,),
        in_specs=[pl.BlockSpec((T_ROWS, tc), lambda j: (0, j)), pl.BlockSpec((CONV_K, tc), lambda j: (0, j)),
                  pl.BlockSpec((1, tc), lambda j: (0, j))],
        out_specs=pl.BlockSpec((T_ROWS, tc), lambda j: (0, j)),
        out_shape=jax.ShapeDtypeStruct((T_ROWS, cols), F32),
        compiler_params=_cparams("parallel"),
    )(x, w, b)


def _conv_bwd(dy, x, w, b, *, silu, name):
    cols = x.shape[1]
    tc = min(CONV_TC, cols)

    def body(dy_ref, x_ref, w_ref, b_ref, dx_ref, dw_ref, db_ref):
        wv, bv = w_ref[...], b_ref[...]
        next8 = jnp.zeros((8, tc), F32)
        dws = [jnp.zeros((1, tc), F32) for _ in range(CONV_K)]
        db = jnp.zeros((1, tc), F32)
        for c in reversed(range(N_CHUNKS)):
            r0 = c * CHUNK
            pre, shifted = _conv_pre(x_ref, wv, bv, c)
            dpre = dy_ref[r0:r0 + CHUNK, :]
            if silu:
                dpre = dpre * _silu_grad(pre)
            mask = _row_mask(c)
            if mask is not None:
                dpre = dpre * mask
            cat = jnp.concatenate([dpre, next8], axis=0)
            dx = dpre * wv[CONV_K - 1:CONV_K]
            for s in range(1, CONV_K):
                dx = dx + pltpu.roll(cat, CHUNK + 8 - s, 0)[0:CHUNK] * wv[CONV_K - 1 - s:CONV_K - s]
            dx_ref[r0:r0 + CHUNK, :] = dx.astype(_MXU)
            for s in range(CONV_K):
                k = CONV_K - 1 - s
                dws[k] = dws[k] + jnp.sum(dpre * shifted[s], axis=0, keepdims=True)
            db = db + jnp.sum(dpre, axis=0, keepdims=True)
            next8 = dpre[0:8]
        dw_ref[...] = jnp.concatenate(dws, axis=0)
        db_ref[...] = db

    return pl.pallas_call(
        body, name=name, grid=(cols // tc,),
        in_specs=[pl.BlockSpec((T_ROWS, tc), lambda j: (0, j)), pl.BlockSpec((T_ROWS, tc), lambda j: (0, j)),
                  pl.BlockSpec((CONV_K, tc), lambda j: (0, j)), pl.BlockSpec((1, tc), lambda j: (0, j))],
        out_specs=[pl.BlockSpec((T_ROWS, tc), lambda j: (0, j)), pl.BlockSpec((CONV_K, tc), lambda j: (0, j)),
                   pl.BlockSpec((1, tc), lambda j: (0, j))],
        out_shape=[jax.ShapeDtypeStruct((T_ROWS, cols), _MXU), jax.ShapeDtypeStruct((CONV_K, cols), F32),
                   jax.ShapeDtypeStruct((1, cols), F32)],
        compiler_params=_cparams("parallel"),
    )(dy, x, w, b)


def _ssd_chunk_common(dt_raw, prm, c):
    a_row = -jnp.exp(prm[1:2])
    dt = _softplus(dt_raw + prm[0:1])
    rows = lax.broadcasted_iota(jnp.int32, (CHUNK, 1), 0)
    real = jnp.logical_or(c > 0, rows >= PAD_ROWS)
    dt = jnp.where(real, dt, 0.0)
    li = lax.broadcasted_iota(jnp.int32, (CHUNK, CHUNK), 0)
    si = lax.broadcasted_iota(jnp.int32, (CHUNK, CHUNK), 1)
    causal = li >= si
    tri = causal.astype(F32)
    cs = _dot_onehot(tri, dt * a_row, data=1)
    return dt, a_row, cs, cs.T, causal, tri, real


def _gated_norm_fwd(y, z, w):
    g = y * _silu(z)
    half = SSD_WIDTH // SSD_GROUPS
    outs = [_rms_fwd(g[:, k * half:(k + 1) * half], w[:, k * half:(k + 1) * half]) for k in range(SSD_GROUPS)]
    return jnp.concatenate(outs, axis=1)


GROUP_W = SSD_WIDTH // SSD_GROUPS
PAIR_W = 2 * SSD_HEAD_DIM
STATE_SHAPE = (SSD_GROUPS, SSD_STATE, GROUP_W)


def _head_expander():
    r = lax.broadcasted_iota(jnp.int32, (128, SSD_WIDTH), 0)
    c = lax.broadcasted_iota(jnp.int32, (128, SSD_WIDTH), 1)
    return (c // SSD_HEAD_DIM == r).astype(F32)


def _ssd_expand(dt, cs, prm, ex):
    cs_x = _dot_onehot(cs, ex)
    cs_last_x = cs_x[CHUNK - 1:CHUNK, :]
    return (_dot_onehot(dt, ex, pieces=2), _dot_onehot(prm, ex)[2:3], jnp.exp(cs_x), jnp.exp(cs_last_x),
            jnp.exp(cs_last_x - cs_x))


def _ssd_fwd(xs, bc, dt_raw, z, prm, norm_w, ex):
    def body(xs_ref, bc_ref, dt_ref, z_ref, prm_ref, nw_ref, ex_ref, y_ref, yn_ref, prev_ref, state):
        c = pl.program_id(0)

        @pl.when(c == 0)
        def _():
            state[...] = jnp.zeros_like(state)

        prm = prm_ref[...]
        dt, a_row, cs, cs_t, causal, _, _ = _ssd_chunk_common(dt_ref[...], prm, c)
        dt_x, d_x, e_cs_x, e_last_x, dec_x = _ssd_expand(dt, cs, prm, ex_ref[...])
        xs_all = xs_ref[...]
        bc_all = bc_ref[...]
        xdt = xs_all * dt_x
        xdec = xdt * dec_x
        lane_lo = lax.broadcasted_iota(jnp.int32, (1, PAIR_W), 1) < SSD_HEAD_DIM
        for g in range(SSD_GROUPS):
            gs = slice(g * GROUP_W, (g + 1) * GROUP_W)
            b_g = bc_all[:, g * SSD_STATE:(g + 1) * SSD_STATE]
            c_g = bc_all[:, (SSD_GROUPS + g) * SSD_STATE:(SSD_GROUPS + g + 1) * SSD_STATE]
            st = state[g]
            prev_ref[0, g] = st
            y_off = _dot(c_g, st) * e_cs_x[:, gs]
            state[g] = st * e_last_x[:, gs] + _dot(b_g.T, xdec[:, gs])
            cb = _dot(c_g, b_g, NT)
            for k in range(SSD_HPG // 2):
                h0 = g * SSD_HPG + 2 * k
                ps = slice(h0 * SSD_HEAD_DIM, h0 * SSD_HEAD_DIM + PAIR_W)
                xdt_pair = xdt[:, ps]
                yd = []
                for h in (h0, h0 + 1):
                    lmat = jnp.where(causal, jnp.exp(cs[:, h:h + 1] - cs_t[h:h + 1, :]), 0.0)
                    yd.append(_dot(cb * lmat, xdt_pair))
                y_ref[:, ps] = (jnp.where(lane_lo, yd[0], yd[1]) + y_off[:, k * PAIR_W:(k + 1) * PAIR_W]
                                + xs_all[:, ps] * d_x[:, ps])
        yn_ref[...] = _gated_norm_fwd(y_ref[...], z_ref[...], nw_ref[...]).astype(_MXU)

    row = lambda w: pl.BlockSpec((CHUNK, w), lambda c: (c, 0))
    return pl.pallas_call(
        body, name="ssd_fwd", grid=(N_CHUNKS,),
        in_specs=[row(SSD_WIDTH), row(512), row(128), row(SSD_WIDTH),
                  pl.BlockSpec((8, 128), lambda c: (0, 0)), pl.BlockSpec((1, SSD_WIDTH), lambda c: (0, 0)),
                  pl.BlockSpec((128, SSD_WIDTH), lambda c: (0, 0))],
        out_specs=[row(SSD_WIDTH), row(SSD_WIDTH),
                   pl.BlockSpec((1,) + STATE_SHAPE, lambda c: (c, 0, 0, 0))],
        out_shape=[jax.ShapeDtypeStruct((T_ROWS, SSD_WIDTH), F32), jax.ShapeDtypeStruct((T_ROWS, SSD_WIDTH), _MXU),
                   jax.ShapeDtypeStruct((N_CHUNKS,) + STATE_SHAPE, F32)],
        scratch_shapes=[pltpu.VMEM(STATE_SHAPE, F32)],
        compiler_params=_cparams("arbitrary"),
    )(xs, bc, dt_raw, z, prm, norm_w, ex)


def _ssd_bwd(dyn, dyn_block, z, y_pre, xs, bc, dt_raw, prev, prm, norm_w, ex):
    def body(dyn_ref, z_ref, y_ref, xs_ref, bc_ref, dt_ref, prev_ref, prm_ref, nw_ref, ex_ref,
             dz_ref, dxs_ref, dbc_ref, ddt_ref, dprm_ref, dnw_ref, dstate):
        step = pl.program_id(0)
        c = N_CHUNKS - 1 - step

        @pl.when(step == 0)
        def _():
            dstate[...] = jnp.zeros_like(dstate)
            dprm_ref[...] = jnp.zeros_like(dprm_ref)
            dnw_ref[...] = jnp.zeros_like(dnw_ref)

        prm = prm_ref[...]
        dt, a_row, cs, cs_t, causal, tri, real = _ssd_chunk_common(dt_ref[...], prm, c)
        realf = real.astype(F32)
        z = z_ref[...]
        y_all = y_ref[...]
        nw = nw_ref[...]
        dyn_all = dyn_ref[...]
        sz = _silu(z)
        gated = y_all * sz
        half = SSD_WIDTH // SSD_GROUPS
        dgs, dnws = [], []
        for k in range(SSD_GROUPS):
            sl = slice(k * half, (k + 1) * half)
            dgk, dwk = _rms_bwd(gated[:, sl], nw[:, sl], dyn_all[:, sl])
            dgs.append(dgk)
            dnws.append(jnp.sum(dwk, axis=0, keepdims=True))
        dgated = jnp.concatenate(dgs, axis=1)
        dnw_ref[...] += jnp.concatenate(dnws, axis=1)
        dz_ref[...] = (dgated * y_all * _silu_grad(z)).astype(_MXU)
        dy_all = dgated * sz

        ex = ex_ref[...]
        dt_x, d_x, e_cs_x, e_last_x, dec_x = _ssd_expand(dt, cs, prm, ex)
        xs_all = xs_ref[...]
        bc_all = bc_ref[...]
        xdt = xs_all * dt_x
        xdt_mxu = xdt.astype(_MXU).astype(F32)
        xdec = xdt * dec_x
        dcp = dy_all * e_cs_x
        lane_lo = lax.broadcasted_iota(jnp.int32, (1, PAIR_W), 1) < SSD_HEAD_DIM
        upper = (lax.broadcasted_iota(jnp.int32, (CHUNK, CHUNK), 0)
                 <= lax.broadcasted_iota(jnp.int32, (CHUNK, CHUNK), 1))
        last_row = (lax.broadcasted_iota(jnp.int32, (CHUNK, 1), 0) == CHUNK - 1).astype(F32)
        dbs, dcs_, dxdt_parts, last_parts = [], [], [], []
        for g in range(SSD_GROUPS):
            gs = slice(g * GROUP_W, (g + 1) * GROUP_W)
            b_g = bc_all[:, g * SSD_STATE:(g + 1) * SSD_STATE]
            c_g = bc_all[:, (SSD_GROUPS + g) * SSD_STATE:(SSD_GROUPS + g + 1) * SSD_STATE]
            prev_t = prev_ref[0, g]
            dst = dstate[g]
            dc_g = _dot(dcp[:, gs], prev_t, NT)
            db_g = _dot(xdec[:, gs], dst, NT)
            dxdt_state = _dot(b_g, dst) * dec_x[:, gs]
            dstate[g] = dst * e_last_x[:, gs] + _dot(c_g.T, dcp[:, gs])
            last_parts.append(jnp.sum(xdt_mxu[:, gs] * dxdt_state, axis=0, keepdims=True)
                              + jnp.sum(dst * prev_t, axis=0, keepdims=True) * e_last_x[:, gs])
            cb_t = _dot(b_g, c_g, NT)
            dcb_t = jnp.zeros((CHUNK, CHUNK), F32)
            for k in range(SSD_HPG // 2):
                h0 = g * SSD_HPG + 2 * k
                ps = slice(h0 * SSD_HEAD_DIM, h0 * SSD_HEAD_DIM + PAIR_W)
                dy_pair = dy_all[:, ps]
                xdt_pair = xdt[:, ps]
                dd = []
                for h in (h0, h0 + 1):
                    lmat_t = jnp.where(upper, jnp.exp(cs_t[h:h + 1, :] - cs[:, h:h + 1]), 0.0)
                    dd.append(_dot(cb_t * lmat_t, dy_pair))
                    mine = lane_lo if h == h0 else jnp.logical_not(lane_lo)
                    dcb_t = dcb_t + _dot(jnp.where(mine, xdt_pair, 0.0), dy_pair, NT) * lmat_t
                dxdt_parts.append(jnp.where(lane_lo, dd[0], dd[1]) + dxdt_state[:, k * PAIR_W:(k + 1) * PAIR_W])
            dc_g = dc_g + _dot(dcb_t, b_g, TN)
            db_g = db_g + _dot(dcb_t, c_g)
            dbs.append(db_g * realf)
            dcs_.append(dc_g * realf)
        dbc_ref[...] = jnp.concatenate(dbs + dcs_, axis=1)
        dxdt = jnp.concatenate(dxdt_parts, axis=1)
        dxs_ref[...] = (dxdt * dt_x + dy_all * d_x) * realf
        ddt_all = _dot_onehot(dxdt * xs_all, ex, NT, pieces=2)
        rows = jnp.concatenate([jnp.concatenate(last_parts, axis=1), jnp.sum(dy_all * xs_all, axis=0, keepdims=True),
                                jnp.zeros((6, SSD_WIDTH), F32)], axis=0)
        rows = _dot_onehot(rows, ex, NT, pieces=2)
        dd_row = rows[1:2]
        dy_mxu = dy_all.astype(_MXU).astype(F32)
        dcs_all = (_dot_onehot(dy_mxu * (y_all - xs_all * d_x), ex, NT) - _dot_onehot(xdt_mxu * dxdt, ex, NT)
                   + last_row * rows[0:1])
        dda = _dot_onehot(tri, dcs_all, TN, data=1)
        ddt = (ddt_all + dda * a_row) * realf
        ddt_raw = ddt * _sigmoid(dt_ref[...] + prm[0:1])
        ddt_ref[...] = ddt_raw.astype(_MXU)
        da_log = jnp.sum(dda * dt, axis=0, keepdims=True) * a_row
        dprm_ref[0:1, :] += jnp.sum(ddt_raw, axis=0, keepdims=True)
        dprm_ref[1:2, :] += da_log
        dprm_ref[2:3, :] += dd_row

    rev = lambda w, blk=0: pl.BlockSpec((CHUNK, w), lambda s, blk=blk: (N_CHUNKS - 1 - s, blk))
    return pl.pallas_call(
        body, name="ssd_bwd", grid=(N_CHUNKS,),
        in_specs=[rev(SSD_WIDTH, dyn_block), rev(SSD_WIDTH), rev(SSD_WIDTH), rev(SSD_WIDTH), rev(512), rev(128),
                  pl.BlockSpec((1,) + STATE_SHAPE, lambda s: (N_CHUNKS - 1 - s, 0, 0, 0)),
                  pl.BlockSpec((8, 128), lambda s: (0, 0)), pl.BlockSpec((1, SSD_WIDTH), lambda s: (0, 0)),
                  pl.BlockSpec((128, SSD_WIDTH), lambda s: (0, 0))],
        out_specs=[rev(SSD_WIDTH), rev(SSD_WIDTH), rev(512), rev(128),
                   pl.BlockSpec((8, 128), lambda s: (0, 0)), pl.BlockSpec((1, SSD_WIDTH), lambda s: (0, 0))],
        out_shape=[jax.ShapeDtypeStruct((T_ROWS, SSD_WIDTH), _MXU), jax.ShapeDtypeStruct((T_ROWS, SSD_WIDTH), F32),
                   jax.ShapeDtypeStruct((T_ROWS, 512), F32), jax.ShapeDtypeStruct((T_ROWS, 128), _MXU),
                   jax.ShapeDtypeStruct((8, 128), F32), jax.ShapeDtypeStruct((1, SSD_WIDTH), F32)],
        scratch_shapes=[pltpu.VMEM(STATE_SHAPE, F32)],
        compiler_params=_cparams("arbitrary"),
    )(dyn, z, y_pre, xs, bc, dt_raw, prev, prm, norm_w, ex)


LRU_PAIRS = 8


def _lru_gates(xr, wa_ref, wx_ref, prm):
    pre_r, pre_i = [], []
    for k in range(LRU_PAIRS):
        xk = xr[:, k * 128:(k + 1) * 128]
        pre_r.append(_dot(xk, wa_ref[k]))
        pre_i.append(_dot(xk, wx_ref[k]))
    r = _sigmoid(jnp.concatenate(pre_r, axis=1) + prm[0:1])
    i = _sigmoid(jnp.concatenate(pre_i, axis=1) + prm[1:2])
    sp = _softplus(-prm[2:3])
    log_a = (-LRU_C) * r * sp
    a = jnp.exp(log_a)
    s = jnp.sqrt(-jnp.tanh(log_a) * (a * a + 1.0))
    return r, i, a, s, sp


def _lru_fwd(xr, gate, wa, wx, prm):
    def body(xr_ref, g_ref, wa_ref, wx_ref, prm_ref, hs_ref, yn_ref, carry, a_s, u_s):
        @pl.when(pl.program_id(0) == 0)
        def _():
            carry[...] = jnp.zeros_like(carry)

        prm = prm_ref[...]
        xr_t = xr_ref[...]
        _, i, a, s, _ = _lru_gates(xr_t, wa_ref, wx_ref, prm)
        a_s[...] = a
        u_s[...] = s * (i * xr_t)
        rid = lax.broadcasted_iota(jnp.int32, (8, LRU_WIDTH), 0)

        def group(k, before):
            off = pl.multiple_of(k * 8, 8)
            a8 = a_s[pl.ds(off, 8), :]
            u8 = u_s[pl.ds(off, 8), :]
            for d in (1, 2, 4):
                keep = rid >= d
                u8 = u8 + a8 * jnp.where(keep, pltpu.roll(u8, d, 0), 0.0)
                a8 = a8 * jnp.where(keep, pltpu.roll(a8, d, 0), 1.0)
            h8 = u8 + a8 * before
            hs_ref[pl.ds(off, 8), :] = h8
            return jnp.broadcast_to(h8[7:8], (8, LRU_WIDTH))

        carry[...] = lax.fori_loop(0, CHUNK // 8, group, carry[...])
        gel, _ = _gelu_and_grad(g_ref[...])
        yn_ref[...] = _rms_fwd(gel * hs_ref[...], prm[3:4]).astype(_MXU)

    row = pl.BlockSpec((CHUNK, LRU_WIDTH), lambda t: (t, 0))
    wspec = pl.BlockSpec((LRU_PAIRS, 128, 128), lambda t: (0, 0, 0))
    return pl.pallas_call(
        body, name="lru_fwd", grid=(N_CHUNKS,),
        in_specs=[row, row, wspec, wspec, pl.BlockSpec((8, LRU_WIDTH), lambda t: (0, 0))],
        out_specs=[row, row],
        out_shape=[jax.ShapeDtypeStruct((T_ROWS, LRU_WIDTH), F32), jax.ShapeDtypeStruct((T_ROWS, LRU_WIDTH), _MXU)],
        scratch_shapes=[pltpu.VMEM((8, LRU_WIDTH), F32), pltpu.VMEM((CHUNK, LRU_WIDTH), F32),
                        pltpu.VMEM((CHUNK, LRU_WIDTH), F32)],
        compiler_params=_cparams("arbitrary"),
    )(xr, gate, wa, wx, prm)


def _lru_bwd(dyn, dyn_block, gate, xr, hs, wa, wx, wa_t, wx_t, prm):
    def body(dyn_ref, g_ref, xr_ref, hs_ref, hsp_ref, wa_ref, wx_ref, wat_ref, wxt_ref, prm_ref,
             dg_ref, dxr_ref, dwa_ref, dwx_ref, dprm_ref, carry, a_s, d_s):
        step = pl.program_id(0)
        tile = N_CHUNKS - 1 - step

        @pl.when(step == 0)
        def _():
            carry[...] = jnp.zeros_like(carry)
            dwa_ref[...] = jnp.zeros_like(dwa_ref)
            dwx_ref[...] = jnp.zeros_like(dwx_ref)
            dprm_ref[...] = jnp.zeros_like(dprm_ref)

        prm = prm_ref[...]
        xr_t = xr_ref[...]
        r, i, a, s, sp = _lru_gates(xr_t, wa_ref, wx_ref, prm)
        hs_t = hs_ref[...]
        gel, dgel = _gelu_and_grad(g_ref[...])
        dy, dnw = _rms_bwd(gel * hs_t, prm[3:4], dyn_ref[...])
        dg_ref[...] = (dy * hs_t * dgel).astype(_MXU)
        a_s[...] = a
        d_s[...] = dy * gel
        rid = lax.broadcasted_iota(jnp.int32, (8, LRU_WIDTH), 0)

        def group(k, behind):
            off = pl.multiple_of((CHUNK // 8 - 1 - k) * 8, 8)
            a8 = a_s[pl.ds(off, 8), :]
            d8 = d_s[pl.ds(off, 8), :]
            c8 = jnp.where(rid == 7, 1.0, pltpu.roll(a8, 7, 0))
            for d in (1, 2, 4):
                keep = rid < 8 - d
                d8 = d8 + c8 * jnp.where(keep, pltpu.roll(d8, 8 - d, 0), 0.0)
                c8 = c8 * jnp.where(keep, pltpu.roll(c8, 8 - d, 0), 1.0)
            dht8 = d8 + c8 * behind
            d_s[pl.ds(off, 8), :] = dht8
            return jnp.broadcast_to(a8[0:1] * dht8[0:1], (8, LRU_WIDTH))

        carry[...] = lax.fori_loop(0, CHUNK // 8, group, carry[...])
        dht = d_s[...]
        before = hsp_ref[CHUNK - 8:CHUNK, :][7:8] * (tile > 0).astype(F32)
        first = lax.broadcasted_iota(jnp.int32, (CHUNK, 1), 0) == 0
        hprev = jnp.where(first, before, pltpu.roll(hs_t, 1, 0))
        da = dht * hprev
        ixr = i * xr_t
        ds = dht * ixr
        dlog_a = da * a - ds * (a * a) * lax.rsqrt(s * s)
        dr = dlog_a * ((-LRU_C) * sp)
        dsp = jnp.sum(dlog_a * ((-LRU_C) * r), axis=0, keepdims=True)
        dlam = dsp * (-_sigmoid(-prm[2:3]))
        di = dht * s * xr_t
        dpre_r = dr * r * (1.0 - r)
        dpre_i = di * i * (1.0 - i)
        dxr = dht * s * i
        parts = []
        for k in range(LRU_PAIRS):
            sl = slice(k * 128, (k + 1) * 128)
            parts.append(_dot(dpre_r[:, sl], wat_ref[k]) + _dot(dpre_i[:, sl], wxt_ref[k]))
            dwa_ref[k] += _dot(xr_t[:, sl], dpre_r[:, sl], TN)
            dwx_ref[k] += _dot(xr_t[:, sl], dpre_i[:, sl], TN)
        dxr_ref[...] = dxr + jnp.concatenate(parts, axis=1)
        dprm_ref[0:1, :] += jnp.sum(dpre_r, axis=0, keepdims=True)
        dprm_ref[1:2, :] += jnp.sum(dpre_i, axis=0, keepdims=True)
        dprm_ref[2:3, :] += dlam
        dprm_ref[3:4, :] += jnp.sum(dnw, axis=0, keepdims=True)

    rev = lambda blk=0: pl.BlockSpec((CHUNK, LRU_WIDTH), lambda s, blk=blk: (N_CHUNKS - 1 - s, blk))
    wspec = pl.BlockSpec((LRU_PAIRS, 128, 128), lambda s: (0, 0, 0))
    return pl.pallas_call(
        body, name="lru_bwd", grid=(N_CHUNKS,),
        in_specs=[rev(dyn_block), rev(), rev(), rev(),
                  pl.BlockSpec((CHUNK, LRU_WIDTH), lambda s: (jnp.maximum(N_CHUNKS - 2 - s, 0), 0)),
                  wspec, wspec, wspec, wspec, pl.BlockSpec((8, LRU_WIDTH), lambda s: (0, 0))],
        out_specs=[rev(), rev(), wspec, wspec, pl.BlockSpec((8, LRU_WIDTH), lambda s: (0, 0))],
        out_shape=[jax.ShapeDtypeStruct((T_ROWS, LRU_WIDTH), _MXU), jax.ShapeDtypeStruct((T_ROWS, LRU_WIDTH), F32),
                   jax.ShapeDtypeStruct((LRU_PAIRS, 128, 128), F32), jax.ShapeDtypeStruct((LRU_PAIRS, 128, 128), F32),
                   jax.ShapeDtypeStruct((8, LRU_WIDTH), F32)],
        scratch_shapes=[pltpu.VMEM((8, LRU_WIDTH), F32), pltpu.VMEM((CHUNK, LRU_WIDTH), F32),
                        pltpu.VMEM((CHUNK, LRU_WIDTH), F32)],
        compiler_params=_cparams("arbitrary"),
    )(dyn, gate, xr, hs, hs, wa, wx, wa_t, wx_t, prm)


SEC_NAMES = ("z", "xs", "bc", "dt", "g", "x")
SEC_WIDTH = {"z": 1024, "xs": 1024, "bc": 512, "dt": 128, "g": 1024, "x": 1024}


def _pair_blocks(w):
    w = w.reshape(LRU_PAIRS, 2, 64, 64)
    zero = jnp.zeros((LRU_PAIRS, 64, 64), w.dtype)
    top = jnp.concatenate([w[:, 0], zero], axis=2)
    bot = jnp.concatenate([zero, w[:, 1]], axis=2)
    return jnp.concatenate([top, bot], axis=1)


def _unpair_blocks(wp):
    return jnp.stack([wp[:, :64, :64], wp[:, 64:, 64:]], axis=1).reshape(16, 64, 64)


def _pad_lanes(v, width=128):
    return jnp.pad(v, ((0, 0), (0, width - v.shape[1])))


class _Resident:
    before_embed = ()

    def __init__(self, w_in_sections, w_out, w_gate, w_up, w_down):
        self._w_in, self._w_out, self._ffn = w_in_sections, w_out, (w_gate, w_up, w_down)

    def w_in(self, after):
        return self._w_in

    def mid_forward(self, after):
        return jnp.zeros((1, 1), F32)

    def w_out(self, after):
        return self._w_out

    def ffn(self, after):
        return self._ffn

    def grads_ready(self, names, g, g_mxu):
        return jnp.zeros((1, 1), F32)

    def small_ready(self, g, loss):
        return jnp.zeros((1, 1), F32)

    def small_middle(self, after):
        return jnp.zeros((1, 1), F32)


def _local_step(x, target, meta, p, late):
    g, g_mxu = {}, {}
    ex = _head_expander()
    h0 = _embed(x, meta, late.before_embed)
    w_in = late.w_in(h0)
    u1, projs = _norm_proj(h0, p["norm1_w"], [w_in[s] for s in SEC_NAMES], name="norm_in_proj")
    proj = dict(zip(SEC_NAMES, projs))
    ssd_prm = jnp.concatenate([_pad_lanes(p["ssd_dt_bias"]), _pad_lanes(p["ssd_a_log"]), _pad_lanes(p["ssd_d"]),
                               jnp.zeros((5, 128), F32)], axis=0)
    xs_act = _conv_fwd(proj["xs"], p["ssd_conv_w"][:, :SSD_WIDTH], p["ssd_conv_b"][:, :SSD_WIDTH], silu=True,
                       name="ssd_conv_xs")
    bc_act = _conv_fwd(proj["bc"], p["ssd_conv_w"][:, SSD_WIDTH:], p["ssd_conv_b"][:, SSD_WIDTH:], silu=True,
                       name="ssd_conv_bc")
    y_pre, y_ssd, prev = _ssd_fwd(xs_act, bc_act, proj["dt"], proj["z"], ssd_prm, p["ssd_norm_w"], ex)
    xr = _conv_fwd(proj["x"], p["lru_conv_w"], p["lru_conv_b"], silu=False, name="lru_conv")
    wa_p, wx_p = _pair_blocks(p["lru_wa"]), _pair_blocks(p["lru_wx"])
    lru_prm = jnp.concatenate([p["lru_ba"], p["lru_bx"], p["lru_lambda"], p["lru_norm_w"],
                               jnp.zeros((4, LRU_WIDTH), F32)], axis=0)
    hs, y_lru = _lru_fwd(xr, proj["g"], wa_p.astype(_MXU), wx_p.astype(_MXU),
                         lru_prm + late.mid_forward([xr, y_ssd]))
    ycat = jnp.concatenate([y_ssd, y_lru], axis=1)
    w_out = late.w_out(ycat)
    h1 = _mm([(ycat, 0, w_out, 0, 2 * D_MODEL)], T_ROWS, D_MODEL, tm=T_ROWS, tn=256, mode="nn", out_dtype=F32,
             name="out_proj", residual=h0)
    u2 = _rmsnorm(h1, p["norm2_w"], name="norm2")
    w_gate, w_up, w_down = late.ffn(u2)
    gp, up, act = _ffn_up(u2, w_gate, w_up)
    h2 = _mm([(act, 0, w_down, 0, D_FF)], T_ROWS, D_MODEL, tm=T_ROWS, tn=256, mode="nn", out_dtype=F32,
             name="ffn_down", residual=h1)
    loss, dh2, dh2b, g["final_norm_w"] = _loss_head(h2, target, p["final_norm_w"])
    dgp, dup = _ffn_bwd_act(dh2b, w_down, gp, up)
    g["w_down"], g_mxu["w_down"] = _mm([(act, 0, dh2b, 0, T_ROWS)], D_FF, D_MODEL, tm=1408, tn=512, mode="tn",
                                       out_dtype=F32, name="dw_down", also_mxu=True)
    dh1, dh1b, g["norm2_w"] = _mm_norm_bwd([(dgp, w_gate, D_FF), (dup, w_up, D_FF)], h1, p["norm2_w"], dh2,
                                           name="ffn_bwd_in")
    g["w_gate"], g_mxu["w_gate"] = _mm([(dgp, 0, u2, 0, T_ROWS)], D_FF, D_MODEL, tm=1408, tn=512, mode="tn",
                                       out_dtype=F32, name="dw_gate", also_mxu=True)
    g["w_up"], g_mxu["w_up"] = _mm([(dup, 0, u2, 0, T_ROWS)], D_FF, D_MODEL, tm=1408, tn=512, mode="tn",
                                   out_dtype=F32, name="dw_up", also_mxu=True)
    g["w_out"], g_mxu["w_out"] = _mm([(ycat, 0, dh1b, 0, T_ROWS)], 2 * D_MODEL, D_MODEL, tm=1024, tn=512, mode="tn",
                                     out_dtype=F32, name="dw_out", also_mxu=True)
    sent = late.grads_ready(("w_down", "w_gate", "w_up", "w_out"), g, g_mxu)
    dycat = _mm([(dh1b, 0, w_out, 0, D_MODEL)], T_ROWS, 2 * D_MODEL, tm=T_ROWS, tn=256, mode="nt", out_dtype=F32,
                name="out_proj_bwd", behind=(sent,))
    dgate, dxr, dwa_p, dwx_p, dlru_prm = _lru_bwd(dycat, 1, proj["g"], xr, hs, wa_p.astype(_MXU), wx_p.astype(_MXU),
                                                  jnp.swapaxes(wa_p, 1, 2).astype(_MXU),
                                                  jnp.swapaxes(wx_p, 1, 2).astype(_MXU), lru_prm)
    g["lru_wa"], g["lru_wx"] = _unpair_blocks(dwa_p), _unpair_blocks(dwx_p)
    g["lru_ba"], g["lru_bx"], g["lru_lambda"], g["lru_norm_w"] = (dlru_prm[k:k + 1] for k in range(4))
    dx_lru, g["lru_conv_w"], g["lru_conv_b"] = _conv_bwd(dxr, proj["x"], p["lru_conv_w"], p["lru_conv_b"], silu=False,
                                                         name="lru_conv_bwd")
    dz, dxs_act, dbc_act, ddt, dssd_prm, g["ssd_norm_w"] = _ssd_bwd(dycat, 0, proj["z"], y_pre, xs_act, bc_act,
                                                                    proj["dt"], prev, ssd_prm, p["ssd_norm_w"], ex)
    g["ssd_dt_bias"], g["ssd_a_log"], g["ssd_d"] = (dssd_prm[k:k + 1, :SSD_HEADS] for k in range(3))
    dxs, dcw_xs, dcb_xs = _conv_bwd(dxs_act, proj["xs"], p["ssd_conv_w"][:, :SSD_WIDTH],
                                    p["ssd_conv_b"][:, :SSD_WIDTH], silu=True, name="ssd_conv_xs_bwd")
    dbc, dcw_bc, dcb_bc = _conv_bwd(dbc_act, proj["bc"], p["ssd_conv_w"][:, SSD_WIDTH:],
                                    p["ssd_conv_b"][:, SSD_WIDTH:], silu=True, name="ssd_conv_bc_bwd")
    g["ssd_conv_w"] = jnp.concatenate([dcw_xs, dcw_bc], axis=1)
    g["ssd_conv_b"] = jnp.concatenate([dcb_xs, dcb_bc], axis=1)
    dproj = {"z": dz, "xs": dxs, "bc": dbc, "dt": ddt, "g": dgate, "x": dx_lru}
    dh0, _, g["norm1_w"] = _mm_norm_bwd([(dproj[s], w_in[s], SEC_WIDTH[s]) for s in SEC_NAMES], h0,
                                        p["norm1_w"], dh1, name="in_proj_bwd")
    g["meta_tokens"] = dh0[PAD_ROWS:X_ROW0]
    sent = late.small_ready(g, loss)
    for s in SEC_NAMES:
        wdt = SEC_WIDTH[s]
        g["w_in_" + s], g_mxu["w_in_" + s] = _mm([(dproj[s], 0, u1, 0, T_ROWS)], wdt, D_MODEL, tm=min(wdt, 1024),
                                                 tn=512, mode="tn", out_dtype=F32, name="dw_in_" + s, also_mxu=True,
                                                 behind=(sent,))
        if s == "bc":
            sent = late.small_middle([g["w_in_z"], g["w_in_xs"], g["w_in_bc"]])
    late.grads_ready(("w_in",), g, g_mxu)
    return loss, dh0[X_ROW0:], g, g_mxu


MESH = pl.DeviceIdType.MESH
ANY = pl.BlockSpec(memory_space=pl.ANY)


def _my_place():
    return lax.axis_index("x"), lax.axis_index("y"), lax.axis_index("c")


def _other_chips(x, y):
    return [(1 - x, y), (x, 1 - y), (1 - x, 1 - y)]


HBM_SPEC = pl.BlockSpec(memory_space=pltpu.HBM)
SEM_SPEC = pl.BlockSpec(memory_space=pltpu.SEMAPHORE)
SPLIT_EFFECT = pltpu.SideEffectType.DATAFLOW_SIDE_EFFECTING


def _half_cols(buf, c, other=False):
    half = buf.shape[-1] // 2
    return pl.ds(pl.multiple_of(((1 - c) if other else c) * half, 128), half)


def _halves_plan(bufs, x, y, c, incoming):
    plan = []
    for buf in bufs:
        cols = _half_cols(buf, c)
        for (px, py) in _other_chips(x, y):
            slot = 2 * px + py if incoming else 2 * x + y
            plan.append((buf.at[2 * x + y, :, cols], buf.at[slot, :, cols], (px, py, c)))
    return plan


def _forward_plan(bufs, x, y, c, incoming):
    plan = []
    for buf in bufs:
        for (px, py) in _other_chips(x, y):
            slot = 2 * px + py
            plan.append((buf.at[slot, :, _half_cols(buf, c)], buf.at[slot, :, _half_cols(buf, c, other=incoming)],
                         (x, y, 1 - c)))
    return plan


def _scatter_plan(bufs, x, y, c, incoming):
    n = len(bufs) // 2
    plan = []
    for k in range(n):
        for j, (px, py) in enumerate(_other_chips(x, y)):
            plan.append((bufs[k].at[2 * px + py], bufs[n + k].at[j], (px, py, c)))
    return plan


def _split_start(bufs, plan, n_copies, after, *, name):
    n = len(bufs)
    extra = [] if after is None else [after]

    def body(*refs):
        ins = refs[:n]
        send_sems, recv_sems = refs[n + len(extra)], refs[n + len(extra) + 1]
        token = refs[-1]
        x, y, c = _my_place()
        for i, (src, dst, dev) in enumerate(plan(ins, x, y, c, False)):
            pltpu.make_async_remote_copy(src_ref=src, dst_ref=dst, send_sem=send_sems.at[i], recv_sem=recv_sems.at[i],
                                         device_id=dev, device_id_type=MESH).start()
        token[...] = jnp.zeros_like(token)

    outs = pl.pallas_call(
        body, name=name,
        out_shape=(pltpu.SemaphoreType.DMA((n_copies,)), pltpu.SemaphoreType.DMA((n_copies,)),
                   *[pltpu.HBM(b.shape, b.dtype) for b in bufs], jax.ShapeDtypeStruct((8, 128), F32)),
        in_specs=[HBM_SPEC] * n + [ANY] * len(extra),
        out_specs=(SEM_SPEC, SEM_SPEC, *[HBM_SPEC] * n, pl.BlockSpec(memory_space=pltpu.VMEM)),
        input_output_aliases={k: 2 + k for k in range(n)},
        compiler_params=pltpu.CompilerParams(has_side_effects=SPLIT_EFFECT),
    )(*[pltpu.with_memory_space_constraint(b, pltpu.HBM) for b in bufs], *extra)
    return outs[0], outs[1], list(outs[2:2 + n]), outs[-1]


def _split_wait(bufs, send_sems, recv_sems, plan, after, *, name):
    n = len(bufs)
    after = list(after) if isinstance(after, (list, tuple)) else [after]

    def body(*refs):
        ins = refs[:n]
        send_sems_ref, recv_sems_ref = refs[n], refs[n + 1]
        x, y, c = _my_place()
        for i, (src, dst, dev) in enumerate(plan(ins, x, y, c, True)):
            cp = pltpu.make_async_remote_copy(src_ref=src, dst_ref=dst, send_sem=send_sems_ref.at[i],
                                              recv_sem=recv_sems_ref.at[i], device_id=dev, device_id_type=MESH)
            cp.wait_send()
            cp.wait_recv()

    outs = pl.pallas_call(
        body, name=name, out_shape=tuple(pltpu.HBM(b.shape, b.dtype) for b in bufs),
        in_specs=[HBM_SPEC] * n + [SEM_SPEC, SEM_SPEC] + [ANY] * len(after), out_specs=tuple([HBM_SPEC] * n),
        input_output_aliases={k: k for k in range(n)},
        compiler_params=pltpu.CompilerParams(has_side_effects=SPLIT_EFFECT),
    )(*bufs, send_sems, recv_sems, *after)
    return list(outs)


def _fill_own_slots(shards, me_arr, *, name, behind=()):
    n = len(shards)
    n_in = n + len(behind)

    def body(me_ref, *refs):
        for k in range(n):
            refs[n_in + k][0] = refs[k][...].astype(_MXU)

    half = D_MODEL // 2
    return pl.pallas_call(
        body, name=name,
        grid_spec=pltpu.PrefetchScalarGridSpec(
            num_scalar_prefetch=1, grid=(2,),
            in_specs=[pl.BlockSpec((s.shape[0], half), lambda i, me: (0, i)) for s in shards]
            + [pl.BlockSpec(memory_space=pl.ANY)] * len(behind),
            out_specs=[pl.BlockSpec((1, s.shape[0], half), lambda i, me: (me[0], 0, i)) for s in shards]),
        out_shape=[jax.ShapeDtypeStruct((N_SHARDS,) + s.shape, _MXU) for s in shards],
        compiler_params=_cparams("parallel"),
    )(me_arr, *shards, *behind)


def _gather_small(small):
    def body(s_ref, o_ref, send_sems, recv_sems, local_sem):
        x, y, c = _my_place()
        me = 2 * x + y
        local = pltpu.make_async_copy(s_ref, o_ref.at[me], local_sem)
        local.start()
        copies = [(pltpu.make_async_remote_copy(src_ref=s_ref, dst_ref=o_ref.at[me], send_sem=send_sems.at[j],
                                                recv_sem=recv_sems.at[j], device_id=(px, py, c), device_id_type=MESH),
                   2 * px + py) for j, (px, py) in enumerate(_other_chips(x, y))]
        for cp, _ in copies:
            cp.start()
        for j, (cp, slot) in enumerate(copies):
            cp.wait_send()
            pltpu.make_async_remote_copy(src_ref=s_ref, dst_ref=o_ref.at[slot], send_sem=send_sems.at[j],
                                         recv_sem=recv_sems.at[j], device_id=(x, y, c),
                                         device_id_type=MESH).wait_recv()
        local.wait()

    return pl.pallas_call(
        body, name="gather_small", in_specs=[ANY], out_specs=ANY,
        out_shape=jax.ShapeDtypeStruct((N_SHARDS,) + small.shape, small.dtype),
        scratch_shapes=[pltpu.SemaphoreType.DMA((3,)), pltpu.SemaphoreType.DMA((3,)), pltpu.SemaphoreType.DMA],
    )(small)


def _swap_with_sibling(parts, *, name):
    n = len(parts)

    def body(*refs):
        ins, outs = refs[:n], refs[n:2 * n]
        send_sems, recv_sems = refs[2 * n:]
        x, y, c = _my_place()
        copies = [pltpu.make_async_remote_copy(
            src_ref=ins[k], dst_ref=outs[k], send_sem=send_sems.at[k], recv_sem=recv_sems.at[k],
            device_id=(x, y, 1 - c), device_id_type=MESH) for k in range(n)]
        for cp in copies:
            cp.start()
        for cp in copies:
            cp.wait()

    return pl.pallas_call(
        body, name=name, in_specs=[ANY] * n, out_specs=[ANY] * n,
        out_shape=[jax.ShapeDtypeStruct(a.shape, a.dtype) for a in parts],
        scratch_shapes=[pltpu.SemaphoreType.DMA((n,)), pltpu.SemaphoreType.DMA((n,))],
    )(*parts)


def _other_devices(x, y, c):
    out = []
    for mask in range(1, N_DEV):
        px, py, pc = x ^ (mask >> 2 & 1), y ^ (mask >> 1 & 1), c ^ (mask & 1)
        out.append(((px, py, pc), 4 * px + 2 * py + pc))
    return out


def _pieces_plan(bufs, x, y, c, incoming):
    pack, land = bufs
    me = 4 * x + 2 * y + c
    return [(pack.at[num], land.at[num if incoming else me], dev) for dev, num in _other_devices(x, y, c)]


def _spread_plan(bufs, x, y, c, incoming):
    piece, land = bufs
    me = 4 * x + 2 * y + c
    return [(piece, land.at[num if incoming else me], dev) for dev, num in _other_devices(x, y, c)]


def _sum_pieces(pack, land, dev_arr, *, name):
    def body(dev_ref, pack_ref, land_ref, o_ref):
        dev = dev_ref[0]
        own = pack_ref[dev]
        acc = None
        for d in range(N_DEV):
            term = jnp.where(dev == d, own, land_ref[d])
            acc = term if acc is None else acc + term
        o_ref[...] = acc

    vmem = pl.BlockSpec(memory_space=pltpu.VMEM)
    return pl.pallas_call(
        body, name=name, in_specs=[pl.BlockSpec(memory_space=pltpu.SMEM), vmem, vmem], out_specs=vmem,
        out_shape=jax.ShapeDtypeStruct(pack.shape[1:], F32),
    )(dev_arr, pack, land)


def _join_pieces(piece, land, dev_arr, *, name):
    def body(dev_ref, piece_ref, land_ref, o_ref):
        dev = dev_ref[0]
        for d in range(N_DEV):
            o_ref[d] = jnp.where(dev == d, piece_ref[...], land_ref[d])

    vmem = pl.BlockSpec(memory_space=pltpu.VMEM)
    return pl.pallas_call(
        body, name=name, in_specs=[pl.BlockSpec(memory_space=pltpu.SMEM), vmem, vmem], out_specs=vmem,
        out_shape=jax.ShapeDtypeStruct(land.shape, F32),
    )(dev_arr, piece, land)


def _adamw_native(ws, gs, ms, vs):
    n = len(ws)

    def body(*refs):
        for k in range(n):
            w_ref, g_ref, m_ref, v_ref = (refs[j * n + k] for j in range(4))
            delta, m_new, v_new = _adamw_math(w_ref[...], g_ref[...], m_ref[...], v_ref[...])
            refs[4 * n + k][...] = delta
            refs[5 * n + k][...] = m_new
            refs[6 * n + k][...] = v_new

    vmem = pl.BlockSpec(memory_space=pltpu.VMEM)
    shapes = [jax.ShapeDtypeStruct(a.shape, F32) for a in ws]
    outs = pl.pallas_call(
        body, name="adamw_small", in_specs=[vmem] * (4 * n), out_specs=[vmem] * (3 * n), out_shape=shapes * 3,
        compiler_params=pltpu.CompilerParams(vmem_limit_bytes=VMEM_LIMIT_BYTES),
    )(*ws, *gs, *ms, *vs)
    return outs[:n], outs[n:2 * n], outs[2 * n:]


def _elementwise_tile(rows, cols):
    for t in range(256, 15, -16):
        if rows % t == 0:
            return (t, cols), rows // t, lambda i: (i, 0)
    assert cols % 256 == 0
    return (rows, 256), cols // 256, lambda i: (0, i)


def _partial_sum(own, land, me_arr, *, name):
    r, c = own.shape[-2:]
    tile, steps, imap = _elementwise_tile(r, c)
    whole = own.ndim == 3

    def body(me_ref, own_ref, land_ref, o_ref):
        acc = own_ref[0] if whole else own_ref[...]
        for j in range(3):
            acc = acc + land_ref[j].astype(F32)
        o_ref[...] = acc.astype(_MXU)

    own_spec = (pl.BlockSpec((1,) + tile, lambda i, me: (me[0],) + imap(i)) if whole
                else pl.BlockSpec(tile, lambda i, me: imap(i)))
    return pl.pallas_call(
        body, name=name,
        grid_spec=pltpu.PrefetchScalarGridSpec(
            num_scalar_prefetch=1, grid=(steps,),
            in_specs=[own_spec, pl.BlockSpec((3,) + tile, lambda i, me: (0,) + imap(i))],
            out_specs=pl.BlockSpec(tile, lambda i, me: imap(i))),
        out_shape=jax.ShapeDtypeStruct((r, c), _MXU),
        compiler_params=_cparams("parallel"),
    )(me_arr, own, land)


def _adamw_math(w, g, m, v):
    m = ADAM_B1 * m + (1.0 - ADAM_B1) * g
    v = ADAM_B2 * v + (1.0 - ADAM_B2) * (g * g)
    m_hat = m / (1.0 - ADAM_B1 ** ADAM_STEP)
    v_hat = v / (1.0 - ADAM_B2 ** ADAM_STEP)
    delta = -ADAM_LR * (m_hat / (jnp.sqrt(v_hat) + ADAM_EPS) + ADAM_WD * w)
    return delta, m, v


def _adamw(w, grad_parts, m, v, *, name):
    if w.ndim == 3:
        r = w.shape[0]
        steps = 4
        assert r % steps == 0
        tile = pl.BlockSpec((r // steps,) + w.shape[1:], lambda i: (i, 0, 0))
    else:
        tile_shape, steps, imap = _elementwise_tile(*w.shape)
        tile = pl.BlockSpec(tile_shape, imap)
    n = len(grad_parts)

    def body(*refs):
        w_ref, m_ref, v_ref = refs[:3]
        g_refs = refs[3:3 + n]
        g_out, d_out, m_out, v_out = refs[3 + n:]
        g = g_refs[0][...].astype(F32)
        for k in range(1, n):
            g = g + g_refs[k][...].astype(F32)
        delta, m_new, v_new = _adamw_math(w_ref[...], g, m_ref[...], v_ref[...])
        g_out[...] = g
        d_out[...] = delta
        m_out[...] = m_new
        v_out[...] = v_new

    return pl.pallas_call(
        body, name=name, grid=(steps,), in_specs=[tile] * (3 + n), out_specs=[tile] * 4,
        out_shape=[jax.ShapeDtypeStruct(w.shape, F32)] * 4,
        compiler_params=_cparams("parallel"),
    )(w, m, v, *grad_parts)


WEIGHT_NAMES = ("meta_tokens", "norm1_w", "w_in", "ssd_conv_w", "ssd_conv_b", "ssd_dt_bias", "ssd_a_log", "ssd_d",
                "ssd_norm_w", "lru_conv_w", "lru_conv_b", "lru_wa", "lru_ba", "lru_wx", "lru_bx", "lru_lambda",
                "lru_norm_w", "w_out", "norm2_w", "w_gate", "w_up", "w_down", "final_norm_w")
BIG = ("w_in", "w_out", "w_gate", "w_up", "w_down")
FFN = ("w_gate", "w_up", "w_down")
LATE = ("w_out",) + FFN
SMALL_SHARDED = {"meta_tokens": (N_META, D_MODEL), "ssd_conv_w": (CONV_K, 1536), "lru_conv_w": (CONV_K, LRU_WIDTH)}
SMALL = tuple(n for n in WEIGHT_NAMES if n not in BIG)
PACK_COLS = 1024


def _pack(arrays, row_multiple):
    flat = jnp.concatenate([a.reshape(-1) for a in arrays])
    rows = -(-flat.shape[0] // (row_multiple * PACK_COLS)) * row_multiple
    return jnp.pad(flat, (0, rows * PACK_COLS - flat.shape[0])).reshape(rows, PACK_COLS)


def _unpack(pack, shapes):
    flat = pack.reshape(-1)
    out, off = [], 0
    for s in shapes:
        size = math.prod(s)
        out.append(flat[off:off + size].reshape(s))
        off += size
    return out


def _unshard_cols(g4):
    return jnp.swapaxes(g4, 0, 1).reshape(g4.shape[1], -1)


COL_SHARDED = ("w_in", "w_gate", "w_up")
IN_ROWS = {"z": (0, 1024), "xs": (1024, 2048), "bc": (2048, 2560), "dt": (2560, 2576), "g": (2576, 3600),
           "x": (3600, IN_COLS)}


def _rows_of_shards(shards4, lo, hi):
    r = shards4.shape[1]
    parts = [shards4[k, max(lo, k * r) - k * r:min(hi, (k + 1) * r) - k * r]
             for k in range(N_SHARDS) if max(lo, k * r) < min(hi, (k + 1) * r)]
    return parts[0] if len(parts) == 1 else jnp.concatenate(parts, axis=0)


def _w_in_shard_rows(k, sections):
    lo, hi = k * (IN_COLS // N_SHARDS), (k + 1) * (IN_COLS // N_SHARDS)
    parts = []
    for arr, (a, b) in zip(sections, IN_ROWS.values()):
        if max(lo, a) < min(hi, b):
            parts.append(arr[max(lo, a) - a:min(hi, b) - a])
    return jnp.concatenate(parts, axis=0)


def _rows_view(name, block):
    return jnp.swapaxes(block[0], 0, 1) if name in COL_SHARDED else block[0]


def _param_view(name, rows):
    return (jnp.swapaxes(rows, 0, 1) if name in COL_SHARDED else rows)[None]


def kernel(x, meta_tokens, norm1_w, w_in, ssd_conv_w, ssd_conv_b, ssd_dt_bias, ssd_a_log, ssd_d, ssd_norm_w, lru_conv_w, lru_conv_b, lru_wa, lru_ba, lru_wx, lru_bx, lru_lambda, lru_norm_w, w_out, norm2_w, w_gate, w_up, w_down, final_norm_w, loss_target, m_meta_tokens, m_norm1_w, m_w_in, m_ssd_conv_w, m_ssd_conv_b, m_ssd_dt_bias, m_ssd_a_log, m_ssd_d, m_ssd_norm_w, m_lru_conv_w, m_lru_conv_b, m_lru_wa, m_lru_ba, m_lru_wx, m_lru_bx, m_lru_lambda, m_lru_norm_w, m_w_out, m_norm2_w, m_w_gate, m_w_up, m_w_down, m_final_norm_w, v_meta_tokens, v_norm1_w, v_w_in, v_ssd_conv_w, v_ssd_conv_b, v_ssd_dt_bias, v_ssd_a_log, v_ssd_d, v_ssd_norm_w, v_lru_conv_w, v_lru_conv_b, v_lru_wa, v_lru_ba, v_lru_wx, v_lru_bx, v_lru_lambda, v_lru_norm_w, v_w_out, v_norm2_w, v_w_gate, v_w_up, v_w_down, v_final_norm_w):
    w = dict(zip(WEIGHT_NAMES, (meta_tokens, norm1_w, w_in, ssd_conv_w, ssd_conv_b, ssd_dt_bias, ssd_a_log, ssd_d, ssd_norm_w, lru_conv_w, lru_conv_b, lru_wa, lru_ba, lru_wx, lru_bx, lru_lambda, lru_norm_w, w_out, norm2_w, w_gate, w_up, w_down, final_norm_w)))
    m = dict(zip(WEIGHT_NAMES, (m_meta_tokens, m_norm1_w, m_w_in, m_ssd_conv_w, m_ssd_conv_b, m_ssd_dt_bias, m_ssd_a_log, m_ssd_d, m_ssd_norm_w, m_lru_conv_w, m_lru_conv_b, m_lru_wa, m_lru_ba, m_lru_wx, m_lru_bx, m_lru_lambda, m_lru_norm_w, m_w_out, m_norm2_w, m_w_gate, m_w_up, m_w_down, m_final_norm_w)))
    v = dict(zip(WEIGHT_NAMES, (v_meta_tokens, v_norm1_w, v_w_in, v_ssd_conv_w, v_ssd_conv_b, v_ssd_dt_bias, v_ssd_a_log, v_ssd_d, v_ssd_norm_w, v_lru_conv_w, v_lru_conv_b, v_lru_wa, v_lru_ba, v_lru_wx, v_lru_bx, v_lru_lambda, v_lru_norm_w, v_w_out, v_norm2_w, v_w_gate, v_w_up, v_w_down, v_final_norm_w)))
    me = 2 * lax.axis_index("x") + lax.axis_index("y")

    big2d = {n: _rows_view(n, w[n]) for n in BIG}
    small_local = jnp.concatenate([w["meta_tokens"].reshape(-1), w["ssd_conv_w"].reshape(-1),
                                   w["lru_conv_w"].reshape(-1)])[None]
    me_arr = me.astype(jnp.int32).reshape(1)
    dev_arr = (2 * me + lax.axis_index("c")).astype(jnp.int32).reshape(1)
    small4 = _gather_small(small_local)
    (w_in_slot,) = _fill_own_slots([big2d["w_in"]], me_arr, name="own_slot_w_in")
    in_send, in_recv, in_bufs, in_tok = _split_start([w_in_slot], _halves_plan, 3, small4, name="gather_w_in_start")
    late_slots = _fill_own_slots([big2d[n] for n in LATE], me_arr, name="own_slots_late", behind=(in_tok,))
    sm = small4[:, 0]
    meta_full = _unshard_cols(sm[:, :4096].reshape(N_SHARDS, N_META, 256))
    ssd_conv_w_full = _unshard_cols(sm[:, 4096:5632].reshape(N_SHARDS, CONV_K, 384))
    lru_conv_w_full = _unshard_cols(sm[:, 5632:].reshape(N_SHARDS, CONV_K, 256))

    p = {"ssd_conv_w": ssd_conv_w_full, "lru_conv_w": lru_conv_w_full,
         "lru_wa": w["lru_wa"][0], "lru_wx": w["lru_wx"][0], "final_norm_w": w["final_norm_w"][None]}
    for n in ("norm1_w", "ssd_conv_b", "ssd_dt_bias", "ssd_a_log", "ssd_d", "ssd_norm_w", "lru_conv_b", "lru_ba",
              "lru_bx", "lru_lambda", "lru_norm_w", "norm2_w"):
        p[n] = w[n]

    class Late:
        def __init__(self):
            self.pending = []
            self.before_embed = (late_slots[0],)

        def w_in(self, after):
            (buf,) = _split_wait(in_bufs, in_send, in_recv, _halves_plan, after, name="gather_w_in_wait")
            send, recv, bufs, tok = _split_start([buf], _forward_plan, 3, None, name="forward_w_in_start")
            self.late_gather = _split_start(late_slots, _halves_plan, 3 * len(LATE), tok, name="gather_late_start")
            (w_in4,) = _split_wait(bufs, send, recv, _forward_plan, self.late_gather[2][0], name="forward_w_in_wait")
            sections = {s: _rows_of_shards(w_in4, lo, hi) for s, (lo, hi) in IN_ROWS.items()}
            sections["dt"] = jnp.pad(sections["dt"], ((0, SEC_WIDTH["dt"] - SSD_HEADS), (0, 0)))
            return sections

        def mid_forward(self, after):
            send, recv, bufs, _ = self.late_gather
            bufs = _split_wait(bufs, send, recv, _halves_plan, after, name="gather_late_wait")
            self.forward = _split_start(bufs, _forward_plan, 3 * len(LATE), None, name="forward_late_start")
            return self.forward[3][:1, :1]

        def w_out(self, after):
            send, recv, bufs, _ = self.forward
            bufs = _split_wait(bufs, send, recv, _forward_plan, after, name="forward_late_wait")
            self.late = dict(zip(LATE, (b.reshape(-1, D_MODEL) for b in bufs)))
            return self.late["w_out"]

        def ffn(self, after):
            return tuple(self.late[n] for n in FFN)

        def grads_ready(self, names, g, g_mxu):
            if names == ("w_in",):
                g_mxu["w_in"] = jnp.stack([_w_in_shard_rows(k, [g_mxu["w_in_" + s] for s in SEC_NAMES])
                                           for k in range(N_SHARDS)])
            srcs = [g_mxu[n].reshape(N_SHARDS, -1, D_MODEL) for n in names]
            lands = [lax.empty((3,) + s.shape[1:], _MXU) for s in srcs]
            tag = "_".join(names)
            send, recv, bufs, tok = _split_start(srcs + lands, _scatter_plan, 3 * len(names), None,
                                                 name="scatter_" + tag + "_start")
            self.pending.append((names, send, recv, bufs, tag))
            self.in_flight = bufs[0]
            return tok[:1, :1]

        def landed(self, after, which):
            land = {}
            for names, send, recv, bufs, tag in self.pending:
                if names[0] in which:
                    bufs = _split_wait(bufs, send, recv, _scatter_plan, after, name="scatter_" + tag + "_wait")
                    land.update(zip(names, bufs[len(names):]))
            return land

        def small_ready(self, g, loss):
            pack = _pack([g[n] for n in SMALL] + [loss[0, :1]], 8 * N_DEV)
            pack = pack.reshape(N_DEV, -1, PACK_COLS)
            self.small = _split_start([pack, lax.empty(pack.shape, F32)], _pieces_plan, N_DEV - 1, loss,
                                      name="small_pieces_start")
            return self.small[3]

        def small_middle(self, after):
            send, recv, bufs, _ = self.small
            pack, land = _split_wait(bufs, send, recv, _pieces_plan, after, name="small_pieces_wait")
            piece = _sum_pieces(pack, land, dev_arr, name="small_pieces_sum")
            self.small = _split_start([piece, lax.empty(pack.shape, F32)], _spread_plan, N_DEV - 1, None,
                                      name="small_spread_start")
            return self.small[3]

        def small_sum(self, after):
            send, recv, bufs, _ = self.small
            piece, land = _split_wait(bufs, send, recv, _spread_plan, after, name="small_spread_wait")
            return _join_pieces(piece, land, dev_arr, name="small_join")

    late = Late()

    loss, grad_x, g, g_mxu = _local_step(x[0], loss_target[0], meta_full, p, late)

    g4 = {n: g[n].reshape(N_SHARDS, -1, D_MODEL) for n in LATE}
    g4["w_in"] = lax.switch(me, [functools.partial(_w_in_shard_rows, k) for k in range(N_SHARDS)],
                            [g["w_in_" + s] for s in SEC_NAMES])
    land = late.landed(late.in_flight, LATE)
    part = {n: _partial_sum(g4[n], land[n], me_arr, name="partial_" + n) for n in LATE}
    sib = dict(zip(LATE, _swap_with_sibling([part[n] for n in LATE], name="swap_late")))

    small_full_shape = {n: (SMALL_SHARDED[n] if n in SMALL_SHARDED else w[n].shape) for n in SMALL}
    red_list = _unpack(late.small_sum(sib["w_out"]), [small_full_shape[n] for n in SMALL] + [(1,)])
    loss_total = red_list[-1][0]
    g_small = {}
    for n, arr in zip(SMALL, red_list[:-1]):
        if n in SMALL_SHARDED:
            cols = SMALL_SHARDED[n][1] // N_SHARDS
            arr = lax.dynamic_slice_in_dim(arr, me * cols, cols, axis=1)
        g_small[n] = arr.reshape(w[n].shape)

    grad, delta, new_m, new_v = {}, {}, {}, {}

    def update_big(n):
        shape = (-1, 8, 128) if n == "w_in" else big2d[n].shape
        outs = _adamw(big2d[n].reshape(shape), [part[n].reshape(shape), sib[n].reshape(shape)],
                      _rows_view(n, m[n]).reshape(shape), _rows_view(n, v[n]).reshape(shape), name="adamw_" + n)
        outs = [o.reshape(big2d[n].shape) for o in outs]
        grad[n], delta[n], new_m[n], new_v[n] = (_param_view(n, o) for o in outs)
        return outs[0]

    two_d = lambda a: a.reshape(1, -1) if a.ndim == 1 else a
    deltas, new_ms, new_vs = _adamw_native(*[[two_d(d[n]) for n in SMALL] for d in (w, g_small, m, v)])
    for n, dn, mn, vn in zip(SMALL, deltas, new_ms, new_vs):
        grad[n], delta[n], new_m[n], new_v[n] = (g_small[n], dn.reshape(w[n].shape), mn.reshape(w[n].shape),
                                                 vn.reshape(w[n].shape))
    land.update(late.landed([update_big(n) for n in LATE] + [deltas[0]], ("w_in",)))
    part["w_in"] = _partial_sum(g4["w_in"], land["w_in"], me_arr, name="partial_w_in")
    (sib["w_in"],) = _swap_with_sibling([part["w_in"]], name="swap_w_in")
    update_big("w_in")

    return (loss_total, grad_x[None], *[grad[n] for n in WEIGHT_NAMES], *[delta[n] for n in WEIGHT_NAMES],
            *[new_m[n] for n in WEIGHT_NAMES], *[new_v[n] for n in WEIGHT_NAMES])
```

```python
import functools
import math

import jax
import jax.numpy as jnp
from jax import lax
from jax.experimental import pallas as pl
from jax.experimental.pallas import tpu as pltpu

F32 = jnp.float32
_MXU = jnp.bfloat16

D_MODEL = 1024
SEQ = 2048
N_META = 16
CHUNK = 128
T_ROWS = 2176
N_CHUNKS = T_ROWS // CHUNK
PAD_ROWS = T_ROWS - SEQ - N_META
X_ROW0 = PAD_ROWS + N_META
SSD_HEADS = 16
SSD_HEAD_DIM = 64
SSD_STATE = 128
SSD_GROUPS = 2
SSD_HPG = SSD_HEADS // SSD_GROUPS
SSD_WIDTH = 1024
LRU_WIDTH = 1024
LRU_C = 8.0
D_FF = 2816
EPS = 1e-6
IN_COLS = 4624
N_SHARDS = 4
N_DEV = 8

ADAM_LR = 0.001
ADAM_B1 = 0.9
ADAM_B2 = 0.999
ADAM_EPS = 1e-08
ADAM_WD = 0.01
ADAM_STEP = 10

VMEM_LIMIT_BYTES = 56 * 1024 * 1024

NN = (((1,), (0,)), ((), ()))
NT = (((1,), (1,)), ((), ()))
TN = (((0,), (0,)), ((), ()))


def _cparams(*sem):
    return pltpu.CompilerParams(dimension_semantics=sem, vmem_limit_bytes=VMEM_LIMIT_BYTES)


def _dot(a, b, dims=NN):
    return lax.dot_general(a.astype(_MXU), b.astype(_MXU), dims, preferred_element_type=F32)


def _dot_onehot(a, b, dims=NN, *, data=0, pieces=3):
    ops = [a, b]
    mask = ops[1 - data].astype(jnp.bfloat16)
    rest = ops[data]
    acc = None
    for _ in range(pieces):
        piece = rest.astype(jnp.bfloat16)
        ops[data], ops[1 - data] = piece, mask
        d = lax.dot_general(ops[0], ops[1], dims, preferred_element_type=F32)
        acc = d if acc is None else acc + d
        rest = rest - piece.astype(F32)
    return acc


def _sigmoid(x):
    return 0.5 * (1.0 + jnp.tanh(0.5 * x))


def _softplus(x):
    return jnp.maximum(x, 0.0) + jnp.log(1.0 + jnp.exp(-jnp.abs(x)))


def _silu(x):
    return x * _sigmoid(x)


def _silu_grad(x):
    s = _sigmoid(x)
    return s * (1.0 + x * (1.0 - s))


_GELU_C = math.sqrt(2.0 / math.pi)


def _gelu_and_grad(x):
    inner = _GELU_C * (x + 0.044715 * x * x * x)
    t = jnp.tanh(inner)
    g = 0.5 * x * (1.0 + t)
    dg = 0.5 * (1.0 + t) + 0.5 * x * (1.0 - t * t) * _GELU_C * (1.0 + 3.0 * 0.044715 * x * x)
    return g, dg


def _rms_fwd(x, w):
    rstd = lax.rsqrt(jnp.mean(x * x, axis=-1, keepdims=True) + EPS)
    return x * rstd * w


def _rms_bwd(x, w, dy):
    rstd = lax.rsqrt(jnp.mean(x * x, axis=-1, keepdims=True) + EPS)
    xhat = x * rstd
    dxhat = dy * w
    dx = rstd * (dxhat - xhat * jnp.mean(dxhat * xhat, axis=-1, keepdims=True))
    return dx, dy * xhat


def _mm(terms, m, n, *, tm, tn, mode, out_dtype, name, residual=None, n_outer=False, also_mxu=False, behind=()):
    gm, gn = m // tm, n // tn
    assert gm * tm == m and gn * tn == n
    if n_outer:
        grid = (gn, gm)
        mi = lambda g0, g1: g1
        ni = lambda g0, g1: g0
    else:
        grid = (gm, gn)
        mi = lambda g0, g1: g0
        ni = lambda g0, g1: g1
    in_specs, args = [], []
    for (a, ka, b, kb, k) in terms:
        if mode == "tn":
            in_specs.append(pl.BlockSpec((k, tm), lambda g0, g1, ka=ka: (ka, mi(g0, g1))))
        else:
            in_specs.append(pl.BlockSpec((tm, k), lambda g0, g1, ka=ka: (mi(g0, g1), ka)))
        if mode == "nt":
            in_specs.append(pl.BlockSpec((tn, k), lambda g0, g1, kb=kb: (ni(g0, g1), kb)))
        else:
            in_specs.append(pl.BlockSpec((k, tn), lambda g0, g1, kb=kb: (kb, ni(g0, g1))))
        args += [a, b]
    if residual is not None:
        in_specs.append(pl.BlockSpec((tm, tn), lambda g0, g1: (mi(g0, g1), ni(g0, g1))))
        args.append(residual)
    dims = {"nn": NN, "nt": NT, "tn": TN}[mode]
    n_terms = len(terms)
    has_res = residual is not None
    in_specs += [pl.BlockSpec(memory_space=pl.ANY)] * len(behind)
    args += list(behind)
    n_in = len(args)

    def body(*refs):
        acc = None
        for t in range(n_terms):
            d = lax.dot_general(refs[2 * t][...], refs[2 * t + 1][...], dims, preferred_element_type=F32)
            acc = d if acc is None else acc + d
        if has_res:
            acc = acc + refs[2 * n_terms][...]
        refs[n_in][...] = acc.astype(out_dtype)
        if also_mxu:
            refs[n_in + 1][...] = acc.astype(_MXU)

    tile = pl.BlockSpec((tm, tn), lambda g0, g1: (mi(g0, g1), ni(g0, g1)))
    shape = jax.ShapeDtypeStruct((m, n), out_dtype)
    return pl.pallas_call(
        body, name=name, grid=grid, in_specs=in_specs,
        out_specs=[tile, tile] if also_mxu else tile,
        out_shape=[shape, jax.ShapeDtypeStruct((m, n), _MXU)] if also_mxu else shape,
        compiler_params=_cparams("parallel", "parallel"),
    )(*args)


def _embed(x, meta, behind=()):
    def body(x_ref, meta_ref, *rest):
        o_ref = rest[-1]
        i = pl.program_id(0)

        @pl.when(i == 0)
        def _():
            o_ref[0:PAD_ROWS, :] = jnp.zeros((PAD_ROWS, D_MODEL), F32)
            o_ref[PAD_ROWS:CHUNK, :] = meta_ref[...]

        @pl.when(i > 0)
        def _():
            o_ref[...] = x_ref[...]

    return pl.pallas_call(
        body, name="embed", grid=(N_CHUNKS,),
        in_specs=[pl.BlockSpec((CHUNK, D_MODEL), lambda i: (jnp.maximum(i - 1, 0), 0)),
                  pl.BlockSpec((N_META, D_MODEL), lambda i: (0, 0))] + [pl.BlockSpec(memory_space=pl.ANY)] * len(behind),
        out_specs=pl.BlockSpec((CHUNK, D_MODEL), lambda i: (i, 0)),
        out_shape=jax.ShapeDtypeStruct((T_ROWS, D_MODEL), F32),
        compiler_params=_cparams("parallel"),
    )(x, meta, *behind)


def _rmsnorm(h, w, *, name, tm=544):
    def body(h_ref, w_ref, o_ref):
        o_ref[...] = _rms_fwd(h_ref[...], w_ref[...]).astype(_MXU)

    return pl.pallas_call(
        body, name=name, grid=(T_ROWS // tm,),
        in_specs=[pl.BlockSpec((tm, D_MODEL), lambda i: (i, 0)), pl.BlockSpec((1, D_MODEL), lambda i: (0, 0))],
        out_specs=pl.BlockSpec((tm, D_MODEL), lambda i: (i, 0)),
        out_shape=jax.ShapeDtypeStruct((T_ROWS, D_MODEL), _MXU),
        compiler_params=_cparams("parallel"),
    )(h, w)


def _norm_proj(h, w, sections, *, name, tm=544):
    widths = [s.shape[0] for s in sections]
    n = len(sections)

    def body(*refs):
        h_ref, w_ref = refs[:2]
        u_ref = refs[2 + n]
        u = _rms_fwd(h_ref[...], w_ref[...]).astype(_MXU)
        u_ref[...] = u
        for k in range(n):
            refs[3 + n + k][...] = lax.dot_general(u, refs[2 + k][...], NT, preferred_element_type=F32)

    row = lambda width: pl.BlockSpec((tm, width), lambda i: (i, 0))
    outs = pl.pallas_call(
        body, name=name, grid=(T_ROWS // tm,),
        in_specs=[row(D_MODEL), pl.BlockSpec((1, D_MODEL), lambda i: (0, 0))]
        + [pl.BlockSpec((wd, D_MODEL), lambda i: (0, 0)) for wd in widths],
        out_specs=[row(D_MODEL)] + [row(wd) for wd in widths],
        out_shape=[jax.ShapeDtypeStruct((T_ROWS, D_MODEL), _MXU)]
        + [jax.ShapeDtypeStruct((T_ROWS, wd), F32) for wd in widths],
        compiler_params=_cparams("parallel"),
    )(h, w, *sections)
    return outs[0], list(outs[1:])


def _loss_head(h2, target, fw):
    def body(h_ref, t_ref, w_ref, loss_ref, dh_ref, dhb_ref, dw_ref, acc_ref):
        i = pl.program_id(0)

        @pl.when(i == 0)
        def _():
            acc_ref[...] = jnp.zeros_like(acc_ref)
            dw_ref[...] = jnp.zeros_like(dw_ref)

        h = h_ref[...]
        w = w_ref[...]
        y = _rms_fwd(h, w)
        live = (i > 0).astype(F32)
        err = (y - t_ref[...]) * live
        acc_ref[...] += jnp.sum(err * err, axis=0, keepdims=True)
        dy = err * (1.0 / D_MODEL)
        dx, dwr = _rms_bwd(h, w, dy)
        dh_ref[...] = dx
        dhb_ref[...] = dx.astype(_MXU)
        dw_ref[...] += jnp.sum(dwr, axis=0, keepdims=True)

        @pl.when(i == N_CHUNKS - 1)
        def _():
            tot = jnp.sum(acc_ref[...], axis=1, keepdims=True) * (0.5 / D_MODEL)
            loss_ref[...] = jnp.broadcast_to(tot, (1, 128))

    return pl.pallas_call(
        body, name="loss_head", grid=(N_CHUNKS,),
        in_specs=[pl.BlockSpec((CHUNK, D_MODEL), lambda i: (i, 0)),
                  pl.BlockSpec((CHUNK, D_MODEL), lambda i: (jnp.maximum(i - 1, 0), 0)),
                  pl.BlockSpec((1, D_MODEL), lambda i: (0, 0))],
        out_specs=[pl.BlockSpec((1, 128), lambda i: (0, 0)),
                   pl.BlockSpec((CHUNK, D_MODEL), lambda i: (i, 0)),
                   pl.BlockSpec((CHUNK, D_MODEL), lambda i: (i, 0)),
                   pl.BlockSpec((1, D_MODEL), lambda i: (0, 0))],
        out_shape=[jax.ShapeDtypeStruct((1, 128), F32),
                   jax.ShapeDtypeStruct((T_ROWS, D_MODEL), F32),
                   jax.ShapeDtypeStruct((T_ROWS, D_MODEL), _MXU),
                   jax.ShapeDtypeStruct((1, D_MODEL), F32)],
        scratch_shapes=[pltpu.VMEM((1, D_MODEL), F32)],
        compiler_params=_cparams("arbitrary"),
    )(h2, target, fw)


def _mm_norm_bwd(terms, h, w, dres, *, name, tm=272, behind=()):
    n_terms = len(terms)
    in_specs, args = [], []
    for (a, b, k) in terms:
        in_specs += [pl.BlockSpec((tm, k), lambda i: (i, 0)), pl.BlockSpec((k, D_MODEL), lambda i: (0, 0))]
        args += [a, b]
    in_specs += [pl.BlockSpec((tm, D_MODEL), lambda i: (i, 0)), pl.BlockSpec((1, D_MODEL), lambda i: (0, 0)),
                 pl.BlockSpec((tm, D_MODEL), lambda i: (i, 0))] + [pl.BlockSpec(memory_space=pl.ANY)] * len(behind)
    args += [h, w, dres, *behind]

    def body(*refs):
        h_ref, w_ref, dres_ref = refs[2 * n_terms:2 * n_terms + 3]
        dh_ref, dhb_ref, dw_ref = refs[2 * n_terms + 3 + len(behind):]

        @pl.when(pl.program_id(0) == 0)
        def _():
            dw_ref[...] = jnp.zeros_like(dw_ref)

        du = None
        for t in range(n_terms):
            d = lax.dot_general(refs[2 * t][...], refs[2 * t + 1][...], NN, preferred_element_type=F32)
            du = d if du is None else du + d
        dx, dwr = _rms_bwd(h_ref[...], w_ref[...], du)
        dh = dres_ref[...] + dx
        dh_ref[...] = dh
        dhb_ref[...] = dh.astype(_MXU)
        dw_ref[...] += jnp.sum(dwr, axis=0, keepdims=True)

    return pl.pallas_call(
        body, name=name, grid=(T_ROWS // tm,), in_specs=in_specs,
        out_specs=[pl.BlockSpec((tm, D_MODEL), lambda i: (i, 0)), pl.BlockSpec((tm, D_MODEL), lambda i: (i, 0)),
                   pl.BlockSpec((1, D_MODEL), lambda i: (0, 0))],
        out_shape=[jax.ShapeDtypeStruct((T_ROWS, D_MODEL), F32), jax.ShapeDtypeStruct((T_ROWS, D_MODEL), _MXU),
                   jax.ShapeDtypeStruct((1, D_MODEL), F32)],
        compiler_params=_cparams("arbitrary"),
    )(*args)


FFN_TM = T_ROWS
FFN_TN = 256


def _ffn_up(u2, wg_t, wu_t):
    def body(u_ref, wg_ref, wu_ref, gp_ref, up_ref, act_ref):
        u = u_ref[...]
        gp = lax.dot_general(u, wg_ref[...], NT, preferred_element_type=F32)
        up = lax.dot_general(u, wu_ref[...], NT, preferred_element_type=F32)
        gp_ref[...] = gp.astype(_MXU)
        up_ref[...] = up.astype(_MXU)
        act_ref[...] = (_silu(gp) * up).astype(_MXU)

    tile = pl.BlockSpec((FFN_TM, FFN_TN), lambda j, i: (i, j))
    return pl.pallas_call(
        body, name="ffn_up", grid=(D_FF // FFN_TN, T_ROWS // FFN_TM),
        in_specs=[pl.BlockSpec((FFN_TM, D_MODEL), lambda j, i: (i, 0)),
                  pl.BlockSpec((FFN_TN, D_MODEL), lambda j, i: (j, 0)),
                  pl.BlockSpec((FFN_TN, D_MODEL), lambda j, i: (j, 0))],
        out_specs=[tile, tile, tile],
        out_shape=[jax.ShapeDtypeStruct((T_ROWS, D_FF), _MXU)] * 3,
        compiler_params=_cparams("parallel", "parallel"),
    )(u2, wg_t, wu_t)


def _ffn_bwd_act(dh2b, wd, gp, up):
    def body(dh_ref, wd_ref, gp_ref, up_ref, dgp_ref, dup_ref):
        dact = lax.dot_general(dh_ref[...], wd_ref[...], NT, preferred_element_type=F32)
        gp = gp_ref[...].astype(F32)
        dgp_ref[...] = (dact * up_ref[...].astype(F32) * _silu_grad(gp)).astype(_MXU)
        dup_ref[...] = (dact * _silu(gp)).astype(_MXU)

    tile = pl.BlockSpec((FFN_TM, FFN_TN), lambda j, i: (i, j))
    return pl.pallas_call(
        body, name="ffn_bwd_act", grid=(D_FF // FFN_TN, T_ROWS // FFN_TM),
        in_specs=[pl.BlockSpec((FFN_TM, D_MODEL), lambda j, i: (i, 0)),
                  pl.BlockSpec((FFN_TN, D_MODEL), lambda j, i: (j, 0)), tile, tile],
        out_specs=[tile, tile],
        out_shape=[jax.ShapeDtypeStruct((T_ROWS, D_FF), _MXU), jax.ShapeDtypeStruct((T_ROWS, D_FF), _MXU)],
        compiler_params=_cparams("parallel", "parallel"),
    )(dh2b, wd, gp, up)


CONV_TC = 512
CONV_K = 4


def _conv_pre(x_ref, wv, bv, c):
    tc = wv.shape[1]
    r0 = c * CHUNK
    cur = x_ref[r0:r0 + CHUNK, :]
    if c == 0:
        cat = jnp.concatenate([jnp.zeros((8, tc), F32), cur], axis=0)
        shifted = [cur] + [pltpu.roll(cat, s, 0)[8:8 + CHUNK] for s in range(1, CONV_K)]
    else:
        shifted = [cur] + [x_ref[r0 - s:r0 - s + CHUNK, :] for s in range(1, CONV_K)]
    pre = bv
    for s in range(CONV_K):
        pre = pre + shifted[s] * wv[CONV_K - 1 - s:CONV_K - s]
    return pre, shifted


def _row_mask(c):
    if c > 0:
        return None
    return (lax.broadcasted_iota(jnp.int32, (CHUNK, 1), 0) >= PAD_ROWS).astype(F32)


def _conv_fwd(x, w, b, *, silu, name):
    cols = x.shape[1]
    tc = min(CONV_TC, cols)

    def body(x_ref, w_ref, b_ref, o_ref):
        wv, bv = w_ref[...], b_ref[...]
        for c in range(N_CHUNKS):
            pre, _ = _conv_pre(x_ref, wv, bv, c)
            y = _silu(pre) if silu else pre
            mask = _row_mask(c)
            if mask is not None:
                y = y * mask
            o_ref[c * CHUNK:(c + 1) * CHUNK, :] = y

    return pl.pallas_call(
        body, name=name, grid=(cols // tc,),
        in_specs=[pl.BlockSpec((T_ROWS, tc), lambda j: (0, j)), pl.BlockSpec((CONV_K, tc), lambda j: (0, j)),
                  pl.BlockSpec((1, tc), lambda j: (0, j))],
        out_specs=pl.BlockSpec((T_ROWS, tc), lambda j: (0, j)),
        out_shape=jax.ShapeDtypeStruct((T_ROWS, cols), F32),
        compiler_params=_cparams("parallel"),
    )(x, w, b)


def _conv_bwd(dy, x, w, b, *, silu, name):
    cols = x.shape[1]
    tc = min(CONV_TC, cols)

    def body(dy_ref, x_ref, w_ref, b_ref, dx_ref, dw_ref, db_ref):
        wv, bv = w_ref[...], b_ref[...]
        next8 = jnp.zeros((8, tc), F32)
        dws = [jnp.zeros((1, tc), F32) for _ in range(CONV_K)]
        db = jnp.zeros((1, tc), F32)
        for c in reversed(range(N_CHUNKS)):
            r0 = c * CHUNK
            pre, shifted = _conv_pre(x_ref, wv, bv, c)
            dpre = dy_ref[r0:r0 + CHUNK, :]
            if silu:
                dpre = dpre * _silu_grad(pre)
            mask = _row_mask(c)
            if mask is not None:
                dpre = dpre * mask
            cat = jnp.concatenate([dpre, next8], axis=0)
            dx = dpre * wv[CONV_K - 1:CONV_K]
            for s in range(1, CONV_K):
                dx = dx + pltpu.roll(cat, CHUNK + 8 - s, 0)[0:CHUNK] * wv[CONV_K - 1 - s:CONV_K - s]
            dx_ref[r0:r0 + CHUNK, :] = dx.astype(_MXU)
            for s in range(CONV_K):
                k = CONV_K - 1 - s
                dws[k] = dws[k] + jnp.sum(dpre * shifted[s], axis=0, keepdims=True)
            db = db + jnp.sum(dpre, axis=0, keepdims=True)
            next8 = dpre[0:8]
        dw_ref[...] = jnp.concatenate(dws, axis=0)
        db_ref[...] = db

    return pl.pallas_call(
        body, name=name, grid=(cols // tc,),
        in_specs=[pl.BlockSpec((T_ROWS, tc), lambda j: (0, j)), pl.BlockSpec((T_ROWS, tc), lambda j: (0, j)),
                  pl.BlockSpec((CONV_K, tc), lambda j: (0, j)), pl.BlockSpec((1, tc), lambda j: (0, j))],
        out_specs=[pl.BlockSpec((T_ROWS, tc), lambda j: (0, j)), pl.BlockSpec((CONV_K, tc), lambda j: (0, j)),
                   pl.BlockSpec((1, tc), lambda j: (0, j))],
        out_shape=[jax.ShapeDtypeStruct((T_ROWS, cols), _MXU), jax.ShapeDtypeStruct((CONV_K, cols), F32),
                   jax.ShapeDtypeStruct((1, cols), F32)],
        compiler_params=_cparams("parallel"),
    )(dy, x, w, b)


def _ssd_chunk_common(dt_raw, prm, c):
    a_row = -jnp.exp(prm[1:2])
    dt = _softplus(dt_raw + prm[0:1])
    rows = lax.broadcasted_iota(jnp.int32, (CHUNK, 1), 0)
    real = jnp.logical_or(c > 0, rows >= PAD_ROWS)
    dt = jnp.where(real, dt, 0.0)
    li = lax.broadcasted_iota(jnp.int32, (CHUNK, CHUNK), 0)
    si = lax.broadcasted_iota(jnp.int32, (CHUNK, CHUNK), 1)
    causal = li >= si
    tri = causal.astype(F32)
    cs = _dot_onehot(tri, dt * a_row, data=1)
    return dt, a_row, cs, cs.T, causal, tri, real


def _gated_norm_fwd(y, z, w):
    g = y * _silu(z)
    half = SSD_WIDTH // SSD_GROUPS
    outs = [_rms_fwd(g[:, k * half:(k + 1) * half], w[:, k * half:(k + 1) * half]) for k in range(SSD_GROUPS)]
    return jnp.concatenate(outs, axis=1)


GROUP_W = SSD_WIDTH // SSD_GROUPS
PAIR_W = 2 * SSD_HEAD_DIM
STATE_SHAPE = (SSD_GROUPS, SSD_STATE, GROUP_W)


def _head_expander():
    r = lax.broadcasted_iota(jnp.int32, (128, SSD_WIDTH), 0)
    c = lax.broadcasted_iota(jnp.int32, (128, SSD_WIDTH), 1)
    return (c // SSD_HEAD_DIM == r).astype(F32)


def _ssd_expand(dt, cs, prm, ex):
    cs_x = _dot_onehot(cs, ex)
    cs_last_x = cs_x[CHUNK - 1:CHUNK, :]
    return (_dot_onehot(dt, ex, pieces=2), _dot_onehot(prm, ex)[2:3], jnp.exp(cs_x), jnp.exp(cs_last_x),
            jnp.exp(cs_last_x - cs_x))


def _ssd_fwd(xs, bc, dt_raw, z, prm, norm_w, ex):
    def body(xs_ref, bc_ref, dt_ref, z_ref, prm_ref, nw_ref, ex_ref, y_ref, yn_ref, prev_ref, state):
        c = pl.program_id(0)

        @pl.when(c == 0)
        def _():
            state[...] = jnp.zeros_like(state)

        prm = prm_ref[...]
        dt, a_row, cs, cs_t, causal, _, _ = _ssd_chunk_common(dt_ref[...], prm, c)
        dt_x, d_x, e_cs_x, e_last_x, dec_x = _ssd_expand(dt, cs, prm, ex_ref[...])
        xs_all = xs_ref[...]
        bc_all = bc_ref[...]
        xdt = xs_all * dt_x
        xdec = xdt * dec_x
        lane_lo = lax.broadcasted_iota(jnp.int32, (1, PAIR_W), 1) < SSD_HEAD_DIM
        for g in range(SSD_GROUPS):
            gs = slice(g * GROUP_W, (g + 1) * GROUP_W)
            b_g = bc_all[:, g * SSD_STATE:(g + 1) * SSD_STATE]
            c_g = bc_all[:, (SSD_GROUPS + g) * SSD_STATE:(SSD_GROUPS + g + 1) * SSD_STATE]
            st = state[g]
            prev_ref[0, g] = st
            y_off = _dot(c_g, st) * e_cs_x[:, gs]
            state[g] = st * e_last_x[:, gs] + _dot(b_g.T, xdec[:, gs])
            cb = _dot(c_g, b_g, NT)
            for k in range(SSD_HPG // 2):
                h0 = g * SSD_HPG + 2 * k
                ps = slice(h0 * SSD_HEAD_DIM, h0 * SSD_HEAD_DIM + PAIR_W)
                xdt_pair = xdt[:, ps]
                yd = []
                for h in (h0, h0 + 1):
                    lmat = jnp.where(causal, jnp.exp(cs[:, h:h + 1] - cs_t[h:h + 1, :]), 0.0)
                    yd.append(_dot(cb * lmat, xdt_pair))
                y_ref[:, ps] = (jnp.where(lane_lo, yd[0], yd[1]) + y_off[:, k * PAIR_W:(k + 1) * PAIR_W]
                                + xs_all[:, ps] * d_x[:, ps])
        yn_ref[...] = _gated_norm_fwd(y_ref[...], z_ref[...], nw_ref[...]).astype(_MXU)

    row = lambda w: pl.BlockSpec((CHUNK, w), lambda c: (c, 0))
    return pl.pallas_call(
        body, name="ssd_fwd", grid=(N_CHUNKS,),
        in_specs=[row(SSD_WIDTH), row(512), row(128), row(SSD_WIDTH),
                  pl.BlockSpec((8, 128), lambda c: (0, 0)), pl.BlockSpec((1, SSD_WIDTH), lambda c: (0, 0)),
                  pl.BlockSpec((128, SSD_WIDTH), lambda c: (0, 0))],
        out_specs=[row(SSD_WIDTH), row(SSD_WIDTH),
                   pl.BlockSpec((1,) + STATE_SHAPE, lambda c: (c, 0, 0, 0))],
        out_shape=[jax.ShapeDtypeStruct((T_ROWS, SSD_WIDTH), F32), jax.ShapeDtypeStruct((T_ROWS, SSD_WIDTH), _MXU),
                   jax.ShapeDtypeStruct((N_CHUNKS,) + STATE_SHAPE, F32)],
        scratch_shapes=[pltpu.VMEM(STATE_SHAPE, F32)],
        compiler_params=_cparams("arbitrary"),
    )(xs, bc, dt_raw, z, prm, norm_w, ex)


def _ssd_bwd(dyn, dyn_block, z, y_pre, xs, bc, dt_raw, prev, prm, norm_w, ex):
    def body(dyn_ref, z_ref, y_ref, xs_ref, bc_ref, dt_ref, prev_ref, prm_ref, nw_ref, ex_ref,
             dz_ref, dxs_ref, dbc_ref, ddt_ref, dprm_ref, dnw_ref, dstate):
        step = pl.program_id(0)
        c = N_CHUNKS - 1 - step

        @pl.when(step == 0)
        def _():
            dstate[...] = jnp.zeros_like(dstate)
            dprm_ref[...] = jnp.zeros_like(dprm_ref)
            dnw_ref[...] = jnp.zeros_like(dnw_ref)

        prm = prm_ref[...]
        dt, a_row, cs, cs_t, causal, tri, real = _ssd_chunk_common(dt_ref[...], prm, c)
        realf = real.astype(F32)
        z = z_ref[...]
        y_all = y_ref[...]
        nw = nw_ref[...]
        dyn_all = dyn_ref[...]
        sz = _silu(z)
        gated = y_all * sz
        half = SSD_WIDTH // SSD_GROUPS
        dgs, dnws = [], []
        for k in range(SSD_GROUPS):
            sl = slice(k * half, (k + 1) * half)
            dgk, dwk = _rms_bwd(gated[:, sl], nw[:, sl], dyn_all[:, sl])
            dgs.append(dgk)
            dnws.append(jnp.sum(dwk, axis=0, keepdims=True))
        dgated = jnp.concatenate(dgs, axis=1)
        dnw_ref[...] += jnp.concatenate(dnws, axis=1)
        dz_ref[...] = (dgated * y_all * _silu_grad(z)).astype(_MXU)
        dy_all = dgated * sz

        ex = ex_ref[...]
        dt_x, d_x, e_cs_x, e_last_x, dec_x = _ssd_expand(dt, cs, prm, ex)
        xs_all = xs_ref[...]
        bc_all = bc_ref[...]
        xdt = xs_all * dt_x
        xdt_mxu = xdt.astype(_MXU).astype(F32)
        xdec = xdt * dec_x
        dcp = dy_all * e_cs_x
        lane_lo = lax.broadcasted_iota(jnp.int32, (1, PAIR_W), 1) < SSD_HEAD_DIM
        upper = (lax.broadcasted_iota(jnp.int32, (CHUNK, CHUNK), 0)
                 <= lax.broadcasted_iota(jnp.int32, (CHUNK, CHUNK), 1))
        last_row = (lax.broadcasted_iota(jnp.int32, (CHUNK, 1), 0) == CHUNK - 1).astype(F32)
        dbs, dcs_, dxdt_parts, last_parts = [], [], [], []
        for g in range(SSD_GROUPS):
            gs = slice(g * GROUP_W, (g + 1) * GROUP_W)
            b_g = bc_all[:, g * SSD_STATE:(g + 1) * SSD_STATE]
            c_g = bc_all[:, (SSD_GROUPS + g) * SSD_STATE:(SSD_GROUPS + g + 1) * SSD_STATE]
            prev_t = prev_ref[0, g]
            dst = dstate[g]
            dc_g = _dot(dcp[:, gs], prev_t, NT)
            db_g = _dot(xdec[:, gs], dst, NT)
            dxdt_state = _dot(b_g, dst) * dec_x[:, gs]
            dstate[g] = dst * e_last_x[:, gs] + _dot(c_g.T, dcp[:, gs])
            last_parts.append(jnp.sum(xdt_mxu[:, gs] * dxdt_state, axis=0, keepdims=True)
                              + jnp.sum(dst * prev_t, axis=0, keepdims=True) * e_last_x[:, gs])
            cb_t = _dot(b_g, c_g, NT)
            dcb_t = jnp.zeros((CHUNK, CHUNK), F32)
            for k in range(SSD_HPG // 2):
                h0 = g * SSD_HPG + 2 * k
                ps = slice(h0 * SSD_HEAD_DIM, h0 * SSD_HEAD_DIM + PAIR_W)
                dy_pair = dy_all[:, ps]
                xdt_pair = xdt[:, ps]
                dd = []
                for h in (h0, h0 + 1):
                    lmat_t = jnp.where(upper, jnp.exp(cs_t[h:h + 1, :] - cs[:, h:h + 1]), 0.0)
                    dd.append(_dot(cb_t * lmat_t, dy_pair))
                    mine = lane_lo if h == h0 else jnp.logical_not(lane_lo)
                    dcb_t = dcb_t + _dot(jnp.where(mine, xdt_pair, 0.0), dy_pair, NT) * lmat_t
                dxdt_parts.append(jnp.where(lane_lo, dd[0], dd[1]) + dxdt_state[:, k * PAIR_W:(k + 1) * PAIR_W])
            dc_g = dc_g + _dot(dcb_t, b_g, TN)
            db_g = db_g + _dot(dcb_t, c_g)
            dbs.append(db_g * realf)
            dcs_.append(dc_g * realf)
        dbc_ref[...] = jnp.concatenate(dbs + dcs_, axis=1)
        dxdt = jnp.concatenate(dxdt_parts, axis=1)
        dxs_ref[...] = (dxdt * dt_x + dy_all * d_x) * realf
        ddt_all = _dot_onehot(dxdt * xs_all, ex, NT, pieces=2)
        rows = jnp.concatenate([jnp.concatenate(last_parts, axis=1), jnp.sum(dy_all * xs_all, axis=0, keepdims=True),
                                jnp.zeros((6, SSD_WIDTH), F32)], axis=0)
        rows = _dot_onehot(rows, ex, NT, pieces=2)
        dd_row = rows[1:2]
        dy_mxu = dy_all.astype(_MXU).astype(F32)
        dcs_all = (_dot_onehot(dy_mxu * (y_all - xs_all * d_x), ex, NT) - _dot_onehot(xdt_mxu * dxdt, ex, NT)
                   + last_row * rows[0:1])
        dda = _dot_onehot(tri, dcs_all, TN, data=1)
        ddt = (ddt_all + dda * a_row) * realf
        ddt_raw = ddt * _sigmoid(dt_ref[...] + prm[0:1])
        ddt_ref[...] = ddt_raw.astype(_MXU)
        da_log = jnp.sum(dda * dt, axis=0, keepdims=True) * a_row
        dprm_ref[0:1, :] += jnp.sum(ddt_raw, axis=0, keepdims=True)
        dprm_ref[1:2, :] += da_log
        dprm_ref[2:3, :] += dd_row

    rev = lambda w, blk=0: pl.BlockSpec((CHUNK, w), lambda s, blk=blk: (N_CHUNKS - 1 - s, blk))
    return pl.pallas_call(
        body, name="ssd_bwd", grid=(N_CHUNKS,),
        in_specs=[rev(SSD_WIDTH, dyn_block), rev(SSD_WIDTH), rev(SSD_WIDTH), rev(SSD_WIDTH), rev(512), rev(128),
                  pl.BlockSpec((1,) + STATE_SHAPE, lambda s: (N_CHUNKS - 1 - s, 0, 0, 0)),
                  pl.BlockSpec((8, 128), lambda s: (0, 0)), pl.BlockSpec((1, SSD_WIDTH), lambda s: (0, 0)),
                  pl.BlockSpec((128, SSD_WIDTH), lambda s: (0, 0))],
        out_specs=[rev(SSD_WIDTH), rev(SSD_WIDTH), rev(512), rev(128),
                   pl.BlockSpec((8, 128), lambda s: (0, 0)), pl.BlockSpec((1, SSD_WIDTH), lambda s: (0, 0))],
        out_shape=[jax.ShapeDtypeStruct((T_ROWS, SSD_WIDTH), _MXU), jax.ShapeDtypeStruct((T_ROWS, SSD_WIDTH), F32),
                   jax.ShapeDtypeStruct((T_ROWS, 512), F32), jax.ShapeDtypeStruct((T_ROWS, 128), _MXU),
                   jax.ShapeDtypeStruct((8, 128), F32), jax.ShapeDtypeStruct((1, SSD_WIDTH), F32)],
        scratch_shapes=[pltpu.VMEM(STATE_SHAPE, F32)],
        compiler_params=_cparams("arbitrary"),
    )(dyn, z, y_pre, xs, bc, dt_raw, prev, prm, norm_w, ex)


LRU_PAIRS = 8


def _lru_gates(xr, wa_ref, wx_ref, prm):
    pre_r, pre_i = [], []
    for k in range(LRU_PAIRS):
        xk = xr[:, k * 128:(k + 1) * 128]
        pre_r.append(_dot(xk, wa_ref[k]))
        pre_i.append(_dot(xk, wx_ref[k]))
    r = _sigmoid(jnp.concatenate(pre_r, axis=1) + prm[0:1])
    i = _sigmoid(jnp.concatenate(pre_i, axis=1) + prm[1:2])
    sp = _softplus(-prm[2:3])
    log_a = (-LRU_C) * r * sp
    a = jnp.exp(log_a)
    s = jnp.sqrt(-jnp.tanh(log_a) * (a * a + 1.0))
    return r, i, a, s, sp


def _lru_fwd(xr, gate, wa, wx, prm):
    def body(xr_ref, g_ref, wa_ref, wx_ref, prm_ref, hs_ref, yn_ref, carry, a_s, u_s):
        @pl.when(pl.program_id(0) == 0)
        def _():
            carry[...] = jnp.zeros_like(carry)

        prm = prm_ref[...]
        xr_t = xr_ref[...]
        _, i, a, s, _ = _lru_gates(xr_t, wa_ref, wx_ref, prm)
        a_s[...] = a
        u_s[...] = s * (i * xr_t)
        rid = lax.broadcasted_iota(jnp.int32, (8, LRU_WIDTH), 0)

        def group(k, before):
            off = pl.multiple_of(k * 8, 8)
            a8 = a_s[pl.ds(off, 8), :]
            u8 = u_s[pl.ds(off, 8), :]
            for d in (1, 2, 4):
                keep = rid >= d
                u8 = u8 + a8 * jnp.where(keep, pltpu.roll(u8, d, 0), 0.0)
                a8 = a8 * jnp.where(keep, pltpu.roll(a8, d, 0), 1.0)
            h8 = u8 + a8 * before
            hs_ref[pl.ds(off, 8), :] = h8
            return jnp.broadcast_to(h8[7:8], (8, LRU_WIDTH))

        carry[...] = lax.fori_loop(0, CHUNK // 8, group, carry[...])
        gel, _ = _gelu_and_grad(g_ref[...])
        yn_ref[...] = _rms_fwd(gel * hs_ref[...], prm[3:4]).astype(_MXU)

    row = pl.BlockSpec((CHUNK, LRU_WIDTH), lambda t: (t, 0))
    wspec = pl.BlockSpec((LRU_PAIRS, 128, 128), lambda t: (0, 0, 0))
    return pl.pallas_call(
        body, name="lru_fwd", grid=(N_CHUNKS,),
        in_specs=[row, row, wspec, wspec, pl.BlockSpec((8, LRU_WIDTH), lambda t: (0, 0))],
        out_specs=[row, row],
        out_shape=[jax.ShapeDtypeStruct((T_ROWS, LRU_WIDTH), F32), jax.ShapeDtypeStruct((T_ROWS, LRU_WIDTH), _MXU)],
        scratch_shapes=[pltpu.VMEM((8, LRU_WIDTH), F32), pltpu.VMEM((CHUNK, LRU_WIDTH), F32),
                        pltpu.VMEM((CHUNK, LRU_WIDTH), F32)],
        compiler_params=_cparams("arbitrary"),
    )(xr, gate, wa, wx, prm)


def _lru_bwd(dyn, dyn_block, gate, xr, hs, wa, wx, wa_t, wx_t, prm):
    def body(dyn_ref, g_ref, xr_ref, hs_ref, hsp_ref, wa_ref, wx_ref, wat_ref, wxt_ref, prm_ref,
             dg_ref, dxr_ref, dwa_ref, dwx_ref, dprm_ref, carry, a_s, d_s):
        step = pl.program_id(0)
        tile = N_CHUNKS - 1 - step

        @pl.when(step == 0)
        def _():
            carry[...] = jnp.zeros_like(carry)
            dwa_ref[...] = jnp.zeros_like(dwa_ref)
            dwx_ref[...] = jnp.zeros_like(dwx_ref)
            dprm_ref[...] = jnp.zeros_like(dprm_ref)

        prm = prm_ref[...]
        xr_t = xr_ref[...]
        r, i, a, s, sp = _lru_gates(xr_t, wa_ref, wx_ref, prm)
        hs_t = hs_ref[...]
        gel, dgel = _gelu_and_grad(g_ref[...])
        dy, dnw = _rms_bwd(gel * hs_t, prm[3:4], dyn_ref[...])
        dg_ref[...] = (dy * hs_t * dgel).astype(_MXU)
        a_s[...] = a
        d_s[...] = dy * gel
        rid = lax.broadcasted_iota(jnp.int32, (8, LRU_WIDTH), 0)

        def group(k, behind):
            off = pl.multiple_of((CHUNK // 8 - 1 - k) * 8, 8)
            a8 = a_s[pl.ds(off, 8), :]
            d8 = d_s[pl.ds(off, 8), :]
            c8 = jnp.where(rid == 7, 1.0, pltpu.roll(a8, 7, 0))
            for d in (1, 2, 4):
                keep = rid < 8 - d
                d8 = d8 + c8 * jnp.where(keep, pltpu.roll(d8, 8 - d, 0), 0.0)
                c8 = c8 * jnp.where(keep, pltpu.roll(c8, 8 - d, 0), 1.0)
            dht8 = d8 + c8 * behind
            d_s[pl.ds(off, 8), :] = dht8
            return jnp.broadcast_to(a8[0:1] * dht8[0:1], (8, LRU_WIDTH))

        carry[...] = lax.fori_loop(0, CHUNK // 8, group, carry[...])
        dht = d_s[...]
        before = hsp_ref[CHUNK - 8:CHUNK, :][7:8] * (tile > 0).astype(F32)
        first = lax.broadcasted_iota(jnp.int32, (CHUNK, 1), 0) == 0
        hprev = jnp.where(first, before, pltpu.roll(hs_t, 1, 0))
        da = dht * hprev
        ixr = i * xr_t
        ds = dht * ixr
        dlog_a = da * a - ds * (a * a) * lax.rsqrt(s * s)
        dr = dlog_a * ((-LRU_C) * sp)
        dsp = jnp.sum(dlog_a * ((-LRU_C) * r), axis=0, keepdims=True)
        dlam = dsp * (-_sigmoid(-prm[2:3]))
        di = dht * s * xr_t
        dpre_r = dr * r * (1.0 - r)
        dpre_i = di * i * (1.0 - i)
        dxr = dht * s * i
        parts = []
        for k in range(LRU_PAIRS):
            sl = slice(k * 128, (k + 1) * 128)
            parts.append(_dot(dpre_r[:, sl], wat_ref[k]) + _dot(dpre_i[:, sl], wxt_ref[k]))
            dwa_ref[k] += _dot(xr_t[:, sl], dpre_r[:, sl], TN)
            dwx_ref[k] += _dot(xr_t[:, sl], dpre_i[:, sl], TN)
        dxr_ref[...] = dxr + jnp.concatenate(parts, axis=1)
        dprm_ref[0:1, :] += jnp.sum(dpre_r, axis=0, keepdims=True)
        dprm_ref[1:2, :] += jnp.sum(dpre_i, axis=0, keepdims=True)
        dprm_ref[2:3, :] += dlam
        dprm_ref[3:4, :] += jnp.sum(dnw, axis=0, keepdims=True)

    rev = lambda blk=0: pl.BlockSpec((CHUNK, LRU_WIDTH), lambda s, blk=blk: (N_CHUNKS - 1 - s, blk))
    wspec = pl.BlockSpec((LRU_PAIRS, 128, 128), lambda s: (0, 0, 0))
    return pl.pallas_call(
        body, name="lru_bwd", grid=(N_CHUNKS,),
        in_specs=[rev(dyn_block), rev(), rev(), rev(),
                  pl.BlockSpec((CHUNK, LRU_WIDTH), lambda s: (jnp.maximum(N_CHUNKS - 2 - s, 0), 0)),
                  wspec, wspec, wspec, wspec, pl.BlockSpec((8, LRU_WIDTH), lambda s: (0, 0))],
        out_specs=[rev(), rev(), wspec, wspec, pl.BlockSpec((8, LRU_WIDTH), lambda s: (0, 0))],
        out_shape=[jax.ShapeDtypeStruct((T_ROWS, LRU_WIDTH), _MXU), jax.ShapeDtypeStruct((T_ROWS, LRU_WIDTH), F32),
                   jax.ShapeDtypeStruct((LRU_PAIRS, 128, 128), F32), jax.ShapeDtypeStruct((LRU_PAIRS, 128, 128), F32),
                   jax.ShapeDtypeStruct((8, LRU_WIDTH), F32)],
        scratch_shapes=[pltpu.VMEM((8, LRU_WIDTH), F32), pltpu.VMEM((CHUNK, LRU_WIDTH), F32),
                        pltpu.VMEM((CHUNK, LRU_WIDTH), F32)],
        compiler_params=_cparams("arbitrary"),
    )(dyn, gate, xr, hs, hs, wa, wx, wa_t, wx_t, prm)


SEC_NAMES = ("z", "xs", "bc", "dt", "g", "x")
SEC_WIDTH = {"z": 1024, "xs": 1024, "bc": 512, "dt": 128, "g": 1024, "x": 1024}


def _pair_blocks(w):
    w = w.reshape(LRU_PAIRS, 2, 64, 64)
    zero = jnp.zeros((LRU_PAIRS, 64, 64), w.dtype)
    top = jnp.concatenate([w[:, 0], zero], axis=2)
    bot = jnp.concatenate([zero, w[:, 1]], axis=2)
    return jnp.concatenate([top, bot], axis=1)


def _unpair_blocks(wp):
    return jnp.stack([wp[:, :64, :64], wp[:, 64:, 64:]], axis=1).reshape(16, 64, 64)


def _pad_lanes(v, width=128):
    return jnp.pad(v, ((0, 0), (0, width - v.shape[1])))


class _Resident:
    before_embed = ()

    def __init__(self, w_in_sections, w_out, w_gate, w_up, w_down):
        self._w_in, self._w_out, self._ffn = w_in_sections, w_out, (w_gate, w_up, w_down)

    def w_in(self, after):
        return self._w_in

    def mid_forward(self, after):
        return jnp.zeros((1, 1), F32)

    def w_out(self, after):
        return self._w_out

    def ffn(self, after):
        return self._ffn

    def grads_ready(self, names, g, g_mxu):
        return jnp.zeros((1, 1), F32)

    def small_ready(self, g, loss):
        return jnp.zeros((1, 1), F32)

    def small_middle(self, after):
        return jnp.zeros((1, 1), F32)


def _local_step(x, target, meta, p, late):
    g, g_mxu = {}, {}
    ex = _head_expander()
    h0 = _embed(x, meta, late.before_embed)
    w_in = late.w_in(h0)
    u1, projs = _norm_proj(h0, p["norm1_w"], [w_in[s] for s in SEC_NAMES], name="norm_in_proj")
    proj = dict(zip(SEC_NAMES, projs))
    ssd_prm = jnp.concatenate([_pad_lanes(p["ssd_dt_bias"]), _pad_lanes(p["ssd_a_log"]), _pad_lanes(p["ssd_d"]),
                               jnp.zeros((5, 128), F32)], axis=0)
    xs_act = _conv_fwd(proj["xs"], p["ssd_conv_w"][:, :SSD_WIDTH], p["ssd_conv_b"][:, :SSD_WIDTH], silu=True,
                       name="ssd_conv_xs")
    bc_act = _conv_fwd(proj["bc"], p["ssd_conv_w"][:, SSD_WIDTH:], p["ssd_conv_b"][:, SSD_WIDTH:], silu=True,
                       name="ssd_conv_bc")
    y_pre, y_ssd, prev = _ssd_fwd(xs_act, bc_act, proj["dt"], proj["z"], ssd_prm, p["ssd_norm_w"], ex)
    xr = _conv_fwd(proj["x"], p["lru_conv_w"], p["lru_conv_b"], silu=False, name="lru_conv")
    wa_p, wx_p = _pair_blocks(p["lru_wa"]), _pair_blocks(p["lru_wx"])
    lru_prm = jnp.concatenate([p["lru_ba"], p["lru_bx"], p["lru_lambda"], p["lru_norm_w"],
                               jnp.zeros((4, LRU_WIDTH), F32)], axis=0)
    hs, y_lru = _lru_fwd(xr, proj["g"], wa_p.astype(_MXU), wx_p.astype(_MXU),
                         lru_prm + late.mid_forward([xr, y_ssd]))
    ycat = jnp.concatenate([y_ssd, y_lru], axis=1)
    w_out = late.w_out(ycat)
    h1 = _mm([(ycat, 0, w_out, 0, 2 * D_MODEL)], T_ROWS, D_MODEL, tm=T_ROWS, tn=256, mode="nn", out_dtype=F32,
             name="out_proj", residual=h0)
    u2 = _rmsnorm(h1, p["norm2_w"], name="norm2")
    w_gate, w_up, w_down = late.ffn(u2)
    gp, up, act = _ffn_up(u2, w_gate, w_up)
    h2 = _mm([(act, 0, w_down, 0, D_FF)], T_ROWS, D_MODEL, tm=T_ROWS, tn=256, mode="nn", out_dtype=F32,
             name="ffn_down", residual=h1)
    loss, dh2, dh2b, g["final_norm_w"] = _loss_head(h2, target, p["final_norm_w"])
    dgp, dup = _ffn_bwd_act(dh2b, w_down, gp, up)
    g["w_down"], g_mxu["w_down"] = _mm([(act, 0, dh2b, 0, T_ROWS)], D_FF, D_MODEL, tm=1408, tn=512, mode="tn",
                                       out_dtype=F32, name="dw_down", also_mxu=True)
    dh1, dh1b, g["norm2_w"] = _mm_norm_bwd([(dgp, w_gate, D_FF), (dup, w_up, D_FF)], h1, p["norm2_w"], dh2,
                                           name="ffn_bwd_in")
    g["w_gate"], g_mxu["w_gate"] = _mm([(dgp, 0, u2, 0, T_ROWS)], D_FF, D_MODEL, tm=1408, tn=512, mode="tn",
                                       out_dtype=F32, name="dw_gate", also_mxu=True)
    g["w_up"], g_mxu["w_up"] = _mm([(dup, 0, u2, 0, T_ROWS)], D_FF, D_MODEL, tm=1408, tn=512, mode="tn",
                                   out_dtype=F32, name="dw_up", also_mxu=True)
    g["w_out"], g_mxu["w_out"] = _mm([(ycat, 0, dh1b, 0, T_ROWS)], 2 * D_MODEL, D_MODEL, tm=1024, tn=512, mode="tn",
                                     out_dtype=F32, name="dw_out", also_mxu=True)
    sent = late.grads_ready(("w_down", "w_gate", "w_up", "w_out"), g, g_mxu)
    dycat = _mm([(dh1b, 0, w_out, 0, D_MODEL)], T_ROWS, 2 * D_MODEL, tm=T_ROWS, tn=256, mode="nt", out_dtype=F32,
                name="out_proj_bwd", behind=(sent,))
    dgate, dxr, dwa_p, dwx_p, dlru_prm = _lru_bwd(dycat, 1, proj["g"], xr, hs, wa_p.astype(_MXU), wx_p.astype(_MXU),
                                                  jnp.swapaxes(wa_p, 1, 2).astype(_MXU),
                                                  jnp.swapaxes(wx_p, 1, 2).astype(_MXU), lru_prm)
    g["lru_wa"], g["lru_wx"] = _unpair_blocks(dwa_p), _unpair_blocks(dwx_p)
    g["lru_ba"], g["lru_bx"], g["lru_lambda"], g["lru_norm_w"] = (dlru_prm[k:k + 1] for k in range(4))
    dx_lru, g["lru_conv_w"], g["lru_conv_b"] = _conv_bwd(dxr, proj["x"], p["lru_conv_w"], p["lru_conv_b"], silu=False,
                                                         name="lru_conv_bwd")
    dz, dxs_act, dbc_act, ddt, dssd_prm, g["ssd_norm_w"] = _ssd_bwd(dycat, 0, proj["z"], y_pre, xs_act, bc_act,
                                                                    proj["dt"], prev, ssd_prm, p["ssd_norm_w"], ex)
    g["ssd_dt_bias"], g["ssd_a_log"], g["ssd_d"] = (dssd_prm[k:k + 1, :SSD_HEADS] for k in range(3))
    dxs, dcw_xs, dcb_xs = _conv_bwd(dxs_act, proj["xs"], p["ssd_conv_w"][:, :SSD_WIDTH],
                                    p["ssd_conv_b"][:, :SSD_WIDTH], silu=True, name="ssd_conv_xs_bwd")
    dbc, dcw_bc, dcb_bc = _conv_bwd(dbc_act, proj["bc"], p["ssd_conv_w"][:, SSD_WIDTH:],
                                    p["ssd_conv_b"][:, SSD_WIDTH:], silu=True, name="ssd_conv_bc_bwd")
    g["ssd_conv_w"] = jnp.concatenate([dcw_xs, dcw_bc], axis=1)
    g["ssd_conv_b"] = jnp.concatenate([dcb_xs, dcb_bc], axis=1)
    dproj = {"z": dz, "xs": dxs, "bc": dbc, "dt": ddt, "g": dgate, "x": dx_lru}
    dh0, _, g["norm1_w"] = _mm_norm_bwd([(dproj[s], w_in[s], SEC_WIDTH[s]) for s in SEC_NAMES], h0,
                                        p["norm1_w"], dh1, name="in_proj_bwd")
    g["meta_tokens"] = dh0[PAD_ROWS:X_ROW0]
    sent = late.small_ready(g, loss)
    for s in SEC_NAMES:
        wdt = SEC_WIDTH[s]
        g["w_in_" + s], g_mxu["w_in_" + s] = _mm([(dproj[s], 0, u1, 0, T_ROWS)], wdt, D_MODEL, tm=min(wdt, 1024),
                                                 tn=512, mode="tn", out_dtype=F32, name="dw_in_" + s, also_mxu=True,
                                                 behind=(sent,))
        if s == "bc":
            sent = late.small_middle([g["w_in_z"], g["w_in_xs"], g["w_in_bc"]])
    late.grads_ready(("w_in",), g, g_mxu)
    return loss, dh0[X_ROW0:], g, g_mxu


MESH = pl.DeviceIdType.MESH
ANY = pl.BlockSpec(memory_space=pl.ANY)


def _my_place():
    return lax.axis_index("x"), lax.axis_index("y"), lax.axis_index("c")


def _other_chips(x, y):
    return [(1 - x, y), (x, 1 - y), (1 - x, 1 - y)]


HBM_SPEC = pl.BlockSpec(memory_space=pltpu.HBM)
SEM_SPEC = pl.BlockSpec(memory_space=pltpu.SEMAPHORE)
SPLIT_EFFECT = pltpu.SideEffectType.DATAFLOW_SIDE_EFFECTING


def _half_cols(buf, c, other=False):
    half = buf.shape[-1] // 2
    return pl.ds(pl.multiple_of(((1 - c) if other else c) * half, 128), half)


def _halves_plan(bufs, x, y, c, incoming):
    plan = []
    for buf in bufs:
        cols = _half_cols(buf, c)
        for (px, py) in _other_chips(x, y):
            slot = 2 * px + py if incoming else 2 * x + y
            plan.append((buf.at[2 * x + y, :, cols], buf.at[slot, :, cols], (px, py, c)))
    return plan


def _forward_plan(bufs, x, y, c, incoming):
    plan = []
    for buf in bufs:
        for (px, py) in _other_chips(x, y):
            slot = 2 * px + py
            plan.append((buf.at[slot, :, _half_cols(buf, c)], buf.at[slot, :, _half_cols(buf, c, other=incoming)],
                         (x, y, 1 - c)))
    return plan


def _scatter_plan(bufs, x, y, c, incoming):
    n = len(bufs) // 2
    plan = []
    for k in range(n):
        for j, (px, py) in enumerate(_other_chips(x, y)):
            plan.append((bufs[k].at[2 * px + py], bufs[n + k].at[j], (px, py, c)))
    return plan


def _split_start(bufs, plan, n_copies, after, *, name):
    n = len(bufs)
    extra = [] if after is None else [after]

    def body(*refs):
        ins = refs[:n]
        send_sems, recv_sems = refs[n + len(extra)], refs[n + len(extra) + 1]
        token = refs[-1]
        x, y, c = _my_place()
        for i, (src, dst, dev) in enumerate(plan(ins, x, y, c, False)):
            pltpu.make_async_remote_copy(src_ref=src, dst_ref=dst, send_sem=send_sems.at[i], recv_sem=recv_sems.at[i],
                                         device_id=dev, device_id_type=MESH).start()
        token[...] = jnp.zeros_like(token)

    outs = pl.pallas_call(
        body, name=name,
        out_shape=(pltpu.SemaphoreType.DMA((n_copies,)), pltpu.SemaphoreType.DMA((n_copies,)),
                   *[pltpu.HBM(b.shape, b.dtype) for b in bufs], jax.ShapeDtypeStruct((8, 128), F32)),
        in_specs=[HBM_SPEC] * n + [ANY] * len(extra),
        out_specs=(SEM_SPEC, SEM_SPEC, *[HBM_SPEC] * n, pl.BlockSpec(memory_space=pltpu.VMEM)),
        input_output_aliases={k: 2 + k for k in range(n)},
        compiler_params=pltpu.CompilerParams(has_side_effects=SPLIT_EFFECT),
    )(*[pltpu.with_memory_space_constraint(b, pltpu.HBM) for b in bufs], *extra)
    return outs[0], outs[1], list(outs[2:2 + n]), outs[-1]


def _split_wait(bufs, send_sems, recv_sems, plan, after, *, name):
    n = len(bufs)
    after = list(after) if isinstance(after, (list, tuple)) else [after]

    def body(*refs):
        ins = refs[:n]
        send_sems_ref, recv_sems_ref = refs[n], refs[n + 1]
        x, y, c = _my_place()
        for i, (src, dst, dev) in enumerate(plan(ins, x, y, c, True)):
            cp = pltpu.make_async_remote_copy(src_ref=src, dst_ref=dst, send_sem=send_sems_ref.at[i],
                                              recv_sem=recv_sems_ref.at[i], device_id=dev, device_id_type=MESH)
            cp.wait_send()
            cp.wait_recv()

    outs = pl.pallas_call(
        body, name=name, out_shape=tuple(pltpu.HBM(b.shape, b.dtype) for b in bufs),
        in_specs=[HBM_SPEC] * n + [SEM_SPEC, SEM_SPEC] + [ANY] * len(after), out_specs=tuple([HBM_SPEC] * n),
        input_output_aliases={k: k for k in range(n)},
        compiler_params=pltpu.CompilerParams(has_side_effects=SPLIT_EFFECT),
    )(*bufs, send_sems, recv_sems, *after)
    return list(outs)


def _fill_own_slots(shards, me_arr, *, name, behind=()):
    n = len(shards)
    n_in = n + len(behind)

    def body(me_ref, *refs):
        for k in range(n):
            refs[n_in + k][0] = refs[k][...].astype(_MXU)

    half = D_MODEL // 2
    return pl.pallas_call(
        body, name=name,
        grid_spec=pltpu.PrefetchScalarGridSpec(
            num_scalar_prefetch=1, grid=(2,),
            in_specs=[pl.BlockSpec((s.shape[0], half), lambda i, me: (0, i)) for s in shards]
            + [pl.BlockSpec(memory_space=pl.ANY)] * len(behind),
            out_specs=[pl.BlockSpec((1, s.shape[0], half), lambda i, me: (me[0], 0, i)) for s in shards]),
        out_shape=[jax.ShapeDtypeStruct((N_SHARDS,) + s.shape, _MXU) for s in shards],
        compiler_params=_cparams("parallel"),
    )(me_arr, *shards, *behind)


def _gather_small(small):
    def body(s_ref, o_ref, send_sems, recv_sems, local_sem):
        x, y, c = _my_place()
        me = 2 * x + y
        local = pltpu.make_async_copy(s_ref, o_ref.at[me], local_sem)
        local.start()
        copies = [(pltpu.make_async_remote_copy(src_ref=s_ref, dst_ref=o_ref.at[me], send_sem=send_sems.at[j],
                                                recv_sem=recv_sems.at[j], device_id=(px, py, c), device_id_type=MESH),
                   2 * px + py) for j, (px, py) in enumerate(_other_chips(x, y))]
        for cp, _ in copies:
            cp.start()
        for j, (cp, slot) in enumerate(copies):
            cp.wait_send()
            pltpu.make_async_remote_copy(src_ref=s_ref, dst_ref=o_ref.at[slot], send_sem=send_sems.at[j],
                                         recv_sem=recv_sems.at[j], device_id=(x, y, c),
                                         device_id_type=MESH).wait_recv()
        local.wait()

    return pl.pallas_call(
        body, name="gather_small", in_specs=[ANY], out_specs=ANY,
        out_shape=jax.ShapeDtypeStruct((N_SHARDS,) + small.shape, small.dtype),
        scratch_shapes=[pltpu.SemaphoreType.DMA((3,)), pltpu.SemaphoreType.DMA((3,)), pltpu.SemaphoreType.DMA],
    )(small)


def _swap_with_sibling(parts, *, name):
    n = len(parts)

    def body(*refs):
        ins, outs = refs[:n], refs[n:2 * n]
        send_sems, recv_sems = refs[2 * n:]
        x, y, c = _my_place()
        copies = [pltpu.make_async_remote_copy(
            src_ref=ins[k], dst_ref=outs[k], send_sem=send_sems.at[k], recv_sem=recv_sems.at[k],
            device_id=(x, y, 1 - c), device_id_type=MESH) for k in range(n)]
        for cp in copies:
            cp.start()
        for cp in copies:
            cp.wait()

    return pl.pallas_call(
        body, name=name, in_specs=[ANY] * n, out_specs=[ANY] * n,
        out_shape=[jax.ShapeDtypeStruct(a.shape, a.dtype) for a in parts],
        scratch_shapes=[pltpu.SemaphoreType.DMA((n,)), pltpu.SemaphoreType.DMA((n,))],
    )(*parts)


def _other_devices(x, y, c):
    out = []
    for mask in range(1, N_DEV):
        px, py, pc = x ^ (mask >> 2 & 1), y ^ (mask >> 1 & 1), c ^ (mask & 1)
        out.append(((px, py, pc), 4 * px + 2 * py + pc))
    return out


def _pieces_plan(bufs, x, y, c, incoming):
    pack, land = bufs
    me = 4 * x + 2 * y + c
    return [(pack.at[num], land.at[num if incoming else me], dev) for dev, num in _other_devices(x, y, c)]


def _spread_plan(bufs, x, y, c, incoming):
    piece, land = bufs
    me = 4 * x + 2 * y + c
    return [(piece, land.at[num if incoming else me], dev) for dev, num in _other_devices(x, y, c)]


def _sum_pieces(pack, land, dev_arr, *, name):
    def body(dev_ref, pack_ref, land_ref, o_ref):
        dev = dev_ref[0]
        own = pack_ref[dev]
        acc = None
        for d in range(N_DEV):
            term = jnp.where(dev == d, own, land_ref[d])
            acc = term if acc is None else acc + term
        o_ref[...] = acc

    vmem = pl.BlockSpec(memory_space=pltpu.VMEM)
    return pl.pallas_call(
        body, name=name, in_specs=[pl.BlockSpec(memory_space=pltpu.SMEM), vmem, vmem], out_specs=vmem,
        out_shape=jax.ShapeDtypeStruct(pack.shape[1:], F32),
    )(dev_arr, pack, land)


def _join_pieces(piece, land, dev_arr, *, name):
    def body(dev_ref, piece_ref, land_ref, o_ref):
        dev = dev_ref[0]
        for d in range(N_DEV):
            o_ref[d] = jnp.where(dev == d, piece_ref[...], land_ref[d])

    vmem = pl.BlockSpec(memory_space=pltpu.VMEM)
    return pl.pallas_call(
        body, name=name, in_specs=[pl.BlockSpec(memory_space=pltpu.SMEM), vmem, vmem], out_specs=vmem,
        out_shape=jax.ShapeDtypeStruct(land.shape, F32),
    )(dev_arr, piece, land)


def _adamw_native(ws, gs, ms, vs):
    n = len(ws)

    def body(*refs):
        for k in range(n):
            w_ref, g_ref, m_ref, v_ref = (refs[j * n + k] for j in range(4))
            delta, m_new, v_new = _adamw_math(w_ref[...], g_ref[...], m_ref[...], v_ref[...])
            refs[4 * n + k][...] = delta
            refs[5 * n + k][...] = m_new
            refs[6 * n + k][...] = v_new

    vmem = pl.BlockSpec(memory_space=pltpu.VMEM)
    shapes = [jax.ShapeDtypeStruct(a.shape, F32) for a in ws]
    outs = pl.pallas_call(
        body, name="adamw_small", in_specs=[vmem] * (4 * n), out_specs=[vmem] * (3 * n), out_shape=shapes * 3,
        compiler_params=pltpu.CompilerParams(vmem_limit_bytes=VMEM_LIMIT_BYTES),
    )(*ws, *gs, *ms, *vs)
    return outs[:n], outs[n:2 * n], outs[2 * n:]


def _elementwise_tile(rows, cols):
    for t in range(256, 15, -16):
        if rows % t == 0:
            return (t, cols), rows // t, lambda i: (i, 0)
    assert cols % 256 == 0
    return (rows, 256), cols // 256, lambda i: (0, i)


def _partial_sum(own, land, me_arr, *, name):
    r, c = own.shape[-2:]
    tile, steps, imap = _elementwise_tile(r, c)
    whole = own.ndim == 3

    def body(me_ref, own_ref, land_ref, o_ref):
        acc = own_ref[0] if whole else own_ref[...]
        for j in range(3):
            acc = acc + land_ref[j].astype(F32)
        o_ref[...] = acc.astype(_MXU)

    own_spec = (pl.BlockSpec((1,) + tile, lambda i, me: (me[0],) + imap(i)) if whole
                else pl.BlockSpec(tile, lambda i, me: imap(i)))
    return pl.pallas_call(
        body, name=name,
        grid_spec=pltpu.PrefetchScalarGridSpec(
            num_scalar_prefetch=1, grid=(steps,),
            in_specs=[own_spec, pl.BlockSpec((3,) + tile, lambda i, me: (0,) + imap(i))],
            out_specs=pl.BlockSpec(tile, lambda i, me: imap(i))),
        out_shape=jax.ShapeDtypeStruct((r, c), _MXU),
        compiler_params=_cparams("parallel"),
    )(me_arr, own, land)


def _adamw_math(w, g, m, v):
    m = ADAM_B1 * m + (1.0 - ADAM_B1) * g
    v = ADAM_B2 * v + (1.0 - ADAM_B2) * (g * g)
    m_hat = m / (1.0 - ADAM_B1 ** ADAM_STEP)
    v_hat = v / (1.0 - ADAM_B2 ** ADAM_STEP)
    delta = -ADAM_LR * (m_hat / (jnp.sqrt(v_hat) + ADAM_EPS) + ADAM_WD * w)
    return delta, m, v


def _adamw(w, grad_parts, m, v, *, name):
    if w.ndim == 3:
        steps = 4
        assert w.shape[0] % steps == 0
        tile_shape, imap = (w.shape[0] // steps,) + w.shape[1:], lambda i: (i, 0, 0)
    else:
        tile_shape, steps, imap = _elementwise_tile(*w.shape)
    n = len(grad_parts)

    def body(*refs):
        w_ref, m_ref, v_ref = refs[:3]
        g_refs = refs[3:3 + n]
        g_out, d_out, m_out, v_out = refs[3 + n:]
        g = g_refs[0][...].astype(F32)
        for k in range(1, n):
            g = g + g_refs[k][...].astype(F32)
        delta, m_new, v_new = _adamw_math(w_ref[...], g, m_ref[...], v_ref[...])
        g_out[...] = g
        d_out[...] = delta
        m_out[...] = m_new
        v_out[...] = v_new

    tile = pl.BlockSpec(tile_shape, imap)
    return pl.pallas_call(
        body, name=name, grid=(steps,), in_specs=[tile] * (3 + n), out_specs=[tile] * 4,
        out_shape=[jax.ShapeDtypeStruct(w.shape, F32)] * 4,
        compiler_params=_cparams("parallel"),
    )(w, m, v, *grad_parts)


WEIGHT_NAMES = ("meta_tokens", "norm1_w", "w_in", "ssd_conv_w", "ssd_conv_b", "ssd_dt_bias", "ssd_a_log", "ssd_d",
                "ssd_norm_w", "lru_conv_w", "lru_conv_b", "lru_wa", "lru_ba", "lru_wx", "lru_bx", "lru_lambda",
                "lru_norm_w", "w_out", "norm2_w", "w_gate", "w_up", "w_down", "final_norm_w")
BIG = ("w_in", "w_out", "w_gate", "w_up", "w_down")
FFN = ("w_gate", "w_up", "w_down")
LATE = ("w_out",) + FFN
SMALL_SHARDED = {"meta_tokens": (N_META, D_MODEL), "ssd_conv_w": (CONV_K, 1536), "lru_conv_w": (CONV_K, LRU_WIDTH)}
SMALL = tuple(n for n in WEIGHT_NAMES if n not in BIG)
PACK_COLS = 1024


def _pack(arrays, row_multiple):
    flat = jnp.concatenate([a.reshape(-1) for a in arrays])
    rows = -(-flat.shape[0] // (row_multiple * PACK_COLS)) * row_multiple
    return jnp.pad(flat, (0, rows * PACK_COLS - flat.shape[0])).reshape(rows, PACK_COLS)


def _unpack(pack, shapes):
    flat = pack.reshape(-1)
    out, off = [], 0
    for s in shapes:
        size = math.prod(s)
        out.append(flat[off:off + size].reshape(s))
        off += size
    return out


def _unshard_cols(g4):
    return jnp.swapaxes(g4, 0, 1).reshape(g4.shape[1], -1)


COL_SHARDED = ("w_in", "w_gate", "w_up")
IN_ROWS = {"z": (0, 1024), "xs": (1024, 2048), "bc": (2048, 2560), "dt": (2560, 2576), "g": (2576, 3600),
           "x": (3600, IN_COLS)}


def _rows_of_shards(shards4, lo, hi):
    r = shards4.shape[1]
    parts = [shards4[k, max(lo, k * r) - k * r:min(hi, (k + 1) * r) - k * r]
             for k in range(N_SHARDS) if max(lo, k * r) < min(hi, (k + 1) * r)]
    return parts[0] if len(parts) == 1 else jnp.concatenate(parts, axis=0)


def _w_in_shard_rows(k, sections):
    lo, hi = k * (IN_COLS // N_SHARDS), (k + 1) * (IN_COLS // N_SHARDS)
    parts = []
    for arr, (a, b) in zip(sections, IN_ROWS.values()):
        if max(lo, a) < min(hi, b):
            parts.append(arr[max(lo, a) - a:min(hi, b) - a])
    return jnp.concatenate(parts, axis=0)


def _rows_view(name, block):
    return jnp.swapaxes(block[0], 0, 1) if name in COL_SHARDED else block[0]


def _param_view(name, rows):
    return (jnp.swapaxes(rows, 0, 1) if name in COL_SHARDED else rows)[None]


def kernel(x, meta_tokens, norm1_w, w_in, ssd_conv_w, ssd_conv_b, ssd_dt_bias, ssd_a_log, ssd_d, ssd_norm_w, lru_conv_w, lru_conv_b, lru_wa, lru_ba, lru_wx, lru_bx, lru_lambda, lru_norm_w, w_out, norm2_w, w_gate, w_up, w_down, final_norm_w, loss_target, m_meta_tokens, m_norm1_w, m_w_in, m_ssd_conv_w, m_ssd_conv_b, m_ssd_dt_bias, m_ssd_a_log, m_ssd_d, m_ssd_norm_w, m_lru_conv_w, m_lru_conv_b, m_lru_wa, m_lru_ba, m_lru_wx, m_lru_bx, m_lru_lambda, m_lru_norm_w, m_w_out, m_norm2_w, m_w_gate, m_w_up, m_w_down, m_final_norm_w, v_meta_tokens, v_norm1_w, v_w_in, v_ssd_conv_w, v_ssd_conv_b, v_ssd_dt_bias, v_ssd_a_log, v_ssd_d, v_ssd_norm_w, v_lru_conv_w, v_lru_conv_b, v_lru_wa, v_lru_ba, v_lru_wx, v_lru_bx, v_lru_lambda, v_lru_norm_w, v_w_out, v_norm2_w, v_w_gate, v_w_up, v_w_down, v_final_norm_w):
    w = dict(zip(WEIGHT_NAMES, (meta_tokens, norm1_w, w_in, ssd_conv_w, ssd_conv_b, ssd_dt_bias, ssd_a_log, ssd_d, ssd_norm_w, lru_conv_w, lru_conv_b, lru_wa, lru_ba, lru_wx, lru_bx, lru_lambda, lru_norm_w, w_out, norm2_w, w_gate, w_up, w_down, final_norm_w)))
    m = dict(zip(WEIGHT_NAMES, (m_meta_tokens, m_norm1_w, m_w_in, m_ssd_conv_w, m_ssd_conv_b, m_ssd_dt_bias, m_ssd_a_log, m_ssd_d, m_ssd_norm_w, m_lru_conv_w, m_lru_conv_b, m_lru_wa, m_lru_ba, m_lru_wx, m_lru_bx, m_lru_lambda, m_lru_norm_w, m_w_out, m_norm2_w, m_w_gate, m_w_up, m_w_down, m_final_norm_w)))
    v = dict(zip(WEIGHT_NAMES, (v_meta_tokens, v_norm1_w, v_w_in, v_ssd_conv_w, v_ssd_conv_b, v_ssd_dt_bias, v_ssd_a_log, v_ssd_d, v_ssd_norm_w, v_lru_conv_w, v_lru_conv_b, v_lru_wa, v_lru_ba, v_lru_wx, v_lru_bx, v_lru_lambda, v_lru_norm_w, v_w_out, v_norm2_w, v_w_gate, v_w_up, v_w_down, v_final_norm_w)))
    me = 2 * lax.axis_index("x") + lax.axis_index("y")

    big2d = {n: _rows_view(n, w[n]) for n in BIG}
    small_local = jnp.concatenate([w["meta_tokens"].reshape(-1), w["ssd_conv_w"].reshape(-1),
                                   w["lru_conv_w"].reshape(-1)])[None]
    me_arr = me.astype(jnp.int32).reshape(1)
    dev_arr = (2 * me + lax.axis_index("c")).astype(jnp.int32).reshape(1)
    small4 = _gather_small(small_local)
    (w_in_slot,) = _fill_own_slots([big2d["w_in"]], me_arr, name="own_slot_w_in")
    in_send, in_recv, in_bufs, in_tok = _split_start([w_in_slot], _halves_plan, 3, small4, name="gather_w_in_start")
    late_slots = _fill_own_slots([big2d[n] for n in LATE], me_arr, name="own_slots_late", behind=(in_tok,))
    sm = small4[:, 0]
    meta_full = _unshard_cols(sm[:, :4096].reshape(N_SHARDS, N_META, 256))
    ssd_conv_w_full = _unshard_cols(sm[:, 4096:5632].reshape(N_SHARDS, CONV_K, 384))
    lru_conv_w_full = _unshard_cols(sm[:, 5632:].reshape(N_SHARDS, CONV_K, 256))

    p = {"ssd_conv_w": ssd_conv_w_full, "lru_conv_w": lru_conv_w_full,
         "lru_wa": w["lru_wa"][0], "lru_wx": w["lru_wx"][0], "final_norm_w": w["final_norm_w"][None]}
    for n in ("norm1_w", "ssd_conv_b", "ssd_dt_bias", "ssd_a_log", "ssd_d", "ssd_norm_w", "lru_conv_b", "lru_ba",
              "lru_bx", "lru_lambda", "lru_norm_w", "norm2_w"):
        p[n] = w[n]

    class Late:
        def __init__(self):
            self.pending = []
            self.before_embed = (late_slots[0],)

        def w_in(self, after):
            (buf,) = _split_wait(in_bufs, in_send, in_recv, _halves_plan, after, name="gather_w_in_wait")
            send, recv, bufs, tok = _split_start([buf], _forward_plan, 3, None, name="forward_w_in_start")
            self.late_gather = _split_start(late_slots, _halves_plan, 3 * len(LATE), tok, name="gather_late_start")
            (w_in4,) = _split_wait(bufs, send, recv, _forward_plan, self.late_gather[2][0], name="forward_w_in_wait")
            sections = {s: _rows_of_shards(w_in4, lo, hi) for s, (lo, hi) in IN_ROWS.items()}
            sections["dt"] = jnp.pad(sections["dt"], ((0, SEC_WIDTH["dt"] - SSD_HEADS), (0, 0)))
            return sections

        def mid_forward(self, after):
            send, recv, bufs, _ = self.late_gather
            bufs = _split_wait(bufs, send, recv, _halves_plan, after, name="gather_late_wait")
            self.forward = _split_start(bufs, _forward_plan, 3 * len(LATE), None, name="forward_late_start")
            return self.forward[3][:1, :1]

        def w_out(self, after):
            send, recv, bufs, _ = self.forward
            bufs = _split_wait(bufs, send, recv, _forward_plan, after, name="forward_late_wait")
            self.late = dict(zip(LATE, (b.reshape(-1, D_MODEL) for b in bufs)))
            return self.late["w_out"]

        def ffn(self, after):
            return tuple(self.late[n] for n in FFN)

        def grads_ready(self, names, g, g_mxu):
            if names == ("w_in",):
                g_mxu["w_in"] = jnp.stack([_w_in_shard_rows(k, [g_mxu["w_in_" + s] for s in SEC_NAMES])
                                           for k in range(N_SHARDS)])
            srcs = [g_mxu[n].reshape(N_SHARDS, -1, D_MODEL) for n in names]
            lands = [lax.empty((3,) + s.shape[1:], _MXU) for s in srcs]
            tag = "_".join(names)
            send, recv, bufs, tok = _split_start(srcs + lands, _scatter_plan, 3 * len(names), None,
                                                 name="scatter_" + tag + "_start")
            self.pending.append((names, send, recv, bufs, tag))
            self.in_flight = bufs[0]
            return tok[:1, :1]

        def landed(self, after, which):
            land = {}
            for names, send, recv, bufs, tag in self.pending:
                if names[0] in which:
                    bufs = _split_wait(bufs, send, recv, _scatter_plan, after, name="scatter_" + tag + "_wait")
                    land.update(zip(names, bufs[len(names):]))
            return land

        def small_ready(self, g, loss):
            pack = _pack([g[n] for n in SMALL] + [loss[0, :1]], 8 * N_DEV)
            pack = pack.reshape(N_DEV, -1, PACK_COLS)
            self.small = _split_start([pack, lax.empty(pack.shape, F32)], _pieces_plan, N_DEV - 1, loss,
                                      name="small_pieces_start")
            return self.small[3]

        def small_middle(self, after):
            send, recv, bufs, _ = self.small
            pack, land = _split_wait(bufs, send, recv, _pieces_plan, after, name="small_pieces_wait")
            piece = _sum_pieces(pack, land, dev_arr, name="small_pieces_sum")
            self.small = _split_start([piece, lax.empty(pack.shape, F32)], _spread_plan, N_DEV - 1, None,
                                      name="small_spread_start")
            return self.small[3]

        def small_sum(self, after):
            send, recv, bufs, _ = self.small
            piece, land = _split_wait(bufs, send, recv, _spread_plan, after, name="small_spread_wait")
            return _join_pieces(piece, land, dev_arr, name="small_join")

    late = Late()

    loss, grad_x, g, g_mxu = _local_step(x[0], loss_target[0], meta_full, p, late)

    g4 = {n: g[n].reshape(N_SHARDS, -1, D_MODEL) for n in LATE}
    g4["w_in"] = lax.switch(me, [functools.partial(_w_in_shard_rows, k) for k in range(N_SHARDS)],
                            [g["w_in_" + s] for s in SEC_NAMES])
    land = late.landed(late.in_flight, LATE)
    part = {n: _partial_sum(g4[n], land[n], me_arr, name="partial_" + n) for n in LATE}
    sib = dict(zip(LATE, _swap_with_sibling([part[n] for n in LATE], name="swap_late")))

    small_full_shape = {n: (SMALL_SHARDED[n] if n in SMALL_SHARDED else w[n].shape) for n in SMALL}
    red_list = _unpack(late.small_sum(sib["w_out"]), [small_full_shape[n] for n in SMALL] + [(1,)])
    loss_total = red_list[-1][0]
    g_small = {}
    for n, arr in zip(SMALL, red_list[:-1]):
        if n in SMALL_SHARDED:
            cols = SMALL_SHARDED[n][1] // N_SHARDS
            arr = lax.dynamic_slice_in_dim(arr, me * cols, cols, axis=1)
        g_small[n] = arr.reshape(w[n].shape)

    grad, delta, new_m, new_v = {}, {}, {}, {}

    def update_big(n):
        if n == "w_in":
            lanes = lambda a: a[0].reshape(8, 128, -1).transpose(2, 0, 1)
            pieces = lambda a: a.reshape(-1, 8, 128)
            outs = _adamw(lanes(w[n]), [pieces(part[n]), pieces(sib[n])], lanes(m[n]), lanes(v[n]),
                          name="adamw_" + n)
            grad[n], delta[n], new_m[n], new_v[n] = (o.transpose(1, 2, 0).reshape(1, D_MODEL, -1) for o in outs)
            return outs[0]
        outs = _adamw(big2d[n], [part[n], sib[n]], _rows_view(n, m[n]), _rows_view(n, v[n]), name="adamw_" + n)
        grad[n], delta[n], new_m[n], new_v[n] = (_param_view(n, o) for o in outs)
        return outs[0]

    two_d = lambda a: a.reshape(1, -1) if a.ndim == 1 else a
    deltas, new_ms, new_vs = _adamw_native(*[[two_d(d[n]) for n in SMALL] for d in (w, g_small, m, v)])
    for n, dn, mn, vn in zip(SMALL, deltas, new_ms, new_vs):
        grad[n], delta[n], new_m[n], new_v[n] = (g_small[n], dn.reshape(w[n].shape), mn.reshape(w[n].shape),
                                                 vn.reshape(w[n].shape))
    land.update(late.landed([update_big(n) for n in LATE] + [deltas[0]], ("w_in",)))
    part["w_in"] = _partial_sum(g4["w_in"], land["w_in"], me_arr, name="partial_w_in")
    (sib["w_in"],) = _swap_with_sibling([part["w_in"]], name="swap_w_in")
    update_big("w_in")

    return (loss_total, grad_x[None], *[grad[n] for n in WEIGHT_NAMES], *[delta[n] for n in WEIGHT_NAMES],
            *[new_m[n] for n in WEIGHT_NAMES], *[new_v[n] for n in WEIGHT_NAMES])
```

```python
import functools
import math

import jax
import jax.numpy as jnp
from jax import lax
from jax.experimental import pallas as pl
from jax.experimental.pallas import tpu as pltpu

F32 = jnp.float32
_MXU = jnp.bfloat16

D_MODEL = 1024
SEQ = 2048
N_META = 16
CHUNK = 128
T_ROWS = 2176
N_CHUNKS = T_ROWS // CHUNK
PAD_ROWS = T_ROWS - SEQ - N_META
X_ROW0 = PAD_ROWS + N_META
SSD_HEADS = 16
SSD_HEAD_DIM = 64
SSD_STATE = 128
SSD_GROUPS = 2
SSD_HPG = SSD_HEADS // SSD_GROUPS
SSD_WIDTH = 1024
LRU_WIDTH = 1024
LRU_C = 8.0
D_FF = 2816
EPS = 1e-6
IN_COLS = 4624
N_SHARDS = 4
N_DEV = 8

ADAM_LR = 0.001
ADAM_B1 = 0.9
ADAM_B2 = 0.999
ADAM_EPS = 1e-08
ADAM_WD = 0.01
ADAM_STEP = 10

VMEM_LIMIT_BYTES = 56 * 1024 * 1024

NN = (((1,), (0,)), ((), ()))
NT = (((1,), (1,)), ((), ()))
TN = (((0,), (0,)), ((), ()))


def _cparams(*sem):
    return pltpu.CompilerParams(dimension_semantics=sem, vmem_limit_bytes=VMEM_LIMIT_BYTES)


def _dot(a, b, dims=NN):
    return lax.dot_general(a.astype(_MXU), b.astype(_MXU), dims, preferred_element_type=F32)


def _dot_onehot(a, b, dims=NN, *, data=0, pieces=3):
    ops = [a, b]
    mask = ops[1 - data].astype(jnp.bfloat16)
    rest = ops[data]
    acc = None
    for _ in range(pieces):
        piece = rest.astype(jnp.bfloat16)
        ops[data], ops[1 - data] = piece, mask
        d = lax.dot_general(ops[0], ops[1], dims, preferred_element_type=F32)
        acc = d if acc is None else acc + d
        rest = rest - piece.astype(F32)
    return acc


def _sigmoid(x):
    return 0.5 * (1.0 + jnp.tanh(0.5 * x))


def _softplus(x):
    return jnp.maximum(x, 0.0) + jnp.log(1.0 + jnp.exp(-jnp.abs(x)))


def _silu(x):
    return x * _sigmoid(x)


def _silu_grad(x):
    s = _sigmoid(x)
    return s * (1.0 + x * (1.0 - s))


_GELU_C = math.sqrt(2.0 / math.pi)


def _gelu_and_grad(x):
    inner = _GELU_C * (x + 0.044715 * x * x * x)
    t = jnp.tanh(inner)
    g = 0.5 * x * (1.0 + t)
    dg = 0.5 * (1.0 + t) + 0.5 * x * (1.0 - t * t) * _GELU_C * (1.0 + 3.0 * 0.044715 * x * x)
    return g, dg


def _rms_fwd(x, w):
    rstd = lax.rsqrt(jnp.mean(x * x, axis=-1, keepdims=True) + EPS)
    return x * rstd * w


def _rms_bwd(x, w, dy):
    rstd = lax.rsqrt(jnp.mean(x * x, axis=-1, keepdims=True) + EPS)
    xhat = x * rstd
    dxhat = dy * w
    dx = rstd * (dxhat - xhat * jnp.mean(dxhat * xhat, axis=-1, keepdims=True))
    return dx, dy * xhat


def _mm(terms, m, n, *, tm, tn, mode, out_dtype, name, residual=None, n_outer=False, also_mxu=False, behind=()):
    gm, gn = m // tm, n // tn
    assert gm * tm == m and gn * tn == n
    if n_outer:
        grid = (gn, gm)
        mi = lambda g0, g1: g1
        ni = lambda g0, g1: g0
    else:
        grid = (gm, gn)
        mi = lambda g0, g1: g0
        ni = lambda g0, g1: g1
    in_specs, args = [], []
    for (a, ka, b, kb, k) in terms:
        if mode == "tn":
            in_specs.append(pl.BlockSpec((k, tm), lambda g0, g1, ka=ka: (ka, mi(g0, g1))))
        else:
            in_specs.append(pl.BlockSpec((tm, k), lambda g0, g1, ka=ka: (mi(g0, g1), ka)))
        if mode == "nt":
            in_specs.append(pl.BlockSpec((tn, k), lambda g0, g1, kb=kb: (ni(g0, g1), kb)))
        else:
            in_specs.append(pl.BlockSpec((k, tn), lambda g0, g1, kb=kb: (kb, ni(g0, g1))))
        args += [a, b]
    if residual is not None:
        in_specs.append(pl.BlockSpec((tm, tn), lambda g0, g1: (mi(g0, g1), ni(g0, g1))))
        args.append(residual)
    dims = {"nn": NN, "nt": NT, "tn": TN}[mode]
    n_terms = len(terms)
    has_res = residual is not None
    in_specs += [pl.BlockSpec(memory_space=pl.ANY)] * len(behind)
    args += list(behind)
    n_in = len(args)

    def body(*refs):
        acc = None
        for t in range(n_terms):
            d = lax.dot_general(refs[2 * t][...], refs[2 * t + 1][...], dims, preferred_element_type=F32)
            acc = d if acc is None else acc + d
        if has_res:
            acc = acc + refs[2 * n_terms][...]
        refs[n_in][...] = acc.astype(out_dtype)
        if also_mxu:
            refs[n_in + 1][...] = acc.astype(_MXU)

    tile = pl.BlockSpec((tm, tn), lambda g0, g1: (mi(g0, g1), ni(g0, g1)))
    shape = jax.ShapeDtypeStruct((m, n), out_dtype)
    return pl.pallas_call(
        body, name=name, grid=grid, in_specs=in_specs,
        out_specs=[tile, tile] if also_mxu else tile,
        out_shape=[shape, jax.ShapeDtypeStruct((m, n), _MXU)] if also_mxu else shape,
        compiler_params=_cparams("parallel", "parallel"),
    )(*args)


def _embed(x, meta, behind=()):
    def body(x_ref, meta_ref, *rest):
        o_ref = rest[-1]
        i = pl.program_id(0)

        @pl.when(i == 0)
        def _():
            o_ref[0:PAD_ROWS, :] = jnp.zeros((PAD_ROWS, D_MODEL), F32)
            o_ref[PAD_ROWS:CHUNK, :] = meta_ref[...]

        @pl.when(i > 0)
        def _():
            o_ref[...] = x_ref[...]

    return pl.pallas_call(
        body, name="embed", grid=(N_CHUNKS,),
        in_specs=[pl.BlockSpec((CHUNK, D_MODEL), lambda i: (jnp.maximum(i - 1, 0), 0)),
                  pl.BlockSpec((N_META, D_MODEL), lambda i: (0, 0))] + [pl.BlockSpec(memory_space=pl.ANY)] * len(behind),
        out_specs=pl.BlockSpec((CHUNK, D_MODEL), lambda i: (i, 0)),
        out_shape=jax.ShapeDtypeStruct((T_ROWS, D_MODEL), F32),
        compiler_params=_cparams("parallel"),
    )(x, meta, *behind)


def _rmsnorm(h, w, *, name, tm=544):
    def body(h_ref, w_ref, o_ref):
        o_ref[...] = _rms_fwd(h_ref[...], w_ref[...]).astype(_MXU)

    return pl.pallas_call(
        body, name=name, grid=(T_ROWS // tm,),
        in_specs=[pl.BlockSpec((tm, D_MODEL), lambda i: (i, 0)), pl.BlockSpec((1, D_MODEL), lambda i: (0, 0))],
        out_specs=pl.BlockSpec((tm, D_MODEL), lambda i: (i, 0)),
        out_shape=jax.ShapeDtypeStruct((T_ROWS, D_MODEL), _MXU),
        compiler_params=_cparams("parallel"),
    )(h, w)


def _norm_proj(h, w, sections, *, name, tm=544):
    widths = [s.shape[0] for s in sections]
    n = len(sections)

    def body(*refs):
        h_ref, w_ref = refs[:2]
        u_ref = refs[2 + n]
        u = _rms_fwd(h_ref[...], w_ref[...]).astype(_MXU)
        u_ref[...] = u
        for k in range(n):
            refs[3 + n + k][...] = lax.dot_general(u, refs[2 + k][...], NT, preferred_element_type=F32)

    row = lambda width: pl.BlockSpec((tm, width), lambda i: (i, 0))
    outs = pl.pallas_call(
        body, name=name, grid=(T_ROWS // tm,),
        in_specs=[row(D_MODEL), pl.BlockSpec((1, D_MODEL), lambda i: (0, 0))]
        + [pl.BlockSpec((wd, D_MODEL), lambda i: (0, 0)) for wd in widths],
        out_specs=[row(D_MODEL)] + [row(wd) for wd in widths],
        out_shape=[jax.ShapeDtypeStruct((T_ROWS, D_MODEL), _MXU)]
        + [jax.ShapeDtypeStruct((T_ROWS, wd), F32) for wd in widths],
        compiler_params=_cparams("parallel"),
    )(h, w, *sections)
    return outs[0], list(outs[1:])


def _loss_head(h2, target, fw):
    def body(h_ref, t_ref, w_ref, loss_ref, dh_ref, dhb_ref, dw_ref, acc_ref):
        i = pl.program_id(0)

        @pl.when(i == 0)
        def _():
            acc_ref[...] = jnp.zeros_like(acc_ref)
            dw_ref[...] = jnp.zeros_like(dw_ref)

        h = h_ref[...]
        w = w_ref[...]
        y = _rms_fwd(h, w)
        live = (i > 0).astype(F32)
        err = (y - t_ref[...]) * live
        acc_ref[...] += jnp.sum(err * err, axis=0, keepdims=True)
        dy = err * (1.0 / D_MODEL)
        dx, dwr = _rms_bwd(h, w, dy)
        dh_ref[...] = dx
        dhb_ref[...] = dx.astype(_MXU)
        dw_ref[...] += jnp.sum(dwr, axis=0, keepdims=True)

        @pl.when(i == N_CHUNKS - 1)
        def _():
            tot = jnp.sum(acc_ref[...], axis=1, keepdims=True) * (0.5 / D_MODEL)
            loss_ref[...] = jnp.broadcast_to(tot, (1, 128))

    return pl.pallas_call(
        body, name="loss_head", grid=(N_CHUNKS,),
        in_specs=[pl.BlockSpec((CHUNK, D_MODEL), lambda i: (i, 0)),
                  pl.BlockSpec((CHUNK, D_MODEL), lambda i: (jnp.maximum(i - 1, 0), 0)),
                  pl.BlockSpec((1, D_MODEL), lambda i: (0, 0))],
        out_specs=[pl.BlockSpec((1, 128), lambda i: (0, 0)),
                   pl.BlockSpec((CHUNK, D_MODEL), lambda i: (i, 0)),
                   pl.BlockSpec((CHUNK, D_MODEL), lambda i: (i, 0)),
                   pl.BlockSpec((1, D_MODEL), lambda i: (0, 0))],
        out_shape=[jax.ShapeDtypeStruct((1, 128), F32),
                   jax.ShapeDtypeStruct((T_ROWS, D_MODEL), F32),
                   jax.ShapeDtypeStruct((T_ROWS, D_MODEL), _MXU),
                   jax.ShapeDtypeStruct((1, D_MODEL), F32)],
        scratch_shapes=[pltpu.VMEM((1, D_MODEL), F32)],
        compiler_params=_cparams("arbitrary"),
    )(h2, target, fw)


def _mm_norm_bwd(terms, h, w, dres, *, name, tm=272, behind=()):
    n_terms = len(terms)
    in_specs, args = [], []
    for (a, b, k) in terms:
        in_specs += [pl.BlockSpec((tm, k), lambda i: (i, 0)), pl.BlockSpec((k, D_MODEL), lambda i: (0, 0))]
        args += [a, b]
    in_specs += [pl.BlockSpec((tm, D_MODEL), lambda i: (i, 0)), pl.BlockSpec((1, D_MODEL), lambda i: (0, 0)),
                 pl.BlockSpec((tm, D_MODEL), lambda i: (i, 0))] + [pl.BlockSpec(memory_space=pl.ANY)] * len(behind)
    args += [h, w, dres, *behind]

    def body(*refs):
        h_ref, w_ref, dres_ref = refs[2 * n_terms:2 * n_terms + 3]
        dh_ref, dhb_ref, dw_ref = refs[2 * n_terms + 3 + len(behind):]

        @pl.when(pl.program_id(0) == 0)
        def _():
            dw_ref[...] = jnp.zeros_like(dw_ref)

        du = None
        for t in range(n_terms):
            d = lax.dot_general(refs[2 * t][...], refs[2 * t + 1][...], NN, preferred_element_type=F32)
            du = d if du is None else du + d
        dx, dwr = _rms_bwd(h_ref[...], w_ref[...], du)
        dh = dres_ref[...] + dx
        dh_ref[...] = dh
        dhb_ref[...] = dh.astype(_MXU)
        dw_ref[...] += jnp.sum(dwr, axis=0, keepdims=True)

    return pl.pallas_call(
        body, name=name, grid=(T_ROWS // tm,), in_specs=in_specs,
        out_specs=[pl.BlockSpec((tm, D_MODEL), lambda i: (i, 0)), pl.BlockSpec((tm, D_MODEL), lambda i: (i, 0)),
                   pl.BlockSpec((1, D_MODEL), lambda i: (0, 0))],
        out_shape=[jax.ShapeDtypeStruct((T_ROWS, D_MODEL), F32), jax.ShapeDtypeStruct((T_ROWS, D_MODEL), _MXU),
                   jax.ShapeDtypeStruct((1, D_MODEL), F32)],
        compiler_params=_cparams("arbitrary"),
    )(*args)


FFN_TM = T_ROWS
FFN_TN = 256


def _ffn_up(u2, wg_t, wu_t):
    def body(u_ref, wg_ref, wu_ref, gp_ref, up_ref, act_ref):
        u = u_ref[...]
        gp = lax.dot_general(u, wg_ref[...], NT, preferred_element_type=F32)
        up = lax.dot_general(u, wu_ref[...], NT, preferred_element_type=F32)
        gp_ref[...] = gp.astype(_MXU)
        up_ref[...] = up.astype(_MXU)
        act_ref[...] = (_silu(gp) * up).astype(_MXU)

    tile = pl.BlockSpec((FFN_TM, FFN_TN), lambda j, i: (i, j))
    return pl.pallas_call(
        body, name="ffn_up", grid=(D_FF // FFN_TN, T_ROWS // FFN_TM),
        in_specs=[pl.BlockSpec((FFN_TM, D_MODEL), lambda j, i: (i, 0)),
                  pl.BlockSpec((FFN_TN, D_MODEL), lambda j, i: (j, 0)),
                  pl.BlockSpec((FFN_TN, D_MODEL), lambda j, i: (j, 0))],
        out_specs=[tile, tile, tile],
        out_shape=[jax.ShapeDtypeStruct((T_ROWS, D_FF), _MXU)] * 3,
        compiler_params=_cparams("parallel", "parallel"),
    )(u2, wg_t, wu_t)


def _ffn_bwd_act(dh2b, wd, gp, up):
    def body(dh_ref, wd_ref, gp_ref, up_ref, dgp_ref, dup_ref):
        dact = lax.dot_general(dh_ref[...], wd_ref[...], NT, preferred_element_type=F32)
        gp = gp_ref[...].astype(F32)
        dgp_ref[...] = (dact * up_ref[...].astype(F32) * _silu_grad(gp)).astype(_MXU)
        dup_ref[...] = (dact * _silu(gp)).astype(_MXU)

    tile = pl.BlockSpec((FFN_TM, FFN_TN), lambda j, i: (i, j))
    return pl.pallas_call(
        body, name="ffn_bwd_act", grid=(D_FF // FFN_TN, T_ROWS // FFN_TM),
        in_specs=[pl.BlockSpec((FFN_TM, D_MODEL), lambda j, i: (i, 0)),
                  pl.BlockSpec((FFN_TN, D_MODEL), lambda j, i: (j, 0)), tile, tile],
        out_specs=[tile, tile],
        out_shape=[jax.ShapeDtypeStruct((T_ROWS, D_FF), _MXU), jax.ShapeDtypeStruct((T_ROWS, D_FF), _MXU)],
        compiler_params=_cparams("parallel", "parallel"),
    )(dh2b, wd, gp, up)


CONV_TC = 512
CONV_K = 4


def _conv_pre(x_ref, wv, bv, c):
    tc = wv.shape[1]
    r0 = c * CHUNK
    cur = x_ref[r0:r0 + CHUNK, :]
    if c == 0:
        cat = jnp.concatenate([jnp.zeros((8, tc), F32), cur], axis=0)
        shifted = [cur] + [pltpu.roll(cat, s, 0)[8:8 + CHUNK] for s in range(1, CONV_K)]
    else:
        shifted = [cur] + [x_ref[r0 - s:r0 - s + CHUNK, :] for s in range(1, CONV_K)]
    pre = bv
    for s in range(CONV_K):
        pre = pre + shifted[s] * wv[CONV_K - 1 - s:CONV_K - s]
    return pre, shifted


def _row_mask(c):
    if c > 0:
        return None
    return (lax.broadcasted_iota(jnp.int32, (CHUNK, 1), 0) >= PAD_ROWS).astype(F32)


def _conv_fwd(x, w, b, *, silu, name):
    cols = x.shape[1]
    tc = min(CONV_TC, cols)

    def body(x_ref, w_ref, b_ref, o_ref):
        wv, bv = w_ref[...], b_ref[...]
        for c in range(N_CHUNKS):
            pre, _ = _conv_pre(x_ref, wv, bv, c)
            y = _silu(pre) if silu else pre
            mask = _row_mask(c)
            if mask is not None:
                y = y * mask
            o_ref[c * CHUNK:(c + 1) * CHUNK, :] = y

    return pl.pallas_call(
        body, name=name, grid=(cols // tc,),
        in_specs=[pl.BlockSpec((T_ROWS, tc), lambda j: (0, j)), pl.BlockSpec((CONV_K, tc), lambda j: (0, j)),
                  pl.BlockSpec((1, tc), lambda j: (0, j))],
        out_specs=pl.BlockSpec((T_ROWS, tc), lambda j: (0, j)),
        out_shape=jax.ShapeDtypeStruct((T_ROWS, cols), F32),
        compiler_params=_cparams("parallel"),
    )(x, w, b)


def _conv_bwd(dy, x, w, b, *, silu, name):
    cols = x.shape[1]
    tc = min(CONV_TC, cols)

    def body(dy_ref, x_ref, w_ref, b_ref, dx_ref, dw_ref, db_ref):
        wv, bv = w_ref[...], b_ref[...]
        next8 = jnp.zeros((8, tc), F32)
        dws = [jnp.zeros((1, tc), F32) for _ in range(CONV_K)]
        db = jnp.zeros((1, tc), F32)
        for c in reversed(range(N_CHUNKS)):
            r0 = c * CHUNK
            pre, shifted = _conv_pre(x_ref, wv, bv, c)
            dpre = dy_ref[r0:r0 + CHUNK, :]
            if silu:
                dpre = dpre * _silu_grad(pre)
            mask = _row_mask(c)
            if mask is not None:
                dpre = dpre * mask
            cat = jnp.concatenate([dpre, next8], axis=0)
            dx = dpre * wv[CONV_K - 1:CONV_K]
            for s in range(1, CONV_K):
                dx = dx + pltpu.roll(cat, CHUNK + 8 - s, 0)[0:CHUNK] * wv[CONV_K - 1 - s:CONV_K - s]
            dx_ref[r0:r0 + CHUNK, :] = dx.astype(_MXU)
            for s in range(CONV_K):
                k = CONV_K - 1 - s
                dws[k] = dws[k] + jnp.sum(dpre * shifted[s], axis=0, keepdims=True)
            db = db + jnp.sum(dpre, axis=0, keepdims=True)
            next8 = dpre[0:8]
        dw_ref[...] = jnp.concatenate(dws, axis=0)
        db_ref[...] = db

    return pl.pallas_call(
        body, name=name, grid=(cols // tc,),
        in_specs=[pl.BlockSpec((T_ROWS, tc), lambda j: (0, j)), pl.BlockSpec((T_ROWS, tc), lambda j: (0, j)),
                  pl.BlockSpec((CONV_K, tc), lambda j: (0, j)), pl.BlockSpec((1, tc), lambda j: (0, j))],
        out_specs=[pl.BlockSpec((T_ROWS, tc), lambda j: (0, j)), pl.BlockSpec((CONV_K, tc), lambda j: (0, j)),
                   pl.BlockSpec((1, tc), lambda j: (0, j))],
        out_shape=[jax.ShapeDtypeStruct((T_ROWS, cols), _MXU), jax.ShapeDtypeStruct((CONV_K, cols), F32),
                   jax.ShapeDtypeStruct((1, cols), F32)],
        compiler_params=_cparams("parallel"),
    )(dy, x, w, b)


def _ssd_chunk_common(dt_raw, prm, c):
    a_row = -jnp.exp(prm[1:2])
    dt = _softplus(dt_raw + prm[0:1])
    rows = lax.broadcasted_iota(jnp.int32, (CHUNK, 1), 0)
    real = jnp.logical_or(c > 0, rows >= PAD_ROWS)
    dt = jnp.where(real, dt, 0.0)
    li = lax.broadcasted_iota(jnp.int32, (CHUNK, CHUNK), 0)
    si = lax.broadcasted_iota(jnp.int32, (CHUNK, CHUNK), 1)
    causal = li >= si
    tri = causal.astype(F32)
    cs = _dot_onehot(tri, dt * a_row, data=1)
    return dt, a_row, cs, cs.T, causal, tri, real


def _gated_norm_fwd(y, z, w):
    g = y * _silu(z)
    half = SSD_WIDTH // SSD_GROUPS
    outs = [_rms_fwd(g[:, k * half:(k + 1) * half], w[:, k * half:(k + 1) * half]) for k in range(SSD_GROUPS)]
    return jnp.concatenate(outs, axis=1)


GROUP_W = SSD_WIDTH // SSD_GROUPS
PAIR_W = 2 * SSD_HEAD_DIM
STATE_SHAPE = (SSD_GROUPS, SSD_STATE, GROUP_W)


def _head_expander():
    r = lax.broadcasted_iota(jnp.int32, (128, SSD_WIDTH), 0)
    c = lax.broadcasted_iota(jnp.int32, (128, SSD_WIDTH), 1)
    return (c // SSD_HEAD_DIM == r).astype(F32)


def _ssd_expand(dt, cs, prm, ex):
    cs_x = _dot_onehot(cs, ex)
    cs_last_x = cs_x[CHUNK - 1:CHUNK, :]
    return (_dot_onehot(dt, ex, pieces=2), _dot_onehot(prm, ex)[2:3], jnp.exp(cs_x), jnp.exp(cs_last_x),
            jnp.exp(cs_last_x - cs_x))


def _ssd_fwd(xs, bc, dt_raw, z, prm, norm_w, ex):
    def body(xs_ref, bc_ref, dt_ref, z_ref, prm_ref, nw_ref, ex_ref, y_ref, yn_ref, prev_ref, state):
        c = pl.program_id(0)

        @pl.when(c == 0)
        def _():
            state[...] = jnp.zeros_like(state)

        prm = prm_ref[...]
        dt, a_row, cs, cs_t, causal, _, _ = _ssd_chunk_common(dt_ref[...], prm, c)
        dt_x, d_x, e_cs_x, e_last_x, dec_x = _ssd_expand(dt, cs, prm, ex_ref[...])
        xs_all = xs_ref[...]
        bc_all = bc_ref[...]
        xdt = xs_all * dt_x
        xdec = xdt * dec_x
        lane_lo = lax.broadcasted_iota(jnp.int32, (1, PAIR_W), 1) < SSD_HEAD_DIM
        for g in range(SSD_GROUPS):
            gs = slice(g * GROUP_W, (g + 1) * GROUP_W)
            b_g = bc_all[:, g * SSD_STATE:(g + 1) * SSD_STATE]
            c_g = bc_all[:, (SSD_GROUPS + g) * SSD_STATE:(SSD_GROUPS + g + 1) * SSD_STATE]
            st = state[g]
            prev_ref[0, g] = st
            y_off = _dot(c_g, st) * e_cs_x[:, gs]
            state[g] = st * e_last_x[:, gs] + _dot(b_g.T, xdec[:, gs])
            cb = _dot(c_g, b_g, NT)
            for k in range(SSD_HPG // 2):
                h0 = g * SSD_HPG + 2 * k
                ps = slice(h0 * SSD_HEAD_DIM, h0 * SSD_HEAD_DIM + PAIR_W)
                xdt_pair = xdt[:, ps]
                yd = []
                for h in (h0, h0 + 1):
                    lmat = jnp.where(causal, jnp.exp(cs[:, h:h + 1] - cs_t[h:h + 1, :]), 0.0)
                    yd.append(_dot(cb * lmat, xdt_pair))
                y_ref[:, ps] = (jnp.where(lane_lo, yd[0], yd[1]) + y_off[:, k * PAIR_W:(k + 1) * PAIR_W]
                                + xs_all[:, ps] * d_x[:, ps])
        yn_ref[...] = _gated_norm_fwd(y_ref[...], z_ref[...], nw_ref[...]).astype(_MXU)

    row = lambda w: pl.BlockSpec((CHUNK, w), lambda c: (c, 0))
    return pl.pallas_call(
        body, name="ssd_fwd", grid=(N_CHUNKS,),
        in_specs=[row(SSD_WIDTH), row(512), row(128), row(SSD_WIDTH),
                  pl.BlockSpec((8, 128), lambda c: (0, 0)), pl.BlockSpec((1, SSD_WIDTH), lambda c: (0, 0)),
                  pl.BlockSpec((128, SSD_WIDTH), lambda c: (0, 0))],
        out_specs=[row(SSD_WIDTH), row(SSD_WIDTH),
                   pl.BlockSpec((1,) + STATE_SHAPE, lambda c: (c, 0, 0, 0))],
        out_shape=[jax.ShapeDtypeStruct((T_ROWS, SSD_WIDTH), F32), jax.ShapeDtypeStruct((T_ROWS, SSD_WIDTH), _MXU),
                   jax.ShapeDtypeStruct((N_CHUNKS,) + STATE_SHAPE, F32)],
        scratch_shapes=[pltpu.VMEM(STATE_SHAPE, F32)],
        compiler_params=_cparams("arbitrary"),
    )(xs, bc, dt_raw, z, prm, norm_w, ex)


def _ssd_bwd(dyn, dyn_block, z, y_pre, xs, bc, dt_raw, prev, prm, norm_w, ex):
    def body(dyn_ref, z_ref, y_ref, xs_ref, bc_ref, dt_ref, prev_ref, prm_ref, nw_ref, ex_ref,
             dz_ref, dxs_ref, dbc_ref, ddt_ref, dprm_ref, dnw_ref, dstate):
        step = pl.program_id(0)
        c = N_CHUNKS - 1 - step

        @pl.when(step == 0)
        def _():
            dstate[...] = jnp.zeros_like(dstate)
            dprm_ref[...] = jnp.zeros_like(dprm_ref)
            dnw_ref[...] = jnp.zeros_like(dnw_ref)

        prm = prm_ref[...]
        dt, a_row, cs, cs_t, causal, tri, real = _ssd_chunk_common(dt_ref[...], prm, c)
        realf = real.astype(F32)
        z = z_ref[...]
        y_all = y_ref[...]
        nw = nw_ref[...]
        dyn_all = dyn_ref[...]
        sz = _silu(z)
        gated = y_all * sz
        half = SSD_WIDTH // SSD_GROUPS
        dgs, dnws = [], []
        for k in range(SSD_GROUPS):
            sl = slice(k * half, (k + 1) * half)
            dgk, dwk = _rms_bwd(gated[:, sl], nw[:, sl], dyn_all[:, sl])
            dgs.append(dgk)
            dnws.append(jnp.sum(dwk, axis=0, keepdims=True))
        dgated = jnp.concatenate(dgs, axis=1)
        dnw_ref[...] += jnp.concatenate(dnws, axis=1)
        dz_ref[...] = (dgated * y_all * _silu_grad(z)).astype(_MXU)
        dy_all = dgated * sz

        ex = ex_ref[...]
        dt_x, d_x, e_cs_x, e_last_x, dec_x = _ssd_expand(dt, cs, prm, ex)
        xs_all = xs_ref[...]
        bc_all = bc_ref[...]
        xdt = xs_all * dt_x
        xdt_mxu = xdt.astype(_MXU).astype(F32)
        xdec = xdt * dec_x
        dcp = dy_all * e_cs_x
        lane_lo = lax.broadcasted_iota(jnp.int32, (1, PAIR_W), 1) < SSD_HEAD_DIM
        upper = (lax.broadcasted_iota(jnp.int32, (CHUNK, CHUNK), 0)
                 <= lax.broadcasted_iota(jnp.int32, (CHUNK, CHUNK), 1))
        last_row = (lax.broadcasted_iota(jnp.int32, (CHUNK, 1), 0) == CHUNK - 1).astype(F32)
        dbs, dcs_, dxdt_parts, last_parts = [], [], [], []
        for g in range(SSD_GROUPS):
            gs = slice(g * GROUP_W, (g + 1) * GROUP_W)
            b_g = bc_all[:, g * SSD_STATE:(g + 1) * SSD_STATE]
            c_g = bc_all[:, (SSD_GROUPS + g) * SSD_STATE:(SSD_GROUPS + g + 1) * SSD_STATE]
            prev_t = prev_ref[0, g]
            dst = dstate[g]
            dc_g = _dot(dcp[:, gs], prev_t, NT)
            db_g = _dot(xdec[:, gs], dst, NT)
            dxdt_state = _dot(b_g, dst) * dec_x[:, gs]
            dstate[g] = dst * e_last_x[:, gs] + _dot(c_g.T, dcp[:, gs])
            last_parts.append(jnp.sum(xdt_mxu[:, gs] * dxdt_state, axis=0, keepdims=True)
                              + jnp.sum(dst * prev_t, axis=0, keepdims=True) * e_last_x[:, gs])
            cb_t = _dot(b_g, c_g, NT)
            dcb_t = jnp.zeros((CHUNK, CHUNK), F32)
            for k in range(SSD_HPG // 2):
                h0 = g * SSD_HPG + 2 * k
                ps = slice(h0 * SSD_HEAD_DIM, h0 * SSD_HEAD_DIM + PAIR_W)
                dy_pair = dy_all[:, ps]
                xdt_pair = xdt[:, ps]
                dd = []
                for h in (h0, h0 + 1):
                    lmat_t = jnp.where(upper, jnp.exp(cs_t[h:h + 1, :] - cs[:, h:h + 1]), 0.0)
                    dd.append(_dot(cb_t * lmat_t, dy_pair))
                    mine = lane_lo if h == h0 else jnp.logical_not(lane_lo)
                    dcb_t = dcb_t + _dot(jnp.where(mine, xdt_pair, 0.0), dy_pair, NT) * lmat_t
                dxdt_parts.append(jnp.where(lane_lo, dd[0], dd[1]) + dxdt_state[:, k * PAIR_W:(k + 1) * PAIR_W])
            dc_g = dc_g + _dot(dcb_t, b_g, TN)
            db_g = db_g + _dot(dcb_t, c_g)
            dbs.append(db_g * realf)
            dcs_.append(dc_g * realf)
        dbc_ref[...] = jnp.concatenate(dbs + dcs_, axis=1)
        dxdt = jnp.concatenate(dxdt_parts, axis=1)
        dxs_ref[...] = (dxdt * dt_x + dy_all * d_x) * realf
        ddt_all = _dot_onehot(dxdt * xs_all, ex, NT, pieces=2)
        rows = jnp.concatenate([jnp.concatenate(last_parts, axis=1), jnp.sum(dy_all * xs_all, axis=0, keepdims=True),
                                jnp.zeros((6, SSD_WIDTH), F32)], axis=0)
        rows = _dot_onehot(rows, ex, NT, pieces=2)
        dd_row = rows[1:2]
        dy_mxu = dy_all.astype(_MXU).astype(F32)
        dcs_all = (_dot_onehot(dy_mxu * (y_all - xs_all * d_x), ex, NT) - _dot_onehot(xdt_mxu * dxdt, ex, NT)
                   + last_row * rows[0:1])
        dda = _dot_onehot(tri, dcs_all, TN, data=1)
        ddt = (ddt_all + dda * a_row) * realf
        ddt_raw = ddt * _sigmoid(dt_ref[...] + prm[0:1])
        ddt_ref[...] = ddt_raw.astype(_MXU)
        da_log = jnp.sum(dda * dt, axis=0, keepdims=True) * a_row
        dprm_ref[0:1, :] += jnp.sum(ddt_raw, axis=0, keepdims=True)
        dprm_ref[1:2, :] += da_log
        dprm_ref[2:3, :] += dd_row

    rev = lambda w, blk=0: pl.BlockSpec((CHUNK, w), lambda s, blk=blk: (N_CHUNKS - 1 - s, blk))
    return pl.pallas_call(
        body, name="ssd_bwd", grid=(N_CHUNKS,),
        in_specs=[rev(SSD_WIDTH, dyn_block), rev(SSD_WIDTH), rev(SSD_WIDTH), rev(SSD_WIDTH), rev(512), rev(128),
                  pl.BlockSpec((1,) + STATE_SHAPE, lambda s: (N_CHUNKS - 1 - s, 0, 0, 0)),
                  pl.BlockSpec((8, 128), lambda s: (0, 0)), pl.BlockSpec((1, SSD_WIDTH), lambda s: (0, 0)),
                  pl.BlockSpec((128, SSD_WIDTH), lambda s: (0, 0))],
        out_specs=[rev(SSD_WIDTH), rev(SSD_WIDTH), rev(512), rev(128),
                   pl.BlockSpec((8, 128), lambda s: (0, 0)), pl.BlockSpec((1, SSD_WIDTH), lambda s: (0, 0))],
        out_shape=[jax.ShapeDtypeStruct((T_ROWS, SSD_WIDTH), _MXU), jax.ShapeDtypeStruct((T_ROWS, SSD_WIDTH), F32),
                   jax.ShapeDtypeStruct((T_ROWS, 512), F32), jax.ShapeDtypeStruct((T_ROWS, 128), _MXU),
                   jax.ShapeDtypeStruct((8, 128), F32), jax.ShapeDtypeStruct((1, SSD_WIDTH), F32)],
        scratch_shapes=[pltpu.VMEM(STATE_SHAPE, F32)],
        compiler_params=_cparams("arbitrary"),
    )(dyn, z, y_pre, xs, bc, dt_raw, prev, prm, norm_w, ex)


LRU_PAIRS = 8


def _lru_gates(xr, wa_ref, wx_ref, prm):
    pre_r, pre_i = [], []
    for k in range(LRU_PAIRS):
        xk = xr[:, k * 128:(k + 1) * 128]
        pre_r.append(_dot(xk, wa_ref[k]))
        pre_i.append(_dot(xk, wx_ref[k]))
    r = _sigmoid(jnp.concatenate(pre_r, axis=1) + prm[0:1])
    i = _sigmoid(jnp.concatenate(pre_i, axis=1) + prm[1:2])
    sp = _softplus(-prm[2:3])
    log_a = (-LRU_C) * r * sp
    a = jnp.exp(log_a)
    s = jnp.sqrt(-jnp.tanh(log_a) * (a * a + 1.0))
    return r, i, a, s, sp


def _lru_fwd(xr, gate, wa, wx, prm):
    def body(xr_ref, g_ref, wa_ref, wx_ref, prm_ref, hs_ref, yn_ref, carry, a_s, u_s):
        @pl.when(pl.program_id(0) == 0)
        def _():
            carry[...] = jnp.zeros_like(carry)

        prm = prm_ref[...]
        xr_t = xr_ref[...]
        _, i, a, s, _ = _lru_gates(xr_t, wa_ref, wx_ref, prm)
        a_s[...] = a
        u_s[...] = s * (i * xr_t)
        rid = lax.broadcasted_iota(jnp.int32, (8, LRU_WIDTH), 0)

        def group(k, before):
            off = pl.multiple_of(k * 8, 8)
            a8 = a_s[pl.ds(off, 8), :]
            u8 = u_s[pl.ds(off, 8), :]
            for d in (1, 2, 4):
                keep = rid >= d
                u8 = u8 + a8 * jnp.where(keep, pltpu.roll(u8, d, 0), 0.0)
                a8 = a8 * jnp.where(keep, pltpu.roll(a8, d, 0), 1.0)
            h8 = u8 + a8 * before
            hs_ref[pl.ds(off, 8), :] = h8
            return jnp.broadcast_to(h8[7:8], (8, LRU_WIDTH))

        carry[...] = lax.fori_loop(0, CHUNK // 8, group, carry[...])
        gel, _ = _gelu_and_grad(g_ref[...])
        yn_ref[...] = _rms_fwd(gel * hs_ref[...], prm[3:4]).astype(_MXU)

    row = pl.BlockSpec((CHUNK, LRU_WIDTH), lambda t: (t, 0))
    wspec = pl.BlockSpec((LRU_PAIRS, 128, 128), lambda t: (0, 0, 0))
    return pl.pallas_call(
        body, name="lru_fwd", grid=(N_CHUNKS,),
        in_specs=[row, row, wspec, wspec, pl.BlockSpec((8, LRU_WIDTH), lambda t: (0, 0))],
        out_specs=[row, row],
        out_shape=[jax.ShapeDtypeStruct((T_ROWS, LRU_WIDTH), F32), jax.ShapeDtypeStruct((T_ROWS, LRU_WIDTH), _MXU)],
        scratch_shapes=[pltpu.VMEM((8, LRU_WIDTH), F32), pltpu.VMEM((CHUNK, LRU_WIDTH), F32),
                        pltpu.VMEM((CHUNK, LRU_WIDTH), F32)],
        compiler_params=_cparams("arbitrary"),
    )(xr, gate, wa, wx, prm)


def _lru_bwd(dyn, dyn_block, gate, xr, hs, wa, wx, wa_t, wx_t, prm):
    def body(dyn_ref, g_ref, xr_ref, hs_ref, hsp_ref, wa_ref, wx_ref, wat_ref, wxt_ref, prm_ref,
             dg_ref, dxr_ref, dwa_ref, dwx_ref, dprm_ref, carry, a_s, d_s):
        step = pl.program_id(0)
        tile = N_CHUNKS - 1 - step

        @pl.when(step == 0)
        def _():
            carry[...] = jnp.zeros_like(carry)
            dwa_ref[...] = jnp.zeros_like(dwa_ref)
            dwx_ref[...] = jnp.zeros_like(dwx_ref)
            dprm_ref[...] = jnp.zeros_like(dprm_ref)

        prm = prm_ref[...]
        xr_t = xr_ref[...]
        r, i, a, s, sp = _lru_gates(xr_t, wa_ref, wx_ref, prm)
        hs_t = hs_ref[...]
        gel, dgel = _gelu_and_grad(g_ref[...])
        dy, dnw = _rms_bwd(gel * hs_t, prm[3:4], dyn_ref[...])
        dg_ref[...] = (dy * hs_t * dgel).astype(_MXU)
        a_s[...] = a
        d_s[...] = dy * gel
        rid = lax.broadcasted_iota(jnp.int32, (8, LRU_WIDTH), 0)

        def group(k, behind):
            off = pl.multiple_of((CHUNK // 8 - 1 - k) * 8, 8)
            a8 = a_s[pl.ds(off, 8), :]
            d8 = d_s[pl.ds(off, 8), :]
            c8 = jnp.where(rid == 7, 1.0, pltpu.roll(a8, 7, 0))
            for d in (1, 2, 4):
                keep = rid < 8 - d
                d8 = d8 + c8 * jnp.where(keep, pltpu.roll(d8, 8 - d, 0), 0.0)
                c8 = c8 * jnp.where(keep, pltpu.roll(c8, 8 - d, 0), 1.0)
            dht8 = d8 + c8 * behind
            d_s[pl.ds(off, 8), :] = dht8
            return jnp.broadcast_to(a8[0:1] * dht8[0:1], (8, LRU_WIDTH))

        carry[...] = lax.fori_loop(0, CHUNK // 8, group, carry[...])
        dht = d_s[...]
        before = hsp_ref[CHUNK - 8:CHUNK, :][7:8] * (tile > 0).astype(F32)
        first = lax.broadcasted_iota(jnp.int32, (CHUNK, 1), 0) == 0
        hprev = jnp.where(first, before, pltpu.roll(hs_t, 1, 0))
        da = dht * hprev
        ixr = i * xr_t
        ds = dht * ixr
        dlog_a = da * a - ds * (a * a) * lax.rsqrt(s * s)
        dr = dlog_a * ((-LRU_C) * sp)
        dsp = jnp.sum(dlog_a * ((-LRU_C) * r), axis=0, keepdims=True)
        dlam = dsp * (-_sigmoid(-prm[2:3]))
        di = dht * s * xr_t
        dpre_r = dr * r * (1.0 - r)
        dpre_i = di * i * (1.0 - i)
        dxr = dht * s * i
        parts = []
        for k in range(LRU_PAIRS):
            sl = slice(k * 128, (k + 1) * 128)
            parts.append(_dot(dpre_r[:, sl], wat_ref[k]) + _dot(dpre_i[:, sl], wxt_ref[k]))
            dwa_ref[k] += _dot(xr_t[:, sl], dpre_r[:, sl], TN)
            dwx_ref[k] += _dot(xr_t[:, sl], dpre_i[:, sl], TN)
        dxr_ref[...] = dxr + jnp.concatenate(parts, axis=1)
        dprm_ref[0:1, :] += jnp.sum(dpre_r, axis=0, keepdims=True)
        dprm_ref[1:2, :] += jnp.sum(dpre_i, axis=0, keepdims=True)
        dprm_ref[2:3, :] += dlam
        dprm_ref[3:4, :] += jnp.sum(dnw, axis=0, keepdims=True)

    rev = lambda blk=0: pl.BlockSpec((CHUNK, LRU_WIDTH), lambda s, blk=blk: (N_CHUNKS - 1 - s, blk))
    wspec = pl.BlockSpec((LRU_PAIRS, 128, 128), lambda s: (0, 0, 0))
    return pl.pallas_call(
        body, name="lru_bwd", grid=(N_CHUNKS,),
        in_specs=[rev(dyn_block), rev(), rev(), rev(),
                  pl.BlockSpec((CHUNK, LRU_WIDTH), lambda s: (jnp.maximum(N_CHUNKS - 2 - s, 0), 0)),
                  wspec, wspec, wspec, wspec, pl.BlockSpec((8, LRU_WIDTH), lambda s: (0, 0))],
        out_specs=[rev(), rev(), wspec, wspec, pl.BlockSpec((8, LRU_WIDTH), lambda s: (0, 0))],
        out_shape=[jax.ShapeDtypeStruct((T_ROWS, LRU_WIDTH), _MXU), jax.ShapeDtypeStruct((T_ROWS, LRU_WIDTH), F32),
                   jax.ShapeDtypeStruct((LRU_PAIRS, 128, 128), F32), jax.ShapeDtypeStruct((LRU_PAIRS, 128, 128), F32),
                   jax.ShapeDtypeStruct((8, LRU_WIDTH), F32)],
        scratch_shapes=[pltpu.VMEM((8, LRU_WIDTH), F32), pltpu.VMEM((CHUNK, LRU_WIDTH), F32),
                        pltpu.VMEM((CHUNK, LRU_WIDTH), F32)],
        compiler_params=_cparams("arbitrary"),
    )(dyn, gate, xr, hs, hs, wa, wx, wa_t, wx_t, prm)


SEC_NAMES = ("z", "xs", "bc", "dt", "g", "x")
SEC_WIDTH = {"z": 1024, "xs": 1024, "bc": 512, "dt": 128, "g": 1024, "x": 1024}


def _pair_blocks(w):
    w = w.reshape(LRU_PAIRS, 2, 64, 64)
    zero = jnp.zeros((LRU_PAIRS, 64, 64), w.dtype)
    top = jnp.concatenate([w[:, 0], zero], axis=2)
    bot = jnp.concatenate([zero, w[:, 1]], axis=2)
    return jnp.concatenate([top, bot], axis=1)


def _unpair_blocks(wp):
    return jnp.stack([wp[:, :64, :64], wp[:, 64:, 64:]], axis=1).reshape(16, 64, 64)


def _pad_lanes(v, width=128):
    return jnp.pad(v, ((0, 0), (0, width - v.shape[1])))


class _Resident:
    before_embed = ()

    def __init__(self, w_in_sections, w_out, w_gate, w_up, w_down):
        self._w_in, self._w_out, self._ffn = w_in_sections, w_out, (w_gate, w_up, w_down)

    def w_in(self, after):
        return self._w_in

    def mid_forward(self, after):
        return jnp.zeros((1, 1), F32)

    def w_out(self, after):
        return self._w_out

    def ffn(self, after):
        return self._ffn

    def grads_ready(self, names, g, g_mxu):
        return jnp.zeros((1, 1), F32)

    def small_ready(self, g, loss):
        return jnp.zeros((1, 1), F32)

    def small_middle(self, after):
        return jnp.zeros((1, 1), F32)


def _local_step(x, target, meta, p, late):
    g, g_mxu = {}, {}
    ex = _head_expander()
    h0 = _embed(x, meta, late.before_embed)
    w_in = late.w_in(h0)
    u1, projs = _norm_proj(h0, p["norm1_w"], [w_in[s] for s in SEC_NAMES], name="norm_in_proj")
    proj = dict(zip(SEC_NAMES, projs))
    ssd_prm = jnp.concatenate([_pad_lanes(p["ssd_dt_bias"]), _pad_lanes(p["ssd_a_log"]), _pad_lanes(p["ssd_d"]),
                               jnp.zeros((5, 128), F32)], axis=0)
    xs_act = _conv_fwd(proj["xs"], p["ssd_conv_w"][:, :SSD_WIDTH], p["ssd_conv_b"][:, :SSD_WIDTH], silu=True,
                       name="ssd_conv_xs")
    bc_act = _conv_fwd(proj["bc"], p["ssd_conv_w"][:, SSD_WIDTH:], p["ssd_conv_b"][:, SSD_WIDTH:], silu=True,
                       name="ssd_conv_bc")
    y_pre, y_ssd, prev = _ssd_fwd(xs_act, bc_act, proj["dt"], proj["z"], ssd_prm, p["ssd_norm_w"], ex)
    xr = _conv_fwd(proj["x"], p["lru_conv_w"], p["lru_conv_b"], silu=False, name="lru_conv")
    wa_p, wx_p = _pair_blocks(p["lru_wa"]), _pair_blocks(p["lru_wx"])
    lru_prm = jnp.concatenate([p["lru_ba"], p["lru_bx"], p["lru_lambda"], p["lru_norm_w"],
                               jnp.zeros((4, LRU_WIDTH), F32)], axis=0)
    hs, y_lru = _lru_fwd(xr, proj["g"], wa_p.astype(_MXU), wx_p.astype(_MXU),
                         lru_prm + late.mid_forward([xr, y_ssd]))
    ycat = jnp.concatenate([y_ssd, y_lru], axis=1)
    w_out = late.w_out(ycat)
    h1 = _mm([(ycat, 0, w_out, 0, 2 * D_MODEL)], T_ROWS, D_MODEL, tm=T_ROWS, tn=256, mode="nn", out_dtype=F32,
             name="out_proj", residual=h0)
    u2 = _rmsnorm(h1, p["norm2_w"], name="norm2")
    w_gate, w_up, w_down = late.ffn(u2)
    gp, up, act = _ffn_up(u2, w_gate, w_up)
    h2 = _mm([(act, 0, w_down, 0, D_FF)], T_ROWS, D_MODEL, tm=T_ROWS, tn=256, mode="nn", out_dtype=F32,
             name="ffn_down", residual=h1)
    loss, dh2, dh2b, g["final_norm_w"] = _loss_head(h2, target, p["final_norm_w"])
    dgp, dup = _ffn_bwd_act(dh2b, w_down, gp, up)
    g["w_down"], g_mxu["w_down"] = _mm([(act, 0, dh2b, 0, T_ROWS)], D_FF, D_MODEL, tm=1408, tn=512, mode="tn",
                                       out_dtype=F32, name="dw_down", also_mxu=True)
    dh1, dh1b, g["norm2_w"] = _mm_norm_bwd([(dgp, w_gate, D_FF), (dup, w_up, D_FF)], h1, p["norm2_w"], dh2,
                                           name="ffn_bwd_in")
    g["w_gate"], g_mxu["w_gate"] = _mm([(dgp, 0, u2, 0, T_ROWS)], D_FF, D_MODEL, tm=1408, tn=512, mode="tn",
                                       out_dtype=F32, name="dw_gate", also_mxu=True)
    g["w_up"], g_mxu["w_up"] = _mm([(dup, 0, u2, 0, T_ROWS)], D_FF, D_MODEL, tm=1408, tn=512, mode="tn",
                                   out_dtype=F32, name="dw_up", also_mxu=True)
    g["w_out"], g_mxu["w_out"] = _mm([(ycat, 0, dh1b, 0, T_ROWS)], 2 * D_MODEL, D_MODEL, tm=1024, tn=512, mode="tn",
                                     out_dtype=F32, name="dw_out", also_mxu=True)
    sent = late.grads_ready(("w_down", "w_gate", "w_up", "w_out"), g, g_mxu)
    dycat = _mm([(dh1b, 0, w_out, 0, D_MODEL)], T_ROWS, 2 * D_MODEL, tm=T_ROWS, tn=256, mode="nt", out_dtype=F32,
                name="out_proj_bwd", behind=(sent,))
    dgate, dxr, dwa_p, dwx_p, dlru_prm = _lru_bwd(dycat, 1, proj["g"], xr, hs, wa_p.astype(_MXU), wx_p.astype(_MXU),
                                                  jnp.swapaxes(wa_p, 1, 2).astype(_MXU),
                                                  jnp.swapaxes(wx_p, 1, 2).astype(_MXU), lru_prm)
    g["lru_wa"], g["lru_wx"] = _unpair_blocks(dwa_p), _unpair_blocks(dwx_p)
    g["lru_ba"], g["lru_bx"], g["lru_lambda"], g["lru_norm_w"] = (dlru_prm[k:k + 1] for k in range(4))
    dx_lru, g["lru_conv_w"], g["lru_conv_b"] = _conv_bwd(dxr, proj["x"], p["lru_conv_w"], p["lru_conv_b"], silu=False,
                                                         name="lru_conv_bwd")
    dz, dxs_act, dbc_act, ddt, dssd_prm, g["ssd_norm_w"] = _ssd_bwd(dycat, 0, proj["z"], y_pre, xs_act, bc_act,
                                                                    proj["dt"], prev, ssd_prm, p["ssd_norm_w"], ex)
    g["ssd_dt_bias"], g["ssd_a_log"], g["ssd_d"] = (dssd_prm[k:k + 1, :SSD_HEADS] for k in range(3))
    dxs, dcw_xs, dcb_xs = _conv_bwd(dxs_act, proj["xs"], p["ssd_conv_w"][:, :SSD_WIDTH],
                                    p["ssd_conv_b"][:, :SSD_WIDTH], silu=True, name="ssd_conv_xs_bwd")
    dbc, dcw_bc, dcb_bc = _conv_bwd(dbc_act, proj["bc"], p["ssd_conv_w"][:, SSD_WIDTH:],
                                    p["ssd_conv_b"][:, SSD_WIDTH:], silu=True, name="ssd_conv_bc_bwd")
    g["ssd_conv_w"] = jnp.concatenate([dcw_xs, dcw_bc], axis=1)
    g["ssd_conv_b"] = jnp.concatenate([dcb_xs, dcb_bc], axis=1)
    dproj = {"z": dz, "xs": dxs, "bc": dbc, "dt": ddt, "g": dgate, "x": dx_lru}
    dh0, _, g["norm1_w"] = _mm_norm_bwd([(dproj[s], w_in[s], SEC_WIDTH[s]) for s in SEC_NAMES], h0,
                                        p["norm1_w"], dh1, name="in_proj_bwd")
    g["meta_tokens"] = dh0[PAD_ROWS:X_ROW0]
    sent = late.small_ready(g, loss)
    for s in SEC_NAMES:
        wdt = SEC_WIDTH[s]
        g["w_in_" + s], g_mxu["w_in_" + s] = _mm([(dproj[s], 0, u1, 0, T_ROWS)], wdt, D_MODEL, tm=min(wdt, 1024),
                                                 tn=512, mode="tn", out_dtype=F32, name="dw_in_" + s, also_mxu=True,
                                                 behind=(sent,))
        if s == "bc":
            sent = late.small_middle([g["w_in_z"], g["w_in_xs"], g["w_in_bc"]])
    late.grads_ready(("w_in",), g, g_mxu)
    return loss, dh0[X_ROW0:], g, g_mxu


MESH = pl.DeviceIdType.MESH
ANY = pl.BlockSpec(memory_space=pl.ANY)


def _my_place():
    return lax.axis_index("x"), lax.axis_index("y"), lax.axis_index("c")


def _other_chips(x, y):
    return [(1 - x, y), (x, 1 - y), (1 - x, 1 - y)]


HBM_SPEC = pl.BlockSpec(memory_space=pltpu.HBM)
SEM_SPEC = pl.BlockSpec(memory_space=pltpu.SEMAPHORE)
SPLIT_EFFECT = pltpu.SideEffectType.DATAFLOW_SIDE_EFFECTING


def _half_cols(buf, c, other=False):
    half = buf.shape[-1] // 2
    return pl.ds(pl.multiple_of(((1 - c) if other else c) * half, 128), half)


def _halves_plan(bufs, x, y, c, incoming):
    plan = []
    for buf in bufs:
        cols = _half_cols(buf, c)
        for (px, py) in _other_chips(x, y):
            slot = 2 * px + py if incoming else 2 * x + y
            plan.append((buf.at[2 * x + y, :, cols], buf.at[slot, :, cols], (px, py, c)))
    return plan


def _forward_plan(bufs, x, y, c, incoming):
    plan = []
    for buf in bufs:
        for (px, py) in _other_chips(x, y):
            slot = 2 * px + py
            plan.append((buf.at[slot, :, _half_cols(buf, c)], buf.at[slot, :, _half_cols(buf, c, other=incoming)],
                         (x, y, 1 - c)))
    return plan


def _scatter_plan(bufs, x, y, c, incoming):
    n = len(bufs) // 2
    plan = []
    for k in range(n):
        for j, (px, py) in enumerate(_other_chips(x, y)):
            plan.append((bufs[k].at[2 * px + py], bufs[n + k].at[j], (px, py, c)))
    return plan


def _split_start(bufs, plan, n_copies, after, *, name):
    n = len(bufs)
    extra = [] if after is None else [after]

    def body(*refs):
        ins = refs[:n]
        send_sems, recv_sems = refs[n + len(extra)], refs[n + len(extra) + 1]
        token = refs[-1]
        x, y, c = _my_place()
        for i, (src, dst, dev) in enumerate(plan(ins, x, y, c, False)):
            pltpu.make_async_remote_copy(src_ref=src, dst_ref=dst, send_sem=send_sems.at[i], recv_sem=recv_sems.at[i],
                                         device_id=dev, device_id_type=MESH).start()
        token[...] = jnp.zeros_like(token)

    outs = pl.pallas_call(
        body, name=name,
        out_shape=(pltpu.SemaphoreType.DMA((n_copies,)), pltpu.SemaphoreType.DMA((n_copies,)),
                   *[pltpu.HBM(b.shape, b.dtype) for b in bufs], jax.ShapeDtypeStruct((8, 128), F32)),
        in_specs=[HBM_SPEC] * n + [ANY] * len(extra),
        out_specs=(SEM_SPEC, SEM_SPEC, *[HBM_SPEC] * n, pl.BlockSpec(memory_space=pltpu.VMEM)),
        input_output_aliases={k: 2 + k for k in range(n)},
        compiler_params=pltpu.CompilerParams(has_side_effects=SPLIT_EFFECT),
    )(*[pltpu.with_memory_space_constraint(b, pltpu.HBM) for b in bufs], *extra)
    return outs[0], outs[1], list(outs[2:2 + n]), outs[-1]


def _split_wait(bufs, send_sems, recv_sems, plan, after, *, name):
    n = len(bufs)
    after = list(after) if isinstance(after, (list, tuple)) else [after]

    def body(*refs):
        ins = refs[:n]
        send_sems_ref, recv_sems_ref = refs[n], refs[n + 1]
        x, y, c = _my_place()
        for i, (src, dst, dev) in enumerate(plan(ins, x, y, c, True)):
            cp = pltpu.make_async_remote_copy(src_ref=src, dst_ref=dst, send_sem=send_sems_ref.at[i],
                                              recv_sem=recv_sems_ref.at[i], device_id=dev, device_id_type=MESH)
            cp.wait_send()
            cp.wait_recv()

    outs = pl.pallas_call(
        body, name=name, out_shape=tuple(pltpu.HBM(b.shape, b.dtype) for b in bufs),
        in_specs=[HBM_SPEC] * n + [SEM_SPEC, SEM_SPEC] + [ANY] * len(after), out_specs=tuple([HBM_SPEC] * n),
        input_output_aliases={k: k for k in range(n)},
        compiler_params=pltpu.CompilerParams(has_side_effects=SPLIT_EFFECT),
    )(*bufs, send_sems, recv_sems, *after)
    return list(outs)


def _fill_own_slots(shards, me_arr, *, name, behind=()):
    n = len(shards)
    n_in = n + len(behind)

    def body(me_ref, *refs):
        for k in range(n):
            refs[n_in + k][0] = refs[k][...].astype(_MXU)

    half = D_MODEL // 2
    return pl.pallas_call(
        body, name=name,
        grid_spec=pltpu.PrefetchScalarGridSpec(
            num_scalar_prefetch=1, grid=(2,),
            in_specs=[pl.BlockSpec((s.shape[0], half), lambda i, me: (0, i)) for s in shards]
            + [pl.BlockSpec(memory_space=pl.ANY)] * len(behind),
            out_specs=[pl.BlockSpec((1, s.shape[0], half), lambda i, me: (me[0], 0, i)) for s in shards]),
        out_shape=[jax.ShapeDtypeStruct((N_SHARDS,) + s.shape, _MXU) for s in shards],
        compiler_params=_cparams("parallel"),
    )(me_arr, *shards, *behind)


def _gather_small(small):
    def body(s_ref, o_ref, send_sems, recv_sems, local_sem):
        x, y, c = _my_place()
        me = 2 * x + y
        local = pltpu.make_async_copy(s_ref, o_ref.at[me], local_sem)
        local.start()
        copies = [(pltpu.make_async_remote_copy(src_ref=s_ref, dst_ref=o_ref.at[me], send_sem=send_sems.at[j],
                                                recv_sem=recv_sems.at[j], device_id=(px, py, c), device_id_type=MESH),
                   2 * px + py) for j, (px, py) in enumerate(_other_chips(x, y))]
        for cp, _ in copies:
            cp.start()
        for j, (cp, slot) in enumerate(copies):
            cp.wait_send()
            pltpu.make_async_remote_copy(src_ref=s_ref, dst_ref=o_ref.at[slot], send_sem=send_sems.at[j],
                                         recv_sem=recv_sems.at[j], device_id=(x, y, c),
                                         device_id_type=MESH).wait_recv()
        local.wait()

    return pl.pallas_call(
        body, name="gather_small", in_specs=[ANY], out_specs=ANY,
        out_shape=jax.ShapeDtypeStruct((N_SHARDS,) + small.shape, small.dtype),
        scratch_shapes=[pltpu.SemaphoreType.DMA((3,)), pltpu.SemaphoreType.DMA((3,)), pltpu.SemaphoreType.DMA],
    )(small)


def _swap_with_sibling(parts, *, name):
    n = len(parts)

    def body(*refs):
        ins, outs = refs[:n], refs[n:2 * n]
        send_sems, recv_sems = refs[2 * n:]
        x, y, c = _my_place()
        copies = [pltpu.make_async_remote_copy(
            src_ref=ins[k], dst_ref=outs[k], send_sem=send_sems.at[k], recv_sem=recv_sems.at[k],
            device_id=(x, y, 1 - c), device_id_type=MESH) for k in range(n)]
        for cp in copies:
            cp.start()
        for cp in copies:
            cp.wait()

    return pl.pallas_call(
        body, name=name, in_specs=[ANY] * n, out_specs=[ANY] * n,
        out_shape=[jax.ShapeDtypeStruct(a.shape, a.dtype) for a in parts],
        scratch_shapes=[pltpu.SemaphoreType.DMA((n,)), pltpu.SemaphoreType.DMA((n,))],
    )(*parts)


def _other_devices(x, y, c):
    out = []
    for mask in range(1, N_DEV):
        px, py, pc = x ^ (mask >> 2 & 1), y ^ (mask >> 1 & 1), c ^ (mask & 1)
        out.append(((px, py, pc), 4 * px + 2 * py + pc))
    return out


def _pieces_plan(bufs, x, y, c, incoming):
    pack, land = bufs
    me = 4 * x + 2 * y + c
    return [(pack.at[num], land.at[num if incoming else me], dev) for dev, num in _other_devices(x, y, c)]


def _spread_plan(bufs, x, y, c, incoming):
    piece, land = bufs
    me = 4 * x + 2 * y + c
    return [(piece, land.at[num if incoming else me], dev) for dev, num in _other_devices(x, y, c)]


def _sum_pieces(pack, land, dev_arr, *, name):
    def body(dev_ref, pack_ref, land_ref, o_ref):
        dev = dev_ref[0]
        own = pack_ref[dev]
        acc = None
        for d in range(N_DEV):
            term = jnp.where(dev == d, own, land_ref[d])
            acc = term if acc is None else acc + term
        o_ref[...] = acc

    vmem = pl.BlockSpec(memory_space=pltpu.VMEM)
    return pl.pallas_call(
        body, name=name, in_specs=[pl.BlockSpec(memory_space=pltpu.SMEM), vmem, vmem], out_specs=vmem,
        out_shape=jax.ShapeDtypeStruct(pack.shape[1:], F32),
    )(dev_arr, pack, land)


def _join_pieces(piece, land, dev_arr, *, name):
    def body(dev_ref, piece_ref, land_ref, o_ref):
        dev = dev_ref[0]
        for d in range(N_DEV):
            o_ref[d] = jnp.where(dev == d, piece_ref[...], land_ref[d])

    vmem = pl.BlockSpec(memory_space=pltpu.VMEM)
    return pl.pallas_call(
        body, name=name, in_specs=[pl.BlockSpec(memory_space=pltpu.SMEM), vmem, vmem], out_specs=vmem,
        out_shape=jax.ShapeDtypeStruct(land.shape, F32),
    )(dev_arr, piece, land)


def _adamw_native(ws, gs, ms, vs):
    n = len(ws)

    def body(*refs):
        for k in range(n):
            w_ref, g_ref, m_ref, v_ref = (refs[j * n + k] for j in range(4))
            delta, m_new, v_new = _adamw_math(w_ref[...], g_ref[...], m_ref[...], v_ref[...])
            refs[4 * n + k][...] = delta
            refs[5 * n + k][...] = m_new
            refs[6 * n + k][...] = v_new

    vmem = pl.BlockSpec(memory_space=pltpu.VMEM)
    shapes = [jax.ShapeDtypeStruct(a.shape, F32) for a in ws]
    outs = pl.pallas_call(
        body, name="adamw_small", in_specs=[vmem] * (4 * n), out_specs=[vmem] * (3 * n), out_shape=shapes * 3,
        compiler_params=pltpu.CompilerParams(vmem_limit_bytes=VMEM_LIMIT_BYTES),
    )(*ws, *gs, *ms, *vs)
    return outs[:n], outs[n:2 * n], outs[2 * n:]


def _elementwise_tile(rows, cols):
    for t in range(256, 15, -16):
        if rows % t == 0:
            return (t, cols), rows // t, lambda i: (i, 0)
    assert cols % 256 == 0
    return (rows, 256), cols // 256, lambda i: (0, i)


def _partial_sum(own, land, me_arr, *, name):
    r, c = own.shape[-2:]
    tile, steps, imap = _elementwise_tile(r, c)
    whole = own.ndim == 3

    def body(me_ref, own_ref, land_ref, o_ref):
        acc = own_ref[0] if whole else own_ref[...]
        for j in range(3):
            acc = acc + land_ref[j].astype(F32)
        o_ref[...] = acc.astype(_MXU)

    own_spec = (pl.BlockSpec((1,) + tile, lambda i, me: (me[0],) + imap(i)) if whole
                else pl.BlockSpec(tile, lambda i, me: imap(i)))
    return pl.pallas_call(
        body, name=name,
        grid_spec=pltpu.PrefetchScalarGridSpec(
            num_scalar_prefetch=1, grid=(steps,),
            in_specs=[own_spec, pl.BlockSpec((3,) + tile, lambda i, me: (0,) + imap(i))],
            out_specs=pl.BlockSpec(tile, lambda i, me: imap(i))),
        out_shape=jax.ShapeDtypeStruct((r, c), _MXU),
        compiler_params=_cparams("parallel"),
    )(me_arr, own, land)


LANE_TILE = 256


def _partial_sums(owns, lands, me_arr, *, name):
    n = len(owns)

    def body(me_ref, *refs):
        for k in range(n):
            acc = refs[k][0]
            for j in range(3):
                acc = acc + refs[n + k][j].astype(F32)
            refs[2 * n + k][...] = acc.astype(_MXU)

    rows = [o.shape[1] for o in owns]
    return pl.pallas_call(
        body, name=name,
        grid_spec=pltpu.PrefetchScalarGridSpec(
            num_scalar_prefetch=1, grid=(D_MODEL // LANE_TILE,),
            in_specs=[pl.BlockSpec((1, r, LANE_TILE), lambda i, me: (me[0], 0, i)) for r in rows]
            + [pl.BlockSpec((3, r, LANE_TILE), lambda i, me: (0, 0, i)) for r in rows],
            out_specs=[pl.BlockSpec((r, LANE_TILE), lambda i, me: (0, i)) for r in rows]),
        out_shape=[jax.ShapeDtypeStruct((r, D_MODEL), _MXU) for r in rows],
        compiler_params=_cparams("parallel"),
    )(me_arr, *owns, *lands)


def _adamws(ws, parts_a, parts_b, ms, vs, *, name):
    n = len(ws)

    def body(*refs):
        for k in range(n):
            w_ref, a_ref, b_ref, m_ref, v_ref = (refs[j * n + k] for j in range(5))
            g = a_ref[...].astype(F32) + b_ref[...].astype(F32)
            delta, m_new, v_new = _adamw_math(w_ref[...], g, m_ref[...], v_ref[...])
            for j, val in enumerate((g, delta, m_new, v_new)):
                refs[(5 + j) * n + k][...] = val

    tiles = [pl.BlockSpec((w.shape[0], LANE_TILE), lambda i: (0, i)) for w in ws]
    outs = pl.pallas_call(
        body, name=name, grid=(D_MODEL // LANE_TILE,), in_specs=tiles * 5, out_specs=tiles * 4,
        out_shape=[jax.ShapeDtypeStruct(w.shape, F32) for w in ws] * 4,
        compiler_params=_cparams("parallel"),
    )(*ws, *parts_a, *parts_b, *ms, *vs)
    return [outs[j * n:(j + 1) * n] for j in range(4)]


def _adamw_math(w, g, m, v):
    m = ADAM_B1 * m + (1.0 - ADAM_B1) * g
    v = ADAM_B2 * v + (1.0 - ADAM_B2) * (g * g)
    m_hat = m / (1.0 - ADAM_B1 ** ADAM_STEP)
    v_hat = v / (1.0 - ADAM_B2 ** ADAM_STEP)
    delta = -ADAM_LR * (m_hat / (jnp.sqrt(v_hat) + ADAM_EPS) + ADAM_WD * w)
    return delta, m, v


def _adamw(w, grad_parts, m, v, *, name):
    if w.ndim == 3:
        steps = 4
        assert w.shape[0] % steps == 0
        tile_shape, imap = (w.shape[0] // steps,) + w.shape[1:], lambda i: (i, 0, 0)
    else:
        tile_shape, steps, imap = _elementwise_tile(*w.shape)
    n = len(grad_parts)

    def body(*refs):
        w_ref, m_ref, v_ref = refs[:3]
        g_refs = refs[3:3 + n]
        g_out, d_out, m_out, v_out = refs[3 + n:]
        g = g_refs[0][...].astype(F32)
        for k in range(1, n):
            g = g + g_refs[k][...].astype(F32)
        delta, m_new, v_new = _adamw_math(w_ref[...], g, m_ref[...], v_ref[...])
        g_out[...] = g
        d_out[...] = delta
        m_out[...] = m_new
        v_out[...] = v_new

    tile = pl.BlockSpec(tile_shape, imap)
    return pl.pallas_call(
        body, name=name, grid=(steps,), in_specs=[tile] * (3 + n), out_specs=[tile] * 4,
        out_shape=[jax.ShapeDtypeStruct(w.shape, F32)] * 4,
        compiler_params=_cparams("parallel"),
    )(w, m, v, *grad_parts)


WEIGHT_NAMES = ("meta_tokens", "norm1_w", "w_in", "ssd_conv_w", "ssd_conv_b", "ssd_dt_bias", "ssd_a_log", "ssd_d",
                "ssd_norm_w", "lru_conv_w", "lru_conv_b", "lru_wa", "lru_ba", "lru_wx", "lru_bx", "lru_lambda",
                "lru_norm_w", "w_out", "norm2_w", "w_gate", "w_up", "w_down", "final_norm_w")
BIG = ("w_in", "w_out", "w_gate", "w_up", "w_down")
FFN = ("w_gate", "w_up", "w_down")
LATE = ("w_out",) + FFN
SMALL_SHARDED = {"meta_tokens": (N_META, D_MODEL), "ssd_conv_w": (CONV_K, 1536), "lru_conv_w": (CONV_K, LRU_WIDTH)}
SMALL = tuple(n for n in WEIGHT_NAMES if n not in BIG)
PACK_COLS = 1024


def _pack(arrays, row_multiple):
    flat = jnp.concatenate([a.reshape(-1) for a in arrays])
    rows = -(-flat.shape[0] // (row_multiple * PACK_COLS)) * row_multiple
    return jnp.pad(flat, (0, rows * PACK_COLS - flat.shape[0])).reshape(rows, PACK_COLS)


def _unpack(pack, shapes):
    flat = pack.reshape(-1)
    out, off = [], 0
    for s in shapes:
        size = math.prod(s)
        out.append(flat[off:off + size].reshape(s))
        off += size
    return out


def _unshard_cols(g4):
    return jnp.swapaxes(g4, 0, 1).reshape(g4.shape[1], -1)


COL_SHARDED = ("w_in", "w_gate", "w_up")
IN_ROWS = {"z": (0, 1024), "xs": (1024, 2048), "bc": (2048, 2560), "dt": (2560, 2576), "g": (2576, 3600),
           "x": (3600, IN_COLS)}


def _rows_of_shards(shards4, lo, hi):
    r = shards4.shape[1]
    parts = [shards4[k, max(lo, k * r) - k * r:min(hi, (k + 1) * r) - k * r]
             for k in range(N_SHARDS) if max(lo, k * r) < min(hi, (k + 1) * r)]
    return parts[0] if len(parts) == 1 else jnp.concatenate(parts, axis=0)


def _w_in_shard_rows(k, sections):
    lo, hi = k * (IN_COLS // N_SHARDS), (k + 1) * (IN_COLS // N_SHARDS)
    parts = []
    for arr, (a, b) in zip(sections, IN_ROWS.values()):
        if max(lo, a) < min(hi, b):
            parts.append(arr[max(lo, a) - a:min(hi, b) - a])
    return jnp.concatenate(parts, axis=0)


def _rows_view(name, block):
    return jnp.swapaxes(block[0], 0, 1) if name in COL_SHARDED else block[0]


def _param_view(name, rows):
    return (jnp.swapaxes(rows, 0, 1) if name in COL_SHARDED else rows)[None]


def kernel(x, meta_tokens, norm1_w, w_in, ssd_conv_w, ssd_conv_b, ssd_dt_bias, ssd_a_log, ssd_d, ssd_norm_w, lru_conv_w, lru_conv_b, lru_wa, lru_ba, lru_wx, lru_bx, lru_lambda, lru_norm_w, w_out, norm2_w, w_gate, w_up, w_down, final_norm_w, loss_target, m_meta_tokens, m_norm1_w, m_w_in, m_ssd_conv_w, m_ssd_conv_b, m_ssd_dt_bias, m_ssd_a_log, m_ssd_d, m_ssd_norm_w, m_lru_conv_w, m_lru_conv_b, m_lru_wa, m_lru_ba, m_lru_wx, m_lru_bx, m_lru_lambda, m_lru_norm_w, m_w_out, m_norm2_w, m_w_gate, m_w_up, m_w_down, m_final_norm_w, v_meta_tokens, v_norm1_w, v_w_in, v_ssd_conv_w, v_ssd_conv_b, v_ssd_dt_bias, v_ssd_a_log, v_ssd_d, v_ssd_norm_w, v_lru_conv_w, v_lru_conv_b, v_lru_wa, v_lru_ba, v_lru_wx, v_lru_bx, v_lru_lambda, v_lru_norm_w, v_w_out, v_norm2_w, v_w_gate, v_w_up, v_w_down, v_final_norm_w):
    w = dict(zip(WEIGHT_NAMES, (meta_tokens, norm1_w, w_in, ssd_conv_w, ssd_conv_b, ssd_dt_bias, ssd_a_log, ssd_d, ssd_norm_w, lru_conv_w, lru_conv_b, lru_wa, lru_ba, lru_wx, lru_bx, lru_lambda, lru_norm_w, w_out, norm2_w, w_gate, w_up, w_down, final_norm_w)))
    m = dict(zip(WEIGHT_NAMES, (m_meta_tokens, m_norm1_w, m_w_in, m_ssd_conv_w, m_ssd_conv_b, m_ssd_dt_bias, m_ssd_a_log, m_ssd_d, m_ssd_norm_w, m_lru_conv_w, m_lru_conv_b, m_lru_wa, m_lru_ba, m_lru_wx, m_lru_bx, m_lru_lambda, m_lru_norm_w, m_w_out, m_norm2_w, m_w_gate, m_w_up, m_w_down, m_final_norm_w)))
    v = dict(zip(WEIGHT_NAMES, (v_meta_tokens, v_norm1_w, v_w_in, v_ssd_conv_w, v_ssd_conv_b, v_ssd_dt_bias, v_ssd_a_log, v_ssd_d, v_ssd_norm_w, v_lru_conv_w, v_lru_conv_b, v_lru_wa, v_lru_ba, v_lru_wx, v_lru_bx, v_lru_lambda, v_lru_norm_w, v_w_out, v_norm2_w, v_w_gate, v_w_up, v_w_down, v_final_norm_w)))
    me = 2 * lax.axis_index("x") + lax.axis_index("y")

    big2d = {n: _rows_view(n, w[n]) for n in BIG}
    small_local = jnp.concatenate([w["meta_tokens"].reshape(-1), w["ssd_conv_w"].reshape(-1),
                                   w["lru_conv_w"].reshape(-1)])[None]
    me_arr = me.astype(jnp.int32).reshape(1)
    dev_arr = (2 * me + lax.axis_index("c")).astype(jnp.int32).reshape(1)
    small4 = _gather_small(small_local)
    (w_in_slot,) = _fill_own_slots([big2d["w_in"]], me_arr, name="own_slot_w_in")
    in_send, in_recv, in_bufs, in_tok = _split_start([w_in_slot], _halves_plan, 3, small4, name="gather_w_in_start")
    late_slots = _fill_own_slots([big2d[n] for n in LATE], me_arr, name="own_slots_late", behind=(in_tok,))
    sm = small4[:, 0]
    meta_full = _unshard_cols(sm[:, :4096].reshape(N_SHARDS, N_META, 256))
    ssd_conv_w_full = _unshard_cols(sm[:, 4096:5632].reshape(N_SHARDS, CONV_K, 384))
    lru_conv_w_full = _unshard_cols(sm[:, 5632:].reshape(N_SHARDS, CONV_K, 256))

    p = {"ssd_conv_w": ssd_conv_w_full, "lru_conv_w": lru_conv_w_full,
         "lru_wa": w["lru_wa"][0], "lru_wx": w["lru_wx"][0], "final_norm_w": w["final_norm_w"][None]}
    for n in ("norm1_w", "ssd_conv_b", "ssd_dt_bias", "ssd_a_log", "ssd_d", "ssd_norm_w", "lru_conv_b", "lru_ba",
              "lru_bx", "lru_lambda", "lru_norm_w", "norm2_w"):
        p[n] = w[n]

    class Late:
        def __init__(self):
            self.pending = []
            self.before_embed = (late_slots[0],)

        def w_in(self, after):
            (buf,) = _split_wait(in_bufs, in_send, in_recv, _halves_plan, after, name="gather_w_in_wait")
            send, recv, bufs, tok = _split_start([buf], _forward_plan, 3, None, name="forward_w_in_start")
            self.late_gather = _split_start(late_slots, _halves_plan, 3 * len(LATE), tok, name="gather_late_start")
            (w_in4,) = _split_wait(bufs, send, recv, _forward_plan, self.late_gather[2][0], name="forward_w_in_wait")
            sections = {s: _rows_of_shards(w_in4, lo, hi) for s, (lo, hi) in IN_ROWS.items()}
            sections["dt"] = jnp.pad(sections["dt"], ((0, SEC_WIDTH["dt"] - SSD_HEADS), (0, 0)))
            return sections

        def mid_forward(self, after):
            send, recv, bufs, _ = self.late_gather
            bufs = _split_wait(bufs, send, recv, _halves_plan, after, name="gather_late_wait")
            self.forward = _split_start(bufs, _forward_plan, 3 * len(LATE), None, name="forward_late_start")
            return self.forward[3][:1, :1]

        def w_out(self, after):
            send, recv, bufs, _ = self.forward
            bufs = _split_wait(bufs, send, recv, _forward_plan, after, name="forward_late_wait")
            self.late = dict(zip(LATE, (b.reshape(-1, D_MODEL) for b in bufs)))
            return self.late["w_out"]

        def ffn(self, after):
            return tuple(self.late[n] for n in FFN)

        def grads_ready(self, names, g, g_mxu):
            if names == ("w_in",):
                g_mxu["w_in"] = jnp.stack([_w_in_shard_rows(k, [g_mxu["w_in_" + s] for s in SEC_NAMES])
                                           for k in range(N_SHARDS)])
            srcs = [g_mxu[n].reshape(N_SHARDS, -1, D_MODEL) for n in names]
            lands = [lax.empty((3,) + s.shape[1:], _MXU) for s in srcs]
            tag = "_".join(names)
            send, recv, bufs, tok = _split_start(srcs + lands, _scatter_plan, 3 * len(names), None,
                                                 name="scatter_" + tag + "_start")
            self.pending.append((names, send, recv, bufs, tag))
            self.in_flight = bufs[0]
            return tok[:1, :1]

        def landed(self, after, which):
            land = {}
            for names, send, recv, bufs, tag in self.pending:
                if names[0] in which:
                    bufs = _split_wait(bufs, send, recv, _scatter_plan, after, name="scatter_" + tag + "_wait")
                    land.update(zip(names, bufs[len(names):]))
            return land

        def small_ready(self, g, loss):
            pack = _pack([g[n] for n in SMALL] + [loss[0, :1]], 8 * N_DEV)
            pack = pack.reshape(N_DEV, -1, PACK_COLS)
            self.small = _split_start([pack, lax.empty(pack.shape, F32)], _pieces_plan, N_DEV - 1, loss,
                                      name="small_pieces_start")
            return self.small[3]

        def small_middle(self, after):
            send, recv, bufs, _ = self.small
            pack, land = _split_wait(bufs, send, recv, _pieces_plan, after, name="small_pieces_wait")
            piece = _sum_pieces(pack, land, dev_arr, name="small_pieces_sum")
            self.small = _split_start([piece, lax.empty(pack.shape, F32)], _spread_plan, N_DEV - 1, None,
                                      name="small_spread_start")
            return self.small[3]

        def small_sum(self, after):
            send, recv, bufs, _ = self.small
            piece, land = _split_wait(bufs, send, recv, _spread_plan, after, name="small_spread_wait")
            return _join_pieces(piece, land, dev_arr, name="small_join")

    late = Late()

    loss, grad_x, g, g_mxu = _local_step(x[0], loss_target[0], meta_full, p, late)

    g4 = {n: g[n].reshape(N_SHARDS, -1, D_MODEL) for n in LATE}
    g4["w_in"] = lax.switch(me, [functools.partial(_w_in_shard_rows, k) for k in range(N_SHARDS)],
                            [g["w_in_" + s] for s in SEC_NAMES])
    land = late.landed(late.in_flight, LATE)
    part = dict(zip(LATE, _partial_sums([g4[n] for n in LATE], [land[n] for n in LATE], me_arr,
                                        name="partial_late")))
    sib = dict(zip(LATE, _swap_with_sibling([part[n] for n in LATE], name="swap_late")))

    small_full_shape = {n: (SMALL_SHARDED[n] if n in SMALL_SHARDED else w[n].shape) for n in SMALL}
    red_list = _unpack(late.small_sum(sib["w_out"]), [small_full_shape[n] for n in SMALL] + [(1,)])
    loss_total = red_list[-1][0]
    g_small = {}
    for n, arr in zip(SMALL, red_list[:-1]):
        if n in SMALL_SHARDED:
            cols = SMALL_SHARDED[n][1] // N_SHARDS
            arr = lax.dynamic_slice_in_dim(arr, me * cols, cols, axis=1)
        g_small[n] = arr.reshape(w[n].shape)

    grad, delta, new_m, new_v = {}, {}, {}, {}

    two_d = lambda a: a.reshape(1, -1) if a.ndim == 1 else a
    deltas, new_ms, new_vs = _adamw_native(*[[two_d(d[n]) for n in SMALL] for d in (w, g_small, m, v)])
    for n, dn, mn, vn in zip(SMALL, deltas, new_ms, new_vs):
        grad[n], delta[n], new_m[n], new_v[n] = (g_small[n], dn.reshape(w[n].shape), mn.reshape(w[n].shape),
                                                 vn.reshape(w[n].shape))
    late_outs = _adamws([big2d[n] for n in LATE], [part[n] for n in LATE], [sib[n] for n in LATE],
                        [_rows_view(n, m[n]) for n in LATE], [_rows_view(n, v[n]) for n in LATE], name="adamw_late")
    for d, outs in zip((grad, delta, new_m, new_v), late_outs):
        d.update({n: _param_view(n, o) for n, o in zip(LATE, outs)})

    land.update(late.landed([late_outs[0][0], deltas[0]], ("w_in",)))
    part_in = _partial_sum(g4["w_in"], land["w_in"], me_arr, name="partial_w_in")
    (sib_in,) = _swap_with_sibling([part_in], name="swap_w_in")
    lanes = lambda a: a[0].reshape(8, 128, -1).transpose(2, 0, 1)
    pieces = lambda a: a.reshape(-1, 8, 128)
    outs = _adamw(lanes(w["w_in"]), [pieces(part_in), pieces(sib_in)], lanes(m["w_in"]), lanes(v["w_in"]),
                  name="adamw_w_in")
    grad["w_in"], delta["w_in"], new_m["w_in"], new_v["w_in"] = (o.transpose(1, 2, 0).reshape(1, D_MODEL, -1)
                                                                 for o in outs)

    return (loss_total, grad_x[None], *[grad[n] for n in WEIGHT_NAMES], *[delta[n] for n in WEIGHT_NAMES],
            *[new_m[n] for n in WEIGHT_NAMES], *[new_v[n] for n in WEIGHT_NAMES])
```

```python
import functools
import math

import jax
import jax.numpy as jnp
from jax import lax
from jax.experimental import pallas as pl
from jax.experimental.pallas import tpu as pltpu

F32 = jnp.float32
_MXU = jnp.bfloat16

D_MODEL = 1024
SEQ = 2048
N_META = 16
CHUNK = 128
T_ROWS = 2176
N_CHUNKS = T_ROWS // CHUNK
PAD_ROWS = T_ROWS - SEQ - N_META
X_ROW0 = PAD_ROWS + N_META
SSD_HEADS = 16
SSD_HEAD_DIM = 64
SSD_STATE = 128
SSD_GROUPS = 2
SSD_HPG = SSD_HEADS // SSD_GROUPS
SSD_WIDTH = 1024
LRU_WIDTH = 1024
LRU_C = 8.0
D_FF = 2816
EPS = 1e-6
IN_COLS = 4624
N_SHARDS = 4
N_DEV = 8

ADAM_LR = 0.001
ADAM_B1 = 0.9
ADAM_B2 = 0.999
ADAM_EPS = 1e-08
ADAM_WD = 0.01
ADAM_STEP = 10

VMEM_LIMIT_BYTES = 56 * 1024 * 1024

NN = (((1,), (0,)), ((), ()))
NT = (((1,), (1,)), ((), ()))
TN = (((0,), (0,)), ((), ()))


def _cparams(*sem):
    return pltpu.CompilerParams(dimension_semantics=sem, vmem_limit_bytes=VMEM_LIMIT_BYTES)


def _dot(a, b, dims=NN):
    return lax.dot_general(a.astype(_MXU), b.astype(_MXU), dims, preferred_element_type=F32)


def _dot_onehot(a, b, dims=NN, *, data=0, pieces=3):
    ops = [a, b]
    mask = ops[1 - data].astype(jnp.bfloat16)
    rest = ops[data]
    acc = None
    for _ in range(pieces):
        piece = rest.astype(jnp.bfloat16)
        ops[data], ops[1 - data] = piece, mask
        d = lax.dot_general(ops[0], ops[1], dims, preferred_element_type=F32)
        acc = d if acc is None else acc + d
        rest = rest - piece.astype(F32)
    return acc


def _sigmoid(x):
    return 0.5 * (1.0 + jnp.tanh(0.5 * x))


def _softplus(x):
    return jnp.maximum(x, 0.0) + jnp.log(1.0 + jnp.exp(-jnp.abs(x)))


def _silu(x):
    return x * _sigmoid(x)


def _silu_grad(x):
    s = _sigmoid(x)
    return s * (1.0 + x * (1.0 - s))


_GELU_C = math.sqrt(2.0 / math.pi)


def _gelu_and_grad(x):
    inner = _GELU_C * (x + 0.044715 * x * x * x)
    t = jnp.tanh(inner)
    g = 0.5 * x * (1.0 + t)
    dg = 0.5 * (1.0 + t) + 0.5 * x * (1.0 - t * t) * _GELU_C * (1.0 + 3.0 * 0.044715 * x * x)
    return g, dg


def _rms_fwd(x, w):
    rstd = lax.rsqrt(jnp.mean(x * x, axis=-1, keepdims=True) + EPS)
    return x * rstd * w


def _rms_bwd(x, w, dy):
    rstd = lax.rsqrt(jnp.mean(x * x, axis=-1, keepdims=True) + EPS)
    xhat = x * rstd
    dxhat = dy * w
    dx = rstd * (dxhat - xhat * jnp.mean(dxhat * xhat, axis=-1, keepdims=True))
    return dx, dy * xhat


def _mm(terms, m, n, *, tm, tn, mode, out_dtype, name, residual=None, n_outer=False, also_mxu=False, behind=()):
    gm, gn = m // tm, n // tn
    assert gm * tm == m and gn * tn == n
    if n_outer:
        grid = (gn, gm)
        mi = lambda g0, g1: g1
        ni = lambda g0, g1: g0
    else:
        grid = (gm, gn)
        mi = lambda g0, g1: g0
        ni = lambda g0, g1: g1
    in_specs, args = [], []
    for (a, ka, b, kb, k) in terms:
        if mode == "tn":
            in_specs.append(pl.BlockSpec((k, tm), lambda g0, g1, ka=ka: (ka, mi(g0, g1))))
        else:
            in_specs.append(pl.BlockSpec((tm, k), lambda g0, g1, ka=ka: (mi(g0, g1), ka)))
        if mode == "nt":
            in_specs.append(pl.BlockSpec((tn, k), lambda g0, g1, kb=kb: (ni(g0, g1), kb)))
        else:
            in_specs.append(pl.BlockSpec((k, tn), lambda g0, g1, kb=kb: (kb, ni(g0, g1))))
        args += [a, b]
    if residual is not None:
        in_specs.append(pl.BlockSpec((tm, tn), lambda g0, g1: (mi(g0, g1), ni(g0, g1))))
        args.append(residual)
    dims = {"nn": NN, "nt": NT, "tn": TN}[mode]
    n_terms = len(terms)
    has_res = residual is not None
    in_specs += [pl.BlockSpec(memory_space=pl.ANY)] * len(behind)
    args += list(behind)
    n_in = len(args)

    def body(*refs):
        acc = None
        for t in range(n_terms):
            d = lax.dot_general(refs[2 * t][...], refs[2 * t + 1][...], dims, preferred_element_type=F32)
            acc = d if acc is None else acc + d
        if has_res:
            acc = acc + refs[2 * n_terms][...]
        refs[n_in][...] = acc.astype(out_dtype)
        if also_mxu:
            refs[n_in + 1][...] = acc.astype(_MXU)

    tile = pl.BlockSpec((tm, tn), lambda g0, g1: (mi(g0, g1), ni(g0, g1)))
    shape = jax.ShapeDtypeStruct((m, n), out_dtype)
    return pl.pallas_call(
        body, name=name, grid=grid, in_specs=in_specs,
        out_specs=[tile, tile] if also_mxu else tile,
        out_shape=[shape, jax.ShapeDtypeStruct((m, n), _MXU)] if also_mxu else shape,
        compiler_params=_cparams("parallel", "parallel"),
    )(*args)


def _embed(x, meta, behind=()):
    def body(x_ref, meta_ref, *rest):
        o_ref = rest[-1]
        i = pl.program_id(0)

        @pl.when(i == 0)
        def _():
            o_ref[0:PAD_ROWS, :] = jnp.zeros((PAD_ROWS, D_MODEL), F32)
            o_ref[PAD_ROWS:CHUNK, :] = meta_ref[...]

        @pl.when(i > 0)
        def _():
            o_ref[...] = x_ref[...]

    return pl.pallas_call(
        body, name="embed", grid=(N_CHUNKS,),
        in_specs=[pl.BlockSpec((CHUNK, D_MODEL), lambda i: (jnp.maximum(i - 1, 0), 0)),
                  pl.BlockSpec((N_META, D_MODEL), lambda i: (0, 0))] + [pl.BlockSpec(memory_space=pl.ANY)] * len(behind),
        out_specs=pl.BlockSpec((CHUNK, D_MODEL), lambda i: (i, 0)),
        out_shape=jax.ShapeDtypeStruct((T_ROWS, D_MODEL), F32),
        compiler_params=_cparams("parallel"),
    )(x, meta, *behind)


def _rmsnorm(h, w, *, name, tm=544):
    def body(h_ref, w_ref, o_ref):
        o_ref[...] = _rms_fwd(h_ref[...], w_ref[...]).astype(_MXU)

    return pl.pallas_call(
        body, name=name, grid=(T_ROWS // tm,),
        in_specs=[pl.BlockSpec((tm, D_MODEL), lambda i: (i, 0)), pl.BlockSpec((1, D_MODEL), lambda i: (0, 0))],
        out_specs=pl.BlockSpec((tm, D_MODEL), lambda i: (i, 0)),
        out_shape=jax.ShapeDtypeStruct((T_ROWS, D_MODEL), _MXU),
        compiler_params=_cparams("parallel"),
    )(h, w)


def _norm_proj(h, w, sections, *, name, tm=544):
    widths = [s.shape[0] for s in sections]
    n = len(sections)

    def body(*refs):
        h_ref, w_ref = refs[:2]
        u_ref = refs[2 + n]
        u = _rms_fwd(h_ref[...], w_ref[...]).astype(_MXU)
        u_ref[...] = u
        for k in range(n):
            refs[3 + n + k][...] = lax.dot_general(u, refs[2 + k][...], NT, preferred_element_type=F32)

    row = lambda width: pl.BlockSpec((tm, width), lambda i: (i, 0))
    outs = pl.pallas_call(
        body, name=name, grid=(T_ROWS // tm,),
        in_specs=[row(D_MODEL), pl.BlockSpec((1, D_MODEL), lambda i: (0, 0))]
        + [pl.BlockSpec((wd, D_MODEL), lambda i: (0, 0)) for wd in widths],
        out_specs=[row(D_MODEL)] + [row(wd) for wd in widths],
        out_shape=[jax.ShapeDtypeStruct((T_ROWS, D_MODEL), _MXU)]
        + [jax.ShapeDtypeStruct((T_ROWS, wd), F32) for wd in widths],
        compiler_params=_cparams("parallel"),
    )(h, w, *sections)
    return outs[0], list(outs[1:])


def _loss_head(h2, target, fw):
    def body(h_ref, t_ref, w_ref, loss_ref, dh_ref, dhb_ref, dw_ref, acc_ref):
        i = pl.program_id(0)

        @pl.when(i == 0)
        def _():
            acc_ref[...] = jnp.zeros_like(acc_ref)
            dw_ref[...] = jnp.zeros_like(dw_ref)

        h = h_ref[...]
        w = w_ref[...]
        y = _rms_fwd(h, w)
        live = (i > 0).astype(F32)
        err = (y - t_ref[...]) * live
        acc_ref[...] += jnp.sum(err * err, axis=0, keepdims=True)
        dy = err * (1.0 / D_MODEL)
        dx, dwr = _rms_bwd(h, w, dy)
        dh_ref[...] = dx
        dhb_ref[...] = dx.astype(_MXU)
        dw_ref[...] += jnp.sum(dwr, axis=0, keepdims=True)

        @pl.when(i == N_CHUNKS - 1)
        def _():
            tot = jnp.sum(acc_ref[...], axis=1, keepdims=True) * (0.5 / D_MODEL)
            loss_ref[...] = jnp.broadcast_to(tot, (1, 128))

    return pl.pallas_call(
        body, name="loss_head", grid=(N_CHUNKS,),
        in_specs=[pl.BlockSpec((CHUNK, D_MODEL), lambda i: (i, 0)),
                  pl.BlockSpec((CHUNK, D_MODEL), lambda i: (jnp.maximum(i - 1, 0), 0)),
                  pl.BlockSpec((1, D_MODEL), lambda i: (0, 0))],
        out_specs=[pl.BlockSpec((1, 128), lambda i: (0, 0)),
                   pl.BlockSpec((CHUNK, D_MODEL), lambda i: (i, 0)),
                   pl.BlockSpec((CHUNK, D_MODEL), lambda i: (i, 0)),
                   pl.BlockSpec((1, D_MODEL), lambda i: (0, 0))],
        out_shape=[jax.ShapeDtypeStruct((1, 128), F32),
                   jax.ShapeDtypeStruct((T_ROWS, D_MODEL), F32),
                   jax.ShapeDtypeStruct((T_ROWS, D_MODEL), _MXU),
                   jax.ShapeDtypeStruct((1, D_MODEL), F32)],
        scratch_shapes=[pltpu.VMEM((1, D_MODEL), F32)],
        compiler_params=_cparams("arbitrary"),
    )(h2, target, fw)


def _mm_norm_bwd(terms, h, w, dres, *, name, tm=272, behind=()):
    n_terms = len(terms)
    in_specs, args = [], []
    for (a, b, k) in terms:
        in_specs += [pl.BlockSpec((tm, k), lambda i: (i, 0)), pl.BlockSpec((k, D_MODEL), lambda i: (0, 0))]
        args += [a, b]
    in_specs += [pl.BlockSpec((tm, D_MODEL), lambda i: (i, 0)), pl.BlockSpec((1, D_MODEL), lambda i: (0, 0)),
                 pl.BlockSpec((tm, D_MODEL), lambda i: (i, 0))] + [pl.BlockSpec(memory_space=pl.ANY)] * len(behind)
    args += [h, w, dres, *behind]

    def body(*refs):
        h_ref, w_ref, dres_ref = refs[2 * n_terms:2 * n_terms + 3]
        dh_ref, dhb_ref, dw_ref = refs[2 * n_terms + 3 + len(behind):]

        @pl.when(pl.program_id(0) == 0)
        def _():
            dw_ref[...] = jnp.zeros_like(dw_ref)

        du = None
        for t in range(n_terms):
            d = lax.dot_general(refs[2 * t][...], refs[2 * t + 1][...], NN, preferred_element_type=F32)
            du = d if du is None else du + d
        dx, dwr = _rms_bwd(h_ref[...], w_ref[...], du)
        dh = dres_ref[...] + dx
        dh_ref[...] = dh
        dhb_ref[...] = dh.astype(_MXU)
        dw_ref[...] += jnp.sum(dwr, axis=0, keepdims=True)

    return pl.pallas_call(
        body, name=name, grid=(T_ROWS // tm,), in_specs=in_specs,
        out_specs=[pl.BlockSpec((tm, D_MODEL), lambda i: (i, 0)), pl.BlockSpec((tm, D_MODEL), lambda i: (i, 0)),
                   pl.BlockSpec((1, D_MODEL), lambda i: (0, 0))],
        out_shape=[jax.ShapeDtypeStruct((T_ROWS, D_MODEL), F32), jax.ShapeDtypeStruct((T_ROWS, D_MODEL), _MXU),
                   jax.ShapeDtypeStruct((1, D_MODEL), F32)],
        compiler_params=_cparams("arbitrary"),
    )(*args)


FFN_TM = T_ROWS
FFN_TN = 256


def _ffn_up(u2, wg_t, wu_t):
    def body(u_ref, wg_ref, wu_ref, gp_ref, up_ref, act_ref):
        u = u_ref[...]
        gp = lax.dot_general(u, wg_ref[...], NT, preferred_element_type=F32)
        up = lax.dot_general(u, wu_ref[...], NT, preferred_element_type=F32)
        gp_ref[...] = gp.astype(_MXU)
        up_ref[...] = up.astype(_MXU)
        act_ref[...] = (_silu(gp) * up).astype(_MXU)

    tile = pl.BlockSpec((FFN_TM, FFN_TN), lambda j, i: (i, j))
    return pl.pallas_call(
        body, name="ffn_up", grid=(D_FF // FFN_TN, T_ROWS // FFN_TM),
        in_specs=[pl.BlockSpec((FFN_TM, D_MODEL), lambda j, i: (i, 0)),
                  pl.BlockSpec((FFN_TN, D_MODEL), lambda j, i: (j, 0)),
                  pl.BlockSpec((FFN_TN, D_MODEL), lambda j, i: (j, 0))],
        out_specs=[tile, tile, tile],
        out_shape=[jax.ShapeDtypeStruct((T_ROWS, D_FF), _MXU)] * 3,
        compiler_params=_cparams("parallel", "parallel"),
    )(u2, wg_t, wu_t)


def _ffn_bwd_act(dh2b, wd, gp, up):
    def body(dh_ref, wd_ref, gp_ref, up_ref, dgp_ref, dup_ref):
        dact = lax.dot_general(dh_ref[...], wd_ref[...], NT, preferred_element_type=F32)
        gp = gp_ref[...].astype(F32)
        dgp_ref[...] = (dact * up_ref[...].astype(F32) * _silu_grad(gp)).astype(_MXU)
        dup_ref[...] = (dact * _silu(gp)).astype(_MXU)

    tile = pl.BlockSpec((FFN_TM, FFN_TN), lambda j, i: (i, j))
    return pl.pallas_call(
        body, name="ffn_bwd_act", grid=(D_FF // FFN_TN, T_ROWS // FFN_TM),
        in_specs=[pl.BlockSpec((FFN_TM, D_MODEL), lambda j, i: (i, 0)),
                  pl.BlockSpec((FFN_TN, D_MODEL), lambda j, i: (j, 0)), tile, tile],
        out_specs=[tile, tile],
        out_shape=[jax.ShapeDtypeStruct((T_ROWS, D_FF), _MXU), jax.ShapeDtypeStruct((T_ROWS, D_FF), _MXU)],
        compiler_params=_cparams("parallel", "parallel"),
    )(dh2b, wd, gp, up)


CONV_TC = 512
CONV_K = 4


def _conv_pre(x_ref, wv, bv, c):
    tc = wv.shape[1]
    r0 = c * CHUNK
    cur = x_ref[r0:r0 + CHUNK, :]
    if c == 0:
        cat = jnp.concatenate([jnp.zeros((8, tc), F32), cur], axis=0)
        shifted = [cur] + [pltpu.roll(cat, s, 0)[8:8 + CHUNK] for s in range(1, CONV_K)]
    else:
        shifted = [cur] + [x_ref[r0 - s:r0 - s + CHUNK, :] for s in range(1, CONV_K)]
    pre = bv
    for s in range(CONV_K):
        pre = pre + shifted[s] * wv[CONV_K - 1 - s:CONV_K - s]
    return pre, shifted


def _row_mask(c):
    if c > 0:
        return None
    return (lax.broadcasted_iota(jnp.int32, (CHUNK, 1), 0) >= PAD_ROWS).astype(F32)


def _conv_fwd(x, w, b, *, silu, name):
    cols = x.shape[1]
    tc = min(CONV_TC, cols)

    def body(x_ref, w_ref, b_ref, o_ref):
        wv, bv = w_ref[...], b_ref[...]
        for c in range(N_CHUNKS):
            pre, _ = _conv_pre(x_ref, wv, bv, c)
            y = _silu(pre) if silu else pre
            mask = _row_mask(c)
            if mask is not None:
                y = y * mask
            o_ref[c * CHUNK:(c + 1) * CHUNK, :] = y

    return pl.pallas_call(
        body, name=name, grid=(cols // tc,),
        in_specs=[pl.BlockSpec((T_ROWS, tc), lambda j: (0, j)), pl.BlockSpec((CONV_K, tc), lambda j: (0, j)),
                  pl.BlockSpec((1, tc), lambda j: (0, j))],
        out_specs=pl.BlockSpec((T_ROWS, tc), lambda j: (0, j)),
        out_shape=jax.ShapeDtypeStruct((T_ROWS, cols), F32),
        compiler_params=_cparams("parallel"),
    )(x, w, b)


def _conv_bwd(dy, x, w, b, *, silu, name):
    cols = x.shape[1]
    tc = min(CONV_TC, cols)

    def body(dy_ref, x_ref, w_ref, b_ref, dx_ref, dw_ref, db_ref):
        wv, bv = w_ref[...], b_ref[...]
        next8 = jnp.zeros((8, tc), F32)
        dws = [jnp.zeros((1, tc), F32) for _ in range(CONV_K)]
        db = jnp.zeros((1, tc), F32)
        for c in reversed(range(N_CHUNKS)):
            r0 = c * CHUNK
            pre, shifted = _conv_pre(x_ref, wv, bv, c)
            dpre = dy_ref[r0:r0 + CHUNK, :]
            if silu:
                dpre = dpre * _silu_grad(pre)
            mask = _row_mask(c)
            if mask is not None:
                dpre = dpre * mask
            cat = jnp.concatenate([dpre, next8], axis=0)
            dx = dpre * wv[CONV_K - 1:CONV_K]
            for s in range(1, CONV_K):
                dx = dx + pltpu.roll(cat, CHUNK + 8 - s, 0)[0:CHUNK] * wv[CONV_K - 1 - s:CONV_K - s]
            dx_ref[r0:r0 + CHUNK, :] = dx.astype(_MXU)
            for s in range(CONV_K):
                k = CONV_K - 1 - s
                dws[k] = dws[k] + jnp.sum(dpre * shifted[s], axis=0, keepdims=True)
            db = db + jnp.sum(dpre, axis=0, keepdims=True)
            next8 = dpre[0:8]
        dw_ref[...] = jnp.concatenate(dws, axis=0)
        db_ref[...] = db

    return pl.pallas_call(
        body, name=name, grid=(cols // tc,),
        in_specs=[pl.BlockSpec((T_ROWS, tc), lambda j: (0, j)), pl.BlockSpec((T_ROWS, tc), lambda j: (0, j)),
                  pl.BlockSpec((CONV_K, tc), lambda j: (0, j)), pl.BlockSpec((1, tc), lambda j: (0, j))],
        out_specs=[pl.BlockSpec((T_ROWS, tc), lambda j: (0, j)), pl.BlockSpec((CONV_K, tc), lambda j: (0, j)),
                   pl.BlockSpec((1, tc), lambda j: (0, j))],
        out_shape=[jax.ShapeDtypeStruct((T_ROWS, cols), _MXU), jax.ShapeDtypeStruct((CONV_K, cols), F32),
                   jax.ShapeDtypeStruct((1, cols), F32)],
        compiler_params=_cparams("parallel"),
    )(dy, x, w, b)


def _ssd_chunk_common(dt_raw, prm, c):
    a_row = -jnp.exp(prm[1:2])
    dt = _softplus(dt_raw + prm[0:1])
    rows = lax.broadcasted_iota(jnp.int32, (CHUNK, 1), 0)
    real = jnp.logical_or(c > 0, rows >= PAD_ROWS)
    dt = jnp.where(real, dt, 0.0)
    li = lax.broadcasted_iota(jnp.int32, (CHUNK, CHUNK), 0)
    si = lax.broadcasted_iota(jnp.int32, (CHUNK, CHUNK), 1)
    causal = li >= si
    tri = causal.astype(F32)
    cs = _dot_onehot(tri, dt * a_row, data=1)
    return dt, a_row, cs, cs.T, causal, tri, real


def _gated_norm_fwd(y, z, w):
    g = y * _silu(z)
    half = SSD_WIDTH // SSD_GROUPS
    outs = [_rms_fwd(g[:, k * half:(k + 1) * half], w[:, k * half:(k + 1) * half]) for k in range(SSD_GROUPS)]
    return jnp.concatenate(outs, axis=1)


GROUP_W = SSD_WIDTH // SSD_GROUPS
PAIR_W = 2 * SSD_HEAD_DIM
STATE_SHAPE = (SSD_GROUPS, SSD_STATE, GROUP_W)


def _head_expander():
    r = lax.broadcasted_iota(jnp.int32, (128, SSD_WIDTH), 0)
    c = lax.broadcasted_iota(jnp.int32, (128, SSD_WIDTH), 1)
    return (c // SSD_HEAD_DIM == r).astype(F32)


def _ssd_expand(dt, cs, prm, ex):
    cs_x = _dot_onehot(cs, ex)
    cs_last_x = cs_x[CHUNK - 1:CHUNK, :]
    return (_dot_onehot(dt, ex, pieces=2), _dot_onehot(prm, ex)[2:3], jnp.exp(cs_x), jnp.exp(cs_last_x),
            jnp.exp(cs_last_x - cs_x))


def _ssd_fwd(xs, bc, dt_raw, z, prm, norm_w, ex):
    def body(xs_ref, bc_ref, dt_ref, z_ref, prm_ref, nw_ref, ex_ref, y_ref, yn_ref, prev_ref, state):
        c = pl.program_id(0)

        @pl.when(c == 0)
        def _():
            state[...] = jnp.zeros_like(state)

        prm = prm_ref[...]
        dt, a_row, cs, cs_t, causal, _, _ = _ssd_chunk_common(dt_ref[...], prm, c)
        dt_x, d_x, e_cs_x, e_last_x, dec_x = _ssd_expand(dt, cs, prm, ex_ref[...])
        xs_all = xs_ref[...]
        bc_all = bc_ref[...]
        xdt = xs_all * dt_x
        xdec = xdt * dec_x
        lane_lo = lax.broadcasted_iota(jnp.int32, (1, PAIR_W), 1) < SSD_HEAD_DIM
        for g in range(SSD_GROUPS):
            gs = slice(g * GROUP_W, (g + 1) * GROUP_W)
            b_g = bc_all[:, g * SSD_STATE:(g + 1) * SSD_STATE]
            c_g = bc_all[:, (SSD_GROUPS + g) * SSD_STATE:(SSD_GROUPS + g + 1) * SSD_STATE]
            st = state[g]
            prev_ref[0, g] = st
            y_off = _dot(c_g, st) * e_cs_x[:, gs]
            state[g] = st * e_last_x[:, gs] + _dot(b_g.T, xdec[:, gs])
            cb = _dot(c_g, b_g, NT)
            for k in range(SSD_HPG // 2):
                h0 = g * SSD_HPG + 2 * k
                ps = slice(h0 * SSD_HEAD_DIM, h0 * SSD_HEAD_DIM + PAIR_W)
                xdt_pair = xdt[:, ps]
                yd = []
                for h in (h0, h0 + 1):
                    lmat = jnp.where(causal, jnp.exp(cs[:, h:h + 1] - cs_t[h:h + 1, :]), 0.0)
                    yd.append(_dot(cb * lmat, xdt_pair))
                y_ref[:, ps] = (jnp.where(lane_lo, yd[0], yd[1]) + y_off[:, k * PAIR_W:(k + 1) * PAIR_W]
                                + xs_all[:, ps] * d_x[:, ps])
        yn_ref[...] = _gated_norm_fwd(y_ref[...], z_ref[...], nw_ref[...]).astype(_MXU)

    row = lambda w: pl.BlockSpec((CHUNK, w), lambda c: (c, 0))
    return pl.pallas_call(
        body, name="ssd_fwd", grid=(N_CHUNKS,),
        in_specs=[row(SSD_WIDTH), row(512), row(128), row(SSD_WIDTH),
                  pl.BlockSpec((8, 128), lambda c: (0, 0)), pl.BlockSpec((1, SSD_WIDTH), lambda c: (0, 0)),
                  pl.BlockSpec((128, SSD_WIDTH), lambda c: (0, 0))],
        out_specs=[row(SSD_WIDTH), row(SSD_WIDTH),
                   pl.BlockSpec((1,) + STATE_SHAPE, lambda c: (c, 0, 0, 0))],
        out_shape=[jax.ShapeDtypeStruct((T_ROWS, SSD_WIDTH), F32), jax.ShapeDtypeStruct((T_ROWS, SSD_WIDTH), _MXU),
                   jax.ShapeDtypeStruct((N_CHUNKS,) + STATE_SHAPE, F32)],
        scratch_shapes=[pltpu.VMEM(STATE_SHAPE, F32)],
        compiler_params=_cparams("arbitrary"),
    )(xs, bc, dt_raw, z, prm, norm_w, ex)


def _ssd_bwd(dyn, dyn_block, z, y_pre, xs, bc, dt_raw, prev, prm, norm_w, ex):
    def body(dyn_ref, z_ref, y_ref, xs_ref, bc_ref, dt_ref, prev_ref, prm_ref, nw_ref, ex_ref,
             dz_ref, dxs_ref, dbc_ref, ddt_ref, dprm_ref, dnw_ref, dstate):
        step = pl.program_id(0)
        c = N_CHUNKS - 1 - step

        @pl.when(step == 0)
        def _():
            dstate[...] = jnp.zeros_like(dstate)
            dprm_ref[...] = jnp.zeros_like(dprm_ref)
            dnw_ref[...] = jnp.zeros_like(dnw_ref)

        prm = prm_ref[...]
        dt, a_row, cs, cs_t, causal, tri, real = _ssd_chunk_common(dt_ref[...], prm, c)
        realf = real.astype(F32)
        z = z_ref[...]
        y_all = y_ref[...]
        nw = nw_ref[...]
        dyn_all = dyn_ref[...]
        sz = _silu(z)
        gated = y_all * sz
        half = SSD_WIDTH // SSD_GROUPS
        dgs, dnws = [], []
        for k in range(SSD_GROUPS):
            sl = slice(k * half, (k + 1) * half)
            dgk, dwk = _rms_bwd(gated[:, sl], nw[:, sl], dyn_all[:, sl])
            dgs.append(dgk)
            dnws.append(jnp.sum(dwk, axis=0, keepdims=True))
        dgated = jnp.concatenate(dgs, axis=1)
        dnw_ref[...] += jnp.concatenate(dnws, axis=1)
        dz_ref[...] = (dgated * y_all * _silu_grad(z)).astype(_MXU)
        dy_all = dgated * sz

        ex = ex_ref[...]
        dt_x, d_x, e_cs_x, e_last_x, dec_x = _ssd_expand(dt, cs, prm, ex)
        xs_all = xs_ref[...]
        bc_all = bc_ref[...]
        xdt = xs_all * dt_x
        xdt_mxu = xdt.astype(_MXU).astype(F32)
        xdec = xdt * dec_x
        dcp = dy_all * e_cs_x
        lane_lo = lax.broadcasted_iota(jnp.int32, (1, PAIR_W), 1) < SSD_HEAD_DIM
        upper = (lax.broadcasted_iota(jnp.int32, (CHUNK, CHUNK), 0)
                 <= lax.broadcasted_iota(jnp.int32, (CHUNK, CHUNK), 1))
        last_row = (lax.broadcasted_iota(jnp.int32, (CHUNK, 1), 0) == CHUNK - 1).astype(F32)
        dbs, dcs_, dxdt_parts, last_parts = [], [], [], []
        for g in range(SSD_GROUPS):
            gs = slice(g * GROUP_W, (g + 1) * GROUP_W)
            b_g = bc_all[:, g * SSD_STATE:(g + 1) * SSD_STATE]
            c_g = bc_all[:, (SSD_GROUPS + g) * SSD_STATE:(SSD_GROUPS + g + 1) * SSD_STATE]
            prev_t = prev_ref[0, g]
            dst = dstate[g]
            dc_g = _dot(dcp[:, gs], prev_t, NT)
            db_g = _dot(xdec[:, gs], dst, NT)
            dxdt_state = _dot(b_g, dst) * dec_x[:, gs]
            dstate[g] = dst * e_last_x[:, gs] + _dot(c_g.T, dcp[:, gs])
            last_parts.append(jnp.sum(xdt_mxu[:, gs] * dxdt_state, axis=0, keepdims=True)
                              + jnp.sum(dst * prev_t, axis=0, keepdims=True) * e_last_x[:, gs])
            cb_t = _dot(b_g, c_g, NT)
            dcb_t = jnp.zeros((CHUNK, CHUNK), F32)
            for k in range(SSD_HPG // 2):
                h0 = g * SSD_HPG + 2 * k
                ps = slice(h0 * SSD_HEAD_DIM, h0 * SSD_HEAD_DIM + PAIR_W)
                dy_pair = dy_all[:, ps]
                xdt_pair = xdt[:, ps]
                dd = []
                for h in (h0, h0 + 1):
                    lmat_t = jnp.where(upper, jnp.exp(cs_t[h:h + 1, :] - cs[:, h:h + 1]), 0.0)
                    dd.append(_dot(cb_t * lmat_t, dy_pair))
                    mine = lane_lo if h == h0 else jnp.logical_not(lane_lo)
                    dcb_t = dcb_t + _dot(jnp.where(mine, xdt_pair, 0.0), dy_pair, NT) * lmat_t
                dxdt_parts.append(jnp.where(lane_lo, dd[0], dd[1]) + dxdt_state[:, k * PAIR_W:(k + 1) * PAIR_W])
            dc_g = dc_g + _dot(dcb_t, b_g, TN)
            db_g = db_g + _dot(dcb_t, c_g)
            dbs.append(db_g * realf)
            dcs_.append(dc_g * realf)
        dbc_ref[...] = jnp.concatenate(dbs + dcs_, axis=1)
        dxdt = jnp.concatenate(dxdt_parts, axis=1)
        dxs_ref[...] = (dxdt * dt_x + dy_all * d_x) * realf
        ddt_all = _dot_onehot(dxdt * xs_all, ex, NT, pieces=2)
        rows = jnp.concatenate([jnp.concatenate(last_parts, axis=1), jnp.sum(dy_all * xs_all, axis=0, keepdims=True),
                                jnp.zeros((6, SSD_WIDTH), F32)], axis=0)
        rows = _dot_onehot(rows, ex, NT, pieces=2)
        dd_row = rows[1:2]
        dy_mxu = dy_all.astype(_MXU).astype(F32)
        dcs_all = (_dot_onehot(dy_mxu * (y_all - xs_all * d_x), ex, NT) - _dot_onehot(xdt_mxu * dxdt, ex, NT)
                   + last_row * rows[0:1])
        dda = _dot_onehot(tri, dcs_all, TN, data=1)
        ddt = (ddt_all + dda * a_row) * realf
        ddt_raw = ddt * _sigmoid(dt_ref[...] + prm[0:1])
        ddt_ref[...] = ddt_raw.astype(_MXU)
        da_log = jnp.sum(dda * dt, axis=0, keepdims=True) * a_row
        dprm_ref[0:1, :] += jnp.sum(ddt_raw, axis=0, keepdims=True)
        dprm_ref[1:2, :] += da_log
        dprm_ref[2:3, :] += dd_row

    rev = lambda w, blk=0: pl.BlockSpec((CHUNK, w), lambda s, blk=blk: (N_CHUNKS - 1 - s, blk))
    return pl.pallas_call(
        body, name="ssd_bwd", grid=(N_CHUNKS,),
        in_specs=[rev(SSD_WIDTH, dyn_block), rev(SSD_WIDTH), rev(SSD_WIDTH), rev(SSD_WIDTH), rev(512), rev(128),
                  pl.BlockSpec((1,) + STATE_SHAPE, lambda s: (N_CHUNKS - 1 - s, 0, 0, 0)),
                  pl.BlockSpec((8, 128), lambda s: (0, 0)), pl.BlockSpec((1, SSD_WIDTH), lambda s: (0, 0)),
                  pl.BlockSpec((128, SSD_WIDTH), lambda s: (0, 0))],
        out_specs=[rev(SSD_WIDTH), rev(SSD_WIDTH), rev(512), rev(128),
                   pl.BlockSpec((8, 128), lambda s: (0, 0)), pl.BlockSpec((1, SSD_WIDTH), lambda s: (0, 0))],
        out_shape=[jax.ShapeDtypeStruct((T_ROWS, SSD_WIDTH), _MXU), jax.ShapeDtypeStruct((T_ROWS, SSD_WIDTH), F32),
                   jax.ShapeDtypeStruct((T_ROWS, 512), F32), jax.ShapeDtypeStruct((T_ROWS, 128), _MXU),
                   jax.ShapeDtypeStruct((8, 128), F32), jax.ShapeDtypeStruct((1, SSD_WIDTH), F32)],
        scratch_shapes=[pltpu.VMEM(STATE_SHAPE, F32)],
        compiler_params=_cparams("arbitrary"),
    )(dyn, z, y_pre, xs, bc, dt_raw, prev, prm, norm_w, ex)


LRU_PAIRS = 8


def _lru_gates(xr, wa_ref, wx_ref, prm):
    pre_r, pre_i = [], []
    for k in range(LRU_PAIRS):
        xk = xr[:, k * 128:(k + 1) * 128]
        pre_r.append(_dot(xk, wa_ref[k]))
        pre_i.append(_dot(xk, wx_ref[k]))
    r = _sigmoid(jnp.concatenate(pre_r, axis=1) + prm[0:1])
    i = _sigmoid(jnp.concatenate(pre_i, axis=1) + prm[1:2])
    sp = _softplus(-prm[2:3])
    log_a = (-LRU_C) * r * sp
    a = jnp.exp(log_a)
    s = jnp.sqrt(-jnp.tanh(log_a) * (a * a + 1.0))
    return r, i, a, s, sp


def _lru_fwd(xr, gate, wa, wx, prm):
    def body(xr_ref, g_ref, wa_ref, wx_ref, prm_ref, hs_ref, yn_ref, carry, a_s, u_s):
        @pl.when(pl.program_id(0) == 0)
        def _():
            carry[...] = jnp.zeros_like(carry)

        prm = prm_ref[...]
        xr_t = xr_ref[...]
        _, i, a, s, _ = _lru_gates(xr_t, wa_ref, wx_ref, prm)
        a_s[...] = a
        u_s[...] = s * (i * xr_t)
        rid = lax.broadcasted_iota(jnp.int32, (8, LRU_WIDTH), 0)

        def group(k, before):
            off = pl.multiple_of(k * 8, 8)
            a8 = a_s[pl.ds(off, 8), :]
            u8 = u_s[pl.ds(off, 8), :]
            for d in (1, 2, 4):
                keep = rid >= d
                u8 = u8 + a8 * jnp.where(keep, pltpu.roll(u8, d, 0), 0.0)
                a8 = a8 * jnp.where(keep, pltpu.roll(a8, d, 0), 1.0)
            h8 = u8 + a8 * before
            hs_ref[pl.ds(off, 8), :] = h8
            return jnp.broadcast_to(h8[7:8], (8, LRU_WIDTH))

        carry[...] = lax.fori_loop(0, CHUNK // 8, group, carry[...])
        gel, _ = _gelu_and_grad(g_ref[...])
        yn_ref[...] = _rms_fwd(gel * hs_ref[...], prm[3:4]).astype(_MXU)

    row = pl.BlockSpec((CHUNK, LRU_WIDTH), lambda t: (t, 0))
    wspec = pl.BlockSpec((LRU_PAIRS, 128, 128), lambda t: (0, 0, 0))
    return pl.pallas_call(
        body, name="lru_fwd", grid=(N_CHUNKS,),
        in_specs=[row, row, wspec, wspec, pl.BlockSpec((8, LRU_WIDTH), lambda t: (0, 0))],
        out_specs=[row, row],
        out_shape=[jax.ShapeDtypeStruct((T_ROWS, LRU_WIDTH), F32), jax.ShapeDtypeStruct((T_ROWS, LRU_WIDTH), _MXU)],
        scratch_shapes=[pltpu.VMEM((8, LRU_WIDTH), F32), pltpu.VMEM((CHUNK, LRU_WIDTH), F32),
                        pltpu.VMEM((CHUNK, LRU_WIDTH), F32)],
        compiler_params=_cparams("arbitrary"),
    )(xr, gate, wa, wx, prm)


def _lru_bwd(dyn, dyn_block, gate, xr, hs, wa, wx, wa_t, wx_t, prm):
    def body(dyn_ref, g_ref, xr_ref, hs_ref, hsp_ref, wa_ref, wx_ref, wat_ref, wxt_ref, prm_ref,
             dg_ref, dxr_ref, dwa_ref, dwx_ref, dprm_ref, carry, a_s, d_s):
        step = pl.program_id(0)
        tile = N_CHUNKS - 1 - step

        @pl.when(step == 0)
        def _():
            carry[...] = jnp.zeros_like(carry)
            dwa_ref[...] = jnp.zeros_like(dwa_ref)
            dwx_ref[...] = jnp.zeros_like(dwx_ref)
            dprm_ref[...] = jnp.zeros_like(dprm_ref)

        prm = prm_ref[...]
        xr_t = xr_ref[...]
        r, i, a, s, sp = _lru_gates(xr_t, wa_ref, wx_ref, prm)
        hs_t = hs_ref[...]
        gel, dgel = _gelu_and_grad(g_ref[...])
        dy, dnw = _rms_bwd(gel * hs_t, prm[3:4], dyn_ref[...])
        dg_ref[...] = (dy * hs_t * dgel).astype(_MXU)
        a_s[...] = a
        d_s[...] = dy * gel
        rid = lax.broadcasted_iota(jnp.int32, (8, LRU_WIDTH), 0)

        def group(k, behind):
            off = pl.multiple_of((CHUNK // 8 - 1 - k) * 8, 8)
            a8 = a_s[pl.ds(off, 8), :]
            d8 = d_s[pl.ds(off, 8), :]
            c8 = jnp.where(rid == 7, 1.0, pltpu.roll(a8, 7, 0))
            for d in (1, 2, 4):
                keep = rid < 8 - d
                d8 = d8 + c8 * jnp.where(keep, pltpu.roll(d8, 8 - d, 0), 0.0)
                c8 = c8 * jnp.where(keep, pltpu.roll(c8, 8 - d, 0), 1.0)
            dht8 = d8 + c8 * behind
            d_s[pl.ds(off, 8), :] = dht8
            return jnp.broadcast_to(a8[0:1] * dht8[0:1], (8, LRU_WIDTH))

        carry[...] = lax.fori_loop(0, CHUNK // 8, group, carry[...])
        dht = d_s[...]
        before = hsp_ref[CHUNK - 8:CHUNK, :][7:8] * (tile > 0).astype(F32)
        first = lax.broadcasted_iota(jnp.int32, (CHUNK, 1), 0) == 0
        hprev = jnp.where(first, before, pltpu.roll(hs_t, 1, 0))
        da = dht * hprev
        ixr = i * xr_t
        ds = dht * ixr
        dlog_a = da * a - ds * (a * a) * lax.rsqrt(s * s)
        dr = dlog_a * ((-LRU_C) * sp)
        dsp = jnp.sum(dlog_a * ((-LRU_C) * r), axis=0, keepdims=True)
        dlam = dsp * (-_sigmoid(-prm[2:3]))
        di = dht * s * xr_t
        dpre_r = dr * r * (1.0 - r)
        dpre_i = di * i * (1.0 - i)
        dxr = dht * s * i
        parts = []
        for k in range(LRU_PAIRS):
            sl = slice(k * 128, (k + 1) * 128)
            parts.append(_dot(dpre_r[:, sl], wat_ref[k]) + _dot(dpre_i[:, sl], wxt_ref[k]))
            dwa_ref[k] += _dot(xr_t[:, sl], dpre_r[:, sl], TN)
            dwx_ref[k] += _dot(xr_t[:, sl], dpre_i[:, sl], TN)
        dxr_ref[...] = dxr + jnp.concatenate(parts, axis=1)
        dprm_ref[0:1, :] += jnp.sum(dpre_r, axis=0, keepdims=True)
        dprm_ref[1:2, :] += jnp.sum(dpre_i, axis=0, keepdims=True)
        dprm_ref[2:3, :] += dlam
        dprm_ref[3:4, :] += jnp.sum(dnw, axis=0, keepdims=True)

    rev = lambda blk=0: pl.BlockSpec((CHUNK, LRU_WIDTH), lambda s, blk=blk: (N_CHUNKS - 1 - s, blk))
    wspec = pl.BlockSpec((LRU_PAIRS, 128, 128), lambda s: (0, 0, 0))
    return pl.pallas_call(
        body, name="lru_bwd", grid=(N_CHUNKS,),
        in_specs=[rev(dyn_block), rev(), rev(), rev(),
                  pl.BlockSpec((CHUNK, LRU_WIDTH), lambda s: (jnp.maximum(N_CHUNKS - 2 - s, 0), 0)),
                  wspec, wspec, wspec, wspec, pl.BlockSpec((8, LRU_WIDTH), lambda s: (0, 0))],
        out_specs=[rev(), rev(), wspec, wspec, pl.BlockSpec((8, LRU_WIDTH), lambda s: (0, 0))],
        out_shape=[jax.ShapeDtypeStruct((T_ROWS, LRU_WIDTH), _MXU), jax.ShapeDtypeStruct((T_ROWS, LRU_WIDTH), F32),
                   jax.ShapeDtypeStruct((LRU_PAIRS, 128, 128), F32), jax.ShapeDtypeStruct((LRU_PAIRS, 128, 128), F32),
                   jax.ShapeDtypeStruct((8, LRU_WIDTH), F32)],
        scratch_shapes=[pltpu.VMEM((8, LRU_WIDTH), F32), pltpu.VMEM((CHUNK, LRU_WIDTH), F32),
                        pltpu.VMEM((CHUNK, LRU_WIDTH), F32)],
        compiler_params=_cparams("arbitrary"),
    )(dyn, gate, xr, hs, hs, wa, wx, wa_t, wx_t, prm)


SEC_NAMES = ("z", "xs", "bc", "dt", "g", "x")
SEC_WIDTH = {"z": 1024, "xs": 1024, "bc": 512, "dt": 128, "g": 1024, "x": 1024}


def _pair_blocks(w):
    w = w.reshape(LRU_PAIRS, 2, 64, 64)
    zero = jnp.zeros((LRU_PAIRS, 64, 64), w.dtype)
    top = jnp.concatenate([w[:, 0], zero], axis=2)
    bot = jnp.concatenate([zero, w[:, 1]], axis=2)
    return jnp.concatenate([top, bot], axis=1)


def _unpair_blocks(wp):
    return jnp.stack([wp[:, :64, :64], wp[:, 64:, 64:]], axis=1).reshape(16, 64, 64)


def _pad_lanes(v, width=128):
    return jnp.pad(v, ((0, 0), (0, width - v.shape[1])))


class _Resident:
    before_embed = ()

    def __init__(self, w_in_sections, w_out, w_gate, w_up, w_down):
        self._w_in, self._w_out, self._ffn = w_in_sections, w_out, (w_gate, w_up, w_down)

    def w_in(self, after):
        return self._w_in

    def mid_forward(self, after):
        return jnp.zeros((1, 1), F32)

    def w_out(self, after):
        return self._w_out

    def ffn(self, after):
        return self._ffn

    def grads_ready(self, names, g, g_mxu):
        return jnp.zeros((1, 1), F32)

    def small_ready(self, g, loss):
        return jnp.zeros((1, 1), F32)

    def small_middle(self, after):
        return jnp.zeros((1, 1), F32)


def _local_step(x, target, meta, p, late):
    g, g_mxu = {}, {}
    ex = _head_expander()
    h0 = _embed(x, meta, late.before_embed)
    w_in = late.w_in(h0)
    u1, projs = _norm_proj(h0, p["norm1_w"], [w_in[s] for s in SEC_NAMES], name="norm_in_proj")
    proj = dict(zip(SEC_NAMES, projs))
    ssd_prm = jnp.concatenate([_pad_lanes(p["ssd_dt_bias"]), _pad_lanes(p["ssd_a_log"]), _pad_lanes(p["ssd_d"]),
                               jnp.zeros((5, 128), F32)], axis=0)
    xs_act = _conv_fwd(proj["xs"], p["ssd_conv_w"][:, :SSD_WIDTH], p["ssd_conv_b"][:, :SSD_WIDTH], silu=True,
                       name="ssd_conv_xs")
    bc_act = _conv_fwd(proj["bc"], p["ssd_conv_w"][:, SSD_WIDTH:], p["ssd_conv_b"][:, SSD_WIDTH:], silu=True,
                       name="ssd_conv_bc")
    y_pre, y_ssd, prev = _ssd_fwd(xs_act, bc_act, proj["dt"], proj["z"], ssd_prm, p["ssd_norm_w"], ex)
    xr = _conv_fwd(proj["x"], p["lru_conv_w"], p["lru_conv_b"], silu=False, name="lru_conv")
    wa_p, wx_p = _pair_blocks(p["lru_wa"]), _pair_blocks(p["lru_wx"])
    lru_prm = jnp.concatenate([p["lru_ba"], p["lru_bx"], p["lru_lambda"], p["lru_norm_w"],
                               jnp.zeros((4, LRU_WIDTH), F32)], axis=0)
    hs, y_lru = _lru_fwd(xr, proj["g"], wa_p.astype(_MXU), wx_p.astype(_MXU),
                         lru_prm + late.mid_forward([xr, y_ssd]))
    ycat = jnp.concatenate([y_ssd, y_lru], axis=1)
    w_out = late.w_out(ycat)
    h1 = _mm([(ycat, 0, w_out, 0, 2 * D_MODEL)], T_ROWS, D_MODEL, tm=T_ROWS, tn=256, mode="nn", out_dtype=F32,
             name="out_proj", residual=h0)
    u2 = _rmsnorm(h1, p["norm2_w"], name="norm2")
    w_gate, w_up, w_down = late.ffn(u2)
    gp, up, act = _ffn_up(u2, w_gate, w_up)
    h2 = _mm([(act, 0, w_down, 0, D_FF)], T_ROWS, D_MODEL, tm=T_ROWS, tn=256, mode="nn", out_dtype=F32,
             name="ffn_down", residual=h1)
    loss, dh2, dh2b, g["final_norm_w"] = _loss_head(h2, target, p["final_norm_w"])
    dgp, dup = _ffn_bwd_act(dh2b, w_down, gp, up)
    g["w_down"], g_mxu["w_down"] = _mm([(act, 0, dh2b, 0, T_ROWS)], D_FF, D_MODEL, tm=1408, tn=512, mode="tn",
                                       out_dtype=F32, name="dw_down", also_mxu=True)
    dh1, dh1b, g["norm2_w"] = _mm_norm_bwd([(dgp, w_gate, D_FF), (dup, w_up, D_FF)], h1, p["norm2_w"], dh2,
                                           name="ffn_bwd_in")
    g["w_gate"], g_mxu["w_gate"] = _mm([(dgp, 0, u2, 0, T_ROWS)], D_FF, D_MODEL, tm=1408, tn=512, mode="tn",
                                       out_dtype=F32, name="dw_gate", also_mxu=True)
    g["w_up"], g_mxu["w_up"] = _mm([(dup, 0, u2, 0, T_ROWS)], D_FF, D_MODEL, tm=1408, tn=512, mode="tn",
                                   out_dtype=F32, name="dw_up", also_mxu=True)
    g["w_out"], g_mxu["w_out"] = _mm([(ycat, 0, dh1b, 0, T_ROWS)], 2 * D_MODEL, D_MODEL, tm=1024, tn=512, mode="tn",
                                     out_dtype=F32, name="dw_out", also_mxu=True)
    sent = late.grads_ready(("w_down", "w_gate", "w_up", "w_out"), g, g_mxu)
    dycat = _mm([(dh1b, 0, w_out, 0, D_MODEL)], T_ROWS, 2 * D_MODEL, tm=T_ROWS, tn=256, mode="nt", out_dtype=F32,
                name="out_proj_bwd", behind=(sent,))
    dgate, dxr, dwa_p, dwx_p, dlru_prm = _lru_bwd(dycat, 1, proj["g"], xr, hs, wa_p.astype(_MXU), wx_p.astype(_MXU),
                                                  jnp.swapaxes(wa_p, 1, 2).astype(_MXU),
                                                  jnp.swapaxes(wx_p, 1, 2).astype(_MXU), lru_prm)
    g["lru_wa"], g["lru_wx"] = _unpair_blocks(dwa_p), _unpair_blocks(dwx_p)
    g["lru_ba"], g["lru_bx"], g["lru_lambda"], g["lru_norm_w"] = (dlru_prm[k:k + 1] for k in range(4))
    dx_lru, g["lru_conv_w"], g["lru_conv_b"] = _conv_bwd(dxr, proj["x"], p["lru_conv_w"], p["lru_conv_b"], silu=False,
                                                         name="lru_conv_bwd")
    dz, dxs_act, dbc_act, ddt, dssd_prm, g["ssd_norm_w"] = _ssd_bwd(dycat, 0, proj["z"], y_pre, xs_act, bc_act,
                                                                    proj["dt"], prev, ssd_prm, p["ssd_norm_w"], ex)
    g["ssd_dt_bias"], g["ssd_a_log"], g["ssd_d"] = (dssd_prm[k:k + 1, :SSD_HEADS] for k in range(3))
    dxs, dcw_xs, dcb_xs = _conv_bwd(dxs_act, proj["xs"], p["ssd_conv_w"][:, :SSD_WIDTH],
                                    p["ssd_conv_b"][:, :SSD_WIDTH], silu=True, name="ssd_conv_xs_bwd")
    dbc, dcw_bc, dcb_bc = _conv_bwd(dbc_act, proj["bc"], p["ssd_conv_w"][:, SSD_WIDTH:],
                                    p["ssd_conv_b"][:, SSD_WIDTH:], silu=True, name="ssd_conv_bc_bwd")
    g["ssd_conv_w"] = jnp.concatenate([dcw_xs, dcw_bc], axis=1)
    g["ssd_conv_b"] = jnp.concatenate([dcb_xs, dcb_bc], axis=1)
    dproj = {"z": dz, "xs": dxs, "bc": dbc, "dt": ddt, "g": dgate, "x": dx_lru}
    for s in SEC_NAMES:
        wdt = SEC_WIDTH[s]
        g["w_in_" + s], g_mxu["w_in_" + s] = _mm([(dproj[s], 0, u1, 0, T_ROWS)], wdt, D_MODEL, tm=min(wdt, 1024),
                                                 tn=512, mode="tn", out_dtype=F32, name="dw_in_" + s, also_mxu=True)
    sent = late.grads_ready(("w_in",), g, g_mxu)
    dh0, _, g["norm1_w"] = _mm_norm_bwd([(dproj[s], w_in[s], SEC_WIDTH[s]) for s in SEC_NAMES], h0,
                                        p["norm1_w"], dh1, name="in_proj_bwd", behind=(sent,))
    g["meta_tokens"] = dh0[PAD_ROWS:X_ROW0]
    late.small_ready(g, loss)
    return loss, dh0[X_ROW0:], g, g_mxu


MESH = pl.DeviceIdType.MESH
ANY = pl.BlockSpec(memory_space=pl.ANY)


def _my_place():
    return lax.axis_index("x"), lax.axis_index("y"), lax.axis_index("c")


def _other_chips(x, y):
    return [(1 - x, y), (x, 1 - y), (1 - x, 1 - y)]


HBM_SPEC = pl.BlockSpec(memory_space=pltpu.HBM)
SEM_SPEC = pl.BlockSpec(memory_space=pltpu.SEMAPHORE)
SPLIT_EFFECT = pltpu.SideEffectType.DATAFLOW_SIDE_EFFECTING


def _half_cols(buf, c, other=False):
    half = buf.shape[-1] // 2
    return pl.ds(pl.multiple_of(((1 - c) if other else c) * half, 128), half)


def _halves_plan(bufs, x, y, c, incoming):
    plan = []
    for buf in bufs:
        cols = _half_cols(buf, c)
        for (px, py) in _other_chips(x, y):
            slot = 2 * px + py if incoming else 2 * x + y
            plan.append((buf.at[2 * x + y, :, cols], buf.at[slot, :, cols], (px, py, c)))
    return plan


def _forward_plan(bufs, x, y, c, incoming):
    plan = []
    for buf in bufs:
        for (px, py) in _other_chips(x, y):
            slot = 2 * px + py
            plan.append((buf.at[slot, :, _half_cols(buf, c)], buf.at[slot, :, _half_cols(buf, c, other=incoming)],
                         (x, y, 1 - c)))
    return plan


def _scatter_plan(bufs, x, y, c, incoming):
    n = len(bufs) // 2
    plan = []
    for k in range(n):
        for j, (px, py) in enumerate(_other_chips(x, y)):
            plan.append((bufs[k].at[2 * px + py], bufs[n + k].at[j], (px, py, c)))
    return plan


def _split_start(bufs, plan, n_copies, after, *, name):
    n = len(bufs)
    extra = [] if after is None else [after]

    def body(*refs):
        ins = refs[:n]
        send_sems, recv_sems = refs[n + len(extra)], refs[n + len(extra) + 1]
        token = refs[-1]
        x, y, c = _my_place()
        for i, (src, dst, dev) in enumerate(plan(ins, x, y, c, False)):
            pltpu.make_async_remote_copy(src_ref=src, dst_ref=dst, send_sem=send_sems.at[i], recv_sem=recv_sems.at[i],
                                         device_id=dev, device_id_type=MESH).start()
        token[...] = jnp.zeros_like(token)

    outs = pl.pallas_call(
        body, name=name,
        out_shape=(pltpu.SemaphoreType.DMA((n_copies,)), pltpu.SemaphoreType.DMA((n_copies,)),
                   *[pltpu.HBM(b.shape, b.dtype) for b in bufs], jax.ShapeDtypeStruct((8, 128), F32)),
        in_specs=[HBM_SPEC] * n + [ANY] * len(extra),
        out_specs=(SEM_SPEC, SEM_SPEC, *[HBM_SPEC] * n, pl.BlockSpec(memory_space=pltpu.VMEM)),
        input_output_aliases={k: 2 + k for k in range(n)},
        compiler_params=pltpu.CompilerParams(has_side_effects=SPLIT_EFFECT),
    )(*[pltpu.with_memory_space_constraint(b, pltpu.HBM) for b in bufs], *extra)
    return outs[0], outs[1], list(outs[2:2 + n]), outs[-1]


def _split_wait(bufs, send_sems, recv_sems, plan, after, *, name):
    n = len(bufs)
    after = list(after) if isinstance(after, (list, tuple)) else [after]

    def body(*refs):
        ins = refs[:n]
        send_sems_ref, recv_sems_ref = refs[n], refs[n + 1]
        x, y, c = _my_place()
        for i, (src, dst, dev) in enumerate(plan(ins, x, y, c, True)):
            cp = pltpu.make_async_remote_copy(src_ref=src, dst_ref=dst, send_sem=send_sems_ref.at[i],
                                              recv_sem=recv_sems_ref.at[i], device_id=dev, device_id_type=MESH)
            cp.wait_send()
            cp.wait_recv()

    outs = pl.pallas_call(
        body, name=name, out_shape=tuple(pltpu.HBM(b.shape, b.dtype) for b in bufs),
        in_specs=[HBM_SPEC] * n + [SEM_SPEC, SEM_SPEC] + [ANY] * len(after), out_specs=tuple([HBM_SPEC] * n),
        input_output_aliases={k: k for k in range(n)},
        compiler_params=pltpu.CompilerParams(has_side_effects=SPLIT_EFFECT),
    )(*bufs, send_sems, recv_sems, *after)
    return list(outs)


def _fill_own_slots(shards, me_arr, *, name, behind=()):
    n = len(shards)
    n_in = n + len(behind)

    def body(me_ref, *refs):
        for k in range(n):
            refs[n_in + k][0] = refs[k][...].astype(_MXU)

    half = D_MODEL // 2
    return pl.pallas_call(
        body, name=name,
        grid_spec=pltpu.PrefetchScalarGridSpec(
            num_scalar_prefetch=1, grid=(2,),
            in_specs=[pl.BlockSpec((s.shape[0], half), lambda i, me: (0, i)) for s in shards]
            + [pl.BlockSpec(memory_space=pl.ANY)] * len(behind),
            out_specs=[pl.BlockSpec((1, s.shape[0], half), lambda i, me: (me[0], 0, i)) for s in shards]),
        out_shape=[jax.ShapeDtypeStruct((N_SHARDS,) + s.shape, _MXU) for s in shards],
        compiler_params=_cparams("parallel"),
    )(me_arr, *shards, *behind)


def _gather_small(small):
    def body(s_ref, o_ref, send_sems, recv_sems, local_sem):
        x, y, c = _my_place()
        me = 2 * x + y
        local = pltpu.make_async_copy(s_ref, o_ref.at[me], local_sem)
        local.start()
        copies = [(pltpu.make_async_remote_copy(src_ref=s_ref, dst_ref=o_ref.at[me], send_sem=send_sems.at[j],
                                                recv_sem=recv_sems.at[j], device_id=(px, py, c), device_id_type=MESH),
                   2 * px + py) for j, (px, py) in enumerate(_other_chips(x, y))]
        for cp, _ in copies:
            cp.start()
        for j, (cp, slot) in enumerate(copies):
            cp.wait_send()
            pltpu.make_async_remote_copy(src_ref=s_ref, dst_ref=o_ref.at[slot], send_sem=send_sems.at[j],
                                         recv_sem=recv_sems.at[j], device_id=(x, y, c),
                                         device_id_type=MESH).wait_recv()
        local.wait()

    return pl.pallas_call(
        body, name="gather_small", in_specs=[ANY], out_specs=ANY,
        out_shape=jax.ShapeDtypeStruct((N_SHARDS,) + small.shape, small.dtype),
        scratch_shapes=[pltpu.SemaphoreType.DMA((3,)), pltpu.SemaphoreType.DMA((3,)), pltpu.SemaphoreType.DMA],
    )(small)


def _swap_with_sibling(parts, *, name, behind=()):
    n = len(parts)
    nb = len(behind)

    def body(*refs):
        ins, outs = refs[:n], refs[n + nb:2 * n + nb]
        send_sems, recv_sems = refs[2 * n + nb:]
        x, y, c = _my_place()
        copies = [pltpu.make_async_remote_copy(
            src_ref=ins[k], dst_ref=outs[k], send_sem=send_sems.at[k], recv_sem=recv_sems.at[k],
            device_id=(x, y, 1 - c), device_id_type=MESH) for k in range(n)]
        for cp in copies:
            cp.start()
        for cp in copies:
            cp.wait()

    return pl.pallas_call(
        body, name=name, in_specs=[ANY] * (n + nb), out_specs=[ANY] * n,
        out_shape=[jax.ShapeDtypeStruct(a.shape, a.dtype) for a in parts],
        scratch_shapes=[pltpu.SemaphoreType.DMA((n,)), pltpu.SemaphoreType.DMA((n,))],
    )(*parts, *behind)


def _other_devices(x, y, c):
    out = []
    for mask in range(1, N_DEV):
        px, py, pc = x ^ (mask >> 2 & 1), y ^ (mask >> 1 & 1), c ^ (mask & 1)
        out.append(((px, py, pc), 4 * px + 2 * py + pc))
    return out


def _pieces_plan(bufs, x, y, c, incoming):
    pack, land = bufs
    me = 4 * x + 2 * y + c
    return [(pack.at[num], land.at[num if incoming else me], dev) for dev, num in _other_devices(x, y, c)]


def _spread_plan(bufs, x, y, c, incoming):
    piece, land = bufs
    me = 4 * x + 2 * y + c
    return [(piece, land.at[num if incoming else me], dev) for dev, num in _other_devices(x, y, c)]


def _sum_pieces(pack, land, dev_arr, *, name):
    def body(dev_ref, pack_ref, land_ref, o_ref):
        dev = dev_ref[0]
        own = pack_ref[dev]
        acc = None
        for d in range(N_DEV):
            term = jnp.where(dev == d, own, land_ref[d])
            acc = term if acc is None else acc + term
        o_ref[...] = acc

    vmem = pl.BlockSpec(memory_space=pltpu.VMEM)
    return pl.pallas_call(
        body, name=name, in_specs=[pl.BlockSpec(memory_space=pltpu.SMEM), vmem, vmem], out_specs=vmem,
        out_shape=jax.ShapeDtypeStruct(pack.shape[1:], F32),
    )(dev_arr, pack, land)


def _join_pieces(piece, land, dev_arr, *, name):
    def body(dev_ref, piece_ref, land_ref, o_ref):
        dev = dev_ref[0]
        for d in range(N_DEV):
            o_ref[d] = jnp.where(dev == d, piece_ref[...], land_ref[d])

    vmem = pl.BlockSpec(memory_space=pltpu.VMEM)
    return pl.pallas_call(
        body, name=name, in_specs=[pl.BlockSpec(memory_space=pltpu.SMEM), vmem, vmem], out_specs=vmem,
        out_shape=jax.ShapeDtypeStruct(land.shape, F32),
    )(dev_arr, piece, land)


def _adamw_native(ws, gs, ms, vs):
    n = len(ws)

    def body(*refs):
        for k in range(n):
            w_ref, g_ref, m_ref, v_ref = (refs[j * n + k] for j in range(4))
            delta, m_new, v_new = _adamw_math(w_ref[...], g_ref[...], m_ref[...], v_ref[...])
            refs[4 * n + k][...] = delta
            refs[5 * n + k][...] = m_new
            refs[6 * n + k][...] = v_new

    vmem = pl.BlockSpec(memory_space=pltpu.VMEM)
    shapes = [jax.ShapeDtypeStruct(a.shape, F32) for a in ws]
    outs = pl.pallas_call(
        body, name="adamw_small", in_specs=[vmem] * (4 * n), out_specs=[vmem] * (3 * n), out_shape=shapes * 3,
        compiler_params=pltpu.CompilerParams(vmem_limit_bytes=VMEM_LIMIT_BYTES),
    )(*ws, *gs, *ms, *vs)
    return outs[:n], outs[n:2 * n], outs[2 * n:]


def _elementwise_tile(rows, cols):
    for t in range(256, 15, -16):
        if rows % t == 0:
            return (t, cols), rows // t, lambda i: (i, 0)
    assert cols % 256 == 0
    return (rows, 256), cols // 256, lambda i: (0, i)


def _partial_sum(own, land, me_arr, *, name):
    r, c = own.shape[-2:]
    tile, steps, imap = _elementwise_tile(r, c)
    whole = own.ndim == 3

    def body(me_ref, own_ref, land_ref, o_ref):
        acc = own_ref[0] if whole else own_ref[...]
        for j in range(3):
            acc = acc + land_ref[j].astype(F32)
        o_ref[...] = acc.astype(_MXU)

    own_spec = (pl.BlockSpec((1,) + tile, lambda i, me: (me[0],) + imap(i)) if whole
                else pl.BlockSpec(tile, lambda i, me: imap(i)))
    return pl.pallas_call(
        body, name=name,
        grid_spec=pltpu.PrefetchScalarGridSpec(
            num_scalar_prefetch=1, grid=(steps,),
            in_specs=[own_spec, pl.BlockSpec((3,) + tile, lambda i, me: (0,) + imap(i))],
            out_specs=pl.BlockSpec(tile, lambda i, me: imap(i))),
        out_shape=jax.ShapeDtypeStruct((r, c), _MXU),
        compiler_params=_cparams("parallel"),
    )(me_arr, own, land)


LANE_TILE = 256


def _partial_sums(owns, lands, me_arr, *, name):
    n = len(owns)

    def body(me_ref, *refs):
        for k in range(n):
            acc = refs[k][0]
            for j in range(3):
                acc = acc + refs[n + k][j].astype(F32)
            refs[2 * n + k][...] = acc.astype(_MXU)

    rows = [o.shape[1] for o in owns]
    return pl.pallas_call(
        body, name=name,
        grid_spec=pltpu.PrefetchScalarGridSpec(
            num_scalar_prefetch=1, grid=(D_MODEL // LANE_TILE,),
            in_specs=[pl.BlockSpec((1, r, LANE_TILE), lambda i, me: (me[0], 0, i)) for r in rows]
            + [pl.BlockSpec((3, r, LANE_TILE), lambda i, me: (0, 0, i)) for r in rows],
            out_specs=[pl.BlockSpec((r, LANE_TILE), lambda i, me: (0, i)) for r in rows]),
        out_shape=[jax.ShapeDtypeStruct((r, D_MODEL), _MXU) for r in rows],
        compiler_params=_cparams("parallel"),
    )(me_arr, *owns, *lands)


def _adamws(ws, parts_a, parts_b, ms, vs, *, name):
    n = len(ws)

    def body(*refs):
        for k in range(n):
            w_ref, a_ref, b_ref, m_ref, v_ref = (refs[j * n + k] for j in range(5))
            g = a_ref[...].astype(F32) + b_ref[...].astype(F32)
            delta, m_new, v_new = _adamw_math(w_ref[...], g, m_ref[...], v_ref[...])
            for j, val in enumerate((g, delta, m_new, v_new)):
                refs[(5 + j) * n + k][...] = val

    tiles = [pl.BlockSpec((w.shape[0], LANE_TILE), lambda i: (0, i)) for w in ws]
    outs = pl.pallas_call(
        body, name=name, grid=(D_MODEL // LANE_TILE,), in_specs=tiles * 5, out_specs=tiles * 4,
        out_shape=[jax.ShapeDtypeStruct(w.shape, F32) for w in ws] * 4,
        compiler_params=_cparams("parallel"),
    )(*ws, *parts_a, *parts_b, *ms, *vs)
    return [outs[j * n:(j + 1) * n] for j in range(4)]


def _adamw_math(w, g, m, v):
    m = ADAM_B1 * m + (1.0 - ADAM_B1) * g
    v = ADAM_B2 * v + (1.0 - ADAM_B2) * (g * g)
    m_hat = m / (1.0 - ADAM_B1 ** ADAM_STEP)
    v_hat = v / (1.0 - ADAM_B2 ** ADAM_STEP)
    delta = -ADAM_LR * (m_hat / (jnp.sqrt(v_hat) + ADAM_EPS) + ADAM_WD * w)
    return delta, m, v


def _adamw(w, grad_parts, m, v, *, name):
    if w.ndim == 3:
        steps = 4
        assert w.shape[0] % steps == 0
        tile_shape, imap = (w.shape[0] // steps,) + w.shape[1:], lambda i: (i, 0, 0)
    else:
        tile_shape, steps, imap = _elementwise_tile(*w.shape)
    n = len(grad_parts)

    def body(*refs):
        w_ref, m_ref, v_ref = refs[:3]
        g_refs = refs[3:3 + n]
        g_out, d_out, m_out, v_out = refs[3 + n:]
        g = g_refs[0][...].astype(F32)
        for k in range(1, n):
            g = g + g_refs[k][...].astype(F32)
        delta, m_new, v_new = _adamw_math(w_ref[...], g, m_ref[...], v_ref[...])
        g_out[...] = g
        d_out[...] = delta
        m_out[...] = m_new
        v_out[...] = v_new

    tile = pl.BlockSpec(tile_shape, imap)
    return pl.pallas_call(
        body, name=name, grid=(steps,), in_specs=[tile] * (3 + n), out_specs=[tile] * 4,
        out_shape=[jax.ShapeDtypeStruct(w.shape, F32)] * 4,
        compiler_params=_cparams("parallel"),
    )(w, m, v, *grad_parts)


WEIGHT_NAMES = ("meta_tokens", "norm1_w", "w_in", "ssd_conv_w", "ssd_conv_b", "ssd_dt_bias", "ssd_a_log", "ssd_d",
                "ssd_norm_w", "lru_conv_w", "lru_conv_b", "lru_wa", "lru_ba", "lru_wx", "lru_bx", "lru_lambda",
                "lru_norm_w", "w_out", "norm2_w", "w_gate", "w_up", "w_down", "final_norm_w")
BIG = ("w_in", "w_out", "w_gate", "w_up", "w_down")
FFN = ("w_gate", "w_up", "w_down")
LATE = ("w_out",) + FFN
SMALL_SHARDED = {"meta_tokens": (N_META, D_MODEL), "ssd_conv_w": (CONV_K, 1536), "lru_conv_w": (CONV_K, LRU_WIDTH)}
SMALL = tuple(n for n in WEIGHT_NAMES if n not in BIG)
PACK_COLS = 1024


def _pack(arrays, row_multiple):
    flat = jnp.concatenate([a.reshape(-1) for a in arrays])
    rows = -(-flat.shape[0] // (row_multiple * PACK_COLS)) * row_multiple
    return jnp.pad(flat, (0, rows * PACK_COLS - flat.shape[0])).reshape(rows, PACK_COLS)


def _unpack(pack, shapes):
    flat = pack.reshape(-1)
    out, off = [], 0
    for s in shapes:
        size = math.prod(s)
        out.append(flat[off:off + size].reshape(s))
        off += size
    return out


def _unshard_cols(g4):
    return jnp.swapaxes(g4, 0, 1).reshape(g4.shape[1], -1)


COL_SHARDED = ("w_in", "w_gate", "w_up")
IN_ROWS = {"z": (0, 1024), "xs": (1024, 2048), "bc": (2048, 2560), "dt": (2560, 2576), "g": (2576, 3600),
           "x": (3600, IN_COLS)}


def _rows_of_shards(shards4, lo, hi):
    r = shards4.shape[1]
    parts = [shards4[k, max(lo, k * r) - k * r:min(hi, (k + 1) * r) - k * r]
             for k in range(N_SHARDS) if max(lo, k * r) < min(hi, (k + 1) * r)]
    return parts[0] if len(parts) == 1 else jnp.concatenate(parts, axis=0)


def _w_in_shard_rows(k, sections):
    lo, hi = k * (IN_COLS // N_SHARDS), (k + 1) * (IN_COLS // N_SHARDS)
    parts = []
    for arr, (a, b) in zip(sections, IN_ROWS.values()):
        if max(lo, a) < min(hi, b):
            parts.append(arr[max(lo, a) - a:min(hi, b) - a])
    return jnp.concatenate(parts, axis=0)


def _rows_view(name, block):
    return jnp.swapaxes(block[0], 0, 1) if name in COL_SHARDED else block[0]


def _param_view(name, rows):
    return (jnp.swapaxes(rows, 0, 1) if name in COL_SHARDED else rows)[None]


def kernel(x, meta_tokens, norm1_w, w_in, ssd_conv_w, ssd_conv_b, ssd_dt_bias, ssd_a_log, ssd_d, ssd_norm_w, lru_conv_w, lru_conv_b, lru_wa, lru_ba, lru_wx, lru_bx, lru_lambda, lru_norm_w, w_out, norm2_w, w_gate, w_up, w_down, final_norm_w, loss_target, m_meta_tokens, m_norm1_w, m_w_in, m_ssd_conv_w, m_ssd_conv_b, m_ssd_dt_bias, m_ssd_a_log, m_ssd_d, m_ssd_norm_w, m_lru_conv_w, m_lru_conv_b, m_lru_wa, m_lru_ba, m_lru_wx, m_lru_bx, m_lru_lambda, m_lru_norm_w, m_w_out, m_norm2_w, m_w_gate, m_w_up, m_w_down, m_final_norm_w, v_meta_tokens, v_norm1_w, v_w_in, v_ssd_conv_w, v_ssd_conv_b, v_ssd_dt_bias, v_ssd_a_log, v_ssd_d, v_ssd_norm_w, v_lru_conv_w, v_lru_conv_b, v_lru_wa, v_lru_ba, v_lru_wx, v_lru_bx, v_lru_lambda, v_lru_norm_w, v_w_out, v_norm2_w, v_w_gate, v_w_up, v_w_down, v_final_norm_w):
    w = dict(zip(WEIGHT_NAMES, (meta_tokens, norm1_w, w_in, ssd_conv_w, ssd_conv_b, ssd_dt_bias, ssd_a_log, ssd_d, ssd_norm_w, lru_conv_w, lru_conv_b, lru_wa, lru_ba, lru_wx, lru_bx, lru_lambda, lru_norm_w, w_out, norm2_w, w_gate, w_up, w_down, final_norm_w)))
    m = dict(zip(WEIGHT_NAMES, (m_meta_tokens, m_norm1_w, m_w_in, m_ssd_conv_w, m_ssd_conv_b, m_ssd_dt_bias, m_ssd_a_log, m_ssd_d, m_ssd_norm_w, m_lru_conv_w, m_lru_conv_b, m_lru_wa, m_lru_ba, m_lru_wx, m_lru_bx, m_lru_lambda, m_lru_norm_w, m_w_out, m_norm2_w, m_w_gate, m_w_up, m_w_down, m_final_norm_w)))
    v = dict(zip(WEIGHT_NAMES, (v_meta_tokens, v_norm1_w, v_w_in, v_ssd_conv_w, v_ssd_conv_b, v_ssd_dt_bias, v_ssd_a_log, v_ssd_d, v_ssd_norm_w, v_lru_conv_w, v_lru_conv_b, v_lru_wa, v_lru_ba, v_lru_wx, v_lru_bx, v_lru_lambda, v_lru_norm_w, v_w_out, v_norm2_w, v_w_gate, v_w_up, v_w_down, v_final_norm_w)))
    me = 2 * lax.axis_index("x") + lax.axis_index("y")

    big2d = {n: _rows_view(n, w[n]) for n in BIG}
    small_local = jnp.concatenate([w["meta_tokens"].reshape(-1), w["ssd_conv_w"].reshape(-1),
                                   w["lru_conv_w"].reshape(-1)])[None]
    me_arr = me.astype(jnp.int32).reshape(1)
    dev_arr = (2 * me + lax.axis_index("c")).astype(jnp.int32).reshape(1)
    small4 = _gather_small(small_local)
    (w_in_slot,) = _fill_own_slots([big2d["w_in"]], me_arr, name="own_slot_w_in")
    in_send, in_recv, in_bufs, in_tok = _split_start([w_in_slot], _halves_plan, 3, small4, name="gather_w_in_start")
    late_slots = _fill_own_slots([big2d[n] for n in LATE], me_arr, name="own_slots_late", behind=(in_tok,))
    sm = small4[:, 0]
    meta_full = _unshard_cols(sm[:, :4096].reshape(N_SHARDS, N_META, 256))
    ssd_conv_w_full = _unshard_cols(sm[:, 4096:5632].reshape(N_SHARDS, CONV_K, 384))
    lru_conv_w_full = _unshard_cols(sm[:, 5632:].reshape(N_SHARDS, CONV_K, 256))

    p = {"ssd_conv_w": ssd_conv_w_full, "lru_conv_w": lru_conv_w_full,
         "lru_wa": w["lru_wa"][0], "lru_wx": w["lru_wx"][0], "final_norm_w": w["final_norm_w"][None]}
    for n in ("norm1_w", "ssd_conv_b", "ssd_dt_bias", "ssd_a_log", "ssd_d", "ssd_norm_w", "lru_conv_b", "lru_ba",
              "lru_bx", "lru_lambda", "lru_norm_w", "norm2_w"):
        p[n] = w[n]

    class Late:
        def __init__(self):
            self.pending = []
            self.before_embed = (late_slots[0],)

        def w_in(self, after):
            (buf,) = _split_wait(in_bufs, in_send, in_recv, _halves_plan, after, name="gather_w_in_wait")
            send, recv, bufs, tok = _split_start([buf], _forward_plan, 3, None, name="forward_w_in_start")
            self.late_gather = _split_start(late_slots, _halves_plan, 3 * len(LATE), tok, name="gather_late_start")
            (w_in4,) = _split_wait(bufs, send, recv, _forward_plan, self.late_gather[2][0], name="forward_w_in_wait")
            sections = {s: _rows_of_shards(w_in4, lo, hi) for s, (lo, hi) in IN_ROWS.items()}
            sections["dt"] = jnp.pad(sections["dt"], ((0, SEC_WIDTH["dt"] - SSD_HEADS), (0, 0)))
            return sections

        def mid_forward(self, after):
            send, recv, bufs, _ = self.late_gather
            bufs = _split_wait(bufs, send, recv, _halves_plan, after, name="gather_late_wait")
            self.forward = _split_start(bufs, _forward_plan, 3 * len(LATE), None, name="forward_late_start")
            return self.forward[3][:1, :1]

        def w_out(self, after):
            send, recv, bufs, _ = self.forward
            bufs = _split_wait(bufs, send, recv, _forward_plan, after, name="forward_late_wait")
            self.late = dict(zip(LATE, (b.reshape(-1, D_MODEL) for b in bufs)))
            return self.late["w_out"]

        def ffn(self, after):
            return tuple(self.late[n] for n in FFN)

        def grads_ready(self, names, g, g_mxu):
            if names == ("w_in",):
                g_mxu["w_in"] = jnp.stack([_w_in_shard_rows(k, [g_mxu["w_in_" + s] for s in SEC_NAMES])
                                           for k in range(N_SHARDS)])
            srcs = [g_mxu[n].reshape(N_SHARDS, -1, D_MODEL) for n in names]
            lands = [lax.empty((3,) + s.shape[1:], _MXU) for s in srcs]
            tag = "_".join(names)
            send, recv, bufs, tok = _split_start(srcs + lands, _scatter_plan, 3 * len(names), None,
                                                 name="scatter_" + tag + "_start")
            self.pending.append((names, send, recv, bufs, tag))
            self.in_flight = bufs[0]
            return tok[:1, :1]

        def landed(self, after, which):
            land = {}
            for names, send, recv, bufs, tag in self.pending:
                if names[0] in which:
                    bufs = _split_wait(bufs, send, recv, _scatter_plan, after, name="scatter_" + tag + "_wait")
                    land.update(zip(names, bufs[len(names):]))
            return land

        def small_ready(self, g, loss):
            pack = _pack([g[n] for n in SMALL] + [loss[0, :1]], 8 * N_DEV)
            pack = pack.reshape(N_DEV, -1, PACK_COLS)
            self.small = _split_start([pack, lax.empty(pack.shape, F32)], _pieces_plan, N_DEV - 1, loss,
                                      name="small_pieces_start")
            return self.small[3]

        def small_middle(self, after):
            send, recv, bufs, _ = self.small
            pack, land = _split_wait(bufs, send, recv, _pieces_plan, after, name="small_pieces_wait")
            piece = _sum_pieces(pack, land, dev_arr, name="small_pieces_sum")
            self.small = _split_start([piece, lax.empty(pack.shape, F32)], _spread_plan, N_DEV - 1, None,
                                      name="small_spread_start")
            return self.small[3]

        def small_sum(self, after):
            send, recv, bufs, _ = self.small
            piece, land = _split_wait(bufs, send, recv, _spread_plan, after, name="small_spread_wait")
            return _join_pieces(piece, land, dev_arr, name="small_join")

    late = Late()

    loss, grad_x, g, g_mxu = _local_step(x[0], loss_target[0], meta_full, p, late)

    g4 = {n: g[n].reshape(N_SHARDS, -1, D_MODEL) for n in LATE}
    g4["w_in"] = lax.switch(me, [functools.partial(_w_in_shard_rows, k) for k in range(N_SHARDS)],
                            [g["w_in_" + s] for s in SEC_NAMES])
    land = late.landed([late.in_flight, late.small[2][0]], LATE)
    part = dict(zip(LATE, _partial_sums([g4[n] for n in LATE], [land[n] for n in LATE], me_arr,
                                        name="partial_late")))
    sib = dict(zip(LATE, _swap_with_sibling([part[n] for n in LATE], name="swap_late")))

    grad, delta, new_m, new_v = {}, {}, {}, {}
    late_outs = _adamws([big2d[n] for n in LATE], [part[n] for n in LATE], [sib[n] for n in LATE],
                        [_rows_view(n, m[n]) for n in LATE], [_rows_view(n, v[n]) for n in LATE], name="adamw_late")
    for d, outs in zip((grad, delta, new_m, new_v), late_outs):
        d.update({n: _param_view(n, o) for n, o in zip(LATE, outs)})

    land.update(late.landed(late_outs[0][0], ("w_in",)))
    spread = late.small_middle(land["w_in"])
    part_in = _partial_sum(g4["w_in"], land["w_in"], me_arr, name="partial_w_in")
    (sib_in,) = _swap_with_sibling([part_in], name="swap_w_in", behind=(spread,))
    lanes = lambda a: a[0].reshape(8, 128, -1).transpose(2, 0, 1)
    pieces = lambda a: a.reshape(-1, 8, 128)
    outs = _adamw(lanes(w["w_in"]), [pieces(part_in), pieces(sib_in)], lanes(m["w_in"]), lanes(v["w_in"]),
                  name="adamw_w_in")
    grad["w_in"], delta["w_in"], new_m["w_in"], new_v["w_in"] = (o.transpose(1, 2, 0).reshape(1, D_MODEL, -1)
                                                                 for o in outs)

    small_full_shape = {n: (SMALL_SHARDED[n] if n in SMALL_SHARDED else w[n].shape) for n in SMALL}
    red_list = _unpack(late.small_sum(outs[0]), [small_full_shape[n] for n in SMALL] + [(1,)])
    loss_total = red_list[-1][0]
    g_small = {}
    for n, arr in zip(SMALL, red_list[:-1]):
        if n in SMALL_SHARDED:
            cols = SMALL_SHARDED[n][1] // N_SHARDS
            arr = lax.dynamic_slice_in_dim(arr, me * cols, cols, axis=1)
        g_small[n] = arr.reshape(w[n].shape)
    two_d = lambda a: a.reshape(1, -1) if a.ndim == 1 else a
    deltas, new_ms, new_vs = _adamw_native(*[[two_d(d[n]) for n in SMALL] for d in (w, g_small, m, v)])
    for n, dn, mn, vn in zip(SMALL, deltas, new_ms, new_vs):
        grad[n], delta[n], new_m[n], new_v[n] = (g_small[n], dn.reshape(w[n].shape), mn.reshape(w[n].shape),
                                                 vn.reshape(w[n].shape))

    return (loss_total, grad_x[None], *[grad[n] for n in WEIGHT_NAMES], *[delta[n] for n in WEIGHT_NAMES],
            *[new_m[n] for n in WEIGHT_NAMES], *[new_v[n] for n in WEIGHT_NAMES])
```

```python
import functools
import math

import jax
import jax.numpy as jnp
from jax import lax
from jax.experimental import pallas as pl
from jax.experimental.pallas import tpu as pltpu

F32 = jnp.float32
_MXU = jnp.bfloat16

D_MODEL = 1024
SEQ = 2048
N_META = 16
CHUNK = 128
T_ROWS = 2176
N_CHUNKS = T_ROWS // CHUNK
PAD_ROWS = T_ROWS - SEQ - N_META
X_ROW0 = PAD_ROWS + N_META
SSD_HEADS = 16
SSD_HEAD_DIM = 64
SSD_STATE = 128
SSD_GROUPS = 2
SSD_HPG = SSD_HEADS // SSD_GROUPS
SSD_WIDTH = 1024
LRU_WIDTH = 1024
LRU_C = 8.0
D_FF = 2816
EPS = 1e-6
IN_COLS = 4624
N_SHARDS = 4
N_DEV = 8

ADAM_LR = 0.001
ADAM_B1 = 0.9
ADAM_B2 = 0.999
ADAM_EPS = 1e-08
ADAM_WD = 0.01
ADAM_STEP = 10

VMEM_LIMIT_BYTES = 56 * 1024 * 1024

NN = (((1,), (0,)), ((), ()))
NT = (((1,), (1,)), ((), ()))
TN = (((0,), (0,)), ((), ()))


def _cparams(*sem):
    return pltpu.CompilerParams(dimension_semantics=sem, vmem_limit_bytes=VMEM_LIMIT_BYTES)


def _dot(a, b, dims=NN):
    return lax.dot_general(a.astype(_MXU), b.astype(_MXU), dims, preferred_element_type=F32)


def _dot_onehot(a, b, dims=NN, *, data=0, pieces=3):
    ops = [a, b]
    mask = ops[1 - data].astype(jnp.bfloat16)
    rest = ops[data]
    acc = None
    for _ in range(pieces):
        piece = rest.astype(jnp.bfloat16)
        ops[data], ops[1 - data] = piece, mask
        d = lax.dot_general(ops[0], ops[1], dims, preferred_element_type=F32)
        acc = d if acc is None else acc + d
        rest = rest - piece.astype(F32)
    return acc


def _sigmoid(x):
    return 0.5 * (1.0 + jnp.tanh(0.5 * x))


def _softplus(x):
    return jnp.maximum(x, 0.0) + jnp.log(1.0 + jnp.exp(-jnp.abs(x)))


def _silu(x):
    return x * _sigmoid(x)


def _silu_grad(x):
    s = _sigmoid(x)
    return s * (1.0 + x * (1.0 - s))


_GELU_C = math.sqrt(2.0 / math.pi)


def _gelu_and_grad(x):
    inner = _GELU_C * (x + 0.044715 * x * x * x)
    t = jnp.tanh(inner)
    g = 0.5 * x * (1.0 + t)
    dg = 0.5 * (1.0 + t) + 0.5 * x * (1.0 - t * t) * _GELU_C * (1.0 + 3.0 * 0.044715 * x * x)
    return g, dg


def _rms_fwd(x, w):
    rstd = lax.rsqrt(jnp.mean(x * x, axis=-1, keepdims=True) + EPS)
    return x * rstd * w


def _rms_bwd(x, w, dy):
    rstd = lax.rsqrt(jnp.mean(x * x, axis=-1, keepdims=True) + EPS)
    xhat = x * rstd
    dxhat = dy * w
    dx = rstd * (dxhat - xhat * jnp.mean(dxhat * xhat, axis=-1, keepdims=True))
    return dx, dy * xhat


def _mm(terms, m, n, *, tm, tn, mode, out_dtype, name, residual=None, n_outer=False, also_mxu=False, behind=()):
    gm, gn = m // tm, n // tn
    assert gm * tm == m and gn * tn == n
    if n_outer:
        grid = (gn, gm)
        mi = lambda g0, g1: g1
        ni = lambda g0, g1: g0
    else:
        grid = (gm, gn)
        mi = lambda g0, g1: g0
        ni = lambda g0, g1: g1
    in_specs, args = [], []
    for (a, ka, b, kb, k) in terms:
        if mode == "tn":
            in_specs.append(pl.BlockSpec((k, tm), lambda g0, g1, ka=ka: (ka, mi(g0, g1))))
        else:
            in_specs.append(pl.BlockSpec((tm, k), lambda g0, g1, ka=ka: (mi(g0, g1), ka)))
        if mode == "nt":
            in_specs.append(pl.BlockSpec((tn, k), lambda g0, g1, kb=kb: (ni(g0, g1), kb)))
        else:
            in_specs.append(pl.BlockSpec((k, tn), lambda g0, g1, kb=kb: (kb, ni(g0, g1))))
        args += [a, b]
    if residual is not None:
        in_specs.append(pl.BlockSpec((tm, tn), lambda g0, g1: (mi(g0, g1), ni(g0, g1))))
        args.append(residual)
    dims = {"nn": NN, "nt": NT, "tn": TN}[mode]
    n_terms = len(terms)
    has_res = residual is not None
    in_specs += [pl.BlockSpec(memory_space=pl.ANY)] * len(behind)
    args += list(behind)
    n_in = len(args)

    def body(*refs):
        acc = None
        for t in range(n_terms):
            d = lax.dot_general(refs[2 * t][...], refs[2 * t + 1][...], dims, preferred_element_type=F32)
            acc = d if acc is None else acc + d
        if has_res:
            acc = acc + refs[2 * n_terms][...]
        refs[n_in][...] = acc.astype(out_dtype)
        if also_mxu:
            refs[n_in + 1][...] = acc.astype(_MXU)

    tile = pl.BlockSpec((tm, tn), lambda g0, g1: (mi(g0, g1), ni(g0, g1)))
    shape = jax.ShapeDtypeStruct((m, n), out_dtype)
    return pl.pallas_call(
        body, name=name, grid=grid, in_specs=in_specs,
        out_specs=[tile, tile] if also_mxu else tile,
        out_shape=[shape, jax.ShapeDtypeStruct((m, n), _MXU)] if also_mxu else shape,
        compiler_params=_cparams("parallel", "parallel"),
    )(*args)


def _embed(x, meta, behind=()):
    def body(x_ref, meta_ref, *rest):
        o_ref = rest[-1]
        i = pl.program_id(0)

        @pl.when(i == 0)
        def _():
            o_ref[0:PAD_ROWS, :] = jnp.zeros((PAD_ROWS, D_MODEL), F32)
            o_ref[PAD_ROWS:CHUNK, :] = meta_ref[...]

        @pl.when(i > 0)
        def _():
            o_ref[...] = x_ref[...]

    return pl.pallas_call(
        body, name="embed", grid=(N_CHUNKS,),
        in_specs=[pl.BlockSpec((CHUNK, D_MODEL), lambda i: (jnp.maximum(i - 1, 0), 0)),
                  pl.BlockSpec((N_META, D_MODEL), lambda i: (0, 0))] + [pl.BlockSpec(memory_space=pl.ANY)] * len(behind),
        out_specs=pl.BlockSpec((CHUNK, D_MODEL), lambda i: (i, 0)),
        out_shape=jax.ShapeDtypeStruct((T_ROWS, D_MODEL), F32),
        compiler_params=_cparams("parallel"),
    )(x, meta, *behind)


def _rmsnorm(h, w, *, name, tm=544):
    def body(h_ref, w_ref, o_ref):
        o_ref[...] = _rms_fwd(h_ref[...], w_ref[...]).astype(_MXU)

    return pl.pallas_call(
        body, name=name, grid=(T_ROWS // tm,),
        in_specs=[pl.BlockSpec((tm, D_MODEL), lambda i: (i, 0)), pl.BlockSpec((1, D_MODEL), lambda i: (0, 0))],
        out_specs=pl.BlockSpec((tm, D_MODEL), lambda i: (i, 0)),
        out_shape=jax.ShapeDtypeStruct((T_ROWS, D_MODEL), _MXU),
        compiler_params=_cparams("parallel"),
    )(h, w)


def _norm_proj(h, w, sections, *, name, tm=544):
    widths = [s.shape[0] for s in sections]
    n = len(sections)

    def body(*refs):
        h_ref, w_ref = refs[:2]
        u_ref = refs[2 + n]
        u = _rms_fwd(h_ref[...], w_ref[...]).astype(_MXU)
        u_ref[...] = u
        for k in range(n):
            refs[3 + n + k][...] = lax.dot_general(u, refs[2 + k][...], NT, preferred_element_type=F32)

    row = lambda width: pl.BlockSpec((tm, width), lambda i: (i, 0))
    outs = pl.pallas_call(
        body, name=name, grid=(T_ROWS // tm,),
        in_specs=[row(D_MODEL), pl.BlockSpec((1, D_MODEL), lambda i: (0, 0))]
        + [pl.BlockSpec((wd, D_MODEL), lambda i: (0, 0)) for wd in widths],
        out_specs=[row(D_MODEL)] + [row(wd) for wd in widths],
        out_shape=[jax.ShapeDtypeStruct((T_ROWS, D_MODEL), _MXU)]
        + [jax.ShapeDtypeStruct((T_ROWS, wd), F32) for wd in widths],
        compiler_params=_cparams("parallel"),
    )(h, w, *sections)
    return outs[0], list(outs[1:])


def _loss_head(h2, target, fw):
    def body(h_ref, t_ref, w_ref, loss_ref, dh_ref, dhb_ref, dw_ref, acc_ref):
        i = pl.program_id(0)

        @pl.when(i == 0)
        def _():
            acc_ref[...] = jnp.zeros_like(acc_ref)
            dw_ref[...] = jnp.zeros_like(dw_ref)

        h = h_ref[...]
        w = w_ref[...]
        y = _rms_fwd(h, w)
        live = (i > 0).astype(F32)
        err = (y - t_ref[...]) * live
        acc_ref[...] += jnp.sum(err * err, axis=0, keepdims=True)
        dy = err * (1.0 / D_MODEL)
        dx, dwr = _rms_bwd(h, w, dy)
        dh_ref[...] = dx
        dhb_ref[...] = dx.astype(_MXU)
        dw_ref[...] += jnp.sum(dwr, axis=0, keepdims=True)

        @pl.when(i == N_CHUNKS - 1)
        def _():
            tot = jnp.sum(acc_ref[...], axis=1, keepdims=True) * (0.5 / D_MODEL)
            loss_ref[...] = jnp.broadcast_to(tot, (1, 128))

    return pl.pallas_call(
        body, name="loss_head", grid=(N_CHUNKS,),
        in_specs=[pl.BlockSpec((CHUNK, D_MODEL), lambda i: (i, 0)),
                  pl.BlockSpec((CHUNK, D_MODEL), lambda i: (jnp.maximum(i - 1, 0), 0)),
                  pl.BlockSpec((1, D_MODEL), lambda i: (0, 0))],
        out_specs=[pl.BlockSpec((1, 128), lambda i: (0, 0)),
                   pl.BlockSpec((CHUNK, D_MODEL), lambda i: (i, 0)),
                   pl.BlockSpec((CHUNK, D_MODEL), lambda i: (i, 0)),
                   pl.BlockSpec((1, D_MODEL), lambda i: (0, 0))],
        out_shape=[jax.ShapeDtypeStruct((1, 128), F32),
                   jax.ShapeDtypeStruct((T_ROWS, D_MODEL), F32),
                   jax.ShapeDtypeStruct((T_ROWS, D_MODEL), _MXU),
                   jax.ShapeDtypeStruct((1, D_MODEL), F32)],
        scratch_shapes=[pltpu.VMEM((1, D_MODEL), F32)],
        compiler_params=_cparams("arbitrary"),
    )(h2, target, fw)


def _mm_norm_bwd(terms, h, w, dres, *, name, tm=272, behind=()):
    n_terms = len(terms)
    in_specs, args = [], []
    for (a, b, k) in terms:
        in_specs += [pl.BlockSpec((tm, k), lambda i: (i, 0)), pl.BlockSpec((k, D_MODEL), lambda i: (0, 0))]
        args += [a, b]
    in_specs += [pl.BlockSpec((tm, D_MODEL), lambda i: (i, 0)), pl.BlockSpec((1, D_MODEL), lambda i: (0, 0)),
                 pl.BlockSpec((tm, D_MODEL), lambda i: (i, 0))] + [pl.BlockSpec(memory_space=pl.ANY)] * len(behind)
    args += [h, w, dres, *behind]

    def body(*refs):
        h_ref, w_ref, dres_ref = refs[2 * n_terms:2 * n_terms + 3]
        dh_ref, dhb_ref, dw_ref = refs[2 * n_terms + 3 + len(behind):]

        @pl.when(pl.program_id(0) == 0)
        def _():
            dw_ref[...] = jnp.zeros_like(dw_ref)

        du = None
        for t in range(n_terms):
            d = lax.dot_general(refs[2 * t][...], refs[2 * t + 1][...], NN, preferred_element_type=F32)
            du = d if du is None else du + d
        dx, dwr = _rms_bwd(h_ref[...], w_ref[...], du)
        dh = dres_ref[...] + dx
        dh_ref[...] = dh
        dhb_ref[...] = dh.astype(_MXU)
        dw_ref[...] += jnp.sum(dwr, axis=0, keepdims=True)

    return pl.pallas_call(
        body, name=name, grid=(T_ROWS // tm,), in_specs=in_specs,
        out_specs=[pl.BlockSpec((tm, D_MODEL), lambda i: (i, 0)), pl.BlockSpec((tm, D_MODEL), lambda i: (i, 0)),
                   pl.BlockSpec((1, D_MODEL), lambda i: (0, 0))],
        out_shape=[jax.ShapeDtypeStruct((T_ROWS, D_MODEL), F32), jax.ShapeDtypeStruct((T_ROWS, D_MODEL), _MXU),
                   jax.ShapeDtypeStruct((1, D_MODEL), F32)],
        compiler_params=_cparams("arbitrary"),
    )(*args)


FFN_TM = T_ROWS
FFN_TN = 256


def _ffn_up(u2, wg_t, wu_t):
    def body(u_ref, wg_ref, wu_ref, gp_ref, up_ref, act_ref, act_t_ref):
        u = u_ref[...]
        gp = lax.dot_general(u, wg_ref[...], NT, preferred_element_type=F32)
        up = lax.dot_general(u, wu_ref[...], NT, preferred_element_type=F32)
        gp_ref[...] = gp.astype(_MXU)
        up_ref[...] = up.astype(_MXU)
        act = _silu(gp) * up
        act_ref[...] = act.astype(_MXU)
        act_t_ref[...] = act.T.astype(_MXU)

    tile = pl.BlockSpec((FFN_TM, FFN_TN), lambda j, i: (i, j))
    tile_t = pl.BlockSpec((FFN_TN, FFN_TM), lambda j, i: (j, i))
    return pl.pallas_call(
        body, name="ffn_up", grid=(D_FF // FFN_TN, T_ROWS // FFN_TM),
        in_specs=[pl.BlockSpec((FFN_TM, D_MODEL), lambda j, i: (i, 0)),
                  pl.BlockSpec((FFN_TN, D_MODEL), lambda j, i: (j, 0)),
                  pl.BlockSpec((FFN_TN, D_MODEL), lambda j, i: (j, 0))],
        out_specs=[tile, tile, tile, tile_t],
        out_shape=[jax.ShapeDtypeStruct((T_ROWS, D_FF), _MXU)] * 3 + [jax.ShapeDtypeStruct((D_FF, T_ROWS), _MXU)],
        compiler_params=_cparams("parallel", "parallel"),
    )(u2, wg_t, wu_t)


def _ffn_bwd_act(dh2b, wd, gp, up):
    def body(dh_ref, wd_ref, gp_ref, up_ref, dgp_ref, dup_ref, dgp_t_ref, dup_t_ref):
        dact = lax.dot_general(dh_ref[...], wd_ref[...], NT, preferred_element_type=F32)
        gp = gp_ref[...].astype(F32)
        dgp = dact * up_ref[...].astype(F32) * _silu_grad(gp)
        dup = dact * _silu(gp)
        dgp_ref[...] = dgp.astype(_MXU)
        dup_ref[...] = dup.astype(_MXU)
        dgp_t_ref[...] = dgp.T.astype(_MXU)
        dup_t_ref[...] = dup.T.astype(_MXU)

    tile = pl.BlockSpec((FFN_TM, FFN_TN), lambda j, i: (i, j))
    tile_t = pl.BlockSpec((FFN_TN, FFN_TM), lambda j, i: (j, i))
    return pl.pallas_call(
        body, name="ffn_bwd_act", grid=(D_FF // FFN_TN, T_ROWS // FFN_TM),
        in_specs=[pl.BlockSpec((FFN_TM, D_MODEL), lambda j, i: (i, 0)),
                  pl.BlockSpec((FFN_TN, D_MODEL), lambda j, i: (j, 0)), tile, tile],
        out_specs=[tile, tile, tile_t, tile_t],
        out_shape=[jax.ShapeDtypeStruct((T_ROWS, D_FF), _MXU)] * 2 + [jax.ShapeDtypeStruct((D_FF, T_ROWS), _MXU)] * 2,
        compiler_params=_cparams("parallel", "parallel"),
    )(dh2b, wd, gp, up)


CONV_TC = 512
CONV_K = 4


def _conv_pre(x_ref, wv, bv, c):
    tc = wv.shape[1]
    r0 = c * CHUNK
    cur = x_ref[r0:r0 + CHUNK, :]
    if c == 0:
        cat = jnp.concatenate([jnp.zeros((8, tc), F32), cur], axis=0)
        shifted = [cur] + [pltpu.roll(cat, s, 0)[8:8 + CHUNK] for s in range(1, CONV_K)]
    else:
        shifted = [cur] + [x_ref[r0 - s:r0 - s + CHUNK, :] for s in range(1, CONV_K)]
    pre = bv
    for s in range(CONV_K):
        pre = pre + shifted[s] * wv[CONV_K - 1 - s:CONV_K - s]
    return pre, shifted


def _row_mask(c):
    if c > 0:
        return None
    return (lax.broadcasted_iota(jnp.int32, (CHUNK, 1), 0) >= PAD_ROWS).astype(F32)


def _conv_fwd(x, w, b, *, silu, name):
    cols = x.shape[1]
    tc = min(CONV_TC, cols)

    def body(x_ref, w_ref, b_ref, o_ref):
        wv, bv = w_ref[...], b_ref[...]
        for c in range(N_CHUNKS):
            pre, _ = _conv_pre(x_ref, wv, bv, c)
            y = _silu(pre) if silu else pre
            mask = _row_mask(c)
            if mask is not None:
                y = y * mask
            o_ref[c * CHUNK:(c + 1) * CHUNK, :] = y

    return pl.pallas_call(
        body, name=name, grid=(cols // tc,),
        in_specs=[pl.BlockSpec((T_ROWS, tc), lambda j: (0, j)), pl.BlockSpec((CONV_K, tc), lambda j: (0, j)),
                  pl.BlockSpec((1, tc), lambda j: (0, j))],
        out_specs=pl.BlockSpec((T_ROWS, tc), lambda j: (0, j)),
        out_shape=jax.ShapeDtypeStruct((T_ROWS, cols), F32),
        compiler_params=_cparams("parallel"),
    )(x, w, b)


def _conv_bwd(dy, x, w, b, *, silu, name):
    cols = x.shape[1]
    tc = min(CONV_TC, cols)

    def body(dy_ref, x_ref, w_ref, b_ref, dx_ref, dw_ref, db_ref):
        wv, bv = w_ref[...], b_ref[...]
        next8 = jnp.zeros((8, tc), F32)
        dws = [jnp.zeros((1, tc), F32) for _ in range(CONV_K)]
        db = jnp.zeros((1, tc), F32)
        for c in reversed(range(N_CHUNKS)):
            r0 = c * CHUNK
            pre, shifted = _conv_pre(x_ref, wv, bv, c)
            dpre = dy_ref[r0:r0 + CHUNK, :]
            if silu:
                dpre = dpre * _silu_grad(pre)
            mask = _row_mask(c)
            if mask is not None:
                dpre = dpre * mask
            cat = jnp.concatenate([dpre, next8], axis=0)
            dx = dpre * wv[CONV_K - 1:CONV_K]
            for s in range(1, CONV_K):
                dx = dx + pltpu.roll(cat, CHUNK + 8 - s, 0)[0:CHUNK] * wv[CONV_K - 1 - s:CONV_K - s]
            dx_ref[r0:r0 + CHUNK, :] = dx.astype(_MXU)
            for s in range(CONV_K):
                k = CONV_K - 1 - s
                dws[k] = dws[k] + jnp.sum(dpre * shifted[s], axis=0, keepdims=True)
            db = db + jnp.sum(dpre, axis=0, keepdims=True)
            next8 = dpre[0:8]
        dw_ref[...] = jnp.concatenate(dws, axis=0)
        db_ref[...] = db

    return pl.pallas_call(
        body, name=name, grid=(cols // tc,),
        in_specs=[pl.BlockSpec((T_ROWS, tc), lambda j: (0, j)), pl.BlockSpec((T_ROWS, tc), lambda j: (0, j)),
                  pl.BlockSpec((CONV_K, tc), lambda j: (0, j)), pl.BlockSpec((1, tc), lambda j: (0, j))],
        out_specs=[pl.BlockSpec((T_ROWS, tc), lambda j: (0, j)), pl.BlockSpec((CONV_K, tc), lambda j: (0, j)),
                   pl.BlockSpec((1, tc), lambda j: (0, j))],
        out_shape=[jax.ShapeDtypeStruct((T_ROWS, cols), _MXU), jax.ShapeDtypeStruct((CONV_K, cols), F32),
                   jax.ShapeDtypeStruct((1, cols), F32)],
        compiler_params=_cparams("parallel"),
    )(dy, x, w, b)


def _ssd_chunk_common(dt_raw, prm, c):
    a_row = -jnp.exp(prm[1:2])
    dt = _softplus(dt_raw + prm[0:1])
    rows = lax.broadcasted_iota(jnp.int32, (CHUNK, 1), 0)
    real = jnp.logical_or(c > 0, rows >= PAD_ROWS)
    dt = jnp.where(real, dt, 0.0)
    li = lax.broadcasted_iota(jnp.int32, (CHUNK, CHUNK), 0)
    si = lax.broadcasted_iota(jnp.int32, (CHUNK, CHUNK), 1)
    causal = li >= si
    tri = causal.astype(F32)
    cs = _dot_onehot(tri, dt * a_row, data=1)
    return dt, a_row, cs, cs.T, causal, tri, real


def _gated_norm_fwd(y, z, w):
    g = y * _silu(z)
    half = SSD_WIDTH // SSD_GROUPS
    outs = [_rms_fwd(g[:, k * half:(k + 1) * half], w[:, k * half:(k + 1) * half]) for k in range(SSD_GROUPS)]
    return jnp.concatenate(outs, axis=1)


GROUP_W = SSD_WIDTH // SSD_GROUPS
PAIR_W = 2 * SSD_HEAD_DIM
STATE_SHAPE = (SSD_GROUPS, SSD_STATE, GROUP_W)


def _head_expander():
    r = lax.broadcasted_iota(jnp.int32, (128, SSD_WIDTH), 0)
    c = lax.broadcasted_iota(jnp.int32, (128, SSD_WIDTH), 1)
    return (c // SSD_HEAD_DIM == r).astype(F32)


def _ssd_expand(dt, cs, prm, ex):
    cs_x = _dot_onehot(cs, ex)
    cs_last_x = cs_x[CHUNK - 1:CHUNK, :]
    return (_dot_onehot(dt, ex, pieces=2), _dot_onehot(prm, ex)[2:3], jnp.exp(cs_x), jnp.exp(cs_last_x),
            jnp.exp(cs_last_x - cs_x))


def _ssd_fwd(xs, bc, dt_raw, z, prm, norm_w, ex):
    def body(xs_ref, bc_ref, dt_ref, z_ref, prm_ref, nw_ref, ex_ref, y_ref, yn_ref, prev_ref, state):
        c = pl.program_id(0)

        @pl.when(c == 0)
        def _():
            state[...] = jnp.zeros_like(state)

        prm = prm_ref[...]
        dt, a_row, cs, cs_t, causal, _, _ = _ssd_chunk_common(dt_ref[...], prm, c)
        dt_x, d_x, e_cs_x, e_last_x, dec_x = _ssd_expand(dt, cs, prm, ex_ref[...])
        xs_all = xs_ref[...]
        bc_all = bc_ref[...]
        xdt = xs_all * dt_x
        xdec = xdt * dec_x
        lane_lo = lax.broadcasted_iota(jnp.int32, (1, PAIR_W), 1) < SSD_HEAD_DIM
        for g in range(SSD_GROUPS):
            gs = slice(g * GROUP_W, (g + 1) * GROUP_W)
            b_g = bc_all[:, g * SSD_STATE:(g + 1) * SSD_STATE]
            c_g = bc_all[:, (SSD_GROUPS + g) * SSD_STATE:(SSD_GROUPS + g + 1) * SSD_STATE]
            st = state[g]
            prev_ref[0, g] = st
            y_off = _dot(c_g, st) * e_cs_x[:, gs]
            state[g] = st * e_last_x[:, gs] + _dot(b_g.T, xdec[:, gs])
            cb = _dot(c_g, b_g, NT)
            for k in range(SSD_HPG // 2):
                h0 = g * SSD_HPG + 2 * k
                ps = slice(h0 * SSD_HEAD_DIM, h0 * SSD_HEAD_DIM + PAIR_W)
                xdt_pair = xdt[:, ps]
                yd = []
                for h in (h0, h0 + 1):
                    lmat = jnp.where(causal, jnp.exp(cs[:, h:h + 1] - cs_t[h:h + 1, :]), 0.0)
                    yd.append(_dot(cb * lmat, xdt_pair))
                y_ref[:, ps] = (jnp.where(lane_lo, yd[0], yd[1]) + y_off[:, k * PAIR_W:(k + 1) * PAIR_W]
                                + xs_all[:, ps] * d_x[:, ps])
        yn_ref[...] = _gated_norm_fwd(y_ref[...], z_ref[...], nw_ref[...]).astype(_MXU)

    row = lambda w: pl.BlockSpec((CHUNK, w), lambda c: (c, 0))
    return pl.pallas_call(
        body, name="ssd_fwd", grid=(N_CHUNKS,),
        in_specs=[row(SSD_WIDTH), row(512), row(128), row(SSD_WIDTH),
                  pl.BlockSpec((8, 128), lambda c: (0, 0)), pl.BlockSpec((1, SSD_WIDTH), lambda c: (0, 0)),
                  pl.BlockSpec((128, SSD_WIDTH), lambda c: (0, 0))],
        out_specs=[row(SSD_WIDTH), row(SSD_WIDTH),
                   pl.BlockSpec((1,) + STATE_SHAPE, lambda c: (c, 0, 0, 0))],
        out_shape=[jax.ShapeDtypeStruct((T_ROWS, SSD_WIDTH), F32), jax.ShapeDtypeStruct((T_ROWS, SSD_WIDTH), _MXU),
                   jax.ShapeDtypeStruct((N_CHUNKS,) + STATE_SHAPE, F32)],
        scratch_shapes=[pltpu.VMEM(STATE_SHAPE, F32)],
        compiler_params=_cparams("arbitrary"),
    )(xs, bc, dt_raw, z, prm, norm_w, ex)


def _ssd_bwd(dyn, dyn_block, z, y_pre, xs, bc, dt_raw, prev, prm, norm_w, ex):
    def body(dyn_ref, z_ref, y_ref, xs_ref, bc_ref, dt_ref, prev_ref, prm_ref, nw_ref, ex_ref,
             dz_ref, dxs_ref, dbc_ref, ddt_ref, dprm_ref, dnw_ref, dstate):
        step = pl.program_id(0)
        c = N_CHUNKS - 1 - step

        @pl.when(step == 0)
        def _():
            dstate[...] = jnp.zeros_like(dstate)
            dprm_ref[...] = jnp.zeros_like(dprm_ref)
            dnw_ref[...] = jnp.zeros_like(dnw_ref)

        prm = prm_ref[...]
        dt, a_row, cs, cs_t, causal, tri, real = _ssd_chunk_common(dt_ref[...], prm, c)
        realf = real.astype(F32)
        z = z_ref[...]
        y_all = y_ref[...]
        nw = nw_ref[...]
        dyn_all = dyn_ref[...]
        sz = _silu(z)
        gated = y_all * sz
        half = SSD_WIDTH // SSD_GROUPS
        dgs, dnws = [], []
        for k in range(SSD_GROUPS):
            sl = slice(k * half, (k + 1) * half)
            dgk, dwk = _rms_bwd(gated[:, sl], nw[:, sl], dyn_all[:, sl])
            dgs.append(dgk)
            dnws.append(jnp.sum(dwk, axis=0, keepdims=True))
        dgated = jnp.concatenate(dgs, axis=1)
        dnw_ref[...] += jnp.concatenate(dnws, axis=1)
        dz_ref[...] = (dgated * y_all * _silu_grad(z)).astype(_MXU)
        dy_all = dgated * sz

        ex = ex_ref[...]
        dt_x, d_x, e_cs_x, e_last_x, dec_x = _ssd_expand(dt, cs, prm, ex)
        xs_all = xs_ref[...]
        bc_all = bc_ref[...]
        xdt = xs_all * dt_x
        xdt_mxu = xdt.astype(_MXU).astype(F32)
        xdec = xdt * dec_x
        dcp = dy_all * e_cs_x
        lane_lo = lax.broadcasted_iota(jnp.int32, (1, PAIR_W), 1) < SSD_HEAD_DIM
        upper = (lax.broadcasted_iota(jnp.int32, (CHUNK, CHUNK), 0)
                 <= lax.broadcasted_iota(jnp.int32, (CHUNK, CHUNK), 1))
        last_row = (lax.broadcasted_iota(jnp.int32, (CHUNK, 1), 0) == CHUNK - 1).astype(F32)
        dbs, dcs_, dxdt_parts, last_parts = [], [], [], []
        for g in range(SSD_GROUPS):
            gs = slice(g * GROUP_W, (g + 1) * GROUP_W)
            b_g = bc_all[:, g * SSD_STATE:(g + 1) * SSD_STATE]
            c_g = bc_all[:, (SSD_GROUPS + g) * SSD_STATE:(SSD_GROUPS + g + 1) * SSD_STATE]
            prev_t = prev_ref[0, g]
            dst = dstate[g]
            dc_g = _dot(dcp[:, gs], prev_t, NT)
            db_g = _dot(xdec[:, gs], dst, NT)
            dxdt_state = _dot(b_g, dst) * dec_x[:, gs]
            dstate[g] = dst * e_last_x[:, gs] + _dot(c_g.T, dcp[:, gs])
            last_parts.append(jnp.sum(xdt_mxu[:, gs] * dxdt_state, axis=0, keepdims=True)
                              + jnp.sum(dst * prev_t, axis=0, keepdims=True) * e_last_x[:, gs])
            cb_t = _dot(b_g, c_g, NT)
            dcb_t = jnp.zeros((CHUNK, CHUNK), F32)
            for k in range(SSD_HPG // 2):
                h0 = g * SSD_HPG + 2 * k
                ps = slice(h0 * SSD_HEAD_DIM, h0 * SSD_HEAD_DIM + PAIR_W)
                dy_pair = dy_all[:, ps]
                xdt_pair = xdt[:, ps]
                dd = []
                for h in (h0, h0 + 1):
                    lmat_t = jnp.where(upper, jnp.exp(cs_t[h:h + 1, :] - cs[:, h:h + 1]), 0.0)
                    dd.append(_dot(cb_t * lmat_t, dy_pair))
                    mine = lane_lo if h == h0 else jnp.logical_not(lane_lo)
                    dcb_t = dcb_t + _dot(jnp.where(mine, xdt_pair, 0.0), dy_pair, NT) * lmat_t
                dxdt_parts.append(jnp.where(lane_lo, dd[0], dd[1]) + dxdt_state[:, k * PAIR_W:(k + 1) * PAIR_W])
            dc_g = dc_g + _dot(dcb_t, b_g, TN)
            db_g = db_g + _dot(dcb_t, c_g)
            dbs.append(db_g * realf)
            dcs_.append(dc_g * realf)
        dbc_ref[...] = jnp.concatenate(dbs + dcs_, axis=1)
        dxdt = jnp.concatenate(dxdt_parts, axis=1)
        dxs_ref[...] = (dxdt * dt_x + dy_all * d_x) * realf
        ddt_all = _dot_onehot(dxdt * xs_all, ex, NT, pieces=2)
        rows = jnp.concatenate([jnp.concatenate(last_parts, axis=1), jnp.sum(dy_all * xs_all, axis=0, keepdims=True),
                                jnp.zeros((6, SSD_WIDTH), F32)], axis=0)
        rows = _dot_onehot(rows, ex, NT, pieces=2)
        dd_row = rows[1:2]
        dy_mxu = dy_all.astype(_MXU).astype(F32)
        dcs_all = (_dot_onehot(dy_mxu * (y_all - xs_all * d_x), ex, NT) - _dot_onehot(xdt_mxu * dxdt, ex, NT)
                   + last_row * rows[0:1])
        dda = _dot_onehot(tri, dcs_all, TN, data=1)
        ddt = (ddt_all + dda * a_row) * realf
        ddt_raw = ddt * _sigmoid(dt_ref[...] + prm[0:1])
        ddt_ref[...] = ddt_raw.astype(_MXU)
        da_log = jnp.sum(dda * dt, axis=0, keepdims=True) * a_row
        dprm_ref[0:1, :] += jnp.sum(ddt_raw, axis=0, keepdims=True)
        dprm_ref[1:2, :] += da_log
        dprm_ref[2:3, :] += dd_row

    rev = lambda w, blk=0: pl.BlockSpec((CHUNK, w), lambda s, blk=blk: (N_CHUNKS - 1 - s, blk))
    return pl.pallas_call(
        body, name="ssd_bwd", grid=(N_CHUNKS,),
        in_specs=[rev(SSD_WIDTH, dyn_block), rev(SSD_WIDTH), rev(SSD_WIDTH), rev(SSD_WIDTH), rev(512), rev(128),
                  pl.BlockSpec((1,) + STATE_SHAPE, lambda s: (N_CHUNKS - 1 - s, 0, 0, 0)),
                  pl.BlockSpec((8, 128), lambda s: (0, 0)), pl.BlockSpec((1, SSD_WIDTH), lambda s: (0, 0)),
                  pl.BlockSpec((128, SSD_WIDTH), lambda s: (0, 0))],
        out_specs=[rev(SSD_WIDTH), rev(SSD_WIDTH), rev(512), rev(128),
                   pl.BlockSpec((8, 128), lambda s: (0, 0)), pl.BlockSpec((1, SSD_WIDTH), lambda s: (0, 0))],
        out_shape=[jax.ShapeDtypeStruct((T_ROWS, SSD_WIDTH), _MXU), jax.ShapeDtypeStruct((T_ROWS, SSD_WIDTH), F32),
                   jax.ShapeDtypeStruct((T_ROWS, 512), F32), jax.ShapeDtypeStruct((T_ROWS, 128), _MXU),
                   jax.ShapeDtypeStruct((8, 128), F32), jax.ShapeDtypeStruct((1, SSD_WIDTH), F32)],
        scratch_shapes=[pltpu.VMEM(STATE_SHAPE, F32)],
        compiler_params=_cparams("arbitrary"),
    )(dyn, z, y_pre, xs, bc, dt_raw, prev, prm, norm_w, ex)


LRU_PAIRS = 8


def _lru_gates(xr, wa_ref, wx_ref, prm):
    pre_r, pre_i = [], []
    for k in range(LRU_PAIRS):
        xk = xr[:, k * 128:(k + 1) * 128]
        pre_r.append(_dot(xk, wa_ref[k]))
        pre_i.append(_dot(xk, wx_ref[k]))
    r = _sigmoid(jnp.concatenate(pre_r, axis=1) + prm[0:1])
    i = _sigmoid(jnp.concatenate(pre_i, axis=1) + prm[1:2])
    sp = _softplus(-prm[2:3])
    log_a = (-LRU_C) * r * sp
    a = jnp.exp(log_a)
    s = jnp.sqrt(-jnp.tanh(log_a) * (a * a + 1.0))
    return r, i, a, s, sp


def _lru_fwd(xr, gate, wa, wx, prm):
    def body(xr_ref, g_ref, wa_ref, wx_ref, prm_ref, hs_ref, yn_ref, carry, a_s, u_s):
        @pl.when(pl.program_id(0) == 0)
        def _():
            carry[...] = jnp.zeros_like(carry)

        prm = prm_ref[...]
        xr_t = xr_ref[...]
        _, i, a, s, _ = _lru_gates(xr_t, wa_ref, wx_ref, prm)
        a_s[...] = a
        u_s[...] = s * (i * xr_t)
        rid = lax.broadcasted_iota(jnp.int32, (8, LRU_WIDTH), 0)

        def group(k, before):
            off = pl.multiple_of(k * 8, 8)
            a8 = a_s[pl.ds(off, 8), :]
            u8 = u_s[pl.ds(off, 8), :]
            for d in (1, 2, 4):
                keep = rid >= d
                u8 = u8 + a8 * jnp.where(keep, pltpu.roll(u8, d, 0), 0.0)
                a8 = a8 * jnp.where(keep, pltpu.roll(a8, d, 0), 1.0)
            h8 = u8 + a8 * before
            hs_ref[pl.ds(off, 8), :] = h8
            return jnp.broadcast_to(h8[7:8], (8, LRU_WIDTH))

        carry[...] = lax.fori_loop(0, CHUNK // 8, group, carry[...])
        gel, _ = _gelu_and_grad(g_ref[...])
        yn_ref[...] = _rms_fwd(gel * hs_ref[...], prm[3:4]).astype(_MXU)

    row = pl.BlockSpec((CHUNK, LRU_WIDTH), lambda t: (t, 0))
    wspec = pl.BlockSpec((LRU_PAIRS, 128, 128), lambda t: (0, 0, 0))
    return pl.pallas_call(
        body, name="lru_fwd", grid=(N_CHUNKS,),
        in_specs=[row, row, wspec, wspec, pl.BlockSpec((8, LRU_WIDTH), lambda t: (0, 0))],
        out_specs=[row, row],
        out_shape=[jax.ShapeDtypeStruct((T_ROWS, LRU_WIDTH), F32), jax.ShapeDtypeStruct((T_ROWS, LRU_WIDTH), _MXU)],
        scratch_shapes=[pltpu.VMEM((8, LRU_WIDTH), F32), pltpu.VMEM((CHUNK, LRU_WIDTH), F32),
                        pltpu.VMEM((CHUNK, LRU_WIDTH), F32)],
        compiler_params=_cparams("arbitrary"),
    )(xr, gate, wa, wx, prm)


def _lru_bwd(dyn, dyn_block, gate, xr, hs, wa, wx, wa_t, wx_t, prm):
    def body(dyn_ref, g_ref, xr_ref, hs_ref, hsp_ref, wa_ref, wx_ref, wat_ref, wxt_ref, prm_ref,
             dg_ref, dxr_ref, dwa_ref, dwx_ref, dprm_ref, carry, a_s, d_s):
        step = pl.program_id(0)
        tile = N_CHUNKS - 1 - step

        @pl.when(step == 0)
        def _():
            carry[...] = jnp.zeros_like(carry)
            dwa_ref[...] = jnp.zeros_like(dwa_ref)
            dwx_ref[...] = jnp.zeros_like(dwx_ref)
            dprm_ref[...] = jnp.zeros_like(dprm_ref)

        prm = prm_ref[...]
        xr_t = xr_ref[...]
        r, i, a, s, sp = _lru_gates(xr_t, wa_ref, wx_ref, prm)
        hs_t = hs_ref[...]
        gel, dgel = _gelu_and_grad(g_ref[...])
        dy, dnw = _rms_bwd(gel * hs_t, prm[3:4], dyn_ref[...])
        dg_ref[...] = (dy * hs_t * dgel).astype(_MXU)
        a_s[...] = a
        d_s[...] = dy * gel
        rid = lax.broadcasted_iota(jnp.int32, (8, LRU_WIDTH), 0)

        def group(k, behind):
            off = pl.multiple_of((CHUNK // 8 - 1 - k) * 8, 8)
            a8 = a_s[pl.ds(off, 8), :]
            d8 = d_s[pl.ds(off, 8), :]
            c8 = jnp.where(rid == 7, 1.0, pltpu.roll(a8, 7, 0))
            for d in (1, 2, 4):
                keep = rid < 8 - d
                d8 = d8 + c8 * jnp.where(keep, pltpu.roll(d8, 8 - d, 0), 0.0)
                c8 = c8 * jnp.where(keep, pltpu.roll(c8, 8 - d, 0), 1.0)
            dht8 = d8 + c8 * behind
            d_s[pl.ds(off, 8), :] = dht8
            return jnp.broadcast_to(a8[0:1] * dht8[0:1], (8, LRU_WIDTH))

        carry[...] = lax.fori_loop(0, CHUNK // 8, group, carry[...])
        dht = d_s[...]
        before = hsp_ref[CHUNK - 8:CHUNK, :][7:8] * (tile > 0).astype(F32)
        first = lax.broadcasted_iota(jnp.int32, (CHUNK, 1), 0) == 0
        hprev = jnp.where(first, before, pltpu.roll(hs_t, 1, 0))
        da = dht * hprev
        ixr = i * xr_t
        ds = dht * ixr
        dlog_a = da * a - ds * (a * a) * lax.rsqrt(s * s)
        dr = dlog_a * ((-LRU_C) * sp)
        dsp = jnp.sum(dlog_a * ((-LRU_C) * r), axis=0, keepdims=True)
        dlam = dsp * (-_sigmoid(-prm[2:3]))
        di = dht * s * xr_t
        dpre_r = dr * r * (1.0 - r)
        dpre_i = di * i * (1.0 - i)
        dxr = dht * s * i
        parts = []
        for k in range(LRU_PAIRS):
            sl = slice(k * 128, (k + 1) * 128)
            parts.append(_dot(dpre_r[:, sl], wat_ref[k]) + _dot(dpre_i[:, sl], wxt_ref[k]))
            dwa_ref[k] += _dot(xr_t[:, sl], dpre_r[:, sl], TN)
            dwx_ref[k] += _dot(xr_t[:, sl], dpre_i[:, sl], TN)
        dxr_ref[...] = dxr + jnp.concatenate(parts, axis=1)
        dprm_ref[0:1, :] += jnp.sum(dpre_r, axis=0, keepdims=True)
        dprm_ref[1:2, :] += jnp.sum(dpre_i, axis=0, keepdims=True)
        dprm_ref[2:3, :] += dlam
        dprm_ref[3:4, :] += jnp.sum(dnw, axis=0, keepdims=True)

    rev = lambda blk=0: pl.BlockSpec((CHUNK, LRU_WIDTH), lambda s, blk=blk: (N_CHUNKS - 1 - s, blk))
    wspec = pl.BlockSpec((LRU_PAIRS, 128, 128), lambda s: (0, 0, 0))
    return pl.pallas_call(
        body, name="lru_bwd", grid=(N_CHUNKS,),
        in_specs=[rev(dyn_block), rev(), rev(), rev(),
                  pl.BlockSpec((CHUNK, LRU_WIDTH), lambda s: (jnp.maximum(N_CHUNKS - 2 - s, 0), 0)),
                  wspec, wspec, wspec, wspec, pl.BlockSpec((8, LRU_WIDTH), lambda s: (0, 0))],
        out_specs=[rev(), rev(), wspec, wspec, pl.BlockSpec((8, LRU_WIDTH), lambda s: (0, 0))],
        out_shape=[jax.ShapeDtypeStruct((T_ROWS, LRU_WIDTH), _MXU), jax.ShapeDtypeStruct((T_ROWS, LRU_WIDTH), F32),
                   jax.ShapeDtypeStruct((LRU_PAIRS, 128, 128), F32), jax.ShapeDtypeStruct((LRU_PAIRS, 128, 128), F32),
                   jax.ShapeDtypeStruct((8, LRU_WIDTH), F32)],
        scratch_shapes=[pltpu.VMEM((8, LRU_WIDTH), F32), pltpu.VMEM((CHUNK, LRU_WIDTH), F32),
                        pltpu.VMEM((CHUNK, LRU_WIDTH), F32)],
        compiler_params=_cparams("arbitrary"),
    )(dyn, gate, xr, hs, hs, wa, wx, wa_t, wx_t, prm)


SEC_NAMES = ("z", "xs", "bc", "dt", "g", "x")
SEC_WIDTH = {"z": 1024, "xs": 1024, "bc": 512, "dt": 128, "g": 1024, "x": 1024}


def _pair_blocks(w):
    w = w.reshape(LRU_PAIRS, 2, 64, 64)
    zero = jnp.zeros((LRU_PAIRS, 64, 64), w.dtype)
    top = jnp.concatenate([w[:, 0], zero], axis=2)
    bot = jnp.concatenate([zero, w[:, 1]], axis=2)
    return jnp.concatenate([top, bot], axis=1)


def _unpair_blocks(wp):
    return jnp.stack([wp[:, :64, :64], wp[:, 64:, 64:]], axis=1).reshape(16, 64, 64)


def _pad_lanes(v, width=128):
    return jnp.pad(v, ((0, 0), (0, width - v.shape[1])))


class _Resident:
    before_embed = ()

    def __init__(self, w_in_sections, w_out, w_gate, w_up, w_down):
        self._w_in, self._w_out, self._ffn = w_in_sections, w_out, (w_gate, w_up, w_down)

    def w_in(self, after):
        return self._w_in

    def mid_forward(self, after):
        return jnp.zeros((1, 1), F32)

    def w_out(self, after):
        return self._w_out

    def ffn(self, after):
        return self._ffn

    def grads_ready(self, names, g, g_mxu):
        return jnp.zeros((1, 1), F32)

    def small_ready(self, g, loss):
        return jnp.zeros((1, 1), F32)

    def small_middle(self, after):
        return jnp.zeros((1, 1), F32)


def _local_step(x, target, meta, p, late):
    g, g_mxu = {}, {}
    ex = _head_expander()
    h0 = _embed(x, meta, late.before_embed)
    w_in = late.w_in(h0)
    u1, projs = _norm_proj(h0, p["norm1_w"], [w_in[s] for s in SEC_NAMES], name="norm_in_proj")
    proj = dict(zip(SEC_NAMES, projs))
    ssd_prm = jnp.concatenate([_pad_lanes(p["ssd_dt_bias"]), _pad_lanes(p["ssd_a_log"]), _pad_lanes(p["ssd_d"]),
                               jnp.zeros((5, 128), F32)], axis=0)
    xs_act = _conv_fwd(proj["xs"], p["ssd_conv_w"][:, :SSD_WIDTH], p["ssd_conv_b"][:, :SSD_WIDTH], silu=True,
                       name="ssd_conv_xs")
    bc_act = _conv_fwd(proj["bc"], p["ssd_conv_w"][:, SSD_WIDTH:], p["ssd_conv_b"][:, SSD_WIDTH:], silu=True,
                       name="ssd_conv_bc")
    y_pre, y_ssd, prev = _ssd_fwd(xs_act, bc_act, proj["dt"], proj["z"], ssd_prm, p["ssd_norm_w"], ex)
    xr = _conv_fwd(proj["x"], p["lru_conv_w"], p["lru_conv_b"], silu=False, name="lru_conv")
    wa_p, wx_p = _pair_blocks(p["lru_wa"]), _pair_blocks(p["lru_wx"])
    lru_prm = jnp.concatenate([p["lru_ba"], p["lru_bx"], p["lru_lambda"], p["lru_norm_w"],
                               jnp.zeros((4, LRU_WIDTH), F32)], axis=0)
    hs, y_lru = _lru_fwd(xr, proj["g"], wa_p.astype(_MXU), wx_p.astype(_MXU),
                         lru_prm + late.mid_forward([xr, y_ssd]))
    ycat = jnp.concatenate([y_ssd, y_lru], axis=1)
    w_out = late.w_out(ycat)
    h1 = _mm([(ycat, 0, w_out, 0, 2 * D_MODEL)], T_ROWS, D_MODEL, tm=T_ROWS, tn=256, mode="nn", out_dtype=F32,
             name="out_proj", residual=h0)
    u2 = _rmsnorm(h1, p["norm2_w"], name="norm2")
    w_gate, w_up, w_down = late.ffn(u2)
    gp, up, act, act_t = _ffn_up(u2, w_gate, w_up)
    h2 = _mm([(act, 0, w_down, 0, D_FF)], T_ROWS, D_MODEL, tm=T_ROWS, tn=256, mode="nn", out_dtype=F32,
             name="ffn_down", residual=h1)
    loss, dh2, dh2b, g["final_norm_w"] = _loss_head(h2, target, p["final_norm_w"])
    dgp, dup, dgp_t, dup_t = _ffn_bwd_act(dh2b, w_down, gp, up)
    g["w_down"], g_mxu["w_down"] = _mm([(act_t, 0, dh2b, 0, T_ROWS)], D_FF, D_MODEL, tm=1408, tn=512, mode="nn",
                                       out_dtype=F32, name="dw_down", also_mxu=True)
    dh1, dh1b, g["norm2_w"] = _mm_norm_bwd([(dgp, w_gate, D_FF), (dup, w_up, D_FF)], h1, p["norm2_w"], dh2,
                                           name="ffn_bwd_in")
    g["w_gate"], g_mxu["w_gate"] = _mm([(dgp_t, 0, u2, 0, T_ROWS)], D_FF, D_MODEL, tm=1408, tn=512, mode="nn",
                                       out_dtype=F32, name="dw_gate", also_mxu=True)
    g["w_up"], g_mxu["w_up"] = _mm([(dup_t, 0, u2, 0, T_ROWS)], D_FF, D_MODEL, tm=1408, tn=512, mode="nn",
                                   out_dtype=F32, name="dw_up", also_mxu=True)
    g["w_out"], g_mxu["w_out"] = _mm([(ycat, 0, dh1b, 0, T_ROWS)], 2 * D_MODEL, D_MODEL, tm=1024, tn=512, mode="tn",
                                     out_dtype=F32, name="dw_out", also_mxu=True)
    sent = late.grads_ready(("w_down", "w_gate", "w_up", "w_out"), g, g_mxu)
    dycat = _mm([(dh1b, 0, w_out, 0, D_MODEL)], T_ROWS, 2 * D_MODEL, tm=T_ROWS, tn=256, mode="nt", out_dtype=F32,
                name="out_proj_bwd", behind=(sent,))
    dgate, dxr, dwa_p, dwx_p, dlru_prm = _lru_bwd(dycat, 1, proj["g"], xr, hs, wa_p.astype(_MXU), wx_p.astype(_MXU),
                                                  jnp.swapaxes(wa_p, 1, 2).astype(_MXU),
                                                  jnp.swapaxes(wx_p, 1, 2).astype(_MXU), lru_prm)
    g["lru_wa"], g["lru_wx"] = _unpair_blocks(dwa_p), _unpair_blocks(dwx_p)
    g["lru_ba"], g["lru_bx"], g["lru_lambda"], g["lru_norm_w"] = (dlru_prm[k:k + 1] for k in range(4))
    dx_lru, g["lru_conv_w"], g["lru_conv_b"] = _conv_bwd(dxr, proj["x"], p["lru_conv_w"], p["lru_conv_b"], silu=False,
                                                         name="lru_conv_bwd")
    dz, dxs_act, dbc_act, ddt, dssd_prm, g["ssd_norm_w"] = _ssd_bwd(dycat, 0, proj["z"], y_pre, xs_act, bc_act,
                                                                    proj["dt"], prev, ssd_prm, p["ssd_norm_w"], ex)
    g["ssd_dt_bias"], g["ssd_a_log"], g["ssd_d"] = (dssd_prm[k:k + 1, :SSD_HEADS] for k in range(3))
    dxs, dcw_xs, dcb_xs = _conv_bwd(dxs_act, proj["xs"], p["ssd_conv_w"][:, :SSD_WIDTH],
                                    p["ssd_conv_b"][:, :SSD_WIDTH], silu=True, name="ssd_conv_xs_bwd")
    dbc, dcw_bc, dcb_bc = _conv_bwd(dbc_act, proj["bc"], p["ssd_conv_w"][:, SSD_WIDTH:],
                                    p["ssd_conv_b"][:, SSD_WIDTH:], silu=True, name="ssd_conv_bc_bwd")
    g["ssd_conv_w"] = jnp.concatenate([dcw_xs, dcw_bc], axis=1)
    g["ssd_conv_b"] = jnp.concatenate([dcb_xs, dcb_bc], axis=1)
    dproj = {"z": dz, "xs": dxs, "bc": dbc, "dt": ddt, "g": dgate, "x": dx_lru}
    for s in SEC_NAMES:
        wdt = SEC_WIDTH[s]
        g["w_in_" + s], g_mxu["w_in_" + s] = _mm([(dproj[s], 0, u1, 0, T_ROWS)], wdt, D_MODEL, tm=min(wdt, 1024),
                                                 tn=512, mode="tn", out_dtype=F32, name="dw_in_" + s, also_mxu=True)
    sent = late.grads_ready(("w_in",), g, g_mxu)
    dh0, _, g["norm1_w"] = _mm_norm_bwd([(dproj[s], w_in[s], SEC_WIDTH[s]) for s in SEC_NAMES], h0,
                                        p["norm1_w"], dh1, name="in_proj_bwd", behind=(sent,))
    g["meta_tokens"] = dh0[PAD_ROWS:X_ROW0]
    late.small_ready(g, loss)
    return loss, dh0[X_ROW0:], g, g_mxu


MESH = pl.DeviceIdType.MESH
ANY = pl.BlockSpec(memory_space=pl.ANY)


def _my_place():
    return lax.axis_index("x"), lax.axis_index("y"), lax.axis_index("c")


def _other_chips(x, y):
    return [(1 - x, y), (x, 1 - y), (1 - x, 1 - y)]


HBM_SPEC = pl.BlockSpec(memory_space=pltpu.HBM)
SEM_SPEC = pl.BlockSpec(memory_space=pltpu.SEMAPHORE)
SPLIT_EFFECT = pltpu.SideEffectType.DATAFLOW_SIDE_EFFECTING


def _half_cols(buf, c, other=False):
    half = buf.shape[-1] // 2
    return pl.ds(pl.multiple_of(((1 - c) if other else c) * half, 128), half)


def _halves_plan(bufs, x, y, c, incoming):
    plan = []
    for buf in bufs:
        cols = _half_cols(buf, c)
        for (px, py) in _other_chips(x, y):
            slot = 2 * px + py if incoming else 2 * x + y
            plan.append((buf.at[2 * x + y, :, cols], buf.at[slot, :, cols], (px, py, c)))
    return plan


def _forward_plan(bufs, x, y, c, incoming):
    plan = []
    for buf in bufs:
        for (px, py) in _other_chips(x, y):
            slot = 2 * px + py
            plan.append((buf.at[slot, :, _half_cols(buf, c)], buf.at[slot, :, _half_cols(buf, c, other=incoming)],
                         (x, y, 1 - c)))
    return plan


def _scatter_plan(bufs, x, y, c, incoming):
    n = len(bufs) // 2
    plan = []
    for k in range(n):
        for j, (px, py) in enumerate(_other_chips(x, y)):
            plan.append((bufs[k].at[2 * px + py], bufs[n + k].at[j], (px, py, c)))
    return plan


def _split_start(bufs, plan, n_copies, after, *, name):
    n = len(bufs)
    extra = [] if after is None else [after]

    def body(*refs):
        ins = refs[:n]
        send_sems, recv_sems = refs[n + len(extra)], refs[n + len(extra) + 1]
        token = refs[-1]
        x, y, c = _my_place()
        for i, (src, dst, dev) in enumerate(plan(ins, x, y, c, False)):
            pltpu.make_async_remote_copy(src_ref=src, dst_ref=dst, send_sem=send_sems.at[i], recv_sem=recv_sems.at[i],
                                         device_id=dev, device_id_type=MESH).start()
        token[...] = jnp.zeros_like(token)

    outs = pl.pallas_call(
        body, name=name,
        out_shape=(pltpu.SemaphoreType.DMA((n_copies,)), pltpu.SemaphoreType.DMA((n_copies,)),
                   *[pltpu.HBM(b.shape, b.dtype) for b in bufs], jax.ShapeDtypeStruct((8, 128), F32)),
        in_specs=[HBM_SPEC] * n + [ANY] * len(extra),
        out_specs=(SEM_SPEC, SEM_SPEC, *[HBM_SPEC] * n, pl.BlockSpec(memory_space=pltpu.VMEM)),
        input_output_aliases={k: 2 + k for k in range(n)},
        compiler_params=pltpu.CompilerParams(has_side_effects=SPLIT_EFFECT),
    )(*[pltpu.with_memory_space_constraint(b, pltpu.HBM) for b in bufs], *extra)
    return outs[0], outs[1], list(outs[2:2 + n]), outs[-1]


def _split_wait(bufs, send_sems, recv_sems, plan, after, *, name):
    n = len(bufs)
    after = list(after) if isinstance(after, (list, tuple)) else [after]

    def body(*refs):
        ins = refs[:n]
        send_sems_ref, recv_sems_ref = refs[n], refs[n + 1]
        x, y, c = _my_place()
        for i, (src, dst, dev) in enumerate(plan(ins, x, y, c, True)):
            cp = pltpu.make_async_remote_copy(src_ref=src, dst_ref=dst, send_sem=send_sems_ref.at[i],
                                              recv_sem=recv_sems_ref.at[i], device_id=dev, device_id_type=MESH)
            cp.wait_send()
            cp.wait_recv()

    outs = pl.pallas_call(
        body, name=name, out_shape=tuple(pltpu.HBM(b.shape, b.dtype) for b in bufs),
        in_specs=[HBM_SPEC] * n + [SEM_SPEC, SEM_SPEC] + [ANY] * len(after), out_specs=tuple([HBM_SPEC] * n),
        input_output_aliases={k: k for k in range(n)},
        compiler_params=pltpu.CompilerParams(has_side_effects=SPLIT_EFFECT),
    )(*bufs, send_sems, recv_sems, *after)
    return list(outs)


def _fill_own_slots(shards, me_arr, *, name, behind=()):
    n = len(shards)
    n_in = n + len(behind)

    def body(me_ref, *refs):
        for k in range(n):
            refs[n_in + k][0] = refs[k][...].astype(_MXU)

    half = D_MODEL // 2
    return pl.pallas_call(
        body, name=name,
        grid_spec=pltpu.PrefetchScalarGridSpec(
            num_scalar_prefetch=1, grid=(2,),
            in_specs=[pl.BlockSpec((s.shape[0], half), lambda i, me: (0, i)) for s in shards]
            + [pl.BlockSpec(memory_space=pl.ANY)] * len(behind),
            out_specs=[pl.BlockSpec((1, s.shape[0], half), lambda i, me: (me[0], 0, i)) for s in shards]),
        out_shape=[jax.ShapeDtypeStruct((N_SHARDS,) + s.shape, _MXU) for s in shards],
        compiler_params=_cparams("parallel"),
    )(me_arr, *shards, *behind)


def _gather_small(small):
    def body(s_ref, o_ref, send_sems, recv_sems, local_sem):
        x, y, c = _my_place()
        me = 2 * x + y
        local = pltpu.make_async_copy(s_ref, o_ref.at[me], local_sem)
        local.start()
        copies = [(pltpu.make_async_remote_copy(src_ref=s_ref, dst_ref=o_ref.at[me], send_sem=send_sems.at[j],
                                                recv_sem=recv_sems.at[j], device_id=(px, py, c), device_id_type=MESH),
                   2 * px + py) for j, (px, py) in enumerate(_other_chips(x, y))]
        for cp, _ in copies:
            cp.start()
        for j, (cp, slot) in enumerate(copies):
            cp.wait_send()
            pltpu.make_async_remote_copy(src_ref=s_ref, dst_ref=o_ref.at[slot], send_sem=send_sems.at[j],
                                         recv_sem=recv_sems.at[j], device_id=(x, y, c),
                                         device_id_type=MESH).wait_recv()
        local.wait()

    return pl.pallas_call(
        body, name="gather_small", in_specs=[ANY], out_specs=ANY,
        out_shape=jax.ShapeDtypeStruct((N_SHARDS,) + small.shape, small.dtype),
        scratch_shapes=[pltpu.SemaphoreType.DMA((3,)), pltpu.SemaphoreType.DMA((3,)), pltpu.SemaphoreType.DMA],
    )(small)


def _swap_with_sibling(parts, *, name, behind=()):
    n = len(parts)
    nb = len(behind)

    def body(*refs):
        ins, outs = refs[:n], refs[n + nb:2 * n + nb]
        send_sems, recv_sems = refs[2 * n + nb:]
        x, y, c = _my_place()
        copies = [pltpu.make_async_remote_copy(
            src_ref=ins[k], dst_ref=outs[k], send_sem=send_sems.at[k], recv_sem=recv_sems.at[k],
            device_id=(x, y, 1 - c), device_id_type=MESH) for k in range(n)]
        for cp in copies:
            cp.start()
        for cp in copies:
            cp.wait()

    return pl.pallas_call(
        body, name=name, in_specs=[ANY] * (n + nb), out_specs=[ANY] * n,
        out_shape=[jax.ShapeDtypeStruct(a.shape, a.dtype) for a in parts],
        scratch_shapes=[pltpu.SemaphoreType.DMA((n,)), pltpu.SemaphoreType.DMA((n,))],
    )(*parts, *behind)


def _other_devices(x, y, c):
    out = []
    for mask in range(1, N_DEV):
        px, py, pc = x ^ (mask >> 2 & 1), y ^ (mask >> 1 & 1), c ^ (mask & 1)
        out.append(((px, py, pc), 4 * px + 2 * py + pc))
    return out


def _pieces_plan(bufs, x, y, c, incoming):
    pack, land = bufs
    me = 4 * x + 2 * y + c
    return [(pack.at[num], land.at[num if incoming else me], dev) for dev, num in _other_devices(x, y, c)]


def _spread_plan(bufs, x, y, c, incoming):
    piece, land = bufs
    me = 4 * x + 2 * y + c
    return [(piece, land.at[num if incoming else me], dev) for dev, num in _other_devices(x, y, c)]


def _sum_pieces(pack, land, dev_arr, *, name):
    def body(dev_ref, pack_ref, land_ref, o_ref):
        dev = dev_ref[0]
        own = pack_ref[dev]
        acc = None
        for d in range(N_DEV):
            term = jnp.where(dev == d, own, land_ref[d])
            acc = term if acc is None else acc + term
        o_ref[...] = acc

    vmem = pl.BlockSpec(memory_space=pltpu.VMEM)
    return pl.pallas_call(
        body, name=name, in_specs=[pl.BlockSpec(memory_space=pltpu.SMEM), vmem, vmem], out_specs=vmem,
        out_shape=jax.ShapeDtypeStruct(pack.shape[1:], F32),
    )(dev_arr, pack, land)


def _join_pieces(piece, land, dev_arr, *, name):
    def body(dev_ref, piece_ref, land_ref, o_ref):
        dev = dev_ref[0]
        for d in range(N_DEV):
            o_ref[d] = jnp.where(dev == d, piece_ref[...], land_ref[d])

    vmem = pl.BlockSpec(memory_space=pltpu.VMEM)
    return pl.pallas_call(
        body, name=name, in_specs=[pl.BlockSpec(memory_space=pltpu.SMEM), vmem, vmem], out_specs=vmem,
        out_shape=jax.ShapeDtypeStruct(land.shape, F32),
    )(dev_arr, piece, land)


def _adamw_native(ws, gs, ms, vs):
    n = len(ws)

    def body(*refs):
        for k in range(n):
            w_ref, g_ref, m_ref, v_ref = (refs[j * n + k] for j in range(4))
            delta, m_new, v_new = _adamw_math(w_ref[...], g_ref[...], m_ref[...], v_ref[...])
            refs[4 * n + k][...] = delta
            refs[5 * n + k][...] = m_new
            refs[6 * n + k][...] = v_new

    vmem = pl.BlockSpec(memory_space=pltpu.VMEM)
    shapes = [jax.ShapeDtypeStruct(a.shape, F32) for a in ws]
    outs = pl.pallas_call(
        body, name="adamw_small", in_specs=[vmem] * (4 * n), out_specs=[vmem] * (3 * n), out_shape=shapes * 3,
        compiler_params=pltpu.CompilerParams(vmem_limit_bytes=VMEM_LIMIT_BYTES),
    )(*ws, *gs, *ms, *vs)
    return outs[:n], outs[n:2 * n], outs[2 * n:]


def _elementwise_tile(rows, cols):
    for t in range(256, 15, -16):
        if rows % t == 0:
            return (t, cols), rows // t, lambda i: (i, 0)
    assert cols % 256 == 0
    return (rows, 256), cols // 256, lambda i: (0, i)


def _partial_sum(own, land, me_arr, *, name):
    r, c = own.shape[-2:]
    tile, steps, imap = _elementwise_tile(r, c)
    whole = own.ndim == 3

    def body(me_ref, own_ref, land_ref, o_ref):
        acc = own_ref[0] if whole else own_ref[...]
        for j in range(3):
            acc = acc + land_ref[j].astype(F32)
        o_ref[...] = acc.astype(_MXU)

    own_spec = (pl.BlockSpec((1,) + tile, lambda i, me: (me[0],) + imap(i)) if whole
                else pl.BlockSpec(tile, lambda i, me: imap(i)))
    return pl.pallas_call(
        body, name=name,
        grid_spec=pltpu.PrefetchScalarGridSpec(
            num_scalar_prefetch=1, grid=(steps,),
            in_specs=[own_spec, pl.BlockSpec((3,) + tile, lambda i, me: (0,) + imap(i))],
            out_specs=pl.BlockSpec(tile, lambda i, me: imap(i))),
        out_shape=jax.ShapeDtypeStruct((r, c), _MXU),
        compiler_params=_cparams("parallel"),
    )(me_arr, own, land)


LANE_TILE = 256


def _partial_sums(owns, lands, me_arr, *, name):
    n = len(owns)

    def body(me_ref, *refs):
        for k in range(n):
            acc = refs[k][0]
            for j in range(3):
                acc = acc + refs[n + k][j].astype(F32)
            refs[2 * n + k][...] = acc.astype(_MXU)

    rows = [o.shape[1] for o in owns]
    return pl.pallas_call(
        body, name=name,
        grid_spec=pltpu.PrefetchScalarGridSpec(
            num_scalar_prefetch=1, grid=(D_MODEL // LANE_TILE,),
            in_specs=[pl.BlockSpec((1, r, LANE_TILE), lambda i, me: (me[0], 0, i)) for r in rows]
            + [pl.BlockSpec((3, r, LANE_TILE), lambda i, me: (0, 0, i)) for r in rows],
            out_specs=[pl.BlockSpec((r, LANE_TILE), lambda i, me: (0, i)) for r in rows]),
        out_shape=[jax.ShapeDtypeStruct((r, D_MODEL), _MXU) for r in rows],
        compiler_params=_cparams("parallel"),
    )(me_arr, *owns, *lands)


def _adamws(ws, parts_a, parts_b, ms, vs, *, name):
    n = len(ws)

    def body(*refs):
        for k in range(n):
            w_ref, a_ref, b_ref, m_ref, v_ref = (refs[j * n + k] for j in range(5))
            g = a_ref[...].astype(F32) + b_ref[...].astype(F32)
            delta, m_new, v_new = _adamw_math(w_ref[...], g, m_ref[...], v_ref[...])
            for j, val in enumerate((g, delta, m_new, v_new)):
                refs[(5 + j) * n + k][...] = val

    tiles = [pl.BlockSpec((w.shape[0], LANE_TILE), lambda i: (0, i)) for w in ws]
    outs = pl.pallas_call(
        body, name=name, grid=(D_MODEL // LANE_TILE,), in_specs=tiles * 5, out_specs=tiles * 4,
        out_shape=[jax.ShapeDtypeStruct(w.shape, F32) for w in ws] * 4,
        compiler_params=_cparams("parallel"),
    )(*ws, *parts_a, *parts_b, *ms, *vs)
    return [outs[j * n:(j + 1) * n] for j in range(4)]


def _adamw_math(w, g, m, v):
    m = ADAM_B1 * m + (1.0 - ADAM_B1) * g
    v = ADAM_B2 * v + (1.0 - ADAM_B2) * (g * g)
    m_hat = m / (1.0 - ADAM_B1 ** ADAM_STEP)
    v_hat = v / (1.0 - ADAM_B2 ** ADAM_STEP)
    delta = -ADAM_LR * (m_hat / (jnp.sqrt(v_hat) + ADAM_EPS) + ADAM_WD * w)
    return delta, m, v


def _adamw(w, grad_parts, m, v, *, name):
    if w.ndim == 3:
        steps = 4
        assert w.shape[0] % steps == 0
        tile_shape, imap = (w.shape[0] // steps,) + w.shape[1:], lambda i: (i, 0, 0)
    else:
        tile_shape, steps, imap = _elementwise_tile(*w.shape)
    n = len(grad_parts)

    def body(*refs):
        w_ref, m_ref, v_ref = refs[:3]
        g_refs = refs[3:3 + n]
        g_out, d_out, m_out, v_out = refs[3 + n:]
        g = g_refs[0][...].astype(F32)
        for k in range(1, n):
            g = g + g_refs[k][...].astype(F32)
        delta, m_new, v_new = _adamw_math(w_ref[...], g, m_ref[...], v_ref[...])
        g_out[...] = g
        d_out[...] = delta
        m_out[...] = m_new
        v_out[...] = v_new

    tile = pl.BlockSpec(tile_shape, imap)
    return pl.pallas_call(
        body, name=name, grid=(steps,), in_specs=[tile] * (3 + n), out_specs=[tile] * 4,
        out_shape=[jax.ShapeDtypeStruct(w.shape, F32)] * 4,
        compiler_params=_cparams("parallel"),
    )(w, m, v, *grad_parts)


WEIGHT_NAMES = ("meta_tokens", "norm1_w", "w_in", "ssd_conv_w", "ssd_conv_b", "ssd_dt_bias", "ssd_a_log", "ssd_d",
                "ssd_norm_w", "lru_conv_w", "lru_conv_b", "lru_wa", "lru_ba", "lru_wx", "lru_bx", "lru_lambda",
                "lru_norm_w", "w_out", "norm2_w", "w_gate", "w_up", "w_down", "final_norm_w")
BIG = ("w_in", "w_out", "w_gate", "w_up", "w_down")
FFN = ("w_gate", "w_up", "w_down")
LATE = ("w_out",) + FFN
SMALL_SHARDED = {"meta_tokens": (N_META, D_MODEL), "ssd_conv_w": (CONV_K, 1536), "lru_conv_w": (CONV_K, LRU_WIDTH)}
SMALL = tuple(n for n in WEIGHT_NAMES if n not in BIG)
PACK_COLS = 1024


def _pack(arrays, row_multiple):
    flat = jnp.concatenate([a.reshape(-1) for a in arrays])
    rows = -(-flat.shape[0] // (row_multiple * PACK_COLS)) * row_multiple
    return jnp.pad(flat, (0, rows * PACK_COLS - flat.shape[0])).reshape(rows, PACK_COLS)


def _unpack(pack, shapes):
    flat = pack.reshape(-1)
    out, off = [], 0
    for s in shapes:
        size = math.prod(s)
        out.append(flat[off:off + size].reshape(s))
        off += size
    return out


def _unshard_cols(g4):
    return jnp.swapaxes(g4, 0, 1).reshape(g4.shape[1], -1)


COL_SHARDED = ("w_in", "w_gate", "w_up")
IN_ROWS = {"z": (0, 1024), "xs": (1024, 2048), "bc": (2048, 2560), "dt": (2560, 2576), "g": (2576, 3600),
           "x": (3600, IN_COLS)}


def _rows_of_shards(shards4, lo, hi):
    r = shards4.shape[1]
    parts = [shards4[k, max(lo, k * r) - k * r:min(hi, (k + 1) * r) - k * r]
             for k in range(N_SHARDS) if max(lo, k * r) < min(hi, (k + 1) * r)]
    return parts[0] if len(parts) == 1 else jnp.concatenate(parts, axis=0)


def _w_in_shard_rows(k, sections):
    lo, hi = k * (IN_COLS // N_SHARDS), (k + 1) * (IN_COLS // N_SHARDS)
    parts = []
    for arr, (a, b) in zip(sections, IN_ROWS.values()):
        if max(lo, a) < min(hi, b):
            parts.append(arr[max(lo, a) - a:min(hi, b) - a])
    return jnp.concatenate(parts, axis=0)


def _rows_view(name, block):
    return jnp.swapaxes(block[0], 0, 1) if name in COL_SHARDED else block[0]


def _param_view(name, rows):
    return (jnp.swapaxes(rows, 0, 1) if name in COL_SHARDED else rows)[None]


def kernel(x, meta_tokens, norm1_w, w_in, ssd_conv_w, ssd_conv_b, ssd_dt_bias, ssd_a_log, ssd_d, ssd_norm_w, lru_conv_w, lru_conv_b, lru_wa, lru_ba, lru_wx, lru_bx, lru_lambda, lru_norm_w, w_out, norm2_w, w_gate, w_up, w_down, final_norm_w, loss_target, m_meta_tokens, m_norm1_w, m_w_in, m_ssd_conv_w, m_ssd_conv_b, m_ssd_dt_bias, m_ssd_a_log, m_ssd_d, m_ssd_norm_w, m_lru_conv_w, m_lru_conv_b, m_lru_wa, m_lru_ba, m_lru_wx, m_lru_bx, m_lru_lambda, m_lru_norm_w, m_w_out, m_norm2_w, m_w_gate, m_w_up, m_w_down, m_final_norm_w, v_meta_tokens, v_norm1_w, v_w_in, v_ssd_conv_w, v_ssd_conv_b, v_ssd_dt_bias, v_ssd_a_log, v_ssd_d, v_ssd_norm_w, v_lru_conv_w, v_lru_conv_b, v_lru_wa, v_lru_ba, v_lru_wx, v_lru_bx, v_lru_lambda, v_lru_norm_w, v_w_out, v_norm2_w, v_w_gate, v_w_up, v_w_down, v_final_norm_w):
    w = dict(zip(WEIGHT_NAMES, (meta_tokens, norm1_w, w_in, ssd_conv_w, ssd_conv_b, ssd_dt_bias, ssd_a_log, ssd_d, ssd_norm_w, lru_conv_w, lru_conv_b, lru_wa, lru_ba, lru_wx, lru_bx, lru_lambda, lru_norm_w, w_out, norm2_w, w_gate, w_up, w_down, final_norm_w)))
    m = dict(zip(WEIGHT_NAMES, (m_meta_tokens, m_norm1_w, m_w_in, m_ssd_conv_w, m_ssd_conv_b, m_ssd_dt_bias, m_ssd_a_log, m_ssd_d, m_ssd_norm_w, m_lru_conv_w, m_lru_conv_b, m_lru_wa, m_lru_ba, m_lru_wx, m_lru_bx, m_lru_lambda, m_lru_norm_w, m_w_out, m_norm2_w, m_w_gate, m_w_up, m_w_down, m_final_norm_w)))
    v = dict(zip(WEIGHT_NAMES, (v_meta_tokens, v_norm1_w, v_w_in, v_ssd_conv_w, v_ssd_conv_b, v_ssd_dt_bias, v_ssd_a_log, v_ssd_d, v_ssd_norm_w, v_lru_conv_w, v_lru_conv_b, v_lru_wa, v_lru_ba, v_lru_wx, v_lru_bx, v_lru_lambda, v_lru_norm_w, v_w_out, v_norm2_w, v_w_gate, v_w_up, v_w_down, v_final_norm_w)))
    me = 2 * lax.axis_index("x") + lax.axis_index("y")

    big2d = {n: _rows_view(n, w[n]) for n in BIG}
    small_local = jnp.concatenate([w["meta_tokens"].reshape(-1), w["ssd_conv_w"].reshape(-1),
                                   w["lru_conv_w"].reshape(-1)])[None]
    me_arr = me.astype(jnp.int32).reshape(1)
    dev_arr = (2 * me + lax.axis_index("c")).astype(jnp.int32).reshape(1)
    small4 = _gather_small(small_local)
    (w_in_slot,) = _fill_own_slots([big2d["w_in"]], me_arr, name="own_slot_w_in")
    in_send, in_recv, in_bufs, in_tok = _split_start([w_in_slot], _halves_plan, 3, small4, name="gather_w_in_start")
    late_slots = _fill_own_slots([big2d[n] for n in LATE], me_arr, name="own_slots_late", behind=(in_tok,))
    sm = small4[:, 0]
    meta_full = _unshard_cols(sm[:, :4096].reshape(N_SHARDS, N_META, 256))
    ssd_conv_w_full = _unshard_cols(sm[:, 4096:5632].reshape(N_SHARDS, CONV_K, 384))
    lru_conv_w_full = _unshard_cols(sm[:, 5632:].reshape(N_SHARDS, CONV_K, 256))

    p = {"ssd_conv_w": ssd_conv_w_full, "lru_conv_w": lru_conv_w_full,
         "lru_wa": w["lru_wa"][0], "lru_wx": w["lru_wx"][0], "final_norm_w": w["final_norm_w"][None]}
    for n in ("norm1_w", "ssd_conv_b", "ssd_dt_bias", "ssd_a_log", "ssd_d", "ssd_norm_w", "lru_conv_b", "lru_ba",
              "lru_bx", "lru_lambda", "lru_norm_w", "norm2_w"):
        p[n] = w[n]

    class Late:
        def __init__(self):
            self.pending = []
            self.before_embed = (late_slots[0],)

        def w_in(self, after):
            (buf,) = _split_wait(in_bufs, in_send, in_recv, _halves_plan, after, name="gather_w_in_wait")
            send, recv, bufs, tok = _split_start([buf], _forward_plan, 3, None, name="forward_w_in_start")
            self.late_gather = _split_start(late_slots, _halves_plan, 3 * len(LATE), tok, name="gather_late_start")
            (w_in4,) = _split_wait(bufs, send, recv, _forward_plan, self.late_gather[2][0], name="forward_w_in_wait")
            sections = {s: _rows_of_shards(w_in4, lo, hi) for s, (lo, hi) in IN_ROWS.items()}
            sections["dt"] = jnp.pad(sections["dt"], ((0, SEC_WIDTH["dt"] - SSD_HEADS), (0, 0)))
            return sections

        def mid_forward(self, after):
            send, recv, bufs, _ = self.late_gather
            bufs = _split_wait(bufs, send, recv, _halves_plan, after, name="gather_late_wait")
            self.forward = _split_start(bufs, _forward_plan, 3 * len(LATE), None, name="forward_late_start")
            return self.forward[3][:1, :1]

        def w_out(self, after):
            send, recv, bufs, _ = self.forward
            bufs = _split_wait(bufs, send, recv, _forward_plan, after, name="forward_late_wait")
            self.late = dict(zip(LATE, (b.reshape(-1, D_MODEL) for b in bufs)))
            return self.late["w_out"]

        def ffn(self, after):
            return tuple(self.late[n] for n in FFN)

        def grads_ready(self, names, g, g_mxu):
            if names == ("w_in",):
                g_mxu["w_in"] = jnp.stack([_w_in_shard_rows(k, [g_mxu["w_in_" + s] for s in SEC_NAMES])
                                           for k in range(N_SHARDS)])
            srcs = [g_mxu[n].reshape(N_SHARDS, -1, D_MODEL) for n in names]
            lands = [lax.empty((3,) + s.shape[1:], _MXU) for s in srcs]
            tag = "_".join(names)
            send, recv, bufs, tok = _split_start(srcs + lands, _scatter_plan, 3 * len(names), None,
                                                 name="scatter_" + tag + "_start")
            self.pending.append((names, send, recv, bufs, tag))
            self.in_flight = bufs[0]
            return tok[:1, :1]

        def landed(self, after, which):
            land = {}
            for names, send, recv, bufs, tag in self.pending:
                if names[0] in which:
                    bufs = _split_wait(bufs, send, recv, _scatter_plan, after, name="scatter_" + tag + "_wait")
                    land.update(zip(names, bufs[len(names):]))
            return land

        def small_ready(self, g, loss):
            pack = _pack([g[n] for n in SMALL] + [loss[0, :1]], 8 * N_DEV)
            pack = pack.reshape(N_DEV, -1, PACK_COLS)
            self.small = _split_start([pack, lax.empty(pack.shape, F32)], _pieces_plan, N_DEV - 1, loss,
                                      name="small_pieces_start")
            return self.small[3]

        def small_middle(self, after):
            send, recv, bufs, _ = self.small
            pack, land = _split_wait(bufs, send, recv, _pieces_plan, after, name="small_pieces_wait")
            piece = _sum_pieces(pack, land, dev_arr, name="small_pieces_sum")
            self.small = _split_start([piece, lax.empty(pack.shape, F32)], _spread_plan, N_DEV - 1, None,
                                      name="small_spread_start")
            return self.small[3]

        def small_sum(self, after):
            send, recv, bufs, _ = self.small
            piece, land = _split_wait(bufs, send, recv, _spread_plan, after, name="small_spread_wait")
            return _join_pieces(piece, land, dev_arr, name="small_join")

    late = Late()

    loss, grad_x, g, g_mxu = _local_step(x[0], loss_target[0], meta_full, p, late)

    g4 = {n: g[n].reshape(N_SHARDS, -1, D_MODEL) for n in LATE}
    g4["w_in"] = lax.switch(me, [functools.partial(_w_in_shard_rows, k) for k in range(N_SHARDS)],
                            [g["w_in_" + s] for s in SEC_NAMES])
    land = late.landed([late.in_flight, late.small[2][0]], LATE)
    part = dict(zip(LATE, _partial_sums([g4[n] for n in LATE], [land[n] for n in LATE], me_arr,
                                        name="partial_late")))
    sib = dict(zip(LATE, _swap_with_sibling([part[n] for n in LATE], name="swap_late")))

    grad, delta, new_m, new_v = {}, {}, {}, {}
    late_outs = _adamws([big2d[n] for n in LATE], [part[n] for n in LATE], [sib[n] for n in LATE],
                        [_rows_view(n, m[n]) for n in LATE], [_rows_view(n, v[n]) for n in LATE], name="adamw_late")
    for d, outs in zip((grad, delta, new_m, new_v), late_outs):
        d.update({n: _param_view(n, o) for n, o in zip(LATE, outs)})

    land.update(late.landed(late_outs[0][0], ("w_in",)))
    spread = late.small_middle(land["w_in"])
    part_in = _partial_sum(g4["w_in"], land["w_in"], me_arr, name="partial_w_in")
    (sib_in,) = _swap_with_sibling([part_in], name="swap_w_in", behind=(spread,))
    lanes = lambda a: a[0].reshape(8, 128, -1).transpose(2, 0, 1)
    pieces = lambda a: a.reshape(-1, 8, 128)
    outs = _adamw(lanes(w["w_in"]), [pieces(part_in), pieces(sib_in)], lanes(m["w_in"]), lanes(v["w_in"]),
                  name="adamw_w_in")
    grad["w_in"], delta["w_in"], new_m["w_in"], new_v["w_in"] = (o.transpose(1, 2, 0).reshape(1, D_MODEL, -1)
                                                                 for o in outs)

    small_full_shape = {n: (SMALL_SHARDED[n] if n in SMALL_SHARDED else w[n].shape) for n in SMALL}
    red_list = _unpack(late.small_sum(outs[0]), [small_full_shape[n] for n in SMALL] + [(1,)])
    loss_total = red_list[-1][0]
    g_small = {}
    for n, arr in zip(SMALL, red_list[:-1]):
        if n in SMALL_SHARDED:
            cols = SMALL_SHARDED[n][1] // N_SHARDS
            arr = lax.dynamic_slice_in_dim(arr, me * cols, cols, axis=1)
        g_small[n] = arr.reshape(w[n].shape)
    two_d = lambda a: a.reshape(1, -1) if a.ndim == 1 else a
    deltas, new_ms, new_vs = _adamw_native(*[[two_d(d[n]) for n in SMALL] for d in (w, g_small, m, v)])
    for n, dn, mn, vn in zip(SMALL, deltas, new_ms, new_vs):
        grad[n], delta[n], new_m[n], new_v[n] = (g_small[n], dn.reshape(w[n].shape), mn.reshape(w[n].shape),
                                                 vn.reshape(w[n].shape))

    return (loss_total, grad_x[None], *[grad[n] for n in WEIGHT_NAMES], *[delta[n] for n in WEIGHT_NAMES],
            *[new_m[n] for n in WEIGHT_NAMES], *[new_v[n] for n in WEIGHT_NAMES])
```

```python
import functools
import math

import jax
import jax.numpy as jnp
from jax import lax
from jax.experimental import pallas as pl
from jax.experimental.pallas import tpu as pltpu

F32 = jnp.float32
_MXU = jnp.bfloat16

D_MODEL = 1024
SEQ = 2048
N_META = 16
CHUNK = 128
T_ROWS = 2176
N_CHUNKS = T_ROWS // CHUNK
PAD_ROWS = T_ROWS - SEQ - N_META
X_ROW0 = PAD_ROWS + N_META
SSD_HEADS = 16
SSD_HEAD_DIM = 64
SSD_STATE = 128
SSD_GROUPS = 2
SSD_HPG = SSD_HEADS // SSD_GROUPS
SSD_WIDTH = 1024
LRU_WIDTH = 1024
LRU_C = 8.0
D_FF = 2816
EPS = 1e-6
IN_COLS = 4624
N_SHARDS = 4
N_DEV = 8

ADAM_LR = 0.001
ADAM_B1 = 0.9
ADAM_B2 = 0.999
ADAM_EPS = 1e-08
ADAM_WD = 0.01
ADAM_STEP = 10

VMEM_LIMIT_BYTES = 56 * 1024 * 1024

NN = (((1,), (0,)), ((), ()))
NT = (((1,), (1,)), ((), ()))
TN = (((0,), (0,)), ((), ()))


def _cparams(*sem):
    return pltpu.CompilerParams(dimension_semantics=sem, vmem_limit_bytes=VMEM_LIMIT_BYTES)


def _dot(a, b, dims=NN):
    return lax.dot_general(a.astype(_MXU), b.astype(_MXU), dims, preferred_element_type=F32)


def _dot_onehot(a, b, dims=NN, *, data=0, pieces=3):
    ops = [a, b]
    mask = ops[1 - data].astype(jnp.bfloat16)
    rest = ops[data]
    acc = None
    for _ in range(pieces):
        piece = rest.astype(jnp.bfloat16)
        ops[data], ops[1 - data] = piece, mask
        d = lax.dot_general(ops[0], ops[1], dims, preferred_element_type=F32)
        acc = d if acc is None else acc + d
        rest = rest - piece.astype(F32)
    return acc


def _sigmoid(x):
    return 0.5 * (1.0 + jnp.tanh(0.5 * x))


def _softplus(x):
    return jnp.maximum(x, 0.0) + jnp.log(1.0 + jnp.exp(-jnp.abs(x)))


def _silu(x):
    return x * _sigmoid(x)


def _silu_grad(x):
    s = _sigmoid(x)
    return s * (1.0 + x * (1.0 - s))


_GELU_C = math.sqrt(2.0 / math.pi)


def _gelu_and_grad(x):
    inner = _GELU_C * (x + 0.044715 * x * x * x)
    t = jnp.tanh(inner)
    g = 0.5 * x * (1.0 + t)
    dg = 0.5 * (1.0 + t) + 0.5 * x * (1.0 - t * t) * _GELU_C * (1.0 + 3.0 * 0.044715 * x * x)
    return g, dg


def _rms_fwd(x, w):
    rstd = lax.rsqrt(jnp.mean(x * x, axis=-1, keepdims=True) + EPS)
    return x * rstd * w


def _rms_bwd(x, w, dy):
    rstd = lax.rsqrt(jnp.mean(x * x, axis=-1, keepdims=True) + EPS)
    xhat = x * rstd
    dxhat = dy * w
    dx = rstd * (dxhat - xhat * jnp.mean(dxhat * xhat, axis=-1, keepdims=True))
    return dx, dy * xhat


def _mm(terms, m, n, *, tm, tn, mode, out_dtype, name, residual=None, n_outer=False, also_mxu=False, behind=()):
    gm, gn = m // tm, n // tn
    assert gm * tm == m and gn * tn == n
    if n_outer:
        grid = (gn, gm)
        mi = lambda g0, g1: g1
        ni = lambda g0, g1: g0
    else:
        grid = (gm, gn)
        mi = lambda g0, g1: g0
        ni = lambda g0, g1: g1
    in_specs, args = [], []
    for (a, ka, b, kb, k) in terms:
        if mode == "tn":
            in_specs.append(pl.BlockSpec((k, tm), lambda g0, g1, ka=ka: (ka, mi(g0, g1))))
        else:
            in_specs.append(pl.BlockSpec((tm, k), lambda g0, g1, ka=ka: (mi(g0, g1), ka)))
        if mode == "nt":
            in_specs.append(pl.BlockSpec((tn, k), lambda g0, g1, kb=kb: (ni(g0, g1), kb)))
        else:
            in_specs.append(pl.BlockSpec((k, tn), lambda g0, g1, kb=kb: (kb, ni(g0, g1))))
        args += [a, b]
    if residual is not None:
        in_specs.append(pl.BlockSpec((tm, tn), lambda g0, g1: (mi(g0, g1), ni(g0, g1))))
        args.append(residual)
    dims = {"nn": NN, "nt": NT, "tn": TN}[mode]
    n_terms = len(terms)
    has_res = residual is not None
    in_specs += [pl.BlockSpec(memory_space=pl.ANY)] * len(behind)
    args += list(behind)
    n_in = len(args)

    def body(*refs):
        acc = None
        for t in range(n_terms):
            d = lax.dot_general(refs[2 * t][...], refs[2 * t + 1][...], dims, preferred_element_type=F32)
            acc = d if acc is None else acc + d
        if has_res:
            acc = acc + refs[2 * n_terms][...]
        refs[n_in][...] = acc.astype(out_dtype)
        if also_mxu:
            refs[n_in + 1][...] = acc.astype(_MXU)

    tile = pl.BlockSpec((tm, tn), lambda g0, g1: (mi(g0, g1), ni(g0, g1)))
    shape = jax.ShapeDtypeStruct((m, n), out_dtype)
    return pl.pallas_call(
        body, name=name, grid=grid, in_specs=in_specs,
        out_specs=[tile, tile] if also_mxu else tile,
        out_shape=[shape, jax.ShapeDtypeStruct((m, n), _MXU)] if also_mxu else shape,
        compiler_params=_cparams("parallel", "parallel"),
    )(*args)


def _embed(x, meta, behind=()):
    def body(x_ref, meta_ref, *rest):
        o_ref = rest[-1]
        i = pl.program_id(0)

        @pl.when(i == 0)
        def _():
            o_ref[0:PAD_ROWS, :] = jnp.zeros((PAD_ROWS, D_MODEL), F32)
            o_ref[PAD_ROWS:CHUNK, :] = meta_ref[...]

        @pl.when(i > 0)
        def _():
            o_ref[...] = x_ref[...]

    return pl.pallas_call(
        body, name="embed", grid=(N_CHUNKS,),
        in_specs=[pl.BlockSpec((CHUNK, D_MODEL), lambda i: (jnp.maximum(i - 1, 0), 0)),
                  pl.BlockSpec((N_META, D_MODEL), lambda i: (0, 0))] + [pl.BlockSpec(memory_space=pl.ANY)] * len(behind),
        out_specs=pl.BlockSpec((CHUNK, D_MODEL), lambda i: (i, 0)),
        out_shape=jax.ShapeDtypeStruct((T_ROWS, D_MODEL), F32),
        compiler_params=_cparams("parallel"),
    )(x, meta, *behind)


def _rmsnorm(h, w, *, name, tm=544):
    def body(h_ref, w_ref, o_ref):
        o_ref[...] = _rms_fwd(h_ref[...], w_ref[...]).astype(_MXU)

    return pl.pallas_call(
        body, name=name, grid=(T_ROWS // tm,),
        in_specs=[pl.BlockSpec((tm, D_MODEL), lambda i: (i, 0)), pl.BlockSpec((1, D_MODEL), lambda i: (0, 0))],
        out_specs=pl.BlockSpec((tm, D_MODEL), lambda i: (i, 0)),
        out_shape=jax.ShapeDtypeStruct((T_ROWS, D_MODEL), _MXU),
        compiler_params=_cparams("parallel"),
    )(h, w)


def _norm_proj(h, w, sections, convs, *, name, tm=272):
    widths = [s.shape[0] for s in sections]
    n = len(sections)
    conv_ks = sorted(convs)
    nc = len(conv_ks)

    def body(*refs):
        h_ref, w_ref = refs[:2]
        sec_refs = refs[2:2 + n]
        cw_refs = refs[2 + n:2 + n + nc]
        cb_refs = refs[2 + n + nc:2 + n + 2 * nc]
        u_ref = refs[2 + n + 2 * nc]
        proj_refs = refs[3 + n + 2 * nc:3 + 2 * n + 2 * nc]
        act_refs = refs[3 + 2 * n + 2 * nc:3 + 2 * n + 3 * nc]
        halo_refs = refs[3 + 2 * n + 3 * nc:]
        i = pl.program_id(0)

        @pl.when(i == 0)
        def _():
            for hr in halo_refs:
                hr[...] = jnp.zeros_like(hr)

        u = _rms_fwd(h_ref[...], w_ref[...]).astype(_MXU)
        u_ref[...] = u
        real = (i * tm + lax.broadcasted_iota(jnp.int32, (tm, 1), 0) >= PAD_ROWS).astype(F32)
        for k in range(n):
            raw = lax.dot_general(u, sec_refs[k][...], NT, preferred_element_type=F32)
            proj_refs[k][...] = raw
            if k not in convs:
                continue
            q = conv_ks.index(k)
            wv, bv = cw_refs[q][...], cb_refs[q][...]
            cat = jnp.concatenate([halo_refs[q][...], raw], axis=0)
            pre = bv + raw * wv[CONV_K - 1:CONV_K]
            for s in range(1, CONV_K):
                pre = pre + pltpu.roll(cat, s, 0)[8:8 + tm] * wv[CONV_K - 1 - s:CONV_K - s]
            y = _silu(pre) if convs[k][2] else pre
            act_refs[q][...] = y * real
            halo_refs[q][...] = raw[tm - 8:tm]

    row = lambda width: pl.BlockSpec((tm, width), lambda i: (i, 0))
    whole = lambda a: pl.BlockSpec(a.shape, lambda i: (0, 0))
    cws = [convs[k][0] for k in conv_ks]
    cbs = [convs[k][1] for k in conv_ks]
    outs = pl.pallas_call(
        body, name=name, grid=(T_ROWS // tm,),
        in_specs=[row(D_MODEL), pl.BlockSpec((1, D_MODEL), lambda i: (0, 0))]
        + [pl.BlockSpec((wd, D_MODEL), lambda i: (0, 0)) for wd in widths]
        + [whole(a) for a in cws] + [whole(a) for a in cbs],
        out_specs=[row(D_MODEL)] + [row(wd) for wd in widths] + [row(widths[k]) for k in conv_ks],
        out_shape=[jax.ShapeDtypeStruct((T_ROWS, D_MODEL), _MXU)]
        + [jax.ShapeDtypeStruct((T_ROWS, wd), F32) for wd in widths]
        + [jax.ShapeDtypeStruct((T_ROWS, widths[k]), F32) for k in conv_ks],
        scratch_shapes=[pltpu.VMEM((8, widths[k]), F32) for k in conv_ks],
        compiler_params=_cparams("arbitrary"),
    )(h, w, *sections, *cws, *cbs)
    return outs[0], list(outs[1:1 + n]), dict(zip(conv_ks, outs[1 + n:]))


def _loss_head(h2, target, fw):
    def body(h_ref, t_ref, w_ref, loss_ref, dh_ref, dhb_ref, dw_ref, acc_ref):
        i = pl.program_id(0)

        @pl.when(i == 0)
        def _():
            acc_ref[...] = jnp.zeros_like(acc_ref)
            dw_ref[...] = jnp.zeros_like(dw_ref)

        h = h_ref[...]
        w = w_ref[...]
        y = _rms_fwd(h, w)
        live = (i > 0).astype(F32)
        err = (y - t_ref[...]) * live
        acc_ref[...] += jnp.sum(err * err, axis=0, keepdims=True)
        dy = err * (1.0 / D_MODEL)
        dx, dwr = _rms_bwd(h, w, dy)
        dh_ref[...] = dx
        dhb_ref[...] = dx.astype(_MXU)
        dw_ref[...] += jnp.sum(dwr, axis=0, keepdims=True)

        @pl.when(i == N_CHUNKS - 1)
        def _():
            tot = jnp.sum(acc_ref[...], axis=1, keepdims=True) * (0.5 / D_MODEL)
            loss_ref[...] = jnp.broadcast_to(tot, (1, 128))

    return pl.pallas_call(
        body, name="loss_head", grid=(N_CHUNKS,),
        in_specs=[pl.BlockSpec((CHUNK, D_MODEL), lambda i: (i, 0)),
                  pl.BlockSpec((CHUNK, D_MODEL), lambda i: (jnp.maximum(i - 1, 0), 0)),
                  pl.BlockSpec((1, D_MODEL), lambda i: (0, 0))],
        out_specs=[pl.BlockSpec((1, 128), lambda i: (0, 0)),
                   pl.BlockSpec((CHUNK, D_MODEL), lambda i: (i, 0)),
                   pl.BlockSpec((CHUNK, D_MODEL), lambda i: (i, 0)),
                   pl.BlockSpec((1, D_MODEL), lambda i: (0, 0))],
        out_shape=[jax.ShapeDtypeStruct((1, 128), F32),
                   jax.ShapeDtypeStruct((T_ROWS, D_MODEL), F32),
                   jax.ShapeDtypeStruct((T_ROWS, D_MODEL), _MXU),
                   jax.ShapeDtypeStruct((1, D_MODEL), F32)],
        scratch_shapes=[pltpu.VMEM((1, D_MODEL), F32)],
        compiler_params=_cparams("arbitrary"),
    )(h2, target, fw)


def _mm_norm_bwd(terms, h, w, dres, *, name, tm=272, behind=()):
    n_terms = len(terms)
    in_specs, args = [], []
    for (a, b, k) in terms:
        in_specs += [pl.BlockSpec((tm, k), lambda i: (i, 0)), pl.BlockSpec((k, D_MODEL), lambda i: (0, 0))]
        args += [a, b]
    in_specs += [pl.BlockSpec((tm, D_MODEL), lambda i: (i, 0)), pl.BlockSpec((1, D_MODEL), lambda i: (0, 0)),
                 pl.BlockSpec((tm, D_MODEL), lambda i: (i, 0))] + [pl.BlockSpec(memory_space=pl.ANY)] * len(behind)
    args += [h, w, dres, *behind]

    def body(*refs):
        h_ref, w_ref, dres_ref = refs[2 * n_terms:2 * n_terms + 3]
        dh_ref, dhb_ref, dw_ref = refs[2 * n_terms + 3 + len(behind):]

        @pl.when(pl.program_id(0) == 0)
        def _():
            dw_ref[...] = jnp.zeros_like(dw_ref)

        du = None
        for t in range(n_terms):
            d = lax.dot_general(refs[2 * t][...], refs[2 * t + 1][...], NN, preferred_element_type=F32)
            du = d if du is None else du + d
        dx, dwr = _rms_bwd(h_ref[...], w_ref[...], du)
        dh = dres_ref[...] + dx
        dh_ref[...] = dh
        dhb_ref[...] = dh.astype(_MXU)
        dw_ref[...] += jnp.sum(dwr, axis=0, keepdims=True)

    return pl.pallas_call(
        body, name=name, grid=(T_ROWS // tm,), in_specs=in_specs,
        out_specs=[pl.BlockSpec((tm, D_MODEL), lambda i: (i, 0)), pl.BlockSpec((tm, D_MODEL), lambda i: (i, 0)),
                   pl.BlockSpec((1, D_MODEL), lambda i: (0, 0))],
        out_shape=[jax.ShapeDtypeStruct((T_ROWS, D_MODEL), F32), jax.ShapeDtypeStruct((T_ROWS, D_MODEL), _MXU),
                   jax.ShapeDtypeStruct((1, D_MODEL), F32)],
        compiler_params=_cparams("arbitrary"),
    )(*args)


FFN_TM = T_ROWS
FFN_TN = 256


def _ffn_up(u2, wg_t, wu_t):
    def body(u_ref, wg_ref, wu_ref, gp_ref, up_ref, act_ref):
        u = u_ref[...]
        gp = lax.dot_general(u, wg_ref[...], NT, preferred_element_type=F32)
        up = lax.dot_general(u, wu_ref[...], NT, preferred_element_type=F32)
        gp_ref[...] = gp.astype(_MXU)
        up_ref[...] = up.astype(_MXU)
        act_ref[...] = (_silu(gp) * up).astype(_MXU)

    tile = pl.BlockSpec((FFN_TM, FFN_TN), lambda j, i: (i, j))
    return pl.pallas_call(
        body, name="ffn_up", grid=(D_FF // FFN_TN, T_ROWS // FFN_TM),
        in_specs=[pl.BlockSpec((FFN_TM, D_MODEL), lambda j, i: (i, 0)),
                  pl.BlockSpec((FFN_TN, D_MODEL), lambda j, i: (j, 0)),
                  pl.BlockSpec((FFN_TN, D_MODEL), lambda j, i: (j, 0))],
        out_specs=[tile, tile, tile],
        out_shape=[jax.ShapeDtypeStruct((T_ROWS, D_FF), _MXU)] * 3,
        compiler_params=_cparams("parallel", "parallel"),
    )(u2, wg_t, wu_t)


def _ffn_bwd_act(dh2b, wd, gp, up):
    def body(dh_ref, wd_ref, gp_ref, up_ref, dgp_ref, dup_ref):
        dact = lax.dot_general(dh_ref[...], wd_ref[...], NT, preferred_element_type=F32)
        gp = gp_ref[...].astype(F32)
        dgp_ref[...] = (dact * up_ref[...].astype(F32) * _silu_grad(gp)).astype(_MXU)
        dup_ref[...] = (dact * _silu(gp)).astype(_MXU)

    tile = pl.BlockSpec((FFN_TM, FFN_TN), lambda j, i: (i, j))
    return pl.pallas_call(
        body, name="ffn_bwd_act", grid=(D_FF // FFN_TN, T_ROWS // FFN_TM),
        in_specs=[pl.BlockSpec((FFN_TM, D_MODEL), lambda j, i: (i, 0)),
                  pl.BlockSpec((FFN_TN, D_MODEL), lambda j, i: (j, 0)), tile, tile],
        out_specs=[tile, tile],
        out_shape=[jax.ShapeDtypeStruct((T_ROWS, D_FF), _MXU), jax.ShapeDtypeStruct((T_ROWS, D_FF), _MXU)],
        compiler_params=_cparams("parallel", "parallel"),
    )(dh2b, wd, gp, up)


CONV_TC = 512
CONV_K = 4


def _conv_pre(x_ref, wv, bv, c):
    tc = wv.shape[1]
    r0 = c * CHUNK
    cur = x_ref[r0:r0 + CHUNK, :]
    if c == 0:
        cat = jnp.concatenate([jnp.zeros((8, tc), F32), cur], axis=0)
        shifted = [cur] + [pltpu.roll(cat, s, 0)[8:8 + CHUNK] for s in range(1, CONV_K)]
    else:
        shifted = [cur] + [x_ref[r0 - s:r0 - s + CHUNK, :] for s in range(1, CONV_K)]
    pre = bv
    for s in range(CONV_K):
        pre = pre + shifted[s] * wv[CONV_K - 1 - s:CONV_K - s]
    return pre, shifted


def _row_mask(c):
    if c > 0:
        return None
    return (lax.broadcasted_iota(jnp.int32, (CHUNK, 1), 0) >= PAD_ROWS).astype(F32)


def _conv_bwd(dy, x, w, b, *, silu, name):
    cols = x.shape[1]
    tc = min(CONV_TC, cols)

    def body(dy_ref, x_ref, w_ref, b_ref, dx_ref, dw_ref, db_ref):
        wv, bv = w_ref[...], b_ref[...]
        next8 = jnp.zeros((8, tc), F32)
        dws = [jnp.zeros((1, tc), F32) for _ in range(CONV_K)]
        db = jnp.zeros((1, tc), F32)
        for c in reversed(range(N_CHUNKS)):
            r0 = c * CHUNK
            pre, shifted = _conv_pre(x_ref, wv, bv, c)
            dpre = dy_ref[r0:r0 + CHUNK, :]
            if silu:
                dpre = dpre * _silu_grad(pre)
            mask = _row_mask(c)
            if mask is not None:
                dpre = dpre * mask
            cat = jnp.concatenate([dpre, next8], axis=0)
            dx = dpre * wv[CONV_K - 1:CONV_K]
            for s in range(1, CONV_K):
                dx = dx + pltpu.roll(cat, CHUNK + 8 - s, 0)[0:CHUNK] * wv[CONV_K - 1 - s:CONV_K - s]
            dx_ref[r0:r0 + CHUNK, :] = dx.astype(_MXU)
            for s in range(CONV_K):
                k = CONV_K - 1 - s
                dws[k] = dws[k] + jnp.sum(dpre * shifted[s], axis=0, keepdims=True)
            db = db + jnp.sum(dpre, axis=0, keepdims=True)
            next8 = dpre[0:8]
        dw_ref[...] = jnp.concatenate(dws, axis=0)
        db_ref[...] = db

    return pl.pallas_call(
        body, name=name, grid=(cols // tc,),
        in_specs=[pl.BlockSpec((T_ROWS, tc), lambda j: (0, j)), pl.BlockSpec((T_ROWS, tc), lambda j: (0, j)),
                  pl.BlockSpec((CONV_K, tc), lambda j: (0, j)), pl.BlockSpec((1, tc), lambda j: (0, j))],
        out_specs=[pl.BlockSpec((T_ROWS, tc), lambda j: (0, j)), pl.BlockSpec((CONV_K, tc), lambda j: (0, j)),
                   pl.BlockSpec((1, tc), lambda j: (0, j))],
        out_shape=[jax.ShapeDtypeStruct((T_ROWS, cols), _MXU), jax.ShapeDtypeStruct((CONV_K, cols), F32),
                   jax.ShapeDtypeStruct((1, cols), F32)],
        compiler_params=_cparams("parallel"),
    )(dy, x, w, b)


def _ssd_chunk_common(dt_raw, prm, c):
    a_row = -jnp.exp(prm[1:2])
    dt = _softplus(dt_raw + prm[0:1])
    rows = lax.broadcasted_iota(jnp.int32, (CHUNK, 1), 0)
    real = jnp.logical_or(c > 0, rows >= PAD_ROWS)
    dt = jnp.where(real, dt, 0.0)
    li = lax.broadcasted_iota(jnp.int32, (CHUNK, CHUNK), 0)
    si = lax.broadcasted_iota(jnp.int32, (CHUNK, CHUNK), 1)
    causal = li >= si
    tri = causal.astype(F32)
    cs = _dot_onehot(tri, dt * a_row, data=1)
    return dt, a_row, cs, cs.T, causal, tri, real


def _gated_norm_fwd(y, z, w):
    g = y * _silu(z)
    half = SSD_WIDTH // SSD_GROUPS
    outs = [_rms_fwd(g[:, k * half:(k + 1) * half], w[:, k * half:(k + 1) * half]) for k in range(SSD_GROUPS)]
    return jnp.concatenate(outs, axis=1)


GROUP_W = SSD_WIDTH // SSD_GROUPS
PAIR_W = 2 * SSD_HEAD_DIM
STATE_SHAPE = (SSD_GROUPS, SSD_STATE, GROUP_W)


def _head_expander():
    r = lax.broadcasted_iota(jnp.int32, (128, SSD_WIDTH), 0)
    c = lax.broadcasted_iota(jnp.int32, (128, SSD_WIDTH), 1)
    return (c // SSD_HEAD_DIM == r).astype(F32)


def _ssd_expand(dt, cs, prm, ex):
    cs_x = _dot_onehot(cs, ex)
    cs_last_x = cs_x[CHUNK - 1:CHUNK, :]
    return (_dot_onehot(dt, ex, pieces=2), _dot_onehot(prm, ex)[2:3], jnp.exp(cs_x), jnp.exp(cs_last_x),
            jnp.exp(cs_last_x - cs_x))


def _ssd_fwd(xs, bc, dt_raw, z, prm, norm_w, ex):
    def body(xs_ref, bc_ref, dt_ref, z_ref, prm_ref, nw_ref, ex_ref, y_ref, yn_ref, prev_ref, state):
        c = pl.program_id(0)

        @pl.when(c == 0)
        def _():
            state[...] = jnp.zeros_like(state)

        prm = prm_ref[...]
        dt, a_row, cs, cs_t, causal, _, _ = _ssd_chunk_common(dt_ref[...], prm, c)
        dt_x, d_x, e_cs_x, e_last_x, dec_x = _ssd_expand(dt, cs, prm, ex_ref[...])
        xs_all = xs_ref[...]
        bc_all = bc_ref[...]
        xdt = xs_all * dt_x
        xdec = xdt * dec_x
        lane_lo = lax.broadcasted_iota(jnp.int32, (1, PAIR_W), 1) < SSD_HEAD_DIM
        for g in range(SSD_GROUPS):
            gs = slice(g * GROUP_W, (g + 1) * GROUP_W)
            b_g = bc_all[:, g * SSD_STATE:(g + 1) * SSD_STATE]
            c_g = bc_all[:, (SSD_GROUPS + g) * SSD_STATE:(SSD_GROUPS + g + 1) * SSD_STATE]
            st = state[g]
            prev_ref[0, g] = st
            y_off = _dot(c_g, st) * e_cs_x[:, gs]
            state[g] = st * e_last_x[:, gs] + _dot(b_g.T, xdec[:, gs])
            cb = _dot(c_g, b_g, NT)
            for k in range(SSD_HPG // 2):
                h0 = g * SSD_HPG + 2 * k
                ps = slice(h0 * SSD_HEAD_DIM, h0 * SSD_HEAD_DIM + PAIR_W)
                xdt_pair = xdt[:, ps]
                yd = []
                for h in (h0, h0 + 1):
                    lmat = jnp.where(causal, jnp.exp(cs[:, h:h + 1] - cs_t[h:h + 1, :]), 0.0)
                    yd.append(_dot(cb * lmat, xdt_pair))
                y_ref[:, ps] = (jnp.where(lane_lo, yd[0], yd[1]) + y_off[:, k * PAIR_W:(k + 1) * PAIR_W]
                                + xs_all[:, ps] * d_x[:, ps])
        yn_ref[...] = _gated_norm_fwd(y_ref[...], z_ref[...], nw_ref[...]).astype(_MXU)

    row = lambda w: pl.BlockSpec((CHUNK, w), lambda c: (c, 0))
    return pl.pallas_call(
        body, name="ssd_fwd", grid=(N_CHUNKS,),
        in_specs=[row(SSD_WIDTH), row(512), row(128), row(SSD_WIDTH),
                  pl.BlockSpec((8, 128), lambda c: (0, 0)), pl.BlockSpec((1, SSD_WIDTH), lambda c: (0, 0)),
                  pl.BlockSpec((128, SSD_WIDTH), lambda c: (0, 0))],
        out_specs=[row(SSD_WIDTH), row(SSD_WIDTH),
                   pl.BlockSpec((1,) + STATE_SHAPE, lambda c: (c, 0, 0, 0))],
        out_shape=[jax.ShapeDtypeStruct((T_ROWS, SSD_WIDTH), F32), jax.ShapeDtypeStruct((T_ROWS, SSD_WIDTH), _MXU),
                   jax.ShapeDtypeStruct((N_CHUNKS,) + STATE_SHAPE, F32)],
        scratch_shapes=[pltpu.VMEM(STATE_SHAPE, F32)],
        compiler_params=_cparams("arbitrary"),
    )(xs, bc, dt_raw, z, prm, norm_w, ex)


def _ssd_bwd(dyn, dyn_block, z, y_pre, xs, bc, dt_raw, prev, prm, norm_w, ex):
    def body(dyn_ref, z_ref, y_ref, xs_ref, bc_ref, dt_ref, prev_ref, prm_ref, nw_ref, ex_ref,
             dz_ref, dxs_ref, dbc_ref, ddt_ref, dprm_ref, dnw_ref, dstate):
        step = pl.program_id(0)
        c = N_CHUNKS - 1 - step

        @pl.when(step == 0)
        def _():
            dstate[...] = jnp.zeros_like(dstate)
            dprm_ref[...] = jnp.zeros_like(dprm_ref)
            dnw_ref[...] = jnp.zeros_like(dnw_ref)

        prm = prm_ref[...]
        dt, a_row, cs, cs_t, causal, tri, real = _ssd_chunk_common(dt_ref[...], prm, c)
        realf = real.astype(F32)
        z = z_ref[...]
        y_all = y_ref[...]
        nw = nw_ref[...]
        dyn_all = dyn_ref[...]
        sz = _silu(z)
        gated = y_all * sz
        half = SSD_WIDTH // SSD_GROUPS
        dgs, dnws = [], []
        for k in range(SSD_GROUPS):
            sl = slice(k * half, (k + 1) * half)
            dgk, dwk = _rms_bwd(gated[:, sl], nw[:, sl], dyn_all[:, sl])
            dgs.append(dgk)
            dnws.append(jnp.sum(dwk, axis=0, keepdims=True))
        dgated = jnp.concatenate(dgs, axis=1)
        dnw_ref[...] += jnp.concatenate(dnws, axis=1)
        dz_ref[...] = (dgated * y_all * _silu_grad(z)).astype(_MXU)
        dy_all = dgated * sz

        ex = ex_ref[...]
        dt_x, d_x, e_cs_x, e_last_x, dec_x = _ssd_expand(dt, cs, prm, ex)
        xs_all = xs_ref[...]
        bc_all = bc_ref[...]
        xdt = xs_all * dt_x
        xdt_mxu = xdt.astype(_MXU).astype(F32)
        xdec = xdt * dec_x
        dcp = dy_all * e_cs_x
        lane_lo = lax.broadcasted_iota(jnp.int32, (1, PAIR_W), 1) < SSD_HEAD_DIM
        upper = (lax.broadcasted_iota(jnp.int32, (CHUNK, CHUNK), 0)
                 <= lax.broadcasted_iota(jnp.int32, (CHUNK, CHUNK), 1))
        last_row = (lax.broadcasted_iota(jnp.int32, (CHUNK, 1), 0) == CHUNK - 1).astype(F32)
        dbs, dcs_, dxdt_parts, last_parts = [], [], [], []
        for g in range(SSD_GROUPS):
            gs = slice(g * GROUP_W, (g + 1) * GROUP_W)
            b_g = bc_all[:, g * SSD_STATE:(g + 1) * SSD_STATE]
            c_g = bc_all[:, (SSD_GROUPS + g) * SSD_STATE:(SSD_GROUPS + g + 1) * SSD_STATE]
            prev_t = prev_ref[0, g]
            dst = dstate[g]
            dc_g = _dot(dcp[:, gs], prev_t, NT)
            db_g = _dot(xdec[:, gs], dst, NT)
            dxdt_state = _dot(b_g, dst) * dec_x[:, gs]
            dstate[g] = dst * e_last_x[:, gs] + _dot(c_g.T, dcp[:, gs])
            last_parts.append(jnp.sum(xdt_mxu[:, gs] * dxdt_state, axis=0, keepdims=True)
                              + jnp.sum(dst * prev_t, axis=0, keepdims=True) * e_last_x[:, gs])
            cb_t = _dot(b_g, c_g, NT)
            dcb_t = jnp.zeros((CHUNK, CHUNK), F32)
            for k in range(SSD_HPG // 2):
                h0 = g * SSD_HPG + 2 * k
                ps = slice(h0 * SSD_HEAD_DIM, h0 * SSD_HEAD_DIM + PAIR_W)
                dy_pair = dy_all[:, ps]
                xdt_pair = xdt[:, ps]
                dd = []
                for h in (h0, h0 + 1):
                    lmat_t = jnp.where(upper, jnp.exp(cs_t[h:h + 1, :] - cs[:, h:h + 1]), 0.0)
                    dd.append(_dot(cb_t * lmat_t, dy_pair))
                    mine = lane_lo if h == h0 else jnp.logical_not(lane_lo)
                    dcb_t = dcb_t + _dot(jnp.where(mine, xdt_pair, 0.0), dy_pair, NT) * lmat_t
                dxdt_parts.append(jnp.where(lane_lo, dd[0], dd[1]) + dxdt_state[:, k * PAIR_W:(k + 1) * PAIR_W])
            dc_g = dc_g + _dot(dcb_t, b_g, TN)
            db_g = db_g + _dot(dcb_t, c_g)
            dbs.append(db_g * realf)
            dcs_.append(dc_g * realf)
        dbc_ref[...] = jnp.concatenate(dbs + dcs_, axis=1)
        dxdt = jnp.concatenate(dxdt_parts, axis=1)
        dxs_ref[...] = (dxdt * dt_x + dy_all * d_x) * realf
        ddt_all = _dot_onehot(dxdt * xs_all, ex, NT, pieces=2)
        rows = jnp.concatenate([jnp.concatenate(last_parts, axis=1), jnp.sum(dy_all * xs_all, axis=0, keepdims=True),
                                jnp.zeros((6, SSD_WIDTH), F32)], axis=0)
        rows = _dot_onehot(rows, ex, NT, pieces=2)
        dd_row = rows[1:2]
        dy_mxu = dy_all.astype(_MXU).astype(F32)
        dcs_all = (_dot_onehot(dy_mxu * (y_all - xs_all * d_x), ex, NT) - _dot_onehot(xdt_mxu * dxdt, ex, NT)
                   + last_row * rows[0:1])
        dda = _dot_onehot(tri, dcs_all, TN, data=1)
        ddt = (ddt_all + dda * a_row) * realf
        ddt_raw = ddt * _sigmoid(dt_ref[...] + prm[0:1])
        ddt_ref[...] = ddt_raw.astype(_MXU)
        da_log = jnp.sum(dda * dt, axis=0, keepdims=True) * a_row
        dprm_ref[0:1, :] += jnp.sum(ddt_raw, axis=0, keepdims=True)
        dprm_ref[1:2, :] += da_log
        dprm_ref[2:3, :] += dd_row

    rev = lambda w, blk=0: pl.BlockSpec((CHUNK, w), lambda s, blk=blk: (N_CHUNKS - 1 - s, blk))
    return pl.pallas_call(
        body, name="ssd_bwd", grid=(N_CHUNKS,),
        in_specs=[rev(SSD_WIDTH, dyn_block), rev(SSD_WIDTH), rev(SSD_WIDTH), rev(SSD_WIDTH), rev(512), rev(128),
                  pl.BlockSpec((1,) + STATE_SHAPE, lambda s: (N_CHUNKS - 1 - s, 0, 0, 0)),
                  pl.BlockSpec((8, 128), lambda s: (0, 0)), pl.BlockSpec((1, SSD_WIDTH), lambda s: (0, 0)),
                  pl.BlockSpec((128, SSD_WIDTH), lambda s: (0, 0))],
        out_specs=[rev(SSD_WIDTH), rev(SSD_WIDTH), rev(512), rev(128),
                   pl.BlockSpec((8, 128), lambda s: (0, 0)), pl.BlockSpec((1, SSD_WIDTH), lambda s: (0, 0))],
        out_shape=[jax.ShapeDtypeStruct((T_ROWS, SSD_WIDTH), _MXU), jax.ShapeDtypeStruct((T_ROWS, SSD_WIDTH), F32),
                   jax.ShapeDtypeStruct((T_ROWS, 512), F32), jax.ShapeDtypeStruct((T_ROWS, 128), _MXU),
                   jax.ShapeDtypeStruct((8, 128), F32), jax.ShapeDtypeStruct((1, SSD_WIDTH), F32)],
        scratch_shapes=[pltpu.VMEM(STATE_SHAPE, F32)],
        compiler_params=_cparams("arbitrary"),
    )(dyn, z, y_pre, xs, bc, dt_raw, prev, prm, norm_w, ex)


LRU_PAIRS = 8


def _lru_gates(xr, wa_ref, wx_ref, prm):
    pre_r, pre_i = [], []
    for k in range(LRU_PAIRS):
        xk = xr[:, k * 128:(k + 1) * 128]
        pre_r.append(_dot(xk, wa_ref[k]))
        pre_i.append(_dot(xk, wx_ref[k]))
    r = _sigmoid(jnp.concatenate(pre_r, axis=1) + prm[0:1])
    i = _sigmoid(jnp.concatenate(pre_i, axis=1) + prm[1:2])
    sp = _softplus(-prm[2:3])
    log_a = (-LRU_C) * r * sp
    a = jnp.exp(log_a)
    s = jnp.sqrt(-jnp.tanh(log_a) * (a * a + 1.0))
    return r, i, a, s, sp


def _lru_fwd(xr, gate, wa, wx, prm):
    def body(xr_ref, g_ref, wa_ref, wx_ref, prm_ref, hs_ref, yn_ref, carry, a_s, u_s):
        @pl.when(pl.program_id(0) == 0)
        def _():
            carry[...] = jnp.zeros_like(carry)

        prm = prm_ref[...]
        xr_t = xr_ref[...]
        _, i, a, s, _ = _lru_gates(xr_t, wa_ref, wx_ref, prm)
        a_s[...] = a
        u_s[...] = s * (i * xr_t)
        rid = lax.broadcasted_iota(jnp.int32, (8, LRU_WIDTH), 0)

        def group(k, before):
            off = pl.multiple_of(k * 8, 8)
            a8 = a_s[pl.ds(off, 8), :]
            u8 = u_s[pl.ds(off, 8), :]
            for d in (1, 2, 4):
                keep = rid >= d
                u8 = u8 + a8 * jnp.where(keep, pltpu.roll(u8, d, 0), 0.0)
                a8 = a8 * jnp.where(keep, pltpu.roll(a8, d, 0), 1.0)
            h8 = u8 + a8 * before
            hs_ref[pl.ds(off, 8), :] = h8
            return jnp.broadcast_to(h8[7:8], (8, LRU_WIDTH))

        carry[...] = lax.fori_loop(0, CHUNK // 8, group, carry[...])
        gel, _ = _gelu_and_grad(g_ref[...])
        yn_ref[...] = _rms_fwd(gel * hs_ref[...], prm[3:4]).astype(_MXU)

    row = pl.BlockSpec((CHUNK, LRU_WIDTH), lambda t: (t, 0))
    wspec = pl.BlockSpec((LRU_PAIRS, 128, 128), lambda t: (0, 0, 0))
    return pl.pallas_call(
        body, name="lru_fwd", grid=(N_CHUNKS,),
        in_specs=[row, row, wspec, wspec, pl.BlockSpec((8, LRU_WIDTH), lambda t: (0, 0))],
        out_specs=[row, row],
        out_shape=[jax.ShapeDtypeStruct((T_ROWS, LRU_WIDTH), F32), jax.ShapeDtypeStruct((T_ROWS, LRU_WIDTH), _MXU)],
        scratch_shapes=[pltpu.VMEM((8, LRU_WIDTH), F32), pltpu.VMEM((CHUNK, LRU_WIDTH), F32),
                        pltpu.VMEM((CHUNK, LRU_WIDTH), F32)],
        compiler_params=_cparams("arbitrary"),
    )(xr, gate, wa, wx, prm)


def _lru_bwd(dyn, dyn_block, gate, xr, hs, wa, wx, wa_t, wx_t, prm):
    def body(dyn_ref, g_ref, xr_ref, hs_ref, hsp_ref, wa_ref, wx_ref, wat_ref, wxt_ref, prm_ref,
             dg_ref, dxr_ref, dwa_ref, dwx_ref, dprm_ref, carry, a_s, d_s):
        step = pl.program_id(0)
        tile = N_CHUNKS - 1 - step

        @pl.when(step == 0)
        def _():
            carry[...] = jnp.zeros_like(carry)
            dwa_ref[...] = jnp.zeros_like(dwa_ref)
            dwx_ref[...] = jnp.zeros_like(dwx_ref)
            dprm_ref[...] = jnp.zeros_like(dprm_ref)

        prm = prm_ref[...]
        xr_t = xr_ref[...]
        r, i, a, s, sp = _lru_gates(xr_t, wa_ref, wx_ref, prm)
        hs_t = hs_ref[...]
        gel, dgel = _gelu_and_grad(g_ref[...])
        dy, dnw = _rms_bwd(gel * hs_t, prm[3:4], dyn_ref[...])
        dg_ref[...] = (dy * hs_t * dgel).astype(_MXU)
        a_s[...] = a
        d_s[...] = dy * gel
        rid = lax.broadcasted_iota(jnp.int32, (8, LRU_WIDTH), 0)

        def group(k, behind):
            off = pl.multiple_of((CHUNK // 8 - 1 - k) * 8, 8)
            a8 = a_s[pl.ds(off, 8), :]
            d8 = d_s[pl.ds(off, 8), :]
            c8 = jnp.where(rid == 7, 1.0, pltpu.roll(a8, 7, 0))
            for d in (1, 2, 4):
                keep = rid < 8 - d
                d8 = d8 + c8 * jnp.where(keep, pltpu.roll(d8, 8 - d, 0), 0.0)
                c8 = c8 * jnp.where(keep, pltpu.roll(c8, 8 - d, 0), 1.0)
            dht8 = d8 + c8 * behind
            d_s[pl.ds(off, 8), :] = dht8
            return jnp.broadcast_to(a8[0:1] * dht8[0:1], (8, LRU_WIDTH))

        carry[...] = lax.fori_loop(0, CHUNK // 8, group, carry[...])
        dht = d_s[...]
        before = hsp_ref[CHUNK - 8:CHUNK, :][7:8] * (tile > 0).astype(F32)
        first = lax.broadcasted_iota(jnp.int32, (CHUNK, 1), 0) == 0
        hprev = jnp.where(first, before, pltpu.roll(hs_t, 1, 0))
        da = dht * hprev
        ixr = i * xr_t
        ds = dht * ixr
        dlog_a = da * a - ds * (a * a) * lax.rsqrt(s * s)
        dr = dlog_a * ((-LRU_C) * sp)
        dsp = jnp.sum(dlog_a * ((-LRU_C) * r), axis=0, keepdims=True)
        dlam = dsp * (-_sigmoid(-prm[2:3]))
        di = dht * s * xr_t
        dpre_r = dr * r * (1.0 - r)
        dpre_i = di * i * (1.0 - i)
        dxr = dht * s * i
        parts = []
        for k in range(LRU_PAIRS):
            sl = slice(k * 128, (k + 1) * 128)
            parts.append(_dot(dpre_r[:, sl], wat_ref[k]) + _dot(dpre_i[:, sl], wxt_ref[k]))
            dwa_ref[k] += _dot(xr_t[:, sl], dpre_r[:, sl], TN)
            dwx_ref[k] += _dot(xr_t[:, sl], dpre_i[:, sl], TN)
        dxr_ref[...] = dxr + jnp.concatenate(parts, axis=1)
        dprm_ref[0:1, :] += jnp.sum(dpre_r, axis=0, keepdims=True)
        dprm_ref[1:2, :] += jnp.sum(dpre_i, axis=0, keepdims=True)
        dprm_ref[2:3, :] += dlam
        dprm_ref[3:4, :] += jnp.sum(dnw, axis=0, keepdims=True)

    rev = lambda blk=0: pl.BlockSpec((CHUNK, LRU_WIDTH), lambda s, blk=blk: (N_CHUNKS - 1 - s, blk))
    wspec = pl.BlockSpec((LRU_PAIRS, 128, 128), lambda s: (0, 0, 0))
    return pl.pallas_call(
        body, name="lru_bwd", grid=(N_CHUNKS,),
        in_specs=[rev(dyn_block), rev(), rev(), rev(),
                  pl.BlockSpec((CHUNK, LRU_WIDTH), lambda s: (jnp.maximum(N_CHUNKS - 2 - s, 0), 0)),
                  wspec, wspec, wspec, wspec, pl.BlockSpec((8, LRU_WIDTH), lambda s: (0, 0))],
        out_specs=[rev(), rev(), wspec, wspec, pl.BlockSpec((8, LRU_WIDTH), lambda s: (0, 0))],
        out_shape=[jax.ShapeDtypeStruct((T_ROWS, LRU_WIDTH), _MXU), jax.ShapeDtypeStruct((T_ROWS, LRU_WIDTH), F32),
                   jax.ShapeDtypeStruct((LRU_PAIRS, 128, 128), F32), jax.ShapeDtypeStruct((LRU_PAIRS, 128, 128), F32),
                   jax.ShapeDtypeStruct((8, LRU_WIDTH), F32)],
        scratch_shapes=[pltpu.VMEM((8, LRU_WIDTH), F32), pltpu.VMEM((CHUNK, LRU_WIDTH), F32),
                        pltpu.VMEM((CHUNK, LRU_WIDTH), F32)],
        compiler_params=_cparams("arbitrary"),
    )(dyn, gate, xr, hs, hs, wa, wx, wa_t, wx_t, prm)


SEC_NAMES = ("z", "xs", "bc", "dt", "g", "x")
SEC_WIDTH = {"z": 1024, "xs": 1024, "bc": 512, "dt": 128, "g": 1024, "x": 1024}


def _pair_blocks(w):
    w = w.reshape(LRU_PAIRS, 2, 64, 64)
    zero = jnp.zeros((LRU_PAIRS, 64, 64), w.dtype)
    top = jnp.concatenate([w[:, 0], zero], axis=2)
    bot = jnp.concatenate([zero, w[:, 1]], axis=2)
    return jnp.concatenate([top, bot], axis=1)


def _unpair_blocks(wp):
    return jnp.stack([wp[:, :64, :64], wp[:, 64:, 64:]], axis=1).reshape(16, 64, 64)


def _pad_lanes(v, width=128):
    return jnp.pad(v, ((0, 0), (0, width - v.shape[1])))


class _Resident:
    before_embed = ()

    def __init__(self, w_in_sections, w_out, w_gate, w_up, w_down):
        self._w_in, self._w_out, self._ffn = w_in_sections, w_out, (w_gate, w_up, w_down)

    def w_in(self, after):
        return self._w_in

    def mid_forward(self, after):
        return jnp.zeros((1, 1), F32)

    def w_out(self, after):
        return self._w_out

    def ffn(self, after):
        return self._ffn

    def grads_ready(self, names, g, g_mxu):
        return jnp.zeros((1, 1), F32)

    def small_ready(self, g, loss):
        return jnp.zeros((1, 1), F32)

    def small_middle(self, after):
        return jnp.zeros((1, 1), F32)


def _local_step(x, target, meta, p, late):
    g, g_mxu = {}, {}
    ex = _head_expander()
    h0 = _embed(x, meta, late.before_embed)
    w_in = late.w_in(h0)
    convs = {SEC_NAMES.index("xs"): (p["ssd_conv_w"][:, :SSD_WIDTH], p["ssd_conv_b"][:, :SSD_WIDTH], True),
             SEC_NAMES.index("bc"): (p["ssd_conv_w"][:, SSD_WIDTH:], p["ssd_conv_b"][:, SSD_WIDTH:], True),
             SEC_NAMES.index("x"): (p["lru_conv_w"], p["lru_conv_b"], False)}
    u1, projs, acts = _norm_proj(h0, p["norm1_w"], [w_in[s] for s in SEC_NAMES], convs, name="norm_in_proj")
    proj = dict(zip(SEC_NAMES, projs))
    xs_act, bc_act, xr = (acts[SEC_NAMES.index(s)] for s in ("xs", "bc", "x"))
    ssd_prm = jnp.concatenate([_pad_lanes(p["ssd_dt_bias"]), _pad_lanes(p["ssd_a_log"]), _pad_lanes(p["ssd_d"]),
                               jnp.zeros((5, 128), F32)], axis=0)
    y_pre, y_ssd, prev = _ssd_fwd(xs_act, bc_act, proj["dt"], proj["z"], ssd_prm, p["ssd_norm_w"], ex)
    wa_p, wx_p = _pair_blocks(p["lru_wa"]), _pair_blocks(p["lru_wx"])
    lru_prm = jnp.concatenate([p["lru_ba"], p["lru_bx"], p["lru_lambda"], p["lru_norm_w"],
                               jnp.zeros((4, LRU_WIDTH), F32)], axis=0)
    hs, y_lru = _lru_fwd(xr, proj["g"], wa_p.astype(_MXU), wx_p.astype(_MXU),
                         lru_prm + late.mid_forward([xr, y_ssd]))
    ycat = jnp.concatenate([y_ssd, y_lru], axis=1)
    w_out = late.w_out(ycat)
    h1 = _mm([(ycat, 0, w_out, 0, 2 * D_MODEL)], T_ROWS, D_MODEL, tm=T_ROWS, tn=256, mode="nn", out_dtype=F32,
             name="out_proj", residual=h0)
    u2 = _rmsnorm(h1, p["norm2_w"], name="norm2")
    w_gate, w_up, w_down = late.ffn(u2)
    gp, up, act = _ffn_up(u2, w_gate, w_up)
    h2 = _mm([(act, 0, w_down, 0, D_FF)], T_ROWS, D_MODEL, tm=T_ROWS, tn=256, mode="nn", out_dtype=F32,
             name="ffn_down", residual=h1)
    loss, dh2, dh2b, g["final_norm_w"] = _loss_head(h2, target, p["final_norm_w"])
    dgp, dup = _ffn_bwd_act(dh2b, w_down, gp, up)
    g["w_down"], g_mxu["w_down"] = _mm([(act, 0, dh2b, 0, T_ROWS)], D_FF, D_MODEL, tm=1408, tn=512, mode="tn",
                                       out_dtype=F32, name="dw_down", also_mxu=True)
    dh1, dh1b, g["norm2_w"] = _mm_norm_bwd([(dgp, w_gate, D_FF), (dup, w_up, D_FF)], h1, p["norm2_w"], dh2,
                                           name="ffn_bwd_in")
    g["w_gate"], g_mxu["w_gate"] = _mm([(dgp, 0, u2, 0, T_ROWS)], D_FF, D_MODEL, tm=1408, tn=512, mode="tn",
                                       out_dtype=F32, name="dw_gate", also_mxu=True)
    g["w_up"], g_mxu["w_up"] = _mm([(dup, 0, u2, 0, T_ROWS)], D_FF, D_MODEL, tm=1408, tn=512, mode="tn",
                                   out_dtype=F32, name="dw_up", also_mxu=True)
    g["w_out"], g_mxu["w_out"] = _mm([(ycat, 0, dh1b, 0, T_ROWS)], 2 * D_MODEL, D_MODEL, tm=1024, tn=512, mode="tn",
                                     out_dtype=F32, name="dw_out", also_mxu=True)
    sent = late.grads_ready(("w_down", "w_gate", "w_up", "w_out"), g, g_mxu)
    dycat = _mm([(dh1b, 0, w_out, 0, D_MODEL)], T_ROWS, 2 * D_MODEL, tm=T_ROWS, tn=256, mode="nt", out_dtype=F32,
                name="out_proj_bwd", behind=(sent,))
    dgate, dxr, dwa_p, dwx_p, dlru_prm = _lru_bwd(dycat, 1, proj["g"], xr, hs, wa_p.astype(_MXU), wx_p.astype(_MXU),
                                                  jnp.swapaxes(wa_p, 1, 2).astype(_MXU),
                                                  jnp.swapaxes(wx_p, 1, 2).astype(_MXU), lru_prm)
    g["lru_wa"], g["lru_wx"] = _unpair_blocks(dwa_p), _unpair_blocks(dwx_p)
    g["lru_ba"], g["lru_bx"], g["lru_lambda"], g["lru_norm_w"] = (dlru_prm[k:k + 1] for k in range(4))
    dx_lru, g["lru_conv_w"], g["lru_conv_b"] = _conv_bwd(dxr, proj["x"], p["lru_conv_w"], p["lru_conv_b"], silu=False,
                                                         name="lru_conv_bwd")
    dz, dxs_act, dbc_act, ddt, dssd_prm, g["ssd_norm_w"] = _ssd_bwd(dycat, 0, proj["z"], y_pre, xs_act, bc_act,
                                                                    proj["dt"], prev, ssd_prm, p["ssd_norm_w"], ex)
    g["ssd_dt_bias"], g["ssd_a_log"], g["ssd_d"] = (dssd_prm[k:k + 1, :SSD_HEADS] for k in range(3))
    dxs, dcw_xs, dcb_xs = _conv_bwd(dxs_act, proj["xs"], p["ssd_conv_w"][:, :SSD_WIDTH],
                                    p["ssd_conv_b"][:, :SSD_WIDTH], silu=True, name="ssd_conv_xs_bwd")
    dbc, dcw_bc, dcb_bc = _conv_bwd(dbc_act, proj["bc"], p["ssd_conv_w"][:, SSD_WIDTH:],
                                    p["ssd_conv_b"][:, SSD_WIDTH:], silu=True, name="ssd_conv_bc_bwd")
    g["ssd_conv_w"] = jnp.concatenate([dcw_xs, dcw_bc], axis=1)
    g["ssd_conv_b"] = jnp.concatenate([dcb_xs, dcb_bc], axis=1)
    dproj = {"z": dz, "xs": dxs, "bc": dbc, "dt": ddt, "g": dgate, "x": dx_lru}
    for s in SEC_NAMES:
        wdt = SEC_WIDTH[s]
        g["w_in_" + s], g_mxu["w_in_" + s] = _mm([(dproj[s], 0, u1, 0, T_ROWS)], wdt, D_MODEL, tm=min(wdt, 1024),
                                                 tn=512, mode="tn", out_dtype=F32, name="dw_in_" + s, also_mxu=True)
    sent = late.grads_ready(("w_in",), g, g_mxu)
    dh0, _, g["norm1_w"] = _mm_norm_bwd([(dproj[s], w_in[s], SEC_WIDTH[s]) for s in SEC_NAMES], h0,
                                        p["norm1_w"], dh1, name="in_proj_bwd", behind=(sent,))
    g["meta_tokens"] = dh0[PAD_ROWS:X_ROW0]
    late.small_ready(g, loss)
    return loss, dh0[X_ROW0:], g, g_mxu


MESH = pl.DeviceIdType.MESH
ANY = pl.BlockSpec(memory_space=pl.ANY)


def _my_place():
    return lax.axis_index("x"), lax.axis_index("y"), lax.axis_index("c")


def _other_chips(x, y):
    return [(1 - x, y), (x, 1 - y), (1 - x, 1 - y)]


HBM_SPEC = pl.BlockSpec(memory_space=pltpu.HBM)
SEM_SPEC = pl.BlockSpec(memory_space=pltpu.SEMAPHORE)
SPLIT_EFFECT = pltpu.SideEffectType.DATAFLOW_SIDE_EFFECTING


def _half_cols(buf, c, other=False):
    half = buf.shape[-1] // 2
    return pl.ds(pl.multiple_of(((1 - c) if other else c) * half, 128), half)


def _halves_plan(bufs, x, y, c, incoming):
    plan = []
    for buf in bufs:
        cols = _half_cols(buf, c)
        for (px, py) in _other_chips(x, y):
            slot = 2 * px + py if incoming else 2 * x + y
            plan.append((buf.at[2 * x + y, :, cols], buf.at[slot, :, cols], (px, py, c)))
    return plan


def _forward_plan(bufs, x, y, c, incoming):
    plan = []
    for buf in bufs:
        for (px, py) in _other_chips(x, y):
            slot = 2 * px + py
            plan.append((buf.at[slot, :, _half_cols(buf, c)], buf.at[slot, :, _half_cols(buf, c, other=incoming)],
                         (x, y, 1 - c)))
    return plan


def _scatter_plan(bufs, x, y, c, incoming):
    n = len(bufs) // 2
    plan = []
    for k in range(n):
        for j, (px, py) in enumerate(_other_chips(x, y)):
            plan.append((bufs[k].at[2 * px + py], bufs[n + k].at[j], (px, py, c)))
    return plan


def _split_start(bufs, plan, n_copies, after, *, name):
    n = len(bufs)
    extra = [] if after is None else [after]

    def body(*refs):
        ins = refs[:n]
        send_sems, recv_sems = refs[n + len(extra)], refs[n + len(extra) + 1]
        token = refs[-1]
        x, y, c = _my_place()
        for i, (src, dst, dev) in enumerate(plan(ins, x, y, c, False)):
            pltpu.make_async_remote_copy(src_ref=src, dst_ref=dst, send_sem=send_sems.at[i], recv_sem=recv_sems.at[i],
                                         device_id=dev, device_id_type=MESH).start()
        token[...] = jnp.zeros_like(token)

    outs = pl.pallas_call(
        body, name=name,
        out_shape=(pltpu.SemaphoreType.DMA((n_copies,)), pltpu.SemaphoreType.DMA((n_copies,)),
                   *[pltpu.HBM(b.shape, b.dtype) for b in bufs], jax.ShapeDtypeStruct((8, 128), F32)),
        in_specs=[HBM_SPEC] * n + [ANY] * len(extra),
        out_specs=(SEM_SPEC, SEM_SPEC, *[HBM_SPEC] * n, pl.BlockSpec(memory_space=pltpu.VMEM)),
        input_output_aliases={k: 2 + k for k in range(n)},
        compiler_params=pltpu.CompilerParams(has_side_effects=SPLIT_EFFECT),
    )(*[pltpu.with_memory_space_constraint(b, pltpu.HBM) for b in bufs], *extra)
    return outs[0], outs[1], list(outs[2:2 + n]), outs[-1]


def _split_wait(bufs, send_sems, recv_sems, plan, after, *, name):
    n = len(bufs)
    after = list(after) if isinstance(after, (list, tuple)) else [after]

    def body(*refs):
        ins = refs[:n]
        send_sems_ref, recv_sems_ref = refs[n], refs[n + 1]
        x, y, c = _my_place()
        for i, (src, dst, dev) in enumerate(plan(ins, x, y, c, True)):
            cp = pltpu.make_async_remote_copy(src_ref=src, dst_ref=dst, send_sem=send_sems_ref.at[i],
                                              recv_sem=recv_sems_ref.at[i], device_id=dev, device_id_type=MESH)
            cp.wait_send()
            cp.wait_recv()

    outs = pl.pallas_call(
        body, name=name, out_shape=tuple(pltpu.HBM(b.shape, b.dtype) for b in bufs),
        in_specs=[HBM_SPEC] * n + [SEM_SPEC, SEM_SPEC] + [ANY] * len(after), out_specs=tuple([HBM_SPEC] * n),
        input_output_aliases={k: k for k in range(n)},
        compiler_params=pltpu.CompilerParams(has_side_effects=SPLIT_EFFECT),
    )(*bufs, send_sems, recv_sems, *after)
    return list(outs)


def _fill_own_slots(shards, me_arr, *, name, behind=()):
    n = len(shards)
    n_in = n + len(behind)

    def body(me_ref, *refs):
        for k in range(n):
            refs[n_in + k][0] = refs[k][...].astype(_MXU)

    half = D_MODEL // 2
    return pl.pallas_call(
        body, name=name,
        grid_spec=pltpu.PrefetchScalarGridSpec(
            num_scalar_prefetch=1, grid=(2,),
            in_specs=[pl.BlockSpec((s.shape[0], half), lambda i, me: (0, i)) for s in shards]
            + [pl.BlockSpec(memory_space=pl.ANY)] * len(behind),
            out_specs=[pl.BlockSpec((1, s.shape[0], half), lambda i, me: (me[0], 0, i)) for s in shards]),
        out_shape=[jax.ShapeDtypeStruct((N_SHARDS,) + s.shape, _MXU) for s in shards],
        compiler_params=_cparams("parallel"),
    )(me_arr, *shards, *behind)


def _gather_small(small):
    def body(s_ref, o_ref, send_sems, recv_sems, local_sem):
        x, y, c = _my_place()
        me = 2 * x + y
        local = pltpu.make_async_copy(s_ref, o_ref.at[me], local_sem)
        local.start()
        copies = [(pltpu.make_async_remote_copy(src_ref=s_ref, dst_ref=o_ref.at[me], send_sem=send_sems.at[j],
                                                recv_sem=recv_sems.at[j], device_id=(px, py, c), device_id_type=MESH),
                   2 * px + py) for j, (px, py) in enumerate(_other_chips(x, y))]
        for cp, _ in copies:
            cp.start()
        for j, (cp, slot) in enumerate(copies):
            cp.wait_send()
            pltpu.make_async_remote_copy(src_ref=s_ref, dst_ref=o_ref.at[slot], send_sem=send_sems.at[j],
                                         recv_sem=recv_sems.at[j], device_id=(x, y, c),
                                         device_id_type=MESH).wait_recv()
        local.wait()

    return pl.pallas_call(
        body, name="gather_small", in_specs=[ANY], out_specs=ANY,
        out_shape=jax.ShapeDtypeStruct((N_SHARDS,) + small.shape, small.dtype),
        scratch_shapes=[pltpu.SemaphoreType.DMA((3,)), pltpu.SemaphoreType.DMA((3,)), pltpu.SemaphoreType.DMA],
    )(small)


def _swap_with_sibling(parts, *, name, behind=()):
    n = len(parts)
    nb = len(behind)

    def body(*refs):
        ins, outs = refs[:n], refs[n + nb:2 * n + nb]
        send_sems, recv_sems = refs[2 * n + nb:]
        x, y, c = _my_place()
        copies = [pltpu.make_async_remote_copy(
            src_ref=ins[k], dst_ref=outs[k], send_sem=send_sems.at[k], recv_sem=recv_sems.at[k],
            device_id=(x, y, 1 - c), device_id_type=MESH) for k in range(n)]
        for cp in copies:
            cp.start()
        for cp in copies:
            cp.wait()

    return pl.pallas_call(
        body, name=name, in_specs=[ANY] * (n + nb), out_specs=[ANY] * n,
        out_shape=[jax.ShapeDtypeStruct(a.shape, a.dtype) for a in parts],
        scratch_shapes=[pltpu.SemaphoreType.DMA((n,)), pltpu.SemaphoreType.DMA((n,))],
    )(*parts, *behind)


def _other_devices(x, y, c):
    out = []
    for mask in range(1, N_DEV):
        px, py, pc = x ^ (mask >> 2 & 1), y ^ (mask >> 1 & 1), c ^ (mask & 1)
        out.append(((px, py, pc), 4 * px + 2 * py + pc))
    return out


def _pieces_plan(bufs, x, y, c, incoming):
    pack, land = bufs
    me = 4 * x + 2 * y + c
    return [(pack.at[num], land.at[num if incoming else me], dev) for dev, num in _other_devices(x, y, c)]


def _spread_plan(bufs, x, y, c, incoming):
    piece, land = bufs
    me = 4 * x + 2 * y + c
    return [(piece, land.at[num if incoming else me], dev) for dev, num in _other_devices(x, y, c)]


def _sum_pieces(pack, land, dev_arr, *, name):
    def body(dev_ref, pack_ref, land_ref, o_ref):
        dev = dev_ref[0]
        own = pack_ref[dev]
        acc = None
        for d in range(N_DEV):
            term = jnp.where(dev == d, own, land_ref[d])
            acc = term if acc is None else acc + term
        o_ref[...] = acc

    vmem = pl.BlockSpec(memory_space=pltpu.VMEM)
    return pl.pallas_call(
        body, name=name, in_specs=[pl.BlockSpec(memory_space=pltpu.SMEM), vmem, vmem], out_specs=vmem,
        out_shape=jax.ShapeDtypeStruct(pack.shape[1:], F32),
    )(dev_arr, pack, land)


def _join_pieces(piece, land, dev_arr, *, name):
    def body(dev_ref, piece_ref, land_ref, o_ref):
        dev = dev_ref[0]
        for d in range(N_DEV):
            o_ref[d] = jnp.where(dev == d, piece_ref[...], land_ref[d])

    vmem = pl.BlockSpec(memory_space=pltpu.VMEM)
    return pl.pallas_call(
        body, name=name, in_specs=[pl.BlockSpec(memory_space=pltpu.SMEM), vmem, vmem], out_specs=vmem,
        out_shape=jax.ShapeDtypeStruct(land.shape, F32),
    )(dev_arr, piece, land)


def _adamw_native(ws, gs, ms, vs):
    n = len(ws)

    def body(*refs):
        for k in range(n):
            w_ref, g_ref, m_ref, v_ref = (refs[j * n + k] for j in range(4))
            delta, m_new, v_new = _adamw_math(w_ref[...], g_ref[...], m_ref[...], v_ref[...])
            refs[4 * n + k][...] = delta
            refs[5 * n + k][...] = m_new
            refs[6 * n + k][...] = v_new

    vmem = pl.BlockSpec(memory_space=pltpu.VMEM)
    shapes = [jax.ShapeDtypeStruct(a.shape, F32) for a in ws]
    outs = pl.pallas_call(
        body, name="adamw_small", in_specs=[vmem] * (4 * n), out_specs=[vmem] * (3 * n), out_shape=shapes * 3,
        compiler_params=pltpu.CompilerParams(vmem_limit_bytes=VMEM_LIMIT_BYTES),
    )(*ws, *gs, *ms, *vs)
    return outs[:n], outs[n:2 * n], outs[2 * n:]


def _elementwise_tile(rows, cols):
    for t in range(256, 15, -16):
        if rows % t == 0:
            return (t, cols), rows // t, lambda i: (i, 0)
    assert cols % 256 == 0
    return (rows, 256), cols // 256, lambda i: (0, i)


def _partial_sum(own, land, me_arr, *, name):
    r, c = own.shape[-2:]
    tile, steps, imap = _elementwise_tile(r, c)
    whole = own.ndim == 3

    def body(me_ref, own_ref, land_ref, o_ref):
        acc = own_ref[0] if whole else own_ref[...]
        for j in range(3):
            acc = acc + land_ref[j].astype(F32)
        o_ref[...] = acc.astype(_MXU)

    own_spec = (pl.BlockSpec((1,) + tile, lambda i, me: (me[0],) + imap(i)) if whole
                else pl.BlockSpec(tile, lambda i, me: imap(i)))
    return pl.pallas_call(
        body, name=name,
        grid_spec=pltpu.PrefetchScalarGridSpec(
            num_scalar_prefetch=1, grid=(steps,),
            in_specs=[own_spec, pl.BlockSpec((3,) + tile, lambda i, me: (0,) + imap(i))],
            out_specs=pl.BlockSpec(tile, lambda i, me: imap(i))),
        out_shape=jax.ShapeDtypeStruct((r, c), _MXU),
        compiler_params=_cparams("parallel"),
    )(me_arr, own, land)


LANE_TILE = 256


def _partial_sums(owns, lands, me_arr, *, name):
    n = len(owns)

    def body(me_ref, *refs):
        for k in range(n):
            acc = refs[k][0]
            for j in range(3):
                acc = acc + refs[n + k][j].astype(F32)
            refs[2 * n + k][...] = acc.astype(_MXU)

    rows = [o.shape[1] for o in owns]
    return pl.pallas_call(
        body, name=name,
        grid_spec=pltpu.PrefetchScalarGridSpec(
            num_scalar_prefetch=1, grid=(D_MODEL // LANE_TILE,),
            in_specs=[pl.BlockSpec((1, r, LANE_TILE), lambda i, me: (me[0], 0, i)) for r in rows]
            + [pl.BlockSpec((3, r, LANE_TILE), lambda i, me: (0, 0, i)) for r in rows],
            out_specs=[pl.BlockSpec((r, LANE_TILE), lambda i, me: (0, i)) for r in rows]),
        out_shape=[jax.ShapeDtypeStruct((r, D_MODEL), _MXU) for r in rows],
        compiler_params=_cparams("parallel"),
    )(me_arr, *owns, *lands)


def _adamws(ws, parts_a, parts_b, ms, vs, *, name):
    n = len(ws)

    def body(*refs):
        for k in range(n):
            w_ref, a_ref, b_ref, m_ref, v_ref = (refs[j * n + k] for j in range(5))
            g = a_ref[...].astype(F32) + b_ref[...].astype(F32)
            delta, m_new, v_new = _adamw_math(w_ref[...], g, m_ref[...], v_ref[...])
            for j, val in enumerate((g, delta, m_new, v_new)):
                refs[(5 + j) * n + k][...] = val

    tiles = [pl.BlockSpec((w.shape[0], LANE_TILE), lambda i: (0, i)) for w in ws]
    outs = pl.pallas_call(
        body, name=name, grid=(D_MODEL // LANE_TILE,), in_specs=tiles * 5, out_specs=tiles * 4,
        out_shape=[jax.ShapeDtypeStruct(w.shape, F32) for w in ws] * 4,
        compiler_params=_cparams("parallel"),
    )(*ws, *parts_a, *parts_b, *ms, *vs)
    return [outs[j * n:(j + 1) * n] for j in range(4)]


def _adamw_math(w, g, m, v):
    m = ADAM_B1 * m + (1.0 - ADAM_B1) * g
    v = ADAM_B2 * v + (1.0 - ADAM_B2) * (g * g)
    m_hat = m / (1.0 - ADAM_B1 ** ADAM_STEP)
    v_hat = v / (1.0 - ADAM_B2 ** ADAM_STEP)
    delta = -ADAM_LR * (m_hat / (jnp.sqrt(v_hat) + ADAM_EPS) + ADAM_WD * w)
    return delta, m, v


def _adamw(w, grad_parts, m, v, *, name):
    if w.ndim == 3:
        steps = 4
        assert w.shape[0] % steps == 0
        tile_shape, imap = (w.shape[0] // steps,) + w.shape[1:], lambda i: (i, 0, 0)
    else:
        tile_shape, steps, imap = _elementwise_tile(*w.shape)
    n = len(grad_parts)

    def body(*refs):
        w_ref, m_ref, v_ref = refs[:3]
        g_refs = refs[3:3 + n]
        g_out, d_out, m_out, v_out = refs[3 + n:]
        g = g_refs[0][...].astype(F32)
        for k in range(1, n):
            g = g + g_refs[k][...].astype(F32)
        delta, m_new, v_new = _adamw_math(w_ref[...], g, m_ref[...], v_ref[...])
        g_out[...] = g
        d_out[...] = delta
        m_out[...] = m_new
        v_out[...] = v_new

    tile = pl.BlockSpec(tile_shape, imap)
    return pl.pallas_call(
        body, name=name, grid=(steps,), in_specs=[tile] * (3 + n), out_specs=[tile] * 4,
        out_shape=[jax.ShapeDtypeStruct(w.shape, F32)] * 4,
        compiler_params=_cparams("parallel"),
    )(w, m, v, *grad_parts)


WEIGHT_NAMES = ("meta_tokens", "norm1_w", "w_in", "ssd_conv_w", "ssd_conv_b", "ssd_dt_bias", "ssd_a_log", "ssd_d",
                "ssd_norm_w", "lru_conv_w", "lru_conv_b", "lru_wa", "lru_ba", "lru_wx", "lru_bx", "lru_lambda",
                "lru_norm_w", "w_out", "norm2_w", "w_gate", "w_up", "w_down", "final_norm_w")
BIG = ("w_in", "w_out", "w_gate", "w_up", "w_down")
FFN = ("w_gate", "w_up", "w_down")
LATE = ("w_out",) + FFN
SMALL_SHARDED = {"meta_tokens": (N_META, D_MODEL), "ssd_conv_w": (CONV_K, 1536), "lru_conv_w": (CONV_K, LRU_WIDTH)}
SMALL = tuple(n for n in WEIGHT_NAMES if n not in BIG)
PACK_COLS = 1024


def _pack(arrays, row_multiple):
    flat = jnp.concatenate([a.reshape(-1) for a in arrays])
    rows = -(-flat.shape[0] // (row_multiple * PACK_COLS)) * row_multiple
    return jnp.pad(flat, (0, rows * PACK_COLS - flat.shape[0])).reshape(rows, PACK_COLS)


def _unpack(pack, shapes):
    flat = pack.reshape(-1)
    out, off = [], 0
    for s in shapes:
        size = math.prod(s)
        out.append(flat[off:off + size].reshape(s))
        off += size
    return out


def _unshard_cols(g4):
    return jnp.swapaxes(g4, 0, 1).reshape(g4.shape[1], -1)


COL_SHARDED = ("w_in", "w_gate", "w_up")
IN_ROWS = {"z": (0, 1024), "xs": (1024, 2048), "bc": (2048, 2560), "dt": (2560, 2576), "g": (2576, 3600),
           "x": (3600, IN_COLS)}


def _rows_of_shards(shards4, lo, hi):
    r = shards4.shape[1]
    parts = [shards4[k, max(lo, k * r) - k * r:min(hi, (k + 1) * r) - k * r]
             for k in range(N_SHARDS) if max(lo, k * r) < min(hi, (k + 1) * r)]
    return parts[0] if len(parts) == 1 else jnp.concatenate(parts, axis=0)


def _w_in_shard_rows(k, sections):
    lo, hi = k * (IN_COLS // N_SHARDS), (k + 1) * (IN_COLS // N_SHARDS)
    parts = []
    for arr, (a, b) in zip(sections, IN_ROWS.values()):
        if max(lo, a) < min(hi, b):
            parts.append(arr[max(lo, a) - a:min(hi, b) - a])
    return jnp.concatenate(parts, axis=0)


def _rows_view(name, block):
    return jnp.swapaxes(block[0], 0, 1) if name in COL_SHARDED else block[0]


def _param_view(name, rows):
    return (jnp.swapaxes(rows, 0, 1) if name in COL_SHARDED else rows)[None]


def kernel(x, meta_tokens, norm1_w, w_in, ssd_conv_w, ssd_conv_b, ssd_dt_bias, ssd_a_log, ssd_d, ssd_norm_w, lru_conv_w, lru_conv_b, lru_wa, lru_ba, lru_wx, lru_bx, lru_lambda, lru_norm_w, w_out, norm2_w, w_gate, w_up, w_down, final_norm_w, loss_target, m_meta_tokens, m_norm1_w, m_w_in, m_ssd_conv_w, m_ssd_conv_b, m_ssd_dt_bias, m_ssd_a_log, m_ssd_d, m_ssd_norm_w, m_lru_conv_w, m_lru_conv_b, m_lru_wa, m_lru_ba, m_lru_wx, m_lru_bx, m_lru_lambda, m_lru_norm_w, m_w_out, m_norm2_w, m_w_gate, m_w_up, m_w_down, m_final_norm_w, v_meta_tokens, v_norm1_w, v_w_in, v_ssd_conv_w, v_ssd_conv_b, v_ssd_dt_bias, v_ssd_a_log, v_ssd_d, v_ssd_norm_w, v_lru_conv_w, v_lru_conv_b, v_lru_wa, v_lru_ba, v_lru_wx, v_lru_bx, v_lru_lambda, v_lru_norm_w, v_w_out, v_norm2_w, v_w_gate, v_w_up, v_w_down, v_final_norm_w):
    w = dict(zip(WEIGHT_NAMES, (meta_tokens, norm1_w, w_in, ssd_conv_w, ssd_conv_b, ssd_dt_bias, ssd_a_log, ssd_d, ssd_norm_w, lru_conv_w, lru_conv_b, lru_wa, lru_ba, lru_wx, lru_bx, lru_lambda, lru_norm_w, w_out, norm2_w, w_gate, w_up, w_down, final_norm_w)))
    m = dict(zip(WEIGHT_NAMES, (m_meta_tokens, m_norm1_w, m_w_in, m_ssd_conv_w, m_ssd_conv_b, m_ssd_dt_bias, m_ssd_a_log, m_ssd_d, m_ssd_norm_w, m_lru_conv_w, m_lru_conv_b, m_lru_wa, m_lru_ba, m_lru_wx, m_lru_bx, m_lru_lambda, m_lru_norm_w, m_w_out, m_norm2_w, m_w_gate, m_w_up, m_w_down, m_final_norm_w)))
    v = dict(zip(WEIGHT_NAMES, (v_meta_tokens, v_norm1_w, v_w_in, v_ssd_conv_w, v_ssd_conv_b, v_ssd_dt_bias, v_ssd_a_log, v_ssd_d, v_ssd_norm_w, v_lru_conv_w, v_lru_conv_b, v_lru_wa, v_lru_ba, v_lru_wx, v_lru_bx, v_lru_lambda, v_lru_norm_w, v_w_out, v_norm2_w, v_w_gate, v_w_up, v_w_down, v_final_norm_w)))
    me = 2 * lax.axis_index("x") + lax.axis_index("y")

    big2d = {n: _rows_view(n, w[n]) for n in BIG}
    small_local = jnp.concatenate([w["meta_tokens"].reshape(-1), w["ssd_conv_w"].reshape(-1),
                                   w["lru_conv_w"].reshape(-1)])[None]
    me_arr = me.astype(jnp.int32).reshape(1)
    dev_arr = (2 * me + lax.axis_index("c")).astype(jnp.int32).reshape(1)
    small4 = _gather_small(small_local)
    (w_in_slot,) = _fill_own_slots([big2d["w_in"]], me_arr, name="own_slot_w_in")
    in_send, in_recv, in_bufs, in_tok = _split_start([w_in_slot], _halves_plan, 3, small4, name="gather_w_in_start")
    late_slots = _fill_own_slots([big2d[n] for n in LATE], me_arr, name="own_slots_late", behind=(in_tok,))
    sm = small4[:, 0]
    meta_full = _unshard_cols(sm[:, :4096].reshape(N_SHARDS, N_META, 256))
    ssd_conv_w_full = _unshard_cols(sm[:, 4096:5632].reshape(N_SHARDS, CONV_K, 384))
    lru_conv_w_full = _unshard_cols(sm[:, 5632:].reshape(N_SHARDS, CONV_K, 256))

    p = {"ssd_conv_w": ssd_conv_w_full, "lru_conv_w": lru_conv_w_full,
         "lru_wa": w["lru_wa"][0], "lru_wx": w["lru_wx"][0], "final_norm_w": w["final_norm_w"][None]}
    for n in ("norm1_w", "ssd_conv_b", "ssd_dt_bias", "ssd_a_log", "ssd_d", "ssd_norm_w", "lru_conv_b", "lru_ba",
              "lru_bx", "lru_lambda", "lru_norm_w", "norm2_w"):
        p[n] = w[n]

    class Late:
        def __init__(self):
            self.pending = []
            self.before_embed = (late_slots[0],)

        def w_in(self, after):
            (buf,) = _split_wait(in_bufs, in_send, in_recv, _halves_plan, after, name="gather_w_in_wait")
            send, recv, bufs, tok = _split_start([buf], _forward_plan, 3, None, name="forward_w_in_start")
            self.late_gather = _split_start(late_slots, _halves_plan, 3 * len(LATE), tok, name="gather_late_start")
            (w_in4,) = _split_wait(bufs, send, recv, _forward_plan, self.late_gather[2][0], name="forward_w_in_wait")
            sections = {s: _rows_of_shards(w_in4, lo, hi) for s, (lo, hi) in IN_ROWS.items()}
            sections["dt"] = jnp.pad(sections["dt"], ((0, SEC_WIDTH["dt"] - SSD_HEADS), (0, 0)))
            return sections

        def mid_forward(self, after):
            send, recv, bufs, _ = self.late_gather
            bufs = _split_wait(bufs, send, recv, _halves_plan, after, name="gather_late_wait")
            self.forward = _split_start(bufs, _forward_plan, 3 * len(LATE), None, name="forward_late_start")
            return self.forward[3][:1, :1]

        def w_out(self, after):
            send, recv, bufs, _ = self.forward
            bufs = _split_wait(bufs, send, recv, _forward_plan, after, name="forward_late_wait")
            self.late = dict(zip(LATE, (b.reshape(-1, D_MODEL) for b in bufs)))
            return self.late["w_out"]

        def ffn(self, after):
            return tuple(self.late[n] for n in FFN)

        def grads_ready(self, names, g, g_mxu):
            if names == ("w_in",):
                g_mxu["w_in"] = jnp.stack([_w_in_shard_rows(k, [g_mxu["w_in_" + s] for s in SEC_NAMES])
                                           for k in range(N_SHARDS)])
            srcs = [g_mxu[n].reshape(N_SHARDS, -1, D_MODEL) for n in names]
            lands = [lax.empty((3,) + s.shape[1:], _MXU) for s in srcs]
            tag = "_".join(names)
            send, recv, bufs, tok = _split_start(srcs + lands, _scatter_plan, 3 * len(names), None,
                                                 name="scatter_" + tag + "_start")
            self.pending.append((names, send, recv, bufs, tag))
            self.in_flight = bufs[0]
            return tok[:1, :1]

        def landed(self, after, which):
            land = {}
            for names, send, recv, bufs, tag in self.pending:
                if names[0] in which:
                    bufs = _split_wait(bufs, send, recv, _scatter_plan, after, name="scatter_" + tag + "_wait")
                    land.update(zip(names, bufs[len(names):]))
            return land

        def small_ready(self, g, loss):
            pack = _pack([g[n] for n in SMALL] + [loss[0, :1]], 8 * N_DEV)
            pack = pack.reshape(N_DEV, -1, PACK_COLS)
            self.small = _split_start([pack, lax.empty(pack.shape, F32)], _pieces_plan, N_DEV - 1, loss,
                                      name="small_pieces_start")
            return self.small[3]

        def small_middle(self, after):
            send, recv, bufs, _ = self.small
            pack, land = _split_wait(bufs, send, recv, _pieces_plan, after, name="small_pieces_wait")
            piece = _sum_pieces(pack, land, dev_arr, name="small_pieces_sum")
            self.small = _split_start([piece, lax.empty(pack.shape, F32)], _spread_plan, N_DEV - 1, None,
                                      name="small_spread_start")
            return self.small[3]

        def small_sum(self, after):
            send, recv, bufs, _ = self.small
            piece, land = _split_wait(bufs, send, recv, _spread_plan, after, name="small_spread_wait")
            return _join_pieces(piece, land, dev_arr, name="small_join")

    late = Late()

    loss, grad_x, g, g_mxu = _local_step(x[0], loss_target[0], meta_full, p, late)

    g4 = {n: g[n].reshape(N_SHARDS, -1, D_MODEL) for n in LATE}
    g4["w_in"] = lax.switch(me, [functools.partial(_w_in_shard_rows, k) for k in range(N_SHARDS)],
                            [g["w_in_" + s] for s in SEC_NAMES])
    land = late.landed([late.in_flight, late.small[2][0]], LATE)
    part = dict(zip(LATE, _partial_sums([g4[n] for n in LATE], [land[n] for n in LATE], me_arr,
                                        name="partial_late")))
    sib = dict(zip(LATE, _swap_with_sibling([part[n] for n in LATE], name="swap_late")))

    grad, delta, new_m, new_v = {}, {}, {}, {}
    late_outs = _adamws([big2d[n] for n in LATE], [part[n] for n in LATE], [sib[n] for n in LATE],
                        [_rows_view(n, m[n]) for n in LATE], [_rows_view(n, v[n]) for n in LATE], name="adamw_late")
    for d, outs in zip((grad, delta, new_m, new_v), late_outs):
        d.update({n: _param_view(n, o) for n, o in zip(LATE, outs)})

    land.update(late.landed(late_outs[0][0], ("w_in",)))
    spread = late.small_middle(land["w_in"])
    part_in = _partial_sum(g4["w_in"], land["w_in"], me_arr, name="partial_w_in")
    (sib_in,) = _swap_with_sibling([part_in], name="swap_w_in", behind=(spread,))
    lanes = lambda a: a[0].reshape(8, 128, -1).transpose(2, 0, 1)
    pieces = lambda a: a.reshape(-1, 8, 128)
    outs = _adamw(lanes(w["w_in"]), [pieces(part_in), pieces(sib_in)], lanes(m["w_in"]), lanes(v["w_in"]),
                  name="adamw_w_in")
    grad["w_in"], delta["w_in"], new_m["w_in"], new_v["w_in"] = (o.transpose(1, 2, 0).reshape(1, D_MODEL, -1)
                                                                 for o in outs)

    small_full_shape = {n: (SMALL_SHARDED[n] if n in SMALL_SHARDED else w[n].shape) for n in SMALL}
    red_list = _unpack(late.small_sum(outs[0]), [small_full_shape[n] for n in SMALL] + [(1,)])
    loss_total = red_list[-1][0]
    g_small = {}
    for n, arr in zip(SMALL, red_list[:-1]):
        if n in SMALL_SHARDED:
            cols = SMALL_SHARDED[n][1] // N_SHARDS
            arr = lax.dynamic_slice_in_dim(arr, me * cols, cols, axis=1)
        g_small[n] = arr.reshape(w[n].shape)
    two_d = lambda a: a.reshape(1, -1) if a.ndim == 1 else a
    deltas, new_ms, new_vs = _adamw_native(*[[two_d(d[n]) for n in SMALL] for d in (w, g_small, m, v)])
    for n, dn, mn, vn in zip(SMALL, deltas, new_ms, new_vs):
        grad[n], delta[n], new_m[n], new_v[n] = (g_small[n], dn.reshape(w[n].shape), mn.reshape(w[n].shape),
                                                 vn.reshape(w[n].shape))

    return (loss_total, grad_x[None], *[grad[n] for n in WEIGHT_NAMES], *[delta[n] for n in WEIGHT_NAMES],
            *[new_m[n] for n in WEIGHT_NAMES], *[new_v[n] for n in WEIGHT_NAMES])
```

```python
import functools
import math

import jax
import jax.numpy as jnp
from jax import lax
from jax.experimental import pallas as pl
from jax.experimental.pallas import tpu as pltpu

F32 = jnp.float32
_MXU = jnp.bfloat16

D_MODEL = 1024
SEQ = 2048
N_META = 16
CHUNK = 128
T_ROWS = 2176
N_CHUNKS = T_ROWS // CHUNK
PAD_ROWS = T_ROWS - SEQ - N_META
X_ROW0 = PAD_ROWS + N_META
SSD_HEADS = 16
SSD_HEAD_DIM = 64
SSD_STATE = 128
SSD_GROUPS = 2
SSD_HPG = SSD_HEADS // SSD_GROUPS
SSD_WIDTH = 1024
LRU_WIDTH = 1024
LRU_C = 8.0
D_FF = 2816
EPS = 1e-6
IN_COLS = 4624
N_SHARDS = 4
N_DEV = 8

ADAM_LR = 0.001
ADAM_B1 = 0.9
ADAM_B2 = 0.999
ADAM_EPS = 1e-08
ADAM_WD = 0.01
ADAM_STEP = 10

VMEM_LIMIT_BYTES = 56 * 1024 * 1024

NN = (((1,), (0,)), ((), ()))
NT = (((1,), (1,)), ((), ()))
TN = (((0,), (0,)), ((), ()))


def _cparams(*sem):
    return pltpu.CompilerParams(dimension_semantics=sem, vmem_limit_bytes=VMEM_LIMIT_BYTES)


def _dot(a, b, dims=NN):
    return lax.dot_general(a.astype(_MXU), b.astype(_MXU), dims, preferred_element_type=F32)


def _dot_onehot(a, b, dims=NN, *, data=0, pieces=3):
    ops = [a, b]
    mask = ops[1 - data].astype(jnp.bfloat16)
    rest = ops[data]
    acc = None
    for _ in range(pieces):
        piece = rest.astype(jnp.bfloat16)
        ops[data], ops[1 - data] = piece, mask
        d = lax.dot_general(ops[0], ops[1], dims, preferred_element_type=F32)
        acc = d if acc is None else acc + d
        rest = rest - piece.astype(F32)
    return acc


def _sigmoid(x):
    return 0.5 * (1.0 + jnp.tanh(0.5 * x))


def _softplus(x):
    return jnp.maximum(x, 0.0) + jnp.log(1.0 + jnp.exp(-jnp.abs(x)))


def _silu(x):
    return x * _sigmoid(x)


def _silu_grad(x):
    s = _sigmoid(x)
    return s * (1.0 + x * (1.0 - s))


_GELU_C = math.sqrt(2.0 / math.pi)


def _gelu_and_grad(x):
    inner = _GELU_C * (x + 0.044715 * x * x * x)
    t = jnp.tanh(inner)
    g = 0.5 * x * (1.0 + t)
    dg = 0.5 * (1.0 + t) + 0.5 * x * (1.0 - t * t) * _GELU_C * (1.0 + 3.0 * 0.044715 * x * x)
    return g, dg


def _rms_fwd(x, w):
    rstd = lax.rsqrt(jnp.mean(x * x, axis=-1, keepdims=True) + EPS)
    return x * rstd * w


def _rms_bwd(x, w, dy):
    rstd = lax.rsqrt(jnp.mean(x * x, axis=-1, keepdims=True) + EPS)
    xhat = x * rstd
    dxhat = dy * w
    dx = rstd * (dxhat - xhat * jnp.mean(dxhat * xhat, axis=-1, keepdims=True))
    return dx, dy * xhat


def _mm(terms, m, n, *, tm, tn, mode, out_dtype, name, residual=None, n_outer=False, also_mxu=False, behind=()):
    gm, gn = m // tm, n // tn
    assert gm * tm == m and gn * tn == n
    if n_outer:
        grid = (gn, gm)
        mi = lambda g0, g1: g1
        ni = lambda g0, g1: g0
    else:
        grid = (gm, gn)
        mi = lambda g0, g1: g0
        ni = lambda g0, g1: g1
    in_specs, args = [], []
    for (a, ka, b, kb, k) in terms:
        if mode == "tn":
            in_specs.append(pl.BlockSpec((k, tm), lambda g0, g1, ka=ka: (ka, mi(g0, g1))))
        else:
            in_specs.append(pl.BlockSpec((tm, k), lambda g0, g1, ka=ka: (mi(g0, g1), ka)))
        if mode == "nt":
            in_specs.append(pl.BlockSpec((tn, k), lambda g0, g1, kb=kb: (ni(g0, g1), kb)))
        else:
            in_specs.append(pl.BlockSpec((k, tn), lambda g0, g1, kb=kb: (kb, ni(g0, g1))))
        args += [a, b]
    if residual is not None:
        in_specs.append(pl.BlockSpec((tm, tn), lambda g0, g1: (mi(g0, g1), ni(g0, g1))))
        args.append(residual)
    dims = {"nn": NN, "nt": NT, "tn": TN}[mode]
    n_terms = len(terms)
    has_res = residual is not None
    in_specs += [pl.BlockSpec(memory_space=pl.ANY)] * len(behind)
    args += list(behind)
    n_in = len(args)

    def body(*refs):
        acc = None
        for t in range(n_terms):
            d = lax.dot_general(refs[2 * t][...], refs[2 * t + 1][...], dims, preferred_element_type=F32)
            acc = d if acc is None else acc + d
        if has_res:
            acc = acc + refs[2 * n_terms][...]
        refs[n_in][...] = acc.astype(out_dtype)
        if also_mxu:
            refs[n_in + 1][...] = acc.astype(_MXU)

    tile = pl.BlockSpec((tm, tn), lambda g0, g1: (mi(g0, g1), ni(g0, g1)))
    shape = jax.ShapeDtypeStruct((m, n), out_dtype)
    return pl.pallas_call(
        body, name=name, grid=grid, in_specs=in_specs,
        out_specs=[tile, tile] if also_mxu else tile,
        out_shape=[shape, jax.ShapeDtypeStruct((m, n), _MXU)] if also_mxu else shape,
        compiler_params=_cparams("parallel", "parallel"),
    )(*args)


def _embed(x, meta, behind=()):
    def body(x_ref, meta_ref, *rest):
        o_ref = rest[-1]
        i = pl.program_id(0)

        @pl.when(i == 0)
        def _():
            o_ref[0:PAD_ROWS, :] = jnp.zeros((PAD_ROWS, D_MODEL), F32)
            o_ref[PAD_ROWS:CHUNK, :] = meta_ref[...]

        @pl.when(i > 0)
        def _():
            o_ref[...] = x_ref[...]

    return pl.pallas_call(
        body, name="embed", grid=(N_CHUNKS,),
        in_specs=[pl.BlockSpec((CHUNK, D_MODEL), lambda i: (jnp.maximum(i - 1, 0), 0)),
                  pl.BlockSpec((N_META, D_MODEL), lambda i: (0, 0))] + [pl.BlockSpec(memory_space=pl.ANY)] * len(behind),
        out_specs=pl.BlockSpec((CHUNK, D_MODEL), lambda i: (i, 0)),
        out_shape=jax.ShapeDtypeStruct((T_ROWS, D_MODEL), F32),
        compiler_params=_cparams("parallel"),
    )(x, meta, *behind)


def _rmsnorm(h, w, *, name, tm=544):
    def body(h_ref, w_ref, o_ref):
        o_ref[...] = _rms_fwd(h_ref[...], w_ref[...]).astype(_MXU)

    return pl.pallas_call(
        body, name=name, grid=(T_ROWS // tm,),
        in_specs=[pl.BlockSpec((tm, D_MODEL), lambda i: (i, 0)), pl.BlockSpec((1, D_MODEL), lambda i: (0, 0))],
        out_specs=pl.BlockSpec((tm, D_MODEL), lambda i: (i, 0)),
        out_shape=jax.ShapeDtypeStruct((T_ROWS, D_MODEL), _MXU),
        compiler_params=_cparams("parallel"),
    )(h, w)


def _norm_proj(h, w, sections, convs, *, name, tm=272):
    widths = [s.shape[0] for s in sections]
    n = len(sections)
    conv_ks = sorted(convs)
    nc = len(conv_ks)

    def body(*refs):
        h_ref, w_ref = refs[:2]
        sec_refs = refs[2:2 + n]
        cw_refs = refs[2 + n:2 + n + nc]
        cb_refs = refs[2 + n + nc:2 + n + 2 * nc]
        u_ref = refs[2 + n + 2 * nc]
        proj_refs = refs[3 + n + 2 * nc:3 + 2 * n + 2 * nc]
        act_refs = refs[3 + 2 * n + 2 * nc:3 + 2 * n + 3 * nc]
        halo_refs = refs[3 + 2 * n + 3 * nc:]
        i = pl.program_id(0)

        @pl.when(i == 0)
        def _():
            for hr in halo_refs:
                hr[...] = jnp.zeros_like(hr)

        u = _rms_fwd(h_ref[...], w_ref[...]).astype(_MXU)
        u_ref[...] = u
        real = (i * tm + lax.broadcasted_iota(jnp.int32, (tm, 1), 0) >= PAD_ROWS).astype(F32)
        for k in range(n):
            raw = lax.dot_general(u, sec_refs[k][...], NT, preferred_element_type=F32)
            proj_refs[k][...] = raw
            if k not in convs:
                continue
            q = conv_ks.index(k)
            wv, bv = cw_refs[q][...], cb_refs[q][...]
            cat = jnp.concatenate([halo_refs[q][...], raw], axis=0)
            pre = bv + raw * wv[CONV_K - 1:CONV_K]
            for s in range(1, CONV_K):
                pre = pre + pltpu.roll(cat, s, 0)[8:8 + tm] * wv[CONV_K - 1 - s:CONV_K - s]
            y = _silu(pre) if convs[k][2] else pre
            act_refs[q][...] = y * real
            halo_refs[q][...] = raw[tm - 8:tm]

    row = lambda width: pl.BlockSpec((tm, width), lambda i: (i, 0))
    whole = lambda a: pl.BlockSpec(a.shape, lambda i: (0, 0))
    cws = [convs[k][0] for k in conv_ks]
    cbs = [convs[k][1] for k in conv_ks]
    outs = pl.pallas_call(
        body, name=name, grid=(T_ROWS // tm,),
        in_specs=[row(D_MODEL), pl.BlockSpec((1, D_MODEL), lambda i: (0, 0))]
        + [pl.BlockSpec((wd, D_MODEL), lambda i: (0, 0)) for wd in widths]
        + [whole(a) for a in cws] + [whole(a) for a in cbs],
        out_specs=[row(D_MODEL)] + [row(wd) for wd in widths] + [row(widths[k]) for k in conv_ks],
        out_shape=[jax.ShapeDtypeStruct((T_ROWS, D_MODEL), _MXU)]
        + [jax.ShapeDtypeStruct((T_ROWS, wd), F32) for wd in widths]
        + [jax.ShapeDtypeStruct((T_ROWS, widths[k]), F32) for k in conv_ks],
        scratch_shapes=[pltpu.VMEM((8, widths[k]), F32) for k in conv_ks],
        compiler_params=_cparams("arbitrary"),
    )(h, w, *sections, *cws, *cbs)
    return outs[0], list(outs[1:1 + n]), dict(zip(conv_ks, outs[1 + n:]))


def _loss_head(h2, target, fw):
    def body(h_ref, t_ref, w_ref, loss_ref, dh_ref, dhb_ref, dw_ref, acc_ref):
        i = pl.program_id(0)

        @pl.when(i == 0)
        def _():
            acc_ref[...] = jnp.zeros_like(acc_ref)
            dw_ref[...] = jnp.zeros_like(dw_ref)

        h = h_ref[...]
        w = w_ref[...]
        y = _rms_fwd(h, w)
        live = (i > 0).astype(F32)
        err = (y - t_ref[...]) * live
        acc_ref[...] += jnp.sum(err * err, axis=0, keepdims=True)
        dy = err * (1.0 / D_MODEL)
        dx, dwr = _rms_bwd(h, w, dy)
        dh_ref[...] = dx
        dhb_ref[...] = dx.astype(_MXU)
        dw_ref[...] += jnp.sum(dwr, axis=0, keepdims=True)

        @pl.when(i == N_CHUNKS - 1)
        def _():
            tot = jnp.sum(acc_ref[...], axis=1, keepdims=True) * (0.5 / D_MODEL)
            loss_ref[...] = jnp.broadcast_to(tot, (1, 128))

    return pl.pallas_call(
        body, name="loss_head", grid=(N_CHUNKS,),
        in_specs=[pl.BlockSpec((CHUNK, D_MODEL), lambda i: (i, 0)),
                  pl.BlockSpec((CHUNK, D_MODEL), lambda i: (jnp.maximum(i - 1, 0), 0)),
                  pl.BlockSpec((1, D_MODEL), lambda i: (0, 0))],
        out_specs=[pl.BlockSpec((1, 128), lambda i: (0, 0)),
                   pl.BlockSpec((CHUNK, D_MODEL), lambda i: (i, 0)),
                   pl.BlockSpec((CHUNK, D_MODEL), lambda i: (i, 0)),
                   pl.BlockSpec((1, D_MODEL), lambda i: (0, 0))],
        out_shape=[jax.ShapeDtypeStruct((1, 128), F32),
                   jax.ShapeDtypeStruct((T_ROWS, D_MODEL), F32),
                   jax.ShapeDtypeStruct((T_ROWS, D_MODEL), _MXU),
                   jax.ShapeDtypeStruct((1, D_MODEL), F32)],
        scratch_shapes=[pltpu.VMEM((1, D_MODEL), F32)],
        compiler_params=_cparams("arbitrary"),
    )(h2, target, fw)


def _mm_norm_bwd(terms, h, w, dres, *, name, tm=272, behind=()):
    n_terms = len(terms)
    in_specs, args = [], []
    for (a, b, k) in terms:
        in_specs += [pl.BlockSpec((tm, k), lambda i: (i, 0)), pl.BlockSpec((k, D_MODEL), lambda i: (0, 0))]
        args += [a, b]
    in_specs += [pl.BlockSpec((tm, D_MODEL), lambda i: (i, 0)), pl.BlockSpec((1, D_MODEL), lambda i: (0, 0)),
                 pl.BlockSpec((tm, D_MODEL), lambda i: (i, 0))] + [pl.BlockSpec(memory_space=pl.ANY)] * len(behind)
    args += [h, w, dres, *behind]

    def body(*refs):
        h_ref, w_ref, dres_ref = refs[2 * n_terms:2 * n_terms + 3]
        dh_ref, dhb_ref, dw_ref = refs[2 * n_terms + 3 + len(behind):]

        @pl.when(pl.program_id(0) == 0)
        def _():
            dw_ref[...] = jnp.zeros_like(dw_ref)

        du = None
        for t in range(n_terms):
            d = lax.dot_general(refs[2 * t][...], refs[2 * t + 1][...], NN, preferred_element_type=F32)
            du = d if du is None else du + d
        dx, dwr = _rms_bwd(h_ref[...], w_ref[...], du)
        dh = dres_ref[...] + dx
        dh_ref[...] = dh
        dhb_ref[...] = dh.astype(_MXU)
        dw_ref[...] += jnp.sum(dwr, axis=0, keepdims=True)

    return pl.pallas_call(
        body, name=name, grid=(T_ROWS // tm,), in_specs=in_specs,
        out_specs=[pl.BlockSpec((tm, D_MODEL), lambda i: (i, 0)), pl.BlockSpec((tm, D_MODEL), lambda i: (i, 0)),
                   pl.BlockSpec((1, D_MODEL), lambda i: (0, 0))],
        out_shape=[jax.ShapeDtypeStruct((T_ROWS, D_MODEL), F32), jax.ShapeDtypeStruct((T_ROWS, D_MODEL), _MXU),
                   jax.ShapeDtypeStruct((1, D_MODEL), F32)],
        compiler_params=_cparams("arbitrary"),
    )(*args)


FFN_TM = T_ROWS
FFN_TN = 256


def _ffn_up(u2, wg_t, wu_t):
    def body(u_ref, wg_ref, wu_ref, gp_ref, up_ref, act_ref):
        u = u_ref[...]
        gp = lax.dot_general(u, wg_ref[...], NT, preferred_element_type=F32)
        up = lax.dot_general(u, wu_ref[...], NT, preferred_element_type=F32)
        gp_ref[...] = gp.astype(_MXU)
        up_ref[...] = up.astype(_MXU)
        act_ref[...] = (_silu(gp) * up).astype(_MXU)

    tile = pl.BlockSpec((FFN_TM, FFN_TN), lambda j, i: (i, j))
    return pl.pallas_call(
        body, name="ffn_up", grid=(D_FF // FFN_TN, T_ROWS // FFN_TM),
        in_specs=[pl.BlockSpec((FFN_TM, D_MODEL), lambda j, i: (i, 0)),
                  pl.BlockSpec((FFN_TN, D_MODEL), lambda j, i: (j, 0)),
                  pl.BlockSpec((FFN_TN, D_MODEL), lambda j, i: (j, 0))],
        out_specs=[tile, tile, tile],
        out_shape=[jax.ShapeDtypeStruct((T_ROWS, D_FF), _MXU)] * 3,
        compiler_params=_cparams("parallel", "parallel"),
    )(u2, wg_t, wu_t)


def _ffn_bwd_act(dh2b, wd, gp, up):
    def body(dh_ref, wd_ref, gp_ref, up_ref, dgp_ref, dup_ref):
        dact = lax.dot_general(dh_ref[...], wd_ref[...], NT, preferred_element_type=F32)
        gp = gp_ref[...].astype(F32)
        dgp_ref[...] = (dact * up_ref[...].astype(F32) * _silu_grad(gp)).astype(_MXU)
        dup_ref[...] = (dact * _silu(gp)).astype(_MXU)

    tile = pl.BlockSpec((FFN_TM, FFN_TN), lambda j, i: (i, j))
    return pl.pallas_call(
        body, name="ffn_bwd_act", grid=(D_FF // FFN_TN, T_ROWS // FFN_TM),
        in_specs=[pl.BlockSpec((FFN_TM, D_MODEL), lambda j, i: (i, 0)),
                  pl.BlockSpec((FFN_TN, D_MODEL), lambda j, i: (j, 0)), tile, tile],
        out_specs=[tile, tile],
        out_shape=[jax.ShapeDtypeStruct((T_ROWS, D_FF), _MXU), jax.ShapeDtypeStruct((T_ROWS, D_FF), _MXU)],
        compiler_params=_cparams("parallel", "parallel"),
    )(dh2b, wd, gp, up)


CONV_TC = 512
CONV_K = 4


def _conv_pre(x_ref, wv, bv, c):
    tc = wv.shape[1]
    r0 = c * CHUNK
    cur = x_ref[r0:r0 + CHUNK, :]
    if c == 0:
        cat = jnp.concatenate([jnp.zeros((8, tc), F32), cur], axis=0)
        shifted = [cur] + [pltpu.roll(cat, s, 0)[8:8 + CHUNK] for s in range(1, CONV_K)]
    else:
        shifted = [cur] + [x_ref[r0 - s:r0 - s + CHUNK, :] for s in range(1, CONV_K)]
    pre = bv
    for s in range(CONV_K):
        pre = pre + shifted[s] * wv[CONV_K - 1 - s:CONV_K - s]
    return pre, shifted


def _row_mask(c):
    if c > 0:
        return None
    return (lax.broadcasted_iota(jnp.int32, (CHUNK, 1), 0) >= PAD_ROWS).astype(F32)


def _conv_bwd(dy, x, w, b, *, silu, name):
    cols = x.shape[1]
    tc = min(CONV_TC, cols)

    def body(dy_ref, x_ref, w_ref, b_ref, dx_ref, dw_ref, db_ref):
        wv, bv = w_ref[...], b_ref[...]
        next8 = jnp.zeros((8, tc), F32)
        dws = [jnp.zeros((1, tc), F32) for _ in range(CONV_K)]
        db = jnp.zeros((1, tc), F32)
        for c in reversed(range(N_CHUNKS)):
            r0 = c * CHUNK
            pre, shifted = _conv_pre(x_ref, wv, bv, c)
            dpre = dy_ref[r0:r0 + CHUNK, :]
            if silu:
                dpre = dpre * _silu_grad(pre)
            mask = _row_mask(c)
            if mask is not None:
                dpre = dpre * mask
            cat = jnp.concatenate([dpre, next8], axis=0)
            dx = dpre * wv[CONV_K - 1:CONV_K]
            for s in range(1, CONV_K):
                dx = dx + pltpu.roll(cat, CHUNK + 8 - s, 0)[0:CHUNK] * wv[CONV_K - 1 - s:CONV_K - s]
            dx_ref[r0:r0 + CHUNK, :] = dx.astype(_MXU)
            for s in range(CONV_K):
                k = CONV_K - 1 - s
                dws[k] = dws[k] + jnp.sum(dpre * shifted[s], axis=0, keepdims=True)
            db = db + jnp.sum(dpre, axis=0, keepdims=True)
            next8 = dpre[0:8]
        dw_ref[...] = jnp.concatenate(dws, axis=0)
        db_ref[...] = db

    return pl.pallas_call(
        body, name=name, grid=(cols // tc,),
        in_specs=[pl.BlockSpec((T_ROWS, tc), lambda j: (0, j)), pl.BlockSpec((T_ROWS, tc), lambda j: (0, j)),
                  pl.BlockSpec((CONV_K, tc), lambda j: (0, j)), pl.BlockSpec((1, tc), lambda j: (0, j))],
        out_specs=[pl.BlockSpec((T_ROWS, tc), lambda j: (0, j)), pl.BlockSpec((CONV_K, tc), lambda j: (0, j)),
                   pl.BlockSpec((1, tc), lambda j: (0, j))],
        out_shape=[jax.ShapeDtypeStruct((T_ROWS, cols), _MXU), jax.ShapeDtypeStruct((CONV_K, cols), F32),
                   jax.ShapeDtypeStruct((1, cols), F32)],
        compiler_params=_cparams("parallel"),
    )(dy, x, w, b)


def _ssd_chunk_common(dt_raw, prm, c):
    a_row = -jnp.exp(prm[1:2])
    dt = _softplus(dt_raw + prm[0:1])
    rows = lax.broadcasted_iota(jnp.int32, (CHUNK, 1), 0)
    real = jnp.logical_or(c > 0, rows >= PAD_ROWS)
    dt = jnp.where(real, dt, 0.0)
    li = lax.broadcasted_iota(jnp.int32, (CHUNK, CHUNK), 0)
    si = lax.broadcasted_iota(jnp.int32, (CHUNK, CHUNK), 1)
    causal = li >= si
    tri = causal.astype(F32)
    cs = _dot_onehot(tri, dt * a_row, data=1)
    return dt, a_row, cs, cs.T, causal, tri, real


def _gated_norm_fwd(y, z, w):
    g = y * _silu(z)
    half = SSD_WIDTH // SSD_GROUPS
    outs = [_rms_fwd(g[:, k * half:(k + 1) * half], w[:, k * half:(k + 1) * half]) for k in range(SSD_GROUPS)]
    return jnp.concatenate(outs, axis=1)


GROUP_W = SSD_WIDTH // SSD_GROUPS
PAIR_W = 2 * SSD_HEAD_DIM
STATE_SHAPE = (SSD_GROUPS, SSD_STATE, GROUP_W)


def _head_expander():
    r = lax.broadcasted_iota(jnp.int32, (128, SSD_WIDTH), 0)
    c = lax.broadcasted_iota(jnp.int32, (128, SSD_WIDTH), 1)
    return (c // SSD_HEAD_DIM == r).astype(F32)


def _ssd_expand(dt, cs, prm, ex):
    cs_x = _dot_onehot(cs, ex)
    cs_last_x = cs_x[CHUNK - 1:CHUNK, :]
    return (_dot_onehot(dt, ex, pieces=2), _dot_onehot(prm, ex)[2:3], jnp.exp(cs_x), jnp.exp(cs_last_x),
            jnp.exp(cs_last_x - cs_x))


def _ssd_fwd(xs, bc, dt_raw, z, prm, norm_w, ex):
    def body(xs_ref, bc_ref, dt_ref, z_ref, prm_ref, nw_ref, ex_ref, y_ref, yn_ref, prev_ref, state):
        c = pl.program_id(0)

        @pl.when(c == 0)
        def _():
            state[...] = jnp.zeros_like(state)

        prm = prm_ref[...]
        dt, a_row, cs, cs_t, causal, _, _ = _ssd_chunk_common(dt_ref[...], prm, c)
        dt_x, d_x, e_cs_x, e_last_x, dec_x = _ssd_expand(dt, cs, prm, ex_ref[...])
        xs_all = xs_ref[...]
        bc_all = bc_ref[...]
        xdt = xs_all * dt_x
        xdec = xdt * dec_x
        lane_lo = lax.broadcasted_iota(jnp.int32, (1, PAIR_W), 1) < SSD_HEAD_DIM
        for g in range(SSD_GROUPS):
            gs = slice(g * GROUP_W, (g + 1) * GROUP_W)
            b_g = bc_all[:, g * SSD_STATE:(g + 1) * SSD_STATE]
            c_g = bc_all[:, (SSD_GROUPS + g) * SSD_STATE:(SSD_GROUPS + g + 1) * SSD_STATE]
            st = state[g]
            prev_ref[0, g] = st
            y_off = _dot(c_g, st) * e_cs_x[:, gs]
            state[g] = st * e_last_x[:, gs] + _dot(b_g.T, xdec[:, gs])
            cb = _dot(c_g, b_g, NT)
            for k in range(SSD_HPG // 2):
                h0 = g * SSD_HPG + 2 * k
                ps = slice(h0 * SSD_HEAD_DIM, h0 * SSD_HEAD_DIM + PAIR_W)
                xdt_pair = xdt[:, ps]
                yd = []
                for h in (h0, h0 + 1):
                    lmat = jnp.where(causal, jnp.exp(cs[:, h:h + 1] - cs_t[h:h + 1, :]), 0.0)
                    yd.append(_dot(cb * lmat, xdt_pair))
                y_ref[:, ps] = (jnp.where(lane_lo, yd[0], yd[1]) + y_off[:, k * PAIR_W:(k + 1) * PAIR_W]
                                + xs_all[:, ps] * d_x[:, ps])
        yn_ref[...] = _gated_norm_fwd(y_ref[...], z_ref[...], nw_ref[...]).astype(_MXU)

    row = lambda w: pl.BlockSpec((CHUNK, w), lambda c: (c, 0))
    return pl.pallas_call(
        body, name="ssd_fwd", grid=(N_CHUNKS,),
        in_specs=[row(SSD_WIDTH), row(512), row(128), row(SSD_WIDTH),
                  pl.BlockSpec((8, 128), lambda c: (0, 0)), pl.BlockSpec((1, SSD_WIDTH), lambda c: (0, 0)),
                  pl.BlockSpec((128, SSD_WIDTH), lambda c: (0, 0))],
        out_specs=[row(SSD_WIDTH), row(SSD_WIDTH),
                   pl.BlockSpec((1,) + STATE_SHAPE, lambda c: (c, 0, 0, 0))],
        out_shape=[jax.ShapeDtypeStruct((T_ROWS, SSD_WIDTH), F32), jax.ShapeDtypeStruct((T_ROWS, SSD_WIDTH), _MXU),
                   jax.ShapeDtypeStruct((N_CHUNKS,) + STATE_SHAPE, F32)],
        scratch_shapes=[pltpu.VMEM(STATE_SHAPE, F32)],
        compiler_params=_cparams("arbitrary"),
    )(xs, bc, dt_raw, z, prm, norm_w, ex)


def _ssd_bwd(dyn, dyn_block, z, y_pre, xs, bc, dt_raw, prev, prm, norm_w, ex):
    def body(dyn_ref, z_ref, y_ref, xs_ref, bc_ref, dt_ref, prev_ref, prm_ref, nw_ref, ex_ref,
             dz_ref, dxs_ref, dbc_ref, ddt_ref, dprm_ref, dnw_ref, dstate):
        step = pl.program_id(0)
        c = N_CHUNKS - 1 - step

        @pl.when(step == 0)
        def _():
            dstate[...] = jnp.zeros_like(dstate)
            dprm_ref[...] = jnp.zeros_like(dprm_ref)
            dnw_ref[...] = jnp.zeros_like(dnw_ref)

        prm = prm_ref[...]
        dt, a_row, cs, cs_t, causal, tri, real = _ssd_chunk_common(dt_ref[...], prm, c)
        realf = real.astype(F32)
        z = z_ref[...]
        y_all = y_ref[...]
        nw = nw_ref[...]
        dyn_all = dyn_ref[...]
        sz = _silu(z)
        gated = y_all * sz
        half = SSD_WIDTH // SSD_GROUPS
        dgs, dnws = [], []
        for k in range(SSD_GROUPS):
            sl = slice(k * half, (k + 1) * half)
            dgk, dwk = _rms_bwd(gated[:, sl], nw[:, sl], dyn_all[:, sl])
            dgs.append(dgk)
            dnws.append(jnp.sum(dwk, axis=0, keepdims=True))
        dgated = jnp.concatenate(dgs, axis=1)
        dnw_ref[...] += jnp.concatenate(dnws, axis=1)
        dz_ref[...] = (dgated * y_all * _silu_grad(z)).astype(_MXU)
        dy_all = dgated * sz

        ex = ex_ref[...]
        dt_x, d_x, e_cs_x, e_last_x, dec_x = _ssd_expand(dt, cs, prm, ex)
        xs_all = xs_ref[...]
        bc_all = bc_ref[...]
        xdt = xs_all * dt_x
        xdt_mxu = xdt.astype(_MXU).astype(F32)
        xdec = xdt * dec_x
        dcp = dy_all * e_cs_x
        lane_lo = lax.broadcasted_iota(jnp.int32, (1, PAIR_W), 1) < SSD_HEAD_DIM
        upper = (lax.broadcasted_iota(jnp.int32, (CHUNK, CHUNK), 0)
                 <= lax.broadcasted_iota(jnp.int32, (CHUNK, CHUNK), 1))
        last_row = (lax.broadcasted_iota(jnp.int32, (CHUNK, 1), 0) == CHUNK - 1).astype(F32)
        dbs, dcs_, dxdt_parts, last_parts = [], [], [], []
        for g in range(SSD_GROUPS):
            gs = slice(g * GROUP_W, (g + 1) * GROUP_W)
            b_g = bc_all[:, g * SSD_STATE:(g + 1) * SSD_STATE]
            c_g = bc_all[:, (SSD_GROUPS + g) * SSD_STATE:(SSD_GROUPS + g + 1) * SSD_STATE]
            prev_t = prev_ref[0, g]
            dst = dstate[g]
            dc_g = _dot(dcp[:, gs], prev_t, NT)
            db_g = _dot(xdec[:, gs], dst, NT)
            dxdt_state = _dot(b_g, dst) * dec_x[:, gs]
            dstate[g] = dst * e_last_x[:, gs] + _dot(c_g.T, dcp[:, gs])
            last_parts.append(jnp.sum(xdt_mxu[:, gs] * dxdt_state, axis=0, keepdims=True)
                              + jnp.sum(dst * prev_t, axis=0, keepdims=True) * e_last_x[:, gs])
            cb_t = _dot(b_g, c_g, NT)
            dcb_t = jnp.zeros((CHUNK, CHUNK), F32)
            for k in range(SSD_HPG // 2):
                h0 = g * SSD_HPG + 2 * k
                ps = slice(h0 * SSD_HEAD_DIM, h0 * SSD_HEAD_DIM + PAIR_W)
                dy_pair = dy_all[:, ps]
                xdt_pair = xdt[:, ps]
                dd = []
                for h in (h0, h0 + 1):
                    lmat_t = jnp.where(upper, jnp.exp(cs_t[h:h + 1, :] - cs[:, h:h + 1]), 0.0)
                    dd.append(_dot(cb_t * lmat_t, dy_pair))
                    mine = lane_lo if h == h0 else jnp.logical_not(lane_lo)
                    dcb_t = dcb_t + _dot(jnp.where(mine, xdt_pair, 0.0), dy_pair, NT) * lmat_t
                dxdt_parts.append(jnp.where(lane_lo, dd[0], dd[1]) + dxdt_state[:, k * PAIR_W:(k + 1) * PAIR_W])
            dc_g = dc_g + _dot(dcb_t, b_g, TN)
            db_g = db_g + _dot(dcb_t, c_g)
            dbs.append(db_g * realf)
            dcs_.append(dc_g * realf)
        dbc_ref[...] = jnp.concatenate(dbs + dcs_, axis=1)
        dxdt = jnp.concatenate(dxdt_parts, axis=1)
        dxs_ref[...] = (dxdt * dt_x + dy_all * d_x) * realf
        ddt_all = _dot_onehot(dxdt * xs_all, ex, NT, pieces=2)
        rows = jnp.concatenate([jnp.concatenate(last_parts, axis=1), jnp.sum(dy_all * xs_all, axis=0, keepdims=True),
                                jnp.zeros((6, SSD_WIDTH), F32)], axis=0)
        rows = _dot_onehot(rows, ex, NT, pieces=2)
        dd_row = rows[1:2]
        dy_mxu = dy_all.astype(_MXU).astype(F32)
        dcs_all = (_dot_onehot(dy_mxu * (y_all - xs_all * d_x), ex, NT) - _dot_onehot(xdt_mxu * dxdt, ex, NT)
                   + last_row * rows[0:1])
        dda = _dot_onehot(tri, dcs_all, TN, data=1)
        ddt = (ddt_all + dda * a_row) * realf
        ddt_raw = ddt * _sigmoid(dt_ref[...] + prm[0:1])
        ddt_ref[...] = ddt_raw.astype(_MXU)
        da_log = jnp.sum(dda * dt, axis=0, keepdims=True) * a_row
        dprm_ref[0:1, :] += jnp.sum(ddt_raw, axis=0, keepdims=True)
        dprm_ref[1:2, :] += da_log
        dprm_ref[2:3, :] += dd_row

    rev = lambda w, blk=0: pl.BlockSpec((CHUNK, w), lambda s, blk=blk: (N_CHUNKS - 1 - s, blk))
    return pl.pallas_call(
        body, name="ssd_bwd", grid=(N_CHUNKS,),
        in_specs=[rev(SSD_WIDTH, dyn_block), rev(SSD_WIDTH), rev(SSD_WIDTH), rev(SSD_WIDTH), rev(512), rev(128),
                  pl.BlockSpec((1,) + STATE_SHAPE, lambda s: (N_CHUNKS - 1 - s, 0, 0, 0)),
                  pl.BlockSpec((8, 128), lambda s: (0, 0)), pl.BlockSpec((1, SSD_WIDTH), lambda s: (0, 0)),
                  pl.BlockSpec((128, SSD_WIDTH), lambda s: (0, 0))],
        out_specs=[rev(SSD_WIDTH), rev(SSD_WIDTH), rev(512), rev(128),
                   pl.BlockSpec((8, 128), lambda s: (0, 0)), pl.BlockSpec((1, SSD_WIDTH), lambda s: (0, 0))],
        out_shape=[jax.ShapeDtypeStruct((T_ROWS, SSD_WIDTH), _MXU), jax.ShapeDtypeStruct((T_ROWS, SSD_WIDTH), F32),
                   jax.ShapeDtypeStruct((T_ROWS, 512), F32), jax.ShapeDtypeStruct((T_ROWS, 128), _MXU),
                   jax.ShapeDtypeStruct((8, 128), F32), jax.ShapeDtypeStruct((1, SSD_WIDTH), F32)],
        scratch_shapes=[pltpu.VMEM(STATE_SHAPE, F32)],
        compiler_params=_cparams("arbitrary"),
    )(dyn, z, y_pre, xs, bc, dt_raw, prev, prm, norm_w, ex)


LRU_PAIRS = 8


def _lru_gates(xr, wa_ref, wx_ref, prm):
    pre_r, pre_i = [], []
    for k in range(LRU_PAIRS):
        xk = xr[:, k * 128:(k + 1) * 128]
        pre_r.append(_dot(xk, wa_ref[k]))
        pre_i.append(_dot(xk, wx_ref[k]))
    r = _sigmoid(jnp.concatenate(pre_r, axis=1) + prm[0:1])
    i = _sigmoid(jnp.concatenate(pre_i, axis=1) + prm[1:2])
    sp = _softplus(-prm[2:3])
    log_a = (-LRU_C) * r * sp
    a = jnp.exp(log_a)
    s = jnp.sqrt(-jnp.tanh(log_a) * (a * a + 1.0))
    return r, i, a, s, sp


def _lru_fwd(xr, gate, wa, wx, prm):
    def body(xr_ref, g_ref, wa_ref, wx_ref, prm_ref, hs_ref, yn_ref, carry, a_s, u_s):
        @pl.when(pl.program_id(0) == 0)
        def _():
            carry[...] = jnp.zeros_like(carry)

        prm = prm_ref[...]
        xr_t = xr_ref[...]
        _, i, a, s, _ = _lru_gates(xr_t, wa_ref, wx_ref, prm)
        a_s[...] = a
        u_s[...] = s * (i * xr_t)
        rid = lax.broadcasted_iota(jnp.int32, (8, LRU_WIDTH), 0)

        def group(k, before):
            off = pl.multiple_of(k * 8, 8)
            a8 = a_s[pl.ds(off, 8), :]
            u8 = u_s[pl.ds(off, 8), :]
            for d in (1, 2, 4):
                keep = rid >= d
                u8 = u8 + a8 * jnp.where(keep, pltpu.roll(u8, d, 0), 0.0)
                a8 = a8 * jnp.where(keep, pltpu.roll(a8, d, 0), 1.0)
            h8 = u8 + a8 * before
            hs_ref[pl.ds(off, 8), :] = h8
            return jnp.broadcast_to(h8[7:8], (8, LRU_WIDTH))

        carry[...] = lax.fori_loop(0, CHUNK // 8, group, carry[...])
        gel, _ = _gelu_and_grad(g_ref[...])
        yn_ref[...] = _rms_fwd(gel * hs_ref[...], prm[3:4]).astype(_MXU)

    row = pl.BlockSpec((CHUNK, LRU_WIDTH), lambda t: (t, 0))
    wspec = pl.BlockSpec((LRU_PAIRS, 128, 128), lambda t: (0, 0, 0))
    return pl.pallas_call(
        body, name="lru_fwd", grid=(N_CHUNKS,),
        in_specs=[row, row, wspec, wspec, pl.BlockSpec((8, LRU_WIDTH), lambda t: (0, 0))],
        out_specs=[row, row],
        out_shape=[jax.ShapeDtypeStruct((T_ROWS, LRU_WIDTH), F32), jax.ShapeDtypeStruct((T_ROWS, LRU_WIDTH), _MXU)],
        scratch_shapes=[pltpu.VMEM((8, LRU_WIDTH), F32), pltpu.VMEM((CHUNK, LRU_WIDTH), F32),
                        pltpu.VMEM((CHUNK, LRU_WIDTH), F32)],
        compiler_params=_cparams("arbitrary"),
    )(xr, gate, wa, wx, prm)


def _lru_bwd(dyn, dyn_block, gate, xr, hs, wa, wx, wa_t, wx_t, prm):
    def body(dyn_ref, g_ref, xr_ref, hs_ref, hsp_ref, wa_ref, wx_ref, wat_ref, wxt_ref, prm_ref,
             dg_ref, dxr_ref, dwa_ref, dwx_ref, dprm_ref, carry, a_s, d_s):
        step = pl.program_id(0)
        tile = N_CHUNKS - 1 - step

        @pl.when(step == 0)
        def _():
            carry[...] = jnp.zeros_like(carry)
            dwa_ref[...] = jnp.zeros_like(dwa_ref)
            dwx_ref[...] = jnp.zeros_like(dwx_ref)
            dprm_ref[...] = jnp.zeros_like(dprm_ref)

        prm = prm_ref[...]
        xr_t = xr_ref[...]
        r, i, a, s, sp = _lru_gates(xr_t, wa_ref, wx_ref, prm)
        hs_t = hs_ref[...]
        gel, dgel = _gelu_and_grad(g_ref[...])
        dy, dnw = _rms_bwd(gel * hs_t, prm[3:4], dyn_ref[...])
        dg_ref[...] = (dy * hs_t * dgel).astype(_MXU)
        a_s[...] = a
        d_s[...] = dy * gel
        rid = lax.broadcasted_iota(jnp.int32, (8, LRU_WIDTH), 0)

        def group(k, behind):
            off = pl.multiple_of((CHUNK // 8 - 1 - k) * 8, 8)
            a8 = a_s[pl.ds(off, 8), :]
            d8 = d_s[pl.ds(off, 8), :]
            c8 = jnp.where(rid == 7, 1.0, pltpu.roll(a8, 7, 0))
            for d in (1, 2, 4):
                keep = rid < 8 - d
                d8 = d8 + c8 * jnp.where(keep, pltpu.roll(d8, 8 - d, 0), 0.0)
                c8 = c8 * jnp.where(keep, pltpu.roll(c8, 8 - d, 0), 1.0)
            dht8 = d8 + c8 * behind
            d_s[pl.ds(off, 8), :] = dht8
            return jnp.broadcast_to(a8[0:1] * dht8[0:1], (8, LRU_WIDTH))

        carry[...] = lax.fori_loop(0, CHUNK // 8, group, carry[...])
        dht = d_s[...]
        before = hsp_ref[CHUNK - 8:CHUNK, :][7:8] * (tile > 0).astype(F32)
        first = lax.broadcasted_iota(jnp.int32, (CHUNK, 1), 0) == 0
        hprev = jnp.where(first, before, pltpu.roll(hs_t, 1, 0))
        da = dht * hprev
        ixr = i * xr_t
        ds = dht * ixr
        dlog_a = da * a - ds * (a * a) * lax.rsqrt(s * s)
        dr = dlog_a * ((-LRU_C) * sp)
        dsp = jnp.sum(dlog_a * ((-LRU_C) * r), axis=0, keepdims=True)
        dlam = dsp * (-_sigmoid(-prm[2:3]))
        di = dht * s * xr_t
        dpre_r = dr * r * (1.0 - r)
        dpre_i = di * i * (1.0 - i)
        dxr = dht * s * i
        parts = []
        for k in range(LRU_PAIRS):
            sl = slice(k * 128, (k + 1) * 128)
            parts.append(_dot(dpre_r[:, sl], wat_ref[k]) + _dot(dpre_i[:, sl], wxt_ref[k]))
            dwa_ref[k] += _dot(xr_t[:, sl], dpre_r[:, sl], TN)
            dwx_ref[k] += _dot(xr_t[:, sl], dpre_i[:, sl], TN)
        dxr_ref[...] = dxr + jnp.concatenate(parts, axis=1)
        dprm_ref[0:1, :] += jnp.sum(dpre_r, axis=0, keepdims=True)
        dprm_ref[1:2, :] += jnp.sum(dpre_i, axis=0, keepdims=True)
        dprm_ref[2:3, :] += dlam
        dprm_ref[3:4, :] += jnp.sum(dnw, axis=0, keepdims=True)

    rev = lambda blk=0: pl.BlockSpec((CHUNK, LRU_WIDTH), lambda s, blk=blk: (N_CHUNKS - 1 - s, blk))
    wspec = pl.BlockSpec((LRU_PAIRS, 128, 128), lambda s: (0, 0, 0))
    return pl.pallas_call(
        body, name="lru_bwd", grid=(N_CHUNKS,),
        in_specs=[rev(dyn_block), rev(), rev(), rev(),
                  pl.BlockSpec((CHUNK, LRU_WIDTH), lambda s: (jnp.maximum(N_CHUNKS - 2 - s, 0), 0)),
                  wspec, wspec, wspec, wspec, pl.BlockSpec((8, LRU_WIDTH), lambda s: (0, 0))],
        out_specs=[rev(), rev(), wspec, wspec, pl.BlockSpec((8, LRU_WIDTH), lambda s: (0, 0))],
        out_shape=[jax.ShapeDtypeStruct((T_ROWS, LRU_WIDTH), _MXU), jax.ShapeDtypeStruct((T_ROWS, LRU_WIDTH), F32),
                   jax.ShapeDtypeStruct((LRU_PAIRS, 128, 128), F32), jax.ShapeDtypeStruct((LRU_PAIRS, 128, 128), F32),
                   jax.ShapeDtypeStruct((8, LRU_WIDTH), F32)],
        scratch_shapes=[pltpu.VMEM((8, LRU_WIDTH), F32), pltpu.VMEM((CHUNK, LRU_WIDTH), F32),
                        pltpu.VMEM((CHUNK, LRU_WIDTH), F32)],
        compiler_params=_cparams("arbitrary"),
    )(dyn, gate, xr, hs, hs, wa, wx, wa_t, wx_t, prm)


SEC_NAMES = ("z", "xs", "bc", "dt", "g", "x")
SEC_WIDTH = {"z": 1024, "xs": 1024, "bc": 512, "dt": 128, "g": 1024, "x": 1024}


def _pair_blocks(w):
    w = w.reshape(LRU_PAIRS, 2, 64, 64)
    zero = jnp.zeros((LRU_PAIRS, 64, 64), w.dtype)
    top = jnp.concatenate([w[:, 0], zero], axis=2)
    bot = jnp.concatenate([zero, w[:, 1]], axis=2)
    return jnp.concatenate([top, bot], axis=1)


def _unpair_blocks(wp):
    return jnp.stack([wp[:, :64, :64], wp[:, 64:, 64:]], axis=1).reshape(16, 64, 64)


def _pad_lanes(v, width=128):
    return jnp.pad(v, ((0, 0), (0, width - v.shape[1])))


class _Resident:
    before_embed = ()
    behind_out_proj = ()

    def __init__(self, w_in_sections, w_out, w_gate, w_up, w_down):
        self._w_in, self._w_out, self._ffn = w_in_sections, w_out, (w_gate, w_up, w_down)

    def w_in(self, after):
        return self._w_in

    def mid_forward(self, after):
        return jnp.zeros((1, 1), F32)

    def w_out(self, after):
        return self._w_out

    def ffn(self, after):
        return self._ffn

    def grads_ready(self, names, g, g_mxu):
        return jnp.zeros((1, 1), F32)

    def small_ready(self, g, loss):
        return jnp.zeros((1, 1), F32)

    def small_middle(self, after):
        return jnp.zeros((1, 1), F32)


def _local_step(x, target, meta, p, late):
    g, g_mxu = {}, {}
    ex = _head_expander()
    h0 = _embed(x, meta, late.before_embed)
    w_in = late.w_in(h0)
    convs = {SEC_NAMES.index("xs"): (p["ssd_conv_w"][:, :SSD_WIDTH], p["ssd_conv_b"][:, :SSD_WIDTH], True),
             SEC_NAMES.index("bc"): (p["ssd_conv_w"][:, SSD_WIDTH:], p["ssd_conv_b"][:, SSD_WIDTH:], True),
             SEC_NAMES.index("x"): (p["lru_conv_w"], p["lru_conv_b"], False)}
    u1, projs, acts = _norm_proj(h0, p["norm1_w"], [w_in[s] for s in SEC_NAMES], convs, name="norm_in_proj")
    proj = dict(zip(SEC_NAMES, projs))
    xs_act, bc_act, xr = (acts[SEC_NAMES.index(s)] for s in ("xs", "bc", "x"))
    ssd_prm = jnp.concatenate([_pad_lanes(p["ssd_dt_bias"]), _pad_lanes(p["ssd_a_log"]), _pad_lanes(p["ssd_d"]),
                               jnp.zeros((5, 128), F32)], axis=0)
    y_pre, y_ssd, prev = _ssd_fwd(xs_act, bc_act, proj["dt"], proj["z"], ssd_prm, p["ssd_norm_w"], ex)
    wa_p, wx_p = _pair_blocks(p["lru_wa"]), _pair_blocks(p["lru_wx"])
    lru_prm = jnp.concatenate([p["lru_ba"], p["lru_bx"], p["lru_lambda"], p["lru_norm_w"],
                               jnp.zeros((4, LRU_WIDTH), F32)], axis=0)
    hs, y_lru = _lru_fwd(xr, proj["g"], wa_p.astype(_MXU), wx_p.astype(_MXU),
                         lru_prm + late.mid_forward([xr, y_ssd]))
    ycat = jnp.concatenate([y_ssd, y_lru], axis=1)
    w_out = late.w_out(ycat)
    h1 = _mm([(ycat, 0, w_out, 0, 2 * D_MODEL)], T_ROWS, D_MODEL, tm=T_ROWS, tn=256, mode="nn", out_dtype=F32,
             name="out_proj", residual=h0, behind=late.behind_out_proj)
    u2 = _rmsnorm(h1, p["norm2_w"], name="norm2")
    w_gate, w_up, w_down = late.ffn(u2)
    gp, up, act = _ffn_up(u2, w_gate, w_up)
    h2 = _mm([(act, 0, w_down, 0, D_FF)], T_ROWS, D_MODEL, tm=T_ROWS, tn=256, mode="nn", out_dtype=F32,
             name="ffn_down", residual=h1)
    loss, dh2, dh2b, g["final_norm_w"] = _loss_head(h2, target, p["final_norm_w"])
    dgp, dup = _ffn_bwd_act(dh2b, w_down, gp, up)
    g["w_down"], g_mxu["w_down"] = _mm([(act, 0, dh2b, 0, T_ROWS)], D_FF, D_MODEL, tm=1408, tn=512, mode="tn",
                                       out_dtype=F32, name="dw_down", also_mxu=True)
    dh1, dh1b, g["norm2_w"] = _mm_norm_bwd([(dgp, w_gate, D_FF), (dup, w_up, D_FF)], h1, p["norm2_w"], dh2,
                                           name="ffn_bwd_in")
    g["w_gate"], g_mxu["w_gate"] = _mm([(dgp, 0, u2, 0, T_ROWS)], D_FF, D_MODEL, tm=1408, tn=512, mode="tn",
                                       out_dtype=F32, name="dw_gate", also_mxu=True)
    g["w_up"], g_mxu["w_up"] = _mm([(dup, 0, u2, 0, T_ROWS)], D_FF, D_MODEL, tm=1408, tn=512, mode="tn",
                                   out_dtype=F32, name="dw_up", also_mxu=True)
    g["w_out"], g_mxu["w_out"] = _mm([(ycat, 0, dh1b, 0, T_ROWS)], 2 * D_MODEL, D_MODEL, tm=1024, tn=512, mode="tn",
                                     out_dtype=F32, name="dw_out", also_mxu=True)
    sent = late.grads_ready(("w_down", "w_gate", "w_up", "w_out"), g, g_mxu)
    dycat = _mm([(dh1b, 0, w_out, 0, D_MODEL)], T_ROWS, 2 * D_MODEL, tm=T_ROWS, tn=256, mode="nt", out_dtype=F32,
                name="out_proj_bwd", behind=(sent,))
    dgate, dxr, dwa_p, dwx_p, dlru_prm = _lru_bwd(dycat, 1, proj["g"], xr, hs, wa_p.astype(_MXU), wx_p.astype(_MXU),
                                                  jnp.swapaxes(wa_p, 1, 2).astype(_MXU),
                                                  jnp.swapaxes(wx_p, 1, 2).astype(_MXU), lru_prm)
    g["lru_wa"], g["lru_wx"] = _unpair_blocks(dwa_p), _unpair_blocks(dwx_p)
    g["lru_ba"], g["lru_bx"], g["lru_lambda"], g["lru_norm_w"] = (dlru_prm[k:k + 1] for k in range(4))
    dx_lru, g["lru_conv_w"], g["lru_conv_b"] = _conv_bwd(dxr, proj["x"], p["lru_conv_w"], p["lru_conv_b"], silu=False,
                                                         name="lru_conv_bwd")
    dz, dxs_act, dbc_act, ddt, dssd_prm, g["ssd_norm_w"] = _ssd_bwd(dycat, 0, proj["z"], y_pre, xs_act, bc_act,
                                                                    proj["dt"], prev, ssd_prm, p["ssd_norm_w"], ex)
    g["ssd_dt_bias"], g["ssd_a_log"], g["ssd_d"] = (dssd_prm[k:k + 1, :SSD_HEADS] for k in range(3))
    dxs, dcw_xs, dcb_xs = _conv_bwd(dxs_act, proj["xs"], p["ssd_conv_w"][:, :SSD_WIDTH],
                                    p["ssd_conv_b"][:, :SSD_WIDTH], silu=True, name="ssd_conv_xs_bwd")
    dbc, dcw_bc, dcb_bc = _conv_bwd(dbc_act, proj["bc"], p["ssd_conv_w"][:, SSD_WIDTH:],
                                    p["ssd_conv_b"][:, SSD_WIDTH:], silu=True, name="ssd_conv_bc_bwd")
    g["ssd_conv_w"] = jnp.concatenate([dcw_xs, dcw_bc], axis=1)
    g["ssd_conv_b"] = jnp.concatenate([dcb_xs, dcb_bc], axis=1)
    dproj = {"z": dz, "xs": dxs, "bc": dbc, "dt": ddt, "g": dgate, "x": dx_lru}
    for s in SEC_NAMES:
        wdt = SEC_WIDTH[s]
        g["w_in_" + s], g_mxu["w_in_" + s] = _mm([(dproj[s], 0, u1, 0, T_ROWS)], wdt, D_MODEL, tm=min(wdt, 1024),
                                                 tn=512, mode="tn", out_dtype=F32, name="dw_in_" + s, also_mxu=True)
    sent = late.grads_ready(("w_in",), g, g_mxu)
    dh0, _, g["norm1_w"] = _mm_norm_bwd([(dproj[s], w_in[s], SEC_WIDTH[s]) for s in SEC_NAMES], h0,
                                        p["norm1_w"], dh1, name="in_proj_bwd", behind=(sent,))
    g["meta_tokens"] = dh0[PAD_ROWS:X_ROW0]
    late.small_ready(g, loss)
    return loss, dh0[X_ROW0:], g, g_mxu


MESH = pl.DeviceIdType.MESH
ANY = pl.BlockSpec(memory_space=pl.ANY)


def _my_place():
    return lax.axis_index("x"), lax.axis_index("y"), lax.axis_index("c")


def _other_chips(x, y):
    return [(1 - x, y), (x, 1 - y), (1 - x, 1 - y)]


HBM_SPEC = pl.BlockSpec(memory_space=pltpu.HBM)
SEM_SPEC = pl.BlockSpec(memory_space=pltpu.SEMAPHORE)
SPLIT_EFFECT = pltpu.SideEffectType.DATAFLOW_SIDE_EFFECTING


def _half_cols(buf, c, other=False):
    half = buf.shape[-1] // 2
    return pl.ds(pl.multiple_of(((1 - c) if other else c) * half, 128), half)


def _halves_plan(bufs, x, y, c, incoming):
    plan = []
    for buf in bufs:
        cols = _half_cols(buf, c)
        for (px, py) in _other_chips(x, y):
            slot = 2 * px + py if incoming else 2 * x + y
            plan.append((buf.at[2 * x + y, :, cols], buf.at[slot, :, cols], (px, py, c)))
    return plan


def _forward_plan(bufs, x, y, c, incoming):
    plan = []
    for buf in bufs:
        for (px, py) in _other_chips(x, y):
            slot = 2 * px + py
            plan.append((buf.at[slot, :, _half_cols(buf, c)], buf.at[slot, :, _half_cols(buf, c, other=incoming)],
                         (x, y, 1 - c)))
    return plan


def _scatter_plan(bufs, x, y, c, incoming):
    n = len(bufs) // 2
    plan = []
    for k in range(n):
        for j, (px, py) in enumerate(_other_chips(x, y)):
            plan.append((bufs[k].at[2 * px + py], bufs[n + k].at[j], (px, py, c)))
    return plan


def _split_start(bufs, plan, n_copies, after, *, name):
    n = len(bufs)
    extra = [] if after is None else [after]

    def body(*refs):
        ins = refs[:n]
        send_sems, recv_sems = refs[n + len(extra)], refs[n + len(extra) + 1]
        token = refs[-1]
        x, y, c = _my_place()
        for i, (src, dst, dev) in enumerate(plan(ins, x, y, c, False)):
            pltpu.make_async_remote_copy(src_ref=src, dst_ref=dst, send_sem=send_sems.at[i], recv_sem=recv_sems.at[i],
                                         device_id=dev, device_id_type=MESH).start()
        token[...] = jnp.zeros_like(token)

    outs = pl.pallas_call(
        body, name=name,
        out_shape=(pltpu.SemaphoreType.DMA((n_copies,)), pltpu.SemaphoreType.DMA((n_copies,)),
                   *[pltpu.HBM(b.shape, b.dtype) for b in bufs], jax.ShapeDtypeStruct((8, 128), F32)),
        in_specs=[HBM_SPEC] * n + [ANY] * len(extra),
        out_specs=(SEM_SPEC, SEM_SPEC, *[HBM_SPEC] * n, pl.BlockSpec(memory_space=pltpu.VMEM)),
        input_output_aliases={k: 2 + k for k in range(n)},
        compiler_params=pltpu.CompilerParams(has_side_effects=SPLIT_EFFECT),
    )(*[pltpu.with_memory_space_constraint(b, pltpu.HBM) for b in bufs], *extra)
    return outs[0], outs[1], list(outs[2:2 + n]), outs[-1]


def _split_wait(bufs, send_sems, recv_sems, plan, after, *, name):
    n = len(bufs)
    after = list(after) if isinstance(after, (list, tuple)) else [after]

    def body(*refs):
        ins = refs[:n]
        send_sems_ref, recv_sems_ref = refs[n], refs[n + 1]
        x, y, c = _my_place()
        for i, (src, dst, dev) in enumerate(plan(ins, x, y, c, True)):
            cp = pltpu.make_async_remote_copy(src_ref=src, dst_ref=dst, send_sem=send_sems_ref.at[i],
                                              recv_sem=recv_sems_ref.at[i], device_id=dev, device_id_type=MESH)
            cp.wait_send()
            cp.wait_recv()

    outs = pl.pallas_call(
        body, name=name, out_shape=tuple(pltpu.HBM(b.shape, b.dtype) for b in bufs),
        in_specs=[HBM_SPEC] * n + [SEM_SPEC, SEM_SPEC] + [ANY] * len(after), out_specs=tuple([HBM_SPEC] * n),
        input_output_aliases={k: k for k in range(n)},
        compiler_params=pltpu.CompilerParams(has_side_effects=SPLIT_EFFECT),
    )(*bufs, send_sems, recv_sems, *after)
    return list(outs)


def _fill_own_slots(shards, me_arr, *, name, behind=()):
    n = len(shards)
    n_in = n + len(behind)

    def body(me_ref, *refs):
        for k in range(n):
            refs[n_in + k][0] = refs[k][...].astype(_MXU)

    half = D_MODEL // 2
    return pl.pallas_call(
        body, name=name,
        grid_spec=pltpu.PrefetchScalarGridSpec(
            num_scalar_prefetch=1, grid=(2,),
            in_specs=[pl.BlockSpec((s.shape[0], half), lambda i, me: (0, i)) for s in shards]
            + [pl.BlockSpec(memory_space=pl.ANY)] * len(behind),
            out_specs=[pl.BlockSpec((1, s.shape[0], half), lambda i, me: (me[0], 0, i)) for s in shards]),
        out_shape=[jax.ShapeDtypeStruct((N_SHARDS,) + s.shape, _MXU) for s in shards],
        compiler_params=_cparams("parallel"),
    )(me_arr, *shards, *behind)


def _gather_small(small):
    def body(s_ref, o_ref, send_sems, recv_sems, local_sem):
        x, y, c = _my_place()
        me = 2 * x + y
        local = pltpu.make_async_copy(s_ref, o_ref.at[me], local_sem)
        local.start()
        copies = [(pltpu.make_async_remote_copy(src_ref=s_ref, dst_ref=o_ref.at[me], send_sem=send_sems.at[j],
                                                recv_sem=recv_sems.at[j], device_id=(px, py, c), device_id_type=MESH),
                   2 * px + py) for j, (px, py) in enumerate(_other_chips(x, y))]
        for cp, _ in copies:
            cp.start()
        for j, (cp, slot) in enumerate(copies):
            cp.wait_send()
            pltpu.make_async_remote_copy(src_ref=s_ref, dst_ref=o_ref.at[slot], send_sem=send_sems.at[j],
                                         recv_sem=recv_sems.at[j], device_id=(x, y, c),
                                         device_id_type=MESH).wait_recv()
        local.wait()

    return pl.pallas_call(
        body, name="gather_small", in_specs=[ANY], out_specs=ANY,
        out_shape=jax.ShapeDtypeStruct((N_SHARDS,) + small.shape, small.dtype),
        scratch_shapes=[pltpu.SemaphoreType.DMA((3,)), pltpu.SemaphoreType.DMA((3,)), pltpu.SemaphoreType.DMA],
    )(small)


def _swap_with_sibling(parts, *, name, behind=()):
    n = len(parts)
    nb = len(behind)

    def body(*refs):
        ins, outs = refs[:n], refs[n + nb:2 * n + nb]
        send_sems, recv_sems = refs[2 * n + nb:]
        x, y, c = _my_place()
        copies = [pltpu.make_async_remote_copy(
            src_ref=ins[k], dst_ref=outs[k], send_sem=send_sems.at[k], recv_sem=recv_sems.at[k],
            device_id=(x, y, 1 - c), device_id_type=MESH) for k in range(n)]
        for cp in copies:
            cp.start()
        for cp in copies:
            cp.wait()

    return pl.pallas_call(
        body, name=name, in_specs=[ANY] * (n + nb), out_specs=[ANY] * n,
        out_shape=[jax.ShapeDtypeStruct(a.shape, a.dtype) for a in parts],
        scratch_shapes=[pltpu.SemaphoreType.DMA((n,)), pltpu.SemaphoreType.DMA((n,))],
    )(*parts, *behind)


def _other_devices(x, y, c):
    out = []
    for mask in range(1, N_DEV):
        px, py, pc = x ^ (mask >> 2 & 1), y ^ (mask >> 1 & 1), c ^ (mask & 1)
        out.append(((px, py, pc), 4 * px + 2 * py + pc))
    return out


def _pieces_plan(bufs, x, y, c, incoming):
    pack, land = bufs
    me = 4 * x + 2 * y + c
    return [(pack.at[num], land.at[num if incoming else me], dev) for dev, num in _other_devices(x, y, c)]


def _spread_plan(bufs, x, y, c, incoming):
    piece, land = bufs
    me = 4 * x + 2 * y + c
    return [(piece, land.at[num if incoming else me], dev) for dev, num in _other_devices(x, y, c)]


def _sum_pieces(pack, land, dev_arr, *, name):
    def body(dev_ref, pack_ref, land_ref, o_ref):
        dev = dev_ref[0]
        own = pack_ref[dev]
        acc = None
        for d in range(N_DEV):
            term = jnp.where(dev == d, own, land_ref[d])
            acc = term if acc is None else acc + term
        o_ref[...] = acc

    vmem = pl.BlockSpec(memory_space=pltpu.VMEM)
    return pl.pallas_call(
        body, name=name, in_specs=[pl.BlockSpec(memory_space=pltpu.SMEM), vmem, vmem], out_specs=vmem,
        out_shape=jax.ShapeDtypeStruct(pack.shape[1:], F32),
    )(dev_arr, pack, land)


def _join_pieces(piece, land, dev_arr, *, name):
    def body(dev_ref, piece_ref, land_ref, o_ref):
        dev = dev_ref[0]
        for d in range(N_DEV):
            o_ref[d] = jnp.where(dev == d, piece_ref[...], land_ref[d])

    vmem = pl.BlockSpec(memory_space=pltpu.VMEM)
    return pl.pallas_call(
        body, name=name, in_specs=[pl.BlockSpec(memory_space=pltpu.SMEM), vmem, vmem], out_specs=vmem,
        out_shape=jax.ShapeDtypeStruct(land.shape, F32),
    )(dev_arr, piece, land)


def _adamw_native(ws, gs, ms, vs):
    n = len(ws)

    def body(*refs):
        for k in range(n):
            w_ref, g_ref, m_ref, v_ref = (refs[j * n + k] for j in range(4))
            delta, m_new, v_new = _adamw_math(w_ref[...], g_ref[...], m_ref[...], v_ref[...])
            refs[4 * n + k][...] = delta
            refs[5 * n + k][...] = m_new
            refs[6 * n + k][...] = v_new

    vmem = pl.BlockSpec(memory_space=pltpu.VMEM)
    shapes = [jax.ShapeDtypeStruct(a.shape, F32) for a in ws]
    outs = pl.pallas_call(
        body, name="adamw_small", in_specs=[vmem] * (4 * n), out_specs=[vmem] * (3 * n), out_shape=shapes * 3,
        compiler_params=pltpu.CompilerParams(vmem_limit_bytes=VMEM_LIMIT_BYTES),
    )(*ws, *gs, *ms, *vs)
    return outs[:n], outs[n:2 * n], outs[2 * n:]


def _elementwise_tile(rows, cols):
    for t in range(256, 15, -16):
        if rows % t == 0:
            return (t, cols), rows // t, lambda i: (i, 0)
    assert cols % 256 == 0
    return (rows, 256), cols // 256, lambda i: (0, i)


def _partial_sum(own, land, me_arr, *, name):
    r, c = own.shape[-2:]
    tile, steps, imap = _elementwise_tile(r, c)
    whole = own.ndim == 3

    def body(me_ref, own_ref, land_ref, o_ref):
        acc = own_ref[0] if whole else own_ref[...]
        for j in range(3):
            acc = acc + land_ref[j].astype(F32)
        o_ref[...] = acc.astype(_MXU)

    own_spec = (pl.BlockSpec((1,) + tile, lambda i, me: (me[0],) + imap(i)) if whole
                else pl.BlockSpec(tile, lambda i, me: imap(i)))
    return pl.pallas_call(
        body, name=name,
        grid_spec=pltpu.PrefetchScalarGridSpec(
            num_scalar_prefetch=1, grid=(steps,),
            in_specs=[own_spec, pl.BlockSpec((3,) + tile, lambda i, me: (0,) + imap(i))],
            out_specs=pl.BlockSpec(tile, lambda i, me: imap(i))),
        out_shape=jax.ShapeDtypeStruct((r, c), _MXU),
        compiler_params=_cparams("parallel"),
    )(me_arr, own, land)


LANE_TILE = 256


def _partial_sums(owns, lands, me_arr, *, name):
    n = len(owns)

    def body(me_ref, *refs):
        for k in range(n):
            acc = refs[k][0]
            for j in range(3):
                acc = acc + refs[n + k][j].astype(F32)
            refs[2 * n + k][...] = acc.astype(_MXU)

    rows = [o.shape[1] for o in owns]
    return pl.pallas_call(
        body, name=name,
        grid_spec=pltpu.PrefetchScalarGridSpec(
            num_scalar_prefetch=1, grid=(D_MODEL // LANE_TILE,),
            in_specs=[pl.BlockSpec((1, r, LANE_TILE), lambda i, me: (me[0], 0, i)) for r in rows]
            + [pl.BlockSpec((3, r, LANE_TILE), lambda i, me: (0, 0, i)) for r in rows],
            out_specs=[pl.BlockSpec((r, LANE_TILE), lambda i, me: (0, i)) for r in rows]),
        out_shape=[jax.ShapeDtypeStruct((r, D_MODEL), _MXU) for r in rows],
        compiler_params=_cparams("parallel"),
    )(me_arr, *owns, *lands)


def _adamws(ws, parts_a, parts_b, ms, vs, *, name):
    n = len(ws)

    def body(*refs):
        for k in range(n):
            w_ref, a_ref, b_ref, m_ref, v_ref = (refs[j * n + k] for j in range(5))
            g = a_ref[...].astype(F32) + b_ref[...].astype(F32)
            delta, m_new, v_new = _adamw_math(w_ref[...], g, m_ref[...], v_ref[...])
            for j, val in enumerate((g, delta, m_new, v_new)):
                refs[(5 + j) * n + k][...] = val

    tiles = [pl.BlockSpec((w.shape[0], LANE_TILE), lambda i: (0, i)) for w in ws]
    outs = pl.pallas_call(
        body, name=name, grid=(D_MODEL // LANE_TILE,), in_specs=tiles * 5, out_specs=tiles * 4,
        out_shape=[jax.ShapeDtypeStruct(w.shape, F32) for w in ws] * 4,
        compiler_params=_cparams("parallel"),
    )(*ws, *parts_a, *parts_b, *ms, *vs)
    return [outs[j * n:(j + 1) * n] for j in range(4)]


def _adamw_math(w, g, m, v):
    m = ADAM_B1 * m + (1.0 - ADAM_B1) * g
    v = ADAM_B2 * v + (1.0 - ADAM_B2) * (g * g)
    m_hat = m / (1.0 - ADAM_B1 ** ADAM_STEP)
    v_hat = v / (1.0 - ADAM_B2 ** ADAM_STEP)
    delta = -ADAM_LR * (m_hat / (jnp.sqrt(v_hat) + ADAM_EPS) + ADAM_WD * w)
    return delta, m, v


def _adamw(w, grad_parts, m, v, *, name):
    if w.ndim == 3:
        steps = 4
        assert w.shape[0] % steps == 0
        tile_shape, imap = (w.shape[0] // steps,) + w.shape[1:], lambda i: (i, 0, 0)
    else:
        tile_shape, steps, imap = _elementwise_tile(*w.shape)
    n = len(grad_parts)

    def body(*refs):
        w_ref, m_ref, v_ref = refs[:3]
        g_refs = refs[3:3 + n]
        g_out, d_out, m_out, v_out = refs[3 + n:]
        g = g_refs[0][...].astype(F32)
        for k in range(1, n):
            g = g + g_refs[k][...].astype(F32)
        delta, m_new, v_new = _adamw_math(w_ref[...], g, m_ref[...], v_ref[...])
        g_out[...] = g
        d_out[...] = delta
        m_out[...] = m_new
        v_out[...] = v_new

    tile = pl.BlockSpec(tile_shape, imap)
    return pl.pallas_call(
        body, name=name, grid=(steps,), in_specs=[tile] * (3 + n), out_specs=[tile] * 4,
        out_shape=[jax.ShapeDtypeStruct(w.shape, F32)] * 4,
        compiler_params=_cparams("parallel"),
    )(w, m, v, *grad_parts)


WEIGHT_NAMES = ("meta_tokens", "norm1_w", "w_in", "ssd_conv_w", "ssd_conv_b", "ssd_dt_bias", "ssd_a_log", "ssd_d",
                "ssd_norm_w", "lru_conv_w", "lru_conv_b", "lru_wa", "lru_ba", "lru_wx", "lru_bx", "lru_lambda",
                "lru_norm_w", "w_out", "norm2_w", "w_gate", "w_up", "w_down", "final_norm_w")
BIG = ("w_in", "w_out", "w_gate", "w_up", "w_down")
FFN = ("w_gate", "w_up", "w_down")
LATE = ("w_out",) + FFN
SMALL_SHARDED = {"meta_tokens": (N_META, D_MODEL), "ssd_conv_w": (CONV_K, 1536), "lru_conv_w": (CONV_K, LRU_WIDTH)}
SMALL = tuple(n for n in WEIGHT_NAMES if n not in BIG)
PACK_COLS = 1024


def _pack(arrays, row_multiple):
    flat = jnp.concatenate([a.reshape(-1) for a in arrays])
    rows = -(-flat.shape[0] // (row_multiple * PACK_COLS)) * row_multiple
    return jnp.pad(flat, (0, rows * PACK_COLS - flat.shape[0])).reshape(rows, PACK_COLS)


def _unpack(pack, shapes):
    flat = pack.reshape(-1)
    out, off = [], 0
    for s in shapes:
        size = math.prod(s)
        out.append(flat[off:off + size].reshape(s))
        off += size
    return out


def _unshard_cols(g4):
    return jnp.swapaxes(g4, 0, 1).reshape(g4.shape[1], -1)


COL_SHARDED = ("w_in", "w_gate", "w_up")
IN_ROWS = {"z": (0, 1024), "xs": (1024, 2048), "bc": (2048, 2560), "dt": (2560, 2576), "g": (2576, 3600),
           "x": (3600, IN_COLS)}


def _rows_of_shards(shards4, lo, hi):
    r = shards4.shape[1]
    parts = [shards4[k, max(lo, k * r) - k * r:min(hi, (k + 1) * r) - k * r]
             for k in range(N_SHARDS) if max(lo, k * r) < min(hi, (k + 1) * r)]
    return parts[0] if len(parts) == 1 else jnp.concatenate(parts, axis=0)


def _w_in_shard_rows(k, sections):
    lo, hi = k * (IN_COLS // N_SHARDS), (k + 1) * (IN_COLS // N_SHARDS)
    parts = []
    for arr, (a, b) in zip(sections, IN_ROWS.values()):
        if max(lo, a) < min(hi, b):
            parts.append(arr[max(lo, a) - a:min(hi, b) - a])
    return jnp.concatenate(parts, axis=0)


def _rows_view(name, block):
    return jnp.swapaxes(block[0], 0, 1) if name in COL_SHARDED else block[0]


def _param_view(name, rows):
    return (jnp.swapaxes(rows, 0, 1) if name in COL_SHARDED else rows)[None]


def kernel(x, meta_tokens, norm1_w, w_in, ssd_conv_w, ssd_conv_b, ssd_dt_bias, ssd_a_log, ssd_d, ssd_norm_w, lru_conv_w, lru_conv_b, lru_wa, lru_ba, lru_wx, lru_bx, lru_lambda, lru_norm_w, w_out, norm2_w, w_gate, w_up, w_down, final_norm_w, loss_target, m_meta_tokens, m_norm1_w, m_w_in, m_ssd_conv_w, m_ssd_conv_b, m_ssd_dt_bias, m_ssd_a_log, m_ssd_d, m_ssd_norm_w, m_lru_conv_w, m_lru_conv_b, m_lru_wa, m_lru_ba, m_lru_wx, m_lru_bx, m_lru_lambda, m_lru_norm_w, m_w_out, m_norm2_w, m_w_gate, m_w_up, m_w_down, m_final_norm_w, v_meta_tokens, v_norm1_w, v_w_in, v_ssd_conv_w, v_ssd_conv_b, v_ssd_dt_bias, v_ssd_a_log, v_ssd_d, v_ssd_norm_w, v_lru_conv_w, v_lru_conv_b, v_lru_wa, v_lru_ba, v_lru_wx, v_lru_bx, v_lru_lambda, v_lru_norm_w, v_w_out, v_norm2_w, v_w_gate, v_w_up, v_w_down, v_final_norm_w):
    w = dict(zip(WEIGHT_NAMES, (meta_tokens, norm1_w, w_in, ssd_conv_w, ssd_conv_b, ssd_dt_bias, ssd_a_log, ssd_d, ssd_norm_w, lru_conv_w, lru_conv_b, lru_wa, lru_ba, lru_wx, lru_bx, lru_lambda, lru_norm_w, w_out, norm2_w, w_gate, w_up, w_down, final_norm_w)))
    m = dict(zip(WEIGHT_NAMES, (m_meta_tokens, m_norm1_w, m_w_in, m_ssd_conv_w, m_ssd_conv_b, m_ssd_dt_bias, m_ssd_a_log, m_ssd_d, m_ssd_norm_w, m_lru_conv_w, m_lru_conv_b, m_lru_wa, m_lru_ba, m_lru_wx, m_lru_bx, m_lru_lambda, m_lru_norm_w, m_w_out, m_norm2_w, m_w_gate, m_w_up, m_w_down, m_final_norm_w)))
    v = dict(zip(WEIGHT_NAMES, (v_meta_tokens, v_norm1_w, v_w_in, v_ssd_conv_w, v_ssd_conv_b, v_ssd_dt_bias, v_ssd_a_log, v_ssd_d, v_ssd_norm_w, v_lru_conv_w, v_lru_conv_b, v_lru_wa, v_lru_ba, v_lru_wx, v_lru_bx, v_lru_lambda, v_lru_norm_w, v_w_out, v_norm2_w, v_w_gate, v_w_up, v_w_down, v_final_norm_w)))
    me = 2 * lax.axis_index("x") + lax.axis_index("y")

    big2d = {n: _rows_view(n, w[n]) for n in BIG}
    small_local = jnp.concatenate([w["meta_tokens"].reshape(-1), w["ssd_conv_w"].reshape(-1),
                                   w["lru_conv_w"].reshape(-1)])[None]
    me_arr = me.astype(jnp.int32).reshape(1)
    dev_arr = (2 * me + lax.axis_index("c")).astype(jnp.int32).reshape(1)
    small4 = _gather_small(small_local)
    (w_in_slot,) = _fill_own_slots([big2d["w_in"]], me_arr, name="own_slot_w_in")
    in_send, in_recv, in_bufs, in_tok = _split_start([w_in_slot], _halves_plan, 3, small4, name="gather_w_in_start")
    late_slots = _fill_own_slots([big2d[n] for n in LATE], me_arr, name="own_slots_late", behind=(in_tok,))
    sm = small4[:, 0]
    meta_full = _unshard_cols(sm[:, :4096].reshape(N_SHARDS, N_META, 256))
    ssd_conv_w_full = _unshard_cols(sm[:, 4096:5632].reshape(N_SHARDS, CONV_K, 384))
    lru_conv_w_full = _unshard_cols(sm[:, 5632:].reshape(N_SHARDS, CONV_K, 256))

    p = {"ssd_conv_w": ssd_conv_w_full, "lru_conv_w": lru_conv_w_full,
         "lru_wa": w["lru_wa"][0], "lru_wx": w["lru_wx"][0], "final_norm_w": w["final_norm_w"][None]}
    for n in ("norm1_w", "ssd_conv_b", "ssd_dt_bias", "ssd_a_log", "ssd_d", "ssd_norm_w", "lru_conv_b", "lru_ba",
              "lru_bx", "lru_lambda", "lru_norm_w", "norm2_w"):
        p[n] = w[n]

    class Late:
        def __init__(self):
            self.pending = []
            self.before_embed = (late_slots[0],)

        def w_in(self, after):
            (buf,) = _split_wait(in_bufs, in_send, in_recv, _halves_plan, after, name="gather_w_in_wait")
            send, recv, bufs, tok = _split_start([buf], _forward_plan, 3, None, name="forward_w_in_start")
            self.out_gather = _split_start(late_slots[:1], _halves_plan, 3, tok, name="gather_w_out_start")
            self.ffn_gather = _split_start(late_slots[1:], _halves_plan, 3 * len(FFN), self.out_gather[3],
                                           name="gather_ffn_start")
            (w_in4,) = _split_wait(bufs, send, recv, _forward_plan, self.ffn_gather[2][0], name="forward_w_in_wait")
            sections = {s: _rows_of_shards(w_in4, lo, hi) for s, (lo, hi) in IN_ROWS.items()}
            sections["dt"] = jnp.pad(sections["dt"], ((0, SEC_WIDTH["dt"] - SSD_HEADS), (0, 0)))
            return sections

        def mid_forward(self, after):
            send, recv, bufs, _ = self.out_gather
            bufs = _split_wait(bufs, send, recv, _halves_plan, after, name="gather_w_out_wait")
            self.forward = _split_start(bufs, _forward_plan, 3, None, name="forward_w_out_start")
            return self.forward[3][:1, :1]

        def w_out(self, after):
            send, recv, bufs, _ = self.forward
            (w,) = _split_wait(bufs, send, recv, _forward_plan, after, name="forward_w_out_wait")
            send, recv, bufs, _ = self.ffn_gather
            bufs = _split_wait(bufs, send, recv, _halves_plan, after, name="gather_ffn_wait")
            self.forward = _split_start(bufs, _forward_plan, 3 * len(FFN), None, name="forward_ffn_start")
            self.behind_out_proj = (self.forward[2][0],)
            return w.reshape(-1, D_MODEL)

        def ffn(self, after):
            send, recv, bufs, _ = self.forward
            bufs = _split_wait(bufs, send, recv, _forward_plan, after, name="forward_ffn_wait")
            return tuple(b.reshape(-1, D_MODEL) for b in bufs)

        def grads_ready(self, names, g, g_mxu):
            if names == ("w_in",):
                g_mxu["w_in"] = jnp.stack([_w_in_shard_rows(k, [g_mxu["w_in_" + s] for s in SEC_NAMES])
                                           for k in range(N_SHARDS)])
            srcs = [g_mxu[n].reshape(N_SHARDS, -1, D_MODEL) for n in names]
            lands = [lax.empty((3,) + s.shape[1:], _MXU) for s in srcs]
            tag = "_".join(names)
            send, recv, bufs, tok = _split_start(srcs + lands, _scatter_plan, 3 * len(names), None,
                                                 name="scatter_" + tag + "_start")
            self.pending.append((names, send, recv, bufs, tag))
            self.in_flight = bufs[0]
            return tok[:1, :1]

        def landed(self, after, which):
            land = {}
            for names, send, recv, bufs, tag in self.pending:
                if names[0] in which:
                    bufs = _split_wait(bufs, send, recv, _scatter_plan, after, name="scatter_" + tag + "_wait")
                    land.update(zip(names, bufs[len(names):]))
            return land

        def small_ready(self, g, loss):
            pack = _pack([g[n] for n in SMALL] + [loss[0, :1]], 8 * N_DEV)
            pack = pack.reshape(N_DEV, -1, PACK_COLS)
            self.small = _split_start([pack, lax.empty(pack.shape, F32)], _pieces_plan, N_DEV - 1, loss,
                                      name="small_pieces_start")
            return self.small[3]

        def small_middle(self, after):
            send, recv, bufs, _ = self.small
            pack, land = _split_wait(bufs, send, recv, _pieces_plan, after, name="small_pieces_wait")
            piece = _sum_pieces(pack, land, dev_arr, name="small_pieces_sum")
            self.small = _split_start([piece, lax.empty(pack.shape, F32)], _spread_plan, N_DEV - 1, None,
                                      name="small_spread_start")
            return self.small[3]

        def small_sum(self, after):
            send, recv, bufs, _ = self.small
            piece, land = _split_wait(bufs, send, recv, _spread_plan, after, name="small_spread_wait")
            return _join_pieces(piece, land, dev_arr, name="small_join")

    late = Late()

    loss, grad_x, g, g_mxu = _local_step(x[0], loss_target[0], meta_full, p, late)

    g4 = {n: g[n].reshape(N_SHARDS, -1, D_MODEL) for n in LATE}
    g4["w_in"] = lax.switch(me, [functools.partial(_w_in_shard_rows, k) for k in range(N_SHARDS)],
                            [g["w_in_" + s] for s in SEC_NAMES])
    land = late.landed([late.in_flight, late.small[2][0]], LATE)
    part = dict(zip(LATE, _partial_sums([g4[n] for n in LATE], [land[n] for n in LATE], me_arr,
                                        name="partial_late")))
    sib = dict(zip(LATE, _swap_with_sibling([part[n] for n in LATE], name="swap_late")))

    grad, delta, new_m, new_v = {}, {}, {}, {}
    late_outs = _adamws([big2d[n] for n in LATE], [part[n] for n in LATE], [sib[n] for n in LATE],
                        [_rows_view(n, m[n]) for n in LATE], [_rows_view(n, v[n]) for n in LATE], name="adamw_late")
    for d, outs in zip((grad, delta, new_m, new_v), late_outs):
        d.update({n: _param_view(n, o) for n, o in zip(LATE, outs)})

    land.update(late.landed(late_outs[0][0], ("w_in",)))
    spread = late.small_middle(land["w_in"])
    part_in = _partial_sum(g4["w_in"], land["w_in"], me_arr, name="partial_w_in")
    (sib_in,) = _swap_with_sibling([part_in], name="swap_w_in", behind=(spread,))
    lanes = lambda a: a[0].reshape(8, 128, -1).transpose(2, 0, 1)
    pieces = lambda a: a.reshape(-1, 8, 128)
    outs = _adamw(lanes(w["w_in"]), [pieces(part_in), pieces(sib_in)], lanes(m["w_in"]), lanes(v["w_in"]),
                  name="adamw_w_in")
    grad["w_in"], delta["w_in"], new_m["w_in"], new_v["w_in"] = (o.transpose(1, 2, 0).reshape(1, D_MODEL, -1)
                                                                 for o in outs)

    small_full_shape = {n: (SMALL_SHARDED[n] if n in SMALL_SHARDED else w[n].shape) for n in SMALL}
    red_list = _unpack(late.small_sum(outs[0]), [small_full_shape[n] for n in SMALL] + [(1,)])
    loss_total = red_list[-1][0]
    g_small = {}
    for n, arr in zip(SMALL, red_list[:-1]):
        if n in SMALL_SHARDED:
            cols = SMALL_SHARDED[n][1] // N_SHARDS
            arr = lax.dynamic_slice_in_dim(arr, me * cols, cols, axis=1)
        g_small[n] = arr.reshape(w[n].shape)
    two_d = lambda a: a.reshape(1, -1) if a.ndim == 1 else a
    deltas, new_ms, new_vs = _adamw_native(*[[two_d(d[n]) for n in SMALL] for d in (w, g_small, m, v)])
    for n, dn, mn, vn in zip(SMALL, deltas, new_ms, new_vs):
        grad[n], delta[n], new_m[n], new_v[n] = (g_small[n], dn.reshape(w[n].shape), mn.reshape(w[n].shape),
                                                 vn.reshape(w[n].shape))

    return (loss_total, grad_x[None], *[grad[n] for n in WEIGHT_NAMES], *[delta[n] for n in WEIGHT_NAMES],
            *[new_m[n] for n in WEIGHT_NAMES], *[new_v[n] for n in WEIGHT_NAMES])
```

```python
import functools
import math

import jax
import jax.numpy as jnp
from jax import lax
from jax.experimental import pallas as pl
from jax.experimental.pallas import tpu as pltpu

F32 = jnp.float32
_MXU = jnp.bfloat16

D_MODEL = 1024
SEQ = 2048
N_META = 16
CHUNK = 128
T_ROWS = 2176
N_CHUNKS = T_ROWS // CHUNK
PAD_ROWS = T_ROWS - SEQ - N_META
X_ROW0 = PAD_ROWS + N_META
SSD_HEADS = 16
SSD_HEAD_DIM = 64
SSD_STATE = 128
SSD_GROUPS = 2
SSD_HPG = SSD_HEADS // SSD_GROUPS
SSD_WIDTH = 1024
LRU_WIDTH = 1024
LRU_C = 8.0
D_FF = 2816
EPS = 1e-6
IN_COLS = 4624
N_SHARDS = 4
N_DEV = 8

ADAM_LR = 0.001
ADAM_B1 = 0.9
ADAM_B2 = 0.999
ADAM_EPS = 1e-08
ADAM_WD = 0.01
ADAM_STEP = 10

VMEM_LIMIT_BYTES = 56 * 1024 * 1024

NN = (((1,), (0,)), ((), ()))
NT = (((1,), (1,)), ((), ()))
TN = (((0,), (0,)), ((), ()))


def _cparams(*sem):
    return pltpu.CompilerParams(dimension_semantics=sem, vmem_limit_bytes=VMEM_LIMIT_BYTES)


def _dot(a, b, dims=NN):
    return lax.dot_general(a.astype(_MXU), b.astype(_MXU), dims, preferred_element_type=F32)


def _dot_onehot(a, b, dims=NN, *, data=0, pieces=3):
    ops = [a, b]
    mask = ops[1 - data].astype(jnp.bfloat16)
    rest = ops[data]
    acc = None
    for _ in range(pieces):
        piece = rest.astype(jnp.bfloat16)
        ops[data], ops[1 - data] = piece, mask
        d = lax.dot_general(ops[0], ops[1], dims, preferred_element_type=F32)
        acc = d if acc is None else acc + d
        rest = rest - piece.astype(F32)
    return acc


def _sigmoid(x):
    return 0.5 * (1.0 + jnp.tanh(0.5 * x))


def _softplus(x):
    return jnp.maximum(x, 0.0) + jnp.log(1.0 + jnp.exp(-jnp.abs(x)))


def _silu(x):
    return x * _sigmoid(x)


def _silu_grad(x):
    s = _sigmoid(x)
    return s * (1.0 + x * (1.0 - s))


_GELU_C = math.sqrt(2.0 / math.pi)


def _gelu_and_grad(x):
    inner = _GELU_C * (x + 0.044715 * x * x * x)
    t = jnp.tanh(inner)
    g = 0.5 * x * (1.0 + t)
    dg = 0.5 * (1.0 + t) + 0.5 * x * (1.0 - t * t) * _GELU_C * (1.0 + 3.0 * 0.044715 * x * x)
    return g, dg


def _rms_fwd(x, w):
    rstd = lax.rsqrt(jnp.mean(x * x, axis=-1, keepdims=True) + EPS)
    return x * rstd * w


def _rms_bwd(x, w, dy):
    rstd = lax.rsqrt(jnp.mean(x * x, axis=-1, keepdims=True) + EPS)
    xhat = x * rstd
    dxhat = dy * w
    dx = rstd * (dxhat - xhat * jnp.mean(dxhat * xhat, axis=-1, keepdims=True))
    return dx, dy * xhat


def _mm(terms, m, n, *, tm, tn, mode, out_dtype, name, residual=None, n_outer=False, also_mxu=False, behind=()):
    gm, gn = m // tm, n // tn
    assert gm * tm == m and gn * tn == n
    if n_outer:
        grid = (gn, gm)
        mi = lambda g0, g1: g1
        ni = lambda g0, g1: g0
    else:
        grid = (gm, gn)
        mi = lambda g0, g1: g0
        ni = lambda g0, g1: g1
    in_specs, args = [], []
    for (a, ka, b, kb, k) in terms:
        if mode == "tn":
            in_specs.append(pl.BlockSpec((k, tm), lambda g0, g1, ka=ka: (ka, mi(g0, g1))))
        else:
            in_specs.append(pl.BlockSpec((tm, k), lambda g0, g1, ka=ka: (mi(g0, g1), ka)))
        if mode == "nt":
            in_specs.append(pl.BlockSpec((tn, k), lambda g0, g1, kb=kb: (ni(g0, g1), kb)))
        else:
            in_specs.append(pl.BlockSpec((k, tn), lambda g0, g1, kb=kb: (kb, ni(g0, g1))))
        args += [a, b]
    if residual is not None:
        in_specs.append(pl.BlockSpec((tm, tn), lambda g0, g1: (mi(g0, g1), ni(g0, g1))))
        args.append(residual)
    dims = {"nn": NN, "nt": NT, "tn": TN}[mode]
    n_terms = len(terms)
    has_res = residual is not None
    in_specs += [pl.BlockSpec(memory_space=pl.ANY)] * len(behind)
    args += list(behind)
    n_in = len(args)

    def body(*refs):
        acc = None
        for t in range(n_terms):
            d = lax.dot_general(refs[2 * t][...], refs[2 * t + 1][...], dims, preferred_element_type=F32)
            acc = d if acc is None else acc + d
        if has_res:
            acc = acc + refs[2 * n_terms][...]
        refs[n_in][...] = acc.astype(out_dtype)
        if also_mxu:
            refs[n_in + 1][...] = acc.astype(_MXU)

    tile = pl.BlockSpec((tm, tn), lambda g0, g1: (mi(g0, g1), ni(g0, g1)))
    shape = jax.ShapeDtypeStruct((m, n), out_dtype)
    return pl.pallas_call(
        body, name=name, grid=grid, in_specs=in_specs,
        out_specs=[tile, tile] if also_mxu else tile,
        out_shape=[shape, jax.ShapeDtypeStruct((m, n), _MXU)] if also_mxu else shape,
        compiler_params=_cparams("parallel", "parallel"),
    )(*args)


def _embed(x, meta, behind=()):
    def body(x_ref, meta_ref, *rest):
        o_ref = rest[-1]
        i = pl.program_id(0)

        @pl.when(i == 0)
        def _():
            o_ref[0:PAD_ROWS, :] = jnp.zeros((PAD_ROWS, D_MODEL), F32)
            o_ref[PAD_ROWS:CHUNK, :] = meta_ref[...]

        @pl.when(i > 0)
        def _():
            o_ref[...] = x_ref[...]

    return pl.pallas_call(
        body, name="embed", grid=(N_CHUNKS,),
        in_specs=[pl.BlockSpec((CHUNK, D_MODEL), lambda i: (jnp.maximum(i - 1, 0), 0)),
                  pl.BlockSpec((N_META, D_MODEL), lambda i: (0, 0))] + [pl.BlockSpec(memory_space=pl.ANY)] * len(behind),
        out_specs=pl.BlockSpec((CHUNK, D_MODEL), lambda i: (i, 0)),
        out_shape=jax.ShapeDtypeStruct((T_ROWS, D_MODEL), F32),
        compiler_params=_cparams("parallel"),
    )(x, meta, *behind)


def _rmsnorm(h, w, *, name, tm=544):
    def body(h_ref, w_ref, o_ref):
        o_ref[...] = _rms_fwd(h_ref[...], w_ref[...]).astype(_MXU)

    return pl.pallas_call(
        body, name=name, grid=(T_ROWS // tm,),
        in_specs=[pl.BlockSpec((tm, D_MODEL), lambda i: (i, 0)), pl.BlockSpec((1, D_MODEL), lambda i: (0, 0))],
        out_specs=pl.BlockSpec((tm, D_MODEL), lambda i: (i, 0)),
        out_shape=jax.ShapeDtypeStruct((T_ROWS, D_MODEL), _MXU),
        compiler_params=_cparams("parallel"),
    )(h, w)


def _norm_proj(h, w, sections, convs, *, name, tm=272):
    widths = [s.shape[0] for s in sections]
    n = len(sections)
    conv_ks = sorted(convs)
    nc = len(conv_ks)

    def body(*refs):
        h_ref, w_ref = refs[:2]
        sec_refs = refs[2:2 + n]
        cw_refs = refs[2 + n:2 + n + nc]
        cb_refs = refs[2 + n + nc:2 + n + 2 * nc]
        u_ref = refs[2 + n + 2 * nc]
        proj_refs = refs[3 + n + 2 * nc:3 + 2 * n + 2 * nc]
        act_refs = refs[3 + 2 * n + 2 * nc:3 + 2 * n + 3 * nc]
        halo_refs = refs[3 + 2 * n + 3 * nc:]
        i = pl.program_id(0)

        @pl.when(i == 0)
        def _():
            for hr in halo_refs:
                hr[...] = jnp.zeros_like(hr)

        u = _rms_fwd(h_ref[...], w_ref[...]).astype(_MXU)
        u_ref[...] = u
        real = (i * tm + lax.broadcasted_iota(jnp.int32, (tm, 1), 0) >= PAD_ROWS).astype(F32)
        for k in range(n):
            raw = lax.dot_general(u, sec_refs[k][...], NT, preferred_element_type=F32)
            proj_refs[k][...] = raw
            if k not in convs:
                continue
            q = conv_ks.index(k)
            wv, bv = cw_refs[q][...], cb_refs[q][...]
            cat = jnp.concatenate([halo_refs[q][...], raw], axis=0)
            pre = bv + raw * wv[CONV_K - 1:CONV_K]
            for s in range(1, CONV_K):
                pre = pre + pltpu.roll(cat, s, 0)[8:8 + tm] * wv[CONV_K - 1 - s:CONV_K - s]
            y = _silu(pre) if convs[k][2] else pre
            act_refs[q][...] = y * real
            halo_refs[q][...] = raw[tm - 8:tm]

    row = lambda width: pl.BlockSpec((tm, width), lambda i: (i, 0))
    whole = lambda a: pl.BlockSpec(a.shape, lambda i: (0, 0))
    cws = [convs[k][0] for k in conv_ks]
    cbs = [convs[k][1] for k in conv_ks]
    outs = pl.pallas_call(
        body, name=name, grid=(T_ROWS // tm,),
        in_specs=[row(D_MODEL), pl.BlockSpec((1, D_MODEL), lambda i: (0, 0))]
        + [pl.BlockSpec((wd, D_MODEL), lambda i: (0, 0)) for wd in widths]
        + [whole(a) for a in cws] + [whole(a) for a in cbs],
        out_specs=[row(D_MODEL)] + [row(wd) for wd in widths] + [row(widths[k]) for k in conv_ks],
        out_shape=[jax.ShapeDtypeStruct((T_ROWS, D_MODEL), _MXU)]
        + [jax.ShapeDtypeStruct((T_ROWS, wd), F32) for wd in widths]
        + [jax.ShapeDtypeStruct((T_ROWS, widths[k]), F32) for k in conv_ks],
        scratch_shapes=[pltpu.VMEM((8, widths[k]), F32) for k in conv_ks],
        compiler_params=_cparams("arbitrary"),
    )(h, w, *sections, *cws, *cbs)
    return outs[0], list(outs[1:1 + n]), dict(zip(conv_ks, outs[1 + n:]))


def _loss_head(h2, target, fw):
    def body(h_ref, t_ref, w_ref, loss_ref, dh_ref, dhb_ref, dw_ref, acc_ref):
        i = pl.program_id(0)

        @pl.when(i == 0)
        def _():
            acc_ref[...] = jnp.zeros_like(acc_ref)
            dw_ref[...] = jnp.zeros_like(dw_ref)

        h = h_ref[...]
        w = w_ref[...]
        y = _rms_fwd(h, w)
        live = (i > 0).astype(F32)
        err = (y - t_ref[...]) * live
        acc_ref[...] += jnp.sum(err * err, axis=0, keepdims=True)
        dy = err * (1.0 / D_MODEL)
        dx, dwr = _rms_bwd(h, w, dy)
        dh_ref[...] = dx
        dhb_ref[...] = dx.astype(_MXU)
        dw_ref[...] += jnp.sum(dwr, axis=0, keepdims=True)

        @pl.when(i == N_CHUNKS - 1)
        def _():
            tot = jnp.sum(acc_ref[...], axis=1, keepdims=True) * (0.5 / D_MODEL)
            loss_ref[...] = jnp.broadcast_to(tot, (1, 128))

    return pl.pallas_call(
        body, name="loss_head", grid=(N_CHUNKS,),
        in_specs=[pl.BlockSpec((CHUNK, D_MODEL), lambda i: (i, 0)),
                  pl.BlockSpec((CHUNK, D_MODEL), lambda i: (jnp.maximum(i - 1, 0), 0)),
                  pl.BlockSpec((1, D_MODEL), lambda i: (0, 0))],
        out_specs=[pl.BlockSpec((1, 128), lambda i: (0, 0)),
                   pl.BlockSpec((CHUNK, D_MODEL), lambda i: (i, 0)),
                   pl.BlockSpec((CHUNK, D_MODEL), lambda i: (i, 0)),
                   pl.BlockSpec((1, D_MODEL), lambda i: (0, 0))],
        out_shape=[jax.ShapeDtypeStruct((1, 128), F32),
                   jax.ShapeDtypeStruct((T_ROWS, D_MODEL), F32),
                   jax.ShapeDtypeStruct((T_ROWS, D_MODEL), _MXU),
                   jax.ShapeDtypeStruct((1, D_MODEL), F32)],
        scratch_shapes=[pltpu.VMEM((1, D_MODEL), F32)],
        compiler_params=_cparams("arbitrary"),
    )(h2, target, fw)


def _mm_norm_bwd(terms, h, w, dres, *, name, tm=272, behind=()):
    n_terms = len(terms)
    in_specs, args = [], []
    for (a, b, k) in terms:
        in_specs += [pl.BlockSpec((tm, k), lambda i: (i, 0)), pl.BlockSpec((k, D_MODEL), lambda i: (0, 0))]
        args += [a, b]
    in_specs += [pl.BlockSpec((tm, D_MODEL), lambda i: (i, 0)), pl.BlockSpec((1, D_MODEL), lambda i: (0, 0)),
                 pl.BlockSpec((tm, D_MODEL), lambda i: (i, 0))] + [pl.BlockSpec(memory_space=pl.ANY)] * len(behind)
    args += [h, w, dres, *behind]

    def body(*refs):
        h_ref, w_ref, dres_ref = refs[2 * n_terms:2 * n_terms + 3]
        dh_ref, dhb_ref, dw_ref = refs[2 * n_terms + 3 + len(behind):]

        @pl.when(pl.program_id(0) == 0)
        def _():
            dw_ref[...] = jnp.zeros_like(dw_ref)

        du = None
        for t in range(n_terms):
            d = lax.dot_general(refs[2 * t][...], refs[2 * t + 1][...], NN, preferred_element_type=F32)
            du = d if du is None else du + d
        dx, dwr = _rms_bwd(h_ref[...], w_ref[...], du)
        dh = dres_ref[...] + dx
        dh_ref[...] = dh
        dhb_ref[...] = dh.astype(_MXU)
        dw_ref[...] += jnp.sum(dwr, axis=0, keepdims=True)

    return pl.pallas_call(
        body, name=name, grid=(T_ROWS // tm,), in_specs=in_specs,
        out_specs=[pl.BlockSpec((tm, D_MODEL), lambda i: (i, 0)), pl.BlockSpec((tm, D_MODEL), lambda i: (i, 0)),
                   pl.BlockSpec((1, D_MODEL), lambda i: (0, 0))],
        out_shape=[jax.ShapeDtypeStruct((T_ROWS, D_MODEL), F32), jax.ShapeDtypeStruct((T_ROWS, D_MODEL), _MXU),
                   jax.ShapeDtypeStruct((1, D_MODEL), F32)],
        compiler_params=_cparams("arbitrary"),
    )(*args)


FFN_TM = T_ROWS
FFN_TN = 256


def _ffn_up(u2, wg_t, wu_t):
    def body(u_ref, wg_ref, wu_ref, gp_ref, up_ref, act_ref):
        u = u_ref[...]
        gp = lax.dot_general(u, wg_ref[...], NT, preferred_element_type=F32)
        up = lax.dot_general(u, wu_ref[...], NT, preferred_element_type=F32)
        gp_ref[...] = gp.astype(_MXU)
        up_ref[...] = up.astype(_MXU)
        act_ref[...] = (_silu(gp) * up).astype(_MXU)

    tile = pl.BlockSpec((FFN_TM, FFN_TN), lambda j, i: (i, j))
    return pl.pallas_call(
        body, name="ffn_up", grid=(D_FF // FFN_TN, T_ROWS // FFN_TM),
        in_specs=[pl.BlockSpec((FFN_TM, D_MODEL), lambda j, i: (i, 0)),
                  pl.BlockSpec((FFN_TN, D_MODEL), lambda j, i: (j, 0)),
                  pl.BlockSpec((FFN_TN, D_MODEL), lambda j, i: (j, 0))],
        out_specs=[tile, tile, tile],
        out_shape=[jax.ShapeDtypeStruct((T_ROWS, D_FF), _MXU)] * 3,
        compiler_params=_cparams("parallel", "parallel"),
    )(u2, wg_t, wu_t)


def _ffn_bwd_act(dh2b, wd, gp, up):
    def body(dh_ref, wd_ref, gp_ref, up_ref, dgp_ref, dup_ref):
        dact = lax.dot_general(dh_ref[...], wd_ref[...], NT, preferred_element_type=F32)
        gp = gp_ref[...].astype(F32)
        dgp_ref[...] = (dact * up_ref[...].astype(F32) * _silu_grad(gp)).astype(_MXU)
        dup_ref[...] = (dact * _silu(gp)).astype(_MXU)

    tile = pl.BlockSpec((FFN_TM, FFN_TN), lambda j, i: (i, j))
    return pl.pallas_call(
        body, name="ffn_bwd_act", grid=(D_FF // FFN_TN, T_ROWS // FFN_TM),
        in_specs=[pl.BlockSpec((FFN_TM, D_MODEL), lambda j, i: (i, 0)),
                  pl.BlockSpec((FFN_TN, D_MODEL), lambda j, i: (j, 0)), tile, tile],
        out_specs=[tile, tile],
        out_shape=[jax.ShapeDtypeStruct((T_ROWS, D_FF), _MXU), jax.ShapeDtypeStruct((T_ROWS, D_FF), _MXU)],
        compiler_params=_cparams("parallel", "parallel"),
    )(dh2b, wd, gp, up)


CONV_TC = 512
CONV_K = 4


def _conv_pre(x_ref, wv, bv, c):
    tc = wv.shape[1]
    r0 = c * CHUNK
    cur = x_ref[r0:r0 + CHUNK, :]
    if c == 0:
        cat = jnp.concatenate([jnp.zeros((8, tc), F32), cur], axis=0)
        shifted = [cur] + [pltpu.roll(cat, s, 0)[8:8 + CHUNK] for s in range(1, CONV_K)]
    else:
        shifted = [cur] + [x_ref[r0 - s:r0 - s + CHUNK, :] for s in range(1, CONV_K)]
    pre = bv
    for s in range(CONV_K):
        pre = pre + shifted[s] * wv[CONV_K - 1 - s:CONV_K - s]
    return pre, shifted


def _row_mask(c):
    if c > 0:
        return None
    return (lax.broadcasted_iota(jnp.int32, (CHUNK, 1), 0) >= PAD_ROWS).astype(F32)


def _conv_bwd(dy, x, w, b, *, silu, name):
    cols = x.shape[1]
    tc = min(CONV_TC, cols)

    def body(dy_ref, x_ref, w_ref, b_ref, dx_ref, dw_ref, db_ref):
        wv, bv = w_ref[...], b_ref[...]
        next8 = jnp.zeros((8, tc), F32)
        dws = [jnp.zeros((1, tc), F32) for _ in range(CONV_K)]
        db = jnp.zeros((1, tc), F32)
        for c in reversed(range(N_CHUNKS)):
            r0 = c * CHUNK
            pre, shifted = _conv_pre(x_ref, wv, bv, c)
            dpre = dy_ref[r0:r0 + CHUNK, :]
            if silu:
                dpre = dpre * _silu_grad(pre)
            mask = _row_mask(c)
            if mask is not None:
                dpre = dpre * mask
            cat = jnp.concatenate([dpre, next8], axis=0)
            dx = dpre * wv[CONV_K - 1:CONV_K]
            for s in range(1, CONV_K):
                dx = dx + pltpu.roll(cat, CHUNK + 8 - s, 0)[0:CHUNK] * wv[CONV_K - 1 - s:CONV_K - s]
            dx_ref[r0:r0 + CHUNK, :] = dx.astype(_MXU)
            for s in range(CONV_K):
                k = CONV_K - 1 - s
                dws[k] = dws[k] + jnp.sum(dpre * shifted[s], axis=0, keepdims=True)
            db = db + jnp.sum(dpre, axis=0, keepdims=True)
            next8 = dpre[0:8]
        dw_ref[...] = jnp.concatenate(dws, axis=0)
        db_ref[...] = db

    return pl.pallas_call(
        body, name=name, grid=(cols // tc,),
        in_specs=[pl.BlockSpec((T_ROWS, tc), lambda j: (0, j)), pl.BlockSpec((T_ROWS, tc), lambda j: (0, j)),
                  pl.BlockSpec((CONV_K, tc), lambda j: (0, j)), pl.BlockSpec((1, tc), lambda j: (0, j))],
        out_specs=[pl.BlockSpec((T_ROWS, tc), lambda j: (0, j)), pl.BlockSpec((CONV_K, tc), lambda j: (0, j)),
                   pl.BlockSpec((1, tc), lambda j: (0, j))],
        out_shape=[jax.ShapeDtypeStruct((T_ROWS, cols), _MXU), jax.ShapeDtypeStruct((CONV_K, cols), F32),
                   jax.ShapeDtypeStruct((1, cols), F32)],
        compiler_params=_cparams("parallel"),
    )(dy, x, w, b)


def _ssd_chunk_common(dt_raw, prm, c):
    a_row = -jnp.exp(prm[1:2])
    dt = _softplus(dt_raw + prm[0:1])
    rows = lax.broadcasted_iota(jnp.int32, (CHUNK, 1), 0)
    real = jnp.logical_or(c > 0, rows >= PAD_ROWS)
    dt = jnp.where(real, dt, 0.0)
    li = lax.broadcasted_iota(jnp.int32, (CHUNK, CHUNK), 0)
    si = lax.broadcasted_iota(jnp.int32, (CHUNK, CHUNK), 1)
    causal = li >= si
    tri = causal.astype(F32)
    cs = _dot_onehot(tri, dt * a_row, data=1)
    return dt, a_row, cs, cs.T, causal, tri, real


def _gated_norm_fwd(y, z, w):
    g = y * _silu(z)
    half = SSD_WIDTH // SSD_GROUPS
    outs = [_rms_fwd(g[:, k * half:(k + 1) * half], w[:, k * half:(k + 1) * half]) for k in range(SSD_GROUPS)]
    return jnp.concatenate(outs, axis=1)


GROUP_W = SSD_WIDTH // SSD_GROUPS
PAIR_W = 2 * SSD_HEAD_DIM
STATE_SHAPE = (SSD_GROUPS, SSD_STATE, GROUP_W)


def _head_expander():
    r = lax.broadcasted_iota(jnp.int32, (128, SSD_WIDTH), 0)
    c = lax.broadcasted_iota(jnp.int32, (128, SSD_WIDTH), 1)
    return (c // SSD_HEAD_DIM == r).astype(F32)


def _ssd_expand(dt, cs, prm, ex):
    cs_x = _dot_onehot(cs, ex)
    cs_last_x = cs_x[CHUNK - 1:CHUNK, :]
    return (_dot_onehot(dt, ex, pieces=2), _dot_onehot(prm, ex)[2:3], jnp.exp(cs_x), jnp.exp(cs_last_x),
            jnp.exp(cs_last_x - cs_x))


def _ssd_fwd(xs, bc, dt_raw, z, prm, norm_w, ex):
    def body(xs_ref, bc_ref, dt_ref, z_ref, prm_ref, nw_ref, ex_ref, y_ref, yn_ref, prev_ref, state):
        c = pl.program_id(0)

        @pl.when(c == 0)
        def _():
            state[...] = jnp.zeros_like(state)

        prm = prm_ref[...]
        dt, a_row, cs, cs_t, causal, _, _ = _ssd_chunk_common(dt_ref[...], prm, c)
        dt_x, d_x, e_cs_x, e_last_x, dec_x = _ssd_expand(dt, cs, prm, ex_ref[...])
        xs_all = xs_ref[...]
        bc_all = bc_ref[...]
        xdt = xs_all * dt_x
        xdec = xdt * dec_x
        lane_lo = lax.broadcasted_iota(jnp.int32, (1, PAIR_W), 1) < SSD_HEAD_DIM
        for g in range(SSD_GROUPS):
            gs = slice(g * GROUP_W, (g + 1) * GROUP_W)
            b_g = bc_all[:, g * SSD_STATE:(g + 1) * SSD_STATE]
            c_g = bc_all[:, (SSD_GROUPS + g) * SSD_STATE:(SSD_GROUPS + g + 1) * SSD_STATE]
            st = state[g]
            prev_ref[0, g] = st
            y_off = _dot(c_g, st) * e_cs_x[:, gs]
            state[g] = st * e_last_x[:, gs] + _dot(b_g.T, xdec[:, gs])
            cb = _dot(c_g, b_g, NT)
            for k in range(SSD_HPG // 2):
                h0 = g * SSD_HPG + 2 * k
                ps = slice(h0 * SSD_HEAD_DIM, h0 * SSD_HEAD_DIM + PAIR_W)
                xdt_pair = xdt[:, ps]
                yd = []
                for h in (h0, h0 + 1):
                    lmat = jnp.where(causal, jnp.exp(cs[:, h:h + 1] - cs_t[h:h + 1, :]), 0.0)
                    yd.append(_dot(cb * lmat, xdt_pair))
                y_ref[:, ps] = (jnp.where(lane_lo, yd[0], yd[1]) + y_off[:, k * PAIR_W:(k + 1) * PAIR_W]
                                + xs_all[:, ps] * d_x[:, ps])
        yn_ref[...] = _gated_norm_fwd(y_ref[...], z_ref[...], nw_ref[...]).astype(_MXU)

    row = lambda w: pl.BlockSpec((CHUNK, w), lambda c: (c, 0))
    return pl.pallas_call(
        body, name="ssd_fwd", grid=(N_CHUNKS,),
        in_specs=[row(SSD_WIDTH), row(512), row(128), row(SSD_WIDTH),
                  pl.BlockSpec((8, 128), lambda c: (0, 0)), pl.BlockSpec((1, SSD_WIDTH), lambda c: (0, 0)),
                  pl.BlockSpec((128, SSD_WIDTH), lambda c: (0, 0))],
        out_specs=[row(SSD_WIDTH), row(SSD_WIDTH),
                   pl.BlockSpec((1,) + STATE_SHAPE, lambda c: (c, 0, 0, 0))],
        out_shape=[jax.ShapeDtypeStruct((T_ROWS, SSD_WIDTH), F32), jax.ShapeDtypeStruct((T_ROWS, SSD_WIDTH), _MXU),
                   jax.ShapeDtypeStruct((N_CHUNKS,) + STATE_SHAPE, F32)],
        scratch_shapes=[pltpu.VMEM(STATE_SHAPE, F32)],
        compiler_params=_cparams("arbitrary"),
    )(xs, bc, dt_raw, z, prm, norm_w, ex)


def _ssd_bwd(dyn, dyn_block, z, y_pre, xs, bc, dt_raw, prev, prm, norm_w, ex):
    def body(dyn_ref, z_ref, y_ref, xs_ref, bc_ref, dt_ref, prev_ref, prm_ref, nw_ref, ex_ref,
             dz_ref, dxs_ref, dbc_ref, ddt_ref, dprm_ref, dnw_ref, dstate):
        step = pl.program_id(0)
        c = N_CHUNKS - 1 - step

        @pl.when(step == 0)
        def _():
            dstate[...] = jnp.zeros_like(dstate)
            dprm_ref[...] = jnp.zeros_like(dprm_ref)
            dnw_ref[...] = jnp.zeros_like(dnw_ref)

        prm = prm_ref[...]
        dt, a_row, cs, cs_t, causal, tri, real = _ssd_chunk_common(dt_ref[...], prm, c)
        realf = real.astype(F32)
        z = z_ref[...]
        y_all = y_ref[...]
        nw = nw_ref[...]
        dyn_all = dyn_ref[...]
        sz = _silu(z)
        gated = y_all * sz
        half = SSD_WIDTH // SSD_GROUPS
        dgs, dnws = [], []
        for k in range(SSD_GROUPS):
            sl = slice(k * half, (k + 1) * half)
            dgk, dwk = _rms_bwd(gated[:, sl], nw[:, sl], dyn_all[:, sl])
            dgs.append(dgk)
            dnws.append(jnp.sum(dwk, axis=0, keepdims=True))
        dgated = jnp.concatenate(dgs, axis=1)
        dnw_ref[...] += jnp.concatenate(dnws, axis=1)
        dz_ref[...] = (dgated * y_all * _silu_grad(z)).astype(_MXU)
        dy_all = dgated * sz

        ex = ex_ref[...]
        dt_x, d_x, e_cs_x, e_last_x, dec_x = _ssd_expand(dt, cs, prm, ex)
        xs_all = xs_ref[...]
        bc_all = bc_ref[...]
        xdt = xs_all * dt_x
        xdt_mxu = xdt.astype(_MXU).astype(F32)
        xdec = xdt * dec_x
        dcp = dy_all * e_cs_x
        lane_lo = lax.broadcasted_iota(jnp.int32, (1, PAIR_W), 1) < SSD_HEAD_DIM
        upper = (lax.broadcasted_iota(jnp.int32, (CHUNK, CHUNK), 0)
                 <= lax.broadcasted_iota(jnp.int32, (CHUNK, CHUNK), 1))
        last_row = (lax.broadcasted_iota(jnp.int32, (CHUNK, 1), 0) == CHUNK - 1).astype(F32)
        dbs, dcs_, dxdt_parts, last_parts = [], [], [], []
        for g in range(SSD_GROUPS):
            gs = slice(g * GROUP_W, (g + 1) * GROUP_W)
            b_g = bc_all[:, g * SSD_STATE:(g + 1) * SSD_STATE]
            c_g = bc_all[:, (SSD_GROUPS + g) * SSD_STATE:(SSD_GROUPS + g + 1) * SSD_STATE]
            prev_t = prev_ref[0, g]
            dst = dstate[g]
            dc_g = _dot(dcp[:, gs], prev_t, NT)
            db_g = _dot(xdec[:, gs], dst, NT)
            dxdt_state = _dot(b_g, dst) * dec_x[:, gs]
            dstate[g] = dst * e_last_x[:, gs] + _dot(c_g.T, dcp[:, gs])
            last_parts.append(jnp.sum(xdt_mxu[:, gs] * dxdt_state, axis=0, keepdims=True)
                              + jnp.sum(dst * prev_t, axis=0, keepdims=True) * e_last_x[:, gs])
            cb_t = _dot(b_g, c_g, NT)
            dcb_t = jnp.zeros((CHUNK, CHUNK), F32)
            for k in range(SSD_HPG // 2):
                h0 = g * SSD_HPG + 2 * k
                ps = slice(h0 * SSD_HEAD_DIM, h0 * SSD_HEAD_DIM + PAIR_W)
                dy_pair = dy_all[:, ps]
                xdt_pair = xdt[:, ps]
                dd = []
                for h in (h0, h0 + 1):
                    lmat_t = jnp.where(upper, jnp.exp(cs_t[h:h + 1, :] - cs[:, h:h + 1]), 0.0)
                    dd.append(_dot(cb_t * lmat_t, dy_pair))
                    mine = lane_lo if h == h0 else jnp.logical_not(lane_lo)
                    dcb_t = dcb_t + _dot(jnp.where(mine, xdt_pair, 0.0), dy_pair, NT) * lmat_t
                dxdt_parts.append(jnp.where(lane_lo, dd[0], dd[1]) + dxdt_state[:, k * PAIR_W:(k + 1) * PAIR_W])
            dc_g = dc_g + _dot(dcb_t, b_g, TN)
            db_g = db_g + _dot(dcb_t, c_g)
            dbs.append(db_g * realf)
            dcs_.append(dc_g * realf)
        dbc_ref[...] = jnp.concatenate(dbs + dcs_, axis=1)
        dxdt = jnp.concatenate(dxdt_parts, axis=1)
        dxs_ref[...] = (dxdt * dt_x + dy_all * d_x) * realf
        ddt_all = _dot_onehot(dxdt * xs_all, ex, NT, pieces=2)
        rows = jnp.concatenate([jnp.concatenate(last_parts, axis=1), jnp.sum(dy_all * xs_all, axis=0, keepdims=True),
                                jnp.zeros((6, SSD_WIDTH), F32)], axis=0)
        rows = _dot_onehot(rows, ex, NT, pieces=2)
        dd_row = rows[1:2]
        dy_mxu = dy_all.astype(_MXU).astype(F32)
        dcs_all = (_dot_onehot(dy_mxu * (y_all - xs_all * d_x), ex, NT) - _dot_onehot(xdt_mxu * dxdt, ex, NT)
                   + last_row * rows[0:1])
        dda = _dot_onehot(tri, dcs_all, TN, data=1)
        ddt = (ddt_all + dda * a_row) * realf
        ddt_raw = ddt * _sigmoid(dt_ref[...] + prm[0:1])
        ddt_ref[...] = ddt_raw.astype(_MXU)
        da_log = jnp.sum(dda * dt, axis=0, keepdims=True) * a_row
        dprm_ref[0:1, :] += jnp.sum(ddt_raw, axis=0, keepdims=True)
        dprm_ref[1:2, :] += da_log
        dprm_ref[2:3, :] += dd_row

    rev = lambda w, blk=0: pl.BlockSpec((CHUNK, w), lambda s, blk=blk: (N_CHUNKS - 1 - s, blk))
    return pl.pallas_call(
        body, name="ssd_bwd", grid=(N_CHUNKS,),
        in_specs=[rev(SSD_WIDTH, dyn_block), rev(SSD_WIDTH), rev(SSD_WIDTH), rev(SSD_WIDTH), rev(512), rev(128),
                  pl.BlockSpec((1,) + STATE_SHAPE, lambda s: (N_CHUNKS - 1 - s, 0, 0, 0)),
                  pl.BlockSpec((8, 128), lambda s: (0, 0)), pl.BlockSpec((1, SSD_WIDTH), lambda s: (0, 0)),
                  pl.BlockSpec((128, SSD_WIDTH), lambda s: (0, 0))],
        out_specs=[rev(SSD_WIDTH), rev(SSD_WIDTH), rev(512), rev(128),
                   pl.BlockSpec((8, 128), lambda s: (0, 0)), pl.BlockSpec((1, SSD_WIDTH), lambda s: (0, 0))],
        out_shape=[jax.ShapeDtypeStruct((T_ROWS, SSD_WIDTH), _MXU), jax.ShapeDtypeStruct((T_ROWS, SSD_WIDTH), F32),
                   jax.ShapeDtypeStruct((T_ROWS, 512), F32), jax.ShapeDtypeStruct((T_ROWS, 128), _MXU),
                   jax.ShapeDtypeStruct((8, 128), F32), jax.ShapeDtypeStruct((1, SSD_WIDTH), F32)],
        scratch_shapes=[pltpu.VMEM(STATE_SHAPE, F32)],
        compiler_params=_cparams("arbitrary"),
    )(dyn, z, y_pre, xs, bc, dt_raw, prev, prm, norm_w, ex)


LRU_PAIRS = 8


def _lru_gates(xr, wa_ref, wx_ref, prm):
    pre_r, pre_i = [], []
    for k in range(LRU_PAIRS):
        xk = xr[:, k * 128:(k + 1) * 128]
        pre_r.append(_dot(xk, wa_ref[k]))
        pre_i.append(_dot(xk, wx_ref[k]))
    r = _sigmoid(jnp.concatenate(pre_r, axis=1) + prm[0:1])
    i = _sigmoid(jnp.concatenate(pre_i, axis=1) + prm[1:2])
    sp = _softplus(-prm[2:3])
    log_a = (-LRU_C) * r * sp
    a = jnp.exp(log_a)
    s = jnp.sqrt(-jnp.tanh(log_a) * (a * a + 1.0))
    return r, i, a, s, sp


def _lru_fwd(xr, gate, wa, wx, prm):
    def body(xr_ref, g_ref, wa_ref, wx_ref, prm_ref, hs_ref, yn_ref, carry, a_s, u_s):
        @pl.when(pl.program_id(0) == 0)
        def _():
            carry[...] = jnp.zeros_like(carry)

        prm = prm_ref[...]
        xr_t = xr_ref[...]
        _, i, a, s, _ = _lru_gates(xr_t, wa_ref, wx_ref, prm)
        a_s[...] = a
        u_s[...] = s * (i * xr_t)
        rid = lax.broadcasted_iota(jnp.int32, (8, LRU_WIDTH), 0)

        def group(k, before):
            off = pl.multiple_of(k * 8, 8)
            a8 = a_s[pl.ds(off, 8), :]
            u8 = u_s[pl.ds(off, 8), :]
            for d in (1, 2, 4):
                keep = rid >= d
                u8 = u8 + a8 * jnp.where(keep, pltpu.roll(u8, d, 0), 0.0)
                a8 = a8 * jnp.where(keep, pltpu.roll(a8, d, 0), 1.0)
            h8 = u8 + a8 * before
            hs_ref[pl.ds(off, 8), :] = h8
            return jnp.broadcast_to(h8[7:8], (8, LRU_WIDTH))

        carry[...] = lax.fori_loop(0, CHUNK // 8, group, carry[...])
        gel, _ = _gelu_and_grad(g_ref[...])
        yn_ref[...] = _rms_fwd(gel * hs_ref[...], prm[3:4]).astype(_MXU)

    row = pl.BlockSpec((CHUNK, LRU_WIDTH), lambda t: (t, 0))
    wspec = pl.BlockSpec((LRU_PAIRS, 128, 128), lambda t: (0, 0, 0))
    return pl.pallas_call(
        body, name="lru_fwd", grid=(N_CHUNKS,),
        in_specs=[row, row, wspec, wspec, pl.BlockSpec((8, LRU_WIDTH), lambda t: (0, 0))],
        out_specs=[row, row],
        out_shape=[jax.ShapeDtypeStruct((T_ROWS, LRU_WIDTH), F32), jax.ShapeDtypeStruct((T_ROWS, LRU_WIDTH), _MXU)],
        scratch_shapes=[pltpu.VMEM((8, LRU_WIDTH), F32), pltpu.VMEM((CHUNK, LRU_WIDTH), F32),
                        pltpu.VMEM((CHUNK, LRU_WIDTH), F32)],
        compiler_params=_cparams("arbitrary"),
    )(xr, gate, wa, wx, prm)


def _lru_bwd(dyn, dyn_block, gate, xr, hs, wa, wx, wa_t, wx_t, prm):
    def body(dyn_ref, g_ref, xr_ref, hs_ref, hsp_ref, wa_ref, wx_ref, wat_ref, wxt_ref, prm_ref,
             dg_ref, dxr_ref, dwa_ref, dwx_ref, dprm_ref, carry, a_s, d_s):
        step = pl.program_id(0)
        tile = N_CHUNKS - 1 - step

        @pl.when(step == 0)
        def _():
            carry[...] = jnp.zeros_like(carry)
            dwa_ref[...] = jnp.zeros_like(dwa_ref)
            dwx_ref[...] = jnp.zeros_like(dwx_ref)
            dprm_ref[...] = jnp.zeros_like(dprm_ref)

        prm = prm_ref[...]
        xr_t = xr_ref[...]
        r, i, a, s, sp = _lru_gates(xr_t, wa_ref, wx_ref, prm)
        hs_t = hs_ref[...]
        gel, dgel = _gelu_and_grad(g_ref[...])
        dy, dnw = _rms_bwd(gel * hs_t, prm[3:4], dyn_ref[...])
        dg_ref[...] = (dy * hs_t * dgel).astype(_MXU)
        a_s[...] = a
        d_s[...] = dy * gel
        rid = lax.broadcasted_iota(jnp.int32, (8, LRU_WIDTH), 0)

        def group(k, behind):
            off = pl.multiple_of((CHUNK // 8 - 1 - k) * 8, 8)
            a8 = a_s[pl.ds(off, 8), :]
            d8 = d_s[pl.ds(off, 8), :]
            c8 = jnp.where(rid == 7, 1.0, pltpu.roll(a8, 7, 0))
            for d in (1, 2, 4):
                keep = rid < 8 - d
                d8 = d8 + c8 * jnp.where(keep, pltpu.roll(d8, 8 - d, 0), 0.0)
                c8 = c8 * jnp.where(keep, pltpu.roll(c8, 8 - d, 0), 1.0)
            dht8 = d8 + c8 * behind
            d_s[pl.ds(off, 8), :] = dht8
            return jnp.broadcast_to(a8[0:1] * dht8[0:1], (8, LRU_WIDTH))

        carry[...] = lax.fori_loop(0, CHUNK // 8, group, carry[...])
        dht = d_s[...]
        before = hsp_ref[CHUNK - 8:CHUNK, :][7:8] * (tile > 0).astype(F32)
        first = lax.broadcasted_iota(jnp.int32, (CHUNK, 1), 0) == 0
        hprev = jnp.where(first, before, pltpu.roll(hs_t, 1, 0))
        da = dht * hprev
        ixr = i * xr_t
        ds = dht * ixr
        dlog_a = da * a - ds * (a * a) * lax.rsqrt(s * s)
        dr = dlog_a * ((-LRU_C) * sp)
        dsp = jnp.sum(dlog_a * ((-LRU_C) * r), axis=0, keepdims=True)
        dlam = dsp * (-_sigmoid(-prm[2:3]))
        di = dht * s * xr_t
        dpre_r = dr * r * (1.0 - r)
        dpre_i = di * i * (1.0 - i)
        dxr = dht * s * i
        parts = []
        for k in range(LRU_PAIRS):
            sl = slice(k * 128, (k + 1) * 128)
            parts.append(_dot(dpre_r[:, sl], wat_ref[k]) + _dot(dpre_i[:, sl], wxt_ref[k]))
            dwa_ref[k] += _dot(xr_t[:, sl], dpre_r[:, sl], TN)
            dwx_ref[k] += _dot(xr_t[:, sl], dpre_i[:, sl], TN)
        dxr_ref[...] = dxr + jnp.concatenate(parts, axis=1)
        dprm_ref[0:1, :] += jnp.sum(dpre_r, axis=0, keepdims=True)
        dprm_ref[1:2, :] += jnp.sum(dpre_i, axis=0, keepdims=True)
        dprm_ref[2:3, :] += dlam
        dprm_ref[3:4, :] += jnp.sum(dnw, axis=0, keepdims=True)

    rev = lambda blk=0: pl.BlockSpec((CHUNK, LRU_WIDTH), lambda s, blk=blk: (N_CHUNKS - 1 - s, blk))
    wspec = pl.BlockSpec((LRU_PAIRS, 128, 128), lambda s: (0, 0, 0))
    return pl.pallas_call(
        body, name="lru_bwd", grid=(N_CHUNKS,),
        in_specs=[rev(dyn_block), rev(), rev(), rev(),
                  pl.BlockSpec((CHUNK, LRU_WIDTH), lambda s: (jnp.maximum(N_CHUNKS - 2 - s, 0), 0)),
                  wspec, wspec, wspec, wspec, pl.BlockSpec((8, LRU_WIDTH), lambda s: (0, 0))],
        out_specs=[rev(), rev(), wspec, wspec, pl.BlockSpec((8, LRU_WIDTH), lambda s: (0, 0))],
        out_shape=[jax.ShapeDtypeStruct((T_ROWS, LRU_WIDTH), _MXU), jax.ShapeDtypeStruct((T_ROWS, LRU_WIDTH), F32),
                   jax.ShapeDtypeStruct((LRU_PAIRS, 128, 128), F32), jax.ShapeDtypeStruct((LRU_PAIRS, 128, 128), F32),
                   jax.ShapeDtypeStruct((8, LRU_WIDTH), F32)],
        scratch_shapes=[pltpu.VMEM((8, LRU_WIDTH), F32), pltpu.VMEM((CHUNK, LRU_WIDTH), F32),
                        pltpu.VMEM((CHUNK, LRU_WIDTH), F32)],
        compiler_params=_cparams("arbitrary"),
    )(dyn, gate, xr, hs, hs, wa, wx, wa_t, wx_t, prm)


SEC_NAMES = ("z", "xs", "bc", "dt", "g", "x")
SEC_WIDTH = {"z": 1024, "xs": 1024, "bc": 512, "dt": 128, "g": 1024, "x": 1024}


def _pair_blocks(w):
    w = w.reshape(LRU_PAIRS, 2, 64, 64)
    zero = jnp.zeros((LRU_PAIRS, 64, 64), w.dtype)
    top = jnp.concatenate([w[:, 0], zero], axis=2)
    bot = jnp.concatenate([zero, w[:, 1]], axis=2)
    return jnp.concatenate([top, bot], axis=1)


def _unpair_blocks(wp):
    return jnp.stack([wp[:, :64, :64], wp[:, 64:, 64:]], axis=1).reshape(16, 64, 64)


def _pad_lanes(v, width=128):
    return jnp.pad(v, ((0, 0), (0, width - v.shape[1])))


class _Resident:
    before_embed = ()
    behind_out_proj = ()

    def __init__(self, w_in_sections, w_out, w_gate, w_up, w_down):
        self._w_in, self._w_out, self._ffn = w_in_sections, w_out, (w_gate, w_up, w_down)

    def w_in(self, after):
        return self._w_in

    def mid_forward(self, after):
        return jnp.zeros((1, 1), F32)

    def w_out(self, after):
        return self._w_out

    def ffn(self, after):
        return self._ffn

    def grads_ready(self, names, g, g_mxu):
        return jnp.zeros((1, 1), F32)

    def small_ready(self, g, loss):
        return jnp.zeros((1, 1), F32)

    def small_middle(self, after):
        return jnp.zeros((1, 1), F32)


def _local_step(x, target, meta, p, late):
    g, g_mxu = {}, {}
    ex = _head_expander()
    h0 = _embed(x, meta, late.before_embed)
    w_in = late.w_in(h0)
    convs = {SEC_NAMES.index("xs"): (p["ssd_conv_w"][:, :SSD_WIDTH], p["ssd_conv_b"][:, :SSD_WIDTH], True),
             SEC_NAMES.index("bc"): (p["ssd_conv_w"][:, SSD_WIDTH:], p["ssd_conv_b"][:, SSD_WIDTH:], True),
             SEC_NAMES.index("x"): (p["lru_conv_w"], p["lru_conv_b"], False)}
    u1, projs, acts = _norm_proj(h0, p["norm1_w"], [w_in[s] for s in SEC_NAMES], convs, name="norm_in_proj")
    proj = dict(zip(SEC_NAMES, projs))
    xs_act, bc_act, xr = (acts[SEC_NAMES.index(s)] for s in ("xs", "bc", "x"))
    ssd_prm = jnp.concatenate([_pad_lanes(p["ssd_dt_bias"]), _pad_lanes(p["ssd_a_log"]), _pad_lanes(p["ssd_d"]),
                               jnp.zeros((5, 128), F32)], axis=0)
    y_pre, y_ssd, prev = _ssd_fwd(xs_act, bc_act, proj["dt"], proj["z"], ssd_prm, p["ssd_norm_w"], ex)
    wa_p, wx_p = _pair_blocks(p["lru_wa"]), _pair_blocks(p["lru_wx"])
    lru_prm = jnp.concatenate([p["lru_ba"], p["lru_bx"], p["lru_lambda"], p["lru_norm_w"],
                               jnp.zeros((4, LRU_WIDTH), F32)], axis=0)
    hs, y_lru = _lru_fwd(xr, proj["g"], wa_p.astype(_MXU), wx_p.astype(_MXU),
                         lru_prm + late.mid_forward([xr, y_ssd]))
    ycat = jnp.concatenate([y_ssd, y_lru], axis=1)
    w_out = late.w_out(ycat)
    h1 = _mm([(ycat, 0, w_out, 0, 2 * D_MODEL)], T_ROWS, D_MODEL, tm=T_ROWS, tn=256, mode="nn", out_dtype=F32,
             name="out_proj", residual=h0, behind=late.behind_out_proj)
    u2 = _rmsnorm(h1, p["norm2_w"], name="norm2")
    w_gate, w_up, w_down = late.ffn(u2)
    gp, up, act = _ffn_up(u2, w_gate, w_up)
    h2 = _mm([(act, 0, w_down, 0, D_FF)], T_ROWS, D_MODEL, tm=T_ROWS, tn=256, mode="nn", out_dtype=F32,
             name="ffn_down", residual=h1)
    loss, dh2, dh2b, g["final_norm_w"] = _loss_head(h2, target, p["final_norm_w"])
    dgp, dup = _ffn_bwd_act(dh2b, w_down, gp, up)
    g["w_down"], g_mxu["w_down"] = _mm([(act, 0, dh2b, 0, T_ROWS)], D_FF, D_MODEL, tm=1408, tn=512, mode="tn",
                                       out_dtype=F32, name="dw_down", also_mxu=True)
    dh1, dh1b, g["norm2_w"] = _mm_norm_bwd([(dgp, w_gate, D_FF), (dup, w_up, D_FF)], h1, p["norm2_w"], dh2,
                                           name="ffn_bwd_in")
    g["w_gate"], g_mxu["w_gate"] = _mm([(dgp, 0, u2, 0, T_ROWS)], D_FF, D_MODEL, tm=1408, tn=512, mode="tn",
                                       out_dtype=F32, name="dw_gate", also_mxu=True)
    g["w_up"], g_mxu["w_up"] = _mm([(dup, 0, u2, 0, T_ROWS)], D_FF, D_MODEL, tm=1408, tn=512, mode="tn",
                                   out_dtype=F32, name="dw_up", also_mxu=True)
    g["w_out"], g_mxu["w_out"] = _mm([(ycat, 0, dh1b, 0, T_ROWS)], 2 * D_MODEL, D_MODEL, tm=1024, tn=512, mode="tn",
                                     out_dtype=F32, name="dw_out", also_mxu=True)
    sent = late.grads_ready(("w_down", "w_gate", "w_up", "w_out"), g, g_mxu)
    dycat = _mm([(dh1b, 0, w_out, 0, D_MODEL)], T_ROWS, 2 * D_MODEL, tm=T_ROWS, tn=256, mode="nt", out_dtype=F32,
                name="out_proj_bwd", behind=(sent,))
    dgate, dxr, dwa_p, dwx_p, dlru_prm = _lru_bwd(dycat, 1, proj["g"], xr, hs, wa_p.astype(_MXU), wx_p.astype(_MXU),
                                                  jnp.swapaxes(wa_p, 1, 2).astype(_MXU),
                                                  jnp.swapaxes(wx_p, 1, 2).astype(_MXU), lru_prm)
    g["lru_wa"], g["lru_wx"] = _unpair_blocks(dwa_p), _unpair_blocks(dwx_p)
    g["lru_ba"], g["lru_bx"], g["lru_lambda"], g["lru_norm_w"] = (dlru_prm[k:k + 1] for k in range(4))
    dx_lru, g["lru_conv_w"], g["lru_conv_b"] = _conv_bwd(dxr, proj["x"], p["lru_conv_w"], p["lru_conv_b"], silu=False,
                                                         name="lru_conv_bwd")
    dz, dxs_act, dbc_act, ddt, dssd_prm, g["ssd_norm_w"] = _ssd_bwd(dycat, 0, proj["z"], y_pre, xs_act, bc_act,
                                                                    proj["dt"], prev, ssd_prm, p["ssd_norm_w"], ex)
    g["ssd_dt_bias"], g["ssd_a_log"], g["ssd_d"] = (dssd_prm[k:k + 1, :SSD_HEADS] for k in range(3))
    dxs, dcw_xs, dcb_xs = _conv_bwd(dxs_act, proj["xs"], p["ssd_conv_w"][:, :SSD_WIDTH],
                                    p["ssd_conv_b"][:, :SSD_WIDTH], silu=True, name="ssd_conv_xs_bwd")
    dbc, dcw_bc, dcb_bc = _conv_bwd(dbc_act, proj["bc"], p["ssd_conv_w"][:, SSD_WIDTH:],
                                    p["ssd_conv_b"][:, SSD_WIDTH:], silu=True, name="ssd_conv_bc_bwd")
    g["ssd_conv_w"] = jnp.concatenate([dcw_xs, dcw_bc], axis=1)
    g["ssd_conv_b"] = jnp.concatenate([dcb_xs, dcb_bc], axis=1)
    dproj = {"z": dz, "xs": dxs, "bc": dbc, "dt": ddt, "g": dgate, "x": dx_lru}
    for s in SEC_NAMES:
        wdt = SEC_WIDTH[s]
        g["w_in_" + s], g_mxu["w_in_" + s] = _mm([(dproj[s], 0, u1, 0, T_ROWS)], wdt, D_MODEL, tm=min(wdt, 1024),
                                                 tn=512, mode="tn", out_dtype=F32, name="dw_in_" + s, also_mxu=True)
    sent = late.grads_ready(("w_in",), g, g_mxu)
    dh0, _, g["norm1_w"] = _mm_norm_bwd([(dproj[s], w_in[s], SEC_WIDTH[s]) for s in SEC_NAMES], h0,
                                        p["norm1_w"], dh1, name="in_proj_bwd", behind=(sent,))
    g["meta_tokens"] = dh0[PAD_ROWS:X_ROW0]
    late.small_ready(g, loss)
    return loss, dh0[X_ROW0:], g, g_mxu


MESH = pl.DeviceIdType.MESH
ANY = pl.BlockSpec(memory_space=pl.ANY)


def _my_place():
    return lax.axis_index("x"), lax.axis_index("y"), lax.axis_index("c")


def _other_chips(x, y):
    return [(1 - x, y), (x, 1 - y), (1 - x, 1 - y)]


HBM_SPEC = pl.BlockSpec(memory_space=pltpu.HBM)
SEM_SPEC = pl.BlockSpec(memory_space=pltpu.SEMAPHORE)
SPLIT_EFFECT = pltpu.SideEffectType.DATAFLOW_SIDE_EFFECTING


def _half_cols(buf, c, other=False):
    half = buf.shape[-1] // 2
    return pl.ds(pl.multiple_of(((1 - c) if other else c) * half, 128), half)


def _halves_plan(bufs, x, y, c, incoming):
    plan = []
    for buf in bufs:
        cols = _half_cols(buf, c)
        for (px, py) in _other_chips(x, y):
            slot = 2 * px + py if incoming else 2 * x + y
            plan.append((buf.at[2 * x + y, :, cols], buf.at[slot, :, cols], (px, py, c)))
    return plan


def _forward_plan(bufs, x, y, c, incoming):
    plan = []
    for buf in bufs:
        for (px, py) in _other_chips(x, y):
            slot = 2 * px + py
            plan.append((buf.at[slot, :, _half_cols(buf, c)], buf.at[slot, :, _half_cols(buf, c, other=incoming)],
                         (x, y, 1 - c)))
    return plan


def _scatter_plan(bufs, x, y, c, incoming):
    n = len(bufs) // 2
    plan = []
    for k in range(n):
        for j, (px, py) in enumerate(_other_chips(x, y)):
            plan.append((bufs[k].at[2 * px + py], bufs[n + k].at[j], (px, py, c)))
    return plan


def _split_start(bufs, plan, n_copies, after, *, name):
    n = len(bufs)
    extra = [] if after is None else [after]

    def body(*refs):
        ins = refs[:n]
        send_sems, recv_sems = refs[n + len(extra)], refs[n + len(extra) + 1]
        token = refs[-1]
        x, y, c = _my_place()
        for i, (src, dst, dev) in enumerate(plan(ins, x, y, c, False)):
            pltpu.make_async_remote_copy(src_ref=src, dst_ref=dst, send_sem=send_sems.at[i], recv_sem=recv_sems.at[i],
                                         device_id=dev, device_id_type=MESH).start()
        token[...] = jnp.zeros_like(token)

    outs = pl.pallas_call(
        body, name=name,
        out_shape=(pltpu.SemaphoreType.DMA((n_copies,)), pltpu.SemaphoreType.DMA((n_copies,)),
                   *[pltpu.HBM(b.shape, b.dtype) for b in bufs], jax.ShapeDtypeStruct((8, 128), F32)),
        in_specs=[HBM_SPEC] * n + [ANY] * len(extra),
        out_specs=(SEM_SPEC, SEM_SPEC, *[HBM_SPEC] * n, pl.BlockSpec(memory_space=pltpu.VMEM)),
        input_output_aliases={k: 2 + k for k in range(n)},
        compiler_params=pltpu.CompilerParams(has_side_effects=SPLIT_EFFECT),
    )(*[pltpu.with_memory_space_constraint(b, pltpu.HBM) for b in bufs], *extra)
    return outs[0], outs[1], list(outs[2:2 + n]), outs[-1]


def _split_wait(bufs, send_sems, recv_sems, plan, after, *, name, first=0):
    n = len(bufs)
    after = list(after) if isinstance(after, (list, tuple)) else [after]

    def body(*refs):
        ins = refs[:n]
        send_sems_ref, recv_sems_ref = refs[n], refs[n + 1]
        x, y, c = _my_place()
        for i, (src, dst, dev) in enumerate(plan(ins, x, y, c, True)):
            cp = pltpu.make_async_remote_copy(src_ref=src, dst_ref=dst, send_sem=send_sems_ref.at[first + i],
                                              recv_sem=recv_sems_ref.at[first + i], device_id=dev, device_id_type=MESH)
            cp.wait_send()
            cp.wait_recv()

    outs = pl.pallas_call(
        body, name=name, out_shape=tuple(pltpu.HBM(b.shape, b.dtype) for b in bufs),
        in_specs=[HBM_SPEC] * n + [SEM_SPEC, SEM_SPEC] + [ANY] * len(after), out_specs=tuple([HBM_SPEC] * n),
        input_output_aliases={k: k for k in range(n)},
        compiler_params=pltpu.CompilerParams(has_side_effects=SPLIT_EFFECT),
    )(*bufs, send_sems, recv_sems, *after)
    return list(outs)


def _fill_own_slots(shards, me_arr, *, name, behind=()):
    n = len(shards)
    n_in = n + len(behind)

    def body(me_ref, *refs):
        for k in range(n):
            refs[n_in + k][0] = refs[k][...].astype(_MXU)

    half = D_MODEL // 2
    return pl.pallas_call(
        body, name=name,
        grid_spec=pltpu.PrefetchScalarGridSpec(
            num_scalar_prefetch=1, grid=(2,),
            in_specs=[pl.BlockSpec((s.shape[0], half), lambda i, me: (0, i)) for s in shards]
            + [pl.BlockSpec(memory_space=pl.ANY)] * len(behind),
            out_specs=[pl.BlockSpec((1, s.shape[0], half), lambda i, me: (me[0], 0, i)) for s in shards]),
        out_shape=[jax.ShapeDtypeStruct((N_SHARDS,) + s.shape, _MXU) for s in shards],
        compiler_params=_cparams("parallel"),
    )(me_arr, *shards, *behind)


def _gather_small(small):
    def body(s_ref, o_ref, send_sems, recv_sems, local_sem):
        x, y, c = _my_place()
        me = 2 * x + y
        local = pltpu.make_async_copy(s_ref, o_ref.at[me], local_sem)
        local.start()
        copies = [(pltpu.make_async_remote_copy(src_ref=s_ref, dst_ref=o_ref.at[me], send_sem=send_sems.at[j],
                                                recv_sem=recv_sems.at[j], device_id=(px, py, c), device_id_type=MESH),
                   2 * px + py) for j, (px, py) in enumerate(_other_chips(x, y))]
        for cp, _ in copies:
            cp.start()
        for j, (cp, slot) in enumerate(copies):
            cp.wait_send()
            pltpu.make_async_remote_copy(src_ref=s_ref, dst_ref=o_ref.at[slot], send_sem=send_sems.at[j],
                                         recv_sem=recv_sems.at[j], device_id=(x, y, c),
                                         device_id_type=MESH).wait_recv()
        local.wait()

    return pl.pallas_call(
        body, name="gather_small", in_specs=[ANY], out_specs=ANY,
        out_shape=jax.ShapeDtypeStruct((N_SHARDS,) + small.shape, small.dtype),
        scratch_shapes=[pltpu.SemaphoreType.DMA((3,)), pltpu.SemaphoreType.DMA((3,)), pltpu.SemaphoreType.DMA],
    )(small)


def _swap_with_sibling(parts, *, name, behind=()):
    n = len(parts)
    nb = len(behind)

    def body(*refs):
        ins, outs = refs[:n], refs[n + nb:2 * n + nb]
        send_sems, recv_sems = refs[2 * n + nb:]
        x, y, c = _my_place()
        copies = [pltpu.make_async_remote_copy(
            src_ref=ins[k], dst_ref=outs[k], send_sem=send_sems.at[k], recv_sem=recv_sems.at[k],
            device_id=(x, y, 1 - c), device_id_type=MESH) for k in range(n)]
        for cp in copies:
            cp.start()
        for cp in copies:
            cp.wait()

    return pl.pallas_call(
        body, name=name, in_specs=[ANY] * (n + nb), out_specs=[ANY] * n,
        out_shape=[jax.ShapeDtypeStruct(a.shape, a.dtype) for a in parts],
        scratch_shapes=[pltpu.SemaphoreType.DMA((n,)), pltpu.SemaphoreType.DMA((n,))],
    )(*parts, *behind)


def _other_devices(x, y, c):
    out = []
    for mask in range(1, N_DEV):
        px, py, pc = x ^ (mask >> 2 & 1), y ^ (mask >> 1 & 1), c ^ (mask & 1)
        out.append(((px, py, pc), 4 * px + 2 * py + pc))
    return out


def _pieces_plan(bufs, x, y, c, incoming):
    pack, land = bufs
    me = 4 * x + 2 * y + c
    return [(pack.at[num], land.at[num if incoming else me], dev) for dev, num in _other_devices(x, y, c)]


def _spread_plan(bufs, x, y, c, incoming):
    piece, land = bufs
    me = 4 * x + 2 * y + c
    return [(piece, land.at[num if incoming else me], dev) for dev, num in _other_devices(x, y, c)]


def _sum_pieces(pack, land, dev_arr, *, name):
    def body(dev_ref, pack_ref, land_ref, o_ref):
        dev = dev_ref[0]
        own = pack_ref[dev]
        acc = None
        for d in range(N_DEV):
            term = jnp.where(dev == d, own, land_ref[d])
            acc = term if acc is None else acc + term
        o_ref[...] = acc

    vmem = pl.BlockSpec(memory_space=pltpu.VMEM)
    return pl.pallas_call(
        body, name=name, in_specs=[pl.BlockSpec(memory_space=pltpu.SMEM), vmem, vmem], out_specs=vmem,
        out_shape=jax.ShapeDtypeStruct(pack.shape[1:], F32),
    )(dev_arr, pack, land)


def _join_pieces(piece, land, dev_arr, *, name):
    def body(dev_ref, piece_ref, land_ref, o_ref):
        dev = dev_ref[0]
        for d in range(N_DEV):
            o_ref[d] = jnp.where(dev == d, piece_ref[...], land_ref[d])

    vmem = pl.BlockSpec(memory_space=pltpu.VMEM)
    return pl.pallas_call(
        body, name=name, in_specs=[pl.BlockSpec(memory_space=pltpu.SMEM), vmem, vmem], out_specs=vmem,
        out_shape=jax.ShapeDtypeStruct(land.shape, F32),
    )(dev_arr, piece, land)


def _adamw_native(ws, gs, ms, vs):
    n = len(ws)

    def body(*refs):
        for k in range(n):
            w_ref, g_ref, m_ref, v_ref = (refs[j * n + k] for j in range(4))
            delta, m_new, v_new = _adamw_math(w_ref[...], g_ref[...], m_ref[...], v_ref[...])
            refs[4 * n + k][...] = delta
            refs[5 * n + k][...] = m_new
            refs[6 * n + k][...] = v_new

    vmem = pl.BlockSpec(memory_space=pltpu.VMEM)
    shapes = [jax.ShapeDtypeStruct(a.shape, F32) for a in ws]
    outs = pl.pallas_call(
        body, name="adamw_small", in_specs=[vmem] * (4 * n), out_specs=[vmem] * (3 * n), out_shape=shapes * 3,
        compiler_params=pltpu.CompilerParams(vmem_limit_bytes=VMEM_LIMIT_BYTES),
    )(*ws, *gs, *ms, *vs)
    return outs[:n], outs[n:2 * n], outs[2 * n:]


def _elementwise_tile(rows, cols):
    for t in range(256, 15, -16):
        if rows % t == 0:
            return (t, cols), rows // t, lambda i: (i, 0)
    assert cols % 256 == 0
    return (rows, 256), cols // 256, lambda i: (0, i)


def _partial_sum(own, land, me_arr, *, name):
    r, c = own.shape[-2:]
    tile, steps, imap = _elementwise_tile(r, c)
    whole = own.ndim == 3

    def body(me_ref, own_ref, land_ref, o_ref):
        acc = own_ref[0] if whole else own_ref[...]
        for j in range(3):
            acc = acc + land_ref[j].astype(F32)
        o_ref[...] = acc.astype(_MXU)

    own_spec = (pl.BlockSpec((1,) + tile, lambda i, me: (me[0],) + imap(i)) if whole
                else pl.BlockSpec(tile, lambda i, me: imap(i)))
    return pl.pallas_call(
        body, name=name,
        grid_spec=pltpu.PrefetchScalarGridSpec(
            num_scalar_prefetch=1, grid=(steps,),
            in_specs=[own_spec, pl.BlockSpec((3,) + tile, lambda i, me: (0,) + imap(i))],
            out_specs=pl.BlockSpec(tile, lambda i, me: imap(i))),
        out_shape=jax.ShapeDtypeStruct((r, c), _MXU),
        compiler_params=_cparams("parallel"),
    )(me_arr, own, land)


LANE_TILE = 256


def _partial_sums(owns, lands, me_arr, *, name):
    n = len(owns)

    def body(me_ref, *refs):
        for k in range(n):
            acc = refs[k][0]
            for j in range(3):
                acc = acc + refs[n + k][j].astype(F32)
            refs[2 * n + k][...] = acc.astype(_MXU)

    rows = [o.shape[1] for o in owns]
    return pl.pallas_call(
        body, name=name,
        grid_spec=pltpu.PrefetchScalarGridSpec(
            num_scalar_prefetch=1, grid=(D_MODEL // LANE_TILE,),
            in_specs=[pl.BlockSpec((1, r, LANE_TILE), lambda i, me: (me[0], 0, i)) for r in rows]
            + [pl.BlockSpec((3, r, LANE_TILE), lambda i, me: (0, 0, i)) for r in rows],
            out_specs=[pl.BlockSpec((r, LANE_TILE), lambda i, me: (0, i)) for r in rows]),
        out_shape=[jax.ShapeDtypeStruct((r, D_MODEL), _MXU) for r in rows],
        compiler_params=_cparams("parallel"),
    )(me_arr, *owns, *lands)


def _adamws(ws, parts_a, parts_b, ms, vs, *, name):
    n = len(ws)

    def body(*refs):
        for k in range(n):
            w_ref, a_ref, b_ref, m_ref, v_ref = (refs[j * n + k] for j in range(5))
            g = a_ref[...].astype(F32) + b_ref[...].astype(F32)
            delta, m_new, v_new = _adamw_math(w_ref[...], g, m_ref[...], v_ref[...])
            for j, val in enumerate((g, delta, m_new, v_new)):
                refs[(5 + j) * n + k][...] = val

    tiles = [pl.BlockSpec((w.shape[0], LANE_TILE), lambda i: (0, i)) for w in ws]
    outs = pl.pallas_call(
        body, name=name, grid=(D_MODEL // LANE_TILE,), in_specs=tiles * 5, out_specs=tiles * 4,
        out_shape=[jax.ShapeDtypeStruct(w.shape, F32) for w in ws] * 4,
        compiler_params=_cparams("parallel"),
    )(*ws, *parts_a, *parts_b, *ms, *vs)
    return [outs[j * n:(j + 1) * n] for j in range(4)]


def _adamw_math(w, g, m, v):
    m = ADAM_B1 * m + (1.0 - ADAM_B1) * g
    v = ADAM_B2 * v + (1.0 - ADAM_B2) * (g * g)
    m_hat = m / (1.0 - ADAM_B1 ** ADAM_STEP)
    v_hat = v / (1.0 - ADAM_B2 ** ADAM_STEP)
    delta = -ADAM_LR * (m_hat / (jnp.sqrt(v_hat) + ADAM_EPS) + ADAM_WD * w)
    return delta, m, v


def _adamw(w, grad_parts, m, v, *, name):
    if w.ndim == 3:
        steps = 4
        assert w.shape[0] % steps == 0
        tile_shape, imap = (w.shape[0] // steps,) + w.shape[1:], lambda i: (i, 0, 0)
    else:
        tile_shape, steps, imap = _elementwise_tile(*w.shape)
    n = len(grad_parts)

    def body(*refs):
        w_ref, m_ref, v_ref = refs[:3]
        g_refs = refs[3:3 + n]
        g_out, d_out, m_out, v_out = refs[3 + n:]
        g = g_refs[0][...].astype(F32)
        for k in range(1, n):
            g = g + g_refs[k][...].astype(F32)
        delta, m_new, v_new = _adamw_math(w_ref[...], g, m_ref[...], v_ref[...])
        g_out[...] = g
        d_out[...] = delta
        m_out[...] = m_new
        v_out[...] = v_new

    tile = pl.BlockSpec(tile_shape, imap)
    return pl.pallas_call(
        body, name=name, grid=(steps,), in_specs=[tile] * (3 + n), out_specs=[tile] * 4,
        out_shape=[jax.ShapeDtypeStruct(w.shape, F32)] * 4,
        compiler_params=_cparams("parallel"),
    )(w, m, v, *grad_parts)


WEIGHT_NAMES = ("meta_tokens", "norm1_w", "w_in", "ssd_conv_w", "ssd_conv_b", "ssd_dt_bias", "ssd_a_log", "ssd_d",
                "ssd_norm_w", "lru_conv_w", "lru_conv_b", "lru_wa", "lru_ba", "lru_wx", "lru_bx", "lru_lambda",
                "lru_norm_w", "w_out", "norm2_w", "w_gate", "w_up", "w_down", "final_norm_w")
BIG = ("w_in", "w_out", "w_gate", "w_up", "w_down")
FFN = ("w_gate", "w_up", "w_down")
LATE = ("w_out",) + FFN
SMALL_SHARDED = {"meta_tokens": (N_META, D_MODEL), "ssd_conv_w": (CONV_K, 1536), "lru_conv_w": (CONV_K, LRU_WIDTH)}
SMALL = tuple(n for n in WEIGHT_NAMES if n not in BIG)
PACK_COLS = 1024


def _pack(arrays, row_multiple):
    flat = jnp.concatenate([a.reshape(-1) for a in arrays])
    rows = -(-flat.shape[0] // (row_multiple * PACK_COLS)) * row_multiple
    return jnp.pad(flat, (0, rows * PACK_COLS - flat.shape[0])).reshape(rows, PACK_COLS)


def _unpack(pack, shapes):
    flat = pack.reshape(-1)
    out, off = [], 0
    for s in shapes:
        size = math.prod(s)
        out.append(flat[off:off + size].reshape(s))
        off += size
    return out


def _unshard_cols(g4):
    return jnp.swapaxes(g4, 0, 1).reshape(g4.shape[1], -1)


COL_SHARDED = ("w_in", "w_gate", "w_up")
IN_ROWS = {"z": (0, 1024), "xs": (1024, 2048), "bc": (2048, 2560), "dt": (2560, 2576), "g": (2576, 3600),
           "x": (3600, IN_COLS)}


def _rows_of_shards(shards4, lo, hi):
    r = shards4.shape[1]
    parts = [shards4[k, max(lo, k * r) - k * r:min(hi, (k + 1) * r) - k * r]
             for k in range(N_SHARDS) if max(lo, k * r) < min(hi, (k + 1) * r)]
    return parts[0] if len(parts) == 1 else jnp.concatenate(parts, axis=0)


def _w_in_shard_rows(k, sections):
    lo, hi = k * (IN_COLS // N_SHARDS), (k + 1) * (IN_COLS // N_SHARDS)
    parts = []
    for arr, (a, b) in zip(sections, IN_ROWS.values()):
        if max(lo, a) < min(hi, b):
            parts.append(arr[max(lo, a) - a:min(hi, b) - a])
    return jnp.concatenate(parts, axis=0)


def _rows_view(name, block):
    return jnp.swapaxes(block[0], 0, 1) if name in COL_SHARDED else block[0]


def _param_view(name, rows):
    return (jnp.swapaxes(rows, 0, 1) if name in COL_SHARDED else rows)[None]


def kernel(x, meta_tokens, norm1_w, w_in, ssd_conv_w, ssd_conv_b, ssd_dt_bias, ssd_a_log, ssd_d, ssd_norm_w, lru_conv_w, lru_conv_b, lru_wa, lru_ba, lru_wx, lru_bx, lru_lambda, lru_norm_w, w_out, norm2_w, w_gate, w_up, w_down, final_norm_w, loss_target, m_meta_tokens, m_norm1_w, m_w_in, m_ssd_conv_w, m_ssd_conv_b, m_ssd_dt_bias, m_ssd_a_log, m_ssd_d, m_ssd_norm_w, m_lru_conv_w, m_lru_conv_b, m_lru_wa, m_lru_ba, m_lru_wx, m_lru_bx, m_lru_lambda, m_lru_norm_w, m_w_out, m_norm2_w, m_w_gate, m_w_up, m_w_down, m_final_norm_w, v_meta_tokens, v_norm1_w, v_w_in, v_ssd_conv_w, v_ssd_conv_b, v_ssd_dt_bias, v_ssd_a_log, v_ssd_d, v_ssd_norm_w, v_lru_conv_w, v_lru_conv_b, v_lru_wa, v_lru_ba, v_lru_wx, v_lru_bx, v_lru_lambda, v_lru_norm_w, v_w_out, v_norm2_w, v_w_gate, v_w_up, v_w_down, v_final_norm_w):
    w = dict(zip(WEIGHT_NAMES, (meta_tokens, norm1_w, w_in, ssd_conv_w, ssd_conv_b, ssd_dt_bias, ssd_a_log, ssd_d, ssd_norm_w, lru_conv_w, lru_conv_b, lru_wa, lru_ba, lru_wx, lru_bx, lru_lambda, lru_norm_w, w_out, norm2_w, w_gate, w_up, w_down, final_norm_w)))
    m = dict(zip(WEIGHT_NAMES, (m_meta_tokens, m_norm1_w, m_w_in, m_ssd_conv_w, m_ssd_conv_b, m_ssd_dt_bias, m_ssd_a_log, m_ssd_d, m_ssd_norm_w, m_lru_conv_w, m_lru_conv_b, m_lru_wa, m_lru_ba, m_lru_wx, m_lru_bx, m_lru_lambda, m_lru_norm_w, m_w_out, m_norm2_w, m_w_gate, m_w_up, m_w_down, m_final_norm_w)))
    v = dict(zip(WEIGHT_NAMES, (v_meta_tokens, v_norm1_w, v_w_in, v_ssd_conv_w, v_ssd_conv_b, v_ssd_dt_bias, v_ssd_a_log, v_ssd_d, v_ssd_norm_w, v_lru_conv_w, v_lru_conv_b, v_lru_wa, v_lru_ba, v_lru_wx, v_lru_bx, v_lru_lambda, v_lru_norm_w, v_w_out, v_norm2_w, v_w_gate, v_w_up, v_w_down, v_final_norm_w)))
    me = 2 * lax.axis_index("x") + lax.axis_index("y")

    big2d = {n: _rows_view(n, w[n]) for n in BIG}
    small_local = jnp.concatenate([w["meta_tokens"].reshape(-1), w["ssd_conv_w"].reshape(-1),
                                   w["lru_conv_w"].reshape(-1)])[None]
    me_arr = me.astype(jnp.int32).reshape(1)
    dev_arr = (2 * me + lax.axis_index("c")).astype(jnp.int32).reshape(1)
    small4 = _gather_small(small_local)
    (w_in_slot,) = _fill_own_slots([big2d["w_in"]], me_arr, name="own_slot_w_in")
    in_send, in_recv, in_bufs, in_tok = _split_start([w_in_slot], _halves_plan, 3, small4, name="gather_w_in_start")
    late_slots = _fill_own_slots([big2d[n] for n in LATE], me_arr, name="own_slots_late", behind=(in_tok,))
    sm = small4[:, 0]
    meta_full = _unshard_cols(sm[:, :4096].reshape(N_SHARDS, N_META, 256))
    ssd_conv_w_full = _unshard_cols(sm[:, 4096:5632].reshape(N_SHARDS, CONV_K, 384))
    lru_conv_w_full = _unshard_cols(sm[:, 5632:].reshape(N_SHARDS, CONV_K, 256))

    p = {"ssd_conv_w": ssd_conv_w_full, "lru_conv_w": lru_conv_w_full,
         "lru_wa": w["lru_wa"][0], "lru_wx": w["lru_wx"][0], "final_norm_w": w["final_norm_w"][None]}
    for n in ("norm1_w", "ssd_conv_b", "ssd_dt_bias", "ssd_a_log", "ssd_d", "ssd_norm_w", "lru_conv_b", "lru_ba",
              "lru_bx", "lru_lambda", "lru_norm_w", "norm2_w"):
        p[n] = w[n]

    class Late:
        def __init__(self):
            self.pending = []
            self.before_embed = (late_slots[0],)

        def w_in(self, after):
            (buf,) = _split_wait(in_bufs, in_send, in_recv, _halves_plan, after, name="gather_w_in_wait")
            send, recv, bufs, tok = _split_start([buf], _forward_plan, 3, None, name="forward_w_in_start")
            self.late_gather = _split_start(late_slots, _halves_plan, 3 * len(LATE), tok, name="gather_late_start")
            (w_in4,) = _split_wait(bufs, send, recv, _forward_plan, self.late_gather[2][0], name="forward_w_in_wait")
            sections = {s: _rows_of_shards(w_in4, lo, hi) for s, (lo, hi) in IN_ROWS.items()}
            sections["dt"] = jnp.pad(sections["dt"], ((0, SEC_WIDTH["dt"] - SSD_HEADS), (0, 0)))
            return sections

        def mid_forward(self, after):
            send, recv, bufs, _ = self.late_gather
            bufs = _split_wait(bufs[:1], send, recv, _halves_plan, after, name="gather_w_out_wait")
            self.forward = _split_start(bufs, _forward_plan, 3, None, name="forward_w_out_start")
            return self.forward[3][:1, :1]

        def w_out(self, after):
            send, recv, bufs, _ = self.forward
            (w,) = _split_wait(bufs, send, recv, _forward_plan, after, name="forward_w_out_wait")
            send, recv, bufs, _ = self.late_gather
            bufs = _split_wait(bufs[1:], send, recv, _halves_plan, after, name="gather_ffn_wait", first=3)
            self.forward = _split_start(bufs, _forward_plan, 3 * len(FFN), None, name="forward_ffn_start")
            self.behind_out_proj = (self.forward[2][0],)
            return w.reshape(-1, D_MODEL)

        def ffn(self, after):
            send, recv, bufs, _ = self.forward
            bufs = _split_wait(bufs, send, recv, _forward_plan, after, name="forward_ffn_wait")
            return tuple(b.reshape(-1, D_MODEL) for b in bufs)

        def grads_ready(self, names, g, g_mxu):
            if names == ("w_in",):
                g_mxu["w_in"] = jnp.stack([_w_in_shard_rows(k, [g_mxu["w_in_" + s] for s in SEC_NAMES])
                                           for k in range(N_SHARDS)])
            srcs = [g_mxu[n].reshape(N_SHARDS, -1, D_MODEL) for n in names]
            lands = [lax.empty((3,) + s.shape[1:], _MXU) for s in srcs]
            tag = "_".join(names)
            send, recv, bufs, tok = _split_start(srcs + lands, _scatter_plan, 3 * len(names), None,
                                                 name="scatter_" + tag + "_start")
            self.pending.append((names, send, recv, bufs, tag))
            self.in_flight = bufs[0]
            return tok[:1, :1]

        def landed(self, after, which):
            land = {}
            for names, send, recv, bufs, tag in self.pending:
                if names[0] in which:
                    bufs = _split_wait(bufs, send, recv, _scatter_plan, after, name="scatter_" + tag + "_wait")
                    land.update(zip(names, bufs[len(names):]))
            return land

        def small_ready(self, g, loss):
            pack = _pack([g[n] for n in SMALL] + [loss[0, :1]], 8 * N_DEV)
            pack = pack.reshape(N_DEV, -1, PACK_COLS)
            self.small = _split_start([pack, lax.empty(pack.shape, F32)], _pieces_plan, N_DEV - 1, loss,
                                      name="small_pieces_start")
            return self.small[3]

        def small_middle(self, after):
            send, recv, bufs, _ = self.small
            pack, land = _split_wait(bufs, send, recv, _pieces_plan, after, name="small_pieces_wait")
            piece = _sum_pieces(pack, land, dev_arr, name="small_pieces_sum")
            self.small = _split_start([piece, lax.empty(pack.shape, F32)], _spread_plan, N_DEV - 1, None,
                                      name="small_spread_start")
            return self.small[3]

        def small_sum(self, after):
            send, recv, bufs, _ = self.small
            piece, land = _split_wait(bufs, send, recv, _spread_plan, after, name="small_spread_wait")
            return _join_pieces(piece, land, dev_arr, name="small_join")

    late = Late()

    loss, grad_x, g, g_mxu = _local_step(x[0], loss_target[0], meta_full, p, late)

    g4 = {n: g[n].reshape(N_SHARDS, -1, D_MODEL) for n in LATE}
    g4["w_in"] = lax.switch(me, [functools.partial(_w_in_shard_rows, k) for k in range(N_SHARDS)],
                            [g["w_in_" + s] for s in SEC_NAMES])
    land = late.landed([late.in_flight, late.small[2][0]], LATE)
    part = dict(zip(LATE, _partial_sums([g4[n] for n in LATE], [land[n] for n in LATE], me_arr,
                                        name="partial_late")))
    sib = dict(zip(LATE, _swap_with_sibling([part[n] for n in LATE], name="swap_late")))

    grad, delta, new_m, new_v = {}, {}, {}, {}
    late_outs = _adamws([big2d[n] for n in LATE], [part[n] for n in LATE], [sib[n] for n in LATE],
                        [_rows_view(n, m[n]) for n in LATE], [_rows_view(n, v[n]) for n in LATE], name="adamw_late")
    for d, outs in zip((grad, delta, new_m, new_v), late_outs):
        d.update({n: _param_view(n, o) for n, o in zip(LATE, outs)})

    land.update(late.landed(late_outs[0][0], ("w_in",)))
    spread = late.small_middle(land["w_in"])
    part_in = _partial_sum(g4["w_in"], land["w_in"], me_arr, name="partial_w_in")
    (sib_in,) = _swap_with_sibling([part_in], name="swap_w_in", behind=(spread,))
    lanes = lambda a: a[0].reshape(8, 128, -1).transpose(2, 0, 1)
    pieces = lambda a: a.reshape(-1, 8, 128)
    outs = _adamw(lanes(w["w_in"]), [pieces(part_in), pieces(sib_in)], lanes(m["w_in"]), lanes(v["w_in"]),
                  name="adamw_w_in")
    grad["w_in"], delta["w_in"], new_m["w_in"], new_v["w_in"] = (o.transpose(1, 2, 0).reshape(1, D_MODEL, -1)
                                                                 for o in outs)

    small_full_shape = {n: (SMALL_SHARDED[n] if n in SMALL_SHARDED else w[n].shape) for n in SMALL}
    red_list = _unpack(late.small_sum(outs[0]), [small_full_shape[n] for n in SMALL] + [(1,)])
    loss_total = red_list[-1][0]
    g_small = {}
    for n, arr in zip(SMALL, red_list[:-1]):
        if n in SMALL_SHARDED:
            cols = SMALL_SHARDED[n][1] // N_SHARDS
            arr = lax.dynamic_slice_in_dim(arr, me * cols, cols, axis=1)
        g_small[n] = arr.reshape(w[n].shape)
    two_d = lambda a: a.reshape(1, -1) if a.ndim == 1 else a
    deltas, new_ms, new_vs = _adamw_native(*[[two_d(d[n]) for n in SMALL] for d in (w, g_small, m, v)])
    for n, dn, mn, vn in zip(SMALL, deltas, new_ms, new_vs):
        grad[n], delta[n], new_m[n], new_v[n] = (g_small[n], dn.reshape(w[n].shape), mn.reshape(w[n].shape),
                                                 vn.reshape(w[n].shape))

    return (loss_total, grad_x[None], *[grad[n] for n in WEIGHT_NAMES], *[delta[n] for n in WEIGHT_NAMES],
            *[new_m[n] for n in WEIGHT_NAMES], *[new_v[n] for n in WEIGHT_NAMES])
```

```python
import functools
import math

import jax
import jax.numpy as jnp
from jax import lax
from jax.experimental import pallas as pl
from jax.experimental.pallas import tpu as pltpu

F32 = jnp.float32
_MXU = jnp.bfloat16

D_MODEL = 1024
SEQ = 2048
N_META = 16
CHUNK = 128
T_ROWS = 2176
N_CHUNKS = T_ROWS // CHUNK
PAD_ROWS = T_ROWS - SEQ - N_META
X_ROW0 = PAD_ROWS + N_META
SSD_HEADS = 16
SSD_HEAD_DIM = 64
SSD_STATE = 128
SSD_GROUPS = 2
SSD_HPG = SSD_HEADS // SSD_GROUPS
SSD_WIDTH = 1024
LRU_WIDTH = 1024
LRU_C = 8.0
D_FF = 2816
EPS = 1e-6
IN_COLS = 4624
N_SHARDS = 4
N_DEV = 8

ADAM_LR = 0.001
ADAM_B1 = 0.9
ADAM_B2 = 0.999
ADAM_EPS = 1e-08
ADAM_WD = 0.01
ADAM_STEP = 10

VMEM_LIMIT_BYTES = 56 * 1024 * 1024

NN = (((1,), (0,)), ((), ()))
NT = (((1,), (1,)), ((), ()))
TN = (((0,), (0,)), ((), ()))


def _cparams(*sem):
    return pltpu.CompilerParams(dimension_semantics=sem, vmem_limit_bytes=VMEM_LIMIT_BYTES)


def _dot(a, b, dims=NN):
    return lax.dot_general(a.astype(_MXU), b.astype(_MXU), dims, preferred_element_type=F32)


def _dot_onehot(a, b, dims=NN, *, data=0, pieces=3):
    ops = [a, b]
    mask = ops[1 - data].astype(jnp.bfloat16)
    rest = ops[data]
    acc = None
    for _ in range(pieces):
        piece = rest.astype(jnp.bfloat16)
        ops[data], ops[1 - data] = piece, mask
        d = lax.dot_general(ops[0], ops[1], dims, preferred_element_type=F32)
        acc = d if acc is None else acc + d
        rest = rest - piece.astype(F32)
    return acc


def _sigmoid(x):
    return 0.5 * (1.0 + jnp.tanh(0.5 * x))


def _softplus(x):
    return jnp.maximum(x, 0.0) + jnp.log(1.0 + jnp.exp(-jnp.abs(x)))


def _silu(x):
    return x * _sigmoid(x)


def _silu_grad(x):
    s = _sigmoid(x)
    return s * (1.0 + x * (1.0 - s))


_GELU_C = math.sqrt(2.0 / math.pi)


def _gelu_and_grad(x):
    inner = _GELU_C * (x + 0.044715 * x * x * x)
    t = jnp.tanh(inner)
    g = 0.5 * x * (1.0 + t)
    dg = 0.5 * (1.0 + t) + 0.5 * x * (1.0 - t * t) * _GELU_C * (1.0 + 3.0 * 0.044715 * x * x)
    return g, dg


def _rms_fwd(x, w):
    rstd = lax.rsqrt(jnp.mean(x * x, axis=-1, keepdims=True) + EPS)
    return x * rstd * w


def _rms_bwd(x, w, dy):
    rstd = lax.rsqrt(jnp.mean(x * x, axis=-1, keepdims=True) + EPS)
    xhat = x * rstd
    dxhat = dy * w
    dx = rstd * (dxhat - xhat * jnp.mean(dxhat * xhat, axis=-1, keepdims=True))
    return dx, dy * xhat


def _mm(terms, m, n, *, tm, tn, mode, out_dtype, name, residual=None, n_outer=False, also_mxu=False, behind=()):
    gm, gn = m // tm, n // tn
    assert gm * tm == m and gn * tn == n
    if n_outer:
        grid = (gn, gm)
        mi = lambda g0, g1: g1
        ni = lambda g0, g1: g0
    else:
        grid = (gm, gn)
        mi = lambda g0, g1: g0
        ni = lambda g0, g1: g1
    in_specs, args = [], []
    for (a, ka, b, kb, k) in terms:
        if mode == "tn":
            in_specs.append(pl.BlockSpec((k, tm), lambda g0, g1, ka=ka: (ka, mi(g0, g1))))
        else:
            in_specs.append(pl.BlockSpec((tm, k), lambda g0, g1, ka=ka: (mi(g0, g1), ka)))
        if mode == "nt":
            in_specs.append(pl.BlockSpec((tn, k), lambda g0, g1, kb=kb: (ni(g0, g1), kb)))
        else:
            in_specs.append(pl.BlockSpec((k, tn), lambda g0, g1, kb=kb: (kb, ni(g0, g1))))
        args += [a, b]
    if residual is not None:
        in_specs.append(pl.BlockSpec((tm, tn), lambda g0, g1: (mi(g0, g1), ni(g0, g1))))
        args.append(residual)
    dims = {"nn": NN, "nt": NT, "tn": TN}[mode]
    n_terms = len(terms)
    has_res = residual is not None
    in_specs += [pl.BlockSpec(memory_space=pl.ANY)] * len(behind)
    args += list(behind)
    n_in = len(args)

    def body(*refs):
        acc = None
        for t in range(n_terms):
            d = lax.dot_general(refs[2 * t][...], refs[2 * t + 1][...], dims, preferred_element_type=F32)
            acc = d if acc is None else acc + d
        if has_res:
            acc = acc + refs[2 * n_terms][...]
        refs[n_in][...] = acc.astype(out_dtype)
        if also_mxu:
            refs[n_in + 1][...] = acc.astype(_MXU)

    tile = pl.BlockSpec((tm, tn), lambda g0, g1: (mi(g0, g1), ni(g0, g1)))
    shape = jax.ShapeDtypeStruct((m, n), out_dtype)
    return pl.pallas_call(
        body, name=name, grid=grid, in_specs=in_specs,
        out_specs=[tile, tile] if also_mxu else tile,
        out_shape=[shape, jax.ShapeDtypeStruct((m, n), _MXU)] if also_mxu else shape,
        compiler_params=_cparams("parallel", "parallel"),
    )(*args)


def _embed(x, meta, behind=()):
    def body(x_ref, meta_ref, *rest):
        o_ref = rest[-1]
        i = pl.program_id(0)

        @pl.when(i == 0)
        def _():
            o_ref[0:PAD_ROWS, :] = jnp.zeros((PAD_ROWS, D_MODEL), F32)
            o_ref[PAD_ROWS:CHUNK, :] = meta_ref[...]

        @pl.when(i > 0)
        def _():
            o_ref[...] = x_ref[...]

    return pl.pallas_call(
        body, name="embed", grid=(N_CHUNKS,),
        in_specs=[pl.BlockSpec((CHUNK, D_MODEL), lambda i: (jnp.maximum(i - 1, 0), 0)),
                  pl.BlockSpec((N_META, D_MODEL), lambda i: (0, 0))] + [pl.BlockSpec(memory_space=pl.ANY)] * len(behind),
        out_specs=pl.BlockSpec((CHUNK, D_MODEL), lambda i: (i, 0)),
        out_shape=jax.ShapeDtypeStruct((T_ROWS, D_MODEL), F32),
        compiler_params=_cparams("parallel"),
    )(x, meta, *behind)


def _rmsnorm(h, w, *, name, tm=544):
    def body(h_ref, w_ref, o_ref):
        o_ref[...] = _rms_fwd(h_ref[...], w_ref[...]).astype(_MXU)

    return pl.pallas_call(
        body, name=name, grid=(T_ROWS // tm,),
        in_specs=[pl.BlockSpec((tm, D_MODEL), lambda i: (i, 0)), pl.BlockSpec((1, D_MODEL), lambda i: (0, 0))],
        out_specs=pl.BlockSpec((tm, D_MODEL), lambda i: (i, 0)),
        out_shape=jax.ShapeDtypeStruct((T_ROWS, D_MODEL), _MXU),
        compiler_params=_cparams("parallel"),
    )(h, w)


def _norm_proj(h, w, sections, convs, *, name, tm=272):
    widths = [s.shape[0] for s in sections]
    n = len(sections)
    conv_ks = sorted(convs)
    nc = len(conv_ks)

    def body(*refs):
        h_ref, w_ref = refs[:2]
        sec_refs = refs[2:2 + n]
        cw_refs = refs[2 + n:2 + n + nc]
        cb_refs = refs[2 + n + nc:2 + n + 2 * nc]
        u_ref = refs[2 + n + 2 * nc]
        proj_refs = refs[3 + n + 2 * nc:3 + 2 * n + 2 * nc]
        act_refs = refs[3 + 2 * n + 2 * nc:3 + 2 * n + 3 * nc]
        halo_refs = refs[3 + 2 * n + 3 * nc:]
        i = pl.program_id(0)

        @pl.when(i == 0)
        def _():
            for hr in halo_refs:
                hr[...] = jnp.zeros_like(hr)

        u = _rms_fwd(h_ref[...], w_ref[...]).astype(_MXU)
        u_ref[...] = u
        real = (i * tm + lax.broadcasted_iota(jnp.int32, (tm, 1), 0) >= PAD_ROWS).astype(F32)
        for k in range(n):
            raw = lax.dot_general(u, sec_refs[k][...], NT, preferred_element_type=F32)
            proj_refs[k][...] = raw
            if k not in convs:
                continue
            q = conv_ks.index(k)
            wv, bv = cw_refs[q][...], cb_refs[q][...]
            cat = jnp.concatenate([halo_refs[q][...], raw], axis=0)
            pre = bv + raw * wv[CONV_K - 1:CONV_K]
            for s in range(1, CONV_K):
                pre = pre + pltpu.roll(cat, s, 0)[8:8 + tm] * wv[CONV_K - 1 - s:CONV_K - s]
            y = _silu(pre) if convs[k][2] else pre
            act_refs[q][...] = y * real
            halo_refs[q][...] = raw[tm - 8:tm]

    row = lambda width: pl.BlockSpec((tm, width), lambda i: (i, 0))
    whole = lambda a: pl.BlockSpec(a.shape, lambda i: (0, 0))
    cws = [convs[k][0] for k in conv_ks]
    cbs = [convs[k][1] for k in conv_ks]
    outs = pl.pallas_call(
        body, name=name, grid=(T_ROWS // tm,),
        in_specs=[row(D_MODEL), pl.BlockSpec((1, D_MODEL), lambda i: (0, 0))]
        + [pl.BlockSpec((wd, D_MODEL), lambda i: (0, 0)) for wd in widths]
        + [whole(a) for a in cws] + [whole(a) for a in cbs],
        out_specs=[row(D_MODEL)] + [row(wd) for wd in widths] + [row(widths[k]) for k in conv_ks],
        out_shape=[jax.ShapeDtypeStruct((T_ROWS, D_MODEL), _MXU)]
        + [jax.ShapeDtypeStruct((T_ROWS, wd), F32) for wd in widths]
        + [jax.ShapeDtypeStruct((T_ROWS, widths[k]), F32) for k in conv_ks],
        scratch_shapes=[pltpu.VMEM((8, widths[k]), F32) for k in conv_ks],
        compiler_params=_cparams("arbitrary"),
    )(h, w, *sections, *cws, *cbs)
    return outs[0], list(outs[1:1 + n]), dict(zip(conv_ks, outs[1 + n:]))


def _loss_head(h2, target, fw):
    def body(h_ref, t_ref, w_ref, loss_ref, dh_ref, dhb_ref, dw_ref, acc_ref):
        i = pl.program_id(0)

        @pl.when(i == 0)
        def _():
            acc_ref[...] = jnp.zeros_like(acc_ref)
            dw_ref[...] = jnp.zeros_like(dw_ref)

        h = h_ref[...]
        w = w_ref[...]
        y = _rms_fwd(h, w)
        live = (i > 0).astype(F32)
        err = (y - t_ref[...]) * live
        acc_ref[...] += jnp.sum(err * err, axis=0, keepdims=True)
        dy = err * (1.0 / D_MODEL)
        dx, dwr = _rms_bwd(h, w, dy)
        dh_ref[...] = dx
        dhb_ref[...] = dx.astype(_MXU)
        dw_ref[...] += jnp.sum(dwr, axis=0, keepdims=True)

        @pl.when(i == N_CHUNKS - 1)
        def _():
            tot = jnp.sum(acc_ref[...], axis=1, keepdims=True) * (0.5 / D_MODEL)
            loss_ref[...] = jnp.broadcast_to(tot, (1, 128))

    return pl.pallas_call(
        body, name="loss_head", grid=(N_CHUNKS,),
        in_specs=[pl.BlockSpec((CHUNK, D_MODEL), lambda i: (i, 0)),
                  pl.BlockSpec((CHUNK, D_MODEL), lambda i: (jnp.maximum(i - 1, 0), 0)),
                  pl.BlockSpec((1, D_MODEL), lambda i: (0, 0))],
        out_specs=[pl.BlockSpec((1, 128), lambda i: (0, 0)),
                   pl.BlockSpec((CHUNK, D_MODEL), lambda i: (i, 0)),
                   pl.BlockSpec((CHUNK, D_MODEL), lambda i: (i, 0)),
                   pl.BlockSpec((1, D_MODEL), lambda i: (0, 0))],
        out_shape=[jax.ShapeDtypeStruct((1, 128), F32),
                   jax.ShapeDtypeStruct((T_ROWS, D_MODEL), F32),
                   jax.ShapeDtypeStruct((T_ROWS, D_MODEL), _MXU),
                   jax.ShapeDtypeStruct((1, D_MODEL), F32)],
        scratch_shapes=[pltpu.VMEM((1, D_MODEL), F32)],
        compiler_params=_cparams("arbitrary"),
    )(h2, target, fw)


def _mm_norm_bwd(terms, h, w, dres, *, name, tm=272, behind=()):
    n_terms = len(terms)
    in_specs, args = [], []
    for (a, b, k) in terms:
        in_specs += [pl.BlockSpec((tm, k), lambda i: (i, 0)), pl.BlockSpec((k, D_MODEL), lambda i: (0, 0))]
        args += [a, b]
    in_specs += [pl.BlockSpec((tm, D_MODEL), lambda i: (i, 0)), pl.BlockSpec((1, D_MODEL), lambda i: (0, 0)),
                 pl.BlockSpec((tm, D_MODEL), lambda i: (i, 0))] + [pl.BlockSpec(memory_space=pl.ANY)] * len(behind)
    args += [h, w, dres, *behind]

    def body(*refs):
        h_ref, w_ref, dres_ref = refs[2 * n_terms:2 * n_terms + 3]
        dh_ref, dhb_ref, dw_ref = refs[2 * n_terms + 3 + len(behind):]

        @pl.when(pl.program_id(0) == 0)
        def _():
            dw_ref[...] = jnp.zeros_like(dw_ref)

        du = None
        for t in range(n_terms):
            d = lax.dot_general(refs[2 * t][...], refs[2 * t + 1][...], NN, preferred_element_type=F32)
            du = d if du is None else du + d
        dx, dwr = _rms_bwd(h_ref[...], w_ref[...], du)
        dh = dres_ref[...] + dx
        dh_ref[...] = dh
        dhb_ref[...] = dh.astype(_MXU)
        dw_ref[...] += jnp.sum(dwr, axis=0, keepdims=True)

    return pl.pallas_call(
        body, name=name, grid=(T_ROWS // tm,), in_specs=in_specs,
        out_specs=[pl.BlockSpec((tm, D_MODEL), lambda i: (i, 0)), pl.BlockSpec((tm, D_MODEL), lambda i: (i, 0)),
                   pl.BlockSpec((1, D_MODEL), lambda i: (0, 0))],
        out_shape=[jax.ShapeDtypeStruct((T_ROWS, D_MODEL), F32), jax.ShapeDtypeStruct((T_ROWS, D_MODEL), _MXU),
                   jax.ShapeDtypeStruct((1, D_MODEL), F32)],
        compiler_params=_cparams("arbitrary"),
    )(*args)


FFN_TM = T_ROWS
FFN_TN = 256


def _ffn_up(u2, wg_t, wu_t):
    def body(u_ref, wg_ref, wu_ref, gp_ref, up_ref, act_ref):
        u = u_ref[...]
        gp = lax.dot_general(u, wg_ref[...], NT, preferred_element_type=F32)
        up = lax.dot_general(u, wu_ref[...], NT, preferred_element_type=F32)
        gp_ref[...] = gp.astype(_MXU)
        up_ref[...] = up.astype(_MXU)
        act_ref[...] = (_silu(gp) * up).astype(_MXU)

    tile = pl.BlockSpec((FFN_TM, FFN_TN), lambda j, i: (i, j))
    return pl.pallas_call(
        body, name="ffn_up", grid=(D_FF // FFN_TN, T_ROWS // FFN_TM),
        in_specs=[pl.BlockSpec((FFN_TM, D_MODEL), lambda j, i: (i, 0)),
                  pl.BlockSpec((FFN_TN, D_MODEL), lambda j, i: (j, 0)),
                  pl.BlockSpec((FFN_TN, D_MODEL), lambda j, i: (j, 0))],
        out_specs=[tile, tile, tile],
        out_shape=[jax.ShapeDtypeStruct((T_ROWS, D_FF), _MXU)] * 3,
        compiler_params=_cparams("parallel", "parallel"),
    )(u2, wg_t, wu_t)


def _ffn_bwd_act(dh2b, wd, gp, up):
    def body(dh_ref, wd_ref, gp_ref, up_ref, dgp_ref, dup_ref):
        dact = lax.dot_general(dh_ref[...], wd_ref[...], NT, preferred_element_type=F32)
        gp = gp_ref[...].astype(F32)
        dgp_ref[...] = (dact * up_ref[...].astype(F32) * _silu_grad(gp)).astype(_MXU)
        dup_ref[...] = (dact * _silu(gp)).astype(_MXU)

    tile = pl.BlockSpec((FFN_TM, FFN_TN), lambda j, i: (i, j))
    return pl.pallas_call(
        body, name="ffn_bwd_act", grid=(D_FF // FFN_TN, T_ROWS // FFN_TM),
        in_specs=[pl.BlockSpec((FFN_TM, D_MODEL), lambda j, i: (i, 0)),
                  pl.BlockSpec((FFN_TN, D_MODEL), lambda j, i: (j, 0)), tile, tile],
        out_specs=[tile, tile],
        out_shape=[jax.ShapeDtypeStruct((T_ROWS, D_FF), _MXU), jax.ShapeDtypeStruct((T_ROWS, D_FF), _MXU)],
        compiler_params=_cparams("parallel", "parallel"),
    )(dh2b, wd, gp, up)


CONV_TC = 512
CONV_K = 4


def _conv_pre(x_ref, wv, bv, c):
    tc = wv.shape[1]
    r0 = c * CHUNK
    cur = x_ref[r0:r0 + CHUNK, :]
    if c == 0:
        cat = jnp.concatenate([jnp.zeros((8, tc), F32), cur], axis=0)
        shifted = [cur] + [pltpu.roll(cat, s, 0)[8:8 + CHUNK] for s in range(1, CONV_K)]
    else:
        shifted = [cur] + [x_ref[r0 - s:r0 - s + CHUNK, :] for s in range(1, CONV_K)]
    pre = bv
    for s in range(CONV_K):
        pre = pre + shifted[s] * wv[CONV_K - 1 - s:CONV_K - s]
    return pre, shifted


def _row_mask(c):
    if c > 0:
        return None
    return (lax.broadcasted_iota(jnp.int32, (CHUNK, 1), 0) >= PAD_ROWS).astype(F32)


def _conv_bwd(dy, x, w, b, *, silu, name):
    cols = x.shape[1]
    tc = min(CONV_TC, cols)

    def body(dy_ref, x_ref, w_ref, b_ref, dx_ref, dw_ref, db_ref):
        wv, bv = w_ref[...], b_ref[...]
        next8 = jnp.zeros((8, tc), F32)
        dws = [jnp.zeros((1, tc), F32) for _ in range(CONV_K)]
        db = jnp.zeros((1, tc), F32)
        for c in reversed(range(N_CHUNKS)):
            r0 = c * CHUNK
            pre, shifted = _conv_pre(x_ref, wv, bv, c)
            dpre = dy_ref[r0:r0 + CHUNK, :]
            if silu:
                dpre = dpre * _silu_grad(pre)
            mask = _row_mask(c)
            if mask is not None:
                dpre = dpre * mask
            cat = jnp.concatenate([dpre, next8], axis=0)
            dx = dpre * wv[CONV_K - 1:CONV_K]
            for s in range(1, CONV_K):
                dx = dx + pltpu.roll(cat, CHUNK + 8 - s, 0)[0:CHUNK] * wv[CONV_K - 1 - s:CONV_K - s]
            dx_ref[r0:r0 + CHUNK, :] = dx.astype(_MXU)
            for s in range(CONV_K):
                k = CONV_K - 1 - s
                dws[k] = dws[k] + jnp.sum(dpre * shifted[s], axis=0, keepdims=True)
            db = db + jnp.sum(dpre, axis=0, keepdims=True)
            next8 = dpre[0:8]
        dw_ref[...] = jnp.concatenate(dws, axis=0)
        db_ref[...] = db

    return pl.pallas_call(
        body, name=name, grid=(cols // tc,),
        in_specs=[pl.BlockSpec((T_ROWS, tc), lambda j: (0, j)), pl.BlockSpec((T_ROWS, tc), lambda j: (0, j)),
                  pl.BlockSpec((CONV_K, tc), lambda j: (0, j)), pl.BlockSpec((1, tc), lambda j: (0, j))],
        out_specs=[pl.BlockSpec((T_ROWS, tc), lambda j: (0, j)), pl.BlockSpec((CONV_K, tc), lambda j: (0, j)),
                   pl.BlockSpec((1, tc), lambda j: (0, j))],
        out_shape=[jax.ShapeDtypeStruct((T_ROWS, cols), _MXU), jax.ShapeDtypeStruct((CONV_K, cols), F32),
                   jax.ShapeDtypeStruct((1, cols), F32)],
        compiler_params=_cparams("parallel"),
    )(dy, x, w, b)


def _ssd_chunk_common(dt_raw, prm, c):
    a_row = -jnp.exp(prm[1:2])
    dt = _softplus(dt_raw + prm[0:1])
    rows = lax.broadcasted_iota(jnp.int32, (CHUNK, 1), 0)
    real = jnp.logical_or(c > 0, rows >= PAD_ROWS)
    dt = jnp.where(real, dt, 0.0)
    li = lax.broadcasted_iota(jnp.int32, (CHUNK, CHUNK), 0)
    si = lax.broadcasted_iota(jnp.int32, (CHUNK, CHUNK), 1)
    causal = li >= si
    tri = causal.astype(F32)
    cs = _dot_onehot(tri, dt * a_row, data=1)
    return dt, a_row, cs, cs.T, causal, tri, real


def _gated_norm_fwd(y, z, w):
    g = y * _silu(z)
    half = SSD_WIDTH // SSD_GROUPS
    outs = [_rms_fwd(g[:, k * half:(k + 1) * half], w[:, k * half:(k + 1) * half]) for k in range(SSD_GROUPS)]
    return jnp.concatenate(outs, axis=1)


GROUP_W = SSD_WIDTH // SSD_GROUPS
PAIR_W = 2 * SSD_HEAD_DIM
STATE_SHAPE = (SSD_GROUPS, SSD_STATE, GROUP_W)


def _head_expander():
    r = lax.broadcasted_iota(jnp.int32, (128, SSD_WIDTH), 0)
    c = lax.broadcasted_iota(jnp.int32, (128, SSD_WIDTH), 1)
    return (c // SSD_HEAD_DIM == r).astype(F32)


def _ssd_expand(dt, cs, prm, ex):
    cs_x = _dot_onehot(cs, ex)
    cs_last_x = cs_x[CHUNK - 1:CHUNK, :]
    return (_dot_onehot(dt, ex, pieces=2), _dot_onehot(prm, ex)[2:3], jnp.exp(cs_x), jnp.exp(cs_last_x),
            jnp.exp(cs_last_x - cs_x))


def _ssd_fwd(xs, bc, dt_raw, z, prm, norm_w, ex):
    def body(xs_ref, bc_ref, dt_ref, z_ref, prm_ref, nw_ref, ex_ref, y_ref, yn_ref, prev_ref, state):
        c = pl.program_id(0)

        @pl.when(c == 0)
        def _():
            state[...] = jnp.zeros_like(state)

        prm = prm_ref[...]
        dt, a_row, cs, cs_t, causal, _, _ = _ssd_chunk_common(dt_ref[...], prm, c)
        dt_x, d_x, e_cs_x, e_last_x, dec_x = _ssd_expand(dt, cs, prm, ex_ref[...])
        xs_all = xs_ref[...]
        bc_all = bc_ref[...]
        xdt = xs_all * dt_x
        xdec = xdt * dec_x
        lane_lo = lax.broadcasted_iota(jnp.int32, (1, PAIR_W), 1) < SSD_HEAD_DIM
        for g in range(SSD_GROUPS):
            gs = slice(g * GROUP_W, (g + 1) * GROUP_W)
            b_g = bc_all[:, g * SSD_STATE:(g + 1) * SSD_STATE]
            c_g = bc_all[:, (SSD_GROUPS + g) * SSD_STATE:(SSD_GROUPS + g + 1) * SSD_STATE]
            st = state[g]
            prev_ref[0, g] = st
            y_off = _dot(c_g, st) * e_cs_x[:, gs]
            state[g] = st * e_last_x[:, gs] + _dot(b_g.T, xdec[:, gs])
            cb = _dot(c_g, b_g, NT)
            for k in range(SSD_HPG // 2):
                h0 = g * SSD_HPG + 2 * k
                ps = slice(h0 * SSD_HEAD_DIM, h0 * SSD_HEAD_DIM + PAIR_W)
                xdt_pair = xdt[:, ps]
                yd = []
                for h in (h0, h0 + 1):
                    lmat = jnp.where(causal, jnp.exp(cs[:, h:h + 1] - cs_t[h:h + 1, :]), 0.0)
                    yd.append(_dot(cb * lmat, xdt_pair))
                y_ref[:, ps] = (jnp.where(lane_lo, yd[0], yd[1]) + y_off[:, k * PAIR_W:(k + 1) * PAIR_W]
                                + xs_all[:, ps] * d_x[:, ps])
        yn_ref[...] = _gated_norm_fwd(y_ref[...], z_ref[...], nw_ref[...]).astype(_MXU)

    row = lambda w: pl.BlockSpec((CHUNK, w), lambda c: (c, 0))
    return pl.pallas_call(
        body, name="ssd_fwd", grid=(N_CHUNKS,),
        in_specs=[row(SSD_WIDTH), row(512), row(128), row(SSD_WIDTH),
                  pl.BlockSpec((8, 128), lambda c: (0, 0)), pl.BlockSpec((1, SSD_WIDTH), lambda c: (0, 0)),
                  pl.BlockSpec((128, SSD_WIDTH), lambda c: (0, 0))],
        out_specs=[row(SSD_WIDTH), row(SSD_WIDTH),
                   pl.BlockSpec((1,) + STATE_SHAPE, lambda c: (c, 0, 0, 0))],
        out_shape=[jax.ShapeDtypeStruct((T_ROWS, SSD_WIDTH), F32), jax.ShapeDtypeStruct((T_ROWS, SSD_WIDTH), _MXU),
                   jax.ShapeDtypeStruct((N_CHUNKS,) + STATE_SHAPE, F32)],
        scratch_shapes=[pltpu.VMEM(STATE_SHAPE, F32)],
        compiler_params=_cparams("arbitrary"),
    )(xs, bc, dt_raw, z, prm, norm_w, ex)


def _ssd_bwd(dyn, dyn_block, z, y_pre, xs, bc, dt_raw, prev, prm, norm_w, ex):
    def body(dyn_ref, z_ref, y_ref, xs_ref, bc_ref, dt_ref, prev_ref, prm_ref, nw_ref, ex_ref,
             dz_ref, dxs_ref, dbc_ref, ddt_ref, dprm_ref, dnw_ref, dstate):
        step = pl.program_id(0)
        c = N_CHUNKS - 1 - step

        @pl.when(step == 0)
        def _():
            dstate[...] = jnp.zeros_like(dstate)
            dprm_ref[...] = jnp.zeros_like(dprm_ref)
            dnw_ref[...] = jnp.zeros_like(dnw_ref)

        prm = prm_ref[...]
        dt, a_row, cs, cs_t, causal, tri, real = _ssd_chunk_common(dt_ref[...], prm, c)
        realf = real.astype(F32)
        z = z_ref[...]
        y_all = y_ref[...]
        nw = nw_ref[...]
        dyn_all = dyn_ref[...]
        sz = _silu(z)
        gated = y_all * sz
        half = SSD_WIDTH // SSD_GROUPS
        dgs, dnws = [], []
        for k in range(SSD_GROUPS):
            sl = slice(k * half, (k + 1) * half)
            dgk, dwk = _rms_bwd(gated[:, sl], nw[:, sl], dyn_all[:, sl])
            dgs.append(dgk)
            dnws.append(jnp.sum(dwk, axis=0, keepdims=True))
        dgated = jnp.concatenate(dgs, axis=1)
        dnw_ref[...] += jnp.concatenate(dnws, axis=1)
        dz_ref[...] = (dgated * y_all * _silu_grad(z)).astype(_MXU)
        dy_all = dgated * sz

        ex = ex_ref[...]
        dt_x, d_x, e_cs_x, e_last_x, dec_x = _ssd_expand(dt, cs, prm, ex)
        xs_all = xs_ref[...]
        bc_all = bc_ref[...]
        xdt = xs_all * dt_x
        xdt_mxu = xdt.astype(_MXU).astype(F32)
        xdec = xdt * dec_x
        dcp = dy_all * e_cs_x
        lane_lo = lax.broadcasted_iota(jnp.int32, (1, PAIR_W), 1) < SSD_HEAD_DIM
        upper = (lax.broadcasted_iota(jnp.int32, (CHUNK, CHUNK), 0)
                 <= lax.broadcasted_iota(jnp.int32, (CHUNK, CHUNK), 1))
        last_row = (lax.broadcasted_iota(jnp.int32, (CHUNK, 1), 0) == CHUNK - 1).astype(F32)
        dbs, dcs_, dxdt_parts, last_parts = [], [], [], []
        for g in range(SSD_GROUPS):
            gs = slice(g * GROUP_W, (g + 1) * GROUP_W)
            b_g = bc_all[:, g * SSD_STATE:(g + 1) * SSD_STATE]
            c_g = bc_all[:, (SSD_GROUPS + g) * SSD_STATE:(SSD_GROUPS + g + 1) * SSD_STATE]
            prev_t = prev_ref[0, g]
            dst = dstate[g]
            dc_g = _dot(dcp[:, gs], prev_t, NT)
            db_g = _dot(xdec[:, gs], dst, NT)
            dxdt_state = _dot(b_g, dst) * dec_x[:, gs]
            dstate[g] = dst * e_last_x[:, gs] + _dot(c_g.T, dcp[:, gs])
            last_parts.append(jnp.sum(xdt_mxu[:, gs] * dxdt_state, axis=0, keepdims=True)
                              + jnp.sum(dst * prev_t, axis=0, keepdims=True) * e_last_x[:, gs])
            cb_t = _dot(b_g, c_g, NT)
            dcb_t = jnp.zeros((CHUNK, CHUNK), F32)
            for k in range(SSD_HPG // 2):
                h0 = g * SSD_HPG + 2 * k
                ps = slice(h0 * SSD_HEAD_DIM, h0 * SSD_HEAD_DIM + PAIR_W)
                dy_pair = dy_all[:, ps]
                xdt_pair = xdt[:, ps]
                dd = []
                for h in (h0, h0 + 1):
                    lmat_t = jnp.where(upper, jnp.exp(cs_t[h:h + 1, :] - cs[:, h:h + 1]), 0.0)
                    dd.append(_dot(cb_t * lmat_t, dy_pair))
                    mine = lane_lo if h == h0 else jnp.logical_not(lane_lo)
                    dcb_t = dcb_t + _dot(jnp.where(mine, xdt_pair, 0.0), dy_pair, NT) * lmat_t
                dxdt_parts.append(jnp.where(lane_lo, dd[0], dd[1]) + dxdt_state[:, k * PAIR_W:(k + 1) * PAIR_W])
            dc_g = dc_g + _dot(dcb_t, b_g, TN)
            db_g = db_g + _dot(dcb_t, c_g)
            dbs.append(db_g * realf)
            dcs_.append(dc_g * realf)
        dbc_ref[...] = jnp.concatenate(dbs + dcs_, axis=1)
        dxdt = jnp.concatenate(dxdt_parts, axis=1)
        dxs_ref[...] = (dxdt * dt_x + dy_all * d_x) * realf
        ddt_all = _dot_onehot(dxdt * xs_all, ex, NT, pieces=2)
        rows = jnp.concatenate([jnp.concatenate(last_parts, axis=1), jnp.sum(dy_all * xs_all, axis=0, keepdims=True),
                                jnp.zeros((6, SSD_WIDTH), F32)], axis=0)
        rows = _dot_onehot(rows, ex, NT, pieces=2)
        dd_row = rows[1:2]
        dy_mxu = dy_all.astype(_MXU).astype(F32)
        dcs_all = (_dot_onehot(dy_mxu * (y_all - xs_all * d_x), ex, NT) - _dot_onehot(xdt_mxu * dxdt, ex, NT)
                   + last_row * rows[0:1])
        dda = _dot_onehot(tri, dcs_all, TN, data=1)
        ddt = (ddt_all + dda * a_row) * realf
        ddt_raw = ddt * _sigmoid(dt_ref[...] + prm[0:1])
        ddt_ref[...] = ddt_raw.astype(_MXU)
        da_log = jnp.sum(dda * dt, axis=0, keepdims=True) * a_row
        dprm_ref[0:1, :] += jnp.sum(ddt_raw, axis=0, keepdims=True)
        dprm_ref[1:2, :] += da_log
        dprm_ref[2:3, :] += dd_row

    rev = lambda w, blk=0: pl.BlockSpec((CHUNK, w), lambda s, blk=blk: (N_CHUNKS - 1 - s, blk))
    return pl.pallas_call(
        body, name="ssd_bwd", grid=(N_CHUNKS,),
        in_specs=[rev(SSD_WIDTH, dyn_block), rev(SSD_WIDTH), rev(SSD_WIDTH), rev(SSD_WIDTH), rev(512), rev(128),
                  pl.BlockSpec((1,) + STATE_SHAPE, lambda s: (N_CHUNKS - 1 - s, 0, 0, 0)),
                  pl.BlockSpec((8, 128), lambda s: (0, 0)), pl.BlockSpec((1, SSD_WIDTH), lambda s: (0, 0)),
                  pl.BlockSpec((128, SSD_WIDTH), lambda s: (0, 0))],
        out_specs=[rev(SSD_WIDTH), rev(SSD_WIDTH), rev(512), rev(128),
                   pl.BlockSpec((8, 128), lambda s: (0, 0)), pl.BlockSpec((1, SSD_WIDTH), lambda s: (0, 0))],
        out_shape=[jax.ShapeDtypeStruct((T_ROWS, SSD_WIDTH), _MXU), jax.ShapeDtypeStruct((T_ROWS, SSD_WIDTH), F32),
                   jax.ShapeDtypeStruct((T_ROWS, 512), F32), jax.ShapeDtypeStruct((T_ROWS, 128), _MXU),
                   jax.ShapeDtypeStruct((8, 128), F32), jax.ShapeDtypeStruct((1, SSD_WIDTH), F32)],
        scratch_shapes=[pltpu.VMEM(STATE_SHAPE, F32)],
        compiler_params=_cparams("arbitrary"),
    )(dyn, z, y_pre, xs, bc, dt_raw, prev, prm, norm_w, ex)


LRU_PAIRS = 8


def _lru_gates(xr, wa_ref, wx_ref, prm):
    pre_r, pre_i = [], []
    for k in range(LRU_PAIRS):
        xk = xr[:, k * 128:(k + 1) * 128]
        pre_r.append(_dot(xk, wa_ref[k]))
        pre_i.append(_dot(xk, wx_ref[k]))
    r = _sigmoid(jnp.concatenate(pre_r, axis=1) + prm[0:1])
    i = _sigmoid(jnp.concatenate(pre_i, axis=1) + prm[1:2])
    sp = _softplus(-prm[2:3])
    log_a = (-LRU_C) * r * sp
    a = jnp.exp(log_a)
    s = jnp.sqrt(-jnp.tanh(log_a) * (a * a + 1.0))
    return r, i, a, s, sp


def _lru_fwd(xr, gate, wa, wx, prm):
    def body(xr_ref, g_ref, wa_ref, wx_ref, prm_ref, hs_ref, yn_ref, carry, a_s, u_s):
        @pl.when(pl.program_id(0) == 0)
        def _():
            carry[...] = jnp.zeros_like(carry)

        prm = prm_ref[...]
        xr_t = xr_ref[...]
        _, i, a, s, _ = _lru_gates(xr_t, wa_ref, wx_ref, prm)
        a_s[...] = a
        u_s[...] = s * (i * xr_t)
        rid = lax.broadcasted_iota(jnp.int32, (8, LRU_WIDTH), 0)

        def group(k, before):
            off = pl.multiple_of(k * 8, 8)
            a8 = a_s[pl.ds(off, 8), :]
            u8 = u_s[pl.ds(off, 8), :]
            for d in (1, 2, 4):
                keep = rid >= d
                u8 = u8 + a8 * jnp.where(keep, pltpu.roll(u8, d, 0), 0.0)
                a8 = a8 * jnp.where(keep, pltpu.roll(a8, d, 0), 1.0)
            h8 = u8 + a8 * before
            hs_ref[pl.ds(off, 8), :] = h8
            return jnp.broadcast_to(h8[7:8], (8, LRU_WIDTH))

        carry[...] = lax.fori_loop(0, CHUNK // 8, group, carry[...])
        gel, _ = _gelu_and_grad(g_ref[...])
        yn_ref[...] = _rms_fwd(gel * hs_ref[...], prm[3:4]).astype(_MXU)

    row = pl.BlockSpec((CHUNK, LRU_WIDTH), lambda t: (t, 0))
    wspec = pl.BlockSpec((LRU_PAIRS, 128, 128), lambda t: (0, 0, 0))
    return pl.pallas_call(
        body, name="lru_fwd", grid=(N_CHUNKS,),
        in_specs=[row, row, wspec, wspec, pl.BlockSpec((8, LRU_WIDTH), lambda t: (0, 0))],
        out_specs=[row, row],
        out_shape=[jax.ShapeDtypeStruct((T_ROWS, LRU_WIDTH), F32), jax.ShapeDtypeStruct((T_ROWS, LRU_WIDTH), _MXU)],
        scratch_shapes=[pltpu.VMEM((8, LRU_WIDTH), F32), pltpu.VMEM((CHUNK, LRU_WIDTH), F32),
                        pltpu.VMEM((CHUNK, LRU_WIDTH), F32)],
        compiler_params=_cparams("arbitrary"),
    )(xr, gate, wa, wx, prm)


def _lru_bwd(dyn, dyn_block, gate, xr, hs, wa, wx, wa_t, wx_t, prm):
    def body(dyn_ref, g_ref, xr_ref, hs_ref, hsp_ref, wa_ref, wx_ref, wat_ref, wxt_ref, prm_ref,
             dg_ref, dxr_ref, dwa_ref, dwx_ref, dprm_ref, carry, a_s, d_s):
        step = pl.program_id(0)
        tile = N_CHUNKS - 1 - step

        @pl.when(step == 0)
        def _():
            carry[...] = jnp.zeros_like(carry)
            dwa_ref[...] = jnp.zeros_like(dwa_ref)
            dwx_ref[...] = jnp.zeros_like(dwx_ref)
            dprm_ref[...] = jnp.zeros_like(dprm_ref)

        prm = prm_ref[...]
        xr_t = xr_ref[...]
        r, i, a, s, sp = _lru_gates(xr_t, wa_ref, wx_ref, prm)
        hs_t = hs_ref[...]
        gel, dgel = _gelu_and_grad(g_ref[...])
        dy, dnw = _rms_bwd(gel * hs_t, prm[3:4], dyn_ref[...])
        dg_ref[...] = (dy * hs_t * dgel).astype(_MXU)
        a_s[...] = a
        d_s[...] = dy * gel
        rid = lax.broadcasted_iota(jnp.int32, (8, LRU_WIDTH), 0)

        def group(k, behind):
            off = pl.multiple_of((CHUNK // 8 - 1 - k) * 8, 8)
            a8 = a_s[pl.ds(off, 8), :]
            d8 = d_s[pl.ds(off, 8), :]
            c8 = jnp.where(rid == 7, 1.0, pltpu.roll(a8, 7, 0))
            for d in (1, 2, 4):
                keep = rid < 8 - d
                d8 = d8 + c8 * jnp.where(keep, pltpu.roll(d8, 8 - d, 0), 0.0)
                c8 = c8 * jnp.where(keep, pltpu.roll(c8, 8 - d, 0), 1.0)
            dht8 = d8 + c8 * behind
            d_s[pl.ds(off, 8), :] = dht8
            return jnp.broadcast_to(a8[0:1] * dht8[0:1], (8, LRU_WIDTH))

        carry[...] = lax.fori_loop(0, CHUNK // 8, group, carry[...])
        dht = d_s[...]
        before = hsp_ref[CHUNK - 8:CHUNK, :][7:8] * (tile > 0).astype(F32)
        first = lax.broadcasted_iota(jnp.int32, (CHUNK, 1), 0) == 0
        hprev = jnp.where(first, before, pltpu.roll(hs_t, 1, 0))
        da = dht * hprev
        ixr = i * xr_t
        ds = dht * ixr
        dlog_a = da * a - ds * (a * a) * lax.rsqrt(s * s)
        dr = dlog_a * ((-LRU_C) * sp)
        dsp = jnp.sum(dlog_a * ((-LRU_C) * r), axis=0, keepdims=True)
        dlam = dsp * (-_sigmoid(-prm[2:3]))
        di = dht * s * xr_t
        dpre_r = dr * r * (1.0 - r)
        dpre_i = di * i * (1.0 - i)
        dxr = dht * s * i
        parts = []
        for k in range(LRU_PAIRS):
            sl = slice(k * 128, (k + 1) * 128)
            parts.append(_dot(dpre_r[:, sl], wat_ref[k]) + _dot(dpre_i[:, sl], wxt_ref[k]))
            dwa_ref[k] += _dot(xr_t[:, sl], dpre_r[:, sl], TN)
            dwx_ref[k] += _dot(xr_t[:, sl], dpre_i[:, sl], TN)
        dxr_ref[...] = dxr + jnp.concatenate(parts, axis=1)
        dprm_ref[0:1, :] += jnp.sum(dpre_r, axis=0, keepdims=True)
        dprm_ref[1:2, :] += jnp.sum(dpre_i, axis=0, keepdims=True)
        dprm_ref[2:3, :] += dlam
        dprm_ref[3:4, :] += jnp.sum(dnw, axis=0, keepdims=True)

    rev = lambda blk=0: pl.BlockSpec((CHUNK, LRU_WIDTH), lambda s, blk=blk: (N_CHUNKS - 1 - s, blk))
    wspec = pl.BlockSpec((LRU_PAIRS, 128, 128), lambda s: (0, 0, 0))
    return pl.pallas_call(
        body, name="lru_bwd", grid=(N_CHUNKS,),
        in_specs=[rev(dyn_block), rev(), rev(), rev(),
                  pl.BlockSpec((CHUNK, LRU_WIDTH), lambda s: (jnp.maximum(N_CHUNKS - 2 - s, 0), 0)),
                  wspec, wspec, wspec, wspec, pl.BlockSpec((8, LRU_WIDTH), lambda s: (0, 0))],
        out_specs=[rev(), rev(), wspec, wspec, pl.BlockSpec((8, LRU_WIDTH), lambda s: (0, 0))],
        out_shape=[jax.ShapeDtypeStruct((T_ROWS, LRU_WIDTH), _MXU), jax.ShapeDtypeStruct((T_ROWS, LRU_WIDTH), F32),
                   jax.ShapeDtypeStruct((LRU_PAIRS, 128, 128), F32), jax.ShapeDtypeStruct((LRU_PAIRS, 128, 128), F32),
                   jax.ShapeDtypeStruct((8, LRU_WIDTH), F32)],
        scratch_shapes=[pltpu.VMEM((8, LRU_WIDTH), F32), pltpu.VMEM((CHUNK, LRU_WIDTH), F32),
                        pltpu.VMEM((CHUNK, LRU_WIDTH), F32)],
        compiler_params=_cparams("arbitrary"),
    )(dyn, gate, xr, hs, hs, wa, wx, wa_t, wx_t, prm)


SEC_NAMES = ("z", "xs", "bc", "dt", "g", "x")
SEC_WIDTH = {"z": 1024, "xs": 1024, "bc": 512, "dt": 128, "g": 1024, "x": 1024}


def _pair_blocks(w):
    w = w.reshape(LRU_PAIRS, 2, 64, 64)
    zero = jnp.zeros((LRU_PAIRS, 64, 64), w.dtype)
    top = jnp.concatenate([w[:, 0], zero], axis=2)
    bot = jnp.concatenate([zero, w[:, 1]], axis=2)
    return jnp.concatenate([top, bot], axis=1)


def _unpair_blocks(wp):
    return jnp.stack([wp[:, :64, :64], wp[:, 64:, 64:]], axis=1).reshape(16, 64, 64)


def _pad_lanes(v, width=128):
    return jnp.pad(v, ((0, 0), (0, width - v.shape[1])))


class _Resident:
    before_embed = ()
    behind_out_proj = ()

    def __init__(self, w_in_sections, w_out, w_gate, w_up, w_down):
        self._w_in, self._w_out, self._ffn = w_in_sections, w_out, (w_gate, w_up, w_down)

    def w_in(self, after):
        return self._w_in

    def mid_forward(self, after):
        return jnp.zeros((1, 1), F32)

    def w_out(self, after):
        return self._w_out

    def ffn(self, after):
        return self._ffn

    def grads_ready(self, names, g, g_mxu):
        return jnp.zeros((1, 1), F32)

    def small_ready(self, g, loss):
        return jnp.zeros((1, 1), F32)

    def small_middle(self, after):
        return jnp.zeros((1, 1), F32)


def _local_step(x, target, meta, p, late):
    g, g_mxu = {}, {}
    ex = _head_expander()
    h0 = _embed(x, meta, late.before_embed)
    w_in = late.w_in(h0)
    convs = {SEC_NAMES.index("xs"): (p["ssd_conv_w"][:, :SSD_WIDTH], p["ssd_conv_b"][:, :SSD_WIDTH], True),
             SEC_NAMES.index("bc"): (p["ssd_conv_w"][:, SSD_WIDTH:], p["ssd_conv_b"][:, SSD_WIDTH:], True),
             SEC_NAMES.index("x"): (p["lru_conv_w"], p["lru_conv_b"], False)}
    u1, projs, acts = _norm_proj(h0, p["norm1_w"], [w_in[s] for s in SEC_NAMES], convs, name="norm_in_proj")
    proj = dict(zip(SEC_NAMES, projs))
    xs_act, bc_act, xr = (acts[SEC_NAMES.index(s)] for s in ("xs", "bc", "x"))
    ssd_prm = jnp.concatenate([_pad_lanes(p["ssd_dt_bias"]), _pad_lanes(p["ssd_a_log"]), _pad_lanes(p["ssd_d"]),
                               jnp.zeros((5, 128), F32)], axis=0)
    y_pre, y_ssd, prev = _ssd_fwd(xs_act, bc_act, proj["dt"], proj["z"], ssd_prm, p["ssd_norm_w"], ex)
    wa_p, wx_p = _pair_blocks(p["lru_wa"]), _pair_blocks(p["lru_wx"])
    lru_prm = jnp.concatenate([p["lru_ba"], p["lru_bx"], p["lru_lambda"], p["lru_norm_w"],
                               jnp.zeros((4, LRU_WIDTH), F32)], axis=0)
    hs, y_lru = _lru_fwd(xr, proj["g"], wa_p.astype(_MXU), wx_p.astype(_MXU),
                         lru_prm + late.mid_forward([xr, y_ssd]))
    ycat = jnp.concatenate([y_ssd, y_lru], axis=1)
    w_out = late.w_out(ycat)
    h1 = _mm([(ycat, 0, w_out, 0, 2 * D_MODEL)], T_ROWS, D_MODEL, tm=T_ROWS, tn=256, mode="nn", out_dtype=F32,
             name="out_proj", residual=h0, behind=late.behind_out_proj)
    u2 = _rmsnorm(h1, p["norm2_w"], name="norm2")
    w_gate, w_up, w_down = late.ffn(u2)
    gp, up, act = _ffn_up(u2, w_gate, w_up)
    h2 = _mm([(act, 0, w_down, 0, D_FF)], T_ROWS, D_MODEL, tm=T_ROWS, tn=256, mode="nn", out_dtype=F32,
             name="ffn_down", residual=h1)
    loss, dh2, dh2b, g["final_norm_w"] = _loss_head(h2, target, p["final_norm_w"])
    dgp, dup = _ffn_bwd_act(dh2b, w_down, gp, up)
    g["w_down"], g_mxu["w_down"] = _mm([(act, 0, dh2b, 0, T_ROWS)], D_FF, D_MODEL, tm=1408, tn=512, mode="tn",
                                       out_dtype=F32, name="dw_down", also_mxu=True)
    dh1, dh1b, g["norm2_w"] = _mm_norm_bwd([(dgp, w_gate, D_FF), (dup, w_up, D_FF)], h1, p["norm2_w"], dh2,
                                           name="ffn_bwd_in")
    g["w_gate"], g_mxu["w_gate"] = _mm([(dgp, 0, u2, 0, T_ROWS)], D_FF, D_MODEL, tm=1408, tn=512, mode="tn",
                                       out_dtype=F32, name="dw_gate", also_mxu=True)
    g["w_up"], g_mxu["w_up"] = _mm([(dup, 0, u2, 0, T_ROWS)], D_FF, D_MODEL, tm=1408, tn=512, mode="tn",
                                   out_dtype=F32, name="dw_up", also_mxu=True)
    g["w_out"], g_mxu["w_out"] = _mm([(ycat, 0, dh1b, 0, T_ROWS)], 2 * D_MODEL, D_MODEL, tm=1024, tn=512, mode="tn",
                                     out_dtype=F32, name="dw_out", also_mxu=True)
    sent = late.grads_ready(("w_down", "w_gate", "w_up", "w_out"), g, g_mxu)
    dycat = _mm([(dh1b, 0, w_out, 0, D_MODEL)], T_ROWS, 2 * D_MODEL, tm=T_ROWS, tn=256, mode="nt", out_dtype=F32,
                name="out_proj_bwd", behind=(sent,))
    dgate, dxr, dwa_p, dwx_p, dlru_prm = _lru_bwd(dycat, 1, proj["g"], xr, hs, wa_p.astype(_MXU), wx_p.astype(_MXU),
                                                  jnp.swapaxes(wa_p, 1, 2).astype(_MXU),
                                                  jnp.swapaxes(wx_p, 1, 2).astype(_MXU), lru_prm)
    g["lru_wa"], g["lru_wx"] = _unpair_blocks(dwa_p), _unpair_blocks(dwx_p)
    g["lru_ba"], g["lru_bx"], g["lru_lambda"], g["lru_norm_w"] = (dlru_prm[k:k + 1] for k in range(4))
    dx_lru, g["lru_conv_w"], g["lru_conv_b"] = _conv_bwd(dxr, proj["x"], p["lru_conv_w"], p["lru_conv_b"], silu=False,
                                                         name="lru_conv_bwd")
    dz, dxs_act, dbc_act, ddt, dssd_prm, g["ssd_norm_w"] = _ssd_bwd(dycat, 0, proj["z"], y_pre, xs_act, bc_act,
                                                                    proj["dt"], prev, ssd_prm, p["ssd_norm_w"], ex)
    g["ssd_dt_bias"], g["ssd_a_log"], g["ssd_d"] = (dssd_prm[k:k + 1, :SSD_HEADS] for k in range(3))
    dxs, dcw_xs, dcb_xs = _conv_bwd(dxs_act, proj["xs"], p["ssd_conv_w"][:, :SSD_WIDTH],
                                    p["ssd_conv_b"][:, :SSD_WIDTH], silu=True, name="ssd_conv_xs_bwd")
    dbc, dcw_bc, dcb_bc = _conv_bwd(dbc_act, proj["bc"], p["ssd_conv_w"][:, SSD_WIDTH:],
                                    p["ssd_conv_b"][:, SSD_WIDTH:], silu=True, name="ssd_conv_bc_bwd")
    g["ssd_conv_w"] = jnp.concatenate([dcw_xs, dcw_bc], axis=1)
    g["ssd_conv_b"] = jnp.concatenate([dcb_xs, dcb_bc], axis=1)
    dproj = {"z": dz, "xs": dxs, "bc": dbc, "dt": ddt, "g": dgate, "x": dx_lru}
    for s in SEC_NAMES:
        wdt = SEC_WIDTH[s]
        g["w_in_" + s], g_mxu["w_in_" + s] = _mm([(dproj[s], 0, u1, 0, T_ROWS)], wdt, D_MODEL, tm=min(wdt, 1024),
                                                 tn=512, mode="tn", out_dtype=F32, name="dw_in_" + s, also_mxu=True)
    sent = late.grads_ready(("w_in",), g, g_mxu)
    dh0, _, g["norm1_w"] = _mm_norm_bwd([(dproj[s], w_in[s], SEC_WIDTH[s]) for s in SEC_NAMES], h0,
                                        p["norm1_w"], dh1, name="in_proj_bwd", behind=(sent,))
    g["meta_tokens"] = dh0[PAD_ROWS:X_ROW0]
    late.small_ready(g, loss)
    return loss, dh0[X_ROW0:], g, g_mxu


MESH = pl.DeviceIdType.MESH
ANY = pl.BlockSpec(memory_space=pl.ANY)


def _my_place():
    return lax.axis_index("x"), lax.axis_index("y"), lax.axis_index("c")


def _other_chips(x, y):
    return [(1 - x, y), (x, 1 - y), (1 - x, 1 - y)]


HBM_SPEC = pl.BlockSpec(memory_space=pltpu.HBM)
SEM_SPEC = pl.BlockSpec(memory_space=pltpu.SEMAPHORE)
SPLIT_EFFECT = pltpu.SideEffectType.DATAFLOW_SIDE_EFFECTING


def _half_cols(buf, c, other=False):
    half = buf.shape[-1] // 2
    return pl.ds(pl.multiple_of(((1 - c) if other else c) * half, 128), half)


def _halves_plan(bufs, x, y, c, incoming):
    plan = []
    for buf in bufs:
        cols = _half_cols(buf, c)
        for (px, py) in _other_chips(x, y):
            slot = 2 * px + py if incoming else 2 * x + y
            plan.append((buf.at[2 * x + y, :, cols], buf.at[slot, :, cols], (px, py, c)))
    return plan


def _whole_plan(bufs, x, y, c, incoming):
    plan = []
    for buf in bufs:
        for (px, py) in _other_chips(x, y):
            slot = 2 * px + py if incoming else 2 * x + y
            plan.append((buf.at[2 * x + y], buf.at[slot], (px, py, c)))
    return plan


def _forward_plan(bufs, x, y, c, incoming):
    plan = []
    for buf in bufs:
        for (px, py) in _other_chips(x, y):
            slot = 2 * px + py
            plan.append((buf.at[slot, :, _half_cols(buf, c)], buf.at[slot, :, _half_cols(buf, c, other=incoming)],
                         (x, y, 1 - c)))
    return plan


def _scatter_plan(bufs, x, y, c, incoming):
    n = len(bufs) // 2
    plan = []
    for k in range(n):
        for j, (px, py) in enumerate(_other_chips(x, y)):
            plan.append((bufs[k].at[2 * px + py], bufs[n + k].at[j], (px, py, c)))
    return plan


def _split_start(bufs, plan, n_copies, after, *, name):
    n = len(bufs)
    extra = [] if after is None else [after]

    def body(*refs):
        ins = refs[:n]
        send_sems, recv_sems = refs[n + len(extra)], refs[n + len(extra) + 1]
        token = refs[-1]
        x, y, c = _my_place()
        for i, (src, dst, dev) in enumerate(plan(ins, x, y, c, False)):
            pltpu.make_async_remote_copy(src_ref=src, dst_ref=dst, send_sem=send_sems.at[i], recv_sem=recv_sems.at[i],
                                         device_id=dev, device_id_type=MESH).start()
        token[...] = jnp.zeros_like(token)

    outs = pl.pallas_call(
        body, name=name,
        out_shape=(pltpu.SemaphoreType.DMA((n_copies,)), pltpu.SemaphoreType.DMA((n_copies,)),
                   *[pltpu.HBM(b.shape, b.dtype) for b in bufs], jax.ShapeDtypeStruct((8, 128), F32)),
        in_specs=[HBM_SPEC] * n + [ANY] * len(extra),
        out_specs=(SEM_SPEC, SEM_SPEC, *[HBM_SPEC] * n, pl.BlockSpec(memory_space=pltpu.VMEM)),
        input_output_aliases={k: 2 + k for k in range(n)},
        compiler_params=pltpu.CompilerParams(has_side_effects=SPLIT_EFFECT),
    )(*[pltpu.with_memory_space_constraint(b, pltpu.HBM) for b in bufs], *extra)
    return outs[0], outs[1], list(outs[2:2 + n]), outs[-1]


def _split_wait(bufs, send_sems, recv_sems, plan, after, *, name):
    n = len(bufs)
    after = list(after) if isinstance(after, (list, tuple)) else [after]

    def body(*refs):
        ins = refs[:n]
        send_sems_ref, recv_sems_ref = refs[n], refs[n + 1]
        x, y, c = _my_place()
        for i, (src, dst, dev) in enumerate(plan(ins, x, y, c, True)):
            cp = pltpu.make_async_remote_copy(src_ref=src, dst_ref=dst, send_sem=send_sems_ref.at[i],
                                              recv_sem=recv_sems_ref.at[i], device_id=dev, device_id_type=MESH)
            cp.wait_send()
            cp.wait_recv()

    outs = pl.pallas_call(
        body, name=name, out_shape=tuple(pltpu.HBM(b.shape, b.dtype) for b in bufs),
        in_specs=[HBM_SPEC] * n + [SEM_SPEC, SEM_SPEC] + [ANY] * len(after), out_specs=tuple([HBM_SPEC] * n),
        input_output_aliases={k: k for k in range(n)},
        compiler_params=pltpu.CompilerParams(has_side_effects=SPLIT_EFFECT),
    )(*bufs, send_sems, recv_sems, *after)
    return list(outs)


def _fill_own_slots(shards, me_arr, *, name, behind=()):
    n = len(shards)
    n_in = n + len(behind)

    def body(me_ref, *refs):
        for k in range(n):
            refs[n_in + k][0] = refs[k][...].astype(_MXU)

    half = D_MODEL // 2
    return pl.pallas_call(
        body, name=name,
        grid_spec=pltpu.PrefetchScalarGridSpec(
            num_scalar_prefetch=1, grid=(2,),
            in_specs=[pl.BlockSpec((s.shape[0], half), lambda i, me: (0, i)) for s in shards]
            + [pl.BlockSpec(memory_space=pl.ANY)] * len(behind),
            out_specs=[pl.BlockSpec((1, s.shape[0], half), lambda i, me: (me[0], 0, i)) for s in shards]),
        out_shape=[jax.ShapeDtypeStruct((N_SHARDS,) + s.shape, _MXU) for s in shards],
        compiler_params=_cparams("parallel"),
    )(me_arr, *shards, *behind)


def _swap_with_sibling(parts, *, name, behind=()):
    n = len(parts)
    nb = len(behind)

    def body(*refs):
        ins, outs = refs[:n], refs[n + nb:2 * n + nb]
        send_sems, recv_sems = refs[2 * n + nb:]
        x, y, c = _my_place()
        copies = [pltpu.make_async_remote_copy(
            src_ref=ins[k], dst_ref=outs[k], send_sem=send_sems.at[k], recv_sem=recv_sems.at[k],
            device_id=(x, y, 1 - c), device_id_type=MESH) for k in range(n)]
        for cp in copies:
            cp.start()
        for cp in copies:
            cp.wait()

    return pl.pallas_call(
        body, name=name, in_specs=[ANY] * (n + nb), out_specs=[ANY] * n,
        out_shape=[jax.ShapeDtypeStruct(a.shape, a.dtype) for a in parts],
        scratch_shapes=[pltpu.SemaphoreType.DMA((n,)), pltpu.SemaphoreType.DMA((n,))],
    )(*parts, *behind)


def _other_devices(x, y, c):
    out = []
    for mask in range(1, N_DEV):
        px, py, pc = x ^ (mask >> 2 & 1), y ^ (mask >> 1 & 1), c ^ (mask & 1)
        out.append(((px, py, pc), 4 * px + 2 * py + pc))
    return out


def _pieces_plan(bufs, x, y, c, incoming):
    pack, land = bufs
    me = 4 * x + 2 * y + c
    return [(pack.at[num], land.at[num if incoming else me], dev) for dev, num in _other_devices(x, y, c)]


def _spread_plan(bufs, x, y, c, incoming):
    piece, land = bufs
    me = 4 * x + 2 * y + c
    return [(piece, land.at[num if incoming else me], dev) for dev, num in _other_devices(x, y, c)]


def _sum_pieces(pack, land, dev_arr, *, name):
    def body(dev_ref, pack_ref, land_ref, o_ref):
        dev = dev_ref[0]
        own = pack_ref[dev]
        acc = None
        for d in range(N_DEV):
            term = jnp.where(dev == d, own, land_ref[d])
            acc = term if acc is None else acc + term
        o_ref[...] = acc

    vmem = pl.BlockSpec(memory_space=pltpu.VMEM)
    return pl.pallas_call(
        body, name=name, in_specs=[pl.BlockSpec(memory_space=pltpu.SMEM), vmem, vmem], out_specs=vmem,
        out_shape=jax.ShapeDtypeStruct(pack.shape[1:], F32),
    )(dev_arr, pack, land)


def _join_pieces(piece, land, dev_arr, *, name):
    def body(dev_ref, piece_ref, land_ref, o_ref):
        dev = dev_ref[0]
        for d in range(N_DEV):
            o_ref[d] = jnp.where(dev == d, piece_ref[...], land_ref[d])

    vmem = pl.BlockSpec(memory_space=pltpu.VMEM)
    return pl.pallas_call(
        body, name=name, in_specs=[pl.BlockSpec(memory_space=pltpu.SMEM), vmem, vmem], out_specs=vmem,
        out_shape=jax.ShapeDtypeStruct(land.shape, F32),
    )(dev_arr, piece, land)


def _adamw_native(ws, gs, ms, vs):
    n = len(ws)

    def body(*refs):
        for k in range(n):
            w_ref, g_ref, m_ref, v_ref = (refs[j * n + k] for j in range(4))
            delta, m_new, v_new = _adamw_math(w_ref[...], g_ref[...], m_ref[...], v_ref[...])
            refs[4 * n + k][...] = delta
            refs[5 * n + k][...] = m_new
            refs[6 * n + k][...] = v_new

    vmem = pl.BlockSpec(memory_space=pltpu.VMEM)
    shapes = [jax.ShapeDtypeStruct(a.shape, F32) for a in ws]
    outs = pl.pallas_call(
        body, name="adamw_small", in_specs=[vmem] * (4 * n), out_specs=[vmem] * (3 * n), out_shape=shapes * 3,
        compiler_params=pltpu.CompilerParams(vmem_limit_bytes=VMEM_LIMIT_BYTES),
    )(*ws, *gs, *ms, *vs)
    return outs[:n], outs[n:2 * n], outs[2 * n:]


def _elementwise_tile(rows, cols):
    for t in range(256, 15, -16):
        if rows % t == 0:
            return (t, cols), rows // t, lambda i: (i, 0)
    assert cols % 256 == 0
    return (rows, 256), cols // 256, lambda i: (0, i)


def _partial_sum(own, land, me_arr, *, name):
    r, c = own.shape[-2:]
    tile, steps, imap = _elementwise_tile(r, c)
    whole = own.ndim == 3

    def body(me_ref, own_ref, land_ref, o_ref):
        acc = own_ref[0] if whole else own_ref[...]
        for j in range(3):
            acc = acc + land_ref[j].astype(F32)
        o_ref[...] = acc.astype(_MXU)

    own_spec = (pl.BlockSpec((1,) + tile, lambda i, me: (me[0],) + imap(i)) if whole
                else pl.BlockSpec(tile, lambda i, me: imap(i)))
    return pl.pallas_call(
        body, name=name,
        grid_spec=pltpu.PrefetchScalarGridSpec(
            num_scalar_prefetch=1, grid=(steps,),
            in_specs=[own_spec, pl.BlockSpec((3,) + tile, lambda i, me: (0,) + imap(i))],
            out_specs=pl.BlockSpec(tile, lambda i, me: imap(i))),
        out_shape=jax.ShapeDtypeStruct((r, c), _MXU),
        compiler_params=_cparams("parallel"),
    )(me_arr, own, land)


LANE_TILE = 256


def _partial_sums(owns, lands, me_arr, *, name):
    n = len(owns)

    def body(me_ref, *refs):
        for k in range(n):
            acc = refs[k][0]
            for j in range(3):
                acc = acc + refs[n + k][j].astype(F32)
            refs[2 * n + k][...] = acc.astype(_MXU)

    rows = [o.shape[1] for o in owns]
    return pl.pallas_call(
        body, name=name,
        grid_spec=pltpu.PrefetchScalarGridSpec(
            num_scalar_prefetch=1, grid=(D_MODEL // LANE_TILE,),
            in_specs=[pl.BlockSpec((1, r, LANE_TILE), lambda i, me: (me[0], 0, i)) for r in rows]
            + [pl.BlockSpec((3, r, LANE_TILE), lambda i, me: (0, 0, i)) for r in rows],
            out_specs=[pl.BlockSpec((r, LANE_TILE), lambda i, me: (0, i)) for r in rows]),
        out_shape=[jax.ShapeDtypeStruct((r, D_MODEL), _MXU) for r in rows],
        compiler_params=_cparams("parallel"),
    )(me_arr, *owns, *lands)


def _adamws(ws, parts_a, parts_b, ms, vs, *, name):
    n = len(ws)

    def body(*refs):
        for k in range(n):
            w_ref, a_ref, b_ref, m_ref, v_ref = (refs[j * n + k] for j in range(5))
            g = a_ref[...].astype(F32) + b_ref[...].astype(F32)
            delta, m_new, v_new = _adamw_math(w_ref[...], g, m_ref[...], v_ref[...])
            for j, val in enumerate((g, delta, m_new, v_new)):
                refs[(5 + j) * n + k][...] = val

    tiles = [pl.BlockSpec((w.shape[0], LANE_TILE), lambda i: (0, i)) for w in ws]
    outs = pl.pallas_call(
        body, name=name, grid=(D_MODEL // LANE_TILE,), in_specs=tiles * 5, out_specs=tiles * 4,
        out_shape=[jax.ShapeDtypeStruct(w.shape, F32) for w in ws] * 4,
        compiler_params=_cparams("parallel"),
    )(*ws, *parts_a, *parts_b, *ms, *vs)
    return [outs[j * n:(j + 1) * n] for j in range(4)]


def _adamw_math(w, g, m, v):
    m = ADAM_B1 * m + (1.0 - ADAM_B1) * g
    v = ADAM_B2 * v + (1.0 - ADAM_B2) * (g * g)
    m_hat = m / (1.0 - ADAM_B1 ** ADAM_STEP)
    v_hat = v / (1.0 - ADAM_B2 ** ADAM_STEP)
    delta = -ADAM_LR * (m_hat / (jnp.sqrt(v_hat) + ADAM_EPS) + ADAM_WD * w)
    return delta, m, v


def _adamw(w, grad_parts, m, v, *, name):
    if w.ndim == 3:
        steps = 4
        assert w.shape[0] % steps == 0
        tile_shape, imap = (w.shape[0] // steps,) + w.shape[1:], lambda i: (i, 0, 0)
    else:
        tile_shape, steps, imap = _elementwise_tile(*w.shape)
    n = len(grad_parts)

    def body(*refs):
        w_ref, m_ref, v_ref = refs[:3]
        g_refs = refs[3:3 + n]
        g_out, d_out, m_out, v_out = refs[3 + n:]
        g = g_refs[0][...].astype(F32)
        for k in range(1, n):
            g = g + g_refs[k][...].astype(F32)
        delta, m_new, v_new = _adamw_math(w_ref[...], g, m_ref[...], v_ref[...])
        g_out[...] = g
        d_out[...] = delta
        m_out[...] = m_new
        v_out[...] = v_new

    tile = pl.BlockSpec(tile_shape, imap)
    return pl.pallas_call(
        body, name=name, grid=(steps,), in_specs=[tile] * (3 + n), out_specs=[tile] * 4,
        out_shape=[jax.ShapeDtypeStruct(w.shape, F32)] * 4,
        compiler_params=_cparams("parallel"),
    )(w, m, v, *grad_parts)


WEIGHT_NAMES = ("meta_tokens", "norm1_w", "w_in", "ssd_conv_w", "ssd_conv_b", "ssd_dt_bias", "ssd_a_log", "ssd_d",
                "ssd_norm_w", "lru_conv_w", "lru_conv_b", "lru_wa", "lru_ba", "lru_wx", "lru_bx", "lru_lambda",
                "lru_norm_w", "w_out", "norm2_w", "w_gate", "w_up", "w_down", "final_norm_w")
BIG = ("w_in", "w_out", "w_gate", "w_up", "w_down")
FFN = ("w_gate", "w_up", "w_down")
LATE = ("w_out",) + FFN
SMALL_SHARDED = {"meta_tokens": (N_META, D_MODEL), "ssd_conv_w": (CONV_K, 1536), "lru_conv_w": (CONV_K, LRU_WIDTH)}
SMALL = tuple(n for n in WEIGHT_NAMES if n not in BIG)
PACK_COLS = 1024


def _pack(arrays, row_multiple):
    flat = jnp.concatenate([a.reshape(-1) for a in arrays])
    rows = -(-flat.shape[0] // (row_multiple * PACK_COLS)) * row_multiple
    return jnp.pad(flat, (0, rows * PACK_COLS - flat.shape[0])).reshape(rows, PACK_COLS)


def _unpack(pack, shapes):
    flat = pack.reshape(-1)
    out, off = [], 0
    for s in shapes:
        size = math.prod(s)
        out.append(flat[off:off + size].reshape(s))
        off += size
    return out


def _unshard_cols(g4):
    return jnp.swapaxes(g4, 0, 1).reshape(g4.shape[1], -1)


COL_SHARDED = ("w_in", "w_gate", "w_up")
IN_ROWS = {"z": (0, 1024), "xs": (1024, 2048), "bc": (2048, 2560), "dt": (2560, 2576), "g": (2576, 3600),
           "x": (3600, IN_COLS)}


def _rows_of_shards(shards4, lo, hi):
    r = shards4.shape[1]
    parts = [shards4[k, max(lo, k * r) - k * r:min(hi, (k + 1) * r) - k * r]
             for k in range(N_SHARDS) if max(lo, k * r) < min(hi, (k + 1) * r)]
    return parts[0] if len(parts) == 1 else jnp.concatenate(parts, axis=0)


def _w_in_shard_rows(k, sections):
    lo, hi = k * (IN_COLS // N_SHARDS), (k + 1) * (IN_COLS // N_SHARDS)
    parts = []
    for arr, (a, b) in zip(sections, IN_ROWS.values()):
        if max(lo, a) < min(hi, b):
            parts.append(arr[max(lo, a) - a:min(hi, b) - a])
    return jnp.concatenate(parts, axis=0)


def _rows_view(name, block):
    return jnp.swapaxes(block[0], 0, 1) if name in COL_SHARDED else block[0]


def _param_view(name, rows):
    return (jnp.swapaxes(rows, 0, 1) if name in COL_SHARDED else rows)[None]


def kernel(x, meta_tokens, norm1_w, w_in, ssd_conv_w, ssd_conv_b, ssd_dt_bias, ssd_a_log, ssd_d, ssd_norm_w, lru_conv_w, lru_conv_b, lru_wa, lru_ba, lru_wx, lru_bx, lru_lambda, lru_norm_w, w_out, norm2_w, w_gate, w_up, w_down, final_norm_w, loss_target, m_meta_tokens, m_norm1_w, m_w_in, m_ssd_conv_w, m_ssd_conv_b, m_ssd_dt_bias, m_ssd_a_log, m_ssd_d, m_ssd_norm_w, m_lru_conv_w, m_lru_conv_b, m_lru_wa, m_lru_ba, m_lru_wx, m_lru_bx, m_lru_lambda, m_lru_norm_w, m_w_out, m_norm2_w, m_w_gate, m_w_up, m_w_down, m_final_norm_w, v_meta_tokens, v_norm1_w, v_w_in, v_ssd_conv_w, v_ssd_conv_b, v_ssd_dt_bias, v_ssd_a_log, v_ssd_d, v_ssd_norm_w, v_lru_conv_w, v_lru_conv_b, v_lru_wa, v_lru_ba, v_lru_wx, v_lru_bx, v_lru_lambda, v_lru_norm_w, v_w_out, v_norm2_w, v_w_gate, v_w_up, v_w_down, v_final_norm_w):
    w = dict(zip(WEIGHT_NAMES, (meta_tokens, norm1_w, w_in, ssd_conv_w, ssd_conv_b, ssd_dt_bias, ssd_a_log, ssd_d, ssd_norm_w, lru_conv_w, lru_conv_b, lru_wa, lru_ba, lru_wx, lru_bx, lru_lambda, lru_norm_w, w_out, norm2_w, w_gate, w_up, w_down, final_norm_w)))
    m = dict(zip(WEIGHT_NAMES, (m_meta_tokens, m_norm1_w, m_w_in, m_ssd_conv_w, m_ssd_conv_b, m_ssd_dt_bias, m_ssd_a_log, m_ssd_d, m_ssd_norm_w, m_lru_conv_w, m_lru_conv_b, m_lru_wa, m_lru_ba, m_lru_wx, m_lru_bx, m_lru_lambda, m_lru_norm_w, m_w_out, m_norm2_w, m_w_gate, m_w_up, m_w_down, m_final_norm_w)))
    v = dict(zip(WEIGHT_NAMES, (v_meta_tokens, v_norm1_w, v_w_in, v_ssd_conv_w, v_ssd_conv_b, v_ssd_dt_bias, v_ssd_a_log, v_ssd_d, v_ssd_norm_w, v_lru_conv_w, v_lru_conv_b, v_lru_wa, v_lru_ba, v_lru_wx, v_lru_bx, v_lru_lambda, v_lru_norm_w, v_w_out, v_norm2_w, v_w_gate, v_w_up, v_w_down, v_final_norm_w)))
    me = 2 * lax.axis_index("x") + lax.axis_index("y")

    big2d = {n: _rows_view(n, w[n]) for n in BIG}
    small_local = jnp.concatenate([w["meta_tokens"].reshape(-1), w["ssd_conv_w"].reshape(-1),
                                   w["lru_conv_w"].reshape(-1)])[None]
    me_arr = me.astype(jnp.int32).reshape(1)
    dev_arr = (2 * me + lax.axis_index("c")).astype(jnp.int32).reshape(1)
    small_slots = lax.dynamic_update_slice(jnp.zeros((N_SHARDS,) + small_local.shape, F32), small_local[None],
                                           (me, 0, 0))
    sm_send, sm_recv, sm_bufs, sm_tok = _split_start([small_slots], _whole_plan, 3, None, name="gather_small_start")
    (w_in_slot,) = _fill_own_slots([big2d["w_in"]], me_arr, name="own_slot_w_in", behind=(sm_tok,))
    in_send, in_recv, in_bufs, in_tok = _split_start([w_in_slot], _halves_plan, 3, sm_tok, name="gather_w_in_start")
    (small4,) = _split_wait(sm_bufs, sm_send, sm_recv, _whole_plan, in_bufs[0], name="gather_small_wait")
    late_slots = _fill_own_slots([big2d[n] for n in LATE], me_arr, name="own_slots_late", behind=(in_tok,))
    sm = small4[:, 0]
    meta_full = _unshard_cols(sm[:, :4096].reshape(N_SHARDS, N_META, 256))
    ssd_conv_w_full = _unshard_cols(sm[:, 4096:5632].reshape(N_SHARDS, CONV_K, 384))
    lru_conv_w_full = _unshard_cols(sm[:, 5632:].reshape(N_SHARDS, CONV_K, 256))

    p = {"ssd_conv_w": ssd_conv_w_full, "lru_conv_w": lru_conv_w_full,
         "lru_wa": w["lru_wa"][0], "lru_wx": w["lru_wx"][0], "final_norm_w": w["final_norm_w"][None]}
    for n in ("norm1_w", "ssd_conv_b", "ssd_dt_bias", "ssd_a_log", "ssd_d", "ssd_norm_w", "lru_conv_b", "lru_ba",
              "lru_bx", "lru_lambda", "lru_norm_w", "norm2_w"):
        p[n] = w[n]

    class Late:
        def __init__(self):
            self.pending = []
            self.before_embed = (late_slots[0],)

        def w_in(self, after):
            (buf,) = _split_wait(in_bufs, in_send, in_recv, _halves_plan, after, name="gather_w_in_wait")
            send, recv, bufs, tok = _split_start([buf], _forward_plan, 3, None, name="forward_w_in_start")
            self.out_gather = _split_start(late_slots[:1], _halves_plan, 3, tok, name="gather_w_out_start")
            self.ffn_gather = _split_start(late_slots[1:], _halves_plan, 3 * len(FFN), self.out_gather[3],
                                           name="gather_ffn_start")
            (w_in4,) = _split_wait(bufs, send, recv, _forward_plan, self.ffn_gather[2][0], name="forward_w_in_wait")
            sections = {s: _rows_of_shards(w_in4, lo, hi) for s, (lo, hi) in IN_ROWS.items()}
            sections["dt"] = jnp.pad(sections["dt"], ((0, SEC_WIDTH["dt"] - SSD_HEADS), (0, 0)))
            return sections

        def mid_forward(self, after):
            send, recv, bufs, _ = self.out_gather
            bufs = _split_wait(bufs, send, recv, _halves_plan, after, name="gather_w_out_wait")
            self.forward = _split_start(bufs, _forward_plan, 3, None, name="forward_w_out_start")
            return self.forward[3][:1, :1]

        def w_out(self, after):
            send, recv, bufs, _ = self.forward
            (w,) = _split_wait(bufs, send, recv, _forward_plan, after, name="forward_w_out_wait")
            send, recv, bufs, _ = self.ffn_gather
            bufs = _split_wait(bufs, send, recv, _halves_plan, after, name="gather_ffn_wait")
            self.forward = _split_start(bufs, _forward_plan, 3 * len(FFN), None, name="forward_ffn_start")
            self.behind_out_proj = (self.forward[2][0],)
            return w.reshape(-1, D_MODEL)

        def ffn(self, after):
            send, recv, bufs, _ = self.forward
            bufs = _split_wait(bufs, send, recv, _forward_plan, after, name="forward_ffn_wait")
            return tuple(b.reshape(-1, D_MODEL) for b in bufs)

        def grads_ready(self, names, g, g_mxu):
            if names == ("w_in",):
                g_mxu["w_in"] = jnp.stack([_w_in_shard_rows(k, [g_mxu["w_in_" + s] for s in SEC_NAMES])
                                           for k in range(N_SHARDS)])
            srcs = [g_mxu[n].reshape(N_SHARDS, -1, D_MODEL) for n in names]
            lands = [lax.empty((3,) + s.shape[1:], _MXU) for s in srcs]
            tag = "_".join(names)
            send, recv, bufs, tok = _split_start(srcs + lands, _scatter_plan, 3 * len(names), None,
                                                 name="scatter_" + tag + "_start")
            self.pending.append((names, send, recv, bufs, tag))
            self.in_flight = bufs[0]
            return tok[:1, :1]

        def landed(self, after, which):
            land = {}
            for names, send, recv, bufs, tag in self.pending:
                if names[0] in which:
                    bufs = _split_wait(bufs, send, recv, _scatter_plan, after, name="scatter_" + tag + "_wait")
                    land.update(zip(names, bufs[len(names):]))
            return land

        def small_ready(self, g, loss):
            pack = _pack([g[n] for n in SMALL] + [loss[0, :1]], 8 * N_DEV)
            pack = pack.reshape(N_DEV, -1, PACK_COLS)
            self.small = _split_start([pack, lax.empty(pack.shape, F32)], _pieces_plan, N_DEV - 1, loss,
                                      name="small_pieces_start")
            return self.small[3]

        def small_middle(self, after):
            send, recv, bufs, _ = self.small
            pack, land = _split_wait(bufs, send, recv, _pieces_plan, after, name="small_pieces_wait")
            piece = _sum_pieces(pack, land, dev_arr, name="small_pieces_sum")
            self.small = _split_start([piece, lax.empty(pack.shape, F32)], _spread_plan, N_DEV - 1, None,
                                      name="small_spread_start")
            return self.small[3]

        def small_sum(self, after):
            send, recv, bufs, _ = self.small
            piece, land = _split_wait(bufs, send, recv, _spread_plan, after, name="small_spread_wait")
            return _join_pieces(piece, land, dev_arr, name="small_join")

    late = Late()

    loss, grad_x, g, g_mxu = _local_step(x[0], loss_target[0], meta_full, p, late)

    g4 = {n: g[n].reshape(N_SHARDS, -1, D_MODEL) for n in LATE}
    g4["w_in"] = lax.switch(me, [functools.partial(_w_in_shard_rows, k) for k in range(N_SHARDS)],
                            [g["w_in_" + s] for s in SEC_NAMES])
    land = late.landed([late.in_flight, late.small[2][0]], LATE)
    part = dict(zip(LATE, _partial_sums([g4[n] for n in LATE], [land[n] for n in LATE], me_arr,
                                        name="partial_late")))
    sib = dict(zip(LATE, _swap_with_sibling([part[n] for n in LATE], name="swap_late")))

    grad, delta, new_m, new_v = {}, {}, {}, {}
    late_outs = _adamws([big2d[n] for n in LATE], [part[n] for n in LATE], [sib[n] for n in LATE],
                        [_rows_view(n, m[n]) for n in LATE], [_rows_view(n, v[n]) for n in LATE], name="adamw_late")
    for d, outs in zip((grad, delta, new_m, new_v), late_outs):
        d.update({n: _param_view(n, o) for n, o in zip(LATE, outs)})

    land.update(late.landed(late_outs[0][0], ("w_in",)))
    spread = late.small_middle(land["w_in"])
    part_in = _partial_sum(g4["w_in"], land["w_in"], me_arr, name="partial_w_in")
    (sib_in,) = _swap_with_sibling([part_in], name="swap_w_in", behind=(spread,))
    lanes = lambda a: a[0].reshape(8, 128, -1).transpose(2, 0, 1)
    pieces = lambda a: a.reshape(-1, 8, 128)
    outs = _adamw(lanes(w["w_in"]), [pieces(part_in), pieces(sib_in)], lanes(m["w_in"]), lanes(v["w_in"]),
                  name="adamw_w_in")
    grad["w_in"], delta["w_in"], new_m["w_in"], new_v["w_in"] = (o.transpose(1, 2, 0).reshape(1, D_MODEL, -1)
                                                                 for o in outs)

    small_full_shape = {n: (SMALL_SHARDED[n] if n in SMALL_SHARDED else w[n].shape) for n in SMALL}
    red_list = _unpack(late.small_sum(outs[0]), [small_full_shape[n] for n in SMALL] + [(1,)])
    loss_total = red_list[-1][0]
    g_small = {}
    for n, arr in zip(SMALL, red_list[:-1]):
        if n in SMALL_SHARDED:
            cols = SMALL_SHARDED[n][1] // N_SHARDS
            arr = lax.dynamic_slice_in_dim(arr, me * cols, cols, axis=1)
        g_small[n] = arr.reshape(w[n].shape)
    two_d = lambda a: a.reshape(1, -1) if a.ndim == 1 else a
    deltas, new_ms, new_vs = _adamw_native(*[[two_d(d[n]) for n in SMALL] for d in (w, g_small, m, v)])
    for n, dn, mn, vn in zip(SMALL, deltas, new_ms, new_vs):
        grad[n], delta[n], new_m[n], new_v[n] = (g_small[n], dn.reshape(w[n].shape), mn.reshape(w[n].shape),
                                                 vn.reshape(w[n].shape))

    return (loss_total, grad_x[None], *[grad[n] for n in WEIGHT_NAMES], *[delta[n] for n in WEIGHT_NAMES],
            *[new_m[n] for n in WEIGHT_NAMES], *[new_v[n] for n in WEIGHT_NAMES])
```
